```python
import math
import jax, jax.numpy as jnp
from jax import lax
import numpy as np

D_MODEL = 1024
BATCH = 8
SEQ = 4096
DEPTH = 1

SSM_WIDTH = 512
SSM_GROUP = 16
SSM_GROUPS = SSM_WIDTH // SSM_GROUP
SSM_STATE = 64
DT_MIN = 1e-3
DT_MAX = 1e-1
SGU_WIDTH = 512
SGU_GROUPS = 8
SGU_GROUP_DIM = SGU_WIDTH // SGU_GROUPS
CHUNK = 128
N_BRANCH = 2
IN_COLS = SSM_WIDTH + 2 * SGU_WIDTH + N_BRANCH * D_MODEL
D_FF = 2816
CONV_WIDTH = 3
EPS = 1e-6

kernel_name = "hybrid_s5_gmlp_gated_block"


def rms_norm(x, g):
    x32 = x.astype(jnp.float32)
    y = x32 * lax.rsqrt(jnp.mean(x32 * x32, axis=-1, keepdims=True) + EPS)
    return (y * g.astype(jnp.float32)).astype(x.dtype)


def s5_branch(u, a_re, a_im, log_dt, b_re, b_im, c_re, c_im, d_skip, w_glu, b_glu):
    bsz, s, _ = u.shape
    f32 = jnp.float32
    u32 = u.astype(f32).reshape(bsz, s, SSM_GROUPS, SSM_GROUP)
    dt = jnp.exp(log_dt.astype(f32))[:, None]
    ar = a_re.astype(f32)
    ai = a_im.astype(f32)
    mag = jnp.exp(dt * ar)
    abar_re = mag * jnp.cos(dt * ai)
    abar_im = mag * jnp.sin(dt * ai)
    den = ar * ar + ai * ai
    nr = abar_re - 1.0
    ni = abar_im
    f_re = (nr * ar + ni * ai) / den
    f_im = (ni * ar - nr * ai) / den
    br = b_re.astype(f32)
    bi = b_im.astype(f32)
    bbar_re = f_re[..., None] * br - f_im[..., None] * bi
    bbar_im = f_re[..., None] * bi + f_im[..., None] * br
    bu_re = jnp.einsum("bsgh,gph->bsgp", u32, bbar_re)
    bu_im = jnp.einsum("bsgh,gph->bsgp", u32, bbar_im)
    a_seq_re = jnp.broadcast_to(abar_re[None, None], (1, s, SSM_GROUPS, SSM_STATE))
    a_seq_im = jnp.broadcast_to(abar_im[None, None], (1, s, SSM_GROUPS, SSM_STATE))

    def combine(left, right):
        alr, ali, blr, bli = left
        arr, ari, brr, bri = right
        return (arr * alr - ari * ali,
                arr * ali + ari * alr,
                arr * blr - ari * bli + brr,
                arr * bli + ari * blr + bri)

    _, _, st_re, st_im = lax.associative_scan(
        combine, (a_seq_re, a_seq_im, bu_re, bu_im), axis=1)
    y = (jnp.einsum("bsgp,ghp->bsgh", st_re, c_re.astype(f32))
         - jnp.einsum("bsgp,ghp->bsgh", st_im, c_im.astype(f32)))
    y = y + d_skip.astype(f32).reshape(SSM_GROUPS, SSM_GROUP) * u32
    y = jax.nn.gelu(y.reshape(bsz, s, SSM_WIDTH))
    y = y * jax.nn.sigmoid(y @ w_glu.astype(f32) + b_glu.astype(f32))
    return y.astype(u.dtype)


def sgu_branch(uv, g_sgu, w_s, b_s):
    bsz, s, _ = uv.shape
    uv = jax.nn.gelu(uv)
    u, v = jnp.split(uv, 2, axis=-1)
    v = rms_norm(v, g_sgu)
    n_chunks = s // CHUNK
    v = v.reshape(bsz, n_chunks, CHUNK, SGU_GROUPS, SGU_GROUP_DIM)
    mask = jnp.tril(jnp.ones((CHUNK, CHUNK), dtype=bool))
    ws = jnp.where(mask[None], w_s, 0.0)
    mixed = jnp.einsum("gts,bcsgd->bctgd", ws, v) + b_s.T[:, :, None]
    mixed = mixed.reshape(bsz, s, SGU_WIDTH)
    return u * mixed


def causal_depthwise_conv(y, w, b):
    s = y.shape[1]
    yp = jnp.pad(y, ((0, 0), (CONV_WIDTH - 1, 0), (0, 0)))
    return sum(yp[:, k:k + s] * w[k] for k in range(CONV_WIDTH)) + b


def _fwd_setup_inputs(seed: int = 0) -> dict:
    key = jax.random.key(seed)
    ks = jax.random.split(key, 25)
    L = DEPTH
    f32 = jnp.float32

    def nrm(k, shape, scale):
        return jax.random.normal(k, shape, f32) * scale

    n_idx = jnp.arange(SSM_STATE, dtype=f32)
    return {
        "x": nrm(ks[0], (BATCH, SEQ, D_MODEL), 1.0),
        "g_mix": 1.0 + nrm(ks[1], (L, D_MODEL), 0.02),
        "w_in": nrm(ks[2], (L, D_MODEL, IN_COLS), D_MODEL ** -0.5),
        "a_re": -0.5 + nrm(ks[3], (L, SSM_GROUPS, SSM_STATE), 0.01),
        "a_im": math.pi * n_idx + nrm(ks[4], (L, SSM_GROUPS, SSM_STATE), 0.01),
        "log_dt": jax.random.uniform(ks[5], (L, SSM_GROUPS), f32,
                                     minval=math.log(DT_MIN), maxval=math.log(DT_MAX)),
        "b_re": nrm(ks[6], (L, SSM_GROUPS, SSM_STATE, SSM_GROUP), (2 * SSM_GROUP) ** -0.5),
        "b_im": nrm(ks[7], (L, SSM_GROUPS, SSM_STATE, SSM_GROUP), (2 * SSM_GROUP) ** -0.5),
        "c_re": nrm(ks[8], (L, SSM_GROUPS, SSM_GROUP, SSM_STATE), SSM_STATE ** -0.5),
        "c_im": nrm(ks[9], (L, SSM_GROUPS, SSM_GROUP, SSM_STATE), SSM_STATE ** -0.5),
        "d_skip": nrm(ks[10], (L, SSM_WIDTH), 1.0),
        "w_glu": nrm(ks[11], (L, SSM_WIDTH, SSM_WIDTH), SSM_WIDTH ** -0.5),
        "b_glu": nrm(ks[12], (L, SSM_WIDTH), 0.01),
        "w_proj_a": nrm(ks[13], (L, SSM_WIDTH, D_MODEL), SSM_WIDTH ** -0.5),
        "g_sgu": 1.0 + nrm(ks[14], (L, SGU_WIDTH), 0.02),
        "w_s": nrm(ks[15], (L, SGU_GROUPS, CHUNK, CHUNK), CHUNK ** -0.5),
        "b_s": 1.0 + nrm(ks[16], (L, SGU_GROUPS, CHUNK), 0.01),
        "w_proj_b": nrm(ks[17], (L, SGU_WIDTH, D_MODEL), SGU_WIDTH ** -0.5),
        "w_out": nrm(ks[18], (L, D_MODEL, D_MODEL), D_MODEL ** -0.5),
        "g_ffn": 1.0 + nrm(ks[19], (L, D_MODEL), 0.02),
        "w_up": nrm(ks[20], (L, D_MODEL, 2 * D_FF), D_MODEL ** -0.5),
        "conv_w": nrm(ks[21], (L, CONV_WIDTH, 2 * D_FF), CONV_WIDTH ** -0.5),
        "conv_b": nrm(ks[22], (L, 2 * D_FF), 0.01),
        "w_down": nrm(ks[23], (L, D_FF, D_MODEL), D_FF ** -0.5),
        "g_final": 1.0 + nrm(ks[24], (D_MODEL,), 0.02),
    }


def _fwd_reference(x, g_mix, w_in, a_re, a_im, log_dt, b_re, b_im, c_re, c_im, d_skip,
              w_glu, b_glu, w_proj_a, g_sgu, w_s, b_s, w_proj_b, w_out, g_ffn,
              w_up, conv_w, conv_b, w_down, g_final):
    for l in range(DEPTH):
        h = rms_norm(x, g_mix[l])
        p = h @ w_in[l]
        u_ssm = p[..., :SSM_WIDTH]
        uv_sgu = p[..., SSM_WIDTH:SSM_WIDTH + 2 * SGU_WIDTH]
        gate_logits = p[..., SSM_WIDTH + 2 * SGU_WIDTH:]
        y_a = s5_branch(u_ssm, a_re[l], a_im[l], log_dt[l], b_re[l], b_im[l],
                        c_re[l], c_im[l], d_skip[l], w_glu[l], b_glu[l]) @ w_proj_a[l]
        y_b = sgu_branch(uv_sgu, g_sgu[l], w_s[l], b_s[l]) @ w_proj_b[l]
        g_a, g_b = jnp.split(jax.nn.sigmoid(gate_logits), N_BRANCH, axis=-1)
        x = x + (g_a * y_a + g_b * y_b) @ w_out[l]
        h = rms_norm(x, g_ffn[l])
        up = causal_depthwise_conv(h @ w_up[l], conv_w[l], conv_b[l])
        a, b = jnp.split(up, 2, axis=-1)
        x = x + (jax.nn.silu(a) * b) @ w_down[l]
    return rms_norm(x, g_final)


import jax as _jax
import jax.numpy as _jnp

TWIN_FORMAT = 'train_step'
FWD_PARAMS = ['x', 'g_mix', 'w_in', 'a_re', 'a_im', 'log_dt', 'b_re', 'b_im', 'c_re', 'c_im', 'd_skip', 'w_glu', 'b_glu', 'w_proj_a', 'g_sgu', 'w_s', 'b_s', 'w_proj_b', 'w_out', 'g_ffn', 'w_up', 'conv_w', 'conv_b', 'w_down', 'g_final']
TWIN_WEIGHTS = ['g_mix', 'w_in', 'a_re', 'a_im', 'log_dt', 'b_re', 'b_im', 'c_re', 'c_im', 'd_skip', 'w_glu', 'b_glu', 'w_proj_a', 'g_sgu', 'w_s', 'b_s', 'w_proj_b', 'w_out', 'g_ffn', 'w_up', 'conv_w', 'conv_b', 'w_down', 'g_final']
TWIN_DIFF_INPUT = 'x'
TWIN_INPUTS = ['x', 'g_mix', 'w_in', 'a_re', 'a_im', 'log_dt', 'b_re', 'b_im', 'c_re', 'c_im', 'd_skip', 'w_glu', 'b_glu', 'w_proj_a', 'g_sgu', 'w_s', 'b_s', 'w_proj_b', 'w_out', 'g_ffn', 'w_up', 'conv_w', 'conv_b', 'w_down', 'g_final', 'loss_target', 'm_g_mix', 'm_w_in', 'm_a_re', 'm_a_im', 'm_log_dt', 'm_b_re', 'm_b_im', 'm_c_re', 'm_c_im', 'm_d_skip', 'm_w_glu', 'm_b_glu', 'm_w_proj_a', 'm_g_sgu', 'm_w_s', 'm_b_s', 'm_w_proj_b', 'm_w_out', 'm_g_ffn', 'm_w_up', 'm_conv_w', 'm_conv_b', 'm_w_down', 'm_g_final', 'v_g_mix', 'v_w_in', 'v_a_re', 'v_a_im', 'v_log_dt', 'v_b_re', 'v_b_im', 'v_c_re', 'v_c_im', 'v_d_skip', 'v_w_glu', 'v_b_glu', 'v_w_proj_a', 'v_g_sgu', 'v_w_s', 'v_b_s', 'v_w_proj_b', 'v_w_out', 'v_g_ffn', 'v_w_up', 'v_conv_w', 'v_conv_b', 'v_w_down', 'v_g_final']
TWIN_OUTPUTS = ['loss', 'grad_x', 'grad_g_mix', 'grad_w_in', 'grad_a_re', 'grad_a_im', 'grad_log_dt', 'grad_b_re', 'grad_b_im', 'grad_c_re', 'grad_c_im', 'grad_d_skip', 'grad_w_glu', 'grad_b_glu', 'grad_w_proj_a', 'grad_g_sgu', 'grad_w_s', 'grad_b_s', 'grad_w_proj_b', 'grad_w_out', 'grad_g_ffn', 'grad_w_up', 'grad_conv_w', 'grad_conv_b', 'grad_w_down', 'grad_g_final', 'delta_g_mix', 'delta_w_in', 'delta_a_re', 'delta_a_im', 'delta_log_dt', 'delta_b_re', 'delta_b_im', 'delta_c_re', 'delta_c_im', 'delta_d_skip', 'delta_w_glu', 'delta_b_glu', 'delta_w_proj_a', 'delta_g_sgu', 'delta_w_s', 'delta_b_s', 'delta_w_proj_b', 'delta_w_out', 'delta_g_ffn', 'delta_w_up', 'delta_conv_w', 'delta_conv_b', 'delta_w_down', 'delta_g_final', 'new_m_g_mix', 'new_m_w_in', 'new_m_a_re', 'new_m_a_im', 'new_m_log_dt', 'new_m_b_re', 'new_m_b_im', 'new_m_c_re', 'new_m_c_im', 'new_m_d_skip', 'new_m_w_glu', 'new_m_b_glu', 'new_m_w_proj_a', 'new_m_g_sgu', 'new_m_w_s', 'new_m_b_s', 'new_m_w_proj_b', 'new_m_w_out', 'new_m_g_ffn', 'new_m_w_up', 'new_m_conv_w', 'new_m_conv_b', 'new_m_w_down', 'new_m_g_final', 'new_v_g_mix', 'new_v_w_in', 'new_v_a_re', 'new_v_a_im', 'new_v_log_dt', 'new_v_b_re', 'new_v_b_im', 'new_v_c_re', 'new_v_c_im', 'new_v_d_skip', 'new_v_w_glu', 'new_v_b_glu', 'new_v_w_proj_a', 'new_v_g_sgu', 'new_v_w_s', 'new_v_b_s', 'new_v_w_proj_b', 'new_v_w_out', 'new_v_g_ffn', 'new_v_w_up', 'new_v_conv_w', 'new_v_conv_b', 'new_v_w_down', 'new_v_g_final']
TWIN_LEAF_KINDS = {'loss': 'loss', 'grad_x': 'grad_x', 'grad_g_mix': 'grad_w', 'grad_w_in': 'grad_w', 'grad_a_re': 'grad_w', 'grad_a_im': 'grad_w', 'grad_log_dt': 'grad_w', 'grad_b_re': 'grad_w', 'grad_b_im': 'grad_w', 'grad_c_re': 'grad_w', 'grad_c_im': 'grad_w', 'grad_d_skip': 'grad_w', 'grad_w_glu': 'grad_w', 'grad_b_glu': 'grad_w', 'grad_w_proj_a': 'grad_w', 'grad_g_sgu': 'grad_w', 'grad_w_s': 'grad_w', 'grad_b_s': 'grad_w', 'grad_w_proj_b': 'grad_w', 'grad_w_out': 'grad_w', 'grad_g_ffn': 'grad_w', 'grad_w_up': 'grad_w', 'grad_conv_w': 'grad_w', 'grad_conv_b': 'grad_w', 'grad_w_down': 'grad_w', 'grad_g_final': 'grad_w', 'delta_g_mix': 'delta_w', 'delta_w_in': 'delta_w', 'delta_a_re': 'delta_w', 'delta_a_im': 'delta_w', 'delta_log_dt': 'delta_w', 'delta_b_re': 'delta_w', 'delta_b_im': 'delta_w', 'delta_c_re': 'delta_w', 'delta_c_im': 'delta_w', 'delta_d_skip': 'delta_w', 'delta_w_glu': 'delta_w', 'delta_b_glu': 'delta_w', 'delta_w_proj_a': 'delta_w', 'delta_g_sgu': 'delta_w', 'delta_w_s': 'delta_w', 'delta_b_s': 'delta_w', 'delta_w_proj_b': 'delta_w', 'delta_w_out': 'delta_w', 'delta_g_ffn': 'delta_w', 'delta_w_up': 'delta_w', 'delta_conv_w': 'delta_w', 'delta_conv_b': 'delta_w', 'delta_w_down': 'delta_w', 'delta_g_final': 'delta_w', 'new_m_g_mix': 'new_m', 'new_m_w_in': 'new_m', 'new_m_a_re': 'new_m', 'new_m_a_im': 'new_m', 'new_m_log_dt': 'new_m', 'new_m_b_re': 'new_m', 'new_m_b_im': 'new_m', 'new_m_c_re': 'new_m', 'new_m_c_im': 'new_m', 'new_m_d_skip': 'new_m', 'new_m_w_glu': 'new_m', 'new_m_b_glu': 'new_m', 'new_m_w_proj_a': 'new_m', 'new_m_g_sgu': 'new_m', 'new_m_w_s': 'new_m', 'new_m_b_s': 'new_m', 'new_m_w_proj_b': 'new_m', 'new_m_w_out': 'new_m', 'new_m_g_ffn': 'new_m', 'new_m_w_up': 'new_m', 'new_m_conv_w': 'new_m', 'new_m_conv_b': 'new_m', 'new_m_w_down': 'new_m', 'new_m_g_final': 'new_m', 'new_v_g_mix': 'new_v', 'new_v_w_in': 'new_v', 'new_v_a_re': 'new_v', 'new_v_a_im': 'new_v', 'new_v_log_dt': 'new_v', 'new_v_b_re': 'new_v', 'new_v_b_im': 'new_v', 'new_v_c_re': 'new_v', 'new_v_c_im': 'new_v', 'new_v_d_skip': 'new_v', 'new_v_w_glu': 'new_v', 'new_v_b_glu': 'new_v', 'new_v_w_proj_a': 'new_v', 'new_v_g_sgu': 'new_v', 'new_v_w_s': 'new_v', 'new_v_b_s': 'new_v', 'new_v_w_proj_b': 'new_v', 'new_v_w_out': 'new_v', 'new_v_g_ffn': 'new_v', 'new_v_w_up': 'new_v', 'new_v_conv_w': 'new_v', 'new_v_conv_b': 'new_v', 'new_v_w_down': 'new_v', 'new_v_g_final': 'new_v'}


def _forward(args):
    return _fwd_reference(*[args[k] for k in FWD_PARAMS])


def _output_shape():
    out = _jax.eval_shape(lambda: _forward(_fwd_setup_inputs(0)))
    return out.shape, out.dtype

N_MICROBATCH = 1
ADAM_LR = 0.001
ADAM_B1 = 0.9
ADAM_B2 = 0.999
ADAM_EPS = 1e-08
ADAM_WD = 0.01
ADAM_STEP = 10
PER_EXAMPLE_BATCH_AXIS = {'x': 0, 'loss_target': 0}
SHARED_INPUTS = []
_WEIGHT_DTYPES = {'g_mix': _jnp.float32, 'w_in': _jnp.float32, 'a_re': _jnp.float32, 'a_im': _jnp.float32, 'log_dt': _jnp.float32, 'b_re': _jnp.float32, 'b_im': _jnp.float32, 'c_re': _jnp.float32, 'c_im': _jnp.float32, 'd_skip': _jnp.float32, 'w_glu': _jnp.float32, 'b_glu': _jnp.float32, 'w_proj_a': _jnp.float32, 'g_sgu': _jnp.float32, 'w_s': _jnp.float32, 'b_s': _jnp.float32, 'w_proj_b': _jnp.float32, 'w_out': _jnp.float32, 'g_ffn': _jnp.float32, 'w_up': _jnp.float32, 'conv_w': _jnp.float32, 'conv_b': _jnp.float32, 'w_down': _jnp.float32, 'g_final': _jnp.float32}
MOMENT_SCALE = {'g_mix': 1.104619e-01, 'w_in': 5.617159e-02, 'a_re': 4.260433e-03, 'a_im': 3.517866e-03, 'log_dt': 3.772538e+00, 'b_re': 2.486628e-03, 'b_im': 2.500358e-03, 'c_re': 3.494411e-03, 'c_im': 3.611753e-03, 'd_skip': 5.156074e-02, 'w_glu': 1.567616e-02, 'b_glu': 2.360514e-02, 'w_proj_a': 3.653923e-02, 'g_sgu': 6.459871e-02, 'w_s': 4.480098e-02, 'b_s': 6.094333e-02, 'w_proj_b': 7.904035e-02, 'w_out': 8.561094e-02, 'g_ffn': 1.309672e-01, 'w_up': 5.318507e-02, 'conv_w': 5.242691e-02, 'conv_b': 5.301094e-02, 'w_down': 8.709321e-02, 'g_final': 3.196341e+01}


def _to_microbatches(a, axis):
    t = _jnp.moveaxis(a, axis, 0)
    t = t.reshape((N_MICROBATCH, t.shape[0] // N_MICROBATCH) + t.shape[1:])
    return _jnp.moveaxis(t, 1, axis + 1)


def setup_inputs(seed: int = 0) -> dict:
    inp = _fwd_setup_inputs(seed)
    key = _jax.random.fold_in(_jax.random.key(seed), 7919)
    shape, _ = _output_shape()
    out = dict(inp)
    out["loss_target"] = _jax.random.normal(_jax.random.fold_in(key, 0), shape, _jnp.float32)
    for i, name in enumerate(TWIN_WEIGHTS):
        w = inp[name].astype(_jnp.float32)
        if MOMENT_SCALE is None:
            s = _jnp.sqrt(_jnp.mean(_jnp.square(w)) + 1e-30)
        else:
            s = MOMENT_SCALE[name]
        km, kv = _jax.random.split(_jax.random.fold_in(key, i + 1))
        out[name] = w
        out["m_" + name] = s * _jax.random.normal(km, w.shape, _jnp.float32)
        out["v_" + name] = (s * s) * _jax.random.uniform(kv, w.shape, _jnp.float32, 0.5, 1.5)
    if N_MICROBATCH > 1:
        for name, axis in PER_EXAMPLE_BATCH_AXIS.items():
            out[name] = _to_microbatches(out[name], axis)
    return {'x': out['x'], 'g_mix': out['g_mix'], 'w_in': out['w_in'], 'a_re': out['a_re'], 'a_im': out['a_im'], 'log_dt': out['log_dt'], 'b_re': out['b_re'], 'b_im': out['b_im'], 'c_re': out['c_re'], 'c_im': out['c_im'], 'd_skip': out['d_skip'], 'w_glu': out['w_glu'], 'b_glu': out['b_glu'], 'w_proj_a': out['w_proj_a'], 'g_sgu': out['g_sgu'], 'w_s': out['w_s'], 'b_s': out['b_s'], 'w_proj_b': out['w_proj_b'], 'w_out': out['w_out'], 'g_ffn': out['g_ffn'], 'w_up': out['w_up'], 'conv_w': out['conv_w'], 'conv_b': out['conv_b'], 'w_down': out['w_down'], 'g_final': out['g_final'], 'loss_target': out['loss_target'], 'm_g_mix': out['m_g_mix'], 'm_w_in': out['m_w_in'], 'm_a_re': out['m_a_re'], 'm_a_im': out['m_a_im'], 'm_log_dt': out['m_log_dt'], 'm_b_re': out['m_b_re'], 'm_b_im': out['m_b_im'], 'm_c_re': out['m_c_re'], 'm_c_im': out['m_c_im'], 'm_d_skip': out['m_d_skip'], 'm_w_glu': out['m_w_glu'], 'm_b_glu': out['m_b_glu'], 'm_w_proj_a': out['m_w_proj_a'], 'm_g_sgu': out['m_g_sgu'], 'm_w_s': out['m_w_s'], 'm_b_s': out['m_b_s'], 'm_w_proj_b': out['m_w_proj_b'], 'm_w_out': out['m_w_out'], 'm_g_ffn': out['m_g_ffn'], 'm_w_up': out['m_w_up'], 'm_conv_w': out['m_conv_w'], 'm_conv_b': out['m_conv_b'], 'm_w_down': out['m_w_down'], 'm_g_final': out['m_g_final'], 'v_g_mix': out['v_g_mix'], 'v_w_in': out['v_w_in'], 'v_a_re': out['v_a_re'], 'v_a_im': out['v_a_im'], 'v_log_dt': out['v_log_dt'], 'v_b_re': out['v_b_re'], 'v_b_im': out['v_b_im'], 'v_c_re': out['v_c_re'], 'v_c_im': out['v_c_im'], 'v_d_skip': out['v_d_skip'], 'v_w_glu': out['v_w_glu'], 'v_b_glu': out['v_b_glu'], 'v_w_proj_a': out['v_w_proj_a'], 'v_g_sgu': out['v_g_sgu'], 'v_w_s': out['v_w_s'], 'v_b_s': out['v_b_s'], 'v_w_proj_b': out['v_w_proj_b'], 'v_w_out': out['v_w_out'], 'v_g_ffn': out['v_g_ffn'], 'v_w_up': out['v_w_up'], 'v_conv_w': out['v_conv_w'], 'v_conv_b': out['v_conv_b'], 'v_w_down': out['v_w_down'], 'v_g_final': out['v_g_final']}


def _loss(weights, diff, rest, loss_target):
    with _jax.named_scope("forward"):
        args = {**rest, TWIN_DIFF_INPUT: diff, **{k: w.astype(_WEIGHT_DTYPES[k]) for k, w in weights.items()}}
        y = _forward(args)
    with _jax.named_scope("loss_head"):
        err = _jnp.square(y.astype(_jnp.float32) - loss_target)
        return 0.5 * _jnp.sum(_jnp.mean(err, axis=-1)) if err.ndim else 0.5 * err


def _adamw(w, g, m, v):
    m = ADAM_B1 * m + (1.0 - ADAM_B1) * g
    v = ADAM_B2 * v + (1.0 - ADAM_B2) * _jnp.square(g)
    m_hat = m / (1.0 - ADAM_B1 ** ADAM_STEP)
    v_hat = v / (1.0 - ADAM_B2 ** ADAM_STEP)
    delta = -ADAM_LR * (m_hat / (_jnp.sqrt(v_hat) + ADAM_EPS) + ADAM_WD * w)
    return delta, m, v


def reference(x, g_mix, w_in, a_re, a_im, log_dt, b_re, b_im, c_re, c_im, d_skip, w_glu, b_glu, w_proj_a, g_sgu, w_s, b_s, w_proj_b, w_out, g_ffn, w_up, conv_w, conv_b, w_down, g_final, loss_target, m_g_mix, m_w_in, m_a_re, m_a_im, m_log_dt, m_b_re, m_b_im, m_c_re, m_c_im, m_d_skip, m_w_glu, m_b_glu, m_w_proj_a, m_g_sgu, m_w_s, m_b_s, m_w_proj_b, m_w_out, m_g_ffn, m_w_up, m_conv_w, m_conv_b, m_w_down, m_g_final, v_g_mix, v_w_in, v_a_re, v_a_im, v_log_dt, v_b_re, v_b_im, v_c_re, v_c_im, v_d_skip, v_w_glu, v_b_glu, v_w_proj_a, v_g_sgu, v_w_s, v_b_s, v_w_proj_b, v_w_out, v_g_ffn, v_w_up, v_conv_w, v_conv_b, v_w_down, v_g_final):
    given = dict(x=x, g_mix=g_mix, w_in=w_in, a_re=a_re, a_im=a_im, log_dt=log_dt, b_re=b_re, b_im=b_im, c_re=c_re, c_im=c_im, d_skip=d_skip, w_glu=w_glu, b_glu=b_glu, w_proj_a=w_proj_a, g_sgu=g_sgu, w_s=w_s, b_s=b_s, w_proj_b=w_proj_b, w_out=w_out, g_ffn=g_ffn, w_up=w_up, conv_w=conv_w, conv_b=conv_b, w_down=w_down, g_final=g_final, loss_target=loss_target, m_g_mix=m_g_mix, m_w_in=m_w_in, m_a_re=m_a_re, m_a_im=m_a_im, m_log_dt=m_log_dt, m_b_re=m_b_re, m_b_im=m_b_im, m_c_re=m_c_re, m_c_im=m_c_im, m_d_skip=m_d_skip, m_w_glu=m_w_glu, m_b_glu=m_b_glu, m_w_proj_a=m_w_proj_a, m_g_sgu=m_g_sgu, m_w_s=m_w_s, m_b_s=m_b_s, m_w_proj_b=m_w_proj_b, m_w_out=m_w_out, m_g_ffn=m_g_ffn, m_w_up=m_w_up, m_conv_w=m_conv_w, m_conv_b=m_conv_b, m_w_down=m_w_down, m_g_final=m_g_final, v_g_mix=v_g_mix, v_w_in=v_w_in, v_a_re=v_a_re, v_a_im=v_a_im, v_log_dt=v_log_dt, v_b_re=v_b_re, v_b_im=v_b_im, v_c_re=v_c_re, v_c_im=v_c_im, v_d_skip=v_d_skip, v_w_glu=v_w_glu, v_b_glu=v_b_glu, v_w_proj_a=v_w_proj_a, v_g_sgu=v_g_sgu, v_w_s=v_w_s, v_b_s=v_b_s, v_w_proj_b=v_w_proj_b, v_w_out=v_w_out, v_g_ffn=v_g_ffn, v_w_up=v_w_up, v_conv_w=v_conv_w, v_conv_b=v_conv_b, v_w_down=v_w_down, v_g_final=v_g_final)
    weights = {n: given[n] for n in TWIN_WEIGHTS}
    shared = {n: given[n] for n in SHARED_INPUTS}
    per_example = {n: given[n] for n in ['x']}
    grad_fn = _jax.value_and_grad(_loss, argnums=(0, 1))

    def one_microbatch(ex, loss_target):
        ex = dict(ex)
        diff = ex.pop(TWIN_DIFF_INPUT)
        return grad_fn(weights, diff, {**shared, **ex}, loss_target)

    if N_MICROBATCH == 1:
        loss, (grad_w, grad_x) = one_microbatch(per_example, given["loss_target"])
    else:
        def body(carry, xs):
            loss_sum, grad_sum = carry
            l_k, (gw_k, gx_k) = one_microbatch(xs[0], xs[1])
            with _jax.named_scope("update"):
                return (loss_sum + l_k, _jax.tree.map(_jnp.add, grad_sum, gw_k)), gx_k

        init = (_jnp.zeros((), _jnp.float32), _jax.tree.map(_jnp.zeros_like, weights))
        (loss, grad_w), grad_x = _jax.lax.scan(body, init, (per_example, given["loss_target"]))
    with _jax.named_scope("update"):
        delta_w, new_m, new_v = {}, {}, {}
        for n in TWIN_WEIGHTS:
            delta_w[n], new_m[n], new_v[n] = _adamw(weights[n], grad_w[n], given["m_" + n], given["v_" + n])
    return (loss, grad_x, *[grad_w[n] for n in TWIN_WEIGHTS], *[delta_w[n] for n in TWIN_WEIGHTS],
            *[new_m[n] for n in TWIN_WEIGHTS], *[new_v[n] for n in TWIN_WEIGHTS])
```

```python
import functools
import math

import jax
import jax.numpy as jnp
from jax import lax
from jax.experimental import pallas as pl
from jax.experimental.pallas import tpu as pltpu

F32 = jnp.float32
MXU = jnp.bfloat16
EPS = 1e-6

D_MODEL = 1024
SSM_W = 512
SSM_G, SSM_H, SSM_P = 32, 16, 64
SSM_BLK = 4
SGU_W = 512
SGU_G, SGU_D, CHUNK = 8, 64, 128
D_FF = 2816
FF_CW = 256
FF_NCB = D_FF // FF_CW
N_DEV = 8
LANES = 128

ADAM_LR, ADAM_B1, ADAM_B2, ADAM_EPS, ADAM_WD, ADAM_STEP = 0.001, 0.9, 0.999, 1e-08, 0.01, 10

VMEM_LIMIT = 48 * 1024 * 1024


def _cp(*sem):
    return pltpu.CompilerParams(dimension_semantics=sem, vmem_limit_bytes=VMEM_LIMIT)


def _full(shape):
    n = len(shape)
    return pl.BlockSpec(shape, lambda *_: (0,) * n)


def _sds(shape, dtype=F32):
    return jax.ShapeDtypeStruct(shape, dtype)


def _dot(a, b):
    return jnp.dot(a, b, preferred_element_type=F32)


def _dot_nt(a, b):
    return lax.dot_general(a, b, (((1,), (1,)), ((), ())), preferred_element_type=F32)


def _dot_tn(a, b):
    return lax.dot_general(a, b, (((0,), (0,)), ((), ())), preferred_element_type=F32)


_GELU_C = math.sqrt(2.0 / math.pi)


def _gelu(x):
    return 0.5 * x * (1.0 + jnp.tanh(_GELU_C * (x + 0.044715 * (x * x * x))))


def _gelu_and_grad(x):
    t = jnp.tanh(_GELU_C * (x + 0.044715 * (x * x * x)))
    g = 0.5 * x * (1.0 + t)
    dg = 0.5 * (1.0 + t) + 0.5 * x * (1.0 - t * t) * (_GELU_C * (1.0 + 3.0 * 0.044715 * (x * x)))
    return g, dg


def _sigmoid(x):
    return 1.0 / (1.0 + jnp.exp(-x))


def _rms(x):
    return lax.rsqrt(jnp.mean(x * x, axis=-1, keepdims=True) + EPS)


def _rms_bwd(dxn, xn, r):
    return r * (dxn - xn * jnp.mean(dxn * xn, axis=-1, keepdims=True))


def _rowsum(x):
    return jnp.sum(x, axis=0, keepdims=True)


def _s5_disc(are, aim, ldt, br, bi):
    dt = jnp.exp(ldt)
    mag = jnp.exp(dt * are)
    abr = mag * jnp.cos(dt * aim)
    abi = mag * jnp.sin(dt * aim)
    den = are * are + aim * aim
    nr = abr - 1.0
    ni = abi
    fr = (nr * are + ni * aim) / den
    fi = (ni * are - nr * aim) / den
    return abr, abi, fr * br - fi * bi, fr * bi + fi * br


def _s5_params_fwd(are, aim, ldt, br, bi):
    def body(are_ref, aim_ref, ldt_ref, br_ref, bi_ref, o0, o1, o2, o3):
        outs = _s5_disc(are_ref[...], aim_ref[...], ldt_ref[...], br_ref[...], bi_ref[...])
        for o, v in zip((o0, o1, o2, o3), outs):
            o[...] = v
    shp = are.shape
    return pl.pallas_call(body, name="s5_params_fwd", out_shape=[_sds(shp)] * 4)(are, aim, ldt, br, bi)


def _s5_params_bwd(are, aim, ldt, br, bi, dabr, dabi, dbr, dbi):
    def body(are_ref, aim_ref, ldt_ref, br_ref, bi_ref, c0, c1, c2, c3, o0, o1, o2, o3, o4):
        prim = (are_ref[...], aim_ref[...], ldt_ref[...], br_ref[...], bi_ref[...])
        _, vjp = jax.vjp(_s5_disc, *prim)
        outs = vjp((c0[...], c1[...], c2[...], c3[...]))
        for o, v in zip((o0, o1, o2, o3, o4), outs):
            o[...] = v
    shp = are.shape
    return pl.pallas_call(body, name="s5_params_bwd", out_shape=[_sds(shp)] * 5)(
        are, aim, ldt, br, bi, dabr, dabi, dbr, dbi)


def _blockdiag(m_t):
    m = m_t.reshape(SSM_BLK, 8, SSM_H, 1, SSM_P)
    eye = jnp.eye(8, dtype=bool).reshape(1, 8, 1, 8, 1)
    return jnp.where(eye, m, jnp.zeros((), m_t.dtype)).reshape(SSM_BLK, 8 * SSM_H, 8 * SSM_P)


def _unblockdiag(pc):
    m = pc.reshape(SSM_BLK, 8, SSM_H, 8, SSM_P)
    return jnp.einsum("jghgp->jghp", m).reshape(SSM_G * SSM_H, SSM_P)


def _in_fwd(x, g_mix, w_in, tm):
    S = x.shape[0]

    def body(x_ref, g_ref, w_ref, h_ref, us_ref, uv_ref, gl_ref):
        xv = x_ref[...]
        h = (xv * _rms(xv) * g_ref[...]).astype(MXU)
        h_ref[...] = h
        us_ref[...] = _dot(h, w_ref[:, 0:SSM_W])
        uv_ref[...] = _dot(h, w_ref[:, SSM_W:SSM_W + 2 * SGU_W])
        gl_ref[...] = _dot(h, w_ref[:, SSM_W + 2 * SGU_W:])

    row = lambda n: pl.BlockSpec((tm, n), lambda i: (i, 0))
    return pl.pallas_call(
        body, name="in_fwd", grid=(S // tm,),
        in_specs=[row(D_MODEL), _full((1, D_MODEL)), _full(w_in.shape)],
        out_specs=[row(D_MODEL), row(SSM_W), row(2 * SGU_W), row(2 * D_MODEL)],
        out_shape=[_sds((S, D_MODEL), MXU), _sds((S, SSM_W)), _sds((S, 2 * SGU_W)), _sds((S, 2 * D_MODEL))],
        compiler_params=_cp("parallel"),
    )(x, g_mix, w_in)


def _scan_tables(ar, ai, reverse):
    n = ar.shape[-1]
    def mul(p, q):
        return p[0] * q[0] - p[1] * q[1], p[0] * q[1] + p[1] * q[0]
    a1 = (ar, ai)
    a2 = mul(a1, a1)
    a3 = mul(a2, a1)
    a4 = mul(a2, a2)
    a5 = mul(a4, a1)
    a6 = mul(a4, a2)
    a7 = mul(a4, a3)
    a8 = mul(a4, a4)
    pw = (a1, a2, a3, a4, a5, a6, a7, a8)
    rows = lax.broadcasted_iota(jnp.int32, (8, n), 0)
    tabs = []
    for s, a in ((1, a1), (2, a2), (4, a4)):
        keep = (rows + s <= 7) if reverse else (rows >= s)
        for comp in a:
            tabs.append(jnp.where(keep, jnp.broadcast_to(comp, (8, n)), 0.0))
    for c in range(2):
        q = jnp.zeros((8, n), F32)
        for r in range(8):
            e = (8 - r) if reverse else (r + 1)
            q = jnp.where(rows == r, jnp.broadcast_to(pw[e - 1][c], (8, n)), q)
        tabs.append(q)
    return tabs


def _scan_group(xr, xi, tab_ref, cr, ci, reverse):
    for t, s in enumerate((1, 2, 4)):
        pr = tab_ref[2 * t]
        pi = tab_ref[2 * t + 1]
        sh = (8 - s) if reverse else s
        sr = pltpu.roll(xr, sh, 0)
        si = pltpu.roll(xi, sh, 0)
        xr, xi = xr + pr * sr - pi * si, xi + pr * si + pi * sr
    qr = tab_ref[6]
    qi = tab_ref[7]
    return xr + qr * cr - qi * ci, xi + qr * ci + qi * cr


def _s5_fwd(us, abar_re, abar_im, b_re, b_im, c_re, c_im, d_skip, tm):
    S = us.shape[0]
    nt = S // tm
    w = 8 * SSM_P

    def body(us_ref, ar_ref, ai_ref, br_ref, bi_ref, cr_ref, ci_ref, d_ref, str_ref, sti_ref, ys_ref, tab_ref, car_ref):
        i = pl.program_id(1)

        @pl.when(i == 0)
        def _():
            car_ref[...] = jnp.zeros_like(car_ref)
            for k, t in enumerate(_scan_tables(ar_ref[...], ai_ref[...], False)):
                tab_ref[k] = t

        u = us_ref[...]
        ub = u.astype(MXU)
        str_ref[...] = _dot(ub, br_ref[0])
        sti_ref[...] = _dot(ub, bi_ref[0])

        def grp(k, carry):
            r0 = pl.multiple_of(k * 8, 8)
            xr, xi = _scan_group(str_ref[pl.ds(r0, 8), :], sti_ref[pl.ds(r0, 8), :], tab_ref, carry[0], carry[1], False)
            str_ref[pl.ds(r0, 8), :] = xr
            sti_ref[pl.ds(r0, 8), :] = xi
            return xr[7:8, :], xi[7:8, :]

        cr, ci = lax.fori_loop(0, tm // 8, grp, (car_ref[0:1, :], car_ref[1:2, :]))
        car_ref[0:1, :] = cr
        car_ref[1:2, :] = ci
        y = _dot_nt(str_ref[...].astype(MXU), cr_ref[0]) - _dot_nt(sti_ref[...].astype(MXU), ci_ref[0])
        ys_ref[...] = y + d_ref[...] * u

    blk = lambda: pl.BlockSpec((1, 8 * SSM_H, w), lambda j, i: (j, 0, 0))
    return pl.pallas_call(
        body, name="s5_fwd", grid=(SSM_BLK, nt),
        in_specs=[pl.BlockSpec((tm, LANES), lambda j, i: (i, j)),
                  pl.BlockSpec((1, w), lambda j, i: (0, j)), pl.BlockSpec((1, w), lambda j, i: (0, j)),
                  blk(), blk(), blk(), blk(),
                  pl.BlockSpec((1, LANES), lambda j, i: (0, j))],
        out_specs=[pl.BlockSpec((tm, w), lambda j, i: (i, j)), pl.BlockSpec((tm, w), lambda j, i: (i, j)),
                   pl.BlockSpec((tm, LANES), lambda j, i: (i, j))],
        out_shape=[_sds((S, SSM_BLK * w)), _sds((S, SSM_BLK * w)), _sds((S, SSM_W))],
        scratch_shapes=[pltpu.VMEM((8, 8, w), F32), pltpu.VMEM((8, w), F32)],
        compiler_params=_cp("parallel", "arbitrary"),
    )(us, abar_re, abar_im, b_re, b_im, c_re, c_im, d_skip)


def _sgu_mix(vnb, ws_ref, grp):
    acc = jnp.zeros(vnb.shape, F32)
    for g in range(SGU_G):
        acc = jnp.where(grp == g, _dot(ws_ref[g], vnb), acc)
    return acc


def _mix_fwd(x, ys, uv, gl, w_glu, b_glu, w_pa, g_sgu, ws, bias_s, w_pb, w_out, g_ffn, tm):
    S = x.shape[0]

    def body(x_ref, ys_ref, uv_ref, gl_ref, wglu_ref, bglu_ref, wpa_ref, gs_ref, ws_ref, bias_ref, wpb_ref, wout_ref,
             gf_ref, yg_ref, yap_ref, sg_ref, ya_ref, yb_ref, m_ref, x1_ref, h2_ref):
        yg = _gelu(ys_ref[...])
        ygb = yg.astype(MXU)
        yg_ref[...] = ygb
        z = _dot(ygb, wglu_ref[...]) + bglu_ref[...]
        yapb = (yg * _sigmoid(z)).astype(MXU)
        yap_ref[...] = yapb
        ya = _dot(yapb, wpa_ref[...])
        ya_ref[...] = ya

        uvg = _gelu(uv_ref[...])
        u2 = uvg[:, :SGU_W]
        v2 = uvg[:, SGU_W:]
        vnb = (v2 * _rms(v2) * gs_ref[...]).astype(MXU)
        grp = lax.broadcasted_iota(jnp.int32, (CHUNK, SGU_W), 1) // SGU_D
        for c in range(tm // CHUNK):
            rs = slice(c * CHUNK, (c + 1) * CHUNK)
            mixed = _sgu_mix(vnb[rs], ws_ref, grp) + bias_ref[...]
            sg_ref[rs, :] = (u2[rs] * mixed).astype(MXU)
        yb = _dot(sg_ref[...], wpb_ref[...])
        yb_ref[...] = yb

        glv = gl_ref[...]
        m = _sigmoid(glv[:, :D_MODEL]) * ya + _sigmoid(glv[:, D_MODEL:]) * yb
        mb = m.astype(MXU)
        m_ref[...] = mb
        x1 = x_ref[...] + _dot(mb, wout_ref[...])
        x1_ref[...] = x1
        h2_ref[...] = (x1 * _rms(x1) * gf_ref[...]).astype(MXU)

    row = lambda n: pl.BlockSpec((tm, n), lambda i: (i, 0))
    return pl.pallas_call(
        body, name="mix_fwd", grid=(S // tm,),
        in_specs=[row(D_MODEL), row(SSM_W), row(2 * SGU_W), row(2 * D_MODEL),
                  _full(w_glu.shape), _full(b_glu.shape), _full(w_pa.shape), _full(g_sgu.shape), _full(ws.shape),
                  _full(bias_s.shape), _full(w_pb.shape), _full(w_out.shape), _full(g_ffn.shape)],
        out_specs=[row(SSM_W), row(SSM_W), row(SGU_W), row(D_MODEL), row(D_MODEL), row(D_MODEL), row(D_MODEL),
                   row(D_MODEL)],
        out_shape=[_sds((S, SSM_W), MXU), _sds((S, SSM_W), MXU), _sds((S, SGU_W), MXU), _sds((S, D_MODEL)),
                   _sds((S, D_MODEL)), _sds((S, D_MODEL), MXU), _sds((S, D_MODEL)), _sds((S, D_MODEL), MXU)],
        compiler_params=_cp("parallel"),
    )(x, ys, uv, gl, w_glu, b_glu, w_pa, g_sgu, ws, bias_s, w_pb, w_out, g_ffn)


def _conv_taps(u, prev8, rows):
    t1 = prev8[7:8, :]
    t0 = prev8[6:7, :]
    s1 = jnp.where(rows == 0, t1, pltpu.roll(u, 1, 0))
    s2 = jnp.where(rows == 0, t0, jnp.where(rows == 1, t1, pltpu.roll(u, 2, 0)))
    return s1, s2


def _ffn_fwd(h2, x1, tgt, w_up, conv_w, conv_b, w_down, g_final, tm):
    S = h2.shape[0]
    nt = S // tm
    ncb = FF_NCB

    def body(h2_ref, wa_ref, wb_ref, cwa_ref, cwb_ref, cba_ref, cbb_ref, wd_ref, x1_ref, gf_ref, tgt_ref,
             upa_ref, upb_ref, ff_ref, dx2_ref, loss_ref, dgf_ref, acc_ref, tail_ref):
        i = pl.program_id(0)
        cb = pl.program_id(1)

        @pl.when(i == 0)
        def _():
            tail_ref[cb] = jnp.zeros((2, 8, FF_CW), F32)

        @pl.when(jnp.logical_and(i == 0, cb == 0))
        def _():
            loss_ref[...] = jnp.zeros_like(loss_ref)
            dgf_ref[...] = jnp.zeros_like(dgf_ref)

        h2v = h2_ref[...]
        ua = _dot(h2v, wa_ref[...])
        ub = _dot(h2v, wb_ref[...])
        upa_ref[...] = ua
        upb_ref[...] = ub
        rows = lax.broadcasted_iota(jnp.int32, (tm, FF_CW), 0)
        s1a, s2a = _conv_taps(ua, tail_ref[cb, 0], rows)
        s1b, s2b = _conv_taps(ub, tail_ref[cb, 1], rows)
        tail_ref[cb, 0] = ua[tm - 8:tm, :]
        tail_ref[cb, 1] = ub[tm - 8:tm, :]
        cwa = cwa_ref[...]
        cwb = cwb_ref[...]
        a = cwa[0:1] * s2a + cwa[1:2] * s1a + cwa[2:3] * ua + cba_ref[...]
        b = cwb[0:1] * s2b + cwb[1:2] * s1b + cwb[2:3] * ub + cbb_ref[...]
        ffb = (a * _sigmoid(a) * b).astype(MXU)
        ff_ref[...] = ffb
        contrib = _dot(ffb, wd_ref[...])

        @pl.when(cb == 0)
        def _():
            acc_ref[...] = contrib

        @pl.when(cb > 0)
        def _():
            acc_ref[...] += contrib

        @pl.when(cb == ncb - 1)
        def _():
            x2 = x1_ref[...] + acc_ref[...]
            r = _rms(x2)
            xn = x2 * r
            g = gf_ref[...]
            diff = xn * g - tgt_ref[...]
            loss_ref[...] += (0.5 / D_MODEL) * jnp.sum(diff * diff)
            dy = diff * (1.0 / D_MODEL)
            dgf_ref[...] += _rowsum(dy * xn)
            dx2_ref[...] = _rms_bwd(dy * g, xn, r)

    row = lambda n: pl.BlockSpec((tm, n), lambda i, c: (i, 0))
    return pl.pallas_call(
        body, name="ffn_fwd", grid=(nt, ncb),
        in_specs=[row(D_MODEL),
                  pl.BlockSpec((D_MODEL, FF_CW), lambda i, c: (0, c)),
                  pl.BlockSpec((D_MODEL, FF_CW), lambda i, c: (0, ncb + c)),
                  pl.BlockSpec((3, FF_CW), lambda i, c: (0, c)),
                  pl.BlockSpec((3, FF_CW), lambda i, c: (0, ncb + c)),
                  pl.BlockSpec((1, FF_CW), lambda i, c: (0, c)),
                  pl.BlockSpec((1, FF_CW), lambda i, c: (0, ncb + c)),
                  pl.BlockSpec((FF_CW, D_MODEL), lambda i, c: (c, 0)),
                  row(D_MODEL), _full((1, D_MODEL)), row(D_MODEL)],
        out_specs=[pl.BlockSpec((tm, FF_CW), lambda i, c: (i, c)),
                   pl.BlockSpec((tm, FF_CW), lambda i, c: (i, c)),
                   pl.BlockSpec((tm, FF_CW), lambda i, c: (i, c)),
                   row(D_MODEL), _full((1, LANES)), _full((1, D_MODEL))],
        out_shape=[_sds((S, D_FF)), _sds((S, D_FF)), _sds((S, D_FF), MXU), _sds((S, D_MODEL)),
                   _sds((1, LANES)), _sds((1, D_MODEL))],
        scratch_shapes=[pltpu.VMEM((tm, D_MODEL), F32), pltpu.VMEM((ncb, 2, 8, FF_CW), F32)],
        compiler_params=_cp("arbitrary", "arbitrary"),
    )(h2, w_up, w_up, conv_w, conv_w, conv_b, conv_b, w_down, x1, g_final, tgt)


def _ffn_bwd(dx2, upa, upb, x1, w_up, conv_w, conv_b, w_down, g_ffn, tm):
    S = dx2.shape[0]
    nt = S // tm
    ncb = FF_NCB
    hb = tm // 8

    def body(dx2_ref, upa_ref, upb_ref, hpa_ref, hpb_ref, cwa_ref, cwb_ref, cba_ref, cbb_ref, wd_ref, wa_ref, wb_ref,
             x1_ref, g_ref, dupa_ref, dupb_ref, dx1_ref, dconv_ref, dg_ref, acc_ref, head_ref):
        i = pl.program_id(0)
        cb = pl.program_id(1)
        ri = nt - 1 - i

        @pl.when(i == 0)
        def _():
            head_ref[cb] = jnp.zeros((2, 8, FF_CW), F32)
            dconv_ref[cb] = jnp.zeros((8, FF_CW), F32)
            dconv_ref[ncb + cb] = jnp.zeros((8, FF_CW), F32)

        @pl.when(jnp.logical_and(i == 0, cb == 0))
        def _():
            dg_ref[...] = jnp.zeros_like(dg_ref)

        dx2v = dx2_ref[...]
        dff = _dot_nt(dx2v.astype(MXU), wd_ref[...])
        ua = upa_ref[...]
        ub = upb_ref[...]
        rows = lax.broadcasted_iota(jnp.int32, (tm, FF_CW), 0)
        first = ri == 0
        s1a, s2a = _conv_taps(ua, jnp.where(first, 0.0, hpa_ref[...]), rows)
        s1b, s2b = _conv_taps(ub, jnp.where(first, 0.0, hpb_ref[...]), rows)
        cwa = cwa_ref[...]
        cwb = cwb_ref[...]
        a = cwa[0:1] * s2a + cwa[1:2] * s1a + cwa[2:3] * ua + cba_ref[...]
        b = cwb[0:1] * s2b + cwb[1:2] * s1b + cwb[2:3] * ub + cbb_ref[...]
        sa = _sigmoid(a)
        da = dff * b * (sa * (1.0 + a * (1.0 - sa)))
        db = dff * (a * sa)

        def conv_bwd(dup, head8, cw):
            h0 = head8[0:1, :]
            h1 = head8[1:2, :]
            n1 = jnp.where(rows == tm - 1, h0, pltpu.roll(dup, tm - 1, 0))
            n2 = jnp.where(rows == tm - 2, h0, jnp.where(rows == tm - 1, h1, pltpu.roll(dup, tm - 2, 0)))
            return cw[2:3] * dup + cw[1:2] * n1 + cw[0:1] * n2

        dpa = conv_bwd(da, head_ref[cb, 0], cwa).astype(MXU)
        dpb = conv_bwd(db, head_ref[cb, 1], cwb).astype(MXU)
        head_ref[cb, 0] = da[0:8, :]
        head_ref[cb, 1] = db[0:8, :]
        dupa_ref[...] = dpa
        dupb_ref[...] = dpb
        for slot, dup, s2, s1, u in ((cb, da, s2a, s1a, ua), (ncb + cb, db, s2b, s1b, ub)):
            dconv_ref[slot, 0:1, :] += _rowsum(dup * s2)
            dconv_ref[slot, 1:2, :] += _rowsum(dup * s1)
            dconv_ref[slot, 2:3, :] += _rowsum(dup * u)
            dconv_ref[slot, 3:4, :] += _rowsum(dup)
        contrib = _dot_nt(dpa, wa_ref[...]) + _dot_nt(dpb, wb_ref[...])

        @pl.when(cb == 0)
        def _():
            acc_ref[...] = contrib

        @pl.when(cb > 0)
        def _():
            acc_ref[...] += contrib

        @pl.when(cb == ncb - 1)
        def _():
            x1v = x1_ref[...]
            r = _rms(x1v)
            xn = x1v * r
            dh2 = acc_ref[...]
            dg_ref[...] += _rowsum(dh2 * xn)
            dx1_ref[...] = dx2v + _rms_bwd(dh2 * g_ref[...], xn, r)

    row = lambda n: pl.BlockSpec((tm, n), lambda i, c: (nt - 1 - i, 0))
    colb = lambda: pl.BlockSpec((tm, FF_CW), lambda i, c: (nt - 1 - i, c))
    halo = lambda: pl.BlockSpec((8, FF_CW), lambda i, c: (jnp.maximum((nt - 1 - i) * hb - 1, 0), c))
    return pl.pallas_call(
        body, name="ffn_bwd", grid=(nt, ncb),
        in_specs=[row(D_MODEL), colb(), colb(), halo(), halo(),
                  pl.BlockSpec((3, FF_CW), lambda i, c: (0, c)),
                  pl.BlockSpec((3, FF_CW), lambda i, c: (0, ncb + c)),
                  pl.BlockSpec((1, FF_CW), lambda i, c: (0, c)),
                  pl.BlockSpec((1, FF_CW), lambda i, c: (0, ncb + c)),
                  pl.BlockSpec((FF_CW, D_MODEL), lambda i, c: (c, 0)),
                  pl.BlockSpec((D_MODEL, FF_CW), lambda i, c: (0, c)),
                  pl.BlockSpec((D_MODEL, FF_CW), lambda i, c: (0, ncb + c)),
                  row(D_MODEL), _full((1, D_MODEL))],
        out_specs=[colb(), colb(), row(D_MODEL), _full((2 * ncb, 8, FF_CW)), _full((1, D_MODEL))],
        out_shape=[_sds((S, D_FF), MXU), _sds((S, D_FF), MXU), _sds((S, D_MODEL)), _sds((2 * ncb, 8, FF_CW)),
                   _sds((1, D_MODEL))],
        scratch_shapes=[pltpu.VMEM((tm, D_MODEL), F32), pltpu.VMEM((ncb, 2, 8, FF_CW), F32)],
        compiler_params=_cp("arbitrary", "arbitrary"),
    )(dx2, upa, upb, upa, upb, conv_w, conv_w, conv_b, conv_b, w_down, w_up, w_up, x1, g_ffn)


def _mix_bwd(dx1, gl, ya, yb, ys, uv, w_out, w_pa, w_pb, w_glu, b_glu, g_sgu, ws, ws_t, bias_s, tm):
    S = dx1.shape[0]

    def body(dx1_ref, gl_ref, ya_ref, yb_ref, ys_ref, uv_ref, wout_ref, wpa_ref, wpb_ref, wglu_ref, bglu_ref, gs_ref,
             ws_ref, wst_ref, bias_ref,
             dgl_ref, dya_ref, dyb_ref, dz_ref, dys_ref, duv_ref, dbglu_ref, dgs_ref, dws_ref, dbs_ref,
             du2_ref, dvn_ref):
        i = pl.program_id(0)

        @pl.when(i == 0)
        def _():
            dbglu_ref[...] = jnp.zeros_like(dbglu_ref)
            dgs_ref[...] = jnp.zeros_like(dgs_ref)
            dws_ref[...] = jnp.zeros_like(dws_ref)
            dbs_ref[...] = jnp.zeros_like(dbs_ref)

        dm = _dot_nt(dx1_ref[...].astype(MXU), wout_ref[...])
        glv = gl_ref[...]
        ga = _sigmoid(glv[:, :D_MODEL])
        gb = _sigmoid(glv[:, D_MODEL:])
        dgl_ref[:, :D_MODEL] = (dm * ya_ref[...] * ga * (1.0 - ga)).astype(MXU)
        dgl_ref[:, D_MODEL:] = (dm * yb_ref[...] * gb * (1.0 - gb)).astype(MXU)
        dyab = (dm * ga).astype(MXU)
        dybb = (dm * gb).astype(MXU)
        dya_ref[...] = dyab
        dyb_ref[...] = dybb

        dyap = _dot_nt(dyab, wpa_ref[...])
        yg, dgelu = _gelu_and_grad(ys_ref[...])
        sz = _sigmoid(_dot(yg.astype(MXU), wglu_ref[...]) + bglu_ref[...])
        dz = dyap * yg * sz * (1.0 - sz)
        dzb = dz.astype(MXU)
        dz_ref[...] = dzb
        dbglu_ref[...] += _rowsum(dz)
        dys_ref[...] = (dyap * sz + _dot_nt(dzb, wglu_ref[...])) * dgelu

        dsg = _dot_nt(dybb, wpb_ref[...])
        uvg, duvg = _gelu_and_grad(uv_ref[...])
        u2 = uvg[:, :SGU_W]
        v2 = uvg[:, SGU_W:]
        rv = _rms(v2)
        vhat = v2 * rv
        gs = gs_ref[...]
        vnb = (vhat * gs).astype(MXU)
        grp = lax.broadcasted_iota(jnp.int32, (CHUNK, SGU_W), 1) // SGU_D
        tril = (lax.broadcasted_iota(jnp.int32, (CHUNK, CHUNK), 0)
                >= lax.broadcasted_iota(jnp.int32, (CHUNK, CHUNK), 1))
        for c in range(tm // CHUNK):
            rs = slice(c * CHUNK, (c + 1) * CHUNK)
            vc = vnb[rs]
            mixed = _sgu_mix(vc, ws_ref, grp) + bias_ref[...]
            dsg_c = dsg[rs]
            du2_ref[rs, :] = dsg_c * mixed
            dmx = dsg_c * u2[rs]
            dbs_ref[...] += dmx
            dmb = dmx.astype(MXU)
            dvn_ref[rs, :] = _sgu_mix(dmb, wst_ref, grp)
            for g in range(SGU_G):
                part = _dot_nt(jnp.where(grp == g, dmb, jnp.zeros((), MXU)), vc)
                dws_ref[g] += jnp.where(tril, part, 0.0)
        dvn = dvn_ref[...]
        dgs_ref[...] += _rowsum(dvn * vhat)
        dv2 = _rms_bwd(dvn * gs, vhat, rv)
        duv_ref[:, :SGU_W] = (du2_ref[...] * duvg[:, :SGU_W]).astype(MXU)
        duv_ref[:, SGU_W:] = (dv2 * duvg[:, SGU_W:]).astype(MXU)

    row = lambda n: pl.BlockSpec((tm, n), lambda i: (i, 0))
    return pl.pallas_call(
        body, name="mix_bwd", grid=(S // tm,),
        in_specs=[row(D_MODEL), row(2 * D_MODEL), row(D_MODEL), row(D_MODEL), row(SSM_W), row(2 * SGU_W),
                  _full(w_out.shape), _full(w_pa.shape), _full(w_pb.shape), _full(w_glu.shape), _full(b_glu.shape),
                  _full(g_sgu.shape), _full(ws.shape), _full(ws_t.shape), _full(bias_s.shape)],
        out_specs=[row(2 * D_MODEL), row(D_MODEL), row(D_MODEL), row(SSM_W), row(SSM_W), row(2 * SGU_W),
                   _full((1, SSM_W)), _full((1, SGU_W)), _full((SGU_G, CHUNK, CHUNK)), _full((CHUNK, SGU_W))],
        out_shape=[_sds((S, 2 * D_MODEL), MXU), _sds((S, D_MODEL), MXU), _sds((S, D_MODEL), MXU), _sds((S, SSM_W), MXU),
                   _sds((S, SSM_W)), _sds((S, 2 * SGU_W), MXU),
                   _sds((1, SSM_W)), _sds((1, SGU_W)), _sds((SGU_G, CHUNK, CHUNK)), _sds((CHUNK, SGU_W))],
        scratch_shapes=[pltpu.VMEM((tm, SGU_W), F32), pltpu.VMEM((tm, SGU_W), F32)],
        compiler_params=_cp("arbitrary"),
    )(dx1, gl, ya, yb, ys, uv, w_out, w_pa, w_pb, w_glu, b_glu, g_sgu, ws, ws_t, bias_s)


def _s5_bwd(dys, us, st_re, st_im, abar_re, abar_im, b_re, b_im, c_re, c_im, d_skip, tm):
    S = us.shape[0]
    nt = S // tm
    w = 8 * SSM_P
    hb = tm // 8

    def body(dys_ref, us_ref, str_ref, sti_ref, hr_ref, hi_ref, ar_ref, ai_ref, br_ref, bi_ref, cr_ref, ci_ref, d_ref,
             dus_ref, dab_ref, dd_ref, dbr_ref, dbi_ref, dcr_ref, dci_ref, tab_ref, car_ref, gr_ref, gi_ref):
        i = pl.program_id(1)
        ri = nt - 1 - i

        @pl.when(i == 0)
        def _():
            car_ref[...] = jnp.zeros_like(car_ref)
            for k, t in enumerate(_scan_tables(ar_ref[...], -ai_ref[...], True)):
                tab_ref[k] = t
            for r in (dab_ref, dd_ref, dbr_ref, dbi_ref, dcr_ref, dci_ref):
                r[...] = jnp.zeros_like(r)

        dys_v = dys_ref[...]
        dyb = dys_v.astype(MXU)
        gr_ref[...] = _dot(dyb, cr_ref[0])
        gi_ref[...] = -_dot(dyb, ci_ref[0])

        def grp(kk, carry):
            r0 = pl.multiple_of((hb - 1 - kk) * 8, 8)
            xr, xi = _scan_group(gr_ref[pl.ds(r0, 8), :], gi_ref[pl.ds(r0, 8), :], tab_ref, carry[0], carry[1], True)
            gr_ref[pl.ds(r0, 8), :] = xr
            gi_ref[pl.ds(r0, 8), :] = xi
            return xr[0:1, :], xi[0:1, :]

        cr, ci = lax.fori_loop(0, hb, grp, (car_ref[0:1, :], car_ref[1:2, :]))
        car_ref[0:1, :] = cr
        car_ref[1:2, :] = ci

        gsr = gr_ref[...]
        gsi = gi_ref[...]
        sr = str_ref[...]
        si = sti_ref[...]
        rows = lax.broadcasted_iota(jnp.int32, (tm, w), 0)
        first = ri == 0
        spr = jnp.where(rows == 0, jnp.where(first, 0.0, hr_ref[7:8, :]), pltpu.roll(sr, 1, 0))
        spi = jnp.where(rows == 0, jnp.where(first, 0.0, hi_ref[7:8, :]), pltpu.roll(si, 1, 0))
        dab_ref[0, 0:1, :] += _rowsum(gsr * spr + gsi * spi)
        dab_ref[0, 1:2, :] += _rowsum(gsi * spr - gsr * spi)

        gbr = gsr.astype(MXU)
        gbi = gsi.astype(MXU)
        u = us_ref[...]
        ub = u.astype(MXU)
        dus_ref[...] = (_dot_nt(gbr, br_ref[0]) + _dot_nt(gbi, bi_ref[0]) + d_ref[...] * dys_v).astype(MXU)
        dd_ref[0, 0:1, :] += _rowsum(dys_v * u)
        dbr_ref[0] += _dot_tn(ub, gbr)
        dbi_ref[0] += _dot_tn(ub, gbi)
        dcr_ref[0] += _dot_tn(dyb, sr.astype(MXU))
        dci_ref[0] -= _dot_tn(dyb, si.astype(MXU))

    blk = lambda: pl.BlockSpec((1, 8 * SSM_H, w), lambda j, i: (j, 0, 0))
    rowl = lambda: pl.BlockSpec((tm, LANES), lambda j, i: (nt - 1 - i, j))
    roww = lambda: pl.BlockSpec((tm, w), lambda j, i: (nt - 1 - i, j))
    halo = lambda: pl.BlockSpec((8, w), lambda j, i: (jnp.maximum((nt - 1 - i) * hb - 1, 0), j))
    return pl.pallas_call(
        body, name="s5_bwd", grid=(SSM_BLK, nt),
        in_specs=[rowl(), rowl(), roww(), roww(), halo(), halo(),
                  pl.BlockSpec((1, w), lambda j, i: (0, j)), pl.BlockSpec((1, w), lambda j, i: (0, j)),
                  blk(), blk(), blk(), blk(),
                  pl.BlockSpec((1, LANES), lambda j, i: (0, j))],
        out_specs=[rowl(),
                   pl.BlockSpec((1, 8, w), lambda j, i: (j, 0, 0)), pl.BlockSpec((1, 8, LANES), lambda j, i: (j, 0, 0)),
                   blk(), blk(), blk(), blk()],
        out_shape=[_sds((S, SSM_W), MXU), _sds((SSM_BLK, 8, w)), _sds((SSM_BLK, 8, LANES)),
                   _sds((SSM_BLK, 8 * SSM_H, w)), _sds((SSM_BLK, 8 * SSM_H, w)),
                   _sds((SSM_BLK, 8 * SSM_H, w)), _sds((SSM_BLK, 8 * SSM_H, w))],
        scratch_shapes=[pltpu.VMEM((8, 8, w), F32), pltpu.VMEM((8, w), F32),
                        pltpu.VMEM((tm, w), F32), pltpu.VMEM((tm, w), F32)],
        compiler_params=_cp("parallel", "arbitrary"),
    )(dys, us, st_re, st_im, st_re, st_im, abar_re, abar_im, b_re, b_im, c_re, c_im, d_skip)


def _in_bwd(dus, duv, dgl, dx1, x, g_mix, w_in, tm):
    S = x.shape[0]

    def body(dus_ref, duv_ref, dgl_ref, dx1_ref, x_ref, g_ref, w_ref, gx_ref, dg_ref):
        @pl.when(pl.program_id(0) == 0)
        def _():
            dg_ref[...] = jnp.zeros_like(dg_ref)

        dh = (_dot_nt(dus_ref[...], w_ref[:, 0:SSM_W])
              + _dot_nt(duv_ref[...], w_ref[:, SSM_W:SSM_W + 2 * SGU_W])
              + _dot_nt(dgl_ref[...], w_ref[:, SSM_W + 2 * SGU_W:]))
        xv = x_ref[...]
        r = _rms(xv)
        xn = xv * r
        dg_ref[...] += _rowsum(dh * xn)
        gx_ref[...] = dx1_ref[...] + _rms_bwd(dh * g_ref[...], xn, r)

    row = lambda n: pl.BlockSpec((tm, n), lambda i: (i, 0))
    return pl.pallas_call(
        body, name="in_bwd", grid=(S // tm,),
        in_specs=[row(SSM_W), row(2 * SGU_W), row(2 * D_MODEL), row(D_MODEL), row(D_MODEL), _full((1, D_MODEL)),
                  _full(w_in.shape)],
        out_specs=[row(D_MODEL), _full((1, D_MODEL))],
        out_shape=[_sds((S, D_MODEL)), _sds((1, D_MODEL))],
        compiler_params=_cp("arbitrary"),
    )(dus, duv, dgl, dx1, x, g_mix, w_in)


def _pick(n, cands):
    for c in cands:
        if n % c == 0:
            return c
    return n


def _wgrad(a, b, name):
    S, K = a.shape
    N = b.shape[1]
    ts = _pick(S, (512, 256, 128))
    tk = _pick(K, (512, 256, 128))
    tn = _pick(N, (1024, 512, 256, 128))
    ns = S // ts

    def body(a_ref, b_ref, o_ref):
        part = _dot_tn(a_ref[...].astype(MXU), b_ref[...].astype(MXU))

        @pl.when(pl.program_id(2) == 0)
        def _():
            o_ref[...] = part

        @pl.when(pl.program_id(2) > 0)
        def _():
            o_ref[...] += part

    return pl.pallas_call(
        body, name=name, grid=(K // tk, N // tn, ns),
        in_specs=[pl.BlockSpec((ts, tk), lambda k, n, s: (s, k)), pl.BlockSpec((ts, tn), lambda k, n, s: (s, n))],
        out_specs=pl.BlockSpec((tk, tn), lambda k, n, s: (k, n)),
        out_shape=_sds((K, N)),
        compiler_params=_cp("parallel", "parallel", "arbitrary"),
    )(a, b)


def _tile(S, want):
    return want if S % want == 0 else S


def _local_step(x, tgt, p):
    S = x.shape[0]
    tm = _tile(S, 256)
    tl = _tile(S, 512)

    rep = lambda a: jnp.repeat(a, SSM_H, axis=0)
    are = rep(p["a_re"])
    aim = rep(p["a_im"])
    ldt = jnp.broadcast_to(rep(p["log_dt"].reshape(SSM_G, 1)), are.shape)
    br_t = p["b_re"].transpose(0, 2, 1).reshape(are.shape)
    bi_t = p["b_im"].transpose(0, 2, 1).reshape(are.shape)
    abr, abi, bbr, bbi = _s5_params_fwd(are, aim, ldt, br_t, bi_t)
    head = lambda a: a.reshape(SSM_G, SSM_H, SSM_P)[:, 0, :].reshape(1, SSM_G * SSM_P)
    abar_re, abar_im = head(abr), head(abi)
    bd_br = _blockdiag(bbr).astype(MXU)
    bd_bi = _blockdiag(bbi).astype(MXU)
    bd_cr = _blockdiag(p["c_re"].reshape(are.shape)).astype(MXU)
    bd_ci = _blockdiag(p["c_im"].reshape(are.shape)).astype(MXU)
    d_skip = p["d_skip"].reshape(1, SSM_W)

    tril = jnp.tril(jnp.ones((CHUNK, CHUNK), dtype=bool))
    ws = jnp.where(tril[None], p["w_s"], 0.0)
    ws_b = ws.astype(MXU)
    ws_t = ws.transpose(0, 2, 1).astype(MXU)
    bias_s = jnp.repeat(p["b_s"].T, SGU_D, axis=1)

    g_mix = p["g_mix"].reshape(1, D_MODEL)
    g_ffn = p["g_ffn"].reshape(1, D_MODEL)
    g_final = p["g_final"].reshape(1, D_MODEL)
    g_sgu = p["g_sgu"].reshape(1, SGU_W)
    b_glu = p["b_glu"].reshape(1, SSM_W)
    conv_b = p["conv_b"].reshape(1, 2 * D_FF)

    h1, us, uv, gl = _in_fwd(x, g_mix, p["w_in"], tm)
    st_re, st_im, ys = _s5_fwd(us, abar_re, abar_im, bd_br, bd_bi, bd_cr, bd_ci, d_skip, tl)
    yg, yap, sg, ya, yb, m, x1, h2 = _mix_fwd(x, ys, uv, gl, p["w_glu"], b_glu, p["w_proj_a"], g_sgu, ws_b, bias_s,
                                              p["w_proj_b"], p["w_out"], g_ffn, tm)
    upa, upb, ff, dx2, loss, dg_final = _ffn_fwd(h2, x1, tgt, p["w_up"], p["conv_w"], conv_b, p["w_down"], g_final, tl)

    dupa, dupb, dx1, dconv, dg_ffn = _ffn_bwd(dx2, upa, upb, x1, p["w_up"], p["conv_w"], conv_b, p["w_down"], g_ffn, tl)
    dgl, dya, dyb, dz, dys, duv, db_glu, dg_sgu, dws, dbs = _mix_bwd(
        dx1, gl, ya, yb, ys, uv, p["w_out"], p["w_proj_a"], p["w_proj_b"], p["w_glu"], b_glu, g_sgu, ws_b, ws_t, bias_s, tm)
    dus, dab, dd, dbbr, dbbi, dcr, dci = _s5_bwd(dys, us, st_re, st_im, abar_re, abar_im, bd_br, bd_bi, bd_cr, bd_ci,
                                                 d_skip, tl)
    grad_x, dg_mix = _in_bwd(dus, duv, dgl, dx1, x, g_mix, p["w_in"], tm)

    spread = lambda v: jnp.repeat(v.reshape(SSM_G, SSM_P), SSM_H, axis=0) * (1.0 / SSM_H)
    dabr = spread(dab[:, 0, :])
    dabi = spread(dab[:, 1, :])
    dare, daim, dldt, dbr_t, dbi_t = _s5_params_bwd(are, aim, ldt, br_t, bi_t, dabr, dabi,
                                                    _unblockdiag(dbbr), _unblockdiag(dbbi))
    fold = lambda a: a.reshape(SSM_G, SSM_H, SSM_P).sum(axis=1)
    unt = lambda a: a.reshape(SSM_G, SSM_H, SSM_P).transpose(0, 2, 1)

    dconv = dconv.reshape(2 * FF_NCB, 8, FF_CW)
    grads = {
        "g_mix": dg_mix,
        "w_in": jnp.concatenate([_wgrad(h1, dus, "wgrad_in_s"), _wgrad(h1, duv, "wgrad_in_uv"),
                                 _wgrad(h1, dgl, "wgrad_in_gl")], axis=1),
        "a_re": fold(dare), "a_im": fold(daim), "log_dt": fold(dldt).sum(axis=1),
        "b_re": unt(dbr_t), "b_im": unt(dbi_t),
        "c_re": _unblockdiag(dcr).reshape(SSM_G, SSM_H, SSM_P),
        "c_im": _unblockdiag(dci).reshape(SSM_G, SSM_H, SSM_P),
        "d_skip": dd[:, 0, :].reshape(SSM_W),
        "w_glu": _wgrad(yg, dz, "wgrad_glu"),
        "b_glu": db_glu,
        "w_proj_a": _wgrad(yap, dya, "wgrad_pa"),
        "g_sgu": dg_sgu,
        "w_s": dws,
        "b_s": dbs.reshape(CHUNK, SGU_G, SGU_D).sum(axis=-1).T,
        "w_proj_b": _wgrad(sg, dyb, "wgrad_pb"),
        "w_out": _wgrad(m, dx1, "wgrad_out"),
        "g_ffn": dg_ffn,
        "w_up": jnp.concatenate([_wgrad(h2, dupa, "wgrad_up_a"), _wgrad(h2, dupb, "wgrad_up_b")], axis=1),
        "conv_w": dconv[:, 0:3, :].transpose(1, 0, 2).reshape(3, 2 * D_FF),
        "conv_b": dconv[:, 3, :].reshape(2 * D_FF),
        "w_down": _wgrad(ff, dx2, "wgrad_down"),
        "g_final": dg_final,
    }
    return loss, grad_x, grads


_ANY = pl.BlockSpec(memory_space=pl.ANY)
_MESH = pl.DeviceIdType.MESH


def _allgather(buf, name):
    def body(x_ref, out_ref, send_sems, recv_sems, local_sem):
        x, y, c = lax.axis_index("x"), lax.axis_index("y"), lax.axis_index("c")
        me, sibling = (x, y, c), (x, y, 1 - c)
        chips = [(1 - x, y), (x, 1 - y), (1 - x, 1 - y)]

        def slot(px, py, pc):
            return out_ref.at[4 * px + 2 * py + pc]

        def copy(k, block, to, src=None):
            return pltpu.make_async_remote_copy(
                src_ref=slot(*block) if src is None else src, dst_ref=slot(*block),
                send_sem=send_sems.at[k], recv_sem=recv_sems.at[k], device_id=to, device_id_type=_MESH)

        mine = pltpu.make_async_copy(x_ref, slot(*me), local_sem)
        mine.start()
        first = [copy(0, me, sibling, src=x_ref)]
        first += [copy(1 + j, me, (*chip, c), src=x_ref) for j, chip in enumerate(chips)]
        for cp in first:
            cp.start()
        passed = [copy(4 + j, (*chip, c), sibling) for j, chip in enumerate(chips)]
        for j, chip in enumerate(chips):
            copy(1 + j, (*chip, c), me).wait_recv()
            passed[j].start()
        copy(0, sibling, me).wait_recv()
        for j, chip in enumerate(chips):
            copy(4 + j, (*chip, 1 - c), me).wait_recv()
        for cp in first + passed:
            cp.wait_send()
        mine.wait()

    return pl.pallas_call(
        body, name=name, in_specs=[_ANY], out_specs=_ANY,
        out_shape=_sds((N_DEV,) + buf.shape, buf.dtype),
        scratch_shapes=[pltpu.SemaphoreType.DMA((7,)), pltpu.SemaphoreType.DMA((7,)), pltpu.SemaphoreType.DMA],
    )(buf)


def _all_to_all(send, name):
    def body(send_ref, recv_ref, send_sems, recv_sems, local_sem):
        x, y, c = lax.axis_index("x"), lax.axis_index("y"), lax.axis_index("c")
        me = 4 * x + 2 * y + c
        mine = pltpu.make_async_copy(send_ref.at[me], recv_ref.at[me], local_sem)
        mine.start()
        copies = []
        for k in range(1, N_DEV):
            px = 1 - x if k & 4 else x
            py = 1 - y if k & 2 else y
            pc = 1 - c if k & 1 else c
            peer = 4 * px + 2 * py + pc
            cp = pltpu.make_async_remote_copy(
                src_ref=send_ref.at[peer], dst_ref=recv_ref.at[me],
                send_sem=send_sems.at[k - 1], recv_sem=recv_sems.at[k - 1],
                device_id=(px, py, pc), device_id_type=_MESH)
            cp.start()
            landing = pltpu.make_async_remote_copy(
                src_ref=send_ref.at[peer], dst_ref=recv_ref.at[peer],
                send_sem=send_sems.at[k - 1], recv_sem=recv_sems.at[k - 1],
                device_id=(px, py, pc), device_id_type=_MESH)
            copies.append((cp, landing))
        for _, landing in copies:
            landing.wait_recv()
        for cp, _ in copies:
            cp.wait_send()
        mine.wait()

    return pl.pallas_call(
        body, name=name, in_specs=[_ANY], out_specs=_ANY, out_shape=_sds(send.shape, send.dtype),
        scratch_shapes=[pltpu.SemaphoreType.DMA((7,)), pltpu.SemaphoreType.DMA((7,)), pltpu.SemaphoreType.DMA],
    )(send)


def _adamw(w, g, m, v):
    m2 = ADAM_B1 * m + (1.0 - ADAM_B1) * g
    v2 = ADAM_B2 * v + (1.0 - ADAM_B2) * (g * g)
    m_hat = m2 / (1.0 - ADAM_B1 ** ADAM_STEP)
    v_hat = v2 / (1.0 - ADAM_B2 ** ADAM_STEP)
    delta = -ADAM_LR * (m_hat / (jnp.sqrt(v_hat) + ADAM_EPS) + ADAM_WD * w)
    return delta, m2, v2


def _adam_packed(parts, w, m, v, tr, name):
    R = w.shape[0]
    n_parts = parts.shape[0]

    def body(p_ref, w_ref, m_ref, v_ref, g_ref, d_ref, m2_ref, v2_ref):
        g = p_ref[0].astype(F32)
        for s in range(1, n_parts):
            g = g + p_ref[s].astype(F32)
        g_ref[...] = g
        d_ref[...], m2_ref[...], v2_ref[...] = _adamw(w_ref[...], g, m_ref[...], v_ref[...])

    row = lambda: pl.BlockSpec((tr, LANES), lambda i: (i, 0))
    return pl.pallas_call(
        body, name=name, grid=(R // tr,),
        in_specs=[pl.BlockSpec((n_parts, tr, LANES), lambda i: (0, i, 0)), row(), row(), row()],
        out_specs=[row(), row(), row(), row()], out_shape=[_sds((R, LANES))] * 4,
        compiler_params=_cp("parallel"),
    )(parts, w, m, v)


def _sum_slots(parts, name):
    R = parts.shape[1]

    def body(p_ref, o_ref):
        g = p_ref[0]
        for s in range(1, N_DEV):
            g = g + p_ref[s]
        o_ref[...] = g

    return pl.pallas_call(body, name=name, out_shape=_sds((R, LANES)))(parts)


def _pad_to(a, n, axis):
    extra = n - a.shape[axis]
    if extra == 0:
        return a
    widths = [(0, 0)] * a.ndim
    widths[axis] = (0, extra)
    return jnp.pad(a, widths)


def _ceil_to(n, k):
    return -(-n // k) * k


def _pack_rows(flats, rows_multiple):
    parts = [_pad_to(f, _ceil_to(f.shape[-1], LANES), f.ndim - 1) for f in flats]
    cat = jnp.concatenate(parts, axis=-1)
    total = _ceil_to(cat.shape[-1], LANES * rows_multiple)
    cat = _pad_to(cat, total, cat.ndim - 1)
    return cat.reshape(cat.shape[:-1] + (total // LANES, LANES))


def _unpack_rows(buf, sizes):
    flat = buf.reshape(buf.shape[:-2] + (-1,))
    out, off = [], 0
    for n in sizes:
        out.append(flat[..., off:off + n])
        off += _ceil_to(n, LANES)
    return out


_SHARDED = ("w_in", "w_glu", "w_proj_a", "w_proj_b", "w_out", "w_up", "conv_w", "w_down")
_COL_SHARDED = ("w_in", "w_proj_a", "w_proj_b", "w_up", "conv_w")
_SMALL = ("g_mix", "a_re", "a_im", "log_dt", "b_re", "b_im", "c_re", "c_im", "d_skip", "b_glu", "g_sgu", "w_s", "b_s",
          "g_ffn", "conv_b", "g_final")
_SHARD_ROWS_MULTIPLE = 512
_SMALL_ROWS_MULTIPLE = 16 * N_DEV


def _to_shards(full, name):
    r, c = full.shape
    if name in _COL_SHARDED:
        return full.reshape(r, N_DEV, c // N_DEV).transpose(1, 0, 2).reshape(N_DEV, -1)
    return full.reshape(N_DEV, -1)


def _from_shards(sh, name, shard_shape):
    r, c = shard_shape
    blocks = sh.reshape(N_DEV, r, c)
    if name in _COL_SHARDED:
        return blocks.transpose(1, 0, 2).reshape(r, N_DEV * c)
    return blocks.reshape(N_DEV * r, c)


def _f32_as_bf16(a):
    h = lax.bitcast_convert_type(a, jnp.bfloat16)
    return h.reshape(a.shape[:-1] + (2 * a.shape[-1],))


def _bf16_as_f32(a):
    return lax.bitcast_convert_type(a.reshape(a.shape[:-1] + (a.shape[-1] // 2, 2)), F32)


def kernel(x, g_mix, w_in, a_re, a_im, log_dt, b_re, b_im, c_re, c_im, d_skip, w_glu, b_glu, w_proj_a, g_sgu, w_s, b_s, w_proj_b, w_out, g_ffn, w_up, conv_w, conv_b, w_down, g_final, loss_target, m_g_mix, m_w_in, m_a_re, m_a_im, m_log_dt, m_b_re, m_b_im, m_c_re, m_c_im, m_d_skip, m_w_glu, m_b_glu, m_w_proj_a, m_g_sgu, m_w_s, m_b_s, m_w_proj_b, m_w_out, m_g_ffn, m_w_up, m_conv_w, m_conv_b, m_w_down, m_g_final, v_g_mix, v_w_in, v_a_re, v_a_im, v_log_dt, v_b_re, v_b_im, v_c_re, v_c_im, v_d_skip, v_w_glu, v_b_glu, v_w_proj_a, v_g_sgu, v_w_s, v_b_s, v_w_proj_b, v_w_out, v_g_ffn, v_w_up, v_conv_w, v_conv_b, v_w_down, v_g_final):
    args = dict(locals())
    weights = {n: args[n] for n in _SHARDED + _SMALL}
    axes = ("x", "y", "c")

    shard_shapes = {n: weights[n].shape[-2:] for n in _SHARDED}
    flat = []
    for n in _SHARDED:
        w = weights[n].reshape(-1)
        flat.append(_f32_as_bf16(w) if n == "conv_w" else w.astype(MXU))
    sizes_w = [f.shape[0] for f in flat]
    gathered = _allgather(_pack_rows(flat, _SHARD_ROWS_MULTIPLE), "allgather_weights")
    p = {}
    for n, sh in zip(_SHARDED, _unpack_rows(gathered, sizes_w)):
        if n == "conv_w":
            sh = _bf16_as_f32(sh)
        p[n] = _from_shards(sh, n, shard_shapes[n])
    for n in _SMALL:
        p[n] = weights[n][0] if n != "g_final" else weights[n]

    loss_part, grad_x, grads = _local_step(x[0], loss_target[0], p)
    loss = lax.psum(loss_part[0, 0], axes)

    sizes_g = [math.prod(shard_shapes[n]) for n in _SHARDED]
    send_big = _pack_rows([_to_shards(grads[n], n).astype(MXU) for n in _SHARDED], _SHARD_ROWS_MULTIPLE)
    R = send_big.shape[1]
    small_shapes = {n: (weights[n].shape[1:] if n != "g_final" else weights[n].shape) for n in _SMALL}
    small_sizes = [math.prod(small_shapes[n]) for n in _SMALL]
    pack_small = lambda d: _pack_rows([d[n].reshape(-1) for n in _SMALL], _SMALL_ROWS_MULTIPLE)
    g_small = pack_small(grads)
    rs8 = g_small.shape[0] // N_DEV
    send_small = _f32_as_bf16(g_small.reshape(N_DEV, rs8, LANES)).reshape(N_DEV, 2 * rs8, LANES)
    recv = _all_to_all(jnp.concatenate([send_big, send_small], axis=1), "all_to_all_grads")
    small_parts = _bf16_as_f32(recv[:, R:, :].reshape(N_DEV, rs8, 2 * LANES))
    small_mine = _sum_slots(small_parts, "sum_small")
    g_small_all = _allgather(small_mine, "allgather_small").reshape(N_DEV * rs8, LANES)

    def pack_big(prefix):
        return _pack_rows([args[prefix + n].reshape(-1) for n in _SHARDED], _SHARD_ROWS_MULTIPLE)
    big = _adam_packed(recv, pack_big(""), pack_big("m_"), pack_big("v_"), _SHARD_ROWS_MULTIPLE, "adam_sharded")
    small = _adam_packed(g_small_all.reshape(1, N_DEV * rs8, LANES),
                         pack_small({n: args[n] for n in _SMALL}), pack_small({n: args["m_" + n] for n in _SMALL}),
                         pack_small({n: args["v_" + n] for n in _SMALL}), N_DEV * rs8, "adam_small")

    out = {}
    for kind, bufs_big, bufs_small in zip(("grad_", "delta_", "new_m_", "new_v_"), big, small):
        for n, v in zip(_SHARDED, _unpack_rows(bufs_big, sizes_g)):
            out[kind + n] = v.reshape(weights[n].shape)
        for n, v in zip(_SMALL, _unpack_rows(bufs_small, small_sizes)):
            out[kind + n] = v.reshape(weights[n].shape)
    order = ("g_mix", "w_in", "a_re", "a_im", "log_dt", "b_re", "b_im", "c_re", "c_im", "d_skip", "w_glu", "b_glu",
             "w_proj_a", "g_sgu", "w_s", "b_s", "w_proj_b", "w_out", "g_ffn", "w_up", "conv_w", "conv_b", "w_down",
             "g_final")
    res = [loss, grad_x.reshape(x.shape)]
    for kind in ("grad_", "delta_", "new_m_", "new_v_"):
        res += [out[kind + n] for n in order]
    return tuple(res)
```

```python
import functools
import math

import jax
import jax.numpy as jnp
from jax import lax
from jax.experimental import pallas as pl
from jax.experimental.pallas import tpu as pltpu

F32 = jnp.float32
MXU = jnp.bfloat16
EPS = 1e-6

D_MODEL = 1024
SSM_W = 512
SSM_G, SSM_H, SSM_P = 32, 16, 64
SSM_BLK = 4
SGU_W = 512
SGU_G, SGU_D, CHUNK = 8, 64, 128
D_FF = 2816
N_DEV = 8
FF_CW = 2 * D_FF // N_DEV
FF_NCB = D_FF // FF_CW
LANES = 128

ADAM_LR, ADAM_B1, ADAM_B2, ADAM_EPS, ADAM_WD, ADAM_STEP = 0.001, 0.9, 0.999, 1e-08, 0.01, 10

VMEM_LIMIT = 48 * 1024 * 1024


def _cp(*sem):
    return pltpu.CompilerParams(dimension_semantics=sem, vmem_limit_bytes=VMEM_LIMIT)


def _full(shape):
    n = len(shape)
    return pl.BlockSpec(shape, lambda *_: (0,) * n)


def _sds(shape, dtype=F32):
    return jax.ShapeDtypeStruct(shape, dtype)


def _dot(a, b):
    return jnp.dot(a, b, preferred_element_type=F32)


def _dot_nt(a, b):
    return lax.dot_general(a, b, (((1,), (1,)), ((), ())), preferred_element_type=F32)


def _dot_tn(a, b):
    return lax.dot_general(a, b, (((0,), (0,)), ((), ())), preferred_element_type=F32)


_GELU_C = math.sqrt(2.0 / math.pi)


def _gelu(x):
    return 0.5 * x * (1.0 + jnp.tanh(_GELU_C * (x + 0.044715 * (x * x * x))))


def _gelu_and_grad(x):
    t = jnp.tanh(_GELU_C * (x + 0.044715 * (x * x * x)))
    g = 0.5 * x * (1.0 + t)
    dg = 0.5 * (1.0 + t) + 0.5 * x * (1.0 - t * t) * (_GELU_C * (1.0 + 3.0 * 0.044715 * (x * x)))
    return g, dg


def _sigmoid(x):
    return 1.0 / (1.0 + jnp.exp(-x))


def _rms(x):
    return lax.rsqrt(jnp.mean(x * x, axis=-1, keepdims=True) + EPS)


def _rms_bwd(dxn, xn, r):
    return r * (dxn - xn * jnp.mean(dxn * xn, axis=-1, keepdims=True))


def _rowsum(x):
    return jnp.sum(x, axis=0, keepdims=True)


def _s5_disc(are, aim, ldt, br, bi):
    dt = jnp.exp(ldt)
    mag = jnp.exp(dt * are)
    abr = mag * jnp.cos(dt * aim)
    abi = mag * jnp.sin(dt * aim)
    den = are * are + aim * aim
    nr = abr - 1.0
    ni = abi
    fr = (nr * are + ni * aim) / den
    fi = (ni * are - nr * aim) / den
    return abr, abi, fr * br - fi * bi, fr * bi + fi * br


def _s5_params_fwd(are, aim, ldt, br, bi):
    def body(are_ref, aim_ref, ldt_ref, br_ref, bi_ref, o0, o1, o2, o3):
        outs = _s5_disc(are_ref[...], aim_ref[...], ldt_ref[...], br_ref[...], bi_ref[...])
        for o, v in zip((o0, o1, o2, o3), outs):
            o[...] = v
    shp = are.shape
    return pl.pallas_call(body, name="s5_params_fwd", out_shape=[_sds(shp)] * 4)(are, aim, ldt, br, bi)


def _s5_params_bwd(are, aim, ldt, br, bi, dabr, dabi, dbr, dbi):
    def body(are_ref, aim_ref, ldt_ref, br_ref, bi_ref, c0, c1, c2, c3, o0, o1, o2, o3, o4):
        prim = (are_ref[...], aim_ref[...], ldt_ref[...], br_ref[...], bi_ref[...])
        _, vjp = jax.vjp(_s5_disc, *prim)
        outs = vjp((c0[...], c1[...], c2[...], c3[...]))
        for o, v in zip((o0, o1, o2, o3, o4), outs):
            o[...] = v
    shp = are.shape
    return pl.pallas_call(body, name="s5_params_bwd", out_shape=[_sds(shp)] * 5)(
        are, aim, ldt, br, bi, dabr, dabi, dbr, dbi)


def _blockdiag(m_t):
    m = m_t.reshape(SSM_BLK, 8, SSM_H, 1, SSM_P)
    eye = jnp.eye(8, dtype=bool).reshape(1, 8, 1, 8, 1)
    return jnp.where(eye, m, jnp.zeros((), m_t.dtype)).reshape(SSM_BLK, 8 * SSM_H, 8 * SSM_P)


def _unblockdiag(pc):
    m = pc.reshape(SSM_BLK, 8, SSM_H, 8, SSM_P)
    return jnp.einsum("jghgp->jghp", m).reshape(SSM_G * SSM_H, SSM_P)


def _in_fwd(x, g_mix, w_in, tm):
    S = x.shape[0]

    def body(x_ref, g_ref, w_ref, h_ref, us_ref, uv_ref, gl_ref):
        xv = x_ref[...]
        h = (xv * _rms(xv) * g_ref[...]).astype(MXU)
        h_ref[...] = h
        us_ref[...] = _dot(h, w_ref[:, 0:SSM_W])
        uv_ref[...] = _dot(h, w_ref[:, SSM_W:SSM_W + 2 * SGU_W])
        gl_ref[...] = _dot(h, w_ref[:, SSM_W + 2 * SGU_W:])

    row = lambda n: pl.BlockSpec((tm, n), lambda i: (i, 0))
    return pl.pallas_call(
        body, name="in_fwd", grid=(S // tm,),
        in_specs=[row(D_MODEL), _full((1, D_MODEL)), _full(w_in.shape)],
        out_specs=[row(D_MODEL), row(SSM_W), row(2 * SGU_W), row(2 * D_MODEL)],
        out_shape=[_sds((S, D_MODEL), MXU), _sds((S, SSM_W)), _sds((S, 2 * SGU_W)), _sds((S, 2 * D_MODEL))],
        compiler_params=_cp("parallel"),
    )(x, g_mix, w_in)


def _scan_tables(ar, ai, reverse):
    n = ar.shape[-1]
    def mul(p, q):
        return p[0] * q[0] - p[1] * q[1], p[0] * q[1] + p[1] * q[0]
    a1 = (ar, ai)
    a2 = mul(a1, a1)
    a3 = mul(a2, a1)
    a4 = mul(a2, a2)
    a5 = mul(a4, a1)
    a6 = mul(a4, a2)
    a7 = mul(a4, a3)
    a8 = mul(a4, a4)
    pw = (a1, a2, a3, a4, a5, a6, a7, a8)
    rows = lax.broadcasted_iota(jnp.int32, (8, n), 0)
    tabs = []
    for s, a in ((1, a1), (2, a2), (4, a4)):
        keep = (rows + s <= 7) if reverse else (rows >= s)
        for comp in a:
            tabs.append(jnp.where(keep, jnp.broadcast_to(comp, (8, n)), 0.0))
    for c in range(2):
        q = jnp.zeros((8, n), F32)
        for r in range(8):
            e = (8 - r) if reverse else (r + 1)
            q = jnp.where(rows == r, jnp.broadcast_to(pw[e - 1][c], (8, n)), q)
        tabs.append(q)
    return tabs


def _scan_group(xr, xi, tab_ref, cr, ci, reverse):
    for t, s in enumerate((1, 2, 4)):
        pr = tab_ref[2 * t]
        pi = tab_ref[2 * t + 1]
        sh = (8 - s) if reverse else s
        sr = pltpu.roll(xr, sh, 0)
        si = pltpu.roll(xi, sh, 0)
        xr, xi = xr + pr * sr - pi * si, xi + pr * si + pi * sr
    qr = tab_ref[6]
    qi = tab_ref[7]
    return xr + qr * cr - qi * ci, xi + qr * ci + qi * cr


def _s5_fwd(us, abar_re, abar_im, b_re, b_im, c_re, c_im, d_skip, tm):
    S = us.shape[0]
    nt = S // tm
    w = 8 * SSM_P

    def body(us_ref, ar_ref, ai_ref, br_ref, bi_ref, cr_ref, ci_ref, d_ref, str_ref, sti_ref, ys_ref, tab_ref, car_ref):
        i = pl.program_id(1)

        @pl.when(i == 0)
        def _():
            car_ref[...] = jnp.zeros_like(car_ref)
            for k, t in enumerate(_scan_tables(ar_ref[...], ai_ref[...], False)):
                tab_ref[k] = t

        u = us_ref[...]
        ub = u.astype(MXU)
        str_ref[...] = _dot(ub, br_ref[0])
        sti_ref[...] = _dot(ub, bi_ref[0])

        def grp(k, carry):
            r0 = pl.multiple_of(k * 8, 8)
            xr, xi = _scan_group(str_ref[pl.ds(r0, 8), :], sti_ref[pl.ds(r0, 8), :], tab_ref, carry[0], carry[1], False)
            str_ref[pl.ds(r0, 8), :] = xr
            sti_ref[pl.ds(r0, 8), :] = xi
            return xr[7:8, :], xi[7:8, :]

        cr, ci = lax.fori_loop(0, tm // 8, grp, (car_ref[0:1, :], car_ref[1:2, :]))
        car_ref[0:1, :] = cr
        car_ref[1:2, :] = ci
        y = _dot_nt(str_ref[...].astype(MXU), cr_ref[0]) - _dot_nt(sti_ref[...].astype(MXU), ci_ref[0])
        ys_ref[...] = y + d_ref[...] * u

    blk = lambda: pl.BlockSpec((1, 8 * SSM_H, w), lambda j, i: (j, 0, 0))
    return pl.pallas_call(
        body, name="s5_fwd", grid=(SSM_BLK, nt),
        in_specs=[pl.BlockSpec((tm, LANES), lambda j, i: (i, j)),
                  pl.BlockSpec((1, w), lambda j, i: (0, j)), pl.BlockSpec((1, w), lambda j, i: (0, j)),
                  blk(), blk(), blk(), blk(),
                  pl.BlockSpec((1, LANES), lambda j, i: (0, j))],
        out_specs=[pl.BlockSpec((tm, w), lambda j, i: (i, j)), pl.BlockSpec((tm, w), lambda j, i: (i, j)),
                   pl.BlockSpec((tm, LANES), lambda j, i: (i, j))],
        out_shape=[_sds((S, SSM_BLK * w)), _sds((S, SSM_BLK * w)), _sds((S, SSM_W))],
        scratch_shapes=[pltpu.VMEM((8, 8, w), F32), pltpu.VMEM((8, w), F32)],
        compiler_params=_cp("parallel", "arbitrary"),
    )(us, abar_re, abar_im, b_re, b_im, c_re, c_im, d_skip)


def _sgu_mix(vnb, ws_ref, grp):
    acc = jnp.zeros(vnb.shape, F32)
    for g in range(SGU_G):
        acc = jnp.where(grp == g, _dot(ws_ref[g], vnb), acc)
    return acc


def _mix_fwd(x, ys, uv, gl, w_glu, b_glu, w_pa, g_sgu, ws, bias_s, w_pb, w_out, g_ffn, tm):
    S = x.shape[0]

    def body(x_ref, ys_ref, uv_ref, gl_ref, wglu_ref, bglu_ref, wpa_ref, gs_ref, ws_ref, bias_ref, wpb_ref, wout_ref,
             gf_ref, yg_ref, yap_ref, sg_ref, ya_ref, yb_ref, m_ref, x1_ref, h2_ref):
        yg = _gelu(ys_ref[...])
        ygb = yg.astype(MXU)
        yg_ref[...] = ygb
        z = _dot(ygb, wglu_ref[...]) + bglu_ref[...]
        yapb = (yg * _sigmoid(z)).astype(MXU)
        yap_ref[...] = yapb
        ya = _dot(yapb, wpa_ref[...])
        ya_ref[...] = ya

        uvg = _gelu(uv_ref[...])
        u2 = uvg[:, :SGU_W]
        v2 = uvg[:, SGU_W:]
        vnb = (v2 * _rms(v2) * gs_ref[...]).astype(MXU)
        grp = lax.broadcasted_iota(jnp.int32, (CHUNK, SGU_W), 1) // SGU_D
        for c in range(tm // CHUNK):
            rs = slice(c * CHUNK, (c + 1) * CHUNK)
            mixed = _sgu_mix(vnb[rs], ws_ref, grp) + bias_ref[...]
            sg_ref[rs, :] = (u2[rs] * mixed).astype(MXU)
        yb = _dot(sg_ref[...], wpb_ref[...])
        yb_ref[...] = yb

        glv = gl_ref[...]
        m = _sigmoid(glv[:, :D_MODEL]) * ya + _sigmoid(glv[:, D_MODEL:]) * yb
        mb = m.astype(MXU)
        m_ref[...] = mb
        x1 = x_ref[...] + _dot(mb, wout_ref[...])
        x1_ref[...] = x1
        h2_ref[...] = (x1 * _rms(x1) * gf_ref[...]).astype(MXU)

    row = lambda n: pl.BlockSpec((tm, n), lambda i: (i, 0))
    return pl.pallas_call(
        body, name="mix_fwd", grid=(S // tm,),
        in_specs=[row(D_MODEL), row(SSM_W), row(2 * SGU_W), row(2 * D_MODEL),
                  _full(w_glu.shape), _full(b_glu.shape), _full(w_pa.shape), _full(g_sgu.shape), _full(ws.shape),
                  _full(bias_s.shape), _full(w_pb.shape), _full(w_out.shape), _full(g_ffn.shape)],
        out_specs=[row(SSM_W), row(SSM_W), row(SGU_W), row(D_MODEL), row(D_MODEL), row(D_MODEL), row(D_MODEL),
                   row(D_MODEL)],
        out_shape=[_sds((S, SSM_W), MXU), _sds((S, SSM_W), MXU), _sds((S, SGU_W), MXU), _sds((S, D_MODEL)),
                   _sds((S, D_MODEL)), _sds((S, D_MODEL), MXU), _sds((S, D_MODEL)), _sds((S, D_MODEL), MXU)],
        compiler_params=_cp("parallel"),
    )(x, ys, uv, gl, w_glu, b_glu, w_pa, g_sgu, ws, bias_s, w_pb, w_out, g_ffn)


def _conv_taps(u, prev8, rows):
    t1 = prev8[7:8, :]
    t0 = prev8[6:7, :]
    s1 = jnp.where(rows == 0, t1, pltpu.roll(u, 1, 0))
    s2 = jnp.where(rows == 0, t0, jnp.where(rows == 1, t1, pltpu.roll(u, 2, 0)))
    return s1, s2


def _ffn_fwd(h2, x1, tgt, w_up, conv_w, conv_b, w_down, g_final, tm):
    S = h2.shape[0]
    nt = S // tm
    ncb = FF_NCB

    def body(h2_ref, wa_ref, wb_ref, cwa_ref, cwb_ref, cba_ref, cbb_ref, wd_ref, x1_ref, gf_ref, tgt_ref,
             up_ref, ff_ref, dx2_ref, loss_ref, dgf_ref, acc_ref, tail_ref):
        i = pl.program_id(0)
        cb = pl.program_id(1)

        @pl.when(i == 0)
        def _():
            tail_ref[cb] = jnp.zeros((2, 8, FF_CW), F32)

        @pl.when(jnp.logical_and(i == 0, cb == 0))
        def _():
            loss_ref[...] = jnp.zeros_like(loss_ref)
            dgf_ref[...] = jnp.zeros_like(dgf_ref)

        h2v = h2_ref[...]
        ua = _dot(h2v, wa_ref[0])
        ub = _dot(h2v, wb_ref[0])
        up_ref[0, 0] = ua
        up_ref[1, 0] = ub
        rows = lax.broadcasted_iota(jnp.int32, (tm, FF_CW), 0)
        s1a, s2a = _conv_taps(ua, tail_ref[cb, 0], rows)
        s1b, s2b = _conv_taps(ub, tail_ref[cb, 1], rows)
        tail_ref[cb, 0] = ua[tm - 8:tm, :]
        tail_ref[cb, 1] = ub[tm - 8:tm, :]
        cwa = cwa_ref[0]
        cwb = cwb_ref[0]
        a = cwa[0:1] * s2a + cwa[1:2] * s1a + cwa[2:3] * ua + cba_ref[0]
        b = cwb[0:1] * s2b + cwb[1:2] * s1b + cwb[2:3] * ub + cbb_ref[0]
        ffb = (a * _sigmoid(a) * b).astype(MXU)
        ff_ref[0] = ffb
        contrib = _dot(ffb, wd_ref[...])

        @pl.when(cb == 0)
        def _():
            acc_ref[...] = contrib

        @pl.when(cb > 0)
        def _():
            acc_ref[...] += contrib

        @pl.when(cb == ncb - 1)
        def _():
            x2 = x1_ref[...] + acc_ref[...]
            r = _rms(x2)
            xn = x2 * r
            g = gf_ref[...]
            diff = xn * g - tgt_ref[...]
            loss_ref[...] += (0.5 / D_MODEL) * jnp.sum(diff * diff)
            dy = diff * (1.0 / D_MODEL)
            dgf_ref[...] += _rowsum(dy * xn)
            dx2_ref[...] = _rms_bwd(dy * g, xn, r)

    row = lambda n: pl.BlockSpec((tm, n), lambda i, c: (i, 0))
    gate = lambda r: pl.BlockSpec((1, r, FF_CW), lambda i, c: (c, 0, 0))
    lin = lambda r: pl.BlockSpec((1, r, FF_CW), lambda i, c: (ncb + c, 0, 0))
    return pl.pallas_call(
        body, name="ffn_fwd", grid=(nt, ncb),
        in_specs=[row(D_MODEL), gate(D_MODEL), lin(D_MODEL), gate(3), lin(3), gate(1), lin(1),
                  pl.BlockSpec((FF_CW, D_MODEL), lambda i, c: (c, 0)),
                  row(D_MODEL), _full((1, D_MODEL)), row(D_MODEL)],
        out_specs=[pl.BlockSpec((2, 1, tm, FF_CW), lambda i, c: (0, c, i, 0)),
                   pl.BlockSpec((1, tm, FF_CW), lambda i, c: (c, i, 0)),
                   row(D_MODEL), _full((1, LANES)), _full((1, D_MODEL))],
        out_shape=[_sds((2, ncb, S, FF_CW)), _sds((ncb, S, FF_CW), MXU), _sds((S, D_MODEL)),
                   _sds((1, LANES)), _sds((1, D_MODEL))],
        scratch_shapes=[pltpu.VMEM((tm, D_MODEL), F32), pltpu.VMEM((ncb, 2, 8, FF_CW), F32)],
        compiler_params=_cp("arbitrary", "arbitrary"),
    )(h2, w_up, w_up, conv_w, conv_w, conv_b, conv_b, w_down, x1, g_final, tgt)


def _ffn_bwd(dx2, up, x1, w_up, conv_w, conv_b, w_down, g_ffn, tm):
    S = dx2.shape[0]
    nt = S // tm
    ncb = FF_NCB
    hb = tm // 8

    def body(dx2_ref, up_ref, hp_ref, cwa_ref, cwb_ref, cba_ref, cbb_ref, wd_ref, wa_ref, wb_ref,
             x1_ref, g_ref, dup_ref, dx1_ref, dconv_ref, dg_ref, acc_ref, head_ref):
        i = pl.program_id(0)
        cb = pl.program_id(1)
        ri = nt - 1 - i

        @pl.when(i == 0)
        def _():
            head_ref[cb] = jnp.zeros((2, 8, FF_CW), F32)
            dconv_ref[cb] = jnp.zeros((8, FF_CW), F32)
            dconv_ref[ncb + cb] = jnp.zeros((8, FF_CW), F32)

        @pl.when(jnp.logical_and(i == 0, cb == 0))
        def _():
            dg_ref[...] = jnp.zeros_like(dg_ref)

        dx2v = dx2_ref[...]
        dff = _dot_nt(dx2v.astype(MXU), wd_ref[...])
        ua = up_ref[0, 0]
        ub = up_ref[1, 0]
        rows = lax.broadcasted_iota(jnp.int32, (tm, FF_CW), 0)
        first = ri == 0
        s1a, s2a = _conv_taps(ua, jnp.where(first, 0.0, hp_ref[0, 0]), rows)
        s1b, s2b = _conv_taps(ub, jnp.where(first, 0.0, hp_ref[1, 0]), rows)
        cwa = cwa_ref[0]
        cwb = cwb_ref[0]
        a = cwa[0:1] * s2a + cwa[1:2] * s1a + cwa[2:3] * ua + cba_ref[0]
        b = cwb[0:1] * s2b + cwb[1:2] * s1b + cwb[2:3] * ub + cbb_ref[0]
        sa = _sigmoid(a)
        da = dff * b * (sa * (1.0 + a * (1.0 - sa)))
        db = dff * (a * sa)

        def conv_bwd(dup, head8, cw):
            h0 = head8[0:1, :]
            h1 = head8[1:2, :]
            n1 = jnp.where(rows == tm - 1, h0, pltpu.roll(dup, tm - 1, 0))
            n2 = jnp.where(rows == tm - 2, h0, jnp.where(rows == tm - 1, h1, pltpu.roll(dup, tm - 2, 0)))
            return cw[2:3] * dup + cw[1:2] * n1 + cw[0:1] * n2

        dpa = conv_bwd(da, head_ref[cb, 0], cwa).astype(MXU)
        dpb = conv_bwd(db, head_ref[cb, 1], cwb).astype(MXU)
        head_ref[cb, 0] = da[0:8, :]
        head_ref[cb, 1] = db[0:8, :]
        dup_ref[0, 0] = dpa
        dup_ref[1, 0] = dpb
        for slot, dup, s2, s1, u in ((cb, da, s2a, s1a, ua), (ncb + cb, db, s2b, s1b, ub)):
            dconv_ref[slot, 0:1, :] += _rowsum(dup * s2)
            dconv_ref[slot, 1:2, :] += _rowsum(dup * s1)
            dconv_ref[slot, 2:3, :] += _rowsum(dup * u)
            dconv_ref[slot, 3:4, :] += _rowsum(dup)
        contrib = _dot_nt(dpa, wa_ref[0]) + _dot_nt(dpb, wb_ref[0])

        @pl.when(cb == 0)
        def _():
            acc_ref[...] = contrib

        @pl.when(cb > 0)
        def _():
            acc_ref[...] += contrib

        @pl.when(cb == ncb - 1)
        def _():
            x1v = x1_ref[...]
            r = _rms(x1v)
            xn = x1v * r
            dh2 = acc_ref[...]
            dg_ref[...] += _rowsum(dh2 * xn)
            dx1_ref[...] = dx2v + _rms_bwd(dh2 * g_ref[...], xn, r)

    row = lambda n: pl.BlockSpec((tm, n), lambda i, c: (nt - 1 - i, 0))
    colb = lambda: pl.BlockSpec((2, 1, tm, FF_CW), lambda i, c: (0, c, nt - 1 - i, 0))
    halo = lambda: pl.BlockSpec((2, 1, 8, FF_CW), lambda i, c: (0, c, jnp.maximum((nt - 1 - i) * hb - 1, 0), 0))
    gate = lambda r: pl.BlockSpec((1, r, FF_CW), lambda i, c: (c, 0, 0))
    lin = lambda r: pl.BlockSpec((1, r, FF_CW), lambda i, c: (ncb + c, 0, 0))
    return pl.pallas_call(
        body, name="ffn_bwd", grid=(nt, ncb),
        in_specs=[row(D_MODEL), colb(), halo(), gate(3), lin(3), gate(1), lin(1),
                  pl.BlockSpec((FF_CW, D_MODEL), lambda i, c: (c, 0)),
                  gate(D_MODEL), lin(D_MODEL), row(D_MODEL), _full((1, D_MODEL))],
        out_specs=[colb(), row(D_MODEL), _full((2 * ncb, 8, FF_CW)), _full((1, D_MODEL))],
        out_shape=[_sds((2, ncb, S, FF_CW), MXU), _sds((S, D_MODEL)), _sds((2 * ncb, 8, FF_CW)), _sds((1, D_MODEL))],
        scratch_shapes=[pltpu.VMEM((tm, D_MODEL), F32), pltpu.VMEM((ncb, 2, 8, FF_CW), F32)],
        compiler_params=_cp("arbitrary", "arbitrary"),
    )(dx2, up, up, conv_w, conv_w, conv_b, conv_b, w_down, w_up, w_up, x1, g_ffn)


def _mix_bwd(dx1, gl, ya, yb, ys, uv, w_out, w_pa, w_pb, w_glu, b_glu, g_sgu, ws, ws_t, bias_s, tm):
    S = dx1.shape[0]

    def body(dx1_ref, gl_ref, ya_ref, yb_ref, ys_ref, uv_ref, wout_ref, wpa_ref, wpb_ref, wglu_ref, bglu_ref, gs_ref,
             ws_ref, wst_ref, bias_ref,
             dgl_ref, dya_ref, dyb_ref, dz_ref, dys_ref, duv_ref, dbglu_ref, dgs_ref, dws_ref, dbs_ref,
             du2_ref, dvn_ref):
        i = pl.program_id(0)

        @pl.when(i == 0)
        def _():
            dbglu_ref[...] = jnp.zeros_like(dbglu_ref)
            dgs_ref[...] = jnp.zeros_like(dgs_ref)
            dws_ref[...] = jnp.zeros_like(dws_ref)
            dbs_ref[...] = jnp.zeros_like(dbs_ref)

        dm = _dot_nt(dx1_ref[...].astype(MXU), wout_ref[...])
        glv = gl_ref[...]
        ga = _sigmoid(glv[:, :D_MODEL])
        gb = _sigmoid(glv[:, D_MODEL:])
        dgl_ref[:, :D_MODEL] = (dm * ya_ref[...] * ga * (1.0 - ga)).astype(MXU)
        dgl_ref[:, D_MODEL:] = (dm * yb_ref[...] * gb * (1.0 - gb)).astype(MXU)
        dyab = (dm * ga).astype(MXU)
        dybb = (dm * gb).astype(MXU)
        dya_ref[...] = dyab
        dyb_ref[...] = dybb

        dyap = _dot_nt(dyab, wpa_ref[...])
        yg, dgelu = _gelu_and_grad(ys_ref[...])
        sz = _sigmoid(_dot(yg.astype(MXU), wglu_ref[...]) + bglu_ref[...])
        dz = dyap * yg * sz * (1.0 - sz)
        dzb = dz.astype(MXU)
        dz_ref[...] = dzb
        dbglu_ref[...] += _rowsum(dz)
        dys_ref[...] = (dyap * sz + _dot_nt(dzb, wglu_ref[...])) * dgelu

        dsg = _dot_nt(dybb, wpb_ref[...])
        uvg, duvg = _gelu_and_grad(uv_ref[...])
        u2 = uvg[:, :SGU_W]
        v2 = uvg[:, SGU_W:]
        rv = _rms(v2)
        vhat = v2 * rv
        gs = gs_ref[...]
        vnb = (vhat * gs).astype(MXU)
        grp = lax.broadcasted_iota(jnp.int32, (CHUNK, SGU_W), 1) // SGU_D
        tril = (lax.broadcasted_iota(jnp.int32, (CHUNK, CHUNK), 0)
                >= lax.broadcasted_iota(jnp.int32, (CHUNK, CHUNK), 1))
        for c in range(tm // CHUNK):
            rs = slice(c * CHUNK, (c + 1) * CHUNK)
            vc = vnb[rs]
            mixed = _sgu_mix(vc, ws_ref, grp) + bias_ref[...]
            dsg_c = dsg[rs]
            du2_ref[rs, :] = dsg_c * mixed
            dmx = dsg_c * u2[rs]
            dbs_ref[...] += dmx
            dmb = dmx.astype(MXU)
            dvn_ref[rs, :] = _sgu_mix(dmb, wst_ref, grp)
            for g in range(SGU_G):
                part = _dot_nt(jnp.where(grp == g, dmb, jnp.zeros((), MXU)), vc)
                dws_ref[g] += jnp.where(tril, part, 0.0)
        dvn = dvn_ref[...]
        dgs_ref[...] += _rowsum(dvn * vhat)
        dv2 = _rms_bwd(dvn * gs, vhat, rv)
        duv_ref[:, :SGU_W] = (du2_ref[...] * duvg[:, :SGU_W]).astype(MXU)
        duv_ref[:, SGU_W:] = (dv2 * duvg[:, SGU_W:]).astype(MXU)

    row = lambda n: pl.BlockSpec((tm, n), lambda i: (i, 0))
    return pl.pallas_call(
        body, name="mix_bwd", grid=(S // tm,),
        in_specs=[row(D_MODEL), row(2 * D_MODEL), row(D_MODEL), row(D_MODEL), row(SSM_W), row(2 * SGU_W),
                  _full(w_out.shape), _full(w_pa.shape), _full(w_pb.shape), _full(w_glu.shape), _full(b_glu.shape),
                  _full(g_sgu.shape), _full(ws.shape), _full(ws_t.shape), _full(bias_s.shape)],
        out_specs=[row(2 * D_MODEL), row(D_MODEL), row(D_MODEL), row(SSM_W), row(SSM_W), row(2 * SGU_W),
                   _full((1, SSM_W)), _full((1, SGU_W)), _full((SGU_G, CHUNK, CHUNK)), _full((CHUNK, SGU_W))],
        out_shape=[_sds((S, 2 * D_MODEL), MXU), _sds((S, D_MODEL), MXU), _sds((S, D_MODEL), MXU), _sds((S, SSM_W), MXU),
                   _sds((S, SSM_W)), _sds((S, 2 * SGU_W), MXU),
                   _sds((1, SSM_W)), _sds((1, SGU_W)), _sds((SGU_G, CHUNK, CHUNK)), _sds((CHUNK, SGU_W))],
        scratch_shapes=[pltpu.VMEM((tm, SGU_W), F32), pltpu.VMEM((tm, SGU_W), F32)],
        compiler_params=_cp("arbitrary"),
    )(dx1, gl, ya, yb, ys, uv, w_out, w_pa, w_pb, w_glu, b_glu, g_sgu, ws, ws_t, bias_s)


def _s5_bwd(dys, us, st_re, st_im, abar_re, abar_im, b_re, b_im, c_re, c_im, d_skip, tm):
    S = us.shape[0]
    nt = S // tm
    w = 8 * SSM_P
    hb = tm // 8

    def body(dys_ref, us_ref, str_ref, sti_ref, hr_ref, hi_ref, ar_ref, ai_ref, br_ref, bi_ref, cr_ref, ci_ref, d_ref,
             dus_ref, dab_ref, dd_ref, dbr_ref, dbi_ref, dcr_ref, dci_ref, tab_ref, car_ref, gr_ref, gi_ref):
        i = pl.program_id(1)
        ri = nt - 1 - i

        @pl.when(i == 0)
        def _():
            car_ref[...] = jnp.zeros_like(car_ref)
            for k, t in enumerate(_scan_tables(ar_ref[...], -ai_ref[...], True)):
                tab_ref[k] = t
            for r in (dab_ref, dd_ref, dbr_ref, dbi_ref, dcr_ref, dci_ref):
                r[...] = jnp.zeros_like(r)

        dys_v = dys_ref[...]
        dyb = dys_v.astype(MXU)
        gr_ref[...] = _dot(dyb, cr_ref[0])
        gi_ref[...] = -_dot(dyb, ci_ref[0])

        def grp(kk, carry):
            r0 = pl.multiple_of((hb - 1 - kk) * 8, 8)
            xr, xi = _scan_group(gr_ref[pl.ds(r0, 8), :], gi_ref[pl.ds(r0, 8), :], tab_ref, carry[0], carry[1], True)
            gr_ref[pl.ds(r0, 8), :] = xr
            gi_ref[pl.ds(r0, 8), :] = xi
            return xr[0:1, :], xi[0:1, :]

        cr, ci = lax.fori_loop(0, hb, grp, (car_ref[0:1, :], car_ref[1:2, :]))
        car_ref[0:1, :] = cr
        car_ref[1:2, :] = ci

        gsr = gr_ref[...]
        gsi = gi_ref[...]
        sr = str_ref[...]
        si = sti_ref[...]
        rows = lax.broadcasted_iota(jnp.int32, (tm, w), 0)
        first = ri == 0
        spr = jnp.where(rows == 0, jnp.where(first, 0.0, hr_ref[7:8, :]), pltpu.roll(sr, 1, 0))
        spi = jnp.where(rows == 0, jnp.where(first, 0.0, hi_ref[7:8, :]), pltpu.roll(si, 1, 0))
        dab_ref[0, 0:1, :] += _rowsum(gsr * spr + gsi * spi)
        dab_ref[0, 1:2, :] += _rowsum(gsi * spr - gsr * spi)

        gbr = gsr.astype(MXU)
        gbi = gsi.astype(MXU)
        u = us_ref[...]
        ub = u.astype(MXU)
        dus_ref[...] = (_dot_nt(gbr, br_ref[0]) + _dot_nt(gbi, bi_ref[0]) + d_ref[...] * dys_v).astype(MXU)
        dd_ref[0, 0:1, :] += _rowsum(dys_v * u)
        dbr_ref[0] += _dot_tn(ub, gbr)
        dbi_ref[0] += _dot_tn(ub, gbi)
        dcr_ref[0] += _dot_tn(dyb, sr.astype(MXU))
        dci_ref[0] -= _dot_tn(dyb, si.astype(MXU))

    blk = lambda: pl.BlockSpec((1, 8 * SSM_H, w), lambda j, i: (j, 0, 0))
    rowl = lambda: pl.BlockSpec((tm, LANES), lambda j, i: (nt - 1 - i, j))
    roww = lambda: pl.BlockSpec((tm, w), lambda j, i: (nt - 1 - i, j))
    halo = lambda: pl.BlockSpec((8, w), lambda j, i: (jnp.maximum((nt - 1 - i) * hb - 1, 0), j))
    return pl.pallas_call(
        body, name="s5_bwd", grid=(SSM_BLK, nt),
        in_specs=[rowl(), rowl(), roww(), roww(), halo(), halo(),
                  pl.BlockSpec((1, w), lambda j, i: (0, j)), pl.BlockSpec((1, w), lambda j, i: (0, j)),
                  blk(), blk(), blk(), blk(),
                  pl.BlockSpec((1, LANES), lambda j, i: (0, j))],
        out_specs=[rowl(),
                   pl.BlockSpec((1, 8, w), lambda j, i: (j, 0, 0)), pl.BlockSpec((1, 8, LANES), lambda j, i: (j, 0, 0)),
                   blk(), blk(), blk(), blk()],
        out_shape=[_sds((S, SSM_W), MXU), _sds((SSM_BLK, 8, w)), _sds((SSM_BLK, 8, LANES)),
                   _sds((SSM_BLK, 8 * SSM_H, w)), _sds((SSM_BLK, 8 * SSM_H, w)),
                   _sds((SSM_BLK, 8 * SSM_H, w)), _sds((SSM_BLK, 8 * SSM_H, w))],
        scratch_shapes=[pltpu.VMEM((8, 8, w), F32), pltpu.VMEM((8, w), F32),
                        pltpu.VMEM((tm, w), F32), pltpu.VMEM((tm, w), F32)],
        compiler_params=_cp("parallel", "arbitrary"),
    )(dys, us, st_re, st_im, st_re, st_im, abar_re, abar_im, b_re, b_im, c_re, c_im, d_skip)


def _in_bwd(dus, duv, dgl, dx1, x, g_mix, w_in, tm):
    S = x.shape[0]

    def body(dus_ref, duv_ref, dgl_ref, dx1_ref, x_ref, g_ref, w_ref, gx_ref, dg_ref):
        @pl.when(pl.program_id(0) == 0)
        def _():
            dg_ref[...] = jnp.zeros_like(dg_ref)

        dh = (_dot_nt(dus_ref[...], w_ref[:, 0:SSM_W])
              + _dot_nt(duv_ref[...], w_ref[:, SSM_W:SSM_W + 2 * SGU_W])
              + _dot_nt(dgl_ref[...], w_ref[:, SSM_W + 2 * SGU_W:]))
        xv = x_ref[...]
        r = _rms(xv)
        xn = xv * r
        dg_ref[...] += _rowsum(dh * xn)
        gx_ref[...] = dx1_ref[...] + _rms_bwd(dh * g_ref[...], xn, r)

    row = lambda n: pl.BlockSpec((tm, n), lambda i: (i, 0))
    return pl.pallas_call(
        body, name="in_bwd", grid=(S // tm,),
        in_specs=[row(SSM_W), row(2 * SGU_W), row(2 * D_MODEL), row(D_MODEL), row(D_MODEL), _full((1, D_MODEL)),
                  _full(w_in.shape)],
        out_specs=[row(D_MODEL), _full((1, D_MODEL))],
        out_shape=[_sds((S, D_MODEL)), _sds((1, D_MODEL))],
        compiler_params=_cp("arbitrary"),
    )(dus, duv, dgl, dx1, x, g_mix, w_in)


def _pick(n, cands):
    for c in cands:
        if n % c == 0:
            return c
    return n


def _wgrad_split(a, bs, nsplit, tk, name):
    S, K = a.shape
    widths = [b.shape[1] for b in bs]
    N = sum(widths)
    c = N // nsplit
    ts = _pick(S, (512, 256, 128))
    ns = S // ts

    def body(*refs):
        a_ref = refs[0]
        b_refs = refs[1:1 + len(bs)]
        o_ref = refs[1 + len(bs)]
        acc_ref = refs[2 + len(bs)]
        s = pl.program_id(1)
        av = a_ref[...].astype(MXU)
        off = 0
        for b_ref, wdt in zip(b_refs, widths):
            part = _dot_tn(av, b_ref[...].astype(MXU))

            @pl.when(s == 0)
            def _():
                acc_ref[:, off:off + wdt] = part

            @pl.when(s > 0)
            def _():
                acc_ref[:, off:off + wdt] += part

            off += wdt

        @pl.when(s == ns - 1)
        def _():
            for d in range(nsplit):
                o_ref[d] = acc_ref[:, c * d:c * (d + 1)].astype(MXU)

    return pl.pallas_call(
        body, name=name, grid=(K // tk, ns),
        in_specs=[pl.BlockSpec((ts, tk), lambda k, s: (s, k))]
                 + [pl.BlockSpec((ts, wdt), lambda k, s: (s, 0)) for wdt in widths],
        out_specs=pl.BlockSpec((nsplit, tk, c), lambda k, s: (0, k, 0)),
        out_shape=_sds((nsplit, K, c), MXU),
        scratch_shapes=[pltpu.VMEM((tk, N), F32)],
        compiler_params=_cp("parallel", "arbitrary"),
    )(a, *bs)


def _wgrad_blk(a3, b3, nblk, a_of, b_of, name):
    S, K = a3.shape[1:]
    N = b3.shape[2]
    ts = _pick(S, (512, 256, 128))
    ns = S // ts

    def body(a_ref, b_ref, o_ref, acc_ref):
        s = pl.program_id(1)
        part = _dot_tn(a_ref[0].astype(MXU), b_ref[0].astype(MXU))

        @pl.when(s == 0)
        def _():
            acc_ref[...] = part

        @pl.when(s > 0)
        def _():
            acc_ref[...] += part

        @pl.when(s == ns - 1)
        def _():
            o_ref[0] = acc_ref[...].astype(MXU)

    return pl.pallas_call(
        body, name=name, grid=(nblk, ns),
        in_specs=[pl.BlockSpec((1, ts, K), lambda b, s: (a_of(b), s, 0)),
                  pl.BlockSpec((1, ts, N), lambda b, s: (b_of(b), s, 0))],
        out_specs=pl.BlockSpec((1, K, N), lambda b, s: (b, 0, 0)),
        out_shape=_sds((nblk, K, N), MXU),
        scratch_shapes=[pltpu.VMEM((K, N), F32)],
        compiler_params=_cp("parallel", "arbitrary"),
    )(a3, b3)


def _assemble_cols(blocks_list, name):
    def body(*refs):
        n = len(blocks_list)
        for b_ref, o_ref in zip(refs[:n], refs[n:]):
            c = b_ref.shape[2]
            for d in range(N_DEV):
                o_ref[:, c * d:c * (d + 1)] = b_ref[d]

    return pl.pallas_call(
        body, name=name,
        out_shape=[_sds((b.shape[1], N_DEV * b.shape[2]), b.dtype) for b in blocks_list],
        compiler_params=pltpu.CompilerParams(vmem_limit_bytes=VMEM_LIMIT),
    )(*blocks_list)


def _tile(S, want):
    return want if S % want == 0 else S


def _local_step(x, tgt, p):
    S = x.shape[0]
    tm = _tile(S, 256)
    tl = _tile(S, 512)

    rep = lambda a: jnp.repeat(a, SSM_H, axis=0)
    are = rep(p["a_re"])
    aim = rep(p["a_im"])
    ldt = jnp.broadcast_to(rep(p["log_dt"].reshape(SSM_G, 1)), are.shape)
    br_t = p["b_re"].transpose(0, 2, 1).reshape(are.shape)
    bi_t = p["b_im"].transpose(0, 2, 1).reshape(are.shape)
    abr, abi, bbr, bbi = _s5_params_fwd(are, aim, ldt, br_t, bi_t)
    head = lambda a: a.reshape(SSM_G, SSM_H, SSM_P)[:, 0, :].reshape(1, SSM_G * SSM_P)
    abar_re, abar_im = head(abr), head(abi)
    bd_br = _blockdiag(bbr).astype(MXU)
    bd_bi = _blockdiag(bbi).astype(MXU)
    bd_cr = _blockdiag(p["c_re"].reshape(are.shape)).astype(MXU)
    bd_ci = _blockdiag(p["c_im"].reshape(are.shape)).astype(MXU)
    d_skip = p["d_skip"].reshape(1, SSM_W)

    tril = jnp.tril(jnp.ones((CHUNK, CHUNK), dtype=bool))
    ws = jnp.where(tril[None], p["w_s"], 0.0)
    ws_b = ws.astype(MXU)
    ws_t = ws.transpose(0, 2, 1).astype(MXU)
    bias_s = jnp.repeat(p["b_s"].T, SGU_D, axis=1)

    g_mix = p["g_mix"].reshape(1, D_MODEL)
    g_ffn = p["g_ffn"].reshape(1, D_MODEL)
    g_final = p["g_final"].reshape(1, D_MODEL)
    g_sgu = p["g_sgu"].reshape(1, SGU_W)
    b_glu = p["b_glu"].reshape(1, SSM_W)
    conv_b = p["conv_b"].reshape(N_DEV, 1, FF_CW)

    h1, us, uv, gl = _in_fwd(x, g_mix, p["w_in"], tm)
    st_re, st_im, ys = _s5_fwd(us, abar_re, abar_im, bd_br, bd_bi, bd_cr, bd_ci, d_skip, tl)
    yg, yap, sg, ya, yb, m, x1, h2 = _mix_fwd(x, ys, uv, gl, p["w_glu"], b_glu, p["w_proj_a"], g_sgu, ws_b, bias_s,
                                              p["w_proj_b"], p["w_out"], g_ffn, tm)
    up, ff, dx2, loss, dg_final = _ffn_fwd(h2, x1, tgt, p["w_up"], p["conv_w"], conv_b, p["w_down"], g_final, tl)

    dup, dx1, dconv, dg_ffn = _ffn_bwd(dx2, up, x1, p["w_up"], p["conv_w"], conv_b, p["w_down"], g_ffn, tl)
    dgl, dya, dyb, dz, dys, duv, db_glu, dg_sgu, dws, dbs = _mix_bwd(
        dx1, gl, ya, yb, ys, uv, p["w_out"], p["w_proj_a"], p["w_proj_b"], p["w_glu"], b_glu, g_sgu, ws_b, ws_t, bias_s, tm)
    dus, dab, dd, dbbr, dbbi, dcr, dci = _s5_bwd(dys, us, st_re, st_im, abar_re, abar_im, bd_br, bd_bi, bd_cr, bd_ci,
                                                 d_skip, tl)
    grad_x, dg_mix = _in_bwd(dus, duv, dgl, dx1, x, g_mix, p["w_in"], tm)

    spread = lambda v: jnp.repeat(v.reshape(SSM_G, SSM_P), SSM_H, axis=0) * (1.0 / SSM_H)
    dabr = spread(dab[:, 0, :])
    dabi = spread(dab[:, 1, :])
    dare, daim, dldt, dbr_t, dbi_t = _s5_params_bwd(are, aim, ldt, br_t, bi_t, dabr, dabi,
                                                    _unblockdiag(dbbr), _unblockdiag(dbbi))
    fold = lambda a: a.reshape(SSM_G, SSM_H, SSM_P).sum(axis=1)
    unt = lambda a: a.reshape(SSM_G, SSM_H, SSM_P).transpose(0, 2, 1)

    rows8 = lambda g: g.reshape(N_DEV, g.shape[1] // N_DEV, g.shape[2])
    grads = {
        "g_mix": dg_mix,
        "w_in": _wgrad_split(h1, [dus, duv, dgl], N_DEV, 256, "wgrad_in"),
        "a_re": fold(dare), "a_im": fold(daim), "log_dt": fold(dldt).sum(axis=1),
        "b_re": unt(dbr_t), "b_im": unt(dbi_t),
        "c_re": _unblockdiag(dcr).reshape(SSM_G, SSM_H, SSM_P),
        "c_im": _unblockdiag(dci).reshape(SSM_G, SSM_H, SSM_P),
        "d_skip": dd[:, 0, :].reshape(SSM_W),
        "w_glu": rows8(_wgrad_split(yg, [dz], 1, SSM_W, "wgrad_glu")),
        "b_glu": db_glu,
        "w_proj_a": _wgrad_split(yap, [dya], N_DEV, SSM_W, "wgrad_pa"),
        "g_sgu": dg_sgu,
        "w_s": dws,
        "b_s": dbs.reshape(CHUNK, SGU_G, SGU_D).sum(axis=-1).T,
        "w_proj_b": _wgrad_split(sg, [dyb], N_DEV, SGU_W, "wgrad_pb"),
        "w_out": rows8(_wgrad_split(m, [dx1], 1, 512, "wgrad_out")),
        "g_ffn": dg_ffn,
        "w_up": _wgrad_blk(h2[None], dup.reshape(N_DEV, S, FF_CW), N_DEV, lambda b: 0, lambda b: b, "wgrad_up"),
        "conv_w": dconv[:, 0:3, :],
        "conv_b": dconv[:, 3, :].reshape(2 * D_FF),
        "w_down": _wgrad_blk(ff, dx2[None], FF_NCB, lambda b: b, lambda b: 0, "wgrad_down").reshape(
            N_DEV, D_FF // N_DEV, D_MODEL),
        "g_final": dg_final,
    }
    return loss, grad_x, grads


_ANY = pl.BlockSpec(memory_space=pl.ANY)
_MESH = pl.DeviceIdType.MESH


def _allgather(shards, dtypes, name):
    n = len(shards)

    def body(*refs):
        in_refs, out_refs, stage = refs[:n], refs[n:2 * n], refs[2 * n:3 * n]
        send_sems, recv_sems, local_sems = refs[3 * n:]
        for a in range(n):
            stage[a][...] = in_refs[a][...].astype(dtypes[a])
        x, y, c = lax.axis_index("x"), lax.axis_index("y"), lax.axis_index("c")
        me, sibling = (x, y, c), (x, y, 1 - c)
        chips = [(1 - x, y), (x, 1 - y), (1 - x, 1 - y)]

        def slot(a, px, py, pc):
            return out_refs[a].at[4 * px + 2 * py + pc]

        def copy(a, k, block, to, src=None):
            return pltpu.make_async_remote_copy(
                src_ref=slot(a, *block) if src is None else src, dst_ref=slot(a, *block),
                send_sem=send_sems.at[a, k], recv_sem=recv_sems.at[a, k], device_id=to, device_id_type=_MESH)

        mine = [pltpu.make_async_copy(stage[a], slot(a, *me), local_sems.at[a]) for a in range(n)]
        for cp in mine:
            cp.start()
        first = []
        for j, chip in enumerate(chips):
            first += [copy(a, 1 + j, me, (*chip, c), src=stage[a]) for a in range(n)]
        first += [copy(a, 0, me, sibling, src=stage[a]) for a in range(n)]
        for cp in first:
            cp.start()
        passed = []
        for j, chip in enumerate(chips):
            for a in range(n):
                copy(a, 1 + j, (*chip, c), me).wait_recv()
                fwd = copy(a, 4 + j, (*chip, c), sibling)
                fwd.start()
                passed.append(fwd)
        for a in range(n):
            copy(a, 0, sibling, me).wait_recv()
        for j, chip in enumerate(chips):
            for a in range(n):
                copy(a, 4 + j, (*chip, 1 - c), me).wait_recv()
        for cp in first + passed:
            cp.wait_send()
        for cp in mine:
            cp.wait()

    vmem = pl.BlockSpec(memory_space=pltpu.VMEM)
    return pl.pallas_call(
        body, name=name, in_specs=[vmem] * n, out_specs=[_ANY] * n,
        out_shape=[_sds((N_DEV,) + s.shape, dt) for s, dt in zip(shards, dtypes)],
        scratch_shapes=[pltpu.VMEM(s.shape, dt) for s, dt in zip(shards, dtypes)]
                       + [pltpu.SemaphoreType.DMA((n, 7)), pltpu.SemaphoreType.DMA((n, 7)), pltpu.SemaphoreType.DMA((n,))],
        compiler_params=pltpu.CompilerParams(vmem_limit_bytes=VMEM_LIMIT),
    )(*shards)


def _all_to_all(sends, name):
    n = len(sends)

    def body(*refs):
        send_refs, recv_refs = refs[:n], refs[n:2 * n]
        send_sems, recv_sems, local_sems = refs[2 * n:]
        x, y, c = lax.axis_index("x"), lax.axis_index("y"), lax.axis_index("c")
        me = 4 * x + 2 * y + c
        mine = [pltpu.make_async_copy(send_refs[a].at[me], recv_refs[a].at[me], local_sems.at[a]) for a in range(n)]
        for cp in mine:
            cp.start()
        copies = []
        for k in (2, 4, 6, 3, 5, 7, 1):
            px = 1 - x if k & 4 else x
            py = 1 - y if k & 2 else y
            pc = 1 - c if k & 1 else c
            peer = 4 * px + 2 * py + pc
            for a in range(n):
                sems = dict(send_sem=send_sems.at[a, k - 1], recv_sem=recv_sems.at[a, k - 1],
                            device_id=(px, py, pc), device_id_type=_MESH)
                cp = pltpu.make_async_remote_copy(src_ref=send_refs[a].at[peer], dst_ref=recv_refs[a].at[me], **sems)
                cp.start()
                landing = pltpu.make_async_remote_copy(src_ref=send_refs[a].at[peer], dst_ref=recv_refs[a].at[peer],
                                                       **sems)
                copies.append((cp, landing))
        for _, landing in copies:
            landing.wait_recv()
        for cp, _ in copies:
            cp.wait_send()
        for cp in mine:
            cp.wait()

    return pl.pallas_call(
        body, name=name, in_specs=[_ANY] * n, out_specs=[_ANY] * n,
        out_shape=[_sds(s.shape, s.dtype) for s in sends],
        scratch_shapes=[pltpu.SemaphoreType.DMA((n, 7)), pltpu.SemaphoreType.DMA((n, 7)), pltpu.SemaphoreType.DMA((n,))],
    )(*sends)


def _adamw(w, g, m, v):
    m2 = ADAM_B1 * m + (1.0 - ADAM_B1) * g
    v2 = ADAM_B2 * v + (1.0 - ADAM_B2) * (g * g)
    m_hat = m2 / (1.0 - ADAM_B1 ** ADAM_STEP)
    v_hat = v2 / (1.0 - ADAM_B2 ** ADAM_STEP)
    delta = -ADAM_LR * (m_hat / (jnp.sqrt(v_hat) + ADAM_EPS) + ADAM_WD * w)
    return delta, m2, v2


def _adam_shard(parts, w, m, v, name):
    _, r, c = w.shape
    tr = 256 if r % 256 == 0 else r

    def body(p_ref, w_ref, m_ref, v_ref, g_ref, d_ref, m2_ref, v2_ref):
        g = p_ref[0].astype(F32)
        for s in range(1, N_DEV):
            g = g + p_ref[s].astype(F32)
        g_ref[0] = g
        d_ref[0], m2_ref[0], v2_ref[0] = _adamw(w_ref[0], g, m_ref[0], v_ref[0])

    row = lambda: pl.BlockSpec((1, tr, c), lambda i: (0, i, 0))
    return pl.pallas_call(
        body, name=name, grid=(r // tr,),
        in_specs=[pl.BlockSpec((N_DEV, tr, c), lambda i: (0, i, 0)), row(), row(), row()],
        out_specs=[row(), row(), row(), row()], out_shape=[_sds((1, r, c))] * 4,
        compiler_params=_cp("parallel"),
    )(parts, w, m, v)


def _adam_small(gs, ws, ms, vs, name):
    n = len(gs)

    def body(*refs):
        ins, outs = refs[:4 * n], refs[4 * n:]
        for i in range(n):
            g = ins[i][...]
            d, m2, v2 = _adamw(ins[n + i][...], g, ins[2 * n + i][...], ins[3 * n + i][...])
            outs[i][...] = d
            outs[n + i][...] = m2
            outs[2 * n + i][...] = v2

    res = pl.pallas_call(
        body, name=name, out_shape=[_sds(w.shape) for w in ws] * 3,
        compiler_params=pltpu.CompilerParams(vmem_limit_bytes=VMEM_LIMIT),
    )(*gs, *ws, *ms, *vs)
    return res[:n], res[n:2 * n], res[2 * n:]


def _sum_slots(parts, name):
    R = parts.shape[1]

    def body(p_ref, o_ref):
        g = p_ref[0]
        for s in range(1, N_DEV):
            g = g + p_ref[s]
        o_ref[...] = g

    return pl.pallas_call(body, name=name, out_shape=_sds((R, LANES)))(parts)


def _pad_to(a, n, axis):
    extra = n - a.shape[axis]
    if extra == 0:
        return a
    widths = [(0, 0)] * a.ndim
    widths[axis] = (0, extra)
    return jnp.pad(a, widths)


def _ceil_to(n, k):
    return -(-n // k) * k


def _pack_rows(flats, rows_multiple):
    parts = [_pad_to(f, _ceil_to(f.shape[-1], LANES), f.ndim - 1) for f in flats]
    cat = jnp.concatenate(parts, axis=-1)
    total = _ceil_to(cat.shape[-1], LANES * rows_multiple)
    cat = _pad_to(cat, total, cat.ndim - 1)
    return cat.reshape(cat.shape[:-1] + (total // LANES, LANES))


def _unpack_rows(buf, sizes):
    flat = buf.reshape(buf.shape[:-2] + (-1,))
    out, off = [], 0
    for n in sizes:
        out.append(flat[..., off:off + n])
        off += _ceil_to(n, LANES)
    return out


_BIG = ("w_in", "w_glu", "w_proj_a", "w_proj_b", "w_out", "w_up", "w_down")
_SMALL = ("g_mix", "a_re", "a_im", "log_dt", "b_re", "b_im", "c_re", "c_im", "d_skip", "b_glu", "g_sgu", "w_s", "b_s",
          "g_ffn", "conv_b", "g_final")
_SMALL_ROWS_MULTIPLE = 8 * N_DEV


def _as_2d(a):
    return a.reshape(-1, a.shape[-1]) if a.ndim > 1 else a.reshape(1, -1)


def kernel(x, g_mix, w_in, a_re, a_im, log_dt, b_re, b_im, c_re, c_im, d_skip, w_glu, b_glu, w_proj_a, g_sgu, w_s, b_s, w_proj_b, w_out, g_ffn, w_up, conv_w, conv_b, w_down, g_final, loss_target, m_g_mix, m_w_in, m_a_re, m_a_im, m_log_dt, m_b_re, m_b_im, m_c_re, m_c_im, m_d_skip, m_w_glu, m_b_glu, m_w_proj_a, m_g_sgu, m_w_s, m_b_s, m_w_proj_b, m_w_out, m_g_ffn, m_w_up, m_conv_w, m_conv_b, m_w_down, m_g_final, v_g_mix, v_w_in, v_a_re, v_a_im, v_log_dt, v_b_re, v_b_im, v_c_re, v_c_im, v_d_skip, v_w_glu, v_b_glu, v_w_proj_a, v_g_sgu, v_w_s, v_b_s, v_w_proj_b, v_w_out, v_g_ffn, v_w_up, v_conv_w, v_conv_b, v_w_down, v_g_final):
    args = dict(locals())
    me = 4 * lax.axis_index("x") + 2 * lax.axis_index("y") + lax.axis_index("c")

    gathered = _allgather([args[n][0] for n in _BIG] + [conv_w[0]], [MXU] * len(_BIG) + [F32], "allgather_weights")
    g = dict(zip(_BIG + ("conv_w",), gathered))
    w_in_full, w_pa_full, w_pb_full = _assemble_cols([g["w_in"], g["w_proj_a"], g["w_proj_b"]], "assemble_cols")
    p = {n: (args[n][0] if n != "g_final" else args[n]) for n in _SMALL}
    p.update(w_in=w_in_full, w_proj_a=w_pa_full, w_proj_b=w_pb_full,
             w_glu=g["w_glu"].reshape(SSM_W, SSM_W), w_out=g["w_out"].reshape(D_MODEL, D_MODEL),
             w_down=g["w_down"].reshape(D_FF, D_MODEL), w_up=g["w_up"], conv_w=g["conv_w"])

    loss_part, grad_x, grads = _local_step(x[0], loss_target[0], p)

    small_names = _SMALL + ("conv_w", "loss")
    small_g = dict(grads, loss=loss_part[0, 0:1])
    flats = [small_g[n].reshape(-1) for n in small_names]
    small_sizes = [f.shape[0] for f in flats]
    g_small = _pack_rows(flats, _SMALL_ROWS_MULTIPLE)
    rs8 = g_small.shape[0] // N_DEV
    recv = _all_to_all([grads[n] for n in _BIG] + [g_small.reshape(N_DEV, rs8, LANES)], "all_to_all_grads")
    small_mine = _sum_slots(recv[-1], "sum_small")
    g_small_all = _allgather([small_mine], [F32], "allgather_small")[0].reshape(N_DEV * rs8, LANES)
    pieces = dict(zip(small_names, _unpack_rows(g_small_all, small_sizes)))
    loss = pieces["loss"][0]
    dconv_w = lax.dynamic_index_in_dim(pieces["conv_w"].reshape(N_DEV, 3, FF_CW), me, axis=0, keepdims=False)

    out = {}
    for n, parts in zip(_BIG, recv[:-1]):
        res = _adam_shard(parts, args[n], args["m_" + n], args["v_" + n], "adam_" + n)
        for kind, v in zip(("grad_", "delta_", "new_m_", "new_v_"), res):
            out[kind + n] = v
    names2 = _SMALL + ("conv_w",)
    gs = [pieces[n].reshape(_as_2d(args[n]).shape) for n in _SMALL] + [dconv_w]
    ds, m2s, v2s = _adam_small(gs, [_as_2d(args[n]) for n in names2], [_as_2d(args["m_" + n]) for n in names2],
                               [_as_2d(args["v_" + n]) for n in names2], "adam_small")
    for n, res in zip(names2, zip(gs, ds, m2s, v2s)):
        for kind, v in zip(("grad_", "delta_", "new_m_", "new_v_"), res):
            out[kind + n] = v.reshape(args[n].shape)
    order = ("g_mix", "w_in", "a_re", "a_im", "log_dt", "b_re", "b_im", "c_re", "c_im", "d_skip", "w_glu", "b_glu",
             "w_proj_a", "g_sgu", "w_s", "b_s", "w_proj_b", "w_out", "g_ffn", "w_up", "conv_w", "conv_b", "w_down",
             "g_final")
    res = [loss, grad_x.reshape(x.shape)]
    for kind in ("grad_", "delta_", "new_m_", "new_v_"):
        res += [out[kind + n] for n in order]
    return tuple(res)
```

```python
import functools
import math

import jax
import jax.numpy as jnp
from jax import lax
from jax.experimental import pallas as pl
from jax.experimental.pallas import tpu as pltpu

F32 = jnp.float32
MXU = jnp.bfloat16
EPS = 1e-6

D_MODEL = 1024
SSM_W = 512
SSM_G, SSM_H, SSM_P = 32, 16, 64
SSM_BLK = 4
SGU_W = 512
SGU_G, SGU_D, CHUNK = 8, 64, 128
D_FF = 2816
N_DEV = 8
FF_CW = 2 * D_FF // N_DEV
FF_NCB = D_FF // FF_CW
LANES = 128

ADAM_LR, ADAM_B1, ADAM_B2, ADAM_EPS, ADAM_WD, ADAM_STEP = 0.001, 0.9, 0.999, 1e-08, 0.01, 10

VMEM_LIMIT = 48 * 1024 * 1024


def _cp(*sem):
    return pltpu.CompilerParams(dimension_semantics=sem, vmem_limit_bytes=VMEM_LIMIT)


def _full(shape):
    n = len(shape)
    return pl.BlockSpec(shape, lambda *_: (0,) * n)


def _sds(shape, dtype=F32):
    return jax.ShapeDtypeStruct(shape, dtype)


def _dot(a, b):
    return jnp.dot(a, b, preferred_element_type=F32)


def _dot_nt(a, b):
    return lax.dot_general(a, b, (((1,), (1,)), ((), ())), preferred_element_type=F32)


def _dot_tn(a, b):
    return lax.dot_general(a, b, (((0,), (0,)), ((), ())), preferred_element_type=F32)


_GELU_C = math.sqrt(2.0 / math.pi)


def _gelu(x):
    return 0.5 * x * (1.0 + jnp.tanh(_GELU_C * (x + 0.044715 * (x * x * x))))


def _gelu_and_grad(x):
    t = jnp.tanh(_GELU_C * (x + 0.044715 * (x * x * x)))
    g = 0.5 * x * (1.0 + t)
    dg = 0.5 * (1.0 + t) + 0.5 * x * (1.0 - t * t) * (_GELU_C * (1.0 + 3.0 * 0.044715 * (x * x)))
    return g, dg


def _sigmoid(x):
    return 1.0 / (1.0 + jnp.exp(-x))


def _rms(x):
    return lax.rsqrt(jnp.mean(x * x, axis=-1, keepdims=True) + EPS)


def _rms_bwd(dxn, xn, r):
    return r * (dxn - xn * jnp.mean(dxn * xn, axis=-1, keepdims=True))


def _rowsum(x):
    return jnp.sum(x, axis=0, keepdims=True)


def _s5_disc(are, aim, ldt, br, bi):
    dt = jnp.exp(ldt)
    mag = jnp.exp(dt * are)
    abr = mag * jnp.cos(dt * aim)
    abi = mag * jnp.sin(dt * aim)
    den = are * are + aim * aim
    nr = abr - 1.0
    ni = abi
    fr = (nr * are + ni * aim) / den
    fi = (ni * are - nr * aim) / den
    return abr, abi, fr * br - fi * bi, fr * bi + fi * br


def _s5_params_fwd(are, aim, ldt, br, bi):
    def body(are_ref, aim_ref, ldt_ref, br_ref, bi_ref, o0, o1, o2, o3):
        outs = _s5_disc(are_ref[...], aim_ref[...], ldt_ref[...], br_ref[...], bi_ref[...])
        for o, v in zip((o0, o1, o2, o3), outs):
            o[...] = v
    shp = are.shape
    return pl.pallas_call(body, name="s5_params_fwd", out_shape=[_sds(shp)] * 4)(are, aim, ldt, br, bi)


def _s5_params_bwd(are, aim, ldt, br, bi, dabr, dabi, dbr, dbi):
    def body(are_ref, aim_ref, ldt_ref, br_ref, bi_ref, c0, c1, c2, c3, o0, o1, o2, o3, o4):
        prim = (are_ref[...], aim_ref[...], ldt_ref[...], br_ref[...], bi_ref[...])
        _, vjp = jax.vjp(_s5_disc, *prim)
        outs = vjp((c0[...], c1[...], c2[...], c3[...]))
        for o, v in zip((o0, o1, o2, o3, o4), outs):
            o[...] = v
    shp = are.shape
    return pl.pallas_call(body, name="s5_params_bwd", out_shape=[_sds(shp)] * 5)(
        are, aim, ldt, br, bi, dabr, dabi, dbr, dbi)


def _blockdiag(m_t):
    m = m_t.reshape(SSM_BLK, 8, SSM_H, 1, SSM_P)
    eye = jnp.eye(8, dtype=bool).reshape(1, 8, 1, 8, 1)
    return jnp.where(eye, m, jnp.zeros((), m_t.dtype)).reshape(SSM_BLK, 8 * SSM_H, 8 * SSM_P)


def _unblockdiag(pc):
    m = pc.reshape(SSM_BLK, 8, SSM_H, 8, SSM_P)
    return jnp.einsum("jghgp->jghp", m).reshape(SSM_G * SSM_H, SSM_P)


def _in_fwd(x, g_mix, w_in, tm):
    S = x.shape[0]

    def body(x_ref, g_ref, w_ref, h_ref, us_ref, uv_ref, gl_ref):
        xv = x_ref[...]
        h = (xv * _rms(xv) * g_ref[...]).astype(MXU)
        h_ref[...] = h
        us_ref[...] = _dot(h, w_ref[:, 0:SSM_W])
        uv_ref[...] = _dot(h, w_ref[:, SSM_W:SSM_W + 2 * SGU_W])
        gl_ref[...] = _dot(h, w_ref[:, SSM_W + 2 * SGU_W:])

    row = lambda n: pl.BlockSpec((tm, n), lambda i: (i, 0))
    return pl.pallas_call(
        body, name="in_fwd", grid=(S // tm,),
        in_specs=[row(D_MODEL), _full((1, D_MODEL)), _full(w_in.shape)],
        out_specs=[row(D_MODEL), row(SSM_W), row(2 * SGU_W), row(2 * D_MODEL)],
        out_shape=[_sds((S, D_MODEL), MXU), _sds((S, SSM_W)), _sds((S, 2 * SGU_W)), _sds((S, 2 * D_MODEL))],
        compiler_params=_cp("parallel"),
    )(x, g_mix, w_in)


def _scan_tables(ar, ai, reverse):
    n = ar.shape[-1]
    def mul(p, q):
        return p[0] * q[0] - p[1] * q[1], p[0] * q[1] + p[1] * q[0]
    a1 = (ar, ai)
    a2 = mul(a1, a1)
    a3 = mul(a2, a1)
    a4 = mul(a2, a2)
    a5 = mul(a4, a1)
    a6 = mul(a4, a2)
    a7 = mul(a4, a3)
    a8 = mul(a4, a4)
    pw = (a1, a2, a3, a4, a5, a6, a7, a8)
    rows = lax.broadcasted_iota(jnp.int32, (8, n), 0)
    tabs = []
    for s, a in ((1, a1), (2, a2), (4, a4)):
        keep = (rows + s <= 7) if reverse else (rows >= s)
        for comp in a:
            tabs.append(jnp.where(keep, jnp.broadcast_to(comp, (8, n)), 0.0))
    for c in range(2):
        q = jnp.zeros((8, n), F32)
        for r in range(8):
            e = (8 - r) if reverse else (r + 1)
            q = jnp.where(rows == r, jnp.broadcast_to(pw[e - 1][c], (8, n)), q)
        tabs.append(q)
    return tabs


def _scan_group(xr, xi, tab_ref, cr, ci, reverse):
    for t, s in enumerate((1, 2, 4)):
        pr = tab_ref[2 * t]
        pi = tab_ref[2 * t + 1]
        sh = (8 - s) if reverse else s
        sr = pltpu.roll(xr, sh, 0)
        si = pltpu.roll(xi, sh, 0)
        xr, xi = xr + pr * sr - pi * si, xi + pr * si + pi * sr
    qr = tab_ref[6]
    qi = tab_ref[7]
    return xr + qr * cr - qi * ci, xi + qr * ci + qi * cr


def _s5_fwd(us, abar_re, abar_im, b_re, b_im, c_re, c_im, d_skip, tm):
    S = us.shape[0]
    nt = S // tm
    w = 8 * SSM_P

    def body(us_ref, ar_ref, ai_ref, br_ref, bi_ref, cr_ref, ci_ref, d_ref, str_ref, sti_ref, ys_ref, tab_ref, car_ref):
        i = pl.program_id(1)

        @pl.when(i == 0)
        def _():
            car_ref[...] = jnp.zeros_like(car_ref)
            for k, t in enumerate(_scan_tables(ar_ref[...], ai_ref[...], False)):
                tab_ref[k] = t

        u = us_ref[...]
        ub = u.astype(MXU)
        str_ref[...] = _dot(ub, br_ref[0])
        sti_ref[...] = _dot(ub, bi_ref[0])

        def grp(k, carry):
            r0 = pl.multiple_of(k * 8, 8)
            xr, xi = _scan_group(str_ref[pl.ds(r0, 8), :], sti_ref[pl.ds(r0, 8), :], tab_ref, carry[0], carry[1], False)
            str_ref[pl.ds(r0, 8), :] = xr
            sti_ref[pl.ds(r0, 8), :] = xi
            return xr[7:8, :], xi[7:8, :]

        cr, ci = lax.fori_loop(0, tm // 8, grp, (car_ref[0:1, :], car_ref[1:2, :]))
        car_ref[0:1, :] = cr
        car_ref[1:2, :] = ci
        y = _dot_nt(str_ref[...].astype(MXU), cr_ref[0]) - _dot_nt(sti_ref[...].astype(MXU), ci_ref[0])
        ys_ref[...] = y + d_ref[...] * u

    blk = lambda: pl.BlockSpec((1, 8 * SSM_H, w), lambda j, i: (j, 0, 0))
    return pl.pallas_call(
        body, name="s5_fwd", grid=(SSM_BLK, nt),
        in_specs=[pl.BlockSpec((tm, LANES), lambda j, i: (i, j)),
                  pl.BlockSpec((1, w), lambda j, i: (0, j)), pl.BlockSpec((1, w), lambda j, i: (0, j)),
                  blk(), blk(), blk(), blk(),
                  pl.BlockSpec((1, LANES), lambda j, i: (0, j))],
        out_specs=[pl.BlockSpec((tm, w), lambda j, i: (i, j)), pl.BlockSpec((tm, w), lambda j, i: (i, j)),
                   pl.BlockSpec((tm, LANES), lambda j, i: (i, j))],
        out_shape=[_sds((S, SSM_BLK * w)), _sds((S, SSM_BLK * w)), _sds((S, SSM_W))],
        scratch_shapes=[pltpu.VMEM((8, 8, w), F32), pltpu.VMEM((8, w), F32)],
        compiler_params=_cp("parallel", "arbitrary"),
    )(us, abar_re, abar_im, b_re, b_im, c_re, c_im, d_skip)


def _sgu_mix(vnb, ws_ref, grp):
    acc = jnp.zeros(vnb.shape, F32)
    for g in range(SGU_G):
        acc = jnp.where(grp == g, _dot(ws_ref[g], vnb), acc)
    return acc


def _mix_fwd(x, ys, uv, gl, w_glu, b_glu, w_pa, g_sgu, ws, bias_s, w_pb, w_out, g_ffn, tm):
    S = x.shape[0]

    def body(x_ref, ys_ref, uv_ref, gl_ref, wglu_ref, bglu_ref, wpa_ref, gs_ref, ws_ref, bias_ref, wpb_ref, wout_ref,
             gf_ref, yg_ref, yap_ref, sg_ref, ya_ref, yb_ref, m_ref, x1_ref, h2_ref):
        yg = _gelu(ys_ref[...])
        ygb = yg.astype(MXU)
        yg_ref[...] = ygb
        z = _dot(ygb, wglu_ref[...]) + bglu_ref[...]
        yapb = (yg * _sigmoid(z)).astype(MXU)
        yap_ref[...] = yapb
        ya = _dot(yapb, wpa_ref[...])
        ya_ref[...] = ya

        uvg = _gelu(uv_ref[...])
        u2 = uvg[:, :SGU_W]
        v2 = uvg[:, SGU_W:]
        vnb = (v2 * _rms(v2) * gs_ref[...]).astype(MXU)
        grp = lax.broadcasted_iota(jnp.int32, (CHUNK, SGU_W), 1) // SGU_D
        for c in range(tm // CHUNK):
            rs = slice(c * CHUNK, (c + 1) * CHUNK)
            mixed = _sgu_mix(vnb[rs], ws_ref, grp) + bias_ref[...]
            sg_ref[rs, :] = (u2[rs] * mixed).astype(MXU)
        yb = _dot(sg_ref[...], wpb_ref[...])
        yb_ref[...] = yb

        glv = gl_ref[...]
        m = _sigmoid(glv[:, :D_MODEL]) * ya + _sigmoid(glv[:, D_MODEL:]) * yb
        mb = m.astype(MXU)
        m_ref[...] = mb
        x1 = x_ref[...] + _dot(mb, wout_ref[...])
        x1_ref[...] = x1
        h2_ref[...] = (x1 * _rms(x1) * gf_ref[...]).astype(MXU)

    row = lambda n: pl.BlockSpec((tm, n), lambda i: (i, 0))
    return pl.pallas_call(
        body, name="mix_fwd", grid=(S // tm,),
        in_specs=[row(D_MODEL), row(SSM_W), row(2 * SGU_W), row(2 * D_MODEL),
                  _full(w_glu.shape), _full(b_glu.shape), _full(w_pa.shape), _full(g_sgu.shape), _full(ws.shape),
                  _full(bias_s.shape), _full(w_pb.shape), _full(w_out.shape), _full(g_ffn.shape)],
        out_specs=[row(SSM_W), row(SSM_W), row(SGU_W), row(D_MODEL), row(D_MODEL), row(D_MODEL), row(D_MODEL),
                   row(D_MODEL)],
        out_shape=[_sds((S, SSM_W), MXU), _sds((S, SSM_W), MXU), _sds((S, SGU_W), MXU), _sds((S, D_MODEL)),
                   _sds((S, D_MODEL)), _sds((S, D_MODEL), MXU), _sds((S, D_MODEL)), _sds((S, D_MODEL), MXU)],
        compiler_params=_cp("parallel"),
    )(x, ys, uv, gl, w_glu, b_glu, w_pa, g_sgu, ws, bias_s, w_pb, w_out, g_ffn)


def _conv_taps(u, prev8, rows):
    t1 = prev8[7:8, :]
    t0 = prev8[6:7, :]
    s1 = jnp.where(rows == 0, t1, pltpu.roll(u, 1, 0))
    s2 = jnp.where(rows == 0, t0, jnp.where(rows == 1, t1, pltpu.roll(u, 2, 0)))
    return s1, s2


def _ffn_fwd(h2, x1, tgt, w_up, conv_w, conv_b, w_down, g_final, tm):
    S = h2.shape[0]
    nt = S // tm
    ncb = FF_NCB

    def body(h2_ref, wa_ref, wb_ref, cwa_ref, cwb_ref, cba_ref, cbb_ref, wd_ref, x1_ref, gf_ref, tgt_ref,
             up_ref, ff_ref, dx2_ref, loss_ref, dgf_ref, acc_ref, tail_ref):
        i = pl.program_id(0)
        cb = pl.program_id(1)

        @pl.when(i == 0)
        def _():
            tail_ref[cb] = jnp.zeros((2, 8, FF_CW), F32)

        @pl.when(jnp.logical_and(i == 0, cb == 0))
        def _():
            loss_ref[...] = jnp.zeros_like(loss_ref)
            dgf_ref[...] = jnp.zeros_like(dgf_ref)

        h2v = h2_ref[...]
        ua = _dot(h2v, wa_ref[0])
        ub = _dot(h2v, wb_ref[0])
        up_ref[0, 0] = ua
        up_ref[1, 0] = ub
        rows = lax.broadcasted_iota(jnp.int32, (tm, FF_CW), 0)
        s1a, s2a = _conv_taps(ua, tail_ref[cb, 0], rows)
        s1b, s2b = _conv_taps(ub, tail_ref[cb, 1], rows)
        tail_ref[cb, 0] = ua[tm - 8:tm, :]
        tail_ref[cb, 1] = ub[tm - 8:tm, :]
        cwa = cwa_ref[0]
        cwb = cwb_ref[0]
        a = cwa[0:1] * s2a + cwa[1:2] * s1a + cwa[2:3] * ua + cba_ref[0]
        b = cwb[0:1] * s2b + cwb[1:2] * s1b + cwb[2:3] * ub + cbb_ref[0]
        ffb = (a * _sigmoid(a) * b).astype(MXU)
        ff_ref[0] = ffb
        contrib = _dot(ffb, wd_ref[...])

        @pl.when(cb == 0)
        def _():
            acc_ref[...] = contrib

        @pl.when(cb > 0)
        def _():
            acc_ref[...] += contrib

        @pl.when(cb == ncb - 1)
        def _():
            x2 = x1_ref[...] + acc_ref[...]
            r = _rms(x2)
            xn = x2 * r
            g = gf_ref[...]
            diff = xn * g - tgt_ref[...]
            loss_ref[...] += (0.5 / D_MODEL) * jnp.sum(diff * diff)
            dy = diff * (1.0 / D_MODEL)
            dgf_ref[...] += _rowsum(dy * xn)
            dx2_ref[...] = _rms_bwd(dy * g, xn, r)

    row = lambda n: pl.BlockSpec((tm, n), lambda i, c: (i, 0))
    gate = lambda r: pl.BlockSpec((1, r, FF_CW), lambda i, c: (c, 0, 0))
    lin = lambda r: pl.BlockSpec((1, r, FF_CW), lambda i, c: (ncb + c, 0, 0))
    return pl.pallas_call(
        body, name="ffn_fwd", grid=(nt, ncb),
        in_specs=[row(D_MODEL), gate(D_MODEL), lin(D_MODEL), gate(3), lin(3), gate(1), lin(1),
                  pl.BlockSpec((FF_CW, D_MODEL), lambda i, c: (c, 0)),
                  row(D_MODEL), _full((1, D_MODEL)), row(D_MODEL)],
        out_specs=[pl.BlockSpec((2, 1, tm, FF_CW), lambda i, c: (0, c, i, 0)),
                   pl.BlockSpec((1, tm, FF_CW), lambda i, c: (c, i, 0)),
                   row(D_MODEL), _full((1, LANES)), _full((1, D_MODEL))],
        out_shape=[_sds((2, ncb, S, FF_CW)), _sds((ncb, S, FF_CW), MXU), _sds((S, D_MODEL)),
                   _sds((1, LANES)), _sds((1, D_MODEL))],
        scratch_shapes=[pltpu.VMEM((tm, D_MODEL), F32), pltpu.VMEM((ncb, 2, 8, FF_CW), F32)],
        compiler_params=_cp("arbitrary", "arbitrary"),
    )(h2, w_up, w_up, conv_w, conv_w, conv_b, conv_b, w_down, x1, g_final, tgt)


def _ffn_bwd(dx2, up, x1, w_up, conv_w, conv_b, w_down, g_ffn, tm):
    S = dx2.shape[0]
    nt = S // tm
    ncb = FF_NCB
    hb = tm // 8

    def body(dx2_ref, up_ref, hp_ref, cwa_ref, cwb_ref, cba_ref, cbb_ref, wd_ref, wa_ref, wb_ref,
             x1_ref, g_ref, dup_ref, dx1_ref, dconv_ref, dg_ref, acc_ref, head_ref):
        i = pl.program_id(0)
        cb = pl.program_id(1)
        ri = nt - 1 - i

        @pl.when(i == 0)
        def _():
            head_ref[cb] = jnp.zeros((2, 8, FF_CW), F32)
            dconv_ref[cb] = jnp.zeros((8, FF_CW), F32)
            dconv_ref[ncb + cb] = jnp.zeros((8, FF_CW), F32)

        @pl.when(jnp.logical_and(i == 0, cb == 0))
        def _():
            dg_ref[...] = jnp.zeros_like(dg_ref)

        dx2v = dx2_ref[...]
        dff = _dot_nt(dx2v.astype(MXU), wd_ref[...])
        ua = up_ref[0, 0]
        ub = up_ref[1, 0]
        rows = lax.broadcasted_iota(jnp.int32, (tm, FF_CW), 0)
        first = ri == 0
        s1a, s2a = _conv_taps(ua, jnp.where(first, 0.0, hp_ref[0, 0]), rows)
        s1b, s2b = _conv_taps(ub, jnp.where(first, 0.0, hp_ref[1, 0]), rows)
        cwa = cwa_ref[0]
        cwb = cwb_ref[0]
        a = cwa[0:1] * s2a + cwa[1:2] * s1a + cwa[2:3] * ua + cba_ref[0]
        b = cwb[0:1] * s2b + cwb[1:2] * s1b + cwb[2:3] * ub + cbb_ref[0]
        sa = _sigmoid(a)
        da = dff * b * (sa * (1.0 + a * (1.0 - sa)))
        db = dff * (a * sa)

        def conv_bwd(dup, head8, cw):
            h0 = head8[0:1, :]
            h1 = head8[1:2, :]
            n1 = jnp.where(rows == tm - 1, h0, pltpu.roll(dup, tm - 1, 0))
            n2 = jnp.where(rows == tm - 2, h0, jnp.where(rows == tm - 1, h1, pltpu.roll(dup, tm - 2, 0)))
            return cw[2:3] * dup + cw[1:2] * n1 + cw[0:1] * n2

        dpa = conv_bwd(da, head_ref[cb, 0], cwa).astype(MXU)
        dpb = conv_bwd(db, head_ref[cb, 1], cwb).astype(MXU)
        head_ref[cb, 0] = da[0:8, :]
        head_ref[cb, 1] = db[0:8, :]
        dup_ref[0, 0] = dpa
        dup_ref[1, 0] = dpb
        for slot, dup, s2, s1, u in ((cb, da, s2a, s1a, ua), (ncb + cb, db, s2b, s1b, ub)):
            dconv_ref[slot, 0:1, :] += _rowsum(dup * s2)
            dconv_ref[slot, 1:2, :] += _rowsum(dup * s1)
            dconv_ref[slot, 2:3, :] += _rowsum(dup * u)
            dconv_ref[slot, 3:4, :] += _rowsum(dup)
        contrib = _dot_nt(dpa, wa_ref[0]) + _dot_nt(dpb, wb_ref[0])

        @pl.when(cb == 0)
        def _():
            acc_ref[...] = contrib

        @pl.when(cb > 0)
        def _():
            acc_ref[...] += contrib

        @pl.when(cb == ncb - 1)
        def _():
            x1v = x1_ref[...]
            r = _rms(x1v)
            xn = x1v * r
            dh2 = acc_ref[...]
            dg_ref[...] += _rowsum(dh2 * xn)
            dx1_ref[...] = dx2v + _rms_bwd(dh2 * g_ref[...], xn, r)

    row = lambda n: pl.BlockSpec((tm, n), lambda i, c: (nt - 1 - i, 0))
    colb = lambda: pl.BlockSpec((2, 1, tm, FF_CW), lambda i, c: (0, c, nt - 1 - i, 0))
    halo = lambda: pl.BlockSpec((2, 1, 8, FF_CW), lambda i, c: (0, c, jnp.maximum((nt - 1 - i) * hb - 1, 0), 0))
    gate = lambda r: pl.BlockSpec((1, r, FF_CW), lambda i, c: (c, 0, 0))
    lin = lambda r: pl.BlockSpec((1, r, FF_CW), lambda i, c: (ncb + c, 0, 0))
    return pl.pallas_call(
        body, name="ffn_bwd", grid=(nt, ncb),
        in_specs=[row(D_MODEL), colb(), halo(), gate(3), lin(3), gate(1), lin(1),
                  pl.BlockSpec((FF_CW, D_MODEL), lambda i, c: (c, 0)),
                  gate(D_MODEL), lin(D_MODEL), row(D_MODEL), _full((1, D_MODEL))],
        out_specs=[colb(), row(D_MODEL), _full((2 * ncb, 8, FF_CW)), _full((1, D_MODEL))],
        out_shape=[_sds((2, ncb, S, FF_CW), MXU), _sds((S, D_MODEL)), _sds((2 * ncb, 8, FF_CW)), _sds((1, D_MODEL))],
        scratch_shapes=[pltpu.VMEM((tm, D_MODEL), F32), pltpu.VMEM((ncb, 2, 8, FF_CW), F32)],
        compiler_params=_cp("arbitrary", "arbitrary"),
    )(dx2, up, up, conv_w, conv_w, conv_b, conv_b, w_down, w_up, w_up, x1, g_ffn)


def _mix_bwd(dx1, gl, ya, yb, ys, uv, w_out, w_pa, w_pb, w_glu, b_glu, g_sgu, ws, ws_t, bias_s, tm):
    S = dx1.shape[0]

    def body(dx1_ref, gl_ref, ya_ref, yb_ref, ys_ref, uv_ref, wout_ref, wpa_ref, wpb_ref, wglu_ref, bglu_ref, gs_ref,
             ws_ref, wst_ref, bias_ref,
             dgl_ref, dya_ref, dyb_ref, dz_ref, dys_ref, duv_ref, dbglu_ref, dgs_ref, dws_ref, dbs_ref,
             du2_ref, dvn_ref):
        i = pl.program_id(0)

        @pl.when(i == 0)
        def _():
            dbglu_ref[...] = jnp.zeros_like(dbglu_ref)
            dgs_ref[...] = jnp.zeros_like(dgs_ref)
            dws_ref[...] = jnp.zeros_like(dws_ref)
            dbs_ref[...] = jnp.zeros_like(dbs_ref)

        dm = _dot_nt(dx1_ref[...].astype(MXU), wout_ref[...])
        glv = gl_ref[...]
        ga = _sigmoid(glv[:, :D_MODEL])
        gb = _sigmoid(glv[:, D_MODEL:])
        dgl_ref[:, :D_MODEL] = (dm * ya_ref[...] * ga * (1.0 - ga)).astype(MXU)
        dgl_ref[:, D_MODEL:] = (dm * yb_ref[...] * gb * (1.0 - gb)).astype(MXU)
        dyab = (dm * ga).astype(MXU)
        dybb = (dm * gb).astype(MXU)
        dya_ref[...] = dyab
        dyb_ref[...] = dybb

        dyap = _dot_nt(dyab, wpa_ref[...])
        yg, dgelu = _gelu_and_grad(ys_ref[...])
        sz = _sigmoid(_dot(yg.astype(MXU), wglu_ref[...]) + bglu_ref[...])
        dz = dyap * yg * sz * (1.0 - sz)
        dzb = dz.astype(MXU)
        dz_ref[...] = dzb
        dbglu_ref[...] += _rowsum(dz)
        dys_ref[...] = (dyap * sz + _dot_nt(dzb, wglu_ref[...])) * dgelu

        dsg = _dot_nt(dybb, wpb_ref[...])
        uvg, duvg = _gelu_and_grad(uv_ref[...])
        u2 = uvg[:, :SGU_W]
        v2 = uvg[:, SGU_W:]
        rv = _rms(v2)
        vhat = v2 * rv
        gs = gs_ref[...]
        vnb = (vhat * gs).astype(MXU)
        grp = lax.broadcasted_iota(jnp.int32, (CHUNK, SGU_W), 1) // SGU_D
        tril = (lax.broadcasted_iota(jnp.int32, (CHUNK, CHUNK), 0)
                >= lax.broadcasted_iota(jnp.int32, (CHUNK, CHUNK), 1))
        for c in range(tm // CHUNK):
            rs = slice(c * CHUNK, (c + 1) * CHUNK)
            vc = vnb[rs]
            mixed = _sgu_mix(vc, ws_ref, grp) + bias_ref[...]
            dsg_c = dsg[rs]
            du2_ref[rs, :] = dsg_c * mixed
            dmx = dsg_c * u2[rs]
            dbs_ref[...] += dmx
            dmb = dmx.astype(MXU)
            dvn_ref[rs, :] = _sgu_mix(dmb, wst_ref, grp)
            for g in range(SGU_G):
                part = _dot_nt(jnp.where(grp == g, dmb, jnp.zeros((), MXU)), vc)
                dws_ref[g] += jnp.where(tril, part, 0.0)
        dvn = dvn_ref[...]
        dgs_ref[...] += _rowsum(dvn * vhat)
        dv2 = _rms_bwd(dvn * gs, vhat, rv)
        duv_ref[:, :SGU_W] = (du2_ref[...] * duvg[:, :SGU_W]).astype(MXU)
        duv_ref[:, SGU_W:] = (dv2 * duvg[:, SGU_W:]).astype(MXU)

    row = lambda n: pl.BlockSpec((tm, n), lambda i: (i, 0))
    return pl.pallas_call(
        body, name="mix_bwd", grid=(S // tm,),
        in_specs=[row(D_MODEL), row(2 * D_MODEL), row(D_MODEL), row(D_MODEL), row(SSM_W), row(2 * SGU_W),
                  _full(w_out.shape), _full(w_pa.shape), _full(w_pb.shape), _full(w_glu.shape), _full(b_glu.shape),
                  _full(g_sgu.shape), _full(ws.shape), _full(ws_t.shape), _full(bias_s.shape)],
        out_specs=[row(2 * D_MODEL), row(D_MODEL), row(D_MODEL), row(SSM_W), row(SSM_W), row(2 * SGU_W),
                   _full((1, SSM_W)), _full((1, SGU_W)), _full((SGU_G, CHUNK, CHUNK)), _full((CHUNK, SGU_W))],
        out_shape=[_sds((S, 2 * D_MODEL), MXU), _sds((S, D_MODEL), MXU), _sds((S, D_MODEL), MXU), _sds((S, SSM_W), MXU),
                   _sds((S, SSM_W)), _sds((S, 2 * SGU_W), MXU),
                   _sds((1, SSM_W)), _sds((1, SGU_W)), _sds((SGU_G, CHUNK, CHUNK)), _sds((CHUNK, SGU_W))],
        scratch_shapes=[pltpu.VMEM((tm, SGU_W), F32), pltpu.VMEM((tm, SGU_W), F32)],
        compiler_params=_cp("arbitrary"),
    )(dx1, gl, ya, yb, ys, uv, w_out, w_pa, w_pb, w_glu, b_glu, g_sgu, ws, ws_t, bias_s)


def _s5_bwd(dys, us, st_re, st_im, abar_re, abar_im, b_re, b_im, c_re, c_im, d_skip, tm):
    S = us.shape[0]
    nt = S // tm
    w = 8 * SSM_P
    hb = tm // 8

    def body(dys_ref, us_ref, str_ref, sti_ref, hr_ref, hi_ref, ar_ref, ai_ref, br_ref, bi_ref, cr_ref, ci_ref, d_ref,
             dus_ref, dab_ref, dd_ref, dbr_ref, dbi_ref, dcr_ref, dci_ref, tab_ref, car_ref, gr_ref, gi_ref):
        i = pl.program_id(1)
        ri = nt - 1 - i

        @pl.when(i == 0)
        def _():
            car_ref[...] = jnp.zeros_like(car_ref)
            for k, t in enumerate(_scan_tables(ar_ref[...], -ai_ref[...], True)):
                tab_ref[k] = t
            for r in (dab_ref, dd_ref, dbr_ref, dbi_ref, dcr_ref, dci_ref):
                r[...] = jnp.zeros_like(r)

        dys_v = dys_ref[...]
        dyb = dys_v.astype(MXU)
        gr_ref[...] = _dot(dyb, cr_ref[0])
        gi_ref[...] = -_dot(dyb, ci_ref[0])

        def grp(kk, carry):
            r0 = pl.multiple_of((hb - 1 - kk) * 8, 8)
            xr, xi = _scan_group(gr_ref[pl.ds(r0, 8), :], gi_ref[pl.ds(r0, 8), :], tab_ref, carry[0], carry[1], True)
            gr_ref[pl.ds(r0, 8), :] = xr
            gi_ref[pl.ds(r0, 8), :] = xi
            return xr[0:1, :], xi[0:1, :]

        cr, ci = lax.fori_loop(0, hb, grp, (car_ref[0:1, :], car_ref[1:2, :]))
        car_ref[0:1, :] = cr
        car_ref[1:2, :] = ci

        gsr = gr_ref[...]
        gsi = gi_ref[...]
        sr = str_ref[...]
        si = sti_ref[...]
        rows = lax.broadcasted_iota(jnp.int32, (tm, w), 0)
        first = ri == 0
        spr = jnp.where(rows == 0, jnp.where(first, 0.0, hr_ref[7:8, :]), pltpu.roll(sr, 1, 0))
        spi = jnp.where(rows == 0, jnp.where(first, 0.0, hi_ref[7:8, :]), pltpu.roll(si, 1, 0))
        dab_ref[0, 0:1, :] += _rowsum(gsr * spr + gsi * spi)
        dab_ref[0, 1:2, :] += _rowsum(gsi * spr - gsr * spi)

        gbr = gsr.astype(MXU)
        gbi = gsi.astype(MXU)
        u = us_ref[...]
        ub = u.astype(MXU)
        dus_ref[...] = (_dot_nt(gbr, br_ref[0]) + _dot_nt(gbi, bi_ref[0]) + d_ref[...] * dys_v).astype(MXU)
        dd_ref[0, 0:1, :] += _rowsum(dys_v * u)
        dbr_ref[0] += _dot_tn(ub, gbr)
        dbi_ref[0] += _dot_tn(ub, gbi)
        dcr_ref[0] += _dot_tn(dyb, sr.astype(MXU))
        dci_ref[0] -= _dot_tn(dyb, si.astype(MXU))

    blk = lambda: pl.BlockSpec((1, 8 * SSM_H, w), lambda j, i: (j, 0, 0))
    rowl = lambda: pl.BlockSpec((tm, LANES), lambda j, i: (nt - 1 - i, j))
    roww = lambda: pl.BlockSpec((tm, w), lambda j, i: (nt - 1 - i, j))
    halo = lambda: pl.BlockSpec((8, w), lambda j, i: (jnp.maximum((nt - 1 - i) * hb - 1, 0), j))
    return pl.pallas_call(
        body, name="s5_bwd", grid=(SSM_BLK, nt),
        in_specs=[rowl(), rowl(), roww(), roww(), halo(), halo(),
                  pl.BlockSpec((1, w), lambda j, i: (0, j)), pl.BlockSpec((1, w), lambda j, i: (0, j)),
                  blk(), blk(), blk(), blk(),
                  pl.BlockSpec((1, LANES), lambda j, i: (0, j))],
        out_specs=[rowl(),
                   pl.BlockSpec((1, 8, w), lambda j, i: (j, 0, 0)), pl.BlockSpec((1, 8, LANES), lambda j, i: (j, 0, 0)),
                   blk(), blk(), blk(), blk()],
        out_shape=[_sds((S, SSM_W), MXU), _sds((SSM_BLK, 8, w)), _sds((SSM_BLK, 8, LANES)),
                   _sds((SSM_BLK, 8 * SSM_H, w)), _sds((SSM_BLK, 8 * SSM_H, w)),
                   _sds((SSM_BLK, 8 * SSM_H, w)), _sds((SSM_BLK, 8 * SSM_H, w))],
        scratch_shapes=[pltpu.VMEM((8, 8, w), F32), pltpu.VMEM((8, w), F32),
                        pltpu.VMEM((tm, w), F32), pltpu.VMEM((tm, w), F32)],
        compiler_params=_cp("parallel", "arbitrary"),
    )(dys, us, st_re, st_im, st_re, st_im, abar_re, abar_im, b_re, b_im, c_re, c_im, d_skip)


def _in_bwd(dus, duv, dgl, dx1, x, g_mix, w_in, tm):
    S = x.shape[0]

    def body(dus_ref, duv_ref, dgl_ref, dx1_ref, x_ref, g_ref, w_ref, gx_ref, dg_ref):
        @pl.when(pl.program_id(0) == 0)
        def _():
            dg_ref[...] = jnp.zeros_like(dg_ref)

        dh = (_dot_nt(dus_ref[...], w_ref[:, 0:SSM_W])
              + _dot_nt(duv_ref[...], w_ref[:, SSM_W:SSM_W + 2 * SGU_W])
              + _dot_nt(dgl_ref[...], w_ref[:, SSM_W + 2 * SGU_W:]))
        xv = x_ref[...]
        r = _rms(xv)
        xn = xv * r
        dg_ref[...] += _rowsum(dh * xn)
        gx_ref[...] = dx1_ref[...] + _rms_bwd(dh * g_ref[...], xn, r)

    row = lambda n: pl.BlockSpec((tm, n), lambda i: (i, 0))
    return pl.pallas_call(
        body, name="in_bwd", grid=(S // tm,),
        in_specs=[row(SSM_W), row(2 * SGU_W), row(2 * D_MODEL), row(D_MODEL), row(D_MODEL), _full((1, D_MODEL)),
                  _full(w_in.shape)],
        out_specs=[row(D_MODEL), _full((1, D_MODEL))],
        out_shape=[_sds((S, D_MODEL)), _sds((1, D_MODEL))],
        compiler_params=_cp("arbitrary"),
    )(dus, duv, dgl, dx1, x, g_mix, w_in)


def _pick(n, cands):
    for c in cands:
        if n % c == 0:
            return c
    return n


def _wgrad_split(a, bs, nsplit, tk, name):
    S, K = a.shape
    widths = [b.shape[1] for b in bs]
    N = sum(widths)
    c = N // nsplit
    ts = _pick(S, (512, 256, 128))
    ns = S // ts

    def body(*refs):
        a_ref = refs[0]
        b_refs = refs[1:1 + len(bs)]
        o_ref = refs[1 + len(bs)]
        acc_ref = refs[2 + len(bs)]
        s = pl.program_id(1)
        av = a_ref[...].astype(MXU)
        off = 0
        for b_ref, wdt in zip(b_refs, widths):
            part = _dot_tn(av, b_ref[...].astype(MXU))

            @pl.when(s == 0)
            def _():
                acc_ref[:, off:off + wdt] = part

            @pl.when(s > 0)
            def _():
                acc_ref[:, off:off + wdt] += part

            off += wdt

        @pl.when(s == ns - 1)
        def _():
            for d in range(nsplit):
                o_ref[d] = acc_ref[:, c * d:c * (d + 1)].astype(MXU)

    return pl.pallas_call(
        body, name=name, grid=(K // tk, ns),
        in_specs=[pl.BlockSpec((ts, tk), lambda k, s: (s, k))]
                 + [pl.BlockSpec((ts, wdt), lambda k, s: (s, 0)) for wdt in widths],
        out_specs=pl.BlockSpec((nsplit, tk, c), lambda k, s: (0, k, 0)),
        out_shape=_sds((nsplit, K, c), MXU),
        scratch_shapes=[pltpu.VMEM((tk, N), F32)],
        compiler_params=_cp("parallel", "arbitrary"),
    )(a, *bs)


def _wgrad_blk(a3, b3, nblk, a_of, b_of, name):
    S, K = a3.shape[1:]
    N = b3.shape[2]
    ts = _pick(S, (512, 256, 128))
    ns = S // ts

    def body(a_ref, b_ref, o_ref, acc_ref):
        s = pl.program_id(1)
        part = _dot_tn(a_ref[0].astype(MXU), b_ref[0].astype(MXU))

        @pl.when(s == 0)
        def _():
            acc_ref[...] = part

        @pl.when(s > 0)
        def _():
            acc_ref[...] += part

        @pl.when(s == ns - 1)
        def _():
            o_ref[0] = acc_ref[...].astype(MXU)

    return pl.pallas_call(
        body, name=name, grid=(nblk, ns),
        in_specs=[pl.BlockSpec((1, ts, K), lambda b, s: (a_of(b), s, 0)),
                  pl.BlockSpec((1, ts, N), lambda b, s: (b_of(b), s, 0))],
        out_specs=pl.BlockSpec((1, K, N), lambda b, s: (b, 0, 0)),
        out_shape=_sds((nblk, K, N), MXU),
        scratch_shapes=[pltpu.VMEM((K, N), F32)],
        compiler_params=_cp("parallel", "arbitrary"),
    )(a3, b3)


def _assemble_cols(blocks_list, name):
    def body(*refs):
        n = len(blocks_list)
        for b_ref, o_ref in zip(refs[:n], refs[n:]):
            c = b_ref.shape[2]
            for d in range(N_DEV):
                o_ref[:, c * d:c * (d + 1)] = b_ref[d]

    return pl.pallas_call(
        body, name=name,
        out_shape=[_sds((b.shape[1], N_DEV * b.shape[2]), b.dtype) for b in blocks_list],
        compiler_params=pltpu.CompilerParams(vmem_limit_bytes=VMEM_LIMIT),
    )(*blocks_list)


def _tile(S, want):
    return want if S % want == 0 else S


def _local_step(x, tgt, p, ffn_weights, ffn_grads_out):
    S = x.shape[0]
    tm = _tile(S, 256)
    tl = _tile(S, 512)

    rep = lambda a: jnp.repeat(a, SSM_H, axis=0)
    are = rep(p["a_re"])
    aim = rep(p["a_im"])
    ldt = jnp.broadcast_to(rep(p["log_dt"].reshape(SSM_G, 1)), are.shape)
    br_t = p["b_re"].transpose(0, 2, 1).reshape(are.shape)
    bi_t = p["b_im"].transpose(0, 2, 1).reshape(are.shape)
    abr, abi, bbr, bbi = _s5_params_fwd(are, aim, ldt, br_t, bi_t)
    head = lambda a: a.reshape(SSM_G, SSM_H, SSM_P)[:, 0, :].reshape(1, SSM_G * SSM_P)
    abar_re, abar_im = head(abr), head(abi)
    bd_br = _blockdiag(bbr).astype(MXU)
    bd_bi = _blockdiag(bbi).astype(MXU)
    bd_cr = _blockdiag(p["c_re"].reshape(are.shape)).astype(MXU)
    bd_ci = _blockdiag(p["c_im"].reshape(are.shape)).astype(MXU)
    d_skip = p["d_skip"].reshape(1, SSM_W)

    tril = jnp.tril(jnp.ones((CHUNK, CHUNK), dtype=bool))
    ws = jnp.where(tril[None], p["w_s"], 0.0)
    ws_b = ws.astype(MXU)
    ws_t = ws.transpose(0, 2, 1).astype(MXU)
    bias_s = jnp.repeat(p["b_s"].T, SGU_D, axis=1)

    g_mix = p["g_mix"].reshape(1, D_MODEL)
    g_ffn = p["g_ffn"].reshape(1, D_MODEL)
    g_final = p["g_final"].reshape(1, D_MODEL)
    g_sgu = p["g_sgu"].reshape(1, SGU_W)
    b_glu = p["b_glu"].reshape(1, SSM_W)
    conv_b = p["conv_b"].reshape(N_DEV, 1, FF_CW)

    h1, us, uv, gl = _in_fwd(x, g_mix, p["w_in"], tm)
    st_re, st_im, ys = _s5_fwd(us, abar_re, abar_im, bd_br, bd_bi, bd_cr, bd_ci, d_skip, tl)
    yg, yap, sg, ya, yb, m, x1, h2 = _mix_fwd(x, ys, uv, gl, p["w_glu"], b_glu, p["w_proj_a"], g_sgu, ws_b, bias_s,
                                              p["w_proj_b"], p["w_out"], g_ffn, tm)
    w_up, conv_w, w_down = ffn_weights(h2)
    up, ff, dx2, loss, dg_final = _ffn_fwd(h2, x1, tgt, w_up, conv_w, conv_b, w_down, g_final, tl)

    dup, dx1, dconv, dg_ffn = _ffn_bwd(dx2, up, x1, w_up, conv_w, conv_b, w_down, g_ffn, tl)
    g_up = _wgrad_blk(h2[None], dup.reshape(N_DEV, S, FF_CW), N_DEV, lambda b: 0, lambda b: b, "wgrad_up")
    g_down = _wgrad_blk(ff, dx2[None], FF_NCB, lambda b: b, lambda b: 0, "wgrad_down").reshape(
        N_DEV, D_FF // N_DEV, D_MODEL)
    token = ffn_grads_out(g_up, g_down)
    dgl, dya, dyb, dz, dys, duv, db_glu, dg_sgu, dws, dbs = _mix_bwd(
        dx1, gl, ya, yb, ys, uv, p["w_out"], p["w_proj_a"], p["w_proj_b"], p["w_glu"], b_glu + token[0:1, 0:1], g_sgu,
        ws_b, ws_t, bias_s, tm)
    dus, dab, dd, dbbr, dbbi, dcr, dci = _s5_bwd(dys, us, st_re, st_im, abar_re, abar_im, bd_br, bd_bi, bd_cr, bd_ci,
                                                 d_skip, tl)
    grad_x, dg_mix = _in_bwd(dus, duv, dgl, dx1, x, g_mix, p["w_in"], tm)

    spread = lambda v: jnp.repeat(v.reshape(SSM_G, SSM_P), SSM_H, axis=0) * (1.0 / SSM_H)
    dabr = spread(dab[:, 0, :])
    dabi = spread(dab[:, 1, :])
    dare, daim, dldt, dbr_t, dbi_t = _s5_params_bwd(are, aim, ldt, br_t, bi_t, dabr, dabi,
                                                    _unblockdiag(dbbr), _unblockdiag(dbbi))
    fold = lambda a: a.reshape(SSM_G, SSM_H, SSM_P).sum(axis=1)
    unt = lambda a: a.reshape(SSM_G, SSM_H, SSM_P).transpose(0, 2, 1)

    rows8 = lambda g: g.reshape(N_DEV, g.shape[1] // N_DEV, g.shape[2])
    grads = {
        "g_mix": dg_mix,
        "w_in": _wgrad_split(h1, [dus, duv, dgl], N_DEV, 256, "wgrad_in"),
        "a_re": fold(dare), "a_im": fold(daim), "log_dt": fold(dldt).sum(axis=1),
        "b_re": unt(dbr_t), "b_im": unt(dbi_t),
        "c_re": _unblockdiag(dcr).reshape(SSM_G, SSM_H, SSM_P),
        "c_im": _unblockdiag(dci).reshape(SSM_G, SSM_H, SSM_P),
        "d_skip": dd[:, 0, :].reshape(SSM_W),
        "w_glu": rows8(_wgrad_split(yg, [dz], 1, SSM_W, "wgrad_glu")),
        "b_glu": db_glu,
        "w_proj_a": _wgrad_split(yap, [dya], N_DEV, SSM_W, "wgrad_pa"),
        "g_sgu": dg_sgu,
        "w_s": dws,
        "b_s": dbs.reshape(CHUNK, SGU_G, SGU_D).sum(axis=-1).T,
        "w_proj_b": _wgrad_split(sg, [dyb], N_DEV, SGU_W, "wgrad_pb"),
        "w_out": rows8(_wgrad_split(m, [dx1], 1, 512, "wgrad_out")),
        "g_ffn": dg_ffn,
        "conv_w": dconv[:, 0:3, :],
        "conv_b": dconv[:, 3, :].reshape(2 * D_FF),
        "g_final": dg_final,
    }
    return loss, grad_x, grads


_ANY = pl.BlockSpec(memory_space=pl.ANY)
_MESH = pl.DeviceIdType.MESH


def _allgather(shards, dtypes, name, cast_only=()):
    n = len(shards)
    e = len(cast_only)

    def body(*refs):
        in_refs, extra_in = refs[:n], refs[n:n + e]
        out_refs, extra_out = refs[n + e:2 * n + e], refs[2 * n + e:2 * n + 2 * e]
        stage = refs[2 * n + 2 * e:3 * n + 2 * e]
        send_sems, recv_sems, local_sems = refs[3 * n + 2 * e:]
        for a in range(n):
            stage[a][...] = in_refs[a][...].astype(dtypes[a])
        for i in range(e):
            extra_out[i][...] = extra_in[i][...].astype(MXU)
        x, y, c = lax.axis_index("x"), lax.axis_index("y"), lax.axis_index("c")
        me, sibling = (x, y, c), (x, y, 1 - c)
        chips = [(1 - x, y), (x, 1 - y), (1 - x, 1 - y)]

        def slot(a, px, py, pc):
            return out_refs[a].at[4 * px + 2 * py + pc]

        def copy(a, k, block, to, src=None):
            return pltpu.make_async_remote_copy(
                src_ref=slot(a, *block) if src is None else src, dst_ref=slot(a, *block),
                send_sem=send_sems.at[a, k], recv_sem=recv_sems.at[a, k], device_id=to, device_id_type=_MESH)

        mine = [pltpu.make_async_copy(stage[a], slot(a, *me), local_sems.at[a]) for a in range(n)]
        for cp in mine:
            cp.start()
        first = []
        for j, chip in enumerate(chips):
            first += [copy(a, 1 + j, me, (*chip, c), src=stage[a]) for a in range(n)]
        first += [copy(a, 0, me, sibling, src=stage[a]) for a in range(n)]
        for cp in first:
            cp.start()
        passed = []
        for j, chip in enumerate(chips):
            for a in range(n):
                copy(a, 1 + j, (*chip, c), me).wait_recv()
                fwd = copy(a, 4 + j, (*chip, c), sibling)
                fwd.start()
                passed.append(fwd)
        for a in range(n):
            copy(a, 0, sibling, me).wait_recv()
        for j, chip in enumerate(chips):
            for a in range(n):
                copy(a, 4 + j, (*chip, 1 - c), me).wait_recv()
        for cp in first + passed:
            cp.wait_send()
        for cp in mine:
            cp.wait()

    vmem = pl.BlockSpec(memory_space=pltpu.VMEM)
    res = pl.pallas_call(
        body, name=name, in_specs=[vmem] * (n + e), out_specs=[_ANY] * n + [vmem] * e,
        out_shape=[_sds((N_DEV,) + s.shape, dt) for s, dt in zip(shards, dtypes)]
                  + [_sds(s.shape, MXU) for s in cast_only],
        scratch_shapes=[pltpu.VMEM(s.shape, dt) for s, dt in zip(shards, dtypes)]
                       + [pltpu.SemaphoreType.DMA((n, 7)), pltpu.SemaphoreType.DMA((n, 7)), pltpu.SemaphoreType.DMA((n,))],
        compiler_params=pltpu.CompilerParams(vmem_limit_bytes=VMEM_LIMIT),
    )(*shards, *cast_only)
    return res[:n], res[n:]


def _all_to_all(sends, name):
    n = len(sends)

    def body(*refs):
        send_refs, recv_refs = refs[:n], refs[n:2 * n]
        send_sems, recv_sems, local_sems = refs[2 * n:]
        x, y, c = lax.axis_index("x"), lax.axis_index("y"), lax.axis_index("c")
        me = 4 * x + 2 * y + c
        mine = [pltpu.make_async_copy(send_refs[a].at[me], recv_refs[a].at[me], local_sems.at[a]) for a in range(n)]
        for cp in mine:
            cp.start()
        copies = []
        for k in (2, 4, 6, 3, 5, 7, 1):
            px = 1 - x if k & 4 else x
            py = 1 - y if k & 2 else y
            pc = 1 - c if k & 1 else c
            peer = 4 * px + 2 * py + pc
            for a in range(n):
                sems = dict(send_sem=send_sems.at[a, k - 1], recv_sem=recv_sems.at[a, k - 1],
                            device_id=(px, py, pc), device_id_type=_MESH)
                cp = pltpu.make_async_remote_copy(src_ref=send_refs[a].at[peer], dst_ref=recv_refs[a].at[me], **sems)
                cp.start()
                landing = pltpu.make_async_remote_copy(src_ref=send_refs[a].at[peer], dst_ref=recv_refs[a].at[peer],
                                                       **sems)
                copies.append((cp, landing))
        for _, landing in copies:
            landing.wait_recv()
        for cp, _ in copies:
            cp.wait_send()
        for cp in mine:
            cp.wait()

    return pl.pallas_call(
        body, name=name, in_specs=[_ANY] * n, out_specs=[_ANY] * n,
        out_shape=[_sds(s.shape, s.dtype) for s in sends],
        scratch_shapes=[pltpu.SemaphoreType.DMA((n, 7)), pltpu.SemaphoreType.DMA((n, 7)), pltpu.SemaphoreType.DMA((n,))],
    )(*sends)


_HBM = pl.BlockSpec(memory_space=pltpu.HBM)
_SEM = pl.BlockSpec(memory_space=pltpu.SEMAPHORE)
_EFFECT = pltpu.SideEffectType.DATAFLOW_SIDE_EFFECTING
_PEER_ORDER = (2, 4, 6, 3, 5, 7, 1)


def _peer(k):
    x, y, c = lax.axis_index("x"), lax.axis_index("y"), lax.axis_index("c")
    px = 1 - x if k & 4 else x
    py = 1 - y if k & 2 else y
    pc = 1 - c if k & 1 else c
    return (px, py, pc), 4 * px + 2 * py + pc


def _push_start(srcs, lands, slotted, name):
    n = len(srcs)

    def body(*refs):
        src_refs, land_refs = refs[:n], refs[n:2 * n]
        send_sems, recv_sems, token = refs[2 * n], refs[2 * n + 1], refs[-1]
        me = 4 * lax.axis_index("x") + 2 * lax.axis_index("y") + lax.axis_index("c")
        for k in _PEER_ORDER:
            dev, peer = _peer(k)
            for a in range(n):
                pltpu.make_async_remote_copy(
                    src_ref=src_refs[a].at[peer] if slotted else src_refs[a], dst_ref=land_refs[a].at[me],
                    send_sem=send_sems.at[7 * a + k - 1], recv_sem=recv_sems.at[7 * a + k - 1],
                    device_id=dev, device_id_type=_MESH).start()
        token[...] = jnp.zeros_like(token)

    bufs = list(srcs) + list(lands)
    res = pl.pallas_call(
        body, name=name, in_specs=[_HBM] * (2 * n),
        out_specs=(_SEM, _SEM, *[_HBM] * (2 * n), pl.BlockSpec(memory_space=pltpu.VMEM)),
        out_shape=(pltpu.SemaphoreType.DMA((7 * n,)), pltpu.SemaphoreType.DMA((7 * n,)),
                   *[pltpu.HBM(b.shape, b.dtype) for b in bufs], _sds((8, LANES))),
        input_output_aliases={i: 2 + i for i in range(2 * n)},
        compiler_params=pltpu.CompilerParams(has_side_effects=_EFFECT),
    )(*[pltpu.with_memory_space_constraint(b, pltpu.HBM) for b in bufs])
    return res[0], res[1], res[2:2 + n], res[2 + n:2 + 2 * n], res[-1]


def _push_wait(send_sems, recv_sems, srcs, lands, slotted, after, name):
    n = len(srcs)

    def body(*refs):
        src_refs, land_refs = refs[:n], refs[n:2 * n]
        send_sems, recv_sems = refs[2 * n], refs[2 * n + 1]
        for k in _PEER_ORDER:
            dev, peer = _peer(k)
            for a in range(n):
                cp = pltpu.make_async_remote_copy(
                    src_ref=src_refs[a].at[peer] if slotted else src_refs[a], dst_ref=land_refs[a].at[peer],
                    send_sem=send_sems.at[7 * a + k - 1], recv_sem=recv_sems.at[7 * a + k - 1],
                    device_id=dev, device_id_type=_MESH)
                cp.wait_send()
                cp.wait_recv()

    bufs = list(srcs) + list(lands)
    res = pl.pallas_call(
        body, name=name, in_specs=[_HBM] * (2 * n) + [_SEM, _SEM, _ANY], out_specs=[_HBM] * (2 * n),
        out_shape=[pltpu.HBM(b.shape, b.dtype) for b in bufs],
        input_output_aliases={i: i for i in range(2 * n)},
        compiler_params=pltpu.CompilerParams(has_side_effects=_EFFECT),
    )(*bufs, send_sems, recv_sems, after)
    return res[n:]


def _adamw(w, g, m, v):
    m2 = ADAM_B1 * m + (1.0 - ADAM_B1) * g
    v2 = ADAM_B2 * v + (1.0 - ADAM_B2) * (g * g)
    m_hat = m2 / (1.0 - ADAM_B1 ** ADAM_STEP)
    v_hat = v2 / (1.0 - ADAM_B2 ** ADAM_STEP)
    delta = -ADAM_LR * (m_hat / (jnp.sqrt(v_hat) + ADAM_EPS) + ADAM_WD * w)
    return delta, m2, v2


def _adam_shard(parts, w, m, v, name):
    _, r, c = w.shape
    tr = 256 if r % 256 == 0 else r

    def body(p_ref, w_ref, m_ref, v_ref, g_ref, d_ref, m2_ref, v2_ref):
        g = p_ref[0].astype(F32)
        for s in range(1, N_DEV):
            g = g + p_ref[s].astype(F32)
        g_ref[0] = g
        d_ref[0], m2_ref[0], v2_ref[0] = _adamw(w_ref[0], g, m_ref[0], v_ref[0])

    row = lambda: pl.BlockSpec((1, tr, c), lambda i: (0, i, 0))
    return pl.pallas_call(
        body, name=name, grid=(r // tr,),
        in_specs=[pl.BlockSpec((N_DEV, tr, c), lambda i: (0, i, 0)), row(), row(), row()],
        out_specs=[row(), row(), row(), row()], out_shape=[_sds((1, r, c))] * 4,
        compiler_params=_cp("parallel"),
    )(parts, w, m, v)


def _adam_small(gs, ws, ms, vs, name):
    n = len(gs)

    def body(*refs):
        ins, outs = refs[:4 * n], refs[4 * n:]
        for i in range(n):
            g = ins[i][...]
            d, m2, v2 = _adamw(ins[n + i][...], g, ins[2 * n + i][...], ins[3 * n + i][...])
            outs[i][...] = d
            outs[n + i][...] = m2
            outs[2 * n + i][...] = v2

    res = pl.pallas_call(
        body, name=name, out_shape=[_sds(w.shape) for w in ws] * 3,
        compiler_params=pltpu.CompilerParams(vmem_limit_bytes=VMEM_LIMIT),
    )(*gs, *ws, *ms, *vs)
    return res[:n], res[n:2 * n], res[2 * n:]


def _sum_slots(parts, name):
    R = parts.shape[1]

    def body(p_ref, o_ref):
        g = p_ref[0]
        for s in range(1, N_DEV):
            g = g + p_ref[s]
        o_ref[...] = g

    return pl.pallas_call(body, name=name, out_shape=_sds((R, LANES)))(parts)


def _pad_to(a, n, axis):
    extra = n - a.shape[axis]
    if extra == 0:
        return a
    widths = [(0, 0)] * a.ndim
    widths[axis] = (0, extra)
    return jnp.pad(a, widths)


def _ceil_to(n, k):
    return -(-n // k) * k


def _pack_rows(flats, rows_multiple):
    parts = [_pad_to(f, _ceil_to(f.shape[-1], LANES), f.ndim - 1) for f in flats]
    cat = jnp.concatenate(parts, axis=-1)
    total = _ceil_to(cat.shape[-1], LANES * rows_multiple)
    cat = _pad_to(cat, total, cat.ndim - 1)
    return cat.reshape(cat.shape[:-1] + (total // LANES, LANES))


def _unpack_rows(buf, sizes):
    flat = buf.reshape(buf.shape[:-2] + (-1,))
    out, off = [], 0
    for n in sizes:
        out.append(flat[..., off:off + n])
        off += _ceil_to(n, LANES)
    return out


_MIX_BIG = ("w_in", "w_glu", "w_proj_a", "w_proj_b", "w_out")
_BIG = _MIX_BIG + ("w_up", "w_down")
_SMALL = ("g_mix", "a_re", "a_im", "log_dt", "b_re", "b_im", "c_re", "c_im", "d_skip", "b_glu", "g_sgu", "w_s", "b_s",
          "g_ffn", "conv_b", "g_final")
_SMALL_ROWS_MULTIPLE = 8 * N_DEV


def _as_2d(a):
    return a.reshape(-1, a.shape[-1]) if a.ndim > 1 else a.reshape(1, -1)


def kernel(x, g_mix, w_in, a_re, a_im, log_dt, b_re, b_im, c_re, c_im, d_skip, w_glu, b_glu, w_proj_a, g_sgu, w_s, b_s, w_proj_b, w_out, g_ffn, w_up, conv_w, conv_b, w_down, g_final, loss_target, m_g_mix, m_w_in, m_a_re, m_a_im, m_log_dt, m_b_re, m_b_im, m_c_re, m_c_im, m_d_skip, m_w_glu, m_b_glu, m_w_proj_a, m_g_sgu, m_w_s, m_b_s, m_w_proj_b, m_w_out, m_g_ffn, m_w_up, m_conv_w, m_conv_b, m_w_down, m_g_final, v_g_mix, v_w_in, v_a_re, v_a_im, v_log_dt, v_b_re, v_b_im, v_c_re, v_c_im, v_d_skip, v_w_glu, v_b_glu, v_w_proj_a, v_g_sgu, v_w_s, v_b_s, v_w_proj_b, v_w_out, v_g_ffn, v_w_up, v_conv_w, v_conv_b, v_w_down, v_g_final):
    args = dict(locals())
    me = 4 * lax.axis_index("x") + 2 * lax.axis_index("y") + lax.axis_index("c")

    def own_slot(buf, block):
        return lax.dynamic_update_slice(buf, block[None], (me,) + (0,) * block.ndim)

    gathered, (up_sh, down_sh) = _allgather([args[n][0] for n in _MIX_BIG], [MXU] * len(_MIX_BIG), "allgather_mixer",
                                            cast_only=(w_up[0], w_down[0]))
    g = dict(zip(_MIX_BIG, gathered))
    ffn_srcs = [up_sh, down_sh, conv_w[0]]
    ffn_lands = [own_slot(lax.empty((N_DEV,) + s.shape, s.dtype), s) for s in ffn_srcs]
    ag_send, ag_recv, ffn_srcs, ffn_lands, ag_token = _push_start(ffn_srcs, ffn_lands, False, "push_ffn_weights")
    w_in_full, w_pa_full, w_pb_full = _assemble_cols([g["w_in"], g["w_proj_a"], g["w_proj_b"]], "assemble_cols")
    p = {n: (args[n][0] if n != "g_final" else args[n]) for n in _SMALL}
    p.update(w_in=w_in_full, w_proj_a=w_pa_full, w_proj_b=w_pb_full,
             w_glu=g["w_glu"].reshape(SSM_W, SSM_W), w_out=g["w_out"].reshape(D_MODEL, D_MODEL))
    p["g_mix"] = p["g_mix"] + ag_token[0:1, 0:1]

    def ffn_weights(after):
        w_up_g, w_down_g, conv_w_g = _push_wait(ag_send, ag_recv, ffn_srcs, ffn_lands, False, after, "wait_ffn_weights")
        return w_up_g, conv_w_g, w_down_g.reshape(D_FF, D_MODEL)

    pushed = {}

    def ffn_grads_out(g_up, g_down):
        sends = [g_up, g_down]
        lands = [own_slot(lax.empty(s.shape, s.dtype), lax.dynamic_index_in_dim(s, me, 0, keepdims=False))
                 for s in sends]
        pushed["send"], pushed["recv"], pushed["srcs"], pushed["lands"], token = _push_start(
            sends, lands, True, "push_ffn_grads")
        return token

    loss_part, grad_x, grads = _local_step(x[0], loss_target[0], p, ffn_weights, ffn_grads_out)

    small_names = _SMALL + ("conv_w", "loss")
    small_g = dict(grads, loss=loss_part[0, 0:1])
    flats = [small_g[n].reshape(-1) for n in small_names]
    small_sizes = [f.shape[0] for f in flats]
    g_small = _pack_rows(flats, _SMALL_ROWS_MULTIPLE)
    rs8 = g_small.shape[0] // N_DEV
    recv = _all_to_all([grads[n] for n in _MIX_BIG] + [g_small.reshape(N_DEV, rs8, LANES)], "all_to_all_grads")
    small_mine = _sum_slots(recv[-1], "sum_small")
    g_small_all = _allgather([small_mine], [F32], "allgather_small")[0][0].reshape(N_DEV * rs8, LANES)
    recv_up, recv_down = _push_wait(pushed["send"], pushed["recv"], pushed["srcs"], pushed["lands"], True, g_small_all,
                                    "wait_ffn_grads")
    parts_of = dict(zip(_MIX_BIG, recv[:-1]), w_up=recv_up, w_down=recv_down)
    pieces = dict(zip(small_names, _unpack_rows(g_small_all, small_sizes)))
    loss = pieces["loss"][0]
    dconv_w = lax.dynamic_index_in_dim(pieces["conv_w"].reshape(N_DEV, 3, FF_CW), me, axis=0, keepdims=False)

    out = {}
    for n in _BIG:
        res = _adam_shard(parts_of[n], args[n], args["m_" + n], args["v_" + n], "adam_" + n)
        for kind, v in zip(("grad_", "delta_", "new_m_", "new_v_"), res):
            out[kind + n] = v
    names2 = _SMALL + ("conv_w",)
    gs = [pieces[n].reshape(_as_2d(args[n]).shape) for n in _SMALL] + [dconv_w]
    ds, m2s, v2s = _adam_small(gs, [_as_2d(args[n]) for n in names2], [_as_2d(args["m_" + n]) for n in names2],
                               [_as_2d(args["v_" + n]) for n in names2], "adam_small")
    for n, res in zip(names2, zip(gs, ds, m2s, v2s)):
        for kind, v in zip(("grad_", "delta_", "new_m_", "new_v_"), res):
            out[kind + n] = v.reshape(args[n].shape)
    order = ("g_mix", "w_in", "a_re", "a_im", "log_dt", "b_re", "b_im", "c_re", "c_im", "d_skip", "w_glu", "b_glu",
             "w_proj_a", "g_sgu", "w_s", "b_s", "w_proj_b", "w_out", "g_ffn", "w_up", "conv_w", "conv_b", "w_down",
             "g_final")
    res = [loss, grad_x.reshape(x.shape)]
    for kind in ("grad_", "delta_", "new_m_", "new_v_"):
        res += [out[kind + n] for n in order]
    return tuple(res)
```

```python
import functools
import math

import jax
import jax.numpy as jnp
from jax import lax
from jax.experimental import pallas as pl
from jax.experimental.pallas import tpu as pltpu

F32 = jnp.float32
MXU = jnp.bfloat16
EPS = 1e-6

D_MODEL = 1024
SSM_W = 512
SSM_G, SSM_H, SSM_P = 32, 16, 64
SSM_BLK = 4
SGU_W = 512
SGU_G, SGU_D, CHUNK = 8, 64, 128
D_FF = 2816
N_DEV = 8
FF_CW = 2 * D_FF // N_DEV
FF_NCB = D_FF // FF_CW
LANES = 128

ADAM_LR, ADAM_B1, ADAM_B2, ADAM_EPS, ADAM_WD, ADAM_STEP = 0.001, 0.9, 0.999, 1e-08, 0.01, 10

VMEM_LIMIT = 48 * 1024 * 1024


def _cp(*sem):
    return pltpu.CompilerParams(dimension_semantics=sem, vmem_limit_bytes=VMEM_LIMIT)


def _full(shape):
    n = len(shape)
    return pl.BlockSpec(shape, lambda *_: (0,) * n)


def _sds(shape, dtype=F32):
    return jax.ShapeDtypeStruct(shape, dtype)


def _dot(a, b):
    return jnp.dot(a, b, preferred_element_type=F32)


def _dot_nt(a, b):
    return lax.dot_general(a, b, (((1,), (1,)), ((), ())), preferred_element_type=F32)


def _dot_tn(a, b):
    return lax.dot_general(a, b, (((0,), (0,)), ((), ())), preferred_element_type=F32)


_GELU_C = math.sqrt(2.0 / math.pi)


def _gelu(x):
    return 0.5 * x * (1.0 + jnp.tanh(_GELU_C * (x + 0.044715 * (x * x * x))))


def _gelu_and_grad(x):
    t = jnp.tanh(_GELU_C * (x + 0.044715 * (x * x * x)))
    g = 0.5 * x * (1.0 + t)
    dg = 0.5 * (1.0 + t) + 0.5 * x * (1.0 - t * t) * (_GELU_C * (1.0 + 3.0 * 0.044715 * (x * x)))
    return g, dg


def _sigmoid(x):
    return 1.0 / (1.0 + jnp.exp(-x))


def _rms(x):
    return lax.rsqrt(jnp.mean(x * x, axis=-1, keepdims=True) + EPS)


def _rms_bwd(dxn, xn, r):
    return r * (dxn - xn * jnp.mean(dxn * xn, axis=-1, keepdims=True))


def _rowsum(x):
    return jnp.sum(x, axis=0, keepdims=True)


def _s5_disc(are, aim, ldt, br, bi):
    dt = jnp.exp(ldt)
    mag = jnp.exp(dt * are)
    abr = mag * jnp.cos(dt * aim)
    abi = mag * jnp.sin(dt * aim)
    den = are * are + aim * aim
    nr = abr - 1.0
    ni = abi
    fr = (nr * are + ni * aim) / den
    fi = (ni * are - nr * aim) / den
    return abr, abi, fr * br - fi * bi, fr * bi + fi * br


def _s5_params_fwd(are, aim, ldt, br, bi):
    def body(are_ref, aim_ref, ldt_ref, br_ref, bi_ref, o0, o1, o2, o3):
        outs = _s5_disc(are_ref[...], aim_ref[...], ldt_ref[...], br_ref[...], bi_ref[...])
        for o, v in zip((o0, o1, o2, o3), outs):
            o[...] = v
    shp = are.shape
    return pl.pallas_call(body, name="s5_params_fwd", out_shape=[_sds(shp)] * 4)(are, aim, ldt, br, bi)


def _s5_params_bwd(are, aim, ldt, br, bi, dabr, dabi, dbr, dbi):
    def body(are_ref, aim_ref, ldt_ref, br_ref, bi_ref, c0, c1, c2, c3, o0, o1, o2, o3, o4):
        prim = (are_ref[...], aim_ref[...], ldt_ref[...], br_ref[...], bi_ref[...])
        _, vjp = jax.vjp(_s5_disc, *prim)
        outs = vjp((c0[...], c1[...], c2[...], c3[...]))
        for o, v in zip((o0, o1, o2, o3, o4), outs):
            o[...] = v
    shp = are.shape
    return pl.pallas_call(body, name="s5_params_bwd", out_shape=[_sds(shp)] * 5)(
        are, aim, ldt, br, bi, dabr, dabi, dbr, dbi)


def _blockdiag(m_t):
    m = m_t.reshape(SSM_BLK, 8, SSM_H, 1, SSM_P)
    eye = jnp.eye(8, dtype=bool).reshape(1, 8, 1, 8, 1)
    return jnp.where(eye, m, jnp.zeros((), m_t.dtype)).reshape(SSM_BLK, 8 * SSM_H, 8 * SSM_P)


def _unblockdiag(pc):
    m = pc.reshape(SSM_BLK, 8, SSM_H, 8, SSM_P)
    return jnp.einsum("jghgp->jghp", m).reshape(SSM_G * SSM_H, SSM_P)


def _in_fwd(x, g_mix, w_in, tm):
    S = x.shape[0]

    def body(x_ref, g_ref, w_ref, h_ref, us_ref, uv_ref, gl_ref):
        xv = x_ref[...]
        h = (xv * _rms(xv) * g_ref[...]).astype(MXU)
        h_ref[...] = h
        us_ref[...] = _dot(h, w_ref[:, 0:SSM_W])
        uv_ref[...] = _dot(h, w_ref[:, SSM_W:SSM_W + 2 * SGU_W])
        gl_ref[...] = _dot(h, w_ref[:, SSM_W + 2 * SGU_W:])

    row = lambda n: pl.BlockSpec((tm, n), lambda i: (i, 0))
    return pl.pallas_call(
        body, name="in_fwd", grid=(S // tm,),
        in_specs=[row(D_MODEL), _full((1, D_MODEL)), _full(w_in.shape)],
        out_specs=[row(D_MODEL), row(SSM_W), row(2 * SGU_W), row(2 * D_MODEL)],
        out_shape=[_sds((S, D_MODEL), MXU), _sds((S, SSM_W)), _sds((S, 2 * SGU_W)), _sds((S, 2 * D_MODEL))],
        compiler_params=_cp("parallel"),
    )(x, g_mix, w_in)


def _scan_tables(ar, ai, reverse):
    n = ar.shape[-1]
    def mul(p, q):
        return p[0] * q[0] - p[1] * q[1], p[0] * q[1] + p[1] * q[0]
    a1 = (ar, ai)
    a2 = mul(a1, a1)
    a3 = mul(a2, a1)
    a4 = mul(a2, a2)
    a5 = mul(a4, a1)
    a6 = mul(a4, a2)
    a7 = mul(a4, a3)
    a8 = mul(a4, a4)
    pw = (a1, a2, a3, a4, a5, a6, a7, a8)
    rows = lax.broadcasted_iota(jnp.int32, (8, n), 0)
    tabs = []
    for s, a in ((1, a1), (2, a2), (4, a4)):
        keep = (rows + s <= 7) if reverse else (rows >= s)
        for comp in a:
            tabs.append(jnp.where(keep, jnp.broadcast_to(comp, (8, n)), 0.0))
    for c in range(2):
        q = jnp.zeros((8, n), F32)
        for r in range(8):
            e = (8 - r) if reverse else (r + 1)
            q = jnp.where(rows == r, jnp.broadcast_to(pw[e - 1][c], (8, n)), q)
        tabs.append(q)
    return tabs


def _scan_group(xr, xi, tab_ref, cr, ci, reverse):
    for t, s in enumerate((1, 2, 4)):
        pr = tab_ref[2 * t]
        pi = tab_ref[2 * t + 1]
        sh = (8 - s) if reverse else s
        sr = pltpu.roll(xr, sh, 0)
        si = pltpu.roll(xi, sh, 0)
        xr, xi = xr + pr * sr - pi * si, xi + pr * si + pi * sr
    qr = tab_ref[6]
    qi = tab_ref[7]
    return xr + qr * cr - qi * ci, xi + qr * ci + qi * cr


def _s5_fwd(us, abar_re, abar_im, b_re, b_im, c_re, c_im, d_skip, tm):
    S = us.shape[0]
    nt = S // tm
    w = 8 * SSM_P

    def body(us_ref, ar_ref, ai_ref, br_ref, bi_ref, cr_ref, ci_ref, d_ref, str_ref, sti_ref, ys_ref, tab_ref, car_ref):
        i = pl.program_id(1)

        @pl.when(i == 0)
        def _():
            car_ref[...] = jnp.zeros_like(car_ref)
            for k, t in enumerate(_scan_tables(ar_ref[...], ai_ref[...], False)):
                tab_ref[k] = t

        u = us_ref[...]
        ub = u.astype(MXU)
        str_ref[...] = _dot(ub, br_ref[0])
        sti_ref[...] = _dot(ub, bi_ref[0])

        def grp(k, carry):
            r0 = pl.multiple_of(k * 8, 8)
            xr, xi = _scan_group(str_ref[pl.ds(r0, 8), :], sti_ref[pl.ds(r0, 8), :], tab_ref, carry[0], carry[1], False)
            str_ref[pl.ds(r0, 8), :] = xr
            sti_ref[pl.ds(r0, 8), :] = xi
            return xr[7:8, :], xi[7:8, :]

        cr, ci = lax.fori_loop(0, tm // 8, grp, (car_ref[0:1, :], car_ref[1:2, :]))
        car_ref[0:1, :] = cr
        car_ref[1:2, :] = ci
        y = _dot_nt(str_ref[...].astype(MXU), cr_ref[0]) - _dot_nt(sti_ref[...].astype(MXU), ci_ref[0])
        ys_ref[...] = y + d_ref[...] * u

    blk = lambda: pl.BlockSpec((1, 8 * SSM_H, w), lambda j, i: (j, 0, 0))
    return pl.pallas_call(
        body, name="s5_fwd", grid=(SSM_BLK, nt),
        in_specs=[pl.BlockSpec((tm, LANES), lambda j, i: (i, j)),
                  pl.BlockSpec((1, w), lambda j, i: (0, j)), pl.BlockSpec((1, w), lambda j, i: (0, j)),
                  blk(), blk(), blk(), blk(),
                  pl.BlockSpec((1, LANES), lambda j, i: (0, j))],
        out_specs=[pl.BlockSpec((tm, w), lambda j, i: (i, j)), pl.BlockSpec((tm, w), lambda j, i: (i, j)),
                   pl.BlockSpec((tm, LANES), lambda j, i: (i, j))],
        out_shape=[_sds((S, SSM_BLK * w)), _sds((S, SSM_BLK * w)), _sds((S, SSM_W))],
        scratch_shapes=[pltpu.VMEM((8, 8, w), F32), pltpu.VMEM((8, w), F32)],
        compiler_params=_cp("parallel", "arbitrary"),
    )(us, abar_re, abar_im, b_re, b_im, c_re, c_im, d_skip)


def _sgu_mix(vnb, ws_ref, grp):
    acc = jnp.zeros(vnb.shape, F32)
    for g in range(SGU_G):
        acc = jnp.where(grp == g, _dot(ws_ref[g], vnb), acc)
    return acc


def _mix_fwd(x, ys, uv, gl, w_glu, b_glu, w_pa, g_sgu, ws, bias_s, w_pb, w_out, g_ffn, tm):
    S = x.shape[0]

    def body(x_ref, ys_ref, uv_ref, gl_ref, wglu_ref, bglu_ref, wpa_ref, gs_ref, ws_ref, bias_ref, wpb_ref, wout_ref,
             gf_ref, yg_ref, yap_ref, sg_ref, ya_ref, yb_ref, m_ref, x1_ref, h2_ref):
        yg = _gelu(ys_ref[...])
        ygb = yg.astype(MXU)
        yg_ref[...] = ygb
        z = _dot(ygb, wglu_ref[...]) + bglu_ref[...]
        yapb = (yg * _sigmoid(z)).astype(MXU)
        yap_ref[...] = yapb
        ya = _dot(yapb, wpa_ref[...])
        ya_ref[...] = ya

        uvg = _gelu(uv_ref[...])
        u2 = uvg[:, :SGU_W]
        v2 = uvg[:, SGU_W:]
        vnb = (v2 * _rms(v2) * gs_ref[...]).astype(MXU)
        grp = lax.broadcasted_iota(jnp.int32, (CHUNK, SGU_W), 1) // SGU_D
        for c in range(tm // CHUNK):
            rs = slice(c * CHUNK, (c + 1) * CHUNK)
            mixed = _sgu_mix(vnb[rs], ws_ref, grp) + bias_ref[...]
            sg_ref[rs, :] = (u2[rs] * mixed).astype(MXU)
        yb = _dot(sg_ref[...], wpb_ref[...])
        yb_ref[...] = yb

        glv = gl_ref[...]
        m = _sigmoid(glv[:, :D_MODEL]) * ya + _sigmoid(glv[:, D_MODEL:]) * yb
        mb = m.astype(MXU)
        m_ref[...] = mb
        x1 = x_ref[...] + _dot(mb, wout_ref[...])
        x1_ref[...] = x1
        h2_ref[...] = (x1 * _rms(x1) * gf_ref[...]).astype(MXU)

    row = lambda n: pl.BlockSpec((tm, n), lambda i: (i, 0))
    return pl.pallas_call(
        body, name="mix_fwd", grid=(S // tm,),
        in_specs=[row(D_MODEL), row(SSM_W), row(2 * SGU_W), row(2 * D_MODEL),
                  _full(w_glu.shape), _full(b_glu.shape), _full(w_pa.shape), _full(g_sgu.shape), _full(ws.shape),
                  _full(bias_s.shape), _full(w_pb.shape), _full(w_out.shape), _full(g_ffn.shape)],
        out_specs=[row(SSM_W), row(SSM_W), row(SGU_W), row(D_MODEL), row(D_MODEL), row(D_MODEL), row(D_MODEL),
                   row(D_MODEL)],
        out_shape=[_sds((S, SSM_W), MXU), _sds((S, SSM_W), MXU), _sds((S, SGU_W), MXU), _sds((S, D_MODEL)),
                   _sds((S, D_MODEL)), _sds((S, D_MODEL), MXU), _sds((S, D_MODEL)), _sds((S, D_MODEL), MXU)],
        compiler_params=_cp("parallel"),
    )(x, ys, uv, gl, w_glu, b_glu, w_pa, g_sgu, ws, bias_s, w_pb, w_out, g_ffn)


def _conv_taps(u, prev8, rows):
    t1 = prev8[7:8, :]
    t0 = prev8[6:7, :]
    s1 = jnp.where(rows == 0, t1, pltpu.roll(u, 1, 0))
    s2 = jnp.where(rows == 0, t0, jnp.where(rows == 1, t1, pltpu.roll(u, 2, 0)))
    return s1, s2


def _ffn_fwd(h2, x1, tgt, w_up, conv_w, conv_b, w_down, g_final, tm):
    S = h2.shape[0]
    nt = S // tm
    ncb = FF_NCB

    def body(h2_ref, wa_ref, wb_ref, cwa_ref, cwb_ref, cba_ref, cbb_ref, wd_ref, x1_ref, gf_ref, tgt_ref,
             up_ref, ff_ref, dx2_ref, loss_ref, dgf_ref, acc_ref, tail_ref):
        i = pl.program_id(0)
        cb = pl.program_id(1)

        @pl.when(i == 0)
        def _():
            tail_ref[cb] = jnp.zeros((2, 8, FF_CW), F32)

        @pl.when(jnp.logical_and(i == 0, cb == 0))
        def _():
            loss_ref[...] = jnp.zeros_like(loss_ref)
            dgf_ref[...] = jnp.zeros_like(dgf_ref)

        h2v = h2_ref[...]
        ua = _dot(h2v, wa_ref[0])
        ub = _dot(h2v, wb_ref[0])
        up_ref[0, 0] = ua
        up_ref[1, 0] = ub
        rows = lax.broadcasted_iota(jnp.int32, (tm, FF_CW), 0)
        s1a, s2a = _conv_taps(ua, tail_ref[cb, 0], rows)
        s1b, s2b = _conv_taps(ub, tail_ref[cb, 1], rows)
        tail_ref[cb, 0] = ua[tm - 8:tm, :]
        tail_ref[cb, 1] = ub[tm - 8:tm, :]
        cwa = cwa_ref[0]
        cwb = cwb_ref[0]
        a = cwa[0:1] * s2a + cwa[1:2] * s1a + cwa[2:3] * ua + cba_ref[0]
        b = cwb[0:1] * s2b + cwb[1:2] * s1b + cwb[2:3] * ub + cbb_ref[0]
        ffb = (a * _sigmoid(a) * b).astype(MXU)
        ff_ref[0] = ffb
        contrib = _dot(ffb, wd_ref[...])

        @pl.when(cb == 0)
        def _():
            acc_ref[...] = contrib

        @pl.when(cb > 0)
        def _():
            acc_ref[...] += contrib

        @pl.when(cb == ncb - 1)
        def _():
            x2 = x1_ref[...] + acc_ref[...]
            r = _rms(x2)
            xn = x2 * r
            g = gf_ref[...]
            diff = xn * g - tgt_ref[...]
            loss_ref[...] += (0.5 / D_MODEL) * jnp.sum(diff * diff)
            dy = diff * (1.0 / D_MODEL)
            dgf_ref[...] += _rowsum(dy * xn)
            dx2_ref[...] = _rms_bwd(dy * g, xn, r)

    row = lambda n: pl.BlockSpec((tm, n), lambda i, c: (i, 0))
    gate = lambda r: pl.BlockSpec((1, r, FF_CW), lambda i, c: (c, 0, 0))
    lin = lambda r: pl.BlockSpec((1, r, FF_CW), lambda i, c: (ncb + c, 0, 0))
    return pl.pallas_call(
        body, name="ffn_fwd", grid=(nt, ncb),
        in_specs=[row(D_MODEL), gate(D_MODEL), lin(D_MODEL), gate(3), lin(3), gate(1), lin(1),
                  pl.BlockSpec((FF_CW, D_MODEL), lambda i, c: (c, 0)),
                  row(D_MODEL), _full((1, D_MODEL)), row(D_MODEL)],
        out_specs=[pl.BlockSpec((2, 1, tm, FF_CW), lambda i, c: (0, c, i, 0)),
                   pl.BlockSpec((1, tm, FF_CW), lambda i, c: (c, i, 0)),
                   row(D_MODEL), _full((1, LANES)), _full((1, D_MODEL))],
        out_shape=[_sds((2, ncb, S, FF_CW)), _sds((ncb, S, FF_CW), MXU), _sds((S, D_MODEL)),
                   _sds((1, LANES)), _sds((1, D_MODEL))],
        scratch_shapes=[pltpu.VMEM((tm, D_MODEL), F32), pltpu.VMEM((ncb, 2, 8, FF_CW), F32)],
        compiler_params=_cp("arbitrary", "arbitrary"),
    )(h2, w_up, w_up, conv_w, conv_w, conv_b, conv_b, w_down, x1, g_final, tgt)


def _ffn_bwd(dx2, up, x1, w_up, conv_w, conv_b, w_down, g_ffn, tm):
    S = dx2.shape[0]
    nt = S // tm
    ncb = FF_NCB
    hb = tm // 8

    def body(dx2_ref, up_ref, hp_ref, cwa_ref, cwb_ref, cba_ref, cbb_ref, wd_ref, wa_ref, wb_ref,
             x1_ref, g_ref, dup_ref, dx1_ref, dconv_ref, dg_ref, acc_ref, head_ref):
        i = pl.program_id(0)
        cb = pl.program_id(1)
        ri = nt - 1 - i

        @pl.when(i == 0)
        def _():
            head_ref[cb] = jnp.zeros((2, 8, FF_CW), F32)
            dconv_ref[cb] = jnp.zeros((8, FF_CW), F32)
            dconv_ref[ncb + cb] = jnp.zeros((8, FF_CW), F32)

        @pl.when(jnp.logical_and(i == 0, cb == 0))
        def _():
            dg_ref[...] = jnp.zeros_like(dg_ref)

        dx2v = dx2_ref[...]
        dff = _dot_nt(dx2v.astype(MXU), wd_ref[...])
        ua = up_ref[0, 0]
        ub = up_ref[1, 0]
        rows = lax.broadcasted_iota(jnp.int32, (tm, FF_CW), 0)
        first = ri == 0
        s1a, s2a = _conv_taps(ua, jnp.where(first, 0.0, hp_ref[0, 0]), rows)
        s1b, s2b = _conv_taps(ub, jnp.where(first, 0.0, hp_ref[1, 0]), rows)
        cwa = cwa_ref[0]
        cwb = cwb_ref[0]
        a = cwa[0:1] * s2a + cwa[1:2] * s1a + cwa[2:3] * ua + cba_ref[0]
        b = cwb[0:1] * s2b + cwb[1:2] * s1b + cwb[2:3] * ub + cbb_ref[0]
        sa = _sigmoid(a)
        da = dff * b * (sa * (1.0 + a * (1.0 - sa)))
        db = dff * (a * sa)

        def conv_bwd(dup, head8, cw):
            h0 = head8[0:1, :]
            h1 = head8[1:2, :]
            n1 = jnp.where(rows == tm - 1, h0, pltpu.roll(dup, tm - 1, 0))
            n2 = jnp.where(rows == tm - 2, h0, jnp.where(rows == tm - 1, h1, pltpu.roll(dup, tm - 2, 0)))
            return cw[2:3] * dup + cw[1:2] * n1 + cw[0:1] * n2

        dpa = conv_bwd(da, head_ref[cb, 0], cwa).astype(MXU)
        dpb = conv_bwd(db, head_ref[cb, 1], cwb).astype(MXU)
        head_ref[cb, 0] = da[0:8, :]
        head_ref[cb, 1] = db[0:8, :]
        dup_ref[0, 0] = dpa
        dup_ref[1, 0] = dpb
        for slot, dup, s2, s1, u in ((cb, da, s2a, s1a, ua), (ncb + cb, db, s2b, s1b, ub)):
            dconv_ref[slot, 0:1, :] += _rowsum(dup * s2)
            dconv_ref[slot, 1:2, :] += _rowsum(dup * s1)
            dconv_ref[slot, 2:3, :] += _rowsum(dup * u)
            dconv_ref[slot, 3:4, :] += _rowsum(dup)
        contrib = _dot_nt(dpa, wa_ref[0]) + _dot_nt(dpb, wb_ref[0])

        @pl.when(cb == 0)
        def _():
            acc_ref[...] = contrib

        @pl.when(cb > 0)
        def _():
            acc_ref[...] += contrib

        @pl.when(cb == ncb - 1)
        def _():
            x1v = x1_ref[...]
            r = _rms(x1v)
            xn = x1v * r
            dh2 = acc_ref[...]
            dg_ref[...] += _rowsum(dh2 * xn)
            dx1_ref[...] = dx2v + _rms_bwd(dh2 * g_ref[...], xn, r)

    row = lambda n: pl.BlockSpec((tm, n), lambda i, c: (nt - 1 - i, 0))
    colb = lambda: pl.BlockSpec((2, 1, tm, FF_CW), lambda i, c: (0, c, nt - 1 - i, 0))
    halo = lambda: pl.BlockSpec((2, 1, 8, FF_CW), lambda i, c: (0, c, jnp.maximum((nt - 1 - i) * hb - 1, 0), 0))
    gate = lambda r: pl.BlockSpec((1, r, FF_CW), lambda i, c: (c, 0, 0))
    lin = lambda r: pl.BlockSpec((1, r, FF_CW), lambda i, c: (ncb + c, 0, 0))
    return pl.pallas_call(
        body, name="ffn_bwd", grid=(nt, ncb),
        in_specs=[row(D_MODEL), colb(), halo(), gate(3), lin(3), gate(1), lin(1),
                  pl.BlockSpec((FF_CW, D_MODEL), lambda i, c: (c, 0)),
                  gate(D_MODEL), lin(D_MODEL), row(D_MODEL), _full((1, D_MODEL))],
        out_specs=[colb(), row(D_MODEL), _full((2 * ncb, 8, FF_CW)), _full((1, D_MODEL))],
        out_shape=[_sds((2, ncb, S, FF_CW), MXU), _sds((S, D_MODEL)), _sds((2 * ncb, 8, FF_CW)), _sds((1, D_MODEL))],
        scratch_shapes=[pltpu.VMEM((tm, D_MODEL), F32), pltpu.VMEM((ncb, 2, 8, FF_CW), F32)],
        compiler_params=_cp("arbitrary", "arbitrary"),
    )(dx2, up, up, conv_w, conv_w, conv_b, conv_b, w_down, w_up, w_up, x1, g_ffn)


def _mix_bwd(dx1, gl, ya, yb, ys, uv, w_out, w_pa, w_pb, w_glu, b_glu, g_sgu, ws, ws_t, bias_s, tm):
    S = dx1.shape[0]

    def body(dx1_ref, gl_ref, ya_ref, yb_ref, ys_ref, uv_ref, wout_ref, wpa_ref, wpb_ref, wglu_ref, bglu_ref, gs_ref,
             ws_ref, wst_ref, bias_ref,
             dgl_ref, dya_ref, dyb_ref, dz_ref, dys_ref, duv_ref, dbglu_ref, dgs_ref, dws_ref, dbs_ref,
             du2_ref, dvn_ref):
        i = pl.program_id(0)

        @pl.when(i == 0)
        def _():
            dbglu_ref[...] = jnp.zeros_like(dbglu_ref)
            dgs_ref[...] = jnp.zeros_like(dgs_ref)
            dws_ref[...] = jnp.zeros_like(dws_ref)
            dbs_ref[...] = jnp.zeros_like(dbs_ref)

        dm = _dot_nt(dx1_ref[...].astype(MXU), wout_ref[...])
        glv = gl_ref[...]
        ga = _sigmoid(glv[:, :D_MODEL])
        gb = _sigmoid(glv[:, D_MODEL:])
        dgl_ref[:, :D_MODEL] = (dm * ya_ref[...] * ga * (1.0 - ga)).astype(MXU)
        dgl_ref[:, D_MODEL:] = (dm * yb_ref[...] * gb * (1.0 - gb)).astype(MXU)
        dyab = (dm * ga).astype(MXU)
        dybb = (dm * gb).astype(MXU)
        dya_ref[...] = dyab
        dyb_ref[...] = dybb

        dyap = _dot_nt(dyab, wpa_ref[...])
        yg, dgelu = _gelu_and_grad(ys_ref[...])
        sz = _sigmoid(_dot(yg.astype(MXU), wglu_ref[...]) + bglu_ref[...])
        dz = dyap * yg * sz * (1.0 - sz)
        dzb = dz.astype(MXU)
        dz_ref[...] = dzb
        dbglu_ref[...] += _rowsum(dz)
        dys_ref[...] = (dyap * sz + _dot_nt(dzb, wglu_ref[...])) * dgelu

        dsg = _dot_nt(dybb, wpb_ref[...])
        uvg, duvg = _gelu_and_grad(uv_ref[...])
        u2 = uvg[:, :SGU_W]
        v2 = uvg[:, SGU_W:]
        rv = _rms(v2)
        vhat = v2 * rv
        gs = gs_ref[...]
        vnb = (vhat * gs).astype(MXU)
        grp = lax.broadcasted_iota(jnp.int32, (CHUNK, SGU_W), 1) // SGU_D
        tril = (lax.broadcasted_iota(jnp.int32, (CHUNK, CHUNK), 0)
                >= lax.broadcasted_iota(jnp.int32, (CHUNK, CHUNK), 1))
        for c in range(tm // CHUNK):
            rs = slice(c * CHUNK, (c + 1) * CHUNK)
            vc = vnb[rs]
            mixed = _sgu_mix(vc, ws_ref, grp) + bias_ref[...]
            dsg_c = dsg[rs]
            du2_ref[rs, :] = dsg_c * mixed
            dmx = dsg_c * u2[rs]
            dbs_ref[...] += dmx
            dmb = dmx.astype(MXU)
            dvn_ref[rs, :] = _sgu_mix(dmb, wst_ref, grp)
            for g in range(SGU_G):
                part = _dot_nt(jnp.where(grp == g, dmb, jnp.zeros((), MXU)), vc)
                dws_ref[g] += jnp.where(tril, part, 0.0)
        dvn = dvn_ref[...]
        dgs_ref[...] += _rowsum(dvn * vhat)
        dv2 = _rms_bwd(dvn * gs, vhat, rv)
        duv_ref[:, :SGU_W] = (du2_ref[...] * duvg[:, :SGU_W]).astype(MXU)
        duv_ref[:, SGU_W:] = (dv2 * duvg[:, SGU_W:]).astype(MXU)

    row = lambda n: pl.BlockSpec((tm, n), lambda i: (i, 0))
    return pl.pallas_call(
        body, name="mix_bwd", grid=(S // tm,),
        in_specs=[row(D_MODEL), row(2 * D_MODEL), row(D_MODEL), row(D_MODEL), row(SSM_W), row(2 * SGU_W),
                  _full(w_out.shape), _full(w_pa.shape), _full(w_pb.shape), _full(w_glu.shape), _full(b_glu.shape),
                  _full(g_sgu.shape), _full(ws.shape), _full(ws_t.shape), _full(bias_s.shape)],
        out_specs=[row(2 * D_MODEL), row(D_MODEL), row(D_MODEL), row(SSM_W), row(SSM_W), row(2 * SGU_W),
                   _full((1, SSM_W)), _full((1, SGU_W)), _full((SGU_G, CHUNK, CHUNK)), _full((CHUNK, SGU_W))],
        out_shape=[_sds((S, 2 * D_MODEL), MXU), _sds((S, D_MODEL), MXU), _sds((S, D_MODEL), MXU), _sds((S, SSM_W), MXU),
                   _sds((S, SSM_W)), _sds((S, 2 * SGU_W), MXU),
                   _sds((1, SSM_W)), _sds((1, SGU_W)), _sds((SGU_G, CHUNK, CHUNK)), _sds((CHUNK, SGU_W))],
        scratch_shapes=[pltpu.VMEM((tm, SGU_W), F32), pltpu.VMEM((tm, SGU_W), F32)],
        compiler_params=_cp("arbitrary"),
    )(dx1, gl, ya, yb, ys, uv, w_out, w_pa, w_pb, w_glu, b_glu, g_sgu, ws, ws_t, bias_s)


def _s5_bwd(dys, us, st_re, st_im, abar_re, abar_im, b_re, b_im, c_re, c_im, d_skip, tm):
    S = us.shape[0]
    nt = S // tm
    w = 8 * SSM_P
    hb = tm // 8

    def body(dys_ref, us_ref, str_ref, sti_ref, hr_ref, hi_ref, ar_ref, ai_ref, br_ref, bi_ref, cr_ref, ci_ref, d_ref,
             dus_ref, dab_ref, dd_ref, dbr_ref, dbi_ref, dcr_ref, dci_ref, tab_ref, car_ref, gr_ref, gi_ref):
        i = pl.program_id(1)
        ri = nt - 1 - i

        @pl.when(i == 0)
        def _():
            car_ref[...] = jnp.zeros_like(car_ref)
            for k, t in enumerate(_scan_tables(ar_ref[...], -ai_ref[...], True)):
                tab_ref[k] = t
            for r in (dab_ref, dd_ref, dbr_ref, dbi_ref, dcr_ref, dci_ref):
                r[...] = jnp.zeros_like(r)

        dys_v = dys_ref[...]
        dyb = dys_v.astype(MXU)
        gr_ref[...] = _dot(dyb, cr_ref[0])
        gi_ref[...] = -_dot(dyb, ci_ref[0])

        def grp(kk, carry):
            r0 = pl.multiple_of((hb - 1 - kk) * 8, 8)
            xr, xi = _scan_group(gr_ref[pl.ds(r0, 8), :], gi_ref[pl.ds(r0, 8), :], tab_ref, carry[0], carry[1], True)
            gr_ref[pl.ds(r0, 8), :] = xr
            gi_ref[pl.ds(r0, 8), :] = xi
            return xr[0:1, :], xi[0:1, :]

        cr, ci = lax.fori_loop(0, hb, grp, (car_ref[0:1, :], car_ref[1:2, :]))
        car_ref[0:1, :] = cr
        car_ref[1:2, :] = ci

        gsr = gr_ref[...]
        gsi = gi_ref[...]
        sr = str_ref[...]
        si = sti_ref[...]
        rows = lax.broadcasted_iota(jnp.int32, (tm, w), 0)
        first = ri == 0
        spr = jnp.where(rows == 0, jnp.where(first, 0.0, hr_ref[7:8, :]), pltpu.roll(sr, 1, 0))
        spi = jnp.where(rows == 0, jnp.where(first, 0.0, hi_ref[7:8, :]), pltpu.roll(si, 1, 0))
        dab_ref[0, 0:1, :] += _rowsum(gsr * spr + gsi * spi)
        dab_ref[0, 1:2, :] += _rowsum(gsi * spr - gsr * spi)

        gbr = gsr.astype(MXU)
        gbi = gsi.astype(MXU)
        u = us_ref[...]
        ub = u.astype(MXU)
        dus_ref[...] = (_dot_nt(gbr, br_ref[0]) + _dot_nt(gbi, bi_ref[0]) + d_ref[...] * dys_v).astype(MXU)
        dd_ref[0, 0:1, :] += _rowsum(dys_v * u)
        dbr_ref[0] += _dot_tn(ub, gbr)
        dbi_ref[0] += _dot_tn(ub, gbi)
        dcr_ref[0] += _dot_tn(dyb, sr.astype(MXU))
        dci_ref[0] -= _dot_tn(dyb, si.astype(MXU))

    blk = lambda: pl.BlockSpec((1, 8 * SSM_H, w), lambda j, i: (j, 0, 0))
    rowl = lambda: pl.BlockSpec((tm, LANES), lambda j, i: (nt - 1 - i, j))
    roww = lambda: pl.BlockSpec((tm, w), lambda j, i: (nt - 1 - i, j))
    halo = lambda: pl.BlockSpec((8, w), lambda j, i: (jnp.maximum((nt - 1 - i) * hb - 1, 0), j))
    return pl.pallas_call(
        body, name="s5_bwd", grid=(SSM_BLK, nt),
        in_specs=[rowl(), rowl(), roww(), roww(), halo(), halo(),
                  pl.BlockSpec((1, w), lambda j, i: (0, j)), pl.BlockSpec((1, w), lambda j, i: (0, j)),
                  blk(), blk(), blk(), blk(),
                  pl.BlockSpec((1, LANES), lambda j, i: (0, j))],
        out_specs=[rowl(),
                   pl.BlockSpec((1, 8, w), lambda j, i: (j, 0, 0)), pl.BlockSpec((1, 8, LANES), lambda j, i: (j, 0, 0)),
                   blk(), blk(), blk(), blk()],
        out_shape=[_sds((S, SSM_W), MXU), _sds((SSM_BLK, 8, w)), _sds((SSM_BLK, 8, LANES)),
                   _sds((SSM_BLK, 8 * SSM_H, w)), _sds((SSM_BLK, 8 * SSM_H, w)),
                   _sds((SSM_BLK, 8 * SSM_H, w)), _sds((SSM_BLK, 8 * SSM_H, w))],
        scratch_shapes=[pltpu.VMEM((8, 8, w), F32), pltpu.VMEM((8, w), F32),
                        pltpu.VMEM((tm, w), F32), pltpu.VMEM((tm, w), F32)],
        compiler_params=_cp("parallel", "arbitrary"),
    )(dys, us, st_re, st_im, st_re, st_im, abar_re, abar_im, b_re, b_im, c_re, c_im, d_skip)


def _in_bwd(dus, duv, dgl, dx1, x, g_mix, w_in, tm):
    S = x.shape[0]

    def body(dus_ref, duv_ref, dgl_ref, dx1_ref, x_ref, g_ref, w_ref, gx_ref, dg_ref):
        @pl.when(pl.program_id(0) == 0)
        def _():
            dg_ref[...] = jnp.zeros_like(dg_ref)

        dh = (_dot_nt(dus_ref[...], w_ref[:, 0:SSM_W])
              + _dot_nt(duv_ref[...], w_ref[:, SSM_W:SSM_W + 2 * SGU_W])
              + _dot_nt(dgl_ref[...], w_ref[:, SSM_W + 2 * SGU_W:]))
        xv = x_ref[...]
        r = _rms(xv)
        xn = xv * r
        dg_ref[...] += _rowsum(dh * xn)
        gx_ref[...] = dx1_ref[...] + _rms_bwd(dh * g_ref[...], xn, r)

    row = lambda n: pl.BlockSpec((tm, n), lambda i: (i, 0))
    return pl.pallas_call(
        body, name="in_bwd", grid=(S // tm,),
        in_specs=[row(SSM_W), row(2 * SGU_W), row(2 * D_MODEL), row(D_MODEL), row(D_MODEL), _full((1, D_MODEL)),
                  _full(w_in.shape)],
        out_specs=[row(D_MODEL), _full((1, D_MODEL))],
        out_shape=[_sds((S, D_MODEL)), _sds((1, D_MODEL))],
        compiler_params=_cp("arbitrary"),
    )(dus, duv, dgl, dx1, x, g_mix, w_in)


def _pick(n, cands):
    for c in cands:
        if n % c == 0:
            return c
    return n


def _wgrad_split(a, bs, nsplit, tk, name):
    S, K = a.shape
    widths = [b.shape[1] for b in bs]
    N = sum(widths)
    c = N // nsplit
    ts = _pick(S, (512, 256, 128))
    ns = S // ts

    def body(*refs):
        a_ref = refs[0]
        b_refs = refs[1:1 + len(bs)]
        o_ref = refs[1 + len(bs)]
        acc_ref = refs[2 + len(bs)]
        s = pl.program_id(1)
        av = a_ref[...].astype(MXU)
        off = 0
        for b_ref, wdt in zip(b_refs, widths):
            part = _dot_tn(av, b_ref[...].astype(MXU))

            @pl.when(s == 0)
            def _():
                acc_ref[:, off:off + wdt] = part

            @pl.when(s > 0)
            def _():
                acc_ref[:, off:off + wdt] += part

            off += wdt

        @pl.when(s == ns - 1)
        def _():
            for d in range(nsplit):
                o_ref[d] = acc_ref[:, c * d:c * (d + 1)].astype(MXU)

    return pl.pallas_call(
        body, name=name, grid=(K // tk, ns),
        in_specs=[pl.BlockSpec((ts, tk), lambda k, s: (s, k))]
                 + [pl.BlockSpec((ts, wdt), lambda k, s: (s, 0)) for wdt in widths],
        out_specs=pl.BlockSpec((nsplit, tk, c), lambda k, s: (0, k, 0)),
        out_shape=_sds((nsplit, K, c), MXU),
        scratch_shapes=[pltpu.VMEM((tk, N), F32)],
        compiler_params=_cp("parallel", "arbitrary"),
    )(a, *bs)


def _wgrad_blk(a3, b3, nblk, a_of, b_of, name):
    S, K = a3.shape[1:]
    N = b3.shape[2]
    ts = _pick(S, (512, 256, 128))
    ns = S // ts

    def body(a_ref, b_ref, o_ref, acc_ref):
        s = pl.program_id(1)
        part = _dot_tn(a_ref[0].astype(MXU), b_ref[0].astype(MXU))

        @pl.when(s == 0)
        def _():
            acc_ref[...] = part

        @pl.when(s > 0)
        def _():
            acc_ref[...] += part

        @pl.when(s == ns - 1)
        def _():
            o_ref[0] = acc_ref[...].astype(MXU)

    return pl.pallas_call(
        body, name=name, grid=(nblk, ns),
        in_specs=[pl.BlockSpec((1, ts, K), lambda b, s: (a_of(b), s, 0)),
                  pl.BlockSpec((1, ts, N), lambda b, s: (b_of(b), s, 0))],
        out_specs=pl.BlockSpec((1, K, N), lambda b, s: (b, 0, 0)),
        out_shape=_sds((nblk, K, N), MXU),
        scratch_shapes=[pltpu.VMEM((K, N), F32)],
        compiler_params=_cp("parallel", "arbitrary"),
    )(a3, b3)


def _assemble_cols(blocks_list, name):
    def body(*refs):
        n = len(blocks_list)
        for b_ref, o_ref in zip(refs[:n], refs[n:]):
            c = b_ref.shape[2]
            for d in range(N_DEV):
                o_ref[:, c * d:c * (d + 1)] = b_ref[d]

    return pl.pallas_call(
        body, name=name,
        out_shape=[_sds((b.shape[1], N_DEV * b.shape[2]), b.dtype) for b in blocks_list],
        compiler_params=pltpu.CompilerParams(vmem_limit_bytes=VMEM_LIMIT),
    )(*blocks_list)


def _tile(S, want):
    return want if S % want == 0 else S


def _local_step(x, tgt, p, ffn_weights, grads_out):
    S = x.shape[0]
    tm = _tile(S, 256)
    tl = _tile(S, 512)

    rep = lambda a: jnp.repeat(a, SSM_H, axis=0)
    are = rep(p["a_re"])
    aim = rep(p["a_im"])
    ldt = jnp.broadcast_to(rep(p["log_dt"].reshape(SSM_G, 1)), are.shape)
    br_t = p["b_re"].transpose(0, 2, 1).reshape(are.shape)
    bi_t = p["b_im"].transpose(0, 2, 1).reshape(are.shape)
    abr, abi, bbr, bbi = _s5_params_fwd(are, aim, ldt, br_t, bi_t)
    head = lambda a: a.reshape(SSM_G, SSM_H, SSM_P)[:, 0, :].reshape(1, SSM_G * SSM_P)
    abar_re, abar_im = head(abr), head(abi)
    bd_br = _blockdiag(bbr).astype(MXU)
    bd_bi = _blockdiag(bbi).astype(MXU)
    bd_cr = _blockdiag(p["c_re"].reshape(are.shape)).astype(MXU)
    bd_ci = _blockdiag(p["c_im"].reshape(are.shape)).astype(MXU)
    d_skip = p["d_skip"].reshape(1, SSM_W)

    tril = jnp.tril(jnp.ones((CHUNK, CHUNK), dtype=bool))
    ws = jnp.where(tril[None], p["w_s"], 0.0)
    ws_b = ws.astype(MXU)
    ws_t = ws.transpose(0, 2, 1).astype(MXU)
    bias_s = jnp.repeat(p["b_s"].T, SGU_D, axis=1)

    g_mix = p["g_mix"].reshape(1, D_MODEL)
    g_ffn = p["g_ffn"].reshape(1, D_MODEL)
    g_final = p["g_final"].reshape(1, D_MODEL)
    g_sgu = p["g_sgu"].reshape(1, SGU_W)
    b_glu = p["b_glu"].reshape(1, SSM_W)
    conv_b = p["conv_b"].reshape(N_DEV, 1, FF_CW)

    h1, us, uv, gl = _in_fwd(x, g_mix, p["w_in"], tm)
    st_re, st_im, ys = _s5_fwd(us, abar_re, abar_im, bd_br, bd_bi, bd_cr, bd_ci, d_skip, tl)
    yg, yap, sg, ya, yb, m, x1, h2 = _mix_fwd(x, ys, uv, gl, p["w_glu"], b_glu, p["w_proj_a"], g_sgu, ws_b, bias_s,
                                              p["w_proj_b"], p["w_out"], g_ffn, tm)
    w_up, conv_w, w_down = ffn_weights(h2)
    up, ff, dx2, loss, dg_final = _ffn_fwd(h2, x1, tgt, w_up, conv_w, conv_b, w_down, g_final, tl)

    dup, dx1, dconv, dg_ffn = _ffn_bwd(dx2, up, x1, w_up, conv_w, conv_b, w_down, g_ffn, tl)
    rows8 = lambda g: g.reshape(N_DEV, g.shape[1] // N_DEV, g.shape[2])
    g_up = _wgrad_blk(h2[None], dup.reshape(N_DEV, S, FF_CW), N_DEV, lambda b: 0, lambda b: b, "wgrad_up")
    g_down = _wgrad_blk(ff, dx2[None], FF_NCB, lambda b: b, lambda b: 0, "wgrad_down").reshape(
        N_DEV, D_FF // N_DEV, D_MODEL)
    token = grads_out(("w_up", "w_down"), (g_up, g_down))
    dgl, dya, dyb, dz, dys, duv, db_glu, dg_sgu, dws, dbs = _mix_bwd(
        dx1, gl, ya, yb, ys, uv, p["w_out"], p["w_proj_a"], p["w_proj_b"], p["w_glu"], b_glu + token[0:1, 0:1], g_sgu,
        ws_b, ws_t, bias_s, tm)
    token = grads_out(("w_glu", "w_proj_a", "w_proj_b", "w_out"),
                      (rows8(_wgrad_split(yg, [dz], 1, SSM_W, "wgrad_glu")),
                       _wgrad_split(yap, [dya], N_DEV, SSM_W, "wgrad_pa"),
                       _wgrad_split(sg, [dyb], N_DEV, SGU_W, "wgrad_pb"),
                       rows8(_wgrad_split(m, [dx1], 1, 512, "wgrad_out"))))
    dus, dab, dd, dbbr, dbbi, dcr, dci = _s5_bwd(dys, us, st_re, st_im, abar_re, abar_im, bd_br, bd_bi, bd_cr, bd_ci,
                                                 d_skip + token[0:1, 0:1], tl)
    token = grads_out(("w_in",), (_wgrad_split(h1, [dus, duv, dgl], N_DEV, 256, "wgrad_in"),))
    grad_x, dg_mix = _in_bwd(dus, duv, dgl, dx1, x, g_mix + token[0:1, 0:1], p["w_in"], tm)

    spread = lambda v: jnp.repeat(v.reshape(SSM_G, SSM_P), SSM_H, axis=0) * (1.0 / SSM_H)
    dabr = spread(dab[:, 0, :])
    dabi = spread(dab[:, 1, :])
    dare, daim, dldt, dbr_t, dbi_t = _s5_params_bwd(are, aim, ldt, br_t, bi_t, dabr, dabi,
                                                    _unblockdiag(dbbr), _unblockdiag(dbbi))
    fold = lambda a: a.reshape(SSM_G, SSM_H, SSM_P).sum(axis=1)
    unt = lambda a: a.reshape(SSM_G, SSM_H, SSM_P).transpose(0, 2, 1)

    grads = {
        "g_mix": dg_mix,
        "a_re": fold(dare), "a_im": fold(daim), "log_dt": fold(dldt).sum(axis=1),
        "b_re": unt(dbr_t), "b_im": unt(dbi_t),
        "c_re": _unblockdiag(dcr).reshape(SSM_G, SSM_H, SSM_P),
        "c_im": _unblockdiag(dci).reshape(SSM_G, SSM_H, SSM_P),
        "d_skip": dd[:, 0, :].reshape(SSM_W),
        "b_glu": db_glu,
        "g_sgu": dg_sgu,
        "w_s": dws,
        "b_s": dbs.reshape(CHUNK, SGU_G, SGU_D).sum(axis=-1).T,
        "g_ffn": dg_ffn,
        "conv_w": dconv[:, 0:3, :],
        "conv_b": dconv[:, 3, :].reshape(2 * D_FF),
        "g_final": dg_final,
    }
    return loss, grad_x, grads


_ANY = pl.BlockSpec(memory_space=pl.ANY)
_MESH = pl.DeviceIdType.MESH


def _allgather(shards, dtypes, name, cast_only=()):
    n = len(shards)
    e = len(cast_only)

    def body(*refs):
        in_refs, extra_in = refs[:n], refs[n:n + e]
        out_refs, extra_out = refs[n + e:2 * n + e], refs[2 * n + e:2 * n + 2 * e]
        stage = refs[2 * n + 2 * e:3 * n + 2 * e]
        send_sems, recv_sems, local_sems = refs[3 * n + 2 * e:]
        for a in range(n):
            stage[a][...] = in_refs[a][...].astype(dtypes[a])
        for i in range(e):
            extra_out[i][...] = extra_in[i][...].astype(MXU)
        x, y, c = lax.axis_index("x"), lax.axis_index("y"), lax.axis_index("c")
        me, sibling = (x, y, c), (x, y, 1 - c)
        chips = [(1 - x, y), (x, 1 - y), (1 - x, 1 - y)]

        def slot(a, px, py, pc):
            return out_refs[a].at[4 * px + 2 * py + pc]

        def copy(a, k, block, to, src=None):
            return pltpu.make_async_remote_copy(
                src_ref=slot(a, *block) if src is None else src, dst_ref=slot(a, *block),
                send_sem=send_sems.at[a, k], recv_sem=recv_sems.at[a, k], device_id=to, device_id_type=_MESH)

        mine = [pltpu.make_async_copy(stage[a], slot(a, *me), local_sems.at[a]) for a in range(n)]
        for cp in mine:
            cp.start()
        first = []
        for j, chip in enumerate(chips):
            first += [copy(a, 1 + j, me, (*chip, c), src=stage[a]) for a in range(n)]
        first += [copy(a, 0, me, sibling, src=stage[a]) for a in range(n)]
        for cp in first:
            cp.start()
        passed = []
        for j, chip in enumerate(chips):
            for a in range(n):
                copy(a, 1 + j, (*chip, c), me).wait_recv()
                fwd = copy(a, 4 + j, (*chip, c), sibling)
                fwd.start()
                passed.append(fwd)
        for a in range(n):
            copy(a, 0, sibling, me).wait_recv()
        for j, chip in enumerate(chips):
            for a in range(n):
                copy(a, 4 + j, (*chip, 1 - c), me).wait_recv()
        for cp in first + passed:
            cp.wait_send()
        for cp in mine:
            cp.wait()

    vmem = pl.BlockSpec(memory_space=pltpu.VMEM)
    res = pl.pallas_call(
        body, name=name, in_specs=[vmem] * (n + e), out_specs=[_ANY] * n + [vmem] * e,
        out_shape=[_sds((N_DEV,) + s.shape, dt) for s, dt in zip(shards, dtypes)]
                  + [_sds(s.shape, MXU) for s in cast_only],
        scratch_shapes=[pltpu.VMEM(s.shape, dt) for s, dt in zip(shards, dtypes)]
                       + [pltpu.SemaphoreType.DMA((n, 7)), pltpu.SemaphoreType.DMA((n, 7)), pltpu.SemaphoreType.DMA((n,))],
        compiler_params=pltpu.CompilerParams(vmem_limit_bytes=VMEM_LIMIT),
    )(*shards, *cast_only)
    return res[:n], res[n:]


def _all_to_all(sends, name):
    n = len(sends)

    def body(*refs):
        send_refs, recv_refs = refs[:n], refs[n:2 * n]
        send_sems, recv_sems, local_sems = refs[2 * n:]
        x, y, c = lax.axis_index("x"), lax.axis_index("y"), lax.axis_index("c")
        me = 4 * x + 2 * y + c
        mine = [pltpu.make_async_copy(send_refs[a].at[me], recv_refs[a].at[me], local_sems.at[a]) for a in range(n)]
        for cp in mine:
            cp.start()
        copies = []
        for k in (2, 4, 6, 3, 5, 7, 1):
            px = 1 - x if k & 4 else x
            py = 1 - y if k & 2 else y
            pc = 1 - c if k & 1 else c
            peer = 4 * px + 2 * py + pc
            for a in range(n):
                sems = dict(send_sem=send_sems.at[a, k - 1], recv_sem=recv_sems.at[a, k - 1],
                            device_id=(px, py, pc), device_id_type=_MESH)
                cp = pltpu.make_async_remote_copy(src_ref=send_refs[a].at[peer], dst_ref=recv_refs[a].at[me], **sems)
                cp.start()
                landing = pltpu.make_async_remote_copy(src_ref=send_refs[a].at[peer], dst_ref=recv_refs[a].at[peer],
                                                       **sems)
                copies.append((cp, landing))
        for _, landing in copies:
            landing.wait_recv()
        for cp, _ in copies:
            cp.wait_send()
        for cp in mine:
            cp.wait()

    return pl.pallas_call(
        body, name=name, in_specs=[_ANY] * n, out_specs=[_ANY] * n,
        out_shape=[_sds(s.shape, s.dtype) for s in sends],
        scratch_shapes=[pltpu.SemaphoreType.DMA((n, 7)), pltpu.SemaphoreType.DMA((n, 7)), pltpu.SemaphoreType.DMA((n,))],
    )(*sends)


_HBM = pl.BlockSpec(memory_space=pltpu.HBM)
_SEM = pl.BlockSpec(memory_space=pltpu.SEMAPHORE)
_EFFECT = pltpu.SideEffectType.DATAFLOW_SIDE_EFFECTING
_PEER_ORDER = (2, 4, 6, 3, 5, 7, 1)


def _peer(k):
    x, y, c = lax.axis_index("x"), lax.axis_index("y"), lax.axis_index("c")
    px = 1 - x if k & 4 else x
    py = 1 - y if k & 2 else y
    pc = 1 - c if k & 1 else c
    return (px, py, pc), 4 * px + 2 * py + pc


def _push_start(srcs, lands, slotted, name):
    n = len(srcs)

    def body(*refs):
        src_refs, land_refs = refs[:n], refs[n:2 * n]
        send_sems, recv_sems, token = refs[2 * n], refs[2 * n + 1], refs[-1]
        me = 4 * lax.axis_index("x") + 2 * lax.axis_index("y") + lax.axis_index("c")
        for k in _PEER_ORDER:
            dev, peer = _peer(k)
            for a in range(n):
                pltpu.make_async_remote_copy(
                    src_ref=src_refs[a].at[peer] if slotted else src_refs[a], dst_ref=land_refs[a].at[me],
                    send_sem=send_sems.at[7 * a + k - 1], recv_sem=recv_sems.at[7 * a + k - 1],
                    device_id=dev, device_id_type=_MESH).start()
        token[...] = jnp.zeros_like(token)

    bufs = list(srcs) + list(lands)
    res = pl.pallas_call(
        body, name=name, in_specs=[_HBM] * (2 * n),
        out_specs=(_SEM, _SEM, *[_HBM] * (2 * n), pl.BlockSpec(memory_space=pltpu.VMEM)),
        out_shape=(pltpu.SemaphoreType.DMA((7 * n,)), pltpu.SemaphoreType.DMA((7 * n,)),
                   *[pltpu.HBM(b.shape, b.dtype) for b in bufs], _sds((8, LANES))),
        input_output_aliases={i: 2 + i for i in range(2 * n)},
        compiler_params=pltpu.CompilerParams(has_side_effects=_EFFECT),
    )(*[pltpu.with_memory_space_constraint(b, pltpu.HBM) for b in bufs])
    return res[0], res[1], res[2:2 + n], res[2 + n:2 + 2 * n], res[-1]


def _push_wait(send_sems, recv_sems, srcs, lands, slotted, after, name):
    n = len(srcs)

    def body(*refs):
        src_refs, land_refs = refs[:n], refs[n:2 * n]
        send_sems, recv_sems = refs[2 * n], refs[2 * n + 1]
        for k in _PEER_ORDER:
            dev, peer = _peer(k)
            for a in range(n):
                cp = pltpu.make_async_remote_copy(
                    src_ref=src_refs[a].at[peer] if slotted else src_refs[a], dst_ref=land_refs[a].at[peer],
                    send_sem=send_sems.at[7 * a + k - 1], recv_sem=recv_sems.at[7 * a + k - 1],
                    device_id=dev, device_id_type=_MESH)
                cp.wait_send()
                cp.wait_recv()

    bufs = list(srcs) + list(lands)
    res = pl.pallas_call(
        body, name=name, in_specs=[_HBM] * (2 * n) + [_SEM, _SEM] + [_ANY] * len(after), out_specs=[_HBM] * (2 * n),
        out_shape=[pltpu.HBM(b.shape, b.dtype) for b in bufs],
        input_output_aliases={i: i for i in range(2 * n)},
        compiler_params=pltpu.CompilerParams(has_side_effects=_EFFECT),
    )(*bufs, send_sems, recv_sems, *after)
    return res[n:]


def _adamw(w, g, m, v):
    m2 = ADAM_B1 * m + (1.0 - ADAM_B1) * g
    v2 = ADAM_B2 * v + (1.0 - ADAM_B2) * (g * g)
    m_hat = m2 / (1.0 - ADAM_B1 ** ADAM_STEP)
    v_hat = v2 / (1.0 - ADAM_B2 ** ADAM_STEP)
    delta = -ADAM_LR * (m_hat / (jnp.sqrt(v_hat) + ADAM_EPS) + ADAM_WD * w)
    return delta, m2, v2


def _adam_shard(parts, w, m, v, name):
    _, r, c = w.shape
    tr = 256 if r % 256 == 0 else r

    def body(p_ref, w_ref, m_ref, v_ref, g_ref, d_ref, m2_ref, v2_ref):
        g = p_ref[0].astype(F32)
        for s in range(1, N_DEV):
            g = g + p_ref[s].astype(F32)
        g_ref[0] = g
        d_ref[0], m2_ref[0], v2_ref[0] = _adamw(w_ref[0], g, m_ref[0], v_ref[0])

    row = lambda: pl.BlockSpec((1, tr, c), lambda i: (0, i, 0))
    return pl.pallas_call(
        body, name=name, grid=(r // tr,),
        in_specs=[pl.BlockSpec((N_DEV, tr, c), lambda i: (0, i, 0)), row(), row(), row()],
        out_specs=[row(), row(), row(), row()], out_shape=[_sds((1, r, c))] * 4,
        compiler_params=_cp("parallel"),
    )(parts, w, m, v)


def _adam_small(gs, ws, ms, vs, name):
    n = len(gs)

    def body(*refs):
        ins, outs = refs[:4 * n], refs[4 * n:]
        for i in range(n):
            g = ins[i][...]
            d, m2, v2 = _adamw(ins[n + i][...], g, ins[2 * n + i][...], ins[3 * n + i][...])
            outs[i][...] = d
            outs[n + i][...] = m2
            outs[2 * n + i][...] = v2

    res = pl.pallas_call(
        body, name=name, out_shape=[_sds(w.shape) for w in ws] * 3,
        compiler_params=pltpu.CompilerParams(vmem_limit_bytes=VMEM_LIMIT),
    )(*gs, *ws, *ms, *vs)
    return res[:n], res[n:2 * n], res[2 * n:]


def _sum_slots(parts, name):
    R = parts.shape[1]

    def body(p_ref, o_ref):
        g = p_ref[0]
        for s in range(1, N_DEV):
            g = g + p_ref[s]
        o_ref[...] = g

    return pl.pallas_call(body, name=name, out_shape=_sds((R, LANES)))(parts)


def _pad_to(a, n, axis):
    extra = n - a.shape[axis]
    if extra == 0:
        return a
    widths = [(0, 0)] * a.ndim
    widths[axis] = (0, extra)
    return jnp.pad(a, widths)


def _ceil_to(n, k):
    return -(-n // k) * k


def _pack_rows(flats, rows_multiple):
    parts = [_pad_to(f, _ceil_to(f.shape[-1], LANES), f.ndim - 1) for f in flats]
    cat = jnp.concatenate(parts, axis=-1)
    total = _ceil_to(cat.shape[-1], LANES * rows_multiple)
    cat = _pad_to(cat, total, cat.ndim - 1)
    return cat.reshape(cat.shape[:-1] + (total // LANES, LANES))


def _unpack_rows(buf, sizes):
    flat = buf.reshape(buf.shape[:-2] + (-1,))
    out, off = [], 0
    for n in sizes:
        out.append(flat[..., off:off + n])
        off += _ceil_to(n, LANES)
    return out


_MIX_BIG = ("w_in", "w_glu", "w_proj_a", "w_proj_b", "w_out")
_BIG = _MIX_BIG + ("w_up", "w_down")
_SMALL = ("g_mix", "a_re", "a_im", "log_dt", "b_re", "b_im", "c_re", "c_im", "d_skip", "b_glu", "g_sgu", "w_s", "b_s",
          "g_ffn", "conv_b", "g_final")
_SMALL_ROWS_MULTIPLE = 8 * N_DEV


def _as_2d(a):
    return a.reshape(-1, a.shape[-1]) if a.ndim > 1 else a.reshape(1, -1)


def kernel(x, g_mix, w_in, a_re, a_im, log_dt, b_re, b_im, c_re, c_im, d_skip, w_glu, b_glu, w_proj_a, g_sgu, w_s, b_s, w_proj_b, w_out, g_ffn, w_up, conv_w, conv_b, w_down, g_final, loss_target, m_g_mix, m_w_in, m_a_re, m_a_im, m_log_dt, m_b_re, m_b_im, m_c_re, m_c_im, m_d_skip, m_w_glu, m_b_glu, m_w_proj_a, m_g_sgu, m_w_s, m_b_s, m_w_proj_b, m_w_out, m_g_ffn, m_w_up, m_conv_w, m_conv_b, m_w_down, m_g_final, v_g_mix, v_w_in, v_a_re, v_a_im, v_log_dt, v_b_re, v_b_im, v_c_re, v_c_im, v_d_skip, v_w_glu, v_b_glu, v_w_proj_a, v_g_sgu, v_w_s, v_b_s, v_w_proj_b, v_w_out, v_g_ffn, v_w_up, v_conv_w, v_conv_b, v_w_down, v_g_final):
    args = dict(locals())
    me = 4 * lax.axis_index("x") + 2 * lax.axis_index("y") + lax.axis_index("c")

    def own_slot(buf, block):
        return lax.dynamic_update_slice(buf, block[None], (me,) + (0,) * block.ndim)

    gathered, (up_sh, down_sh) = _allgather([args[n][0] for n in _MIX_BIG], [MXU] * len(_MIX_BIG), "allgather_mixer",
                                            cast_only=(w_up[0], w_down[0]))
    g = dict(zip(_MIX_BIG, gathered))
    ffn_srcs = [up_sh, down_sh, conv_w[0]]
    ffn_lands = [own_slot(lax.empty((N_DEV,) + s.shape, s.dtype), s) for s in ffn_srcs]
    ag_send, ag_recv, ffn_srcs, ffn_lands, ag_token = _push_start(ffn_srcs, ffn_lands, False, "push_ffn_weights")
    w_in_full, w_pa_full, w_pb_full = _assemble_cols([g["w_in"], g["w_proj_a"], g["w_proj_b"]], "assemble_cols")
    p = {n: (args[n][0] if n != "g_final" else args[n]) for n in _SMALL}
    p.update(w_in=w_in_full, w_proj_a=w_pa_full, w_proj_b=w_pb_full,
             w_glu=g["w_glu"].reshape(SSM_W, SSM_W), w_out=g["w_out"].reshape(D_MODEL, D_MODEL))
    p["g_mix"] = p["g_mix"] + ag_token[0:1, 0:1]

    def ffn_weights(after):
        w_up_g, w_down_g, conv_w_g = _push_wait(ag_send, ag_recv, ffn_srcs, ffn_lands, False, [after], "wait_ffn_weights")
        return w_up_g, conv_w_g, w_down_g.reshape(D_FF, D_MODEL)

    pushes = []

    def grads_out(names, sends):
        lands = [own_slot(lax.empty(s.shape, s.dtype), lax.dynamic_index_in_dim(s, me, 0, keepdims=False))
                 for s in sends]
        send_sems, recv_sems, srcs, lands, token = _push_start(list(sends), lands, True, "push_grads_" + names[0])
        pushes.append((names, send_sems, recv_sems, srcs, lands))
        return token

    loss_part, grad_x, grads = _local_step(x[0], loss_target[0], p, ffn_weights, grads_out)

    small_names = _SMALL + ("conv_w", "loss")
    small_g = dict(grads, loss=loss_part[0, 0:1])
    flats = [small_g[n].reshape(-1) for n in small_names]
    small_sizes = [f.shape[0] for f in flats]
    g_small = _pack_rows(flats, _SMALL_ROWS_MULTIPLE)
    rs8 = g_small.shape[0] // N_DEV
    recv_small, = _all_to_all([g_small.reshape(N_DEV, rs8, LANES)], "all_to_all_small")
    small_mine = _sum_slots(recv_small, "sum_small")
    g_small_all = _allgather([small_mine], [F32], "allgather_small")[0][0].reshape(N_DEV * rs8, LANES)
    pieces = dict(zip(small_names, _unpack_rows(g_small_all, small_sizes)))
    loss = pieces["loss"][0]
    dconv_w = lax.dynamic_index_in_dim(pieces["conv_w"].reshape(N_DEV, 3, FF_CW), me, axis=0, keepdims=False)

    out = {}
    done = [g_small_all]
    for names, send_sems, recv_sems, srcs, lands in pushes:
        parts = _push_wait(send_sems, recv_sems, srcs, lands, True, done, "wait_grads_" + names[0])
        for n, part in zip(names, parts):
            res = _adam_shard(part, args[n], args["m_" + n], args["v_" + n], "adam_" + n)
            for kind, v in zip(("grad_", "delta_", "new_m_", "new_v_"), res):
                out[kind + n] = v
            done = [res[0]]
    names2 = _SMALL + ("conv_w",)
    gs = [pieces[n].reshape(_as_2d(args[n]).shape) for n in _SMALL] + [dconv_w]
    ds, m2s, v2s = _adam_small(gs, [_as_2d(args[n]) for n in names2], [_as_2d(args["m_" + n]) for n in names2],
                               [_as_2d(args["v_" + n]) for n in names2], "adam_small")
    for n, res in zip(names2, zip(gs, ds, m2s, v2s)):
        for kind, v in zip(("grad_", "delta_", "new_m_", "new_v_"), res):
            out[kind + n] = v.reshape(args[n].shape)
    order = ("g_mix", "w_in", "a_re", "a_im", "log_dt", "b_re", "b_im", "c_re", "c_im", "d_skip", "w_glu", "b_glu",
             "w_proj_a", "g_sgu", "w_s", "b_s", "w_proj_b", "w_out", "g_ffn", "w_up", "conv_w", "conv_b", "w_down",
             "g_final")
    res = [loss, grad_x.reshape(x.shape)]
    for kind in ("grad_", "delta_", "new_m_", "new_v_"):
        res += [out[kind + n] for n in order]
    return tuple(res)
```

```python
import functools
import math

import jax
import jax.numpy as jnp
from jax import lax
from jax.experimental import pallas as pl
from jax.experimental.pallas import tpu as pltpu

F32 = jnp.float32
MXU = jnp.bfloat16
EPS = 1e-6

D_MODEL = 1024
SSM_W = 512
SSM_G, SSM_H, SSM_P = 32, 16, 64
SSM_BLK = 4
SGU_W = 512
SGU_G, SGU_D, CHUNK = 8, 64, 128
D_FF = 2816
N_DEV = 8
FF_CW = 2 * D_FF // N_DEV
FF_NCB = D_FF // FF_CW
LANES = 128

ADAM_LR, ADAM_B1, ADAM_B2, ADAM_EPS, ADAM_WD, ADAM_STEP = 0.001, 0.9, 0.999, 1e-08, 0.01, 10

VMEM_LIMIT = 48 * 1024 * 1024


def _cp(*sem):
    return pltpu.CompilerParams(dimension_semantics=sem, vmem_limit_bytes=VMEM_LIMIT)


def _full(shape):
    n = len(shape)
    return pl.BlockSpec(shape, lambda *_: (0,) * n)


def _sds(shape, dtype=F32):
    return jax.ShapeDtypeStruct(shape, dtype)


def _dot(a, b):
    return jnp.dot(a, b, preferred_element_type=F32)


def _dot_nt(a, b):
    return lax.dot_general(a, b, (((1,), (1,)), ((), ())), preferred_element_type=F32)


def _dot_tn(a, b):
    return lax.dot_general(a, b, (((0,), (0,)), ((), ())), preferred_element_type=F32)


_GELU_C = math.sqrt(2.0 / math.pi)


def _gelu(x):
    return 0.5 * x * (1.0 + jnp.tanh(_GELU_C * (x + 0.044715 * (x * x * x))))


def _gelu_and_grad(x):
    t = jnp.tanh(_GELU_C * (x + 0.044715 * (x * x * x)))
    g = 0.5 * x * (1.0 + t)
    dg = 0.5 * (1.0 + t) + 0.5 * x * (1.0 - t * t) * (_GELU_C * (1.0 + 3.0 * 0.044715 * (x * x)))
    return g, dg


def _sigmoid(x):
    return 1.0 / (1.0 + jnp.exp(-x))


def _rms(x):
    return lax.rsqrt(jnp.mean(x * x, axis=-1, keepdims=True) + EPS)


def _rms_bwd(dxn, xn, r):
    return r * (dxn - xn * jnp.mean(dxn * xn, axis=-1, keepdims=True))


def _rowsum(x):
    return jnp.sum(x, axis=0, keepdims=True)


def _s5_disc(are, aim, ldt, br, bi):
    dt = jnp.exp(ldt)
    mag = jnp.exp(dt * are)
    abr = mag * jnp.cos(dt * aim)
    abi = mag * jnp.sin(dt * aim)
    den = are * are + aim * aim
    nr = abr - 1.0
    ni = abi
    fr = (nr * are + ni * aim) / den
    fi = (ni * are - nr * aim) / den
    return abr, abi, fr * br - fi * bi, fr * bi + fi * br


def _s5_params_fwd(are, aim, ldt, br, bi):
    def body(are_ref, aim_ref, ldt_ref, br_ref, bi_ref, o0, o1, o2, o3):
        outs = _s5_disc(are_ref[...], aim_ref[...], ldt_ref[...], br_ref[...], bi_ref[...])
        for o, v in zip((o0, o1, o2, o3), outs):
            o[...] = v
    shp = are.shape
    return pl.pallas_call(body, name="s5_params_fwd", out_shape=[_sds(shp)] * 4)(are, aim, ldt, br, bi)


def _s5_params_bwd(are, aim, ldt, br, bi, dabr, dabi, dbr, dbi):
    def body(are_ref, aim_ref, ldt_ref, br_ref, bi_ref, c0, c1, c2, c3, o0, o1, o2, o3, o4):
        prim = (are_ref[...], aim_ref[...], ldt_ref[...], br_ref[...], bi_ref[...])
        _, vjp = jax.vjp(_s5_disc, *prim)
        outs = vjp((c0[...], c1[...], c2[...], c3[...]))
        for o, v in zip((o0, o1, o2, o3, o4), outs):
            o[...] = v
    shp = are.shape
    return pl.pallas_call(body, name="s5_params_bwd", out_shape=[_sds(shp)] * 5)(
        are, aim, ldt, br, bi, dabr, dabi, dbr, dbi)


def _blockdiag(m_t):
    m = m_t.reshape(SSM_BLK, 8, SSM_H, 1, SSM_P)
    eye = jnp.eye(8, dtype=bool).reshape(1, 8, 1, 8, 1)
    return jnp.where(eye, m, jnp.zeros((), m_t.dtype)).reshape(SSM_BLK, 8 * SSM_H, 8 * SSM_P)


def _unblockdiag(pc):
    m = pc.reshape(SSM_BLK, 8, SSM_H, 8, SSM_P)
    return jnp.einsum("jghgp->jghp", m).reshape(SSM_G * SSM_H, SSM_P)


def _in_fwd(x, g_mix, w_in_t, tm):
    S = x.shape[0]

    def body(x_ref, g_ref, w_ref, h_ref, us_ref, uv_ref, gl_ref):
        xv = x_ref[...]
        h = (xv * _rms(xv) * g_ref[...]).astype(MXU)
        h_ref[...] = h
        us_ref[...] = _dot_nt(h, w_ref[0:SSM_W, :])
        uv_ref[...] = _dot_nt(h, w_ref[SSM_W:SSM_W + 2 * SGU_W, :])
        gl_ref[...] = _dot_nt(h, w_ref[SSM_W + 2 * SGU_W:, :])

    row = lambda n: pl.BlockSpec((tm, n), lambda i: (i, 0))
    return pl.pallas_call(
        body, name="in_fwd", grid=(S // tm,),
        in_specs=[row(D_MODEL), _full((1, D_MODEL)), _full(w_in_t.shape)],
        out_specs=[row(D_MODEL), row(SSM_W), row(2 * SGU_W), row(2 * D_MODEL)],
        out_shape=[_sds((S, D_MODEL), MXU), _sds((S, SSM_W)), _sds((S, 2 * SGU_W)), _sds((S, 2 * D_MODEL))],
        compiler_params=_cp("parallel"),
    )(x, g_mix, w_in_t)


def _scan_tables(ar, ai, reverse):
    n = ar.shape[-1]
    def mul(p, q):
        return p[0] * q[0] - p[1] * q[1], p[0] * q[1] + p[1] * q[0]
    a1 = (ar, ai)
    a2 = mul(a1, a1)
    a3 = mul(a2, a1)
    a4 = mul(a2, a2)
    a5 = mul(a4, a1)
    a6 = mul(a4, a2)
    a7 = mul(a4, a3)
    a8 = mul(a4, a4)
    pw = (a1, a2, a3, a4, a5, a6, a7, a8)
    rows = lax.broadcasted_iota(jnp.int32, (8, n), 0)
    tabs = []
    for s, a in ((1, a1), (2, a2), (4, a4)):
        keep = (rows + s <= 7) if reverse else (rows >= s)
        for comp in a:
            tabs.append(jnp.where(keep, jnp.broadcast_to(comp, (8, n)), 0.0))
    for c in range(2):
        q = jnp.zeros((8, n), F32)
        for r in range(8):
            e = (8 - r) if reverse else (r + 1)
            q = jnp.where(rows == r, jnp.broadcast_to(pw[e - 1][c], (8, n)), q)
        tabs.append(q)
    return tabs


def _scan_group(xr, xi, tab_ref, cr, ci, reverse):
    for t, s in enumerate((1, 2, 4)):
        pr = tab_ref[2 * t]
        pi = tab_ref[2 * t + 1]
        sh = (8 - s) if reverse else s
        sr = pltpu.roll(xr, sh, 0)
        si = pltpu.roll(xi, sh, 0)
        xr, xi = xr + pr * sr - pi * si, xi + pr * si + pi * sr
    qr = tab_ref[6]
    qi = tab_ref[7]
    return xr + qr * cr - qi * ci, xi + qr * ci + qi * cr


def _s5_fwd(us, abar_re, abar_im, b_re, b_im, c_re, c_im, d_skip, tm):
    S = us.shape[0]
    nt = S // tm
    w = 8 * SSM_P

    def body(us_ref, ar_ref, ai_ref, br_ref, bi_ref, cr_ref, ci_ref, d_ref, str_ref, sti_ref, ys_ref, tab_ref, car_ref):
        i = pl.program_id(1)

        @pl.when(i == 0)
        def _():
            car_ref[...] = jnp.zeros_like(car_ref)
            for k, t in enumerate(_scan_tables(ar_ref[...], ai_ref[...], False)):
                tab_ref[k] = t

        u = us_ref[...]
        ub = u.astype(MXU)
        str_ref[...] = _dot(ub, br_ref[0])
        sti_ref[...] = _dot(ub, bi_ref[0])

        def grp(k, carry):
            r0 = pl.multiple_of(k * 8, 8)
            xr, xi = _scan_group(str_ref[pl.ds(r0, 8), :], sti_ref[pl.ds(r0, 8), :], tab_ref, carry[0], carry[1], False)
            str_ref[pl.ds(r0, 8), :] = xr
            sti_ref[pl.ds(r0, 8), :] = xi
            return xr[7:8, :], xi[7:8, :]

        cr, ci = lax.fori_loop(0, tm // 8, grp, (car_ref[0:1, :], car_ref[1:2, :]))
        car_ref[0:1, :] = cr
        car_ref[1:2, :] = ci
        y = _dot_nt(str_ref[...].astype(MXU), cr_ref[0]) - _dot_nt(sti_ref[...].astype(MXU), ci_ref[0])
        ys_ref[...] = y + d_ref[...] * u

    blk = lambda: pl.BlockSpec((1, 8 * SSM_H, w), lambda j, i: (j, 0, 0))
    return pl.pallas_call(
        body, name="s5_fwd", grid=(SSM_BLK, nt),
        in_specs=[pl.BlockSpec((tm, LANES), lambda j, i: (i, j)),
                  pl.BlockSpec((1, w), lambda j, i: (0, j)), pl.BlockSpec((1, w), lambda j, i: (0, j)),
                  blk(), blk(), blk(), blk(),
                  pl.BlockSpec((1, LANES), lambda j, i: (0, j))],
        out_specs=[pl.BlockSpec((tm, w), lambda j, i: (i, j)), pl.BlockSpec((tm, w), lambda j, i: (i, j)),
                   pl.BlockSpec((tm, LANES), lambda j, i: (i, j))],
        out_shape=[_sds((S, SSM_BLK * w)), _sds((S, SSM_BLK * w)), _sds((S, SSM_W))],
        scratch_shapes=[pltpu.VMEM((8, 8, w), F32), pltpu.VMEM((8, w), F32)],
        compiler_params=_cp("parallel", "arbitrary"),
    )(us, abar_re, abar_im, b_re, b_im, c_re, c_im, d_skip)


def _sgu_mix(vnb, ws_ref, grp):
    acc = jnp.zeros(vnb.shape, F32)
    for g in range(SGU_G):
        acc = jnp.where(grp == g, _dot(ws_ref[g], vnb), acc)
    return acc


def _mix_fwd(x, ys, uv, gl, w_glu, b_glu, w_pa, g_sgu, ws, bias_s, w_pb, w_out, g_ffn, tm):
    S = x.shape[0]

    def body(x_ref, ys_ref, uv_ref, gl_ref, wglu_ref, bglu_ref, wpa_ref, gs_ref, ws_ref, bias_ref, wpb_ref, wout_ref,
             gf_ref, yg_ref, yap_ref, sg_ref, ya_ref, yb_ref, m_ref, x1_ref, h2_ref):
        yg = _gelu(ys_ref[...])
        ygb = yg.astype(MXU)
        yg_ref[...] = ygb
        z = _dot(ygb, wglu_ref[...]) + bglu_ref[...]
        yapb = (yg * _sigmoid(z)).astype(MXU)
        yap_ref[...] = yapb
        ya = _dot(yapb, wpa_ref[...])
        ya_ref[...] = ya

        uvg = _gelu(uv_ref[...])
        u2 = uvg[:, :SGU_W]
        v2 = uvg[:, SGU_W:]
        vnb = (v2 * _rms(v2) * gs_ref[...]).astype(MXU)
        grp = lax.broadcasted_iota(jnp.int32, (CHUNK, SGU_W), 1) // SGU_D
        for c in range(tm // CHUNK):
            rs = slice(c * CHUNK, (c + 1) * CHUNK)
            mixed = _sgu_mix(vnb[rs], ws_ref, grp) + bias_ref[...]
            sg_ref[rs, :] = (u2[rs] * mixed).astype(MXU)
        yb = _dot(sg_ref[...], wpb_ref[...])
        yb_ref[...] = yb

        glv = gl_ref[...]
        m = _sigmoid(glv[:, :D_MODEL]) * ya + _sigmoid(glv[:, D_MODEL:]) * yb
        mb = m.astype(MXU)
        m_ref[...] = mb
        x1 = x_ref[...] + _dot(mb, wout_ref[...])
        x1_ref[...] = x1
        h2_ref[...] = (x1 * _rms(x1) * gf_ref[...]).astype(MXU)

    row = lambda n: pl.BlockSpec((tm, n), lambda i: (i, 0))
    return pl.pallas_call(
        body, name="mix_fwd", grid=(S // tm,),
        in_specs=[row(D_MODEL), row(SSM_W), row(2 * SGU_W), row(2 * D_MODEL),
                  _full(w_glu.shape), _full(b_glu.shape), _full(w_pa.shape), _full(g_sgu.shape), _full(ws.shape),
                  _full(bias_s.shape), _full(w_pb.shape), _full(w_out.shape), _full(g_ffn.shape)],
        out_specs=[row(SSM_W), row(SSM_W), row(SGU_W), row(D_MODEL), row(D_MODEL), row(D_MODEL), row(D_MODEL),
                   row(D_MODEL)],
        out_shape=[_sds((S, SSM_W), MXU), _sds((S, SSM_W), MXU), _sds((S, SGU_W), MXU), _sds((S, D_MODEL)),
                   _sds((S, D_MODEL)), _sds((S, D_MODEL), MXU), _sds((S, D_MODEL)), _sds((S, D_MODEL), MXU)],
        compiler_params=_cp("parallel"),
    )(x, ys, uv, gl, w_glu, b_glu, w_pa, g_sgu, ws, bias_s, w_pb, w_out, g_ffn)


def _conv_taps(u, prev8, rows):
    t1 = prev8[7:8, :]
    t0 = prev8[6:7, :]
    s1 = jnp.where(rows == 0, t1, pltpu.roll(u, 1, 0))
    s2 = jnp.where(rows == 0, t0, jnp.where(rows == 1, t1, pltpu.roll(u, 2, 0)))
    return s1, s2


def _ffn_fwd(h2, x1, tgt, w_up, conv_w, conv_b, w_down, g_final, tm):
    S = h2.shape[0]
    nt = S // tm
    ncb = FF_NCB

    def body(h2_ref, wa_ref, wb_ref, cwa_ref, cwb_ref, cba_ref, cbb_ref, wd_ref, x1_ref, gf_ref, tgt_ref,
             up_ref, ff_ref, dx2_ref, loss_ref, dgf_ref, acc_ref, tail_ref):
        i = pl.program_id(0)
        cb = pl.program_id(1)

        @pl.when(i == 0)
        def _():
            tail_ref[cb] = jnp.zeros((2, 8, FF_CW), F32)

        @pl.when(jnp.logical_and(i == 0, cb == 0))
        def _():
            loss_ref[...] = jnp.zeros_like(loss_ref)
            dgf_ref[...] = jnp.zeros_like(dgf_ref)

        h2v = h2_ref[...]
        ua = _dot_nt(h2v, wa_ref[0])
        ub = _dot_nt(h2v, wb_ref[0])
        up_ref[0, 0] = ua
        up_ref[1, 0] = ub
        rows = lax.broadcasted_iota(jnp.int32, (tm, FF_CW), 0)
        s1a, s2a = _conv_taps(ua, tail_ref[cb, 0], rows)
        s1b, s2b = _conv_taps(ub, tail_ref[cb, 1], rows)
        tail_ref[cb, 0] = ua[tm - 8:tm, :]
        tail_ref[cb, 1] = ub[tm - 8:tm, :]
        cwa = cwa_ref[0]
        cwb = cwb_ref[0]
        a = cwa[0:1] * s2a + cwa[1:2] * s1a + cwa[2:3] * ua + cba_ref[0]
        b = cwb[0:1] * s2b + cwb[1:2] * s1b + cwb[2:3] * ub + cbb_ref[0]
        ffb = (a * _sigmoid(a) * b).astype(MXU)
        ff_ref[0] = ffb
        contrib = _dot(ffb, wd_ref[...])

        @pl.when(cb == 0)
        def _():
            acc_ref[...] = contrib

        @pl.when(cb > 0)
        def _():
            acc_ref[...] += contrib

        @pl.when(cb == ncb - 1)
        def _():
            x2 = x1_ref[...] + acc_ref[...]
            r = _rms(x2)
            xn = x2 * r
            g = gf_ref[...]
            diff = xn * g - tgt_ref[...]
            loss_ref[...] += (0.5 / D_MODEL) * jnp.sum(diff * diff)
            dy = diff * (1.0 / D_MODEL)
            dgf_ref[...] += _rowsum(dy * xn)
            dx2_ref[...] = _rms_bwd(dy * g, xn, r)

    row = lambda n: pl.BlockSpec((tm, n), lambda i, c: (i, 0))
    gate = lambda r: pl.BlockSpec((1, r, FF_CW), lambda i, c: (c, 0, 0))
    lin = lambda r: pl.BlockSpec((1, r, FF_CW), lambda i, c: (ncb + c, 0, 0))
    return pl.pallas_call(
        body, name="ffn_fwd", grid=(nt, ncb),
        in_specs=[row(D_MODEL),
                  pl.BlockSpec((1, FF_CW, D_MODEL), lambda i, c: (c, 0, 0)),
                  pl.BlockSpec((1, FF_CW, D_MODEL), lambda i, c: (ncb + c, 0, 0)),
                  gate(3), lin(3), gate(1), lin(1),
                  pl.BlockSpec((FF_CW, D_MODEL), lambda i, c: (c, 0)),
                  row(D_MODEL), _full((1, D_MODEL)), row(D_MODEL)],
        out_specs=[pl.BlockSpec((2, 1, tm, FF_CW), lambda i, c: (0, c, i, 0)),
                   pl.BlockSpec((1, tm, FF_CW), lambda i, c: (c, i, 0)),
                   row(D_MODEL), _full((1, LANES)), _full((1, D_MODEL))],
        out_shape=[_sds((2, ncb, S, FF_CW)), _sds((ncb, S, FF_CW), MXU), _sds((S, D_MODEL)),
                   _sds((1, LANES)), _sds((1, D_MODEL))],
        scratch_shapes=[pltpu.VMEM((tm, D_MODEL), F32), pltpu.VMEM((ncb, 2, 8, FF_CW), F32)],
        compiler_params=_cp("arbitrary", "arbitrary"),
    )(h2, w_up, w_up, conv_w, conv_w, conv_b, conv_b, w_down, x1, g_final, tgt)


def _ffn_bwd(dx2, up, x1, w_up, conv_w, conv_b, w_down, g_ffn, tm):
    S = dx2.shape[0]
    nt = S // tm
    ncb = FF_NCB
    hb = tm // 8

    def body(dx2_ref, up_ref, hp_ref, cwa_ref, cwb_ref, cba_ref, cbb_ref, wd_ref, wa_ref, wb_ref,
             x1_ref, g_ref, dup_ref, dx1_ref, dconv_ref, dg_ref, acc_ref, head_ref):
        i = pl.program_id(0)
        cb = pl.program_id(1)
        ri = nt - 1 - i

        @pl.when(i == 0)
        def _():
            head_ref[cb] = jnp.zeros((2, 8, FF_CW), F32)
            dconv_ref[cb] = jnp.zeros((8, FF_CW), F32)
            dconv_ref[ncb + cb] = jnp.zeros((8, FF_CW), F32)

        @pl.when(jnp.logical_and(i == 0, cb == 0))
        def _():
            dg_ref[...] = jnp.zeros_like(dg_ref)

        dx2v = dx2_ref[...]
        dff = _dot_nt(dx2v.astype(MXU), wd_ref[...])
        ua = up_ref[0, 0]
        ub = up_ref[1, 0]
        rows = lax.broadcasted_iota(jnp.int32, (tm, FF_CW), 0)
        first = ri == 0
        s1a, s2a = _conv_taps(ua, jnp.where(first, 0.0, hp_ref[0, 0]), rows)
        s1b, s2b = _conv_taps(ub, jnp.where(first, 0.0, hp_ref[1, 0]), rows)
        cwa = cwa_ref[0]
        cwb = cwb_ref[0]
        a = cwa[0:1] * s2a + cwa[1:2] * s1a + cwa[2:3] * ua + cba_ref[0]
        b = cwb[0:1] * s2b + cwb[1:2] * s1b + cwb[2:3] * ub + cbb_ref[0]
        sa = _sigmoid(a)
        da = dff * b * (sa * (1.0 + a * (1.0 - sa)))
        db = dff * (a * sa)

        def conv_bwd(dup, head8, cw):
            h0 = head8[0:1, :]
            h1 = head8[1:2, :]
            n1 = jnp.where(rows == tm - 1, h0, pltpu.roll(dup, tm - 1, 0))
            n2 = jnp.where(rows == tm - 2, h0, jnp.where(rows == tm - 1, h1, pltpu.roll(dup, tm - 2, 0)))
            return cw[2:3] * dup + cw[1:2] * n1 + cw[0:1] * n2

        dpa = conv_bwd(da, head_ref[cb, 0], cwa).astype(MXU)
        dpb = conv_bwd(db, head_ref[cb, 1], cwb).astype(MXU)
        head_ref[cb, 0] = da[0:8, :]
        head_ref[cb, 1] = db[0:8, :]
        dup_ref[0, 0] = dpa
        dup_ref[1, 0] = dpb
        for slot, dup, s2, s1, u in ((cb, da, s2a, s1a, ua), (ncb + cb, db, s2b, s1b, ub)):
            dconv_ref[slot, 0:1, :] += _rowsum(dup * s2)
            dconv_ref[slot, 1:2, :] += _rowsum(dup * s1)
            dconv_ref[slot, 2:3, :] += _rowsum(dup * u)
            dconv_ref[slot, 3:4, :] += _rowsum(dup)
        contrib = _dot(dpa, wa_ref[0]) + _dot(dpb, wb_ref[0])

        @pl.when(cb == 0)
        def _():
            acc_ref[...] = contrib

        @pl.when(cb > 0)
        def _():
            acc_ref[...] += contrib

        @pl.when(cb == ncb - 1)
        def _():
            x1v = x1_ref[...]
            r = _rms(x1v)
            xn = x1v * r
            dh2 = acc_ref[...]
            dg_ref[...] += _rowsum(dh2 * xn)
            dx1_ref[...] = dx2v + _rms_bwd(dh2 * g_ref[...], xn, r)

    row = lambda n: pl.BlockSpec((tm, n), lambda i, c: (nt - 1 - i, 0))
    colb = lambda: pl.BlockSpec((2, 1, tm, FF_CW), lambda i, c: (0, c, nt - 1 - i, 0))
    halo = lambda: pl.BlockSpec((2, 1, 8, FF_CW), lambda i, c: (0, c, jnp.maximum((nt - 1 - i) * hb - 1, 0), 0))
    gate = lambda r: pl.BlockSpec((1, r, FF_CW), lambda i, c: (c, 0, 0))
    lin = lambda r: pl.BlockSpec((1, r, FF_CW), lambda i, c: (ncb + c, 0, 0))
    return pl.pallas_call(
        body, name="ffn_bwd", grid=(nt, ncb),
        in_specs=[row(D_MODEL), colb(), halo(), gate(3), lin(3), gate(1), lin(1),
                  pl.BlockSpec((FF_CW, D_MODEL), lambda i, c: (c, 0)),
                  pl.BlockSpec((1, FF_CW, D_MODEL), lambda i, c: (c, 0, 0)),
                  pl.BlockSpec((1, FF_CW, D_MODEL), lambda i, c: (ncb + c, 0, 0)),
                  row(D_MODEL), _full((1, D_MODEL))],
        out_specs=[colb(), row(D_MODEL), _full((2 * ncb, 8, FF_CW)), _full((1, D_MODEL))],
        out_shape=[_sds((2, ncb, S, FF_CW), MXU), _sds((S, D_MODEL)), _sds((2 * ncb, 8, FF_CW)), _sds((1, D_MODEL))],
        scratch_shapes=[pltpu.VMEM((tm, D_MODEL), F32), pltpu.VMEM((ncb, 2, 8, FF_CW), F32)],
        compiler_params=_cp("arbitrary", "arbitrary"),
    )(dx2, up, up, conv_w, conv_w, conv_b, conv_b, w_down, w_up, w_up, x1, g_ffn)


def _mix_bwd(dx1, gl, ya, yb, ys, uv, w_out, w_pa, w_pb, w_glu, b_glu, g_sgu, ws, ws_t, bias_s, tm):
    S = dx1.shape[0]

    def body(dx1_ref, gl_ref, ya_ref, yb_ref, ys_ref, uv_ref, wout_ref, wpa_ref, wpb_ref, wglu_ref, bglu_ref, gs_ref,
             ws_ref, wst_ref, bias_ref,
             dgl_ref, dya_ref, dyb_ref, dz_ref, dys_ref, duv_ref, dbglu_ref, dgs_ref, dws_ref, dbs_ref,
             du2_ref, dvn_ref):
        i = pl.program_id(0)

        @pl.when(i == 0)
        def _():
            dbglu_ref[...] = jnp.zeros_like(dbglu_ref)
            dgs_ref[...] = jnp.zeros_like(dgs_ref)
            dws_ref[...] = jnp.zeros_like(dws_ref)
            dbs_ref[...] = jnp.zeros_like(dbs_ref)

        dm = _dot_nt(dx1_ref[...].astype(MXU), wout_ref[...])
        glv = gl_ref[...]
        ga = _sigmoid(glv[:, :D_MODEL])
        gb = _sigmoid(glv[:, D_MODEL:])
        dgl_ref[:, :D_MODEL] = (dm * ya_ref[...] * ga * (1.0 - ga)).astype(MXU)
        dgl_ref[:, D_MODEL:] = (dm * yb_ref[...] * gb * (1.0 - gb)).astype(MXU)
        dyab = (dm * ga).astype(MXU)
        dybb = (dm * gb).astype(MXU)
        dya_ref[...] = dyab
        dyb_ref[...] = dybb

        dyap = _dot_nt(dyab, wpa_ref[...])
        yg, dgelu = _gelu_and_grad(ys_ref[...])
        sz = _sigmoid(_dot(yg.astype(MXU), wglu_ref[...]) + bglu_ref[...])
        dz = dyap * yg * sz * (1.0 - sz)
        dzb = dz.astype(MXU)
        dz_ref[...] = dzb
        dbglu_ref[...] += _rowsum(dz)
        dys_ref[...] = (dyap * sz + _dot_nt(dzb, wglu_ref[...])) * dgelu

        dsg = _dot_nt(dybb, wpb_ref[...])
        uvg, duvg = _gelu_and_grad(uv_ref[...])
        u2 = uvg[:, :SGU_W]
        v2 = uvg[:, SGU_W:]
        rv = _rms(v2)
        vhat = v2 * rv
        gs = gs_ref[...]
        vnb = (vhat * gs).astype(MXU)
        grp = lax.broadcasted_iota(jnp.int32, (CHUNK, SGU_W), 1) // SGU_D
        tril = (lax.broadcasted_iota(jnp.int32, (CHUNK, CHUNK), 0)
                >= lax.broadcasted_iota(jnp.int32, (CHUNK, CHUNK), 1))
        for c in range(tm // CHUNK):
            rs = slice(c * CHUNK, (c + 1) * CHUNK)
            vc = vnb[rs]
            mixed = _sgu_mix(vc, ws_ref, grp) + bias_ref[...]
            dsg_c = dsg[rs]
            du2_ref[rs, :] = dsg_c * mixed
            dmx = dsg_c * u2[rs]
            dbs_ref[...] += dmx
            dmb = dmx.astype(MXU)
            dvn_ref[rs, :] = _sgu_mix(dmb, wst_ref, grp)
            for g in range(SGU_G):
                part = _dot_nt(jnp.where(grp == g, dmb, jnp.zeros((), MXU)), vc)
                dws_ref[g] += jnp.where(tril, part, 0.0)
        dvn = dvn_ref[...]
        dgs_ref[...] += _rowsum(dvn * vhat)
        dv2 = _rms_bwd(dvn * gs, vhat, rv)
        duv_ref[:, :SGU_W] = (du2_ref[...] * duvg[:, :SGU_W]).astype(MXU)
        duv_ref[:, SGU_W:] = (dv2 * duvg[:, SGU_W:]).astype(MXU)

    row = lambda n: pl.BlockSpec((tm, n), lambda i: (i, 0))
    return pl.pallas_call(
        body, name="mix_bwd", grid=(S // tm,),
        in_specs=[row(D_MODEL), row(2 * D_MODEL), row(D_MODEL), row(D_MODEL), row(SSM_W), row(2 * SGU_W),
                  _full(w_out.shape), _full(w_pa.shape), _full(w_pb.shape), _full(w_glu.shape), _full(b_glu.shape),
                  _full(g_sgu.shape), _full(ws.shape), _full(ws_t.shape), _full(bias_s.shape)],
        out_specs=[row(2 * D_MODEL), row(D_MODEL), row(D_MODEL), row(SSM_W), row(SSM_W), row(2 * SGU_W),
                   _full((1, SSM_W)), _full((1, SGU_W)), _full((SGU_G, CHUNK, CHUNK)), _full((CHUNK, SGU_W))],
        out_shape=[_sds((S, 2 * D_MODEL), MXU), _sds((S, D_MODEL), MXU), _sds((S, D_MODEL), MXU), _sds((S, SSM_W), MXU),
                   _sds((S, SSM_W)), _sds((S, 2 * SGU_W), MXU),
                   _sds((1, SSM_W)), _sds((1, SGU_W)), _sds((SGU_G, CHUNK, CHUNK)), _sds((CHUNK, SGU_W))],
        scratch_shapes=[pltpu.VMEM((tm, SGU_W), F32), pltpu.VMEM((tm, SGU_W), F32)],
        compiler_params=_cp("arbitrary"),
    )(dx1, gl, ya, yb, ys, uv, w_out, w_pa, w_pb, w_glu, b_glu, g_sgu, ws, ws_t, bias_s)


def _s5_bwd(dys, us, st_re, st_im, abar_re, abar_im, b_re, b_im, c_re, c_im, d_skip, tm):
    S = us.shape[0]
    nt = S // tm
    w = 8 * SSM_P
    hb = tm // 8

    def body(dys_ref, us_ref, str_ref, sti_ref, hr_ref, hi_ref, ar_ref, ai_ref, br_ref, bi_ref, cr_ref, ci_ref, d_ref,
             dus_ref, dab_ref, dd_ref, dbr_ref, dbi_ref, dcr_ref, dci_ref, tab_ref, car_ref, gr_ref, gi_ref):
        i = pl.program_id(1)
        ri = nt - 1 - i

        @pl.when(i == 0)
        def _():
            car_ref[...] = jnp.zeros_like(car_ref)
            for k, t in enumerate(_scan_tables(ar_ref[...], -ai_ref[...], True)):
                tab_ref[k] = t
            for r in (dab_ref, dd_ref, dbr_ref, dbi_ref, dcr_ref, dci_ref):
                r[...] = jnp.zeros_like(r)

        dys_v = dys_ref[...]
        dyb = dys_v.astype(MXU)
        gr_ref[...] = _dot(dyb, cr_ref[0])
        gi_ref[...] = -_dot(dyb, ci_ref[0])

        def grp(kk, carry):
            r0 = pl.multiple_of((hb - 1 - kk) * 8, 8)
            xr, xi = _scan_group(gr_ref[pl.ds(r0, 8), :], gi_ref[pl.ds(r0, 8), :], tab_ref, carry[0], carry[1], True)
            gr_ref[pl.ds(r0, 8), :] = xr
            gi_ref[pl.ds(r0, 8), :] = xi
            return xr[0:1, :], xi[0:1, :]

        cr, ci = lax.fori_loop(0, hb, grp, (car_ref[0:1, :], car_ref[1:2, :]))
        car_ref[0:1, :] = cr
        car_ref[1:2, :] = ci

        gsr = gr_ref[...]
        gsi = gi_ref[...]
        sr = str_ref[...]
        si = sti_ref[...]
        rows = lax.broadcasted_iota(jnp.int32, (tm, w), 0)
        first = ri == 0
        spr = jnp.where(rows == 0, jnp.where(first, 0.0, hr_ref[7:8, :]), pltpu.roll(sr, 1, 0))
        spi = jnp.where(rows == 0, jnp.where(first, 0.0, hi_ref[7:8, :]), pltpu.roll(si, 1, 0))
        dab_ref[0, 0:1, :] += _rowsum(gsr * spr + gsi * spi)
        dab_ref[0, 1:2, :] += _rowsum(gsi * spr - gsr * spi)

        gbr = gsr.astype(MXU)
        gbi = gsi.astype(MXU)
        u = us_ref[...]
        ub = u.astype(MXU)
        dus_ref[...] = (_dot_nt(gbr, br_ref[0]) + _dot_nt(gbi, bi_ref[0]) + d_ref[...] * dys_v).astype(MXU)
        dd_ref[0, 0:1, :] += _rowsum(dys_v * u)
        dbr_ref[0] += _dot_tn(ub, gbr)
        dbi_ref[0] += _dot_tn(ub, gbi)
        dcr_ref[0] += _dot_tn(dyb, sr.astype(MXU))
        dci_ref[0] -= _dot_tn(dyb, si.astype(MXU))

    blk = lambda: pl.BlockSpec((1, 8 * SSM_H, w), lambda j, i: (j, 0, 0))
    rowl = lambda: pl.BlockSpec((tm, LANES), lambda j, i: (nt - 1 - i, j))
    roww = lambda: pl.BlockSpec((tm, w), lambda j, i: (nt - 1 - i, j))
    halo = lambda: pl.BlockSpec((8, w), lambda j, i: (jnp.maximum((nt - 1 - i) * hb - 1, 0), j))
    return pl.pallas_call(
        body, name="s5_bwd", grid=(SSM_BLK, nt),
        in_specs=[rowl(), rowl(), roww(), roww(), halo(), halo(),
                  pl.BlockSpec((1, w), lambda j, i: (0, j)), pl.BlockSpec((1, w), lambda j, i: (0, j)),
                  blk(), blk(), blk(), blk(),
                  pl.BlockSpec((1, LANES), lambda j, i: (0, j))],
        out_specs=[rowl(),
                   pl.BlockSpec((1, 8, w), lambda j, i: (j, 0, 0)), pl.BlockSpec((1, 8, LANES), lambda j, i: (j, 0, 0)),
                   blk(), blk(), blk(), blk()],
        out_shape=[_sds((S, SSM_W), MXU), _sds((SSM_BLK, 8, w)), _sds((SSM_BLK, 8, LANES)),
                   _sds((SSM_BLK, 8 * SSM_H, w)), _sds((SSM_BLK, 8 * SSM_H, w)),
                   _sds((SSM_BLK, 8 * SSM_H, w)), _sds((SSM_BLK, 8 * SSM_H, w))],
        scratch_shapes=[pltpu.VMEM((8, 8, w), F32), pltpu.VMEM((8, w), F32),
                        pltpu.VMEM((tm, w), F32), pltpu.VMEM((tm, w), F32)],
        compiler_params=_cp("parallel", "arbitrary"),
    )(dys, us, st_re, st_im, st_re, st_im, abar_re, abar_im, b_re, b_im, c_re, c_im, d_skip)


def _in_bwd(dus, duv, dgl, dx1, x, g_mix, w_in, tm):
    S = x.shape[0]

    def body(dus_ref, duv_ref, dgl_ref, dx1_ref, x_ref, g_ref, w_ref, gx_ref, dg_ref):
        @pl.when(pl.program_id(0) == 0)
        def _():
            dg_ref[...] = jnp.zeros_like(dg_ref)

        dh = (_dot(dus_ref[...], w_ref[0:SSM_W, :])
              + _dot(duv_ref[...], w_ref[SSM_W:SSM_W + 2 * SGU_W, :])
              + _dot(dgl_ref[...], w_ref[SSM_W + 2 * SGU_W:, :]))
        xv = x_ref[...]
        r = _rms(xv)
        xn = xv * r
        dg_ref[...] += _rowsum(dh * xn)
        gx_ref[...] = dx1_ref[...] + _rms_bwd(dh * g_ref[...], xn, r)

    row = lambda n: pl.BlockSpec((tm, n), lambda i: (i, 0))
    return pl.pallas_call(
        body, name="in_bwd", grid=(S // tm,),
        in_specs=[row(SSM_W), row(2 * SGU_W), row(2 * D_MODEL), row(D_MODEL), row(D_MODEL), _full((1, D_MODEL)),
                  _full(w_in.shape)],
        out_specs=[row(D_MODEL), _full((1, D_MODEL))],
        out_shape=[_sds((S, D_MODEL)), _sds((1, D_MODEL))],
        compiler_params=_cp("arbitrary"),
    )(dus, duv, dgl, dx1, x, g_mix, w_in)


def _pick(n, cands):
    for c in cands:
        if n % c == 0:
            return c
    return n


def _wgrad_split(a, bs, nsplit, tk, name):
    S, K = a.shape
    widths = [b.shape[1] for b in bs]
    N = sum(widths)
    c = N // nsplit
    ts = _pick(S, (512, 256, 128))
    ns = S // ts

    def body(*refs):
        a_ref = refs[0]
        b_refs = refs[1:1 + len(bs)]
        o_ref = refs[1 + len(bs)]
        acc_ref = refs[2 + len(bs)]
        s = pl.program_id(1)
        av = a_ref[...].astype(MXU)
        off = 0
        for b_ref, wdt in zip(b_refs, widths):
            part = _dot_tn(av, b_ref[...].astype(MXU))

            @pl.when(s == 0)
            def _():
                acc_ref[:, off:off + wdt] = part

            @pl.when(s > 0)
            def _():
                acc_ref[:, off:off + wdt] += part

            off += wdt

        @pl.when(s == ns - 1)
        def _():
            for d in range(nsplit):
                o_ref[d] = acc_ref[:, c * d:c * (d + 1)].astype(MXU)

    return pl.pallas_call(
        body, name=name, grid=(K // tk, ns),
        in_specs=[pl.BlockSpec((ts, tk), lambda k, s: (s, k))]
                 + [pl.BlockSpec((ts, wdt), lambda k, s: (s, 0)) for wdt in widths],
        out_specs=pl.BlockSpec((nsplit, tk, c), lambda k, s: (0, k, 0)),
        out_shape=_sds((nsplit, K, c), MXU),
        scratch_shapes=[pltpu.VMEM((tk, N), F32)],
        compiler_params=_cp("parallel", "arbitrary"),
    )(a, *bs)


def _wgrad_in_t(dps, h1, name):
    S, K = h1.shape
    widths = [b.shape[1] for b in dps]
    N = sum(widths)
    c = N // N_DEV
    ts = _pick(S, (512, 256, 128))
    ns = S // ts

    def body(*refs):
        b_refs = refs[:len(dps)]
        a_ref, o_ref, acc_ref = refs[len(dps):]
        s = pl.program_id(0)
        av = a_ref[...]
        off = 0
        for b_ref, wdt in zip(b_refs, widths):
            for lo in range(0, wdt, 512):
                part = _dot_tn(b_ref[:, lo:lo + 512], av)
                rows = slice(off + lo, off + lo + 512)

                @pl.when(s == 0)
                def _():
                    acc_ref[rows, :] = part

                @pl.when(s > 0)
                def _():
                    acc_ref[rows, :] += part

            off += wdt

        @pl.when(s == ns - 1)
        def _():
            for d in range(N_DEV):
                o_ref[d] = acc_ref[c * d:c * (d + 1), :].astype(MXU)

    return pl.pallas_call(
        body, name=name, grid=(ns,),
        in_specs=[pl.BlockSpec((ts, wdt), lambda s: (s, 0)) for wdt in widths] + [pl.BlockSpec((ts, K), lambda s: (s, 0))],
        out_specs=_full((N_DEV, c, K)),
        out_shape=_sds((N_DEV, c, K), MXU),
        scratch_shapes=[pltpu.VMEM((N, K), F32)],
        compiler_params=pltpu.CompilerParams(dimension_semantics=("arbitrary",), vmem_limit_bytes=56 * 1024 * 1024),
    )(*dps, h1)


def _wgrad_blk(a3, b3, nblk, a_of, b_of, name):
    S, K = a3.shape[1:]
    N = b3.shape[2]
    ts = _pick(S, (512, 256, 128))
    ns = S // ts

    def body(a_ref, b_ref, o_ref, acc_ref):
        s = pl.program_id(1)
        part = _dot_tn(a_ref[0].astype(MXU), b_ref[0].astype(MXU))

        @pl.when(s == 0)
        def _():
            acc_ref[...] = part

        @pl.when(s > 0)
        def _():
            acc_ref[...] += part

        @pl.when(s == ns - 1)
        def _():
            o_ref[0] = acc_ref[...].astype(MXU)

    return pl.pallas_call(
        body, name=name, grid=(nblk, ns),
        in_specs=[pl.BlockSpec((1, ts, K), lambda b, s: (a_of(b), s, 0)),
                  pl.BlockSpec((1, ts, N), lambda b, s: (b_of(b), s, 0))],
        out_specs=pl.BlockSpec((1, K, N), lambda b, s: (b, 0, 0)),
        out_shape=_sds((nblk, K, N), MXU),
        scratch_shapes=[pltpu.VMEM((K, N), F32)],
        compiler_params=_cp("parallel", "arbitrary"),
    )(a3, b3)


def _assemble_cols(blocks_list, name):
    def body(*refs):
        n = len(blocks_list)
        for b_ref, o_ref in zip(refs[:n], refs[n:]):
            c = b_ref.shape[2]
            for d in range(N_DEV):
                o_ref[:, c * d:c * (d + 1)] = b_ref[d]

    return pl.pallas_call(
        body, name=name,
        out_shape=[_sds((b.shape[1], N_DEV * b.shape[2]), b.dtype) for b in blocks_list],
        compiler_params=pltpu.CompilerParams(vmem_limit_bytes=VMEM_LIMIT),
    )(*blocks_list)


def _tile(S, want):
    return want if S % want == 0 else S


def _local_step(x, tgt, p, ffn_weights, grads_out):
    S = x.shape[0]
    tm = _tile(S, 256)
    tl = _tile(S, 512)

    rep = lambda a: jnp.repeat(a, SSM_H, axis=0)
    are = rep(p["a_re"])
    aim = rep(p["a_im"])
    ldt = jnp.broadcast_to(rep(p["log_dt"].reshape(SSM_G, 1)), are.shape)
    br_t = p["b_re_t"].reshape(are.shape)
    bi_t = p["b_im_t"].reshape(are.shape)
    abr, abi, bbr, bbi = _s5_params_fwd(are, aim, ldt, br_t, bi_t)
    head = lambda a: a.reshape(SSM_G, SSM_H, SSM_P)[:, 0, :].reshape(1, SSM_G * SSM_P)
    abar_re, abar_im = head(abr), head(abi)
    bd_br = _blockdiag(bbr).astype(MXU)
    bd_bi = _blockdiag(bbi).astype(MXU)
    bd_cr = _blockdiag(p["c_re"].reshape(are.shape)).astype(MXU)
    bd_ci = _blockdiag(p["c_im"].reshape(are.shape)).astype(MXU)
    d_skip = p["d_skip"].reshape(1, SSM_W)

    tril = jnp.tril(jnp.ones((CHUNK, CHUNK), dtype=bool))
    ws = jnp.where(tril[None], p["w_s"], 0.0)
    ws_b = ws.astype(MXU)
    ws_t = ws.transpose(0, 2, 1).astype(MXU)
    bias_s = jnp.repeat(p["b_s"].T, SGU_D, axis=1)

    g_mix = p["g_mix"].reshape(1, D_MODEL)
    g_ffn = p["g_ffn"].reshape(1, D_MODEL)
    g_final = p["g_final"].reshape(1, D_MODEL)
    g_sgu = p["g_sgu"].reshape(1, SGU_W)
    b_glu = p["b_glu"].reshape(1, SSM_W)
    conv_b = p["conv_b"].reshape(N_DEV, 1, FF_CW)

    h1, us, uv, gl = _in_fwd(x, g_mix, p["w_in_t"], tm)
    st_re, st_im, ys = _s5_fwd(us, abar_re, abar_im, bd_br, bd_bi, bd_cr, bd_ci, d_skip, tl)
    yg, yap, sg, ya, yb, m, x1, h2 = _mix_fwd(x, ys, uv, gl, p["w_glu"], b_glu, p["w_proj_a"], g_sgu, ws_b, bias_s,
                                              p["w_proj_b"], p["w_out"], g_ffn, tm)
    w_up, conv_w, w_down = ffn_weights(h2)
    up, ff, dx2, loss, dg_final = _ffn_fwd(h2, x1, tgt, w_up, conv_w, conv_b, w_down, g_final, tl)

    dup, dx1, dconv, dg_ffn = _ffn_bwd(dx2, up, x1, w_up, conv_w, conv_b, w_down, g_ffn, tl)
    rows8 = lambda g: g.reshape(N_DEV, g.shape[1] // N_DEV, g.shape[2])
    g_up = _wgrad_blk(dup.reshape(N_DEV, S, FF_CW), h2[None], N_DEV, lambda b: b, lambda b: 0, "wgrad_up")
    g_down = _wgrad_blk(ff, dx2[None], FF_NCB, lambda b: b, lambda b: 0, "wgrad_down").reshape(
        N_DEV, D_FF // N_DEV, D_MODEL)
    token = grads_out(("w_up", "w_down"), (g_up, g_down))
    dgl, dya, dyb, dz, dys, duv, db_glu, dg_sgu, dws, dbs = _mix_bwd(
        dx1, gl, ya, yb, ys, uv, p["w_out"], p["w_proj_a"], p["w_proj_b"], p["w_glu"], b_glu + token[0:1, 0:1], g_sgu,
        ws_b, ws_t, bias_s, tm)
    token = grads_out(("w_glu", "w_proj_a", "w_proj_b", "w_out"),
                      (rows8(_wgrad_split(yg, [dz], 1, SSM_W, "wgrad_glu")),
                       _wgrad_split(yap, [dya], N_DEV, SSM_W, "wgrad_pa"),
                       _wgrad_split(sg, [dyb], N_DEV, SGU_W, "wgrad_pb"),
                       rows8(_wgrad_split(m, [dx1], 1, 512, "wgrad_out"))))
    dus, dab, dd, dbbr, dbbi, dcr, dci = _s5_bwd(dys, us, st_re, st_im, abar_re, abar_im, bd_br, bd_bi, bd_cr, bd_ci,
                                                 d_skip + token[0:1, 0:1], tl)
    token = grads_out(("w_in",), (_wgrad_in_t([dus, duv, dgl], h1, "wgrad_in"),))
    grad_x, dg_mix = _in_bwd(dus, duv, dgl, dx1, x, g_mix + token[0:1, 0:1], p["w_in_t"], tm)

    spread = lambda v: jnp.repeat(v.reshape(SSM_G, SSM_P), SSM_H, axis=0) * (1.0 / SSM_H)
    dabr = spread(dab[:, 0, :])
    dabi = spread(dab[:, 1, :])
    dare, daim, dldt, dbr_t, dbi_t = _s5_params_bwd(are, aim, ldt, br_t, bi_t, dabr, dabi,
                                                    _unblockdiag(dbbr), _unblockdiag(dbbi))
    fold = lambda a: a.reshape(SSM_G, SSM_H, SSM_P).sum(axis=1)

    grads = {
        "g_mix": dg_mix,
        "a_re": fold(dare), "a_im": fold(daim), "log_dt": fold(dldt).sum(axis=1),
        "b_re": dbr_t, "b_im": dbi_t,
        "c_re": _unblockdiag(dcr).reshape(SSM_G, SSM_H, SSM_P),
        "c_im": _unblockdiag(dci).reshape(SSM_G, SSM_H, SSM_P),
        "d_skip": dd[:, 0, :].reshape(SSM_W),
        "b_glu": db_glu,
        "g_sgu": dg_sgu,
        "w_s": dws,
        "b_s": dbs.reshape(CHUNK, SGU_G, SGU_D).sum(axis=-1).T,
        "g_ffn": dg_ffn,
        "conv_w": dconv[:, 0:3, :],
        "conv_b": dconv[:, 3, :].reshape(2 * D_FF),
        "g_final": dg_final,
    }
    return loss, grad_x, grads


_ANY = pl.BlockSpec(memory_space=pl.ANY)
_MESH = pl.DeviceIdType.MESH


def _allgather(shards, dtypes, name, cast_only=()):
    n = len(shards)
    e = len(cast_only)

    def body(*refs):
        in_refs, extra_in = refs[:n], refs[n:n + e]
        out_refs, extra_out = refs[n + e:2 * n + e], refs[2 * n + e:2 * n + 2 * e]
        stage = refs[2 * n + 2 * e:3 * n + 2 * e]
        send_sems, recv_sems, local_sems = refs[3 * n + 2 * e:]
        for a in range(n):
            stage[a][...] = in_refs[a][...].astype(dtypes[a])
        for i in range(e):
            extra_out[i][...] = extra_in[i][...].astype(MXU)
        x, y, c = lax.axis_index("x"), lax.axis_index("y"), lax.axis_index("c")
        me, sibling = (x, y, c), (x, y, 1 - c)
        chips = [(1 - x, y), (x, 1 - y), (1 - x, 1 - y)]

        def slot(a, px, py, pc):
            return out_refs[a].at[4 * px + 2 * py + pc]

        def copy(a, k, block, to, src=None):
            return pltpu.make_async_remote_copy(
                src_ref=slot(a, *block) if src is None else src, dst_ref=slot(a, *block),
                send_sem=send_sems.at[a, k], recv_sem=recv_sems.at[a, k], device_id=to, device_id_type=_MESH)

        mine = [pltpu.make_async_copy(stage[a], slot(a, *me), local_sems.at[a]) for a in range(n)]
        for cp in mine:
            cp.start()
        first = []
        for j, chip in enumerate(chips):
            first += [copy(a, 1 + j, me, (*chip, c), src=stage[a]) for a in range(n)]
        first += [copy(a, 0, me, sibling, src=stage[a]) for a in range(n)]
        for cp in first:
            cp.start()
        passed = []
        for j, chip in enumerate(chips):
            for a in range(n):
                copy(a, 1 + j, (*chip, c), me).wait_recv()
                fwd = copy(a, 4 + j, (*chip, c), sibling)
                fwd.start()
                passed.append(fwd)
        for a in range(n):
            copy(a, 0, sibling, me).wait_recv()
        for j, chip in enumerate(chips):
            for a in range(n):
                copy(a, 4 + j, (*chip, 1 - c), me).wait_recv()
        for cp in first + passed:
            cp.wait_send()
        for cp in mine:
            cp.wait()

    vmem = pl.BlockSpec(memory_space=pltpu.VMEM)
    res = pl.pallas_call(
        body, name=name, in_specs=[vmem] * (n + e), out_specs=[_ANY] * n + [vmem] * e,
        out_shape=[_sds((N_DEV,) + s.shape, dt) for s, dt in zip(shards, dtypes)]
                  + [_sds(s.shape, MXU) for s in cast_only],
        scratch_shapes=[pltpu.VMEM(s.shape, dt) for s, dt in zip(shards, dtypes)]
                       + [pltpu.SemaphoreType.DMA((n, 7)), pltpu.SemaphoreType.DMA((n, 7)), pltpu.SemaphoreType.DMA((n,))],
        compiler_params=pltpu.CompilerParams(vmem_limit_bytes=VMEM_LIMIT),
    )(*shards, *cast_only)
    return res[:n], res[n:]


def _all_to_all(sends, name):
    n = len(sends)

    def body(*refs):
        send_refs, recv_refs = refs[:n], refs[n:2 * n]
        send_sems, recv_sems, local_sems = refs[2 * n:]
        x, y, c = lax.axis_index("x"), lax.axis_index("y"), lax.axis_index("c")
        me = 4 * x + 2 * y + c
        mine = [pltpu.make_async_copy(send_refs[a].at[me], recv_refs[a].at[me], local_sems.at[a]) for a in range(n)]
        for cp in mine:
            cp.start()
        copies = []
        for k in (2, 4, 6, 3, 5, 7, 1):
            px = 1 - x if k & 4 else x
            py = 1 - y if k & 2 else y
            pc = 1 - c if k & 1 else c
            peer = 4 * px + 2 * py + pc
            for a in range(n):
                sems = dict(send_sem=send_sems.at[a, k - 1], recv_sem=recv_sems.at[a, k - 1],
                            device_id=(px, py, pc), device_id_type=_MESH)
                cp = pltpu.make_async_remote_copy(src_ref=send_refs[a].at[peer], dst_ref=recv_refs[a].at[me], **sems)
                cp.start()
                landing = pltpu.make_async_remote_copy(src_ref=send_refs[a].at[peer], dst_ref=recv_refs[a].at[peer],
                                                       **sems)
                copies.append((cp, landing))
        for _, landing in copies:
            landing.wait_recv()
        for cp, _ in copies:
            cp.wait_send()
        for cp in mine:
            cp.wait()

    return pl.pallas_call(
        body, name=name, in_specs=[_ANY] * n, out_specs=[_ANY] * n,
        out_shape=[_sds(s.shape, s.dtype) for s in sends],
        scratch_shapes=[pltpu.SemaphoreType.DMA((n, 7)), pltpu.SemaphoreType.DMA((n, 7)), pltpu.SemaphoreType.DMA((n,))],
    )(*sends)


_HBM = pl.BlockSpec(memory_space=pltpu.HBM)
_SEM = pl.BlockSpec(memory_space=pltpu.SEMAPHORE)
_EFFECT = pltpu.SideEffectType.DATAFLOW_SIDE_EFFECTING
_PEER_ORDER = (2, 4, 6, 3, 5, 7, 1)


def _peer(k):
    x, y, c = lax.axis_index("x"), lax.axis_index("y"), lax.axis_index("c")
    px = 1 - x if k & 4 else x
    py = 1 - y if k & 2 else y
    pc = 1 - c if k & 1 else c
    return (px, py, pc), 4 * px + 2 * py + pc


def _push_start(srcs, lands, slotted, name):
    n = len(srcs)

    def body(*refs):
        src_refs, land_refs = refs[:n], refs[n:2 * n]
        send_sems, recv_sems, token = refs[2 * n], refs[2 * n + 1], refs[-1]
        me = 4 * lax.axis_index("x") + 2 * lax.axis_index("y") + lax.axis_index("c")
        for k in _PEER_ORDER:
            dev, peer = _peer(k)
            for a in range(n):
                pltpu.make_async_remote_copy(
                    src_ref=src_refs[a].at[peer] if slotted else src_refs[a], dst_ref=land_refs[a].at[me],
                    send_sem=send_sems.at[7 * a + k - 1], recv_sem=recv_sems.at[7 * a + k - 1],
                    device_id=dev, device_id_type=_MESH).start()
        token[...] = jnp.zeros_like(token)

    bufs = list(srcs) + list(lands)
    res = pl.pallas_call(
        body, name=name, in_specs=[_HBM] * (2 * n),
        out_specs=(_SEM, _SEM, *[_HBM] * (2 * n), pl.BlockSpec(memory_space=pltpu.VMEM)),
        out_shape=(pltpu.SemaphoreType.DMA((7 * n,)), pltpu.SemaphoreType.DMA((7 * n,)),
                   *[pltpu.HBM(b.shape, b.dtype) for b in bufs], _sds((8, LANES))),
        input_output_aliases={i: 2 + i for i in range(2 * n)},
        compiler_params=pltpu.CompilerParams(has_side_effects=_EFFECT),
    )(*[pltpu.with_memory_space_constraint(b, pltpu.HBM) for b in bufs])
    return res[0], res[1], res[2:2 + n], res[2 + n:2 + 2 * n], res[-1]


def _push_wait(send_sems, recv_sems, srcs, lands, slotted, after, name):
    n = len(srcs)

    def body(*refs):
        src_refs, land_refs = refs[:n], refs[n:2 * n]
        send_sems, recv_sems = refs[2 * n], refs[2 * n + 1]
        for k in _PEER_ORDER:
            dev, peer = _peer(k)
            for a in range(n):
                cp = pltpu.make_async_remote_copy(
                    src_ref=src_refs[a].at[peer] if slotted else src_refs[a], dst_ref=land_refs[a].at[peer],
                    send_sem=send_sems.at[7 * a + k - 1], recv_sem=recv_sems.at[7 * a + k - 1],
                    device_id=dev, device_id_type=_MESH)
                cp.wait_send()
                cp.wait_recv()

    bufs = list(srcs) + list(lands)
    res = pl.pallas_call(
        body, name=name, in_specs=[_HBM] * (2 * n) + [_SEM, _SEM] + [_ANY] * len(after), out_specs=[_HBM] * (2 * n),
        out_shape=[pltpu.HBM(b.shape, b.dtype) for b in bufs],
        input_output_aliases={i: i for i in range(2 * n)},
        compiler_params=pltpu.CompilerParams(has_side_effects=_EFFECT),
    )(*bufs, send_sems, recv_sems, *after)
    return res[n:]


def _adamw(w, g, m, v):
    m2 = ADAM_B1 * m + (1.0 - ADAM_B1) * g
    v2 = ADAM_B2 * v + (1.0 - ADAM_B2) * (g * g)
    m_hat = m2 / (1.0 - ADAM_B1 ** ADAM_STEP)
    v_hat = v2 / (1.0 - ADAM_B2 ** ADAM_STEP)
    delta = -ADAM_LR * (m_hat / (jnp.sqrt(v_hat) + ADAM_EPS) + ADAM_WD * w)
    return delta, m2, v2


def _adam_shard(parts, w, m, v, name):
    _, r, c = w.shape
    tr = max(t for t in range(16, 257, 16) if r % t == 0)

    def body(p_ref, w_ref, m_ref, v_ref, g_ref, d_ref, m2_ref, v2_ref):
        g = p_ref[0].astype(F32)
        for s in range(1, N_DEV):
            g = g + p_ref[s].astype(F32)
        g_ref[0] = g
        d_ref[0], m2_ref[0], v2_ref[0] = _adamw(w_ref[0], g, m_ref[0], v_ref[0])

    row = lambda: pl.BlockSpec((1, tr, c), lambda i: (0, i, 0))
    return pl.pallas_call(
        body, name=name, grid=(r // tr,),
        in_specs=[pl.BlockSpec((N_DEV, tr, c), lambda i: (0, i, 0)), row(), row(), row()],
        out_specs=[row(), row(), row(), row()], out_shape=[_sds((1, r, c))] * 4,
        compiler_params=_cp("parallel"),
    )(parts, w, m, v)


def _adam_small(gs, ws, ms, vs, name):
    n = len(gs)

    def body(*refs):
        ins, outs = refs[:4 * n], refs[4 * n:]
        for i in range(n):
            g = ins[i][...]
            d, m2, v2 = _adamw(ins[n + i][...], g, ins[2 * n + i][...], ins[3 * n + i][...])
            outs[i][...] = d
            outs[n + i][...] = m2
            outs[2 * n + i][...] = v2

    res = pl.pallas_call(
        body, name=name, out_shape=[_sds(w.shape) for w in ws] * 3,
        compiler_params=pltpu.CompilerParams(vmem_limit_bytes=VMEM_LIMIT),
    )(*gs, *ws, *ms, *vs)
    return res[:n], res[n:2 * n], res[2 * n:]


def _sum_slots(parts, name):
    R = parts.shape[1]

    def body(p_ref, o_ref):
        g = p_ref[0]
        for s in range(1, N_DEV):
            g = g + p_ref[s]
        o_ref[...] = g

    return pl.pallas_call(body, name=name, out_shape=_sds((R, LANES)))(parts)


def _pad_to(a, n, axis):
    extra = n - a.shape[axis]
    if extra == 0:
        return a
    widths = [(0, 0)] * a.ndim
    widths[axis] = (0, extra)
    return jnp.pad(a, widths)


def _ceil_to(n, k):
    return -(-n // k) * k


def _pack_rows(flats, rows_multiple):
    parts = [_pad_to(f, _ceil_to(f.shape[-1], LANES), f.ndim - 1) for f in flats]
    cat = jnp.concatenate(parts, axis=-1)
    total = _ceil_to(cat.shape[-1], LANES * rows_multiple)
    cat = _pad_to(cat, total, cat.ndim - 1)
    return cat.reshape(cat.shape[:-1] + (total // LANES, LANES))


def _unpack_rows(buf, sizes):
    flat = buf.reshape(buf.shape[:-2] + (-1,))
    out, off = [], 0
    for n in sizes:
        out.append(flat[..., off:off + n])
        off += _ceil_to(n, LANES)
    return out


_MIX_BIG = ("w_in", "w_glu", "w_proj_a", "w_proj_b", "w_out")
_BIG = _MIX_BIG + ("w_up", "w_down")
_SMALL = ("g_mix", "a_re", "a_im", "log_dt", "b_re", "b_im", "c_re", "c_im", "d_skip", "b_glu", "g_sgu", "w_s", "b_s",
          "g_ffn", "conv_b", "g_final")
_SMALL_ROWS_MULTIPLE = 8 * N_DEV
_TRANSPOSED = ("w_in", "w_up", "b_re", "b_im")


def _as_2d(a):
    return a.reshape(-1, a.shape[-1]) if a.ndim > 1 else a.reshape(1, -1)


def kernel(x, g_mix, w_in, a_re, a_im, log_dt, b_re, b_im, c_re, c_im, d_skip, w_glu, b_glu, w_proj_a, g_sgu, w_s, b_s, w_proj_b, w_out, g_ffn, w_up, conv_w, conv_b, w_down, g_final, loss_target, m_g_mix, m_w_in, m_a_re, m_a_im, m_log_dt, m_b_re, m_b_im, m_c_re, m_c_im, m_d_skip, m_w_glu, m_b_glu, m_w_proj_a, m_g_sgu, m_w_s, m_b_s, m_w_proj_b, m_w_out, m_g_ffn, m_w_up, m_conv_w, m_conv_b, m_w_down, m_g_final, v_g_mix, v_w_in, v_a_re, v_a_im, v_log_dt, v_b_re, v_b_im, v_c_re, v_c_im, v_d_skip, v_w_glu, v_b_glu, v_w_proj_a, v_g_sgu, v_w_s, v_b_s, v_w_proj_b, v_w_out, v_g_ffn, v_w_up, v_conv_w, v_conv_b, v_w_down, v_g_final):
    args = dict(locals())
    me = 4 * lax.axis_index("x") + 2 * lax.axis_index("y") + lax.axis_index("c")

    def own_slot(buf, block):
        return lax.dynamic_update_slice(buf, block[None], (me,) + (0,) * block.ndim)

    for n in _TRANSPOSED:
        for pre in ("", "m_", "v_"):
            args[pre + n] = jnp.swapaxes(args[pre + n], -1, -2)
    gathered, (up_sh, down_sh) = _allgather([args[n][0] for n in _MIX_BIG], [MXU] * len(_MIX_BIG), "allgather_mixer",
                                            cast_only=(args["w_up"][0], w_down[0]))
    g = dict(zip(_MIX_BIG, gathered))
    ffn_srcs = [up_sh, down_sh, conv_w[0]]
    ffn_lands = [own_slot(lax.empty((N_DEV,) + s.shape, s.dtype), s) for s in ffn_srcs]
    ag_send, ag_recv, ffn_srcs, ffn_lands, ag_token = _push_start(ffn_srcs, ffn_lands, False, "push_ffn_weights")
    w_pa_full, w_pb_full = _assemble_cols([g["w_proj_a"], g["w_proj_b"]], "assemble_cols")
    p = {n: (args[n][0] if n != "g_final" else args[n]) for n in _SMALL if n not in _TRANSPOSED}
    p.update(w_in_t=g["w_in"].reshape(SSM_W + 2 * SGU_W + 2 * D_MODEL, D_MODEL), w_proj_a=w_pa_full, w_proj_b=w_pb_full,
             w_glu=g["w_glu"].reshape(SSM_W, SSM_W), w_out=g["w_out"].reshape(D_MODEL, D_MODEL),
             b_re_t=args["b_re"][0], b_im_t=args["b_im"][0])
    p["g_mix"] = p["g_mix"] + ag_token[0:1, 0:1]

    def ffn_weights(after):
        w_up_g, w_down_g, conv_w_g = _push_wait(ag_send, ag_recv, ffn_srcs, ffn_lands, False, [after], "wait_ffn_weights")
        return w_up_g, conv_w_g, w_down_g.reshape(D_FF, D_MODEL)

    pushes = []

    def grads_out(names, sends):
        lands = [own_slot(lax.empty(s.shape, s.dtype), lax.dynamic_index_in_dim(s, me, 0, keepdims=False))
                 for s in sends]
        send_sems, recv_sems, srcs, lands, token = _push_start(list(sends), lands, True, "push_grads_" + names[0])
        pushes.append((names, send_sems, recv_sems, srcs, lands))
        return token

    loss_part, grad_x, grads = _local_step(x[0], loss_target[0], p, ffn_weights, grads_out)

    small_names = _SMALL + ("conv_w", "loss")
    small_g = dict(grads, loss=loss_part[0, 0:1])
    flats = [small_g[n].reshape(-1) for n in small_names]
    small_sizes = [f.shape[0] for f in flats]
    g_small = _pack_rows(flats, _SMALL_ROWS_MULTIPLE)
    rs8 = g_small.shape[0] // N_DEV
    recv_small, = _all_to_all([g_small.reshape(N_DEV, rs8, LANES)], "all_to_all_small")
    small_mine = _sum_slots(recv_small, "sum_small")
    g_small_all = _allgather([small_mine], [F32], "allgather_small")[0][0].reshape(N_DEV * rs8, LANES)
    pieces = dict(zip(small_names, _unpack_rows(g_small_all, small_sizes)))
    loss = pieces["loss"][0]
    dconv_w = lax.dynamic_index_in_dim(pieces["conv_w"].reshape(N_DEV, 3, FF_CW), me, axis=0, keepdims=False)

    out = {}
    done = [g_small_all]
    for names, send_sems, recv_sems, srcs, lands in pushes:
        parts = _push_wait(send_sems, recv_sems, srcs, lands, True, done, "wait_grads_" + names[0])
        for n, part in zip(names, parts):
            res = _adam_shard(part, args[n], args["m_" + n], args["v_" + n], "adam_" + n)
            for kind, v in zip(("grad_", "delta_", "new_m_", "new_v_"), res):
                out[kind + n] = v
            done = [res[0]]
    names2 = _SMALL + ("conv_w",)
    gs = [pieces[n].reshape(_as_2d(args[n]).shape) for n in _SMALL] + [dconv_w]
    ds, m2s, v2s = _adam_small(gs, [_as_2d(args[n]) for n in names2], [_as_2d(args["m_" + n]) for n in names2],
                               [_as_2d(args["v_" + n]) for n in names2], "adam_small")
    for n, res in zip(names2, zip(gs, ds, m2s, v2s)):
        for kind, v in zip(("grad_", "delta_", "new_m_", "new_v_"), res):
            out[kind + n] = v.reshape(args[n].shape)
    order = ("g_mix", "w_in", "a_re", "a_im", "log_dt", "b_re", "b_im", "c_re", "c_im", "d_skip", "w_glu", "b_glu",
             "w_proj_a", "g_sgu", "w_s", "b_s", "w_proj_b", "w_out", "g_ffn", "w_up", "conv_w", "conv_b", "w_down",
             "g_final")
    res = [loss, grad_x.reshape(x.shape)]
    for kind in ("grad_", "delta_", "new_m_", "new_v_"):
        res += [jnp.swapaxes(out[kind + n], -1, -2) if n in _TRANSPOSED else out[kind + n] for n in order]
    return tuple(res)
```

```python
import functools
import math

import jax
import jax.numpy as jnp
from jax import lax
from jax.experimental import pallas as pl
from jax.experimental.pallas import tpu as pltpu

F32 = jnp.float32
MXU = jnp.bfloat16
EPS = 1e-6

D_MODEL = 1024
SSM_W = 512
SSM_G, SSM_H, SSM_P = 32, 16, 64
SSM_BLK = 4
SGU_W = 512
SGU_G, SGU_D, CHUNK = 8, 64, 128
D_FF = 2816
N_DEV = 8
FF_CW = 2 * D_FF // N_DEV
FF_NCB = D_FF // FF_CW
LANES = 128

ADAM_LR, ADAM_B1, ADAM_B2, ADAM_EPS, ADAM_WD, ADAM_STEP = 0.001, 0.9, 0.999, 1e-08, 0.01, 10

VMEM_LIMIT = 48 * 1024 * 1024


def _cp(*sem):
    return pltpu.CompilerParams(dimension_semantics=sem, vmem_limit_bytes=VMEM_LIMIT)


def _full(shape):
    n = len(shape)
    return pl.BlockSpec(shape, lambda *_: (0,) * n)


def _sds(shape, dtype=F32):
    return jax.ShapeDtypeStruct(shape, dtype)


def _dot(a, b):
    return jnp.dot(a, b, preferred_element_type=F32)


def _dot_nt(a, b):
    return lax.dot_general(a, b, (((1,), (1,)), ((), ())), preferred_element_type=F32)


def _dot_tn(a, b):
    return lax.dot_general(a, b, (((0,), (0,)), ((), ())), preferred_element_type=F32)


_GELU_C = math.sqrt(2.0 / math.pi)


def _gelu(x):
    return 0.5 * x * (1.0 + jnp.tanh(_GELU_C * (x + 0.044715 * (x * x * x))))


def _gelu_and_grad(x):
    t = jnp.tanh(_GELU_C * (x + 0.044715 * (x * x * x)))
    g = 0.5 * x * (1.0 + t)
    dg = 0.5 * (1.0 + t) + 0.5 * x * (1.0 - t * t) * (_GELU_C * (1.0 + 3.0 * 0.044715 * (x * x)))
    return g, dg


def _sigmoid(x):
    return 1.0 / (1.0 + jnp.exp(-x))


def _rms(x):
    return lax.rsqrt(jnp.mean(x * x, axis=-1, keepdims=True) + EPS)


def _rms_bwd(dxn, xn, r):
    return r * (dxn - xn * jnp.mean(dxn * xn, axis=-1, keepdims=True))


def _rowsum(x):
    return jnp.sum(x, axis=0, keepdims=True)


def _s5_disc(are, aim, ldt, br, bi):
    dt = jnp.exp(ldt)
    mag = jnp.exp(dt * are)
    abr = mag * jnp.cos(dt * aim)
    abi = mag * jnp.sin(dt * aim)
    den = are * are + aim * aim
    nr = abr - 1.0
    ni = abi
    fr = (nr * are + ni * aim) / den
    fi = (ni * are - nr * aim) / den
    return abr, abi, fr * br - fi * bi, fr * bi + fi * br


def _s5_params_fwd(are, aim, ldt, br, bi):
    def body(are_ref, aim_ref, ldt_ref, br_ref, bi_ref, o0, o1, o2, o3):
        outs = _s5_disc(are_ref[...], aim_ref[...], ldt_ref[...], br_ref[...], bi_ref[...])
        for o, v in zip((o0, o1, o2, o3), outs):
            o[...] = v
    shp = are.shape
    return pl.pallas_call(body, name="s5_params_fwd", out_shape=[_sds(shp)] * 4)(are, aim, ldt, br, bi)


def _s5_params_bwd(are, aim, ldt, br, bi, dabr, dabi, dbr, dbi):
    def body(are_ref, aim_ref, ldt_ref, br_ref, bi_ref, c0, c1, c2, c3, o0, o1, o2, o3, o4):
        prim = (are_ref[...], aim_ref[...], ldt_ref[...], br_ref[...], bi_ref[...])
        _, vjp = jax.vjp(_s5_disc, *prim)
        outs = vjp((c0[...], c1[...], c2[...], c3[...]))
        for o, v in zip((o0, o1, o2, o3, o4), outs):
            o[...] = v
    shp = are.shape
    return pl.pallas_call(body, name="s5_params_bwd", out_shape=[_sds(shp)] * 5)(
        are, aim, ldt, br, bi, dabr, dabi, dbr, dbi)


def _blockdiag(m_t):
    m = m_t.reshape(SSM_BLK, 8, SSM_H, 1, SSM_P)
    eye = jnp.eye(8, dtype=bool).reshape(1, 8, 1, 8, 1)
    return jnp.where(eye, m, jnp.zeros((), m_t.dtype)).reshape(SSM_BLK, 8 * SSM_H, 8 * SSM_P)


def _unblockdiag(pc):
    m = pc.reshape(SSM_BLK, 8, SSM_H, 8, SSM_P)
    return jnp.einsum("jghgp->jghp", m).reshape(SSM_G * SSM_H, SSM_P)


def _in_fwd(x, g_mix, w_in_t, tm):
    S = x.shape[0]

    def body(x_ref, g_ref, w_ref, h_ref, us_ref, uv_ref, gl_ref):
        xv = x_ref[...]
        h = (xv * _rms(xv) * g_ref[...]).astype(MXU)
        h_ref[...] = h
        us_ref[...] = _dot_nt(h, w_ref[0:SSM_W, :])
        uv_ref[...] = _dot_nt(h, w_ref[SSM_W:SSM_W + 2 * SGU_W, :])
        gl_ref[...] = _dot_nt(h, w_ref[SSM_W + 2 * SGU_W:, :])

    row = lambda n: pl.BlockSpec((tm, n), lambda i: (i, 0))
    return pl.pallas_call(
        body, name="in_fwd", grid=(S // tm,),
        in_specs=[row(D_MODEL), _full((1, D_MODEL)), _full(w_in_t.shape)],
        out_specs=[row(D_MODEL), row(SSM_W), row(2 * SGU_W), row(2 * D_MODEL)],
        out_shape=[_sds((S, D_MODEL), MXU), _sds((S, SSM_W)), _sds((S, 2 * SGU_W)), _sds((S, 2 * D_MODEL))],
        compiler_params=_cp("parallel"),
    )(x, g_mix, w_in_t)


def _scan_tables(ar, ai, reverse):
    n = ar.shape[-1]
    def mul(p, q):
        return p[0] * q[0] - p[1] * q[1], p[0] * q[1] + p[1] * q[0]
    a1 = (ar, ai)
    a2 = mul(a1, a1)
    a3 = mul(a2, a1)
    a4 = mul(a2, a2)
    a5 = mul(a4, a1)
    a6 = mul(a4, a2)
    a7 = mul(a4, a3)
    a8 = mul(a4, a4)
    pw = (a1, a2, a3, a4, a5, a6, a7, a8)
    rows = lax.broadcasted_iota(jnp.int32, (8, n), 0)
    tabs = []
    for s, a in ((1, a1), (2, a2), (4, a4)):
        keep = (rows + s <= 7) if reverse else (rows >= s)
        for comp in a:
            tabs.append(jnp.where(keep, jnp.broadcast_to(comp, (8, n)), 0.0))
    for c in range(2):
        q = jnp.zeros((8, n), F32)
        for r in range(8):
            e = (8 - r) if reverse else (r + 1)
            q = jnp.where(rows == r, jnp.broadcast_to(pw[e - 1][c], (8, n)), q)
        tabs.append(q)
    return tabs


def _scan_group(xr, xi, tab_ref, cr, ci, reverse):
    for t, s in enumerate((1, 2, 4)):
        pr = tab_ref[2 * t]
        pi = tab_ref[2 * t + 1]
        sh = (8 - s) if reverse else s
        sr = pltpu.roll(xr, sh, 0)
        si = pltpu.roll(xi, sh, 0)
        xr, xi = xr + pr * sr - pi * si, xi + pr * si + pi * sr
    qr = tab_ref[6]
    qi = tab_ref[7]
    return xr + qr * cr - qi * ci, xi + qr * ci + qi * cr


def _s5_fwd(us, abar_re, abar_im, b_re, b_im, c_re, c_im, d_skip, tm):
    S = us.shape[0]
    nt = S // tm
    w = 8 * SSM_P

    def body(us_ref, ar_ref, ai_ref, br_ref, bi_ref, cr_ref, ci_ref, d_ref, str_ref, sti_ref, ys_ref, tab_ref, car_ref):
        i = pl.program_id(1)

        @pl.when(i == 0)
        def _():
            car_ref[...] = jnp.zeros_like(car_ref)
            for k, t in enumerate(_scan_tables(ar_ref[...], ai_ref[...], False)):
                tab_ref[k] = t

        u = us_ref[...]
        ub = u.astype(MXU)
        str_ref[...] = _dot(ub, br_ref[0])
        sti_ref[...] = _dot(ub, bi_ref[0])

        def grp(k, carry):
            r0 = pl.multiple_of(k * 8, 8)
            xr, xi = _scan_group(str_ref[pl.ds(r0, 8), :], sti_ref[pl.ds(r0, 8), :], tab_ref, carry[0], carry[1], False)
            str_ref[pl.ds(r0, 8), :] = xr
            sti_ref[pl.ds(r0, 8), :] = xi
            return xr[7:8, :], xi[7:8, :]

        cr, ci = lax.fori_loop(0, tm // 8, grp, (car_ref[0:1, :], car_ref[1:2, :]))
        car_ref[0:1, :] = cr
        car_ref[1:2, :] = ci
        y = _dot_nt(str_ref[...].astype(MXU), cr_ref[0]) - _dot_nt(sti_ref[...].astype(MXU), ci_ref[0])
        ys_ref[...] = y + d_ref[...] * u

    blk = lambda: pl.BlockSpec((1, 8 * SSM_H, w), lambda j, i: (j, 0, 0))
    return pl.pallas_call(
        body, name="s5_fwd", grid=(SSM_BLK, nt),
        in_specs=[pl.BlockSpec((tm, LANES), lambda j, i: (i, j)),
                  pl.BlockSpec((1, w), lambda j, i: (0, j)), pl.BlockSpec((1, w), lambda j, i: (0, j)),
                  blk(), blk(), blk(), blk(),
                  pl.BlockSpec((1, LANES), lambda j, i: (0, j))],
        out_specs=[pl.BlockSpec((tm, w), lambda j, i: (i, j)), pl.BlockSpec((tm, w), lambda j, i: (i, j)),
                   pl.BlockSpec((tm, LANES), lambda j, i: (i, j))],
        out_shape=[_sds((S, SSM_BLK * w)), _sds((S, SSM_BLK * w)), _sds((S, SSM_W))],
        scratch_shapes=[pltpu.VMEM((8, 8, w), F32), pltpu.VMEM((8, w), F32)],
        compiler_params=_cp("parallel", "arbitrary"),
    )(us, abar_re, abar_im, b_re, b_im, c_re, c_im, d_skip)


def _sgu_mix(vnb, ws_ref, grp):
    acc = jnp.zeros(vnb.shape, F32)
    for g in range(SGU_G):
        acc = jnp.where(grp == g, _dot(ws_ref[g], vnb), acc)
    return acc


def _mix_fwd(x, ys, uv, gl, w_glu, b_glu, w_pa, g_sgu, ws, bias_s, w_pb, w_out, g_ffn, tm):
    S = x.shape[0]

    def body(x_ref, ys_ref, uv_ref, gl_ref, wglu_ref, bglu_ref, wpa_ref, gs_ref, ws_ref, bias_ref, wpb_ref, wout_ref,
             gf_ref, yg_ref, yap_ref, sg_ref, ya_ref, yb_ref, m_ref, x1_ref, h2_ref):
        yg = _gelu(ys_ref[...])
        ygb = yg.astype(MXU)
        yg_ref[...] = ygb
        z = _dot(ygb, wglu_ref[...]) + bglu_ref[...]
        yapb = (yg * _sigmoid(z)).astype(MXU)
        yap_ref[...] = yapb
        ya = _dot(yapb, wpa_ref[...])
        ya_ref[...] = ya

        uvg = _gelu(uv_ref[...])
        u2 = uvg[:, :SGU_W]
        v2 = uvg[:, SGU_W:]
        vnb = (v2 * _rms(v2) * gs_ref[...]).astype(MXU)
        grp = lax.broadcasted_iota(jnp.int32, (CHUNK, SGU_W), 1) // SGU_D
        for c in range(tm // CHUNK):
            rs = slice(c * CHUNK, (c + 1) * CHUNK)
            mixed = _sgu_mix(vnb[rs], ws_ref, grp) + bias_ref[...]
            sg_ref[rs, :] = (u2[rs] * mixed).astype(MXU)
        yb = _dot(sg_ref[...], wpb_ref[...])
        yb_ref[...] = yb

        glv = gl_ref[...]
        m = _sigmoid(glv[:, :D_MODEL]) * ya + _sigmoid(glv[:, D_MODEL:]) * yb
        mb = m.astype(MXU)
        m_ref[...] = mb
        x1 = x_ref[...] + _dot(mb, wout_ref[...])
        x1_ref[...] = x1
        h2_ref[...] = (x1 * _rms(x1) * gf_ref[...]).astype(MXU)

    row = lambda n: pl.BlockSpec((tm, n), lambda i: (i, 0))
    return pl.pallas_call(
        body, name="mix_fwd", grid=(S // tm,),
        in_specs=[row(D_MODEL), row(SSM_W), row(2 * SGU_W), row(2 * D_MODEL),
                  _full(w_glu.shape), _full(b_glu.shape), _full(w_pa.shape), _full(g_sgu.shape), _full(ws.shape),
                  _full(bias_s.shape), _full(w_pb.shape), _full(w_out.shape), _full(g_ffn.shape)],
        out_specs=[row(SSM_W), row(SSM_W), row(SGU_W), row(D_MODEL), row(D_MODEL), row(D_MODEL), row(D_MODEL),
                   row(D_MODEL)],
        out_shape=[_sds((S, SSM_W), MXU), _sds((S, SSM_W), MXU), _sds((S, SGU_W), MXU), _sds((S, D_MODEL)),
                   _sds((S, D_MODEL)), _sds((S, D_MODEL), MXU), _sds((S, D_MODEL)), _sds((S, D_MODEL), MXU)],
        compiler_params=_cp("parallel"),
    )(x, ys, uv, gl, w_glu, b_glu, w_pa, g_sgu, ws, bias_s, w_pb, w_out, g_ffn)


def _conv_taps(u, prev8, rows):
    t1 = prev8[7:8, :]
    t0 = prev8[6:7, :]
    s1 = jnp.where(rows == 0, t1, pltpu.roll(u, 1, 0))
    s2 = jnp.where(rows == 0, t0, jnp.where(rows == 1, t1, pltpu.roll(u, 2, 0)))
    return s1, s2


def _ffn_fwd(h2, x1, tgt, w_up, conv_w, conv_b, w_down, g_final, tm):
    S = h2.shape[0]
    nt = S // tm
    ncb = FF_NCB

    def body(h2_ref, wa_ref, wb_ref, cwa_ref, cwb_ref, cba_ref, cbb_ref, wd_ref, x1_ref, gf_ref, tgt_ref,
             up_ref, ff_ref, dx2_ref, dx2b_ref, loss_ref, dgf_ref, acc_ref, tail_ref):
        i = pl.program_id(0)
        cb = pl.program_id(1)

        @pl.when(i == 0)
        def _():
            tail_ref[cb] = jnp.zeros((2, 8, FF_CW), F32)

        @pl.when(jnp.logical_and(i == 0, cb == 0))
        def _():
            loss_ref[...] = jnp.zeros_like(loss_ref)
            dgf_ref[...] = jnp.zeros_like(dgf_ref)

        h2v = h2_ref[...]
        ua = _dot_nt(h2v, wa_ref[0])
        ub = _dot_nt(h2v, wb_ref[0])
        up_ref[0, 0] = ua
        up_ref[1, 0] = ub
        rows = lax.broadcasted_iota(jnp.int32, (tm, FF_CW), 0)
        s1a, s2a = _conv_taps(ua, tail_ref[cb, 0], rows)
        s1b, s2b = _conv_taps(ub, tail_ref[cb, 1], rows)
        tail_ref[cb, 0] = ua[tm - 8:tm, :]
        tail_ref[cb, 1] = ub[tm - 8:tm, :]
        cwa = cwa_ref[0]
        cwb = cwb_ref[0]
        a = cwa[0:1] * s2a + cwa[1:2] * s1a + cwa[2:3] * ua + cba_ref[0]
        b = cwb[0:1] * s2b + cwb[1:2] * s1b + cwb[2:3] * ub + cbb_ref[0]
        ffb = (a * _sigmoid(a) * b).astype(MXU)
        ff_ref[0] = ffb
        contrib = _dot(ffb, wd_ref[...])

        @pl.when(cb == 0)
        def _():
            acc_ref[...] = contrib

        @pl.when(cb > 0)
        def _():
            acc_ref[...] += contrib

        @pl.when(cb == ncb - 1)
        def _():
            x2 = x1_ref[...] + acc_ref[...]
            r = _rms(x2)
            xn = x2 * r
            g = gf_ref[...]
            diff = xn * g - tgt_ref[...]
            loss_ref[...] += (0.5 / D_MODEL) * jnp.sum(diff * diff)
            dy = diff * (1.0 / D_MODEL)
            dgf_ref[...] += _rowsum(dy * xn)
            dx2 = _rms_bwd(dy * g, xn, r)
            dx2_ref[...] = dx2
            dx2b_ref[...] = dx2.astype(MXU)

    row = lambda n: pl.BlockSpec((tm, n), lambda i, c: (i, 0))
    gate = lambda r: pl.BlockSpec((1, r, FF_CW), lambda i, c: (c, 0, 0))
    lin = lambda r: pl.BlockSpec((1, r, FF_CW), lambda i, c: (ncb + c, 0, 0))
    return pl.pallas_call(
        body, name="ffn_fwd", grid=(nt, ncb),
        in_specs=[row(D_MODEL),
                  pl.BlockSpec((1, FF_CW, D_MODEL), lambda i, c: (c, 0, 0)),
                  pl.BlockSpec((1, FF_CW, D_MODEL), lambda i, c: (ncb + c, 0, 0)),
                  gate(3), lin(3), gate(1), lin(1),
                  pl.BlockSpec((FF_CW, D_MODEL), lambda i, c: (c, 0)),
                  row(D_MODEL), _full((1, D_MODEL)), row(D_MODEL)],
        out_specs=[pl.BlockSpec((2, 1, tm, FF_CW), lambda i, c: (0, c, i, 0)),
                   pl.BlockSpec((1, tm, FF_CW), lambda i, c: (c, i, 0)),
                   row(D_MODEL), row(D_MODEL), _full((1, LANES)), _full((1, D_MODEL))],
        out_shape=[_sds((2, ncb, S, FF_CW)), _sds((ncb, S, FF_CW), MXU), _sds((S, D_MODEL)), _sds((S, D_MODEL), MXU),
                   _sds((1, LANES)), _sds((1, D_MODEL))],
        scratch_shapes=[pltpu.VMEM((tm, D_MODEL), F32), pltpu.VMEM((ncb, 2, 8, FF_CW), F32)],
        compiler_params=_cp("arbitrary", "arbitrary"),
    )(h2, w_up, w_up, conv_w, conv_w, conv_b, conv_b, w_down, x1, g_final, tgt)


def _ffn_bwd(dx2, up, x1, w_up, conv_w, conv_b, w_down, g_ffn, tm):
    S = dx2.shape[0]
    nt = S // tm
    ncb = FF_NCB
    hb = tm // 8

    def body(dx2_ref, up_ref, hp_ref, cwa_ref, cwb_ref, cba_ref, cbb_ref, wd_ref, wa_ref, wb_ref,
             x1_ref, g_ref, dup_ref, dx1_ref, dx1b_ref, dconv_ref, dg_ref, acc_ref, head_ref):
        i = pl.program_id(0)
        cb = pl.program_id(1)
        ri = nt - 1 - i

        @pl.when(i == 0)
        def _():
            head_ref[cb] = jnp.zeros((2, 8, FF_CW), F32)
            dconv_ref[cb] = jnp.zeros((8, FF_CW), F32)
            dconv_ref[ncb + cb] = jnp.zeros((8, FF_CW), F32)

        @pl.when(jnp.logical_and(i == 0, cb == 0))
        def _():
            dg_ref[...] = jnp.zeros_like(dg_ref)

        dx2v = dx2_ref[...]
        dff = _dot_nt(dx2v.astype(MXU), wd_ref[...])
        ua = up_ref[0, 0]
        ub = up_ref[1, 0]
        rows = lax.broadcasted_iota(jnp.int32, (tm, FF_CW), 0)
        first = ri == 0
        s1a, s2a = _conv_taps(ua, jnp.where(first, 0.0, hp_ref[0, 0]), rows)
        s1b, s2b = _conv_taps(ub, jnp.where(first, 0.0, hp_ref[1, 0]), rows)
        cwa = cwa_ref[0]
        cwb = cwb_ref[0]
        a = cwa[0:1] * s2a + cwa[1:2] * s1a + cwa[2:3] * ua + cba_ref[0]
        b = cwb[0:1] * s2b + cwb[1:2] * s1b + cwb[2:3] * ub + cbb_ref[0]
        sa = _sigmoid(a)
        da = dff * b * (sa * (1.0 + a * (1.0 - sa)))
        db = dff * (a * sa)

        def conv_bwd(dup, head8, cw):
            h0 = head8[0:1, :]
            h1 = head8[1:2, :]
            n1 = jnp.where(rows == tm - 1, h0, pltpu.roll(dup, tm - 1, 0))
            n2 = jnp.where(rows == tm - 2, h0, jnp.where(rows == tm - 1, h1, pltpu.roll(dup, tm - 2, 0)))
            return cw[2:3] * dup + cw[1:2] * n1 + cw[0:1] * n2

        dpa = conv_bwd(da, head_ref[cb, 0], cwa).astype(MXU)
        dpb = conv_bwd(db, head_ref[cb, 1], cwb).astype(MXU)
        head_ref[cb, 0] = da[0:8, :]
        head_ref[cb, 1] = db[0:8, :]
        dup_ref[0, 0] = dpa
        dup_ref[1, 0] = dpb
        for slot, dup, s2, s1, u in ((cb, da, s2a, s1a, ua), (ncb + cb, db, s2b, s1b, ub)):
            dconv_ref[slot, 0:1, :] += _rowsum(dup * s2)
            dconv_ref[slot, 1:2, :] += _rowsum(dup * s1)
            dconv_ref[slot, 2:3, :] += _rowsum(dup * u)
            dconv_ref[slot, 3:4, :] += _rowsum(dup)
        contrib = _dot(dpa, wa_ref[0]) + _dot(dpb, wb_ref[0])

        @pl.when(cb == 0)
        def _():
            acc_ref[...] = contrib

        @pl.when(cb > 0)
        def _():
            acc_ref[...] += contrib

        @pl.when(cb == ncb - 1)
        def _():
            x1v = x1_ref[...]
            r = _rms(x1v)
            xn = x1v * r
            dh2 = acc_ref[...]
            dg_ref[...] += _rowsum(dh2 * xn)
            dx1 = dx2v + _rms_bwd(dh2 * g_ref[...], xn, r)
            dx1_ref[...] = dx1
            dx1b_ref[...] = dx1.astype(MXU)

    row = lambda n: pl.BlockSpec((tm, n), lambda i, c: (nt - 1 - i, 0))
    colb = lambda: pl.BlockSpec((2, 1, tm, FF_CW), lambda i, c: (0, c, nt - 1 - i, 0))
    halo = lambda: pl.BlockSpec((2, 1, 8, FF_CW), lambda i, c: (0, c, jnp.maximum((nt - 1 - i) * hb - 1, 0), 0))
    gate = lambda r: pl.BlockSpec((1, r, FF_CW), lambda i, c: (c, 0, 0))
    lin = lambda r: pl.BlockSpec((1, r, FF_CW), lambda i, c: (ncb + c, 0, 0))
    return pl.pallas_call(
        body, name="ffn_bwd", grid=(nt, ncb),
        in_specs=[row(D_MODEL), colb(), halo(), gate(3), lin(3), gate(1), lin(1),
                  pl.BlockSpec((FF_CW, D_MODEL), lambda i, c: (c, 0)),
                  pl.BlockSpec((1, FF_CW, D_MODEL), lambda i, c: (c, 0, 0)),
                  pl.BlockSpec((1, FF_CW, D_MODEL), lambda i, c: (ncb + c, 0, 0)),
                  row(D_MODEL), _full((1, D_MODEL))],
        out_specs=[colb(), row(D_MODEL), row(D_MODEL), _full((2 * ncb, 8, FF_CW)), _full((1, D_MODEL))],
        out_shape=[_sds((2, ncb, S, FF_CW), MXU), _sds((S, D_MODEL)), _sds((S, D_MODEL), MXU), _sds((2 * ncb, 8, FF_CW)),
                   _sds((1, D_MODEL))],
        scratch_shapes=[pltpu.VMEM((tm, D_MODEL), F32), pltpu.VMEM((ncb, 2, 8, FF_CW), F32)],
        compiler_params=_cp("arbitrary", "arbitrary"),
    )(dx2, up, up, conv_w, conv_w, conv_b, conv_b, w_down, w_up, w_up, x1, g_ffn)


def _mix_bwd(dx1, gl, ya, yb, ys, uv, w_out, w_pa, w_pb, w_glu, b_glu, g_sgu, ws, ws_t, bias_s, tm):
    S = dx1.shape[0]

    def body(dx1_ref, gl_ref, ya_ref, yb_ref, ys_ref, uv_ref, wout_ref, wpa_ref, wpb_ref, wglu_ref, bglu_ref, gs_ref,
             ws_ref, wst_ref, bias_ref,
             dgl_ref, dya_ref, dyb_ref, dz_ref, dys_ref, duv_ref, dbglu_ref, dgs_ref, dws_ref, dbs_ref,
             du2_ref, dvn_ref):
        i = pl.program_id(0)

        @pl.when(i == 0)
        def _():
            dbglu_ref[...] = jnp.zeros_like(dbglu_ref)
            dgs_ref[...] = jnp.zeros_like(dgs_ref)
            dws_ref[...] = jnp.zeros_like(dws_ref)
            dbs_ref[...] = jnp.zeros_like(dbs_ref)

        dm = _dot_nt(dx1_ref[...].astype(MXU), wout_ref[...])
        glv = gl_ref[...]
        ga = _sigmoid(glv[:, :D_MODEL])
        gb = _sigmoid(glv[:, D_MODEL:])
        dgl_ref[:, :D_MODEL] = (dm * ya_ref[...] * ga * (1.0 - ga)).astype(MXU)
        dgl_ref[:, D_MODEL:] = (dm * yb_ref[...] * gb * (1.0 - gb)).astype(MXU)
        dyab = (dm * ga).astype(MXU)
        dybb = (dm * gb).astype(MXU)
        dya_ref[...] = dyab
        dyb_ref[...] = dybb

        dyap = _dot_nt(dyab, wpa_ref[...])
        yg, dgelu = _gelu_and_grad(ys_ref[...])
        sz = _sigmoid(_dot(yg.astype(MXU), wglu_ref[...]) + bglu_ref[...])
        dz = dyap * yg * sz * (1.0 - sz)
        dzb = dz.astype(MXU)
        dz_ref[...] = dzb
        dbglu_ref[...] += _rowsum(dz)
        dys_ref[...] = (dyap * sz + _dot_nt(dzb, wglu_ref[...])) * dgelu

        dsg = _dot_nt(dybb, wpb_ref[...])
        uvg, duvg = _gelu_and_grad(uv_ref[...])
        u2 = uvg[:, :SGU_W]
        v2 = uvg[:, SGU_W:]
        rv = _rms(v2)
        vhat = v2 * rv
        gs = gs_ref[...]
        vnb = (vhat * gs).astype(MXU)
        grp = lax.broadcasted_iota(jnp.int32, (CHUNK, SGU_W), 1) // SGU_D
        tril = (lax.broadcasted_iota(jnp.int32, (CHUNK, CHUNK), 0)
                >= lax.broadcasted_iota(jnp.int32, (CHUNK, CHUNK), 1))
        for c in range(tm // CHUNK):
            rs = slice(c * CHUNK, (c + 1) * CHUNK)
            vc = vnb[rs]
            mixed = _sgu_mix(vc, ws_ref, grp) + bias_ref[...]
            dsg_c = dsg[rs]
            du2_ref[rs, :] = dsg_c * mixed
            dmx = dsg_c * u2[rs]
            dbs_ref[...] += dmx
            dmb = dmx.astype(MXU)
            dvn_ref[rs, :] = _sgu_mix(dmb, wst_ref, grp)
            for g in range(SGU_G):
                part = _dot_nt(jnp.where(grp == g, dmb, jnp.zeros((), MXU)), vc)
                dws_ref[g] += jnp.where(tril, part, 0.0)
        dvn = dvn_ref[...]
        dgs_ref[...] += _rowsum(dvn * vhat)
        dv2 = _rms_bwd(dvn * gs, vhat, rv)
        duv_ref[:, :SGU_W] = (du2_ref[...] * duvg[:, :SGU_W]).astype(MXU)
        duv_ref[:, SGU_W:] = (dv2 * duvg[:, SGU_W:]).astype(MXU)

    row = lambda n: pl.BlockSpec((tm, n), lambda i: (i, 0))
    return pl.pallas_call(
        body, name="mix_bwd", grid=(S // tm,),
        in_specs=[row(D_MODEL), row(2 * D_MODEL), row(D_MODEL), row(D_MODEL), row(SSM_W), row(2 * SGU_W),
                  _full(w_out.shape), _full(w_pa.shape), _full(w_pb.shape), _full(w_glu.shape), _full(b_glu.shape),
                  _full(g_sgu.shape), _full(ws.shape), _full(ws_t.shape), _full(bias_s.shape)],
        out_specs=[row(2 * D_MODEL), row(D_MODEL), row(D_MODEL), row(SSM_W), row(SSM_W), row(2 * SGU_W),
                   _full((1, SSM_W)), _full((1, SGU_W)), _full((SGU_G, CHUNK, CHUNK)), _full((CHUNK, SGU_W))],
        out_shape=[_sds((S, 2 * D_MODEL), MXU), _sds((S, D_MODEL), MXU), _sds((S, D_MODEL), MXU), _sds((S, SSM_W), MXU),
                   _sds((S, SSM_W)), _sds((S, 2 * SGU_W), MXU),
                   _sds((1, SSM_W)), _sds((1, SGU_W)), _sds((SGU_G, CHUNK, CHUNK)), _sds((CHUNK, SGU_W))],
        scratch_shapes=[pltpu.VMEM((tm, SGU_W), F32), pltpu.VMEM((tm, SGU_W), F32)],
        compiler_params=_cp("arbitrary"),
    )(dx1, gl, ya, yb, ys, uv, w_out, w_pa, w_pb, w_glu, b_glu, g_sgu, ws, ws_t, bias_s)


def _s5_bwd(dys, us, st_re, st_im, abar_re, abar_im, b_re, b_im, c_re, c_im, d_skip, tm):
    S = us.shape[0]
    nt = S // tm
    w = 8 * SSM_P
    hb = tm // 8

    def body(dys_ref, us_ref, str_ref, sti_ref, hr_ref, hi_ref, ar_ref, ai_ref, br_ref, bi_ref, cr_ref, ci_ref, d_ref,
             dus_ref, dab_ref, dd_ref, dbr_ref, dbi_ref, dcr_ref, dci_ref, tab_ref, car_ref, gr_ref, gi_ref):
        i = pl.program_id(1)
        ri = nt - 1 - i

        @pl.when(i == 0)
        def _():
            car_ref[...] = jnp.zeros_like(car_ref)
            for k, t in enumerate(_scan_tables(ar_ref[...], -ai_ref[...], True)):
                tab_ref[k] = t
            for r in (dab_ref, dd_ref, dbr_ref, dbi_ref, dcr_ref, dci_ref):
                r[...] = jnp.zeros_like(r)

        dys_v = dys_ref[...]
        dyb = dys_v.astype(MXU)
        gr_ref[...] = _dot(dyb, cr_ref[0])
        gi_ref[...] = -_dot(dyb, ci_ref[0])

        def grp(kk, carry):
            r0 = pl.multiple_of((hb - 1 - kk) * 8, 8)
            xr, xi = _scan_group(gr_ref[pl.ds(r0, 8), :], gi_ref[pl.ds(r0, 8), :], tab_ref, carry[0], carry[1], True)
            gr_ref[pl.ds(r0, 8), :] = xr
            gi_ref[pl.ds(r0, 8), :] = xi
            return xr[0:1, :], xi[0:1, :]

        cr, ci = lax.fori_loop(0, hb, grp, (car_ref[0:1, :], car_ref[1:2, :]))
        car_ref[0:1, :] = cr
        car_ref[1:2, :] = ci

        gsr = gr_ref[...]
        gsi = gi_ref[...]
        sr = str_ref[...]
        si = sti_ref[...]
        rows = lax.broadcasted_iota(jnp.int32, (tm, w), 0)
        first = ri == 0
        spr = jnp.where(rows == 0, jnp.where(first, 0.0, hr_ref[7:8, :]), pltpu.roll(sr, 1, 0))
        spi = jnp.where(rows == 0, jnp.where(first, 0.0, hi_ref[7:8, :]), pltpu.roll(si, 1, 0))
        dab_ref[0, 0:1, :] += _rowsum(gsr * spr + gsi * spi)
        dab_ref[0, 1:2, :] += _rowsum(gsi * spr - gsr * spi)

        gbr = gsr.astype(MXU)
        gbi = gsi.astype(MXU)
        u = us_ref[...]
        ub = u.astype(MXU)
        dus_ref[...] = (_dot_nt(gbr, br_ref[0]) + _dot_nt(gbi, bi_ref[0]) + d_ref[...] * dys_v).astype(MXU)
        dd_ref[0, 0:1, :] += _rowsum(dys_v * u)
        dbr_ref[0] += _dot_tn(ub, gbr)
        dbi_ref[0] += _dot_tn(ub, gbi)
        dcr_ref[0] += _dot_tn(dyb, sr.astype(MXU))
        dci_ref[0] -= _dot_tn(dyb, si.astype(MXU))

    blk = lambda: pl.BlockSpec((1, 8 * SSM_H, w), lambda j, i: (j, 0, 0))
    rowl = lambda: pl.BlockSpec((tm, LANES), lambda j, i: (nt - 1 - i, j))
    roww = lambda: pl.BlockSpec((tm, w), lambda j, i: (nt - 1 - i, j))
    halo = lambda: pl.BlockSpec((8, w), lambda j, i: (jnp.maximum((nt - 1 - i) * hb - 1, 0), j))
    return pl.pallas_call(
        body, name="s5_bwd", grid=(SSM_BLK, nt),
        in_specs=[rowl(), rowl(), roww(), roww(), halo(), halo(),
                  pl.BlockSpec((1, w), lambda j, i: (0, j)), pl.BlockSpec((1, w), lambda j, i: (0, j)),
                  blk(), blk(), blk(), blk(),
                  pl.BlockSpec((1, LANES), lambda j, i: (0, j))],
        out_specs=[rowl(),
                   pl.BlockSpec((1, 8, w), lambda j, i: (j, 0, 0)), pl.BlockSpec((1, 8, LANES), lambda j, i: (j, 0, 0)),
                   blk(), blk(), blk(), blk()],
        out_shape=[_sds((S, SSM_W), MXU), _sds((SSM_BLK, 8, w)), _sds((SSM_BLK, 8, LANES)),
                   _sds((SSM_BLK, 8 * SSM_H, w)), _sds((SSM_BLK, 8 * SSM_H, w)),
                   _sds((SSM_BLK, 8 * SSM_H, w)), _sds((SSM_BLK, 8 * SSM_H, w))],
        scratch_shapes=[pltpu.VMEM((8, 8, w), F32), pltpu.VMEM((8, w), F32),
                        pltpu.VMEM((tm, w), F32), pltpu.VMEM((tm, w), F32)],
        compiler_params=_cp("parallel", "arbitrary"),
    )(dys, us, st_re, st_im, st_re, st_im, abar_re, abar_im, b_re, b_im, c_re, c_im, d_skip)


def _in_bwd(dus, duv, dgl, dx1, x, g_mix, w_in, tm):
    S = x.shape[0]

    def body(dus_ref, duv_ref, dgl_ref, dx1_ref, x_ref, g_ref, w_ref, gx_ref, dg_ref):
        @pl.when(pl.program_id(0) == 0)
        def _():
            dg_ref[...] = jnp.zeros_like(dg_ref)

        dh = (_dot(dus_ref[...], w_ref[0:SSM_W, :])
              + _dot(duv_ref[...], w_ref[SSM_W:SSM_W + 2 * SGU_W, :])
              + _dot(dgl_ref[...], w_ref[SSM_W + 2 * SGU_W:, :]))
        xv = x_ref[...]
        r = _rms(xv)
        xn = xv * r
        dg_ref[...] += _rowsum(dh * xn)
        gx_ref[...] = dx1_ref[...] + _rms_bwd(dh * g_ref[...], xn, r)

    row = lambda n: pl.BlockSpec((tm, n), lambda i: (i, 0))
    return pl.pallas_call(
        body, name="in_bwd", grid=(S // tm,),
        in_specs=[row(SSM_W), row(2 * SGU_W), row(2 * D_MODEL), row(D_MODEL), row(D_MODEL), _full((1, D_MODEL)),
                  _full(w_in.shape)],
        out_specs=[row(D_MODEL), _full((1, D_MODEL))],
        out_shape=[_sds((S, D_MODEL)), _sds((1, D_MODEL))],
        compiler_params=_cp("arbitrary"),
    )(dus, duv, dgl, dx1, x, g_mix, w_in)


def _pick(n, cands):
    for c in cands:
        if n % c == 0:
            return c
    return n


def _wgrad_split(a, b, nsplit, tk, name):
    S, K = a.shape
    N = b.shape[1]
    c = N // nsplit

    def body(a_ref, b_ref, o_ref):
        prod = _dot_tn(a_ref[...], b_ref[...])
        for d in range(nsplit):
            o_ref[d] = prod[:, c * d:c * (d + 1)].astype(MXU)

    return pl.pallas_call(
        body, name=name, grid=(K // tk,),
        in_specs=[pl.BlockSpec((S, tk), lambda k: (0, k)), _full((S, N))],
        out_specs=pl.BlockSpec((nsplit, tk, c), lambda k: (0, k, 0)),
        out_shape=_sds((nsplit, K, c), MXU),
        compiler_params=_cp("parallel"),
    )(a, b)


def _wgrad_in_t(dps, h1, name):
    S, K = h1.shape
    cw = 512
    counts = [b.shape[1] // cw for b in dps]
    starts = [sum(counts[:i]) for i in range(len(dps))]
    nblk = sum(counts)

    def body(*refs):
        b_refs = refs[:len(dps)]
        h_ref, o_ref = refs[len(dps):]
        j = pl.program_id(0)
        for b_ref, st, cnt in zip(b_refs, starts, counts):
            @pl.when(jnp.logical_and(j >= st, j < st + cnt))
            def _():
                o_ref[...] = _dot_tn(b_ref[...], h_ref[...]).astype(MXU)

    def src_spec(st, cnt):
        return pl.BlockSpec((S, cw), lambda j: (0, jnp.clip(j - st, 0, cnt - 1)))

    return pl.pallas_call(
        body, name=name, grid=(nblk,),
        in_specs=[src_spec(st, cnt) for st, cnt in zip(starts, counts)] + [_full((S, K))],
        out_specs=pl.BlockSpec((cw, K), lambda j: (j, 0)),
        out_shape=_sds((nblk * cw, K), MXU),
        compiler_params=_cp("arbitrary"),
    )(*dps, h1)


def _wgrad_blk(a3, b3, nblk, a_of, b_of, name):
    S, K = a3.shape[1:]
    N = b3.shape[2]

    def body(a_ref, b_ref, o_ref):
        o_ref[0] = _dot_tn(a_ref[0], b_ref[0]).astype(MXU)

    return pl.pallas_call(
        body, name=name, grid=(nblk,),
        in_specs=[pl.BlockSpec((1, S, K), lambda b: (a_of(b), 0, 0)),
                  pl.BlockSpec((1, S, N), lambda b: (b_of(b), 0, 0))],
        out_specs=pl.BlockSpec((1, K, N), lambda b: (b, 0, 0)),
        out_shape=_sds((nblk, K, N), MXU),
        compiler_params=_cp("parallel"),
    )(a3, b3)


def _assemble_cols(blocks_list, name):
    def body(*refs):
        n = len(blocks_list)
        for b_ref, o_ref in zip(refs[:n], refs[n:]):
            c = b_ref.shape[2]
            for d in range(N_DEV):
                o_ref[:, c * d:c * (d + 1)] = b_ref[d]

    return pl.pallas_call(
        body, name=name,
        out_shape=[_sds((b.shape[1], N_DEV * b.shape[2]), b.dtype) for b in blocks_list],
        compiler_params=pltpu.CompilerParams(vmem_limit_bytes=VMEM_LIMIT),
    )(*blocks_list)


def _tile(S, want):
    return want if S % want == 0 else S


def _local_step(x, tgt, p, ffn_weights, grads_out):
    S = x.shape[0]
    tm = _tile(S, 256)
    tl = _tile(S, 512)

    rep = lambda a: jnp.repeat(a, SSM_H, axis=0)
    are = rep(p["a_re"])
    aim = rep(p["a_im"])
    ldt = jnp.broadcast_to(rep(p["log_dt"].reshape(SSM_G, 1)), are.shape)
    br_t = p["b_re_t"].reshape(are.shape)
    bi_t = p["b_im_t"].reshape(are.shape)
    abr, abi, bbr, bbi = _s5_params_fwd(are, aim, ldt, br_t, bi_t)
    head = lambda a: a.reshape(SSM_G, SSM_H, SSM_P)[:, 0, :].reshape(1, SSM_G * SSM_P)
    abar_re, abar_im = head(abr), head(abi)
    bd_br = _blockdiag(bbr).astype(MXU)
    bd_bi = _blockdiag(bbi).astype(MXU)
    bd_cr = _blockdiag(p["c_re"].reshape(are.shape)).astype(MXU)
    bd_ci = _blockdiag(p["c_im"].reshape(are.shape)).astype(MXU)
    d_skip = p["d_skip"].reshape(1, SSM_W)

    tril = jnp.tril(jnp.ones((CHUNK, CHUNK), dtype=bool))
    ws = jnp.where(tril[None], p["w_s"], 0.0)
    ws_b = ws.astype(MXU)
    ws_t = ws.transpose(0, 2, 1).astype(MXU)
    bias_s = jnp.repeat(p["b_s"].T, SGU_D, axis=1)

    g_mix = p["g_mix"].reshape(1, D_MODEL)
    g_ffn = p["g_ffn"].reshape(1, D_MODEL)
    g_final = p["g_final"].reshape(1, D_MODEL)
    g_sgu = p["g_sgu"].reshape(1, SGU_W)
    b_glu = p["b_glu"].reshape(1, SSM_W)
    conv_b = p["conv_b"].reshape(N_DEV, 1, FF_CW)

    h1, us, uv, gl = _in_fwd(x, g_mix, p["w_in_t"], tm)
    st_re, st_im, ys = _s5_fwd(us, abar_re, abar_im, bd_br, bd_bi, bd_cr, bd_ci, d_skip, tl)
    yg, yap, sg, ya, yb, m, x1, h2 = _mix_fwd(x, ys, uv, gl, p["w_glu"], b_glu, p["w_proj_a"], g_sgu, ws_b, bias_s,
                                              p["w_proj_b"], p["w_out"], g_ffn, tm)
    w_up, conv_w, w_down = ffn_weights(h2)
    up, ff, dx2, dx2b, loss, dg_final = _ffn_fwd(h2, x1, tgt, w_up, conv_w, conv_b, w_down, g_final, tl)

    dup, dx1, dx1b, dconv, dg_ffn = _ffn_bwd(dx2, up, x1, w_up, conv_w, conv_b, w_down, g_ffn, tl)
    rows8 = lambda g: g.reshape(N_DEV, g.shape[1] // N_DEV, g.shape[2])
    g_up = _wgrad_blk(dup.reshape(N_DEV, S, FF_CW), h2[None], N_DEV, lambda b: b, lambda b: 0, "wgrad_up")
    g_down = _wgrad_blk(ff, dx2b[None], FF_NCB, lambda b: b, lambda b: 0, "wgrad_down").reshape(
        N_DEV, D_FF // N_DEV, D_MODEL)
    token = grads_out(("w_up", "w_down"), (g_up, g_down))
    dgl, dya, dyb, dz, dys, duv, db_glu, dg_sgu, dws, dbs = _mix_bwd(
        dx1, gl, ya, yb, ys, uv, p["w_out"], p["w_proj_a"], p["w_proj_b"], p["w_glu"], b_glu + token[0:1, 0:1], g_sgu,
        ws_b, ws_t, bias_s, tm)
    token = grads_out(("w_glu", "w_proj_a", "w_proj_b", "w_out"),
                      (rows8(_wgrad_split(yg, dz, 1, SSM_W, "wgrad_glu")),
                       _wgrad_split(yap, dya, N_DEV, SSM_W, "wgrad_pa"),
                       _wgrad_split(sg, dyb, N_DEV, SGU_W, "wgrad_pb"),
                       rows8(_wgrad_split(m, dx1b, 1, 512, "wgrad_out"))))
    dus, dab, dd, dbbr, dbbi, dcr, dci = _s5_bwd(dys, us, st_re, st_im, abar_re, abar_im, bd_br, bd_bi, bd_cr, bd_ci,
                                                 d_skip + token[0:1, 0:1], tl)
    g_in = _wgrad_in_t([dus, duv, dgl], h1, "wgrad_in")
    token = grads_out(("w_in",), (g_in.reshape(N_DEV, g_in.shape[0] // N_DEV, D_MODEL),))
    grad_x, dg_mix = _in_bwd(dus, duv, dgl, dx1, x, g_mix + token[0:1, 0:1], p["w_in_t"], tm)

    spread = lambda v: jnp.repeat(v.reshape(SSM_G, SSM_P), SSM_H, axis=0) * (1.0 / SSM_H)
    dabr = spread(dab[:, 0, :])
    dabi = spread(dab[:, 1, :])
    dare, daim, dldt, dbr_t, dbi_t = _s5_params_bwd(are, aim, ldt, br_t, bi_t, dabr, dabi,
                                                    _unblockdiag(dbbr), _unblockdiag(dbbi))
    fold = lambda a: a.reshape(SSM_G, SSM_H, SSM_P).sum(axis=1)

    grads = {
        "g_mix": dg_mix,
        "a_re": fold(dare), "a_im": fold(daim), "log_dt": fold(dldt).sum(axis=1),
        "b_re": dbr_t, "b_im": dbi_t,
        "c_re": _unblockdiag(dcr).reshape(SSM_G, SSM_H, SSM_P),
        "c_im": _unblockdiag(dci).reshape(SSM_G, SSM_H, SSM_P),
        "d_skip": dd[:, 0, :].reshape(SSM_W),
        "b_glu": db_glu,
        "g_sgu": dg_sgu,
        "w_s": dws,
        "b_s": dbs.reshape(CHUNK, SGU_G, SGU_D).sum(axis=-1).T,
        "g_ffn": dg_ffn,
        "conv_w": dconv[:, 0:3, :],
        "conv_b": dconv[:, 3, :].reshape(2 * D_FF),
        "g_final": dg_final,
    }
    return loss, grad_x, grads


_ANY = pl.BlockSpec(memory_space=pl.ANY)
_MESH = pl.DeviceIdType.MESH


def _allgather(shards, dtypes, name, cast_only=()):
    n = len(shards)
    e = len(cast_only)

    def body(*refs):
        in_refs, extra_in = refs[:n], refs[n:n + e]
        out_refs, extra_out = refs[n + e:2 * n + e], refs[2 * n + e:2 * n + 2 * e]
        stage = refs[2 * n + 2 * e:3 * n + 2 * e]
        send_sems, recv_sems, local_sems = refs[3 * n + 2 * e:]
        for a in range(n):
            stage[a][...] = in_refs[a][...].astype(dtypes[a])
        for i in range(e):
            extra_out[i][...] = extra_in[i][...].astype(MXU)
        x, y, c = lax.axis_index("x"), lax.axis_index("y"), lax.axis_index("c")
        me, sibling = (x, y, c), (x, y, 1 - c)
        chips = [(1 - x, y), (x, 1 - y), (1 - x, 1 - y)]

        def slot(a, px, py, pc):
            return out_refs[a].at[4 * px + 2 * py + pc]

        def copy(a, k, block, to, src=None):
            return pltpu.make_async_remote_copy(
                src_ref=slot(a, *block) if src is None else src, dst_ref=slot(a, *block),
                send_sem=send_sems.at[a, k], recv_sem=recv_sems.at[a, k], device_id=to, device_id_type=_MESH)

        mine = [pltpu.make_async_copy(stage[a], slot(a, *me), local_sems.at[a]) for a in range(n)]
        for cp in mine:
            cp.start()
        first = []
        for j, chip in enumerate(chips):
            first += [copy(a, 1 + j, me, (*chip, c), src=stage[a]) for a in range(n)]
        first += [copy(a, 0, me, sibling, src=stage[a]) for a in range(n)]
        for cp in first:
            cp.start()
        passed = []
        for j, chip in enumerate(chips):
            for a in range(n):
                copy(a, 1 + j, (*chip, c), me).wait_recv()
                fwd = copy(a, 4 + j, (*chip, c), sibling)
                fwd.start()
                passed.append(fwd)
        for a in range(n):
            copy(a, 0, sibling, me).wait_recv()
        for j, chip in enumerate(chips):
            for a in range(n):
                copy(a, 4 + j, (*chip, 1 - c), me).wait_recv()
        for cp in first + passed:
            cp.wait_send()
        for cp in mine:
            cp.wait()

    vmem = pl.BlockSpec(memory_space=pltpu.VMEM)
    res = pl.pallas_call(
        body, name=name, in_specs=[vmem] * (n + e), out_specs=[_ANY] * n + [vmem] * e,
        out_shape=[_sds((N_DEV,) + s.shape, dt) for s, dt in zip(shards, dtypes)]
                  + [_sds(s.shape, MXU) for s in cast_only],
        scratch_shapes=[pltpu.VMEM(s.shape, dt) for s, dt in zip(shards, dtypes)]
                       + [pltpu.SemaphoreType.DMA((n, 7)), pltpu.SemaphoreType.DMA((n, 7)), pltpu.SemaphoreType.DMA((n,))],
        compiler_params=pltpu.CompilerParams(vmem_limit_bytes=VMEM_LIMIT),
    )(*shards, *cast_only)
    return res[:n], res[n:]


def _all_to_all(sends, name):
    n = len(sends)

    def body(*refs):
        send_refs, recv_refs = refs[:n], refs[n:2 * n]
        send_sems, recv_sems, local_sems = refs[2 * n:]
        x, y, c = lax.axis_index("x"), lax.axis_index("y"), lax.axis_index("c")
        me = 4 * x + 2 * y + c
        mine = [pltpu.make_async_copy(send_refs[a].at[me], recv_refs[a].at[me], local_sems.at[a]) for a in range(n)]
        for cp in mine:
            cp.start()
        copies = []
        for k in (2, 4, 6, 3, 5, 7, 1):
            px = 1 - x if k & 4 else x
            py = 1 - y if k & 2 else y
            pc = 1 - c if k & 1 else c
            peer = 4 * px + 2 * py + pc
            for a in range(n):
                sems = dict(send_sem=send_sems.at[a, k - 1], recv_sem=recv_sems.at[a, k - 1],
                            device_id=(px, py, pc), device_id_type=_MESH)
                cp = pltpu.make_async_remote_copy(src_ref=send_refs[a].at[peer], dst_ref=recv_refs[a].at[me], **sems)
                cp.start()
                landing = pltpu.make_async_remote_copy(src_ref=send_refs[a].at[peer], dst_ref=recv_refs[a].at[peer],
                                                       **sems)
                copies.append((cp, landing))
        for _, landing in copies:
            landing.wait_recv()
        for cp, _ in copies:
            cp.wait_send()
        for cp in mine:
            cp.wait()

    return pl.pallas_call(
        body, name=name, in_specs=[_ANY] * n, out_specs=[_ANY] * n,
        out_shape=[_sds(s.shape, s.dtype) for s in sends],
        scratch_shapes=[pltpu.SemaphoreType.DMA((n, 7)), pltpu.SemaphoreType.DMA((n, 7)), pltpu.SemaphoreType.DMA((n,))],
    )(*sends)


_HBM = pl.BlockSpec(memory_space=pltpu.HBM)
_SEM = pl.BlockSpec(memory_space=pltpu.SEMAPHORE)
_EFFECT = pltpu.SideEffectType.DATAFLOW_SIDE_EFFECTING
_PEER_ORDER = (2, 4, 6, 3, 5, 7, 1)


def _peer(k):
    x, y, c = lax.axis_index("x"), lax.axis_index("y"), lax.axis_index("c")
    px = 1 - x if k & 4 else x
    py = 1 - y if k & 2 else y
    pc = 1 - c if k & 1 else c
    return (px, py, pc), 4 * px + 2 * py + pc


def _push_start(srcs, lands, slotted, name):
    n = len(srcs)

    def body(*refs):
        src_refs, land_refs = refs[:n], refs[n:2 * n]
        send_sems, recv_sems, token = refs[2 * n], refs[2 * n + 1], refs[-1]
        me = 4 * lax.axis_index("x") + 2 * lax.axis_index("y") + lax.axis_index("c")
        for k in _PEER_ORDER:
            dev, peer = _peer(k)
            for a in range(n):
                pltpu.make_async_remote_copy(
                    src_ref=src_refs[a].at[peer] if slotted else src_refs[a], dst_ref=land_refs[a].at[me],
                    send_sem=send_sems.at[7 * a + k - 1], recv_sem=recv_sems.at[7 * a + k - 1],
                    device_id=dev, device_id_type=_MESH).start()
        token[...] = jnp.zeros_like(token)

    bufs = list(srcs) + list(lands)
    res = pl.pallas_call(
        body, name=name, in_specs=[_HBM] * (2 * n),
        out_specs=(_SEM, _SEM, *[_HBM] * (2 * n), pl.BlockSpec(memory_space=pltpu.VMEM)),
        out_shape=(pltpu.SemaphoreType.DMA((7 * n,)), pltpu.SemaphoreType.DMA((7 * n,)),
                   *[pltpu.HBM(b.shape, b.dtype) for b in bufs], _sds((8, LANES))),
        input_output_aliases={i: 2 + i for i in range(2 * n)},
        compiler_params=pltpu.CompilerParams(has_side_effects=_EFFECT),
    )(*[pltpu.with_memory_space_constraint(b, pltpu.HBM) for b in bufs])
    return res[0], res[1], res[2:2 + n], res[2 + n:2 + 2 * n], res[-1]


def _push_wait(send_sems, recv_sems, srcs, lands, slotted, after, name):
    n = len(srcs)

    def body(*refs):
        src_refs, land_refs = refs[:n], refs[n:2 * n]
        send_sems, recv_sems = refs[2 * n], refs[2 * n + 1]
        for k in _PEER_ORDER:
            dev, peer = _peer(k)
            for a in range(n):
                cp = pltpu.make_async_remote_copy(
                    src_ref=src_refs[a].at[peer] if slotted else src_refs[a], dst_ref=land_refs[a].at[peer],
                    send_sem=send_sems.at[7 * a + k - 1], recv_sem=recv_sems.at[7 * a + k - 1],
                    device_id=dev, device_id_type=_MESH)
                cp.wait_send()
                cp.wait_recv()

    bufs = list(srcs) + list(lands)
    res = pl.pallas_call(
        body, name=name, in_specs=[_HBM] * (2 * n) + [_SEM, _SEM] + [_ANY] * len(after), out_specs=[_HBM] * (2 * n),
        out_shape=[pltpu.HBM(b.shape, b.dtype) for b in bufs],
        input_output_aliases={i: i for i in range(2 * n)},
        compiler_params=pltpu.CompilerParams(has_side_effects=_EFFECT),
    )(*bufs, send_sems, recv_sems, *after)
    return res[n:]


def _adamw(w, g, m, v):
    m2 = ADAM_B1 * m + (1.0 - ADAM_B1) * g
    v2 = ADAM_B2 * v + (1.0 - ADAM_B2) * (g * g)
    m_hat = m2 / (1.0 - ADAM_B1 ** ADAM_STEP)
    v_hat = v2 / (1.0 - ADAM_B2 ** ADAM_STEP)
    delta = -ADAM_LR * (m_hat / (jnp.sqrt(v_hat) + ADAM_EPS) + ADAM_WD * w)
    return delta, m2, v2


def _adam_shard(parts, w, m, v, name):
    _, r, c = w.shape
    tr = max(t for t in range(16, 257, 16) if r % t == 0)

    def body(p_ref, w_ref, m_ref, v_ref, g_ref, d_ref, m2_ref, v2_ref):
        g = p_ref[0].astype(F32)
        for s in range(1, N_DEV):
            g = g + p_ref[s].astype(F32)
        g_ref[0] = g
        d_ref[0], m2_ref[0], v2_ref[0] = _adamw(w_ref[0], g, m_ref[0], v_ref[0])

    row = lambda: pl.BlockSpec((1, tr, c), lambda i: (0, i, 0))
    return pl.pallas_call(
        body, name=name, grid=(r // tr,),
        in_specs=[pl.BlockSpec((N_DEV, tr, c), lambda i: (0, i, 0)), row(), row(), row()],
        out_specs=[row(), row(), row(), row()], out_shape=[_sds((1, r, c))] * 4,
        compiler_params=_cp("parallel"),
    )(parts, w, m, v)


def _adam_small(gs, ws, ms, vs, name):
    n = len(gs)

    def body(*refs):
        ins, outs = refs[:4 * n], refs[4 * n:]
        for i in range(n):
            g = ins[i][...]
            d, m2, v2 = _adamw(ins[n + i][...], g, ins[2 * n + i][...], ins[3 * n + i][...])
            outs[i][...] = d
            outs[n + i][...] = m2
            outs[2 * n + i][...] = v2

    res = pl.pallas_call(
        body, name=name, out_shape=[_sds(w.shape) for w in ws] * 3,
        compiler_params=pltpu.CompilerParams(vmem_limit_bytes=VMEM_LIMIT),
    )(*gs, *ws, *ms, *vs)
    return res[:n], res[n:2 * n], res[2 * n:]


def _sum_slots(parts, name):
    R = parts.shape[1]

    def body(p_ref, o_ref):
        g = p_ref[0]
        for s in range(1, N_DEV):
            g = g + p_ref[s]
        o_ref[...] = g

    return pl.pallas_call(body, name=name, out_shape=_sds((R, LANES)))(parts)


def _pad_to(a, n, axis):
    extra = n - a.shape[axis]
    if extra == 0:
        return a
    widths = [(0, 0)] * a.ndim
    widths[axis] = (0, extra)
    return jnp.pad(a, widths)


def _ceil_to(n, k):
    return -(-n // k) * k


def _pack_rows(flats, rows_multiple):
    parts = [_pad_to(f, _ceil_to(f.shape[-1], LANES), f.ndim - 1) for f in flats]
    cat = jnp.concatenate(parts, axis=-1)
    total = _ceil_to(cat.shape[-1], LANES * rows_multiple)
    cat = _pad_to(cat, total, cat.ndim - 1)
    return cat.reshape(cat.shape[:-1] + (total // LANES, LANES))


def _unpack_rows(buf, sizes):
    flat = buf.reshape(buf.shape[:-2] + (-1,))
    out, off = [], 0
    for n in sizes:
        out.append(flat[..., off:off + n])
        off += _ceil_to(n, LANES)
    return out


_MIX_BIG = ("w_in", "w_glu", "w_proj_a", "w_proj_b", "w_out")
_BIG = _MIX_BIG + ("w_up", "w_down")
_SMALL = ("g_mix", "a_re", "a_im", "log_dt", "b_re", "b_im", "c_re", "c_im", "d_skip", "b_glu", "g_sgu", "w_s", "b_s",
          "g_ffn", "conv_b", "g_final")
_SMALL_ROWS_MULTIPLE = 8 * N_DEV
_TRANSPOSED = ("w_in", "w_up", "b_re", "b_im")


def _as_2d(a):
    return a.reshape(-1, a.shape[-1]) if a.ndim > 1 else a.reshape(1, -1)


def kernel(x, g_mix, w_in, a_re, a_im, log_dt, b_re, b_im, c_re, c_im, d_skip, w_glu, b_glu, w_proj_a, g_sgu, w_s, b_s, w_proj_b, w_out, g_ffn, w_up, conv_w, conv_b, w_down, g_final, loss_target, m_g_mix, m_w_in, m_a_re, m_a_im, m_log_dt, m_b_re, m_b_im, m_c_re, m_c_im, m_d_skip, m_w_glu, m_b_glu, m_w_proj_a, m_g_sgu, m_w_s, m_b_s, m_w_proj_b, m_w_out, m_g_ffn, m_w_up, m_conv_w, m_conv_b, m_w_down, m_g_final, v_g_mix, v_w_in, v_a_re, v_a_im, v_log_dt, v_b_re, v_b_im, v_c_re, v_c_im, v_d_skip, v_w_glu, v_b_glu, v_w_proj_a, v_g_sgu, v_w_s, v_b_s, v_w_proj_b, v_w_out, v_g_ffn, v_w_up, v_conv_w, v_conv_b, v_w_down, v_g_final):
    args = dict(locals())
    me = 4 * lax.axis_index("x") + 2 * lax.axis_index("y") + lax.axis_index("c")

    def own_slot(buf, block):
        return lax.dynamic_update_slice(buf, block[None], (me,) + (0,) * block.ndim)

    for n in _TRANSPOSED:
        for pre in ("", "m_", "v_"):
            args[pre + n] = jnp.swapaxes(args[pre + n], -1, -2)
    gathered, (up_sh, down_sh) = _allgather([args[n][0] for n in _MIX_BIG], [MXU] * len(_MIX_BIG), "allgather_mixer",
                                            cast_only=(args["w_up"][0], w_down[0]))
    g = dict(zip(_MIX_BIG, gathered))
    ffn_srcs = [up_sh, down_sh, conv_w[0]]
    ffn_lands = [own_slot(lax.empty((N_DEV,) + s.shape, s.dtype), s) for s in ffn_srcs]
    ag_send, ag_recv, ffn_srcs, ffn_lands, ag_token = _push_start(ffn_srcs, ffn_lands, False, "push_ffn_weights")
    w_pa_full, w_pb_full = _assemble_cols([g["w_proj_a"], g["w_proj_b"]], "assemble_cols")
    p = {n: (args[n][0] if n != "g_final" else args[n]) for n in _SMALL if n not in _TRANSPOSED}
    p.update(w_in_t=g["w_in"].reshape(SSM_W + 2 * SGU_W + 2 * D_MODEL, D_MODEL), w_proj_a=w_pa_full, w_proj_b=w_pb_full,
             w_glu=g["w_glu"].reshape(SSM_W, SSM_W), w_out=g["w_out"].reshape(D_MODEL, D_MODEL),
             b_re_t=args["b_re"][0], b_im_t=args["b_im"][0])
    p["g_mix"] = p["g_mix"] + ag_token[0:1, 0:1]

    def ffn_weights(after):
        w_up_g, w_down_g, conv_w_g = _push_wait(ag_send, ag_recv, ffn_srcs, ffn_lands, False, [after], "wait_ffn_weights")
        return w_up_g, conv_w_g, w_down_g.reshape(D_FF, D_MODEL)

    pushes = []

    def grads_out(names, sends):
        lands = [own_slot(lax.empty(s.shape, s.dtype), lax.dynamic_index_in_dim(s, me, 0, keepdims=False))
                 for s in sends]
        send_sems, recv_sems, srcs, lands, token = _push_start(list(sends), lands, True, "push_grads_" + names[0])
        pushes.append((names, send_sems, recv_sems, srcs, lands))
        return token

    loss_part, grad_x, grads = _local_step(x[0], loss_target[0], p, ffn_weights, grads_out)

    small_names = _SMALL + ("conv_w", "loss")
    small_g = dict(grads, loss=loss_part[0, 0:1])
    flats = [small_g[n].reshape(-1) for n in small_names]
    small_sizes = [f.shape[0] for f in flats]
    g_small = _pack_rows(flats, _SMALL_ROWS_MULTIPLE)
    rs8 = g_small.shape[0] // N_DEV
    recv_small, = _all_to_all([g_small.reshape(N_DEV, rs8, LANES)], "all_to_all_small")
    small_mine = _sum_slots(recv_small, "sum_small")
    g_small_all = _allgather([small_mine], [F32], "allgather_small")[0][0].reshape(N_DEV * rs8, LANES)
    pieces = dict(zip(small_names, _unpack_rows(g_small_all, small_sizes)))
    loss = pieces["loss"][0]
    dconv_w = lax.dynamic_index_in_dim(pieces["conv_w"].reshape(N_DEV, 3, FF_CW), me, axis=0, keepdims=False)

    out = {}
    done = [g_small_all]
    for names, send_sems, recv_sems, srcs, lands in pushes:
        parts = _push_wait(send_sems, recv_sems, srcs, lands, True, done, "wait_grads_" + names[0])
        for n, part in zip(names, parts):
            res = _adam_shard(part, args[n], args["m_" + n], args["v_" + n], "adam_" + n)
            for kind, v in zip(("grad_", "delta_", "new_m_", "new_v_"), res):
                out[kind + n] = v
            done = [res[0]]
    names2 = _SMALL + ("conv_w",)
    gs = [pieces[n].reshape(_as_2d(args[n]).shape) for n in _SMALL] + [dconv_w]
    ds, m2s, v2s = _adam_small(gs, [_as_2d(args[n]) for n in names2], [_as_2d(args["m_" + n]) for n in names2],
                               [_as_2d(args["v_" + n]) for n in names2], "adam_small")
    for n, res in zip(names2, zip(gs, ds, m2s, v2s)):
        for kind, v in zip(("grad_", "delta_", "new_m_", "new_v_"), res):
            out[kind + n] = v.reshape(args[n].shape)
    order = ("g_mix", "w_in", "a_re", "a_im", "log_dt", "b_re", "b_im", "c_re", "c_im", "d_skip", "w_glu", "b_glu",
             "w_proj_a", "g_sgu", "w_s", "b_s", "w_proj_b", "w_out", "g_ffn", "w_up", "conv_w", "conv_b", "w_down",
             "g_final")
    res = [loss, grad_x.reshape(x.shape)]
    for kind in ("grad_", "delta_", "new_m_", "new_v_"):
        res += [jnp.swapaxes(out[kind + n], -1, -2) if n in _TRANSPOSED else out[kind + n] for n in order]
    return tuple(res)
```

```python
import functools
import math

import jax
import jax.numpy as jnp
from jax import lax
from jax.experimental import pallas as pl
from jax.experimental.pallas import tpu as pltpu

F32 = jnp.float32
MXU = jnp.bfloat16
EPS = 1e-6

D_MODEL = 1024
SSM_W = 512
SSM_G, SSM_H, SSM_P = 32, 16, 64
SSM_BLK = 4
SGU_W = 512
SGU_G, SGU_D, CHUNK = 8, 64, 128
D_FF = 2816
N_DEV = 8
FF_CW = 2 * D_FF // N_DEV
FF_NCB = D_FF // FF_CW
LANES = 128

ADAM_LR, ADAM_B1, ADAM_B2, ADAM_EPS, ADAM_WD, ADAM_STEP = 0.001, 0.9, 0.999, 1e-08, 0.01, 10

VMEM_LIMIT = 48 * 1024 * 1024


def _cp(*sem):
    return pltpu.CompilerParams(dimension_semantics=sem, vmem_limit_bytes=VMEM_LIMIT)


def _full(shape):
    n = len(shape)
    return pl.BlockSpec(shape, lambda *_: (0,) * n)


def _sds(shape, dtype=F32):
    return jax.ShapeDtypeStruct(shape, dtype)


def _dot(a, b):
    return jnp.dot(a, b, preferred_element_type=F32)


def _dot_nt(a, b):
    return lax.dot_general(a, b, (((1,), (1,)), ((), ())), preferred_element_type=F32)


def _dot_tn(a, b):
    return lax.dot_general(a, b, (((0,), (0,)), ((), ())), preferred_element_type=F32)


_GELU_C = math.sqrt(2.0 / math.pi)


def _gelu(x):
    return 0.5 * x * (1.0 + jnp.tanh(_GELU_C * (x + 0.044715 * (x * x * x))))


def _gelu_and_grad(x):
    t = jnp.tanh(_GELU_C * (x + 0.044715 * (x * x * x)))
    g = 0.5 * x * (1.0 + t)
    dg = 0.5 * (1.0 + t) + 0.5 * x * (1.0 - t * t) * (_GELU_C * (1.0 + 3.0 * 0.044715 * (x * x)))
    return g, dg


def _sigmoid(x):
    return 1.0 / (1.0 + jnp.exp(-x))


def _rms(x):
    return lax.rsqrt(jnp.mean(x * x, axis=-1, keepdims=True) + EPS)


def _rms_bwd(dxn, xn, r):
    return r * (dxn - xn * jnp.mean(dxn * xn, axis=-1, keepdims=True))


def _rowsum(x):
    return jnp.sum(x, axis=0, keepdims=True)


def _s5_disc(are, aim, ldt, br, bi):
    dt = jnp.exp(ldt)
    mag = jnp.exp(dt * are)
    abr = mag * jnp.cos(dt * aim)
    abi = mag * jnp.sin(dt * aim)
    den = are * are + aim * aim
    nr = abr - 1.0
    ni = abi
    fr = (nr * are + ni * aim) / den
    fi = (ni * are - nr * aim) / den
    return abr, abi, fr * br - fi * bi, fr * bi + fi * br


def _s5_params_fwd(are, aim, ldt, br, bi):
    def body(are_ref, aim_ref, ldt_ref, br_ref, bi_ref, o0, o1, o2, o3):
        outs = _s5_disc(are_ref[...], aim_ref[...], ldt_ref[...], br_ref[...], bi_ref[...])
        for o, v in zip((o0, o1, o2, o3), outs):
            o[...] = v
    shp = are.shape
    return pl.pallas_call(body, name="s5_params_fwd", out_shape=[_sds(shp)] * 4)(are, aim, ldt, br, bi)


def _s5_params_bwd(are, aim, ldt, br, bi, dabr, dabi, dbr, dbi):
    def body(are_ref, aim_ref, ldt_ref, br_ref, bi_ref, c0, c1, c2, c3, o0, o1, o2, o3, o4):
        prim = (are_ref[...], aim_ref[...], ldt_ref[...], br_ref[...], bi_ref[...])
        _, vjp = jax.vjp(_s5_disc, *prim)
        outs = vjp((c0[...], c1[...], c2[...], c3[...]))
        for o, v in zip((o0, o1, o2, o3, o4), outs):
            o[...] = v
    shp = are.shape
    return pl.pallas_call(body, name="s5_params_bwd", out_shape=[_sds(shp)] * 5)(
        are, aim, ldt, br, bi, dabr, dabi, dbr, dbi)


def _blockdiag(m_t):
    m = m_t.reshape(SSM_BLK, 8, SSM_H, 1, SSM_P)
    eye = jnp.eye(8, dtype=bool).reshape(1, 8, 1, 8, 1)
    return jnp.where(eye, m, jnp.zeros((), m_t.dtype)).reshape(SSM_BLK, 8 * SSM_H, 8 * SSM_P)


def _unblockdiag(pc):
    m = pc.reshape(SSM_BLK, 8, SSM_H, 8, SSM_P)
    return jnp.einsum("jghgp->jghp", m).reshape(SSM_G * SSM_H, SSM_P)


def _in_fwd(x, g_mix, w_in_t, tm):
    S = x.shape[0]

    def body(x_ref, g_ref, w_ref, h_ref, us_ref, uv_ref, gl_ref):
        xv = x_ref[...]
        h = (xv * _rms(xv) * g_ref[...]).astype(MXU)
        h_ref[...] = h
        us_ref[...] = _dot_nt(h, w_ref[0:SSM_W, :])
        uv_ref[...] = _dot_nt(h, w_ref[SSM_W:SSM_W + 2 * SGU_W, :])
        gl_ref[...] = _dot_nt(h, w_ref[SSM_W + 2 * SGU_W:, :])

    row = lambda n: pl.BlockSpec((tm, n), lambda i: (i, 0))
    return pl.pallas_call(
        body, name="in_fwd", grid=(S // tm,),
        in_specs=[row(D_MODEL), _full((1, D_MODEL)), _full(w_in_t.shape)],
        out_specs=[row(D_MODEL), row(SSM_W), row(2 * SGU_W), row(2 * D_MODEL)],
        out_shape=[_sds((S, D_MODEL), MXU), _sds((S, SSM_W)), _sds((S, 2 * SGU_W)), _sds((S, 2 * D_MODEL))],
        compiler_params=_cp("parallel"),
    )(x, g_mix, w_in_t)


def _scan_tables(ar, ai, reverse):
    n = ar.shape[-1]
    def mul(p, q):
        return p[0] * q[0] - p[1] * q[1], p[0] * q[1] + p[1] * q[0]
    a1 = (ar, ai)
    a2 = mul(a1, a1)
    a3 = mul(a2, a1)
    a4 = mul(a2, a2)
    a5 = mul(a4, a1)
    a6 = mul(a4, a2)
    a7 = mul(a4, a3)
    a8 = mul(a4, a4)
    pw = (a1, a2, a3, a4, a5, a6, a7, a8)
    rows = lax.broadcasted_iota(jnp.int32, (8, n), 0)
    tabs = []
    for s, a in ((1, a1), (2, a2), (4, a4)):
        keep = (rows + s <= 7) if reverse else (rows >= s)
        for comp in a:
            tabs.append(jnp.where(keep, jnp.broadcast_to(comp, (8, n)), 0.0))
    for c in range(2):
        q = jnp.zeros((8, n), F32)
        for r in range(8):
            e = (8 - r) if reverse else (r + 1)
            q = jnp.where(rows == r, jnp.broadcast_to(pw[e - 1][c], (8, n)), q)
        tabs.append(q)
    return tabs


def _scan_group(xr, xi, tab_ref, cr, ci, reverse):
    for t, s in enumerate((1, 2, 4)):
        pr = tab_ref[2 * t]
        pi = tab_ref[2 * t + 1]
        sh = (8 - s) if reverse else s
        sr = pltpu.roll(xr, sh, 0)
        si = pltpu.roll(xi, sh, 0)
        xr, xi = xr + pr * sr - pi * si, xi + pr * si + pi * sr
    qr = tab_ref[6]
    qi = tab_ref[7]
    return xr + qr * cr - qi * ci, xi + qr * ci + qi * cr


def _s5_fwd(us, abar_re, abar_im, b_re, b_im, c_re, c_im, d_skip, tm):
    S = us.shape[0]
    nt = S // tm
    w = 8 * SSM_P

    def body(us_ref, ar_ref, ai_ref, br_ref, bi_ref, cr_ref, ci_ref, d_ref, str_ref, sti_ref, ys_ref, tab_ref, car_ref):
        i = pl.program_id(1)

        @pl.when(i == 0)
        def _():
            car_ref[...] = jnp.zeros_like(car_ref)
            for k, t in enumerate(_scan_tables(ar_ref[...], ai_ref[...], False)):
                tab_ref[k] = t

        u = us_ref[...]
        ub = u.astype(MXU)
        str_ref[...] = _dot(ub, br_ref[0])
        sti_ref[...] = _dot(ub, bi_ref[0])

        def grp(k, carry):
            r0 = pl.multiple_of(k * 8, 8)
            xr, xi = _scan_group(str_ref[pl.ds(r0, 8), :], sti_ref[pl.ds(r0, 8), :], tab_ref, carry[0], carry[1], False)
            str_ref[pl.ds(r0, 8), :] = xr
            sti_ref[pl.ds(r0, 8), :] = xi
            return xr[7:8, :], xi[7:8, :]

        cr, ci = lax.fori_loop(0, tm // 8, grp, (car_ref[0:1, :], car_ref[1:2, :]))
        car_ref[0:1, :] = cr
        car_ref[1:2, :] = ci
        y = _dot_nt(str_ref[...].astype(MXU), cr_ref[0]) - _dot_nt(sti_ref[...].astype(MXU), ci_ref[0])
        ys_ref[...] = y + d_ref[...] * u

    blk = lambda: pl.BlockSpec((1, 8 * SSM_H, w), lambda j, i: (j, 0, 0))
    return pl.pallas_call(
        body, name="s5_fwd", grid=(SSM_BLK, nt),
        in_specs=[pl.BlockSpec((tm, LANES), lambda j, i: (i, j)),
                  pl.BlockSpec((1, w), lambda j, i: (0, j)), pl.BlockSpec((1, w), lambda j, i: (0, j)),
                  blk(), blk(), blk(), blk(),
                  pl.BlockSpec((1, LANES), lambda j, i: (0, j))],
        out_specs=[pl.BlockSpec((tm, w), lambda j, i: (i, j)), pl.BlockSpec((tm, w), lambda j, i: (i, j)),
                   pl.BlockSpec((tm, LANES), lambda j, i: (i, j))],
        out_shape=[_sds((S, SSM_BLK * w)), _sds((S, SSM_BLK * w)), _sds((S, SSM_W))],
        scratch_shapes=[pltpu.VMEM((8, 8, w), F32), pltpu.VMEM((8, w), F32)],
        compiler_params=_cp("parallel", "arbitrary"),
    )(us, abar_re, abar_im, b_re, b_im, c_re, c_im, d_skip)


def _sgu_mix(vnb, ws_ref, grp):
    acc = jnp.zeros(vnb.shape, F32)
    for g in range(SGU_G):
        acc = jnp.where(grp == g, _dot(ws_ref[g], vnb), acc)
    return acc


def _mix_fwd(x, ys, uv, gl, w_glu, b_glu, w_pa, g_sgu, ws, bias_s, w_pb, w_out, g_ffn, tm):
    S = x.shape[0]

    def body(x_ref, ys_ref, uv_ref, gl_ref, wglu_ref, bglu_ref, wpa_ref, gs_ref, ws_ref, bias_ref, wpb_ref, wout_ref,
             gf_ref, yg_ref, yap_ref, sg_ref, ya_ref, yb_ref, m_ref, x1_ref, h2_ref):
        yg = _gelu(ys_ref[...])
        ygb = yg.astype(MXU)
        yg_ref[...] = ygb
        z = _dot(ygb, wglu_ref[...]) + bglu_ref[...]
        yapb = (yg * _sigmoid(z)).astype(MXU)
        yap_ref[...] = yapb
        ya = _dot(yapb, wpa_ref[...])
        ya_ref[...] = ya

        uvg = _gelu(uv_ref[...])
        u2 = uvg[:, :SGU_W]
        v2 = uvg[:, SGU_W:]
        vnb = (v2 * _rms(v2) * gs_ref[...]).astype(MXU)
        grp = lax.broadcasted_iota(jnp.int32, (CHUNK, SGU_W), 1) // SGU_D
        for c in range(tm // CHUNK):
            rs = slice(c * CHUNK, (c + 1) * CHUNK)
            mixed = _sgu_mix(vnb[rs], ws_ref, grp) + bias_ref[...]
            sg_ref[rs, :] = (u2[rs] * mixed).astype(MXU)
        yb = _dot(sg_ref[...], wpb_ref[...])
        yb_ref[...] = yb

        glv = gl_ref[...]
        m = _sigmoid(glv[:, :D_MODEL]) * ya + _sigmoid(glv[:, D_MODEL:]) * yb
        mb = m.astype(MXU)
        m_ref[...] = mb
        x1 = x_ref[...] + _dot(mb, wout_ref[...])
        x1_ref[...] = x1
        h2_ref[...] = (x1 * _rms(x1) * gf_ref[...]).astype(MXU)

    row = lambda n: pl.BlockSpec((tm, n), lambda i: (i, 0))
    return pl.pallas_call(
        body, name="mix_fwd", grid=(S // tm,),
        in_specs=[row(D_MODEL), row(SSM_W), row(2 * SGU_W), row(2 * D_MODEL),
                  _full(w_glu.shape), _full(b_glu.shape), _full(w_pa.shape), _full(g_sgu.shape), _full(ws.shape),
                  _full(bias_s.shape), _full(w_pb.shape), _full(w_out.shape), _full(g_ffn.shape)],
        out_specs=[row(SSM_W), row(SSM_W), row(SGU_W), row(D_MODEL), row(D_MODEL), row(D_MODEL), row(D_MODEL),
                   row(D_MODEL)],
        out_shape=[_sds((S, SSM_W), MXU), _sds((S, SSM_W), MXU), _sds((S, SGU_W), MXU), _sds((S, D_MODEL)),
                   _sds((S, D_MODEL)), _sds((S, D_MODEL), MXU), _sds((S, D_MODEL)), _sds((S, D_MODEL), MXU)],
        compiler_params=_cp("parallel"),
    )(x, ys, uv, gl, w_glu, b_glu, w_pa, g_sgu, ws, bias_s, w_pb, w_out, g_ffn)


def _causal_conv3(u, prev8, cw, cb):
    tm = u.shape[0]
    w0, w1, w2 = cw[0:1], cw[1:2], cw[2:3]
    body = w0 * pltpu.roll(u, 2, 0) + w1 * pltpu.roll(u, 1, 0) + w2 * u + cb
    u8 = u[0:8, :]
    r8 = lax.broadcasted_iota(jnp.int32, u8.shape, 0)
    t1 = prev8[7:8, :]
    t0 = prev8[6:7, :]
    s1 = jnp.where(r8 == 0, t1, pltpu.roll(u8, 1, 0))
    s2 = jnp.where(r8 == 0, t0, jnp.where(r8 == 1, t1, pltpu.roll(u8, 2, 0)))
    first = w0 * s2 + w1 * s1 + w2 * u8 + cb
    return jnp.concatenate([first, body[8:tm, :]], axis=0)


def _causal_conv3_adjoint(d, next8, cw):
    tm = d.shape[0]
    w0, w1, w2 = cw[0:1], cw[1:2], cw[2:3]
    n1 = pltpu.roll(d, tm - 1, 0)
    n2 = pltpu.roll(d, tm - 2, 0)
    body = w2 * d + w1 * n1 + w0 * n2
    d8 = d[tm - 8:tm, :]
    r8 = lax.broadcasted_iota(jnp.int32, d8.shape, 0)
    h0 = next8[0:1, :]
    h1 = next8[1:2, :]
    m1 = jnp.where(r8 == 7, h0, pltpu.roll(d8, 7, 0))
    m2 = jnp.where(r8 == 6, h0, jnp.where(r8 == 7, h1, pltpu.roll(d8, 6, 0)))
    last = w2 * d8 + w1 * m1 + w0 * m2
    out = jnp.concatenate([body[0:tm - 8, :], last], axis=0)
    return out, n1, n2, h0 - d[0:1, :], h1 - d[1:2, :]


def _ffn_fwd(h2, x1, tgt, w_up, conv_w, conv_b, w_down, g_final, tm):
    S = h2.shape[0]
    nt = S // tm
    ncb = FF_NCB

    def body(h2_ref, wa_ref, wb_ref, cwa_ref, cwb_ref, cba_ref, cbb_ref, wd_ref, x1_ref, gf_ref, tgt_ref,
             up_ref, ab_ref, ff_ref, dx2_ref, dx2b_ref, loss_ref, dgf_ref, acc_ref, tail_ref):
        i = pl.program_id(0)
        cb = pl.program_id(1)

        @pl.when(i == 0)
        def _():
            tail_ref[cb] = jnp.zeros((2, 8, FF_CW), F32)

        @pl.when(jnp.logical_and(i == 0, cb == 0))
        def _():
            loss_ref[...] = jnp.zeros_like(loss_ref)
            dgf_ref[...] = jnp.zeros_like(dgf_ref)

        h2v = h2_ref[...]
        ua = _dot_nt(h2v, wa_ref[0])
        ub = _dot_nt(h2v, wb_ref[0])
        up_ref[0, 0] = ua.astype(MXU)
        up_ref[1, 0] = ub.astype(MXU)
        a = _causal_conv3(ua, tail_ref[cb, 0], cwa_ref[0], cba_ref[0])
        b = _causal_conv3(ub, tail_ref[cb, 1], cwb_ref[0], cbb_ref[0])
        tail_ref[cb, 0] = ua[tm - 8:tm, :]
        tail_ref[cb, 1] = ub[tm - 8:tm, :]
        ab_ref[0, 0] = a
        ab_ref[1, 0] = b
        ffb = (a * _sigmoid(a) * b).astype(MXU)
        ff_ref[0] = ffb
        contrib = _dot(ffb, wd_ref[...])

        @pl.when(cb == 0)
        def _():
            acc_ref[...] = contrib

        @pl.when(cb > 0)
        def _():
            acc_ref[...] += contrib

        @pl.when(cb == ncb - 1)
        def _():
            x2 = x1_ref[...] + acc_ref[...]
            r = _rms(x2)
            xn = x2 * r
            g = gf_ref[...]
            diff = xn * g - tgt_ref[...]
            loss_ref[...] += (0.5 / D_MODEL) * jnp.sum(diff * diff)
            dy = diff * (1.0 / D_MODEL)
            dgf_ref[...] += _rowsum(dy * xn)
            dx2 = _rms_bwd(dy * g, xn, r)
            dx2_ref[...] = dx2
            dx2b_ref[...] = dx2.astype(MXU)

    row = lambda n: pl.BlockSpec((tm, n), lambda i, c: (i, 0))
    gate = lambda r: pl.BlockSpec((1, r, FF_CW), lambda i, c: (c, 0, 0))
    lin = lambda r: pl.BlockSpec((1, r, FF_CW), lambda i, c: (ncb + c, 0, 0))
    return pl.pallas_call(
        body, name="ffn_fwd", grid=(nt, ncb),
        in_specs=[row(D_MODEL),
                  pl.BlockSpec((1, FF_CW, D_MODEL), lambda i, c: (c, 0, 0)),
                  pl.BlockSpec((1, FF_CW, D_MODEL), lambda i, c: (ncb + c, 0, 0)),
                  gate(3), lin(3), gate(1), lin(1),
                  pl.BlockSpec((FF_CW, D_MODEL), lambda i, c: (c, 0)),
                  row(D_MODEL), _full((1, D_MODEL)), row(D_MODEL)],
        out_specs=[pl.BlockSpec((2, 1, tm, FF_CW), lambda i, c: (0, c, i, 0)),
                   pl.BlockSpec((2, 1, tm, FF_CW), lambda i, c: (0, c, i, 0)),
                   pl.BlockSpec((1, tm, FF_CW), lambda i, c: (c, i, 0)),
                   row(D_MODEL), row(D_MODEL), _full((1, LANES)), _full((1, D_MODEL))],
        out_shape=[_sds((2, ncb, S, FF_CW), MXU), _sds((2, ncb, S, FF_CW)), _sds((ncb, S, FF_CW), MXU),
                   _sds((S, D_MODEL)), _sds((S, D_MODEL), MXU), _sds((1, LANES)), _sds((1, D_MODEL))],
        scratch_shapes=[pltpu.VMEM((tm, D_MODEL), F32), pltpu.VMEM((ncb, 2, 8, FF_CW), F32)],
        compiler_params=_cp("arbitrary", "arbitrary"),
    )(h2, w_up, w_up, conv_w, conv_w, conv_b, conv_b, w_down, x1, g_final, tgt)


def _ffn_bwd(dx2, up, ab, x1, w_up, conv_w, w_down, g_ffn, tm):
    S = dx2.shape[0]
    nt = S // tm
    ncb = FF_NCB

    def body(dx2_ref, up_ref, ab_ref, cwa_ref, cwb_ref, wd_ref, wa_ref, wb_ref,
             x1_ref, g_ref, dup_ref, dx1_ref, dx1b_ref, dconv_ref, dg_ref, acc_ref, head_ref):
        i = pl.program_id(0)
        cb = pl.program_id(1)
        ri = nt - 1 - i

        @pl.when(i == 0)
        def _():
            head_ref[cb] = jnp.zeros((2, 8, FF_CW), F32)
            dconv_ref[cb] = jnp.zeros((8, FF_CW), F32)
            dconv_ref[ncb + cb] = jnp.zeros((8, FF_CW), F32)

        @pl.when(jnp.logical_and(i == 0, cb == 0))
        def _():
            dg_ref[...] = jnp.zeros_like(dg_ref)

        dx2v = dx2_ref[...]
        dff = _dot_nt(dx2v.astype(MXU), wd_ref[...])
        a = ab_ref[0, 0]
        b = ab_ref[1, 0]
        sa = _sigmoid(a)
        silu = a * sa
        da = (dff * b) * (sa + silu * (1.0 - sa))
        db = dff * silu

        dps = []
        for half, slot, d, cw_ref in ((0, cb, da, cwa_ref), (1, ncb + cb, db, cwb_ref)):
            dp, n1, n2, fix0, fix1 = _causal_conv3_adjoint(d, head_ref[cb, half], cw_ref[0])
            head_ref[cb, half] = d[0:8, :]
            dpb16 = dp.astype(MXU)
            dup_ref[half, 0] = dpb16
            dps.append(dpb16)
            u = up_ref[half, 0].astype(F32)
            u_last = u[tm - 1:tm, :]
            dconv_ref[slot, 0:1, :] += _rowsum(n2 * u) + fix0 * u[tm - 2:tm - 1, :] + fix1 * u_last
            dconv_ref[slot, 1:2, :] += _rowsum(n1 * u) + fix0 * u_last
            dconv_ref[slot, 2:3, :] += _rowsum(d * u)
            dconv_ref[slot, 3:4, :] += _rowsum(d)
        contrib = _dot(dps[0], wa_ref[0]) + _dot(dps[1], wb_ref[0])

        @pl.when(cb == 0)
        def _():
            acc_ref[...] = contrib

        @pl.when(cb > 0)
        def _():
            acc_ref[...] += contrib

        @pl.when(cb == ncb - 1)
        def _():
            x1v = x1_ref[...]
            r = _rms(x1v)
            xn = x1v * r
            dh2 = acc_ref[...]
            dg_ref[...] += _rowsum(dh2 * xn)
            dx1 = dx2v + _rms_bwd(dh2 * g_ref[...], xn, r)
            dx1_ref[...] = dx1
            dx1b_ref[...] = dx1.astype(MXU)

    row = lambda n: pl.BlockSpec((tm, n), lambda i, c: (nt - 1 - i, 0))
    colb = lambda: pl.BlockSpec((2, 1, tm, FF_CW), lambda i, c: (0, c, nt - 1 - i, 0))
    gate = lambda r: pl.BlockSpec((1, r, FF_CW), lambda i, c: (c, 0, 0))
    lin = lambda r: pl.BlockSpec((1, r, FF_CW), lambda i, c: (ncb + c, 0, 0))
    return pl.pallas_call(
        body, name="ffn_bwd", grid=(nt, ncb),
        in_specs=[row(D_MODEL), colb(), colb(), gate(3), lin(3),
                  pl.BlockSpec((FF_CW, D_MODEL), lambda i, c: (c, 0)),
                  pl.BlockSpec((1, FF_CW, D_MODEL), lambda i, c: (c, 0, 0)),
                  pl.BlockSpec((1, FF_CW, D_MODEL), lambda i, c: (ncb + c, 0, 0)),
                  row(D_MODEL), _full((1, D_MODEL))],
        out_specs=[colb(), row(D_MODEL), row(D_MODEL), _full((2 * ncb, 8, FF_CW)), _full((1, D_MODEL))],
        out_shape=[_sds((2, ncb, S, FF_CW), MXU), _sds((S, D_MODEL)), _sds((S, D_MODEL), MXU), _sds((2 * ncb, 8, FF_CW)),
                   _sds((1, D_MODEL))],
        scratch_shapes=[pltpu.VMEM((tm, D_MODEL), F32), pltpu.VMEM((ncb, 2, 8, FF_CW), F32)],
        compiler_params=_cp("arbitrary", "arbitrary"),
    )(dx2, up, ab, conv_w, conv_w, w_down, w_up, w_up, x1, g_ffn)


def _mix_bwd(dx1, gl, ya, yb, ys, uv, w_out, w_pa, w_pb, w_glu, b_glu, g_sgu, ws, ws_t, bias_s, tm):
    S = dx1.shape[0]

    def body(dx1_ref, gl_ref, ya_ref, yb_ref, ys_ref, uv_ref, wout_ref, wpa_ref, wpb_ref, wglu_ref, bglu_ref, gs_ref,
             ws_ref, wst_ref, bias_ref,
             dgl_ref, dya_ref, dyb_ref, dz_ref, dys_ref, duv_ref, dbglu_ref, dgs_ref, dws_ref, dbs_ref,
             du2_ref, dvn_ref):
        i = pl.program_id(0)

        @pl.when(i == 0)
        def _():
            dbglu_ref[...] = jnp.zeros_like(dbglu_ref)
            dgs_ref[...] = jnp.zeros_like(dgs_ref)
            dws_ref[...] = jnp.zeros_like(dws_ref)
            dbs_ref[...] = jnp.zeros_like(dbs_ref)

        dm = _dot_nt(dx1_ref[...].astype(MXU), wout_ref[...])
        glv = gl_ref[...]
        ga = _sigmoid(glv[:, :D_MODEL])
        gb = _sigmoid(glv[:, D_MODEL:])
        dgl_ref[:, :D_MODEL] = (dm * ya_ref[...] * ga * (1.0 - ga)).astype(MXU)
        dgl_ref[:, D_MODEL:] = (dm * yb_ref[...] * gb * (1.0 - gb)).astype(MXU)
        dyab = (dm * ga).astype(MXU)
        dybb = (dm * gb).astype(MXU)
        dya_ref[...] = dyab
        dyb_ref[...] = dybb

        dyap = _dot_nt(dyab, wpa_ref[...])
        yg, dgelu = _gelu_and_grad(ys_ref[...])
        sz = _sigmoid(_dot(yg.astype(MXU), wglu_ref[...]) + bglu_ref[...])
        dz = dyap * yg * sz * (1.0 - sz)
        dzb = dz.astype(MXU)
        dz_ref[...] = dzb
        dbglu_ref[...] += _rowsum(dz)
        dys_ref[...] = (dyap * sz + _dot_nt(dzb, wglu_ref[...])) * dgelu

        dsg = _dot_nt(dybb, wpb_ref[...])
        uvg, duvg = _gelu_and_grad(uv_ref[...])
        u2 = uvg[:, :SGU_W]
        v2 = uvg[:, SGU_W:]
        rv = _rms(v2)
        vhat = v2 * rv
        gs = gs_ref[...]
        vnb = (vhat * gs).astype(MXU)
        grp = lax.broadcasted_iota(jnp.int32, (CHUNK, SGU_W), 1) // SGU_D
        tril = (lax.broadcasted_iota(jnp.int32, (CHUNK, CHUNK), 0)
                >= lax.broadcasted_iota(jnp.int32, (CHUNK, CHUNK), 1))
        for c in range(tm // CHUNK):
            rs = slice(c * CHUNK, (c + 1) * CHUNK)
            vc = vnb[rs]
            mixed = _sgu_mix(vc, ws_ref, grp) + bias_ref[...]
            dsg_c = dsg[rs]
            du2_ref[rs, :] = dsg_c * mixed
            dmx = dsg_c * u2[rs]
            dbs_ref[...] += dmx
            dmb = dmx.astype(MXU)
            dvn_ref[rs, :] = _sgu_mix(dmb, wst_ref, grp)
            for g in range(SGU_G):
                part = _dot_nt(jnp.where(grp == g, dmb, jnp.zeros((), MXU)), vc)
                dws_ref[g] += jnp.where(tril, part, 0.0)
        dvn = dvn_ref[...]
        dgs_ref[...] += _rowsum(dvn * vhat)
        dv2 = _rms_bwd(dvn * gs, vhat, rv)
        duv_ref[:, :SGU_W] = (du2_ref[...] * duvg[:, :SGU_W]).astype(MXU)
        duv_ref[:, SGU_W:] = (dv2 * duvg[:, SGU_W:]).astype(MXU)

    row = lambda n: pl.BlockSpec((tm, n), lambda i: (i, 0))
    return pl.pallas_call(
        body, name="mix_bwd", grid=(S // tm,),
        in_specs=[row(D_MODEL), row(2 * D_MODEL), row(D_MODEL), row(D_MODEL), row(SSM_W), row(2 * SGU_W),
                  _full(w_out.shape), _full(w_pa.shape), _full(w_pb.shape), _full(w_glu.shape), _full(b_glu.shape),
                  _full(g_sgu.shape), _full(ws.shape), _full(ws_t.shape), _full(bias_s.shape)],
        out_specs=[row(2 * D_MODEL), row(D_MODEL), row(D_MODEL), row(SSM_W), row(SSM_W), row(2 * SGU_W),
                   _full((1, SSM_W)), _full((1, SGU_W)), _full((SGU_G, CHUNK, CHUNK)), _full((CHUNK, SGU_W))],
        out_shape=[_sds((S, 2 * D_MODEL), MXU), _sds((S, D_MODEL), MXU), _sds((S, D_MODEL), MXU), _sds((S, SSM_W), MXU),
                   _sds((S, SSM_W)), _sds((S, 2 * SGU_W), MXU),
                   _sds((1, SSM_W)), _sds((1, SGU_W)), _sds((SGU_G, CHUNK, CHUNK)), _sds((CHUNK, SGU_W))],
        scratch_shapes=[pltpu.VMEM((tm, SGU_W), F32), pltpu.VMEM((tm, SGU_W), F32)],
        compiler_params=_cp("arbitrary"),
    )(dx1, gl, ya, yb, ys, uv, w_out, w_pa, w_pb, w_glu, b_glu, g_sgu, ws, ws_t, bias_s)


def _s5_bwd(dys, us, st_re, st_im, abar_re, abar_im, b_re, b_im, c_re, c_im, d_skip, tm):
    S = us.shape[0]
    nt = S // tm
    w = 8 * SSM_P
    hb = tm // 8

    def body(dys_ref, us_ref, str_ref, sti_ref, hr_ref, hi_ref, ar_ref, ai_ref, br_ref, bi_ref, cr_ref, ci_ref, d_ref,
             dus_ref, dab_ref, dd_ref, dbr_ref, dbi_ref, dcr_ref, dci_ref, tab_ref, car_ref, gr_ref, gi_ref):
        i = pl.program_id(1)
        ri = nt - 1 - i

        @pl.when(i == 0)
        def _():
            car_ref[...] = jnp.zeros_like(car_ref)
            for k, t in enumerate(_scan_tables(ar_ref[...], -ai_ref[...], True)):
                tab_ref[k] = t
            for r in (dab_ref, dd_ref, dbr_ref, dbi_ref, dcr_ref, dci_ref):
                r[...] = jnp.zeros_like(r)

        dys_v = dys_ref[...]
        dyb = dys_v.astype(MXU)
        gr_ref[...] = _dot(dyb, cr_ref[0])
        gi_ref[...] = -_dot(dyb, ci_ref[0])

        def grp(kk, carry):
            r0 = pl.multiple_of((hb - 1 - kk) * 8, 8)
            xr, xi = _scan_group(gr_ref[pl.ds(r0, 8), :], gi_ref[pl.ds(r0, 8), :], tab_ref, carry[0], carry[1], True)
            gr_ref[pl.ds(r0, 8), :] = xr
            gi_ref[pl.ds(r0, 8), :] = xi
            return xr[0:1, :], xi[0:1, :]

        cr, ci = lax.fori_loop(0, hb, grp, (car_ref[0:1, :], car_ref[1:2, :]))
        car_ref[0:1, :] = cr
        car_ref[1:2, :] = ci

        gsr = gr_ref[...]
        gsi = gi_ref[...]
        sr = str_ref[...]
        si = sti_ref[...]
        rows = lax.broadcasted_iota(jnp.int32, (tm, w), 0)
        first = ri == 0
        spr = jnp.where(rows == 0, jnp.where(first, 0.0, hr_ref[7:8, :]), pltpu.roll(sr, 1, 0))
        spi = jnp.where(rows == 0, jnp.where(first, 0.0, hi_ref[7:8, :]), pltpu.roll(si, 1, 0))
        dab_ref[0, 0:1, :] += _rowsum(gsr * spr + gsi * spi)
        dab_ref[0, 1:2, :] += _rowsum(gsi * spr - gsr * spi)

        gbr = gsr.astype(MXU)
        gbi = gsi.astype(MXU)
        u = us_ref[...]
        ub = u.astype(MXU)
        dus_ref[...] = (_dot_nt(gbr, br_ref[0]) + _dot_nt(gbi, bi_ref[0]) + d_ref[...] * dys_v).astype(MXU)
        dd_ref[0, 0:1, :] += _rowsum(dys_v * u)
        dbr_ref[0] += _dot_tn(ub, gbr)
        dbi_ref[0] += _dot_tn(ub, gbi)
        dcr_ref[0] += _dot_tn(dyb, sr.astype(MXU))
        dci_ref[0] -= _dot_tn(dyb, si.astype(MXU))

    blk = lambda: pl.BlockSpec((1, 8 * SSM_H, w), lambda j, i: (j, 0, 0))
    rowl = lambda: pl.BlockSpec((tm, LANES), lambda j, i: (nt - 1 - i, j))
    roww = lambda: pl.BlockSpec((tm, w), lambda j, i: (nt - 1 - i, j))
    halo = lambda: pl.BlockSpec((8, w), lambda j, i: (jnp.maximum((nt - 1 - i) * hb - 1, 0), j))
    return pl.pallas_call(
        body, name="s5_bwd", grid=(SSM_BLK, nt),
        in_specs=[rowl(), rowl(), roww(), roww(), halo(), halo(),
                  pl.BlockSpec((1, w), lambda j, i: (0, j)), pl.BlockSpec((1, w), lambda j, i: (0, j)),
                  blk(), blk(), blk(), blk(),
                  pl.BlockSpec((1, LANES), lambda j, i: (0, j))],
        out_specs=[rowl(),
                   pl.BlockSpec((1, 8, w), lambda j, i: (j, 0, 0)), pl.BlockSpec((1, 8, LANES), lambda j, i: (j, 0, 0)),
                   blk(), blk(), blk(), blk()],
        out_shape=[_sds((S, SSM_W), MXU), _sds((SSM_BLK, 8, w)), _sds((SSM_BLK, 8, LANES)),
                   _sds((SSM_BLK, 8 * SSM_H, w)), _sds((SSM_BLK, 8 * SSM_H, w)),
                   _sds((SSM_BLK, 8 * SSM_H, w)), _sds((SSM_BLK, 8 * SSM_H, w))],
        scratch_shapes=[pltpu.VMEM((8, 8, w), F32), pltpu.VMEM((8, w), F32),
                        pltpu.VMEM((tm, w), F32), pltpu.VMEM((tm, w), F32)],
        compiler_params=_cp("parallel", "arbitrary"),
    )(dys, us, st_re, st_im, st_re, st_im, abar_re, abar_im, b_re, b_im, c_re, c_im, d_skip)


def _in_bwd(dus, duv, dgl, dx1, x, g_mix, w_in, tm):
    S = x.shape[0]

    def body(dus_ref, duv_ref, dgl_ref, dx1_ref, x_ref, g_ref, w_ref, gx_ref, dg_ref):
        @pl.when(pl.program_id(0) == 0)
        def _():
            dg_ref[...] = jnp.zeros_like(dg_ref)

        dh = (_dot(dus_ref[...], w_ref[0:SSM_W, :])
              + _dot(duv_ref[...], w_ref[SSM_W:SSM_W + 2 * SGU_W, :])
              + _dot(dgl_ref[...], w_ref[SSM_W + 2 * SGU_W:, :]))
        xv = x_ref[...]
        r = _rms(xv)
        xn = xv * r
        dg_ref[...] += _rowsum(dh * xn)
        gx_ref[...] = dx1_ref[...] + _rms_bwd(dh * g_ref[...], xn, r)

    row = lambda n: pl.BlockSpec((tm, n), lambda i: (i, 0))
    return pl.pallas_call(
        body, name="in_bwd", grid=(S // tm,),
        in_specs=[row(SSM_W), row(2 * SGU_W), row(2 * D_MODEL), row(D_MODEL), row(D_MODEL), _full((1, D_MODEL)),
                  _full(w_in.shape)],
        out_specs=[row(D_MODEL), _full((1, D_MODEL))],
        out_shape=[_sds((S, D_MODEL)), _sds((1, D_MODEL))],
        compiler_params=_cp("arbitrary"),
    )(dus, duv, dgl, dx1, x, g_mix, w_in)


def _pick(n, cands):
    for c in cands:
        if n % c == 0:
            return c
    return n


def _wgrad_split(a, b, nsplit, tk, name):
    S, K = a.shape
    N = b.shape[1]
    c = N // nsplit

    def body(a_ref, b_ref, o_ref):
        prod = _dot_tn(a_ref[...], b_ref[...])
        for d in range(nsplit):
            o_ref[d] = prod[:, c * d:c * (d + 1)].astype(MXU)

    return pl.pallas_call(
        body, name=name, grid=(K // tk,),
        in_specs=[pl.BlockSpec((S, tk), lambda k: (0, k)), _full((S, N))],
        out_specs=pl.BlockSpec((nsplit, tk, c), lambda k: (0, k, 0)),
        out_shape=_sds((nsplit, K, c), MXU),
        compiler_params=_cp("parallel"),
    )(a, b)


def _wgrad_in_t(dps, h1, name):
    S, K = h1.shape
    cw = 512
    counts = [b.shape[1] // cw for b in dps]
    starts = [sum(counts[:i]) for i in range(len(dps))]
    nblk = sum(counts)

    def body(*refs):
        b_refs = refs[:len(dps)]
        h_ref, o_ref = refs[len(dps):]
        j = pl.program_id(0)
        for b_ref, st, cnt in zip(b_refs, starts, counts):
            @pl.when(jnp.logical_and(j >= st, j < st + cnt))
            def _():
                o_ref[...] = _dot_tn(b_ref[...], h_ref[...]).astype(MXU)

    def src_spec(st, cnt):
        return pl.BlockSpec((S, cw), lambda j: (0, jnp.clip(j - st, 0, cnt - 1)))

    return pl.pallas_call(
        body, name=name, grid=(nblk,),
        in_specs=[src_spec(st, cnt) for st, cnt in zip(starts, counts)] + [_full((S, K))],
        out_specs=pl.BlockSpec((cw, K), lambda j: (j, 0)),
        out_shape=_sds((nblk * cw, K), MXU),
        compiler_params=_cp("arbitrary"),
    )(*dps, h1)


def _wgrad_blk(a3, b3, nblk, a_of, b_of, name):
    S, K = a3.shape[1:]
    N = b3.shape[2]

    def body(a_ref, b_ref, o_ref):
        o_ref[0] = _dot_tn(a_ref[0], b_ref[0]).astype(MXU)

    return pl.pallas_call(
        body, name=name, grid=(nblk,),
        in_specs=[pl.BlockSpec((1, S, K), lambda b: (a_of(b), 0, 0)),
                  pl.BlockSpec((1, S, N), lambda b: (b_of(b), 0, 0))],
        out_specs=pl.BlockSpec((1, K, N), lambda b: (b, 0, 0)),
        out_shape=_sds((nblk, K, N), MXU),
        compiler_params=_cp("parallel"),
    )(a3, b3)


def _assemble_cols(blocks_list, name):
    def body(*refs):
        n = len(blocks_list)
        for b_ref, o_ref in zip(refs[:n], refs[n:]):
            c = b_ref.shape[2]
            for d in range(N_DEV):
                o_ref[:, c * d:c * (d + 1)] = b_ref[d]

    return pl.pallas_call(
        body, name=name,
        out_shape=[_sds((b.shape[1], N_DEV * b.shape[2]), b.dtype) for b in blocks_list],
        compiler_params=pltpu.CompilerParams(vmem_limit_bytes=VMEM_LIMIT),
    )(*blocks_list)


def _tile(S, want):
    return want if S % want == 0 else S


def _local_step(x, tgt, p, ffn_weights, grads_out):
    S = x.shape[0]
    tm = _tile(S, 256)
    tl = _tile(S, 512)

    rep = lambda a: jnp.repeat(a, SSM_H, axis=0)
    are = rep(p["a_re"])
    aim = rep(p["a_im"])
    ldt = jnp.broadcast_to(rep(p["log_dt"].reshape(SSM_G, 1)), are.shape)
    br_t = p["b_re_t"].reshape(are.shape)
    bi_t = p["b_im_t"].reshape(are.shape)
    abr, abi, bbr, bbi = _s5_params_fwd(are, aim, ldt, br_t, bi_t)
    head = lambda a: a.reshape(SSM_G, SSM_H, SSM_P)[:, 0, :].reshape(1, SSM_G * SSM_P)
    abar_re, abar_im = head(abr), head(abi)
    bd_br = _blockdiag(bbr).astype(MXU)
    bd_bi = _blockdiag(bbi).astype(MXU)
    bd_cr = _blockdiag(p["c_re"].reshape(are.shape)).astype(MXU)
    bd_ci = _blockdiag(p["c_im"].reshape(are.shape)).astype(MXU)
    d_skip = p["d_skip"].reshape(1, SSM_W)

    tril = jnp.tril(jnp.ones((CHUNK, CHUNK), dtype=bool))
    ws = jnp.where(tril[None], p["w_s"], 0.0)
    ws_b = ws.astype(MXU)
    ws_t = ws.transpose(0, 2, 1).astype(MXU)
    bias_s = jnp.repeat(p["b_s"].T, SGU_D, axis=1)

    g_mix = p["g_mix"].reshape(1, D_MODEL)
    g_ffn = p["g_ffn"].reshape(1, D_MODEL)
    g_final = p["g_final"].reshape(1, D_MODEL)
    g_sgu = p["g_sgu"].reshape(1, SGU_W)
    b_glu = p["b_glu"].reshape(1, SSM_W)
    conv_b = p["conv_b"].reshape(N_DEV, 1, FF_CW)

    h1, us, uv, gl = _in_fwd(x, g_mix, p["w_in_t"], tm)
    st_re, st_im, ys = _s5_fwd(us, abar_re, abar_im, bd_br, bd_bi, bd_cr, bd_ci, d_skip, tl)
    yg, yap, sg, ya, yb, m, x1, h2 = _mix_fwd(x, ys, uv, gl, p["w_glu"], b_glu, p["w_proj_a"], g_sgu, ws_b, bias_s,
                                              p["w_proj_b"], p["w_out"], g_ffn, tm)
    w_up, conv_w, w_down = ffn_weights(h2)
    up, ab, ff, dx2, dx2b, loss, dg_final = _ffn_fwd(h2, x1, tgt, w_up, conv_w, conv_b, w_down, g_final, tl)

    dup, dx1, dx1b, dconv, dg_ffn = _ffn_bwd(dx2, up, ab, x1, w_up, conv_w, w_down, g_ffn, tl)
    rows8 = lambda g: g.reshape(N_DEV, g.shape[1] // N_DEV, g.shape[2])
    g_up = _wgrad_blk(dup.reshape(N_DEV, S, FF_CW), h2[None], N_DEV, lambda b: b, lambda b: 0, "wgrad_up")
    g_down = _wgrad_blk(ff, dx2b[None], FF_NCB, lambda b: b, lambda b: 0, "wgrad_down").reshape(
        N_DEV, D_FF // N_DEV, D_MODEL)
    token = grads_out(("w_up", "w_down"), (g_up, g_down))
    dgl, dya, dyb, dz, dys, duv, db_glu, dg_sgu, dws, dbs = _mix_bwd(
        dx1, gl, ya, yb, ys, uv, p["w_out"], p["w_proj_a"], p["w_proj_b"], p["w_glu"], b_glu + token[0:1, 0:1], g_sgu,
        ws_b, ws_t, bias_s, tm)
    token = grads_out(("w_glu", "w_proj_a", "w_proj_b", "w_out"),
                      (rows8(_wgrad_split(yg, dz, 1, SSM_W, "wgrad_glu")),
                       _wgrad_split(yap, dya, N_DEV, SSM_W, "wgrad_pa"),
                       _wgrad_split(sg, dyb, N_DEV, SGU_W, "wgrad_pb"),
                       rows8(_wgrad_split(m, dx1b, 1, 512, "wgrad_out"))))
    dus, dab, dd, dbbr, dbbi, dcr, dci = _s5_bwd(dys, us, st_re, st_im, abar_re, abar_im, bd_br, bd_bi, bd_cr, bd_ci,
                                                 d_skip + token[0:1, 0:1], tl)
    g_in = _wgrad_in_t([dus, duv, dgl], h1, "wgrad_in")
    token = grads_out(("w_in",), (g_in.reshape(N_DEV, g_in.shape[0] // N_DEV, D_MODEL),))
    grad_x, dg_mix = _in_bwd(dus, duv, dgl, dx1, x, g_mix + token[0:1, 0:1], p["w_in_t"], tm)

    spread = lambda v: jnp.repeat(v.reshape(SSM_G, SSM_P), SSM_H, axis=0) * (1.0 / SSM_H)
    dabr = spread(dab[:, 0, :])
    dabi = spread(dab[:, 1, :])
    dare, daim, dldt, dbr_t, dbi_t = _s5_params_bwd(are, aim, ldt, br_t, bi_t, dabr, dabi,
                                                    _unblockdiag(dbbr), _unblockdiag(dbbi))
    fold = lambda a: a.reshape(SSM_G, SSM_H, SSM_P).sum(axis=1)

    grads = {
        "g_mix": dg_mix,
        "a_re": fold(dare), "a_im": fold(daim), "log_dt": fold(dldt).sum(axis=1),
        "b_re": dbr_t, "b_im": dbi_t,
        "c_re": _unblockdiag(dcr).reshape(SSM_G, SSM_H, SSM_P),
        "c_im": _unblockdiag(dci).reshape(SSM_G, SSM_H, SSM_P),
        "d_skip": dd[:, 0, :].reshape(SSM_W),
        "b_glu": db_glu,
        "g_sgu": dg_sgu,
        "w_s": dws,
        "b_s": dbs.reshape(CHUNK, SGU_G, SGU_D).sum(axis=-1).T,
        "g_ffn": dg_ffn,
        "conv_w": dconv[:, 0:3, :],
        "conv_b": dconv[:, 3, :].reshape(2 * D_FF),
        "g_final": dg_final,
    }
    return loss, grad_x, grads


_ANY = pl.BlockSpec(memory_space=pl.ANY)
_MESH = pl.DeviceIdType.MESH


def _allgather(shards, dtypes, name, cast_only=()):
    n = len(shards)
    e = len(cast_only)

    def body(*refs):
        in_refs, extra_in = refs[:n], refs[n:n + e]
        out_refs, extra_out = refs[n + e:2 * n + e], refs[2 * n + e:2 * n + 2 * e]
        stage = refs[2 * n + 2 * e:3 * n + 2 * e]
        send_sems, recv_sems, local_sems = refs[3 * n + 2 * e:]
        for a in range(n):
            stage[a][...] = in_refs[a][...].astype(dtypes[a])
        for i in range(e):
            extra_out[i][...] = extra_in[i][...].astype(MXU)
        x, y, c = lax.axis_index("x"), lax.axis_index("y"), lax.axis_index("c")
        me, sibling = (x, y, c), (x, y, 1 - c)
        chips = [(1 - x, y), (x, 1 - y), (1 - x, 1 - y)]

        def slot(a, px, py, pc):
            return out_refs[a].at[4 * px + 2 * py + pc]

        def copy(a, k, block, to, src=None):
            return pltpu.make_async_remote_copy(
                src_ref=slot(a, *block) if src is None else src, dst_ref=slot(a, *block),
                send_sem=send_sems.at[a, k], recv_sem=recv_sems.at[a, k], device_id=to, device_id_type=_MESH)

        mine = [pltpu.make_async_copy(stage[a], slot(a, *me), local_sems.at[a]) for a in range(n)]
        for cp in mine:
            cp.start()
        first = []
        for j, chip in enumerate(chips):
            first += [copy(a, 1 + j, me, (*chip, c), src=stage[a]) for a in range(n)]
        first += [copy(a, 0, me, sibling, src=stage[a]) for a in range(n)]
        for cp in first:
            cp.start()
        passed = []
        for j, chip in enumerate(chips):
            for a in range(n):
                copy(a, 1 + j, (*chip, c), me).wait_recv()
                fwd = copy(a, 4 + j, (*chip, c), sibling)
                fwd.start()
                passed.append(fwd)
        for a in range(n):
            copy(a, 0, sibling, me).wait_recv()
        for j, chip in enumerate(chips):
            for a in range(n):
                copy(a, 4 + j, (*chip, 1 - c), me).wait_recv()
        for cp in first + passed:
            cp.wait_send()
        for cp in mine:
            cp.wait()

    vmem = pl.BlockSpec(memory_space=pltpu.VMEM)
    res = pl.pallas_call(
        body, name=name, in_specs=[vmem] * (n + e), out_specs=[_ANY] * n + [vmem] * e,
        out_shape=[_sds((N_DEV,) + s.shape, dt) for s, dt in zip(shards, dtypes)]
                  + [_sds(s.shape, MXU) for s in cast_only],
        scratch_shapes=[pltpu.VMEM(s.shape, dt) for s, dt in zip(shards, dtypes)]
                       + [pltpu.SemaphoreType.DMA((n, 7)), pltpu.SemaphoreType.DMA((n, 7)), pltpu.SemaphoreType.DMA((n,))],
        compiler_params=pltpu.CompilerParams(vmem_limit_bytes=VMEM_LIMIT),
    )(*shards, *cast_only)
    return res[:n], res[n:]


def _all_to_all(sends, name):
    n = len(sends)

    def body(*refs):
        send_refs, recv_refs = refs[:n], refs[n:2 * n]
        send_sems, recv_sems, local_sems = refs[2 * n:]
        x, y, c = lax.axis_index("x"), lax.axis_index("y"), lax.axis_index("c")
        me = 4 * x + 2 * y + c
        mine = [pltpu.make_async_copy(send_refs[a].at[me], recv_refs[a].at[me], local_sems.at[a]) for a in range(n)]
        for cp in mine:
            cp.start()
        copies = []
        for k in (2, 4, 6, 3, 5, 7, 1):
            px = 1 - x if k & 4 else x
            py = 1 - y if k & 2 else y
            pc = 1 - c if k & 1 else c
            peer = 4 * px + 2 * py + pc
            for a in range(n):
                sems = dict(send_sem=send_sems.at[a, k - 1], recv_sem=recv_sems.at[a, k - 1],
                            device_id=(px, py, pc), device_id_type=_MESH)
                cp = pltpu.make_async_remote_copy(src_ref=send_refs[a].at[peer], dst_ref=recv_refs[a].at[me], **sems)
                cp.start()
                landing = pltpu.make_async_remote_copy(src_ref=send_refs[a].at[peer], dst_ref=recv_refs[a].at[peer],
                                                       **sems)
                copies.append((cp, landing))
        for _, landing in copies:
            landing.wait_recv()
        for cp, _ in copies:
            cp.wait_send()
        for cp in mine:
            cp.wait()

    return pl.pallas_call(
        body, name=name, in_specs=[_ANY] * n, out_specs=[_ANY] * n,
        out_shape=[_sds(s.shape, s.dtype) for s in sends],
        scratch_shapes=[pltpu.SemaphoreType.DMA((n, 7)), pltpu.SemaphoreType.DMA((n, 7)), pltpu.SemaphoreType.DMA((n,))],
    )(*sends)


_HBM = pl.BlockSpec(memory_space=pltpu.HBM)
_SEM = pl.BlockSpec(memory_space=pltpu.SEMAPHORE)
_EFFECT = pltpu.SideEffectType.DATAFLOW_SIDE_EFFECTING
_PEER_ORDER = (2, 4, 6, 3, 5, 7, 1)


def _peer(k):
    x, y, c = lax.axis_index("x"), lax.axis_index("y"), lax.axis_index("c")
    px = 1 - x if k & 4 else x
    py = 1 - y if k & 2 else y
    pc = 1 - c if k & 1 else c
    return (px, py, pc), 4 * px + 2 * py + pc


def _push_start(srcs, lands, slotted, name):
    n = len(srcs)

    def body(*refs):
        src_refs, land_refs = refs[:n], refs[n:2 * n]
        send_sems, recv_sems, token = refs[2 * n], refs[2 * n + 1], refs[-1]
        me = 4 * lax.axis_index("x") + 2 * lax.axis_index("y") + lax.axis_index("c")
        for k in _PEER_ORDER:
            dev, peer = _peer(k)
            for a in range(n):
                pltpu.make_async_remote_copy(
                    src_ref=src_refs[a].at[peer] if slotted else src_refs[a], dst_ref=land_refs[a].at[me],
                    send_sem=send_sems.at[7 * a + k - 1], recv_sem=recv_sems.at[7 * a + k - 1],
                    device_id=dev, device_id_type=_MESH).start()
        token[...] = jnp.zeros_like(token)

    bufs = list(srcs) + list(lands)
    res = pl.pallas_call(
        body, name=name, in_specs=[_HBM] * (2 * n),
        out_specs=(_SEM, _SEM, *[_HBM] * (2 * n), pl.BlockSpec(memory_space=pltpu.VMEM)),
        out_shape=(pltpu.SemaphoreType.DMA((7 * n,)), pltpu.SemaphoreType.DMA((7 * n,)),
                   *[pltpu.HBM(b.shape, b.dtype) for b in bufs], _sds((8, LANES))),
        input_output_aliases={i: 2 + i for i in range(2 * n)},
        compiler_params=pltpu.CompilerParams(has_side_effects=_EFFECT),
    )(*[pltpu.with_memory_space_constraint(b, pltpu.HBM) for b in bufs])
    return res[0], res[1], res[2:2 + n], res[2 + n:2 + 2 * n], res[-1]


def _push_wait(send_sems, recv_sems, srcs, lands, slotted, after, name):
    n = len(srcs)

    def body(*refs):
        src_refs, land_refs = refs[:n], refs[n:2 * n]
        send_sems, recv_sems = refs[2 * n], refs[2 * n + 1]
        for k in _PEER_ORDER:
            dev, peer = _peer(k)
            for a in range(n):
                cp = pltpu.make_async_remote_copy(
                    src_ref=src_refs[a].at[peer] if slotted else src_refs[a], dst_ref=land_refs[a].at[peer],
                    send_sem=send_sems.at[7 * a + k - 1], recv_sem=recv_sems.at[7 * a + k - 1],
                    device_id=dev, device_id_type=_MESH)
                cp.wait_send()
                cp.wait_recv()

    bufs = list(srcs) + list(lands)
    res = pl.pallas_call(
        body, name=name, in_specs=[_HBM] * (2 * n) + [_SEM, _SEM] + [_ANY] * len(after), out_specs=[_HBM] * (2 * n),
        out_shape=[pltpu.HBM(b.shape, b.dtype) for b in bufs],
        input_output_aliases={i: i for i in range(2 * n)},
        compiler_params=pltpu.CompilerParams(has_side_effects=_EFFECT),
    )(*bufs, send_sems, recv_sems, *after)
    return res[n:]


def _adamw(w, g, m, v):
    m2 = ADAM_B1 * m + (1.0 - ADAM_B1) * g
    v2 = ADAM_B2 * v + (1.0 - ADAM_B2) * (g * g)
    m_hat = m2 / (1.0 - ADAM_B1 ** ADAM_STEP)
    v_hat = v2 / (1.0 - ADAM_B2 ** ADAM_STEP)
    delta = -ADAM_LR * (m_hat / (jnp.sqrt(v_hat) + ADAM_EPS) + ADAM_WD * w)
    return delta, m2, v2


def _adam_shard(parts, w, m, v, name):
    _, r, c = w.shape
    tr = max(t for t in range(16, 257, 16) if r % t == 0)

    def body(p_ref, w_ref, m_ref, v_ref, g_ref, d_ref, m2_ref, v2_ref):
        g = p_ref[0].astype(F32)
        for s in range(1, N_DEV):
            g = g + p_ref[s].astype(F32)
        g_ref[0] = g
        d_ref[0], m2_ref[0], v2_ref[0] = _adamw(w_ref[0], g, m_ref[0], v_ref[0])

    row = lambda: pl.BlockSpec((1, tr, c), lambda i: (0, i, 0))
    return pl.pallas_call(
        body, name=name, grid=(r // tr,),
        in_specs=[pl.BlockSpec((N_DEV, tr, c), lambda i: (0, i, 0)), row(), row(), row()],
        out_specs=[row(), row(), row(), row()], out_shape=[_sds((1, r, c))] * 4,
        compiler_params=_cp("parallel"),
    )(parts, w, m, v)


def _adam_small(gs, ws, ms, vs, name):
    n = len(gs)

    def body(*refs):
        ins, outs = refs[:4 * n], refs[4 * n:]
        for i in range(n):
            g = ins[i][...]
            d, m2, v2 = _adamw(ins[n + i][...], g, ins[2 * n + i][...], ins[3 * n + i][...])
            outs[i][...] = d
            outs[n + i][...] = m2
            outs[2 * n + i][...] = v2

    res = pl.pallas_call(
        body, name=name, out_shape=[_sds(w.shape) for w in ws] * 3,
        compiler_params=pltpu.CompilerParams(vmem_limit_bytes=VMEM_LIMIT),
    )(*gs, *ws, *ms, *vs)
    return res[:n], res[n:2 * n], res[2 * n:]


def _sum_slots(parts, name):
    R = parts.shape[1]

    def body(p_ref, o_ref):
        g = p_ref[0]
        for s in range(1, N_DEV):
            g = g + p_ref[s]
        o_ref[...] = g

    return pl.pallas_call(body, name=name, out_shape=_sds((R, LANES)))(parts)


def _pad_to(a, n, axis):
    extra = n - a.shape[axis]
    if extra == 0:
        return a
    widths = [(0, 0)] * a.ndim
    widths[axis] = (0, extra)
    return jnp.pad(a, widths)


def _ceil_to(n, k):
    return -(-n // k) * k


def _pack_rows(flats, rows_multiple):
    parts = [_pad_to(f, _ceil_to(f.shape[-1], LANES), f.ndim - 1) for f in flats]
    cat = jnp.concatenate(parts, axis=-1)
    total = _ceil_to(cat.shape[-1], LANES * rows_multiple)
    cat = _pad_to(cat, total, cat.ndim - 1)
    return cat.reshape(cat.shape[:-1] + (total // LANES, LANES))


def _unpack_rows(buf, sizes):
    flat = buf.reshape(buf.shape[:-2] + (-1,))
    out, off = [], 0
    for n in sizes:
        out.append(flat[..., off:off + n])
        off += _ceil_to(n, LANES)
    return out


_MIX_BIG = ("w_in", "w_glu", "w_proj_a", "w_proj_b", "w_out")
_BIG = _MIX_BIG + ("w_up", "w_down")
_SMALL = ("g_mix", "a_re", "a_im", "log_dt", "b_re", "b_im", "c_re", "c_im", "d_skip", "b_glu", "g_sgu", "w_s", "b_s",
          "g_ffn", "conv_b", "g_final")
_SMALL_ROWS_MULTIPLE = 8 * N_DEV
_TRANSPOSED = ("w_in", "w_up", "b_re", "b_im")


def _as_2d(a):
    return a.reshape(-1, a.shape[-1]) if a.ndim > 1 else a.reshape(1, -1)


def kernel(x, g_mix, w_in, a_re, a_im, log_dt, b_re, b_im, c_re, c_im, d_skip, w_glu, b_glu, w_proj_a, g_sgu, w_s, b_s, w_proj_b, w_out, g_ffn, w_up, conv_w, conv_b, w_down, g_final, loss_target, m_g_mix, m_w_in, m_a_re, m_a_im, m_log_dt, m_b_re, m_b_im, m_c_re, m_c_im, m_d_skip, m_w_glu, m_b_glu, m_w_proj_a, m_g_sgu, m_w_s, m_b_s, m_w_proj_b, m_w_out, m_g_ffn, m_w_up, m_conv_w, m_conv_b, m_w_down, m_g_final, v_g_mix, v_w_in, v_a_re, v_a_im, v_log_dt, v_b_re, v_b_im, v_c_re, v_c_im, v_d_skip, v_w_glu, v_b_glu, v_w_proj_a, v_g_sgu, v_w_s, v_b_s, v_w_proj_b, v_w_out, v_g_ffn, v_w_up, v_conv_w, v_conv_b, v_w_down, v_g_final):
    args = dict(locals())
    me = 4 * lax.axis_index("x") + 2 * lax.axis_index("y") + lax.axis_index("c")

    def own_slot(buf, block):
        return lax.dynamic_update_slice(buf, block[None], (me,) + (0,) * block.ndim)

    for n in _TRANSPOSED:
        for pre in ("", "m_", "v_"):
            args[pre + n] = jnp.swapaxes(args[pre + n], -1, -2)
    gathered, (up_sh, down_sh) = _allgather([args[n][0] for n in _MIX_BIG], [MXU] * len(_MIX_BIG), "allgather_mixer",
                                            cast_only=(args["w_up"][0], w_down[0]))
    g = dict(zip(_MIX_BIG, gathered))
    ffn_srcs = [up_sh, down_sh, conv_w[0]]
    ffn_lands = [own_slot(lax.empty((N_DEV,) + s.shape, s.dtype), s) for s in ffn_srcs]
    ag_send, ag_recv, ffn_srcs, ffn_lands, ag_token = _push_start(ffn_srcs, ffn_lands, False, "push_ffn_weights")
    w_pa_full, w_pb_full = _assemble_cols([g["w_proj_a"], g["w_proj_b"]], "assemble_cols")
    p = {n: (args[n][0] if n != "g_final" else args[n]) for n in _SMALL if n not in _TRANSPOSED}
    p.update(w_in_t=g["w_in"].reshape(SSM_W + 2 * SGU_W + 2 * D_MODEL, D_MODEL), w_proj_a=w_pa_full, w_proj_b=w_pb_full,
             w_glu=g["w_glu"].reshape(SSM_W, SSM_W), w_out=g["w_out"].reshape(D_MODEL, D_MODEL),
             b_re_t=args["b_re"][0], b_im_t=args["b_im"][0])
    p["g_mix"] = p["g_mix"] + ag_token[0:1, 0:1]

    def ffn_weights(after):
        w_up_g, w_down_g, conv_w_g = _push_wait(ag_send, ag_recv, ffn_srcs, ffn_lands, False, [after], "wait_ffn_weights")
        return w_up_g, conv_w_g, w_down_g.reshape(D_FF, D_MODEL)

    pushes = []

    def grads_out(names, sends):
        lands = [own_slot(lax.empty(s.shape, s.dtype), lax.dynamic_index_in_dim(s, me, 0, keepdims=False))
                 for s in sends]
        send_sems, recv_sems, srcs, lands, token = _push_start(list(sends), lands, True, "push_grads_" + names[0])
        pushes.append((names, send_sems, recv_sems, srcs, lands))
        return token

    loss_part, grad_x, grads = _local_step(x[0], loss_target[0], p, ffn_weights, grads_out)

    small_names = _SMALL + ("conv_w", "loss")
    small_g = dict(grads, loss=loss_part[0, 0:1])
    flats = [small_g[n].reshape(-1) for n in small_names]
    small_sizes = [f.shape[0] for f in flats]
    g_small = _pack_rows(flats, _SMALL_ROWS_MULTIPLE)
    rs8 = g_small.shape[0] // N_DEV
    recv_small, = _all_to_all([g_small.reshape(N_DEV, rs8, LANES)], "all_to_all_small")
    small_mine = _sum_slots(recv_small, "sum_small")
    g_small_all = _allgather([small_mine], [F32], "allgather_small")[0][0].reshape(N_DEV * rs8, LANES)
    pieces = dict(zip(small_names, _unpack_rows(g_small_all, small_sizes)))
    loss = pieces["loss"][0]
    dconv_w = lax.dynamic_index_in_dim(pieces["conv_w"].reshape(N_DEV, 3, FF_CW), me, axis=0, keepdims=False)

    out = {}
    done = [g_small_all]
    for names, send_sems, recv_sems, srcs, lands in pushes:
        parts = _push_wait(send_sems, recv_sems, srcs, lands, True, done, "wait_grads_" + names[0])
        for n, part in zip(names, parts):
            res = _adam_shard(part, args[n], args["m_" + n], args["v_" + n], "adam_" + n)
            for kind, v in zip(("grad_", "delta_", "new_m_", "new_v_"), res):
                out[kind + n] = v
            done = [res[0]]
    names2 = _SMALL + ("conv_w",)
    gs = [pieces[n].reshape(_as_2d(args[n]).shape) for n in _SMALL] + [dconv_w]
    ds, m2s, v2s = _adam_small(gs, [_as_2d(args[n]) for n in names2], [_as_2d(args["m_" + n]) for n in names2],
                               [_as_2d(args["v_" + n]) for n in names2], "adam_small")
    for n, res in zip(names2, zip(gs, ds, m2s, v2s)):
        for kind, v in zip(("grad_", "delta_", "new_m_", "new_v_"), res):
            out[kind + n] = v.reshape(args[n].shape)
    order = ("g_mix", "w_in", "a_re", "a_im", "log_dt", "b_re", "b_im", "c_re", "c_im", "d_skip", "w_glu", "b_glu",
             "w_proj_a", "g_sgu", "w_s", "b_s", "w_proj_b", "w_out", "g_ffn", "w_up", "conv_w", "conv_b", "w_down",
             "g_final")
    res = [loss, grad_x.reshape(x.shape)]
    for kind in ("grad_", "delta_", "new_m_", "new_v_"):
        res += [jnp.swapaxes(out[kind + n], -1, -2) if n in _TRANSPOSED else out[kind + n] for n in order]
    return tuple(res)
```

```python
import functools
import math

import jax
import jax.numpy as jnp
from jax import lax
from jax.experimental import pallas as pl
from jax.experimental.pallas import tpu as pltpu

F32 = jnp.float32
MXU = jnp.bfloat16
EPS = 1e-6

D_MODEL = 1024
SSM_W = 512
SSM_G, SSM_H, SSM_P = 32, 16, 64
SSM_BLK = 4
SGU_W = 512
SGU_G, SGU_D, CHUNK = 8, 64, 128
D_FF = 2816
N_DEV = 8
FF_CW = 2 * D_FF // N_DEV
FF_NCB = D_FF // FF_CW
LANES = 128

ADAM_LR, ADAM_B1, ADAM_B2, ADAM_EPS, ADAM_WD, ADAM_STEP = 0.001, 0.9, 0.999, 1e-08, 0.01, 10

VMEM_LIMIT = 48 * 1024 * 1024


def _cp(*sem):
    return pltpu.CompilerParams(dimension_semantics=sem, vmem_limit_bytes=VMEM_LIMIT)


def _full(shape):
    n = len(shape)
    return pl.BlockSpec(shape, lambda *_: (0,) * n)


def _sds(shape, dtype=F32):
    return jax.ShapeDtypeStruct(shape, dtype)


def _dot(a, b):
    return jnp.dot(a, b, preferred_element_type=F32)


def _dot_nt(a, b):
    return lax.dot_general(a, b, (((1,), (1,)), ((), ())), preferred_element_type=F32)


def _dot_tn(a, b):
    return lax.dot_general(a, b, (((0,), (0,)), ((), ())), preferred_element_type=F32)


_GELU_C = math.sqrt(2.0 / math.pi)


def _gelu(x):
    return 0.5 * x * (1.0 + jnp.tanh(_GELU_C * (x + 0.044715 * (x * x * x))))


def _gelu_and_grad(x):
    t = jnp.tanh(_GELU_C * (x + 0.044715 * (x * x * x)))
    g = 0.5 * x * (1.0 + t)
    dg = 0.5 * (1.0 + t) + 0.5 * x * (1.0 - t * t) * (_GELU_C * (1.0 + 3.0 * 0.044715 * (x * x)))
    return g, dg


def _sigmoid(x):
    return 1.0 / (1.0 + jnp.exp(-x))


def _rms(x):
    return lax.rsqrt(jnp.mean(x * x, axis=-1, keepdims=True) + EPS)


def _rms_bwd(dxn, xn, r):
    return r * (dxn - xn * jnp.mean(dxn * xn, axis=-1, keepdims=True))


def _rowsum(x):
    return jnp.sum(x, axis=0, keepdims=True)


def _s5_disc(are, aim, ldt, br, bi):
    dt = jnp.exp(ldt)
    mag = jnp.exp(dt * are)
    abr = mag * jnp.cos(dt * aim)
    abi = mag * jnp.sin(dt * aim)
    den = are * are + aim * aim
    nr = abr - 1.0
    ni = abi
    fr = (nr * are + ni * aim) / den
    fi = (ni * are - nr * aim) / den
    return abr, abi, fr * br - fi * bi, fr * bi + fi * br


def _s5_params_fwd(are, aim, ldt, br, bi):
    def body(are_ref, aim_ref, ldt_ref, br_ref, bi_ref, o0, o1, o2, o3):
        outs = _s5_disc(are_ref[...], aim_ref[...], ldt_ref[...], br_ref[...], bi_ref[...])
        for o, v in zip((o0, o1, o2, o3), outs):
            o[...] = v
    shp = are.shape
    return pl.pallas_call(body, name="s5_params_fwd", out_shape=[_sds(shp)] * 4)(are, aim, ldt, br, bi)


def _s5_params_bwd(are, aim, ldt, br, bi, dabr, dabi, dbr, dbi):
    def body(are_ref, aim_ref, ldt_ref, br_ref, bi_ref, c0, c1, c2, c3, o0, o1, o2, o3, o4):
        prim = (are_ref[...], aim_ref[...], ldt_ref[...], br_ref[...], bi_ref[...])
        _, vjp = jax.vjp(_s5_disc, *prim)
        outs = vjp((c0[...], c1[...], c2[...], c3[...]))
        for o, v in zip((o0, o1, o2, o3, o4), outs):
            o[...] = v
    shp = are.shape
    return pl.pallas_call(body, name="s5_params_bwd", out_shape=[_sds(shp)] * 5)(
        are, aim, ldt, br, bi, dabr, dabi, dbr, dbi)


def _blockdiag(m_t):
    m = m_t.reshape(SSM_BLK, 8, SSM_H, 1, SSM_P)
    eye = jnp.eye(8, dtype=bool).reshape(1, 8, 1, 8, 1)
    return jnp.where(eye, m, jnp.zeros((), m_t.dtype)).reshape(SSM_BLK, 8 * SSM_H, 8 * SSM_P)


def _unblockdiag(pc):
    m = pc.reshape(SSM_BLK, 8, SSM_H, 8, SSM_P)
    return jnp.einsum("jghgp->jghp", m).reshape(SSM_G * SSM_H, SSM_P)


def _in_fwd(x, g_mix, w_in_t, tm):
    S = x.shape[0]

    def body(x_ref, g_ref, w_ref, h_ref, us_ref, uv_ref, gl_ref):
        xv = x_ref[...]
        h = (xv * _rms(xv) * g_ref[...]).astype(MXU)
        h_ref[...] = h
        us_ref[...] = _dot_nt(h, w_ref[0:SSM_W, :])
        uv_ref[...] = _dot_nt(h, w_ref[SSM_W:SSM_W + 2 * SGU_W, :])
        gl_ref[...] = _dot_nt(h, w_ref[SSM_W + 2 * SGU_W:, :])

    row = lambda n: pl.BlockSpec((tm, n), lambda i: (i, 0))
    return pl.pallas_call(
        body, name="in_fwd", grid=(S // tm,),
        in_specs=[row(D_MODEL), _full((1, D_MODEL)), _full(w_in_t.shape)],
        out_specs=[row(D_MODEL), row(SSM_W), row(2 * SGU_W), row(2 * D_MODEL)],
        out_shape=[_sds((S, D_MODEL), MXU), _sds((S, SSM_W)), _sds((S, 2 * SGU_W)), _sds((S, 2 * D_MODEL))],
        compiler_params=_cp("parallel"),
    )(x, g_mix, w_in_t)


def _scan_tables(ar, ai, reverse):
    n = ar.shape[-1]
    def mul(p, q):
        return p[0] * q[0] - p[1] * q[1], p[0] * q[1] + p[1] * q[0]
    a1 = (ar, ai)
    a2 = mul(a1, a1)
    a3 = mul(a2, a1)
    a4 = mul(a2, a2)
    a5 = mul(a4, a1)
    a6 = mul(a4, a2)
    a7 = mul(a4, a3)
    a8 = mul(a4, a4)
    pw = (a1, a2, a3, a4, a5, a6, a7, a8)
    rows = lax.broadcasted_iota(jnp.int32, (8, n), 0)
    tabs = []
    for s, a in ((1, a1), (2, a2), (4, a4)):
        keep = (rows + s <= 7) if reverse else (rows >= s)
        for comp in a:
            tabs.append(jnp.where(keep, jnp.broadcast_to(comp, (8, n)), 0.0))
    for c in range(2):
        q = jnp.zeros((8, n), F32)
        for r in range(8):
            e = (8 - r) if reverse else (r + 1)
            q = jnp.where(rows == r, jnp.broadcast_to(pw[e - 1][c], (8, n)), q)
        tabs.append(q)
    return tabs


def _scan_group(xr, xi, tab_ref, cr, ci, reverse):
    for t, s in enumerate((1, 2, 4)):
        pr = tab_ref[2 * t]
        pi = tab_ref[2 * t + 1]
        sh = (8 - s) if reverse else s
        sr = pltpu.roll(xr, sh, 0)
        si = pltpu.roll(xi, sh, 0)
        xr, xi = xr + pr * sr - pi * si, xi + pr * si + pi * sr
    qr = tab_ref[6]
    qi = tab_ref[7]
    return xr + qr * cr - qi * ci, xi + qr * ci + qi * cr


def _runs_load(src_ref, dst_ref, run):
    for i in range(run):
        dst_ref[8 * i:8 * i + 8, :] = src_ref[pl.ds(i, 8, stride=run), :]


def _runs_store(val, dst_ref, run):
    for i in range(run):
        dst_ref[pl.ds(i, 8, stride=run), :] = val[8 * i:8 * i + 8, :]


def _cpow2(ar, ai, log2n):
    for _ in range(log2n):
        ar, ai = ar * ar - ai * ai, 2.0 * ar * ai
    return ar, ai


def _s5_fwd(us, abar_re, abar_im, b_re, b_im, c_re, c_im, d_skip, tm):
    S = us.shape[0]
    nt = S // tm
    w = 8 * SSM_P
    run = tm // 8
    assert run & (run - 1) == 0

    def body(us_ref, ar_ref, ai_ref, br_ref, bi_ref, cr_ref, ci_ref, d_ref, str_ref, sti_ref, ys_ref,
             tab_ref, car_ref, up_ref):
        i = pl.program_id(1)

        @pl.when(i == 0)
        def _():
            car_ref[...] = jnp.zeros_like(car_ref)
            for k, t in enumerate(_scan_tables(*_cpow2(ar_ref[...], ai_ref[...], run.bit_length() - 1), False)):
                tab_ref[k] = t

        _runs_load(us_ref, up_ref, run)
        ub = up_ref[...].astype(MXU)
        str_ref[...] = _dot(ub, br_ref[0])
        sti_ref[...] = _dot(ub, bi_ref[0])
        ar = jnp.broadcast_to(ar_ref[...], (8, w))
        ai = jnp.broadcast_to(ai_ref[...], (8, w))

        def advance(k, state):
            r0 = pl.multiple_of(k * 8, 8)
            sr, si = state
            return (ar * sr - ai * si + str_ref[pl.ds(r0, 8), :], ar * si + ai * sr + sti_ref[pl.ds(r0, 8), :])

        def emit(k, state):
            r0 = pl.multiple_of(k * 8, 8)
            sr, si = advance(k, state)
            str_ref[pl.ds(r0, 8), :] = sr
            sti_ref[pl.ds(r0, 8), :] = si
            return sr, si

        zero = jnp.zeros((8, w), F32)
        er, ei = lax.fori_loop(0, run, advance, (zero, zero))
        cr, ci = car_ref[0:1, :], car_ref[1:2, :]
        tr, ti = _scan_group(er, ei, tab_ref, cr, ci, False)
        r8 = lax.broadcasted_iota(jnp.int32, (8, w), 0)
        start = (jnp.where(r8 == 0, cr, pltpu.roll(tr, 1, 0)), jnp.where(r8 == 0, ci, pltpu.roll(ti, 1, 0)))
        car_ref[0:1, :] = tr[7:8, :]
        car_ref[1:2, :] = ti[7:8, :]
        lax.fori_loop(0, run, emit, start)
        y = _dot_nt(str_ref[...].astype(MXU), cr_ref[0]) - _dot_nt(sti_ref[...].astype(MXU), ci_ref[0])
        _runs_store(y, ys_ref, run)
        ys_ref[...] += d_ref[...] * us_ref[...]

    blk = lambda: pl.BlockSpec((1, 8 * SSM_H, w), lambda j, i: (j, 0, 0))
    return pl.pallas_call(
        body, name="s5_fwd", grid=(SSM_BLK, nt),
        in_specs=[pl.BlockSpec((tm, LANES), lambda j, i: (i, j)),
                  pl.BlockSpec((1, w), lambda j, i: (0, j)), pl.BlockSpec((1, w), lambda j, i: (0, j)),
                  blk(), blk(), blk(), blk(),
                  pl.BlockSpec((1, LANES), lambda j, i: (0, j))],
        out_specs=[pl.BlockSpec((tm, w), lambda j, i: (i, j)), pl.BlockSpec((tm, w), lambda j, i: (i, j)),
                   pl.BlockSpec((tm, LANES), lambda j, i: (i, j))],
        out_shape=[_sds((S, SSM_BLK * w)), _sds((S, SSM_BLK * w)), _sds((S, SSM_W))],
        scratch_shapes=[pltpu.VMEM((8, 8, w), F32), pltpu.VMEM((8, w), F32), pltpu.VMEM((tm, LANES), F32)],
        compiler_params=_cp("parallel", "arbitrary"),
    )(us, abar_re, abar_im, b_re, b_im, c_re, c_im, d_skip)


def _sgu_mix(vnb, ws_ref, grp):
    acc = jnp.zeros(vnb.shape, F32)
    for g in range(SGU_G):
        acc = jnp.where(grp == g, _dot(ws_ref[g], vnb), acc)
    return acc


def _mix_fwd(x, ys, uv, gl, w_glu, b_glu, w_pa, g_sgu, ws, bias_s, w_pb, w_out, g_ffn, tm):
    S = x.shape[0]

    def body(x_ref, ys_ref, uv_ref, gl_ref, wglu_ref, bglu_ref, wpa_ref, gs_ref, ws_ref, bias_ref, wpb_ref, wout_ref,
             gf_ref, yg_ref, yap_ref, sg_ref, ya_ref, yb_ref, m_ref, x1_ref, h2_ref):
        yg = _gelu(ys_ref[...])
        ygb = yg.astype(MXU)
        yg_ref[...] = ygb
        z = _dot(ygb, wglu_ref[...]) + bglu_ref[...]
        yapb = (yg * _sigmoid(z)).astype(MXU)
        yap_ref[...] = yapb
        ya = _dot(yapb, wpa_ref[...])
        ya_ref[...] = ya

        uvg = _gelu(uv_ref[...])
        u2 = uvg[:, :SGU_W]
        v2 = uvg[:, SGU_W:]
        vnb = (v2 * _rms(v2) * gs_ref[...]).astype(MXU)
        grp = lax.broadcasted_iota(jnp.int32, (CHUNK, SGU_W), 1) // SGU_D
        for c in range(tm // CHUNK):
            rs = slice(c * CHUNK, (c + 1) * CHUNK)
            mixed = _sgu_mix(vnb[rs], ws_ref, grp) + bias_ref[...]
            sg_ref[rs, :] = (u2[rs] * mixed).astype(MXU)
        yb = _dot(sg_ref[...], wpb_ref[...])
        yb_ref[...] = yb

        glv = gl_ref[...]
        m = _sigmoid(glv[:, :D_MODEL]) * ya + _sigmoid(glv[:, D_MODEL:]) * yb
        mb = m.astype(MXU)
        m_ref[...] = mb
        x1 = x_ref[...] + _dot(mb, wout_ref[...])
        x1_ref[...] = x1
        h2_ref[...] = (x1 * _rms(x1) * gf_ref[...]).astype(MXU)

    row = lambda n: pl.BlockSpec((tm, n), lambda i: (i, 0))
    return pl.pallas_call(
        body, name="mix_fwd", grid=(S // tm,),
        in_specs=[row(D_MODEL), row(SSM_W), row(2 * SGU_W), row(2 * D_MODEL),
                  _full(w_glu.shape), _full(b_glu.shape), _full(w_pa.shape), _full(g_sgu.shape), _full(ws.shape),
                  _full(bias_s.shape), _full(w_pb.shape), _full(w_out.shape), _full(g_ffn.shape)],
        out_specs=[row(SSM_W), row(SSM_W), row(SGU_W), row(D_MODEL), row(D_MODEL), row(D_MODEL), row(D_MODEL),
                   row(D_MODEL)],
        out_shape=[_sds((S, SSM_W), MXU), _sds((S, SSM_W), MXU), _sds((S, SGU_W), MXU), _sds((S, D_MODEL)),
                   _sds((S, D_MODEL)), _sds((S, D_MODEL), MXU), _sds((S, D_MODEL)), _sds((S, D_MODEL), MXU)],
        compiler_params=_cp("parallel"),
    )(x, ys, uv, gl, w_glu, b_glu, w_pa, g_sgu, ws, bias_s, w_pb, w_out, g_ffn)


def _causal_conv3(u, prev8, cw, cb):
    tm = u.shape[0]
    w0, w1, w2 = cw[0:1], cw[1:2], cw[2:3]
    body = w0 * pltpu.roll(u, 2, 0) + w1 * pltpu.roll(u, 1, 0) + w2 * u + cb
    u8 = u[0:8, :]
    r8 = lax.broadcasted_iota(jnp.int32, u8.shape, 0)
    t1 = prev8[7:8, :]
    t0 = prev8[6:7, :]
    s1 = jnp.where(r8 == 0, t1, pltpu.roll(u8, 1, 0))
    s2 = jnp.where(r8 == 0, t0, jnp.where(r8 == 1, t1, pltpu.roll(u8, 2, 0)))
    first = w0 * s2 + w1 * s1 + w2 * u8 + cb
    return jnp.concatenate([first, body[8:tm, :]], axis=0)


def _causal_conv3_adjoint(d, next8, cw):
    tm = d.shape[0]
    w0, w1, w2 = cw[0:1], cw[1:2], cw[2:3]
    n1 = pltpu.roll(d, tm - 1, 0)
    n2 = pltpu.roll(d, tm - 2, 0)
    body = w2 * d + w1 * n1 + w0 * n2
    d8 = d[tm - 8:tm, :]
    r8 = lax.broadcasted_iota(jnp.int32, d8.shape, 0)
    h0 = next8[0:1, :]
    h1 = next8[1:2, :]
    m1 = jnp.where(r8 == 7, h0, pltpu.roll(d8, 7, 0))
    m2 = jnp.where(r8 == 6, h0, jnp.where(r8 == 7, h1, pltpu.roll(d8, 6, 0)))
    last = w2 * d8 + w1 * m1 + w0 * m2
    out = jnp.concatenate([body[0:tm - 8, :], last], axis=0)
    return out, n1, n2, h0 - d[0:1, :], h1 - d[1:2, :]


def _ffn_fwd(h2, x1, tgt, w_up, conv_w, conv_b, w_down, g_final, tm):
    S = h2.shape[0]
    nt = S // tm
    ncb = FF_NCB

    def body(h2_ref, wa_ref, wb_ref, cwa_ref, cwb_ref, cba_ref, cbb_ref, wd_ref, x1_ref, gf_ref, tgt_ref,
             up_ref, ab_ref, ff_ref, dx2_ref, dx2b_ref, loss_ref, dgf_ref, acc_ref, tail_ref):
        i = pl.program_id(0)
        cb = pl.program_id(1)

        @pl.when(i == 0)
        def _():
            tail_ref[cb] = jnp.zeros((2, 8, FF_CW), F32)

        @pl.when(jnp.logical_and(i == 0, cb == 0))
        def _():
            loss_ref[...] = jnp.zeros_like(loss_ref)
            dgf_ref[...] = jnp.zeros_like(dgf_ref)

        h2v = h2_ref[...]
        ua = _dot_nt(h2v, wa_ref[0])
        ub = _dot_nt(h2v, wb_ref[0])
        up_ref[0, 0] = ua.astype(MXU)
        up_ref[1, 0] = ub.astype(MXU)
        a = _causal_conv3(ua, tail_ref[cb, 0], cwa_ref[0], cba_ref[0])
        b = _causal_conv3(ub, tail_ref[cb, 1], cwb_ref[0], cbb_ref[0])
        tail_ref[cb, 0] = ua[tm - 8:tm, :]
        tail_ref[cb, 1] = ub[tm - 8:tm, :]
        ab_ref[0, 0] = a
        ab_ref[1, 0] = b
        ffb = (a * _sigmoid(a) * b).astype(MXU)
        ff_ref[0] = ffb
        contrib = _dot(ffb, wd_ref[...])

        @pl.when(cb == 0)
        def _():
            acc_ref[...] = contrib

        @pl.when(cb > 0)
        def _():
            acc_ref[...] += contrib

        @pl.when(cb == ncb - 1)
        def _():
            x2 = x1_ref[...] + acc_ref[...]
            r = _rms(x2)
            xn = x2 * r
            g = gf_ref[...]
            diff = xn * g - tgt_ref[...]
            loss_ref[...] += (0.5 / D_MODEL) * jnp.sum(diff * diff)
            dy = diff * (1.0 / D_MODEL)
            dgf_ref[...] += _rowsum(dy * xn)
            dx2 = _rms_bwd(dy * g, xn, r)
            dx2_ref[...] = dx2
            dx2b_ref[...] = dx2.astype(MXU)

    row = lambda n: pl.BlockSpec((tm, n), lambda i, c: (i, 0))
    gate = lambda r: pl.BlockSpec((1, r, FF_CW), lambda i, c: (c, 0, 0))
    lin = lambda r: pl.BlockSpec((1, r, FF_CW), lambda i, c: (ncb + c, 0, 0))
    return pl.pallas_call(
        body, name="ffn_fwd", grid=(nt, ncb),
        in_specs=[row(D_MODEL),
                  pl.BlockSpec((1, FF_CW, D_MODEL), lambda i, c: (c, 0, 0)),
                  pl.BlockSpec((1, FF_CW, D_MODEL), lambda i, c: (ncb + c, 0, 0)),
                  gate(3), lin(3), gate(1), lin(1),
                  pl.BlockSpec((FF_CW, D_MODEL), lambda i, c: (c, 0)),
                  row(D_MODEL), _full((1, D_MODEL)), row(D_MODEL)],
        out_specs=[pl.BlockSpec((2, 1, tm, FF_CW), lambda i, c: (0, c, i, 0)),
                   pl.BlockSpec((2, 1, tm, FF_CW), lambda i, c: (0, c, i, 0)),
                   pl.BlockSpec((1, tm, FF_CW), lambda i, c: (c, i, 0)),
                   row(D_MODEL), row(D_MODEL), _full((1, LANES)), _full((1, D_MODEL))],
        out_shape=[_sds((2, ncb, S, FF_CW), MXU), _sds((2, ncb, S, FF_CW)), _sds((ncb, S, FF_CW), MXU),
                   _sds((S, D_MODEL)), _sds((S, D_MODEL), MXU), _sds((1, LANES)), _sds((1, D_MODEL))],
        scratch_shapes=[pltpu.VMEM((tm, D_MODEL), F32), pltpu.VMEM((ncb, 2, 8, FF_CW), F32)],
        compiler_params=_cp("arbitrary", "arbitrary"),
    )(h2, w_up, w_up, conv_w, conv_w, conv_b, conv_b, w_down, x1, g_final, tgt)


def _ffn_bwd(dx2, up, ab, x1, w_up, conv_w, w_down, g_ffn, tm):
    S = dx2.shape[0]
    nt = S // tm
    ncb = FF_NCB

    def body(dx2_ref, up_ref, ab_ref, cwa_ref, cwb_ref, wd_ref, wa_ref, wb_ref,
             x1_ref, g_ref, dup_ref, dx1_ref, dx1b_ref, dconv_ref, dg_ref, acc_ref, head_ref):
        i = pl.program_id(0)
        cb = pl.program_id(1)
        ri = nt - 1 - i

        @pl.when(i == 0)
        def _():
            head_ref[cb] = jnp.zeros((2, 8, FF_CW), F32)
            dconv_ref[cb] = jnp.zeros((8, FF_CW), F32)
            dconv_ref[ncb + cb] = jnp.zeros((8, FF_CW), F32)

        @pl.when(jnp.logical_and(i == 0, cb == 0))
        def _():
            dg_ref[...] = jnp.zeros_like(dg_ref)

        dx2v = dx2_ref[...]
        dff = _dot_nt(dx2v.astype(MXU), wd_ref[...])
        a = ab_ref[0, 0]
        b = ab_ref[1, 0]
        sa = _sigmoid(a)
        silu = a * sa
        da = (dff * b) * (sa + silu * (1.0 - sa))
        db = dff * silu

        dps = []
        for half, slot, d, cw_ref in ((0, cb, da, cwa_ref), (1, ncb + cb, db, cwb_ref)):
            dp, n1, n2, fix0, fix1 = _causal_conv3_adjoint(d, head_ref[cb, half], cw_ref[0])
            head_ref[cb, half] = d[0:8, :]
            dpb16 = dp.astype(MXU)
            dup_ref[half, 0] = dpb16
            dps.append(dpb16)
            u = up_ref[half, 0].astype(F32)
            u_last = u[tm - 1:tm, :]
            dconv_ref[slot, 0:1, :] += _rowsum(n2 * u) + fix0 * u[tm - 2:tm - 1, :] + fix1 * u_last
            dconv_ref[slot, 1:2, :] += _rowsum(n1 * u) + fix0 * u_last
            dconv_ref[slot, 2:3, :] += _rowsum(d * u)
            dconv_ref[slot, 3:4, :] += _rowsum(d)
        contrib = _dot(dps[0], wa_ref[0]) + _dot(dps[1], wb_ref[0])

        @pl.when(cb == 0)
        def _():
            acc_ref[...] = contrib

        @pl.when(cb > 0)
        def _():
            acc_ref[...] += contrib

        @pl.when(cb == ncb - 1)
        def _():
            x1v = x1_ref[...]
            r = _rms(x1v)
            xn = x1v * r
            dh2 = acc_ref[...]
            dg_ref[...] += _rowsum(dh2 * xn)
            dx1 = dx2v + _rms_bwd(dh2 * g_ref[...], xn, r)
            dx1_ref[...] = dx1
            dx1b_ref[...] = dx1.astype(MXU)

    row = lambda n: pl.BlockSpec((tm, n), lambda i, c: (nt - 1 - i, 0))
    colb = lambda: pl.BlockSpec((2, 1, tm, FF_CW), lambda i, c: (0, c, nt - 1 - i, 0))
    gate = lambda r: pl.BlockSpec((1, r, FF_CW), lambda i, c: (c, 0, 0))
    lin = lambda r: pl.BlockSpec((1, r, FF_CW), lambda i, c: (ncb + c, 0, 0))
    return pl.pallas_call(
        body, name="ffn_bwd", grid=(nt, ncb),
        in_specs=[row(D_MODEL), colb(), colb(), gate(3), lin(3),
                  pl.BlockSpec((FF_CW, D_MODEL), lambda i, c: (c, 0)),
                  pl.BlockSpec((1, FF_CW, D_MODEL), lambda i, c: (c, 0, 0)),
                  pl.BlockSpec((1, FF_CW, D_MODEL), lambda i, c: (ncb + c, 0, 0)),
                  row(D_MODEL), _full((1, D_MODEL))],
        out_specs=[colb(), row(D_MODEL), row(D_MODEL), _full((2 * ncb, 8, FF_CW)), _full((1, D_MODEL))],
        out_shape=[_sds((2, ncb, S, FF_CW), MXU), _sds((S, D_MODEL)), _sds((S, D_MODEL), MXU), _sds((2 * ncb, 8, FF_CW)),
                   _sds((1, D_MODEL))],
        scratch_shapes=[pltpu.VMEM((tm, D_MODEL), F32), pltpu.VMEM((ncb, 2, 8, FF_CW), F32)],
        compiler_params=_cp("arbitrary", "arbitrary"),
    )(dx2, up, ab, conv_w, conv_w, w_down, w_up, w_up, x1, g_ffn)


def _mix_bwd(dx1, gl, ya, yb, ys, uv, w_out, w_pa, w_pb, w_glu, b_glu, g_sgu, ws, ws_t, bias_s, tm):
    S = dx1.shape[0]

    def body(dx1_ref, gl_ref, ya_ref, yb_ref, ys_ref, uv_ref, wout_ref, wpa_ref, wpb_ref, wglu_ref, bglu_ref, gs_ref,
             ws_ref, wst_ref, bias_ref,
             dgl_ref, dya_ref, dyb_ref, dz_ref, dys_ref, duv_ref, dbglu_ref, dgs_ref, dws_ref, dbs_ref,
             du2_ref, dvn_ref):
        i = pl.program_id(0)

        @pl.when(i == 0)
        def _():
            dbglu_ref[...] = jnp.zeros_like(dbglu_ref)
            dgs_ref[...] = jnp.zeros_like(dgs_ref)
            dws_ref[...] = jnp.zeros_like(dws_ref)
            dbs_ref[...] = jnp.zeros_like(dbs_ref)

        dm = _dot_nt(dx1_ref[...].astype(MXU), wout_ref[...])
        glv = gl_ref[...]
        ga = _sigmoid(glv[:, :D_MODEL])
        gb = _sigmoid(glv[:, D_MODEL:])
        dgl_ref[:, :D_MODEL] = (dm * ya_ref[...] * ga * (1.0 - ga)).astype(MXU)
        dgl_ref[:, D_MODEL:] = (dm * yb_ref[...] * gb * (1.0 - gb)).astype(MXU)
        dyab = (dm * ga).astype(MXU)
        dybb = (dm * gb).astype(MXU)
        dya_ref[...] = dyab
        dyb_ref[...] = dybb

        dyap = _dot_nt(dyab, wpa_ref[...])
        yg, dgelu = _gelu_and_grad(ys_ref[...])
        sz = _sigmoid(_dot(yg.astype(MXU), wglu_ref[...]) + bglu_ref[...])
        dz = dyap * yg * sz * (1.0 - sz)
        dzb = dz.astype(MXU)
        dz_ref[...] = dzb
        dbglu_ref[...] += _rowsum(dz)
        dys_ref[...] = (dyap * sz + _dot_nt(dzb, wglu_ref[...])) * dgelu

        dsg = _dot_nt(dybb, wpb_ref[...])
        uvg, duvg = _gelu_and_grad(uv_ref[...])
        u2 = uvg[:, :SGU_W]
        v2 = uvg[:, SGU_W:]
        rv = _rms(v2)
        vhat = v2 * rv
        gs = gs_ref[...]
        vnb = (vhat * gs).astype(MXU)
        grp = lax.broadcasted_iota(jnp.int32, (CHUNK, SGU_W), 1) // SGU_D
        tril = (lax.broadcasted_iota(jnp.int32, (CHUNK, CHUNK), 0)
                >= lax.broadcasted_iota(jnp.int32, (CHUNK, CHUNK), 1))
        for c in range(tm // CHUNK):
            rs = slice(c * CHUNK, (c + 1) * CHUNK)
            vc = vnb[rs]
            mixed = _sgu_mix(vc, ws_ref, grp) + bias_ref[...]
            dsg_c = dsg[rs]
            du2_ref[rs, :] = dsg_c * mixed
            dmx = dsg_c * u2[rs]
            dbs_ref[...] += dmx
            dmb = dmx.astype(MXU)
            dvn_ref[rs, :] = _sgu_mix(dmb, wst_ref, grp)
            for g in range(SGU_G):
                part = _dot_nt(jnp.where(grp == g, dmb, jnp.zeros((), MXU)), vc)
                dws_ref[g] += jnp.where(tril, part, 0.0)
        dvn = dvn_ref[...]
        dgs_ref[...] += _rowsum(dvn * vhat)
        dv2 = _rms_bwd(dvn * gs, vhat, rv)
        duv_ref[:, :SGU_W] = (du2_ref[...] * duvg[:, :SGU_W]).astype(MXU)
        duv_ref[:, SGU_W:] = (dv2 * duvg[:, SGU_W:]).astype(MXU)

    row = lambda n: pl.BlockSpec((tm, n), lambda i: (i, 0))
    return pl.pallas_call(
        body, name="mix_bwd", grid=(S // tm,),
        in_specs=[row(D_MODEL), row(2 * D_MODEL), row(D_MODEL), row(D_MODEL), row(SSM_W), row(2 * SGU_W),
                  _full(w_out.shape), _full(w_pa.shape), _full(w_pb.shape), _full(w_glu.shape), _full(b_glu.shape),
                  _full(g_sgu.shape), _full(ws.shape), _full(ws_t.shape), _full(bias_s.shape)],
        out_specs=[row(2 * D_MODEL), row(D_MODEL), row(D_MODEL), row(SSM_W), row(SSM_W), row(2 * SGU_W),
                   _full((1, SSM_W)), _full((1, SGU_W)), _full((SGU_G, CHUNK, CHUNK)), _full((CHUNK, SGU_W))],
        out_shape=[_sds((S, 2 * D_MODEL), MXU), _sds((S, D_MODEL), MXU), _sds((S, D_MODEL), MXU), _sds((S, SSM_W), MXU),
                   _sds((S, SSM_W)), _sds((S, 2 * SGU_W), MXU),
                   _sds((1, SSM_W)), _sds((1, SGU_W)), _sds((SGU_G, CHUNK, CHUNK)), _sds((CHUNK, SGU_W))],
        scratch_shapes=[pltpu.VMEM((tm, SGU_W), F32), pltpu.VMEM((tm, SGU_W), F32)],
        compiler_params=_cp("arbitrary"),
    )(dx1, gl, ya, yb, ys, uv, w_out, w_pa, w_pb, w_glu, b_glu, g_sgu, ws, ws_t, bias_s)


def _s5_bwd(dys, us, st_re, st_im, abar_re, abar_im, b_re, b_im, c_re, c_im, d_skip, tm):
    S = us.shape[0]
    nt = S // tm
    w = 8 * SSM_P
    hb = tm // 8
    run = tm // 8
    assert run & (run - 1) == 0

    def body(dys_ref, us_ref, str_ref, sti_ref, hr_ref, hi_ref, ar_ref, ai_ref, br_ref, bi_ref, cr_ref, ci_ref, d_ref,
             dus_ref, dab_ref, dd_ref, dbr_ref, dbi_ref, dcr_ref, dci_ref,
             tab_ref, car_ref, gr_ref, gi_ref, dyp_ref, up_ref, dun_ref):
        i = pl.program_id(1)
        ri = nt - 1 - i

        @pl.when(i == 0)
        def _():
            car_ref[...] = jnp.zeros_like(car_ref)
            for k, t in enumerate(_scan_tables(*_cpow2(ar_ref[...], -ai_ref[...], run.bit_length() - 1), True)):
                tab_ref[k] = t
            for r in (dab_ref, dd_ref, dbr_ref, dbi_ref, dcr_ref, dci_ref):
                r[...] = jnp.zeros_like(r)

        _runs_load(dys_ref, dyp_ref, run)
        _runs_load(us_ref, up_ref, run)
        dyb = dyp_ref[...].astype(MXU)
        gr_ref[...] = _dot(dyb, cr_ref[0])
        gi_ref[...] = -_dot(dyb, ci_ref[0])
        ar = jnp.broadcast_to(ar_ref[...], (8, w))
        ai = jnp.broadcast_to(-ai_ref[...], (8, w))

        def advance(kk, state):
            r0 = pl.multiple_of((run - 1 - kk) * 8, 8)
            gr, gi = state
            return (ar * gr - ai * gi + gr_ref[pl.ds(r0, 8), :], ar * gi + ai * gr + gi_ref[pl.ds(r0, 8), :])

        def emit(kk, state):
            r0 = pl.multiple_of((run - 1 - kk) * 8, 8)
            gr, gi = advance(kk, state)
            gr_ref[pl.ds(r0, 8), :] = gr
            gi_ref[pl.ds(r0, 8), :] = gi
            return gr, gi

        zero = jnp.zeros((8, w), F32)
        er, ei = lax.fori_loop(0, run, advance, (zero, zero))
        cr, ci = car_ref[0:1, :], car_ref[1:2, :]
        tr, ti = _scan_group(er, ei, tab_ref, cr, ci, True)
        r8 = lax.broadcasted_iota(jnp.int32, (8, w), 0)
        start = (jnp.where(r8 == 7, cr, pltpu.roll(tr, 7, 0)), jnp.where(r8 == 7, ci, pltpu.roll(ti, 7, 0)))
        car_ref[0:1, :] = tr[0:1, :]
        car_ref[1:2, :] = ti[0:1, :]
        lax.fori_loop(0, run, emit, start)

        gsr = gr_ref[...]
        gsi = gi_ref[...]
        sr = str_ref[...]
        si = sti_ref[...]
        first = ri == 0

        def previous(s, halo_ref):
            head = jnp.where(r8 == 0, jnp.where(first, 0.0, halo_ref[7:8, :]), pltpu.roll(s[tm - 8:tm, :], 1, 0))
            return jnp.concatenate([head, s[0:tm - 8, :]], axis=0)

        spr = previous(sr, hr_ref)
        spi = previous(si, hi_ref)
        dab_ref[0, 0:1, :] += _rowsum(gsr * spr + gsi * spi)
        dab_ref[0, 1:2, :] += _rowsum(gsi * spr - gsr * spi)

        gbr = gsr.astype(MXU)
        gbi = gsi.astype(MXU)
        _runs_store(_dot_nt(gbr, br_ref[0]) + _dot_nt(gbi, bi_ref[0]), dun_ref, run)
        dys_v = dys_ref[...]
        dus_ref[...] = (dun_ref[...] + d_ref[...] * dys_v).astype(MXU)
        dd_ref[0, 0:1, :] += _rowsum(dys_v * us_ref[...])
        ub = up_ref[...].astype(MXU)
        dbr_ref[0] += _dot_tn(ub, gbr)
        dbi_ref[0] += _dot_tn(ub, gbi)
        dcr_ref[0] += _dot_tn(dyb, sr.astype(MXU))
        dci_ref[0] -= _dot_tn(dyb, si.astype(MXU))

    blk = lambda: pl.BlockSpec((1, 8 * SSM_H, w), lambda j, i: (j, 0, 0))
    rowl = lambda: pl.BlockSpec((tm, LANES), lambda j, i: (nt - 1 - i, j))
    roww = lambda: pl.BlockSpec((tm, w), lambda j, i: (nt - 1 - i, j))
    halo = lambda: pl.BlockSpec((8, w), lambda j, i: (jnp.maximum((nt - 1 - i) * hb - 1, 0), j))
    return pl.pallas_call(
        body, name="s5_bwd", grid=(SSM_BLK, nt),
        in_specs=[rowl(), rowl(), roww(), roww(), halo(), halo(),
                  pl.BlockSpec((1, w), lambda j, i: (0, j)), pl.BlockSpec((1, w), lambda j, i: (0, j)),
                  blk(), blk(), blk(), blk(),
                  pl.BlockSpec((1, LANES), lambda j, i: (0, j))],
        out_specs=[rowl(),
                   pl.BlockSpec((1, 8, w), lambda j, i: (j, 0, 0)), pl.BlockSpec((1, 8, LANES), lambda j, i: (j, 0, 0)),
                   blk(), blk(), blk(), blk()],
        out_shape=[_sds((S, SSM_W), MXU), _sds((SSM_BLK, 8, w)), _sds((SSM_BLK, 8, LANES)),
                   _sds((SSM_BLK, 8 * SSM_H, w)), _sds((SSM_BLK, 8 * SSM_H, w)),
                   _sds((SSM_BLK, 8 * SSM_H, w)), _sds((SSM_BLK, 8 * SSM_H, w))],
        scratch_shapes=[pltpu.VMEM((8, 8, w), F32), pltpu.VMEM((8, w), F32),
                        pltpu.VMEM((tm, w), F32), pltpu.VMEM((tm, w), F32),
                        pltpu.VMEM((tm, LANES), F32), pltpu.VMEM((tm, LANES), F32), pltpu.VMEM((tm, LANES), F32)],
        compiler_params=_cp("parallel", "arbitrary"),
    )(dys, us, st_re, st_im, st_re, st_im, abar_re, abar_im, b_re, b_im, c_re, c_im, d_skip)


def _in_bwd(dus, duv, dgl, dx1, x, g_mix, w_in, tm):
    S = x.shape[0]

    def body(dus_ref, duv_ref, dgl_ref, dx1_ref, x_ref, g_ref, w_ref, gx_ref, dg_ref):
        @pl.when(pl.program_id(0) == 0)
        def _():
            dg_ref[...] = jnp.zeros_like(dg_ref)

        dh = (_dot(dus_ref[...], w_ref[0:SSM_W, :])
              + _dot(duv_ref[...], w_ref[SSM_W:SSM_W + 2 * SGU_W, :])
              + _dot(dgl_ref[...], w_ref[SSM_W + 2 * SGU_W:, :]))
        xv = x_ref[...]
        r = _rms(xv)
        xn = xv * r
        dg_ref[...] += _rowsum(dh * xn)
        gx_ref[...] = dx1_ref[...] + _rms_bwd(dh * g_ref[...], xn, r)

    row = lambda n: pl.BlockSpec((tm, n), lambda i: (i, 0))
    return pl.pallas_call(
        body, name="in_bwd", grid=(S // tm,),
        in_specs=[row(SSM_W), row(2 * SGU_W), row(2 * D_MODEL), row(D_MODEL), row(D_MODEL), _full((1, D_MODEL)),
                  _full(w_in.shape)],
        out_specs=[row(D_MODEL), _full((1, D_MODEL))],
        out_shape=[_sds((S, D_MODEL)), _sds((1, D_MODEL))],
        compiler_params=_cp("arbitrary"),
    )(dus, duv, dgl, dx1, x, g_mix, w_in)


def _pick(n, cands):
    for c in cands:
        if n % c == 0:
            return c
    return n


def _wgrad_split(a, b, nsplit, tk, name):
    S, K = a.shape
    N = b.shape[1]
    c = N // nsplit

    def body(a_ref, b_ref, o_ref):
        prod = _dot_tn(a_ref[...], b_ref[...])
        for d in range(nsplit):
            o_ref[d] = prod[:, c * d:c * (d + 1)].astype(MXU)

    return pl.pallas_call(
        body, name=name, grid=(K // tk,),
        in_specs=[pl.BlockSpec((S, tk), lambda k: (0, k)), _full((S, N))],
        out_specs=pl.BlockSpec((nsplit, tk, c), lambda k: (0, k, 0)),
        out_shape=_sds((nsplit, K, c), MXU),
        compiler_params=_cp("parallel"),
    )(a, b)


def _wgrad_in_t(dps, h1, name):
    S, K = h1.shape
    cw = 512
    counts = [b.shape[1] // cw for b in dps]
    starts = [sum(counts[:i]) for i in range(len(dps))]
    nblk = sum(counts)

    def body(*refs):
        b_refs = refs[:len(dps)]
        h_ref, o_ref = refs[len(dps):]
        j = pl.program_id(0)
        for b_ref, st, cnt in zip(b_refs, starts, counts):
            @pl.when(jnp.logical_and(j >= st, j < st + cnt))
            def _():
                o_ref[...] = _dot_tn(b_ref[...], h_ref[...]).astype(MXU)

    def src_spec(st, cnt):
        return pl.BlockSpec((S, cw), lambda j: (0, jnp.clip(j - st, 0, cnt - 1)))

    return pl.pallas_call(
        body, name=name, grid=(nblk,),
        in_specs=[src_spec(st, cnt) for st, cnt in zip(starts, counts)] + [_full((S, K))],
        out_specs=pl.BlockSpec((cw, K), lambda j: (j, 0)),
        out_shape=_sds((nblk * cw, K), MXU),
        compiler_params=_cp("arbitrary"),
    )(*dps, h1)


def _wgrad_blk(a3, b3, nblk, a_of, b_of, name):
    S, K = a3.shape[1:]
    N = b3.shape[2]

    def body(a_ref, b_ref, o_ref):
        o_ref[0] = _dot_tn(a_ref[0], b_ref[0]).astype(MXU)

    return pl.pallas_call(
        body, name=name, grid=(nblk,),
        in_specs=[pl.BlockSpec((1, S, K), lambda b: (a_of(b), 0, 0)),
                  pl.BlockSpec((1, S, N), lambda b: (b_of(b), 0, 0))],
        out_specs=pl.BlockSpec((1, K, N), lambda b: (b, 0, 0)),
        out_shape=_sds((nblk, K, N), MXU),
        compiler_params=_cp("parallel"),
    )(a3, b3)


def _assemble_cols(blocks_list, name):
    def body(*refs):
        n = len(blocks_list)
        for b_ref, o_ref in zip(refs[:n], refs[n:]):
            c = b_ref.shape[2]
            for d in range(N_DEV):
                o_ref[:, c * d:c * (d + 1)] = b_ref[d]

    return pl.pallas_call(
        body, name=name,
        out_shape=[_sds((b.shape[1], N_DEV * b.shape[2]), b.dtype) for b in blocks_list],
        compiler_params=pltpu.CompilerParams(vmem_limit_bytes=VMEM_LIMIT),
    )(*blocks_list)


def _tile(S, want):
    return want if S % want == 0 else S


def _local_step(x, tgt, p, ffn_weights, grads_out):
    S = x.shape[0]
    tm = _tile(S, 256)
    tl = _tile(S, 512)

    rep = lambda a: jnp.repeat(a, SSM_H, axis=0)
    are = rep(p["a_re"])
    aim = rep(p["a_im"])
    ldt = jnp.broadcast_to(rep(p["log_dt"].reshape(SSM_G, 1)), are.shape)
    br_t = p["b_re_t"].reshape(are.shape)
    bi_t = p["b_im_t"].reshape(are.shape)
    abr, abi, bbr, bbi = _s5_params_fwd(are, aim, ldt, br_t, bi_t)
    head = lambda a: a.reshape(SSM_G, SSM_H, SSM_P)[:, 0, :].reshape(1, SSM_G * SSM_P)
    abar_re, abar_im = head(abr), head(abi)
    bd_br = _blockdiag(bbr).astype(MXU)
    bd_bi = _blockdiag(bbi).astype(MXU)
    bd_cr = _blockdiag(p["c_re"].reshape(are.shape)).astype(MXU)
    bd_ci = _blockdiag(p["c_im"].reshape(are.shape)).astype(MXU)
    d_skip = p["d_skip"].reshape(1, SSM_W)

    tril = jnp.tril(jnp.ones((CHUNK, CHUNK), dtype=bool))
    ws = jnp.where(tril[None], p["w_s"], 0.0)
    ws_b = ws.astype(MXU)
    ws_t = ws.transpose(0, 2, 1).astype(MXU)
    bias_s = jnp.repeat(p["b_s"].T, SGU_D, axis=1)

    g_mix = p["g_mix"].reshape(1, D_MODEL)
    g_ffn = p["g_ffn"].reshape(1, D_MODEL)
    g_final = p["g_final"].reshape(1, D_MODEL)
    g_sgu = p["g_sgu"].reshape(1, SGU_W)
    b_glu = p["b_glu"].reshape(1, SSM_W)
    conv_b = p["conv_b"].reshape(N_DEV, 1, FF_CW)

    h1, us, uv, gl = _in_fwd(x, g_mix, p["w_in_t"], tm)
    st_re, st_im, ys = _s5_fwd(us, abar_re, abar_im, bd_br, bd_bi, bd_cr, bd_ci, d_skip, tl)
    yg, yap, sg, ya, yb, m, x1, h2 = _mix_fwd(x, ys, uv, gl, p["w_glu"], b_glu, p["w_proj_a"], g_sgu, ws_b, bias_s,
                                              p["w_proj_b"], p["w_out"], g_ffn, tm)
    w_up, conv_w, w_down = ffn_weights(h2)
    up, ab, ff, dx2, dx2b, loss, dg_final = _ffn_fwd(h2, x1, tgt, w_up, conv_w, conv_b, w_down, g_final, tl)

    dup, dx1, dx1b, dconv, dg_ffn = _ffn_bwd(dx2, up, ab, x1, w_up, conv_w, w_down, g_ffn, tl)
    rows8 = lambda g: g.reshape(N_DEV, g.shape[1] // N_DEV, g.shape[2])
    g_up = _wgrad_blk(dup.reshape(N_DEV, S, FF_CW), h2[None], N_DEV, lambda b: b, lambda b: 0, "wgrad_up")
    g_down = _wgrad_blk(ff, dx2b[None], FF_NCB, lambda b: b, lambda b: 0, "wgrad_down").reshape(
        N_DEV, D_FF // N_DEV, D_MODEL)
    token = grads_out(("w_up", "w_down"), (g_up, g_down))
    dgl, dya, dyb, dz, dys, duv, db_glu, dg_sgu, dws, dbs = _mix_bwd(
        dx1, gl, ya, yb, ys, uv, p["w_out"], p["w_proj_a"], p["w_proj_b"], p["w_glu"], b_glu + token[0:1, 0:1], g_sgu,
        ws_b, ws_t, bias_s, tm)
    token = grads_out(("w_glu", "w_proj_a", "w_proj_b", "w_out"),
                      (rows8(_wgrad_split(yg, dz, 1, SSM_W, "wgrad_glu")),
                       _wgrad_split(yap, dya, N_DEV, SSM_W, "wgrad_pa"),
                       _wgrad_split(sg, dyb, N_DEV, SGU_W, "wgrad_pb"),
                       rows8(_wgrad_split(m, dx1b, 1, 512, "wgrad_out"))))
    dus, dab, dd, dbbr, dbbi, dcr, dci = _s5_bwd(dys, us, st_re, st_im, abar_re, abar_im, bd_br, bd_bi, bd_cr, bd_ci,
                                                 d_skip + token[0:1, 0:1], tl)
    g_in = _wgrad_in_t([dus, duv, dgl], h1, "wgrad_in")
    token = grads_out(("w_in",), (g_in.reshape(N_DEV, g_in.shape[0] // N_DEV, D_MODEL),))
    grad_x, dg_mix = _in_bwd(dus, duv, dgl, dx1, x, g_mix + token[0:1, 0:1], p["w_in_t"], tm)

    spread = lambda v: jnp.repeat(v.reshape(SSM_G, SSM_P), SSM_H, axis=0) * (1.0 / SSM_H)
    dabr = spread(dab[:, 0, :])
    dabi = spread(dab[:, 1, :])
    dare, daim, dldt, dbr_t, dbi_t = _s5_params_bwd(are, aim, ldt, br_t, bi_t, dabr, dabi,
                                                    _unblockdiag(dbbr), _unblockdiag(dbbi))
    fold = lambda a: a.reshape(SSM_G, SSM_H, SSM_P).sum(axis=1)

    grads = {
        "g_mix": dg_mix,
        "a_re": fold(dare), "a_im": fold(daim), "log_dt": fold(dldt).sum(axis=1),
        "b_re": dbr_t, "b_im": dbi_t,
        "c_re": _unblockdiag(dcr).reshape(SSM_G, SSM_H, SSM_P),
        "c_im": _unblockdiag(dci).reshape(SSM_G, SSM_H, SSM_P),
        "d_skip": dd[:, 0, :].reshape(SSM_W),
        "b_glu": db_glu,
        "g_sgu": dg_sgu,
        "w_s": dws,
        "b_s": dbs.reshape(CHUNK, SGU_G, SGU_D).sum(axis=-1).T,
        "g_ffn": dg_ffn,
        "conv_w": dconv[:, 0:3, :],
        "conv_b": dconv[:, 3, :].reshape(2 * D_FF),
        "g_final": dg_final,
    }
    return loss, grad_x, grads


_ANY = pl.BlockSpec(memory_space=pl.ANY)
_MESH = pl.DeviceIdType.MESH


def _allgather(shards, dtypes, name, cast_only=()):
    n = len(shards)
    e = len(cast_only)

    def body(*refs):
        in_refs, extra_in = refs[:n], refs[n:n + e]
        out_refs, extra_out = refs[n + e:2 * n + e], refs[2 * n + e:2 * n + 2 * e]
        stage = refs[2 * n + 2 * e:3 * n + 2 * e]
        send_sems, recv_sems, local_sems = refs[3 * n + 2 * e:]
        for a in range(n):
            stage[a][...] = in_refs[a][...].astype(dtypes[a])
        for i in range(e):
            extra_out[i][...] = extra_in[i][...].astype(MXU)
        x, y, c = lax.axis_index("x"), lax.axis_index("y"), lax.axis_index("c")
        me, sibling = (x, y, c), (x, y, 1 - c)
        chips = [(1 - x, y), (x, 1 - y), (1 - x, 1 - y)]

        def slot(a, px, py, pc):
            return out_refs[a].at[4 * px + 2 * py + pc]

        def copy(a, k, block, to, src=None):
            return pltpu.make_async_remote_copy(
                src_ref=slot(a, *block) if src is None else src, dst_ref=slot(a, *block),
                send_sem=send_sems.at[a, k], recv_sem=recv_sems.at[a, k], device_id=to, device_id_type=_MESH)

        mine = [pltpu.make_async_copy(stage[a], slot(a, *me), local_sems.at[a]) for a in range(n)]
        for cp in mine:
            cp.start()
        first = []
        for j, chip in enumerate(chips):
            first += [copy(a, 1 + j, me, (*chip, c), src=stage[a]) for a in range(n)]
        first += [copy(a, 0, me, sibling, src=stage[a]) for a in range(n)]
        for cp in first:
            cp.start()
        passed = []
        for j, chip in enumerate(chips):
            for a in range(n):
                copy(a, 1 + j, (*chip, c), me).wait_recv()
                fwd = copy(a, 4 + j, (*chip, c), sibling)
                fwd.start()
                passed.append(fwd)
        for a in range(n):
            copy(a, 0, sibling, me).wait_recv()
        for j, chip in enumerate(chips):
            for a in range(n):
                copy(a, 4 + j, (*chip, 1 - c), me).wait_recv()
        for cp in first + passed:
            cp.wait_send()
        for cp in mine:
            cp.wait()

    vmem = pl.BlockSpec(memory_space=pltpu.VMEM)
    res = pl.pallas_call(
        body, name=name, in_specs=[vmem] * (n + e), out_specs=[_ANY] * n + [vmem] * e,
        out_shape=[_sds((N_DEV,) + s.shape, dt) for s, dt in zip(shards, dtypes)]
                  + [_sds(s.shape, MXU) for s in cast_only],
        scratch_shapes=[pltpu.VMEM(s.shape, dt) for s, dt in zip(shards, dtypes)]
                       + [pltpu.SemaphoreType.DMA((n, 7)), pltpu.SemaphoreType.DMA((n, 7)), pltpu.SemaphoreType.DMA((n,))],
        compiler_params=pltpu.CompilerParams(vmem_limit_bytes=VMEM_LIMIT),
    )(*shards, *cast_only)
    return res[:n], res[n:]


def _all_to_all(sends, name):
    n = len(sends)

    def body(*refs):
        send_refs, recv_refs = refs[:n], refs[n:2 * n]
        send_sems, recv_sems, local_sems = refs[2 * n:]
        x, y, c = lax.axis_index("x"), lax.axis_index("y"), lax.axis_index("c")
        me = 4 * x + 2 * y + c
        mine = [pltpu.make_async_copy(send_refs[a].at[me], recv_refs[a].at[me], local_sems.at[a]) for a in range(n)]
        for cp in mine:
            cp.start()
        copies = []
        for k in (2, 4, 6, 3, 5, 7, 1):
            px = 1 - x if k & 4 else x
            py = 1 - y if k & 2 else y
            pc = 1 - c if k & 1 else c
            peer = 4 * px + 2 * py + pc
            for a in range(n):
                sems = dict(send_sem=send_sems.at[a, k - 1], recv_sem=recv_sems.at[a, k - 1],
                            device_id=(px, py, pc), device_id_type=_MESH)
                cp = pltpu.make_async_remote_copy(src_ref=send_refs[a].at[peer], dst_ref=recv_refs[a].at[me], **sems)
                cp.start()
                landing = pltpu.make_async_remote_copy(src_ref=send_refs[a].at[peer], dst_ref=recv_refs[a].at[peer],
                                                       **sems)
                copies.append((cp, landing))
        for _, landing in copies:
            landing.wait_recv()
        for cp, _ in copies:
            cp.wait_send()
        for cp in mine:
            cp.wait()

    return pl.pallas_call(
        body, name=name, in_specs=[_ANY] * n, out_specs=[_ANY] * n,
        out_shape=[_sds(s.shape, s.dtype) for s in sends],
        scratch_shapes=[pltpu.SemaphoreType.DMA((n, 7)), pltpu.SemaphoreType.DMA((n, 7)), pltpu.SemaphoreType.DMA((n,))],
    )(*sends)


_HBM = pl.BlockSpec(memory_space=pltpu.HBM)
_SEM = pl.BlockSpec(memory_space=pltpu.SEMAPHORE)
_EFFECT = pltpu.SideEffectType.DATAFLOW_SIDE_EFFECTING
_PEER_ORDER = (2, 4, 6, 3, 5, 7, 1)


def _peer(k):
    x, y, c = lax.axis_index("x"), lax.axis_index("y"), lax.axis_index("c")
    px = 1 - x if k & 4 else x
    py = 1 - y if k & 2 else y
    pc = 1 - c if k & 1 else c
    return (px, py, pc), 4 * px + 2 * py + pc


def _push_start(srcs, lands, slotted, name):
    n = len(srcs)

    def body(*refs):
        src_refs, land_refs = refs[:n], refs[n:2 * n]
        send_sems, recv_sems, token = refs[2 * n], refs[2 * n + 1], refs[-1]
        me = 4 * lax.axis_index("x") + 2 * lax.axis_index("y") + lax.axis_index("c")
        for k in _PEER_ORDER:
            dev, peer = _peer(k)
            for a in range(n):
                pltpu.make_async_remote_copy(
                    src_ref=src_refs[a].at[peer] if slotted else src_refs[a], dst_ref=land_refs[a].at[me],
                    send_sem=send_sems.at[7 * a + k - 1], recv_sem=recv_sems.at[7 * a + k - 1],
                    device_id=dev, device_id_type=_MESH).start()
        token[...] = jnp.zeros_like(token)

    bufs = list(srcs) + list(lands)
    res = pl.pallas_call(
        body, name=name, in_specs=[_HBM] * (2 * n),
        out_specs=(_SEM, _SEM, *[_HBM] * (2 * n), pl.BlockSpec(memory_space=pltpu.VMEM)),
        out_shape=(pltpu.SemaphoreType.DMA((7 * n,)), pltpu.SemaphoreType.DMA((7 * n,)),
                   *[pltpu.HBM(b.shape, b.dtype) for b in bufs], _sds((8, LANES))),
        input_output_aliases={i: 2 + i for i in range(2 * n)},
        compiler_params=pltpu.CompilerParams(has_side_effects=_EFFECT),
    )(*[pltpu.with_memory_space_constraint(b, pltpu.HBM) for b in bufs])
    return res[0], res[1], res[2:2 + n], res[2 + n:2 + 2 * n], res[-1]


def _push_wait(send_sems, recv_sems, srcs, lands, slotted, after, name):
    n = len(srcs)

    def body(*refs):
        src_refs, land_refs = refs[:n], refs[n:2 * n]
        send_sems, recv_sems = refs[2 * n], refs[2 * n + 1]
        for k in _PEER_ORDER:
            dev, peer = _peer(k)
            for a in range(n):
                cp = pltpu.make_async_remote_copy(
                    src_ref=src_refs[a].at[peer] if slotted else src_refs[a], dst_ref=land_refs[a].at[peer],
                    send_sem=send_sems.at[7 * a + k - 1], recv_sem=recv_sems.at[7 * a + k - 1],
                    device_id=dev, device_id_type=_MESH)
                cp.wait_send()
                cp.wait_recv()

    bufs = list(srcs) + list(lands)
    res = pl.pallas_call(
        body, name=name, in_specs=[_HBM] * (2 * n) + [_SEM, _SEM] + [_ANY] * len(after), out_specs=[_HBM] * (2 * n),
        out_shape=[pltpu.HBM(b.shape, b.dtype) for b in bufs],
        input_output_aliases={i: i for i in range(2 * n)},
        compiler_params=pltpu.CompilerParams(has_side_effects=_EFFECT),
    )(*bufs, send_sems, recv_sems, *after)
    return res[n:]


def _adamw(w, g, m, v):
    m2 = ADAM_B1 * m + (1.0 - ADAM_B1) * g
    v2 = ADAM_B2 * v + (1.0 - ADAM_B2) * (g * g)
    m_hat = m2 / (1.0 - ADAM_B1 ** ADAM_STEP)
    v_hat = v2 / (1.0 - ADAM_B2 ** ADAM_STEP)
    delta = -ADAM_LR * (m_hat / (jnp.sqrt(v_hat) + ADAM_EPS) + ADAM_WD * w)
    return delta, m2, v2


def _adam_shard(parts, w, m, v, name):
    _, r, c = w.shape
    tr = max(t for t in range(16, 257, 16) if r % t == 0)

    def body(p_ref, w_ref, m_ref, v_ref, g_ref, d_ref, m2_ref, v2_ref):
        g = p_ref[0].astype(F32)
        for s in range(1, N_DEV):
            g = g + p_ref[s].astype(F32)
        g_ref[0] = g
        d_ref[0], m2_ref[0], v2_ref[0] = _adamw(w_ref[0], g, m_ref[0], v_ref[0])

    row = lambda: pl.BlockSpec((1, tr, c), lambda i: (0, i, 0))
    return pl.pallas_call(
        body, name=name, grid=(r // tr,),
        in_specs=[pl.BlockSpec((N_DEV, tr, c), lambda i: (0, i, 0)), row(), row(), row()],
        out_specs=[row(), row(), row(), row()], out_shape=[_sds((1, r, c))] * 4,
        compiler_params=_cp("parallel"),
    )(parts, w, m, v)


def _adam_small(gs, ws, ms, vs, name):
    n = len(gs)

    def body(*refs):
        ins, outs = refs[:4 * n], refs[4 * n:]
        for i in range(n):
            g = ins[i][...]
            d, m2, v2 = _adamw(ins[n + i][...], g, ins[2 * n + i][...], ins[3 * n + i][...])
            outs[i][...] = d
            outs[n + i][...] = m2
            outs[2 * n + i][...] = v2

    res = pl.pallas_call(
        body, name=name, out_shape=[_sds(w.shape) for w in ws] * 3,
        compiler_params=pltpu.CompilerParams(vmem_limit_bytes=VMEM_LIMIT),
    )(*gs, *ws, *ms, *vs)
    return res[:n], res[n:2 * n], res[2 * n:]


def _sum_slots(parts, name):
    R = parts.shape[1]

    def body(p_ref, o_ref):
        g = p_ref[0]
        for s in range(1, N_DEV):
            g = g + p_ref[s]
        o_ref[...] = g

    return pl.pallas_call(body, name=name, out_shape=_sds((R, LANES)))(parts)


def _pad_to(a, n, axis):
    extra = n - a.shape[axis]
    if extra == 0:
        return a
    widths = [(0, 0)] * a.ndim
    widths[axis] = (0, extra)
    return jnp.pad(a, widths)


def _ceil_to(n, k):
    return -(-n // k) * k


def _pack_rows(flats, rows_multiple):
    parts = [_pad_to(f, _ceil_to(f.shape[-1], LANES), f.ndim - 1) for f in flats]
    cat = jnp.concatenate(parts, axis=-1)
    total = _ceil_to(cat.shape[-1], LANES * rows_multiple)
    cat = _pad_to(cat, total, cat.ndim - 1)
    return cat.reshape(cat.shape[:-1] + (total // LANES, LANES))


def _unpack_rows(buf, sizes):
    flat = buf.reshape(buf.shape[:-2] + (-1,))
    out, off = [], 0
    for n in sizes:
        out.append(flat[..., off:off + n])
        off += _ceil_to(n, LANES)
    return out


_MIX_BIG = ("w_in", "w_glu", "w_proj_a", "w_proj_b", "w_out")
_BIG = _MIX_BIG + ("w_up", "w_down")
_SMALL = ("g_mix", "a_re", "a_im", "log_dt", "b_re", "b_im", "c_re", "c_im", "d_skip", "b_glu", "g_sgu", "w_s", "b_s",
          "g_ffn", "conv_b", "g_final")
_SMALL_ROWS_MULTIPLE = 8 * N_DEV
_TRANSPOSED = ("w_in", "w_up", "b_re", "b_im")


def _as_2d(a):
    return a.reshape(-1, a.shape[-1]) if a.ndim > 1 else a.reshape(1, -1)


def kernel(x, g_mix, w_in, a_re, a_im, log_dt, b_re, b_im, c_re, c_im, d_skip, w_glu, b_glu, w_proj_a, g_sgu, w_s, b_s, w_proj_b, w_out, g_ffn, w_up, conv_w, conv_b, w_down, g_final, loss_target, m_g_mix, m_w_in, m_a_re, m_a_im, m_log_dt, m_b_re, m_b_im, m_c_re, m_c_im, m_d_skip, m_w_glu, m_b_glu, m_w_proj_a, m_g_sgu, m_w_s, m_b_s, m_w_proj_b, m_w_out, m_g_ffn, m_w_up, m_conv_w, m_conv_b, m_w_down, m_g_final, v_g_mix, v_w_in, v_a_re, v_a_im, v_log_dt, v_b_re, v_b_im, v_c_re, v_c_im, v_d_skip, v_w_glu, v_b_glu, v_w_proj_a, v_g_sgu, v_w_s, v_b_s, v_w_proj_b, v_w_out, v_g_ffn, v_w_up, v_conv_w, v_conv_b, v_w_down, v_g_final):
    args = dict(locals())
    me = 4 * lax.axis_index("x") + 2 * lax.axis_index("y") + lax.axis_index("c")

    def own_slot(buf, block):
        return lax.dynamic_update_slice(buf, block[None], (me,) + (0,) * block.ndim)

    for n in _TRANSPOSED:
        for pre in ("", "m_", "v_"):
            args[pre + n] = jnp.swapaxes(args[pre + n], -1, -2)
    gathered, (up_sh, down_sh) = _allgather([args[n][0] for n in _MIX_BIG], [MXU] * len(_MIX_BIG), "allgather_mixer",
                                            cast_only=(args["w_up"][0], w_down[0]))
    g = dict(zip(_MIX_BIG, gathered))
    ffn_srcs = [up_sh, down_sh, conv_w[0]]
    ffn_lands = [own_slot(lax.empty((N_DEV,) + s.shape, s.dtype), s) for s in ffn_srcs]
    ag_send, ag_recv, ffn_srcs, ffn_lands, ag_token = _push_start(ffn_srcs, ffn_lands, False, "push_ffn_weights")
    w_pa_full, w_pb_full = _assemble_cols([g["w_proj_a"], g["w_proj_b"]], "assemble_cols")
    p = {n: (args[n][0] if n != "g_final" else args[n]) for n in _SMALL if n not in _TRANSPOSED}
    p.update(w_in_t=g["w_in"].reshape(SSM_W + 2 * SGU_W + 2 * D_MODEL, D_MODEL), w_proj_a=w_pa_full, w_proj_b=w_pb_full,
             w_glu=g["w_glu"].reshape(SSM_W, SSM_W), w_out=g["w_out"].reshape(D_MODEL, D_MODEL),
             b_re_t=args["b_re"][0], b_im_t=args["b_im"][0])
    p["g_mix"] = p["g_mix"] + ag_token[0:1, 0:1]

    def ffn_weights(after):
        w_up_g, w_down_g, conv_w_g = _push_wait(ag_send, ag_recv, ffn_srcs, ffn_lands, False, [after], "wait_ffn_weights")
        return w_up_g, conv_w_g, w_down_g.reshape(D_FF, D_MODEL)

    pushes = []

    def grads_out(names, sends):
        lands = [own_slot(lax.empty(s.shape, s.dtype), lax.dynamic_index_in_dim(s, me, 0, keepdims=False))
                 for s in sends]
        send_sems, recv_sems, srcs, lands, token = _push_start(list(sends), lands, True, "push_grads_" + names[0])
        pushes.append((names, send_sems, recv_sems, srcs, lands))
        return token

    loss_part, grad_x, grads = _local_step(x[0], loss_target[0], p, ffn_weights, grads_out)

    small_names = _SMALL + ("conv_w", "loss")
    small_g = dict(grads, loss=loss_part[0, 0:1])
    flats = [small_g[n].reshape(-1) for n in small_names]
    small_sizes = [f.shape[0] for f in flats]
    g_small = _pack_rows(flats, _SMALL_ROWS_MULTIPLE)
    rs8 = g_small.shape[0] // N_DEV
    recv_small, = _all_to_all([g_small.reshape(N_DEV, rs8, LANES)], "all_to_all_small")
    small_mine = _sum_slots(recv_small, "sum_small")
    g_small_all = _allgather([small_mine], [F32], "allgather_small")[0][0].reshape(N_DEV * rs8, LANES)
    pieces = dict(zip(small_names, _unpack_rows(g_small_all, small_sizes)))
    loss = pieces["loss"][0]
    dconv_w = lax.dynamic_index_in_dim(pieces["conv_w"].reshape(N_DEV, 3, FF_CW), me, axis=0, keepdims=False)

    out = {}
    done = [g_small_all]
    for names, send_sems, recv_sems, srcs, lands in pushes:
        parts = _push_wait(send_sems, recv_sems, srcs, lands, True, done, "wait_grads_" + names[0])
        for n, part in zip(names, parts):
            res = _adam_shard(part, args[n], args["m_" + n], args["v_" + n], "adam_" + n)
            for kind, v in zip(("grad_", "delta_", "new_m_", "new_v_"), res):
                out[kind + n] = v
            done = [res[0]]
    names2 = _SMALL + ("conv_w",)
    gs = [pieces[n].reshape(_as_2d(args[n]).shape) for n in _SMALL] + [dconv_w]
    ds, m2s, v2s = _adam_small(gs, [_as_2d(args[n]) for n in names2], [_as_2d(args["m_" + n]) for n in names2],
                               [_as_2d(args["v_" + n]) for n in names2], "adam_small")
    for n, res in zip(names2, zip(gs, ds, m2s, v2s)):
        for kind, v in zip(("grad_", "delta_", "new_m_", "new_v_"), res):
            out[kind + n] = v.reshape(args[n].shape)
    order = ("g_mix", "w_in", "a_re", "a_im", "log_dt", "b_re", "b_im", "c_re", "c_im", "d_skip", "w_glu", "b_glu",
             "w_proj_a", "g_sgu", "w_s", "b_s", "w_proj_b", "w_out", "g_ffn", "w_up", "conv_w", "conv_b", "w_down",
             "g_final")
    res = [loss, grad_x.reshape(x.shape)]
    for kind in ("grad_", "delta_", "new_m_", "new_v_"):
        res += [jnp.swapaxes(out[kind + n], -1, -2) if n in _TRANSPOSED else out[kind + n] for n in order]
    return tuple(res)
```

```python
import functools
import math

import jax
import jax.numpy as jnp
from jax import lax
from jax.experimental import pallas as pl
from jax.experimental.pallas import tpu as pltpu

F32 = jnp.float32
MXU = jnp.bfloat16
EPS = 1e-6

D_MODEL = 1024
SSM_W = 512
SSM_G, SSM_H, SSM_P = 32, 16, 64
SSM_BLK = 4
SGU_W = 512
SGU_G, SGU_D, CHUNK = 8, 64, 128
D_FF = 2816
N_DEV = 8
FF_CW = 2 * D_FF // N_DEV
FF_NCB = D_FF // FF_CW
LANES = 128

ADAM_LR, ADAM_B1, ADAM_B2, ADAM_EPS, ADAM_WD, ADAM_STEP = 0.001, 0.9, 0.999, 1e-08, 0.01, 10

VMEM_LIMIT = 48 * 1024 * 1024


def _cp(*sem):
    return pltpu.CompilerParams(dimension_semantics=sem, vmem_limit_bytes=VMEM_LIMIT)


def _full(shape):
    n = len(shape)
    return pl.BlockSpec(shape, lambda *_: (0,) * n)


def _sds(shape, dtype=F32):
    return jax.ShapeDtypeStruct(shape, dtype)


def _dot(a, b):
    return jnp.dot(a, b, preferred_element_type=F32)


def _dot_nt(a, b):
    return lax.dot_general(a, b, (((1,), (1,)), ((), ())), preferred_element_type=F32)


def _dot_tn(a, b):
    return lax.dot_general(a, b, (((0,), (0,)), ((), ())), preferred_element_type=F32)


_GELU_C = math.sqrt(2.0 / math.pi)


def _gelu(x):
    return 0.5 * x * (1.0 + jnp.tanh(_GELU_C * (x + 0.044715 * (x * x * x))))


def _gelu_and_grad(x):
    t = jnp.tanh(_GELU_C * (x + 0.044715 * (x * x * x)))
    g = 0.5 * x * (1.0 + t)
    dg = 0.5 * (1.0 + t) + 0.5 * x * (1.0 - t * t) * (_GELU_C * (1.0 + 3.0 * 0.044715 * (x * x)))
    return g, dg


def _sigmoid(x):
    return 1.0 / (1.0 + jnp.exp(-x))


def _rms(x):
    return lax.rsqrt(jnp.mean(x * x, axis=-1, keepdims=True) + EPS)


def _rms_bwd(dxn, xn, r):
    return r * (dxn - xn * jnp.mean(dxn * xn, axis=-1, keepdims=True))


def _rowsum(x):
    return jnp.sum(x, axis=0, keepdims=True)


def _s5_disc(are, aim, ldt, br, bi):
    dt = jnp.exp(ldt)
    mag = jnp.exp(dt * are)
    abr = mag * jnp.cos(dt * aim)
    abi = mag * jnp.sin(dt * aim)
    den = are * are + aim * aim
    nr = abr - 1.0
    ni = abi
    fr = (nr * are + ni * aim) / den
    fi = (ni * are - nr * aim) / den
    return abr, abi, fr * br - fi * bi, fr * bi + fi * br


def _s5_params_fwd(are, aim, ldt, br, bi):
    def body(are_ref, aim_ref, ldt_ref, br_ref, bi_ref, o0, o1, o2, o3):
        outs = _s5_disc(are_ref[...], aim_ref[...], ldt_ref[...], br_ref[...], bi_ref[...])
        for o, v in zip((o0, o1, o2, o3), outs):
            o[...] = v
    shp = are.shape
    return pl.pallas_call(body, name="s5_params_fwd", out_shape=[_sds(shp)] * 4)(are, aim, ldt, br, bi)


def _s5_params_bwd(are, aim, ldt, br, bi, dabr, dabi, dbr, dbi):
    def body(are_ref, aim_ref, ldt_ref, br_ref, bi_ref, c0, c1, c2, c3, o0, o1, o2, o3, o4):
        prim = (are_ref[...], aim_ref[...], ldt_ref[...], br_ref[...], bi_ref[...])
        _, vjp = jax.vjp(_s5_disc, *prim)
        outs = vjp((c0[...], c1[...], c2[...], c3[...]))
        for o, v in zip((o0, o1, o2, o3, o4), outs):
            o[...] = v
    shp = are.shape
    return pl.pallas_call(body, name="s5_params_bwd", out_shape=[_sds(shp)] * 5)(
        are, aim, ldt, br, bi, dabr, dabi, dbr, dbi)


def _blockdiag(m_t):
    m = m_t.reshape(SSM_BLK, 8, SSM_H, 1, SSM_P)
    eye = jnp.eye(8, dtype=bool).reshape(1, 8, 1, 8, 1)
    return jnp.where(eye, m, jnp.zeros((), m_t.dtype)).reshape(SSM_BLK, 8 * SSM_H, 8 * SSM_P)


def _unblockdiag(pc):
    m = pc.reshape(SSM_BLK, 8, SSM_H, 8, SSM_P)
    return jnp.einsum("jghgp->jghp", m).reshape(SSM_G * SSM_H, SSM_P)


def _in_fwd(x, g_mix, w_in_t, tm):
    S = x.shape[0]

    def body(x_ref, g_ref, w_ref, h_ref, us_ref, uv_ref, gl_ref):
        xv = x_ref[...]
        h = (xv * _rms(xv) * g_ref[...]).astype(MXU)
        h_ref[...] = h
        us_ref[...] = _dot_nt(h, w_ref[0:SSM_W, :])
        uv_ref[...] = _dot_nt(h, w_ref[SSM_W:SSM_W + 2 * SGU_W, :])
        gl_ref[...] = _dot_nt(h, w_ref[SSM_W + 2 * SGU_W:, :])

    row = lambda n: pl.BlockSpec((tm, n), lambda i: (i, 0))
    return pl.pallas_call(
        body, name="in_fwd", grid=(S // tm,),
        in_specs=[row(D_MODEL), _full((1, D_MODEL)), _full(w_in_t.shape)],
        out_specs=[row(D_MODEL), row(SSM_W), row(2 * SGU_W), row(2 * D_MODEL)],
        out_shape=[_sds((S, D_MODEL), MXU), _sds((S, SSM_W)), _sds((S, 2 * SGU_W)), _sds((S, 2 * D_MODEL))],
        compiler_params=_cp("parallel"),
    )(x, g_mix, w_in_t)


def _scan_tables(ar, ai, reverse):
    n = ar.shape[-1]
    def mul(p, q):
        return p[0] * q[0] - p[1] * q[1], p[0] * q[1] + p[1] * q[0]
    a1 = (ar, ai)
    a2 = mul(a1, a1)
    a3 = mul(a2, a1)
    a4 = mul(a2, a2)
    a5 = mul(a4, a1)
    a6 = mul(a4, a2)
    a7 = mul(a4, a3)
    a8 = mul(a4, a4)
    pw = (a1, a2, a3, a4, a5, a6, a7, a8)
    rows = lax.broadcasted_iota(jnp.int32, (8, n), 0)
    tabs = []
    for s, a in ((1, a1), (2, a2), (4, a4)):
        keep = (rows + s <= 7) if reverse else (rows >= s)
        for comp in a:
            tabs.append(jnp.where(keep, jnp.broadcast_to(comp, (8, n)), 0.0))
    for c in range(2):
        q = jnp.zeros((8, n), F32)
        for r in range(8):
            e = (8 - r) if reverse else (r + 1)
            q = jnp.where(rows == r, jnp.broadcast_to(pw[e - 1][c], (8, n)), q)
        tabs.append(q)
    return tabs


def _scan_group(xr, xi, tab_ref, cr, ci, reverse):
    for t, s in enumerate((1, 2, 4)):
        pr = tab_ref[2 * t]
        pi = tab_ref[2 * t + 1]
        sh = (8 - s) if reverse else s
        sr = pltpu.roll(xr, sh, 0)
        si = pltpu.roll(xi, sh, 0)
        xr, xi = xr + pr * sr - pi * si, xi + pr * si + pi * sr
    qr = tab_ref[6]
    qi = tab_ref[7]
    return xr + qr * cr - qi * ci, xi + qr * ci + qi * cr


def _runs_load(src_ref, dst_ref, run):
    for i in range(run):
        dst_ref[8 * i:8 * i + 8, :] = src_ref[pl.ds(i, 8, stride=run), :]


def _runs_store(val, dst_ref, run):
    for i in range(run):
        dst_ref[pl.ds(i, 8, stride=run), :] = val[8 * i:8 * i + 8, :]


def _cpow2(ar, ai, log2n):
    for _ in range(log2n):
        ar, ai = ar * ar - ai * ai, 2.0 * ar * ai
    return ar, ai


def _s5_fwd(us, abar_re, abar_im, b_re, b_im, c_re, c_im, d_skip, tm):
    S = us.shape[0]
    nt = S // tm
    w = 8 * SSM_P
    run = tm // 8
    assert run & (run - 1) == 0

    def body(us_ref, ar_ref, ai_ref, br_ref, bi_ref, cr_ref, ci_ref, d_ref, str_ref, sti_ref, ys_ref,
             tab_ref, car_ref, up_ref):
        i = pl.program_id(1)

        @pl.when(i == 0)
        def _():
            car_ref[...] = jnp.zeros_like(car_ref)
            for k, t in enumerate(_scan_tables(*_cpow2(ar_ref[...], ai_ref[...], run.bit_length() - 1), False)):
                tab_ref[k] = t

        _runs_load(us_ref, up_ref, run)
        ub = up_ref[...].astype(MXU)
        str_ref[...] = _dot(ub, br_ref[0])
        sti_ref[...] = _dot(ub, bi_ref[0])
        ar = jnp.broadcast_to(ar_ref[...], (8, w))
        ai = jnp.broadcast_to(ai_ref[...], (8, w))

        def advance(k, state):
            r0 = pl.multiple_of(k * 8, 8)
            sr, si = state
            return (ar * sr - ai * si + str_ref[pl.ds(r0, 8), :], ar * si + ai * sr + sti_ref[pl.ds(r0, 8), :])

        def emit(k, state):
            r0 = pl.multiple_of(k * 8, 8)
            sr, si = advance(k, state)
            str_ref[pl.ds(r0, 8), :] = sr
            sti_ref[pl.ds(r0, 8), :] = si
            return sr, si

        zero = jnp.zeros((8, w), F32)
        er, ei = lax.fori_loop(0, run, advance, (zero, zero))
        cr, ci = car_ref[0:1, :], car_ref[1:2, :]
        tr, ti = _scan_group(er, ei, tab_ref, cr, ci, False)
        r8 = lax.broadcasted_iota(jnp.int32, (8, w), 0)
        start = (jnp.where(r8 == 0, cr, pltpu.roll(tr, 1, 0)), jnp.where(r8 == 0, ci, pltpu.roll(ti, 1, 0)))
        car_ref[0:1, :] = tr[7:8, :]
        car_ref[1:2, :] = ti[7:8, :]
        lax.fori_loop(0, run, emit, start)
        y = _dot_nt(str_ref[...].astype(MXU), cr_ref[0]) - _dot_nt(sti_ref[...].astype(MXU), ci_ref[0])
        _runs_store(y, ys_ref, run)
        ys_ref[...] += d_ref[...] * us_ref[...]

    blk = lambda: pl.BlockSpec((1, 8 * SSM_H, w), lambda j, i: (j, 0, 0))
    return pl.pallas_call(
        body, name="s5_fwd", grid=(SSM_BLK, nt),
        in_specs=[pl.BlockSpec((tm, LANES), lambda j, i: (i, j)),
                  pl.BlockSpec((1, w), lambda j, i: (0, j)), pl.BlockSpec((1, w), lambda j, i: (0, j)),
                  blk(), blk(), blk(), blk(),
                  pl.BlockSpec((1, LANES), lambda j, i: (0, j))],
        out_specs=[pl.BlockSpec((tm, w), lambda j, i: (i, j)), pl.BlockSpec((tm, w), lambda j, i: (i, j)),
                   pl.BlockSpec((tm, LANES), lambda j, i: (i, j))],
        out_shape=[_sds((S, SSM_BLK * w)), _sds((S, SSM_BLK * w)), _sds((S, SSM_W))],
        scratch_shapes=[pltpu.VMEM((8, 8, w), F32), pltpu.VMEM((8, w), F32), pltpu.VMEM((tm, LANES), F32)],
        compiler_params=_cp("parallel", "arbitrary"),
    )(us, abar_re, abar_im, b_re, b_im, c_re, c_im, d_skip)


def _sgu_mix(vnb, ws_ref, grp):
    acc = jnp.zeros(vnb.shape, F32)
    for g in range(SGU_G):
        acc = jnp.where(grp == g, _dot(ws_ref[g], vnb), acc)
    return acc


def _mix_fwd(x, ys, uv, gl, w_glu, b_glu, w_pa, g_sgu, ws, bias_s, w_pb, w_out, g_ffn, tm):
    S = x.shape[0]

    def body(x_ref, ys_ref, uv_ref, gl_ref, wglu_ref, bglu_ref, wpa_ref, gs_ref, ws_ref, bias_ref, wpb_ref, wout_ref,
             gf_ref, yg_ref, yap_ref, sg_ref, ya_ref, yb_ref, m_ref, x1_ref, h2_ref):
        yg = _gelu(ys_ref[...])
        ygb = yg.astype(MXU)
        yg_ref[...] = ygb
        z = _dot(ygb, wglu_ref[...]) + bglu_ref[...]
        yapb = (yg * _sigmoid(z)).astype(MXU)
        yap_ref[...] = yapb
        ya = _dot(yapb, wpa_ref[...])
        ya_ref[...] = ya

        uvg = _gelu(uv_ref[...])
        u2 = uvg[:, :SGU_W]
        v2 = uvg[:, SGU_W:]
        vnb = (v2 * _rms(v2) * gs_ref[...]).astype(MXU)
        grp = lax.broadcasted_iota(jnp.int32, (CHUNK, SGU_W), 1) // SGU_D
        for c in range(tm // CHUNK):
            rs = slice(c * CHUNK, (c + 1) * CHUNK)
            mixed = _sgu_mix(vnb[rs], ws_ref, grp) + bias_ref[...]
            sg_ref[rs, :] = (u2[rs] * mixed).astype(MXU)
        yb = _dot(sg_ref[...], wpb_ref[...])
        yb_ref[...] = yb

        glv = gl_ref[...]
        m = _sigmoid(glv[:, :D_MODEL]) * ya + _sigmoid(glv[:, D_MODEL:]) * yb
        mb = m.astype(MXU)
        m_ref[...] = mb
        x1 = x_ref[...] + _dot(mb, wout_ref[...])
        x1_ref[...] = x1
        h2_ref[...] = (x1 * _rms(x1) * gf_ref[...]).astype(MXU)

    row = lambda n: pl.BlockSpec((tm, n), lambda i: (i, 0))
    return pl.pallas_call(
        body, name="mix_fwd", grid=(S // tm,),
        in_specs=[row(D_MODEL), row(SSM_W), row(2 * SGU_W), row(2 * D_MODEL),
                  _full(w_glu.shape), _full(b_glu.shape), _full(w_pa.shape), _full(g_sgu.shape), _full(ws.shape),
                  _full(bias_s.shape), _full(w_pb.shape), _full(w_out.shape), _full(g_ffn.shape)],
        out_specs=[row(SSM_W), row(SSM_W), row(SGU_W), row(D_MODEL), row(D_MODEL), row(D_MODEL), row(D_MODEL),
                   row(D_MODEL)],
        out_shape=[_sds((S, SSM_W), MXU), _sds((S, SSM_W), MXU), _sds((S, SGU_W), MXU), _sds((S, D_MODEL)),
                   _sds((S, D_MODEL)), _sds((S, D_MODEL), MXU), _sds((S, D_MODEL)), _sds((S, D_MODEL), MXU)],
        compiler_params=_cp("parallel"),
    )(x, ys, uv, gl, w_glu, b_glu, w_pa, g_sgu, ws, bias_s, w_pb, w_out, g_ffn)


def _causal_conv3(u, prev8, cw, cb):
    tm = u.shape[0]
    w0, w1, w2 = cw[0:1], cw[1:2], cw[2:3]
    body = w0 * pltpu.roll(u, 2, 0) + w1 * pltpu.roll(u, 1, 0) + w2 * u + cb
    u8 = u[0:8, :]
    r8 = lax.broadcasted_iota(jnp.int32, u8.shape, 0)
    t1 = prev8[7:8, :]
    t0 = prev8[6:7, :]
    s1 = jnp.where(r8 == 0, t1, pltpu.roll(u8, 1, 0))
    s2 = jnp.where(r8 == 0, t0, jnp.where(r8 == 1, t1, pltpu.roll(u8, 2, 0)))
    first = w0 * s2 + w1 * s1 + w2 * u8 + cb
    return jnp.concatenate([first, body[8:tm, :]], axis=0)


def _causal_conv3_adjoint(d, next8, cw):
    tm = d.shape[0]
    w0, w1, w2 = cw[0:1], cw[1:2], cw[2:3]
    n1 = pltpu.roll(d, tm - 1, 0)
    n2 = pltpu.roll(d, tm - 2, 0)
    body = w2 * d + w1 * n1 + w0 * n2
    d8 = d[tm - 8:tm, :]
    r8 = lax.broadcasted_iota(jnp.int32, d8.shape, 0)
    h0 = next8[0:1, :]
    h1 = next8[1:2, :]
    m1 = jnp.where(r8 == 7, h0, pltpu.roll(d8, 7, 0))
    m2 = jnp.where(r8 == 6, h0, jnp.where(r8 == 7, h1, pltpu.roll(d8, 6, 0)))
    last = w2 * d8 + w1 * m1 + w0 * m2
    out = jnp.concatenate([body[0:tm - 8, :], last], axis=0)
    return out, n1, n2, h0 - d[0:1, :], h1 - d[1:2, :]


def _ffn_fwd(h2, x1, tgt, w_up, conv_w, conv_b, w_down, g_final, tm):
    S = h2.shape[0]
    nt = S // tm
    ncb = FF_NCB

    def body(h2_ref, wa_ref, wb_ref, cwa_ref, cwb_ref, cba_ref, cbb_ref, wd_ref, x1_ref, gf_ref, tgt_ref,
             up_ref, ab_ref, ff_ref, dx2_ref, dx2b_ref, loss_ref, dgf_ref, acc_ref, tail_ref):
        i = pl.program_id(0)
        cb = pl.program_id(1)

        @pl.when(i == 0)
        def _():
            tail_ref[cb] = jnp.zeros((2, 8, FF_CW), F32)

        @pl.when(jnp.logical_and(i == 0, cb == 0))
        def _():
            loss_ref[...] = jnp.zeros_like(loss_ref)
            dgf_ref[...] = jnp.zeros_like(dgf_ref)

        h2v = h2_ref[...]
        ua = _dot_nt(h2v, wa_ref[0])
        ub = _dot_nt(h2v, wb_ref[0])
        up_ref[0, 0] = ua.astype(MXU)
        up_ref[1, 0] = ub.astype(MXU)
        a = _causal_conv3(ua, tail_ref[cb, 0], cwa_ref[0], cba_ref[0])
        b = _causal_conv3(ub, tail_ref[cb, 1], cwb_ref[0], cbb_ref[0])
        tail_ref[cb, 0] = ua[tm - 8:tm, :]
        tail_ref[cb, 1] = ub[tm - 8:tm, :]
        ab_ref[0, 0] = a
        ab_ref[1, 0] = b
        ffb = (a * _sigmoid(a) * b).astype(MXU)
        ff_ref[0] = ffb
        contrib = _dot(ffb, wd_ref[...])

        @pl.when(cb == 0)
        def _():
            acc_ref[...] = contrib

        @pl.when(cb > 0)
        def _():
            acc_ref[...] += contrib

        @pl.when(cb == ncb - 1)
        def _():
            x2 = x1_ref[...] + acc_ref[...]
            r = _rms(x2)
            xn = x2 * r
            g = gf_ref[...]
            diff = xn * g - tgt_ref[...]
            loss_ref[...] += (0.5 / D_MODEL) * jnp.sum(diff * diff)
            dy = diff * (1.0 / D_MODEL)
            dgf_ref[...] += _rowsum(dy * xn)
            dx2 = _rms_bwd(dy * g, xn, r)
            dx2_ref[...] = dx2
            dx2b_ref[...] = dx2.astype(MXU)

    row = lambda n: pl.BlockSpec((tm, n), lambda i, c: (i, 0))
    gate = lambda r: pl.BlockSpec((1, r, FF_CW), lambda i, c: (c, 0, 0))
    lin = lambda r: pl.BlockSpec((1, r, FF_CW), lambda i, c: (ncb + c, 0, 0))
    return pl.pallas_call(
        body, name="ffn_fwd", grid=(nt, ncb),
        in_specs=[row(D_MODEL),
                  pl.BlockSpec((1, FF_CW, D_MODEL), lambda i, c: (c, 0, 0)),
                  pl.BlockSpec((1, FF_CW, D_MODEL), lambda i, c: (ncb + c, 0, 0)),
                  gate(3), lin(3), gate(1), lin(1),
                  pl.BlockSpec((FF_CW, D_MODEL), lambda i, c: (c, 0)),
                  row(D_MODEL), _full((1, D_MODEL)), row(D_MODEL)],
        out_specs=[pl.BlockSpec((2, 1, tm, FF_CW), lambda i, c: (0, c, i, 0)),
                   pl.BlockSpec((2, 1, tm, FF_CW), lambda i, c: (0, c, i, 0)),
                   pl.BlockSpec((1, tm, FF_CW), lambda i, c: (c, i, 0)),
                   row(D_MODEL), row(D_MODEL), _full((1, LANES)), _full((1, D_MODEL))],
        out_shape=[_sds((2, ncb, S, FF_CW), MXU), _sds((2, ncb, S, FF_CW)), _sds((ncb, S, FF_CW), MXU),
                   _sds((S, D_MODEL)), _sds((S, D_MODEL), MXU), _sds((1, LANES)), _sds((1, D_MODEL))],
        scratch_shapes=[pltpu.VMEM((tm, D_MODEL), F32), pltpu.VMEM((ncb, 2, 8, FF_CW), F32)],
        compiler_params=_cp("arbitrary", "arbitrary"),
    )(h2, w_up, w_up, conv_w, conv_w, conv_b, conv_b, w_down, x1, g_final, tgt)


def _ffn_bwd(dx2, up, ab, x1, w_up, conv_w, w_down, g_ffn, tm):
    S = dx2.shape[0]
    nt = S // tm
    ncb = FF_NCB

    def body(dx2_ref, up_ref, ab_ref, cwa_ref, cwb_ref, wd_ref, wa_ref, wb_ref,
             x1_ref, g_ref, dup_ref, dx1_ref, dx1b_ref, dconv_ref, dg_ref, acc_ref, head_ref):
        i = pl.program_id(0)
        cb = pl.program_id(1)
        ri = nt - 1 - i

        @pl.when(i == 0)
        def _():
            head_ref[cb] = jnp.zeros((2, 8, FF_CW), F32)
            dconv_ref[cb] = jnp.zeros((8, FF_CW), F32)
            dconv_ref[ncb + cb] = jnp.zeros((8, FF_CW), F32)

        @pl.when(jnp.logical_and(i == 0, cb == 0))
        def _():
            dg_ref[...] = jnp.zeros_like(dg_ref)

        dx2v = dx2_ref[...]
        dff = _dot_nt(dx2v.astype(MXU), wd_ref[...])
        a = ab_ref[0, 0]
        b = ab_ref[1, 0]
        sa = _sigmoid(a)
        silu = a * sa
        da = (dff * b) * (sa + silu * (1.0 - sa))
        db = dff * silu

        dps = []
        for half, slot, d, cw_ref in ((0, cb, da, cwa_ref), (1, ncb + cb, db, cwb_ref)):
            dp, n1, n2, fix0, fix1 = _causal_conv3_adjoint(d, head_ref[cb, half], cw_ref[0])
            head_ref[cb, half] = d[0:8, :]
            dpb16 = dp.astype(MXU)
            dup_ref[half, 0] = dpb16
            dps.append(dpb16)
            u = up_ref[half, 0].astype(F32)
            u_last = u[tm - 1:tm, :]
            dconv_ref[slot, 0:1, :] += _rowsum(n2 * u) + fix0 * u[tm - 2:tm - 1, :] + fix1 * u_last
            dconv_ref[slot, 1:2, :] += _rowsum(n1 * u) + fix0 * u_last
            dconv_ref[slot, 2:3, :] += _rowsum(d * u)
            dconv_ref[slot, 3:4, :] += _rowsum(d)
        contrib = _dot(dps[0], wa_ref[0]) + _dot(dps[1], wb_ref[0])

        @pl.when(cb == 0)
        def _():
            acc_ref[...] = contrib

        @pl.when(cb > 0)
        def _():
            acc_ref[...] += contrib

        @pl.when(cb == ncb - 1)
        def _():
            x1v = x1_ref[...]
            r = _rms(x1v)
            xn = x1v * r
            dh2 = acc_ref[...]
            dg_ref[...] += _rowsum(dh2 * xn)
            dx1 = dx2v + _rms_bwd(dh2 * g_ref[...], xn, r)
            dx1_ref[...] = dx1
            dx1b_ref[...] = dx1.astype(MXU)

    row = lambda n: pl.BlockSpec((tm, n), lambda i, c: (nt - 1 - i, 0))
    colb = lambda: pl.BlockSpec((2, 1, tm, FF_CW), lambda i, c: (0, c, nt - 1 - i, 0))
    gate = lambda r: pl.BlockSpec((1, r, FF_CW), lambda i, c: (c, 0, 0))
    lin = lambda r: pl.BlockSpec((1, r, FF_CW), lambda i, c: (ncb + c, 0, 0))
    return pl.pallas_call(
        body, name="ffn_bwd", grid=(nt, ncb),
        in_specs=[row(D_MODEL), colb(), colb(), gate(3), lin(3),
                  pl.BlockSpec((FF_CW, D_MODEL), lambda i, c: (c, 0)),
                  pl.BlockSpec((1, FF_CW, D_MODEL), lambda i, c: (c, 0, 0)),
                  pl.BlockSpec((1, FF_CW, D_MODEL), lambda i, c: (ncb + c, 0, 0)),
                  row(D_MODEL), _full((1, D_MODEL))],
        out_specs=[colb(), row(D_MODEL), row(D_MODEL), _full((2 * ncb, 8, FF_CW)), _full((1, D_MODEL))],
        out_shape=[_sds((2, ncb, S, FF_CW), MXU), _sds((S, D_MODEL)), _sds((S, D_MODEL), MXU), _sds((2 * ncb, 8, FF_CW)),
                   _sds((1, D_MODEL))],
        scratch_shapes=[pltpu.VMEM((tm, D_MODEL), F32), pltpu.VMEM((ncb, 2, 8, FF_CW), F32)],
        compiler_params=_cp("arbitrary", "arbitrary"),
    )(dx2, up, ab, conv_w, conv_w, w_down, w_up, w_up, x1, g_ffn)


def _mix_bwd(dx1, gl, ya, yb, ys, uv, w_out, w_pa, w_pb, w_glu, b_glu, g_sgu, ws, ws_t, bias_s, tm):
    S = dx1.shape[0]

    def body(dx1_ref, gl_ref, ya_ref, yb_ref, ys_ref, uv_ref, wout_ref, wpa_ref, wpb_ref, wglu_ref, bglu_ref, gs_ref,
             ws_ref, wst_ref, bias_ref,
             dgl_ref, dya_ref, dyb_ref, dz_ref, dys_ref, duv_ref, dbglu_ref, dgs_ref, dws_ref, dbs_ref,
             du2_ref, dvn_ref):
        i = pl.program_id(0)

        @pl.when(i == 0)
        def _():
            dbglu_ref[...] = jnp.zeros_like(dbglu_ref)
            dgs_ref[...] = jnp.zeros_like(dgs_ref)
            dws_ref[...] = jnp.zeros_like(dws_ref)
            dbs_ref[...] = jnp.zeros_like(dbs_ref)

        dm = _dot_nt(dx1_ref[...].astype(MXU), wout_ref[...])
        glv = gl_ref[...]
        ga = _sigmoid(glv[:, :D_MODEL])
        gb = _sigmoid(glv[:, D_MODEL:])
        dgl_ref[:, :D_MODEL] = (dm * ya_ref[...] * ga * (1.0 - ga)).astype(MXU)
        dgl_ref[:, D_MODEL:] = (dm * yb_ref[...] * gb * (1.0 - gb)).astype(MXU)
        dyab = (dm * ga).astype(MXU)
        dybb = (dm * gb).astype(MXU)
        dya_ref[...] = dyab
        dyb_ref[...] = dybb

        dyap = _dot_nt(dyab, wpa_ref[...])
        yg, dgelu = _gelu_and_grad(ys_ref[...])
        sz = _sigmoid(_dot(yg.astype(MXU), wglu_ref[...]) + bglu_ref[...])
        dz = dyap * yg * sz * (1.0 - sz)
        dzb = dz.astype(MXU)
        dz_ref[...] = dzb
        dbglu_ref[...] += _rowsum(dz)
        dys_ref[...] = (dyap * sz + _dot_nt(dzb, wglu_ref[...])) * dgelu

        dsg = _dot_nt(dybb, wpb_ref[...])
        uvg, duvg = _gelu_and_grad(uv_ref[...])
        u2 = uvg[:, :SGU_W]
        v2 = uvg[:, SGU_W:]
        rv = _rms(v2)
        vhat = v2 * rv
        gs = gs_ref[...]
        vnb = (vhat * gs).astype(MXU)
        grp = lax.broadcasted_iota(jnp.int32, (CHUNK, SGU_W), 1) // SGU_D
        tril = (lax.broadcasted_iota(jnp.int32, (CHUNK, CHUNK), 0)
                >= lax.broadcasted_iota(jnp.int32, (CHUNK, CHUNK), 1))
        for c in range(tm // CHUNK):
            rs = slice(c * CHUNK, (c + 1) * CHUNK)
            vc = vnb[rs]
            mixed = _sgu_mix(vc, ws_ref, grp) + bias_ref[...]
            dsg_c = dsg[rs]
            du2_ref[rs, :] = dsg_c * mixed
            dmx = dsg_c * u2[rs]
            dbs_ref[...] += dmx
            dmb = dmx.astype(MXU)
            dvn_ref[rs, :] = _sgu_mix(dmb, wst_ref, grp)
            for g in range(SGU_G):
                part = _dot_nt(jnp.where(grp == g, dmb, jnp.zeros((), MXU)), vc)
                dws_ref[g] += jnp.where(tril, part, 0.0)
        dvn = dvn_ref[...]
        dgs_ref[...] += _rowsum(dvn * vhat)
        dv2 = _rms_bwd(dvn * gs, vhat, rv)
        duv_ref[:, :SGU_W] = (du2_ref[...] * duvg[:, :SGU_W]).astype(MXU)
        duv_ref[:, SGU_W:] = (dv2 * duvg[:, SGU_W:]).astype(MXU)

    row = lambda n: pl.BlockSpec((tm, n), lambda i: (i, 0))
    return pl.pallas_call(
        body, name="mix_bwd", grid=(S // tm,),
        in_specs=[row(D_MODEL), row(2 * D_MODEL), row(D_MODEL), row(D_MODEL), row(SSM_W), row(2 * SGU_W),
                  _full(w_out.shape), _full(w_pa.shape), _full(w_pb.shape), _full(w_glu.shape), _full(b_glu.shape),
                  _full(g_sgu.shape), _full(ws.shape), _full(ws_t.shape), _full(bias_s.shape)],
        out_specs=[row(2 * D_MODEL), row(D_MODEL), row(D_MODEL), row(SSM_W), row(SSM_W), row(2 * SGU_W),
                   _full((1, SSM_W)), _full((1, SGU_W)), _full((SGU_G, CHUNK, CHUNK)), _full((CHUNK, SGU_W))],
        out_shape=[_sds((S, 2 * D_MODEL), MXU), _sds((S, D_MODEL), MXU), _sds((S, D_MODEL), MXU), _sds((S, SSM_W), MXU),
                   _sds((S, SSM_W)), _sds((S, 2 * SGU_W), MXU),
                   _sds((1, SSM_W)), _sds((1, SGU_W)), _sds((SGU_G, CHUNK, CHUNK)), _sds((CHUNK, SGU_W))],
        scratch_shapes=[pltpu.VMEM((tm, SGU_W), F32), pltpu.VMEM((tm, SGU_W), F32)],
        compiler_params=_cp("arbitrary"),
    )(dx1, gl, ya, yb, ys, uv, w_out, w_pa, w_pb, w_glu, b_glu, g_sgu, ws, ws_t, bias_s)


def _s5_bwd(dys, us, st_re, st_im, abar_re, abar_im, b_re, b_im, c_re, c_im, d_skip, tm):
    S = us.shape[0]
    nt = S // tm
    w = 8 * SSM_P
    hb = tm // 8
    run = tm // 8
    assert run & (run - 1) == 0

    def body(dys_ref, us_ref, str_ref, sti_ref, hr_ref, hi_ref, ar_ref, ai_ref, br_ref, bi_ref, cr_ref, ci_ref, d_ref,
             dus_ref, dab_ref, dd_ref, dbr_ref, dbi_ref, dcr_ref, dci_ref,
             tab_ref, car_ref, gr_ref, gi_ref, dyp_ref, up_ref, dun_ref):
        i = pl.program_id(1)
        ri = nt - 1 - i

        @pl.when(i == 0)
        def _():
            car_ref[...] = jnp.zeros_like(car_ref)
            for k, t in enumerate(_scan_tables(*_cpow2(ar_ref[...], -ai_ref[...], run.bit_length() - 1), True)):
                tab_ref[k] = t
            for r in (dab_ref, dd_ref, dbr_ref, dbi_ref, dcr_ref, dci_ref):
                r[...] = jnp.zeros_like(r)

        _runs_load(dys_ref, dyp_ref, run)
        _runs_load(us_ref, up_ref, run)
        dyb = dyp_ref[...].astype(MXU)
        gr_ref[...] = _dot(dyb, cr_ref[0])
        gi_ref[...] = -_dot(dyb, ci_ref[0])
        ar = jnp.broadcast_to(ar_ref[...], (8, w))
        ai = jnp.broadcast_to(-ai_ref[...], (8, w))

        def advance(kk, state):
            r0 = pl.multiple_of((run - 1 - kk) * 8, 8)
            gr, gi = state
            return (ar * gr - ai * gi + gr_ref[pl.ds(r0, 8), :], ar * gi + ai * gr + gi_ref[pl.ds(r0, 8), :])

        def emit(kk, state):
            r0 = pl.multiple_of((run - 1 - kk) * 8, 8)
            gr, gi = advance(kk, state)
            gr_ref[pl.ds(r0, 8), :] = gr
            gi_ref[pl.ds(r0, 8), :] = gi
            return gr, gi

        zero = jnp.zeros((8, w), F32)
        er, ei = lax.fori_loop(0, run, advance, (zero, zero))
        cr, ci = car_ref[0:1, :], car_ref[1:2, :]
        tr, ti = _scan_group(er, ei, tab_ref, cr, ci, True)
        r8 = lax.broadcasted_iota(jnp.int32, (8, w), 0)
        start = (jnp.where(r8 == 7, cr, pltpu.roll(tr, 7, 0)), jnp.where(r8 == 7, ci, pltpu.roll(ti, 7, 0)))
        car_ref[0:1, :] = tr[0:1, :]
        car_ref[1:2, :] = ti[0:1, :]
        lax.fori_loop(0, run, emit, start)

        gsr = gr_ref[...]
        gsi = gi_ref[...]
        sr = str_ref[...]
        si = sti_ref[...]
        first = ri == 0

        def previous(s, halo_ref):
            head = jnp.where(r8 == 0, jnp.where(first, 0.0, halo_ref[7:8, :]), pltpu.roll(s[tm - 8:tm, :], 1, 0))
            return jnp.concatenate([head, s[0:tm - 8, :]], axis=0)

        spr = previous(sr, hr_ref)
        spi = previous(si, hi_ref)
        dab_ref[0, 0:1, :] += _rowsum(gsr * spr + gsi * spi)
        dab_ref[0, 1:2, :] += _rowsum(gsi * spr - gsr * spi)

        gbr = gsr.astype(MXU)
        gbi = gsi.astype(MXU)
        _runs_store(_dot_nt(gbr, br_ref[0]) + _dot_nt(gbi, bi_ref[0]), dun_ref, run)
        dys_v = dys_ref[...]
        dus_ref[...] = (dun_ref[...] + d_ref[...] * dys_v).astype(MXU)
        dd_ref[0, 0:1, :] += _rowsum(dys_v * us_ref[...])
        ub = up_ref[...].astype(MXU)
        dbr_ref[0] += _dot_tn(ub, gbr)
        dbi_ref[0] += _dot_tn(ub, gbi)
        dcr_ref[0] += _dot_tn(dyb, sr.astype(MXU))
        dci_ref[0] -= _dot_tn(dyb, si.astype(MXU))

    blk = lambda: pl.BlockSpec((1, 8 * SSM_H, w), lambda j, i: (j, 0, 0))
    rowl = lambda: pl.BlockSpec((tm, LANES), lambda j, i: (nt - 1 - i, j))
    roww = lambda: pl.BlockSpec((tm, w), lambda j, i: (nt - 1 - i, j))
    halo = lambda: pl.BlockSpec((8, w), lambda j, i: (jnp.maximum((nt - 1 - i) * hb - 1, 0), j))
    return pl.pallas_call(
        body, name="s5_bwd", grid=(SSM_BLK, nt),
        in_specs=[rowl(), rowl(), roww(), roww(), halo(), halo(),
                  pl.BlockSpec((1, w), lambda j, i: (0, j)), pl.BlockSpec((1, w), lambda j, i: (0, j)),
                  blk(), blk(), blk(), blk(),
                  pl.BlockSpec((1, LANES), lambda j, i: (0, j))],
        out_specs=[rowl(),
                   pl.BlockSpec((1, 8, w), lambda j, i: (j, 0, 0)), pl.BlockSpec((1, 8, LANES), lambda j, i: (j, 0, 0)),
                   blk(), blk(), blk(), blk()],
        out_shape=[_sds((S, SSM_W), MXU), _sds((SSM_BLK, 8, w)), _sds((SSM_BLK, 8, LANES)),
                   _sds((SSM_BLK, 8 * SSM_H, w)), _sds((SSM_BLK, 8 * SSM_H, w)),
                   _sds((SSM_BLK, 8 * SSM_H, w)), _sds((SSM_BLK, 8 * SSM_H, w))],
        scratch_shapes=[pltpu.VMEM((8, 8, w), F32), pltpu.VMEM((8, w), F32),
                        pltpu.VMEM((tm, w), F32), pltpu.VMEM((tm, w), F32),
                        pltpu.VMEM((tm, LANES), F32), pltpu.VMEM((tm, LANES), F32), pltpu.VMEM((tm, LANES), F32)],
        compiler_params=_cp("parallel", "arbitrary"),
    )(dys, us, st_re, st_im, st_re, st_im, abar_re, abar_im, b_re, b_im, c_re, c_im, d_skip)


def _in_bwd(dus, duv, dgl, dx1, x, g_mix, w_in, tm):
    S = x.shape[0]

    def body(dus_ref, duv_ref, dgl_ref, dx1_ref, x_ref, g_ref, w_ref, gx_ref, dg_ref):
        @pl.when(pl.program_id(0) == 0)
        def _():
            dg_ref[...] = jnp.zeros_like(dg_ref)

        dh = (_dot(dus_ref[...], w_ref[0:SSM_W, :])
              + _dot(duv_ref[...], w_ref[SSM_W:SSM_W + 2 * SGU_W, :])
              + _dot(dgl_ref[...], w_ref[SSM_W + 2 * SGU_W:, :]))
        xv = x_ref[...]
        r = _rms(xv)
        xn = xv * r
        dg_ref[...] += _rowsum(dh * xn)
        gx_ref[...] = dx1_ref[...] + _rms_bwd(dh * g_ref[...], xn, r)

    row = lambda n: pl.BlockSpec((tm, n), lambda i: (i, 0))
    return pl.pallas_call(
        body, name="in_bwd", grid=(S // tm,),
        in_specs=[row(SSM_W), row(2 * SGU_W), row(2 * D_MODEL), row(D_MODEL), row(D_MODEL), _full((1, D_MODEL)),
                  _full(w_in.shape)],
        out_specs=[row(D_MODEL), _full((1, D_MODEL))],
        out_shape=[_sds((S, D_MODEL)), _sds((1, D_MODEL))],
        compiler_params=_cp("arbitrary"),
    )(dus, duv, dgl, dx1, x, g_mix, w_in)


def _pick(n, cands):
    for c in cands:
        if n % c == 0:
            return c
    return n


def _wgrad_split(a, b, nsplit, tk, name):
    S, K = a.shape
    N = b.shape[1]
    c = N // nsplit

    def body(a_ref, b_ref, o_ref):
        prod = _dot_tn(a_ref[...], b_ref[...])
        for d in range(nsplit):
            o_ref[d] = prod[:, c * d:c * (d + 1)].astype(MXU)

    return pl.pallas_call(
        body, name=name, grid=(K // tk,),
        in_specs=[pl.BlockSpec((S, tk), lambda k: (0, k)), _full((S, N))],
        out_specs=pl.BlockSpec((nsplit, tk, c), lambda k: (0, k, 0)),
        out_shape=_sds((nsplit, K, c), MXU),
        compiler_params=_cp("parallel"),
    )(a, b)


def _wgrad_in_t(dps, h1, name):
    S, K = h1.shape
    cw = 512
    counts = [b.shape[1] // cw for b in dps]
    starts = [sum(counts[:i]) for i in range(len(dps))]
    nblk = sum(counts)

    def body(*refs):
        b_refs = refs[:len(dps)]
        h_ref, o_ref = refs[len(dps):]
        j = pl.program_id(0)
        for b_ref, st, cnt in zip(b_refs, starts, counts):
            @pl.when(jnp.logical_and(j >= st, j < st + cnt))
            def _():
                o_ref[...] = _dot_tn(b_ref[...], h_ref[...]).astype(MXU)

    def src_spec(st, cnt):
        return pl.BlockSpec((S, cw), lambda j: (0, jnp.clip(j - st, 0, cnt - 1)))

    return pl.pallas_call(
        body, name=name, grid=(nblk,),
        in_specs=[src_spec(st, cnt) for st, cnt in zip(starts, counts)] + [_full((S, K))],
        out_specs=pl.BlockSpec((cw, K), lambda j: (j, 0)),
        out_shape=_sds((nblk * cw, K), MXU),
        compiler_params=_cp("arbitrary"),
    )(*dps, h1)


def _wgrad_blk(a3, b3, nblk, a_of, b_of, name):
    S, K = a3.shape[1:]
    N = b3.shape[2]

    def body(a_ref, b_ref, o_ref):
        o_ref[0] = _dot_tn(a_ref[0], b_ref[0]).astype(MXU)

    return pl.pallas_call(
        body, name=name, grid=(nblk,),
        in_specs=[pl.BlockSpec((1, S, K), lambda b: (a_of(b), 0, 0)),
                  pl.BlockSpec((1, S, N), lambda b: (b_of(b), 0, 0))],
        out_specs=pl.BlockSpec((1, K, N), lambda b: (b, 0, 0)),
        out_shape=_sds((nblk, K, N), MXU),
        compiler_params=_cp("parallel"),
    )(a3, b3)


def _assemble_cols(blocks_list, name):
    def body(*refs):
        n = len(blocks_list)
        for b_ref, o_ref in zip(refs[:n], refs[n:]):
            c = b_ref.shape[2]
            for d in range(N_DEV):
                o_ref[:, c * d:c * (d + 1)] = b_ref[d]

    return pl.pallas_call(
        body, name=name,
        out_shape=[_sds((b.shape[1], N_DEV * b.shape[2]), b.dtype) for b in blocks_list],
        compiler_params=pltpu.CompilerParams(vmem_limit_bytes=VMEM_LIMIT),
    )(*blocks_list)


def _tile(S, want):
    return want if S % want == 0 else S


def _local_step(x, tgt, p, mixer_weights, ffn_weights, grads_out):
    S = x.shape[0]
    tm = _tile(S, 256)
    tl = _tile(S, 512)

    rep = lambda a: jnp.repeat(a, SSM_H, axis=0)
    are = rep(p["a_re"])
    aim = rep(p["a_im"])
    ldt = jnp.broadcast_to(rep(p["log_dt"].reshape(SSM_G, 1)), are.shape)
    br_t = p["b_re_t"].reshape(are.shape)
    bi_t = p["b_im_t"].reshape(are.shape)
    abr, abi, bbr, bbi = _s5_params_fwd(are, aim, ldt, br_t, bi_t)
    head = lambda a: a.reshape(SSM_G, SSM_H, SSM_P)[:, 0, :].reshape(1, SSM_G * SSM_P)
    abar_re, abar_im = head(abr), head(abi)
    bd_br = _blockdiag(bbr).astype(MXU)
    bd_bi = _blockdiag(bbi).astype(MXU)
    bd_cr = _blockdiag(p["c_re"].reshape(are.shape)).astype(MXU)
    bd_ci = _blockdiag(p["c_im"].reshape(are.shape)).astype(MXU)
    d_skip = p["d_skip"].reshape(1, SSM_W)

    tril = jnp.tril(jnp.ones((CHUNK, CHUNK), dtype=bool))
    ws = jnp.where(tril[None], p["w_s"], 0.0)
    ws_b = ws.astype(MXU)
    ws_t = ws.transpose(0, 2, 1).astype(MXU)
    bias_s = jnp.repeat(p["b_s"].T, SGU_D, axis=1)

    g_mix = p["g_mix"].reshape(1, D_MODEL)
    g_ffn = p["g_ffn"].reshape(1, D_MODEL)
    g_final = p["g_final"].reshape(1, D_MODEL)
    g_sgu = p["g_sgu"].reshape(1, SGU_W)
    b_glu = p["b_glu"].reshape(1, SSM_W)
    conv_b = p["conv_b"].reshape(N_DEV, 1, FF_CW)

    h1, us, uv, gl = _in_fwd(x, g_mix, p["w_in_t"], tm)
    st_re, st_im, ys = _s5_fwd(us, abar_re, abar_im, bd_br, bd_bi, bd_cr, bd_ci, d_skip, tl)
    p = dict(p, **mixer_weights(ys))
    yg, yap, sg, ya, yb, m, x1, h2 = _mix_fwd(x, ys, uv, gl, p["w_glu"], b_glu, p["w_proj_a"], g_sgu, ws_b, bias_s,
                                              p["w_proj_b"], p["w_out"], g_ffn, tm)
    w_up, conv_w, w_down = ffn_weights(h2)
    up, ab, ff, dx2, dx2b, loss, dg_final = _ffn_fwd(h2, x1, tgt, w_up, conv_w, conv_b, w_down, g_final, tl)

    dup, dx1, dx1b, dconv, dg_ffn = _ffn_bwd(dx2, up, ab, x1, w_up, conv_w, w_down, g_ffn, tl)
    rows8 = lambda g: g.reshape(N_DEV, g.shape[1] // N_DEV, g.shape[2])
    g_up = _wgrad_blk(dup.reshape(N_DEV, S, FF_CW), h2[None], N_DEV, lambda b: b, lambda b: 0, "wgrad_up")
    g_down = _wgrad_blk(ff, dx2b[None], FF_NCB, lambda b: b, lambda b: 0, "wgrad_down").reshape(
        N_DEV, D_FF // N_DEV, D_MODEL)
    token = grads_out(("w_up", "w_down"), (g_up, g_down))
    dgl, dya, dyb, dz, dys, duv, db_glu, dg_sgu, dws, dbs = _mix_bwd(
        dx1, gl, ya, yb, ys, uv, p["w_out"], p["w_proj_a"], p["w_proj_b"], p["w_glu"], b_glu + token[0:1, 0:1], g_sgu,
        ws_b, ws_t, bias_s, tm)
    token = grads_out(("w_glu", "w_proj_a", "w_proj_b", "w_out"),
                      (rows8(_wgrad_split(yg, dz, 1, SSM_W, "wgrad_glu")),
                       _wgrad_split(yap, dya, N_DEV, SSM_W, "wgrad_pa"),
                       _wgrad_split(sg, dyb, N_DEV, SGU_W, "wgrad_pb"),
                       rows8(_wgrad_split(m, dx1b, 1, 512, "wgrad_out"))))
    dus, dab, dd, dbbr, dbbi, dcr, dci = _s5_bwd(dys, us, st_re, st_im, abar_re, abar_im, bd_br, bd_bi, bd_cr, bd_ci,
                                                 d_skip + token[0:1, 0:1], tl)
    g_in = _wgrad_in_t([dus, duv, dgl], h1, "wgrad_in")
    token = grads_out(("w_in",), (g_in.reshape(N_DEV, g_in.shape[0] // N_DEV, D_MODEL),))
    grad_x, dg_mix = _in_bwd(dus, duv, dgl, dx1, x, g_mix + token[0:1, 0:1], p["w_in_t"], tm)

    spread = lambda v: jnp.repeat(v.reshape(SSM_G, SSM_P), SSM_H, axis=0) * (1.0 / SSM_H)
    dabr = spread(dab[:, 0, :])
    dabi = spread(dab[:, 1, :])
    dare, daim, dldt, dbr_t, dbi_t = _s5_params_bwd(are, aim, ldt, br_t, bi_t, dabr, dabi,
                                                    _unblockdiag(dbbr), _unblockdiag(dbbi))
    fold = lambda a: a.reshape(SSM_G, SSM_H, SSM_P).sum(axis=1)

    grads = {
        "g_mix": dg_mix,
        "a_re": fold(dare), "a_im": fold(daim), "log_dt": fold(dldt).sum(axis=1),
        "b_re": dbr_t, "b_im": dbi_t,
        "c_re": _unblockdiag(dcr).reshape(SSM_G, SSM_H, SSM_P),
        "c_im": _unblockdiag(dci).reshape(SSM_G, SSM_H, SSM_P),
        "d_skip": dd[:, 0, :].reshape(SSM_W),
        "b_glu": db_glu,
        "g_sgu": dg_sgu,
        "w_s": dws,
        "b_s": dbs.reshape(CHUNK, SGU_G, SGU_D).sum(axis=-1).T,
        "g_ffn": dg_ffn,
        "conv_w": dconv[:, 0:3, :],
        "conv_b": dconv[:, 3, :].reshape(2 * D_FF),
        "g_final": dg_final,
    }
    return loss, grad_x, grads


_ANY = pl.BlockSpec(memory_space=pl.ANY)
_MESH = pl.DeviceIdType.MESH


def _allgather(shards, dtypes, name, cast_only=()):
    n = len(shards)
    e = len(cast_only)

    def body(*refs):
        in_refs, extra_in = refs[:n], refs[n:n + e]
        out_refs, extra_out = refs[n + e:2 * n + e], refs[2 * n + e:2 * n + 2 * e]
        stage = refs[2 * n + 2 * e:3 * n + 2 * e]
        send_sems, recv_sems, local_sems = refs[3 * n + 2 * e:]
        for a in range(n):
            stage[a][...] = in_refs[a][...].astype(dtypes[a])
        for i in range(e):
            extra_out[i][...] = extra_in[i][...].astype(MXU)
        x, y, c = lax.axis_index("x"), lax.axis_index("y"), lax.axis_index("c")
        me, sibling = (x, y, c), (x, y, 1 - c)
        chips = [(1 - x, y), (x, 1 - y), (1 - x, 1 - y)]

        def slot(a, px, py, pc):
            return out_refs[a].at[4 * px + 2 * py + pc]

        def copy(a, k, block, to, src=None):
            return pltpu.make_async_remote_copy(
                src_ref=slot(a, *block) if src is None else src, dst_ref=slot(a, *block),
                send_sem=send_sems.at[a, k], recv_sem=recv_sems.at[a, k], device_id=to, device_id_type=_MESH)

        mine = [pltpu.make_async_copy(stage[a], slot(a, *me), local_sems.at[a]) for a in range(n)]
        for cp in mine:
            cp.start()
        first = []
        for j, chip in enumerate(chips):
            first += [copy(a, 1 + j, me, (*chip, c), src=stage[a]) for a in range(n)]
        first += [copy(a, 0, me, sibling, src=stage[a]) for a in range(n)]
        for cp in first:
            cp.start()
        passed = []
        for j, chip in enumerate(chips):
            for a in range(n):
                copy(a, 1 + j, (*chip, c), me).wait_recv()
                fwd = copy(a, 4 + j, (*chip, c), sibling)
                fwd.start()
                passed.append(fwd)
        for a in range(n):
            copy(a, 0, sibling, me).wait_recv()
        for j, chip in enumerate(chips):
            for a in range(n):
                copy(a, 4 + j, (*chip, 1 - c), me).wait_recv()
        for cp in first + passed:
            cp.wait_send()
        for cp in mine:
            cp.wait()

    vmem = pl.BlockSpec(memory_space=pltpu.VMEM)
    res = pl.pallas_call(
        body, name=name, in_specs=[vmem] * (n + e), out_specs=[_ANY] * n + [vmem] * e,
        out_shape=[_sds((N_DEV,) + s.shape, dt) for s, dt in zip(shards, dtypes)]
                  + [_sds(s.shape, MXU) for s in cast_only],
        scratch_shapes=[pltpu.VMEM(s.shape, dt) for s, dt in zip(shards, dtypes)]
                       + [pltpu.SemaphoreType.DMA((n, 7)), pltpu.SemaphoreType.DMA((n, 7)), pltpu.SemaphoreType.DMA((n,))],
        compiler_params=pltpu.CompilerParams(vmem_limit_bytes=VMEM_LIMIT),
    )(*shards, *cast_only)
    return res[:n], res[n:]


def _all_to_all(sends, name):
    n = len(sends)

    def body(*refs):
        send_refs, recv_refs = refs[:n], refs[n:2 * n]
        send_sems, recv_sems, local_sems = refs[2 * n:]
        x, y, c = lax.axis_index("x"), lax.axis_index("y"), lax.axis_index("c")
        me = 4 * x + 2 * y + c
        mine = [pltpu.make_async_copy(send_refs[a].at[me], recv_refs[a].at[me], local_sems.at[a]) for a in range(n)]
        for cp in mine:
            cp.start()
        copies = []
        for k in (2, 4, 6, 3, 5, 7, 1):
            px = 1 - x if k & 4 else x
            py = 1 - y if k & 2 else y
            pc = 1 - c if k & 1 else c
            peer = 4 * px + 2 * py + pc
            for a in range(n):
                sems = dict(send_sem=send_sems.at[a, k - 1], recv_sem=recv_sems.at[a, k - 1],
                            device_id=(px, py, pc), device_id_type=_MESH)
                cp = pltpu.make_async_remote_copy(src_ref=send_refs[a].at[peer], dst_ref=recv_refs[a].at[me], **sems)
                cp.start()
                landing = pltpu.make_async_remote_copy(src_ref=send_refs[a].at[peer], dst_ref=recv_refs[a].at[peer],
                                                       **sems)
                copies.append((cp, landing))
        for _, landing in copies:
            landing.wait_recv()
        for cp, _ in copies:
            cp.wait_send()
        for cp in mine:
            cp.wait()

    return pl.pallas_call(
        body, name=name, in_specs=[_ANY] * n, out_specs=[_ANY] * n,
        out_shape=[_sds(s.shape, s.dtype) for s in sends],
        scratch_shapes=[pltpu.SemaphoreType.DMA((n, 7)), pltpu.SemaphoreType.DMA((n, 7)), pltpu.SemaphoreType.DMA((n,))],
    )(*sends)


_HBM = pl.BlockSpec(memory_space=pltpu.HBM)
_SEM = pl.BlockSpec(memory_space=pltpu.SEMAPHORE)
_EFFECT = pltpu.SideEffectType.DATAFLOW_SIDE_EFFECTING
_PEER_ORDER = (2, 4, 6, 3, 5, 7, 1)


def _peer(k):
    x, y, c = lax.axis_index("x"), lax.axis_index("y"), lax.axis_index("c")
    px = 1 - x if k & 4 else x
    py = 1 - y if k & 2 else y
    pc = 1 - c if k & 1 else c
    return (px, py, pc), 4 * px + 2 * py + pc


def _push_start(srcs, lands, slotted, name):
    n = len(srcs)

    def body(*refs):
        src_refs, land_refs = refs[:n], refs[n:2 * n]
        send_sems, recv_sems, token = refs[2 * n], refs[2 * n + 1], refs[-1]
        me = 4 * lax.axis_index("x") + 2 * lax.axis_index("y") + lax.axis_index("c")
        for k in _PEER_ORDER:
            dev, peer = _peer(k)
            for a in range(n):
                pltpu.make_async_remote_copy(
                    src_ref=src_refs[a].at[peer] if slotted else src_refs[a], dst_ref=land_refs[a].at[me],
                    send_sem=send_sems.at[7 * a + k - 1], recv_sem=recv_sems.at[7 * a + k - 1],
                    device_id=dev, device_id_type=_MESH).start()
        token[...] = jnp.zeros_like(token)

    bufs = list(srcs) + list(lands)
    res = pl.pallas_call(
        body, name=name, in_specs=[_HBM] * (2 * n),
        out_specs=(_SEM, _SEM, *[_HBM] * (2 * n), pl.BlockSpec(memory_space=pltpu.VMEM)),
        out_shape=(pltpu.SemaphoreType.DMA((7 * n,)), pltpu.SemaphoreType.DMA((7 * n,)),
                   *[pltpu.HBM(b.shape, b.dtype) for b in bufs], _sds((8, LANES))),
        input_output_aliases={i: 2 + i for i in range(2 * n)},
        compiler_params=pltpu.CompilerParams(has_side_effects=_EFFECT),
    )(*[pltpu.with_memory_space_constraint(b, pltpu.HBM) for b in bufs])
    return res[0], res[1], res[2:2 + n], res[2 + n:2 + 2 * n], res[-1]


def _push_wait(send_sems, recv_sems, srcs, lands, slotted, after, name):
    n = len(srcs)

    def body(*refs):
        src_refs, land_refs = refs[:n], refs[n:2 * n]
        send_sems, recv_sems = refs[2 * n], refs[2 * n + 1]
        for k in _PEER_ORDER:
            dev, peer = _peer(k)
            for a in range(n):
                cp = pltpu.make_async_remote_copy(
                    src_ref=src_refs[a].at[peer] if slotted else src_refs[a], dst_ref=land_refs[a].at[peer],
                    send_sem=send_sems.at[7 * a + k - 1], recv_sem=recv_sems.at[7 * a + k - 1],
                    device_id=dev, device_id_type=_MESH)
                cp.wait_send()
                cp.wait_recv()

    bufs = list(srcs) + list(lands)
    res = pl.pallas_call(
        body, name=name, in_specs=[_HBM] * (2 * n) + [_SEM, _SEM] + [_ANY] * len(after), out_specs=[_HBM] * (2 * n),
        out_shape=[pltpu.HBM(b.shape, b.dtype) for b in bufs],
        input_output_aliases={i: i for i in range(2 * n)},
        compiler_params=pltpu.CompilerParams(has_side_effects=_EFFECT),
    )(*bufs, send_sems, recv_sems, *after)
    return res[n:]


def _adamw(w, g, m, v):
    m2 = ADAM_B1 * m + (1.0 - ADAM_B1) * g
    v2 = ADAM_B2 * v + (1.0 - ADAM_B2) * (g * g)
    m_hat = m2 / (1.0 - ADAM_B1 ** ADAM_STEP)
    v_hat = v2 / (1.0 - ADAM_B2 ** ADAM_STEP)
    delta = -ADAM_LR * (m_hat / (jnp.sqrt(v_hat) + ADAM_EPS) + ADAM_WD * w)
    return delta, m2, v2


def _adam_shard(parts, w, m, v, name):
    _, r, c = w.shape
    tr = max(t for t in range(16, 257, 16) if r % t == 0)

    def body(p_ref, w_ref, m_ref, v_ref, g_ref, d_ref, m2_ref, v2_ref):
        g = p_ref[0].astype(F32)
        for s in range(1, N_DEV):
            g = g + p_ref[s].astype(F32)
        g_ref[0] = g
        d_ref[0], m2_ref[0], v2_ref[0] = _adamw(w_ref[0], g, m_ref[0], v_ref[0])

    row = lambda: pl.BlockSpec((1, tr, c), lambda i: (0, i, 0))
    return pl.pallas_call(
        body, name=name, grid=(r // tr,),
        in_specs=[pl.BlockSpec((N_DEV, tr, c), lambda i: (0, i, 0)), row(), row(), row()],
        out_specs=[row(), row(), row(), row()], out_shape=[_sds((1, r, c))] * 4,
        compiler_params=_cp("parallel"),
    )(parts, w, m, v)


def _adam_small(gs, ws, ms, vs, name):
    n = len(gs)

    def body(*refs):
        ins, outs = refs[:4 * n], refs[4 * n:]
        for i in range(n):
            g = ins[i][...]
            d, m2, v2 = _adamw(ins[n + i][...], g, ins[2 * n + i][...], ins[3 * n + i][...])
            outs[i][...] = d
            outs[n + i][...] = m2
            outs[2 * n + i][...] = v2

    res = pl.pallas_call(
        body, name=name, out_shape=[_sds(w.shape) for w in ws] * 3,
        compiler_params=pltpu.CompilerParams(vmem_limit_bytes=VMEM_LIMIT),
    )(*gs, *ws, *ms, *vs)
    return res[:n], res[n:2 * n], res[2 * n:]


def _sum_slots(parts, name):
    R = parts.shape[1]

    def body(p_ref, o_ref):
        g = p_ref[0]
        for s in range(1, N_DEV):
            g = g + p_ref[s]
        o_ref[...] = g

    return pl.pallas_call(body, name=name, out_shape=_sds((R, LANES)))(parts)


def _pad_to(a, n, axis):
    extra = n - a.shape[axis]
    if extra == 0:
        return a
    widths = [(0, 0)] * a.ndim
    widths[axis] = (0, extra)
    return jnp.pad(a, widths)


def _ceil_to(n, k):
    return -(-n // k) * k


def _pack_rows(flats, rows_multiple):
    parts = [_pad_to(f, _ceil_to(f.shape[-1], LANES), f.ndim - 1) for f in flats]
    cat = jnp.concatenate(parts, axis=-1)
    total = _ceil_to(cat.shape[-1], LANES * rows_multiple)
    cat = _pad_to(cat, total, cat.ndim - 1)
    return cat.reshape(cat.shape[:-1] + (total // LANES, LANES))


def _unpack_rows(buf, sizes):
    flat = buf.reshape(buf.shape[:-2] + (-1,))
    out, off = [], 0
    for n in sizes:
        out.append(flat[..., off:off + n])
        off += _ceil_to(n, LANES)
    return out


_MIX_BIG = ("w_in", "w_glu", "w_proj_a", "w_proj_b", "w_out")
_BIG = _MIX_BIG + ("w_up", "w_down")
_SMALL = ("g_mix", "a_re", "a_im", "log_dt", "b_re", "b_im", "c_re", "c_im", "d_skip", "b_glu", "g_sgu", "w_s", "b_s",
          "g_ffn", "conv_b", "g_final")
_SMALL_ROWS_MULTIPLE = 8 * N_DEV
_TRANSPOSED = ("w_in", "w_up", "b_re", "b_im")


def _as_2d(a):
    return a.reshape(-1, a.shape[-1]) if a.ndim > 1 else a.reshape(1, -1)


def kernel(x, g_mix, w_in, a_re, a_im, log_dt, b_re, b_im, c_re, c_im, d_skip, w_glu, b_glu, w_proj_a, g_sgu, w_s, b_s, w_proj_b, w_out, g_ffn, w_up, conv_w, conv_b, w_down, g_final, loss_target, m_g_mix, m_w_in, m_a_re, m_a_im, m_log_dt, m_b_re, m_b_im, m_c_re, m_c_im, m_d_skip, m_w_glu, m_b_glu, m_w_proj_a, m_g_sgu, m_w_s, m_b_s, m_w_proj_b, m_w_out, m_g_ffn, m_w_up, m_conv_w, m_conv_b, m_w_down, m_g_final, v_g_mix, v_w_in, v_a_re, v_a_im, v_log_dt, v_b_re, v_b_im, v_c_re, v_c_im, v_d_skip, v_w_glu, v_b_glu, v_w_proj_a, v_g_sgu, v_w_s, v_b_s, v_w_proj_b, v_w_out, v_g_ffn, v_w_up, v_conv_w, v_conv_b, v_w_down, v_g_final):
    args = dict(locals())
    me = 4 * lax.axis_index("x") + 2 * lax.axis_index("y") + lax.axis_index("c")

    def own_slot(buf, block):
        return lax.dynamic_update_slice(buf, block[None], (me,) + (0,) * block.ndim)

    for n in _TRANSPOSED:
        for pre in ("", "m_", "v_"):
            args[pre + n] = jnp.swapaxes(args[pre + n], -1, -2)
    later = ("w_glu", "w_proj_a", "w_proj_b", "w_out", "w_up", "w_down")
    (w_in_g,), casts = _allgather([args["w_in"][0]], [MXU], "allgather_w_in", cast_only=[args[n][0] for n in later])
    sh = dict(zip(later, casts))

    def start_push(names, srcs, tag):
        lands = [own_slot(lax.empty((N_DEV,) + s.shape, s.dtype), s) for s in srcs]
        send_sems, recv_sems, srcs, lands, token = _push_start(srcs, lands, False, "push_" + tag)
        return (send_sems, recv_sems, srcs, lands), token

    mix_push, token_a = start_push(later[:4], [sh[n] for n in later[:4]], "mixer_weights")
    ffn_push, token_b = start_push(("w_up", "w_down", "conv_w"), [sh["w_up"], sh["w_down"], conv_w[0]], "ffn_weights")
    p = {n: (args[n][0] if n != "g_final" else args[n]) for n in _SMALL if n not in _TRANSPOSED}
    p.update(w_in_t=w_in_g.reshape(SSM_W + 2 * SGU_W + 2 * D_MODEL, D_MODEL),
             b_re_t=args["b_re"][0], b_im_t=args["b_im"][0])
    p["g_mix"] = p["g_mix"] + (token_a[0:1, 0:1] + token_b[0:1, 0:1])

    def mixer_weights(after):
        w_glu_g, w_pa_g, w_pb_g, w_out_g = _push_wait(*mix_push, False, [after], "wait_mixer_weights")
        w_pa_full, w_pb_full = _assemble_cols([w_pa_g, w_pb_g], "assemble_cols")
        return dict(w_glu=w_glu_g.reshape(SSM_W, SSM_W), w_proj_a=w_pa_full, w_proj_b=w_pb_full,
                    w_out=w_out_g.reshape(D_MODEL, D_MODEL))

    def ffn_weights(after):
        w_up_g, w_down_g, conv_w_g = _push_wait(*ffn_push, False, [after], "wait_ffn_weights")
        return w_up_g, conv_w_g, w_down_g.reshape(D_FF, D_MODEL)

    pushes = []

    def grads_out(names, sends):
        lands = [own_slot(lax.empty(s.shape, s.dtype), lax.dynamic_index_in_dim(s, me, 0, keepdims=False))
                 for s in sends]
        send_sems, recv_sems, srcs, lands, token = _push_start(list(sends), lands, True, "push_grads_" + names[0])
        pushes.append((names, send_sems, recv_sems, srcs, lands))
        return token

    loss_part, grad_x, grads = _local_step(x[0], loss_target[0], p, mixer_weights, ffn_weights, grads_out)

    small_names = _SMALL + ("conv_w", "loss")
    small_g = dict(grads, loss=loss_part[0, 0:1])
    flats = [small_g[n].reshape(-1) for n in small_names]
    small_sizes = [f.shape[0] for f in flats]
    g_small = _pack_rows(flats, _SMALL_ROWS_MULTIPLE)
    rs8 = g_small.shape[0] // N_DEV
    grads_out(("small",), (g_small.reshape(N_DEV, rs8, LANES),))

    out = {}
    done = [g_small]
    for names, send_sems, recv_sems, srcs, lands in pushes:
        parts = _push_wait(send_sems, recv_sems, srcs, lands, True, done, "wait_grads_" + names[0])
        if names == ("small",):
            recv_small, = parts
            break
        for n, part in zip(names, parts):
            res = _adam_shard(part, args[n], args["m_" + n], args["v_" + n], "adam_" + n)
            for kind, v in zip(("grad_", "delta_", "new_m_", "new_v_"), res):
                out[kind + n] = v
            done = [res[0]]
    small_mine = _sum_slots(recv_small, "sum_small")
    g_small_all = _allgather([small_mine], [F32], "allgather_small")[0][0].reshape(N_DEV * rs8, LANES)
    pieces = dict(zip(small_names, _unpack_rows(g_small_all, small_sizes)))
    loss = pieces["loss"][0]
    dconv_w = lax.dynamic_index_in_dim(pieces["conv_w"].reshape(N_DEV, 3, FF_CW), me, axis=0, keepdims=False)
    names2 = _SMALL + ("conv_w",)
    gs = [pieces[n].reshape(_as_2d(args[n]).shape) for n in _SMALL] + [dconv_w]
    ds, m2s, v2s = _adam_small(gs, [_as_2d(args[n]) for n in names2], [_as_2d(args["m_" + n]) for n in names2],
                               [_as_2d(args["v_" + n]) for n in names2], "adam_small")
    for n, res in zip(names2, zip(gs, ds, m2s, v2s)):
        for kind, v in zip(("grad_", "delta_", "new_m_", "new_v_"), res):
            out[kind + n] = v.reshape(args[n].shape)
    order = ("g_mix", "w_in", "a_re", "a_im", "log_dt", "b_re", "b_im", "c_re", "c_im", "d_skip", "w_glu", "b_glu",
             "w_proj_a", "g_sgu", "w_s", "b_s", "w_proj_b", "w_out", "g_ffn", "w_up", "conv_w", "conv_b", "w_down",
             "g_final")
    res = [loss, grad_x.reshape(x.shape)]
    for kind in ("grad_", "delta_", "new_m_", "new_v_"):
        res += [jnp.swapaxes(out[kind + n], -1, -2) if n in _TRANSPOSED else out[kind + n] for n in order]
    return tuple(res)
```

```python
import functools
import math

import jax
import jax.numpy as jnp
from jax import lax
from jax.experimental import pallas as pl
from jax.experimental.pallas import tpu as pltpu

F32 = jnp.float32
MXU = jnp.bfloat16
EPS = 1e-6

D_MODEL = 1024
SSM_W = 512
SSM_G, SSM_H, SSM_P = 32, 16, 64
SSM_BLK = 4
SGU_W = 512
SGU_G, SGU_D, CHUNK = 8, 64, 128
D_FF = 2816
N_DEV = 8
FF_CW = 2 * D_FF // N_DEV
FF_NCB = D_FF // FF_CW
LANES = 128

ADAM_LR, ADAM_B1, ADAM_B2, ADAM_EPS, ADAM_WD, ADAM_STEP = 0.001, 0.9, 0.999, 1e-08, 0.01, 10

VMEM_LIMIT = 48 * 1024 * 1024


def _cp(*sem):
    return pltpu.CompilerParams(dimension_semantics=sem, vmem_limit_bytes=VMEM_LIMIT)


def _full(shape):
    n = len(shape)
    return pl.BlockSpec(shape, lambda *_: (0,) * n)


def _sds(shape, dtype=F32):
    return jax.ShapeDtypeStruct(shape, dtype)


def _dot(a, b):
    return jnp.dot(a, b, preferred_element_type=F32)


def _dot_nt(a, b):
    return lax.dot_general(a, b, (((1,), (1,)), ((), ())), preferred_element_type=F32)


def _dot_tn(a, b):
    return lax.dot_general(a, b, (((0,), (0,)), ((), ())), preferred_element_type=F32)


_GELU_C = math.sqrt(2.0 / math.pi)


def _gelu(x):
    return 0.5 * x * (1.0 + jnp.tanh(_GELU_C * (x + 0.044715 * (x * x * x))))


def _gelu_and_grad(x):
    t = jnp.tanh(_GELU_C * (x + 0.044715 * (x * x * x)))
    g = 0.5 * x * (1.0 + t)
    dg = 0.5 * (1.0 + t) + 0.5 * x * (1.0 - t * t) * (_GELU_C * (1.0 + 3.0 * 0.044715 * (x * x)))
    return g, dg


def _sigmoid(x):
    return 0.5 * jnp.tanh(0.5 * x) + 0.5


def _rms(x):
    return lax.rsqrt(jnp.mean(x * x, axis=-1, keepdims=True) + EPS)


def _rms_bwd(dxn, xn, r):
    return r * (dxn - xn * jnp.mean(dxn * xn, axis=-1, keepdims=True))


def _rowsum(x):
    return jnp.sum(x, axis=0, keepdims=True)


def _s5_disc(are, aim, ldt, br, bi):
    dt = jnp.exp(ldt)
    mag = jnp.exp(dt * are)
    abr = mag * jnp.cos(dt * aim)
    abi = mag * jnp.sin(dt * aim)
    den = are * are + aim * aim
    nr = abr - 1.0
    ni = abi
    fr = (nr * are + ni * aim) / den
    fi = (ni * are - nr * aim) / den
    return abr, abi, fr * br - fi * bi, fr * bi + fi * br


def _s5_params_fwd(are, aim, ldt, br, bi):
    def body(are_ref, aim_ref, ldt_ref, br_ref, bi_ref, o0, o1, o2, o3):
        outs = _s5_disc(are_ref[...], aim_ref[...], ldt_ref[...], br_ref[...], bi_ref[...])
        for o, v in zip((o0, o1, o2, o3), outs):
            o[...] = v
    shp = are.shape
    return pl.pallas_call(body, name="s5_params_fwd", grid=(1,), in_specs=[_full(shp)] * 5, out_specs=[_full(shp)] * 4,
                          out_shape=[_sds(shp)] * 4)(are, aim, ldt, br, bi)


def _s5_params_bwd(are, aim, ldt, br, bi, dabr, dabi, dbr, dbi):
    def body(are_ref, aim_ref, ldt_ref, br_ref, bi_ref, c0, c1, c2, c3, o0, o1, o2, o3, o4):
        prim = (are_ref[...], aim_ref[...], ldt_ref[...], br_ref[...], bi_ref[...])
        _, vjp = jax.vjp(_s5_disc, *prim)
        outs = vjp((c0[...], c1[...], c2[...], c3[...]))
        for o, v in zip((o0, o1, o2, o3, o4), outs):
            o[...] = v
    shp = are.shape
    return pl.pallas_call(body, name="s5_params_bwd", grid=(1,), in_specs=[_full(shp)] * 9, out_specs=[_full(shp)] * 5,
                          out_shape=[_sds(shp)] * 5)(are, aim, ldt, br, bi, dabr, dabi, dbr, dbi)


def _blockdiag(m_t):
    m = m_t.reshape(SSM_BLK, 8, SSM_H, 1, SSM_P)
    eye = jnp.eye(8, dtype=bool).reshape(1, 8, 1, 8, 1)
    return jnp.where(eye, m, jnp.zeros((), m_t.dtype)).reshape(SSM_BLK, 8 * SSM_H, 8 * SSM_P)


def _unblockdiag(pc):
    m = pc.reshape(SSM_BLK, 8, SSM_H, 8, SSM_P)
    return jnp.einsum("jghgp->jghp", m).reshape(SSM_G * SSM_H, SSM_P)


def _in_fwd(x, g_mix, w_in_t, tm):
    S = x.shape[0]

    def body(x_ref, g_ref, w_ref, h_ref, us_ref, uv_ref, gl_ref):
        xv = x_ref[...]
        h = (xv * _rms(xv) * g_ref[...]).astype(MXU)
        h_ref[...] = h
        us_ref[...] = _dot_nt(h, w_ref[0:SSM_W, :])
        uv_ref[...] = _dot_nt(h, w_ref[SSM_W:SSM_W + 2 * SGU_W, :])
        gl_ref[...] = _dot_nt(h, w_ref[SSM_W + 2 * SGU_W:, :])

    row = lambda n: pl.BlockSpec((tm, n), lambda i: (i, 0))
    return pl.pallas_call(
        body, name="in_fwd", grid=(S // tm,),
        in_specs=[row(D_MODEL), _full((1, D_MODEL)), _full(w_in_t.shape)],
        out_specs=[row(D_MODEL), row(SSM_W), row(2 * SGU_W), row(2 * D_MODEL)],
        out_shape=[_sds((S, D_MODEL), MXU), _sds((S, SSM_W)), _sds((S, 2 * SGU_W)), _sds((S, 2 * D_MODEL))],
        compiler_params=_cp("parallel"),
    )(x, g_mix, w_in_t)


def _scan_tables(ar, ai, reverse):
    n = ar.shape[-1]
    def mul(p, q):
        return p[0] * q[0] - p[1] * q[1], p[0] * q[1] + p[1] * q[0]
    a1 = (ar, ai)
    a2 = mul(a1, a1)
    a3 = mul(a2, a1)
    a4 = mul(a2, a2)
    a5 = mul(a4, a1)
    a6 = mul(a4, a2)
    a7 = mul(a4, a3)
    a8 = mul(a4, a4)
    pw = (a1, a2, a3, a4, a5, a6, a7, a8)
    rows = lax.broadcasted_iota(jnp.int32, (8, n), 0)
    tabs = []
    for s, a in ((1, a1), (2, a2), (4, a4)):
        keep = (rows + s <= 7) if reverse else (rows >= s)
        for comp in a:
            tabs.append(jnp.where(keep, jnp.broadcast_to(comp, (8, n)), 0.0))
    for c in range(2):
        q = jnp.zeros((8, n), F32)
        for r in range(8):
            e = (8 - r) if reverse else (r + 1)
            q = jnp.where(rows == r, jnp.broadcast_to(pw[e - 1][c], (8, n)), q)
        tabs.append(q)
    return tabs


def _scan_group(xr, xi, tab_ref, cr, ci, reverse):
    for t, s in enumerate((1, 2, 4)):
        pr = tab_ref[2 * t]
        pi = tab_ref[2 * t + 1]
        sh = (8 - s) if reverse else s
        sr = pltpu.roll(xr, sh, 0)
        si = pltpu.roll(xi, sh, 0)
        xr, xi = xr + pr * sr - pi * si, xi + pr * si + pi * sr
    qr = tab_ref[6]
    qi = tab_ref[7]
    return xr + qr * cr - qi * ci, xi + qr * ci + qi * cr


def _runs_load(src_ref, dst_ref, run):
    for i in range(run):
        dst_ref[8 * i:8 * i + 8, :] = src_ref[pl.ds(i, 8, stride=run), :]


def _runs_store(val, dst_ref, run):
    for i in range(run):
        dst_ref[pl.ds(i, 8, stride=run), :] = val[8 * i:8 * i + 8, :]


def _cpow2(ar, ai, log2n):
    for _ in range(log2n):
        ar, ai = ar * ar - ai * ai, 2.0 * ar * ai
    return ar, ai


def _s5_fwd(us, abar_re, abar_im, b_re, b_im, c_re, c_im, d_skip, tm):
    S = us.shape[0]
    nt = S // tm
    w = 8 * SSM_P
    run = tm // 8
    assert run & (run - 1) == 0

    def body(us_ref, ar_ref, ai_ref, br_ref, bi_ref, cr_ref, ci_ref, d_ref, str_ref, sti_ref, ys_ref,
             tab_ref, car_ref, up_ref):
        i = pl.program_id(1)

        @pl.when(i == 0)
        def _():
            car_ref[...] = jnp.zeros_like(car_ref)
            for k, t in enumerate(_scan_tables(*_cpow2(ar_ref[...], ai_ref[...], run.bit_length() - 1), False)):
                tab_ref[k] = t

        _runs_load(us_ref, up_ref, run)
        ub = up_ref[...].astype(MXU)
        str_ref[...] = _dot(ub, br_ref[0])
        sti_ref[...] = _dot(ub, bi_ref[0])
        ar = jnp.broadcast_to(ar_ref[...], (8, w))
        ai = jnp.broadcast_to(ai_ref[...], (8, w))

        def advance(k, state):
            r0 = pl.multiple_of(k * 8, 8)
            sr, si = state
            return (ar * sr - ai * si + str_ref[pl.ds(r0, 8), :], ar * si + ai * sr + sti_ref[pl.ds(r0, 8), :])

        def emit(k, state):
            r0 = pl.multiple_of(k * 8, 8)
            sr, si = advance(k, state)
            str_ref[pl.ds(r0, 8), :] = sr
            sti_ref[pl.ds(r0, 8), :] = si
            return sr, si

        zero = jnp.zeros((8, w), F32)
        er, ei = lax.fori_loop(0, run, advance, (zero, zero))
        cr, ci = car_ref[0:1, :], car_ref[1:2, :]
        tr, ti = _scan_group(er, ei, tab_ref, cr, ci, False)
        r8 = lax.broadcasted_iota(jnp.int32, (8, w), 0)
        start = (jnp.where(r8 == 0, cr, pltpu.roll(tr, 1, 0)), jnp.where(r8 == 0, ci, pltpu.roll(ti, 1, 0)))
        car_ref[0:1, :] = tr[7:8, :]
        car_ref[1:2, :] = ti[7:8, :]
        lax.fori_loop(0, run, emit, start)
        y = _dot_nt(str_ref[...].astype(MXU), cr_ref[0]) - _dot_nt(sti_ref[...].astype(MXU), ci_ref[0])
        _runs_store(y, ys_ref, run)
        ys_ref[...] += d_ref[...] * us_ref[...]

    blk = lambda: pl.BlockSpec((1, 8 * SSM_H, w), lambda j, i: (j, 0, 0))
    return pl.pallas_call(
        body, name="s5_fwd", grid=(SSM_BLK, nt),
        in_specs=[pl.BlockSpec((tm, LANES), lambda j, i: (i, j)),
                  pl.BlockSpec((1, w), lambda j, i: (0, j)), pl.BlockSpec((1, w), lambda j, i: (0, j)),
                  blk(), blk(), blk(), blk(),
                  pl.BlockSpec((1, LANES), lambda j, i: (0, j))],
        out_specs=[pl.BlockSpec((tm, w), lambda j, i: (i, j)), pl.BlockSpec((tm, w), lambda j, i: (i, j)),
                   pl.BlockSpec((tm, LANES), lambda j, i: (i, j))],
        out_shape=[_sds((S, SSM_BLK * w)), _sds((S, SSM_BLK * w)), _sds((S, SSM_W))],
        scratch_shapes=[pltpu.VMEM((8, 8, w), F32), pltpu.VMEM((8, w), F32), pltpu.VMEM((tm, LANES), F32)],
        compiler_params=_cp("parallel", "arbitrary"),
    )(us, abar_re, abar_im, b_re, b_im, c_re, c_im, d_skip)


def _sgu_mix(vnb, ws_ref, grp):
    acc = jnp.zeros(vnb.shape, F32)
    for g in range(SGU_G):
        acc = jnp.where(grp == g, _dot(ws_ref[g], vnb), acc)
    return acc


def _mix_fwd(x, ys, uv, gl, w_glu, b_glu, w_pa, g_sgu, ws, bias_s, w_pb, w_out, g_ffn, tm):
    S = x.shape[0]

    def body(x_ref, ys_ref, uv_ref, gl_ref, wglu_ref, bglu_ref, wpa_ref, gs_ref, ws_ref, bias_ref, wpb_ref, wout_ref,
             gf_ref, yg_ref, yap_ref, sg_ref, ya_ref, yb_ref, m_ref, x1_ref, h2_ref):
        yg = _gelu(ys_ref[...])
        ygb = yg.astype(MXU)
        yg_ref[...] = ygb
        z = _dot(ygb, wglu_ref[...]) + bglu_ref[...]
        yapb = (yg * _sigmoid(z)).astype(MXU)
        yap_ref[...] = yapb
        ya = _dot(yapb, wpa_ref[...])
        ya_ref[...] = ya

        uvg = _gelu(uv_ref[...])
        u2 = uvg[:, :SGU_W]
        v2 = uvg[:, SGU_W:]
        vnb = (v2 * _rms(v2) * gs_ref[...]).astype(MXU)
        grp = lax.broadcasted_iota(jnp.int32, (CHUNK, SGU_W), 1) // SGU_D
        for c in range(tm // CHUNK):
            rs = slice(c * CHUNK, (c + 1) * CHUNK)
            mixed = _sgu_mix(vnb[rs], ws_ref, grp) + bias_ref[...]
            sg_ref[rs, :] = (u2[rs] * mixed).astype(MXU)
        yb = _dot(sg_ref[...], wpb_ref[...])
        yb_ref[...] = yb

        glv = gl_ref[...]
        m = _sigmoid(glv[:, :D_MODEL]) * ya + _sigmoid(glv[:, D_MODEL:]) * yb
        mb = m.astype(MXU)
        m_ref[...] = mb
        x1 = x_ref[...] + _dot(mb, wout_ref[...])
        x1_ref[...] = x1
        h2_ref[...] = (x1 * _rms(x1) * gf_ref[...]).astype(MXU)

    row = lambda n: pl.BlockSpec((tm, n), lambda i: (i, 0))
    return pl.pallas_call(
        body, name="mix_fwd", grid=(S // tm,),
        in_specs=[row(D_MODEL), row(SSM_W), row(2 * SGU_W), row(2 * D_MODEL),
                  _full(w_glu.shape), _full(b_glu.shape), _full(w_pa.shape), _full(g_sgu.shape), _full(ws.shape),
                  _full(bias_s.shape), _full(w_pb.shape), _full(w_out.shape), _full(g_ffn.shape)],
        out_specs=[row(SSM_W), row(SSM_W), row(SGU_W), row(D_MODEL), row(D_MODEL), row(D_MODEL), row(D_MODEL),
                   row(D_MODEL)],
        out_shape=[_sds((S, SSM_W), MXU), _sds((S, SSM_W), MXU), _sds((S, SGU_W), MXU), _sds((S, D_MODEL)),
                   _sds((S, D_MODEL)), _sds((S, D_MODEL), MXU), _sds((S, D_MODEL)), _sds((S, D_MODEL), MXU)],
        compiler_params=_cp("parallel"),
    )(x, ys, uv, gl, w_glu, b_glu, w_pa, g_sgu, ws, bias_s, w_pb, w_out, g_ffn)


def _causal_conv3(u, prev8, cw, cb):
    tm = u.shape[0]
    w0, w1, w2 = cw[0:1], cw[1:2], cw[2:3]
    body = w0 * pltpu.roll(u, 2, 0) + w1 * pltpu.roll(u, 1, 0) + w2 * u + cb
    u8 = u[0:8, :]
    r8 = lax.broadcasted_iota(jnp.int32, u8.shape, 0)
    t1 = prev8[7:8, :]
    t0 = prev8[6:7, :]
    s1 = jnp.where(r8 == 0, t1, pltpu.roll(u8, 1, 0))
    s2 = jnp.where(r8 == 0, t0, jnp.where(r8 == 1, t1, pltpu.roll(u8, 2, 0)))
    first = w0 * s2 + w1 * s1 + w2 * u8 + cb
    return jnp.concatenate([first, body[8:tm, :]], axis=0)


def _causal_conv3_adjoint(d, next8, cw):
    tm = d.shape[0]
    w0, w1, w2 = cw[0:1], cw[1:2], cw[2:3]
    n1 = pltpu.roll(d, tm - 1, 0)
    n2 = pltpu.roll(d, tm - 2, 0)
    body = w2 * d + w1 * n1 + w0 * n2
    d8 = d[tm - 8:tm, :]
    r8 = lax.broadcasted_iota(jnp.int32, d8.shape, 0)
    h0 = next8[0:1, :]
    h1 = next8[1:2, :]
    m1 = jnp.where(r8 == 7, h0, pltpu.roll(d8, 7, 0))
    m2 = jnp.where(r8 == 6, h0, jnp.where(r8 == 7, h1, pltpu.roll(d8, 6, 0)))
    last = w2 * d8 + w1 * m1 + w0 * m2
    out = jnp.concatenate([body[0:tm - 8, :], last], axis=0)
    return out, n1, n2, h0 - d[0:1, :], h1 - d[1:2, :]


def _ffn_fwd(h2, x1, tgt, w_up, conv_w, conv_b, w_down, g_final, tm):
    S = h2.shape[0]
    nt = S // tm
    ncb = FF_NCB

    def body(h2_ref, wa_ref, wb_ref, cwa_ref, cwb_ref, cba_ref, cbb_ref, wd_ref, x1_ref, gf_ref, tgt_ref,
             up_ref, ab_ref, ff_ref, dx2_ref, dx2b_ref, loss_ref, dgf_ref, acc_ref, tail_ref):
        i = pl.program_id(0)
        cb = pl.program_id(1)

        @pl.when(i == 0)
        def _():
            tail_ref[cb] = jnp.zeros((2, 8, FF_CW), F32)

        @pl.when(jnp.logical_and(i == 0, cb == 0))
        def _():
            loss_ref[...] = jnp.zeros_like(loss_ref)
            dgf_ref[...] = jnp.zeros_like(dgf_ref)

        h2v = h2_ref[...]
        ua = _dot_nt(h2v, wa_ref[0])
        ub = _dot_nt(h2v, wb_ref[0])
        up_ref[0, 0] = ua.astype(MXU)
        up_ref[1, 0] = ub.astype(MXU)
        a = _causal_conv3(ua, tail_ref[cb, 0], cwa_ref[0], cba_ref[0])
        b = _causal_conv3(ub, tail_ref[cb, 1], cwb_ref[0], cbb_ref[0])
        tail_ref[cb, 0] = ua[tm - 8:tm, :]
        tail_ref[cb, 1] = ub[tm - 8:tm, :]
        ab_ref[0, 0] = a
        ab_ref[1, 0] = b
        ffb = (a * _sigmoid(a) * b).astype(MXU)
        ff_ref[0] = ffb
        contrib = _dot(ffb, wd_ref[...])

        @pl.when(cb == 0)
        def _():
            acc_ref[...] = contrib

        @pl.when(cb > 0)
        def _():
            acc_ref[...] += contrib

        @pl.when(cb == ncb - 1)
        def _():
            x2 = x1_ref[...] + acc_ref[...]
            r = _rms(x2)
            xn = x2 * r
            g = gf_ref[...]
            diff = xn * g - tgt_ref[...]
            loss_ref[...] += (0.5 / D_MODEL) * jnp.sum(diff * diff)
            dy = diff * (1.0 / D_MODEL)
            dgf_ref[...] += _rowsum(dy * xn)
            dx2 = _rms_bwd(dy * g, xn, r)
            dx2_ref[...] = dx2
            dx2b_ref[...] = dx2.astype(MXU)

    row = lambda n: pl.BlockSpec((tm, n), lambda i, c: (i, 0))
    gate = lambda r: pl.BlockSpec((1, r, FF_CW), lambda i, c: (c, 0, 0))
    lin = lambda r: pl.BlockSpec((1, r, FF_CW), lambda i, c: (ncb + c, 0, 0))
    return pl.pallas_call(
        body, name="ffn_fwd", grid=(nt, ncb),
        in_specs=[row(D_MODEL),
                  pl.BlockSpec((1, FF_CW, D_MODEL), lambda i, c: (c, 0, 0)),
                  pl.BlockSpec((1, FF_CW, D_MODEL), lambda i, c: (ncb + c, 0, 0)),
                  gate(3), lin(3), gate(1), lin(1),
                  pl.BlockSpec((FF_CW, D_MODEL), lambda i, c: (c, 0)),
                  row(D_MODEL), _full((1, D_MODEL)), row(D_MODEL)],
        out_specs=[pl.BlockSpec((2, 1, tm, FF_CW), lambda i, c: (0, c, i, 0)),
                   pl.BlockSpec((2, 1, tm, FF_CW), lambda i, c: (0, c, i, 0)),
                   pl.BlockSpec((1, tm, FF_CW), lambda i, c: (c, i, 0)),
                   row(D_MODEL), row(D_MODEL), _full((1, LANES)), _full((1, D_MODEL))],
        out_shape=[_sds((2, ncb, S, FF_CW), MXU), _sds((2, ncb, S, FF_CW)), _sds((ncb, S, FF_CW), MXU),
                   _sds((S, D_MODEL)), _sds((S, D_MODEL), MXU), _sds((1, LANES)), _sds((1, D_MODEL))],
        scratch_shapes=[pltpu.VMEM((tm, D_MODEL), F32), pltpu.VMEM((ncb, 2, 8, FF_CW), F32)],
        compiler_params=_cp("arbitrary", "arbitrary"),
    )(h2, w_up, w_up, conv_w, conv_w, conv_b, conv_b, w_down, x1, g_final, tgt)


def _ffn_bwd(dx2, up, ab, x1, w_up, conv_w, w_down, g_ffn, tm):
    S = dx2.shape[0]
    nt = S // tm
    ncb = FF_NCB

    def body(dx2_ref, up_ref, ab_ref, cwa_ref, cwb_ref, wd_ref, wa_ref, wb_ref,
             x1_ref, g_ref, dup_ref, dx1_ref, dx1b_ref, dconv_ref, dg_ref, acc_ref, head_ref):
        i = pl.program_id(0)
        cb = pl.program_id(1)
        ri = nt - 1 - i

        @pl.when(i == 0)
        def _():
            head_ref[cb] = jnp.zeros((2, 8, FF_CW), F32)
            dconv_ref[cb] = jnp.zeros((8, FF_CW), F32)
            dconv_ref[ncb + cb] = jnp.zeros((8, FF_CW), F32)

        @pl.when(jnp.logical_and(i == 0, cb == 0))
        def _():
            dg_ref[...] = jnp.zeros_like(dg_ref)

        dff = _dot_nt(dx2_ref[...].astype(MXU), wd_ref[...])
        a = ab_ref[0, 0]
        b = ab_ref[1, 0]
        sa = _sigmoid(a)
        silu = a * sa
        da = (dff * b) * (sa + silu * (1.0 - sa))
        db = dff * silu
        dps = []
        for half, slot, d, cw_ref in ((0, cb, da, cwa_ref), (1, ncb + cb, db, cwb_ref)):
            dp, n1, n2, fix0, fix1 = _causal_conv3_adjoint(d, head_ref[cb, half], cw_ref[0])
            head_ref[cb, half] = d[0:8, :]
            dpb16 = dp.astype(MXU)
            dup_ref[half, 0] = dpb16
            dps.append(dpb16)
            u = up_ref[half, 0].astype(F32)
            u_last = u[tm - 1:tm, :]
            dconv_ref[slot, 0:1, :] += _rowsum(n2 * u) + fix0 * u[tm - 2:tm - 1, :] + fix1 * u_last
            dconv_ref[slot, 1:2, :] += _rowsum(n1 * u) + fix0 * u_last
            dconv_ref[slot, 2:3, :] += _rowsum(d * u)
            dconv_ref[slot, 3:4, :] += _rowsum(d)
        contrib = _dot(dps[0], wa_ref[0]) + _dot(dps[1], wb_ref[0])

        @pl.when(cb == 0)
        def _():
            acc_ref[...] = contrib

        @pl.when(cb > 0)
        def _():
            acc_ref[...] += contrib

        @pl.when(cb == ncb - 1)
        def _():
            x1v = x1_ref[...]
            r = _rms(x1v)
            xn = x1v * r
            dh2 = acc_ref[...]
            dg_ref[...] += _rowsum(dh2 * xn)
            dx1 = dx2_ref[...] + _rms_bwd(dh2 * g_ref[...], xn, r)
            dx1_ref[...] = dx1
            dx1b_ref[...] = dx1.astype(MXU)

    row = lambda n: pl.BlockSpec((tm, n), lambda i, c: (nt - 1 - i, 0))
    colb = lambda: pl.BlockSpec((2, 1, tm, FF_CW), lambda i, c: (0, c, nt - 1 - i, 0))
    gate = lambda r: pl.BlockSpec((1, r, FF_CW), lambda i, c: (c, 0, 0))
    lin = lambda r: pl.BlockSpec((1, r, FF_CW), lambda i, c: (ncb + c, 0, 0))
    return pl.pallas_call(
        body, name="ffn_bwd", grid=(nt, ncb),
        in_specs=[row(D_MODEL), colb(), colb(), gate(3), lin(3),
                  pl.BlockSpec((FF_CW, D_MODEL), lambda i, c: (c, 0)),
                  pl.BlockSpec((1, FF_CW, D_MODEL), lambda i, c: (c, 0, 0)),
                  pl.BlockSpec((1, FF_CW, D_MODEL), lambda i, c: (ncb + c, 0, 0)),
                  row(D_MODEL), _full((1, D_MODEL))],
        out_specs=[colb(), row(D_MODEL), row(D_MODEL), _full((2 * ncb, 8, FF_CW)), _full((1, D_MODEL))],
        out_shape=[_sds((2, ncb, S, FF_CW), MXU), _sds((S, D_MODEL)), _sds((S, D_MODEL), MXU), _sds((2 * ncb, 8, FF_CW)),
                   _sds((1, D_MODEL))],
        scratch_shapes=[pltpu.VMEM((tm, D_MODEL), F32), pltpu.VMEM((ncb, 2, 8, FF_CW), F32)],
        compiler_params=_cp("arbitrary", "arbitrary"),
    )(dx2, up, ab, conv_w, conv_w, w_down, w_up, w_up, x1, g_ffn)


def _mix_bwd(dx1, gl, ya, yb, ys, uv, w_out, w_pa, w_pb, w_glu, b_glu, g_sgu, ws, ws_t, bias_s, tm):
    S = dx1.shape[0]

    def body(dx1_ref, gl_ref, ya_ref, yb_ref, ys_ref, uv_ref, wout_ref, wpa_ref, wpb_ref, wglu_ref, bglu_ref, gs_ref,
             ws_ref, wst_ref, bias_ref,
             dgl_ref, dya_ref, dyb_ref, dz_ref, dys_ref, duv_ref, dbglu_ref, dgs_ref, dws_ref, dbs_ref,
             du2_ref, dvn_ref):
        i = pl.program_id(0)

        @pl.when(i == 0)
        def _():
            dbglu_ref[...] = jnp.zeros_like(dbglu_ref)
            dgs_ref[...] = jnp.zeros_like(dgs_ref)
            dws_ref[...] = jnp.zeros_like(dws_ref)
            dbs_ref[...] = jnp.zeros_like(dbs_ref)

        dm = _dot_nt(dx1_ref[...].astype(MXU), wout_ref[...])
        glv = gl_ref[...]
        ga = _sigmoid(glv[:, :D_MODEL])
        gb = _sigmoid(glv[:, D_MODEL:])
        dgl_ref[:, :D_MODEL] = (dm * ya_ref[...] * ga * (1.0 - ga)).astype(MXU)
        dgl_ref[:, D_MODEL:] = (dm * yb_ref[...] * gb * (1.0 - gb)).astype(MXU)
        dyab = (dm * ga).astype(MXU)
        dybb = (dm * gb).astype(MXU)
        dya_ref[...] = dyab
        dyb_ref[...] = dybb

        dyap = _dot_nt(dyab, wpa_ref[...])
        yg, dgelu = _gelu_and_grad(ys_ref[...])
        sz = _sigmoid(_dot(yg.astype(MXU), wglu_ref[...]) + bglu_ref[...])
        dz = dyap * yg * sz * (1.0 - sz)
        dzb = dz.astype(MXU)
        dz_ref[...] = dzb
        dbglu_ref[...] += _rowsum(dz)
        dys_ref[...] = (dyap * sz + _dot_nt(dzb, wglu_ref[...])) * dgelu

        dsg = _dot_nt(dybb, wpb_ref[...])
        uvg, duvg = _gelu_and_grad(uv_ref[...])
        u2 = uvg[:, :SGU_W]
        v2 = uvg[:, SGU_W:]
        rv = _rms(v2)
        vhat = v2 * rv
        gs = gs_ref[...]
        vnb = (vhat * gs).astype(MXU)
        grp = lax.broadcasted_iota(jnp.int32, (CHUNK, SGU_W), 1) // SGU_D
        tril = (lax.broadcasted_iota(jnp.int32, (CHUNK, CHUNK), 0)
                >= lax.broadcasted_iota(jnp.int32, (CHUNK, CHUNK), 1))
        for c in range(tm // CHUNK):
            rs = slice(c * CHUNK, (c + 1) * CHUNK)
            vc = vnb[rs]
            mixed = _sgu_mix(vc, ws_ref, grp) + bias_ref[...]
            dsg_c = dsg[rs]
            du2_ref[rs, :] = dsg_c * mixed
            dmx = dsg_c * u2[rs]
            dbs_ref[...] += dmx
            dmb = dmx.astype(MXU)
            dvn_ref[rs, :] = _sgu_mix(dmb, wst_ref, grp)
            for g in range(SGU_G):
                part = _dot_nt(jnp.where(grp == g, dmb, jnp.zeros((), MXU)), vc)
                dws_ref[g] += jnp.where(tril, part, 0.0)
        dvn = dvn_ref[...]
        dgs_ref[...] += _rowsum(dvn * vhat)
        dv2 = _rms_bwd(dvn * gs, vhat, rv)
        duv_ref[:, :SGU_W] = (du2_ref[...] * duvg[:, :SGU_W]).astype(MXU)
        duv_ref[:, SGU_W:] = (dv2 * duvg[:, SGU_W:]).astype(MXU)

    row = lambda n: pl.BlockSpec((tm, n), lambda i: (i, 0))
    return pl.pallas_call(
        body, name="mix_bwd", grid=(S // tm,),
        in_specs=[row(D_MODEL), row(2 * D_MODEL), row(D_MODEL), row(D_MODEL), row(SSM_W), row(2 * SGU_W),
                  _full(w_out.shape), _full(w_pa.shape), _full(w_pb.shape), _full(w_glu.shape), _full(b_glu.shape),
                  _full(g_sgu.shape), _full(ws.shape), _full(ws_t.shape), _full(bias_s.shape)],
        out_specs=[row(2 * D_MODEL), row(D_MODEL), row(D_MODEL), row(SSM_W), row(SSM_W), row(2 * SGU_W),
                   _full((1, SSM_W)), _full((1, SGU_W)), _full((SGU_G, CHUNK, CHUNK)), _full((CHUNK, SGU_W))],
        out_shape=[_sds((S, 2 * D_MODEL), MXU), _sds((S, D_MODEL), MXU), _sds((S, D_MODEL), MXU), _sds((S, SSM_W), MXU),
                   _sds((S, SSM_W)), _sds((S, 2 * SGU_W), MXU),
                   _sds((1, SSM_W)), _sds((1, SGU_W)), _sds((SGU_G, CHUNK, CHUNK)), _sds((CHUNK, SGU_W))],
        scratch_shapes=[pltpu.VMEM((tm, SGU_W), F32), pltpu.VMEM((tm, SGU_W), F32)],
        compiler_params=_cp("arbitrary"),
    )(dx1, gl, ya, yb, ys, uv, w_out, w_pa, w_pb, w_glu, b_glu, g_sgu, ws, ws_t, bias_s)


def _s5_bwd(dys, us, st_re, st_im, abar_re, abar_im, b_re, b_im, c_re, c_im, d_skip, tm):
    S = us.shape[0]
    nt = S // tm
    w = 8 * SSM_P
    hb = tm // 8
    run = tm // 8
    assert run & (run - 1) == 0

    def body(dys_ref, us_ref, str_ref, sti_ref, hr_ref, hi_ref, ar_ref, ai_ref, br_ref, bi_ref, cr_ref, ci_ref, d_ref,
             dus_ref, dab_ref, dd_ref, dbr_ref, dbi_ref, dcr_ref, dci_ref,
             tab_ref, car_ref, gr_ref, gi_ref, dyp_ref, up_ref, dun_ref):
        i = pl.program_id(1)
        ri = nt - 1 - i

        @pl.when(i == 0)
        def _():
            car_ref[...] = jnp.zeros_like(car_ref)
            for k, t in enumerate(_scan_tables(*_cpow2(ar_ref[...], -ai_ref[...], run.bit_length() - 1), True)):
                tab_ref[k] = t
            for r in (dab_ref, dd_ref, dbr_ref, dbi_ref, dcr_ref, dci_ref):
                r[...] = jnp.zeros_like(r)

        _runs_load(dys_ref, dyp_ref, run)
        _runs_load(us_ref, up_ref, run)
        dyb = dyp_ref[...].astype(MXU)
        gr_ref[...] = _dot(dyb, cr_ref[0])
        gi_ref[...] = -_dot(dyb, ci_ref[0])
        ar = jnp.broadcast_to(ar_ref[...], (8, w))
        ai = jnp.broadcast_to(-ai_ref[...], (8, w))

        def advance(kk, state):
            r0 = pl.multiple_of((run - 1 - kk) * 8, 8)
            gr, gi = state
            return (ar * gr - ai * gi + gr_ref[pl.ds(r0, 8), :], ar * gi + ai * gr + gi_ref[pl.ds(r0, 8), :])

        def emit(kk, state):
            r0 = pl.multiple_of((run - 1 - kk) * 8, 8)
            gr, gi = advance(kk, state)
            gr_ref[pl.ds(r0, 8), :] = gr
            gi_ref[pl.ds(r0, 8), :] = gi
            return gr, gi

        zero = jnp.zeros((8, w), F32)
        er, ei = lax.fori_loop(0, run, advance, (zero, zero))
        cr, ci = car_ref[0:1, :], car_ref[1:2, :]
        tr, ti = _scan_group(er, ei, tab_ref, cr, ci, True)
        r8 = lax.broadcasted_iota(jnp.int32, (8, w), 0)
        start = (jnp.where(r8 == 7, cr, pltpu.roll(tr, 7, 0)), jnp.where(r8 == 7, ci, pltpu.roll(ti, 7, 0)))
        car_ref[0:1, :] = tr[0:1, :]
        car_ref[1:2, :] = ti[0:1, :]
        lax.fori_loop(0, run, emit, start)

        gsr = gr_ref[...]
        gsi = gi_ref[...]
        sr = str_ref[...]
        si = sti_ref[...]
        first = ri == 0

        def previous(s, halo_ref):
            head = jnp.where(r8 == 0, jnp.where(first, 0.0, halo_ref[7:8, :]), pltpu.roll(s[tm - 8:tm, :], 1, 0))
            return jnp.concatenate([head, s[0:tm - 8, :]], axis=0)

        spr = previous(sr, hr_ref)
        spi = previous(si, hi_ref)
        dab_ref[0, 0:1, :] += _rowsum(gsr * spr + gsi * spi)
        dab_ref[0, 1:2, :] += _rowsum(gsi * spr - gsr * spi)

        gbr = gsr.astype(MXU)
        gbi = gsi.astype(MXU)
        _runs_store(_dot_nt(gbr, br_ref[0]) + _dot_nt(gbi, bi_ref[0]), dun_ref, run)
        dys_v = dys_ref[...]
        dus_ref[...] = (dun_ref[...] + d_ref[...] * dys_v).astype(MXU)
        dd_ref[0, 0:1, :] += _rowsum(dys_v * us_ref[...])
        ub = up_ref[...].astype(MXU)
        dbr_ref[0] += _dot_tn(ub, gbr)
        dbi_ref[0] += _dot_tn(ub, gbi)
        dcr_ref[0] += _dot_tn(dyb, sr.astype(MXU))
        dci_ref[0] -= _dot_tn(dyb, si.astype(MXU))

    blk = lambda: pl.BlockSpec((1, 8 * SSM_H, w), lambda j, i: (j, 0, 0))
    rowl = lambda: pl.BlockSpec((tm, LANES), lambda j, i: (nt - 1 - i, j))
    roww = lambda: pl.BlockSpec((tm, w), lambda j, i: (nt - 1 - i, j))
    halo = lambda: pl.BlockSpec((8, w), lambda j, i: (jnp.maximum((nt - 1 - i) * hb - 1, 0), j))
    return pl.pallas_call(
        body, name="s5_bwd", grid=(SSM_BLK, nt),
        in_specs=[rowl(), rowl(), roww(), roww(), halo(), halo(),
                  pl.BlockSpec((1, w), lambda j, i: (0, j)), pl.BlockSpec((1, w), lambda j, i: (0, j)),
                  blk(), blk(), blk(), blk(),
                  pl.BlockSpec((1, LANES), lambda j, i: (0, j))],
        out_specs=[rowl(),
                   pl.BlockSpec((1, 8, w), lambda j, i: (j, 0, 0)), pl.BlockSpec((1, 8, LANES), lambda j, i: (j, 0, 0)),
                   blk(), blk(), blk(), blk()],
        out_shape=[_sds((S, SSM_W), MXU), _sds((SSM_BLK, 8, w)), _sds((SSM_BLK, 8, LANES)),
                   _sds((SSM_BLK, 8 * SSM_H, w)), _sds((SSM_BLK, 8 * SSM_H, w)),
                   _sds((SSM_BLK, 8 * SSM_H, w)), _sds((SSM_BLK, 8 * SSM_H, w))],
        scratch_shapes=[pltpu.VMEM((8, 8, w), F32), pltpu.VMEM((8, w), F32),
                        pltpu.VMEM((tm, w), F32), pltpu.VMEM((tm, w), F32),
                        pltpu.VMEM((tm, LANES), F32), pltpu.VMEM((tm, LANES), F32), pltpu.VMEM((tm, LANES), F32)],
        compiler_params=_cp("parallel", "arbitrary"),
    )(dys, us, st_re, st_im, st_re, st_im, abar_re, abar_im, b_re, b_im, c_re, c_im, d_skip)


def _in_bwd(dus, duv, dgl, dx1, x, g_mix, w_in, tm):
    S = x.shape[0]

    def body(dus_ref, duv_ref, dgl_ref, dx1_ref, x_ref, g_ref, w_ref, gx_ref, dg_ref):
        @pl.when(pl.program_id(0) == 0)
        def _():
            dg_ref[...] = jnp.zeros_like(dg_ref)

        dh = (_dot(dus_ref[...], w_ref[0:SSM_W, :])
              + _dot(duv_ref[...], w_ref[SSM_W:SSM_W + 2 * SGU_W, :])
              + _dot(dgl_ref[...], w_ref[SSM_W + 2 * SGU_W:, :]))
        xv = x_ref[...]
        r = _rms(xv)
        xn = xv * r
        dg_ref[...] += _rowsum(dh * xn)
        gx_ref[...] = dx1_ref[...] + _rms_bwd(dh * g_ref[...], xn, r)

    row = lambda n: pl.BlockSpec((tm, n), lambda i: (i, 0))
    return pl.pallas_call(
        body, name="in_bwd", grid=(S // tm,),
        in_specs=[row(SSM_W), row(2 * SGU_W), row(2 * D_MODEL), row(D_MODEL), row(D_MODEL), _full((1, D_MODEL)),
                  _full(w_in.shape)],
        out_specs=[row(D_MODEL), _full((1, D_MODEL))],
        out_shape=[_sds((S, D_MODEL)), _sds((1, D_MODEL))],
        compiler_params=_cp("arbitrary"),
    )(dus, duv, dgl, dx1, x, g_mix, w_in)


def _pick(n, cands):
    for c in cands:
        if n % c == 0:
            return c
    return n


def _wgrad_split(a, b, nsplit, tk, name):
    S, K = a.shape
    N = b.shape[1]
    c = N // nsplit

    def body(a_ref, b_ref, o_ref):
        prod = _dot_tn(a_ref[...], b_ref[...])
        for d in range(nsplit):
            o_ref[d] = prod[:, c * d:c * (d + 1)].astype(MXU)

    return pl.pallas_call(
        body, name=name, grid=(K // tk,),
        in_specs=[pl.BlockSpec((S, tk), lambda k: (0, k)), _full((S, N))],
        out_specs=pl.BlockSpec((nsplit, tk, c), lambda k: (0, k, 0)),
        out_shape=_sds((nsplit, K, c), MXU),
        compiler_params=_cp("parallel"),
    )(a, b)


def _wgrad_in_t(dps, h1, name):
    S, K = h1.shape
    cw = 512
    counts = [b.shape[1] // cw for b in dps]
    starts = [sum(counts[:i]) for i in range(len(dps))]
    nblk = sum(counts)

    def body(*refs):
        b_refs = refs[:len(dps)]
        h_ref, o_ref = refs[len(dps):]
        j = pl.program_id(0)
        for b_ref, st, cnt in zip(b_refs, starts, counts):
            @pl.when(jnp.logical_and(j >= st, j < st + cnt))
            def _():
                o_ref[...] = _dot_tn(b_ref[...], h_ref[...]).astype(MXU)

    def src_spec(st, cnt):
        return pl.BlockSpec((S, cw), lambda j: (0, jnp.clip(j - st, 0, cnt - 1)))

    return pl.pallas_call(
        body, name=name, grid=(nblk,),
        in_specs=[src_spec(st, cnt) for st, cnt in zip(starts, counts)] + [_full((S, K))],
        out_specs=pl.BlockSpec((cw, K), lambda j: (j, 0)),
        out_shape=_sds((nblk * cw, K), MXU),
        compiler_params=_cp("arbitrary"),
    )(*dps, h1)


def _wgrad_blk(a3, b3, nblk, a_of, b_of, name):
    S, K = a3.shape[1:]
    N = b3.shape[2]

    def body(a_ref, b_ref, o_ref):
        o_ref[0] = _dot_tn(a_ref[0], b_ref[0]).astype(MXU)

    return pl.pallas_call(
        body, name=name, grid=(nblk,),
        in_specs=[pl.BlockSpec((1, S, K), lambda b: (a_of(b), 0, 0)),
                  pl.BlockSpec((1, S, N), lambda b: (b_of(b), 0, 0))],
        out_specs=pl.BlockSpec((1, K, N), lambda b: (b, 0, 0)),
        out_shape=_sds((nblk, K, N), MXU),
        compiler_params=_cp("parallel"),
    )(a3, b3)


def _assemble_cols(blocks_list, name):
    def body(*refs):
        n = len(blocks_list)
        for b_ref, o_ref in zip(refs[:n], refs[n:]):
            c = b_ref.shape[2]
            for d in range(N_DEV):
                o_ref[:, c * d:c * (d + 1)] = b_ref[d]

    outs = [_sds((b.shape[1], N_DEV * b.shape[2]), b.dtype) for b in blocks_list]
    return pl.pallas_call(
        body, name=name, grid=(1,), in_specs=[_full(b.shape) for b in blocks_list],
        out_specs=[_full(o.shape) for o in outs], out_shape=outs, compiler_params=_cp("arbitrary"),
    )(*blocks_list)


def _tile(S, want):
    return want if S % want == 0 else S


def _local_step(x, tgt, p, mixer_weights, ffn_weights, grads_out):
    S = x.shape[0]
    tm = _tile(S, 256)
    tl = _tile(S, 512)

    rep = lambda a: jnp.repeat(a, SSM_H, axis=0)
    are = rep(p["a_re"])
    aim = rep(p["a_im"])
    ldt = jnp.broadcast_to(rep(p["log_dt"].reshape(SSM_G, 1)), are.shape)
    br_t = p["b_re_t"].reshape(are.shape)
    bi_t = p["b_im_t"].reshape(are.shape)
    abr, abi, bbr, bbi = _s5_params_fwd(are, aim, ldt, br_t, bi_t)
    head = lambda a: a.reshape(SSM_G, SSM_H, SSM_P)[:, 0, :].reshape(1, SSM_G * SSM_P)
    abar_re, abar_im = head(abr), head(abi)
    bd_br = _blockdiag(bbr).astype(MXU)
    bd_bi = _blockdiag(bbi).astype(MXU)
    bd_cr = _blockdiag(p["c_re"].reshape(are.shape)).astype(MXU)
    bd_ci = _blockdiag(p["c_im"].reshape(are.shape)).astype(MXU)
    d_skip = p["d_skip"].reshape(1, SSM_W)

    tril = jnp.tril(jnp.ones((CHUNK, CHUNK), dtype=bool))
    ws = jnp.where(tril[None], p["w_s"], 0.0)
    ws_b = ws.astype(MXU)
    ws_t = ws.transpose(0, 2, 1).astype(MXU)
    bias_s = jnp.repeat(p["b_s"].T, SGU_D, axis=1)

    g_mix = p["g_mix"].reshape(1, D_MODEL)
    g_ffn = p["g_ffn"].reshape(1, D_MODEL)
    g_final = p["g_final"].reshape(1, D_MODEL)
    g_sgu = p["g_sgu"].reshape(1, SGU_W)
    b_glu = p["b_glu"].reshape(1, SSM_W)
    conv_b = p["conv_b"].reshape(N_DEV, 1, FF_CW)

    h1, us, uv, gl = _in_fwd(x, g_mix, p["w_in_t"], tm)
    st_re, st_im, ys = _s5_fwd(us, abar_re, abar_im, bd_br, bd_bi, bd_cr, bd_ci, d_skip, tl)
    p = dict(p, **mixer_weights(ys))
    yg, yap, sg, ya, yb, m, x1, h2 = _mix_fwd(x, ys, uv, gl, p["w_glu"], b_glu, p["w_proj_a"], g_sgu, ws_b, bias_s,
                                              p["w_proj_b"], p["w_out"], g_ffn, tm)
    w_up, conv_w, w_down = ffn_weights(h2)
    up, ab, ff, dx2, dx2b, loss, dg_final = _ffn_fwd(h2, x1, tgt, w_up, conv_w, conv_b, w_down, g_final, tl)

    dup, dx1, dx1b, dconv, dg_ffn = _ffn_bwd(dx2, up, ab, x1, w_up, conv_w, w_down, g_ffn, tl)
    rows8 = lambda g: g.reshape(N_DEV, g.shape[1] // N_DEV, g.shape[2])
    g_up = _wgrad_blk(dup.reshape(N_DEV, S, FF_CW), h2[None], N_DEV, lambda b: b, lambda b: 0, "wgrad_up")
    g_down = _wgrad_blk(ff, dx2b[None], FF_NCB, lambda b: b, lambda b: 0, "wgrad_down").reshape(
        N_DEV, D_FF // N_DEV, D_MODEL)
    token = grads_out(("w_up", "w_down"), (g_up, g_down))
    dgl, dya, dyb, dz, dys, duv, db_glu, dg_sgu, dws, dbs = _mix_bwd(
        dx1, gl, ya, yb, ys, uv, p["w_out"], p["w_proj_a"], p["w_proj_b"], p["w_glu"], b_glu + token[0:1, 0:1], g_sgu,
        ws_b, ws_t, bias_s, tm)
    token = grads_out(("w_glu", "w_proj_a", "w_proj_b", "w_out"),
                      (rows8(_wgrad_split(yg, dz, 1, SSM_W, "wgrad_glu")),
                       _wgrad_split(yap, dya, N_DEV, SSM_W, "wgrad_pa"),
                       _wgrad_split(sg, dyb, N_DEV, SGU_W, "wgrad_pb"),
                       rows8(_wgrad_split(m, dx1b, 1, 512, "wgrad_out"))))
    dus, dab, dd, dbbr, dbbi, dcr, dci = _s5_bwd(dys, us, st_re, st_im, abar_re, abar_im, bd_br, bd_bi, bd_cr, bd_ci,
                                                 d_skip + token[0:1, 0:1], tl)
    g_in = _wgrad_in_t([dus, duv, dgl], h1, "wgrad_in")
    token = grads_out(("w_in",), (g_in.reshape(N_DEV, g_in.shape[0] // N_DEV, D_MODEL),))
    grad_x, dg_mix = _in_bwd(dus, duv, dgl, dx1, x, g_mix + token[0:1, 0:1], p["w_in_t"], tm)

    spread = lambda v: jnp.repeat(v.reshape(SSM_G, SSM_P), SSM_H, axis=0) * (1.0 / SSM_H)
    dabr = spread(dab[:, 0, :])
    dabi = spread(dab[:, 1, :])
    dare, daim, dldt, dbr_t, dbi_t = _s5_params_bwd(are, aim, ldt, br_t, bi_t, dabr, dabi,
                                                    _unblockdiag(dbbr), _unblockdiag(dbbi))
    fold = lambda a: a.reshape(SSM_G, SSM_H, SSM_P).sum(axis=1)

    grads = {
        "g_mix": dg_mix,
        "a_re": fold(dare), "a_im": fold(daim), "log_dt": fold(dldt).sum(axis=1),
        "b_re": dbr_t, "b_im": dbi_t,
        "c_re": _unblockdiag(dcr).reshape(SSM_G, SSM_H, SSM_P),
        "c_im": _unblockdiag(dci).reshape(SSM_G, SSM_H, SSM_P),
        "d_skip": dd[:, 0, :].reshape(SSM_W),
        "b_glu": db_glu,
        "g_sgu": dg_sgu,
        "w_s": dws,
        "b_s": dbs.reshape(CHUNK, SGU_G, SGU_D).sum(axis=-1).T,
        "g_ffn": dg_ffn,
        "conv_w": dconv[:, 0:3, :],
        "conv_b": dconv[:, 3, :].reshape(2 * D_FF),
        "g_final": dg_final,
    }
    return loss, grad_x, grads


_ANY = pl.BlockSpec(memory_space=pl.ANY)
_MESH = pl.DeviceIdType.MESH


def _allgather(shards, dtypes, name, cast_only=()):
    n = len(shards)
    e = len(cast_only)

    def body(*refs):
        in_refs, extra_in = refs[:n], refs[n:n + e]
        out_refs, extra_out = refs[n + e:2 * n + e], refs[2 * n + e:2 * n + 2 * e]
        stage = refs[2 * n + 2 * e:3 * n + 2 * e]
        send_sems, recv_sems, local_sems = refs[3 * n + 2 * e:]
        for a in range(n):
            stage[a][...] = in_refs[a][...].astype(dtypes[a])
        for i in range(e):
            extra_out[i][...] = extra_in[i][...].astype(MXU)
        x, y, c = lax.axis_index("x"), lax.axis_index("y"), lax.axis_index("c")
        me, sibling = (x, y, c), (x, y, 1 - c)
        chips = [(1 - x, y), (x, 1 - y), (1 - x, 1 - y)]

        def slot(a, px, py, pc):
            return out_refs[a].at[4 * px + 2 * py + pc]

        def copy(a, k, block, to, src=None):
            return pltpu.make_async_remote_copy(
                src_ref=slot(a, *block) if src is None else src, dst_ref=slot(a, *block),
                send_sem=send_sems.at[a, k], recv_sem=recv_sems.at[a, k], device_id=to, device_id_type=_MESH)

        mine = [pltpu.make_async_copy(stage[a], slot(a, *me), local_sems.at[a]) for a in range(n)]
        for cp in mine:
            cp.start()
        first = []
        for j, chip in enumerate(chips):
            first += [copy(a, 1 + j, me, (*chip, c), src=stage[a]) for a in range(n)]
        first += [copy(a, 0, me, sibling, src=stage[a]) for a in range(n)]
        for cp in first:
            cp.start()
        passed = []
        for j, chip in enumerate(chips):
            for a in range(n):
                copy(a, 1 + j, (*chip, c), me).wait_recv()
                fwd = copy(a, 4 + j, (*chip, c), sibling)
                fwd.start()
                passed.append(fwd)
        for a in range(n):
            copy(a, 0, sibling, me).wait_recv()
        for j, chip in enumerate(chips):
            for a in range(n):
                copy(a, 4 + j, (*chip, 1 - c), me).wait_recv()
        for cp in first + passed:
            cp.wait_send()
        for cp in mine:
            cp.wait()

    res = pl.pallas_call(
        body, name=name, grid=(1,), in_specs=[_full(s.shape) for s in list(shards) + list(cast_only)],
        out_specs=[_ANY] * n + [_full(s.shape) for s in cast_only],
        out_shape=[_sds((N_DEV,) + s.shape, dt) for s, dt in zip(shards, dtypes)]
                  + [_sds(s.shape, MXU) for s in cast_only],
        scratch_shapes=[pltpu.VMEM(s.shape, dt) for s, dt in zip(shards, dtypes)]
                       + [pltpu.SemaphoreType.DMA((n, 7)), pltpu.SemaphoreType.DMA((n, 7)), pltpu.SemaphoreType.DMA((n,))],
        compiler_params=pltpu.CompilerParams(vmem_limit_bytes=VMEM_LIMIT),
    )(*shards, *cast_only)
    return res[:n], res[n:]


def _all_to_all(sends, name):
    n = len(sends)

    def body(*refs):
        send_refs, recv_refs = refs[:n], refs[n:2 * n]
        send_sems, recv_sems, local_sems = refs[2 * n:]
        x, y, c = lax.axis_index("x"), lax.axis_index("y"), lax.axis_index("c")
        me = 4 * x + 2 * y + c
        mine = [pltpu.make_async_copy(send_refs[a].at[me], recv_refs[a].at[me], local_sems.at[a]) for a in range(n)]
        for cp in mine:
            cp.start()
        copies = []
        for k in (2, 4, 6, 3, 5, 7, 1):
            px = 1 - x if k & 4 else x
            py = 1 - y if k & 2 else y
            pc = 1 - c if k & 1 else c
            peer = 4 * px + 2 * py + pc
            for a in range(n):
                sems = dict(send_sem=send_sems.at[a, k - 1], recv_sem=recv_sems.at[a, k - 1],
                            device_id=(px, py, pc), device_id_type=_MESH)
                cp = pltpu.make_async_remote_copy(src_ref=send_refs[a].at[peer], dst_ref=recv_refs[a].at[me], **sems)
                cp.start()
                landing = pltpu.make_async_remote_copy(src_ref=send_refs[a].at[peer], dst_ref=recv_refs[a].at[peer],
                                                       **sems)
                copies.append((cp, landing))
        for _, landing in copies:
            landing.wait_recv()
        for cp, _ in copies:
            cp.wait_send()
        for cp in mine:
            cp.wait()

    return pl.pallas_call(
        body, name=name, in_specs=[_ANY] * n, out_specs=[_ANY] * n,
        out_shape=[_sds(s.shape, s.dtype) for s in sends],
        scratch_shapes=[pltpu.SemaphoreType.DMA((n, 7)), pltpu.SemaphoreType.DMA((n, 7)), pltpu.SemaphoreType.DMA((n,))],
    )(*sends)


_HBM = pl.BlockSpec(memory_space=pltpu.HBM)
_SEM = pl.BlockSpec(memory_space=pltpu.SEMAPHORE)
_EFFECT = pltpu.SideEffectType.DATAFLOW_SIDE_EFFECTING
_PEER_ORDER = (2, 4, 6, 3, 5, 7, 1)


def _peer(k):
    x, y, c = lax.axis_index("x"), lax.axis_index("y"), lax.axis_index("c")
    px = 1 - x if k & 4 else x
    py = 1 - y if k & 2 else y
    pc = 1 - c if k & 1 else c
    return (px, py, pc), 4 * px + 2 * py + pc


def _push_start(srcs, lands, slotted, name):
    n = len(srcs)

    def body(*refs):
        src_refs, land_refs = refs[:n], refs[n:2 * n]
        send_sems, recv_sems, token = refs[2 * n], refs[2 * n + 1], refs[-1]
        me = 4 * lax.axis_index("x") + 2 * lax.axis_index("y") + lax.axis_index("c")
        for k in _PEER_ORDER:
            dev, peer = _peer(k)
            for a in range(n):
                pltpu.make_async_remote_copy(
                    src_ref=src_refs[a].at[peer] if slotted else src_refs[a], dst_ref=land_refs[a].at[me],
                    send_sem=send_sems.at[7 * a + k - 1], recv_sem=recv_sems.at[7 * a + k - 1],
                    device_id=dev, device_id_type=_MESH).start()
        token[...] = jnp.zeros_like(token)

    bufs = list(srcs) + list(lands)
    res = pl.pallas_call(
        body, name=name, in_specs=[_HBM] * (2 * n),
        out_specs=(_SEM, _SEM, *[_HBM] * (2 * n), pl.BlockSpec(memory_space=pltpu.VMEM)),
        out_shape=(pltpu.SemaphoreType.DMA((7 * n,)), pltpu.SemaphoreType.DMA((7 * n,)),
                   *[pltpu.HBM(b.shape, b.dtype) for b in bufs], _sds((8, LANES))),
        input_output_aliases={i: 2 + i for i in range(2 * n)},
        compiler_params=pltpu.CompilerParams(has_side_effects=_EFFECT),
    )(*[pltpu.with_memory_space_constraint(b, pltpu.HBM) for b in bufs])
    return res[0], res[1], res[2:2 + n], res[2 + n:2 + 2 * n], res[-1]


def _push_wait(send_sems, recv_sems, srcs, lands, slotted, after, name):
    n = len(srcs)

    def body(*refs):
        src_refs, land_refs = refs[:n], refs[n:2 * n]
        send_sems, recv_sems = refs[2 * n], refs[2 * n + 1]
        for k in _PEER_ORDER:
            dev, peer = _peer(k)
            for a in range(n):
                cp = pltpu.make_async_remote_copy(
                    src_ref=src_refs[a].at[peer] if slotted else src_refs[a], dst_ref=land_refs[a].at[peer],
                    send_sem=send_sems.at[7 * a + k - 1], recv_sem=recv_sems.at[7 * a + k - 1],
                    device_id=dev, device_id_type=_MESH)
                cp.wait_send()
                cp.wait_recv()

    bufs = list(srcs) + list(lands)
    res = pl.pallas_call(
        body, name=name, in_specs=[_HBM] * (2 * n) + [_SEM, _SEM] + [_ANY] * len(after), out_specs=[_HBM] * (2 * n),
        out_shape=[pltpu.HBM(b.shape, b.dtype) for b in bufs],
        input_output_aliases={i: i for i in range(2 * n)},
        compiler_params=pltpu.CompilerParams(has_side_effects=_EFFECT),
    )(*bufs, send_sems, recv_sems, *after)
    return res[n:]


def _adamw(w, g, m, v):
    m2 = ADAM_B1 * m + (1.0 - ADAM_B1) * g
    v2 = ADAM_B2 * v + (1.0 - ADAM_B2) * (g * g)
    m_hat = m2 / (1.0 - ADAM_B1 ** ADAM_STEP)
    v_hat = v2 / (1.0 - ADAM_B2 ** ADAM_STEP)
    delta = -ADAM_LR * (m_hat / (jnp.sqrt(v_hat) + ADAM_EPS) + ADAM_WD * w)
    return delta, m2, v2


def _adam_shard(parts, w, m, v, name):
    _, r, c = w.shape
    tr = max(t for t in range(16, 257, 16) if r % t == 0)

    def body(p_ref, w_ref, m_ref, v_ref, g_ref, d_ref, m2_ref, v2_ref):
        g = p_ref[0].astype(F32)
        for s in range(1, N_DEV):
            g = g + p_ref[s].astype(F32)
        g_ref[0] = g
        d_ref[0], m2_ref[0], v2_ref[0] = _adamw(w_ref[0], g, m_ref[0], v_ref[0])

    row = lambda: pl.BlockSpec((1, tr, c), lambda i: (0, i, 0))
    return pl.pallas_call(
        body, name=name, grid=(r // tr,),
        in_specs=[pl.BlockSpec((N_DEV, tr, c), lambda i: (0, i, 0)), row(), row(), row()],
        out_specs=[row(), row(), row(), row()], out_shape=[_sds((1, r, c))] * 4,
        compiler_params=_cp("parallel"),
    )(parts, w, m, v)


def _adam_small(gs, ws, ms, vs, name):
    n = len(gs)

    def body(*refs):
        ins, outs = refs[:4 * n], refs[4 * n:]
        for i in range(n):
            g = ins[i][...]
            d, m2, v2 = _adamw(ins[n + i][...], g, ins[2 * n + i][...], ins[3 * n + i][...])
            outs[i][...] = d
            outs[n + i][...] = m2
            outs[2 * n + i][...] = v2

    res = pl.pallas_call(
        body, name=name, grid=(1,), in_specs=[_full(w.shape) for w in ws] * 4,
        out_specs=[_full(w.shape) for w in ws] * 3, out_shape=[_sds(w.shape) for w in ws] * 3,
        compiler_params=_cp("arbitrary"),
    )(*gs, *ws, *ms, *vs)
    return res[:n], res[n:2 * n], res[2 * n:]


def _sum_slots(parts, name):
    R = parts.shape[1]

    def body(p_ref, o_ref):
        g = p_ref[0]
        for s in range(1, N_DEV):
            g = g + p_ref[s]
        o_ref[...] = g

    return pl.pallas_call(body, name=name, grid=(1,), in_specs=[_full(parts.shape)], out_specs=_full((R, LANES)),
                          out_shape=_sds((R, LANES)))(parts)


def _pad_to(a, n, axis):
    extra = n - a.shape[axis]
    if extra == 0:
        return a
    widths = [(0, 0)] * a.ndim
    widths[axis] = (0, extra)
    return jnp.pad(a, widths)


def _ceil_to(n, k):
    return -(-n // k) * k


def _pack_rows(flats, rows_multiple):
    parts = [_pad_to(f, _ceil_to(f.shape[-1], LANES), f.ndim - 1) for f in flats]
    cat = jnp.concatenate(parts, axis=-1)
    total = _ceil_to(cat.shape[-1], LANES * rows_multiple)
    cat = _pad_to(cat, total, cat.ndim - 1)
    return cat.reshape(cat.shape[:-1] + (total // LANES, LANES))


def _unpack_rows(buf, sizes):
    flat = buf.reshape(buf.shape[:-2] + (-1,))
    out, off = [], 0
    for n in sizes:
        out.append(flat[..., off:off + n])
        off += _ceil_to(n, LANES)
    return out


_MIX_BIG = ("w_in", "w_glu", "w_proj_a", "w_proj_b", "w_out")
_BIG = _MIX_BIG + ("w_up", "w_down")
_SMALL = ("g_mix", "a_re", "a_im", "log_dt", "b_re", "b_im", "c_re", "c_im", "d_skip", "b_glu", "g_sgu", "w_s", "b_s",
          "g_ffn", "conv_b", "g_final")
_SMALL_ROWS_MULTIPLE = 8 * N_DEV
_TRANSPOSED = ("w_in", "w_up", "b_re", "b_im")


def _as_2d(a):
    return a.reshape(-1, a.shape[-1]) if a.ndim > 1 else a.reshape(1, -1)


def kernel(x, g_mix, w_in, a_re, a_im, log_dt, b_re, b_im, c_re, c_im, d_skip, w_glu, b_glu, w_proj_a, g_sgu, w_s, b_s, w_proj_b, w_out, g_ffn, w_up, conv_w, conv_b, w_down, g_final, loss_target, m_g_mix, m_w_in, m_a_re, m_a_im, m_log_dt, m_b_re, m_b_im, m_c_re, m_c_im, m_d_skip, m_w_glu, m_b_glu, m_w_proj_a, m_g_sgu, m_w_s, m_b_s, m_w_proj_b, m_w_out, m_g_ffn, m_w_up, m_conv_w, m_conv_b, m_w_down, m_g_final, v_g_mix, v_w_in, v_a_re, v_a_im, v_log_dt, v_b_re, v_b_im, v_c_re, v_c_im, v_d_skip, v_w_glu, v_b_glu, v_w_proj_a, v_g_sgu, v_w_s, v_b_s, v_w_proj_b, v_w_out, v_g_ffn, v_w_up, v_conv_w, v_conv_b, v_w_down, v_g_final):
    args = dict(locals())
    me = 4 * lax.axis_index("x") + 2 * lax.axis_index("y") + lax.axis_index("c")

    def own_slot(buf, block):
        return lax.dynamic_update_slice(buf, block[None], (me,) + (0,) * block.ndim)

    for n in _TRANSPOSED:
        for pre in ("", "m_", "v_"):
            args[pre + n] = jnp.swapaxes(args[pre + n], -1, -2)
    later = ("w_glu", "w_proj_a", "w_proj_b", "w_out", "w_up", "w_down")
    (w_in_g,), casts = _allgather([args["w_in"][0]], [MXU], "allgather_w_in", cast_only=[args[n][0] for n in later])
    sh = dict(zip(later, casts))

    def start_push(names, srcs, tag):
        lands = [own_slot(lax.empty((N_DEV,) + s.shape, s.dtype), s) for s in srcs]
        send_sems, recv_sems, srcs, lands, token = _push_start(srcs, lands, False, "push_" + tag)
        return (send_sems, recv_sems, srcs, lands), token

    mix_push, token_a = start_push(later[:4], [sh[n] for n in later[:4]], "mixer_weights")
    ffn_push, token_b = start_push(("w_up", "w_down", "conv_w"), [sh["w_up"], sh["w_down"], conv_w[0]], "ffn_weights")
    p = {n: (args[n][0] if n != "g_final" else args[n]) for n in _SMALL if n not in _TRANSPOSED}
    p.update(w_in_t=w_in_g.reshape(SSM_W + 2 * SGU_W + 2 * D_MODEL, D_MODEL),
             b_re_t=args["b_re"][0], b_im_t=args["b_im"][0])
    p["g_mix"] = p["g_mix"] + (token_a[0:1, 0:1] + token_b[0:1, 0:1])

    def mixer_weights(after):
        w_glu_g, w_pa_g, w_pb_g, w_out_g = _push_wait(*mix_push, False, [after], "wait_mixer_weights")
        w_pa_full, w_pb_full = _assemble_cols([w_pa_g, w_pb_g], "assemble_cols")
        return dict(w_glu=w_glu_g.reshape(SSM_W, SSM_W), w_proj_a=w_pa_full, w_proj_b=w_pb_full,
                    w_out=w_out_g.reshape(D_MODEL, D_MODEL))

    def ffn_weights(after):
        w_up_g, w_down_g, conv_w_g = _push_wait(*ffn_push, False, [after], "wait_ffn_weights")
        return w_up_g, conv_w_g, w_down_g.reshape(D_FF, D_MODEL)

    pushes = []

    def grads_out(names, sends):
        lands = [own_slot(lax.empty(s.shape, s.dtype), lax.dynamic_index_in_dim(s, me, 0, keepdims=False))
                 for s in sends]
        send_sems, recv_sems, srcs, lands, token = _push_start(list(sends), lands, True, "push_grads_" + names[0])
        pushes.append((names, send_sems, recv_sems, srcs, lands))
        return token

    loss_part, grad_x, grads = _local_step(x[0], loss_target[0], p, mixer_weights, ffn_weights, grads_out)

    small_names = _SMALL + ("conv_w", "loss")
    small_g = dict(grads, loss=loss_part[0, 0:1])
    flats = [small_g[n].reshape(-1) for n in small_names]
    small_sizes = [f.shape[0] for f in flats]
    g_small = _pack_rows(flats, _SMALL_ROWS_MULTIPLE)
    rs8 = g_small.shape[0] // N_DEV
    grads_out(("small",), (g_small.reshape(N_DEV, rs8, LANES),))

    out = {}
    done = [g_small]
    for names, send_sems, recv_sems, srcs, lands in pushes:
        parts = _push_wait(send_sems, recv_sems, srcs, lands, True, done, "wait_grads_" + names[0])
        if names == ("small",):
            recv_small, = parts
            break
        for n, part in zip(names, parts):
            res = _adam_shard(part, args[n], args["m_" + n], args["v_" + n], "adam_" + n)
            for kind, v in zip(("grad_", "delta_", "new_m_", "new_v_"), res):
                out[kind + n] = v
            done = [res[0]]
    small_mine = _sum_slots(recv_small, "sum_small")
    g_small_all = _allgather([small_mine], [F32], "allgather_small")[0][0].reshape(N_DEV * rs8, LANES)
    pieces = dict(zip(small_names, _unpack_rows(g_small_all, small_sizes)))
    loss = pieces["loss"][0]
    dconv_w = lax.dynamic_index_in_dim(pieces["conv_w"].reshape(N_DEV, 3, FF_CW), me, axis=0, keepdims=False)
    names2 = _SMALL + ("conv_w",)
    gs = [pieces[n].reshape(_as_2d(args[n]).shape) for n in _SMALL] + [dconv_w]
    ds, m2s, v2s = _adam_small(gs, [_as_2d(args[n]) for n in names2], [_as_2d(args["m_" + n]) for n in names2],
                               [_as_2d(args["v_" + n]) for n in names2], "adam_small")
    for n, res in zip(names2, zip(gs, ds, m2s, v2s)):
        for kind, v in zip(("grad_", "delta_", "new_m_", "new_v_"), res):
            out[kind + n] = v.reshape(args[n].shape)
    order = ("g_mix", "w_in", "a_re", "a_im", "log_dt", "b_re", "b_im", "c_re", "c_im", "d_skip", "w_glu", "b_glu",
             "w_proj_a", "g_sgu", "w_s", "b_s", "w_proj_b", "w_out", "g_ffn", "w_up", "conv_w", "conv_b", "w_down",
             "g_final")
    res = [loss, grad_x.reshape(x.shape)]
    for kind in ("grad_", "delta_", "new_m_", "new_v_"):
        res += [jnp.swapaxes(out[kind + n], -1, -2) if n in _TRANSPOSED else out[kind + n] for n in order]
    return tuple(res)
```

```python
import functools
import math

import jax
import jax.numpy as jnp
from jax import lax
from jax.experimental import pallas as pl
from jax.experimental.pallas import tpu as pltpu

F32 = jnp.float32
MXU = jnp.bfloat16
EPS = 1e-6

D_MODEL = 1024
SSM_W = 512
SSM_G, SSM_H, SSM_P = 32, 16, 64
SSM_BLK = 4
SGU_W = 512
SGU_G, SGU_D, CHUNK = 8, 64, 128
D_FF = 2816
N_DEV = 8
FF_CW = 2 * D_FF // N_DEV
FF_NCB = D_FF // FF_CW
LANES = 128

ADAM_LR, ADAM_B1, ADAM_B2, ADAM_EPS, ADAM_WD, ADAM_STEP = 0.001, 0.9, 0.999, 1e-08, 0.01, 10

VMEM_LIMIT = 48 * 1024 * 1024


def _cp(*sem):
    return pltpu.CompilerParams(dimension_semantics=sem, vmem_limit_bytes=VMEM_LIMIT)


def _full(shape):
    n = len(shape)
    return pl.BlockSpec(shape, lambda *_: (0,) * n)


def _sds(shape, dtype=F32):
    return jax.ShapeDtypeStruct(shape, dtype)


def _in_hbm(arrays):
    return [pltpu.with_memory_space_constraint(a, pltpu.HBM) for a in arrays]


def _dot(a, b):
    return jnp.dot(a, b, preferred_element_type=F32)


def _dot_nt(a, b):
    return lax.dot_general(a, b, (((1,), (1,)), ((), ())), preferred_element_type=F32)


def _dot_tn(a, b):
    return lax.dot_general(a, b, (((0,), (0,)), ((), ())), preferred_element_type=F32)


_GELU_C = math.sqrt(2.0 / math.pi)


def _gelu(x):
    return 0.5 * x * (1.0 + jnp.tanh(_GELU_C * (x + 0.044715 * (x * x * x))))


def _gelu_and_grad(x):
    t = jnp.tanh(_GELU_C * (x + 0.044715 * (x * x * x)))
    g = 0.5 * x * (1.0 + t)
    dg = 0.5 * (1.0 + t) + 0.5 * x * (1.0 - t * t) * (_GELU_C * (1.0 + 3.0 * 0.044715 * (x * x)))
    return g, dg


def _sigmoid(x):
    return 0.5 * jnp.tanh(0.5 * x) + 0.5


def _rms(x):
    return lax.rsqrt(jnp.mean(x * x, axis=-1, keepdims=True) + EPS)


def _rms_bwd(dxn, xn, r):
    return r * (dxn - xn * jnp.mean(dxn * xn, axis=-1, keepdims=True))


def _rowsum(x):
    return jnp.sum(x, axis=0, keepdims=True)


def _s5_disc(are, aim, ldt, br, bi):
    dt = jnp.exp(ldt)
    mag = jnp.exp(dt * are)
    abr = mag * jnp.cos(dt * aim)
    abi = mag * jnp.sin(dt * aim)
    den = are * are + aim * aim
    nr = abr - 1.0
    ni = abi
    fr = (nr * are + ni * aim) / den
    fi = (ni * are - nr * aim) / den
    return abr, abi, fr * br - fi * bi, fr * bi + fi * br


def _s5_params_fwd(are, aim, ldt, br, bi):
    def body(are_ref, aim_ref, ldt_ref, br_ref, bi_ref, o0, o1, o2, o3):
        outs = _s5_disc(are_ref[...], aim_ref[...], ldt_ref[...], br_ref[...], bi_ref[...])
        for o, v in zip((o0, o1, o2, o3), outs):
            o[...] = v
    shp = are.shape
    return pl.pallas_call(body, name="s5_params_fwd", grid=(1,), in_specs=[_full(shp)] * 5, out_specs=[_full(shp)] * 4,
                          out_shape=[_sds(shp)] * 4)(*_in_hbm([are, aim, ldt, br, bi]))


def _s5_params_bwd(are, aim, ldt, br, bi, dabr, dabi, dbr, dbi):
    def body(are_ref, aim_ref, ldt_ref, br_ref, bi_ref, c0, c1, c2, c3, o0, o1, o2, o3, o4):
        prim = (are_ref[...], aim_ref[...], ldt_ref[...], br_ref[...], bi_ref[...])
        _, vjp = jax.vjp(_s5_disc, *prim)
        outs = vjp((c0[...], c1[...], c2[...], c3[...]))
        for o, v in zip((o0, o1, o2, o3, o4), outs):
            o[...] = v
    shp = are.shape
    return pl.pallas_call(body, name="s5_params_bwd", grid=(1,), in_specs=[_full(shp)] * 9, out_specs=[_full(shp)] * 5,
                          out_shape=[_sds(shp)] * 5)(*_in_hbm([are, aim, ldt, br, bi, dabr, dabi, dbr, dbi]))


def _blockdiag(m_t):
    m = m_t.reshape(SSM_BLK, 8, SSM_H, 1, SSM_P)
    eye = jnp.eye(8, dtype=bool).reshape(1, 8, 1, 8, 1)
    return jnp.where(eye, m, jnp.zeros((), m_t.dtype)).reshape(SSM_BLK, 8 * SSM_H, 8 * SSM_P)


def _unblockdiag(pc):
    m = pc.reshape(SSM_BLK, 8, SSM_H, 8, SSM_P)
    return jnp.einsum("jghgp->jghp", m).reshape(SSM_G * SSM_H, SSM_P)


def _in_fwd(x, g_mix, w_in_t, tm):
    S = x.shape[0]

    def body(x_ref, g_ref, w_ref, h_ref, us_ref, uv_ref, gl_ref):
        xv = x_ref[...]
        h = (xv * _rms(xv) * g_ref[...]).astype(MXU)
        h_ref[...] = h
        us_ref[...] = _dot_nt(h, w_ref[0:SSM_W, :])
        uv_ref[...] = _dot_nt(h, w_ref[SSM_W:SSM_W + 2 * SGU_W, :])
        gl_ref[...] = _dot_nt(h, w_ref[SSM_W + 2 * SGU_W:, :])

    row = lambda n: pl.BlockSpec((tm, n), lambda i: (i, 0))
    return pl.pallas_call(
        body, name="in_fwd", grid=(S // tm,),
        in_specs=[row(D_MODEL), _full((1, D_MODEL)), _full(w_in_t.shape)],
        out_specs=[row(D_MODEL), row(SSM_W), row(2 * SGU_W), row(2 * D_MODEL)],
        out_shape=[_sds((S, D_MODEL), MXU), _sds((S, SSM_W)), _sds((S, 2 * SGU_W)), _sds((S, 2 * D_MODEL))],
        compiler_params=_cp("parallel"),
    )(*_in_hbm([x, g_mix, w_in_t]))


def _scan_tables(ar, ai, reverse):
    n = ar.shape[-1]
    def mul(p, q):
        return p[0] * q[0] - p[1] * q[1], p[0] * q[1] + p[1] * q[0]
    a1 = (ar, ai)
    a2 = mul(a1, a1)
    a3 = mul(a2, a1)
    a4 = mul(a2, a2)
    a5 = mul(a4, a1)
    a6 = mul(a4, a2)
    a7 = mul(a4, a3)
    a8 = mul(a4, a4)
    pw = (a1, a2, a3, a4, a5, a6, a7, a8)
    rows = lax.broadcasted_iota(jnp.int32, (8, n), 0)
    tabs = []
    for s, a in ((1, a1), (2, a2), (4, a4)):
        keep = (rows + s <= 7) if reverse else (rows >= s)
        for comp in a:
            tabs.append(jnp.where(keep, jnp.broadcast_to(comp, (8, n)), 0.0))
    for c in range(2):
        q = jnp.zeros((8, n), F32)
        for r in range(8):
            e = (8 - r) if reverse else (r + 1)
            q = jnp.where(rows == r, jnp.broadcast_to(pw[e - 1][c], (8, n)), q)
        tabs.append(q)
    return tabs


def _scan_group(xr, xi, tab_ref, cr, ci, reverse):
    for t, s in enumerate((1, 2, 4)):
        pr = tab_ref[2 * t]
        pi = tab_ref[2 * t + 1]
        sh = (8 - s) if reverse else s
        sr = pltpu.roll(xr, sh, 0)
        si = pltpu.roll(xi, sh, 0)
        xr, xi = xr + pr * sr - pi * si, xi + pr * si + pi * sr
    qr = tab_ref[6]
    qi = tab_ref[7]
    return xr + qr * cr - qi * ci, xi + qr * ci + qi * cr


def _runs_load(src_ref, dst_ref, run):
    for i in range(run):
        dst_ref[8 * i:8 * i + 8, :] = src_ref[pl.ds(i, 8, stride=run), :]


def _runs_store(val, dst_ref, run):
    for i in range(run):
        dst_ref[pl.ds(i, 8, stride=run), :] = val[8 * i:8 * i + 8, :]


def _cpow2(ar, ai, log2n):
    for _ in range(log2n):
        ar, ai = ar * ar - ai * ai, 2.0 * ar * ai
    return ar, ai


def _s5_fwd(us, abar_re, abar_im, b_re, b_im, c_re, c_im, d_skip, tm):
    S = us.shape[0]
    nt = S // tm
    w = 8 * SSM_P
    run = tm // 8
    assert run & (run - 1) == 0

    def body(us_ref, ar_ref, ai_ref, br_ref, bi_ref, cr_ref, ci_ref, d_ref, str_ref, sti_ref, ys_ref,
             tab_ref, car_ref, up_ref):
        i = pl.program_id(1)

        @pl.when(i == 0)
        def _():
            car_ref[...] = jnp.zeros_like(car_ref)
            for k, t in enumerate(_scan_tables(*_cpow2(ar_ref[...], ai_ref[...], run.bit_length() - 1), False)):
                tab_ref[k] = t

        _runs_load(us_ref, up_ref, run)
        ub = up_ref[...].astype(MXU)
        str_ref[...] = _dot(ub, br_ref[0])
        sti_ref[...] = _dot(ub, bi_ref[0])
        ar = jnp.broadcast_to(ar_ref[...], (8, w))
        ai = jnp.broadcast_to(ai_ref[...], (8, w))

        def advance(k, state):
            r0 = pl.multiple_of(k * 8, 8)
            sr, si = state
            return (ar * sr - ai * si + str_ref[pl.ds(r0, 8), :], ar * si + ai * sr + sti_ref[pl.ds(r0, 8), :])

        def emit(k, state):
            r0 = pl.multiple_of(k * 8, 8)
            sr, si = advance(k, state)
            str_ref[pl.ds(r0, 8), :] = sr
            sti_ref[pl.ds(r0, 8), :] = si
            return sr, si

        zero = jnp.zeros((8, w), F32)
        er, ei = lax.fori_loop(0, run, advance, (zero, zero))
        cr, ci = car_ref[0:1, :], car_ref[1:2, :]
        tr, ti = _scan_group(er, ei, tab_ref, cr, ci, False)
        r8 = lax.broadcasted_iota(jnp.int32, (8, w), 0)
        start = (jnp.where(r8 == 0, cr, pltpu.roll(tr, 1, 0)), jnp.where(r8 == 0, ci, pltpu.roll(ti, 1, 0)))
        car_ref[0:1, :] = tr[7:8, :]
        car_ref[1:2, :] = ti[7:8, :]
        lax.fori_loop(0, run, emit, start)
        y = _dot_nt(str_ref[...].astype(MXU), cr_ref[0]) - _dot_nt(sti_ref[...].astype(MXU), ci_ref[0])
        _runs_store(y, ys_ref, run)
        ys_ref[...] += d_ref[...] * us_ref[...]

    blk = lambda: pl.BlockSpec((1, 8 * SSM_H, w), lambda j, i: (j, 0, 0))
    return pl.pallas_call(
        body, name="s5_fwd", grid=(SSM_BLK, nt),
        in_specs=[pl.BlockSpec((tm, LANES), lambda j, i: (i, j)),
                  pl.BlockSpec((1, w), lambda j, i: (0, j)), pl.BlockSpec((1, w), lambda j, i: (0, j)),
                  blk(), blk(), blk(), blk(),
                  pl.BlockSpec((1, LANES), lambda j, i: (0, j))],
        out_specs=[pl.BlockSpec((tm, w), lambda j, i: (i, j)), pl.BlockSpec((tm, w), lambda j, i: (i, j)),
                   pl.BlockSpec((tm, LANES), lambda j, i: (i, j))],
        out_shape=[_sds((S, SSM_BLK * w)), _sds((S, SSM_BLK * w)), _sds((S, SSM_W))],
        scratch_shapes=[pltpu.VMEM((8, 8, w), F32), pltpu.VMEM((8, w), F32), pltpu.VMEM((tm, LANES), F32)],
        compiler_params=_cp("parallel", "arbitrary"),
    )(*_in_hbm([us, abar_re, abar_im, b_re, b_im, c_re, c_im, d_skip]))


def _sgu_mix(vnb, ws_ref, grp):
    acc = jnp.zeros(vnb.shape, F32)
    for g in range(SGU_G):
        acc = jnp.where(grp == g, _dot(ws_ref[g], vnb), acc)
    return acc


def _mix_fwd(x, ys, uv, gl, w_glu, b_glu, w_pa, g_sgu, ws, bias_s, w_pb, w_out, g_ffn, tm):
    S = x.shape[0]

    def body(x_ref, ys_ref, uv_ref, gl_ref, wglu_ref, bglu_ref, wpa_ref, gs_ref, ws_ref, bias_ref, wpb_ref, wout_ref,
             gf_ref, yg_ref, yap_ref, sg_ref, ya_ref, yb_ref, m_ref, x1_ref, h2_ref):
        yg = _gelu(ys_ref[...])
        ygb = yg.astype(MXU)
        yg_ref[...] = ygb
        z = _dot(ygb, wglu_ref[...]) + bglu_ref[...]
        yapb = (yg * _sigmoid(z)).astype(MXU)
        yap_ref[...] = yapb
        ya = _dot(yapb, wpa_ref[...])
        ya_ref[...] = ya

        uvg = _gelu(uv_ref[...])
        u2 = uvg[:, :SGU_W]
        v2 = uvg[:, SGU_W:]
        vnb = (v2 * _rms(v2) * gs_ref[...]).astype(MXU)
        grp = lax.broadcasted_iota(jnp.int32, (CHUNK, SGU_W), 1) // SGU_D
        for c in range(tm // CHUNK):
            rs = slice(c * CHUNK, (c + 1) * CHUNK)
            mixed = _sgu_mix(vnb[rs], ws_ref, grp) + bias_ref[...]
            sg_ref[rs, :] = (u2[rs] * mixed).astype(MXU)
        yb = _dot(sg_ref[...], wpb_ref[...])
        yb_ref[...] = yb

        glv = gl_ref[...]
        m = _sigmoid(glv[:, :D_MODEL]) * ya + _sigmoid(glv[:, D_MODEL:]) * yb
        mb = m.astype(MXU)
        m_ref[...] = mb
        x1 = x_ref[...] + _dot(mb, wout_ref[...])
        x1_ref[...] = x1
        h2_ref[...] = (x1 * _rms(x1) * gf_ref[...]).astype(MXU)

    row = lambda n: pl.BlockSpec((tm, n), lambda i: (i, 0))
    return pl.pallas_call(
        body, name="mix_fwd", grid=(S // tm,),
        in_specs=[row(D_MODEL), row(SSM_W), row(2 * SGU_W), row(2 * D_MODEL),
                  _full(w_glu.shape), _full(b_glu.shape), _full(w_pa.shape), _full(g_sgu.shape), _full(ws.shape),
                  _full(bias_s.shape), _full(w_pb.shape), _full(w_out.shape), _full(g_ffn.shape)],
        out_specs=[row(SSM_W), row(SSM_W), row(SGU_W), row(D_MODEL), row(D_MODEL), row(D_MODEL), row(D_MODEL),
                   row(D_MODEL)],
        out_shape=[_sds((S, SSM_W), MXU), _sds((S, SSM_W), MXU), _sds((S, SGU_W), MXU), _sds((S, D_MODEL)),
                   _sds((S, D_MODEL)), _sds((S, D_MODEL), MXU), _sds((S, D_MODEL)), _sds((S, D_MODEL), MXU)],
        compiler_params=_cp("parallel"),
    )(*_in_hbm([x, ys, uv, gl, w_glu, b_glu, w_pa, g_sgu, ws, bias_s, w_pb, w_out, g_ffn]))


def _causal_conv3(u, prev8, cw, cb):
    tm = u.shape[0]
    w0, w1, w2 = cw[0:1], cw[1:2], cw[2:3]
    body = w0 * pltpu.roll(u, 2, 0) + w1 * pltpu.roll(u, 1, 0) + w2 * u + cb
    u8 = u[0:8, :]
    r8 = lax.broadcasted_iota(jnp.int32, u8.shape, 0)
    t1 = prev8[7:8, :]
    t0 = prev8[6:7, :]
    s1 = jnp.where(r8 == 0, t1, pltpu.roll(u8, 1, 0))
    s2 = jnp.where(r8 == 0, t0, jnp.where(r8 == 1, t1, pltpu.roll(u8, 2, 0)))
    first = w0 * s2 + w1 * s1 + w2 * u8 + cb
    return jnp.concatenate([first, body[8:tm, :]], axis=0)


def _causal_conv3_adjoint(d, next8, cw):
    tm = d.shape[0]
    w0, w1, w2 = cw[0:1], cw[1:2], cw[2:3]
    n1 = pltpu.roll(d, tm - 1, 0)
    n2 = pltpu.roll(d, tm - 2, 0)
    body = w2 * d + w1 * n1 + w0 * n2
    d8 = d[tm - 8:tm, :]
    r8 = lax.broadcasted_iota(jnp.int32, d8.shape, 0)
    h0 = next8[0:1, :]
    h1 = next8[1:2, :]
    m1 = jnp.where(r8 == 7, h0, pltpu.roll(d8, 7, 0))
    m2 = jnp.where(r8 == 6, h0, jnp.where(r8 == 7, h1, pltpu.roll(d8, 6, 0)))
    last = w2 * d8 + w1 * m1 + w0 * m2
    out = jnp.concatenate([body[0:tm - 8, :], last], axis=0)
    return out, n1, n2, h0 - d[0:1, :], h1 - d[1:2, :]


def _ffn_fwd(h2, x1, tgt, w_up, conv_w, conv_b, w_down, g_final, tm):
    S = h2.shape[0]
    nt = S // tm
    ncb = FF_NCB

    def body(h2_ref, wa_ref, wb_ref, cwa_ref, cwb_ref, cba_ref, cbb_ref, wd_ref, x1_ref, gf_ref, tgt_ref,
             up_ref, ab_ref, ff_ref, dx2_ref, dx2b_ref, loss_ref, dgf_ref, acc_ref, tail_ref):
        i = pl.program_id(0)
        cb = pl.program_id(1)

        @pl.when(i == 0)
        def _():
            tail_ref[cb] = jnp.zeros((2, 8, FF_CW), F32)

        @pl.when(jnp.logical_and(i == 0, cb == 0))
        def _():
            loss_ref[...] = jnp.zeros_like(loss_ref)
            dgf_ref[...] = jnp.zeros_like(dgf_ref)

        h2v = h2_ref[...]
        ua = _dot_nt(h2v, wa_ref[0])
        ub = _dot_nt(h2v, wb_ref[0])
        up_ref[0, 0] = ua.astype(MXU)
        up_ref[1, 0] = ub.astype(MXU)
        a = _causal_conv3(ua, tail_ref[cb, 0], cwa_ref[0], cba_ref[0])
        b = _causal_conv3(ub, tail_ref[cb, 1], cwb_ref[0], cbb_ref[0])
        tail_ref[cb, 0] = ua[tm - 8:tm, :]
        tail_ref[cb, 1] = ub[tm - 8:tm, :]
        ab_ref[0, 0] = a
        ab_ref[1, 0] = b
        ffb = (a * _sigmoid(a) * b).astype(MXU)
        ff_ref[0] = ffb
        contrib = _dot(ffb, wd_ref[...])

        @pl.when(cb == 0)
        def _():
            acc_ref[...] = contrib

        @pl.when(cb > 0)
        def _():
            acc_ref[...] += contrib

        @pl.when(cb == ncb - 1)
        def _():
            x2 = x1_ref[...] + acc_ref[...]
            r = _rms(x2)
            xn = x2 * r
            g = gf_ref[...]
            diff = xn * g - tgt_ref[...]
            loss_ref[...] += (0.5 / D_MODEL) * jnp.sum(diff * diff)
            dy = diff * (1.0 / D_MODEL)
            dgf_ref[...] += _rowsum(dy * xn)
            dx2 = _rms_bwd(dy * g, xn, r)
            dx2_ref[...] = dx2
            dx2b_ref[...] = dx2.astype(MXU)

    row = lambda n: pl.BlockSpec((tm, n), lambda i, c: (i, 0))
    gate = lambda r: pl.BlockSpec((1, r, FF_CW), lambda i, c: (c, 0, 0))
    lin = lambda r: pl.BlockSpec((1, r, FF_CW), lambda i, c: (ncb + c, 0, 0))
    return pl.pallas_call(
        body, name="ffn_fwd", grid=(nt, ncb),
        in_specs=[row(D_MODEL),
                  pl.BlockSpec((1, FF_CW, D_MODEL), lambda i, c: (c, 0, 0)),
                  pl.BlockSpec((1, FF_CW, D_MODEL), lambda i, c: (ncb + c, 0, 0)),
                  gate(3), lin(3), gate(1), lin(1),
                  pl.BlockSpec((FF_CW, D_MODEL), lambda i, c: (c, 0)),
                  row(D_MODEL), _full((1, D_MODEL)), row(D_MODEL)],
        out_specs=[pl.BlockSpec((2, 1, tm, FF_CW), lambda i, c: (0, c, i, 0)),
                   pl.BlockSpec((2, 1, tm, FF_CW), lambda i, c: (0, c, i, 0)),
                   pl.BlockSpec((1, tm, FF_CW), lambda i, c: (c, i, 0)),
                   row(D_MODEL), row(D_MODEL), _full((1, LANES)), _full((1, D_MODEL))],
        out_shape=[_sds((2, ncb, S, FF_CW), MXU), _sds((2, ncb, S, FF_CW)), _sds((ncb, S, FF_CW), MXU),
                   _sds((S, D_MODEL)), _sds((S, D_MODEL), MXU), _sds((1, LANES)), _sds((1, D_MODEL))],
        scratch_shapes=[pltpu.VMEM((tm, D_MODEL), F32), pltpu.VMEM((ncb, 2, 8, FF_CW), F32)],
        compiler_params=_cp("arbitrary", "arbitrary"),
    )(*_in_hbm([h2, w_up, w_up, conv_w, conv_w, conv_b, conv_b, w_down, x1, g_final, tgt]))


def _ffn_bwd(dx2, up, ab, x1, w_up, conv_w, w_down, g_ffn, tm):
    S = dx2.shape[0]
    nt = S // tm
    ncb = FF_NCB

    def body(dx2_ref, up_ref, ab_ref, cwa_ref, cwb_ref, wd_ref, wa_ref, wb_ref,
             x1_ref, g_ref, dup_ref, dx1_ref, dx1b_ref, dconv_ref, dg_ref, acc_ref, head_ref):
        i = pl.program_id(0)
        cb = pl.program_id(1)
        ri = nt - 1 - i

        @pl.when(i == 0)
        def _():
            head_ref[cb] = jnp.zeros((2, 8, FF_CW), F32)
            dconv_ref[cb] = jnp.zeros((8, FF_CW), F32)
            dconv_ref[ncb + cb] = jnp.zeros((8, FF_CW), F32)

        @pl.when(jnp.logical_and(i == 0, cb == 0))
        def _():
            dg_ref[...] = jnp.zeros_like(dg_ref)

        dff = _dot_nt(dx2_ref[...].astype(MXU), wd_ref[...])
        a = ab_ref[0, 0]
        b = ab_ref[1, 0]
        sa = _sigmoid(a)
        silu = a * sa
        da = (dff * b) * (sa + silu * (1.0 - sa))
        db = dff * silu
        dps = []
        for half, slot, d, cw_ref in ((0, cb, da, cwa_ref), (1, ncb + cb, db, cwb_ref)):
            dp, n1, n2, fix0, fix1 = _causal_conv3_adjoint(d, head_ref[cb, half], cw_ref[0])
            head_ref[cb, half] = d[0:8, :]
            dpb16 = dp.astype(MXU)
            dup_ref[half, 0] = dpb16
            dps.append(dpb16)
            u = up_ref[half, 0].astype(F32)
            u_last = u[tm - 1:tm, :]
            dconv_ref[slot, 0:1, :] += _rowsum(n2 * u) + fix0 * u[tm - 2:tm - 1, :] + fix1 * u_last
            dconv_ref[slot, 1:2, :] += _rowsum(n1 * u) + fix0 * u_last
            dconv_ref[slot, 2:3, :] += _rowsum(d * u)
            dconv_ref[slot, 3:4, :] += _rowsum(d)
        contrib = _dot(dps[0], wa_ref[0]) + _dot(dps[1], wb_ref[0])

        @pl.when(cb == 0)
        def _():
            acc_ref[...] = contrib

        @pl.when(cb > 0)
        def _():
            acc_ref[...] += contrib

        @pl.when(cb == ncb - 1)
        def _():
            x1v = x1_ref[...]
            r = _rms(x1v)
            xn = x1v * r
            dh2 = acc_ref[...]
            dg_ref[...] += _rowsum(dh2 * xn)
            dx1 = dx2_ref[...] + _rms_bwd(dh2 * g_ref[...], xn, r)
            dx1_ref[...] = dx1
            dx1b_ref[...] = dx1.astype(MXU)

    row = lambda n: pl.BlockSpec((tm, n), lambda i, c: (nt - 1 - i, 0))
    colb = lambda: pl.BlockSpec((2, 1, tm, FF_CW), lambda i, c: (0, c, nt - 1 - i, 0))
    gate = lambda r: pl.BlockSpec((1, r, FF_CW), lambda i, c: (c, 0, 0))
    lin = lambda r: pl.BlockSpec((1, r, FF_CW), lambda i, c: (ncb + c, 0, 0))
    return pl.pallas_call(
        body, name="ffn_bwd", grid=(nt, ncb),
        in_specs=[row(D_MODEL), colb(), colb(), gate(3), lin(3),
                  pl.BlockSpec((FF_CW, D_MODEL), lambda i, c: (c, 0)),
                  pl.BlockSpec((1, FF_CW, D_MODEL), lambda i, c: (c, 0, 0)),
                  pl.BlockSpec((1, FF_CW, D_MODEL), lambda i, c: (ncb + c, 0, 0)),
                  row(D_MODEL), _full((1, D_MODEL))],
        out_specs=[colb(), row(D_MODEL), row(D_MODEL), _full((2 * ncb, 8, FF_CW)), _full((1, D_MODEL))],
        out_shape=[_sds((2, ncb, S, FF_CW), MXU), _sds((S, D_MODEL)), _sds((S, D_MODEL), MXU), _sds((2 * ncb, 8, FF_CW)),
                   _sds((1, D_MODEL))],
        scratch_shapes=[pltpu.VMEM((tm, D_MODEL), F32), pltpu.VMEM((ncb, 2, 8, FF_CW), F32)],
        compiler_params=_cp("arbitrary", "arbitrary"),
    )(*_in_hbm([dx2, up, ab, conv_w, conv_w, w_down, w_up, w_up, x1, g_ffn]))


def _mix_bwd(dx1, gl, ya, yb, ys, uv, w_out, w_pa, w_pb, w_glu, b_glu, g_sgu, ws, ws_t, bias_s, tm):
    S = dx1.shape[0]

    def body(dx1_ref, gl_ref, ya_ref, yb_ref, ys_ref, uv_ref, wout_ref, wpa_ref, wpb_ref, wglu_ref, bglu_ref, gs_ref,
             ws_ref, wst_ref, bias_ref,
             dgl_ref, dya_ref, dyb_ref, dz_ref, dys_ref, duv_ref, dbglu_ref, dgs_ref, dws_ref, dbs_ref,
             du2_ref, dvn_ref):
        i = pl.program_id(0)

        @pl.when(i == 0)
        def _():
            dbglu_ref[...] = jnp.zeros_like(dbglu_ref)
            dgs_ref[...] = jnp.zeros_like(dgs_ref)
            dws_ref[...] = jnp.zeros_like(dws_ref)
            dbs_ref[...] = jnp.zeros_like(dbs_ref)

        dm = _dot_nt(dx1_ref[...].astype(MXU), wout_ref[...])
        glv = gl_ref[...]
        ga = _sigmoid(glv[:, :D_MODEL])
        gb = _sigmoid(glv[:, D_MODEL:])
        dgl_ref[:, :D_MODEL] = (dm * ya_ref[...] * ga * (1.0 - ga)).astype(MXU)
        dgl_ref[:, D_MODEL:] = (dm * yb_ref[...] * gb * (1.0 - gb)).astype(MXU)
        dyab = (dm * ga).astype(MXU)
        dybb = (dm * gb).astype(MXU)
        dya_ref[...] = dyab
        dyb_ref[...] = dybb

        dyap = _dot_nt(dyab, wpa_ref[...])
        yg, dgelu = _gelu_and_grad(ys_ref[...])
        sz = _sigmoid(_dot(yg.astype(MXU), wglu_ref[...]) + bglu_ref[...])
        dz = dyap * yg * sz * (1.0 - sz)
        dzb = dz.astype(MXU)
        dz_ref[...] = dzb
        dbglu_ref[...] += _rowsum(dz)
        dys_ref[...] = (dyap * sz + _dot_nt(dzb, wglu_ref[...])) * dgelu

        dsg = _dot_nt(dybb, wpb_ref[...])
        uvg, duvg = _gelu_and_grad(uv_ref[...])
        u2 = uvg[:, :SGU_W]
        v2 = uvg[:, SGU_W:]
        rv = _rms(v2)
        vhat = v2 * rv
        gs = gs_ref[...]
        vnb = (vhat * gs).astype(MXU)
        grp = lax.broadcasted_iota(jnp.int32, (CHUNK, SGU_W), 1) // SGU_D
        tril = (lax.broadcasted_iota(jnp.int32, (CHUNK, CHUNK), 0)
                >= lax.broadcasted_iota(jnp.int32, (CHUNK, CHUNK), 1))
        for c in range(tm // CHUNK):
            rs = slice(c * CHUNK, (c + 1) * CHUNK)
            vc = vnb[rs]
            mixed = _sgu_mix(vc, ws_ref, grp) + bias_ref[...]
            dsg_c = dsg[rs]
            du2_ref[rs, :] = dsg_c * mixed
            dmx = dsg_c * u2[rs]
            dbs_ref[...] += dmx
            dmb = dmx.astype(MXU)
            dvn_ref[rs, :] = _sgu_mix(dmb, wst_ref, grp)
            for g in range(SGU_G):
                part = _dot_nt(jnp.where(grp == g, dmb, jnp.zeros((), MXU)), vc)
                dws_ref[g] += jnp.where(tril, part, 0.0)
        dvn = dvn_ref[...]
        dgs_ref[...] += _rowsum(dvn * vhat)
        dv2 = _rms_bwd(dvn * gs, vhat, rv)
        duv_ref[:, :SGU_W] = (du2_ref[...] * duvg[:, :SGU_W]).astype(MXU)
        duv_ref[:, SGU_W:] = (dv2 * duvg[:, SGU_W:]).astype(MXU)

    row = lambda n: pl.BlockSpec((tm, n), lambda i: (i, 0))
    return pl.pallas_call(
        body, name="mix_bwd", grid=(S // tm,),
        in_specs=[row(D_MODEL), row(2 * D_MODEL), row(D_MODEL), row(D_MODEL), row(SSM_W), row(2 * SGU_W),
                  _full(w_out.shape), _full(w_pa.shape), _full(w_pb.shape), _full(w_glu.shape), _full(b_glu.shape),
                  _full(g_sgu.shape), _full(ws.shape), _full(ws_t.shape), _full(bias_s.shape)],
        out_specs=[row(2 * D_MODEL), row(D_MODEL), row(D_MODEL), row(SSM_W), row(SSM_W), row(2 * SGU_W),
                   _full((1, SSM_W)), _full((1, SGU_W)), _full((SGU_G, CHUNK, CHUNK)), _full((CHUNK, SGU_W))],
        out_shape=[_sds((S, 2 * D_MODEL), MXU), _sds((S, D_MODEL), MXU), _sds((S, D_MODEL), MXU), _sds((S, SSM_W), MXU),
                   _sds((S, SSM_W)), _sds((S, 2 * SGU_W), MXU),
                   _sds((1, SSM_W)), _sds((1, SGU_W)), _sds((SGU_G, CHUNK, CHUNK)), _sds((CHUNK, SGU_W))],
        scratch_shapes=[pltpu.VMEM((tm, SGU_W), F32), pltpu.VMEM((tm, SGU_W), F32)],
        compiler_params=_cp("arbitrary"),
    )(*_in_hbm([dx1, gl, ya, yb, ys, uv, w_out, w_pa, w_pb, w_glu, b_glu, g_sgu, ws, ws_t, bias_s]))


def _s5_bwd(dys, us, st_re, st_im, abar_re, abar_im, b_re, b_im, c_re, c_im, d_skip, tm):
    S = us.shape[0]
    nt = S // tm
    w = 8 * SSM_P
    hb = tm // 8
    run = tm // 8
    assert run & (run - 1) == 0

    def body(dys_ref, us_ref, str_ref, sti_ref, hr_ref, hi_ref, ar_ref, ai_ref, br_ref, bi_ref, cr_ref, ci_ref, d_ref,
             dus_ref, dab_ref, dd_ref, dbr_ref, dbi_ref, dcr_ref, dci_ref,
             tab_ref, car_ref, gr_ref, gi_ref, dyp_ref, up_ref, dun_ref):
        i = pl.program_id(1)
        ri = nt - 1 - i

        @pl.when(i == 0)
        def _():
            car_ref[...] = jnp.zeros_like(car_ref)
            for k, t in enumerate(_scan_tables(*_cpow2(ar_ref[...], -ai_ref[...], run.bit_length() - 1), True)):
                tab_ref[k] = t
            for r in (dab_ref, dd_ref, dbr_ref, dbi_ref, dcr_ref, dci_ref):
                r[...] = jnp.zeros_like(r)

        _runs_load(dys_ref, dyp_ref, run)
        _runs_load(us_ref, up_ref, run)
        dyb = dyp_ref[...].astype(MXU)
        gr_ref[...] = _dot(dyb, cr_ref[0])
        gi_ref[...] = -_dot(dyb, ci_ref[0])
        ar = jnp.broadcast_to(ar_ref[...], (8, w))
        ai = jnp.broadcast_to(-ai_ref[...], (8, w))

        def advance(kk, state):
            r0 = pl.multiple_of((run - 1 - kk) * 8, 8)
            gr, gi = state
            return (ar * gr - ai * gi + gr_ref[pl.ds(r0, 8), :], ar * gi + ai * gr + gi_ref[pl.ds(r0, 8), :])

        def emit(kk, state):
            r0 = pl.multiple_of((run - 1 - kk) * 8, 8)
            gr, gi = advance(kk, state)
            gr_ref[pl.ds(r0, 8), :] = gr
            gi_ref[pl.ds(r0, 8), :] = gi
            return gr, gi

        zero = jnp.zeros((8, w), F32)
        er, ei = lax.fori_loop(0, run, advance, (zero, zero))
        cr, ci = car_ref[0:1, :], car_ref[1:2, :]
        tr, ti = _scan_group(er, ei, tab_ref, cr, ci, True)
        r8 = lax.broadcasted_iota(jnp.int32, (8, w), 0)
        start = (jnp.where(r8 == 7, cr, pltpu.roll(tr, 7, 0)), jnp.where(r8 == 7, ci, pltpu.roll(ti, 7, 0)))
        car_ref[0:1, :] = tr[0:1, :]
        car_ref[1:2, :] = ti[0:1, :]
        lax.fori_loop(0, run, emit, start)

        gsr = gr_ref[...]
        gsi = gi_ref[...]
        sr = str_ref[...]
        si = sti_ref[...]
        first = ri == 0

        def previous(s, halo_ref):
            head = jnp.where(r8 == 0, jnp.where(first, 0.0, halo_ref[7:8, :]), pltpu.roll(s[tm - 8:tm, :], 1, 0))
            return jnp.concatenate([head, s[0:tm - 8, :]], axis=0)

        spr = previous(sr, hr_ref)
        spi = previous(si, hi_ref)
        dab_ref[0, 0:1, :] += _rowsum(gsr * spr + gsi * spi)
        dab_ref[0, 1:2, :] += _rowsum(gsi * spr - gsr * spi)

        gbr = gsr.astype(MXU)
        gbi = gsi.astype(MXU)
        _runs_store(_dot_nt(gbr, br_ref[0]) + _dot_nt(gbi, bi_ref[0]), dun_ref, run)
        dys_v = dys_ref[...]
        dus_ref[...] = (dun_ref[...] + d_ref[...] * dys_v).astype(MXU)
        dd_ref[0, 0:1, :] += _rowsum(dys_v * us_ref[...])
        ub = up_ref[...].astype(MXU)
        dbr_ref[0] += _dot_tn(ub, gbr)
        dbi_ref[0] += _dot_tn(ub, gbi)
        dcr_ref[0] += _dot_tn(dyb, sr.astype(MXU))
        dci_ref[0] -= _dot_tn(dyb, si.astype(MXU))

    blk = lambda: pl.BlockSpec((1, 8 * SSM_H, w), lambda j, i: (j, 0, 0))
    rowl = lambda: pl.BlockSpec((tm, LANES), lambda j, i: (nt - 1 - i, j))
    roww = lambda: pl.BlockSpec((tm, w), lambda j, i: (nt - 1 - i, j))
    halo = lambda: pl.BlockSpec((8, w), lambda j, i: (jnp.maximum((nt - 1 - i) * hb - 1, 0), j))
    return pl.pallas_call(
        body, name="s5_bwd", grid=(SSM_BLK, nt),
        in_specs=[rowl(), rowl(), roww(), roww(), halo(), halo(),
                  pl.BlockSpec((1, w), lambda j, i: (0, j)), pl.BlockSpec((1, w), lambda j, i: (0, j)),
                  blk(), blk(), blk(), blk(),
                  pl.BlockSpec((1, LANES), lambda j, i: (0, j))],
        out_specs=[rowl(),
                   pl.BlockSpec((1, 8, w), lambda j, i: (j, 0, 0)), pl.BlockSpec((1, 8, LANES), lambda j, i: (j, 0, 0)),
                   blk(), blk(), blk(), blk()],
        out_shape=[_sds((S, SSM_W), MXU), _sds((SSM_BLK, 8, w)), _sds((SSM_BLK, 8, LANES)),
                   _sds((SSM_BLK, 8 * SSM_H, w)), _sds((SSM_BLK, 8 * SSM_H, w)),
                   _sds((SSM_BLK, 8 * SSM_H, w)), _sds((SSM_BLK, 8 * SSM_H, w))],
        scratch_shapes=[pltpu.VMEM((8, 8, w), F32), pltpu.VMEM((8, w), F32),
                        pltpu.VMEM((tm, w), F32), pltpu.VMEM((tm, w), F32),
                        pltpu.VMEM((tm, LANES), F32), pltpu.VMEM((tm, LANES), F32), pltpu.VMEM((tm, LANES), F32)],
        compiler_params=_cp("parallel", "arbitrary"),
    )(*_in_hbm([dys, us, st_re, st_im, st_re, st_im, abar_re, abar_im, b_re, b_im, c_re, c_im, d_skip]))


def _in_bwd(dus, duv, dgl, dx1, x, g_mix, w_in, tm):
    S = x.shape[0]

    def body(dus_ref, duv_ref, dgl_ref, dx1_ref, x_ref, g_ref, w_ref, gx_ref, dg_ref):
        @pl.when(pl.program_id(0) == 0)
        def _():
            dg_ref[...] = jnp.zeros_like(dg_ref)

        dh = (_dot(dus_ref[...], w_ref[0:SSM_W, :])
              + _dot(duv_ref[...], w_ref[SSM_W:SSM_W + 2 * SGU_W, :])
              + _dot(dgl_ref[...], w_ref[SSM_W + 2 * SGU_W:, :]))
        xv = x_ref[...]
        r = _rms(xv)
        xn = xv * r
        dg_ref[...] += _rowsum(dh * xn)
        gx_ref[...] = dx1_ref[...] + _rms_bwd(dh * g_ref[...], xn, r)

    row = lambda n: pl.BlockSpec((tm, n), lambda i: (i, 0))
    return pl.pallas_call(
        body, name="in_bwd", grid=(S // tm,),
        in_specs=[row(SSM_W), row(2 * SGU_W), row(2 * D_MODEL), row(D_MODEL), row(D_MODEL), _full((1, D_MODEL)),
                  _full(w_in.shape)],
        out_specs=[row(D_MODEL), _full((1, D_MODEL))],
        out_shape=[_sds((S, D_MODEL)), _sds((1, D_MODEL))],
        compiler_params=_cp("arbitrary"),
    )(*_in_hbm([dus, duv, dgl, dx1, x, g_mix, w_in]))


def _pick(n, cands):
    for c in cands:
        if n % c == 0:
            return c
    return n


def _wgrad_split(a, b, nsplit, tk, name):
    S, K = a.shape
    N = b.shape[1]
    c = N // nsplit

    def body(a_ref, b_ref, o_ref):
        prod = _dot_tn(a_ref[...], b_ref[...])
        for d in range(nsplit):
            o_ref[d] = prod[:, c * d:c * (d + 1)].astype(MXU)

    return pl.pallas_call(
        body, name=name, grid=(K // tk,),
        in_specs=[pl.BlockSpec((S, tk), lambda k: (0, k)), _full((S, N))],
        out_specs=pl.BlockSpec((nsplit, tk, c), lambda k: (0, k, 0)),
        out_shape=_sds((nsplit, K, c), MXU),
        compiler_params=_cp("parallel"),
    )(*_in_hbm([a, b]))


def _wgrad_in_t(dps, h1, name):
    S, K = h1.shape
    cw = 512
    counts = [b.shape[1] // cw for b in dps]
    starts = [sum(counts[:i]) for i in range(len(dps))]
    nblk = sum(counts)

    def body(*refs):
        b_refs = refs[:len(dps)]
        h_ref, o_ref = refs[len(dps):]
        j = pl.program_id(0)
        for b_ref, st, cnt in zip(b_refs, starts, counts):
            @pl.when(jnp.logical_and(j >= st, j < st + cnt))
            def _():
                o_ref[...] = _dot_tn(b_ref[...], h_ref[...]).astype(MXU)

    def src_spec(st, cnt):
        return pl.BlockSpec((S, cw), lambda j: (0, jnp.clip(j - st, 0, cnt - 1)))

    return pl.pallas_call(
        body, name=name, grid=(nblk,),
        in_specs=[src_spec(st, cnt) for st, cnt in zip(starts, counts)] + [_full((S, K))],
        out_specs=pl.BlockSpec((cw, K), lambda j: (j, 0)),
        out_shape=_sds((nblk * cw, K), MXU),
        compiler_params=_cp("arbitrary"),
    )(*_in_hbm([*dps, h1]))


def _wgrad_blk(a3, b3, nblk, a_of, b_of, name):
    S, K = a3.shape[1:]
    N = b3.shape[2]

    def body(a_ref, b_ref, o_ref):
        o_ref[0] = _dot_tn(a_ref[0], b_ref[0]).astype(MXU)

    return pl.pallas_call(
        body, name=name, grid=(nblk,),
        in_specs=[pl.BlockSpec((1, S, K), lambda b: (a_of(b), 0, 0)),
                  pl.BlockSpec((1, S, N), lambda b: (b_of(b), 0, 0))],
        out_specs=pl.BlockSpec((1, K, N), lambda b: (b, 0, 0)),
        out_shape=_sds((nblk, K, N), MXU),
        compiler_params=_cp("parallel"),
    )(*_in_hbm([a3, b3]))


def _assemble_cols(blocks_list, name):
    def body(*refs):
        n = len(blocks_list)
        for b_ref, o_ref in zip(refs[:n], refs[n:]):
            c = b_ref.shape[2]
            for d in range(N_DEV):
                o_ref[:, c * d:c * (d + 1)] = b_ref[d]

    outs = [_sds((b.shape[1], N_DEV * b.shape[2]), b.dtype) for b in blocks_list]
    return pl.pallas_call(
        body, name=name, grid=(1,), in_specs=[_full(b.shape) for b in blocks_list],
        out_specs=[_full(o.shape) for o in outs], out_shape=outs, compiler_params=_cp("arbitrary"),
    )(*_in_hbm(blocks_list))


def _tile(S, want):
    return want if S % want == 0 else S


def _local_step(x, tgt, p, mixer_weights, ffn_weights, grads_out):
    S = x.shape[0]
    tm = _tile(S, 256)
    tl = _tile(S, 512)

    rep = lambda a: jnp.repeat(a, SSM_H, axis=0)
    are = rep(p["a_re"])
    aim = rep(p["a_im"])
    ldt = jnp.broadcast_to(rep(p["log_dt"].reshape(SSM_G, 1)), are.shape)
    br_t = p["b_re_t"].reshape(are.shape)
    bi_t = p["b_im_t"].reshape(are.shape)
    abr, abi, bbr, bbi = _s5_params_fwd(are, aim, ldt, br_t, bi_t)
    head = lambda a: a.reshape(SSM_G, SSM_H, SSM_P)[:, 0, :].reshape(1, SSM_G * SSM_P)
    abar_re, abar_im = head(abr), head(abi)
    bd_br = _blockdiag(bbr).astype(MXU)
    bd_bi = _blockdiag(bbi).astype(MXU)
    bd_cr = _blockdiag(p["c_re"].reshape(are.shape)).astype(MXU)
    bd_ci = _blockdiag(p["c_im"].reshape(are.shape)).astype(MXU)
    d_skip = p["d_skip"].reshape(1, SSM_W)

    tril = jnp.tril(jnp.ones((CHUNK, CHUNK), dtype=bool))
    ws = jnp.where(tril[None], p["w_s"], 0.0)
    ws_b = ws.astype(MXU)
    ws_t = ws.transpose(0, 2, 1).astype(MXU)
    bias_s = jnp.repeat(p["b_s"].T, SGU_D, axis=1)

    g_mix = p["g_mix"].reshape(1, D_MODEL)
    g_ffn = p["g_ffn"].reshape(1, D_MODEL)
    g_final = p["g_final"].reshape(1, D_MODEL)
    g_sgu = p["g_sgu"].reshape(1, SGU_W)
    b_glu = p["b_glu"].reshape(1, SSM_W)
    conv_b = p["conv_b"].reshape(N_DEV, 1, FF_CW)

    h1, us, uv, gl = _in_fwd(x, g_mix, p["w_in_t"], tm)
    st_re, st_im, ys = _s5_fwd(us, abar_re, abar_im, bd_br, bd_bi, bd_cr, bd_ci, d_skip, tl)
    p = dict(p, **mixer_weights(ys))
    yg, yap, sg, ya, yb, m, x1, h2 = _mix_fwd(x, ys, uv, gl, p["w_glu"], b_glu, p["w_proj_a"], g_sgu, ws_b, bias_s,
                                              p["w_proj_b"], p["w_out"], g_ffn, tm)
    w_up, conv_w, w_down = ffn_weights(h2)
    up, ab, ff, dx2, dx2b, loss, dg_final = _ffn_fwd(h2, x1, tgt, w_up, conv_w, conv_b, w_down, g_final, tl)

    dup, dx1, dx1b, dconv, dg_ffn = _ffn_bwd(dx2, up, ab, x1, w_up, conv_w, w_down, g_ffn, tl)
    rows8 = lambda g: g.reshape(N_DEV, g.shape[1] // N_DEV, g.shape[2])
    g_up = _wgrad_blk(dup.reshape(N_DEV, S, FF_CW), h2[None], N_DEV, lambda b: b, lambda b: 0, "wgrad_up")
    g_down = _wgrad_blk(ff, dx2b[None], FF_NCB, lambda b: b, lambda b: 0, "wgrad_down").reshape(
        N_DEV, D_FF // N_DEV, D_MODEL)
    token = grads_out(("w_up", "w_down"), (g_up, g_down))
    dgl, dya, dyb, dz, dys, duv, db_glu, dg_sgu, dws, dbs = _mix_bwd(
        dx1, gl, ya, yb, ys, uv, p["w_out"], p["w_proj_a"], p["w_proj_b"], p["w_glu"], b_glu + token[0:1, 0:1], g_sgu,
        ws_b, ws_t, bias_s, tm)
    token = grads_out(("w_glu", "w_proj_a", "w_proj_b", "w_out"),
                      (rows8(_wgrad_split(yg, dz, 1, SSM_W, "wgrad_glu")),
                       _wgrad_split(yap, dya, N_DEV, SSM_W, "wgrad_pa"),
                       _wgrad_split(sg, dyb, N_DEV, SGU_W, "wgrad_pb"),
                       rows8(_wgrad_split(m, dx1b, 1, 512, "wgrad_out"))))
    dus, dab, dd, dbbr, dbbi, dcr, dci = _s5_bwd(dys, us, st_re, st_im, abar_re, abar_im, bd_br, bd_bi, bd_cr, bd_ci,
                                                 d_skip + token[0:1, 0:1], tl)
    g_in = _wgrad_in_t([dus, duv, dgl], h1, "wgrad_in")
    token = grads_out(("w_in",), (g_in.reshape(N_DEV, g_in.shape[0] // N_DEV, D_MODEL),))
    grad_x, dg_mix = _in_bwd(dus, duv, dgl, dx1, x, g_mix + token[0:1, 0:1], p["w_in_t"], tm)

    spread = lambda v: jnp.repeat(v.reshape(SSM_G, SSM_P), SSM_H, axis=0) * (1.0 / SSM_H)
    dabr = spread(dab[:, 0, :])
    dabi = spread(dab[:, 1, :])
    dare, daim, dldt, dbr_t, dbi_t = _s5_params_bwd(are, aim, ldt, br_t, bi_t, dabr, dabi,
                                                    _unblockdiag(dbbr), _unblockdiag(dbbi))
    fold = lambda a: a.reshape(SSM_G, SSM_H, SSM_P).sum(axis=1)

    grads = {
        "g_mix": dg_mix,
        "a_re": fold(dare), "a_im": fold(daim), "log_dt": fold(dldt).sum(axis=1),
        "b_re": dbr_t, "b_im": dbi_t,
        "c_re": _unblockdiag(dcr).reshape(SSM_G, SSM_H, SSM_P),
        "c_im": _unblockdiag(dci).reshape(SSM_G, SSM_H, SSM_P),
        "d_skip": dd[:, 0, :].reshape(SSM_W),
        "b_glu": db_glu,
        "g_sgu": dg_sgu,
        "w_s": dws,
        "b_s": dbs.reshape(CHUNK, SGU_G, SGU_D).sum(axis=-1).T,
        "g_ffn": dg_ffn,
        "conv_w": dconv[:, 0:3, :],
        "conv_b": dconv[:, 3, :].reshape(2 * D_FF),
        "g_final": dg_final,
    }
    return loss, grad_x, grads


_ANY = pl.BlockSpec(memory_space=pl.ANY)
_MESH = pl.DeviceIdType.MESH


def _allgather(shards, dtypes, name, cast_only=()):
    n = len(shards)
    e = len(cast_only)

    def body(*refs):
        in_refs, extra_in = refs[:n], refs[n:n + e]
        out_refs, extra_out = refs[n + e:2 * n + e], refs[2 * n + e:2 * n + 2 * e]
        stage = refs[2 * n + 2 * e:3 * n + 2 * e]
        send_sems, recv_sems, local_sems = refs[3 * n + 2 * e:]
        for a in range(n):
            stage[a][...] = in_refs[a][...].astype(dtypes[a])
        for i in range(e):
            extra_out[i][...] = extra_in[i][...].astype(MXU)
        x, y, c = lax.axis_index("x"), lax.axis_index("y"), lax.axis_index("c")
        me, sibling = (x, y, c), (x, y, 1 - c)
        chips = [(1 - x, y), (x, 1 - y), (1 - x, 1 - y)]

        def slot(a, px, py, pc):
            return out_refs[a].at[4 * px + 2 * py + pc]

        def copy(a, k, block, to, src=None):
            return pltpu.make_async_remote_copy(
                src_ref=slot(a, *block) if src is None else src, dst_ref=slot(a, *block),
                send_sem=send_sems.at[a, k], recv_sem=recv_sems.at[a, k], device_id=to, device_id_type=_MESH)

        mine = [pltpu.make_async_copy(stage[a], slot(a, *me), local_sems.at[a]) for a in range(n)]
        for cp in mine:
            cp.start()
        first = []
        for j, chip in enumerate(chips):
            first += [copy(a, 1 + j, me, (*chip, c), src=stage[a]) for a in range(n)]
        first += [copy(a, 0, me, sibling, src=stage[a]) for a in range(n)]
        for cp in first:
            cp.start()
        passed = []
        for j, chip in enumerate(chips):
            for a in range(n):
                copy(a, 1 + j, (*chip, c), me).wait_recv()
                fwd = copy(a, 4 + j, (*chip, c), sibling)
                fwd.start()
                passed.append(fwd)
        for a in range(n):
            copy(a, 0, sibling, me).wait_recv()
        for j, chip in enumerate(chips):
            for a in range(n):
                copy(a, 4 + j, (*chip, 1 - c), me).wait_recv()
        for cp in first + passed:
            cp.wait_send()
        for cp in mine:
            cp.wait()

    res = pl.pallas_call(
        body, name=name, grid=(1,), in_specs=[_full(s.shape) for s in list(shards) + list(cast_only)],
        out_specs=[_ANY] * n + [_full(s.shape) for s in cast_only],
        out_shape=[_sds((N_DEV,) + s.shape, dt) for s, dt in zip(shards, dtypes)]
                  + [_sds(s.shape, MXU) for s in cast_only],
        scratch_shapes=[pltpu.VMEM(s.shape, dt) for s, dt in zip(shards, dtypes)]
                       + [pltpu.SemaphoreType.DMA((n, 7)), pltpu.SemaphoreType.DMA((n, 7)), pltpu.SemaphoreType.DMA((n,))],
        compiler_params=pltpu.CompilerParams(vmem_limit_bytes=VMEM_LIMIT),
    )(*_in_hbm([*shards, *cast_only]))
    return res[:n], res[n:]


def _all_to_all(sends, name):
    n = len(sends)

    def body(*refs):
        send_refs, recv_refs = refs[:n], refs[n:2 * n]
        send_sems, recv_sems, local_sems = refs[2 * n:]
        x, y, c = lax.axis_index("x"), lax.axis_index("y"), lax.axis_index("c")
        me = 4 * x + 2 * y + c
        mine = [pltpu.make_async_copy(send_refs[a].at[me], recv_refs[a].at[me], local_sems.at[a]) for a in range(n)]
        for cp in mine:
            cp.start()
        copies = []
        for k in (2, 4, 6, 3, 5, 7, 1):
            px = 1 - x if k & 4 else x
            py = 1 - y if k & 2 else y
            pc = 1 - c if k & 1 else c
            peer = 4 * px + 2 * py + pc
            for a in range(n):
                sems = dict(send_sem=send_sems.at[a, k - 1], recv_sem=recv_sems.at[a, k - 1],
                            device_id=(px, py, pc), device_id_type=_MESH)
                cp = pltpu.make_async_remote_copy(src_ref=send_refs[a].at[peer], dst_ref=recv_refs[a].at[me], **sems)
                cp.start()
                landing = pltpu.make_async_remote_copy(src_ref=send_refs[a].at[peer], dst_ref=recv_refs[a].at[peer],
                                                       **sems)
                copies.append((cp, landing))
        for _, landing in copies:
            landing.wait_recv()
        for cp, _ in copies:
            cp.wait_send()
        for cp in mine:
            cp.wait()

    return pl.pallas_call(
        body, name=name, in_specs=[_ANY] * n, out_specs=[_ANY] * n,
        out_shape=[_sds(s.shape, s.dtype) for s in sends],
        scratch_shapes=[pltpu.SemaphoreType.DMA((n, 7)), pltpu.SemaphoreType.DMA((n, 7)), pltpu.SemaphoreType.DMA((n,))],
    )(*sends)


_HBM = pl.BlockSpec(memory_space=pltpu.HBM)
_SEM = pl.BlockSpec(memory_space=pltpu.SEMAPHORE)
_EFFECT = pltpu.SideEffectType.DATAFLOW_SIDE_EFFECTING
_PEER_ORDER = (2, 4, 6, 3, 5, 7, 1)


def _peer(k):
    x, y, c = lax.axis_index("x"), lax.axis_index("y"), lax.axis_index("c")
    px = 1 - x if k & 4 else x
    py = 1 - y if k & 2 else y
    pc = 1 - c if k & 1 else c
    return (px, py, pc), 4 * px + 2 * py + pc


def _push_start(srcs, lands, slotted, name):
    n = len(srcs)

    def body(*refs):
        src_refs, land_refs = refs[:n], refs[n:2 * n]
        send_sems, recv_sems, token = refs[2 * n], refs[2 * n + 1], refs[-1]
        me = 4 * lax.axis_index("x") + 2 * lax.axis_index("y") + lax.axis_index("c")
        for k in _PEER_ORDER:
            dev, peer = _peer(k)
            for a in range(n):
                pltpu.make_async_remote_copy(
                    src_ref=src_refs[a].at[peer] if slotted else src_refs[a], dst_ref=land_refs[a].at[me],
                    send_sem=send_sems.at[7 * a + k - 1], recv_sem=recv_sems.at[7 * a + k - 1],
                    device_id=dev, device_id_type=_MESH).start()
        token[...] = jnp.zeros_like(token)

    bufs = list(srcs) + list(lands)
    res = pl.pallas_call(
        body, name=name, in_specs=[_HBM] * (2 * n),
        out_specs=(_SEM, _SEM, *[_HBM] * (2 * n), pl.BlockSpec(memory_space=pltpu.VMEM)),
        out_shape=(pltpu.SemaphoreType.DMA((7 * n,)), pltpu.SemaphoreType.DMA((7 * n,)),
                   *[pltpu.HBM(b.shape, b.dtype) for b in bufs], _sds((8, LANES))),
        input_output_aliases={i: 2 + i for i in range(2 * n)},
        compiler_params=pltpu.CompilerParams(has_side_effects=_EFFECT),
    )(*[pltpu.with_memory_space_constraint(b, pltpu.HBM) for b in bufs])
    return res[0], res[1], res[2:2 + n], res[2 + n:2 + 2 * n], res[-1]


def _push_wait(send_sems, recv_sems, srcs, lands, slotted, after, name):
    n = len(srcs)

    def body(*refs):
        src_refs, land_refs = refs[:n], refs[n:2 * n]
        send_sems, recv_sems = refs[2 * n], refs[2 * n + 1]
        for k in _PEER_ORDER:
            dev, peer = _peer(k)
            for a in range(n):
                cp = pltpu.make_async_remote_copy(
                    src_ref=src_refs[a].at[peer] if slotted else src_refs[a], dst_ref=land_refs[a].at[peer],
                    send_sem=send_sems.at[7 * a + k - 1], recv_sem=recv_sems.at[7 * a + k - 1],
                    device_id=dev, device_id_type=_MESH)
                cp.wait_send()
                cp.wait_recv()

    bufs = list(srcs) + list(lands)
    res = pl.pallas_call(
        body, name=name, in_specs=[_HBM] * (2 * n) + [_SEM, _SEM] + [_ANY] * len(after), out_specs=[_HBM] * (2 * n),
        out_shape=[pltpu.HBM(b.shape, b.dtype) for b in bufs],
        input_output_aliases={i: i for i in range(2 * n)},
        compiler_params=pltpu.CompilerParams(has_side_effects=_EFFECT),
    )(*bufs, send_sems, recv_sems, *after)
    return res[n:]


def _adamw(w, g, m, v):
    m2 = ADAM_B1 * m + (1.0 - ADAM_B1) * g
    v2 = ADAM_B2 * v + (1.0 - ADAM_B2) * (g * g)
    m_hat = m2 / (1.0 - ADAM_B1 ** ADAM_STEP)
    v_hat = v2 / (1.0 - ADAM_B2 ** ADAM_STEP)
    delta = -ADAM_LR * (m_hat / (jnp.sqrt(v_hat) + ADAM_EPS) + ADAM_WD * w)
    return delta, m2, v2


def _adam_shard(parts, w, m, v, name):
    _, r, c = w.shape
    tr = max(t for t in range(16, 257, 16) if r % t == 0)

    def body(p_ref, w_ref, m_ref, v_ref, g_ref, d_ref, m2_ref, v2_ref):
        g = p_ref[0].astype(F32)
        for s in range(1, N_DEV):
            g = g + p_ref[s].astype(F32)
        g_ref[0] = g
        d_ref[0], m2_ref[0], v2_ref[0] = _adamw(w_ref[0], g, m_ref[0], v_ref[0])

    row = lambda: pl.BlockSpec((1, tr, c), lambda i: (0, i, 0))
    return pl.pallas_call(
        body, name=name, grid=(r // tr,),
        in_specs=[pl.BlockSpec((N_DEV, tr, c), lambda i: (0, i, 0)), row(), row(), row()],
        out_specs=[row(), row(), row(), row()], out_shape=[_sds((1, r, c))] * 4,
        compiler_params=_cp("parallel"),
    )(*_in_hbm([parts, w, m, v]))


def _adam_small(gs, ws, ms, vs, name):
    n = len(gs)

    def body(*refs):
        ins, outs = refs[:4 * n], refs[4 * n:]
        for i in range(n):
            g = ins[i][...]
            d, m2, v2 = _adamw(ins[n + i][...], g, ins[2 * n + i][...], ins[3 * n + i][...])
            outs[i][...] = d
            outs[n + i][...] = m2
            outs[2 * n + i][...] = v2

    res = pl.pallas_call(
        body, name=name, grid=(1,), in_specs=[_full(w.shape) for w in ws] * 4,
        out_specs=[_full(w.shape) for w in ws] * 3, out_shape=[_sds(w.shape) for w in ws] * 3,
        compiler_params=_cp("arbitrary"),
    )(*_in_hbm([*gs, *ws, *ms, *vs]))
    return res[:n], res[n:2 * n], res[2 * n:]


def _sum_slots(parts, name):
    R = parts.shape[1]

    def body(p_ref, o_ref):
        g = p_ref[0]
        for s in range(1, N_DEV):
            g = g + p_ref[s]
        o_ref[...] = g

    return pl.pallas_call(body, name=name, grid=(1,), in_specs=[_full(parts.shape)], out_specs=_full((R, LANES)),
                          out_shape=_sds((R, LANES)))(*_in_hbm([parts]))


def _pad_to(a, n, axis):
    extra = n - a.shape[axis]
    if extra == 0:
        return a
    widths = [(0, 0)] * a.ndim
    widths[axis] = (0, extra)
    return jnp.pad(a, widths)


def _ceil_to(n, k):
    return -(-n // k) * k


def _pack_rows(flats, rows_multiple):
    parts = [_pad_to(f, _ceil_to(f.shape[-1], LANES), f.ndim - 1) for f in flats]
    cat = jnp.concatenate(parts, axis=-1)
    total = _ceil_to(cat.shape[-1], LANES * rows_multiple)
    cat = _pad_to(cat, total, cat.ndim - 1)
    return cat.reshape(cat.shape[:-1] + (total // LANES, LANES))


def _unpack_rows(buf, sizes):
    flat = buf.reshape(buf.shape[:-2] + (-1,))
    out, off = [], 0
    for n in sizes:
        out.append(flat[..., off:off + n])
        off += _ceil_to(n, LANES)
    return out


_MIX_BIG = ("w_in", "w_glu", "w_proj_a", "w_proj_b", "w_out")
_BIG = _MIX_BIG + ("w_up", "w_down")
_SMALL = ("g_mix", "a_re", "a_im", "log_dt", "b_re", "b_im", "c_re", "c_im", "d_skip", "b_glu", "g_sgu", "w_s", "b_s",
          "g_ffn", "conv_b", "g_final")
_SMALL_ROWS_MULTIPLE = 8 * N_DEV
_TRANSPOSED = ("w_in", "w_up", "b_re", "b_im")


def _as_2d(a):
    return a.reshape(-1, a.shape[-1]) if a.ndim > 1 else a.reshape(1, -1)


def kernel(x, g_mix, w_in, a_re, a_im, log_dt, b_re, b_im, c_re, c_im, d_skip, w_glu, b_glu, w_proj_a, g_sgu, w_s, b_s, w_proj_b, w_out, g_ffn, w_up, conv_w, conv_b, w_down, g_final, loss_target, m_g_mix, m_w_in, m_a_re, m_a_im, m_log_dt, m_b_re, m_b_im, m_c_re, m_c_im, m_d_skip, m_w_glu, m_b_glu, m_w_proj_a, m_g_sgu, m_w_s, m_b_s, m_w_proj_b, m_w_out, m_g_ffn, m_w_up, m_conv_w, m_conv_b, m_w_down, m_g_final, v_g_mix, v_w_in, v_a_re, v_a_im, v_log_dt, v_b_re, v_b_im, v_c_re, v_c_im, v_d_skip, v_w_glu, v_b_glu, v_w_proj_a, v_g_sgu, v_w_s, v_b_s, v_w_proj_b, v_w_out, v_g_ffn, v_w_up, v_conv_w, v_conv_b, v_w_down, v_g_final):
    args = dict(locals())
    me = 4 * lax.axis_index("x") + 2 * lax.axis_index("y") + lax.axis_index("c")

    def own_slot(buf, block):
        return lax.dynamic_update_slice(buf, block[None], (me,) + (0,) * block.ndim)

    for n in _TRANSPOSED:
        for pre in ("", "m_", "v_"):
            args[pre + n] = jnp.swapaxes(args[pre + n], -1, -2)
    later = ("w_glu", "w_proj_a", "w_proj_b", "w_out", "w_up", "w_down")
    (w_in_g,), casts = _allgather([args["w_in"][0]], [MXU], "allgather_w_in", cast_only=[args[n][0] for n in later])
    sh = dict(zip(later, casts))

    def start_push(names, srcs, tag):
        lands = [own_slot(lax.empty((N_DEV,) + s.shape, s.dtype), s) for s in srcs]
        send_sems, recv_sems, srcs, lands, token = _push_start(srcs, lands, False, "push_" + tag)
        return (send_sems, recv_sems, srcs, lands), token

    mix_push, token_a = start_push(later[:4], [sh[n] for n in later[:4]], "mixer_weights")
    ffn_push, token_b = start_push(("w_up", "w_down", "conv_w"), [sh["w_up"], sh["w_down"], conv_w[0]], "ffn_weights")
    p = {n: (args[n][0] if n != "g_final" else args[n]) for n in _SMALL if n not in _TRANSPOSED}
    p.update(w_in_t=w_in_g.reshape(SSM_W + 2 * SGU_W + 2 * D_MODEL, D_MODEL),
             b_re_t=args["b_re"][0], b_im_t=args["b_im"][0])
    p["g_mix"] = p["g_mix"] + (token_a[0:1, 0:1] + token_b[0:1, 0:1])

    def mixer_weights(after):
        w_glu_g, w_pa_g, w_pb_g, w_out_g = _push_wait(*mix_push, False, [after], "wait_mixer_weights")
        w_pa_full, w_pb_full = _assemble_cols([w_pa_g, w_pb_g], "assemble_cols")
        return dict(w_glu=w_glu_g.reshape(SSM_W, SSM_W), w_proj_a=w_pa_full, w_proj_b=w_pb_full,
                    w_out=w_out_g.reshape(D_MODEL, D_MODEL))

    def ffn_weights(after):
        w_up_g, w_down_g, conv_w_g = _push_wait(*ffn_push, False, [after], "wait_ffn_weights")
        return w_up_g, conv_w_g, w_down_g.reshape(D_FF, D_MODEL)

    pushes = []

    def grads_out(names, sends):
        lands = [own_slot(lax.empty(s.shape, s.dtype), lax.dynamic_index_in_dim(s, me, 0, keepdims=False))
                 for s in sends]
        send_sems, recv_sems, srcs, lands, token = _push_start(list(sends), lands, True, "push_grads_" + names[0])
        pushes.append((names, send_sems, recv_sems, srcs, lands))
        return token

    loss_part, grad_x, grads = _local_step(x[0], loss_target[0], p, mixer_weights, ffn_weights, grads_out)

    small_names = _SMALL + ("conv_w", "loss")
    small_g = dict(grads, loss=loss_part[0, 0:1])
    flats = [small_g[n].reshape(-1) for n in small_names]
    small_sizes = [f.shape[0] for f in flats]
    g_small = _pack_rows(flats, _SMALL_ROWS_MULTIPLE)
    rs8 = g_small.shape[0] // N_DEV
    grads_out(("small",), (g_small.reshape(N_DEV, rs8, LANES),))

    out = {}
    done = [g_small]
    for names, send_sems, recv_sems, srcs, lands in pushes:
        parts = _push_wait(send_sems, recv_sems, srcs, lands, True, done, "wait_grads_" + names[0])
        if names == ("small",):
            recv_small, = parts
            break
        for n, part in zip(names, parts):
            res = _adam_shard(part, args[n], args["m_" + n], args["v_" + n], "adam_" + n)
            for kind, v in zip(("grad_", "delta_", "new_m_", "new_v_"), res):
                out[kind + n] = v
            done = [res[0]]
    small_mine = _sum_slots(recv_small, "sum_small")
    g_small_all = _allgather([small_mine], [F32], "allgather_small")[0][0].reshape(N_DEV * rs8, LANES)
    pieces = dict(zip(small_names, _unpack_rows(g_small_all, small_sizes)))
    loss = pieces["loss"][0]
    dconv_w = lax.dynamic_index_in_dim(pieces["conv_w"].reshape(N_DEV, 3, FF_CW), me, axis=0, keepdims=False)
    names2 = _SMALL + ("conv_w",)
    gs = [pieces[n].reshape(_as_2d(args[n]).shape) for n in _SMALL] + [dconv_w]
    ds, m2s, v2s = _adam_small(gs, [_as_2d(args[n]) for n in names2], [_as_2d(args["m_" + n]) for n in names2],
                               [_as_2d(args["v_" + n]) for n in names2], "adam_small")
    for n, res in zip(names2, zip(gs, ds, m2s, v2s)):
        for kind, v in zip(("grad_", "delta_", "new_m_", "new_v_"), res):
            out[kind + n] = v.reshape(args[n].shape)
    order = ("g_mix", "w_in", "a_re", "a_im", "log_dt", "b_re", "b_im", "c_re", "c_im", "d_skip", "w_glu", "b_glu",
             "w_proj_a", "g_sgu", "w_s", "b_s", "w_proj_b", "w_out", "g_ffn", "w_up", "conv_w", "conv_b", "w_down",
             "g_final")
    res = [loss, grad_x.reshape(x.shape)]
    for kind in ("grad_", "delta_", "new_m_", "new_v_"):
        res += [jnp.swapaxes(out[kind + n], -1, -2) if n in _TRANSPOSED else out[kind + n] for n in order]
    return tuple(res)
```

```python
import functools
import math

import jax
import jax.numpy as jnp
from jax import lax
from jax.experimental import pallas as pl
from jax.experimental.pallas import tpu as pltpu

F32 = jnp.float32
MXU = jnp.bfloat16
EPS = 1e-6

D_MODEL = 1024
SSM_W = 512
SSM_G, SSM_H, SSM_P = 32, 16, 64
SSM_BLK = 4
SGU_W = 512
SGU_G, SGU_D, CHUNK = 8, 64, 128
D_FF = 2816
N_DEV = 8
FF_CW = 2 * D_FF // N_DEV
FF_NCB = D_FF // FF_CW
LANES = 128

ADAM_LR, ADAM_B1, ADAM_B2, ADAM_EPS, ADAM_WD, ADAM_STEP = 0.001, 0.9, 0.999, 1e-08, 0.01, 10

VMEM_LIMIT = 48 * 1024 * 1024


def _cp(*sem):
    return pltpu.CompilerParams(dimension_semantics=sem, vmem_limit_bytes=VMEM_LIMIT)


def _full(shape):
    n = len(shape)
    return pl.BlockSpec(shape, lambda *_: (0,) * n)


def _sds(shape, dtype=F32):
    return jax.ShapeDtypeStruct(shape, dtype)


def _in_hbm(arrays):
    return [pltpu.with_memory_space_constraint(a, pltpu.HBM) for a in arrays]


def _dot(a, b):
    return jnp.dot(a, b, preferred_element_type=F32)


def _dot_nt(a, b):
    return lax.dot_general(a, b, (((1,), (1,)), ((), ())), preferred_element_type=F32)


def _dot_tn(a, b):
    return lax.dot_general(a, b, (((0,), (0,)), ((), ())), preferred_element_type=F32)


_GELU_C = math.sqrt(2.0 / math.pi)


def _gelu(x):
    return 0.5 * x * (1.0 + jnp.tanh(_GELU_C * (x + 0.044715 * (x * x * x))))


def _gelu_and_grad(x):
    t = jnp.tanh(_GELU_C * (x + 0.044715 * (x * x * x)))
    g = 0.5 * x * (1.0 + t)
    dg = 0.5 * (1.0 + t) + 0.5 * x * (1.0 - t * t) * (_GELU_C * (1.0 + 3.0 * 0.044715 * (x * x)))
    return g, dg


def _sigmoid(x):
    return 0.5 * jnp.tanh(0.5 * x) + 0.5


def _rms(x):
    return lax.rsqrt(jnp.mean(x * x, axis=-1, keepdims=True) + EPS)


def _rms_bwd(dxn, xn, r):
    return r * (dxn - xn * jnp.mean(dxn * xn, axis=-1, keepdims=True))


def _rowsum(x):
    return jnp.sum(x, axis=0, keepdims=True)


def _s5_disc(are, aim, ldt, br, bi):
    dt = jnp.exp(ldt)
    mag = jnp.exp(dt * are)
    abr = mag * jnp.cos(dt * aim)
    abi = mag * jnp.sin(dt * aim)
    den = are * are + aim * aim
    nr = abr - 1.0
    ni = abi
    fr = (nr * are + ni * aim) / den
    fi = (ni * are - nr * aim) / den
    return abr, abi, fr * br - fi * bi, fr * bi + fi * br


def _s5_params_fwd(are, aim, ldt, br, bi):
    def body(are_ref, aim_ref, ldt_ref, br_ref, bi_ref, o0, o1, o2, o3):
        outs = _s5_disc(are_ref[...], aim_ref[...], ldt_ref[...], br_ref[...], bi_ref[...])
        for o, v in zip((o0, o1, o2, o3), outs):
            o[...] = v
    shp = are.shape
    return pl.pallas_call(body, name="s5_params_fwd", grid=(1,), in_specs=[_full(shp)] * 5, out_specs=[_full(shp)] * 4,
                          out_shape=[_sds(shp)] * 4)(*_in_hbm([are, aim, ldt, br, bi]))


def _s5_params_bwd(are, aim, ldt, br, bi, dabr, dabi, dbr, dbi):
    def body(are_ref, aim_ref, ldt_ref, br_ref, bi_ref, c0, c1, c2, c3, o0, o1, o2, o3, o4):
        prim = (are_ref[...], aim_ref[...], ldt_ref[...], br_ref[...], bi_ref[...])
        _, vjp = jax.vjp(_s5_disc, *prim)
        outs = vjp((c0[...], c1[...], c2[...], c3[...]))
        for o, v in zip((o0, o1, o2, o3, o4), outs):
            o[...] = v
    shp = are.shape
    return pl.pallas_call(body, name="s5_params_bwd", grid=(1,), in_specs=[_full(shp)] * 9, out_specs=[_full(shp)] * 5,
                          out_shape=[_sds(shp)] * 5)(*_in_hbm([are, aim, ldt, br, bi, dabr, dabi, dbr, dbi]))


def _blockdiag(m_t):
    m = m_t.reshape(SSM_BLK, 8, SSM_H, 1, SSM_P)
    eye = jnp.eye(8, dtype=bool).reshape(1, 8, 1, 8, 1)
    return jnp.where(eye, m, jnp.zeros((), m_t.dtype)).reshape(SSM_BLK, 8 * SSM_H, 8 * SSM_P)


def _unblockdiag(pc):
    m = pc.reshape(SSM_BLK, 8, SSM_H, 8, SSM_P)
    return jnp.einsum("jghgp->jghp", m).reshape(SSM_G * SSM_H, SSM_P)


def _in_fwd(x, g_mix, w_in_t, tm):
    S = x.shape[0]

    def body(x_ref, g_ref, w_ref, h_ref, us_ref, uv_ref, gl_ref):
        xv = x_ref[...]
        h = (xv * _rms(xv) * g_ref[...]).astype(MXU)
        h_ref[...] = h
        us_ref[...] = _dot_nt(h, w_ref[0:SSM_W, :])
        uv_ref[...] = _dot_nt(h, w_ref[SSM_W:SSM_W + 2 * SGU_W, :])
        gl_ref[...] = _dot_nt(h, w_ref[SSM_W + 2 * SGU_W:, :])

    row = lambda n: pl.BlockSpec((tm, n), lambda i: (i, 0))
    return pl.pallas_call(
        body, name="in_fwd", grid=(S // tm,),
        in_specs=[row(D_MODEL), _full((1, D_MODEL)), _full(w_in_t.shape)],
        out_specs=[row(D_MODEL), row(SSM_W), row(2 * SGU_W), row(2 * D_MODEL)],
        out_shape=[_sds((S, D_MODEL), MXU), _sds((S, SSM_W)), _sds((S, 2 * SGU_W)), _sds((S, 2 * D_MODEL))],
        compiler_params=_cp("parallel"),
    )(*_in_hbm([x, g_mix, w_in_t]))


def _scan_tables(ar, ai, reverse):
    n = ar.shape[-1]
    def mul(p, q):
        return p[0] * q[0] - p[1] * q[1], p[0] * q[1] + p[1] * q[0]
    a1 = (ar, ai)
    a2 = mul(a1, a1)
    a3 = mul(a2, a1)
    a4 = mul(a2, a2)
    a5 = mul(a4, a1)
    a6 = mul(a4, a2)
    a7 = mul(a4, a3)
    a8 = mul(a4, a4)
    pw = (a1, a2, a3, a4, a5, a6, a7, a8)
    rows = lax.broadcasted_iota(jnp.int32, (8, n), 0)
    tabs = []
    for s, a in ((1, a1), (2, a2), (4, a4)):
        keep = (rows + s <= 7) if reverse else (rows >= s)
        for comp in a:
            tabs.append(jnp.where(keep, jnp.broadcast_to(comp, (8, n)), 0.0))
    for c in range(2):
        q = jnp.zeros((8, n), F32)
        for r in range(8):
            e = (8 - r) if reverse else (r + 1)
            q = jnp.where(rows == r, jnp.broadcast_to(pw[e - 1][c], (8, n)), q)
        tabs.append(q)
    return tabs


def _scan_group(xr, xi, tab_ref, cr, ci, reverse):
    for t, s in enumerate((1, 2, 4)):
        pr = tab_ref[2 * t]
        pi = tab_ref[2 * t + 1]
        sh = (8 - s) if reverse else s
        sr = pltpu.roll(xr, sh, 0)
        si = pltpu.roll(xi, sh, 0)
        xr, xi = xr + pr * sr - pi * si, xi + pr * si + pi * sr
    qr = tab_ref[6]
    qi = tab_ref[7]
    return xr + qr * cr - qi * ci, xi + qr * ci + qi * cr


def _runs_load(src_ref, dst_ref, run):
    for i in range(run):
        dst_ref[8 * i:8 * i + 8, :] = src_ref[pl.ds(i, 8, stride=run), :]


def _runs_store(val, dst_ref, run):
    for i in range(run):
        dst_ref[pl.ds(i, 8, stride=run), :] = val[8 * i:8 * i + 8, :]


def _cpow2(ar, ai, log2n):
    for _ in range(log2n):
        ar, ai = ar * ar - ai * ai, 2.0 * ar * ai
    return ar, ai


def _s5_fwd(us, abar_re, abar_im, b_re, b_im, c_re, c_im, d_skip, tm):
    S = us.shape[0]
    nt = S // tm
    w = 8 * SSM_P
    run = tm // 8
    assert run & (run - 1) == 0

    def body(us_ref, ar_ref, ai_ref, br_ref, bi_ref, cr_ref, ci_ref, d_ref, str_ref, sti_ref, ys_ref,
             tab_ref, car_ref, up_ref):
        i = pl.program_id(1)

        @pl.when(i == 0)
        def _():
            car_ref[...] = jnp.zeros_like(car_ref)
            for k, t in enumerate(_scan_tables(*_cpow2(ar_ref[...], ai_ref[...], run.bit_length() - 1), False)):
                tab_ref[k] = t

        _runs_load(us_ref, up_ref, run)
        ub = up_ref[...].astype(MXU)
        str_ref[...] = _dot(ub, br_ref[0])
        sti_ref[...] = _dot(ub, bi_ref[0])
        ar = jnp.broadcast_to(ar_ref[...], (8, w))
        ai = jnp.broadcast_to(ai_ref[...], (8, w))

        def advance(k, state):
            r0 = pl.multiple_of(k * 8, 8)
            sr, si = state
            return (ar * sr - ai * si + str_ref[pl.ds(r0, 8), :], ar * si + ai * sr + sti_ref[pl.ds(r0, 8), :])

        def emit(k, state):
            r0 = pl.multiple_of(k * 8, 8)
            sr, si = advance(k, state)
            str_ref[pl.ds(r0, 8), :] = sr
            sti_ref[pl.ds(r0, 8), :] = si
            return sr, si

        zero = jnp.zeros((8, w), F32)
        er, ei = lax.fori_loop(0, run, advance, (zero, zero))
        cr, ci = car_ref[0:1, :], car_ref[1:2, :]
        tr, ti = _scan_group(er, ei, tab_ref, cr, ci, False)
        r8 = lax.broadcasted_iota(jnp.int32, (8, w), 0)
        start = (jnp.where(r8 == 0, cr, pltpu.roll(tr, 1, 0)), jnp.where(r8 == 0, ci, pltpu.roll(ti, 1, 0)))
        car_ref[0:1, :] = tr[7:8, :]
        car_ref[1:2, :] = ti[7:8, :]
        lax.fori_loop(0, run, emit, start)
        y = _dot_nt(str_ref[...].astype(MXU), cr_ref[0]) - _dot_nt(sti_ref[...].astype(MXU), ci_ref[0])
        _runs_store(y, ys_ref, run)
        ys_ref[...] += d_ref[...] * us_ref[...]

    blk = lambda: pl.BlockSpec((1, 8 * SSM_H, w), lambda j, i: (j, 0, 0))
    return pl.pallas_call(
        body, name="s5_fwd", grid=(SSM_BLK, nt),
        in_specs=[pl.BlockSpec((tm, LANES), lambda j, i: (i, j)),
                  pl.BlockSpec((1, w), lambda j, i: (0, j)), pl.BlockSpec((1, w), lambda j, i: (0, j)),
                  blk(), blk(), blk(), blk(),
                  pl.BlockSpec((1, LANES), lambda j, i: (0, j))],
        out_specs=[pl.BlockSpec((tm, w), lambda j, i: (i, j)), pl.BlockSpec((tm, w), lambda j, i: (i, j)),
                   pl.BlockSpec((tm, LANES), lambda j, i: (i, j))],
        out_shape=[_sds((S, SSM_BLK * w)), _sds((S, SSM_BLK * w)), _sds((S, SSM_W))],
        scratch_shapes=[pltpu.VMEM((8, 8, w), F32), pltpu.VMEM((8, w), F32), pltpu.VMEM((tm, LANES), F32)],
        compiler_params=_cp("parallel", "arbitrary"),
    )(*_in_hbm([us, abar_re, abar_im, b_re, b_im, c_re, c_im, d_skip]))


def _sgu_mix(vnb, ws_ref, grp):
    acc = jnp.zeros(vnb.shape, F32)
    for g in range(SGU_G):
        acc = jnp.where(grp == g, _dot(ws_ref[g], vnb), acc)
    return acc


def _mix_fwd(x, ys, uv, gl, w_glu, b_glu, w_pa, g_sgu, ws, bias_s, w_pb, w_out, g_ffn, tm):
    S = x.shape[0]

    def body(x_ref, ys_ref, uv_ref, gl_ref, wglu_ref, bglu_ref, wpa_ref, gs_ref, ws_ref, bias_ref, wpb_ref, wout_ref,
             gf_ref, yg_ref, yap_ref, sg_ref, ya_ref, yb_ref, m_ref, x1_ref, h2_ref):
        yg = _gelu(ys_ref[...])
        ygb = yg.astype(MXU)
        yg_ref[...] = ygb
        z = _dot(ygb, wglu_ref[...]) + bglu_ref[...]
        yapb = (yg * _sigmoid(z)).astype(MXU)
        yap_ref[...] = yapb
        ya = _dot(yapb, wpa_ref[...])
        ya_ref[...] = ya

        uvg = _gelu(uv_ref[...])
        u2 = uvg[:, :SGU_W]
        v2 = uvg[:, SGU_W:]
        vnb = (v2 * _rms(v2) * gs_ref[...]).astype(MXU)
        grp = lax.broadcasted_iota(jnp.int32, (CHUNK, SGU_W), 1) // SGU_D
        for c in range(tm // CHUNK):
            rs = slice(c * CHUNK, (c + 1) * CHUNK)
            mixed = _sgu_mix(vnb[rs], ws_ref, grp) + bias_ref[...]
            sg_ref[rs, :] = (u2[rs] * mixed).astype(MXU)
        yb = _dot(sg_ref[...], wpb_ref[...])
        yb_ref[...] = yb

        glv = gl_ref[...]
        m = _sigmoid(glv[:, :D_MODEL]) * ya + _sigmoid(glv[:, D_MODEL:]) * yb
        mb = m.astype(MXU)
        m_ref[...] = mb
        x1 = x_ref[...] + _dot(mb, wout_ref[...])
        x1_ref[...] = x1
        h2_ref[...] = (x1 * _rms(x1) * gf_ref[...]).astype(MXU)

    row = lambda n: pl.BlockSpec((tm, n), lambda i: (i, 0))
    return pl.pallas_call(
        body, name="mix_fwd", grid=(S // tm,),
        in_specs=[row(D_MODEL), row(SSM_W), row(2 * SGU_W), row(2 * D_MODEL),
                  _full(w_glu.shape), _full(b_glu.shape), _full(w_pa.shape), _full(g_sgu.shape), _full(ws.shape),
                  _full(bias_s.shape), _full(w_pb.shape), _full(w_out.shape), _full(g_ffn.shape)],
        out_specs=[row(SSM_W), row(SSM_W), row(SGU_W), row(D_MODEL), row(D_MODEL), row(D_MODEL), row(D_MODEL),
                   row(D_MODEL)],
        out_shape=[_sds((S, SSM_W), MXU), _sds((S, SSM_W), MXU), _sds((S, SGU_W), MXU), _sds((S, D_MODEL)),
                   _sds((S, D_MODEL)), _sds((S, D_MODEL), MXU), _sds((S, D_MODEL)), _sds((S, D_MODEL), MXU)],
        compiler_params=_cp("parallel"),
    )(*_in_hbm([x, ys, uv, gl, w_glu, b_glu, w_pa, g_sgu, ws, bias_s, w_pb, w_out, g_ffn]))


def _causal_conv3(u, prev8, cw, cb):
    tm = u.shape[0]
    w0, w1, w2 = cw[0:1], cw[1:2], cw[2:3]
    body = w0 * pltpu.roll(u, 2, 0) + w1 * pltpu.roll(u, 1, 0) + w2 * u + cb
    u8 = u[0:8, :]
    r8 = lax.broadcasted_iota(jnp.int32, u8.shape, 0)
    t1 = prev8[7:8, :]
    t0 = prev8[6:7, :]
    s1 = jnp.where(r8 == 0, t1, pltpu.roll(u8, 1, 0))
    s2 = jnp.where(r8 == 0, t0, jnp.where(r8 == 1, t1, pltpu.roll(u8, 2, 0)))
    first = w0 * s2 + w1 * s1 + w2 * u8 + cb
    return jnp.concatenate([first, body[8:tm, :]], axis=0)


def _causal_conv3_adjoint(d, next8, cw):
    tm = d.shape[0]
    w0, w1, w2 = cw[0:1], cw[1:2], cw[2:3]
    n1 = pltpu.roll(d, tm - 1, 0)
    n2 = pltpu.roll(d, tm - 2, 0)
    body = w2 * d + w1 * n1 + w0 * n2
    d8 = d[tm - 8:tm, :]
    r8 = lax.broadcasted_iota(jnp.int32, d8.shape, 0)
    h0 = next8[0:1, :]
    h1 = next8[1:2, :]
    m1 = jnp.where(r8 == 7, h0, pltpu.roll(d8, 7, 0))
    m2 = jnp.where(r8 == 6, h0, jnp.where(r8 == 7, h1, pltpu.roll(d8, 6, 0)))
    last = w2 * d8 + w1 * m1 + w0 * m2
    out = jnp.concatenate([body[0:tm - 8, :], last], axis=0)
    return out, n1, n2, h0 - d[0:1, :], h1 - d[1:2, :]


def _ffn_fwd(h2, x1, tgt, w_up, conv_w, conv_b, w_down, g_final, tm):
    S = h2.shape[0]
    nt = S // tm
    ncb = FF_NCB

    def body(h2_ref, wa_ref, wb_ref, cwa_ref, cwb_ref, cba_ref, cbb_ref, wd_ref, x1_ref, gf_ref, tgt_ref,
             up_ref, ab_ref, ff_ref, dx2_ref, dx2b_ref, loss_ref, dgf_ref, acc_ref, tail_ref):
        i = pl.program_id(0)
        cb = pl.program_id(1)

        @pl.when(i == 0)
        def _():
            tail_ref[cb] = jnp.zeros((2, 8, FF_CW), F32)

        @pl.when(jnp.logical_and(i == 0, cb == 0))
        def _():
            loss_ref[...] = jnp.zeros_like(loss_ref)
            dgf_ref[...] = jnp.zeros_like(dgf_ref)

        h2v = h2_ref[...]
        ua = _dot_nt(h2v, wa_ref[0])
        ub = _dot_nt(h2v, wb_ref[0])
        up_ref[0, 0] = ua.astype(MXU)
        up_ref[1, 0] = ub.astype(MXU)
        a = _causal_conv3(ua, tail_ref[cb, 0], cwa_ref[0], cba_ref[0])
        b = _causal_conv3(ub, tail_ref[cb, 1], cwb_ref[0], cbb_ref[0])
        tail_ref[cb, 0] = ua[tm - 8:tm, :]
        tail_ref[cb, 1] = ub[tm - 8:tm, :]
        ab_ref[0, 0] = a
        ab_ref[1, 0] = b
        ffb = (a * _sigmoid(a) * b).astype(MXU)
        ff_ref[0] = ffb
        contrib = _dot(ffb, wd_ref[...])

        @pl.when(cb == 0)
        def _():
            acc_ref[...] = contrib

        @pl.when(cb > 0)
        def _():
            acc_ref[...] += contrib

        @pl.when(cb == ncb - 1)
        def _():
            x2 = x1_ref[...] + acc_ref[...]
            r = _rms(x2)
            xn = x2 * r
            g = gf_ref[...]
            diff = xn * g - tgt_ref[...]
            loss_ref[...] += (0.5 / D_MODEL) * jnp.sum(diff * diff)
            dy = diff * (1.0 / D_MODEL)
            dgf_ref[...] += _rowsum(dy * xn)
            dx2 = _rms_bwd(dy * g, xn, r)
            dx2_ref[...] = dx2
            dx2b_ref[...] = dx2.astype(MXU)

    row = lambda n: pl.BlockSpec((tm, n), lambda i, c: (i, 0))
    gate = lambda r: pl.BlockSpec((1, r, FF_CW), lambda i, c: (c, 0, 0))
    lin = lambda r: pl.BlockSpec((1, r, FF_CW), lambda i, c: (ncb + c, 0, 0))
    return pl.pallas_call(
        body, name="ffn_fwd", grid=(nt, ncb),
        in_specs=[row(D_MODEL),
                  pl.BlockSpec((1, FF_CW, D_MODEL), lambda i, c: (c, 0, 0)),
                  pl.BlockSpec((1, FF_CW, D_MODEL), lambda i, c: (ncb + c, 0, 0)),
                  gate(3), lin(3), gate(1), lin(1),
                  pl.BlockSpec((FF_CW, D_MODEL), lambda i, c: (c, 0)),
                  row(D_MODEL), _full((1, D_MODEL)), row(D_MODEL)],
        out_specs=[pl.BlockSpec((2, 1, tm, FF_CW), lambda i, c: (0, c, i, 0)),
                   pl.BlockSpec((2, 1, tm, FF_CW), lambda i, c: (0, c, i, 0)),
                   pl.BlockSpec((1, tm, FF_CW), lambda i, c: (c, i, 0)),
                   row(D_MODEL), row(D_MODEL), _full((1, LANES)), _full((1, D_MODEL))],
        out_shape=[_sds((2, ncb, S, FF_CW), MXU), _sds((2, ncb, S, FF_CW)), _sds((ncb, S, FF_CW), MXU),
                   _sds((S, D_MODEL)), _sds((S, D_MODEL), MXU), _sds((1, LANES)), _sds((1, D_MODEL))],
        scratch_shapes=[pltpu.VMEM((tm, D_MODEL), F32), pltpu.VMEM((ncb, 2, 8, FF_CW), F32)],
        compiler_params=_cp("arbitrary", "arbitrary"),
    )(*_in_hbm([h2, w_up, w_up, conv_w, conv_w, conv_b, conv_b, w_down, x1, g_final, tgt]))


def _ffn_bwd(dx2, up, ab, x1, w_up, conv_w, w_down, g_ffn, tm):
    S = dx2.shape[0]
    nt = S // tm
    ncb = FF_NCB

    def body(dx2_ref, up_ref, ab_ref, cwa_ref, cwb_ref, wd_ref, wa_ref, wb_ref,
             x1_ref, g_ref, dup_ref, dx1_ref, dx1b_ref, dconv_ref, dg_ref, acc_ref, head_ref):
        i = pl.program_id(0)
        cb = pl.program_id(1)
        ri = nt - 1 - i

        @pl.when(i == 0)
        def _():
            head_ref[cb] = jnp.zeros((2, 8, FF_CW), F32)
            dconv_ref[cb] = jnp.zeros((8, FF_CW), F32)
            dconv_ref[ncb + cb] = jnp.zeros((8, FF_CW), F32)

        @pl.when(jnp.logical_and(i == 0, cb == 0))
        def _():
            dg_ref[...] = jnp.zeros_like(dg_ref)

        dff = _dot_nt(dx2_ref[...].astype(MXU), wd_ref[...])
        a = ab_ref[0, 0]
        b = ab_ref[1, 0]
        sa = _sigmoid(a)
        silu = a * sa
        da = (dff * b) * (sa + silu * (1.0 - sa))
        db = dff * silu
        dps = []
        for half, slot, d, cw_ref in ((0, cb, da, cwa_ref), (1, ncb + cb, db, cwb_ref)):
            dp, n1, n2, fix0, fix1 = _causal_conv3_adjoint(d, head_ref[cb, half], cw_ref[0])
            head_ref[cb, half] = d[0:8, :]
            dpb16 = dp.astype(MXU)
            dup_ref[half, 0] = dpb16
            dps.append(dpb16)
            u = up_ref[half, 0].astype(F32)
            u_last = u[tm - 1:tm, :]
            dconv_ref[slot, 0:1, :] += _rowsum(n2 * u) + fix0 * u[tm - 2:tm - 1, :] + fix1 * u_last
            dconv_ref[slot, 1:2, :] += _rowsum(n1 * u) + fix0 * u_last
            dconv_ref[slot, 2:3, :] += _rowsum(d * u)
            dconv_ref[slot, 3:4, :] += _rowsum(d)
        contrib = _dot(dps[0], wa_ref[0]) + _dot(dps[1], wb_ref[0])

        @pl.when(cb == 0)
        def _():
            acc_ref[...] = contrib

        @pl.when(cb > 0)
        def _():
            acc_ref[...] += contrib

        @pl.when(cb == ncb - 1)
        def _():
            x1v = x1_ref[...]
            r = _rms(x1v)
            xn = x1v * r
            dh2 = acc_ref[...]
            dg_ref[...] += _rowsum(dh2 * xn)
            dx1 = dx2_ref[...] + _rms_bwd(dh2 * g_ref[...], xn, r)
            dx1_ref[...] = dx1
            dx1b_ref[...] = dx1.astype(MXU)

    row = lambda n: pl.BlockSpec((tm, n), lambda i, c: (nt - 1 - i, 0))
    colb = lambda: pl.BlockSpec((2, 1, tm, FF_CW), lambda i, c: (0, c, nt - 1 - i, 0))
    gate = lambda r: pl.BlockSpec((1, r, FF_CW), lambda i, c: (c, 0, 0))
    lin = lambda r: pl.BlockSpec((1, r, FF_CW), lambda i, c: (ncb + c, 0, 0))
    return pl.pallas_call(
        body, name="ffn_bwd", grid=(nt, ncb),
        in_specs=[row(D_MODEL), colb(), colb(), gate(3), lin(3),
                  pl.BlockSpec((FF_CW, D_MODEL), lambda i, c: (c, 0)),
                  pl.BlockSpec((1, FF_CW, D_MODEL), lambda i, c: (c, 0, 0)),
                  pl.BlockSpec((1, FF_CW, D_MODEL), lambda i, c: (ncb + c, 0, 0)),
                  row(D_MODEL), _full((1, D_MODEL))],
        out_specs=[colb(), row(D_MODEL), row(D_MODEL), _full((2 * ncb, 8, FF_CW)), _full((1, D_MODEL))],
        out_shape=[_sds((2, ncb, S, FF_CW), MXU), _sds((S, D_MODEL)), _sds((S, D_MODEL), MXU), _sds((2 * ncb, 8, FF_CW)),
                   _sds((1, D_MODEL))],
        scratch_shapes=[pltpu.VMEM((tm, D_MODEL), F32), pltpu.VMEM((ncb, 2, 8, FF_CW), F32)],
        compiler_params=_cp("arbitrary", "arbitrary"),
    )(*_in_hbm([dx2, up, ab, conv_w, conv_w, w_down, w_up, w_up, x1, g_ffn]))


def _mix_bwd(dx1, gl, ya, yb, ys, uv, w_out, w_pa, w_pb, w_glu, b_glu, g_sgu, ws, ws_t, bias_s, tm):
    S = dx1.shape[0]

    def body(dx1_ref, gl_ref, ya_ref, yb_ref, ys_ref, uv_ref, wout_ref, wpa_ref, wpb_ref, wglu_ref, bglu_ref, gs_ref,
             ws_ref, wst_ref, bias_ref,
             dgl_ref, dya_ref, dyb_ref, dz_ref, dys_ref, duv_ref, dbglu_ref, dgs_ref, dws_ref, dbs_ref,
             du2_ref, dvn_ref):
        i = pl.program_id(0)

        @pl.when(i == 0)
        def _():
            dbglu_ref[...] = jnp.zeros_like(dbglu_ref)
            dgs_ref[...] = jnp.zeros_like(dgs_ref)
            dws_ref[...] = jnp.zeros_like(dws_ref)
            dbs_ref[...] = jnp.zeros_like(dbs_ref)

        dm = _dot_nt(dx1_ref[...].astype(MXU), wout_ref[...])
        glv = gl_ref[...]
        ga = _sigmoid(glv[:, :D_MODEL])
        gb = _sigmoid(glv[:, D_MODEL:])
        dgl_ref[:, :D_MODEL] = (dm * ya_ref[...] * ga * (1.0 - ga)).astype(MXU)
        dgl_ref[:, D_MODEL:] = (dm * yb_ref[...] * gb * (1.0 - gb)).astype(MXU)
        dyab = (dm * ga).astype(MXU)
        dybb = (dm * gb).astype(MXU)
        dya_ref[...] = dyab
        dyb_ref[...] = dybb

        dyap = _dot_nt(dyab, wpa_ref[...])
        yg, dgelu = _gelu_and_grad(ys_ref[...])
        sz = _sigmoid(_dot(yg.astype(MXU), wglu_ref[...]) + bglu_ref[...])
        dz = dyap * yg * sz * (1.0 - sz)
        dzb = dz.astype(MXU)
        dz_ref[...] = dzb
        dbglu_ref[...] += _rowsum(dz)
        dys_ref[...] = (dyap * sz + _dot_nt(dzb, wglu_ref[...])) * dgelu

        dsg = _dot_nt(dybb, wpb_ref[...])
        uvg, duvg = _gelu_and_grad(uv_ref[...])
        u2 = uvg[:, :SGU_W]
        v2 = uvg[:, SGU_W:]
        rv = _rms(v2)
        vhat = v2 * rv
        gs = gs_ref[...]
        vnb = (vhat * gs).astype(MXU)
        grp = lax.broadcasted_iota(jnp.int32, (CHUNK, SGU_W), 1) // SGU_D
        tril = (lax.broadcasted_iota(jnp.int32, (CHUNK, CHUNK), 0)
                >= lax.broadcasted_iota(jnp.int32, (CHUNK, CHUNK), 1))
        for c in range(tm // CHUNK):
            rs = slice(c * CHUNK, (c + 1) * CHUNK)
            vc = vnb[rs]
            mixed = _sgu_mix(vc, ws_ref, grp) + bias_ref[...]
            dsg_c = dsg[rs]
            du2_ref[rs, :] = dsg_c * mixed
            dmx = dsg_c * u2[rs]
            dbs_ref[...] += dmx
            dmb = dmx.astype(MXU)
            dvn_ref[rs, :] = _sgu_mix(dmb, wst_ref, grp)
            for g in range(SGU_G):
                part = _dot_nt(jnp.where(grp == g, dmb, jnp.zeros((), MXU)), vc)
                dws_ref[g] += jnp.where(tril, part, 0.0)
        dvn = dvn_ref[...]
        dgs_ref[...] += _rowsum(dvn * vhat)
        dv2 = _rms_bwd(dvn * gs, vhat, rv)
        duv_ref[:, :SGU_W] = (du2_ref[...] * duvg[:, :SGU_W]).astype(MXU)
        duv_ref[:, SGU_W:] = (dv2 * duvg[:, SGU_W:]).astype(MXU)

    row = lambda n: pl.BlockSpec((tm, n), lambda i: (i, 0))
    return pl.pallas_call(
        body, name="mix_bwd", grid=(S // tm,),
        in_specs=[row(D_MODEL), row(2 * D_MODEL), row(D_MODEL), row(D_MODEL), row(SSM_W), row(2 * SGU_W),
                  _full(w_out.shape), _full(w_pa.shape), _full(w_pb.shape), _full(w_glu.shape), _full(b_glu.shape),
                  _full(g_sgu.shape), _full(ws.shape), _full(ws_t.shape), _full(bias_s.shape)],
        out_specs=[row(2 * D_MODEL), row(D_MODEL), row(D_MODEL), row(SSM_W), row(SSM_W), row(2 * SGU_W),
                   _full((1, SSM_W)), _full((1, SGU_W)), _full((SGU_G, CHUNK, CHUNK)), _full((CHUNK, SGU_W))],
        out_shape=[_sds((S, 2 * D_MODEL), MXU), _sds((S, D_MODEL), MXU), _sds((S, D_MODEL), MXU), _sds((S, SSM_W), MXU),
                   _sds((S, SSM_W)), _sds((S, 2 * SGU_W), MXU),
                   _sds((1, SSM_W)), _sds((1, SGU_W)), _sds((SGU_G, CHUNK, CHUNK)), _sds((CHUNK, SGU_W))],
        scratch_shapes=[pltpu.VMEM((tm, SGU_W), F32), pltpu.VMEM((tm, SGU_W), F32)],
        compiler_params=_cp("arbitrary"),
    )(*_in_hbm([dx1, gl, ya, yb, ys, uv, w_out, w_pa, w_pb, w_glu, b_glu, g_sgu, ws, ws_t, bias_s]))


def _s5_bwd(dys, us, st_re, st_im, abar_re, abar_im, b_re, b_im, c_re, c_im, d_skip, tm):
    S = us.shape[0]
    nt = S // tm
    w = 8 * SSM_P
    hb = tm // 8
    run = tm // 8
    assert run & (run - 1) == 0

    def body(dys_ref, us_ref, str_ref, sti_ref, hr_ref, hi_ref, ar_ref, ai_ref, br_ref, bi_ref, cr_ref, ci_ref, d_ref,
             dus_ref, dab_ref, dd_ref, dbr_ref, dbi_ref, dcr_ref, dci_ref,
             tab_ref, car_ref, gr_ref, gi_ref, dyp_ref, up_ref, dun_ref):
        i = pl.program_id(1)
        ri = nt - 1 - i

        @pl.when(i == 0)
        def _():
            car_ref[...] = jnp.zeros_like(car_ref)
            for k, t in enumerate(_scan_tables(*_cpow2(ar_ref[...], -ai_ref[...], run.bit_length() - 1), True)):
                tab_ref[k] = t
            for r in (dab_ref, dd_ref, dbr_ref, dbi_ref, dcr_ref, dci_ref):
                r[...] = jnp.zeros_like(r)

        _runs_load(dys_ref, dyp_ref, run)
        _runs_load(us_ref, up_ref, run)
        dyb = dyp_ref[...].astype(MXU)
        gr_ref[...] = _dot(dyb, cr_ref[0])
        gi_ref[...] = -_dot(dyb, ci_ref[0])
        ar = jnp.broadcast_to(ar_ref[...], (8, w))
        ai = jnp.broadcast_to(-ai_ref[...], (8, w))

        def advance(kk, state):
            r0 = pl.multiple_of((run - 1 - kk) * 8, 8)
            gr, gi = state
            return (ar * gr - ai * gi + gr_ref[pl.ds(r0, 8), :], ar * gi + ai * gr + gi_ref[pl.ds(r0, 8), :])

        def emit(kk, state):
            r0 = pl.multiple_of((run - 1 - kk) * 8, 8)
            gr, gi = advance(kk, state)
            gr_ref[pl.ds(r0, 8), :] = gr
            gi_ref[pl.ds(r0, 8), :] = gi
            return gr, gi

        zero = jnp.zeros((8, w), F32)
        er, ei = lax.fori_loop(0, run, advance, (zero, zero))
        cr, ci = car_ref[0:1, :], car_ref[1:2, :]
        tr, ti = _scan_group(er, ei, tab_ref, cr, ci, True)
        r8 = lax.broadcasted_iota(jnp.int32, (8, w), 0)
        start = (jnp.where(r8 == 7, cr, pltpu.roll(tr, 7, 0)), jnp.where(r8 == 7, ci, pltpu.roll(ti, 7, 0)))
        car_ref[0:1, :] = tr[0:1, :]
        car_ref[1:2, :] = ti[0:1, :]
        lax.fori_loop(0, run, emit, start)

        gsr = gr_ref[...]
        gsi = gi_ref[...]
        sr = str_ref[...]
        si = sti_ref[...]
        first = ri == 0

        def previous(s, halo_ref):
            head = jnp.where(r8 == 0, jnp.where(first, 0.0, halo_ref[7:8, :]), pltpu.roll(s[tm - 8:tm, :], 1, 0))
            return jnp.concatenate([head, s[0:tm - 8, :]], axis=0)

        spr = previous(sr, hr_ref)
        spi = previous(si, hi_ref)
        dab_ref[0, 0:1, :] += _rowsum(gsr * spr + gsi * spi)
        dab_ref[0, 1:2, :] += _rowsum(gsi * spr - gsr * spi)

        gbr = gsr.astype(MXU)
        gbi = gsi.astype(MXU)
        _runs_store(_dot_nt(gbr, br_ref[0]) + _dot_nt(gbi, bi_ref[0]), dun_ref, run)
        dys_v = dys_ref[...]
        dus_ref[...] = (dun_ref[...] + d_ref[...] * dys_v).astype(MXU)
        dd_ref[0, 0:1, :] += _rowsum(dys_v * us_ref[...])
        ub = up_ref[...].astype(MXU)
        dbr_ref[0] += _dot_tn(ub, gbr)
        dbi_ref[0] += _dot_tn(ub, gbi)
        dcr_ref[0] += _dot_tn(dyb, sr.astype(MXU))
        dci_ref[0] -= _dot_tn(dyb, si.astype(MXU))

    blk = lambda: pl.BlockSpec((1, 8 * SSM_H, w), lambda j, i: (j, 0, 0))
    rowl = lambda: pl.BlockSpec((tm, LANES), lambda j, i: (nt - 1 - i, j))
    roww = lambda: pl.BlockSpec((tm, w), lambda j, i: (nt - 1 - i, j))
    halo = lambda: pl.BlockSpec((8, w), lambda j, i: (jnp.maximum((nt - 1 - i) * hb - 1, 0), j))
    return pl.pallas_call(
        body, name="s5_bwd", grid=(SSM_BLK, nt),
        in_specs=[rowl(), rowl(), roww(), roww(), halo(), halo(),
                  pl.BlockSpec((1, w), lambda j, i: (0, j)), pl.BlockSpec((1, w), lambda j, i: (0, j)),
                  blk(), blk(), blk(), blk(),
                  pl.BlockSpec((1, LANES), lambda j, i: (0, j))],
        out_specs=[rowl(),
                   pl.BlockSpec((1, 8, w), lambda j, i: (j, 0, 0)), pl.BlockSpec((1, 8, LANES), lambda j, i: (j, 0, 0)),
                   blk(), blk(), blk(), blk()],
        out_shape=[_sds((S, SSM_W), MXU), _sds((SSM_BLK, 8, w)), _sds((SSM_BLK, 8, LANES)),
                   _sds((SSM_BLK, 8 * SSM_H, w)), _sds((SSM_BLK, 8 * SSM_H, w)),
                   _sds((SSM_BLK, 8 * SSM_H, w)), _sds((SSM_BLK, 8 * SSM_H, w))],
        scratch_shapes=[pltpu.VMEM((8, 8, w), F32), pltpu.VMEM((8, w), F32),
                        pltpu.VMEM((tm, w), F32), pltpu.VMEM((tm, w), F32),
                        pltpu.VMEM((tm, LANES), F32), pltpu.VMEM((tm, LANES), F32), pltpu.VMEM((tm, LANES), F32)],
        compiler_params=_cp("parallel", "arbitrary"),
    )(*_in_hbm([dys, us, st_re, st_im, st_re, st_im, abar_re, abar_im, b_re, b_im, c_re, c_im, d_skip]))


def _in_bwd(dus, duv, dgl, dx1, x, g_mix, w_in, tm):
    S = x.shape[0]

    def body(dus_ref, duv_ref, dgl_ref, dx1_ref, x_ref, g_ref, w_ref, gx_ref, dg_ref):
        @pl.when(pl.program_id(0) == 0)
        def _():
            dg_ref[...] = jnp.zeros_like(dg_ref)

        dh = (_dot(dus_ref[...], w_ref[0:SSM_W, :])
              + _dot(duv_ref[...], w_ref[SSM_W:SSM_W + 2 * SGU_W, :])
              + _dot(dgl_ref[...], w_ref[SSM_W + 2 * SGU_W:, :]))
        xv = x_ref[...]
        r = _rms(xv)
        xn = xv * r
        dg_ref[...] += _rowsum(dh * xn)
        gx_ref[...] = dx1_ref[...] + _rms_bwd(dh * g_ref[...], xn, r)

    row = lambda n: pl.BlockSpec((tm, n), lambda i: (i, 0))
    return pl.pallas_call(
        body, name="in_bwd", grid=(S // tm,),
        in_specs=[row(SSM_W), row(2 * SGU_W), row(2 * D_MODEL), row(D_MODEL), row(D_MODEL), _full((1, D_MODEL)),
                  _full(w_in.shape)],
        out_specs=[row(D_MODEL), _full((1, D_MODEL))],
        out_shape=[_sds((S, D_MODEL)), _sds((1, D_MODEL))],
        compiler_params=_cp("arbitrary"),
    )(*_in_hbm([dus, duv, dgl, dx1, x, g_mix, w_in]))


def _pick(n, cands):
    for c in cands:
        if n % c == 0:
            return c
    return n


def _wgrad_split(a, b, nsplit, tk, name):
    S, K = a.shape
    N = b.shape[1]
    c = N // nsplit

    def body(a_ref, b_ref, o_ref):
        prod = _dot_tn(a_ref[...], b_ref[...])
        for d in range(nsplit):
            o_ref[d] = prod[:, c * d:c * (d + 1)].astype(MXU)

    return pl.pallas_call(
        body, name=name, grid=(K // tk,),
        in_specs=[pl.BlockSpec((S, tk), lambda k: (0, k)), _full((S, N))],
        out_specs=pl.BlockSpec((nsplit, tk, c), lambda k: (0, k, 0)),
        out_shape=_sds((nsplit, K, c), MXU),
        compiler_params=_cp("parallel"),
    )(*_in_hbm([a, b]))


def _wgrad_in_t(dps, h1, name):
    S, K = h1.shape
    cw = 512
    counts = [b.shape[1] // cw for b in dps]
    starts = [sum(counts[:i]) for i in range(len(dps))]
    nblk = sum(counts)

    def body(*refs):
        b_refs = refs[:len(dps)]
        h_ref, o_ref = refs[len(dps):]
        j = pl.program_id(0)
        for b_ref, st, cnt in zip(b_refs, starts, counts):
            @pl.when(jnp.logical_and(j >= st, j < st + cnt))
            def _():
                o_ref[...] = _dot_tn(b_ref[...], h_ref[...]).astype(MXU)

    def src_spec(st, cnt):
        return pl.BlockSpec((S, cw), lambda j: (0, jnp.clip(j - st, 0, cnt - 1)))

    return pl.pallas_call(
        body, name=name, grid=(nblk,),
        in_specs=[src_spec(st, cnt) for st, cnt in zip(starts, counts)] + [_full((S, K))],
        out_specs=pl.BlockSpec((cw, K), lambda j: (j, 0)),
        out_shape=_sds((nblk * cw, K), MXU),
        compiler_params=_cp("arbitrary"),
    )(*_in_hbm([*dps, h1]))


def _wgrad_blk(a3, b3, nblk, a_of, b_of, name):
    S, K = a3.shape[1:]
    N = b3.shape[2]

    def body(a_ref, b_ref, o_ref):
        o_ref[0] = _dot_tn(a_ref[0], b_ref[0]).astype(MXU)

    return pl.pallas_call(
        body, name=name, grid=(nblk,),
        in_specs=[pl.BlockSpec((1, S, K), lambda b: (a_of(b), 0, 0)),
                  pl.BlockSpec((1, S, N), lambda b: (b_of(b), 0, 0))],
        out_specs=pl.BlockSpec((1, K, N), lambda b: (b, 0, 0)),
        out_shape=_sds((nblk, K, N), MXU),
        compiler_params=_cp("parallel"),
    )(*_in_hbm([a3, b3]))


def _assemble_cols(blocks_list, name):
    def body(*refs):
        n = len(blocks_list)
        for b_ref, o_ref in zip(refs[:n], refs[n:]):
            c = b_ref.shape[2]
            for d in range(N_DEV):
                o_ref[:, c * d:c * (d + 1)] = b_ref[d]

    outs = [_sds((b.shape[1], N_DEV * b.shape[2]), b.dtype) for b in blocks_list]
    return pl.pallas_call(
        body, name=name, grid=(1,), in_specs=[_full(b.shape) for b in blocks_list],
        out_specs=[_full(o.shape) for o in outs], out_shape=outs, compiler_params=_cp("arbitrary"),
    )(*_in_hbm(blocks_list))


def _tile(S, want):
    return want if S % want == 0 else S


def _local_step(x, tgt, p, mixer_relay, mixer_weights, ffn_weights, grads_out):
    S = x.shape[0]
    tm = _tile(S, 256)
    tl = _tile(S, 512)

    rep = lambda a: jnp.repeat(a, SSM_H, axis=0)
    are = rep(p["a_re"])
    aim = rep(p["a_im"])
    ldt = jnp.broadcast_to(rep(p["log_dt"].reshape(SSM_G, 1)), are.shape)
    br_t = p["b_re_t"].reshape(are.shape)
    bi_t = p["b_im_t"].reshape(are.shape)
    abr, abi, bbr, bbi = _s5_params_fwd(are, aim, ldt, br_t, bi_t)
    head = lambda a: a.reshape(SSM_G, SSM_H, SSM_P)[:, 0, :].reshape(1, SSM_G * SSM_P)
    abar_re, abar_im = head(abr), head(abi)
    bd_br = _blockdiag(bbr).astype(MXU)
    bd_bi = _blockdiag(bbi).astype(MXU)
    bd_cr = _blockdiag(p["c_re"].reshape(are.shape)).astype(MXU)
    bd_ci = _blockdiag(p["c_im"].reshape(are.shape)).astype(MXU)
    d_skip = p["d_skip"].reshape(1, SSM_W)

    tril = jnp.tril(jnp.ones((CHUNK, CHUNK), dtype=bool))
    ws = jnp.where(tril[None], p["w_s"], 0.0)
    ws_b = ws.astype(MXU)
    ws_t = ws.transpose(0, 2, 1).astype(MXU)
    bias_s = jnp.repeat(p["b_s"].T, SGU_D, axis=1)

    g_mix = p["g_mix"].reshape(1, D_MODEL)
    g_ffn = p["g_ffn"].reshape(1, D_MODEL)
    g_final = p["g_final"].reshape(1, D_MODEL)
    g_sgu = p["g_sgu"].reshape(1, SGU_W)
    b_glu = p["b_glu"].reshape(1, SSM_W)
    conv_b = p["conv_b"].reshape(N_DEV, 1, FF_CW)

    h1, us, uv, gl = _in_fwd(x, g_mix, p["w_in_t"], tm)
    token = mixer_relay(us)
    st_re, st_im, ys = _s5_fwd(us, abar_re, abar_im, bd_br, bd_bi, bd_cr, bd_ci, d_skip + token[0:1, 0:1], tl)
    p = dict(p, **mixer_weights(ys))
    yg, yap, sg, ya, yb, m, x1, h2 = _mix_fwd(x, ys, uv, gl, p["w_glu"], b_glu, p["w_proj_a"], g_sgu, ws_b, bias_s,
                                              p["w_proj_b"], p["w_out"], g_ffn, tm)
    w_up, conv_w, w_down = ffn_weights(h2)
    up, ab, ff, dx2, dx2b, loss, dg_final = _ffn_fwd(h2, x1, tgt, w_up, conv_w, conv_b, w_down, g_final, tl)

    dup, dx1, dx1b, dconv, dg_ffn = _ffn_bwd(dx2, up, ab, x1, w_up, conv_w, w_down, g_ffn, tl)
    rows8 = lambda g: g.reshape(N_DEV, g.shape[1] // N_DEV, g.shape[2])
    g_up = _wgrad_blk(dup.reshape(N_DEV, S, FF_CW), h2[None], N_DEV, lambda b: b, lambda b: 0, "wgrad_up")
    g_down = _wgrad_blk(ff, dx2b[None], FF_NCB, lambda b: b, lambda b: 0, "wgrad_down").reshape(
        N_DEV, D_FF // N_DEV, D_MODEL)
    token = grads_out(("w_up", "w_down"), (g_up, g_down))
    dgl, dya, dyb, dz, dys, duv, db_glu, dg_sgu, dws, dbs = _mix_bwd(
        dx1, gl, ya, yb, ys, uv, p["w_out"], p["w_proj_a"], p["w_proj_b"], p["w_glu"], b_glu + token[0:1, 0:1], g_sgu,
        ws_b, ws_t, bias_s, tm)
    token = grads_out(("w_glu", "w_proj_a", "w_proj_b", "w_out"),
                      (rows8(_wgrad_split(yg, dz, 1, SSM_W, "wgrad_glu")),
                       _wgrad_split(yap, dya, N_DEV, SSM_W, "wgrad_pa"),
                       _wgrad_split(sg, dyb, N_DEV, SGU_W, "wgrad_pb"),
                       rows8(_wgrad_split(m, dx1b, 1, 512, "wgrad_out"))))
    dus, dab, dd, dbbr, dbbi, dcr, dci = _s5_bwd(dys, us, st_re, st_im, abar_re, abar_im, bd_br, bd_bi, bd_cr, bd_ci,
                                                 d_skip + token[0:1, 0:1], tl)
    g_in = _wgrad_in_t([dus, duv, dgl], h1, "wgrad_in")
    token = grads_out(("w_in",), (g_in.reshape(N_DEV, g_in.shape[0] // N_DEV, D_MODEL),))
    grad_x, dg_mix = _in_bwd(dus, duv, dgl, dx1, x, g_mix + token[0:1, 0:1], p["w_in_t"], tm)

    spread = lambda v: jnp.repeat(v.reshape(SSM_G, SSM_P), SSM_H, axis=0) * (1.0 / SSM_H)
    dabr = spread(dab[:, 0, :])
    dabi = spread(dab[:, 1, :])
    dare, daim, dldt, dbr_t, dbi_t = _s5_params_bwd(are, aim, ldt, br_t, bi_t, dabr, dabi,
                                                    _unblockdiag(dbbr), _unblockdiag(dbbi))
    fold = lambda a: a.reshape(SSM_G, SSM_H, SSM_P).sum(axis=1)

    grads = {
        "g_mix": dg_mix,
        "a_re": fold(dare), "a_im": fold(daim), "log_dt": fold(dldt).sum(axis=1),
        "b_re": dbr_t, "b_im": dbi_t,
        "c_re": _unblockdiag(dcr).reshape(SSM_G, SSM_H, SSM_P),
        "c_im": _unblockdiag(dci).reshape(SSM_G, SSM_H, SSM_P),
        "d_skip": dd[:, 0, :].reshape(SSM_W),
        "b_glu": db_glu,
        "g_sgu": dg_sgu,
        "w_s": dws,
        "b_s": dbs.reshape(CHUNK, SGU_G, SGU_D).sum(axis=-1).T,
        "g_ffn": dg_ffn,
        "conv_w": dconv[:, 0:3, :],
        "conv_b": dconv[:, 3, :].reshape(2 * D_FF),
        "g_final": dg_final,
    }
    return loss, grad_x, grads


_ANY = pl.BlockSpec(memory_space=pl.ANY)
_MESH = pl.DeviceIdType.MESH


def _allgather(shards, dtypes, name, cast_only=()):
    n = len(shards)
    e = len(cast_only)

    def body(*refs):
        in_refs, extra_in = refs[:n], refs[n:n + e]
        out_refs, extra_out = refs[n + e:2 * n + e], refs[2 * n + e:2 * n + 2 * e]
        stage = refs[2 * n + 2 * e:3 * n + 2 * e]
        send_sems, recv_sems, local_sems = refs[3 * n + 2 * e:]
        for a in range(n):
            stage[a][...] = in_refs[a][...].astype(dtypes[a])
        for i in range(e):
            extra_out[i][...] = extra_in[i][...].astype(MXU)
        x, y, c = lax.axis_index("x"), lax.axis_index("y"), lax.axis_index("c")
        me, sibling = (x, y, c), (x, y, 1 - c)
        chips = [(1 - x, y), (x, 1 - y), (1 - x, 1 - y)]

        def slot(a, px, py, pc):
            return out_refs[a].at[4 * px + 2 * py + pc]

        def copy(a, k, block, to, src=None):
            return pltpu.make_async_remote_copy(
                src_ref=slot(a, *block) if src is None else src, dst_ref=slot(a, *block),
                send_sem=send_sems.at[a, k], recv_sem=recv_sems.at[a, k], device_id=to, device_id_type=_MESH)

        mine = [pltpu.make_async_copy(stage[a], slot(a, *me), local_sems.at[a]) for a in range(n)]
        for cp in mine:
            cp.start()
        first = []
        for j, chip in enumerate(chips):
            first += [copy(a, 1 + j, me, (*chip, c), src=stage[a]) for a in range(n)]
        first += [copy(a, 0, me, sibling, src=stage[a]) for a in range(n)]
        for cp in first:
            cp.start()
        passed = []
        for j, chip in enumerate(chips):
            for a in range(n):
                copy(a, 1 + j, (*chip, c), me).wait_recv()
                fwd = copy(a, 4 + j, (*chip, c), sibling)
                fwd.start()
                passed.append(fwd)
        for a in range(n):
            copy(a, 0, sibling, me).wait_recv()
        for j, chip in enumerate(chips):
            for a in range(n):
                copy(a, 4 + j, (*chip, 1 - c), me).wait_recv()
        for cp in first + passed:
            cp.wait_send()
        for cp in mine:
            cp.wait()

    res = pl.pallas_call(
        body, name=name, grid=(1,), in_specs=[_full(s.shape) for s in list(shards) + list(cast_only)],
        out_specs=[_ANY] * n + [_full(s.shape) for s in cast_only],
        out_shape=[_sds((N_DEV,) + s.shape, dt) for s, dt in zip(shards, dtypes)]
                  + [_sds(s.shape, MXU) for s in cast_only],
        scratch_shapes=[pltpu.VMEM(s.shape, dt) for s, dt in zip(shards, dtypes)]
                       + [pltpu.SemaphoreType.DMA((n, 7)), pltpu.SemaphoreType.DMA((n, 7)), pltpu.SemaphoreType.DMA((n,))],
        compiler_params=pltpu.CompilerParams(vmem_limit_bytes=VMEM_LIMIT),
    )(*_in_hbm([*shards, *cast_only]))
    return res[:n], res[n:]


def _all_to_all(sends, name):
    n = len(sends)

    def body(*refs):
        send_refs, recv_refs = refs[:n], refs[n:2 * n]
        send_sems, recv_sems, local_sems = refs[2 * n:]
        x, y, c = lax.axis_index("x"), lax.axis_index("y"), lax.axis_index("c")
        me = 4 * x + 2 * y + c
        mine = [pltpu.make_async_copy(send_refs[a].at[me], recv_refs[a].at[me], local_sems.at[a]) for a in range(n)]
        for cp in mine:
            cp.start()
        copies = []
        for k in (2, 4, 6, 3, 5, 7, 1):
            px = 1 - x if k & 4 else x
            py = 1 - y if k & 2 else y
            pc = 1 - c if k & 1 else c
            peer = 4 * px + 2 * py + pc
            for a in range(n):
                sems = dict(send_sem=send_sems.at[a, k - 1], recv_sem=recv_sems.at[a, k - 1],
                            device_id=(px, py, pc), device_id_type=_MESH)
                cp = pltpu.make_async_remote_copy(src_ref=send_refs[a].at[peer], dst_ref=recv_refs[a].at[me], **sems)
                cp.start()
                landing = pltpu.make_async_remote_copy(src_ref=send_refs[a].at[peer], dst_ref=recv_refs[a].at[peer],
                                                       **sems)
                copies.append((cp, landing))
        for _, landing in copies:
            landing.wait_recv()
        for cp, _ in copies:
            cp.wait_send()
        for cp in mine:
            cp.wait()

    return pl.pallas_call(
        body, name=name, in_specs=[_ANY] * n, out_specs=[_ANY] * n,
        out_shape=[_sds(s.shape, s.dtype) for s in sends],
        scratch_shapes=[pltpu.SemaphoreType.DMA((n, 7)), pltpu.SemaphoreType.DMA((n, 7)), pltpu.SemaphoreType.DMA((n,))],
    )(*sends)


_HBM = pl.BlockSpec(memory_space=pltpu.HBM)
_SEM = pl.BlockSpec(memory_space=pltpu.SEMAPHORE)
_EFFECT = pltpu.SideEffectType.DATAFLOW_SIDE_EFFECTING
_PEER_ORDER = (2, 4, 6, 3, 5, 7, 1)


def _peer(k):
    x, y, c = lax.axis_index("x"), lax.axis_index("y"), lax.axis_index("c")
    px = 1 - x if k & 4 else x
    py = 1 - y if k & 2 else y
    pc = 1 - c if k & 1 else c
    return (px, py, pc), 4 * px + 2 * py + pc


_SAME_CORE_AND_SIBLING = (2, 4, 6, 1)


def _push_start(srcs, lands, slotted, name, peers=_PEER_ORDER):
    n = len(srcs)

    def body(*refs):
        src_refs, land_refs = refs[:n], refs[n:2 * n]
        send_sems, recv_sems, token = refs[2 * n], refs[2 * n + 1], refs[-1]
        me = 4 * lax.axis_index("x") + 2 * lax.axis_index("y") + lax.axis_index("c")
        for k in peers:
            dev, peer = _peer(k)
            for a in range(n):
                pltpu.make_async_remote_copy(
                    src_ref=src_refs[a].at[peer] if slotted else src_refs[a], dst_ref=land_refs[a].at[me],
                    send_sem=send_sems.at[7 * a + k - 1], recv_sem=recv_sems.at[7 * a + k - 1],
                    device_id=dev, device_id_type=_MESH).start()
        token[...] = jnp.zeros_like(token)

    bufs = list(srcs) + list(lands)
    res = pl.pallas_call(
        body, name=name, in_specs=[_HBM] * (2 * n),
        out_specs=(_SEM, _SEM, *[_HBM] * (2 * n), pl.BlockSpec(memory_space=pltpu.VMEM)),
        out_shape=(pltpu.SemaphoreType.DMA((7 * n,)), pltpu.SemaphoreType.DMA((7 * n,)),
                   *[pltpu.HBM(b.shape, b.dtype) for b in bufs], _sds((8, LANES))),
        input_output_aliases={i: 2 + i for i in range(2 * n)},
        compiler_params=pltpu.CompilerParams(has_side_effects=_EFFECT),
    )(*[pltpu.with_memory_space_constraint(b, pltpu.HBM) for b in bufs])
    return res[0], res[1], res[2:2 + n], res[2 + n:2 + 2 * n], res[-1]


def _push_wait(send_sems, recv_sems, srcs, lands, slotted, after, name, peers=_PEER_ORDER):
    n = len(srcs)

    def body(*refs):
        src_refs, land_refs = refs[:n], refs[n:2 * n]
        send_sems, recv_sems = refs[2 * n], refs[2 * n + 1]
        for k in peers:
            dev, peer = _peer(k)
            for a in range(n):
                cp = pltpu.make_async_remote_copy(
                    src_ref=src_refs[a].at[peer] if slotted else src_refs[a], dst_ref=land_refs[a].at[peer],
                    send_sem=send_sems.at[7 * a + k - 1], recv_sem=recv_sems.at[7 * a + k - 1],
                    device_id=dev, device_id_type=_MESH)
                cp.wait_send()
                cp.wait_recv()

    bufs = list(srcs) + list(lands)
    res = pl.pallas_call(
        body, name=name, in_specs=[_HBM] * (2 * n) + [_SEM, _SEM] + [_ANY] * len(after), out_specs=[_HBM] * (2 * n),
        out_shape=[pltpu.HBM(b.shape, b.dtype) for b in bufs],
        input_output_aliases={i: i for i in range(2 * n)},
        compiler_params=pltpu.CompilerParams(has_side_effects=_EFFECT),
    )(*bufs, send_sems, recv_sems, *after)
    return res[n:]


def _other_chips():
    x, y = lax.axis_index("x"), lax.axis_index("y")
    return ((1 - x, y), (x, 1 - y), (1 - x, 1 - y))


def _relay_start(lands, name):
    n = len(lands)

    def body(*refs):
        land_refs = refs[:n]
        send_sems, recv_sems, token = refs[n], refs[n + 1], refs[-1]
        x, y, c = lax.axis_index("x"), lax.axis_index("y"), lax.axis_index("c")
        for j, (px, py) in enumerate(_other_chips()):
            slot = 4 * px + 2 * py + c
            for a in range(n):
                pltpu.make_async_remote_copy(
                    src_ref=land_refs[a].at[slot], dst_ref=land_refs[a].at[slot],
                    send_sem=send_sems.at[3 * a + j], recv_sem=recv_sems.at[3 * a + j],
                    device_id=(x, y, 1 - c), device_id_type=_MESH).start()
        token[...] = jnp.zeros_like(token)

    res = pl.pallas_call(
        body, name=name, in_specs=[_HBM] * n,
        out_specs=(_SEM, _SEM, *[_HBM] * n, pl.BlockSpec(memory_space=pltpu.VMEM)),
        out_shape=(pltpu.SemaphoreType.DMA((3 * n,)), pltpu.SemaphoreType.DMA((3 * n,)),
                   *[pltpu.HBM(b.shape, b.dtype) for b in lands], _sds((8, LANES))),
        input_output_aliases={i: 2 + i for i in range(n)},
        compiler_params=pltpu.CompilerParams(has_side_effects=_EFFECT),
    )(*[pltpu.with_memory_space_constraint(b, pltpu.HBM) for b in lands])
    return res[0], res[1], res[2:2 + n], res[-1]


def _relay_wait(send_sems, recv_sems, lands, after, name):
    n = len(lands)

    def body(*refs):
        land_refs = refs[:n]
        send_sems, recv_sems = refs[n], refs[n + 1]
        x, y, c = lax.axis_index("x"), lax.axis_index("y"), lax.axis_index("c")
        for j, (px, py) in enumerate(_other_chips()):
            sent, received = 4 * px + 2 * py + c, 4 * px + 2 * py + (1 - c)
            for a in range(n):
                cp = pltpu.make_async_remote_copy(
                    src_ref=land_refs[a].at[sent], dst_ref=land_refs[a].at[received],
                    send_sem=send_sems.at[3 * a + j], recv_sem=recv_sems.at[3 * a + j],
                    device_id=(x, y, 1 - c), device_id_type=_MESH)
                cp.wait_send()
                cp.wait_recv()

    return pl.pallas_call(
        body, name=name, in_specs=[_HBM] * n + [_SEM, _SEM] + [_ANY] * len(after), out_specs=[_HBM] * n,
        out_shape=[pltpu.HBM(b.shape, b.dtype) for b in lands],
        input_output_aliases={i: i for i in range(n)},
        compiler_params=pltpu.CompilerParams(has_side_effects=_EFFECT),
    )(*lands, send_sems, recv_sems, *after)


def _adamw(w, g, m, v):
    m2 = ADAM_B1 * m + (1.0 - ADAM_B1) * g
    v2 = ADAM_B2 * v + (1.0 - ADAM_B2) * (g * g)
    m_hat = m2 / (1.0 - ADAM_B1 ** ADAM_STEP)
    v_hat = v2 / (1.0 - ADAM_B2 ** ADAM_STEP)
    delta = -ADAM_LR * (m_hat / (jnp.sqrt(v_hat) + ADAM_EPS) + ADAM_WD * w)
    return delta, m2, v2


def _adam_shard(parts, w, m, v, name):
    _, r, c = w.shape
    tr = max(t for t in range(16, 257, 16) if r % t == 0)

    def body(p_ref, w_ref, m_ref, v_ref, g_ref, d_ref, m2_ref, v2_ref):
        g = p_ref[0].astype(F32)
        for s in range(1, N_DEV):
            g = g + p_ref[s].astype(F32)
        g_ref[0] = g
        d_ref[0], m2_ref[0], v2_ref[0] = _adamw(w_ref[0], g, m_ref[0], v_ref[0])

    row = lambda: pl.BlockSpec((1, tr, c), lambda i: (0, i, 0))
    return pl.pallas_call(
        body, name=name, grid=(r // tr,),
        in_specs=[pl.BlockSpec((N_DEV, tr, c), lambda i: (0, i, 0)), row(), row(), row()],
        out_specs=[row(), row(), row(), row()], out_shape=[_sds((1, r, c))] * 4,
        compiler_params=_cp("parallel"),
    )(*_in_hbm([parts, w, m, v]))


def _adam_small(gs, ws, ms, vs, name):
    n = len(gs)

    def body(*refs):
        ins, outs = refs[:4 * n], refs[4 * n:]
        for i in range(n):
            g = ins[i][...]
            d, m2, v2 = _adamw(ins[n + i][...], g, ins[2 * n + i][...], ins[3 * n + i][...])
            outs[i][...] = d
            outs[n + i][...] = m2
            outs[2 * n + i][...] = v2

    res = pl.pallas_call(
        body, name=name, grid=(1,), in_specs=[_full(w.shape) for w in ws] * 4,
        out_specs=[_full(w.shape) for w in ws] * 3, out_shape=[_sds(w.shape) for w in ws] * 3,
        compiler_params=_cp("arbitrary"),
    )(*_in_hbm([*gs, *ws, *ms, *vs]))
    return res[:n], res[n:2 * n], res[2 * n:]


def _sum_slots(parts, name):
    R = parts.shape[1]

    def body(p_ref, o_ref):
        g = p_ref[0]
        for s in range(1, N_DEV):
            g = g + p_ref[s]
        o_ref[...] = g

    return pl.pallas_call(body, name=name, grid=(1,), in_specs=[_full(parts.shape)], out_specs=_full((R, LANES)),
                          out_shape=_sds((R, LANES)))(*_in_hbm([parts]))


def _pad_to(a, n, axis):
    extra = n - a.shape[axis]
    if extra == 0:
        return a
    widths = [(0, 0)] * a.ndim
    widths[axis] = (0, extra)
    return jnp.pad(a, widths)


def _ceil_to(n, k):
    return -(-n // k) * k


def _pack_rows(flats, rows_multiple):
    parts = [_pad_to(f, _ceil_to(f.shape[-1], LANES), f.ndim - 1) for f in flats]
    cat = jnp.concatenate(parts, axis=-1)
    total = _ceil_to(cat.shape[-1], LANES * rows_multiple)
    cat = _pad_to(cat, total, cat.ndim - 1)
    return cat.reshape(cat.shape[:-1] + (total // LANES, LANES))


def _unpack_rows(buf, sizes):
    flat = buf.reshape(buf.shape[:-2] + (-1,))
    out, off = [], 0
    for n in sizes:
        out.append(flat[..., off:off + n])
        off += _ceil_to(n, LANES)
    return out


_MIX_BIG = ("w_in", "w_glu", "w_proj_a", "w_proj_b", "w_out")
_BIG = _MIX_BIG + ("w_up", "w_down")
_SMALL = ("g_mix", "a_re", "a_im", "log_dt", "b_re", "b_im", "c_re", "c_im", "d_skip", "b_glu", "g_sgu", "w_s", "b_s",
          "g_ffn", "conv_b", "g_final")
_SMALL_ROWS_MULTIPLE = 8 * N_DEV
_TRANSPOSED = ("w_in", "w_up", "b_re", "b_im")


def _as_2d(a):
    return a.reshape(-1, a.shape[-1]) if a.ndim > 1 else a.reshape(1, -1)


def kernel(x, g_mix, w_in, a_re, a_im, log_dt, b_re, b_im, c_re, c_im, d_skip, w_glu, b_glu, w_proj_a, g_sgu, w_s, b_s, w_proj_b, w_out, g_ffn, w_up, conv_w, conv_b, w_down, g_final, loss_target, m_g_mix, m_w_in, m_a_re, m_a_im, m_log_dt, m_b_re, m_b_im, m_c_re, m_c_im, m_d_skip, m_w_glu, m_b_glu, m_w_proj_a, m_g_sgu, m_w_s, m_b_s, m_w_proj_b, m_w_out, m_g_ffn, m_w_up, m_conv_w, m_conv_b, m_w_down, m_g_final, v_g_mix, v_w_in, v_a_re, v_a_im, v_log_dt, v_b_re, v_b_im, v_c_re, v_c_im, v_d_skip, v_w_glu, v_b_glu, v_w_proj_a, v_g_sgu, v_w_s, v_b_s, v_w_proj_b, v_w_out, v_g_ffn, v_w_up, v_conv_w, v_conv_b, v_w_down, v_g_final):
    args = dict(locals())
    me = 4 * lax.axis_index("x") + 2 * lax.axis_index("y") + lax.axis_index("c")

    def own_slot(buf, block):
        return lax.dynamic_update_slice(buf, block[None], (me,) + (0,) * block.ndim)

    for n in _TRANSPOSED:
        for pre in ("", "m_", "v_"):
            args[pre + n] = jnp.swapaxes(args[pre + n], -1, -2)
    later = ("w_glu", "w_proj_a", "w_proj_b", "w_out", "w_up", "w_down")
    (w_in_g,), casts = _allgather([args["w_in"][0]], [MXU], "allgather_w_in", cast_only=[args[n][0] for n in later])
    sh = dict(zip(later, casts))

    def start_push(srcs, tag, peers):
        lands = [own_slot(lax.empty((N_DEV,) + s.shape, s.dtype), s) for s in srcs]
        send_sems, recv_sems, srcs, lands, token = _push_start(srcs, lands, False, "push_" + tag, peers)
        return (send_sems, recv_sems, srcs, lands), token

    mix_push, token_a = start_push([sh[n] for n in later[:4]], "mixer_weights", _SAME_CORE_AND_SIBLING)
    ffn_push, token_b = start_push([sh["w_up"], sh["w_down"], conv_w[0]], "ffn_weights", _PEER_ORDER)
    p = {n: (args[n][0] if n != "g_final" else args[n]) for n in _SMALL if n not in _TRANSPOSED}
    p.update(w_in_t=w_in_g.reshape(SSM_W + 2 * SGU_W + 2 * D_MODEL, D_MODEL),
             b_re_t=args["b_re"][0], b_im_t=args["b_im"][0])
    p["g_mix"] = p["g_mix"] + (token_a[0:1, 0:1] + token_b[0:1, 0:1])
    relay = {}

    def mixer_relay(after):
        lands = _push_wait(*mix_push, False, [after], "wait_mixer_weights", _SAME_CORE_AND_SIBLING)
        relay["send"], relay["recv"], relay["lands"], token = _relay_start(lands, "relay_mixer_weights")
        return token

    def mixer_weights(after):
        w_glu_g, w_pa_g, w_pb_g, w_out_g = _relay_wait(relay["send"], relay["recv"], relay["lands"], [after],
                                                       "wait_relay_mixer_weights")
        w_pa_full, w_pb_full = _assemble_cols([w_pa_g, w_pb_g], "assemble_cols")
        return dict(w_glu=w_glu_g.reshape(SSM_W, SSM_W), w_proj_a=w_pa_full, w_proj_b=w_pb_full,
                    w_out=w_out_g.reshape(D_MODEL, D_MODEL))

    def ffn_weights(after):
        w_up_g, w_down_g, conv_w_g = _push_wait(*ffn_push, False, [after], "wait_ffn_weights")
        return w_up_g, conv_w_g, w_down_g.reshape(D_FF, D_MODEL)

    pushes = []

    def grads_out(names, sends):
        lands = [own_slot(lax.empty(s.shape, s.dtype), lax.dynamic_index_in_dim(s, me, 0, keepdims=False))
                 for s in sends]
        send_sems, recv_sems, srcs, lands, token = _push_start(list(sends), lands, True, "push_grads_" + names[0])
        pushes.append((names, send_sems, recv_sems, srcs, lands))
        return token

    loss_part, grad_x, grads = _local_step(x[0], loss_target[0], p, mixer_relay, mixer_weights, ffn_weights, grads_out)

    small_names = _SMALL + ("conv_w", "loss")
    small_g = dict(grads, loss=loss_part[0, 0:1])
    flats = [small_g[n].reshape(-1) for n in small_names]
    small_sizes = [f.shape[0] for f in flats]
    g_small = _pack_rows(flats, _SMALL_ROWS_MULTIPLE)
    rs8 = g_small.shape[0] // N_DEV
    grads_out(("small",), (g_small.reshape(N_DEV, rs8, LANES),))

    out = {}
    done = [g_small]
    for names, send_sems, recv_sems, srcs, lands in pushes:
        parts = _push_wait(send_sems, recv_sems, srcs, lands, True, done, "wait_grads_" + names[0])
        if names == ("small",):
            recv_small, = parts
            break
        for n, part in zip(names, parts):
            res = _adam_shard(part, args[n], args["m_" + n], args["v_" + n], "adam_" + n)
            for kind, v in zip(("grad_", "delta_", "new_m_", "new_v_"), res):
                out[kind + n] = v
            done = [res[0]]
    small_mine = _sum_slots(recv_small, "sum_small")
    g_small_all = _allgather([small_mine], [F32], "allgather_small")[0][0].reshape(N_DEV * rs8, LANES)
    pieces = dict(zip(small_names, _unpack_rows(g_small_all, small_sizes)))
    loss = pieces["loss"][0]
    dconv_w = lax.dynamic_index_in_dim(pieces["conv_w"].reshape(N_DEV, 3, FF_CW), me, axis=0, keepdims=False)
    names2 = _SMALL + ("conv_w",)
    gs = [pieces[n].reshape(_as_2d(args[n]).shape) for n in _SMALL] + [dconv_w]
    ds, m2s, v2s = _adam_small(gs, [_as_2d(args[n]) for n in names2], [_as_2d(args["m_" + n]) for n in names2],
                               [_as_2d(args["v_" + n]) for n in names2], "adam_small")
    for n, res in zip(names2, zip(gs, ds, m2s, v2s)):
        for kind, v in zip(("grad_", "delta_", "new_m_", "new_v_"), res):
            out[kind + n] = v.reshape(args[n].shape)
    order = ("g_mix", "w_in", "a_re", "a_im", "log_dt", "b_re", "b_im", "c_re", "c_im", "d_skip", "w_glu", "b_glu",
             "w_proj_a", "g_sgu", "w_s", "b_s", "w_proj_b", "w_out", "g_ffn", "w_up", "conv_w", "conv_b", "w_down",
             "g_final")
    res = [loss, grad_x.reshape(x.shape)]
    for kind in ("grad_", "delta_", "new_m_", "new_v_"):
        res += [jnp.swapaxes(out[kind + n], -1, -2) if n in _TRANSPOSED else out[kind + n] for n in order]
    return tuple(res)
```

```python
import functools
import math

import jax
import jax.numpy as jnp
from jax import lax
from jax.experimental import pallas as pl
from jax.experimental.pallas import tpu as pltpu

F32 = jnp.float32
MXU = jnp.bfloat16
EPS = 1e-6

D_MODEL = 1024
SSM_W = 512
SSM_G, SSM_H, SSM_P = 32, 16, 64
SSM_BLK = 4
SGU_W = 512
SGU_G, SGU_D, CHUNK = 8, 64, 128
D_FF = 2816
N_DEV = 8
FF_SHARD = 2 * D_FF // N_DEV
FF_CW = 2 * FF_SHARD
FF_NCB = D_FF // FF_CW
LANES = 128

ADAM_LR, ADAM_B1, ADAM_B2, ADAM_EPS, ADAM_WD, ADAM_STEP = 0.001, 0.9, 0.999, 1e-08, 0.01, 10

VMEM_LIMIT = 48 * 1024 * 1024
WGRAD_VMEM_LIMIT = 58 * 1024 * 1024


def _cp(*sem):
    return pltpu.CompilerParams(dimension_semantics=sem, vmem_limit_bytes=VMEM_LIMIT)


def _full(shape):
    n = len(shape)
    return pl.BlockSpec(shape, lambda *_: (0,) * n)


def _sds(shape, dtype=F32):
    return jax.ShapeDtypeStruct(shape, dtype)


def _in_hbm(arrays):
    return [pltpu.with_memory_space_constraint(a, pltpu.HBM) for a in arrays]


def _dot(a, b):
    return jnp.dot(a, b, preferred_element_type=F32)


def _dot_nt(a, b):
    return lax.dot_general(a, b, (((1,), (1,)), ((), ())), preferred_element_type=F32)


def _dot_tn(a, b):
    return lax.dot_general(a, b, (((0,), (0,)), ((), ())), preferred_element_type=F32)


_GELU_C = math.sqrt(2.0 / math.pi)


def _gelu(x):
    return 0.5 * x * (1.0 + jnp.tanh(_GELU_C * (x + 0.044715 * (x * x * x))))


def _gelu_and_grad(x):
    t = jnp.tanh(_GELU_C * (x + 0.044715 * (x * x * x)))
    g = 0.5 * x * (1.0 + t)
    dg = 0.5 * (1.0 + t) + 0.5 * x * (1.0 - t * t) * (_GELU_C * (1.0 + 3.0 * 0.044715 * (x * x)))
    return g, dg


def _sigmoid(x):
    return 0.5 * jnp.tanh(0.5 * x) + 0.5


def _rms(x):
    return lax.rsqrt(jnp.mean(x * x, axis=-1, keepdims=True) + EPS)


def _rms_bwd(dxn, xn, r):
    return r * (dxn - xn * jnp.mean(dxn * xn, axis=-1, keepdims=True))


def _rowsum(x):
    return jnp.sum(x, axis=0, keepdims=True)


def _s5_disc(are, aim, ldt, br, bi):
    dt = jnp.exp(ldt)
    mag = jnp.exp(dt * are)
    abr = mag * jnp.cos(dt * aim)
    abi = mag * jnp.sin(dt * aim)
    den = are * are + aim * aim
    nr = abr - 1.0
    ni = abi
    fr = (nr * are + ni * aim) / den
    fi = (ni * are - nr * aim) / den
    return abr, abi, fr * br - fi * bi, fr * bi + fi * br


def _s5_params_fwd(are, aim, ldt, br, bi):
    def body(are_ref, aim_ref, ldt_ref, br_ref, bi_ref, o0, o1, o2, o3):
        outs = _s5_disc(are_ref[...], aim_ref[...], ldt_ref[...], br_ref[...], bi_ref[...])
        for o, v in zip((o0, o1, o2, o3), outs):
            o[...] = v
    shp = are.shape
    return pl.pallas_call(body, name="s5_params_fwd", grid=(1,), in_specs=[_full(shp)] * 5, out_specs=[_full(shp)] * 4,
                          out_shape=[_sds(shp)] * 4)(*_in_hbm([are, aim, ldt, br, bi]))


def _s5_params_bwd(are, aim, ldt, br, bi, dabr, dabi, dbr, dbi):
    def body(are_ref, aim_ref, ldt_ref, br_ref, bi_ref, c0, c1, c2, c3, o0, o1, o2, o3, o4):
        prim = (are_ref[...], aim_ref[...], ldt_ref[...], br_ref[...], bi_ref[...])
        _, vjp = jax.vjp(_s5_disc, *prim)
        outs = vjp((c0[...], c1[...], c2[...], c3[...]))
        for o, v in zip((o0, o1, o2, o3, o4), outs):
            o[...] = v
    shp = are.shape
    return pl.pallas_call(body, name="s5_params_bwd", grid=(1,), in_specs=[_full(shp)] * 9, out_specs=[_full(shp)] * 5,
                          out_shape=[_sds(shp)] * 5)(*_in_hbm([are, aim, ldt, br, bi, dabr, dabi, dbr, dbi]))


def _blockdiag(m_t):
    m = m_t.reshape(SSM_BLK, 8, SSM_H, 1, SSM_P)
    eye = jnp.eye(8, dtype=bool).reshape(1, 8, 1, 8, 1)
    return jnp.where(eye, m, jnp.zeros((), m_t.dtype)).reshape(SSM_BLK, 8 * SSM_H, 8 * SSM_P)


def _unblockdiag(pc):
    m = pc.reshape(SSM_BLK, 8, SSM_H, 8, SSM_P)
    return jnp.einsum("jghgp->jghp", m).reshape(SSM_G * SSM_H, SSM_P)


def _in_fwd(x, g_mix, w_in_t, tm):
    S = x.shape[0]

    def body(x_ref, g_ref, w_ref, h_ref, us_ref, uv_ref, gl_ref):
        xv = x_ref[...]
        h = (xv * _rms(xv) * g_ref[...]).astype(MXU)
        h_ref[...] = h
        us_ref[...] = _dot_nt(h, w_ref[0:SSM_W, :])
        uv_ref[...] = _dot_nt(h, w_ref[SSM_W:SSM_W + 2 * SGU_W, :])
        gl_ref[...] = _dot_nt(h, w_ref[SSM_W + 2 * SGU_W:, :])

    row = lambda n: pl.BlockSpec((tm, n), lambda i: (i, 0))
    return pl.pallas_call(
        body, name="in_fwd", grid=(S // tm,),
        in_specs=[row(D_MODEL), _full((1, D_MODEL)), _full(w_in_t.shape)],
        out_specs=[row(D_MODEL), row(SSM_W), row(2 * SGU_W), row(2 * D_MODEL)],
        out_shape=[_sds((S, D_MODEL), MXU), _sds((S, SSM_W)), _sds((S, 2 * SGU_W)), _sds((S, 2 * D_MODEL))],
        compiler_params=_cp("parallel"),
    )(*_in_hbm([x, g_mix, w_in_t]))


def _scan_tables(ar, ai, reverse):
    n = ar.shape[-1]
    def mul(p, q):
        return p[0] * q[0] - p[1] * q[1], p[0] * q[1] + p[1] * q[0]
    a1 = (ar, ai)
    a2 = mul(a1, a1)
    a3 = mul(a2, a1)
    a4 = mul(a2, a2)
    a5 = mul(a4, a1)
    a6 = mul(a4, a2)
    a7 = mul(a4, a3)
    a8 = mul(a4, a4)
    pw = (a1, a2, a3, a4, a5, a6, a7, a8)
    rows = lax.broadcasted_iota(jnp.int32, (8, n), 0)
    tabs = []
    for s, a in ((1, a1), (2, a2), (4, a4)):
        keep = (rows + s <= 7) if reverse else (rows >= s)
        for comp in a:
            tabs.append(jnp.where(keep, jnp.broadcast_to(comp, (8, n)), 0.0))
    for c in range(2):
        q = jnp.zeros((8, n), F32)
        for r in range(8):
            e = (8 - r) if reverse else (r + 1)
            q = jnp.where(rows == r, jnp.broadcast_to(pw[e - 1][c], (8, n)), q)
        tabs.append(q)
    return tabs


def _scan_group(xr, xi, tab_ref, cr, ci, reverse):
    for t, s in enumerate((1, 2, 4)):
        pr = tab_ref[2 * t]
        pi = tab_ref[2 * t + 1]
        sh = (8 - s) if reverse else s
        sr = pltpu.roll(xr, sh, 0)
        si = pltpu.roll(xi, sh, 0)
        xr, xi = xr + pr * sr - pi * si, xi + pr * si + pi * sr
    qr = tab_ref[6]
    qi = tab_ref[7]
    return xr + qr * cr - qi * ci, xi + qr * ci + qi * cr


def _runs_load(src_ref, dst_ref, run):
    for i in range(run):
        dst_ref[8 * i:8 * i + 8, :] = src_ref[pl.ds(i, 8, stride=run), :]


def _runs_store(val, dst_ref, run):
    for i in range(run):
        dst_ref[pl.ds(i, 8, stride=run), :] = val[8 * i:8 * i + 8, :]


def _cpow2(ar, ai, log2n):
    for _ in range(log2n):
        ar, ai = ar * ar - ai * ai, 2.0 * ar * ai
    return ar, ai


def _s5_fwd(us, abar_re, abar_im, b_re, b_im, c_re, c_im, d_skip, tm):
    S = us.shape[0]
    nt = S // tm
    w = 8 * SSM_P
    run = tm // 8
    assert run & (run - 1) == 0

    def body(us_ref, ar_ref, ai_ref, br_ref, bi_ref, cr_ref, ci_ref, d_ref, str_ref, sti_ref, ys_ref,
             tab_ref, car_ref, up_ref):
        i = pl.program_id(1)

        @pl.when(i == 0)
        def _():
            car_ref[...] = jnp.zeros_like(car_ref)
            for k, t in enumerate(_scan_tables(*_cpow2(ar_ref[...], ai_ref[...], run.bit_length() - 1), False)):
                tab_ref[k] = t

        _runs_load(us_ref, up_ref, run)
        ub = up_ref[...].astype(MXU)
        str_ref[...] = _dot(ub, br_ref[0])
        sti_ref[...] = _dot(ub, bi_ref[0])
        ar = jnp.broadcast_to(ar_ref[...], (8, w))
        ai = jnp.broadcast_to(ai_ref[...], (8, w))

        def advance(k, state):
            r0 = pl.multiple_of(k * 8, 8)
            sr, si = state
            return (ar * sr - ai * si + str_ref[pl.ds(r0, 8), :], ar * si + ai * sr + sti_ref[pl.ds(r0, 8), :])

        def emit(k, state):
            r0 = pl.multiple_of(k * 8, 8)
            sr, si = advance(k, state)
            str_ref[pl.ds(r0, 8), :] = sr
            sti_ref[pl.ds(r0, 8), :] = si
            return sr, si

        zero = jnp.zeros((8, w), F32)
        er, ei = lax.fori_loop(0, run, advance, (zero, zero))
        cr, ci = car_ref[0:1, :], car_ref[1:2, :]
        tr, ti = _scan_group(er, ei, tab_ref, cr, ci, False)
        r8 = lax.broadcasted_iota(jnp.int32, (8, w), 0)
        start = (jnp.where(r8 == 0, cr, pltpu.roll(tr, 1, 0)), jnp.where(r8 == 0, ci, pltpu.roll(ti, 1, 0)))
        car_ref[0:1, :] = tr[7:8, :]
        car_ref[1:2, :] = ti[7:8, :]
        lax.fori_loop(0, run, emit, start)
        y = _dot_nt(str_ref[...].astype(MXU), cr_ref[0]) - _dot_nt(sti_ref[...].astype(MXU), ci_ref[0])
        _runs_store(y, ys_ref, run)
        ys_ref[...] += d_ref[...] * us_ref[...]

    blk = lambda: pl.BlockSpec((1, 8 * SSM_H, w), lambda j, i: (j, 0, 0))
    return pl.pallas_call(
        body, name="s5_fwd", grid=(SSM_BLK, nt),
        in_specs=[pl.BlockSpec((tm, LANES), lambda j, i: (i, j)),
                  pl.BlockSpec((1, w), lambda j, i: (0, j)), pl.BlockSpec((1, w), lambda j, i: (0, j)),
                  blk(), blk(), blk(), blk(),
                  pl.BlockSpec((1, LANES), lambda j, i: (0, j))],
        out_specs=[pl.BlockSpec((tm, w), lambda j, i: (i, j)), pl.BlockSpec((tm, w), lambda j, i: (i, j)),
                   pl.BlockSpec((tm, LANES), lambda j, i: (i, j))],
        out_shape=[_sds((S, SSM_BLK * w)), _sds((S, SSM_BLK * w)), _sds((S, SSM_W))],
        scratch_shapes=[pltpu.VMEM((8, 8, w), F32), pltpu.VMEM((8, w), F32), pltpu.VMEM((tm, LANES), F32)],
        compiler_params=_cp("parallel", "arbitrary"),
    )(*_in_hbm([us, abar_re, abar_im, b_re, b_im, c_re, c_im, d_skip]))


def _sgu_mix(vnb, ws_ref, grp):
    acc = jnp.zeros(vnb.shape, F32)
    for g in range(SGU_G):
        acc = jnp.where(grp == g, _dot(ws_ref[g], vnb), acc)
    return acc


def _mix_fwd(x, ys, uv, gl, w_glu, b_glu, w_pa, g_sgu, ws, bias_s, w_pb, w_out, g_ffn, tm):
    S = x.shape[0]

    def body(x_ref, ys_ref, uv_ref, gl_ref, wglu_ref, bglu_ref, wpa_ref, gs_ref, ws_ref, bias_ref, wpb_ref, wout_ref,
             gf_ref, yg_ref, yap_ref, sg_ref, ya_ref, yb_ref, m_ref, x1_ref, h2_ref):
        yg = _gelu(ys_ref[...])
        ygb = yg.astype(MXU)
        yg_ref[...] = ygb
        z = _dot(ygb, wglu_ref[...]) + bglu_ref[...]
        yapb = (yg * _sigmoid(z)).astype(MXU)
        yap_ref[...] = yapb
        ya = _dot(yapb, wpa_ref[...])
        ya_ref[...] = ya

        uvg = _gelu(uv_ref[...])
        u2 = uvg[:, :SGU_W]
        v2 = uvg[:, SGU_W:]
        vnb = (v2 * _rms(v2) * gs_ref[...]).astype(MXU)
        grp = lax.broadcasted_iota(jnp.int32, (CHUNK, SGU_W), 1) // SGU_D
        for c in range(tm // CHUNK):
            rs = slice(c * CHUNK, (c + 1) * CHUNK)
            mixed = _sgu_mix(vnb[rs], ws_ref, grp) + bias_ref[...]
            sg_ref[rs, :] = (u2[rs] * mixed).astype(MXU)
        yb = _dot(sg_ref[...], wpb_ref[...])
        yb_ref[...] = yb

        glv = gl_ref[...]
        m = _sigmoid(glv[:, :D_MODEL]) * ya + _sigmoid(glv[:, D_MODEL:]) * yb
        mb = m.astype(MXU)
        m_ref[...] = mb
        x1 = x_ref[...] + _dot(mb, wout_ref[...])
        x1_ref[...] = x1
        h2_ref[...] = (x1 * _rms(x1) * gf_ref[...]).astype(MXU)

    row = lambda n: pl.BlockSpec((tm, n), lambda i: (i, 0))
    return pl.pallas_call(
        body, name="mix_fwd", grid=(S // tm,),
        in_specs=[row(D_MODEL), row(SSM_W), row(2 * SGU_W), row(2 * D_MODEL),
                  _full(w_glu.shape), _full(b_glu.shape), _full(w_pa.shape), _full(g_sgu.shape), _full(ws.shape),
                  _full(bias_s.shape), _full(w_pb.shape), _full(w_out.shape), _full(g_ffn.shape)],
        out_specs=[row(SSM_W), row(SSM_W), row(SGU_W), row(D_MODEL), row(D_MODEL), row(D_MODEL), row(D_MODEL),
                   row(D_MODEL)],
        out_shape=[_sds((S, SSM_W), MXU), _sds((S, SSM_W), MXU), _sds((S, SGU_W), MXU), _sds((S, D_MODEL)),
                   _sds((S, D_MODEL)), _sds((S, D_MODEL), MXU), _sds((S, D_MODEL)), _sds((S, D_MODEL), MXU)],
        compiler_params=_cp("parallel"),
    )(*_in_hbm([x, ys, uv, gl, w_glu, b_glu, w_pa, g_sgu, ws, bias_s, w_pb, w_out, g_ffn]))


def _causal_conv3(u, prev8, cw, cb):
    tm = u.shape[0]
    w0, w1, w2 = cw[0:1], cw[1:2], cw[2:3]
    body = w0 * pltpu.roll(u, 2, 0) + w1 * pltpu.roll(u, 1, 0) + w2 * u + cb
    u8 = u[0:8, :]
    r8 = lax.broadcasted_iota(jnp.int32, u8.shape, 0)
    t1 = prev8[7:8, :]
    t0 = prev8[6:7, :]
    s1 = jnp.where(r8 == 0, t1, pltpu.roll(u8, 1, 0))
    s2 = jnp.where(r8 == 0, t0, jnp.where(r8 == 1, t1, pltpu.roll(u8, 2, 0)))
    first = w0 * s2 + w1 * s1 + w2 * u8 + cb
    return jnp.concatenate([first, body[8:tm, :]], axis=0)


def _causal_conv3_adjoint(d, next8, cw):
    tm = d.shape[0]
    w0, w1, w2 = cw[0:1], cw[1:2], cw[2:3]
    n1 = pltpu.roll(d, tm - 1, 0)
    n2 = pltpu.roll(d, tm - 2, 0)
    body = w2 * d + w1 * n1 + w0 * n2
    d8 = d[tm - 8:tm, :]
    r8 = lax.broadcasted_iota(jnp.int32, d8.shape, 0)
    h0 = next8[0:1, :]
    h1 = next8[1:2, :]
    m1 = jnp.where(r8 == 7, h0, pltpu.roll(d8, 7, 0))
    m2 = jnp.where(r8 == 6, h0, jnp.where(r8 == 7, h1, pltpu.roll(d8, 6, 0)))
    last = w2 * d8 + w1 * m1 + w0 * m2
    out = jnp.concatenate([body[0:tm - 8, :], last], axis=0)
    return out, n1, n2, h0 - d[0:1, :], h1 - d[1:2, :]


def _ffn_fwd(h2, x1, tgt, w_up, conv_w, conv_b, w_down, g_final, tm):
    S = h2.shape[0]
    nt = S // tm
    ncb = FF_NCB

    def body(h2_ref, wa_ref, wb_ref, cwa_ref, cwb_ref, cba_ref, cbb_ref, wd_ref, x1_ref, gf_ref, tgt_ref,
             up_ref, ab_ref, ff_ref, dx2_ref, dx2b_ref, loss_ref, dgf_ref, acc_ref, tail_ref):
        i = pl.program_id(0)
        cb = pl.program_id(1)

        @pl.when(i == 0)
        def _():
            tail_ref[cb] = jnp.zeros((2, 8, FF_CW), F32)

        @pl.when(jnp.logical_and(i == 0, cb == 0))
        def _():
            loss_ref[...] = jnp.zeros_like(loss_ref)
            dgf_ref[...] = jnp.zeros_like(dgf_ref)

        h2v = h2_ref[...]
        ua = _dot_nt(h2v, wa_ref[0])
        ub = _dot_nt(h2v, wb_ref[0])
        up_ref[0, 0] = ua.astype(MXU)
        up_ref[1, 0] = ub.astype(MXU)
        a = _causal_conv3(ua, tail_ref[cb, 0], cwa_ref[0], cba_ref[0])
        b = _causal_conv3(ub, tail_ref[cb, 1], cwb_ref[0], cbb_ref[0])
        tail_ref[cb, 0] = ua[tm - 8:tm, :]
        tail_ref[cb, 1] = ub[tm - 8:tm, :]
        ab_ref[0, 0] = a
        ab_ref[1, 0] = b
        ffb = (a * _sigmoid(a) * b).astype(MXU)
        ff_ref[0] = ffb
        contrib = _dot(ffb, wd_ref[...])

        @pl.when(cb == 0)
        def _():
            acc_ref[...] = contrib

        @pl.when(cb > 0)
        def _():
            acc_ref[...] += contrib

        @pl.when(cb == ncb - 1)
        def _():
            x2 = x1_ref[...] + acc_ref[...]
            r = _rms(x2)
            xn = x2 * r
            g = gf_ref[...]
            diff = xn * g - tgt_ref[...]
            loss_ref[...] += (0.5 / D_MODEL) * jnp.sum(diff * diff)
            dy = diff * (1.0 / D_MODEL)
            dgf_ref[...] += _rowsum(dy * xn)
            dx2 = _rms_bwd(dy * g, xn, r)
            dx2_ref[...] = dx2
            dx2b_ref[...] = dx2.astype(MXU)

    row = lambda n: pl.BlockSpec((tm, n), lambda i, c: (i, 0))
    gate = lambda r: pl.BlockSpec((1, r, FF_CW), lambda i, c: (c, 0, 0))
    lin = lambda r: pl.BlockSpec((1, r, FF_CW), lambda i, c: (ncb + c, 0, 0))
    return pl.pallas_call(
        body, name="ffn_fwd", grid=(nt, ncb),
        in_specs=[row(D_MODEL),
                  pl.BlockSpec((1, FF_CW, D_MODEL), lambda i, c: (c, 0, 0)),
                  pl.BlockSpec((1, FF_CW, D_MODEL), lambda i, c: (ncb + c, 0, 0)),
                  gate(3), lin(3), gate(1), lin(1),
                  pl.BlockSpec((FF_CW, D_MODEL), lambda i, c: (c, 0)),
                  row(D_MODEL), _full((1, D_MODEL)), row(D_MODEL)],
        out_specs=[pl.BlockSpec((2, 1, tm, FF_CW), lambda i, c: (0, c, i, 0)),
                   pl.BlockSpec((2, 1, tm, FF_CW), lambda i, c: (0, c, i, 0)),
                   pl.BlockSpec((1, tm, FF_CW), lambda i, c: (c, i, 0)),
                   row(D_MODEL), row(D_MODEL), _full((1, LANES)), _full((1, D_MODEL))],
        out_shape=[_sds((2, ncb, S, FF_CW), MXU), _sds((2, ncb, S, FF_CW)), _sds((ncb, S, FF_CW), MXU),
                   _sds((S, D_MODEL)), _sds((S, D_MODEL), MXU), _sds((1, LANES)), _sds((1, D_MODEL))],
        scratch_shapes=[pltpu.VMEM((tm, D_MODEL), F32), pltpu.VMEM((ncb, 2, 8, FF_CW), F32)],
        compiler_params=_cp("arbitrary", "arbitrary"),
    )(*_in_hbm([h2, w_up, w_up, conv_w, conv_w, conv_b, conv_b, w_down, x1, g_final, tgt]))


def _ffn_bwd(dx2, up, ab, x1, w_up, conv_w, w_down, g_ffn, tm):
    S = dx2.shape[0]
    nt = S // tm
    ncb = FF_NCB

    def body(dx2_ref, up_ref, ab_ref, cwa_ref, cwb_ref, wd_ref, wa_ref, wb_ref,
             x1_ref, g_ref, dup_ref, dx1_ref, dx1b_ref, dconv_ref, dg_ref, acc_ref, head_ref):
        i = pl.program_id(0)
        cb = pl.program_id(1)
        ri = nt - 1 - i

        @pl.when(i == 0)
        def _():
            head_ref[cb] = jnp.zeros((2, 8, FF_CW), F32)
            dconv_ref[cb] = jnp.zeros((8, FF_CW), F32)
            dconv_ref[ncb + cb] = jnp.zeros((8, FF_CW), F32)

        @pl.when(jnp.logical_and(i == 0, cb == 0))
        def _():
            dg_ref[...] = jnp.zeros_like(dg_ref)

        dff = _dot_nt(dx2_ref[...].astype(MXU), wd_ref[...])
        a = ab_ref[0, 0]
        b = ab_ref[1, 0]
        sa = _sigmoid(a)
        silu = a * sa
        da = (dff * b) * (sa + silu * (1.0 - sa))
        db = dff * silu
        dps = []
        for half, slot, d, cw_ref in ((0, cb, da, cwa_ref), (1, ncb + cb, db, cwb_ref)):
            dp, n1, n2, fix0, fix1 = _causal_conv3_adjoint(d, head_ref[cb, half], cw_ref[0])
            head_ref[cb, half] = d[0:8, :]
            dpb16 = dp.astype(MXU)
            dup_ref[half, 0] = dpb16
            dps.append(dpb16)
            u = up_ref[half, 0].astype(F32)
            u_last = u[tm - 1:tm, :]
            dconv_ref[slot, 0:1, :] += _rowsum(n2 * u) + fix0 * u[tm - 2:tm - 1, :] + fix1 * u_last
            dconv_ref[slot, 1:2, :] += _rowsum(n1 * u) + fix0 * u_last
            dconv_ref[slot, 2:3, :] += _rowsum(d * u)
            dconv_ref[slot, 3:4, :] += _rowsum(d)
        contrib = _dot(dps[0], wa_ref[0]) + _dot(dps[1], wb_ref[0])

        @pl.when(cb == 0)
        def _():
            acc_ref[...] = contrib

        @pl.when(cb > 0)
        def _():
            acc_ref[...] += contrib

        @pl.when(cb == ncb - 1)
        def _():
            x1v = x1_ref[...]
            r = _rms(x1v)
            xn = x1v * r
            dh2 = acc_ref[...]
            dg_ref[...] += _rowsum(dh2 * xn)
            dx1 = dx2_ref[...] + _rms_bwd(dh2 * g_ref[...], xn, r)
            dx1_ref[...] = dx1
            dx1b_ref[...] = dx1.astype(MXU)

    row = lambda n: pl.BlockSpec((tm, n), lambda i, c: (nt - 1 - i, 0))
    colb = lambda: pl.BlockSpec((2, 1, tm, FF_CW), lambda i, c: (0, c, nt - 1 - i, 0))
    gate = lambda r: pl.BlockSpec((1, r, FF_CW), lambda i, c: (c, 0, 0))
    lin = lambda r: pl.BlockSpec((1, r, FF_CW), lambda i, c: (ncb + c, 0, 0))
    return pl.pallas_call(
        body, name="ffn_bwd", grid=(nt, ncb),
        in_specs=[row(D_MODEL), colb(), colb(), gate(3), lin(3),
                  pl.BlockSpec((FF_CW, D_MODEL), lambda i, c: (c, 0)),
                  pl.BlockSpec((1, FF_CW, D_MODEL), lambda i, c: (c, 0, 0)),
                  pl.BlockSpec((1, FF_CW, D_MODEL), lambda i, c: (ncb + c, 0, 0)),
                  row(D_MODEL), _full((1, D_MODEL))],
        out_specs=[colb(), row(D_MODEL), row(D_MODEL), _full((2 * ncb, 8, FF_CW)), _full((1, D_MODEL))],
        out_shape=[_sds((2, ncb, S, FF_CW), MXU), _sds((S, D_MODEL)), _sds((S, D_MODEL), MXU), _sds((2 * ncb, 8, FF_CW)),
                   _sds((1, D_MODEL))],
        scratch_shapes=[pltpu.VMEM((tm, D_MODEL), F32), pltpu.VMEM((ncb, 2, 8, FF_CW), F32)],
        compiler_params=_cp("arbitrary", "arbitrary"),
    )(*_in_hbm([dx2, up, ab, conv_w, conv_w, w_down, w_up, w_up, x1, g_ffn]))


def _mix_bwd(dx1, gl, ya, yb, ys, uv, w_out, w_pa, w_pb, w_glu, b_glu, g_sgu, ws, ws_t, bias_s, tm):
    S = dx1.shape[0]

    def body(dx1_ref, gl_ref, ya_ref, yb_ref, ys_ref, uv_ref, wout_ref, wpa_ref, wpb_ref, wglu_ref, bglu_ref, gs_ref,
             ws_ref, wst_ref, bias_ref,
             dgl_ref, dya_ref, dyb_ref, dz_ref, dys_ref, duv_ref, dbglu_ref, dgs_ref, dws_ref, dbs_ref,
             du2_ref, dvn_ref):
        i = pl.program_id(0)

        @pl.when(i == 0)
        def _():
            dbglu_ref[...] = jnp.zeros_like(dbglu_ref)
            dgs_ref[...] = jnp.zeros_like(dgs_ref)
            dws_ref[...] = jnp.zeros_like(dws_ref)
            dbs_ref[...] = jnp.zeros_like(dbs_ref)

        dm = _dot_nt(dx1_ref[...].astype(MXU), wout_ref[...])
        glv = gl_ref[...]
        ga = _sigmoid(glv[:, :D_MODEL])
        gb = _sigmoid(glv[:, D_MODEL:])
        dgl_ref[:, :D_MODEL] = (dm * ya_ref[...] * ga * (1.0 - ga)).astype(MXU)
        dgl_ref[:, D_MODEL:] = (dm * yb_ref[...] * gb * (1.0 - gb)).astype(MXU)
        dyab = (dm * ga).astype(MXU)
        dybb = (dm * gb).astype(MXU)
        dya_ref[...] = dyab
        dyb_ref[...] = dybb

        dyap = _dot_nt(dyab, wpa_ref[...])
        yg, dgelu = _gelu_and_grad(ys_ref[...])
        sz = _sigmoid(_dot(yg.astype(MXU), wglu_ref[...]) + bglu_ref[...])
        dz = dyap * yg * sz * (1.0 - sz)
        dzb = dz.astype(MXU)
        dz_ref[...] = dzb
        dbglu_ref[...] += _rowsum(dz)
        dys_ref[...] = (dyap * sz + _dot_nt(dzb, wglu_ref[...])) * dgelu

        dsg = _dot_nt(dybb, wpb_ref[...])
        uvg, duvg = _gelu_and_grad(uv_ref[...])
        u2 = uvg[:, :SGU_W]
        v2 = uvg[:, SGU_W:]
        rv = _rms(v2)
        vhat = v2 * rv
        gs = gs_ref[...]
        vnb = (vhat * gs).astype(MXU)
        grp = lax.broadcasted_iota(jnp.int32, (CHUNK, SGU_W), 1) // SGU_D
        tril = (lax.broadcasted_iota(jnp.int32, (CHUNK, CHUNK), 0)
                >= lax.broadcasted_iota(jnp.int32, (CHUNK, CHUNK), 1))
        for c in range(tm // CHUNK):
            rs = slice(c * CHUNK, (c + 1) * CHUNK)
            vc = vnb[rs]
            mixed = _sgu_mix(vc, ws_ref, grp) + bias_ref[...]
            dsg_c = dsg[rs]
            du2_ref[rs, :] = dsg_c * mixed
            dmx = dsg_c * u2[rs]
            dbs_ref[...] += dmx
            dmb = dmx.astype(MXU)
            dvn_ref[rs, :] = _sgu_mix(dmb, wst_ref, grp)
            for g in range(SGU_G):
                part = _dot_nt(jnp.where(grp == g, dmb, jnp.zeros((), MXU)), vc)
                dws_ref[g] += jnp.where(tril, part, 0.0)
        dvn = dvn_ref[...]
        dgs_ref[...] += _rowsum(dvn * vhat)
        dv2 = _rms_bwd(dvn * gs, vhat, rv)
        duv_ref[:, :SGU_W] = (du2_ref[...] * duvg[:, :SGU_W]).astype(MXU)
        duv_ref[:, SGU_W:] = (dv2 * duvg[:, SGU_W:]).astype(MXU)

    row = lambda n: pl.BlockSpec((tm, n), lambda i: (i, 0))
    return pl.pallas_call(
        body, name="mix_bwd", grid=(S // tm,),
        in_specs=[row(D_MODEL), row(2 * D_MODEL), row(D_MODEL), row(D_MODEL), row(SSM_W), row(2 * SGU_W),
                  _full(w_out.shape), _full(w_pa.shape), _full(w_pb.shape), _full(w_glu.shape), _full(b_glu.shape),
                  _full(g_sgu.shape), _full(ws.shape), _full(ws_t.shape), _full(bias_s.shape)],
        out_specs=[row(2 * D_MODEL), row(D_MODEL), row(D_MODEL), row(SSM_W), row(SSM_W), row(2 * SGU_W),
                   _full((1, SSM_W)), _full((1, SGU_W)), _full((SGU_G, CHUNK, CHUNK)), _full((CHUNK, SGU_W))],
        out_shape=[_sds((S, 2 * D_MODEL), MXU), _sds((S, D_MODEL), MXU), _sds((S, D_MODEL), MXU), _sds((S, SSM_W), MXU),
                   _sds((S, SSM_W)), _sds((S, 2 * SGU_W), MXU),
                   _sds((1, SSM_W)), _sds((1, SGU_W)), _sds((SGU_G, CHUNK, CHUNK)), _sds((CHUNK, SGU_W))],
        scratch_shapes=[pltpu.VMEM((tm, SGU_W), F32), pltpu.VMEM((tm, SGU_W), F32)],
        compiler_params=_cp("arbitrary"),
    )(*_in_hbm([dx1, gl, ya, yb, ys, uv, w_out, w_pa, w_pb, w_glu, b_glu, g_sgu, ws, ws_t, bias_s]))


def _s5_bwd(dys, us, st_re, st_im, abar_re, abar_im, b_re, b_im, c_re, c_im, d_skip, tm):
    S = us.shape[0]
    nt = S // tm
    w = 8 * SSM_P
    hb = tm // 8
    run = tm // 8
    assert run & (run - 1) == 0

    def body(dys_ref, us_ref, str_ref, sti_ref, hr_ref, hi_ref, ar_ref, ai_ref, br_ref, bi_ref, cr_ref, ci_ref, d_ref,
             dus_ref, dab_ref, dd_ref, dbr_ref, dbi_ref, dcr_ref, dci_ref,
             tab_ref, car_ref, gr_ref, gi_ref, dyp_ref, up_ref, dun_ref):
        i = pl.program_id(1)
        ri = nt - 1 - i

        @pl.when(i == 0)
        def _():
            car_ref[...] = jnp.zeros_like(car_ref)
            for k, t in enumerate(_scan_tables(*_cpow2(ar_ref[...], -ai_ref[...], run.bit_length() - 1), True)):
                tab_ref[k] = t
            for r in (dab_ref, dd_ref, dbr_ref, dbi_ref, dcr_ref, dci_ref):
                r[...] = jnp.zeros_like(r)

        _runs_load(dys_ref, dyp_ref, run)
        _runs_load(us_ref, up_ref, run)
        dyb = dyp_ref[...].astype(MXU)
        gr_ref[...] = _dot(dyb, cr_ref[0])
        gi_ref[...] = -_dot(dyb, ci_ref[0])
        ar = jnp.broadcast_to(ar_ref[...], (8, w))
        ai = jnp.broadcast_to(-ai_ref[...], (8, w))

        def advance(kk, state):
            r0 = pl.multiple_of((run - 1 - kk) * 8, 8)
            gr, gi = state
            return (ar * gr - ai * gi + gr_ref[pl.ds(r0, 8), :], ar * gi + ai * gr + gi_ref[pl.ds(r0, 8), :])

        def emit(kk, state):
            r0 = pl.multiple_of((run - 1 - kk) * 8, 8)
            gr, gi = advance(kk, state)
            gr_ref[pl.ds(r0, 8), :] = gr
            gi_ref[pl.ds(r0, 8), :] = gi
            return gr, gi

        zero = jnp.zeros((8, w), F32)
        er, ei = lax.fori_loop(0, run, advance, (zero, zero))
        cr, ci = car_ref[0:1, :], car_ref[1:2, :]
        tr, ti = _scan_group(er, ei, tab_ref, cr, ci, True)
        r8 = lax.broadcasted_iota(jnp.int32, (8, w), 0)
        start = (jnp.where(r8 == 7, cr, pltpu.roll(tr, 7, 0)), jnp.where(r8 == 7, ci, pltpu.roll(ti, 7, 0)))
        car_ref[0:1, :] = tr[0:1, :]
        car_ref[1:2, :] = ti[0:1, :]
        lax.fori_loop(0, run, emit, start)

        gsr = gr_ref[...]
        gsi = gi_ref[...]
        sr = str_ref[...]
        si = sti_ref[...]
        first = ri == 0

        def previous(s, halo_ref):
            head = jnp.where(r8 == 0, jnp.where(first, 0.0, halo_ref[7:8, :]), pltpu.roll(s[tm - 8:tm, :], 1, 0))
            return jnp.concatenate([head, s[0:tm - 8, :]], axis=0)

        spr = previous(sr, hr_ref)
        spi = previous(si, hi_ref)
        dab_ref[0, 0:1, :] += _rowsum(gsr * spr + gsi * spi)
        dab_ref[0, 1:2, :] += _rowsum(gsi * spr - gsr * spi)

        gbr = gsr.astype(MXU)
        gbi = gsi.astype(MXU)
        _runs_store(_dot_nt(gbr, br_ref[0]) + _dot_nt(gbi, bi_ref[0]), dun_ref, run)
        dys_v = dys_ref[...]
        dus_ref[...] = (dun_ref[...] + d_ref[...] * dys_v).astype(MXU)
        dd_ref[0, 0:1, :] += _rowsum(dys_v * us_ref[...])
        ub = up_ref[...].astype(MXU)
        dbr_ref[0] += _dot_tn(ub, gbr)
        dbi_ref[0] += _dot_tn(ub, gbi)
        dcr_ref[0] += _dot_tn(dyb, sr.astype(MXU))
        dci_ref[0] -= _dot_tn(dyb, si.astype(MXU))

    blk = lambda: pl.BlockSpec((1, 8 * SSM_H, w), lambda j, i: (j, 0, 0))
    rowl = lambda: pl.BlockSpec((tm, LANES), lambda j, i: (nt - 1 - i, j))
    roww = lambda: pl.BlockSpec((tm, w), lambda j, i: (nt - 1 - i, j))
    halo = lambda: pl.BlockSpec((8, w), lambda j, i: (jnp.maximum((nt - 1 - i) * hb - 1, 0), j))
    return pl.pallas_call(
        body, name="s5_bwd", grid=(SSM_BLK, nt),
        in_specs=[rowl(), rowl(), roww(), roww(), halo(), halo(),
                  pl.BlockSpec((1, w), lambda j, i: (0, j)), pl.BlockSpec((1, w), lambda j, i: (0, j)),
                  blk(), blk(), blk(), blk(),
                  pl.BlockSpec((1, LANES), lambda j, i: (0, j))],
        out_specs=[rowl(),
                   pl.BlockSpec((1, 8, w), lambda j, i: (j, 0, 0)), pl.BlockSpec((1, 8, LANES), lambda j, i: (j, 0, 0)),
                   blk(), blk(), blk(), blk()],
        out_shape=[_sds((S, SSM_W), MXU), _sds((SSM_BLK, 8, w)), _sds((SSM_BLK, 8, LANES)),
                   _sds((SSM_BLK, 8 * SSM_H, w)), _sds((SSM_BLK, 8 * SSM_H, w)),
                   _sds((SSM_BLK, 8 * SSM_H, w)), _sds((SSM_BLK, 8 * SSM_H, w))],
        scratch_shapes=[pltpu.VMEM((8, 8, w), F32), pltpu.VMEM((8, w), F32),
                        pltpu.VMEM((tm, w), F32), pltpu.VMEM((tm, w), F32),
                        pltpu.VMEM((tm, LANES), F32), pltpu.VMEM((tm, LANES), F32), pltpu.VMEM((tm, LANES), F32)],
        compiler_params=_cp("parallel", "arbitrary"),
    )(*_in_hbm([dys, us, st_re, st_im, st_re, st_im, abar_re, abar_im, b_re, b_im, c_re, c_im, d_skip]))


def _in_bwd(dus, duv, dgl, dx1, x, g_mix, w_in, tm):
    S = x.shape[0]

    def body(dus_ref, duv_ref, dgl_ref, dx1_ref, x_ref, g_ref, w_ref, gx_ref, dg_ref):
        @pl.when(pl.program_id(0) == 0)
        def _():
            dg_ref[...] = jnp.zeros_like(dg_ref)

        dh = (_dot(dus_ref[...], w_ref[0:SSM_W, :])
              + _dot(duv_ref[...], w_ref[SSM_W:SSM_W + 2 * SGU_W, :])
              + _dot(dgl_ref[...], w_ref[SSM_W + 2 * SGU_W:, :]))
        xv = x_ref[...]
        r = _rms(xv)
        xn = xv * r
        dg_ref[...] += _rowsum(dh * xn)
        gx_ref[...] = dx1_ref[...] + _rms_bwd(dh * g_ref[...], xn, r)

    row = lambda n: pl.BlockSpec((tm, n), lambda i: (i, 0))
    return pl.pallas_call(
        body, name="in_bwd", grid=(S // tm,),
        in_specs=[row(SSM_W), row(2 * SGU_W), row(2 * D_MODEL), row(D_MODEL), row(D_MODEL), _full((1, D_MODEL)),
                  _full(w_in.shape)],
        out_specs=[row(D_MODEL), _full((1, D_MODEL))],
        out_shape=[_sds((S, D_MODEL)), _sds((1, D_MODEL))],
        compiler_params=_cp("arbitrary"),
    )(*_in_hbm([dus, duv, dgl, dx1, x, g_mix, w_in]))


def _pick(n, cands):
    for c in cands:
        if n % c == 0:
            return c
    return n


def _wgrad_split(a, b, nsplit, tk, name):
    S, K = a.shape
    N = b.shape[1]
    c = N // nsplit

    def body(a_ref, b_ref, o_ref):
        prod = _dot_tn(a_ref[...], b_ref[...])
        for d in range(nsplit):
            o_ref[d] = prod[:, c * d:c * (d + 1)].astype(MXU)

    return pl.pallas_call(
        body, name=name, grid=(K // tk,),
        in_specs=[pl.BlockSpec((S, tk), lambda k: (0, k)), _full((S, N))],
        out_specs=pl.BlockSpec((nsplit, tk, c), lambda k: (0, k, 0)),
        out_shape=_sds((nsplit, K, c), MXU),
        compiler_params=_cp("parallel"),
    )(*_in_hbm([a, b]))


def _wgrad_in_t(dps, h1, name):
    S, K = h1.shape
    cw = 512
    counts = [b.shape[1] // cw for b in dps]
    starts = [sum(counts[:i]) for i in range(len(dps))]
    nblk = sum(counts)

    def body(*refs):
        b_refs = refs[:len(dps)]
        h_ref, o_ref = refs[len(dps):]
        j = pl.program_id(0)
        for b_ref, st, cnt in zip(b_refs, starts, counts):
            @pl.when(jnp.logical_and(j >= st, j < st + cnt))
            def _():
                o_ref[...] = _dot_tn(b_ref[...], h_ref[...]).astype(MXU)

    def src_spec(st, cnt):
        return pl.BlockSpec((S, cw), lambda j: (0, jnp.clip(j - st, 0, cnt - 1)))

    return pl.pallas_call(
        body, name=name, grid=(nblk,),
        in_specs=[src_spec(st, cnt) for st, cnt in zip(starts, counts)] + [_full((S, K))],
        out_specs=pl.BlockSpec((cw, K), lambda j: (j, 0)),
        out_shape=_sds((nblk * cw, K), MXU),
        compiler_params=_cp("arbitrary"),
    )(*_in_hbm([*dps, h1]))


def _wgrad_blk(a3, b3, nblk, a_of, b_of, name):
    S, K = a3.shape[1:]
    N = b3.shape[2]

    def body(a_ref, b_ref, o_ref):
        o_ref[0] = _dot_tn(a_ref[0], b_ref[0]).astype(MXU)

    return pl.pallas_call(
        body, name=name, grid=(nblk,),
        in_specs=[pl.BlockSpec((1, S, K), lambda b: (a_of(b), 0, 0)),
                  pl.BlockSpec((1, S, N), lambda b: (b_of(b), 0, 0))],
        out_specs=pl.BlockSpec((1, K, N), lambda b: (b, 0, 0)),
        out_shape=_sds((nblk, K, N), MXU),
        compiler_params=pltpu.CompilerParams(dimension_semantics=("parallel",), vmem_limit_bytes=WGRAD_VMEM_LIMIT),
    )(*_in_hbm([a3, b3]))


def _assemble_cols(blocks_list, name):
    def body(*refs):
        n = len(blocks_list)
        for b_ref, o_ref in zip(refs[:n], refs[n:]):
            c = b_ref.shape[2]
            for d in range(N_DEV):
                o_ref[:, c * d:c * (d + 1)] = b_ref[d]

    outs = [_sds((b.shape[1], N_DEV * b.shape[2]), b.dtype) for b in blocks_list]
    return pl.pallas_call(
        body, name=name, grid=(1,), in_specs=[_full(b.shape) for b in blocks_list],
        out_specs=[_full(o.shape) for o in outs], out_shape=outs, compiler_params=_cp("arbitrary"),
    )(*_in_hbm(blocks_list))


def _tile(S, want):
    return want if S % want == 0 else S


def _local_step(x, tgt, p, mixer_relay, mixer_weights, ffn_weights, grads_out):
    S = x.shape[0]
    tm = _tile(S, 256)
    tl = _tile(S, 512)

    rep = lambda a: jnp.repeat(a, SSM_H, axis=0)
    are = rep(p["a_re"])
    aim = rep(p["a_im"])
    ldt = jnp.broadcast_to(rep(p["log_dt"].reshape(SSM_G, 1)), are.shape)
    br_t = p["b_re_t"].reshape(are.shape)
    bi_t = p["b_im_t"].reshape(are.shape)
    abr, abi, bbr, bbi = _s5_params_fwd(are, aim, ldt, br_t, bi_t)
    head = lambda a: a.reshape(SSM_G, SSM_H, SSM_P)[:, 0, :].reshape(1, SSM_G * SSM_P)
    abar_re, abar_im = head(abr), head(abi)
    bd_br = _blockdiag(bbr).astype(MXU)
    bd_bi = _blockdiag(bbi).astype(MXU)
    bd_cr = _blockdiag(p["c_re"].reshape(are.shape)).astype(MXU)
    bd_ci = _blockdiag(p["c_im"].reshape(are.shape)).astype(MXU)
    d_skip = p["d_skip"].reshape(1, SSM_W)

    tril = jnp.tril(jnp.ones((CHUNK, CHUNK), dtype=bool))
    ws = jnp.where(tril[None], p["w_s"], 0.0)
    ws_b = ws.astype(MXU)
    ws_t = ws.transpose(0, 2, 1).astype(MXU)
    bias_s = jnp.repeat(p["b_s"].T, SGU_D, axis=1)

    g_mix = p["g_mix"].reshape(1, D_MODEL)
    g_ffn = p["g_ffn"].reshape(1, D_MODEL)
    g_final = p["g_final"].reshape(1, D_MODEL)
    g_sgu = p["g_sgu"].reshape(1, SGU_W)
    b_glu = p["b_glu"].reshape(1, SSM_W)
    conv_b = p["conv_b"].reshape(2 * FF_NCB, 1, FF_CW)
    tf = _tile(S, 256)

    h1, us, uv, gl = _in_fwd(x, g_mix, p["w_in_t"], tl)
    token = mixer_relay(us)
    st_re, st_im, ys = _s5_fwd(us, abar_re, abar_im, bd_br, bd_bi, bd_cr, bd_ci, d_skip + token[0:1, 0:1], tl)
    p = dict(p, **mixer_weights(ys))
    yg, yap, sg, ya, yb, m, x1, h2 = _mix_fwd(x, ys, uv, gl, p["w_glu"], b_glu, p["w_proj_a"], g_sgu, ws_b, bias_s,
                                              p["w_proj_b"], p["w_out"], g_ffn, tm)
    w_up, conv_w, w_down = ffn_weights(h2)
    pair_lanes = lambda a: a.reshape(N_DEV // 2, 2, a.shape[1], FF_SHARD).transpose(0, 2, 1, 3).reshape(
        N_DEV // 2, a.shape[1], FF_CW)
    w_up = w_up.reshape(2 * FF_NCB, FF_CW, D_MODEL)
    conv_w = pair_lanes(conv_w)
    up, ab, ff, dx2, dx2b, loss, dg_final = _ffn_fwd(h2, x1, tgt, w_up, conv_w, conv_b, w_down, g_final, tf)

    dup, dx1, dx1b, dconv, dg_ffn = _ffn_bwd(dx2, up, ab, x1, w_up, conv_w, w_down, g_ffn, tf)
    rows8 = lambda g: g.reshape(N_DEV, g.shape[1] // N_DEV, g.shape[2])
    g_up = _wgrad_blk(dup.reshape(2 * FF_NCB, S, FF_CW), h2[None], 2 * FF_NCB, lambda b: b, lambda b: 0,
                      "wgrad_up").reshape(N_DEV, FF_SHARD, D_MODEL)
    g_down = _wgrad_blk(ff, dx2b[None], FF_NCB, lambda b: b, lambda b: 0, "wgrad_down").reshape(
        N_DEV, D_FF // N_DEV, D_MODEL)
    token = grads_out(("w_up", "w_down"), (g_up, g_down))
    dgl, dya, dyb, dz, dys, duv, db_glu, dg_sgu, dws, dbs = _mix_bwd(
        dx1, gl, ya, yb, ys, uv, p["w_out"], p["w_proj_a"], p["w_proj_b"], p["w_glu"], b_glu + token[0:1, 0:1], g_sgu,
        ws_b, ws_t, bias_s, tm)
    token = grads_out(("w_glu", "w_proj_a", "w_proj_b", "w_out"),
                      (rows8(_wgrad_split(yg, dz, 1, SSM_W, "wgrad_glu")),
                       _wgrad_split(yap, dya, N_DEV, SSM_W, "wgrad_pa"),
                       _wgrad_split(sg, dyb, N_DEV, SGU_W, "wgrad_pb"),
                       rows8(_wgrad_split(m, dx1b, 1, 512, "wgrad_out"))))
    dus, dab, dd, dbbr, dbbi, dcr, dci = _s5_bwd(dys, us, st_re, st_im, abar_re, abar_im, bd_br, bd_bi, bd_cr, bd_ci,
                                                 d_skip + token[0:1, 0:1], tl)
    g_in = _wgrad_in_t([dus, duv, dgl], h1, "wgrad_in")
    token = grads_out(("w_in",), (g_in.reshape(N_DEV, g_in.shape[0] // N_DEV, D_MODEL),))
    grad_x, dg_mix = _in_bwd(dus, duv, dgl, dx1, x, g_mix + token[0:1, 0:1], p["w_in_t"], tl)

    spread = lambda v: jnp.repeat(v.reshape(SSM_G, SSM_P), SSM_H, axis=0) * (1.0 / SSM_H)
    dabr = spread(dab[:, 0, :])
    dabi = spread(dab[:, 1, :])
    dare, daim, dldt, dbr_t, dbi_t = _s5_params_bwd(are, aim, ldt, br_t, bi_t, dabr, dabi,
                                                    _unblockdiag(dbbr), _unblockdiag(dbbi))
    fold = lambda a: a.reshape(SSM_G, SSM_H, SSM_P).sum(axis=1)

    grads = {
        "g_mix": dg_mix,
        "a_re": fold(dare), "a_im": fold(daim), "log_dt": fold(dldt).sum(axis=1),
        "b_re": dbr_t, "b_im": dbi_t,
        "c_re": _unblockdiag(dcr).reshape(SSM_G, SSM_H, SSM_P),
        "c_im": _unblockdiag(dci).reshape(SSM_G, SSM_H, SSM_P),
        "d_skip": dd[:, 0, :].reshape(SSM_W),
        "b_glu": db_glu,
        "g_sgu": dg_sgu,
        "w_s": dws,
        "b_s": dbs.reshape(CHUNK, SGU_G, SGU_D).sum(axis=-1).T,
        "g_ffn": dg_ffn,
        "conv_w": dconv[:, 0:3, :].reshape(N_DEV // 2, 3, 2, FF_SHARD).transpose(0, 2, 1, 3).reshape(
            N_DEV, 3, FF_SHARD),
        "conv_b": dconv[:, 3, :].reshape(2 * D_FF),
        "g_final": dg_final,
    }
    return loss, grad_x, grads


_ANY = pl.BlockSpec(memory_space=pl.ANY)
_MESH = pl.DeviceIdType.MESH


def _allgather(shards, dtypes, name, cast_only=()):
    n = len(shards)
    e = len(cast_only)

    def body(*refs):
        in_refs, extra_in = refs[:n], refs[n:n + e]
        out_refs, extra_out = refs[n + e:2 * n + e], refs[2 * n + e:2 * n + 2 * e]
        stage = refs[2 * n + 2 * e:3 * n + 2 * e]
        send_sems, recv_sems, local_sems = refs[3 * n + 2 * e:]
        for a in range(n):
            stage[a][...] = in_refs[a][...].astype(dtypes[a])
        for i in range(e):
            extra_out[i][...] = extra_in[i][...].astype(MXU)
        x, y, c = lax.axis_index("x"), lax.axis_index("y"), lax.axis_index("c")
        me, sibling = (x, y, c), (x, y, 1 - c)
        chips = [(1 - x, y), (x, 1 - y), (1 - x, 1 - y)]

        def slot(a, px, py, pc):
            return out_refs[a].at[4 * px + 2 * py + pc]

        def copy(a, k, block, to, src=None):
            return pltpu.make_async_remote_copy(
                src_ref=slot(a, *block) if src is None else src, dst_ref=slot(a, *block),
                send_sem=send_sems.at[a, k], recv_sem=recv_sems.at[a, k], device_id=to, device_id_type=_MESH)

        mine = [pltpu.make_async_copy(stage[a], slot(a, *me), local_sems.at[a]) for a in range(n)]
        for cp in mine:
            cp.start()
        first = []
        for j, chip in enumerate(chips):
            first += [copy(a, 1 + j, me, (*chip, c), src=stage[a]) for a in range(n)]
        first += [copy(a, 0, me, sibling, src=stage[a]) for a in range(n)]
        for cp in first:
            cp.start()
        passed = []
        for j, chip in enumerate(chips):
            for a in range(n):
                copy(a, 1 + j, (*chip, c), me).wait_recv()
                fwd = copy(a, 4 + j, (*chip, c), sibling)
                fwd.start()
                passed.append(fwd)
        for a in range(n):
            copy(a, 0, sibling, me).wait_recv()
        for j, chip in enumerate(chips):
            for a in range(n):
                copy(a, 4 + j, (*chip, 1 - c), me).wait_recv()
        for cp in first + passed:
            cp.wait_send()
        for cp in mine:
            cp.wait()

    res = pl.pallas_call(
        body, name=name, grid=(1,), in_specs=[_full(s.shape) for s in list(shards) + list(cast_only)],
        out_specs=[_ANY] * n + [_full(s.shape) for s in cast_only],
        out_shape=[_sds((N_DEV,) + s.shape, dt) for s, dt in zip(shards, dtypes)]
                  + [_sds(s.shape, MXU) for s in cast_only],
        scratch_shapes=[pltpu.VMEM(s.shape, dt) for s, dt in zip(shards, dtypes)]
                       + [pltpu.SemaphoreType.DMA((n, 7)), pltpu.SemaphoreType.DMA((n, 7)), pltpu.SemaphoreType.DMA((n,))],
        compiler_params=pltpu.CompilerParams(vmem_limit_bytes=VMEM_LIMIT),
    )(*_in_hbm([*shards, *cast_only]))
    return res[:n], res[n:]


def _all_to_all(sends, name):
    n = len(sends)

    def body(*refs):
        send_refs, recv_refs = refs[:n], refs[n:2 * n]
        send_sems, recv_sems, local_sems = refs[2 * n:]
        x, y, c = lax.axis_index("x"), lax.axis_index("y"), lax.axis_index("c")
        me = 4 * x + 2 * y + c
        mine = [pltpu.make_async_copy(send_refs[a].at[me], recv_refs[a].at[me], local_sems.at[a]) for a in range(n)]
        for cp in mine:
            cp.start()
        copies = []
        for k in (2, 4, 6, 3, 5, 7, 1):
            px = 1 - x if k & 4 else x
            py = 1 - y if k & 2 else y
            pc = 1 - c if k & 1 else c
            peer = 4 * px + 2 * py + pc
            for a in range(n):
                sems = dict(send_sem=send_sems.at[a, k - 1], recv_sem=recv_sems.at[a, k - 1],
                            device_id=(px, py, pc), device_id_type=_MESH)
                cp = pltpu.make_async_remote_copy(src_ref=send_refs[a].at[peer], dst_ref=recv_refs[a].at[me], **sems)
                cp.start()
                landing = pltpu.make_async_remote_copy(src_ref=send_refs[a].at[peer], dst_ref=recv_refs[a].at[peer],
                                                       **sems)
                copies.append((cp, landing))
        for _, landing in copies:
            landing.wait_recv()
        for cp, _ in copies:
            cp.wait_send()
        for cp in mine:
            cp.wait()

    return pl.pallas_call(
        body, name=name, in_specs=[_ANY] * n, out_specs=[_ANY] * n,
        out_shape=[_sds(s.shape, s.dtype) for s in sends],
        scratch_shapes=[pltpu.SemaphoreType.DMA((n, 7)), pltpu.SemaphoreType.DMA((n, 7)), pltpu.SemaphoreType.DMA((n,))],
    )(*sends)


_HBM = pl.BlockSpec(memory_space=pltpu.HBM)
_SEM = pl.BlockSpec(memory_space=pltpu.SEMAPHORE)
_EFFECT = pltpu.SideEffectType.DATAFLOW_SIDE_EFFECTING
_PEER_ORDER = (2, 4, 6, 3, 5, 7, 1)


def _peer(k):
    x, y, c = lax.axis_index("x"), lax.axis_index("y"), lax.axis_index("c")
    px = 1 - x if k & 4 else x
    py = 1 - y if k & 2 else y
    pc = 1 - c if k & 1 else c
    return (px, py, pc), 4 * px + 2 * py + pc


_SAME_CORE_AND_SIBLING = (2, 4, 6, 1)


def _push_start(srcs, lands, slotted, name, peers=_PEER_ORDER):
    n = len(srcs)

    def body(*refs):
        src_refs, land_refs = refs[:n], refs[n:2 * n]
        send_sems, recv_sems, token = refs[2 * n], refs[2 * n + 1], refs[-1]
        me = 4 * lax.axis_index("x") + 2 * lax.axis_index("y") + lax.axis_index("c")
        for k in peers:
            dev, peer = _peer(k)
            for a in range(n):
                pltpu.make_async_remote_copy(
                    src_ref=src_refs[a].at[peer] if slotted else src_refs[a], dst_ref=land_refs[a].at[me],
                    send_sem=send_sems.at[7 * a + k - 1], recv_sem=recv_sems.at[7 * a + k - 1],
                    device_id=dev, device_id_type=_MESH).start()
        token[...] = jnp.zeros_like(token)

    bufs = list(srcs) + list(lands)
    res = pl.pallas_call(
        body, name=name, in_specs=[_HBM] * (2 * n),
        out_specs=(_SEM, _SEM, *[_HBM] * (2 * n), pl.BlockSpec(memory_space=pltpu.VMEM)),
        out_shape=(pltpu.SemaphoreType.DMA((7 * n,)), pltpu.SemaphoreType.DMA((7 * n,)),
                   *[pltpu.HBM(b.shape, b.dtype) for b in bufs], _sds((8, LANES))),
        input_output_aliases={i: 2 + i for i in range(2 * n)},
        compiler_params=pltpu.CompilerParams(has_side_effects=_EFFECT),
    )(*[pltpu.with_memory_space_constraint(b, pltpu.HBM) for b in bufs])
    return res[0], res[1], res[2:2 + n], res[2 + n:2 + 2 * n], res[-1]


def _push_wait(send_sems, recv_sems, srcs, lands, slotted, after, name, peers=_PEER_ORDER):
    n = len(srcs)

    def body(*refs):
        src_refs, land_refs = refs[:n], refs[n:2 * n]
        send_sems, recv_sems = refs[2 * n], refs[2 * n + 1]
        for k in peers:
            dev, peer = _peer(k)
            for a in range(n):
                cp = pltpu.make_async_remote_copy(
                    src_ref=src_refs[a].at[peer] if slotted else src_refs[a], dst_ref=land_refs[a].at[peer],
                    send_sem=send_sems.at[7 * a + k - 1], recv_sem=recv_sems.at[7 * a + k - 1],
                    device_id=dev, device_id_type=_MESH)
                cp.wait_send()
                cp.wait_recv()

    bufs = list(srcs) + list(lands)
    res = pl.pallas_call(
        body, name=name, in_specs=[_HBM] * (2 * n) + [_SEM, _SEM] + [_ANY] * len(after), out_specs=[_HBM] * (2 * n),
        out_shape=[pltpu.HBM(b.shape, b.dtype) for b in bufs],
        input_output_aliases={i: i for i in range(2 * n)},
        compiler_params=pltpu.CompilerParams(has_side_effects=_EFFECT),
    )(*bufs, send_sems, recv_sems, *after)
    return res[n:]


def _other_chips():
    x, y = lax.axis_index("x"), lax.axis_index("y")
    return ((1 - x, y), (x, 1 - y), (1 - x, 1 - y))


def _relay_start(lands, name):
    n = len(lands)

    def body(*refs):
        land_refs = refs[:n]
        send_sems, recv_sems, token = refs[n], refs[n + 1], refs[-1]
        x, y, c = lax.axis_index("x"), lax.axis_index("y"), lax.axis_index("c")
        for j, (px, py) in enumerate(_other_chips()):
            slot = 4 * px + 2 * py + c
            for a in range(n):
                pltpu.make_async_remote_copy(
                    src_ref=land_refs[a].at[slot], dst_ref=land_refs[a].at[slot],
                    send_sem=send_sems.at[3 * a + j], recv_sem=recv_sems.at[3 * a + j],
                    device_id=(x, y, 1 - c), device_id_type=_MESH).start()
        token[...] = jnp.zeros_like(token)

    res = pl.pallas_call(
        body, name=name, in_specs=[_HBM] * n,
        out_specs=(_SEM, _SEM, *[_HBM] * n, pl.BlockSpec(memory_space=pltpu.VMEM)),
        out_shape=(pltpu.SemaphoreType.DMA((3 * n,)), pltpu.SemaphoreType.DMA((3 * n,)),
                   *[pltpu.HBM(b.shape, b.dtype) for b in lands], _sds((8, LANES))),
        input_output_aliases={i: 2 + i for i in range(n)},
        compiler_params=pltpu.CompilerParams(has_side_effects=_EFFECT),
    )(*[pltpu.with_memory_space_constraint(b, pltpu.HBM) for b in lands])
    return res[0], res[1], res[2:2 + n], res[-1]


def _relay_wait(send_sems, recv_sems, lands, after, name):
    n = len(lands)

    def body(*refs):
        land_refs = refs[:n]
        send_sems, recv_sems = refs[n], refs[n + 1]
        x, y, c = lax.axis_index("x"), lax.axis_index("y"), lax.axis_index("c")
        for j, (px, py) in enumerate(_other_chips()):
            sent, received = 4 * px + 2 * py + c, 4 * px + 2 * py + (1 - c)
            for a in range(n):
                cp = pltpu.make_async_remote_copy(
                    src_ref=land_refs[a].at[sent], dst_ref=land_refs[a].at[received],
                    send_sem=send_sems.at[3 * a + j], recv_sem=recv_sems.at[3 * a + j],
                    device_id=(x, y, 1 - c), device_id_type=_MESH)
                cp.wait_send()
                cp.wait_recv()

    return pl.pallas_call(
        body, name=name, in_specs=[_HBM] * n + [_SEM, _SEM] + [_ANY] * len(after), out_specs=[_HBM] * n,
        out_shape=[pltpu.HBM(b.shape, b.dtype) for b in lands],
        input_output_aliases={i: i for i in range(n)},
        compiler_params=pltpu.CompilerParams(has_side_effects=_EFFECT),
    )(*lands, send_sems, recv_sems, *after)


def _adamw(w, g, m, v):
    m2 = ADAM_B1 * m + (1.0 - ADAM_B1) * g
    v2 = ADAM_B2 * v + (1.0 - ADAM_B2) * (g * g)
    m_hat = m2 / (1.0 - ADAM_B1 ** ADAM_STEP)
    v_hat = v2 / (1.0 - ADAM_B2 ** ADAM_STEP)
    delta = -ADAM_LR * (m_hat / (jnp.sqrt(v_hat) + ADAM_EPS) + ADAM_WD * w)
    return delta, m2, v2


def _adam_shard(parts, w, m, v, name):
    _, r, c = w.shape
    tr = max(t for t in range(16, 257, 16) if r % t == 0)

    def body(p_ref, w_ref, m_ref, v_ref, g_ref, d_ref, m2_ref, v2_ref):
        g = p_ref[0].astype(F32)
        for s in range(1, N_DEV):
            g = g + p_ref[s].astype(F32)
        g_ref[0] = g
        d_ref[0], m2_ref[0], v2_ref[0] = _adamw(w_ref[0], g, m_ref[0], v_ref[0])

    row = lambda: pl.BlockSpec((1, tr, c), lambda i: (0, i, 0))
    return pl.pallas_call(
        body, name=name, grid=(r // tr,),
        in_specs=[pl.BlockSpec((N_DEV, tr, c), lambda i: (0, i, 0)), row(), row(), row()],
        out_specs=[row(), row(), row(), row()], out_shape=[_sds((1, r, c))] * 4,
        compiler_params=_cp("parallel"),
    )(*_in_hbm([parts, w, m, v]))


def _adam_small(gs, ws, ms, vs, name):
    n = len(gs)

    def body(*refs):
        ins, outs = refs[:4 * n], refs[4 * n:]
        for i in range(n):
            g = ins[i][...]
            d, m2, v2 = _adamw(ins[n + i][...], g, ins[2 * n + i][...], ins[3 * n + i][...])
            outs[i][...] = d
            outs[n + i][...] = m2
            outs[2 * n + i][...] = v2

    res = pl.pallas_call(
        body, name=name, grid=(1,), in_specs=[_full(w.shape) for w in ws] * 4,
        out_specs=[_full(w.shape) for w in ws] * 3, out_shape=[_sds(w.shape) for w in ws] * 3,
        compiler_params=_cp("arbitrary"),
    )(*_in_hbm([*gs, *ws, *ms, *vs]))
    return res[:n], res[n:2 * n], res[2 * n:]


def _sum_slots(parts, name):
    R = parts.shape[1]

    def body(p_ref, o_ref):
        g = p_ref[0]
        for s in range(1, N_DEV):
            g = g + p_ref[s]
        o_ref[...] = g

    return pl.pallas_call(body, name=name, grid=(1,), in_specs=[_full(parts.shape)], out_specs=_full((R, LANES)),
                          out_shape=_sds((R, LANES)))(*_in_hbm([parts]))


def _pad_to(a, n, axis):
    extra = n - a.shape[axis]
    if extra == 0:
        return a
    widths = [(0, 0)] * a.ndim
    widths[axis] = (0, extra)
    return jnp.pad(a, widths)


def _ceil_to(n, k):
    return -(-n // k) * k


def _pack_rows(flats, rows_multiple):
    parts = [_pad_to(f, _ceil_to(f.shape[-1], LANES), f.ndim - 1) for f in flats]
    cat = jnp.concatenate(parts, axis=-1)
    total = _ceil_to(cat.shape[-1], LANES * rows_multiple)
    cat = _pad_to(cat, total, cat.ndim - 1)
    return cat.reshape(cat.shape[:-1] + (total // LANES, LANES))


def _unpack_rows(buf, sizes):
    flat = buf.reshape(buf.shape[:-2] + (-1,))
    out, off = [], 0
    for n in sizes:
        out.append(flat[..., off:off + n])
        off += _ceil_to(n, LANES)
    return out


_MIX_BIG = ("w_in", "w_glu", "w_proj_a", "w_proj_b", "w_out")
_BIG = _MIX_BIG + ("w_up", "w_down")
_SMALL = ("g_mix", "a_re", "a_im", "log_dt", "b_re", "b_im", "c_re", "c_im", "d_skip", "b_glu", "g_sgu", "w_s", "b_s",
          "g_ffn", "conv_b", "g_final")
_SMALL_ROWS_MULTIPLE = 8 * N_DEV
_TRANSPOSED = ("w_in", "w_up", "b_re", "b_im")


def _as_2d(a):
    return a.reshape(-1, a.shape[-1]) if a.ndim > 1 else a.reshape(1, -1)


def kernel(x, g_mix, w_in, a_re, a_im, log_dt, b_re, b_im, c_re, c_im, d_skip, w_glu, b_glu, w_proj_a, g_sgu, w_s, b_s, w_proj_b, w_out, g_ffn, w_up, conv_w, conv_b, w_down, g_final, loss_target, m_g_mix, m_w_in, m_a_re, m_a_im, m_log_dt, m_b_re, m_b_im, m_c_re, m_c_im, m_d_skip, m_w_glu, m_b_glu, m_w_proj_a, m_g_sgu, m_w_s, m_b_s, m_w_proj_b, m_w_out, m_g_ffn, m_w_up, m_conv_w, m_conv_b, m_w_down, m_g_final, v_g_mix, v_w_in, v_a_re, v_a_im, v_log_dt, v_b_re, v_b_im, v_c_re, v_c_im, v_d_skip, v_w_glu, v_b_glu, v_w_proj_a, v_g_sgu, v_w_s, v_b_s, v_w_proj_b, v_w_out, v_g_ffn, v_w_up, v_conv_w, v_conv_b, v_w_down, v_g_final):
    args = dict(locals())
    me = 4 * lax.axis_index("x") + 2 * lax.axis_index("y") + lax.axis_index("c")

    def own_slot(buf, block):
        return lax.dynamic_update_slice(buf, block[None], (me,) + (0,) * block.ndim)

    for n in _TRANSPOSED:
        for pre in ("", "m_", "v_"):
            args[pre + n] = jnp.swapaxes(args[pre + n], -1, -2)
    later = ("w_glu", "w_proj_a", "w_proj_b", "w_out", "w_up", "w_down")
    (w_in_g,), casts = _allgather([args["w_in"][0]], [MXU], "allgather_w_in", cast_only=[args[n][0] for n in later])
    sh = dict(zip(later, casts))

    def start_push(srcs, tag, peers):
        lands = [own_slot(lax.empty((N_DEV,) + s.shape, s.dtype), s) for s in srcs]
        send_sems, recv_sems, srcs, lands, token = _push_start(srcs, lands, False, "push_" + tag, peers)
        return (send_sems, recv_sems, srcs, lands), token

    mix_push, token_a = start_push([sh[n] for n in later[:4]], "mixer_weights", _SAME_CORE_AND_SIBLING)
    ffn_push, token_b = start_push([sh["w_up"], sh["w_down"], conv_w[0]], "ffn_weights", _PEER_ORDER)
    p = {n: (args[n][0] if n != "g_final" else args[n]) for n in _SMALL if n not in _TRANSPOSED}
    p.update(w_in_t=w_in_g.reshape(SSM_W + 2 * SGU_W + 2 * D_MODEL, D_MODEL),
             b_re_t=args["b_re"][0], b_im_t=args["b_im"][0])
    p["g_mix"] = p["g_mix"] + (token_a[0:1, 0:1] + token_b[0:1, 0:1])
    relay = {}

    def mixer_relay(after):
        lands = _push_wait(*mix_push, False, [after], "wait_mixer_weights", _SAME_CORE_AND_SIBLING)
        relay["send"], relay["recv"], relay["lands"], token = _relay_start(lands, "relay_mixer_weights")
        return token

    def mixer_weights(after):
        w_glu_g, w_pa_g, w_pb_g, w_out_g = _relay_wait(relay["send"], relay["recv"], relay["lands"], [after],
                                                       "wait_relay_mixer_weights")
        w_pa_full, w_pb_full = _assemble_cols([w_pa_g, w_pb_g], "assemble_cols")
        return dict(w_glu=w_glu_g.reshape(SSM_W, SSM_W), w_proj_a=w_pa_full, w_proj_b=w_pb_full,
                    w_out=w_out_g.reshape(D_MODEL, D_MODEL))

    def ffn_weights(after):
        w_up_g, w_down_g, conv_w_g = _push_wait(*ffn_push, False, [after], "wait_ffn_weights")
        return w_up_g, conv_w_g, w_down_g.reshape(D_FF, D_MODEL)

    pushes = []

    def grads_out(names, sends):
        lands = [own_slot(lax.empty(s.shape, s.dtype), lax.dynamic_index_in_dim(s, me, 0, keepdims=False))
                 for s in sends]
        send_sems, recv_sems, srcs, lands, token = _push_start(list(sends), lands, True, "push_grads_" + names[0])
        pushes.append((names, send_sems, recv_sems, srcs, lands))
        return token

    loss_part, grad_x, grads = _local_step(x[0], loss_target[0], p, mixer_relay, mixer_weights, ffn_weights, grads_out)

    small_names = _SMALL + ("conv_w", "loss")
    small_g = dict(grads, loss=loss_part[0, 0:1])
    flats = [small_g[n].reshape(-1) for n in small_names]
    small_sizes = [f.shape[0] for f in flats]
    g_small = _pack_rows(flats, _SMALL_ROWS_MULTIPLE)
    rs8 = g_small.shape[0] // N_DEV
    grads_out(("small",), (g_small.reshape(N_DEV, rs8, LANES),))

    out = {}
    done = [g_small]
    for names, send_sems, recv_sems, srcs, lands in pushes:
        parts = _push_wait(send_sems, recv_sems, srcs, lands, True, done, "wait_grads_" + names[0])
        if names == ("small",):
            recv_small, = parts
            break
        for n, part in zip(names, parts):
            res = _adam_shard(part, args[n], args["m_" + n], args["v_" + n], "adam_" + n)
            for kind, v in zip(("grad_", "delta_", "new_m_", "new_v_"), res):
                out[kind + n] = v
            done = [res[0]]
    small_mine = _sum_slots(recv_small, "sum_small")
    g_small_all = _allgather([small_mine], [F32], "allgather_small")[0][0].reshape(N_DEV * rs8, LANES)
    pieces = dict(zip(small_names, _unpack_rows(g_small_all, small_sizes)))
    loss = pieces["loss"][0]
    dconv_w = lax.dynamic_index_in_dim(pieces["conv_w"].reshape(N_DEV, 3, FF_SHARD), me, axis=0, keepdims=False)
    names2 = _SMALL + ("conv_w",)
    gs = [pieces[n].reshape(_as_2d(args[n]).shape) for n in _SMALL] + [dconv_w]
    ds, m2s, v2s = _adam_small(gs, [_as_2d(args[n]) for n in names2], [_as_2d(args["m_" + n]) for n in names2],
                               [_as_2d(args["v_" + n]) for n in names2], "adam_small")
    for n, res in zip(names2, zip(gs, ds, m2s, v2s)):
        for kind, v in zip(("grad_", "delta_", "new_m_", "new_v_"), res):
            out[kind + n] = v.reshape(args[n].shape)
    order = ("g_mix", "w_in", "a_re", "a_im", "log_dt", "b_re", "b_im", "c_re", "c_im", "d_skip", "w_glu", "b_glu",
             "w_proj_a", "g_sgu", "w_s", "b_s", "w_proj_b", "w_out", "g_ffn", "w_up", "conv_w", "conv_b", "w_down",
             "g_final")
    res = [loss, grad_x.reshape(x.shape)]
    for kind in ("grad_", "delta_", "new_m_", "new_v_"):
        res += [jnp.swapaxes(out[kind + n], -1, -2) if n in _TRANSPOSED else out[kind + n] for n in order]
    return tuple(res)
```

```python
import functools
import math

import jax
import jax.numpy as jnp
from jax import lax
from jax.experimental import pallas as pl
from jax.experimental.pallas import tpu as pltpu

F32 = jnp.float32
MXU = jnp.bfloat16
EPS = 1e-6

D_MODEL = 1024
SSM_W = 512
SSM_G, SSM_H, SSM_P = 32, 16, 64
SSM_BLK = 4
SGU_W = 512
SGU_G, SGU_D, CHUNK = 8, 64, 128
D_FF = 2816
N_DEV = 8
FF_SHARD = 2 * D_FF // N_DEV
FF_CW = 2 * FF_SHARD
FF_NCB = D_FF // FF_CW
LANES = 128

ADAM_LR, ADAM_B1, ADAM_B2, ADAM_EPS, ADAM_WD, ADAM_STEP = 0.001, 0.9, 0.999, 1e-08, 0.01, 10

VMEM_LIMIT = 48 * 1024 * 1024
WGRAD_VMEM_LIMIT = 58 * 1024 * 1024
FFN_VMEM_LIMIT = 58 * 1024 * 1024


def _cp(*sem):
    return pltpu.CompilerParams(dimension_semantics=sem, vmem_limit_bytes=VMEM_LIMIT)


def _full(shape):
    n = len(shape)
    return pl.BlockSpec(shape, lambda *_: (0,) * n)


def _sds(shape, dtype=F32):
    return jax.ShapeDtypeStruct(shape, dtype)


def _in_hbm(arrays):
    return [pltpu.with_memory_space_constraint(a, pltpu.HBM) for a in arrays]


def _dot(a, b):
    return jnp.dot(a, b, preferred_element_type=F32)


def _dot_nt(a, b):
    return lax.dot_general(a, b, (((1,), (1,)), ((), ())), preferred_element_type=F32)


def _dot_tn(a, b):
    return lax.dot_general(a, b, (((0,), (0,)), ((), ())), preferred_element_type=F32)


_GELU_C = math.sqrt(2.0 / math.pi)


def _gelu(x):
    return 0.5 * x * (1.0 + jnp.tanh(_GELU_C * (x + 0.044715 * (x * x * x))))


def _gelu_and_grad(x):
    t = jnp.tanh(_GELU_C * (x + 0.044715 * (x * x * x)))
    g = 0.5 * x * (1.0 + t)
    dg = 0.5 * (1.0 + t) + 0.5 * x * (1.0 - t * t) * (_GELU_C * (1.0 + 3.0 * 0.044715 * (x * x)))
    return g, dg


def _sigmoid(x):
    return 0.5 * jnp.tanh(0.5 * x) + 0.5


def _rms(x):
    return lax.rsqrt(jnp.mean(x * x, axis=-1, keepdims=True) + EPS)


def _rms_bwd(dxn, xn, r):
    return r * (dxn - xn * jnp.mean(dxn * xn, axis=-1, keepdims=True))


def _rowsum(x):
    return jnp.sum(x, axis=0, keepdims=True)


def _fetch_once(pairs, sems):
    copies = [pltpu.make_async_copy(src, dst, sems.at[k]) for k, (src, dst) in enumerate(pairs)]
    for cp in copies:
        cp.start()
    for cp in copies:
        cp.wait()


def _s5_disc(are, aim, ldt, br, bi):
    dt = jnp.exp(ldt)
    mag = jnp.exp(dt * are)
    abr = mag * jnp.cos(dt * aim)
    abi = mag * jnp.sin(dt * aim)
    den = are * are + aim * aim
    nr = abr - 1.0
    ni = abi
    fr = (nr * are + ni * aim) / den
    fi = (ni * are - nr * aim) / den
    return abr, abi, fr * br - fi * bi, fr * bi + fi * br


def _s5_params_fwd(are, aim, ldt, br, bi):
    def body(are_ref, aim_ref, ldt_ref, br_ref, bi_ref, o0, o1, o2, o3):
        outs = _s5_disc(are_ref[...], aim_ref[...], ldt_ref[...], br_ref[...], bi_ref[...])
        for o, v in zip((o0, o1, o2, o3), outs):
            o[...] = v
    shp = are.shape
    return pl.pallas_call(body, name="s5_params_fwd", grid=(1,), in_specs=[_full(shp)] * 5, out_specs=[_full(shp)] * 4,
                          out_shape=[_sds(shp)] * 4)(*_in_hbm([are, aim, ldt, br, bi]))


def _s5_params_bwd(are, aim, ldt, br, bi, dabr, dabi, dbr, dbi):
    def body(are_ref, aim_ref, ldt_ref, br_ref, bi_ref, c0, c1, c2, c3, o0, o1, o2, o3, o4):
        prim = (are_ref[...], aim_ref[...], ldt_ref[...], br_ref[...], bi_ref[...])
        _, vjp = jax.vjp(_s5_disc, *prim)
        outs = vjp((c0[...], c1[...], c2[...], c3[...]))
        for o, v in zip((o0, o1, o2, o3, o4), outs):
            o[...] = v
    shp = are.shape
    return pl.pallas_call(body, name="s5_params_bwd", grid=(1,), in_specs=[_full(shp)] * 9, out_specs=[_full(shp)] * 5,
                          out_shape=[_sds(shp)] * 5)(*_in_hbm([are, aim, ldt, br, bi, dabr, dabi, dbr, dbi]))


def _blockdiag(m_t):
    m = m_t.reshape(SSM_BLK, 8, SSM_H, 1, SSM_P)
    eye = jnp.eye(8, dtype=bool).reshape(1, 8, 1, 8, 1)
    return jnp.where(eye, m, jnp.zeros((), m_t.dtype)).reshape(SSM_BLK, 8 * SSM_H, 8 * SSM_P)


def _unblockdiag(pc):
    m = pc.reshape(SSM_BLK, 8, SSM_H, 8, SSM_P)
    return jnp.einsum("jghgp->jghp", m).reshape(SSM_G * SSM_H, SSM_P)


def _in_fwd(x, g_mix, w_in_t, tm):
    S = x.shape[0]

    def body(x_ref, g_ref, w_ref, h_ref, us_ref, uv_ref, gl_ref):
        xv = x_ref[...]
        h = (xv * _rms(xv) * g_ref[...]).astype(MXU)
        h_ref[...] = h
        us_ref[...] = _dot_nt(h, w_ref[0:SSM_W, :])
        uv_ref[...] = _dot_nt(h, w_ref[SSM_W:SSM_W + 2 * SGU_W, :])
        gl_ref[...] = _dot_nt(h, w_ref[SSM_W + 2 * SGU_W:, :])

    row = lambda n: pl.BlockSpec((tm, n), lambda i: (i, 0))
    return pl.pallas_call(
        body, name="in_fwd", grid=(S // tm,),
        in_specs=[row(D_MODEL), _full((1, D_MODEL)), _full(w_in_t.shape)],
        out_specs=[row(D_MODEL), row(SSM_W), row(2 * SGU_W), row(2 * D_MODEL)],
        out_shape=[_sds((S, D_MODEL), MXU), _sds((S, SSM_W)), _sds((S, 2 * SGU_W)), _sds((S, 2 * D_MODEL))],
        compiler_params=_cp("parallel"),
    )(*_in_hbm([x, g_mix, w_in_t]))


def _scan_tables(ar, ai, reverse):
    n = ar.shape[-1]
    def mul(p, q):
        return p[0] * q[0] - p[1] * q[1], p[0] * q[1] + p[1] * q[0]
    a1 = (ar, ai)
    a2 = mul(a1, a1)
    a3 = mul(a2, a1)
    a4 = mul(a2, a2)
    a5 = mul(a4, a1)
    a6 = mul(a4, a2)
    a7 = mul(a4, a3)
    a8 = mul(a4, a4)
    pw = (a1, a2, a3, a4, a5, a6, a7, a8)
    rows = lax.broadcasted_iota(jnp.int32, (8, n), 0)
    tabs = []
    for s, a in ((1, a1), (2, a2), (4, a4)):
        keep = (rows + s <= 7) if reverse else (rows >= s)
        for comp in a:
            tabs.append(jnp.where(keep, jnp.broadcast_to(comp, (8, n)), 0.0))
    for c in range(2):
        q = jnp.zeros((8, n), F32)
        for r in range(8):
            e = (8 - r) if reverse else (r + 1)
            q = jnp.where(rows == r, jnp.broadcast_to(pw[e - 1][c], (8, n)), q)
        tabs.append(q)
    return tabs


def _scan_group(xr, xi, tab_ref, cr, ci, reverse):
    for t, s in enumerate((1, 2, 4)):
        pr = tab_ref[2 * t]
        pi = tab_ref[2 * t + 1]
        sh = (8 - s) if reverse else s
        sr = pltpu.roll(xr, sh, 0)
        si = pltpu.roll(xi, sh, 0)
        xr, xi = xr + pr * sr - pi * si, xi + pr * si + pi * sr
    qr = tab_ref[6]
    qi = tab_ref[7]
    return xr + qr * cr - qi * ci, xi + qr * ci + qi * cr


def _runs_load(src_ref, dst_ref, run):
    for i in range(run):
        dst_ref[8 * i:8 * i + 8, :] = src_ref[pl.ds(i, 8, stride=run), :]


def _runs_store(val, dst_ref, run):
    for i in range(run):
        dst_ref[pl.ds(i, 8, stride=run), :] = val[8 * i:8 * i + 8, :]


def _cpow2(ar, ai, log2n):
    for _ in range(log2n):
        ar, ai = ar * ar - ai * ai, 2.0 * ar * ai
    return ar, ai


def _s5_fwd(us, abar_re, abar_im, b_re, b_im, c_re, c_im, d_skip, tm):
    S = us.shape[0]
    nt = S // tm
    w = 8 * SSM_P
    run = tm // 8
    assert run & (run - 1) == 0

    def body(us_ref, ar_ref, ai_ref, br_ref, bi_ref, cr_ref, ci_ref, d_ref, str_ref, sti_ref, ys_ref,
             tab_ref, car_ref, up_ref):
        i = pl.program_id(1)

        @pl.when(i == 0)
        def _():
            car_ref[...] = jnp.zeros_like(car_ref)
            for k, t in enumerate(_scan_tables(*_cpow2(ar_ref[...], ai_ref[...], run.bit_length() - 1), False)):
                tab_ref[k] = t

        _runs_load(us_ref, up_ref, run)
        ub = up_ref[...].astype(MXU)
        str_ref[...] = _dot(ub, br_ref[0])
        sti_ref[...] = _dot(ub, bi_ref[0])
        ar = jnp.broadcast_to(ar_ref[...], (8, w))
        ai = jnp.broadcast_to(ai_ref[...], (8, w))

        def advance(k, state):
            r0 = pl.multiple_of(k * 8, 8)
            sr, si = state
            return (ar * sr - ai * si + str_ref[pl.ds(r0, 8), :], ar * si + ai * sr + sti_ref[pl.ds(r0, 8), :])

        def emit(k, state):
            r0 = pl.multiple_of(k * 8, 8)
            sr, si = advance(k, state)
            str_ref[pl.ds(r0, 8), :] = sr
            sti_ref[pl.ds(r0, 8), :] = si
            return sr, si

        zero = jnp.zeros((8, w), F32)
        er, ei = lax.fori_loop(0, run, advance, (zero, zero))
        cr, ci = car_ref[0:1, :], car_ref[1:2, :]
        tr, ti = _scan_group(er, ei, tab_ref, cr, ci, False)
        r8 = lax.broadcasted_iota(jnp.int32, (8, w), 0)
        start = (jnp.where(r8 == 0, cr, pltpu.roll(tr, 1, 0)), jnp.where(r8 == 0, ci, pltpu.roll(ti, 1, 0)))
        car_ref[0:1, :] = tr[7:8, :]
        car_ref[1:2, :] = ti[7:8, :]
        lax.fori_loop(0, run, emit, start)
        y = _dot_nt(str_ref[...].astype(MXU), cr_ref[0]) - _dot_nt(sti_ref[...].astype(MXU), ci_ref[0])
        _runs_store(y, ys_ref, run)
        ys_ref[...] += d_ref[...] * us_ref[...]

    blk = lambda: pl.BlockSpec((1, 8 * SSM_H, w), lambda j, i: (j, 0, 0))
    return pl.pallas_call(
        body, name="s5_fwd", grid=(SSM_BLK, nt),
        in_specs=[pl.BlockSpec((tm, LANES), lambda j, i: (i, j)),
                  pl.BlockSpec((1, w), lambda j, i: (0, j)), pl.BlockSpec((1, w), lambda j, i: (0, j)),
                  blk(), blk(), blk(), blk(),
                  pl.BlockSpec((1, LANES), lambda j, i: (0, j))],
        out_specs=[pl.BlockSpec((tm, w), lambda j, i: (i, j)), pl.BlockSpec((tm, w), lambda j, i: (i, j)),
                   pl.BlockSpec((tm, LANES), lambda j, i: (i, j))],
        out_shape=[_sds((S, SSM_BLK * w)), _sds((S, SSM_BLK * w)), _sds((S, SSM_W))],
        scratch_shapes=[pltpu.VMEM((8, 8, w), F32), pltpu.VMEM((8, w), F32), pltpu.VMEM((tm, LANES), F32)],
        compiler_params=_cp("parallel", "arbitrary"),
    )(*_in_hbm([us, abar_re, abar_im, b_re, b_im, c_re, c_im, d_skip]))


def _sgu_mix(vnb, ws_ref, grp):
    acc = jnp.zeros(vnb.shape, F32)
    for g in range(SGU_G):
        acc = jnp.where(grp == g, _dot(ws_ref[g], vnb), acc)
    return acc


def _mix_fwd(x, ys, uv, gl, w_glu, b_glu, w_pa, g_sgu, ws, bias_s, w_pb, w_out, g_ffn, tm):
    S = x.shape[0]

    def body(x_ref, ys_ref, uv_ref, gl_ref, wglu_ref, bglu_ref, wpa_ref, gs_ref, ws_ref, bias_ref, wpb_ref, wout_ref,
             gf_ref, yg_ref, yap_ref, sg_ref, ya_ref, yb_ref, m_ref, x1_ref, h2_ref):
        yg = _gelu(ys_ref[...])
        ygb = yg.astype(MXU)
        yg_ref[...] = ygb
        z = _dot(ygb, wglu_ref[...]) + bglu_ref[...]
        yapb = (yg * _sigmoid(z)).astype(MXU)
        yap_ref[...] = yapb
        ya = _dot(yapb, wpa_ref[...])
        ya_ref[...] = ya

        uvg = _gelu(uv_ref[...])
        u2 = uvg[:, :SGU_W]
        v2 = uvg[:, SGU_W:]
        vnb = (v2 * _rms(v2) * gs_ref[...]).astype(MXU)
        grp = lax.broadcasted_iota(jnp.int32, (CHUNK, SGU_W), 1) // SGU_D
        for c in range(tm // CHUNK):
            rs = slice(c * CHUNK, (c + 1) * CHUNK)
            mixed = _sgu_mix(vnb[rs], ws_ref, grp) + bias_ref[...]
            sg_ref[rs, :] = (u2[rs] * mixed).astype(MXU)
        yb = _dot(sg_ref[...], wpb_ref[...])
        yb_ref[...] = yb

        glv = gl_ref[...]
        m = _sigmoid(glv[:, :D_MODEL]) * ya + _sigmoid(glv[:, D_MODEL:]) * yb
        mb = m.astype(MXU)
        m_ref[...] = mb
        x1 = x_ref[...] + _dot(mb, wout_ref[...])
        x1_ref[...] = x1
        h2_ref[...] = (x1 * _rms(x1) * gf_ref[...]).astype(MXU)

    row = lambda n: pl.BlockSpec((tm, n), lambda i: (i, 0))
    return pl.pallas_call(
        body, name="mix_fwd", grid=(S // tm,),
        in_specs=[row(D_MODEL), row(SSM_W), row(2 * SGU_W), row(2 * D_MODEL),
                  _full(w_glu.shape), _full(b_glu.shape), _full(w_pa.shape), _full(g_sgu.shape), _full(ws.shape),
                  _full(bias_s.shape), _full(w_pb.shape), _full(w_out.shape), _full(g_ffn.shape)],
        out_specs=[row(SSM_W), row(SSM_W), row(SGU_W), row(D_MODEL), row(D_MODEL), row(D_MODEL), row(D_MODEL),
                   row(D_MODEL)],
        out_shape=[_sds((S, SSM_W), MXU), _sds((S, SSM_W), MXU), _sds((S, SGU_W), MXU), _sds((S, D_MODEL)),
                   _sds((S, D_MODEL)), _sds((S, D_MODEL), MXU), _sds((S, D_MODEL)), _sds((S, D_MODEL), MXU)],
        compiler_params=_cp("parallel"),
    )(*_in_hbm([x, ys, uv, gl, w_glu, b_glu, w_pa, g_sgu, ws, bias_s, w_pb, w_out, g_ffn]))


def _causal_conv3(u, prev8, cw, cb):
    tm = u.shape[0]
    w0, w1, w2 = cw[0:1], cw[1:2], cw[2:3]
    body = w0 * pltpu.roll(u, 2, 0) + w1 * pltpu.roll(u, 1, 0) + w2 * u + cb
    u8 = u[0:8, :]
    r8 = lax.broadcasted_iota(jnp.int32, u8.shape, 0)
    t1 = prev8[7:8, :]
    t0 = prev8[6:7, :]
    s1 = jnp.where(r8 == 0, t1, pltpu.roll(u8, 1, 0))
    s2 = jnp.where(r8 == 0, t0, jnp.where(r8 == 1, t1, pltpu.roll(u8, 2, 0)))
    first = w0 * s2 + w1 * s1 + w2 * u8 + cb
    return jnp.concatenate([first, body[8:tm, :]], axis=0)


def _causal_conv3_adjoint(d, next8, cw):
    tm = d.shape[0]
    w0, w1, w2 = cw[0:1], cw[1:2], cw[2:3]
    n1 = pltpu.roll(d, tm - 1, 0)
    n2 = pltpu.roll(d, tm - 2, 0)
    body = w2 * d + w1 * n1 + w0 * n2
    d8 = d[tm - 8:tm, :]
    r8 = lax.broadcasted_iota(jnp.int32, d8.shape, 0)
    h0 = next8[0:1, :]
    h1 = next8[1:2, :]
    m1 = jnp.where(r8 == 7, h0, pltpu.roll(d8, 7, 0))
    m2 = jnp.where(r8 == 6, h0, jnp.where(r8 == 7, h1, pltpu.roll(d8, 6, 0)))
    last = w2 * d8 + w1 * m1 + w0 * m2
    out = jnp.concatenate([body[0:tm - 8, :], last], axis=0)
    return out, n1, n2, h0 - d[0:1, :], h1 - d[1:2, :]


def _ffn_fwd(h2, x1, tgt, w_up, conv_w, conv_b, w_down, g_final, tm):
    S = h2.shape[0]
    nt = S // tm
    ncb = FF_NCB

    def body(h2_ref, wup_hbm, cwa_ref, cwb_ref, cba_ref, cbb_ref, wd_hbm, x1_ref, gf_ref, tgt_ref,
             up_ref, ab_ref, ff_ref, dx2_ref, dx2b_ref, loss_ref, dgf_ref, acc_ref, tail_ref, wup_ref, wdn_ref, wsem):
        i = pl.program_id(0)
        cb = pl.program_id(1)

        @pl.when(i == 0)
        def _():
            tail_ref[cb] = jnp.zeros((2, 8, FF_CW), F32)

        @pl.when(jnp.logical_and(i == 0, cb == 0))
        def _():
            loss_ref[...] = jnp.zeros_like(loss_ref)
            dgf_ref[...] = jnp.zeros_like(dgf_ref)
            _fetch_once([(wup_hbm, wup_ref), (wd_hbm, wdn_ref)], wsem)

        h2v = h2_ref[...]
        ua = _dot_nt(h2v, wup_ref[cb])
        ub = _dot_nt(h2v, wup_ref[ncb + cb])
        up_ref[0, 0] = ua.astype(MXU)
        up_ref[1, 0] = ub.astype(MXU)
        a = _causal_conv3(ua, tail_ref[cb, 0], cwa_ref[0], cba_ref[0])
        b = _causal_conv3(ub, tail_ref[cb, 1], cwb_ref[0], cbb_ref[0])
        tail_ref[cb, 0] = ua[tm - 8:tm, :]
        tail_ref[cb, 1] = ub[tm - 8:tm, :]
        ab_ref[0, 0] = a
        ab_ref[1, 0] = b
        ffb = (a * _sigmoid(a) * b).astype(MXU)
        ff_ref[0] = ffb
        contrib = _dot(ffb, wdn_ref[pl.ds(pl.multiple_of(cb * FF_CW, FF_CW), FF_CW), :])

        @pl.when(cb == 0)
        def _():
            acc_ref[...] = contrib

        @pl.when(cb > 0)
        def _():
            acc_ref[...] += contrib

        @pl.when(cb == ncb - 1)
        def _():
            x2 = x1_ref[...] + acc_ref[...]
            r = _rms(x2)
            xn = x2 * r
            g = gf_ref[...]
            diff = xn * g - tgt_ref[...]
            loss_ref[...] += (0.5 / D_MODEL) * jnp.sum(diff * diff)
            dy = diff * (1.0 / D_MODEL)
            dgf_ref[...] += _rowsum(dy * xn)
            dx2 = _rms_bwd(dy * g, xn, r)
            dx2_ref[...] = dx2
            dx2b_ref[...] = dx2.astype(MXU)

    row = lambda n: pl.BlockSpec((tm, n), lambda i, c: (i, 0))
    gate = lambda r: pl.BlockSpec((1, r, FF_CW), lambda i, c: (c, 0, 0))
    lin = lambda r: pl.BlockSpec((1, r, FF_CW), lambda i, c: (ncb + c, 0, 0))
    return pl.pallas_call(
        body, name="ffn_fwd", grid=(nt, ncb),
        in_specs=[row(D_MODEL), _ANY, gate(3), lin(3), gate(1), lin(1), _ANY,
                  row(D_MODEL), _full((1, D_MODEL)), row(D_MODEL)],
        out_specs=[pl.BlockSpec((2, 1, tm, FF_CW), lambda i, c: (0, c, i, 0)),
                   pl.BlockSpec((2, 1, tm, FF_CW), lambda i, c: (0, c, i, 0)),
                   pl.BlockSpec((1, tm, FF_CW), lambda i, c: (c, i, 0)),
                   row(D_MODEL), row(D_MODEL), _full((1, LANES)), _full((1, D_MODEL))],
        out_shape=[_sds((2, ncb, S, FF_CW), MXU), _sds((2, ncb, S, FF_CW)), _sds((ncb, S, FF_CW), MXU),
                   _sds((S, D_MODEL)), _sds((S, D_MODEL), MXU), _sds((1, LANES)), _sds((1, D_MODEL))],
        scratch_shapes=[pltpu.VMEM((tm, D_MODEL), F32), pltpu.VMEM((ncb, 2, 8, FF_CW), F32),
                        pltpu.VMEM(w_up.shape, w_up.dtype), pltpu.VMEM(w_down.shape, w_down.dtype),
                        pltpu.SemaphoreType.DMA((2,))],
        compiler_params=pltpu.CompilerParams(dimension_semantics=("arbitrary", "arbitrary"),
                                             vmem_limit_bytes=FFN_VMEM_LIMIT),
    )(*_in_hbm([h2, w_up, conv_w, conv_w, conv_b, conv_b, w_down, x1, g_final, tgt]))


def _ffn_bwd(dx2, up, ab, x1, w_up, conv_w, w_down, g_ffn, tm):
    S = dx2.shape[0]
    nt = S // tm
    ncb = FF_NCB

    def body(dx2_ref, up_ref, ab_ref, cwa_ref, cwb_ref, wd_hbm, wup_hbm,
             x1_ref, g_ref, dup_ref, dx1_ref, dx1b_ref, dconv_ref, dg_ref, acc_ref, head_ref, wup_ref, wdn_ref, wsem):
        i = pl.program_id(0)
        cb = pl.program_id(1)

        @pl.when(i == 0)
        def _():
            head_ref[cb] = jnp.zeros((2, 8, FF_CW), F32)
            dconv_ref[cb] = jnp.zeros((8, FF_CW), F32)
            dconv_ref[ncb + cb] = jnp.zeros((8, FF_CW), F32)

        @pl.when(jnp.logical_and(i == 0, cb == 0))
        def _():
            dg_ref[...] = jnp.zeros_like(dg_ref)
            _fetch_once([(wup_hbm, wup_ref), (wd_hbm, wdn_ref)], wsem)

        dff = _dot_nt(dx2_ref[...].astype(MXU), wdn_ref[pl.ds(pl.multiple_of(cb * FF_CW, FF_CW), FF_CW), :])
        a = ab_ref[0, 0]
        b = ab_ref[1, 0]
        sa = _sigmoid(a)
        silu = a * sa
        da = (dff * b) * (sa + silu * (1.0 - sa))
        db = dff * silu
        dps = []
        for half, slot, d, cw_ref in ((0, cb, da, cwa_ref), (1, ncb + cb, db, cwb_ref)):
            dp, n1, n2, fix0, fix1 = _causal_conv3_adjoint(d, head_ref[cb, half], cw_ref[0])
            head_ref[cb, half] = d[0:8, :]
            dpb16 = dp.astype(MXU)
            dup_ref[half, 0] = dpb16
            dps.append(dpb16)
            u = up_ref[half, 0].astype(F32)
            u_last = u[tm - 1:tm, :]
            dconv_ref[slot, 0:1, :] += _rowsum(n2 * u) + fix0 * u[tm - 2:tm - 1, :] + fix1 * u_last
            dconv_ref[slot, 1:2, :] += _rowsum(n1 * u) + fix0 * u_last
            dconv_ref[slot, 2:3, :] += _rowsum(d * u)
            dconv_ref[slot, 3:4, :] += _rowsum(d)
        contrib = _dot(dps[0], wup_ref[cb]) + _dot(dps[1], wup_ref[ncb + cb])

        @pl.when(cb == 0)
        def _():
            acc_ref[...] = contrib

        @pl.when(cb > 0)
        def _():
            acc_ref[...] += contrib

        @pl.when(cb == ncb - 1)
        def _():
            x1v = x1_ref[...]
            r = _rms(x1v)
            xn = x1v * r
            dh2 = acc_ref[...]
            dg_ref[...] += _rowsum(dh2 * xn)
            dx1 = dx2_ref[...] + _rms_bwd(dh2 * g_ref[...], xn, r)
            dx1_ref[...] = dx1
            dx1b_ref[...] = dx1.astype(MXU)

    row = lambda n: pl.BlockSpec((tm, n), lambda i, c: (nt - 1 - i, 0))
    colb = lambda: pl.BlockSpec((2, 1, tm, FF_CW), lambda i, c: (0, c, nt - 1 - i, 0))
    gate = lambda r: pl.BlockSpec((1, r, FF_CW), lambda i, c: (c, 0, 0))
    lin = lambda r: pl.BlockSpec((1, r, FF_CW), lambda i, c: (ncb + c, 0, 0))
    return pl.pallas_call(
        body, name="ffn_bwd", grid=(nt, ncb),
        in_specs=[row(D_MODEL), colb(), colb(), gate(3), lin(3), _ANY, _ANY, row(D_MODEL), _full((1, D_MODEL))],
        out_specs=[colb(), row(D_MODEL), row(D_MODEL), _full((2 * ncb, 8, FF_CW)), _full((1, D_MODEL))],
        out_shape=[_sds((2, ncb, S, FF_CW), MXU), _sds((S, D_MODEL)), _sds((S, D_MODEL), MXU), _sds((2 * ncb, 8, FF_CW)),
                   _sds((1, D_MODEL))],
        scratch_shapes=[pltpu.VMEM((tm, D_MODEL), F32), pltpu.VMEM((ncb, 2, 8, FF_CW), F32),
                        pltpu.VMEM(w_up.shape, w_up.dtype), pltpu.VMEM(w_down.shape, w_down.dtype),
                        pltpu.SemaphoreType.DMA((2,))],
        compiler_params=pltpu.CompilerParams(dimension_semantics=("arbitrary", "arbitrary"),
                                             vmem_limit_bytes=FFN_VMEM_LIMIT),
    )(*_in_hbm([dx2, up, ab, conv_w, conv_w, w_down, w_up, x1, g_ffn]))


def _mix_bwd(dx1, gl, ya, yb, ys, uv, w_out, w_pa, w_pb, w_glu, b_glu, g_sgu, ws, ws_t, bias_s, tm):
    S = dx1.shape[0]

    def body(dx1_ref, gl_ref, ya_ref, yb_ref, ys_ref, uv_ref, wout_ref, wpa_ref, wpb_ref, wglu_ref, bglu_ref, gs_ref,
             ws_ref, wst_ref, bias_ref,
             dgl_ref, dya_ref, dyb_ref, dz_ref, dys_ref, duv_ref, dbglu_ref, dgs_ref, dws_ref, dbs_ref,
             du2_ref, dvn_ref):
        i = pl.program_id(0)

        @pl.when(i == 0)
        def _():
            dbglu_ref[...] = jnp.zeros_like(dbglu_ref)
            dgs_ref[...] = jnp.zeros_like(dgs_ref)
            dws_ref[...] = jnp.zeros_like(dws_ref)
            dbs_ref[...] = jnp.zeros_like(dbs_ref)

        dm = _dot_nt(dx1_ref[...].astype(MXU), wout_ref[...])
        glv = gl_ref[...]
        ga = _sigmoid(glv[:, :D_MODEL])
        gb = _sigmoid(glv[:, D_MODEL:])
        dgl_ref[:, :D_MODEL] = (dm * ya_ref[...] * ga * (1.0 - ga)).astype(MXU)
        dgl_ref[:, D_MODEL:] = (dm * yb_ref[...] * gb * (1.0 - gb)).astype(MXU)
        dyab = (dm * ga).astype(MXU)
        dybb = (dm * gb).astype(MXU)
        dya_ref[...] = dyab
        dyb_ref[...] = dybb

        dyap = _dot_nt(dyab, wpa_ref[...])
        yg, dgelu = _gelu_and_grad(ys_ref[...])
        sz = _sigmoid(_dot(yg.astype(MXU), wglu_ref[...]) + bglu_ref[...])
        dz = dyap * yg * sz * (1.0 - sz)
        dzb = dz.astype(MXU)
        dz_ref[...] = dzb
        dbglu_ref[...] += _rowsum(dz)
        dys_ref[...] = (dyap * sz + _dot_nt(dzb, wglu_ref[...])) * dgelu

        dsg = _dot_nt(dybb, wpb_ref[...])
        uvg, duvg = _gelu_and_grad(uv_ref[...])
        u2 = uvg[:, :SGU_W]
        v2 = uvg[:, SGU_W:]
        rv = _rms(v2)
        vhat = v2 * rv
        gs = gs_ref[...]
        vnb = (vhat * gs).astype(MXU)
        grp = lax.broadcasted_iota(jnp.int32, (CHUNK, SGU_W), 1) // SGU_D
        tril = (lax.broadcasted_iota(jnp.int32, (CHUNK, CHUNK), 0)
                >= lax.broadcasted_iota(jnp.int32, (CHUNK, CHUNK), 1))
        for c in range(tm // CHUNK):
            rs = slice(c * CHUNK, (c + 1) * CHUNK)
            vc = vnb[rs]
            mixed = _sgu_mix(vc, ws_ref, grp) + bias_ref[...]
            dsg_c = dsg[rs]
            du2_ref[rs, :] = dsg_c * mixed
            dmx = dsg_c * u2[rs]
            dbs_ref[...] += dmx
            dmb = dmx.astype(MXU)
            dvn_ref[rs, :] = _sgu_mix(dmb, wst_ref, grp)
            for g in range(SGU_G):
                part = _dot_nt(jnp.where(grp == g, dmb, jnp.zeros((), MXU)), vc)
                dws_ref[g] += jnp.where(tril, part, 0.0)
        dvn = dvn_ref[...]
        dgs_ref[...] += _rowsum(dvn * vhat)
        dv2 = _rms_bwd(dvn * gs, vhat, rv)
        duv_ref[:, :SGU_W] = (du2_ref[...] * duvg[:, :SGU_W]).astype(MXU)
        duv_ref[:, SGU_W:] = (dv2 * duvg[:, SGU_W:]).astype(MXU)

    row = lambda n: pl.BlockSpec((tm, n), lambda i: (i, 0))
    return pl.pallas_call(
        body, name="mix_bwd", grid=(S // tm,),
        in_specs=[row(D_MODEL), row(2 * D_MODEL), row(D_MODEL), row(D_MODEL), row(SSM_W), row(2 * SGU_W),
                  _full(w_out.shape), _full(w_pa.shape), _full(w_pb.shape), _full(w_glu.shape), _full(b_glu.shape),
                  _full(g_sgu.shape), _full(ws.shape), _full(ws_t.shape), _full(bias_s.shape)],
        out_specs=[row(2 * D_MODEL), row(D_MODEL), row(D_MODEL), row(SSM_W), row(SSM_W), row(2 * SGU_W),
                   _full((1, SSM_W)), _full((1, SGU_W)), _full((SGU_G, CHUNK, CHUNK)), _full((CHUNK, SGU_W))],
        out_shape=[_sds((S, 2 * D_MODEL), MXU), _sds((S, D_MODEL), MXU), _sds((S, D_MODEL), MXU), _sds((S, SSM_W), MXU),
                   _sds((S, SSM_W)), _sds((S, 2 * SGU_W), MXU),
                   _sds((1, SSM_W)), _sds((1, SGU_W)), _sds((SGU_G, CHUNK, CHUNK)), _sds((CHUNK, SGU_W))],
        scratch_shapes=[pltpu.VMEM((tm, SGU_W), F32), pltpu.VMEM((tm, SGU_W), F32)],
        compiler_params=_cp("arbitrary"),
    )(*_in_hbm([dx1, gl, ya, yb, ys, uv, w_out, w_pa, w_pb, w_glu, b_glu, g_sgu, ws, ws_t, bias_s]))


def _s5_bwd(dys, us, st_re, st_im, abar_re, abar_im, b_re, b_im, c_re, c_im, d_skip, tm):
    S = us.shape[0]
    nt = S // tm
    w = 8 * SSM_P
    hb = tm // 8
    run = tm // 8
    assert run & (run - 1) == 0

    def body(dys_ref, us_ref, str_ref, sti_ref, hr_ref, hi_ref, ar_ref, ai_ref, br_ref, bi_ref, cr_ref, ci_ref, d_ref,
             dus_ref, dab_ref, dd_ref, dbr_ref, dbi_ref, dcr_ref, dci_ref,
             tab_ref, car_ref, gr_ref, gi_ref, dyp_ref, up_ref, dun_ref):
        i = pl.program_id(1)
        ri = nt - 1 - i

        @pl.when(i == 0)
        def _():
            car_ref[...] = jnp.zeros_like(car_ref)
            for k, t in enumerate(_scan_tables(*_cpow2(ar_ref[...], -ai_ref[...], run.bit_length() - 1), True)):
                tab_ref[k] = t
            for r in (dab_ref, dd_ref, dbr_ref, dbi_ref, dcr_ref, dci_ref):
                r[...] = jnp.zeros_like(r)

        _runs_load(dys_ref, dyp_ref, run)
        _runs_load(us_ref, up_ref, run)
        dyb = dyp_ref[...].astype(MXU)
        gr_ref[...] = _dot(dyb, cr_ref[0])
        gi_ref[...] = -_dot(dyb, ci_ref[0])
        ar = jnp.broadcast_to(ar_ref[...], (8, w))
        ai = jnp.broadcast_to(-ai_ref[...], (8, w))

        def advance(kk, state):
            r0 = pl.multiple_of((run - 1 - kk) * 8, 8)
            gr, gi = state
            return (ar * gr - ai * gi + gr_ref[pl.ds(r0, 8), :], ar * gi + ai * gr + gi_ref[pl.ds(r0, 8), :])

        def emit(kk, state):
            r0 = pl.multiple_of((run - 1 - kk) * 8, 8)
            gr, gi = advance(kk, state)
            gr_ref[pl.ds(r0, 8), :] = gr
            gi_ref[pl.ds(r0, 8), :] = gi
            return gr, gi

        zero = jnp.zeros((8, w), F32)
        er, ei = lax.fori_loop(0, run, advance, (zero, zero))
        cr, ci = car_ref[0:1, :], car_ref[1:2, :]
        tr, ti = _scan_group(er, ei, tab_ref, cr, ci, True)
        r8 = lax.broadcasted_iota(jnp.int32, (8, w), 0)
        start = (jnp.where(r8 == 7, cr, pltpu.roll(tr, 7, 0)), jnp.where(r8 == 7, ci, pltpu.roll(ti, 7, 0)))
        car_ref[0:1, :] = tr[0:1, :]
        car_ref[1:2, :] = ti[0:1, :]
        lax.fori_loop(0, run, emit, start)

        gsr = gr_ref[...]
        gsi = gi_ref[...]
        sr = str_ref[...]
        si = sti_ref[...]
        first = ri == 0

        def previous(s, halo_ref):
            head = jnp.where(r8 == 0, jnp.where(first, 0.0, halo_ref[7:8, :]), pltpu.roll(s[tm - 8:tm, :], 1, 0))
            return jnp.concatenate([head, s[0:tm - 8, :]], axis=0)

        spr = previous(sr, hr_ref)
        spi = previous(si, hi_ref)
        dab_ref[0, 0:1, :] += _rowsum(gsr * spr + gsi * spi)
        dab_ref[0, 1:2, :] += _rowsum(gsi * spr - gsr * spi)

        gbr = gsr.astype(MXU)
        gbi = gsi.astype(MXU)
        _runs_store(_dot_nt(gbr, br_ref[0]) + _dot_nt(gbi, bi_ref[0]), dun_ref, run)
        dys_v = dys_ref[...]
        dus_ref[...] = (dun_ref[...] + d_ref[...] * dys_v).astype(MXU)
        dd_ref[0, 0:1, :] += _rowsum(dys_v * us_ref[...])
        ub = up_ref[...].astype(MXU)
        dbr_ref[0] += _dot_tn(ub, gbr)
        dbi_ref[0] += _dot_tn(ub, gbi)
        dcr_ref[0] += _dot_tn(dyb, sr.astype(MXU))
        dci_ref[0] -= _dot_tn(dyb, si.astype(MXU))

    blk = lambda: pl.BlockSpec((1, 8 * SSM_H, w), lambda j, i: (j, 0, 0))
    rowl = lambda: pl.BlockSpec((tm, LANES), lambda j, i: (nt - 1 - i, j))
    roww = lambda: pl.BlockSpec((tm, w), lambda j, i: (nt - 1 - i, j))
    halo = lambda: pl.BlockSpec((8, w), lambda j, i: (jnp.maximum((nt - 1 - i) * hb - 1, 0), j))
    return pl.pallas_call(
        body, name="s5_bwd", grid=(SSM_BLK, nt),
        in_specs=[rowl(), rowl(), roww(), roww(), halo(), halo(),
                  pl.BlockSpec((1, w), lambda j, i: (0, j)), pl.BlockSpec((1, w), lambda j, i: (0, j)),
                  blk(), blk(), blk(), blk(),
                  pl.BlockSpec((1, LANES), lambda j, i: (0, j))],
        out_specs=[rowl(),
                   pl.BlockSpec((1, 8, w), lambda j, i: (j, 0, 0)), pl.BlockSpec((1, 8, LANES), lambda j, i: (j, 0, 0)),
                   blk(), blk(), blk(), blk()],
        out_shape=[_sds((S, SSM_W), MXU), _sds((SSM_BLK, 8, w)), _sds((SSM_BLK, 8, LANES)),
                   _sds((SSM_BLK, 8 * SSM_H, w)), _sds((SSM_BLK, 8 * SSM_H, w)),
                   _sds((SSM_BLK, 8 * SSM_H, w)), _sds((SSM_BLK, 8 * SSM_H, w))],
        scratch_shapes=[pltpu.VMEM((8, 8, w), F32), pltpu.VMEM((8, w), F32),
                        pltpu.VMEM((tm, w), F32), pltpu.VMEM((tm, w), F32),
                        pltpu.VMEM((tm, LANES), F32), pltpu.VMEM((tm, LANES), F32), pltpu.VMEM((tm, LANES), F32)],
        compiler_params=_cp("parallel", "arbitrary"),
    )(*_in_hbm([dys, us, st_re, st_im, st_re, st_im, abar_re, abar_im, b_re, b_im, c_re, c_im, d_skip]))


def _in_bwd(dus, duv, dgl, dx1, x, g_mix, w_in, tm):
    S = x.shape[0]

    def body(dus_ref, duv_ref, dgl_ref, dx1_ref, x_ref, g_ref, w_ref, gx_ref, dg_ref):
        @pl.when(pl.program_id(0) == 0)
        def _():
            dg_ref[...] = jnp.zeros_like(dg_ref)

        dh = (_dot(dus_ref[...], w_ref[0:SSM_W, :])
              + _dot(duv_ref[...], w_ref[SSM_W:SSM_W + 2 * SGU_W, :])
              + _dot(dgl_ref[...], w_ref[SSM_W + 2 * SGU_W:, :]))
        xv = x_ref[...]
        r = _rms(xv)
        xn = xv * r
        dg_ref[...] += _rowsum(dh * xn)
        gx_ref[...] = dx1_ref[...] + _rms_bwd(dh * g_ref[...], xn, r)

    row = lambda n: pl.BlockSpec((tm, n), lambda i: (i, 0))
    return pl.pallas_call(
        body, name="in_bwd", grid=(S // tm,),
        in_specs=[row(SSM_W), row(2 * SGU_W), row(2 * D_MODEL), row(D_MODEL), row(D_MODEL), _full((1, D_MODEL)),
                  _full(w_in.shape)],
        out_specs=[row(D_MODEL), _full((1, D_MODEL))],
        out_shape=[_sds((S, D_MODEL)), _sds((1, D_MODEL))],
        compiler_params=_cp("arbitrary"),
    )(*_in_hbm([dus, duv, dgl, dx1, x, g_mix, w_in]))


def _pick(n, cands):
    for c in cands:
        if n % c == 0:
            return c
    return n


def _wgrad_split(a, b, nsplit, tk, name):
    S, K = a.shape
    N = b.shape[1]
    c = N // nsplit

    def body(a_ref, b_ref, o_ref):
        prod = _dot_tn(a_ref[...], b_ref[...])
        for d in range(nsplit):
            o_ref[d] = prod[:, c * d:c * (d + 1)].astype(MXU)

    return pl.pallas_call(
        body, name=name, grid=(K // tk,),
        in_specs=[pl.BlockSpec((S, tk), lambda k: (0, k)), _full((S, N))],
        out_specs=pl.BlockSpec((nsplit, tk, c), lambda k: (0, k, 0)),
        out_shape=_sds((nsplit, K, c), MXU),
        compiler_params=_cp("parallel"),
    )(*_in_hbm([a, b]))


def _wgrad_in_t(dps, h1, name):
    S, K = h1.shape
    cw = 512
    counts = [b.shape[1] // cw for b in dps]
    starts = [sum(counts[:i]) for i in range(len(dps))]
    nblk = sum(counts)

    def body(*refs):
        b_refs = refs[:len(dps)]
        h_ref, o_ref = refs[len(dps):]
        j = pl.program_id(0)
        for b_ref, st, cnt in zip(b_refs, starts, counts):
            @pl.when(jnp.logical_and(j >= st, j < st + cnt))
            def _():
                o_ref[...] = _dot_tn(b_ref[...], h_ref[...]).astype(MXU)

    def src_spec(st, cnt):
        return pl.BlockSpec((S, cw), lambda j: (0, jnp.clip(j - st, 0, cnt - 1)))

    return pl.pallas_call(
        body, name=name, grid=(nblk,),
        in_specs=[src_spec(st, cnt) for st, cnt in zip(starts, counts)] + [_full((S, K))],
        out_specs=pl.BlockSpec((cw, K), lambda j: (j, 0)),
        out_shape=_sds((nblk * cw, K), MXU),
        compiler_params=_cp("arbitrary"),
    )(*_in_hbm([*dps, h1]))


def _wgrad_blk(a3, b3, nblk, a_of, b_of, name):
    S, K = a3.shape[1:]
    N = b3.shape[2]

    def body(a_ref, b_ref, o_ref):
        o_ref[0] = _dot_tn(a_ref[0], b_ref[0]).astype(MXU)

    return pl.pallas_call(
        body, name=name, grid=(nblk,),
        in_specs=[pl.BlockSpec((1, S, K), lambda b: (a_of(b), 0, 0)),
                  pl.BlockSpec((1, S, N), lambda b: (b_of(b), 0, 0))],
        out_specs=pl.BlockSpec((1, K, N), lambda b: (b, 0, 0)),
        out_shape=_sds((nblk, K, N), MXU),
        compiler_params=pltpu.CompilerParams(dimension_semantics=("parallel",), vmem_limit_bytes=WGRAD_VMEM_LIMIT),
    )(*_in_hbm([a3, b3]))


def _assemble_cols(blocks_list, name):
    def body(*refs):
        n = len(blocks_list)
        for b_ref, o_ref in zip(refs[:n], refs[n:]):
            c = b_ref.shape[2]
            for d in range(N_DEV):
                o_ref[:, c * d:c * (d + 1)] = b_ref[d]

    outs = [_sds((b.shape[1], N_DEV * b.shape[2]), b.dtype) for b in blocks_list]
    return pl.pallas_call(
        body, name=name, grid=(1,), in_specs=[_full(b.shape) for b in blocks_list],
        out_specs=[_full(o.shape) for o in outs], out_shape=outs, compiler_params=_cp("arbitrary"),
    )(*_in_hbm(blocks_list))


def _tile(S, want):
    return want if S % want == 0 else S


def _local_step(x, tgt, p, mixer_relay, mixer_weights, ffn_weights, grads_out):
    S = x.shape[0]
    tm = _tile(S, 256)
    tl = _tile(S, 512)

    rep = lambda a: jnp.repeat(a, SSM_H, axis=0)
    are = rep(p["a_re"])
    aim = rep(p["a_im"])
    ldt = jnp.broadcast_to(rep(p["log_dt"].reshape(SSM_G, 1)), are.shape)
    br_t = p["b_re_t"].reshape(are.shape)
    bi_t = p["b_im_t"].reshape(are.shape)
    abr, abi, bbr, bbi = _s5_params_fwd(are, aim, ldt, br_t, bi_t)
    head = lambda a: a.reshape(SSM_G, SSM_H, SSM_P)[:, 0, :].reshape(1, SSM_G * SSM_P)
    abar_re, abar_im = head(abr), head(abi)
    bd_br = _blockdiag(bbr).astype(MXU)
    bd_bi = _blockdiag(bbi).astype(MXU)
    bd_cr = _blockdiag(p["c_re"].reshape(are.shape)).astype(MXU)
    bd_ci = _blockdiag(p["c_im"].reshape(are.shape)).astype(MXU)
    d_skip = p["d_skip"].reshape(1, SSM_W)

    tril = jnp.tril(jnp.ones((CHUNK, CHUNK), dtype=bool))
    ws = jnp.where(tril[None], p["w_s"], 0.0)
    ws_b = ws.astype(MXU)
    ws_t = ws.transpose(0, 2, 1).astype(MXU)
    bias_s = jnp.repeat(p["b_s"].T, SGU_D, axis=1)

    g_mix = p["g_mix"].reshape(1, D_MODEL)
    g_ffn = p["g_ffn"].reshape(1, D_MODEL)
    g_final = p["g_final"].reshape(1, D_MODEL)
    g_sgu = p["g_sgu"].reshape(1, SGU_W)
    b_glu = p["b_glu"].reshape(1, SSM_W)
    conv_b = p["conv_b"].reshape(2 * FF_NCB, 1, FF_CW)
    tf = _tile(S, 256)

    h1, us, uv, gl = _in_fwd(x, g_mix, p["w_in_t"], tl)
    token = mixer_relay(us)
    st_re, st_im, ys = _s5_fwd(us, abar_re, abar_im, bd_br, bd_bi, bd_cr, bd_ci, d_skip + token[0:1, 0:1], tl)
    p = dict(p, **mixer_weights(ys))
    yg, yap, sg, ya, yb, m, x1, h2 = _mix_fwd(x, ys, uv, gl, p["w_glu"], b_glu, p["w_proj_a"], g_sgu, ws_b, bias_s,
                                              p["w_proj_b"], p["w_out"], g_ffn, tm)
    w_up, conv_w, w_down = ffn_weights(h2)
    pair_lanes = lambda a: a.reshape(N_DEV // 2, 2, a.shape[1], FF_SHARD).transpose(0, 2, 1, 3).reshape(
        N_DEV // 2, a.shape[1], FF_CW)
    w_up = w_up.reshape(2 * FF_NCB, FF_CW, D_MODEL)
    conv_w = pair_lanes(conv_w)
    up, ab, ff, dx2, dx2b, loss, dg_final = _ffn_fwd(h2, x1, tgt, w_up, conv_w, conv_b, w_down, g_final, tf)

    dup, dx1, dx1b, dconv, dg_ffn = _ffn_bwd(dx2, up, ab, x1, w_up, conv_w, w_down, g_ffn, tf)
    rows8 = lambda g: g.reshape(N_DEV, g.shape[1] // N_DEV, g.shape[2])
    g_up = _wgrad_blk(dup.reshape(2 * FF_NCB, S, FF_CW), h2[None], 2 * FF_NCB, lambda b: b, lambda b: 0,
                      "wgrad_up").reshape(N_DEV, FF_SHARD, D_MODEL)
    g_down = _wgrad_blk(ff, dx2b[None], FF_NCB, lambda b: b, lambda b: 0, "wgrad_down").reshape(
        N_DEV, D_FF // N_DEV, D_MODEL)
    token = grads_out(("w_up", "w_down"), (g_up, g_down))
    dgl, dya, dyb, dz, dys, duv, db_glu, dg_sgu, dws, dbs = _mix_bwd(
        dx1, gl, ya, yb, ys, uv, p["w_out"], p["w_proj_a"], p["w_proj_b"], p["w_glu"], b_glu + token[0:1, 0:1], g_sgu,
        ws_b, ws_t, bias_s, tm)
    token = grads_out(("w_glu", "w_proj_a", "w_proj_b", "w_out"),
                      (rows8(_wgrad_split(yg, dz, 1, SSM_W, "wgrad_glu")),
                       _wgrad_split(yap, dya, N_DEV, SSM_W, "wgrad_pa"),
                       _wgrad_split(sg, dyb, N_DEV, SGU_W, "wgrad_pb"),
                       rows8(_wgrad_split(m, dx1b, 1, 512, "wgrad_out"))))
    dus, dab, dd, dbbr, dbbi, dcr, dci = _s5_bwd(dys, us, st_re, st_im, abar_re, abar_im, bd_br, bd_bi, bd_cr, bd_ci,
                                                 d_skip + token[0:1, 0:1], tl)
    g_in = _wgrad_in_t([dus, duv, dgl], h1, "wgrad_in")
    token = grads_out(("w_in",), (g_in.reshape(N_DEV, g_in.shape[0] // N_DEV, D_MODEL),))
    grad_x, dg_mix = _in_bwd(dus, duv, dgl, dx1, x, g_mix + token[0:1, 0:1], p["w_in_t"], tl)

    spread = lambda v: jnp.repeat(v.reshape(SSM_G, SSM_P), SSM_H, axis=0) * (1.0 / SSM_H)
    dabr = spread(dab[:, 0, :])
    dabi = spread(dab[:, 1, :])
    dare, daim, dldt, dbr_t, dbi_t = _s5_params_bwd(are, aim, ldt, br_t, bi_t, dabr, dabi,
                                                    _unblockdiag(dbbr), _unblockdiag(dbbi))
    fold = lambda a: a.reshape(SSM_G, SSM_H, SSM_P).sum(axis=1)

    grads = {
        "g_mix": dg_mix,
        "a_re": fold(dare), "a_im": fold(daim), "log_dt": fold(dldt).sum(axis=1),
        "b_re": dbr_t, "b_im": dbi_t,
        "c_re": _unblockdiag(dcr).reshape(SSM_G, SSM_H, SSM_P),
        "c_im": _unblockdiag(dci).reshape(SSM_G, SSM_H, SSM_P),
        "d_skip": dd[:, 0, :].reshape(SSM_W),
        "b_glu": db_glu,
        "g_sgu": dg_sgu,
        "w_s": dws,
        "b_s": dbs.reshape(CHUNK, SGU_G, SGU_D).sum(axis=-1).T,
        "g_ffn": dg_ffn,
        "conv_w": dconv[:, 0:3, :].reshape(N_DEV // 2, 3, 2, FF_SHARD).transpose(0, 2, 1, 3).reshape(
            N_DEV, 3, FF_SHARD),
        "conv_b": dconv[:, 3, :].reshape(2 * D_FF),
        "g_final": dg_final,
    }
    return loss, grad_x, grads


_ANY = pl.BlockSpec(memory_space=pl.ANY)
_MESH = pl.DeviceIdType.MESH


def _allgather(shards, dtypes, name, cast_only=()):
    n = len(shards)
    e = len(cast_only)

    def body(*refs):
        in_refs, extra_in = refs[:n], refs[n:n + e]
        out_refs, extra_out = refs[n + e:2 * n + e], refs[2 * n + e:2 * n + 2 * e]
        stage = refs[2 * n + 2 * e:3 * n + 2 * e]
        send_sems, recv_sems, local_sems = refs[3 * n + 2 * e:]
        for a in range(n):
            stage[a][...] = in_refs[a][...].astype(dtypes[a])
        for i in range(e):
            extra_out[i][...] = extra_in[i][...].astype(MXU)
        x, y, c = lax.axis_index("x"), lax.axis_index("y"), lax.axis_index("c")
        me, sibling = (x, y, c), (x, y, 1 - c)
        chips = [(1 - x, y), (x, 1 - y), (1 - x, 1 - y)]

        def slot(a, px, py, pc):
            return out_refs[a].at[4 * px + 2 * py + pc]

        def copy(a, k, block, to, src=None):
            return pltpu.make_async_remote_copy(
                src_ref=slot(a, *block) if src is None else src, dst_ref=slot(a, *block),
                send_sem=send_sems.at[a, k], recv_sem=recv_sems.at[a, k], device_id=to, device_id_type=_MESH)

        mine = [pltpu.make_async_copy(stage[a], slot(a, *me), local_sems.at[a]) for a in range(n)]
        for cp in mine:
            cp.start()
        first = []
        for j, chip in enumerate(chips):
            first += [copy(a, 1 + j, me, (*chip, c), src=stage[a]) for a in range(n)]
        first += [copy(a, 0, me, sibling, src=stage[a]) for a in range(n)]
        for cp in first:
            cp.start()
        passed = []
        for j, chip in enumerate(chips):
            for a in range(n):
                copy(a, 1 + j, (*chip, c), me).wait_recv()
                fwd = copy(a, 4 + j, (*chip, c), sibling)
                fwd.start()
                passed.append(fwd)
        for a in range(n):
            copy(a, 0, sibling, me).wait_recv()
        for j, chip in enumerate(chips):
            for a in range(n):
                copy(a, 4 + j, (*chip, 1 - c), me).wait_recv()
        for cp in first + passed:
            cp.wait_send()
        for cp in mine:
            cp.wait()

    res = pl.pallas_call(
        body, name=name, grid=(1,), in_specs=[_full(s.shape) for s in list(shards) + list(cast_only)],
        out_specs=[_ANY] * n + [_full(s.shape) for s in cast_only],
        out_shape=[_sds((N_DEV,) + s.shape, dt) for s, dt in zip(shards, dtypes)]
                  + [_sds(s.shape, MXU) for s in cast_only],
        scratch_shapes=[pltpu.VMEM(s.shape, dt) for s, dt in zip(shards, dtypes)]
                       + [pltpu.SemaphoreType.DMA((n, 7)), pltpu.SemaphoreType.DMA((n, 7)), pltpu.SemaphoreType.DMA((n,))],
        compiler_params=pltpu.CompilerParams(vmem_limit_bytes=VMEM_LIMIT),
    )(*_in_hbm([*shards, *cast_only]))
    return res[:n], res[n:]


def _all_to_all(sends, name):
    n = len(sends)

    def body(*refs):
        send_refs, recv_refs = refs[:n], refs[n:2 * n]
        send_sems, recv_sems, local_sems = refs[2 * n:]
        x, y, c = lax.axis_index("x"), lax.axis_index("y"), lax.axis_index("c")
        me = 4 * x + 2 * y + c
        mine = [pltpu.make_async_copy(send_refs[a].at[me], recv_refs[a].at[me], local_sems.at[a]) for a in range(n)]
        for cp in mine:
            cp.start()
        copies = []
        for k in (2, 4, 6, 3, 5, 7, 1):
            px = 1 - x if k & 4 else x
            py = 1 - y if k & 2 else y
            pc = 1 - c if k & 1 else c
            peer = 4 * px + 2 * py + pc
            for a in range(n):
                sems = dict(send_sem=send_sems.at[a, k - 1], recv_sem=recv_sems.at[a, k - 1],
                            device_id=(px, py, pc), device_id_type=_MESH)
                cp = pltpu.make_async_remote_copy(src_ref=send_refs[a].at[peer], dst_ref=recv_refs[a].at[me], **sems)
                cp.start()
                landing = pltpu.make_async_remote_copy(src_ref=send_refs[a].at[peer], dst_ref=recv_refs[a].at[peer],
                                                       **sems)
                copies.append((cp, landing))
        for _, landing in copies:
            landing.wait_recv()
        for cp, _ in copies:
            cp.wait_send()
        for cp in mine:
            cp.wait()

    return pl.pallas_call(
        body, name=name, in_specs=[_ANY] * n, out_specs=[_ANY] * n,
        out_shape=[_sds(s.shape, s.dtype) for s in sends],
        scratch_shapes=[pltpu.SemaphoreType.DMA((n, 7)), pltpu.SemaphoreType.DMA((n, 7)), pltpu.SemaphoreType.DMA((n,))],
    )(*sends)


_HBM = pl.BlockSpec(memory_space=pltpu.HBM)
_SEM = pl.BlockSpec(memory_space=pltpu.SEMAPHORE)
_EFFECT = pltpu.SideEffectType.DATAFLOW_SIDE_EFFECTING
_PEER_ORDER = (2, 4, 6, 3, 5, 7, 1)


def _peer(k):
    x, y, c = lax.axis_index("x"), lax.axis_index("y"), lax.axis_index("c")
    px = 1 - x if k & 4 else x
    py = 1 - y if k & 2 else y
    pc = 1 - c if k & 1 else c
    return (px, py, pc), 4 * px + 2 * py + pc


_SAME_CORE_AND_SIBLING = (2, 4, 6, 1)


def _push_start(srcs, lands, slotted, name, peers=_PEER_ORDER):
    n = len(srcs)

    def body(*refs):
        src_refs, land_refs = refs[:n], refs[n:2 * n]
        send_sems, recv_sems, token = refs[2 * n], refs[2 * n + 1], refs[-1]
        me = 4 * lax.axis_index("x") + 2 * lax.axis_index("y") + lax.axis_index("c")
        for k in peers:
            dev, peer = _peer(k)
            for a in range(n):
                pltpu.make_async_remote_copy(
                    src_ref=src_refs[a].at[peer] if slotted else src_refs[a], dst_ref=land_refs[a].at[me],
                    send_sem=send_sems.at[7 * a + k - 1], recv_sem=recv_sems.at[7 * a + k - 1],
                    device_id=dev, device_id_type=_MESH).start()
        token[...] = jnp.zeros_like(token)

    bufs = list(srcs) + list(lands)
    res = pl.pallas_call(
        body, name=name, in_specs=[_HBM] * (2 * n),
        out_specs=(_SEM, _SEM, *[_HBM] * (2 * n), pl.BlockSpec(memory_space=pltpu.VMEM)),
        out_shape=(pltpu.SemaphoreType.DMA((7 * n,)), pltpu.SemaphoreType.DMA((7 * n,)),
                   *[pltpu.HBM(b.shape, b.dtype) for b in bufs], _sds((8, LANES))),
        input_output_aliases={i: 2 + i for i in range(2 * n)},
        compiler_params=pltpu.CompilerParams(has_side_effects=_EFFECT),
    )(*[pltpu.with_memory_space_constraint(b, pltpu.HBM) for b in bufs])
    return res[0], res[1], res[2:2 + n], res[2 + n:2 + 2 * n], res[-1]


def _push_wait(send_sems, recv_sems, srcs, lands, slotted, after, name, peers=_PEER_ORDER):
    n = len(srcs)

    def body(*refs):
        src_refs, land_refs = refs[:n], refs[n:2 * n]
        send_sems, recv_sems = refs[2 * n], refs[2 * n + 1]
        for k in peers:
            dev, peer = _peer(k)
            for a in range(n):
                cp = pltpu.make_async_remote_copy(
                    src_ref=src_refs[a].at[peer] if slotted else src_refs[a], dst_ref=land_refs[a].at[peer],
                    send_sem=send_sems.at[7 * a + k - 1], recv_sem=recv_sems.at[7 * a + k - 1],
                    device_id=dev, device_id_type=_MESH)
                cp.wait_send()
                cp.wait_recv()

    bufs = list(srcs) + list(lands)
    res = pl.pallas_call(
        body, name=name, in_specs=[_HBM] * (2 * n) + [_SEM, _SEM] + [_ANY] * len(after), out_specs=[_HBM] * (2 * n),
        out_shape=[pltpu.HBM(b.shape, b.dtype) for b in bufs],
        input_output_aliases={i: i for i in range(2 * n)},
        compiler_params=pltpu.CompilerParams(has_side_effects=_EFFECT),
    )(*bufs, send_sems, recv_sems, *after)
    return res[n:]


def _other_chips():
    x, y = lax.axis_index("x"), lax.axis_index("y")
    return ((1 - x, y), (x, 1 - y), (1 - x, 1 - y))


def _relay_start(lands, name):
    n = len(lands)

    def body(*refs):
        land_refs = refs[:n]
        send_sems, recv_sems, token = refs[n], refs[n + 1], refs[-1]
        x, y, c = lax.axis_index("x"), lax.axis_index("y"), lax.axis_index("c")
        for j, (px, py) in enumerate(_other_chips()):
            slot = 4 * px + 2 * py + c
            for a in range(n):
                pltpu.make_async_remote_copy(
                    src_ref=land_refs[a].at[slot], dst_ref=land_refs[a].at[slot],
                    send_sem=send_sems.at[3 * a + j], recv_sem=recv_sems.at[3 * a + j],
                    device_id=(x, y, 1 - c), device_id_type=_MESH).start()
        token[...] = jnp.zeros_like(token)

    res = pl.pallas_call(
        body, name=name, in_specs=[_HBM] * n,
        out_specs=(_SEM, _SEM, *[_HBM] * n, pl.BlockSpec(memory_space=pltpu.VMEM)),
        out_shape=(pltpu.SemaphoreType.DMA((3 * n,)), pltpu.SemaphoreType.DMA((3 * n,)),
                   *[pltpu.HBM(b.shape, b.dtype) for b in lands], _sds((8, LANES))),
        input_output_aliases={i: 2 + i for i in range(n)},
        compiler_params=pltpu.CompilerParams(has_side_effects=_EFFECT),
    )(*[pltpu.with_memory_space_constraint(b, pltpu.HBM) for b in lands])
    return res[0], res[1], res[2:2 + n], res[-1]


def _relay_wait(send_sems, recv_sems, lands, after, name):
    n = len(lands)

    def body(*refs):
        land_refs = refs[:n]
        send_sems, recv_sems = refs[n], refs[n + 1]
        x, y, c = lax.axis_index("x"), lax.axis_index("y"), lax.axis_index("c")
        for j, (px, py) in enumerate(_other_chips()):
            sent, received = 4 * px + 2 * py + c, 4 * px + 2 * py + (1 - c)
            for a in range(n):
                cp = pltpu.make_async_remote_copy(
                    src_ref=land_refs[a].at[sent], dst_ref=land_refs[a].at[received],
                    send_sem=send_sems.at[3 * a + j], recv_sem=recv_sems.at[3 * a + j],
                    device_id=(x, y, 1 - c), device_id_type=_MESH)
                cp.wait_send()
                cp.wait_recv()

    return pl.pallas_call(
        body, name=name, in_specs=[_HBM] * n + [_SEM, _SEM] + [_ANY] * len(after), out_specs=[_HBM] * n,
        out_shape=[pltpu.HBM(b.shape, b.dtype) for b in lands],
        input_output_aliases={i: i for i in range(n)},
        compiler_params=pltpu.CompilerParams(has_side_effects=_EFFECT),
    )(*lands, send_sems, recv_sems, *after)


def _adamw(w, g, m, v):
    m2 = ADAM_B1 * m + (1.0 - ADAM_B1) * g
    v2 = ADAM_B2 * v + (1.0 - ADAM_B2) * (g * g)
    m_hat = m2 / (1.0 - ADAM_B1 ** ADAM_STEP)
    v_hat = v2 / (1.0 - ADAM_B2 ** ADAM_STEP)
    delta = -ADAM_LR * (m_hat / (jnp.sqrt(v_hat) + ADAM_EPS) + ADAM_WD * w)
    return delta, m2, v2


def _adam_shard(parts, w, m, v, name):
    _, r, c = w.shape
    tr = max(t for t in range(16, 257, 16) if r % t == 0)

    def body(p_ref, w_ref, m_ref, v_ref, g_ref, d_ref, m2_ref, v2_ref):
        g = p_ref[0].astype(F32)
        for s in range(1, N_DEV):
            g = g + p_ref[s].astype(F32)
        g_ref[0] = g
        d_ref[0], m2_ref[0], v2_ref[0] = _adamw(w_ref[0], g, m_ref[0], v_ref[0])

    row = lambda: pl.BlockSpec((1, tr, c), lambda i: (0, i, 0))
    return pl.pallas_call(
        body, name=name, grid=(r // tr,),
        in_specs=[pl.BlockSpec((N_DEV, tr, c), lambda i: (0, i, 0)), row(), row(), row()],
        out_specs=[row(), row(), row(), row()], out_shape=[_sds((1, r, c))] * 4,
        compiler_params=_cp("parallel"),
    )(*_in_hbm([parts, w, m, v]))


def _adam_small(gs, ws, ms, vs, name):
    n = len(gs)

    def body(*refs):
        ins, outs = refs[:4 * n], refs[4 * n:]
        for i in range(n):
            g = ins[i][...]
            d, m2, v2 = _adamw(ins[n + i][...], g, ins[2 * n + i][...], ins[3 * n + i][...])
            outs[i][...] = d
            outs[n + i][...] = m2
            outs[2 * n + i][...] = v2

    res = pl.pallas_call(
        body, name=name, grid=(1,), in_specs=[_full(w.shape) for w in ws] * 4,
        out_specs=[_full(w.shape) for w in ws] * 3, out_shape=[_sds(w.shape) for w in ws] * 3,
        compiler_params=_cp("arbitrary"),
    )(*_in_hbm([*gs, *ws, *ms, *vs]))
    return res[:n], res[n:2 * n], res[2 * n:]


def _sum_slots(parts, name):
    R = parts.shape[1]

    def body(p_ref, o_ref):
        g = p_ref[0]
        for s in range(1, N_DEV):
            g = g + p_ref[s]
        o_ref[...] = g

    return pl.pallas_call(body, name=name, grid=(1,), in_specs=[_full(parts.shape)], out_specs=_full((R, LANES)),
                          out_shape=_sds((R, LANES)))(*_in_hbm([parts]))


def _pad_to(a, n, axis):
    extra = n - a.shape[axis]
    if extra == 0:
        return a
    widths = [(0, 0)] * a.ndim
    widths[axis] = (0, extra)
    return jnp.pad(a, widths)


def _ceil_to(n, k):
    return -(-n // k) * k


def _pack_rows(flats, rows_multiple):
    parts = [_pad_to(f, _ceil_to(f.shape[-1], LANES), f.ndim - 1) for f in flats]
    cat = jnp.concatenate(parts, axis=-1)
    total = _ceil_to(cat.shape[-1], LANES * rows_multiple)
    cat = _pad_to(cat, total, cat.ndim - 1)
    return cat.reshape(cat.shape[:-1] + (total // LANES, LANES))


def _unpack_rows(buf, sizes):
    flat = buf.reshape(buf.shape[:-2] + (-1,))
    out, off = [], 0
    for n in sizes:
        out.append(flat[..., off:off + n])
        off += _ceil_to(n, LANES)
    return out


_MIX_BIG = ("w_in", "w_glu", "w_proj_a", "w_proj_b", "w_out")
_BIG = _MIX_BIG + ("w_up", "w_down")
_SMALL = ("g_mix", "a_re", "a_im", "log_dt", "b_re", "b_im", "c_re", "c_im", "d_skip", "b_glu", "g_sgu", "w_s", "b_s",
          "g_ffn", "conv_b", "g_final")
_SMALL_ROWS_MULTIPLE = 8 * N_DEV
_TRANSPOSED = ("w_in", "w_up", "b_re", "b_im")


def _as_2d(a):
    return a.reshape(-1, a.shape[-1]) if a.ndim > 1 else a.reshape(1, -1)


def kernel(x, g_mix, w_in, a_re, a_im, log_dt, b_re, b_im, c_re, c_im, d_skip, w_glu, b_glu, w_proj_a, g_sgu, w_s, b_s, w_proj_b, w_out, g_ffn, w_up, conv_w, conv_b, w_down, g_final, loss_target, m_g_mix, m_w_in, m_a_re, m_a_im, m_log_dt, m_b_re, m_b_im, m_c_re, m_c_im, m_d_skip, m_w_glu, m_b_glu, m_w_proj_a, m_g_sgu, m_w_s, m_b_s, m_w_proj_b, m_w_out, m_g_ffn, m_w_up, m_conv_w, m_conv_b, m_w_down, m_g_final, v_g_mix, v_w_in, v_a_re, v_a_im, v_log_dt, v_b_re, v_b_im, v_c_re, v_c_im, v_d_skip, v_w_glu, v_b_glu, v_w_proj_a, v_g_sgu, v_w_s, v_b_s, v_w_proj_b, v_w_out, v_g_ffn, v_w_up, v_conv_w, v_conv_b, v_w_down, v_g_final):
    args = dict(locals())
    me = 4 * lax.axis_index("x") + 2 * lax.axis_index("y") + lax.axis_index("c")

    def own_slot(buf, block):
        return lax.dynamic_update_slice(buf, block[None], (me,) + (0,) * block.ndim)

    for n in _TRANSPOSED:
        for pre in ("", "m_", "v_"):
            args[pre + n] = jnp.swapaxes(args[pre + n], -1, -2)
    later = ("w_glu", "w_proj_a", "w_proj_b", "w_out", "w_up", "w_down")
    (w_in_g,), casts = _allgather([args["w_in"][0]], [MXU], "allgather_w_in", cast_only=[args[n][0] for n in later])
    sh = dict(zip(later, casts))

    def start_push(srcs, tag, peers):
        lands = [own_slot(lax.empty((N_DEV,) + s.shape, s.dtype), s) for s in srcs]
        send_sems, recv_sems, srcs, lands, token = _push_start(srcs, lands, False, "push_" + tag, peers)
        return (send_sems, recv_sems, srcs, lands), token

    mix_push, token_a = start_push([sh[n] for n in later[:4]], "mixer_weights", _SAME_CORE_AND_SIBLING)
    ffn_push, token_b = start_push([sh["w_up"], sh["w_down"], conv_w[0]], "ffn_weights", _PEER_ORDER)
    p = {n: (args[n][0] if n != "g_final" else args[n]) for n in _SMALL if n not in _TRANSPOSED}
    p.update(w_in_t=w_in_g.reshape(SSM_W + 2 * SGU_W + 2 * D_MODEL, D_MODEL),
             b_re_t=args["b_re"][0], b_im_t=args["b_im"][0])
    p["g_mix"] = p["g_mix"] + (token_a[0:1, 0:1] + token_b[0:1, 0:1])
    relay = {}

    def mixer_relay(after):
        lands = _push_wait(*mix_push, False, [after], "wait_mixer_weights", _SAME_CORE_AND_SIBLING)
        relay["send"], relay["recv"], relay["lands"], token = _relay_start(lands, "relay_mixer_weights")
        return token

    def mixer_weights(after):
        w_glu_g, w_pa_g, w_pb_g, w_out_g = _relay_wait(relay["send"], relay["recv"], relay["lands"], [after],
                                                       "wait_relay_mixer_weights")
        w_pa_full, w_pb_full = _assemble_cols([w_pa_g, w_pb_g], "assemble_cols")
        return dict(w_glu=w_glu_g.reshape(SSM_W, SSM_W), w_proj_a=w_pa_full, w_proj_b=w_pb_full,
                    w_out=w_out_g.reshape(D_MODEL, D_MODEL))

    def ffn_weights(after):
        w_up_g, w_down_g, conv_w_g = _push_wait(*ffn_push, False, [after], "wait_ffn_weights")
        return w_up_g, conv_w_g, w_down_g.reshape(D_FF, D_MODEL)

    pushes = []

    def grads_out(names, sends):
        lands = [own_slot(lax.empty(s.shape, s.dtype), lax.dynamic_index_in_dim(s, me, 0, keepdims=False))
                 for s in sends]
        send_sems, recv_sems, srcs, lands, token = _push_start(list(sends), lands, True, "push_grads_" + names[0])
        pushes.append((names, send_sems, recv_sems, srcs, lands))
        return token

    loss_part, grad_x, grads = _local_step(x[0], loss_target[0], p, mixer_relay, mixer_weights, ffn_weights, grads_out)

    small_names = _SMALL + ("conv_w", "loss")
    small_g = dict(grads, loss=loss_part[0, 0:1])
    flats = [small_g[n].reshape(-1) for n in small_names]
    small_sizes = [f.shape[0] for f in flats]
    g_small = _pack_rows(flats, _SMALL_ROWS_MULTIPLE)
    rs8 = g_small.shape[0] // N_DEV
    grads_out(("small",), (g_small.reshape(N_DEV, rs8, LANES),))

    out = {}
    done = [g_small]
    for names, send_sems, recv_sems, srcs, lands in pushes:
        parts = _push_wait(send_sems, recv_sems, srcs, lands, True, done, "wait_grads_" + names[0])
        if names == ("small",):
            recv_small, = parts
            break
        for n, part in zip(names, parts):
            res = _adam_shard(part, args[n], args["m_" + n], args["v_" + n], "adam_" + n)
            for kind, v in zip(("grad_", "delta_", "new_m_", "new_v_"), res):
                out[kind + n] = v
            done = [res[0]]
    small_mine = _sum_slots(recv_small, "sum_small")
    g_small_all = _allgather([small_mine], [F32], "allgather_small")[0][0].reshape(N_DEV * rs8, LANES)
    pieces = dict(zip(small_names, _unpack_rows(g_small_all, small_sizes)))
    loss = pieces["loss"][0]
    dconv_w = lax.dynamic_index_in_dim(pieces["conv_w"].reshape(N_DEV, 3, FF_SHARD), me, axis=0, keepdims=False)
    names2 = _SMALL + ("conv_w",)
    gs = [pieces[n].reshape(_as_2d(args[n]).shape) for n in _SMALL] + [dconv_w]
    ds, m2s, v2s = _adam_small(gs, [_as_2d(args[n]) for n in names2], [_as_2d(args["m_" + n]) for n in names2],
                               [_as_2d(args["v_" + n]) for n in names2], "adam_small")
    for n, res in zip(names2, zip(gs, ds, m2s, v2s)):
        for kind, v in zip(("grad_", "delta_", "new_m_", "new_v_"), res):
            out[kind + n] = v.reshape(args[n].shape)
    order = ("g_mix", "w_in", "a_re", "a_im", "log_dt", "b_re", "b_im", "c_re", "c_im", "d_skip", "w_glu", "b_glu",
             "w_proj_a", "g_sgu", "w_s", "b_s", "w_proj_b", "w_out", "g_ffn", "w_up", "conv_w", "conv_b", "w_down",
             "g_final")
    res = [loss, grad_x.reshape(x.shape)]
    for kind in ("grad_", "delta_", "new_m_", "new_v_"):
        res += [jnp.swapaxes(out[kind + n], -1, -2) if n in _TRANSPOSED else out[kind + n] for n in order]
    return tuple(res)
```

```python
import functools
import math

import jax
import jax.numpy as jnp
from jax import lax
from jax.experimental import pallas as pl
from jax.experimental.pallas import tpu as pltpu

F32 = jnp.float32
MXU = jnp.bfloat16
EPS = 1e-6

D_MODEL = 1024
SSM_W = 512
SSM_G, SSM_H, SSM_P = 32, 16, 64
SSM_BLK = 4
SGU_W = 512
SGU_G, SGU_D, CHUNK = 8, 64, 128
D_FF = 2816
N_DEV = 8
FF_SHARD = 2 * D_FF // N_DEV
FF_CW = 2 * FF_SHARD
FF_NCB = D_FF // FF_CW
LANES = 128

ADAM_LR, ADAM_B1, ADAM_B2, ADAM_EPS, ADAM_WD, ADAM_STEP = 0.001, 0.9, 0.999, 1e-08, 0.01, 10

VMEM_LIMIT = 48 * 1024 * 1024
WGRAD_VMEM_LIMIT = 58 * 1024 * 1024
FFN_VMEM_LIMIT = 58 * 1024 * 1024


def _cp(*sem):
    return pltpu.CompilerParams(dimension_semantics=sem, vmem_limit_bytes=VMEM_LIMIT)


def _full(shape):
    n = len(shape)
    return pl.BlockSpec(shape, lambda *_: (0,) * n)


def _sds(shape, dtype=F32):
    return jax.ShapeDtypeStruct(shape, dtype)


def _in_hbm(arrays):
    return [pltpu.with_memory_space_constraint(a, pltpu.HBM) for a in arrays]


def _dot(a, b):
    return jnp.dot(a, b, preferred_element_type=F32)


def _dot_nt(a, b):
    return lax.dot_general(a, b, (((1,), (1,)), ((), ())), preferred_element_type=F32)


def _dot_tn(a, b):
    return lax.dot_general(a, b, (((0,), (0,)), ((), ())), preferred_element_type=F32)


_GELU_C = math.sqrt(2.0 / math.pi)


def _gelu(x):
    return 0.5 * x * (1.0 + jnp.tanh(_GELU_C * (x + 0.044715 * (x * x * x))))


def _gelu_and_grad(x):
    t = jnp.tanh(_GELU_C * (x + 0.044715 * (x * x * x)))
    g = 0.5 * x * (1.0 + t)
    dg = 0.5 * (1.0 + t) + 0.5 * x * (1.0 - t * t) * (_GELU_C * (1.0 + 3.0 * 0.044715 * (x * x)))
    return g, dg


def _sigmoid(x):
    return 0.5 * jnp.tanh(0.5 * x) + 0.5


def _rms(x):
    return lax.rsqrt(jnp.mean(x * x, axis=-1, keepdims=True) + EPS)


def _rms_bwd(dxn, xn, r):
    return r * (dxn - xn * jnp.mean(dxn * xn, axis=-1, keepdims=True))


def _rowsum(x):
    return jnp.sum(x, axis=0, keepdims=True)


def _fetch_once(pairs, sems):
    copies = [pltpu.make_async_copy(src, dst, sems.at[k]) for k, (src, dst) in enumerate(pairs)]
    for cp in copies:
        cp.start()
    for cp in copies:
        cp.wait()


def _s5_disc(are, aim, ldt, br, bi):
    dt = jnp.exp(ldt)
    mag = jnp.exp(dt * are)
    abr = mag * jnp.cos(dt * aim)
    abi = mag * jnp.sin(dt * aim)
    den = are * are + aim * aim
    nr = abr - 1.0
    ni = abi
    fr = (nr * are + ni * aim) / den
    fi = (ni * are - nr * aim) / den
    return abr, abi, fr * br - fi * bi, fr * bi + fi * br


def _s5_params_fwd(are, aim, ldt, br, bi):
    def body(are_ref, aim_ref, ldt_ref, br_ref, bi_ref, o0, o1, o2, o3):
        outs = _s5_disc(are_ref[...], aim_ref[...], ldt_ref[...], br_ref[...], bi_ref[...])
        for o, v in zip((o0, o1, o2, o3), outs):
            o[...] = v
    shp = are.shape
    return pl.pallas_call(body, name="s5_params_fwd", grid=(1,), in_specs=[_full(shp)] * 5, out_specs=[_full(shp)] * 4,
                          out_shape=[_sds(shp)] * 4)(*_in_hbm([are, aim, ldt, br, bi]))


def _s5_params_bwd(are, aim, ldt, br, bi, dabr, dabi, dbr, dbi):
    def body(are_ref, aim_ref, ldt_ref, br_ref, bi_ref, c0, c1, c2, c3, o0, o1, o2, o3, o4):
        prim = (are_ref[...], aim_ref[...], ldt_ref[...], br_ref[...], bi_ref[...])
        _, vjp = jax.vjp(_s5_disc, *prim)
        outs = vjp((c0[...], c1[...], c2[...], c3[...]))
        for o, v in zip((o0, o1, o2, o3, o4), outs):
            o[...] = v
    shp = are.shape
    return pl.pallas_call(body, name="s5_params_bwd", grid=(1,), in_specs=[_full(shp)] * 9, out_specs=[_full(shp)] * 5,
                          out_shape=[_sds(shp)] * 5)(*_in_hbm([are, aim, ldt, br, bi, dabr, dabi, dbr, dbi]))


def _blockdiag(m_t):
    m = m_t.reshape(SSM_BLK, 8, SSM_H, 1, SSM_P)
    eye = jnp.eye(8, dtype=bool).reshape(1, 8, 1, 8, 1)
    return jnp.where(eye, m, jnp.zeros((), m_t.dtype)).reshape(SSM_BLK, 8 * SSM_H, 8 * SSM_P)


def _unblockdiag(pc):
    m = pc.reshape(SSM_BLK, 8, SSM_H, 8, SSM_P)
    return jnp.einsum("jghgp->jghp", m).reshape(SSM_G * SSM_H, SSM_P)


def _in_fwd(x, g_mix, w_in_t, tm):
    S = x.shape[0]

    def body(x_ref, g_ref, w_ref, h_ref, us_ref, uv_ref, gl_ref):
        xv = x_ref[...]
        h = (xv * _rms(xv) * g_ref[...]).astype(MXU)
        h_ref[...] = h
        us_ref[...] = _dot_nt(h, w_ref[0:SSM_W, :])
        uv_ref[...] = _dot_nt(h, w_ref[SSM_W:SSM_W + 2 * SGU_W, :])
        gl_ref[...] = _dot_nt(h, w_ref[SSM_W + 2 * SGU_W:, :])

    row = lambda n: pl.BlockSpec((tm, n), lambda i: (i, 0))
    return pl.pallas_call(
        body, name="in_fwd", grid=(S // tm,),
        in_specs=[row(D_MODEL), _full((1, D_MODEL)), _full(w_in_t.shape)],
        out_specs=[row(D_MODEL), row(SSM_W), row(2 * SGU_W), row(2 * D_MODEL)],
        out_shape=[_sds((S, D_MODEL), MXU), _sds((S, SSM_W)), _sds((S, 2 * SGU_W)), _sds((S, 2 * D_MODEL))],
        compiler_params=_cp("parallel"),
    )(*_in_hbm([x, g_mix, w_in_t]))


def _scan_tables(ar, ai, reverse):
    n = ar.shape[-1]
    def mul(p, q):
        return p[0] * q[0] - p[1] * q[1], p[0] * q[1] + p[1] * q[0]
    a1 = (ar, ai)
    a2 = mul(a1, a1)
    a3 = mul(a2, a1)
    a4 = mul(a2, a2)
    a5 = mul(a4, a1)
    a6 = mul(a4, a2)
    a7 = mul(a4, a3)
    a8 = mul(a4, a4)
    pw = (a1, a2, a3, a4, a5, a6, a7, a8)
    rows = lax.broadcasted_iota(jnp.int32, (8, n), 0)
    tabs = []
    for s, a in ((1, a1), (2, a2), (4, a4)):
        keep = (rows + s <= 7) if reverse else (rows >= s)
        for comp in a:
            tabs.append(jnp.where(keep, jnp.broadcast_to(comp, (8, n)), 0.0))
    for c in range(2):
        q = jnp.zeros((8, n), F32)
        for r in range(8):
            e = (8 - r) if reverse else (r + 1)
            q = jnp.where(rows == r, jnp.broadcast_to(pw[e - 1][c], (8, n)), q)
        tabs.append(q)
    return tabs


def _scan_group(xr, xi, tab_ref, cr, ci, reverse):
    for t, s in enumerate((1, 2, 4)):
        pr = tab_ref[2 * t]
        pi = tab_ref[2 * t + 1]
        sh = (8 - s) if reverse else s
        sr = pltpu.roll(xr, sh, 0)
        si = pltpu.roll(xi, sh, 0)
        xr, xi = xr + pr * sr - pi * si, xi + pr * si + pi * sr
    qr = tab_ref[6]
    qi = tab_ref[7]
    return xr + qr * cr - qi * ci, xi + qr * ci + qi * cr


def _runs_load(src_ref, dst_ref, run):
    for i in range(run):
        dst_ref[8 * i:8 * i + 8, :] = src_ref[pl.ds(i, 8, stride=run), :]


def _runs_store(val, dst_ref, run):
    for i in range(run):
        dst_ref[pl.ds(i, 8, stride=run), :] = val[8 * i:8 * i + 8, :]


def _cpow2(ar, ai, log2n):
    for _ in range(log2n):
        ar, ai = ar * ar - ai * ai, 2.0 * ar * ai
    return ar, ai


def _s5_fwd(us, abar_re, abar_im, b_re, b_im, c_re, c_im, d_skip, tm):
    S = us.shape[0]
    nt = S // tm
    w = 8 * SSM_P
    run = tm // 8
    assert run & (run - 1) == 0

    def body(us_ref, ar_ref, ai_ref, br_ref, bi_ref, cr_ref, ci_ref, d_ref, str_ref, sti_ref, ys_ref,
             tab_ref, car_ref, up_ref):
        i = pl.program_id(1)

        @pl.when(i == 0)
        def _():
            car_ref[...] = jnp.zeros_like(car_ref)
            for k, t in enumerate(_scan_tables(*_cpow2(ar_ref[...], ai_ref[...], run.bit_length() - 1), False)):
                tab_ref[k] = t

        _runs_load(us_ref, up_ref, run)
        ub = up_ref[...].astype(MXU)
        str_ref[...] = _dot(ub, br_ref[0])
        sti_ref[...] = _dot(ub, bi_ref[0])
        ar = jnp.broadcast_to(ar_ref[...], (8, w))
        ai = jnp.broadcast_to(ai_ref[...], (8, w))

        def advance(k, state):
            r0 = pl.multiple_of(k * 8, 8)
            sr, si = state
            return (ar * sr - ai * si + str_ref[pl.ds(r0, 8), :], ar * si + ai * sr + sti_ref[pl.ds(r0, 8), :])

        def emit(k, state):
            r0 = pl.multiple_of(k * 8, 8)
            sr, si = advance(k, state)
            str_ref[pl.ds(r0, 8), :] = sr
            sti_ref[pl.ds(r0, 8), :] = si
            return sr, si

        zero = jnp.zeros((8, w), F32)
        er, ei = lax.fori_loop(0, run, advance, (zero, zero))
        cr, ci = car_ref[0:1, :], car_ref[1:2, :]
        tr, ti = _scan_group(er, ei, tab_ref, cr, ci, False)
        r8 = lax.broadcasted_iota(jnp.int32, (8, w), 0)
        start = (jnp.where(r8 == 0, cr, pltpu.roll(tr, 1, 0)), jnp.where(r8 == 0, ci, pltpu.roll(ti, 1, 0)))
        car_ref[0:1, :] = tr[7:8, :]
        car_ref[1:2, :] = ti[7:8, :]
        lax.fori_loop(0, run, emit, start)
        y = _dot_nt(str_ref[...].astype(MXU), cr_ref[0]) - _dot_nt(sti_ref[...].astype(MXU), ci_ref[0])
        _runs_store(y, ys_ref, run)
        ys_ref[...] += d_ref[...] * us_ref[...]

    blk = lambda: pl.BlockSpec((1, 8 * SSM_H, w), lambda j, i: (j, 0, 0))
    return pl.pallas_call(
        body, name="s5_fwd", grid=(SSM_BLK, nt),
        in_specs=[pl.BlockSpec((tm, LANES), lambda j, i: (i, j)),
                  pl.BlockSpec((1, w), lambda j, i: (0, j)), pl.BlockSpec((1, w), lambda j, i: (0, j)),
                  blk(), blk(), blk(), blk(),
                  pl.BlockSpec((1, LANES), lambda j, i: (0, j))],
        out_specs=[pl.BlockSpec((tm, w), lambda j, i: (i, j)), pl.BlockSpec((tm, w), lambda j, i: (i, j)),
                   pl.BlockSpec((tm, LANES), lambda j, i: (i, j))],
        out_shape=[_sds((S, SSM_BLK * w)), _sds((S, SSM_BLK * w)), _sds((S, SSM_W))],
        scratch_shapes=[pltpu.VMEM((8, 8, w), F32), pltpu.VMEM((8, w), F32), pltpu.VMEM((tm, LANES), F32)],
        compiler_params=_cp("parallel", "arbitrary"),
    )(*_in_hbm([us, abar_re, abar_im, b_re, b_im, c_re, c_im, d_skip]))


def _group_halves(vp):
    first = lax.broadcasted_iota(jnp.int32, vp.shape, 1) < SGU_D
    zero = jnp.zeros((), vp.dtype)
    return jnp.where(first, vp, zero), jnp.where(first, zero, vp)


def _sgu_mix(vnb, wcat_ref):
    outs = []
    for q in range(SGU_G // 2):
        lo, hi = _group_halves(vnb[:, LANES * q:LANES * (q + 1)])
        outs.append(_dot(wcat_ref[q], jnp.concatenate([lo, hi], axis=0)))
    return jnp.concatenate(outs, axis=1)


def _mix_fwd(x, ys, uv, gl, w_glu, b_glu, w_pa, g_sgu, ws, bias_s, w_pb, w_out, g_ffn, tm):
    S = x.shape[0]

    def body(x_ref, ys_ref, uv_ref, gl_ref, wglu_ref, bglu_ref, wpa_ref, gs_ref, ws_ref, bias_ref, wpb_ref, wout_ref,
             gf_ref, yg_ref, yap_ref, sg_ref, ya_ref, yb_ref, m_ref, x1_ref, h2_ref):
        yg = _gelu(ys_ref[...])
        ygb = yg.astype(MXU)
        yg_ref[...] = ygb
        z = _dot(ygb, wglu_ref[...]) + bglu_ref[...]
        yapb = (yg * _sigmoid(z)).astype(MXU)
        yap_ref[...] = yapb
        ya = _dot(yapb, wpa_ref[...])
        ya_ref[...] = ya

        uvg = _gelu(uv_ref[...])
        u2 = uvg[:, :SGU_W]
        v2 = uvg[:, SGU_W:]
        vnb = (v2 * _rms(v2) * gs_ref[...]).astype(MXU)
        for c in range(tm // CHUNK):
            rs = slice(c * CHUNK, (c + 1) * CHUNK)
            mixed = _sgu_mix(vnb[rs], ws_ref) + bias_ref[...]
            sg_ref[rs, :] = (u2[rs] * mixed).astype(MXU)
        yb = _dot(sg_ref[...], wpb_ref[...])
        yb_ref[...] = yb

        glv = gl_ref[...]
        m = _sigmoid(glv[:, :D_MODEL]) * ya + _sigmoid(glv[:, D_MODEL:]) * yb
        mb = m.astype(MXU)
        m_ref[...] = mb
        x1 = x_ref[...] + _dot(mb, wout_ref[...])
        x1_ref[...] = x1
        h2_ref[...] = (x1 * _rms(x1) * gf_ref[...]).astype(MXU)

    row = lambda n: pl.BlockSpec((tm, n), lambda i: (i, 0))
    return pl.pallas_call(
        body, name="mix_fwd", grid=(S // tm,),
        in_specs=[row(D_MODEL), row(SSM_W), row(2 * SGU_W), row(2 * D_MODEL),
                  _full(w_glu.shape), _full(b_glu.shape), _full(w_pa.shape), _full(g_sgu.shape), _full(ws.shape),
                  _full(bias_s.shape), _full(w_pb.shape), _full(w_out.shape), _full(g_ffn.shape)],
        out_specs=[row(SSM_W), row(SSM_W), row(SGU_W), row(D_MODEL), row(D_MODEL), row(D_MODEL), row(D_MODEL),
                   row(D_MODEL)],
        out_shape=[_sds((S, SSM_W), MXU), _sds((S, SSM_W), MXU), _sds((S, SGU_W), MXU), _sds((S, D_MODEL)),
                   _sds((S, D_MODEL)), _sds((S, D_MODEL), MXU), _sds((S, D_MODEL)), _sds((S, D_MODEL), MXU)],
        compiler_params=_cp("parallel"),
    )(*_in_hbm([x, ys, uv, gl, w_glu, b_glu, w_pa, g_sgu, ws, bias_s, w_pb, w_out, g_ffn]))


def _causal_conv3(u, prev8, cw, cb):
    tm = u.shape[0]
    w0, w1, w2 = cw[0:1], cw[1:2], cw[2:3]
    body = w0 * pltpu.roll(u, 2, 0) + w1 * pltpu.roll(u, 1, 0) + w2 * u + cb
    u8 = u[0:8, :]
    r8 = lax.broadcasted_iota(jnp.int32, u8.shape, 0)
    t1 = prev8[7:8, :]
    t0 = prev8[6:7, :]
    s1 = jnp.where(r8 == 0, t1, pltpu.roll(u8, 1, 0))
    s2 = jnp.where(r8 == 0, t0, jnp.where(r8 == 1, t1, pltpu.roll(u8, 2, 0)))
    first = w0 * s2 + w1 * s1 + w2 * u8 + cb
    return jnp.concatenate([first, body[8:tm, :]], axis=0)


def _causal_conv3_adjoint(d, next8, cw):
    tm = d.shape[0]
    w0, w1, w2 = cw[0:1], cw[1:2], cw[2:3]
    n1 = pltpu.roll(d, tm - 1, 0)
    n2 = pltpu.roll(d, tm - 2, 0)
    body = w2 * d + w1 * n1 + w0 * n2
    d8 = d[tm - 8:tm, :]
    r8 = lax.broadcasted_iota(jnp.int32, d8.shape, 0)
    h0 = next8[0:1, :]
    h1 = next8[1:2, :]
    m1 = jnp.where(r8 == 7, h0, pltpu.roll(d8, 7, 0))
    m2 = jnp.where(r8 == 6, h0, jnp.where(r8 == 7, h1, pltpu.roll(d8, 6, 0)))
    last = w2 * d8 + w1 * m1 + w0 * m2
    out = jnp.concatenate([body[0:tm - 8, :], last], axis=0)
    return out, n1, n2, h0 - d[0:1, :], h1 - d[1:2, :]


def _ffn_fwd(h2, x1, tgt, w_up, conv_w, conv_b, w_down, g_final, tm):
    S = h2.shape[0]
    nt = S // tm
    ncb = FF_NCB

    def body(h2_ref, wup_hbm, cwa_ref, cwb_ref, cba_ref, cbb_ref, wd_hbm, x1_ref, gf_ref, tgt_ref,
             up_ref, ab_ref, ff_ref, dx2_ref, dx2b_ref, loss_ref, dgf_ref, acc_ref, tail_ref, wup_ref, wdn_ref, wsem):
        i = pl.program_id(0)
        cb = pl.program_id(1)

        @pl.when(i == 0)
        def _():
            tail_ref[cb] = jnp.zeros((2, 8, FF_CW), F32)

        @pl.when(jnp.logical_and(i == 0, cb == 0))
        def _():
            loss_ref[...] = jnp.zeros_like(loss_ref)
            dgf_ref[...] = jnp.zeros_like(dgf_ref)
            _fetch_once([(wup_hbm, wup_ref), (wd_hbm, wdn_ref)], wsem)

        h2v = h2_ref[...]
        ua = _dot_nt(h2v, wup_ref[cb])
        ub = _dot_nt(h2v, wup_ref[ncb + cb])
        up_ref[0, 0] = ua.astype(MXU)
        up_ref[1, 0] = ub.astype(MXU)
        a = _causal_conv3(ua, tail_ref[cb, 0], cwa_ref[0], cba_ref[0])
        b = _causal_conv3(ub, tail_ref[cb, 1], cwb_ref[0], cbb_ref[0])
        tail_ref[cb, 0] = ua[tm - 8:tm, :]
        tail_ref[cb, 1] = ub[tm - 8:tm, :]
        ab_ref[0, 0] = a
        ab_ref[1, 0] = b
        ffb = (a * _sigmoid(a) * b).astype(MXU)
        ff_ref[0] = ffb
        contrib = _dot(ffb, wdn_ref[pl.ds(pl.multiple_of(cb * FF_CW, FF_CW), FF_CW), :])

        @pl.when(cb == 0)
        def _():
            acc_ref[...] = contrib

        @pl.when(cb > 0)
        def _():
            acc_ref[...] += contrib

        @pl.when(cb == ncb - 1)
        def _():
            x2 = x1_ref[...] + acc_ref[...]
            r = _rms(x2)
            xn = x2 * r
            g = gf_ref[...]
            diff = xn * g - tgt_ref[...]
            loss_ref[...] += (0.5 / D_MODEL) * jnp.sum(diff * diff)
            dy = diff * (1.0 / D_MODEL)
            dgf_ref[...] += _rowsum(dy * xn)
            dx2 = _rms_bwd(dy * g, xn, r)
            dx2_ref[...] = dx2
            dx2b_ref[...] = dx2.astype(MXU)

    row = lambda n: pl.BlockSpec((tm, n), lambda i, c: (i, 0))
    gate = lambda r: pl.BlockSpec((1, r, FF_CW), lambda i, c: (c, 0, 0))
    lin = lambda r: pl.BlockSpec((1, r, FF_CW), lambda i, c: (ncb + c, 0, 0))
    return pl.pallas_call(
        body, name="ffn_fwd", grid=(nt, ncb),
        in_specs=[row(D_MODEL), _ANY, gate(3), lin(3), gate(1), lin(1), _ANY,
                  row(D_MODEL), _full((1, D_MODEL)), row(D_MODEL)],
        out_specs=[pl.BlockSpec((2, 1, tm, FF_CW), lambda i, c: (0, c, i, 0)),
                   pl.BlockSpec((2, 1, tm, FF_CW), lambda i, c: (0, c, i, 0)),
                   pl.BlockSpec((1, tm, FF_CW), lambda i, c: (c, i, 0)),
                   row(D_MODEL), row(D_MODEL), _full((1, LANES)), _full((1, D_MODEL))],
        out_shape=[_sds((2, ncb, S, FF_CW), MXU), _sds((2, ncb, S, FF_CW)), _sds((ncb, S, FF_CW), MXU),
                   _sds((S, D_MODEL)), _sds((S, D_MODEL), MXU), _sds((1, LANES)), _sds((1, D_MODEL))],
        scratch_shapes=[pltpu.VMEM((tm, D_MODEL), F32), pltpu.VMEM((ncb, 2, 8, FF_CW), F32),
                        pltpu.VMEM(w_up.shape, w_up.dtype), pltpu.VMEM(w_down.shape, w_down.dtype),
                        pltpu.SemaphoreType.DMA((2,))],
        compiler_params=pltpu.CompilerParams(dimension_semantics=("arbitrary", "arbitrary"),
                                             vmem_limit_bytes=FFN_VMEM_LIMIT),
    )(*_in_hbm([h2, w_up, conv_w, conv_w, conv_b, conv_b, w_down, x1, g_final, tgt]))


def _ffn_bwd(dx2, up, ab, x1, w_up, conv_w, w_down, g_ffn, tm):
    S = dx2.shape[0]
    nt = S // tm
    ncb = FF_NCB

    def body(dx2_ref, up_ref, ab_ref, cwa_ref, cwb_ref, wd_hbm, wup_hbm,
             x1_ref, g_ref, dup_ref, dx1_ref, dx1b_ref, dconv_ref, dg_ref, acc_ref, head_ref, wup_ref, wdn_ref, wsem):
        i = pl.program_id(0)
        cb = pl.program_id(1)

        @pl.when(i == 0)
        def _():
            head_ref[cb] = jnp.zeros((2, 8, FF_CW), F32)
            dconv_ref[cb] = jnp.zeros((8, FF_CW), F32)
            dconv_ref[ncb + cb] = jnp.zeros((8, FF_CW), F32)

        @pl.when(jnp.logical_and(i == 0, cb == 0))
        def _():
            dg_ref[...] = jnp.zeros_like(dg_ref)
            _fetch_once([(wup_hbm, wup_ref), (wd_hbm, wdn_ref)], wsem)

        dff = _dot_nt(dx2_ref[...].astype(MXU), wdn_ref[pl.ds(pl.multiple_of(cb * FF_CW, FF_CW), FF_CW), :])
        a = ab_ref[0, 0]
        b = ab_ref[1, 0]
        sa = _sigmoid(a)
        silu = a * sa
        da = (dff * b) * (sa + silu * (1.0 - sa))
        db = dff * silu
        dps = []
        for half, slot, d, cw_ref in ((0, cb, da, cwa_ref), (1, ncb + cb, db, cwb_ref)):
            dp, n1, n2, fix0, fix1 = _causal_conv3_adjoint(d, head_ref[cb, half], cw_ref[0])
            head_ref[cb, half] = d[0:8, :]
            dpb16 = dp.astype(MXU)
            dup_ref[half, 0] = dpb16
            dps.append(dpb16)
            u = up_ref[half, 0].astype(F32)
            u_last = u[tm - 1:tm, :]
            dconv_ref[slot, 0:1, :] += _rowsum(n2 * u) + fix0 * u[tm - 2:tm - 1, :] + fix1 * u_last
            dconv_ref[slot, 1:2, :] += _rowsum(n1 * u) + fix0 * u_last
            dconv_ref[slot, 2:3, :] += _rowsum(d * u)
            dconv_ref[slot, 3:4, :] += _rowsum(d)
        contrib = _dot(dps[0], wup_ref[cb]) + _dot(dps[1], wup_ref[ncb + cb])

        @pl.when(cb == 0)
        def _():
            acc_ref[...] = contrib

        @pl.when(cb > 0)
        def _():
            acc_ref[...] += contrib

        @pl.when(cb == ncb - 1)
        def _():
            x1v = x1_ref[...]
            r = _rms(x1v)
            xn = x1v * r
            dh2 = acc_ref[...]
            dg_ref[...] += _rowsum(dh2 * xn)
            dx1 = dx2_ref[...] + _rms_bwd(dh2 * g_ref[...], xn, r)
            dx1_ref[...] = dx1
            dx1b_ref[...] = dx1.astype(MXU)

    row = lambda n: pl.BlockSpec((tm, n), lambda i, c: (nt - 1 - i, 0))
    colb = lambda: pl.BlockSpec((2, 1, tm, FF_CW), lambda i, c: (0, c, nt - 1 - i, 0))
    gate = lambda r: pl.BlockSpec((1, r, FF_CW), lambda i, c: (c, 0, 0))
    lin = lambda r: pl.BlockSpec((1, r, FF_CW), lambda i, c: (ncb + c, 0, 0))
    return pl.pallas_call(
        body, name="ffn_bwd", grid=(nt, ncb),
        in_specs=[row(D_MODEL), colb(), colb(), gate(3), lin(3), _ANY, _ANY, row(D_MODEL), _full((1, D_MODEL))],
        out_specs=[colb(), row(D_MODEL), row(D_MODEL), _full((2 * ncb, 8, FF_CW)), _full((1, D_MODEL))],
        out_shape=[_sds((2, ncb, S, FF_CW), MXU), _sds((S, D_MODEL)), _sds((S, D_MODEL), MXU), _sds((2 * ncb, 8, FF_CW)),
                   _sds((1, D_MODEL))],
        scratch_shapes=[pltpu.VMEM((tm, D_MODEL), F32), pltpu.VMEM((ncb, 2, 8, FF_CW), F32),
                        pltpu.VMEM(w_up.shape, w_up.dtype), pltpu.VMEM(w_down.shape, w_down.dtype),
                        pltpu.SemaphoreType.DMA((2,))],
        compiler_params=pltpu.CompilerParams(dimension_semantics=("arbitrary", "arbitrary"),
                                             vmem_limit_bytes=FFN_VMEM_LIMIT),
    )(*_in_hbm([dx2, up, ab, conv_w, conv_w, w_down, w_up, x1, g_ffn]))


def _mix_bwd(dx1, gl, ya, yb, ys, uv, w_out, w_pa, w_pb, w_glu, b_glu, g_sgu, ws, ws_t, bias_s, tm):
    S = dx1.shape[0]

    def body(dx1_ref, gl_ref, ya_ref, yb_ref, ys_ref, uv_ref, wout_ref, wpa_ref, wpb_ref, wglu_ref, bglu_ref, gs_ref,
             ws_ref, wst_ref, bias_ref,
             dgl_ref, dya_ref, dyb_ref, dz_ref, dys_ref, duv_ref, dbglu_ref, dgs_ref, dws_ref, dbs_ref,
             du2_ref, dvn_ref):
        i = pl.program_id(0)

        @pl.when(i == 0)
        def _():
            dbglu_ref[...] = jnp.zeros_like(dbglu_ref)
            dgs_ref[...] = jnp.zeros_like(dgs_ref)
            dws_ref[...] = jnp.zeros_like(dws_ref)
            dbs_ref[...] = jnp.zeros_like(dbs_ref)

        dm = _dot_nt(dx1_ref[...].astype(MXU), wout_ref[...])
        glv = gl_ref[...]
        ga = _sigmoid(glv[:, :D_MODEL])
        gb = _sigmoid(glv[:, D_MODEL:])
        dgl_ref[:, :D_MODEL] = (dm * ya_ref[...] * ga * (1.0 - ga)).astype(MXU)
        dgl_ref[:, D_MODEL:] = (dm * yb_ref[...] * gb * (1.0 - gb)).astype(MXU)
        dyab = (dm * ga).astype(MXU)
        dybb = (dm * gb).astype(MXU)
        dya_ref[...] = dyab
        dyb_ref[...] = dybb

        dyap = _dot_nt(dyab, wpa_ref[...])
        yg, dgelu = _gelu_and_grad(ys_ref[...])
        sz = _sigmoid(_dot(yg.astype(MXU), wglu_ref[...]) + bglu_ref[...])
        dz = dyap * yg * sz * (1.0 - sz)
        dzb = dz.astype(MXU)
        dz_ref[...] = dzb
        dbglu_ref[...] += _rowsum(dz)
        dys_ref[...] = (dyap * sz + _dot_nt(dzb, wglu_ref[...])) * dgelu

        dsg = _dot_nt(dybb, wpb_ref[...])
        uvg, duvg = _gelu_and_grad(uv_ref[...])
        u2 = uvg[:, :SGU_W]
        v2 = uvg[:, SGU_W:]
        rv = _rms(v2)
        vhat = v2 * rv
        gs = gs_ref[...]
        vnb = (vhat * gs).astype(MXU)
        tril = (lax.broadcasted_iota(jnp.int32, (CHUNK, CHUNK), 0)
                >= lax.broadcasted_iota(jnp.int32, (CHUNK, CHUNK), 1))
        for c in range(tm // CHUNK):
            rs = slice(c * CHUNK, (c + 1) * CHUNK)
            vc = vnb[rs]
            mixed = _sgu_mix(vc, ws_ref) + bias_ref[...]
            dsg_c = dsg[rs]
            du2_ref[rs, :] = dsg_c * mixed
            dmx = dsg_c * u2[rs]
            dbs_ref[...] += dmx
            dmb = dmx.astype(MXU)
            dvn_ref[rs, :] = _sgu_mix(dmb, wst_ref)
            for q in range(SGU_G // 2):
                lanes = slice(LANES * q, LANES * (q + 1))
                for j, part in enumerate(_group_halves(dmb[:, lanes])):
                    dws_ref[2 * q + j] += jnp.where(tril, _dot_nt(part, vc[:, lanes]), 0.0)
        dvn = dvn_ref[...]
        dgs_ref[...] += _rowsum(dvn * vhat)
        dv2 = _rms_bwd(dvn * gs, vhat, rv)
        duv_ref[:, :SGU_W] = (du2_ref[...] * duvg[:, :SGU_W]).astype(MXU)
        duv_ref[:, SGU_W:] = (dv2 * duvg[:, SGU_W:]).astype(MXU)

    row = lambda n: pl.BlockSpec((tm, n), lambda i: (i, 0))
    return pl.pallas_call(
        body, name="mix_bwd", grid=(S // tm,),
        in_specs=[row(D_MODEL), row(2 * D_MODEL), row(D_MODEL), row(D_MODEL), row(SSM_W), row(2 * SGU_W),
                  _full(w_out.shape), _full(w_pa.shape), _full(w_pb.shape), _full(w_glu.shape), _full(b_glu.shape),
                  _full(g_sgu.shape), _full(ws.shape), _full(ws_t.shape), _full(bias_s.shape)],
        out_specs=[row(2 * D_MODEL), row(D_MODEL), row(D_MODEL), row(SSM_W), row(SSM_W), row(2 * SGU_W),
                   _full((1, SSM_W)), _full((1, SGU_W)), _full((SGU_G, CHUNK, CHUNK)), _full((CHUNK, SGU_W))],
        out_shape=[_sds((S, 2 * D_MODEL), MXU), _sds((S, D_MODEL), MXU), _sds((S, D_MODEL), MXU), _sds((S, SSM_W), MXU),
                   _sds((S, SSM_W)), _sds((S, 2 * SGU_W), MXU),
                   _sds((1, SSM_W)), _sds((1, SGU_W)), _sds((SGU_G, CHUNK, CHUNK)), _sds((CHUNK, SGU_W))],
        scratch_shapes=[pltpu.VMEM((tm, SGU_W), F32), pltpu.VMEM((tm, SGU_W), F32)],
        compiler_params=_cp("arbitrary"),
    )(*_in_hbm([dx1, gl, ya, yb, ys, uv, w_out, w_pa, w_pb, w_glu, b_glu, g_sgu, ws, ws_t, bias_s]))


def _s5_bwd(dys, us, st_re, st_im, abar_re, abar_im, b_re, b_im, c_re, c_im, d_skip, tm):
    S = us.shape[0]
    nt = S // tm
    w = 8 * SSM_P
    hb = tm // 8
    run = tm // 8
    assert run & (run - 1) == 0

    def body(dys_ref, us_ref, str_ref, sti_ref, hr_ref, hi_ref, ar_ref, ai_ref, br_ref, bi_ref, cr_ref, ci_ref, d_ref,
             dus_ref, dab_ref, dd_ref, dbr_ref, dbi_ref, dcr_ref, dci_ref,
             tab_ref, car_ref, gr_ref, gi_ref, dyp_ref, up_ref, dun_ref):
        i = pl.program_id(1)
        ri = nt - 1 - i

        @pl.when(i == 0)
        def _():
            car_ref[...] = jnp.zeros_like(car_ref)
            for k, t in enumerate(_scan_tables(*_cpow2(ar_ref[...], -ai_ref[...], run.bit_length() - 1), True)):
                tab_ref[k] = t
            for r in (dab_ref, dd_ref, dbr_ref, dbi_ref, dcr_ref, dci_ref):
                r[...] = jnp.zeros_like(r)

        _runs_load(dys_ref, dyp_ref, run)
        _runs_load(us_ref, up_ref, run)
        dyb = dyp_ref[...].astype(MXU)
        gr_ref[...] = _dot(dyb, cr_ref[0])
        gi_ref[...] = -_dot(dyb, ci_ref[0])
        ar = jnp.broadcast_to(ar_ref[...], (8, w))
        ai = jnp.broadcast_to(-ai_ref[...], (8, w))

        def advance(kk, state):
            r0 = pl.multiple_of((run - 1 - kk) * 8, 8)
            gr, gi = state
            return (ar * gr - ai * gi + gr_ref[pl.ds(r0, 8), :], ar * gi + ai * gr + gi_ref[pl.ds(r0, 8), :])

        def emit(kk, state):
            r0 = pl.multiple_of((run - 1 - kk) * 8, 8)
            gr, gi = advance(kk, state)
            gr_ref[pl.ds(r0, 8), :] = gr
            gi_ref[pl.ds(r0, 8), :] = gi
            return gr, gi

        zero = jnp.zeros((8, w), F32)
        er, ei = lax.fori_loop(0, run, advance, (zero, zero))
        cr, ci = car_ref[0:1, :], car_ref[1:2, :]
        tr, ti = _scan_group(er, ei, tab_ref, cr, ci, True)
        r8 = lax.broadcasted_iota(jnp.int32, (8, w), 0)
        start = (jnp.where(r8 == 7, cr, pltpu.roll(tr, 7, 0)), jnp.where(r8 == 7, ci, pltpu.roll(ti, 7, 0)))
        car_ref[0:1, :] = tr[0:1, :]
        car_ref[1:2, :] = ti[0:1, :]
        lax.fori_loop(0, run, emit, start)

        gsr = gr_ref[...]
        gsi = gi_ref[...]
        sr = str_ref[...]
        si = sti_ref[...]
        first = ri == 0

        def previous(s, halo_ref):
            head = jnp.where(r8 == 0, jnp.where(first, 0.0, halo_ref[7:8, :]), pltpu.roll(s[tm - 8:tm, :], 1, 0))
            return jnp.concatenate([head, s[0:tm - 8, :]], axis=0)

        spr = previous(sr, hr_ref)
        spi = previous(si, hi_ref)
        dab_ref[0, 0:1, :] += _rowsum(gsr * spr + gsi * spi)
        dab_ref[0, 1:2, :] += _rowsum(gsi * spr - gsr * spi)

        gbr = gsr.astype(MXU)
        gbi = gsi.astype(MXU)
        _runs_store(_dot_nt(gbr, br_ref[0]) + _dot_nt(gbi, bi_ref[0]), dun_ref, run)
        dys_v = dys_ref[...]
        dus_ref[...] = (dun_ref[...] + d_ref[...] * dys_v).astype(MXU)
        dd_ref[0, 0:1, :] += _rowsum(dys_v * us_ref[...])
        ub = up_ref[...].astype(MXU)
        dbr_ref[0] += _dot_tn(ub, gbr)
        dbi_ref[0] += _dot_tn(ub, gbi)
        dcr_ref[0] += _dot_tn(dyb, sr.astype(MXU))
        dci_ref[0] -= _dot_tn(dyb, si.astype(MXU))

    blk = lambda: pl.BlockSpec((1, 8 * SSM_H, w), lambda j, i: (j, 0, 0))
    rowl = lambda: pl.BlockSpec((tm, LANES), lambda j, i: (nt - 1 - i, j))
    roww = lambda: pl.BlockSpec((tm, w), lambda j, i: (nt - 1 - i, j))
    halo = lambda: pl.BlockSpec((8, w), lambda j, i: (jnp.maximum((nt - 1 - i) * hb - 1, 0), j))
    return pl.pallas_call(
        body, name="s5_bwd", grid=(SSM_BLK, nt),
        in_specs=[rowl(), rowl(), roww(), roww(), halo(), halo(),
                  pl.BlockSpec((1, w), lambda j, i: (0, j)), pl.BlockSpec((1, w), lambda j, i: (0, j)),
                  blk(), blk(), blk(), blk(),
                  pl.BlockSpec((1, LANES), lambda j, i: (0, j))],
        out_specs=[rowl(),
                   pl.BlockSpec((1, 8, w), lambda j, i: (j, 0, 0)), pl.BlockSpec((1, 8, LANES), lambda j, i: (j, 0, 0)),
                   blk(), blk(), blk(), blk()],
        out_shape=[_sds((S, SSM_W), MXU), _sds((SSM_BLK, 8, w)), _sds((SSM_BLK, 8, LANES)),
                   _sds((SSM_BLK, 8 * SSM_H, w)), _sds((SSM_BLK, 8 * SSM_H, w)),
                   _sds((SSM_BLK, 8 * SSM_H, w)), _sds((SSM_BLK, 8 * SSM_H, w))],
        scratch_shapes=[pltpu.VMEM((8, 8, w), F32), pltpu.VMEM((8, w), F32),
                        pltpu.VMEM((tm, w), F32), pltpu.VMEM((tm, w), F32),
                        pltpu.VMEM((tm, LANES), F32), pltpu.VMEM((tm, LANES), F32), pltpu.VMEM((tm, LANES), F32)],
        compiler_params=_cp("parallel", "arbitrary"),
    )(*_in_hbm([dys, us, st_re, st_im, st_re, st_im, abar_re, abar_im, b_re, b_im, c_re, c_im, d_skip]))


def _in_bwd(dus, duv, dgl, dx1, x, g_mix, w_in, tm):
    S = x.shape[0]

    def body(dus_ref, duv_ref, dgl_ref, dx1_ref, x_ref, g_ref, w_ref, gx_ref, dg_ref):
        @pl.when(pl.program_id(0) == 0)
        def _():
            dg_ref[...] = jnp.zeros_like(dg_ref)

        dh = (_dot(dus_ref[...], w_ref[0:SSM_W, :])
              + _dot(duv_ref[...], w_ref[SSM_W:SSM_W + 2 * SGU_W, :])
              + _dot(dgl_ref[...], w_ref[SSM_W + 2 * SGU_W:, :]))
        xv = x_ref[...]
        r = _rms(xv)
        xn = xv * r
        dg_ref[...] += _rowsum(dh * xn)
        gx_ref[...] = dx1_ref[...] + _rms_bwd(dh * g_ref[...], xn, r)

    row = lambda n: pl.BlockSpec((tm, n), lambda i: (i, 0))
    return pl.pallas_call(
        body, name="in_bwd", grid=(S // tm,),
        in_specs=[row(SSM_W), row(2 * SGU_W), row(2 * D_MODEL), row(D_MODEL), row(D_MODEL), _full((1, D_MODEL)),
                  _full(w_in.shape)],
        out_specs=[row(D_MODEL), _full((1, D_MODEL))],
        out_shape=[_sds((S, D_MODEL)), _sds((1, D_MODEL))],
        compiler_params=_cp("arbitrary"),
    )(*_in_hbm([dus, duv, dgl, dx1, x, g_mix, w_in]))


def _pick(n, cands):
    for c in cands:
        if n % c == 0:
            return c
    return n


def _wgrad_split(a, b, nsplit, tk, name):
    S, K = a.shape
    N = b.shape[1]
    c = N // nsplit

    def body(a_ref, b_ref, o_ref):
        prod = _dot_tn(a_ref[...], b_ref[...])
        for d in range(nsplit):
            o_ref[d] = prod[:, c * d:c * (d + 1)].astype(MXU)

    return pl.pallas_call(
        body, name=name, grid=(K // tk,),
        in_specs=[pl.BlockSpec((S, tk), lambda k: (0, k)), _full((S, N))],
        out_specs=pl.BlockSpec((nsplit, tk, c), lambda k: (0, k, 0)),
        out_shape=_sds((nsplit, K, c), MXU),
        compiler_params=_cp("parallel"),
    )(*_in_hbm([a, b]))


def _wgrad_in_t(dps, h1, name):
    S, K = h1.shape
    cw = 512
    counts = [b.shape[1] // cw for b in dps]
    starts = [sum(counts[:i]) for i in range(len(dps))]
    nblk = sum(counts)

    def body(*refs):
        b_refs = refs[:len(dps)]
        h_ref, o_ref = refs[len(dps):]
        j = pl.program_id(0)
        for b_ref, st, cnt in zip(b_refs, starts, counts):
            @pl.when(jnp.logical_and(j >= st, j < st + cnt))
            def _():
                o_ref[...] = _dot_tn(b_ref[...], h_ref[...]).astype(MXU)

    def src_spec(st, cnt):
        return pl.BlockSpec((S, cw), lambda j: (0, jnp.clip(j - st, 0, cnt - 1)))

    return pl.pallas_call(
        body, name=name, grid=(nblk,),
        in_specs=[src_spec(st, cnt) for st, cnt in zip(starts, counts)] + [_full((S, K))],
        out_specs=pl.BlockSpec((cw, K), lambda j: (j, 0)),
        out_shape=_sds((nblk * cw, K), MXU),
        compiler_params=_cp("arbitrary"),
    )(*_in_hbm([*dps, h1]))


def _wgrad_blk(a3, b3, nblk, a_of, b_of, name):
    S, K = a3.shape[1:]
    N = b3.shape[2]

    def body(a_ref, b_ref, o_ref):
        o_ref[0] = _dot_tn(a_ref[0], b_ref[0]).astype(MXU)

    return pl.pallas_call(
        body, name=name, grid=(nblk,),
        in_specs=[pl.BlockSpec((1, S, K), lambda b: (a_of(b), 0, 0)),
                  pl.BlockSpec((1, S, N), lambda b: (b_of(b), 0, 0))],
        out_specs=pl.BlockSpec((1, K, N), lambda b: (b, 0, 0)),
        out_shape=_sds((nblk, K, N), MXU),
        compiler_params=pltpu.CompilerParams(dimension_semantics=("parallel",), vmem_limit_bytes=WGRAD_VMEM_LIMIT),
    )(*_in_hbm([a3, b3]))


def _assemble_cols(blocks_list, name):
    def body(*refs):
        n = len(blocks_list)
        for b_ref, o_ref in zip(refs[:n], refs[n:]):
            c = b_ref.shape[2]
            for d in range(N_DEV):
                o_ref[:, c * d:c * (d + 1)] = b_ref[d]

    outs = [_sds((b.shape[1], N_DEV * b.shape[2]), b.dtype) for b in blocks_list]
    return pl.pallas_call(
        body, name=name, grid=(1,), in_specs=[_full(b.shape) for b in blocks_list],
        out_specs=[_full(o.shape) for o in outs], out_shape=outs, compiler_params=_cp("arbitrary"),
    )(*_in_hbm(blocks_list))


def _tile(S, want):
    return want if S % want == 0 else S


def _local_step(x, tgt, p, mixer_relay, mixer_weights, ffn_weights, grads_out):
    S = x.shape[0]
    tm = _tile(S, 256)
    tl = _tile(S, 512)

    rep = lambda a: jnp.repeat(a, SSM_H, axis=0)
    are = rep(p["a_re"])
    aim = rep(p["a_im"])
    ldt = jnp.broadcast_to(rep(p["log_dt"].reshape(SSM_G, 1)), are.shape)
    br_t = p["b_re_t"].reshape(are.shape)
    bi_t = p["b_im_t"].reshape(are.shape)
    abr, abi, bbr, bbi = _s5_params_fwd(are, aim, ldt, br_t, bi_t)
    head = lambda a: a.reshape(SSM_G, SSM_H, SSM_P)[:, 0, :].reshape(1, SSM_G * SSM_P)
    abar_re, abar_im = head(abr), head(abi)
    bd_br = _blockdiag(bbr).astype(MXU)
    bd_bi = _blockdiag(bbi).astype(MXU)
    bd_cr = _blockdiag(p["c_re"].reshape(are.shape)).astype(MXU)
    bd_ci = _blockdiag(p["c_im"].reshape(are.shape)).astype(MXU)
    d_skip = p["d_skip"].reshape(1, SSM_W)

    tril = jnp.tril(jnp.ones((CHUNK, CHUNK), dtype=bool))
    ws = jnp.where(tril[None], p["w_s"], 0.0)
    pair = lambda w: w.reshape(SGU_G // 2, 2, CHUNK, CHUNK).transpose(0, 2, 1, 3).reshape(SGU_G // 2, CHUNK, 2 * CHUNK)
    ws_b = pair(ws).astype(MXU)
    ws_t = pair(ws.transpose(0, 2, 1)).astype(MXU)
    bias_s = jnp.repeat(p["b_s"].T, SGU_D, axis=1)

    g_mix = p["g_mix"].reshape(1, D_MODEL)
    g_ffn = p["g_ffn"].reshape(1, D_MODEL)
    g_final = p["g_final"].reshape(1, D_MODEL)
    g_sgu = p["g_sgu"].reshape(1, SGU_W)
    b_glu = p["b_glu"].reshape(1, SSM_W)
    conv_b = p["conv_b"].reshape(2 * FF_NCB, 1, FF_CW)
    tf = _tile(S, 256)

    h1, us, uv, gl = _in_fwd(x, g_mix, p["w_in_t"], tl)
    token = mixer_relay(us)
    st_re, st_im, ys = _s5_fwd(us, abar_re, abar_im, bd_br, bd_bi, bd_cr, bd_ci, d_skip + token[0:1, 0:1], tl)
    p = dict(p, **mixer_weights(ys))
    yg, yap, sg, ya, yb, m, x1, h2 = _mix_fwd(x, ys, uv, gl, p["w_glu"], b_glu, p["w_proj_a"], g_sgu, ws_b, bias_s,
                                              p["w_proj_b"], p["w_out"], g_ffn, tm)
    w_up, conv_w, w_down = ffn_weights(h2)
    pair_lanes = lambda a: a.reshape(N_DEV // 2, 2, a.shape[1], FF_SHARD).transpose(0, 2, 1, 3).reshape(
        N_DEV // 2, a.shape[1], FF_CW)
    w_up = w_up.reshape(2 * FF_NCB, FF_CW, D_MODEL)
    conv_w = pair_lanes(conv_w)
    up, ab, ff, dx2, dx2b, loss, dg_final = _ffn_fwd(h2, x1, tgt, w_up, conv_w, conv_b, w_down, g_final, tf)

    dup, dx1, dx1b, dconv, dg_ffn = _ffn_bwd(dx2, up, ab, x1, w_up, conv_w, w_down, g_ffn, tf)
    rows8 = lambda g: g.reshape(N_DEV, g.shape[1] // N_DEV, g.shape[2])
    g_up = _wgrad_blk(dup.reshape(2 * FF_NCB, S, FF_CW), h2[None], 2 * FF_NCB, lambda b: b, lambda b: 0,
                      "wgrad_up").reshape(N_DEV, FF_SHARD, D_MODEL)
    g_down = _wgrad_blk(ff, dx2b[None], FF_NCB, lambda b: b, lambda b: 0, "wgrad_down").reshape(
        N_DEV, D_FF // N_DEV, D_MODEL)
    token = grads_out(("w_up", "w_down"), (g_up, g_down))
    dgl, dya, dyb, dz, dys, duv, db_glu, dg_sgu, dws, dbs = _mix_bwd(
        dx1, gl, ya, yb, ys, uv, p["w_out"], p["w_proj_a"], p["w_proj_b"], p["w_glu"], b_glu + token[0:1, 0:1], g_sgu,
        ws_b, ws_t, bias_s, tm)
    token = grads_out(("w_glu", "w_proj_a", "w_proj_b", "w_out"),
                      (rows8(_wgrad_split(yg, dz, 1, SSM_W, "wgrad_glu")),
                       _wgrad_split(yap, dya, N_DEV, SSM_W, "wgrad_pa"),
                       _wgrad_split(sg, dyb, N_DEV, SGU_W, "wgrad_pb"),
                       rows8(_wgrad_split(m, dx1b, 1, 512, "wgrad_out"))))
    dus, dab, dd, dbbr, dbbi, dcr, dci = _s5_bwd(dys, us, st_re, st_im, abar_re, abar_im, bd_br, bd_bi, bd_cr, bd_ci,
                                                 d_skip + token[0:1, 0:1], tl)
    g_in = _wgrad_in_t([dus, duv, dgl], h1, "wgrad_in")
    token = grads_out(("w_in",), (g_in.reshape(N_DEV, g_in.shape[0] // N_DEV, D_MODEL),))
    grad_x, dg_mix = _in_bwd(dus, duv, dgl, dx1, x, g_mix + token[0:1, 0:1], p["w_in_t"], tl)

    spread = lambda v: jnp.repeat(v.reshape(SSM_G, SSM_P), SSM_H, axis=0) * (1.0 / SSM_H)
    dabr = spread(dab[:, 0, :])
    dabi = spread(dab[:, 1, :])
    dare, daim, dldt, dbr_t, dbi_t = _s5_params_bwd(are, aim, ldt, br_t, bi_t, dabr, dabi,
                                                    _unblockdiag(dbbr), _unblockdiag(dbbi))
    fold = lambda a: a.reshape(SSM_G, SSM_H, SSM_P).sum(axis=1)

    grads = {
        "g_mix": dg_mix,
        "a_re": fold(dare), "a_im": fold(daim), "log_dt": fold(dldt).sum(axis=1),
        "b_re": dbr_t, "b_im": dbi_t,
        "c_re": _unblockdiag(dcr).reshape(SSM_G, SSM_H, SSM_P),
        "c_im": _unblockdiag(dci).reshape(SSM_G, SSM_H, SSM_P),
        "d_skip": dd[:, 0, :].reshape(SSM_W),
        "b_glu": db_glu,
        "g_sgu": dg_sgu,
        "w_s": dws,
        "b_s": dbs.reshape(CHUNK, SGU_G, SGU_D).sum(axis=-1).T,
        "g_ffn": dg_ffn,
        "conv_w": dconv[:, 0:3, :].reshape(N_DEV // 2, 3, 2, FF_SHARD).transpose(0, 2, 1, 3).reshape(
            N_DEV, 3, FF_SHARD),
        "conv_b": dconv[:, 3, :].reshape(2 * D_FF),
        "g_final": dg_final,
    }
    return loss, grad_x, grads


_ANY = pl.BlockSpec(memory_space=pl.ANY)
_MESH = pl.DeviceIdType.MESH


def _allgather(shards, dtypes, name, cast_only=()):
    n = len(shards)
    e = len(cast_only)

    def body(*refs):
        in_refs, extra_in = refs[:n], refs[n:n + e]
        out_refs, extra_out = refs[n + e:2 * n + e], refs[2 * n + e:2 * n + 2 * e]
        stage = refs[2 * n + 2 * e:3 * n + 2 * e]
        send_sems, recv_sems, local_sems = refs[3 * n + 2 * e:]
        for a in range(n):
            stage[a][...] = in_refs[a][...].astype(dtypes[a])
        for i in range(e):
            extra_out[i][...] = extra_in[i][...].astype(MXU)
        x, y, c = lax.axis_index("x"), lax.axis_index("y"), lax.axis_index("c")
        me, sibling = (x, y, c), (x, y, 1 - c)
        chips = [(1 - x, y), (x, 1 - y), (1 - x, 1 - y)]

        def slot(a, px, py, pc):
            return out_refs[a].at[4 * px + 2 * py + pc]

        def copy(a, k, block, to, src=None):
            return pltpu.make_async_remote_copy(
                src_ref=slot(a, *block) if src is None else src, dst_ref=slot(a, *block),
                send_sem=send_sems.at[a, k], recv_sem=recv_sems.at[a, k], device_id=to, device_id_type=_MESH)

        mine = [pltpu.make_async_copy(stage[a], slot(a, *me), local_sems.at[a]) for a in range(n)]
        for cp in mine:
            cp.start()
        first = []
        for j, chip in enumerate(chips):
            first += [copy(a, 1 + j, me, (*chip, c), src=stage[a]) for a in range(n)]
        first += [copy(a, 0, me, sibling, src=stage[a]) for a in range(n)]
        for cp in first:
            cp.start()
        passed = []
        for j, chip in enumerate(chips):
            for a in range(n):
                copy(a, 1 + j, (*chip, c), me).wait_recv()
                fwd = copy(a, 4 + j, (*chip, c), sibling)
                fwd.start()
                passed.append(fwd)
        for a in range(n):
            copy(a, 0, sibling, me).wait_recv()
        for j, chip in enumerate(chips):
            for a in range(n):
                copy(a, 4 + j, (*chip, 1 - c), me).wait_recv()
        for cp in first + passed:
            cp.wait_send()
        for cp in mine:
            cp.wait()

    res = pl.pallas_call(
        body, name=name, grid=(1,), in_specs=[_full(s.shape) for s in list(shards) + list(cast_only)],
        out_specs=[_ANY] * n + [_full(s.shape) for s in cast_only],
        out_shape=[_sds((N_DEV,) + s.shape, dt) for s, dt in zip(shards, dtypes)]
                  + [_sds(s.shape, MXU) for s in cast_only],
        scratch_shapes=[pltpu.VMEM(s.shape, dt) for s, dt in zip(shards, dtypes)]
                       + [pltpu.SemaphoreType.DMA((n, 7)), pltpu.SemaphoreType.DMA((n, 7)), pltpu.SemaphoreType.DMA((n,))],
        compiler_params=pltpu.CompilerParams(vmem_limit_bytes=VMEM_LIMIT),
    )(*_in_hbm([*shards, *cast_only]))
    return res[:n], res[n:]


def _all_to_all(sends, name):
    n = len(sends)

    def body(*refs):
        send_refs, recv_refs = refs[:n], refs[n:2 * n]
        send_sems, recv_sems, local_sems = refs[2 * n:]
        x, y, c = lax.axis_index("x"), lax.axis_index("y"), lax.axis_index("c")
        me = 4 * x + 2 * y + c
        mine = [pltpu.make_async_copy(send_refs[a].at[me], recv_refs[a].at[me], local_sems.at[a]) for a in range(n)]
        for cp in mine:
            cp.start()
        copies = []
        for k in (2, 4, 6, 3, 5, 7, 1):
            px = 1 - x if k & 4 else x
            py = 1 - y if k & 2 else y
            pc = 1 - c if k & 1 else c
            peer = 4 * px + 2 * py + pc
            for a in range(n):
                sems = dict(send_sem=send_sems.at[a, k - 1], recv_sem=recv_sems.at[a, k - 1],
                            device_id=(px, py, pc), device_id_type=_MESH)
                cp = pltpu.make_async_remote_copy(src_ref=send_refs[a].at[peer], dst_ref=recv_refs[a].at[me], **sems)
                cp.start()
                landing = pltpu.make_async_remote_copy(src_ref=send_refs[a].at[peer], dst_ref=recv_refs[a].at[peer],
                                                       **sems)
                copies.append((cp, landing))
        for _, landing in copies:
            landing.wait_recv()
        for cp, _ in copies:
            cp.wait_send()
        for cp in mine:
            cp.wait()

    return pl.pallas_call(
        body, name=name, in_specs=[_ANY] * n, out_specs=[_ANY] * n,
        out_shape=[_sds(s.shape, s.dtype) for s in sends],
        scratch_shapes=[pltpu.SemaphoreType.DMA((n, 7)), pltpu.SemaphoreType.DMA((n, 7)), pltpu.SemaphoreType.DMA((n,))],
    )(*sends)


_HBM = pl.BlockSpec(memory_space=pltpu.HBM)
_SEM = pl.BlockSpec(memory_space=pltpu.SEMAPHORE)
_EFFECT = pltpu.SideEffectType.DATAFLOW_SIDE_EFFECTING
_PEER_ORDER = (2, 4, 6, 3, 5, 7, 1)


def _peer(k):
    x, y, c = lax.axis_index("x"), lax.axis_index("y"), lax.axis_index("c")
    px = 1 - x if k & 4 else x
    py = 1 - y if k & 2 else y
    pc = 1 - c if k & 1 else c
    return (px, py, pc), 4 * px + 2 * py + pc


_SAME_CORE_AND_SIBLING = (2, 4, 6, 1)


def _push_start(srcs, lands, slotted, name, peers=_PEER_ORDER):
    n = len(srcs)

    def body(*refs):
        src_refs, land_refs = refs[:n], refs[n:2 * n]
        send_sems, recv_sems, token = refs[2 * n], refs[2 * n + 1], refs[-1]
        me = 4 * lax.axis_index("x") + 2 * lax.axis_index("y") + lax.axis_index("c")
        for k in peers:
            dev, peer = _peer(k)
            for a in range(n):
                pltpu.make_async_remote_copy(
                    src_ref=src_refs[a].at[peer] if slotted else src_refs[a], dst_ref=land_refs[a].at[me],
                    send_sem=send_sems.at[7 * a + k - 1], recv_sem=recv_sems.at[7 * a + k - 1],
                    device_id=dev, device_id_type=_MESH).start()
        token[...] = jnp.zeros_like(token)

    bufs = list(srcs) + list(lands)
    res = pl.pallas_call(
        body, name=name, in_specs=[_HBM] * (2 * n),
        out_specs=(_SEM, _SEM, *[_HBM] * (2 * n), pl.BlockSpec(memory_space=pltpu.VMEM)),
        out_shape=(pltpu.SemaphoreType.DMA((7 * n,)), pltpu.SemaphoreType.DMA((7 * n,)),
                   *[pltpu.HBM(b.shape, b.dtype) for b in bufs], _sds((8, LANES))),
        input_output_aliases={i: 2 + i for i in range(2 * n)},
        compiler_params=pltpu.CompilerParams(has_side_effects=_EFFECT),
    )(*[pltpu.with_memory_space_constraint(b, pltpu.HBM) for b in bufs])
    return res[0], res[1], res[2:2 + n], res[2 + n:2 + 2 * n], res[-1]


def _push_wait(send_sems, recv_sems, srcs, lands, slotted, after, name, peers=_PEER_ORDER):
    n = len(srcs)

    def body(*refs):
        src_refs, land_refs = refs[:n], refs[n:2 * n]
        send_sems, recv_sems = refs[2 * n], refs[2 * n + 1]
        for k in peers:
            dev, peer = _peer(k)
            for a in range(n):
                cp = pltpu.make_async_remote_copy(
                    src_ref=src_refs[a].at[peer] if slotted else src_refs[a], dst_ref=land_refs[a].at[peer],
                    send_sem=send_sems.at[7 * a + k - 1], recv_sem=recv_sems.at[7 * a + k - 1],
                    device_id=dev, device_id_type=_MESH)
                cp.wait_send()
                cp.wait_recv()

    bufs = list(srcs) + list(lands)
    res = pl.pallas_call(
        body, name=name, in_specs=[_HBM] * (2 * n) + [_SEM, _SEM] + [_ANY] * len(after), out_specs=[_HBM] * (2 * n),
        out_shape=[pltpu.HBM(b.shape, b.dtype) for b in bufs],
        input_output_aliases={i: i for i in range(2 * n)},
        compiler_params=pltpu.CompilerParams(has_side_effects=_EFFECT),
    )(*bufs, send_sems, recv_sems, *after)
    return res[n:]


def _other_chips():
    x, y = lax.axis_index("x"), lax.axis_index("y")
    return ((1 - x, y), (x, 1 - y), (1 - x, 1 - y))


def _relay_start(lands, name):
    n = len(lands)

    def body(*refs):
        land_refs = refs[:n]
        send_sems, recv_sems, token = refs[n], refs[n + 1], refs[-1]
        x, y, c = lax.axis_index("x"), lax.axis_index("y"), lax.axis_index("c")
        for j, (px, py) in enumerate(_other_chips()):
            slot = 4 * px + 2 * py + c
            for a in range(n):
                pltpu.make_async_remote_copy(
                    src_ref=land_refs[a].at[slot], dst_ref=land_refs[a].at[slot],
                    send_sem=send_sems.at[3 * a + j], recv_sem=recv_sems.at[3 * a + j],
                    device_id=(x, y, 1 - c), device_id_type=_MESH).start()
        token[...] = jnp.zeros_like(token)

    res = pl.pallas_call(
        body, name=name, in_specs=[_HBM] * n,
        out_specs=(_SEM, _SEM, *[_HBM] * n, pl.BlockSpec(memory_space=pltpu.VMEM)),
        out_shape=(pltpu.SemaphoreType.DMA((3 * n,)), pltpu.SemaphoreType.DMA((3 * n,)),
                   *[pltpu.HBM(b.shape, b.dtype) for b in lands], _sds((8, LANES))),
        input_output_aliases={i: 2 + i for i in range(n)},
        compiler_params=pltpu.CompilerParams(has_side_effects=_EFFECT),
    )(*[pltpu.with_memory_space_constraint(b, pltpu.HBM) for b in lands])
    return res[0], res[1], res[2:2 + n], res[-1]


def _relay_wait(send_sems, recv_sems, lands, after, name):
    n = len(lands)

    def body(*refs):
        land_refs = refs[:n]
        send_sems, recv_sems = refs[n], refs[n + 1]
        x, y, c = lax.axis_index("x"), lax.axis_index("y"), lax.axis_index("c")
        for j, (px, py) in enumerate(_other_chips()):
            sent, received = 4 * px + 2 * py + c, 4 * px + 2 * py + (1 - c)
            for a in range(n):
                cp = pltpu.make_async_remote_copy(
                    src_ref=land_refs[a].at[sent], dst_ref=land_refs[a].at[received],
                    send_sem=send_sems.at[3 * a + j], recv_sem=recv_sems.at[3 * a + j],
                    device_id=(x, y, 1 - c), device_id_type=_MESH)
                cp.wait_send()
                cp.wait_recv()

    return pl.pallas_call(
        body, name=name, in_specs=[_HBM] * n + [_SEM, _SEM] + [_ANY] * len(after), out_specs=[_HBM] * n,
        out_shape=[pltpu.HBM(b.shape, b.dtype) for b in lands],
        input_output_aliases={i: i for i in range(n)},
        compiler_params=pltpu.CompilerParams(has_side_effects=_EFFECT),
    )(*lands, send_sems, recv_sems, *after)


def _adamw(w, g, m, v):
    m2 = ADAM_B1 * m + (1.0 - ADAM_B1) * g
    v2 = ADAM_B2 * v + (1.0 - ADAM_B2) * (g * g)
    m_hat = m2 / (1.0 - ADAM_B1 ** ADAM_STEP)
    v_hat = v2 / (1.0 - ADAM_B2 ** ADAM_STEP)
    delta = -ADAM_LR * (m_hat / (jnp.sqrt(v_hat) + ADAM_EPS) + ADAM_WD * w)
    return delta, m2, v2


def _adam_shard(parts, w, m, v, name):
    _, r, c = w.shape
    tr = max(t for t in range(16, 257, 16) if r % t == 0)

    def body(p_ref, w_ref, m_ref, v_ref, g_ref, d_ref, m2_ref, v2_ref):
        g = p_ref[0].astype(F32)
        for s in range(1, N_DEV):
            g = g + p_ref[s].astype(F32)
        g_ref[0] = g
        d_ref[0], m2_ref[0], v2_ref[0] = _adamw(w_ref[0], g, m_ref[0], v_ref[0])

    row = lambda: pl.BlockSpec((1, tr, c), lambda i: (0, i, 0))
    return pl.pallas_call(
        body, name=name, grid=(r // tr,),
        in_specs=[pl.BlockSpec((N_DEV, tr, c), lambda i: (0, i, 0)), row(), row(), row()],
        out_specs=[row(), row(), row(), row()], out_shape=[_sds((1, r, c))] * 4,
        compiler_params=_cp("parallel"),
    )(*_in_hbm([parts, w, m, v]))


def _adam_small(gs, ws, ms, vs, name):
    n = len(gs)

    def body(*refs):
        ins, outs = refs[:4 * n], refs[4 * n:]
        for i in range(n):
            g = ins[i][...]
            d, m2, v2 = _adamw(ins[n + i][...], g, ins[2 * n + i][...], ins[3 * n + i][...])
            outs[i][...] = d
            outs[n + i][...] = m2
            outs[2 * n + i][...] = v2

    res = pl.pallas_call(
        body, name=name, grid=(1,), in_specs=[_full(w.shape) for w in ws] * 4,
        out_specs=[_full(w.shape) for w in ws] * 3, out_shape=[_sds(w.shape) for w in ws] * 3,
        compiler_params=_cp("arbitrary"),
    )(*_in_hbm([*gs, *ws, *ms, *vs]))
    return res[:n], res[n:2 * n], res[2 * n:]


def _sum_slots(parts, name):
    R = parts.shape[1]

    def body(p_ref, o_ref):
        g = p_ref[0]
        for s in range(1, N_DEV):
            g = g + p_ref[s]
        o_ref[...] = g

    return pl.pallas_call(body, name=name, grid=(1,), in_specs=[_full(parts.shape)], out_specs=_full((R, LANES)),
                          out_shape=_sds((R, LANES)))(*_in_hbm([parts]))


def _pad_to(a, n, axis):
    extra = n - a.shape[axis]
    if extra == 0:
        return a
    widths = [(0, 0)] * a.ndim
    widths[axis] = (0, extra)
    return jnp.pad(a, widths)


def _ceil_to(n, k):
    return -(-n // k) * k


def _pack_rows(flats, rows_multiple):
    parts = [_pad_to(f, _ceil_to(f.shape[-1], LANES), f.ndim - 1) for f in flats]
    cat = jnp.concatenate(parts, axis=-1)
    total = _ceil_to(cat.shape[-1], LANES * rows_multiple)
    cat = _pad_to(cat, total, cat.ndim - 1)
    return cat.reshape(cat.shape[:-1] + (total // LANES, LANES))


def _unpack_rows(buf, sizes):
    flat = buf.reshape(buf.shape[:-2] + (-1,))
    out, off = [], 0
    for n in sizes:
        out.append(flat[..., off:off + n])
        off += _ceil_to(n, LANES)
    return out


_MIX_BIG = ("w_in", "w_glu", "w_proj_a", "w_proj_b", "w_out")
_BIG = _MIX_BIG + ("w_up", "w_down")
_SMALL = ("g_mix", "a_re", "a_im", "log_dt", "b_re", "b_im", "c_re", "c_im", "d_skip", "b_glu", "g_sgu", "w_s", "b_s",
          "g_ffn", "conv_b", "g_final")
_SMALL_ROWS_MULTIPLE = 8 * N_DEV
_TRANSPOSED = ("w_in", "w_up", "b_re", "b_im")


def _as_2d(a):
    return a.reshape(-1, a.shape[-1]) if a.ndim > 1 else a.reshape(1, -1)


def kernel(x, g_mix, w_in, a_re, a_im, log_dt, b_re, b_im, c_re, c_im, d_skip, w_glu, b_glu, w_proj_a, g_sgu, w_s, b_s, w_proj_b, w_out, g_ffn, w_up, conv_w, conv_b, w_down, g_final, loss_target, m_g_mix, m_w_in, m_a_re, m_a_im, m_log_dt, m_b_re, m_b_im, m_c_re, m_c_im, m_d_skip, m_w_glu, m_b_glu, m_w_proj_a, m_g_sgu, m_w_s, m_b_s, m_w_proj_b, m_w_out, m_g_ffn, m_w_up, m_conv_w, m_conv_b, m_w_down, m_g_final, v_g_mix, v_w_in, v_a_re, v_a_im, v_log_dt, v_b_re, v_b_im, v_c_re, v_c_im, v_d_skip, v_w_glu, v_b_glu, v_w_proj_a, v_g_sgu, v_w_s, v_b_s, v_w_proj_b, v_w_out, v_g_ffn, v_w_up, v_conv_w, v_conv_b, v_w_down, v_g_final):
    args = dict(locals())
    me = 4 * lax.axis_index("x") + 2 * lax.axis_index("y") + lax.axis_index("c")

    def own_slot(buf, block):
        return lax.dynamic_update_slice(buf, block[None], (me,) + (0,) * block.ndim)

    for n in _TRANSPOSED:
        for pre in ("", "m_", "v_"):
            args[pre + n] = jnp.swapaxes(args[pre + n], -1, -2)
    later = ("w_glu", "w_proj_a", "w_proj_b", "w_out", "w_up", "w_down")
    (w_in_g,), casts = _allgather([args["w_in"][0]], [MXU], "allgather_w_in", cast_only=[args[n][0] for n in later])
    sh = dict(zip(later, casts))

    def start_push(srcs, tag, peers):
        lands = [own_slot(lax.empty((N_DEV,) + s.shape, s.dtype), s) for s in srcs]
        send_sems, recv_sems, srcs, lands, token = _push_start(srcs, lands, False, "push_" + tag, peers)
        return (send_sems, recv_sems, srcs, lands), token

    mix_push, token_a = start_push([sh[n] for n in later[:4]], "mixer_weights", _SAME_CORE_AND_SIBLING)
    ffn_push, token_b = start_push([sh["w_up"], sh["w_down"], conv_w[0]], "ffn_weights", _PEER_ORDER)
    p = {n: (args[n][0] if n != "g_final" else args[n]) for n in _SMALL if n not in _TRANSPOSED}
    p.update(w_in_t=w_in_g.reshape(SSM_W + 2 * SGU_W + 2 * D_MODEL, D_MODEL),
             b_re_t=args["b_re"][0], b_im_t=args["b_im"][0])
    p["g_mix"] = p["g_mix"] + (token_a[0:1, 0:1] + token_b[0:1, 0:1])
    relay = {}

    def mixer_relay(after):
        lands = _push_wait(*mix_push, False, [after], "wait_mixer_weights", _SAME_CORE_AND_SIBLING)
        relay["send"], relay["recv"], relay["lands"], token = _relay_start(lands, "relay_mixer_weights")
        return token

    def mixer_weights(after):
        w_glu_g, w_pa_g, w_pb_g, w_out_g = _relay_wait(relay["send"], relay["recv"], relay["lands"], [after],
                                                       "wait_relay_mixer_weights")
        w_pa_full, w_pb_full = _assemble_cols([w_pa_g, w_pb_g], "assemble_cols")
        return dict(w_glu=w_glu_g.reshape(SSM_W, SSM_W), w_proj_a=w_pa_full, w_proj_b=w_pb_full,
                    w_out=w_out_g.reshape(D_MODEL, D_MODEL))

    def ffn_weights(after):
        w_up_g, w_down_g, conv_w_g = _push_wait(*ffn_push, False, [after], "wait_ffn_weights")
        return w_up_g, conv_w_g, w_down_g.reshape(D_FF, D_MODEL)

    pushes = []

    def grads_out(names, sends):
        lands = [own_slot(lax.empty(s.shape, s.dtype), lax.dynamic_index_in_dim(s, me, 0, keepdims=False))
                 for s in sends]
        send_sems, recv_sems, srcs, lands, token = _push_start(list(sends), lands, True, "push_grads_" + names[0])
        pushes.append((names, send_sems, recv_sems, srcs, lands))
        return token

    loss_part, grad_x, grads = _local_step(x[0], loss_target[0], p, mixer_relay, mixer_weights, ffn_weights, grads_out)

    small_names = _SMALL + ("conv_w", "loss")
    small_g = dict(grads, loss=loss_part[0, 0:1])
    flats = [small_g[n].reshape(-1) for n in small_names]
    small_sizes = [f.shape[0] for f in flats]
    g_small = _pack_rows(flats, _SMALL_ROWS_MULTIPLE)
    rs8 = g_small.shape[0] // N_DEV
    grads_out(("small",), (g_small.reshape(N_DEV, rs8, LANES),))

    out = {}
    done = [g_small]
    for names, send_sems, recv_sems, srcs, lands in pushes:
        parts = _push_wait(send_sems, recv_sems, srcs, lands, True, done, "wait_grads_" + names[0])
        if names == ("small",):
            recv_small, = parts
            break
        for n, part in zip(names, parts):
            res = _adam_shard(part, args[n], args["m_" + n], args["v_" + n], "adam_" + n)
            for kind, v in zip(("grad_", "delta_", "new_m_", "new_v_"), res):
                out[kind + n] = v
            done = [res[0]]
    small_mine = _sum_slots(recv_small, "sum_small")
    g_small_all = _allgather([small_mine], [F32], "allgather_small")[0][0].reshape(N_DEV * rs8, LANES)
    pieces = dict(zip(small_names, _unpack_rows(g_small_all, small_sizes)))
    loss = pieces["loss"][0]
    dconv_w = lax.dynamic_index_in_dim(pieces["conv_w"].reshape(N_DEV, 3, FF_SHARD), me, axis=0, keepdims=False)
    names2 = _SMALL + ("conv_w",)
    gs = [pieces[n].reshape(_as_2d(args[n]).shape) for n in _SMALL] + [dconv_w]
    ds, m2s, v2s = _adam_small(gs, [_as_2d(args[n]) for n in names2], [_as_2d(args["m_" + n]) for n in names2],
                               [_as_2d(args["v_" + n]) for n in names2], "adam_small")
    for n, res in zip(names2, zip(gs, ds, m2s, v2s)):
        for kind, v in zip(("grad_", "delta_", "new_m_", "new_v_"), res):
            out[kind + n] = v.reshape(args[n].shape)
    order = ("g_mix", "w_in", "a_re", "a_im", "log_dt", "b_re", "b_im", "c_re", "c_im", "d_skip", "w_glu", "b_glu",
             "w_proj_a", "g_sgu", "w_s", "b_s", "w_proj_b", "w_out", "g_ffn", "w_up", "conv_w", "conv_b", "w_down",
             "g_final")
    res = [loss, grad_x.reshape(x.shape)]
    for kind in ("grad_", "delta_", "new_m_", "new_v_"):
        res += [jnp.swapaxes(out[kind + n], -1, -2) if n in _TRANSPOSED else out[kind + n] for n in order]
    return tuple(res)
```

```python
import functools
import math

import jax
import jax.numpy as jnp
from jax import lax
from jax.experimental import pallas as pl
from jax.experimental.pallas import tpu as pltpu

F32 = jnp.float32
MXU = jnp.bfloat16
EPS = 1e-6

D_MODEL = 1024
SSM_W = 512
SSM_G, SSM_H, SSM_P = 32, 16, 64
SSM_BLK = 4
SGU_W = 512
SGU_G, SGU_D, CHUNK = 8, 64, 128
D_FF = 2816
N_DEV = 8
FF_SHARD = 2 * D_FF // N_DEV
FF_CW = 2 * FF_SHARD
FF_NCB = D_FF // FF_CW
LANES = 128

ADAM_LR, ADAM_B1, ADAM_B2, ADAM_EPS, ADAM_WD, ADAM_STEP = 0.001, 0.9, 0.999, 1e-08, 0.01, 10

VMEM_LIMIT = 48 * 1024 * 1024
WGRAD_VMEM_LIMIT = 58 * 1024 * 1024
FFN_VMEM_LIMIT = 58 * 1024 * 1024


def _cp(*sem):
    return pltpu.CompilerParams(dimension_semantics=sem, vmem_limit_bytes=VMEM_LIMIT)


def _full(shape):
    n = len(shape)
    return pl.BlockSpec(shape, lambda *_: (0,) * n)


def _sds(shape, dtype=F32):
    return jax.ShapeDtypeStruct(shape, dtype)


def _in_hbm(arrays):
    return [pltpu.with_memory_space_constraint(a, pltpu.HBM) for a in arrays]


def _dot(a, b):
    return jnp.dot(a, b, preferred_element_type=F32)


def _dot_nt(a, b):
    return lax.dot_general(a, b, (((1,), (1,)), ((), ())), preferred_element_type=F32)


def _dot_tn(a, b):
    return lax.dot_general(a, b, (((0,), (0,)), ((), ())), preferred_element_type=F32)


_GELU_C = math.sqrt(2.0 / math.pi)


def _gelu(x):
    return 0.5 * x * (1.0 + jnp.tanh(_GELU_C * (x + 0.044715 * (x * x * x))))


def _gelu_and_grad(x):
    t = jnp.tanh(_GELU_C * (x + 0.044715 * (x * x * x)))
    g = 0.5 * x * (1.0 + t)
    dg = 0.5 * (1.0 + t) + 0.5 * x * (1.0 - t * t) * (_GELU_C * (1.0 + 3.0 * 0.044715 * (x * x)))
    return g, dg


def _sigmoid(x):
    return 0.5 * jnp.tanh(0.5 * x) + 0.5


def _rms(x):
    return lax.rsqrt(jnp.mean(x * x, axis=-1, keepdims=True) + EPS)


def _rms_bwd(dxn, xn, r):
    return r * (dxn - xn * jnp.mean(dxn * xn, axis=-1, keepdims=True))


def _rowsum(x):
    return jnp.sum(x, axis=0, keepdims=True)


def _fetch_once(pairs, sems):
    copies = [pltpu.make_async_copy(src, dst, sems.at[k]) for k, (src, dst) in enumerate(pairs)]
    for cp in copies:
        cp.start()
    for cp in copies:
        cp.wait()


def _s5_disc(are, aim, ldt, br, bi):
    dt = jnp.exp(ldt)
    mag = jnp.exp(dt * are)
    abr = mag * jnp.cos(dt * aim)
    abi = mag * jnp.sin(dt * aim)
    den = are * are + aim * aim
    nr = abr - 1.0
    ni = abi
    fr = (nr * are + ni * aim) / den
    fi = (ni * are - nr * aim) / den
    return abr, abi, fr * br - fi * bi, fr * bi + fi * br


def _s5_params_fwd(are, aim, ldt, br, bi):
    def body(are_ref, aim_ref, ldt_ref, br_ref, bi_ref, o0, o1, o2, o3):
        outs = _s5_disc(are_ref[...], aim_ref[...], ldt_ref[...], br_ref[...], bi_ref[...])
        for o, v in zip((o0, o1, o2, o3), outs):
            o[...] = v
    shp = are.shape
    return pl.pallas_call(body, name="s5_params_fwd", grid=(1,), in_specs=[_full(shp)] * 5, out_specs=[_full(shp)] * 4,
                          out_shape=[_sds(shp)] * 4)(*_in_hbm([are, aim, ldt, br, bi]))


def _s5_params_bwd(are, aim, ldt, br, bi, dabr, dabi, dbr, dbi):
    def body(are_ref, aim_ref, ldt_ref, br_ref, bi_ref, c0, c1, c2, c3, o0, o1, o2, o3, o4):
        prim = (are_ref[...], aim_ref[...], ldt_ref[...], br_ref[...], bi_ref[...])
        _, vjp = jax.vjp(_s5_disc, *prim)
        outs = vjp((c0[...], c1[...], c2[...], c3[...]))
        for o, v in zip((o0, o1, o2, o3, o4), outs):
            o[...] = v
    shp = are.shape
    return pl.pallas_call(body, name="s5_params_bwd", grid=(1,), in_specs=[_full(shp)] * 9, out_specs=[_full(shp)] * 5,
                          out_shape=[_sds(shp)] * 5)(*_in_hbm([are, aim, ldt, br, bi, dabr, dabi, dbr, dbi]))


def _blockdiag(m_t):
    m = m_t.reshape(SSM_BLK, 8, SSM_H, 1, SSM_P)
    eye = jnp.eye(8, dtype=bool).reshape(1, 8, 1, 8, 1)
    return jnp.where(eye, m, jnp.zeros((), m_t.dtype)).reshape(SSM_BLK, 8 * SSM_H, 8 * SSM_P)


def _unblockdiag(pc):
    m = pc.reshape(SSM_BLK, 8, SSM_H, 8, SSM_P)
    return jnp.einsum("jghgp->jghp", m).reshape(SSM_G * SSM_H, SSM_P)


def _in_fwd(x, g_mix, w_in_t, tm):
    S = x.shape[0]

    def body(x_ref, g_ref, w_ref, h_ref, us_ref, uv_ref, gl_ref):
        xv = x_ref[...]
        h = (xv * _rms(xv) * g_ref[...]).astype(MXU)
        h_ref[...] = h
        us_ref[...] = _dot_nt(h, w_ref[0:SSM_W, :])
        uv_ref[...] = _dot_nt(h, w_ref[SSM_W:SSM_W + 2 * SGU_W, :])
        gl_ref[...] = _dot_nt(h, w_ref[SSM_W + 2 * SGU_W:, :])

    row = lambda n: pl.BlockSpec((tm, n), lambda i: (i, 0))
    return pl.pallas_call(
        body, name="in_fwd", grid=(S // tm,),
        in_specs=[row(D_MODEL), _full((1, D_MODEL)), _full(w_in_t.shape)],
        out_specs=[row(D_MODEL), row(SSM_W), row(2 * SGU_W), row(2 * D_MODEL)],
        out_shape=[_sds((S, D_MODEL), MXU), _sds((S, SSM_W)), _sds((S, 2 * SGU_W)), _sds((S, 2 * D_MODEL))],
        compiler_params=_cp("parallel"),
    )(*_in_hbm([x, g_mix, w_in_t]))


def _scan_tables(ar, ai, reverse):
    n = ar.shape[-1]
    def mul(p, q):
        return p[0] * q[0] - p[1] * q[1], p[0] * q[1] + p[1] * q[0]
    a1 = (ar, ai)
    a2 = mul(a1, a1)
    a3 = mul(a2, a1)
    a4 = mul(a2, a2)
    a5 = mul(a4, a1)
    a6 = mul(a4, a2)
    a7 = mul(a4, a3)
    a8 = mul(a4, a4)
    pw = (a1, a2, a3, a4, a5, a6, a7, a8)
    rows = lax.broadcasted_iota(jnp.int32, (8, n), 0)
    tabs = []
    for s, a in ((1, a1), (2, a2), (4, a4)):
        keep = (rows + s <= 7) if reverse else (rows >= s)
        for comp in a:
            tabs.append(jnp.where(keep, jnp.broadcast_to(comp, (8, n)), 0.0))
    for c in range(2):
        q = jnp.zeros((8, n), F32)
        for r in range(8):
            e = (8 - r) if reverse else (r + 1)
            q = jnp.where(rows == r, jnp.broadcast_to(pw[e - 1][c], (8, n)), q)
        tabs.append(q)
    return tabs


def _scan_group(xr, xi, tab_ref, cr, ci, reverse):
    for t, s in enumerate((1, 2, 4)):
        pr = tab_ref[2 * t]
        pi = tab_ref[2 * t + 1]
        sh = (8 - s) if reverse else s
        sr = pltpu.roll(xr, sh, 0)
        si = pltpu.roll(xi, sh, 0)
        xr, xi = xr + pr * sr - pi * si, xi + pr * si + pi * sr
    qr = tab_ref[6]
    qi = tab_ref[7]
    return xr + qr * cr - qi * ci, xi + qr * ci + qi * cr


def _runs_load(src_ref, dst_ref, run):
    for i in range(run):
        dst_ref[8 * i:8 * i + 8, :] = src_ref[pl.ds(i, 8, stride=run), :]


def _runs_store(val, dst_ref, run):
    for i in range(run):
        dst_ref[pl.ds(i, 8, stride=run), :] = val[8 * i:8 * i + 8, :]


def _cpow2(ar, ai, log2n):
    for _ in range(log2n):
        ar, ai = ar * ar - ai * ai, 2.0 * ar * ai
    return ar, ai


def _s5_fwd(us, abar_re, abar_im, b_re, b_im, c_re, c_im, d_skip, tm):
    S = us.shape[0]
    nt = S // tm
    w = 8 * SSM_P
    run = tm // 8
    assert run & (run - 1) == 0

    def body(us_ref, ar_ref, ai_ref, br_ref, bi_ref, cr_ref, ci_ref, d_ref, str_ref, sti_ref, ys_ref,
             tab_ref, car_ref, up_ref):
        i = pl.program_id(1)

        @pl.when(i == 0)
        def _():
            car_ref[...] = jnp.zeros_like(car_ref)
            for k, t in enumerate(_scan_tables(*_cpow2(ar_ref[...], ai_ref[...], run.bit_length() - 1), False)):
                tab_ref[k] = t

        _runs_load(us_ref, up_ref, run)
        ub = up_ref[...].astype(MXU)
        str_ref[...] = _dot(ub, br_ref[0])
        sti_ref[...] = _dot(ub, bi_ref[0])
        ar = jnp.broadcast_to(ar_ref[...], (8, w))
        ai = jnp.broadcast_to(ai_ref[...], (8, w))

        def advance(k, state):
            r0 = pl.multiple_of(k * 8, 8)
            sr, si = state
            return (ar * sr - ai * si + str_ref[pl.ds(r0, 8), :], ar * si + ai * sr + sti_ref[pl.ds(r0, 8), :])

        def emit(k, state):
            r0 = pl.multiple_of(k * 8, 8)
            sr, si = advance(k, state)
            str_ref[pl.ds(r0, 8), :] = sr
            sti_ref[pl.ds(r0, 8), :] = si
            return sr, si

        zero = jnp.zeros((8, w), F32)
        er, ei = lax.fori_loop(0, run, advance, (zero, zero))
        cr, ci = car_ref[0:1, :], car_ref[1:2, :]
        tr, ti = _scan_group(er, ei, tab_ref, cr, ci, False)
        r8 = lax.broadcasted_iota(jnp.int32, (8, w), 0)
        start = (jnp.where(r8 == 0, cr, pltpu.roll(tr, 1, 0)), jnp.where(r8 == 0, ci, pltpu.roll(ti, 1, 0)))
        car_ref[0:1, :] = tr[7:8, :]
        car_ref[1:2, :] = ti[7:8, :]
        lax.fori_loop(0, run, emit, start)
        y = _dot_nt(str_ref[...].astype(MXU), cr_ref[0]) - _dot_nt(sti_ref[...].astype(MXU), ci_ref[0])
        _runs_store(y, ys_ref, run)
        ys_ref[...] += d_ref[...] * us_ref[...]

    blk = lambda: pl.BlockSpec((1, 8 * SSM_H, w), lambda j, i: (j, 0, 0))
    return pl.pallas_call(
        body, name="s5_fwd", grid=(SSM_BLK, nt),
        in_specs=[pl.BlockSpec((tm, LANES), lambda j, i: (i, j)),
                  pl.BlockSpec((1, w), lambda j, i: (0, j)), pl.BlockSpec((1, w), lambda j, i: (0, j)),
                  blk(), blk(), blk(), blk(),
                  pl.BlockSpec((1, LANES), lambda j, i: (0, j))],
        out_specs=[pl.BlockSpec((tm, w), lambda j, i: (i, j)), pl.BlockSpec((tm, w), lambda j, i: (i, j)),
                   pl.BlockSpec((tm, LANES), lambda j, i: (i, j))],
        out_shape=[_sds((S, SSM_BLK * w)), _sds((S, SSM_BLK * w)), _sds((S, SSM_W))],
        scratch_shapes=[pltpu.VMEM((8, 8, w), F32), pltpu.VMEM((8, w), F32), pltpu.VMEM((tm, LANES), F32)],
        compiler_params=_cp("parallel", "arbitrary"),
    )(*_in_hbm([us, abar_re, abar_im, b_re, b_im, c_re, c_im, d_skip]))


def _group_halves(vp):
    first = lax.broadcasted_iota(jnp.int32, vp.shape, 1) < SGU_D
    zero = jnp.zeros((), vp.dtype)
    return jnp.where(first, vp, zero), jnp.where(first, zero, vp)


def _sgu_mix(vnb, wcat_ref):
    outs = []
    for q in range(SGU_G // 2):
        lo, hi = _group_halves(vnb[:, LANES * q:LANES * (q + 1)])
        outs.append(_dot(wcat_ref[q], jnp.concatenate([lo, hi], axis=0)))
    return jnp.concatenate(outs, axis=1)


def _mix_fwd(x, ys, uv, gl, w_glu, b_glu, w_pa, g_sgu, ws, bias_s, w_pb, w_out, g_ffn, tm):
    S = x.shape[0]

    def body(x_ref, ys_ref, uv_ref, gl_ref, wglu_ref, bglu_ref, wpa_ref, gs_ref, ws_ref, bias_ref, wpb_ref, wout_ref,
             gf_ref, yg_ref, yap_ref, sg_ref, ya_ref, yb_ref, m_ref, x1_ref, h2_ref):
        yg = _gelu(ys_ref[...])
        ygb = yg.astype(MXU)
        yg_ref[...] = ygb
        z = _dot(ygb, wglu_ref[...]) + bglu_ref[...]
        yapb = (yg * _sigmoid(z)).astype(MXU)
        yap_ref[...] = yapb
        ya = _dot(yapb, wpa_ref[...])
        ya_ref[...] = ya

        uvg = _gelu(uv_ref[...])
        u2 = uvg[:, :SGU_W]
        v2 = uvg[:, SGU_W:]
        vnb = (v2 * _rms(v2) * gs_ref[...]).astype(MXU)
        for c in range(tm // CHUNK):
            rs = slice(c * CHUNK, (c + 1) * CHUNK)
            mixed = _sgu_mix(vnb[rs], ws_ref) + bias_ref[...]
            sg_ref[rs, :] = (u2[rs] * mixed).astype(MXU)
        yb = _dot(sg_ref[...], wpb_ref[...])
        yb_ref[...] = yb

        glv = gl_ref[...]
        m = _sigmoid(glv[:, :D_MODEL]) * ya + _sigmoid(glv[:, D_MODEL:]) * yb
        mb = m.astype(MXU)
        m_ref[...] = mb
        x1 = x_ref[...] + _dot(mb, wout_ref[...])
        x1_ref[...] = x1
        h2_ref[...] = (x1 * _rms(x1) * gf_ref[...]).astype(MXU)

    row = lambda n: pl.BlockSpec((tm, n), lambda i: (i, 0))
    return pl.pallas_call(
        body, name="mix_fwd", grid=(S // tm,),
        in_specs=[row(D_MODEL), row(SSM_W), row(2 * SGU_W), row(2 * D_MODEL),
                  _full(w_glu.shape), _full(b_glu.shape), _full(w_pa.shape), _full(g_sgu.shape), _full(ws.shape),
                  _full(bias_s.shape), _full(w_pb.shape), _full(w_out.shape), _full(g_ffn.shape)],
        out_specs=[row(SSM_W), row(SSM_W), row(SGU_W), row(D_MODEL), row(D_MODEL), row(D_MODEL), row(D_MODEL),
                   row(D_MODEL)],
        out_shape=[_sds((S, SSM_W), MXU), _sds((S, SSM_W), MXU), _sds((S, SGU_W), MXU), _sds((S, D_MODEL)),
                   _sds((S, D_MODEL)), _sds((S, D_MODEL), MXU), _sds((S, D_MODEL)), _sds((S, D_MODEL), MXU)],
        compiler_params=_cp("parallel"),
    )(*_in_hbm([x, ys, uv, gl, w_glu, b_glu, w_pa, g_sgu, ws, bias_s, w_pb, w_out, g_ffn]))


def _causal_conv3(u, prev8, cw, cb):
    tm = u.shape[0]
    w0, w1, w2 = cw[0:1], cw[1:2], cw[2:3]
    body = w0 * pltpu.roll(u, 2, 0) + w1 * pltpu.roll(u, 1, 0) + w2 * u + cb
    u8 = u[0:8, :]
    r8 = lax.broadcasted_iota(jnp.int32, u8.shape, 0)
    t1 = prev8[7:8, :]
    t0 = prev8[6:7, :]
    s1 = jnp.where(r8 == 0, t1, pltpu.roll(u8, 1, 0))
    s2 = jnp.where(r8 == 0, t0, jnp.where(r8 == 1, t1, pltpu.roll(u8, 2, 0)))
    first = w0 * s2 + w1 * s1 + w2 * u8 + cb
    return jnp.concatenate([first, body[8:tm, :]], axis=0)


def _causal_conv3_adjoint(d, next8, cw):
    tm = d.shape[0]
    w0, w1, w2 = cw[0:1], cw[1:2], cw[2:3]
    n1 = pltpu.roll(d, tm - 1, 0)
    n2 = pltpu.roll(d, tm - 2, 0)
    body = w2 * d + w1 * n1 + w0 * n2
    d8 = d[tm - 8:tm, :]
    r8 = lax.broadcasted_iota(jnp.int32, d8.shape, 0)
    h0 = next8[0:1, :]
    h1 = next8[1:2, :]
    m1 = jnp.where(r8 == 7, h0, pltpu.roll(d8, 7, 0))
    m2 = jnp.where(r8 == 6, h0, jnp.where(r8 == 7, h1, pltpu.roll(d8, 6, 0)))
    last = w2 * d8 + w1 * m1 + w0 * m2
    out = jnp.concatenate([body[0:tm - 8, :], last], axis=0)
    return out, n1, n2, h0 - d[0:1, :], h1 - d[1:2, :]


def _ffn_fwd(h2, x1, tgt, w_up, conv_w, conv_b, w_down, g_final, tm):
    S = h2.shape[0]
    nt = S // tm
    ncb = FF_NCB

    def body(h2_ref, wup_hbm, cwa_ref, cwb_ref, cba_ref, cbb_ref, wd_hbm, x1_ref, gf_ref, tgt_ref,
             up_ref, ab_ref, ff_ref, dx2_ref, dx2b_ref, loss_ref, dgf_ref, acc_ref, tail_ref, wup_ref, wdn_ref, wsem):
        i = pl.program_id(0)
        cb = pl.program_id(1)

        @pl.when(i == 0)
        def _():
            tail_ref[cb] = jnp.zeros((2, 8, FF_CW), F32)

        @pl.when(jnp.logical_and(i == 0, cb == 0))
        def _():
            loss_ref[...] = jnp.zeros_like(loss_ref)
            dgf_ref[...] = jnp.zeros_like(dgf_ref)
            _fetch_once([(wup_hbm, wup_ref), (wd_hbm, wdn_ref)], wsem)

        h2v = h2_ref[...]
        ua = _dot_nt(h2v, wup_ref[cb])
        ub = _dot_nt(h2v, wup_ref[ncb + cb])
        up_ref[0, 0] = ua.astype(MXU)
        up_ref[1, 0] = ub.astype(MXU)
        a = _causal_conv3(ua, tail_ref[cb, 0], cwa_ref[0], cba_ref[0])
        b = _causal_conv3(ub, tail_ref[cb, 1], cwb_ref[0], cbb_ref[0])
        tail_ref[cb, 0] = ua[tm - 8:tm, :]
        tail_ref[cb, 1] = ub[tm - 8:tm, :]
        ab_ref[0, 0] = a
        ab_ref[1, 0] = b
        ffb = (a * _sigmoid(a) * b).astype(MXU)
        ff_ref[0] = ffb
        contrib = _dot(ffb, wdn_ref[pl.ds(pl.multiple_of(cb * FF_CW, FF_CW), FF_CW), :])

        @pl.when(cb == 0)
        def _():
            acc_ref[...] = contrib

        @pl.when(cb > 0)
        def _():
            acc_ref[...] += contrib

        @pl.when(cb == ncb - 1)
        def _():
            x2 = x1_ref[...] + acc_ref[...]
            r = _rms(x2)
            xn = x2 * r
            g = gf_ref[...]
            diff = xn * g - tgt_ref[...]
            loss_ref[...] += (0.5 / D_MODEL) * jnp.sum(diff * diff)
            dy = diff * (1.0 / D_MODEL)
            dgf_ref[...] += _rowsum(dy * xn)
            dx2 = _rms_bwd(dy * g, xn, r)
            dx2_ref[...] = dx2
            dx2b_ref[...] = dx2.astype(MXU)

    row = lambda n: pl.BlockSpec((tm, n), lambda i, c: (i, 0))
    gate = lambda r: pl.BlockSpec((1, r, FF_CW), lambda i, c: (c, 0, 0))
    lin = lambda r: pl.BlockSpec((1, r, FF_CW), lambda i, c: (ncb + c, 0, 0))
    return pl.pallas_call(
        body, name="ffn_fwd", grid=(nt, ncb),
        in_specs=[row(D_MODEL), _ANY, gate(3), lin(3), gate(1), lin(1), _ANY,
                  row(D_MODEL), _full((1, D_MODEL)), row(D_MODEL)],
        out_specs=[pl.BlockSpec((2, 1, tm, FF_CW), lambda i, c: (0, c, i, 0)),
                   pl.BlockSpec((2, 1, tm, FF_CW), lambda i, c: (0, c, i, 0)),
                   pl.BlockSpec((1, tm, FF_CW), lambda i, c: (c, i, 0)),
                   row(D_MODEL), row(D_MODEL), _full((1, LANES)), _full((1, D_MODEL))],
        out_shape=[_sds((2, ncb, S, FF_CW), MXU), _sds((2, ncb, S, FF_CW)), _sds((ncb, S, FF_CW), MXU),
                   _sds((S, D_MODEL)), _sds((S, D_MODEL), MXU), _sds((1, LANES)), _sds((1, D_MODEL))],
        scratch_shapes=[pltpu.VMEM((tm, D_MODEL), F32), pltpu.VMEM((ncb, 2, 8, FF_CW), F32),
                        pltpu.VMEM(w_up.shape, w_up.dtype), pltpu.VMEM(w_down.shape, w_down.dtype),
                        pltpu.SemaphoreType.DMA((2,))],
        compiler_params=pltpu.CompilerParams(dimension_semantics=("arbitrary", "arbitrary"),
                                             vmem_limit_bytes=FFN_VMEM_LIMIT),
    )(*_in_hbm([h2, w_up, conv_w, conv_w, conv_b, conv_b, w_down, x1, g_final, tgt]))


def _ffn_bwd(dx2, up, ab, x1, w_up, conv_w, w_down, g_ffn, tm):
    S = dx2.shape[0]
    nt = S // tm
    ncb = FF_NCB

    def body(dx2_ref, up_ref, ab_ref, cwa_ref, cwb_ref, wd_hbm, wup_hbm,
             x1_ref, g_ref, dup_ref, dx1_ref, dx1b_ref, dconv_ref, dg_ref, acc_ref, head_ref, wup_ref, wdn_ref, wsem):
        i = pl.program_id(0)
        cb = pl.program_id(1)

        @pl.when(i == 0)
        def _():
            head_ref[cb] = jnp.zeros((2, 8, FF_CW), F32)
            dconv_ref[cb] = jnp.zeros((8, FF_CW), F32)
            dconv_ref[ncb + cb] = jnp.zeros((8, FF_CW), F32)

        @pl.when(jnp.logical_and(i == 0, cb == 0))
        def _():
            dg_ref[...] = jnp.zeros_like(dg_ref)
            _fetch_once([(wup_hbm, wup_ref), (wd_hbm, wdn_ref)], wsem)

        dff = _dot_nt(dx2_ref[...].astype(MXU), wdn_ref[pl.ds(pl.multiple_of(cb * FF_CW, FF_CW), FF_CW), :])
        a = ab_ref[0, 0]
        b = ab_ref[1, 0]
        sa = _sigmoid(a)
        silu = a * sa
        da = (dff * b) * (sa + silu * (1.0 - sa))
        db = dff * silu
        dps = []
        for half, slot, d, cw_ref in ((0, cb, da, cwa_ref), (1, ncb + cb, db, cwb_ref)):
            dp, n1, n2, fix0, fix1 = _causal_conv3_adjoint(d, head_ref[cb, half], cw_ref[0])
            head_ref[cb, half] = d[0:8, :]
            dpb16 = dp.astype(MXU)
            dup_ref[half, 0] = dpb16
            dps.append(dpb16)
            u = up_ref[half, 0].astype(F32)
            u_last = u[tm - 1:tm, :]
            dconv_ref[slot, 0:1, :] += _rowsum(n2 * u) + fix0 * u[tm - 2:tm - 1, :] + fix1 * u_last
            dconv_ref[slot, 1:2, :] += _rowsum(n1 * u) + fix0 * u_last
            dconv_ref[slot, 2:3, :] += _rowsum(d * u)
            dconv_ref[slot, 3:4, :] += _rowsum(d)
        contrib = _dot(dps[0], wup_ref[cb]) + _dot(dps[1], wup_ref[ncb + cb])

        @pl.when(cb == 0)
        def _():
            acc_ref[...] = contrib

        @pl.when(cb > 0)
        def _():
            acc_ref[...] += contrib

        @pl.when(cb == ncb - 1)
        def _():
            x1v = x1_ref[...]
            r = _rms(x1v)
            xn = x1v * r
            dh2 = acc_ref[...]
            dg_ref[...] += _rowsum(dh2 * xn)
            dx1 = dx2_ref[...] + _rms_bwd(dh2 * g_ref[...], xn, r)
            dx1_ref[...] = dx1
            dx1b_ref[...] = dx1.astype(MXU)

    row = lambda n: pl.BlockSpec((tm, n), lambda i, c: (nt - 1 - i, 0))
    colb = lambda: pl.BlockSpec((2, 1, tm, FF_CW), lambda i, c: (0, c, nt - 1 - i, 0))
    gate = lambda r: pl.BlockSpec((1, r, FF_CW), lambda i, c: (c, 0, 0))
    lin = lambda r: pl.BlockSpec((1, r, FF_CW), lambda i, c: (ncb + c, 0, 0))
    return pl.pallas_call(
        body, name="ffn_bwd", grid=(nt, ncb),
        in_specs=[row(D_MODEL), colb(), colb(), gate(3), lin(3), _ANY, _ANY, row(D_MODEL), _full((1, D_MODEL))],
        out_specs=[colb(), row(D_MODEL), row(D_MODEL), _full((2 * ncb, 8, FF_CW)), _full((1, D_MODEL))],
        out_shape=[_sds((2, ncb, S, FF_CW), MXU), _sds((S, D_MODEL)), _sds((S, D_MODEL), MXU), _sds((2 * ncb, 8, FF_CW)),
                   _sds((1, D_MODEL))],
        scratch_shapes=[pltpu.VMEM((tm, D_MODEL), F32), pltpu.VMEM((ncb, 2, 8, FF_CW), F32),
                        pltpu.VMEM(w_up.shape, w_up.dtype), pltpu.VMEM(w_down.shape, w_down.dtype),
                        pltpu.SemaphoreType.DMA((2,))],
        compiler_params=pltpu.CompilerParams(dimension_semantics=("arbitrary", "arbitrary"),
                                             vmem_limit_bytes=FFN_VMEM_LIMIT),
    )(*_in_hbm([dx2, up, ab, conv_w, conv_w, w_down, w_up, x1, g_ffn]))


def _mix_bwd(dx1, gl, ya, yb, ys, uv, w_out, w_pa, w_pb, w_glu, b_glu, g_sgu, ws, ws_t, bias_s, tm):
    S = dx1.shape[0]

    def body(dx1_ref, gl_ref, ya_ref, yb_ref, ys_ref, uv_ref, wout_ref, wpa_ref, wpb_ref, wglu_ref, bglu_ref, gs_ref,
             ws_ref, wst_ref, bias_ref,
             dgl_ref, dya_ref, dyb_ref, dz_ref, dys_ref, duv_ref, dbglu_ref, dgs_ref, dws_ref, dbs_ref,
             du2_ref, dvn_ref):
        i = pl.program_id(0)

        @pl.when(i == 0)
        def _():
            dbglu_ref[...] = jnp.zeros_like(dbglu_ref)
            dgs_ref[...] = jnp.zeros_like(dgs_ref)
            dws_ref[...] = jnp.zeros_like(dws_ref)
            dbs_ref[...] = jnp.zeros_like(dbs_ref)

        dm = _dot_nt(dx1_ref[...].astype(MXU), wout_ref[...])
        glv = gl_ref[...]
        ga = _sigmoid(glv[:, :D_MODEL])
        gb = _sigmoid(glv[:, D_MODEL:])
        dgl_ref[:, :D_MODEL] = (dm * ya_ref[...] * ga * (1.0 - ga)).astype(MXU)
        dgl_ref[:, D_MODEL:] = (dm * yb_ref[...] * gb * (1.0 - gb)).astype(MXU)
        dyab = (dm * ga).astype(MXU)
        dybb = (dm * gb).astype(MXU)
        dya_ref[...] = dyab
        dyb_ref[...] = dybb

        dyap = _dot_nt(dyab, wpa_ref[...])
        yg, dgelu = _gelu_and_grad(ys_ref[...])
        sz = _sigmoid(_dot(yg.astype(MXU), wglu_ref[...]) + bglu_ref[...])
        dz = dyap * yg * sz * (1.0 - sz)
        dzb = dz.astype(MXU)
        dz_ref[...] = dzb
        dbglu_ref[...] += _rowsum(dz)
        dys_ref[...] = (dyap * sz + _dot_nt(dzb, wglu_ref[...])) * dgelu

        dsg = _dot_nt(dybb, wpb_ref[...])
        uvg, duvg = _gelu_and_grad(uv_ref[...])
        u2 = uvg[:, :SGU_W]
        v2 = uvg[:, SGU_W:]
        rv = _rms(v2)
        vhat = v2 * rv
        gs = gs_ref[...]
        vnb = (vhat * gs).astype(MXU)
        tril = (lax.broadcasted_iota(jnp.int32, (CHUNK, CHUNK), 0)
                >= lax.broadcasted_iota(jnp.int32, (CHUNK, CHUNK), 1))
        for c in range(tm // CHUNK):
            rs = slice(c * CHUNK, (c + 1) * CHUNK)
            vc = vnb[rs]
            mixed = _sgu_mix(vc, ws_ref) + bias_ref[...]
            dsg_c = dsg[rs]
            du2_ref[rs, :] = dsg_c * mixed
            dmx = dsg_c * u2[rs]
            dbs_ref[...] += dmx
            dmb = dmx.astype(MXU)
            dvn_ref[rs, :] = _sgu_mix(dmb, wst_ref)
            for q in range(SGU_G // 2):
                lanes = slice(LANES * q, LANES * (q + 1))
                for j, part in enumerate(_group_halves(dmb[:, lanes])):
                    dws_ref[2 * q + j] += jnp.where(tril, _dot_nt(part, vc[:, lanes]), 0.0)
        dvn = dvn_ref[...]
        dgs_ref[...] += _rowsum(dvn * vhat)
        dv2 = _rms_bwd(dvn * gs, vhat, rv)
        duv_ref[:, :SGU_W] = (du2_ref[...] * duvg[:, :SGU_W]).astype(MXU)
        duv_ref[:, SGU_W:] = (dv2 * duvg[:, SGU_W:]).astype(MXU)

    row = lambda n: pl.BlockSpec((tm, n), lambda i: (i, 0))
    return pl.pallas_call(
        body, name="mix_bwd", grid=(S // tm,),
        in_specs=[row(D_MODEL), row(2 * D_MODEL), row(D_MODEL), row(D_MODEL), row(SSM_W), row(2 * SGU_W),
                  _full(w_out.shape), _full(w_pa.shape), _full(w_pb.shape), _full(w_glu.shape), _full(b_glu.shape),
                  _full(g_sgu.shape), _full(ws.shape), _full(ws_t.shape), _full(bias_s.shape)],
        out_specs=[row(2 * D_MODEL), row(D_MODEL), row(D_MODEL), row(SSM_W), row(SSM_W), row(2 * SGU_W),
                   _full((1, SSM_W)), _full((1, SGU_W)), _full((SGU_G, CHUNK, CHUNK)), _full((CHUNK, SGU_W))],
        out_shape=[_sds((S, 2 * D_MODEL), MXU), _sds((S, D_MODEL), MXU), _sds((S, D_MODEL), MXU), _sds((S, SSM_W), MXU),
                   _sds((S, SSM_W)), _sds((S, 2 * SGU_W), MXU),
                   _sds((1, SSM_W)), _sds((1, SGU_W)), _sds((SGU_G, CHUNK, CHUNK)), _sds((CHUNK, SGU_W))],
        scratch_shapes=[pltpu.VMEM((tm, SGU_W), F32), pltpu.VMEM((tm, SGU_W), F32)],
        compiler_params=_cp("arbitrary"),
    )(*_in_hbm([dx1, gl, ya, yb, ys, uv, w_out, w_pa, w_pb, w_glu, b_glu, g_sgu, ws, ws_t, bias_s]))


def _s5_bwd(dys, us, st_re, st_im, abar_re, abar_im, b_re, b_im, c_re, c_im, d_skip, tm):
    S = us.shape[0]
    nt = S // tm
    w = 8 * SSM_P
    hb = tm // 8
    run = tm // 8
    assert run & (run - 1) == 0

    def body(dys_ref, us_ref, str_ref, sti_ref, hr_ref, hi_ref, ar_ref, ai_ref, br_ref, bi_ref, cr_ref, ci_ref, d_ref,
             dus_ref, dab_ref, dd_ref, dbr_ref, dbi_ref, dcr_ref, dci_ref,
             tab_ref, car_ref, gr_ref, gi_ref, dyp_ref, up_ref, dun_ref):
        i = pl.program_id(1)
        ri = nt - 1 - i

        @pl.when(i == 0)
        def _():
            car_ref[...] = jnp.zeros_like(car_ref)
            for k, t in enumerate(_scan_tables(*_cpow2(ar_ref[...], -ai_ref[...], run.bit_length() - 1), True)):
                tab_ref[k] = t
            for r in (dab_ref, dd_ref, dbr_ref, dbi_ref, dcr_ref, dci_ref):
                r[...] = jnp.zeros_like(r)

        _runs_load(dys_ref, dyp_ref, run)
        _runs_load(us_ref, up_ref, run)
        dyb = dyp_ref[...].astype(MXU)
        gr_ref[...] = _dot(dyb, cr_ref[0])
        gi_ref[...] = -_dot(dyb, ci_ref[0])
        ar = jnp.broadcast_to(ar_ref[...], (8, w))
        ai = jnp.broadcast_to(-ai_ref[...], (8, w))

        def advance(kk, state):
            r0 = pl.multiple_of((run - 1 - kk) * 8, 8)
            gr, gi = state
            return (ar * gr - ai * gi + gr_ref[pl.ds(r0, 8), :], ar * gi + ai * gr + gi_ref[pl.ds(r0, 8), :])

        def emit(kk, state):
            r0 = pl.multiple_of((run - 1 - kk) * 8, 8)
            gr, gi = advance(kk, state)
            gr_ref[pl.ds(r0, 8), :] = gr
            gi_ref[pl.ds(r0, 8), :] = gi
            return gr, gi

        zero = jnp.zeros((8, w), F32)
        er, ei = lax.fori_loop(0, run, advance, (zero, zero))
        cr, ci = car_ref[0:1, :], car_ref[1:2, :]
        tr, ti = _scan_group(er, ei, tab_ref, cr, ci, True)
        r8 = lax.broadcasted_iota(jnp.int32, (8, w), 0)
        start = (jnp.where(r8 == 7, cr, pltpu.roll(tr, 7, 0)), jnp.where(r8 == 7, ci, pltpu.roll(ti, 7, 0)))
        car_ref[0:1, :] = tr[0:1, :]
        car_ref[1:2, :] = ti[0:1, :]
        lax.fori_loop(0, run, emit, start)

        gsr = gr_ref[...]
        gsi = gi_ref[...]
        sr = str_ref[...]
        si = sti_ref[...]
        first = ri == 0

        def previous(s, halo_ref):
            head = jnp.where(r8 == 0, jnp.where(first, 0.0, halo_ref[7:8, :]), pltpu.roll(s[tm - 8:tm, :], 1, 0))
            return jnp.concatenate([head, s[0:tm - 8, :]], axis=0)

        spr = previous(sr, hr_ref)
        spi = previous(si, hi_ref)
        dab_ref[0, 0:1, :] += _rowsum(gsr * spr + gsi * spi)
        dab_ref[0, 1:2, :] += _rowsum(gsi * spr - gsr * spi)

        gbr = gsr.astype(MXU)
        gbi = gsi.astype(MXU)
        _runs_store(_dot_nt(gbr, br_ref[0]) + _dot_nt(gbi, bi_ref[0]), dun_ref, run)
        dys_v = dys_ref[...]
        dus_ref[...] = (dun_ref[...] + d_ref[...] * dys_v).astype(MXU)
        dd_ref[0, 0:1, :] += _rowsum(dys_v * us_ref[...])
        ub = up_ref[...].astype(MXU)
        dbr_ref[0] += _dot_tn(ub, gbr)
        dbi_ref[0] += _dot_tn(ub, gbi)
        dcr_ref[0] += _dot_tn(dyb, sr.astype(MXU))
        dci_ref[0] -= _dot_tn(dyb, si.astype(MXU))

    blk = lambda: pl.BlockSpec((1, 8 * SSM_H, w), lambda j, i: (j, 0, 0))
    rowl = lambda: pl.BlockSpec((tm, LANES), lambda j, i: (nt - 1 - i, j))
    roww = lambda: pl.BlockSpec((tm, w), lambda j, i: (nt - 1 - i, j))
    halo = lambda: pl.BlockSpec((8, w), lambda j, i: (jnp.maximum((nt - 1 - i) * hb - 1, 0), j))
    return pl.pallas_call(
        body, name="s5_bwd", grid=(SSM_BLK, nt),
        in_specs=[rowl(), rowl(), roww(), roww(), halo(), halo(),
                  pl.BlockSpec((1, w), lambda j, i: (0, j)), pl.BlockSpec((1, w), lambda j, i: (0, j)),
                  blk(), blk(), blk(), blk(),
                  pl.BlockSpec((1, LANES), lambda j, i: (0, j))],
        out_specs=[rowl(),
                   pl.BlockSpec((1, 8, w), lambda j, i: (j, 0, 0)), pl.BlockSpec((1, 8, LANES), lambda j, i: (j, 0, 0)),
                   blk(), blk(), blk(), blk()],
        out_shape=[_sds((S, SSM_W), MXU), _sds((SSM_BLK, 8, w)), _sds((SSM_BLK, 8, LANES)),
                   _sds((SSM_BLK, 8 * SSM_H, w)), _sds((SSM_BLK, 8 * SSM_H, w)),
                   _sds((SSM_BLK, 8 * SSM_H, w)), _sds((SSM_BLK, 8 * SSM_H, w))],
        scratch_shapes=[pltpu.VMEM((8, 8, w), F32), pltpu.VMEM((8, w), F32),
                        pltpu.VMEM((tm, w), F32), pltpu.VMEM((tm, w), F32),
                        pltpu.VMEM((tm, LANES), F32), pltpu.VMEM((tm, LANES), F32), pltpu.VMEM((tm, LANES), F32)],
        compiler_params=_cp("parallel", "arbitrary"),
    )(*_in_hbm([dys, us, st_re, st_im, st_re, st_im, abar_re, abar_im, b_re, b_im, c_re, c_im, d_skip]))


def _in_bwd(dus, duv, dgl, dx1, x, g_mix, w_in, tm):
    S = x.shape[0]

    def body(dus_ref, duv_ref, dgl_ref, dx1_ref, x_ref, g_ref, w_ref, gx_ref, dg_ref):
        @pl.when(pl.program_id(0) == 0)
        def _():
            dg_ref[...] = jnp.zeros_like(dg_ref)

        dh = (_dot(dus_ref[...], w_ref[0:SSM_W, :])
              + _dot(duv_ref[...], w_ref[SSM_W:SSM_W + 2 * SGU_W, :])
              + _dot(dgl_ref[...], w_ref[SSM_W + 2 * SGU_W:, :]))
        xv = x_ref[...]
        r = _rms(xv)
        xn = xv * r
        dg_ref[...] += _rowsum(dh * xn)
        gx_ref[...] = dx1_ref[...] + _rms_bwd(dh * g_ref[...], xn, r)

    row = lambda n: pl.BlockSpec((tm, n), lambda i: (i, 0))
    return pl.pallas_call(
        body, name="in_bwd", grid=(S // tm,),
        in_specs=[row(SSM_W), row(2 * SGU_W), row(2 * D_MODEL), row(D_MODEL), row(D_MODEL), _full((1, D_MODEL)),
                  _full(w_in.shape)],
        out_specs=[row(D_MODEL), _full((1, D_MODEL))],
        out_shape=[_sds((S, D_MODEL)), _sds((1, D_MODEL))],
        compiler_params=_cp("arbitrary"),
    )(*_in_hbm([dus, duv, dgl, dx1, x, g_mix, w_in]))


def _pick(n, cands):
    for c in cands:
        if n % c == 0:
            return c
    return n


def _wgrad_split(a, b, nsplit, tk, name):
    S, K = a.shape
    N = b.shape[1]
    c = N // nsplit

    def body(a_ref, b_ref, o_ref):
        prod = _dot_tn(a_ref[...], b_ref[...])
        for d in range(nsplit):
            o_ref[d] = prod[:, c * d:c * (d + 1)].astype(MXU)

    return pl.pallas_call(
        body, name=name, grid=(K // tk,),
        in_specs=[pl.BlockSpec((S, tk), lambda k: (0, k)), _full((S, N))],
        out_specs=pl.BlockSpec((nsplit, tk, c), lambda k: (0, k, 0)),
        out_shape=_sds((nsplit, K, c), MXU),
        compiler_params=_cp("parallel"),
    )(*_in_hbm([a, b]))


def _wgrad_in_t(dps, h1, name, after=()):
    S, K = h1.shape
    cw = 512
    counts = [b.shape[1] // cw for b in dps]
    starts = [sum(counts[:i]) for i in range(len(dps))]
    nblk = sum(counts)

    def body(*refs):
        b_refs = refs[:len(dps)]
        h_ref, o_ref = refs[len(dps)], refs[-1]
        j = pl.program_id(0)
        for b_ref, st, cnt in zip(b_refs, starts, counts):
            @pl.when(jnp.logical_and(j >= st, j < st + cnt))
            def _():
                o_ref[...] = _dot_tn(b_ref[...], h_ref[...]).astype(MXU)

    def src_spec(st, cnt):
        return pl.BlockSpec((S, cw), lambda j: (0, jnp.clip(j - st, 0, cnt - 1)))

    return pl.pallas_call(
        body, name=name, grid=(nblk,),
        in_specs=[src_spec(st, cnt) for st, cnt in zip(starts, counts)] + [_full((S, K))] + [_ANY] * len(after),
        out_specs=pl.BlockSpec((cw, K), lambda j: (j, 0)),
        out_shape=_sds((nblk * cw, K), MXU),
        compiler_params=_cp("arbitrary"),
    )(*_in_hbm([*dps, h1]), *after)


def _wgrad_blk(a3, b3, nblk, a_of, b_of, name):
    S, K = a3.shape[1:]
    N = b3.shape[2]

    def body(a_ref, b_ref, o_ref):
        o_ref[0] = _dot_tn(a_ref[0], b_ref[0]).astype(MXU)

    return pl.pallas_call(
        body, name=name, grid=(nblk,),
        in_specs=[pl.BlockSpec((1, S, K), lambda b: (a_of(b), 0, 0)),
                  pl.BlockSpec((1, S, N), lambda b: (b_of(b), 0, 0))],
        out_specs=pl.BlockSpec((1, K, N), lambda b: (b, 0, 0)),
        out_shape=_sds((nblk, K, N), MXU),
        compiler_params=pltpu.CompilerParams(dimension_semantics=("parallel",), vmem_limit_bytes=WGRAD_VMEM_LIMIT),
    )(*_in_hbm([a3, b3]))


def _assemble_cols(blocks_list, name):
    def body(*refs):
        n = len(blocks_list)
        for b_ref, o_ref in zip(refs[:n], refs[n:]):
            c = b_ref.shape[2]
            for d in range(N_DEV):
                o_ref[:, c * d:c * (d + 1)] = b_ref[d]

    outs = [_sds((b.shape[1], N_DEV * b.shape[2]), b.dtype) for b in blocks_list]
    return pl.pallas_call(
        body, name=name, grid=(1,), in_specs=[_full(b.shape) for b in blocks_list],
        out_specs=[_full(o.shape) for o in outs], out_shape=outs, compiler_params=_cp("arbitrary"),
    )(*_in_hbm(blocks_list))


def _tile(S, want):
    return want if S % want == 0 else S


def _local_step(x, tgt, p, mixer_relay, mixer_weights, ffn_weights, grads_out, small_out):
    S = x.shape[0]
    tm = _tile(S, 256)
    tl = _tile(S, 512)

    rep = lambda a: jnp.repeat(a, SSM_H, axis=0)
    are = rep(p["a_re"])
    aim = rep(p["a_im"])
    ldt = jnp.broadcast_to(rep(p["log_dt"].reshape(SSM_G, 1)), are.shape)
    br_t = p["b_re_t"].reshape(are.shape)
    bi_t = p["b_im_t"].reshape(are.shape)
    abr, abi, bbr, bbi = _s5_params_fwd(are, aim, ldt, br_t, bi_t)
    head = lambda a: a.reshape(SSM_G, SSM_H, SSM_P)[:, 0, :].reshape(1, SSM_G * SSM_P)
    abar_re, abar_im = head(abr), head(abi)
    bd_br = _blockdiag(bbr).astype(MXU)
    bd_bi = _blockdiag(bbi).astype(MXU)
    bd_cr = _blockdiag(p["c_re"].reshape(are.shape)).astype(MXU)
    bd_ci = _blockdiag(p["c_im"].reshape(are.shape)).astype(MXU)
    d_skip = p["d_skip"].reshape(1, SSM_W)

    tril = jnp.tril(jnp.ones((CHUNK, CHUNK), dtype=bool))
    ws = jnp.where(tril[None], p["w_s"], 0.0)
    pair = lambda w: w.reshape(SGU_G // 2, 2, CHUNK, CHUNK).transpose(0, 2, 1, 3).reshape(SGU_G // 2, CHUNK, 2 * CHUNK)
    ws_b = pair(ws).astype(MXU)
    ws_t = pair(ws.transpose(0, 2, 1)).astype(MXU)
    bias_s = jnp.repeat(p["b_s"].T, SGU_D, axis=1)

    g_mix = p["g_mix"].reshape(1, D_MODEL)
    g_ffn = p["g_ffn"].reshape(1, D_MODEL)
    g_final = p["g_final"].reshape(1, D_MODEL)
    g_sgu = p["g_sgu"].reshape(1, SGU_W)
    b_glu = p["b_glu"].reshape(1, SSM_W)
    conv_b = p["conv_b"].reshape(2 * FF_NCB, 1, FF_CW)
    tf = _tile(S, 256)

    h1, us, uv, gl = _in_fwd(x, g_mix, p["w_in_t"], tl)
    token = mixer_relay(us)
    st_re, st_im, ys = _s5_fwd(us, abar_re, abar_im, bd_br, bd_bi, bd_cr, bd_ci, d_skip + token[0:1, 0:1], tl)
    p = dict(p, **mixer_weights(ys))
    yg, yap, sg, ya, yb, m, x1, h2 = _mix_fwd(x, ys, uv, gl, p["w_glu"], b_glu, p["w_proj_a"], g_sgu, ws_b, bias_s,
                                              p["w_proj_b"], p["w_out"], g_ffn, tm)
    w_up, conv_w, w_down = ffn_weights(h2)
    pair_lanes = lambda a: a.reshape(N_DEV // 2, 2, a.shape[1], FF_SHARD).transpose(0, 2, 1, 3).reshape(
        N_DEV // 2, a.shape[1], FF_CW)
    w_up = w_up.reshape(2 * FF_NCB, FF_CW, D_MODEL)
    conv_w = pair_lanes(conv_w)
    up, ab, ff, dx2, dx2b, loss, dg_final = _ffn_fwd(h2, x1, tgt, w_up, conv_w, conv_b, w_down, g_final, tf)

    dup, dx1, dx1b, dconv, dg_ffn = _ffn_bwd(dx2, up, ab, x1, w_up, conv_w, w_down, g_ffn, tf)
    rows8 = lambda g: g.reshape(N_DEV, g.shape[1] // N_DEV, g.shape[2])
    g_up = _wgrad_blk(dup.reshape(2 * FF_NCB, S, FF_CW), h2[None], 2 * FF_NCB, lambda b: b, lambda b: 0,
                      "wgrad_up").reshape(N_DEV, FF_SHARD, D_MODEL)
    g_down = _wgrad_blk(ff, dx2b[None], FF_NCB, lambda b: b, lambda b: 0, "wgrad_down").reshape(
        N_DEV, D_FF // N_DEV, D_MODEL)
    token = grads_out(("w_up", "w_down"), (g_up, g_down))
    dgl, dya, dyb, dz, dys, duv, db_glu, dg_sgu, dws, dbs = _mix_bwd(
        dx1, gl, ya, yb, ys, uv, p["w_out"], p["w_proj_a"], p["w_proj_b"], p["w_glu"], b_glu + token[0:1, 0:1], g_sgu,
        ws_b, ws_t, bias_s, tm)
    token = grads_out(("w_glu", "w_proj_a", "w_proj_b", "w_out"),
                      (rows8(_wgrad_split(yg, dz, 1, SSM_W, "wgrad_glu")),
                       _wgrad_split(yap, dya, N_DEV, SSM_W, "wgrad_pa"),
                       _wgrad_split(sg, dyb, N_DEV, SGU_W, "wgrad_pb"),
                       rows8(_wgrad_split(m, dx1b, 1, 512, "wgrad_out"))))
    dus, dab, dd, dbbr, dbbi, dcr, dci = _s5_bwd(dys, us, st_re, st_im, abar_re, abar_im, bd_br, bd_bi, bd_cr, bd_ci,
                                                 d_skip + token[0:1, 0:1], tl)
    grad_x, dg_mix = _in_bwd(dus, duv, dgl, dx1, x, g_mix + token[0:1, 0:1], p["w_in_t"], tl)

    spread = lambda v: jnp.repeat(v.reshape(SSM_G, SSM_P), SSM_H, axis=0) * (1.0 / SSM_H)
    dabr = spread(dab[:, 0, :])
    dabi = spread(dab[:, 1, :])
    dare, daim, dldt, dbr_t, dbi_t = _s5_params_bwd(are, aim, ldt, br_t, bi_t, dabr, dabi,
                                                    _unblockdiag(dbbr), _unblockdiag(dbbi))
    fold = lambda a: a.reshape(SSM_G, SSM_H, SSM_P).sum(axis=1)

    grads = {
        "g_mix": dg_mix,
        "a_re": fold(dare), "a_im": fold(daim), "log_dt": fold(dldt).sum(axis=1),
        "b_re": dbr_t, "b_im": dbi_t,
        "c_re": _unblockdiag(dcr).reshape(SSM_G, SSM_H, SSM_P),
        "c_im": _unblockdiag(dci).reshape(SSM_G, SSM_H, SSM_P),
        "d_skip": dd[:, 0, :].reshape(SSM_W),
        "b_glu": db_glu,
        "g_sgu": dg_sgu,
        "w_s": dws,
        "b_s": dbs.reshape(CHUNK, SGU_G, SGU_D).sum(axis=-1).T,
        "g_ffn": dg_ffn,
        "conv_w": dconv[:, 0:3, :].reshape(N_DEV // 2, 3, 2, FF_SHARD).transpose(0, 2, 1, 3).reshape(
            N_DEV, 3, FF_SHARD),
        "conv_b": dconv[:, 3, :].reshape(2 * D_FF),
        "g_final": dg_final,
    }
    token = small_out(grads, loss)
    g_in = _wgrad_in_t([dus, duv, dgl], h1, "wgrad_in", after=[token])
    grads_out(("w_in",), (g_in.reshape(N_DEV, g_in.shape[0] // N_DEV, D_MODEL),), after=[token])
    return grad_x


_ANY = pl.BlockSpec(memory_space=pl.ANY)
_MESH = pl.DeviceIdType.MESH


def _allgather(shards, dtypes, name, cast_only=()):
    n = len(shards)
    e = len(cast_only)

    def body(*refs):
        in_refs, extra_in = refs[:n], refs[n:n + e]
        out_refs, extra_out = refs[n + e:2 * n + e], refs[2 * n + e:2 * n + 2 * e]
        stage = refs[2 * n + 2 * e:3 * n + 2 * e]
        send_sems, recv_sems, local_sems = refs[3 * n + 2 * e:]
        for a in range(n):
            stage[a][...] = in_refs[a][...].astype(dtypes[a])
        for i in range(e):
            extra_out[i][...] = extra_in[i][...].astype(MXU)
        x, y, c = lax.axis_index("x"), lax.axis_index("y"), lax.axis_index("c")
        me, sibling = (x, y, c), (x, y, 1 - c)
        chips = [(1 - x, y), (x, 1 - y), (1 - x, 1 - y)]

        def slot(a, px, py, pc):
            return out_refs[a].at[4 * px + 2 * py + pc]

        def copy(a, k, block, to, src=None):
            return pltpu.make_async_remote_copy(
                src_ref=slot(a, *block) if src is None else src, dst_ref=slot(a, *block),
                send_sem=send_sems.at[a, k], recv_sem=recv_sems.at[a, k], device_id=to, device_id_type=_MESH)

        mine = [pltpu.make_async_copy(stage[a], slot(a, *me), local_sems.at[a]) for a in range(n)]
        for cp in mine:
            cp.start()
        first = []
        for j, chip in enumerate(chips):
            first += [copy(a, 1 + j, me, (*chip, c), src=stage[a]) for a in range(n)]
        first += [copy(a, 0, me, sibling, src=stage[a]) for a in range(n)]
        for cp in first:
            cp.start()
        passed = []
        for j, chip in enumerate(chips):
            for a in range(n):
                copy(a, 1 + j, (*chip, c), me).wait_recv()
                fwd = copy(a, 4 + j, (*chip, c), sibling)
                fwd.start()
                passed.append(fwd)
        for a in range(n):
            copy(a, 0, sibling, me).wait_recv()
        for j, chip in enumerate(chips):
            for a in range(n):
                copy(a, 4 + j, (*chip, 1 - c), me).wait_recv()
        for cp in first + passed:
            cp.wait_send()
        for cp in mine:
            cp.wait()

    res = pl.pallas_call(
        body, name=name, grid=(1,), in_specs=[_full(s.shape) for s in list(shards) + list(cast_only)],
        out_specs=[_ANY] * n + [_full(s.shape) for s in cast_only],
        out_shape=[_sds((N_DEV,) + s.shape, dt) for s, dt in zip(shards, dtypes)]
                  + [_sds(s.shape, MXU) for s in cast_only],
        scratch_shapes=[pltpu.VMEM(s.shape, dt) for s, dt in zip(shards, dtypes)]
                       + [pltpu.SemaphoreType.DMA((n, 7)), pltpu.SemaphoreType.DMA((n, 7)), pltpu.SemaphoreType.DMA((n,))],
        compiler_params=pltpu.CompilerParams(vmem_limit_bytes=VMEM_LIMIT),
    )(*_in_hbm([*shards, *cast_only]))
    return res[:n], res[n:]


def _all_to_all(sends, name):
    n = len(sends)

    def body(*refs):
        send_refs, recv_refs = refs[:n], refs[n:2 * n]
        send_sems, recv_sems, local_sems = refs[2 * n:]
        x, y, c = lax.axis_index("x"), lax.axis_index("y"), lax.axis_index("c")
        me = 4 * x + 2 * y + c
        mine = [pltpu.make_async_copy(send_refs[a].at[me], recv_refs[a].at[me], local_sems.at[a]) for a in range(n)]
        for cp in mine:
            cp.start()
        copies = []
        for k in (2, 4, 6, 3, 5, 7, 1):
            px = 1 - x if k & 4 else x
            py = 1 - y if k & 2 else y
            pc = 1 - c if k & 1 else c
            peer = 4 * px + 2 * py + pc
            for a in range(n):
                sems = dict(send_sem=send_sems.at[a, k - 1], recv_sem=recv_sems.at[a, k - 1],
                            device_id=(px, py, pc), device_id_type=_MESH)
                cp = pltpu.make_async_remote_copy(src_ref=send_refs[a].at[peer], dst_ref=recv_refs[a].at[me], **sems)
                cp.start()
                landing = pltpu.make_async_remote_copy(src_ref=send_refs[a].at[peer], dst_ref=recv_refs[a].at[peer],
                                                       **sems)
                copies.append((cp, landing))
        for _, landing in copies:
            landing.wait_recv()
        for cp, _ in copies:
            cp.wait_send()
        for cp in mine:
            cp.wait()

    return pl.pallas_call(
        body, name=name, in_specs=[_ANY] * n, out_specs=[_ANY] * n,
        out_shape=[_sds(s.shape, s.dtype) for s in sends],
        scratch_shapes=[pltpu.SemaphoreType.DMA((n, 7)), pltpu.SemaphoreType.DMA((n, 7)), pltpu.SemaphoreType.DMA((n,))],
    )(*sends)


_HBM = pl.BlockSpec(memory_space=pltpu.HBM)
_SEM = pl.BlockSpec(memory_space=pltpu.SEMAPHORE)
_EFFECT = pltpu.SideEffectType.DATAFLOW_SIDE_EFFECTING
_PEER_ORDER = (2, 4, 6, 3, 5, 7, 1)


def _peer(k):
    x, y, c = lax.axis_index("x"), lax.axis_index("y"), lax.axis_index("c")
    px = 1 - x if k & 4 else x
    py = 1 - y if k & 2 else y
    pc = 1 - c if k & 1 else c
    return (px, py, pc), 4 * px + 2 * py + pc


_SAME_CORE_AND_SIBLING = (2, 4, 6, 1)


def _push_start(srcs, lands, slotted, name, peers=_PEER_ORDER, after=()):
    n = len(srcs)
    e = len(after)

    def body(*refs):
        src_refs, land_refs = refs[:n], refs[n:2 * n]
        send_sems, recv_sems, token = refs[2 * n + e], refs[2 * n + e + 1], refs[-1]
        me = 4 * lax.axis_index("x") + 2 * lax.axis_index("y") + lax.axis_index("c")
        for k in peers:
            dev, peer = _peer(k)
            for a in range(n):
                pltpu.make_async_remote_copy(
                    src_ref=src_refs[a].at[peer] if slotted else src_refs[a], dst_ref=land_refs[a].at[me],
                    send_sem=send_sems.at[7 * a + k - 1], recv_sem=recv_sems.at[7 * a + k - 1],
                    device_id=dev, device_id_type=_MESH).start()
        token[...] = jnp.zeros_like(token)

    bufs = list(srcs) + list(lands)
    res = pl.pallas_call(
        body, name=name, in_specs=[_HBM] * (2 * n) + [_ANY] * e,
        out_specs=(_SEM, _SEM, *[_HBM] * (2 * n), pl.BlockSpec(memory_space=pltpu.VMEM)),
        out_shape=(pltpu.SemaphoreType.DMA((7 * n,)), pltpu.SemaphoreType.DMA((7 * n,)),
                   *[pltpu.HBM(b.shape, b.dtype) for b in bufs], _sds((8, LANES))),
        input_output_aliases={i: 2 + i for i in range(2 * n)},
        compiler_params=pltpu.CompilerParams(has_side_effects=_EFFECT),
    )(*[pltpu.with_memory_space_constraint(b, pltpu.HBM) for b in bufs], *after)
    return res[0], res[1], res[2:2 + n], res[2 + n:2 + 2 * n], res[-1]


def _push_wait(send_sems, recv_sems, srcs, lands, slotted, after, name, peers=_PEER_ORDER):
    n = len(srcs)

    def body(*refs):
        src_refs, land_refs = refs[:n], refs[n:2 * n]
        send_sems, recv_sems = refs[2 * n], refs[2 * n + 1]
        for k in peers:
            dev, peer = _peer(k)
            for a in range(n):
                cp = pltpu.make_async_remote_copy(
                    src_ref=src_refs[a].at[peer] if slotted else src_refs[a], dst_ref=land_refs[a].at[peer],
                    send_sem=send_sems.at[7 * a + k - 1], recv_sem=recv_sems.at[7 * a + k - 1],
                    device_id=dev, device_id_type=_MESH)
                cp.wait_send()
                cp.wait_recv()

    bufs = list(srcs) + list(lands)
    res = pl.pallas_call(
        body, name=name, in_specs=[_HBM] * (2 * n) + [_SEM, _SEM] + [_ANY] * len(after), out_specs=[_HBM] * (2 * n),
        out_shape=[pltpu.HBM(b.shape, b.dtype) for b in bufs],
        input_output_aliases={i: i for i in range(2 * n)},
        compiler_params=pltpu.CompilerParams(has_side_effects=_EFFECT),
    )(*bufs, send_sems, recv_sems, *after)
    return res[n:]


def _other_chips():
    x, y = lax.axis_index("x"), lax.axis_index("y")
    return ((1 - x, y), (x, 1 - y), (1 - x, 1 - y))


def _relay_start(lands, name):
    n = len(lands)

    def body(*refs):
        land_refs = refs[:n]
        send_sems, recv_sems, token = refs[n], refs[n + 1], refs[-1]
        x, y, c = lax.axis_index("x"), lax.axis_index("y"), lax.axis_index("c")
        for j, (px, py) in enumerate(_other_chips()):
            slot = 4 * px + 2 * py + c
            for a in range(n):
                pltpu.make_async_remote_copy(
                    src_ref=land_refs[a].at[slot], dst_ref=land_refs[a].at[slot],
                    send_sem=send_sems.at[3 * a + j], recv_sem=recv_sems.at[3 * a + j],
                    device_id=(x, y, 1 - c), device_id_type=_MESH).start()
        token[...] = jnp.zeros_like(token)

    res = pl.pallas_call(
        body, name=name, in_specs=[_HBM] * n,
        out_specs=(_SEM, _SEM, *[_HBM] * n, pl.BlockSpec(memory_space=pltpu.VMEM)),
        out_shape=(pltpu.SemaphoreType.DMA((3 * n,)), pltpu.SemaphoreType.DMA((3 * n,)),
                   *[pltpu.HBM(b.shape, b.dtype) for b in lands], _sds((8, LANES))),
        input_output_aliases={i: 2 + i for i in range(n)},
        compiler_params=pltpu.CompilerParams(has_side_effects=_EFFECT),
    )(*[pltpu.with_memory_space_constraint(b, pltpu.HBM) for b in lands])
    return res[0], res[1], res[2:2 + n], res[-1]


def _relay_wait(send_sems, recv_sems, lands, after, name):
    n = len(lands)

    def body(*refs):
        land_refs = refs[:n]
        send_sems, recv_sems = refs[n], refs[n + 1]
        x, y, c = lax.axis_index("x"), lax.axis_index("y"), lax.axis_index("c")
        for j, (px, py) in enumerate(_other_chips()):
            sent, received = 4 * px + 2 * py + c, 4 * px + 2 * py + (1 - c)
            for a in range(n):
                cp = pltpu.make_async_remote_copy(
                    src_ref=land_refs[a].at[sent], dst_ref=land_refs[a].at[received],
                    send_sem=send_sems.at[3 * a + j], recv_sem=recv_sems.at[3 * a + j],
                    device_id=(x, y, 1 - c), device_id_type=_MESH)
                cp.wait_send()
                cp.wait_recv()

    return pl.pallas_call(
        body, name=name, in_specs=[_HBM] * n + [_SEM, _SEM] + [_ANY] * len(after), out_specs=[_HBM] * n,
        out_shape=[pltpu.HBM(b.shape, b.dtype) for b in lands],
        input_output_aliases={i: i for i in range(n)},
        compiler_params=pltpu.CompilerParams(has_side_effects=_EFFECT),
    )(*lands, send_sems, recv_sems, *after)


def _adamw(w, g, m, v):
    m2 = ADAM_B1 * m + (1.0 - ADAM_B1) * g
    v2 = ADAM_B2 * v + (1.0 - ADAM_B2) * (g * g)
    m_hat = m2 / (1.0 - ADAM_B1 ** ADAM_STEP)
    v_hat = v2 / (1.0 - ADAM_B2 ** ADAM_STEP)
    delta = -ADAM_LR * (m_hat / (jnp.sqrt(v_hat) + ADAM_EPS) + ADAM_WD * w)
    return delta, m2, v2


def _adam_shard(parts, w, m, v, name):
    _, r, c = w.shape
    tr = max(t for t in range(16, 257, 16) if r % t == 0)

    def body(p_ref, w_ref, m_ref, v_ref, g_ref, d_ref, m2_ref, v2_ref):
        g = p_ref[0].astype(F32)
        for s in range(1, N_DEV):
            g = g + p_ref[s].astype(F32)
        g_ref[0] = g
        d_ref[0], m2_ref[0], v2_ref[0] = _adamw(w_ref[0], g, m_ref[0], v_ref[0])

    row = lambda: pl.BlockSpec((1, tr, c), lambda i: (0, i, 0))
    return pl.pallas_call(
        body, name=name, grid=(r // tr,),
        in_specs=[pl.BlockSpec((N_DEV, tr, c), lambda i: (0, i, 0)), row(), row(), row()],
        out_specs=[row(), row(), row(), row()], out_shape=[_sds((1, r, c))] * 4,
        compiler_params=_cp("parallel"),
    )(*_in_hbm([parts, w, m, v]))


def _adam_small(gs, ws, ms, vs, name):
    n = len(gs)

    def body(*refs):
        ins, outs = refs[:4 * n], refs[4 * n:]
        for i in range(n):
            g = ins[i][...]
            d, m2, v2 = _adamw(ins[n + i][...], g, ins[2 * n + i][...], ins[3 * n + i][...])
            outs[i][...] = d
            outs[n + i][...] = m2
            outs[2 * n + i][...] = v2

    res = pl.pallas_call(
        body, name=name, grid=(1,), in_specs=[_full(w.shape) for w in ws] * 4,
        out_specs=[_full(w.shape) for w in ws] * 3, out_shape=[_sds(w.shape) for w in ws] * 3,
        compiler_params=_cp("arbitrary"),
    )(*_in_hbm([*gs, *ws, *ms, *vs]))
    return res[:n], res[n:2 * n], res[2 * n:]


def _sum_slots(parts, name):
    R = parts.shape[1]

    def body(p_ref, o_ref):
        g = p_ref[0]
        for s in range(1, N_DEV):
            g = g + p_ref[s]
        o_ref[...] = g

    return pl.pallas_call(body, name=name, grid=(1,), in_specs=[_full(parts.shape)], out_specs=_full((R, LANES)),
                          out_shape=_sds((R, LANES)))(*_in_hbm([parts]))


def _pad_to(a, n, axis):
    extra = n - a.shape[axis]
    if extra == 0:
        return a
    widths = [(0, 0)] * a.ndim
    widths[axis] = (0, extra)
    return jnp.pad(a, widths)


def _ceil_to(n, k):
    return -(-n // k) * k


def _pack_rows(flats, rows_multiple):
    parts = [_pad_to(f, _ceil_to(f.shape[-1], LANES), f.ndim - 1) for f in flats]
    cat = jnp.concatenate(parts, axis=-1)
    total = _ceil_to(cat.shape[-1], LANES * rows_multiple)
    cat = _pad_to(cat, total, cat.ndim - 1)
    return cat.reshape(cat.shape[:-1] + (total // LANES, LANES))


def _unpack_rows(buf, sizes):
    flat = buf.reshape(buf.shape[:-2] + (-1,))
    out, off = [], 0
    for n in sizes:
        out.append(flat[..., off:off + n])
        off += _ceil_to(n, LANES)
    return out


_MIX_BIG = ("w_in", "w_glu", "w_proj_a", "w_proj_b", "w_out")
_BIG = _MIX_BIG + ("w_up", "w_down")
_SMALL = ("g_mix", "a_re", "a_im", "log_dt", "b_re", "b_im", "c_re", "c_im", "d_skip", "b_glu", "g_sgu", "w_s", "b_s",
          "g_ffn", "conv_b", "g_final")
_SMALL_ROWS_MULTIPLE = 8 * N_DEV
_TRANSPOSED = ("w_in", "w_up", "b_re", "b_im")


def _as_2d(a):
    return a.reshape(-1, a.shape[-1]) if a.ndim > 1 else a.reshape(1, -1)


def kernel(x, g_mix, w_in, a_re, a_im, log_dt, b_re, b_im, c_re, c_im, d_skip, w_glu, b_glu, w_proj_a, g_sgu, w_s, b_s, w_proj_b, w_out, g_ffn, w_up, conv_w, conv_b, w_down, g_final, loss_target, m_g_mix, m_w_in, m_a_re, m_a_im, m_log_dt, m_b_re, m_b_im, m_c_re, m_c_im, m_d_skip, m_w_glu, m_b_glu, m_w_proj_a, m_g_sgu, m_w_s, m_b_s, m_w_proj_b, m_w_out, m_g_ffn, m_w_up, m_conv_w, m_conv_b, m_w_down, m_g_final, v_g_mix, v_w_in, v_a_re, v_a_im, v_log_dt, v_b_re, v_b_im, v_c_re, v_c_im, v_d_skip, v_w_glu, v_b_glu, v_w_proj_a, v_g_sgu, v_w_s, v_b_s, v_w_proj_b, v_w_out, v_g_ffn, v_w_up, v_conv_w, v_conv_b, v_w_down, v_g_final):
    args = dict(locals())
    me = 4 * lax.axis_index("x") + 2 * lax.axis_index("y") + lax.axis_index("c")

    def own_slot(buf, block):
        return lax.dynamic_update_slice(buf, block[None], (me,) + (0,) * block.ndim)

    for n in _TRANSPOSED:
        for pre in ("", "m_", "v_"):
            args[pre + n] = jnp.swapaxes(args[pre + n], -1, -2)
    later = ("w_glu", "w_proj_a", "w_proj_b", "w_out", "w_up", "w_down")
    (w_in_g,), casts = _allgather([args["w_in"][0]], [MXU], "allgather_w_in", cast_only=[args[n][0] for n in later])
    sh = dict(zip(later, casts))

    def start_push(srcs, tag, peers):
        lands = [own_slot(lax.empty((N_DEV,) + s.shape, s.dtype), s) for s in srcs]
        send_sems, recv_sems, srcs, lands, token = _push_start(srcs, lands, False, "push_" + tag, peers)
        return (send_sems, recv_sems, srcs, lands), token

    mix_push, token_a = start_push([sh[n] for n in later[:4]], "mixer_weights", _SAME_CORE_AND_SIBLING)
    ffn_push, token_b = start_push([sh["w_up"], sh["w_down"], conv_w[0]], "ffn_weights", _PEER_ORDER)
    p = {n: (args[n][0] if n != "g_final" else args[n]) for n in _SMALL if n not in _TRANSPOSED}
    p.update(w_in_t=w_in_g.reshape(SSM_W + 2 * SGU_W + 2 * D_MODEL, D_MODEL),
             b_re_t=args["b_re"][0], b_im_t=args["b_im"][0])
    p["g_mix"] = p["g_mix"] + (token_a[0:1, 0:1] + token_b[0:1, 0:1])
    relay = {}

    def mixer_relay(after):
        lands = _push_wait(*mix_push, False, [after], "wait_mixer_weights", _SAME_CORE_AND_SIBLING)
        relay["send"], relay["recv"], relay["lands"], token = _relay_start(lands, "relay_mixer_weights")
        return token

    def mixer_weights(after):
        w_glu_g, w_pa_g, w_pb_g, w_out_g = _relay_wait(relay["send"], relay["recv"], relay["lands"], [after],
                                                       "wait_relay_mixer_weights")
        w_pa_full, w_pb_full = _assemble_cols([w_pa_g, w_pb_g], "assemble_cols")
        return dict(w_glu=w_glu_g.reshape(SSM_W, SSM_W), w_proj_a=w_pa_full, w_proj_b=w_pb_full,
                    w_out=w_out_g.reshape(D_MODEL, D_MODEL))

    def ffn_weights(after):
        w_up_g, w_down_g, conv_w_g = _push_wait(*ffn_push, False, [after], "wait_ffn_weights")
        return w_up_g, conv_w_g, w_down_g.reshape(D_FF, D_MODEL)

    pushes = []

    def grads_out(names, sends, after=()):
        lands = [own_slot(lax.empty(s.shape, s.dtype), lax.dynamic_index_in_dim(s, me, 0, keepdims=False))
                 for s in sends]
        send_sems, recv_sems, srcs, lands, token = _push_start(list(sends), lands, True, "push_grads_" + names[0],
                                                               after=after)
        pushes.append((names, send_sems, recv_sems, srcs, lands))
        return token

    small_names = _SMALL + ("conv_w", "loss")
    small = {}

    def small_out(grads, loss_part):
        small_g = dict(grads, loss=loss_part[0, 0:1])
        flats = [small_g[n].reshape(-1) for n in small_names]
        small["sizes"] = [f.shape[0] for f in flats]
        g_small = _pack_rows(flats, _SMALL_ROWS_MULTIPLE)
        small["rs8"] = g_small.shape[0] // N_DEV
        return grads_out(("small",), (g_small.reshape(N_DEV, small["rs8"], LANES),))

    grad_x = _local_step(x[0], loss_target[0], p, mixer_relay, mixer_weights, ffn_weights, grads_out, small_out)

    out = {}
    done = [grad_x]
    for names, send_sems, recv_sems, srcs, lands in pushes:
        parts = _push_wait(send_sems, recv_sems, srcs, lands, True, done, "wait_grads_" + names[0])
        if names == ("small",):
            small_mine = _sum_slots(parts[0], "sum_small")
            g_small_all = _allgather([small_mine], [F32], "allgather_small")[0][0].reshape(N_DEV * small["rs8"], LANES)
            pieces = dict(zip(small_names, _unpack_rows(g_small_all, small["sizes"])))
            loss = pieces["loss"][0]
            dconv_w = lax.dynamic_index_in_dim(pieces["conv_w"].reshape(N_DEV, 3, FF_SHARD), me, axis=0, keepdims=False)
            names2 = _SMALL + ("conv_w",)
            gs = [pieces[n].reshape(_as_2d(args[n]).shape) for n in _SMALL] + [dconv_w]
            ds, m2s, v2s = _adam_small(gs, [_as_2d(args[n]) for n in names2], [_as_2d(args["m_" + n]) for n in names2],
                                       [_as_2d(args["v_" + n]) for n in names2], "adam_small")
            for n, res in zip(names2, zip(gs, ds, m2s, v2s)):
                for kind, v in zip(("grad_", "delta_", "new_m_", "new_v_"), res):
                    out[kind + n] = v.reshape(args[n].shape)
            done = [ds[0]]
            continue
        for n, part in zip(names, parts):
            res = _adam_shard(part, args[n], args["m_" + n], args["v_" + n], "adam_" + n)
            for kind, v in zip(("grad_", "delta_", "new_m_", "new_v_"), res):
                out[kind + n] = v
            done = [res[0]]
    order = ("g_mix", "w_in", "a_re", "a_im", "log_dt", "b_re", "b_im", "c_re", "c_im", "d_skip", "w_glu", "b_glu",
             "w_proj_a", "g_sgu", "w_s", "b_s", "w_proj_b", "w_out", "g_ffn", "w_up", "conv_w", "conv_b", "w_down",
             "g_final")
    res = [loss, grad_x.reshape(x.shape)]
    for kind in ("grad_", "delta_", "new_m_", "new_v_"):
        res += [jnp.swapaxes(out[kind + n], -1, -2) if n in _TRANSPOSED else out[kind + n] for n in order]
    return tuple(res)
```

```python
import functools
import math

import jax
import jax.numpy as jnp
from jax import lax
from jax.experimental import pallas as pl
from jax.experimental.pallas import tpu as pltpu

F32 = jnp.float32
MXU = jnp.bfloat16
EPS = 1e-6

D_MODEL = 1024
SSM_W = 512
SSM_G, SSM_H, SSM_P = 32, 16, 64
SSM_BLK = 4
SGU_W = 512
SGU_G, SGU_D, CHUNK = 8, 64, 128
D_FF = 2816
N_DEV = 8
FF_SHARD = 2 * D_FF // N_DEV
FF_CW = 2 * FF_SHARD
FF_NCB = D_FF // FF_CW
LANES = 128

ADAM_LR, ADAM_B1, ADAM_B2, ADAM_EPS, ADAM_WD, ADAM_STEP = 0.001, 0.9, 0.999, 1e-08, 0.01, 10

VMEM_LIMIT = 48 * 1024 * 1024
WGRAD_VMEM_LIMIT = 58 * 1024 * 1024
FFN_VMEM_LIMIT = 58 * 1024 * 1024


def _cp(*sem):
    return pltpu.CompilerParams(dimension_semantics=sem, vmem_limit_bytes=VMEM_LIMIT)


def _full(shape):
    n = len(shape)
    return pl.BlockSpec(shape, lambda *_: (0,) * n)


def _sds(shape, dtype=F32):
    return jax.ShapeDtypeStruct(shape, dtype)


def _in_hbm(arrays):
    return [pltpu.with_memory_space_constraint(a, pltpu.HBM) for a in arrays]


def _dot(a, b):
    return jnp.dot(a, b, preferred_element_type=F32)


def _dot_nt(a, b):
    return lax.dot_general(a, b, (((1,), (1,)), ((), ())), preferred_element_type=F32)


def _dot_tn(a, b):
    return lax.dot_general(a, b, (((0,), (0,)), ((), ())), preferred_element_type=F32)


_GELU_C = math.sqrt(2.0 / math.pi)


def _gelu(x):
    return 0.5 * x * (1.0 + jnp.tanh(_GELU_C * (x + 0.044715 * (x * x * x))))


def _gelu_and_grad(x):
    t = jnp.tanh(_GELU_C * (x + 0.044715 * (x * x * x)))
    g = 0.5 * x * (1.0 + t)
    dg = 0.5 * (1.0 + t) + 0.5 * x * (1.0 - t * t) * (_GELU_C * (1.0 + 3.0 * 0.044715 * (x * x)))
    return g, dg


def _sigmoid(x):
    return 0.5 * jnp.tanh(0.5 * x) + 0.5


def _rms(x):
    return lax.rsqrt(jnp.mean(x * x, axis=-1, keepdims=True) + EPS)


def _rms_bwd(dxn, xn, r):
    return r * (dxn - xn * jnp.mean(dxn * xn, axis=-1, keepdims=True))


def _rowsum(x):
    return jnp.sum(x, axis=0, keepdims=True)


def _fetch_once(pairs, sems):
    copies = [pltpu.make_async_copy(src, dst, sems.at[k]) for k, (src, dst) in enumerate(pairs)]
    for cp in copies:
        cp.start()
    for cp in copies:
        cp.wait()


def _s5_disc(are, aim, ldt, br, bi):
    dt = jnp.exp(ldt)
    mag = jnp.exp(dt * are)
    abr = mag * jnp.cos(dt * aim)
    abi = mag * jnp.sin(dt * aim)
    den = are * are + aim * aim
    nr = abr - 1.0
    ni = abi
    fr = (nr * are + ni * aim) / den
    fi = (ni * are - nr * aim) / den
    return abr, abi, fr * br - fi * bi, fr * bi + fi * br


def _s5_params_fwd(are, aim, ldt, br, bi):
    def body(are_ref, aim_ref, ldt_ref, br_ref, bi_ref, o0, o1, o2, o3):
        outs = _s5_disc(are_ref[...], aim_ref[...], ldt_ref[...], br_ref[...], bi_ref[...])
        for o, v in zip((o0, o1, o2, o3), outs):
            o[...] = v
    shp = are.shape
    return pl.pallas_call(body, name="s5_params_fwd", grid=(1,), in_specs=[_full(shp)] * 5, out_specs=[_full(shp)] * 4,
                          out_shape=[_sds(shp)] * 4)(*_in_hbm([are, aim, ldt, br, bi]))


def _s5_params_bwd(are, aim, ldt, br, bi, dabr, dabi, dbr, dbi):
    def body(are_ref, aim_ref, ldt_ref, br_ref, bi_ref, c0, c1, c2, c3, o0, o1, o2, o3, o4):
        prim = (are_ref[...], aim_ref[...], ldt_ref[...], br_ref[...], bi_ref[...])
        _, vjp = jax.vjp(_s5_disc, *prim)
        outs = vjp((c0[...], c1[...], c2[...], c3[...]))
        for o, v in zip((o0, o1, o2, o3, o4), outs):
            o[...] = v
    shp = are.shape
    return pl.pallas_call(body, name="s5_params_bwd", grid=(1,), in_specs=[_full(shp)] * 9, out_specs=[_full(shp)] * 5,
                          out_shape=[_sds(shp)] * 5)(*_in_hbm([are, aim, ldt, br, bi, dabr, dabi, dbr, dbi]))


def _blockdiag(m_t):
    m = m_t.reshape(SSM_BLK, 8, SSM_H, 1, SSM_P)
    eye = jnp.eye(8, dtype=bool).reshape(1, 8, 1, 8, 1)
    return jnp.where(eye, m, jnp.zeros((), m_t.dtype)).reshape(SSM_BLK, 8 * SSM_H, 8 * SSM_P)


def _unblockdiag(pc):
    m = pc.reshape(SSM_BLK, 8, SSM_H, 8, SSM_P)
    return jnp.einsum("jghgp->jghp", m).reshape(SSM_G * SSM_H, SSM_P)


def _in_fwd(x, g_mix, w_in_t, tm):
    S = x.shape[0]

    def body(x_ref, g_ref, w_ref, h_ref, us_ref, uv_ref, gl_ref):
        xv = x_ref[...]
        h = (xv * _rms(xv) * g_ref[...]).astype(MXU)
        h_ref[...] = h
        us_ref[...] = _dot_nt(h, w_ref[0:SSM_W, :])
        uv_ref[...] = _dot_nt(h, w_ref[SSM_W:SSM_W + 2 * SGU_W, :])
        gl_ref[...] = _dot_nt(h, w_ref[SSM_W + 2 * SGU_W:, :])

    row = lambda n: pl.BlockSpec((tm, n), lambda i: (i, 0))
    return pl.pallas_call(
        body, name="in_fwd", grid=(S // tm,),
        in_specs=[row(D_MODEL), _full((1, D_MODEL)), _full(w_in_t.shape)],
        out_specs=[row(D_MODEL), row(SSM_W), row(2 * SGU_W), row(2 * D_MODEL)],
        out_shape=[_sds((S, D_MODEL), MXU), _sds((S, SSM_W)), _sds((S, 2 * SGU_W)), _sds((S, 2 * D_MODEL))],
        compiler_params=_cp("parallel"),
    )(*_in_hbm([x, g_mix, w_in_t]))


def _scan_tables(ar, ai, reverse):
    n = ar.shape[-1]
    def mul(p, q):
        return p[0] * q[0] - p[1] * q[1], p[0] * q[1] + p[1] * q[0]
    a1 = (ar, ai)
    a2 = mul(a1, a1)
    a3 = mul(a2, a1)
    a4 = mul(a2, a2)
    a5 = mul(a4, a1)
    a6 = mul(a4, a2)
    a7 = mul(a4, a3)
    a8 = mul(a4, a4)
    pw = (a1, a2, a3, a4, a5, a6, a7, a8)
    rows = lax.broadcasted_iota(jnp.int32, (8, n), 0)
    tabs = []
    for s, a in ((1, a1), (2, a2), (4, a4)):
        keep = (rows + s <= 7) if reverse else (rows >= s)
        for comp in a:
            tabs.append(jnp.where(keep, jnp.broadcast_to(comp, (8, n)), 0.0))
    for c in range(2):
        q = jnp.zeros((8, n), F32)
        for r in range(8):
            e = (8 - r) if reverse else (r + 1)
            q = jnp.where(rows == r, jnp.broadcast_to(pw[e - 1][c], (8, n)), q)
        tabs.append(q)
    return tabs


def _scan_group(xr, xi, tab_ref, cr, ci, reverse):
    for t, s in enumerate((1, 2, 4)):
        pr = tab_ref[2 * t]
        pi = tab_ref[2 * t + 1]
        sh = (8 - s) if reverse else s
        sr = pltpu.roll(xr, sh, 0)
        si = pltpu.roll(xi, sh, 0)
        xr, xi = xr + pr * sr - pi * si, xi + pr * si + pi * sr
    qr = tab_ref[6]
    qi = tab_ref[7]
    return xr + qr * cr - qi * ci, xi + qr * ci + qi * cr


def _runs_load(src_ref, dst_ref, run):
    for i in range(run):
        dst_ref[8 * i:8 * i + 8, :] = src_ref[pl.ds(i, 8, stride=run), :]


def _runs_store(val, dst_ref, run):
    for i in range(run):
        dst_ref[pl.ds(i, 8, stride=run), :] = val[8 * i:8 * i + 8, :]


def _cpow2(ar, ai, log2n):
    for _ in range(log2n):
        ar, ai = ar * ar - ai * ai, 2.0 * ar * ai
    return ar, ai


def _s5_fwd(us, abar_re, abar_im, b_re, b_im, c_re, c_im, d_skip, tm):
    S = us.shape[0]
    nt = S // tm
    w = 8 * SSM_P
    run = tm // 8
    assert run & (run - 1) == 0

    def body(us_ref, ar_ref, ai_ref, br_ref, bi_ref, cr_ref, ci_ref, d_ref, str_ref, sti_ref, ys_ref,
             tab_ref, car_ref, up_ref):
        i = pl.program_id(1)

        @pl.when(i == 0)
        def _():
            car_ref[...] = jnp.zeros_like(car_ref)
            for k, t in enumerate(_scan_tables(*_cpow2(ar_ref[...], ai_ref[...], run.bit_length() - 1), False)):
                tab_ref[k] = t

        _runs_load(us_ref, up_ref, run)
        ub = up_ref[...].astype(MXU)
        str_ref[...] = _dot(ub, br_ref[0])
        sti_ref[...] = _dot(ub, bi_ref[0])
        ar = jnp.broadcast_to(ar_ref[...], (8, w))
        ai = jnp.broadcast_to(ai_ref[...], (8, w))

        def advance(k, state):
            r0 = pl.multiple_of(k * 8, 8)
            sr, si = state
            return (ar * sr - ai * si + str_ref[pl.ds(r0, 8), :], ar * si + ai * sr + sti_ref[pl.ds(r0, 8), :])

        def emit(k, state):
            r0 = pl.multiple_of(k * 8, 8)
            sr, si = advance(k, state)
            str_ref[pl.ds(r0, 8), :] = sr
            sti_ref[pl.ds(r0, 8), :] = si
            return sr, si

        zero = jnp.zeros((8, w), F32)
        er, ei = lax.fori_loop(0, run, advance, (zero, zero))
        cr, ci = car_ref[0:1, :], car_ref[1:2, :]
        tr, ti = _scan_group(er, ei, tab_ref, cr, ci, False)
        r8 = lax.broadcasted_iota(jnp.int32, (8, w), 0)
        start = (jnp.where(r8 == 0, cr, pltpu.roll(tr, 1, 0)), jnp.where(r8 == 0, ci, pltpu.roll(ti, 1, 0)))
        car_ref[0:1, :] = tr[7:8, :]
        car_ref[1:2, :] = ti[7:8, :]
        lax.fori_loop(0, run, emit, start)
        y = _dot_nt(str_ref[...].astype(MXU), cr_ref[0]) - _dot_nt(sti_ref[...].astype(MXU), ci_ref[0])
        _runs_store(y, ys_ref, run)
        ys_ref[...] += d_ref[...] * us_ref[...]

    blk = lambda: pl.BlockSpec((1, 8 * SSM_H, w), lambda j, i: (j, 0, 0))
    return pl.pallas_call(
        body, name="s5_fwd", grid=(SSM_BLK, nt),
        in_specs=[pl.BlockSpec((tm, LANES), lambda j, i: (i, j)),
                  pl.BlockSpec((1, w), lambda j, i: (0, j)), pl.BlockSpec((1, w), lambda j, i: (0, j)),
                  blk(), blk(), blk(), blk(),
                  pl.BlockSpec((1, LANES), lambda j, i: (0, j))],
        out_specs=[pl.BlockSpec((tm, w), lambda j, i: (i, j)), pl.BlockSpec((tm, w), lambda j, i: (i, j)),
                   pl.BlockSpec((tm, LANES), lambda j, i: (i, j))],
        out_shape=[_sds((S, SSM_BLK * w)), _sds((S, SSM_BLK * w)), _sds((S, SSM_W))],
        scratch_shapes=[pltpu.VMEM((8, 8, w), F32), pltpu.VMEM((8, w), F32), pltpu.VMEM((tm, LANES), F32)],
        compiler_params=_cp("parallel", "arbitrary"),
    )(*_in_hbm([us, abar_re, abar_im, b_re, b_im, c_re, c_im, d_skip]))


def _group_halves(vp):
    first = lax.broadcasted_iota(jnp.int32, vp.shape, 1) < SGU_D
    zero = jnp.zeros((), vp.dtype)
    return jnp.where(first, vp, zero), jnp.where(first, zero, vp)


def _sgu_mix(vnb, wcat_ref):
    outs = []
    for q in range(SGU_G // 2):
        lo, hi = _group_halves(vnb[:, LANES * q:LANES * (q + 1)])
        outs.append(_dot(wcat_ref[q], jnp.concatenate([lo, hi], axis=0)))
    return jnp.concatenate(outs, axis=1)


def _mix_fwd(x, ys, uv, gl, w_glu, b_glu, w_pa, g_sgu, ws, bias_s, w_pb, w_out, g_ffn, tm):
    S = x.shape[0]

    def body(x_ref, ys_ref, uv_ref, gl_ref, wglu_ref, bglu_ref, wpa_ref, gs_ref, ws_ref, bias_ref, wpb_ref, wout_ref,
             gf_ref, yg_ref, yap_ref, sg_ref, ya_ref, yb_ref, m_ref, x1_ref, h2_ref):
        yg = _gelu(ys_ref[...])
        ygb = yg.astype(MXU)
        yg_ref[...] = ygb
        z = _dot(ygb, wglu_ref[...]) + bglu_ref[...]
        yapb = (yg * _sigmoid(z)).astype(MXU)
        yap_ref[...] = yapb
        ya = _dot(yapb, wpa_ref[...])
        ya_ref[...] = ya

        uvg = _gelu(uv_ref[...])
        u2 = uvg[:, :SGU_W]
        v2 = uvg[:, SGU_W:]
        vnb = (v2 * _rms(v2) * gs_ref[...]).astype(MXU)
        for c in range(tm // CHUNK):
            rs = slice(c * CHUNK, (c + 1) * CHUNK)
            mixed = _sgu_mix(vnb[rs], ws_ref) + bias_ref[...]
            sg_ref[rs, :] = (u2[rs] * mixed).astype(MXU)
        yb = _dot(sg_ref[...], wpb_ref[...])
        yb_ref[...] = yb

        glv = gl_ref[...]
        m = _sigmoid(glv[:, :D_MODEL]) * ya + _sigmoid(glv[:, D_MODEL:]) * yb
        mb = m.astype(MXU)
        m_ref[...] = mb
        x1 = x_ref[...] + _dot(mb, wout_ref[...])
        x1_ref[...] = x1
        h2_ref[...] = (x1 * _rms(x1) * gf_ref[...]).astype(MXU)

    row = lambda n: pl.BlockSpec((tm, n), lambda i: (i, 0))
    return pl.pallas_call(
        body, name="mix_fwd", grid=(S // tm,),
        in_specs=[row(D_MODEL), row(SSM_W), row(2 * SGU_W), row(2 * D_MODEL),
                  _full(w_glu.shape), _full(b_glu.shape), _full(w_pa.shape), _full(g_sgu.shape), _full(ws.shape),
                  _full(bias_s.shape), _full(w_pb.shape), _full(w_out.shape), _full(g_ffn.shape)],
        out_specs=[row(SSM_W), row(SSM_W), row(SGU_W), row(D_MODEL), row(D_MODEL), row(D_MODEL), row(D_MODEL),
                   row(D_MODEL)],
        out_shape=[_sds((S, SSM_W), MXU), _sds((S, SSM_W), MXU), _sds((S, SGU_W), MXU), _sds((S, D_MODEL)),
                   _sds((S, D_MODEL)), _sds((S, D_MODEL), MXU), _sds((S, D_MODEL)), _sds((S, D_MODEL), MXU)],
        compiler_params=_cp("parallel"),
    )(*_in_hbm([x, ys, uv, gl, w_glu, b_glu, w_pa, g_sgu, ws, bias_s, w_pb, w_out, g_ffn]))


def _causal_conv3(u, prev8, cw, cb):
    tm = u.shape[0]
    w0, w1, w2 = cw[0:1], cw[1:2], cw[2:3]
    body = w0 * pltpu.roll(u, 2, 0) + w1 * pltpu.roll(u, 1, 0) + w2 * u + cb
    u8 = u[0:8, :]
    r8 = lax.broadcasted_iota(jnp.int32, u8.shape, 0)
    t1 = prev8[7:8, :]
    t0 = prev8[6:7, :]
    s1 = jnp.where(r8 == 0, t1, pltpu.roll(u8, 1, 0))
    s2 = jnp.where(r8 == 0, t0, jnp.where(r8 == 1, t1, pltpu.roll(u8, 2, 0)))
    first = w0 * s2 + w1 * s1 + w2 * u8 + cb
    return jnp.concatenate([first, body[8:tm, :]], axis=0)


def _causal_conv3_adjoint(d, next8, cw):
    tm = d.shape[0]
    w0, w1, w2 = cw[0:1], cw[1:2], cw[2:3]
    n1 = pltpu.roll(d, tm - 1, 0)
    n2 = pltpu.roll(d, tm - 2, 0)
    body = w2 * d + w1 * n1 + w0 * n2
    d8 = d[tm - 8:tm, :]
    r8 = lax.broadcasted_iota(jnp.int32, d8.shape, 0)
    h0 = next8[0:1, :]
    h1 = next8[1:2, :]
    m1 = jnp.where(r8 == 7, h0, pltpu.roll(d8, 7, 0))
    m2 = jnp.where(r8 == 6, h0, jnp.where(r8 == 7, h1, pltpu.roll(d8, 6, 0)))
    last = w2 * d8 + w1 * m1 + w0 * m2
    out = jnp.concatenate([body[0:tm - 8, :], last], axis=0)
    return out, n1, n2, h0 - d[0:1, :], h1 - d[1:2, :]


def _ffn_fwd(h2, x1, tgt, w_up, conv_w, conv_b, w_down, g_final, tm):
    S = h2.shape[0]
    nt = S // tm
    ncb = FF_NCB

    def body(h2_ref, wup_hbm, cwa_ref, cwb_ref, cba_ref, cbb_ref, wd_hbm, x1_ref, gf_ref, tgt_ref,
             up_ref, ab_ref, ff_ref, dx2_ref, dx2b_ref, loss_ref, dgf_ref, acc_ref, tail_ref, wup_ref, wdn_ref, wsem):
        i = pl.program_id(0)
        cb = pl.program_id(1)

        @pl.when(i == 0)
        def _():
            tail_ref[cb] = jnp.zeros((2, 8, FF_CW), F32)

        @pl.when(jnp.logical_and(i == 0, cb == 0))
        def _():
            loss_ref[...] = jnp.zeros_like(loss_ref)
            dgf_ref[...] = jnp.zeros_like(dgf_ref)
            _fetch_once([(wup_hbm, wup_ref), (wd_hbm, wdn_ref)], wsem)

        h2v = h2_ref[...]
        ua = _dot_nt(h2v, wup_ref[cb])
        ub = _dot_nt(h2v, wup_ref[ncb + cb])
        up_ref[0, 0] = ua.astype(MXU)
        up_ref[1, 0] = ub.astype(MXU)
        a = _causal_conv3(ua, tail_ref[cb, 0], cwa_ref[0], cba_ref[0])
        b = _causal_conv3(ub, tail_ref[cb, 1], cwb_ref[0], cbb_ref[0])
        tail_ref[cb, 0] = ua[tm - 8:tm, :]
        tail_ref[cb, 1] = ub[tm - 8:tm, :]
        ab_ref[0, 0] = a
        ab_ref[1, 0] = b
        ffb = (a * _sigmoid(a) * b).astype(MXU)
        ff_ref[0] = ffb
        contrib = _dot(ffb, wdn_ref[pl.ds(pl.multiple_of(cb * FF_CW, FF_CW), FF_CW), :])

        @pl.when(cb == 0)
        def _():
            acc_ref[...] = contrib

        @pl.when(cb > 0)
        def _():
            acc_ref[...] += contrib

        @pl.when(cb == ncb - 1)
        def _():
            x2 = x1_ref[...] + acc_ref[...]
            r = _rms(x2)
            xn = x2 * r
            g = gf_ref[...]
            diff = xn * g - tgt_ref[...]
            loss_ref[...] += (0.5 / D_MODEL) * jnp.sum(diff * diff)
            dy = diff * (1.0 / D_MODEL)
            dgf_ref[...] += _rowsum(dy * xn)
            dx2 = _rms_bwd(dy * g, xn, r)
            dx2_ref[...] = dx2
            dx2b_ref[...] = dx2.astype(MXU)

    row = lambda n: pl.BlockSpec((tm, n), lambda i, c: (i, 0))
    gate = lambda r: pl.BlockSpec((1, r, FF_CW), lambda i, c: (c, 0, 0))
    lin = lambda r: pl.BlockSpec((1, r, FF_CW), lambda i, c: (ncb + c, 0, 0))
    return pl.pallas_call(
        body, name="ffn_fwd", grid=(nt, ncb),
        in_specs=[row(D_MODEL), _ANY, gate(3), lin(3), gate(1), lin(1), _ANY,
                  row(D_MODEL), _full((1, D_MODEL)), row(D_MODEL)],
        out_specs=[pl.BlockSpec((2, 1, tm, FF_CW), lambda i, c: (0, c, i, 0)),
                   pl.BlockSpec((2, 1, tm, FF_CW), lambda i, c: (0, c, i, 0)),
                   pl.BlockSpec((1, tm, FF_CW), lambda i, c: (c, i, 0)),
                   row(D_MODEL), row(D_MODEL), _full((1, LANES)), _full((1, D_MODEL))],
        out_shape=[_sds((2, ncb, S, FF_CW), MXU), _sds((2, ncb, S, FF_CW)), _sds((ncb, S, FF_CW), MXU),
                   _sds((S, D_MODEL)), _sds((S, D_MODEL), MXU), _sds((1, LANES)), _sds((1, D_MODEL))],
        scratch_shapes=[pltpu.VMEM((tm, D_MODEL), F32), pltpu.VMEM((ncb, 2, 8, FF_CW), F32),
                        pltpu.VMEM(w_up.shape, w_up.dtype), pltpu.VMEM(w_down.shape, w_down.dtype),
                        pltpu.SemaphoreType.DMA((2,))],
        compiler_params=pltpu.CompilerParams(dimension_semantics=("arbitrary", "arbitrary"),
                                             vmem_limit_bytes=FFN_VMEM_LIMIT),
    )(*_in_hbm([h2, w_up, conv_w, conv_w, conv_b, conv_b, w_down, x1, g_final, tgt]))


def _ffn_bwd(dx2, up, ab, x1, w_up, conv_w, w_down, g_ffn, tm):
    S = dx2.shape[0]
    nt = S // tm
    ncb = FF_NCB

    def body(dx2_ref, up_ref, ab_ref, cwa_ref, cwb_ref, wd_hbm, wup_hbm,
             x1_ref, g_ref, dup_ref, dx1_ref, dx1b_ref, dconv_ref, dg_ref, acc_ref, head_ref, wup_ref, wdn_ref, wsem):
        i = pl.program_id(0)
        cb = pl.program_id(1)

        @pl.when(i == 0)
        def _():
            head_ref[cb] = jnp.zeros((2, 8, FF_CW), F32)
            dconv_ref[cb] = jnp.zeros((8, FF_CW), F32)
            dconv_ref[ncb + cb] = jnp.zeros((8, FF_CW), F32)

        @pl.when(jnp.logical_and(i == 0, cb == 0))
        def _():
            dg_ref[...] = jnp.zeros_like(dg_ref)
            _fetch_once([(wup_hbm, wup_ref), (wd_hbm, wdn_ref)], wsem)

        dff = _dot_nt(dx2_ref[...].astype(MXU), wdn_ref[pl.ds(pl.multiple_of(cb * FF_CW, FF_CW), FF_CW), :])
        a = ab_ref[0, 0]
        b = ab_ref[1, 0]
        sa = _sigmoid(a)
        silu = a * sa
        da = (dff * b) * (sa + silu * (1.0 - sa))
        db = dff * silu
        dps = []
        for half, slot, d, cw_ref in ((0, cb, da, cwa_ref), (1, ncb + cb, db, cwb_ref)):
            dp, n1, n2, fix0, fix1 = _causal_conv3_adjoint(d, head_ref[cb, half], cw_ref[0])
            head_ref[cb, half] = d[0:8, :]
            dpb16 = dp.astype(MXU)
            dup_ref[half, 0] = dpb16
            dps.append(dpb16)
            u = up_ref[half, 0].astype(F32)
            u_last = u[tm - 1:tm, :]
            dconv_ref[slot, 0:1, :] += _rowsum(n2 * u) + fix0 * u[tm - 2:tm - 1, :] + fix1 * u_last
            dconv_ref[slot, 1:2, :] += _rowsum(n1 * u) + fix0 * u_last
            dconv_ref[slot, 2:3, :] += _rowsum(d * u)
            dconv_ref[slot, 3:4, :] += _rowsum(d)
        contrib = _dot(dps[0], wup_ref[cb]) + _dot(dps[1], wup_ref[ncb + cb])

        @pl.when(cb == 0)
        def _():
            acc_ref[...] = contrib

        @pl.when(cb > 0)
        def _():
            acc_ref[...] += contrib

        @pl.when(cb == ncb - 1)
        def _():
            x1v = x1_ref[...]
            r = _rms(x1v)
            xn = x1v * r
            dh2 = acc_ref[...]
            dg_ref[...] += _rowsum(dh2 * xn)
            dx1 = dx2_ref[...] + _rms_bwd(dh2 * g_ref[...], xn, r)
            dx1_ref[...] = dx1
            dx1b_ref[...] = dx1.astype(MXU)

    row = lambda n: pl.BlockSpec((tm, n), lambda i, c: (nt - 1 - i, 0))
    colb = lambda: pl.BlockSpec((2, 1, tm, FF_CW), lambda i, c: (0, c, nt - 1 - i, 0))
    gate = lambda r: pl.BlockSpec((1, r, FF_CW), lambda i, c: (c, 0, 0))
    lin = lambda r: pl.BlockSpec((1, r, FF_CW), lambda i, c: (ncb + c, 0, 0))
    return pl.pallas_call(
        body, name="ffn_bwd", grid=(nt, ncb),
        in_specs=[row(D_MODEL), colb(), colb(), gate(3), lin(3), _ANY, _ANY, row(D_MODEL), _full((1, D_MODEL))],
        out_specs=[colb(), row(D_MODEL), row(D_MODEL), _full((2 * ncb, 8, FF_CW)), _full((1, D_MODEL))],
        out_shape=[_sds((2, ncb, S, FF_CW), MXU), _sds((S, D_MODEL)), _sds((S, D_MODEL), MXU), _sds((2 * ncb, 8, FF_CW)),
                   _sds((1, D_MODEL))],
        scratch_shapes=[pltpu.VMEM((tm, D_MODEL), F32), pltpu.VMEM((ncb, 2, 8, FF_CW), F32),
                        pltpu.VMEM(w_up.shape, w_up.dtype), pltpu.VMEM(w_down.shape, w_down.dtype),
                        pltpu.SemaphoreType.DMA((2,))],
        compiler_params=pltpu.CompilerParams(dimension_semantics=("arbitrary", "arbitrary"),
                                             vmem_limit_bytes=FFN_VMEM_LIMIT),
    )(*_in_hbm([dx2, up, ab, conv_w, conv_w, w_down, w_up, x1, g_ffn]))


def _mix_bwd(dx1, gl, ya, yb, ys, uv, w_out, w_pa, w_pb, w_glu, b_glu, g_sgu, ws, ws_t, bias_s, tm):
    S = dx1.shape[0]

    def body(dx1_ref, gl_ref, ya_ref, yb_ref, ys_ref, uv_ref, wout_ref, wpa_ref, wpb_ref, wglu_ref, bglu_ref, gs_ref,
             ws_ref, wst_ref, bias_ref,
             dgl_ref, dya_ref, dyb_ref, dz_ref, dys_ref, duv_ref, dbglu_ref, dgs_ref, dws_ref, dbs_ref,
             du2_ref, dvn_ref):
        i = pl.program_id(0)

        @pl.when(i == 0)
        def _():
            dbglu_ref[...] = jnp.zeros_like(dbglu_ref)
            dgs_ref[...] = jnp.zeros_like(dgs_ref)
            dws_ref[...] = jnp.zeros_like(dws_ref)
            dbs_ref[...] = jnp.zeros_like(dbs_ref)

        dm = _dot_nt(dx1_ref[...].astype(MXU), wout_ref[...])
        glv = gl_ref[...]
        ga = _sigmoid(glv[:, :D_MODEL])
        gb = _sigmoid(glv[:, D_MODEL:])
        dgl_ref[:, :D_MODEL] = (dm * ya_ref[...] * ga * (1.0 - ga)).astype(MXU)
        dgl_ref[:, D_MODEL:] = (dm * yb_ref[...] * gb * (1.0 - gb)).astype(MXU)
        dyab = (dm * ga).astype(MXU)
        dybb = (dm * gb).astype(MXU)
        dya_ref[...] = dyab
        dyb_ref[...] = dybb

        dyap = _dot_nt(dyab, wpa_ref[...])
        yg, dgelu = _gelu_and_grad(ys_ref[...])
        sz = _sigmoid(_dot(yg.astype(MXU), wglu_ref[...]) + bglu_ref[...])
        dz = dyap * yg * sz * (1.0 - sz)
        dzb = dz.astype(MXU)
        dz_ref[...] = dzb
        dbglu_ref[...] += _rowsum(dz)
        dys_ref[...] = (dyap * sz + _dot_nt(dzb, wglu_ref[...])) * dgelu

        dsg = _dot_nt(dybb, wpb_ref[...])
        uvg, duvg = _gelu_and_grad(uv_ref[...])
        u2 = uvg[:, :SGU_W]
        v2 = uvg[:, SGU_W:]
        rv = _rms(v2)
        vhat = v2 * rv
        gs = gs_ref[...]
        vnb = (vhat * gs).astype(MXU)
        tril = (lax.broadcasted_iota(jnp.int32, (CHUNK, CHUNK), 0)
                >= lax.broadcasted_iota(jnp.int32, (CHUNK, CHUNK), 1))
        for c in range(tm // CHUNK):
            rs = slice(c * CHUNK, (c + 1) * CHUNK)
            vc = vnb[rs]
            mixed = _sgu_mix(vc, ws_ref) + bias_ref[...]
            dsg_c = dsg[rs]
            du2_ref[rs, :] = dsg_c * mixed
            dmx = dsg_c * u2[rs]
            dbs_ref[...] += dmx
            dmb = dmx.astype(MXU)
            dvn_ref[rs, :] = _sgu_mix(dmb, wst_ref)
            for q in range(SGU_G // 2):
                lanes = slice(LANES * q, LANES * (q + 1))
                for j, part in enumerate(_group_halves(dmb[:, lanes])):
                    dws_ref[2 * q + j] += jnp.where(tril, _dot_nt(part, vc[:, lanes]), 0.0)
        dvn = dvn_ref[...]
        dgs_ref[...] += _rowsum(dvn * vhat)
        dv2 = _rms_bwd(dvn * gs, vhat, rv)
        duv_ref[:, :SGU_W] = (du2_ref[...] * duvg[:, :SGU_W]).astype(MXU)
        duv_ref[:, SGU_W:] = (dv2 * duvg[:, SGU_W:]).astype(MXU)

    row = lambda n: pl.BlockSpec((tm, n), lambda i: (i, 0))
    return pl.pallas_call(
        body, name="mix_bwd", grid=(S // tm,),
        in_specs=[row(D_MODEL), row(2 * D_MODEL), row(D_MODEL), row(D_MODEL), row(SSM_W), row(2 * SGU_W),
                  _full(w_out.shape), _full(w_pa.shape), _full(w_pb.shape), _full(w_glu.shape), _full(b_glu.shape),
                  _full(g_sgu.shape), _full(ws.shape), _full(ws_t.shape), _full(bias_s.shape)],
        out_specs=[row(2 * D_MODEL), row(D_MODEL), row(D_MODEL), row(SSM_W), row(SSM_W), row(2 * SGU_W),
                   _full((1, SSM_W)), _full((1, SGU_W)), _full((SGU_G, CHUNK, CHUNK)), _full((CHUNK, SGU_W))],
        out_shape=[_sds((S, 2 * D_MODEL), MXU), _sds((S, D_MODEL), MXU), _sds((S, D_MODEL), MXU), _sds((S, SSM_W), MXU),
                   _sds((S, SSM_W)), _sds((S, 2 * SGU_W), MXU),
                   _sds((1, SSM_W)), _sds((1, SGU_W)), _sds((SGU_G, CHUNK, CHUNK)), _sds((CHUNK, SGU_W))],
        scratch_shapes=[pltpu.VMEM((tm, SGU_W), F32), pltpu.VMEM((tm, SGU_W), F32)],
        compiler_params=_cp("arbitrary"),
    )(*_in_hbm([dx1, gl, ya, yb, ys, uv, w_out, w_pa, w_pb, w_glu, b_glu, g_sgu, ws, ws_t, bias_s]))


def _s5_bwd(dys, us, st_re, st_im, abar_re, abar_im, b_re, b_im, c_re, c_im, d_skip, tm):
    S = us.shape[0]
    nt = S // tm
    w = 8 * SSM_P
    hb = tm // 8
    run = tm // 8
    assert run & (run - 1) == 0

    def body(dys_ref, us_ref, str_ref, sti_ref, hr_ref, hi_ref, ar_ref, ai_ref, br_ref, bi_ref, cr_ref, ci_ref, d_ref,
             dus_ref, dab_ref, dd_ref, dbr_ref, dbi_ref, dcr_ref, dci_ref,
             tab_ref, car_ref, gr_ref, gi_ref, dyp_ref, up_ref, dun_ref):
        i = pl.program_id(1)
        ri = nt - 1 - i

        @pl.when(i == 0)
        def _():
            car_ref[...] = jnp.zeros_like(car_ref)
            for k, t in enumerate(_scan_tables(*_cpow2(ar_ref[...], -ai_ref[...], run.bit_length() - 1), True)):
                tab_ref[k] = t
            for r in (dab_ref, dd_ref, dbr_ref, dbi_ref, dcr_ref, dci_ref):
                r[...] = jnp.zeros_like(r)

        _runs_load(dys_ref, dyp_ref, run)
        _runs_load(us_ref, up_ref, run)
        dyb = dyp_ref[...].astype(MXU)
        gr_ref[...] = _dot(dyb, cr_ref[0])
        gi_ref[...] = -_dot(dyb, ci_ref[0])
        ar = jnp.broadcast_to(ar_ref[...], (8, w))
        ai = jnp.broadcast_to(-ai_ref[...], (8, w))

        def advance(kk, state):
            r0 = pl.multiple_of((run - 1 - kk) * 8, 8)
            gr, gi = state
            return (ar * gr - ai * gi + gr_ref[pl.ds(r0, 8), :], ar * gi + ai * gr + gi_ref[pl.ds(r0, 8), :])

        def emit(kk, state):
            r0 = pl.multiple_of((run - 1 - kk) * 8, 8)
            gr, gi = advance(kk, state)
            gr_ref[pl.ds(r0, 8), :] = gr
            gi_ref[pl.ds(r0, 8), :] = gi
            return gr, gi

        zero = jnp.zeros((8, w), F32)
        er, ei = lax.fori_loop(0, run, advance, (zero, zero))
        cr, ci = car_ref[0:1, :], car_ref[1:2, :]
        tr, ti = _scan_group(er, ei, tab_ref, cr, ci, True)
        r8 = lax.broadcasted_iota(jnp.int32, (8, w), 0)
        start = (jnp.where(r8 == 7, cr, pltpu.roll(tr, 7, 0)), jnp.where(r8 == 7, ci, pltpu.roll(ti, 7, 0)))
        car_ref[0:1, :] = tr[0:1, :]
        car_ref[1:2, :] = ti[0:1, :]
        lax.fori_loop(0, run, emit, start)

        gsr = gr_ref[...]
        gsi = gi_ref[...]
        sr = str_ref[...]
        si = sti_ref[...]
        first = ri == 0

        def previous(s, halo_ref):
            head = jnp.where(r8 == 0, jnp.where(first, 0.0, halo_ref[7:8, :]), pltpu.roll(s[tm - 8:tm, :], 1, 0))
            return jnp.concatenate([head, s[0:tm - 8, :]], axis=0)

        spr = previous(sr, hr_ref)
        spi = previous(si, hi_ref)
        dab_ref[0, 0:1, :] += _rowsum(gsr * spr + gsi * spi)
        dab_ref[0, 1:2, :] += _rowsum(gsi * spr - gsr * spi)

        gbr = gsr.astype(MXU)
        gbi = gsi.astype(MXU)
        _runs_store(_dot_nt(gbr, br_ref[0]) + _dot_nt(gbi, bi_ref[0]), dun_ref, run)
        dys_v = dys_ref[...]
        dus_ref[...] = (dun_ref[...] + d_ref[...] * dys_v).astype(MXU)
        dd_ref[0, 0:1, :] += _rowsum(dys_v * us_ref[...])
        ub = up_ref[...].astype(MXU)
        dbr_ref[0] += _dot_tn(ub, gbr)
        dbi_ref[0] += _dot_tn(ub, gbi)
        dcr_ref[0] += _dot_tn(dyb, sr.astype(MXU))
        dci_ref[0] -= _dot_tn(dyb, si.astype(MXU))

    blk = lambda: pl.BlockSpec((1, 8 * SSM_H, w), lambda j, i: (j, 0, 0))
    rowl = lambda: pl.BlockSpec((tm, LANES), lambda j, i: (nt - 1 - i, j))
    roww = lambda: pl.BlockSpec((tm, w), lambda j, i: (nt - 1 - i, j))
    halo = lambda: pl.BlockSpec((8, w), lambda j, i: (jnp.maximum((nt - 1 - i) * hb - 1, 0), j))
    return pl.pallas_call(
        body, name="s5_bwd", grid=(SSM_BLK, nt),
        in_specs=[rowl(), rowl(), roww(), roww(), halo(), halo(),
                  pl.BlockSpec((1, w), lambda j, i: (0, j)), pl.BlockSpec((1, w), lambda j, i: (0, j)),
                  blk(), blk(), blk(), blk(),
                  pl.BlockSpec((1, LANES), lambda j, i: (0, j))],
        out_specs=[rowl(),
                   pl.BlockSpec((1, 8, w), lambda j, i: (j, 0, 0)), pl.BlockSpec((1, 8, LANES), lambda j, i: (j, 0, 0)),
                   blk(), blk(), blk(), blk()],
        out_shape=[_sds((S, SSM_W), MXU), _sds((SSM_BLK, 8, w)), _sds((SSM_BLK, 8, LANES)),
                   _sds((SSM_BLK, 8 * SSM_H, w)), _sds((SSM_BLK, 8 * SSM_H, w)),
                   _sds((SSM_BLK, 8 * SSM_H, w)), _sds((SSM_BLK, 8 * SSM_H, w))],
        scratch_shapes=[pltpu.VMEM((8, 8, w), F32), pltpu.VMEM((8, w), F32),
                        pltpu.VMEM((tm, w), F32), pltpu.VMEM((tm, w), F32),
                        pltpu.VMEM((tm, LANES), F32), pltpu.VMEM((tm, LANES), F32), pltpu.VMEM((tm, LANES), F32)],
        compiler_params=_cp("parallel", "arbitrary"),
    )(*_in_hbm([dys, us, st_re, st_im, st_re, st_im, abar_re, abar_im, b_re, b_im, c_re, c_im, d_skip]))


def _in_bwd(dus, duv, dgl, dx1, x, g_mix, w_in, tm):
    S = x.shape[0]

    def body(dus_ref, duv_ref, dgl_ref, dx1_ref, x_ref, g_ref, w_ref, gx_ref, dg_ref):
        @pl.when(pl.program_id(0) == 0)
        def _():
            dg_ref[...] = jnp.zeros_like(dg_ref)

        dh = (_dot(dus_ref[...], w_ref[0:SSM_W, :])
              + _dot(duv_ref[...], w_ref[SSM_W:SSM_W + 2 * SGU_W, :])
              + _dot(dgl_ref[...], w_ref[SSM_W + 2 * SGU_W:, :]))
        xv = x_ref[...]
        r = _rms(xv)
        xn = xv * r
        dg_ref[...] += _rowsum(dh * xn)
        gx_ref[...] = dx1_ref[...] + _rms_bwd(dh * g_ref[...], xn, r)

    row = lambda n: pl.BlockSpec((tm, n), lambda i: (i, 0))
    return pl.pallas_call(
        body, name="in_bwd", grid=(S // tm,),
        in_specs=[row(SSM_W), row(2 * SGU_W), row(2 * D_MODEL), row(D_MODEL), row(D_MODEL), _full((1, D_MODEL)),
                  _full(w_in.shape)],
        out_specs=[row(D_MODEL), _full((1, D_MODEL))],
        out_shape=[_sds((S, D_MODEL)), _sds((1, D_MODEL))],
        compiler_params=_cp("arbitrary"),
    )(*_in_hbm([dus, duv, dgl, dx1, x, g_mix, w_in]))


def _pick(n, cands):
    for c in cands:
        if n % c == 0:
            return c
    return n


def _wgrad_split(a, b, nsplit, tk, name):
    S, K = a.shape
    N = b.shape[1]
    c = N // nsplit

    def body(a_ref, b_ref, o_ref):
        prod = _dot_tn(a_ref[...], b_ref[...])
        for d in range(nsplit):
            o_ref[d] = prod[:, c * d:c * (d + 1)].astype(MXU)

    return pl.pallas_call(
        body, name=name, grid=(K // tk,),
        in_specs=[pl.BlockSpec((S, tk), lambda k: (0, k)), _full((S, N))],
        out_specs=pl.BlockSpec((nsplit, tk, c), lambda k: (0, k, 0)),
        out_shape=_sds((nsplit, K, c), MXU),
        compiler_params=_cp("parallel"),
    )(*_in_hbm([a, b]))


def _wgrad_in_t(dps, h1, name, after=()):
    S, K = h1.shape
    cw = 512
    counts = [b.shape[1] // cw for b in dps]
    starts = [sum(counts[:i]) for i in range(len(dps))]
    nblk = sum(counts)

    def body(*refs):
        b_refs = refs[:len(dps)]
        h_ref, o_ref = refs[len(dps)], refs[-1]
        j = pl.program_id(0)
        for b_ref, st, cnt in zip(b_refs, starts, counts):
            @pl.when(jnp.logical_and(j >= st, j < st + cnt))
            def _():
                o_ref[...] = _dot_tn(b_ref[...], h_ref[...]).astype(MXU)

    def src_spec(st, cnt):
        return pl.BlockSpec((S, cw), lambda j: (0, jnp.clip(j - st, 0, cnt - 1)))

    return pl.pallas_call(
        body, name=name, grid=(nblk,),
        in_specs=[src_spec(st, cnt) for st, cnt in zip(starts, counts)] + [_full((S, K))] + [_ANY] * len(after),
        out_specs=pl.BlockSpec((cw, K), lambda j: (j, 0)),
        out_shape=_sds((nblk * cw, K), MXU),
        compiler_params=_cp("arbitrary"),
    )(*_in_hbm([*dps, h1]), *after)


def _wgrad_blk(a3, b3, nblk, a_of, b_of, name):
    S, K = a3.shape[1:]
    N = b3.shape[2]

    def body(a_ref, b_ref, o_ref):
        o_ref[0] = _dot_tn(a_ref[0], b_ref[0]).astype(MXU)

    return pl.pallas_call(
        body, name=name, grid=(nblk,),
        in_specs=[pl.BlockSpec((1, S, K), lambda b: (a_of(b), 0, 0)),
                  pl.BlockSpec((1, S, N), lambda b: (b_of(b), 0, 0))],
        out_specs=pl.BlockSpec((1, K, N), lambda b: (b, 0, 0)),
        out_shape=_sds((nblk, K, N), MXU),
        compiler_params=pltpu.CompilerParams(dimension_semantics=("parallel",), vmem_limit_bytes=WGRAD_VMEM_LIMIT),
    )(*_in_hbm([a3, b3]))


def _assemble_cols(blocks_list, name):
    def body(*refs):
        n = len(blocks_list)
        for b_ref, o_ref in zip(refs[:n], refs[n:]):
            c = b_ref.shape[2]
            for d in range(N_DEV):
                o_ref[:, c * d:c * (d + 1)] = b_ref[d]

    outs = [_sds((b.shape[1], N_DEV * b.shape[2]), b.dtype) for b in blocks_list]
    return pl.pallas_call(
        body, name=name, grid=(1,), in_specs=[_full(b.shape) for b in blocks_list],
        out_specs=[_full(o.shape) for o in outs], out_shape=outs, compiler_params=_cp("arbitrary"),
    )(*_in_hbm(blocks_list))


def _tile(S, want):
    return want if S % want == 0 else S


def _local_step(x, tgt, p, mixer_relay, mixer_weights, ffn_weights, grads_out, small_out):
    S = x.shape[0]
    tm = _tile(S, 256)
    tl = _tile(S, 512)

    rep = lambda a: jnp.repeat(a, SSM_H, axis=0)
    are = rep(p["a_re"])
    aim = rep(p["a_im"])
    ldt = jnp.broadcast_to(rep(p["log_dt"].reshape(SSM_G, 1)), are.shape)
    br_t = p["b_re_t"].reshape(are.shape)
    bi_t = p["b_im_t"].reshape(are.shape)
    abr, abi, bbr, bbi = _s5_params_fwd(are, aim, ldt, br_t, bi_t)
    head = lambda a: a.reshape(SSM_G, SSM_H, SSM_P)[:, 0, :].reshape(1, SSM_G * SSM_P)
    abar_re, abar_im = head(abr), head(abi)
    bd_br = _blockdiag(bbr).astype(MXU)
    bd_bi = _blockdiag(bbi).astype(MXU)
    bd_cr = _blockdiag(p["c_re"].reshape(are.shape)).astype(MXU)
    bd_ci = _blockdiag(p["c_im"].reshape(are.shape)).astype(MXU)
    d_skip = p["d_skip"].reshape(1, SSM_W)

    tril = jnp.tril(jnp.ones((CHUNK, CHUNK), dtype=bool))
    ws = jnp.where(tril[None], p["w_s"], 0.0)
    pair = lambda w: w.reshape(SGU_G // 2, 2, CHUNK, CHUNK).transpose(0, 2, 1, 3).reshape(SGU_G // 2, CHUNK, 2 * CHUNK)
    ws_b = pair(ws).astype(MXU)
    ws_t = pair(ws.transpose(0, 2, 1)).astype(MXU)
    bias_s = jnp.repeat(p["b_s"].T, SGU_D, axis=1)

    g_mix = p["g_mix"].reshape(1, D_MODEL)
    g_ffn = p["g_ffn"].reshape(1, D_MODEL)
    g_final = p["g_final"].reshape(1, D_MODEL)
    g_sgu = p["g_sgu"].reshape(1, SGU_W)
    b_glu = p["b_glu"].reshape(1, SSM_W)
    conv_b = p["conv_b"].reshape(2 * FF_NCB, 1, FF_CW)
    tf = _tile(S, 256)
    ts = _tile(S, 1024)

    h1, us, uv, gl = _in_fwd(x, g_mix, p["w_in_t"], tl)
    token = mixer_relay(us)
    st_re, st_im, ys = _s5_fwd(us, abar_re, abar_im, bd_br, bd_bi, bd_cr, bd_ci, d_skip + token[0:1, 0:1], ts)
    p = dict(p, **mixer_weights(ys))
    yg, yap, sg, ya, yb, m, x1, h2 = _mix_fwd(x, ys, uv, gl, p["w_glu"], b_glu, p["w_proj_a"], g_sgu, ws_b, bias_s,
                                              p["w_proj_b"], p["w_out"], g_ffn, tl)
    w_up, conv_w, w_down = ffn_weights(h2)
    pair_lanes = lambda a: a.reshape(N_DEV // 2, 2, a.shape[1], FF_SHARD).transpose(0, 2, 1, 3).reshape(
        N_DEV // 2, a.shape[1], FF_CW)
    w_up = w_up.reshape(2 * FF_NCB, FF_CW, D_MODEL)
    conv_w = pair_lanes(conv_w)
    up, ab, ff, dx2, dx2b, loss, dg_final = _ffn_fwd(h2, x1, tgt, w_up, conv_w, conv_b, w_down, g_final, tf)

    dup, dx1, dx1b, dconv, dg_ffn = _ffn_bwd(dx2, up, ab, x1, w_up, conv_w, w_down, g_ffn, tf)
    rows8 = lambda g: g.reshape(N_DEV, g.shape[1] // N_DEV, g.shape[2])
    g_up = _wgrad_blk(dup.reshape(2 * FF_NCB, S, FF_CW), h2[None], 2 * FF_NCB, lambda b: b, lambda b: 0,
                      "wgrad_up").reshape(N_DEV, FF_SHARD, D_MODEL)
    g_down = _wgrad_blk(ff, dx2b[None], FF_NCB, lambda b: b, lambda b: 0, "wgrad_down").reshape(
        N_DEV, D_FF // N_DEV, D_MODEL)
    token = grads_out(("w_up", "w_down"), (g_up, g_down))
    dgl, dya, dyb, dz, dys, duv, db_glu, dg_sgu, dws, dbs = _mix_bwd(
        dx1, gl, ya, yb, ys, uv, p["w_out"], p["w_proj_a"], p["w_proj_b"], p["w_glu"], b_glu + token[0:1, 0:1], g_sgu,
        ws_b, ws_t, bias_s, tm)
    token = grads_out(("w_glu", "w_proj_a", "w_proj_b", "w_out"),
                      (rows8(_wgrad_split(yg, dz, 1, SSM_W, "wgrad_glu")),
                       _wgrad_split(yap, dya, N_DEV, SSM_W, "wgrad_pa"),
                       _wgrad_split(sg, dyb, N_DEV, SGU_W, "wgrad_pb"),
                       rows8(_wgrad_split(m, dx1b, 1, 512, "wgrad_out"))))
    dus, dab, dd, dbbr, dbbi, dcr, dci = _s5_bwd(dys, us, st_re, st_im, abar_re, abar_im, bd_br, bd_bi, bd_cr, bd_ci,
                                                 d_skip + token[0:1, 0:1], ts)
    g_in = _wgrad_in_t([dus, duv, dgl], h1, "wgrad_in")
    token = grads_out(("w_in",), (g_in.reshape(N_DEV, g_in.shape[0] // N_DEV, D_MODEL),))
    grad_x, dg_mix = _in_bwd(dus, duv, dgl, dx1, x, g_mix + token[0:1, 0:1], p["w_in_t"], tl)

    spread = lambda v: jnp.repeat(v.reshape(SSM_G, SSM_P), SSM_H, axis=0) * (1.0 / SSM_H)
    dabr = spread(dab[:, 0, :])
    dabi = spread(dab[:, 1, :])
    dare, daim, dldt, dbr_t, dbi_t = _s5_params_bwd(are, aim, ldt, br_t, bi_t, dabr, dabi,
                                                    _unblockdiag(dbbr), _unblockdiag(dbbi))
    fold = lambda a: a.reshape(SSM_G, SSM_H, SSM_P).sum(axis=1)

    grads = {
        "g_mix": dg_mix,
        "a_re": fold(dare), "a_im": fold(daim), "log_dt": fold(dldt).sum(axis=1),
        "b_re": dbr_t, "b_im": dbi_t,
        "c_re": _unblockdiag(dcr).reshape(SSM_G, SSM_H, SSM_P),
        "c_im": _unblockdiag(dci).reshape(SSM_G, SSM_H, SSM_P),
        "d_skip": dd[:, 0, :].reshape(SSM_W),
        "b_glu": db_glu,
        "g_sgu": dg_sgu,
        "w_s": dws,
        "b_s": dbs.reshape(CHUNK, SGU_G, SGU_D).sum(axis=-1).T,
        "g_ffn": dg_ffn,
        "conv_w": dconv[:, 0:3, :].reshape(N_DEV // 2, 3, 2, FF_SHARD).transpose(0, 2, 1, 3).reshape(
            N_DEV, 3, FF_SHARD),
        "conv_b": dconv[:, 3, :].reshape(2 * D_FF),
        "g_final": dg_final,
    }
    small_out(grads, loss)
    return grad_x


_ANY = pl.BlockSpec(memory_space=pl.ANY)
_MESH = pl.DeviceIdType.MESH


def _allgather(shards, dtypes, name, cast_only=()):
    n = len(shards)
    e = len(cast_only)

    def body(*refs):
        in_refs, extra_in = refs[:n], refs[n:n + e]
        out_refs, extra_out = refs[n + e:2 * n + e], refs[2 * n + e:2 * n + 2 * e]
        stage = refs[2 * n + 2 * e:3 * n + 2 * e]
        send_sems, recv_sems, local_sems = refs[3 * n + 2 * e:]
        for a in range(n):
            stage[a][...] = in_refs[a][...].astype(dtypes[a])
        for i in range(e):
            extra_out[i][...] = extra_in[i][...].astype(MXU)
        x, y, c = lax.axis_index("x"), lax.axis_index("y"), lax.axis_index("c")
        me, sibling = (x, y, c), (x, y, 1 - c)
        chips = [(1 - x, y), (x, 1 - y), (1 - x, 1 - y)]

        def slot(a, px, py, pc):
            return out_refs[a].at[4 * px + 2 * py + pc]

        def copy(a, k, block, to, src=None):
            return pltpu.make_async_remote_copy(
                src_ref=slot(a, *block) if src is None else src, dst_ref=slot(a, *block),
                send_sem=send_sems.at[a, k], recv_sem=recv_sems.at[a, k], device_id=to, device_id_type=_MESH)

        mine = [pltpu.make_async_copy(stage[a], slot(a, *me), local_sems.at[a]) for a in range(n)]
        for cp in mine:
            cp.start()
        first = []
        for j, chip in enumerate(chips):
            first += [copy(a, 1 + j, me, (*chip, c), src=stage[a]) for a in range(n)]
        first += [copy(a, 0, me, sibling, src=stage[a]) for a in range(n)]
        for cp in first:
            cp.start()
        passed = []
        for j, chip in enumerate(chips):
            for a in range(n):
                copy(a, 1 + j, (*chip, c), me).wait_recv()
                fwd = copy(a, 4 + j, (*chip, c), sibling)
                fwd.start()
                passed.append(fwd)
        for a in range(n):
            copy(a, 0, sibling, me).wait_recv()
        for j, chip in enumerate(chips):
            for a in range(n):
                copy(a, 4 + j, (*chip, 1 - c), me).wait_recv()
        for cp in first + passed:
            cp.wait_send()
        for cp in mine:
            cp.wait()

    res = pl.pallas_call(
        body, name=name, grid=(1,), in_specs=[_full(s.shape) for s in list(shards) + list(cast_only)],
        out_specs=[_ANY] * n + [_full(s.shape) for s in cast_only],
        out_shape=[_sds((N_DEV,) + s.shape, dt) for s, dt in zip(shards, dtypes)]
                  + [_sds(s.shape, MXU) for s in cast_only],
        scratch_shapes=[pltpu.VMEM(s.shape, dt) for s, dt in zip(shards, dtypes)]
                       + [pltpu.SemaphoreType.DMA((n, 7)), pltpu.SemaphoreType.DMA((n, 7)), pltpu.SemaphoreType.DMA((n,))],
        compiler_params=pltpu.CompilerParams(vmem_limit_bytes=VMEM_LIMIT),
    )(*_in_hbm([*shards, *cast_only]))
    return res[:n], res[n:]


def _all_to_all(sends, name):
    n = len(sends)

    def body(*refs):
        send_refs, recv_refs = refs[:n], refs[n:2 * n]
        send_sems, recv_sems, local_sems = refs[2 * n:]
        x, y, c = lax.axis_index("x"), lax.axis_index("y"), lax.axis_index("c")
        me = 4 * x + 2 * y + c
        mine = [pltpu.make_async_copy(send_refs[a].at[me], recv_refs[a].at[me], local_sems.at[a]) for a in range(n)]
        for cp in mine:
            cp.start()
        copies = []
        for k in (2, 4, 6, 3, 5, 7, 1):
            px = 1 - x if k & 4 else x
            py = 1 - y if k & 2 else y
            pc = 1 - c if k & 1 else c
            peer = 4 * px + 2 * py + pc
            for a in range(n):
                sems = dict(send_sem=send_sems.at[a, k - 1], recv_sem=recv_sems.at[a, k - 1],
                            device_id=(px, py, pc), device_id_type=_MESH)
                cp = pltpu.make_async_remote_copy(src_ref=send_refs[a].at[peer], dst_ref=recv_refs[a].at[me], **sems)
                cp.start()
                landing = pltpu.make_async_remote_copy(src_ref=send_refs[a].at[peer], dst_ref=recv_refs[a].at[peer],
                                                       **sems)
                copies.append((cp, landing))
        for _, landing in copies:
            landing.wait_recv()
        for cp, _ in copies:
            cp.wait_send()
        for cp in mine:
            cp.wait()

    return pl.pallas_call(
        body, name=name, in_specs=[_ANY] * n, out_specs=[_ANY] * n,
        out_shape=[_sds(s.shape, s.dtype) for s in sends],
        scratch_shapes=[pltpu.SemaphoreType.DMA((n, 7)), pltpu.SemaphoreType.DMA((n, 7)), pltpu.SemaphoreType.DMA((n,))],
    )(*sends)


_HBM = pl.BlockSpec(memory_space=pltpu.HBM)
_SEM = pl.BlockSpec(memory_space=pltpu.SEMAPHORE)
_EFFECT = pltpu.SideEffectType.DATAFLOW_SIDE_EFFECTING
_PEER_ORDER = (2, 4, 6, 3, 5, 7, 1)


def _peer(k):
    x, y, c = lax.axis_index("x"), lax.axis_index("y"), lax.axis_index("c")
    px = 1 - x if k & 4 else x
    py = 1 - y if k & 2 else y
    pc = 1 - c if k & 1 else c
    return (px, py, pc), 4 * px + 2 * py + pc


_SAME_CORE_AND_SIBLING = (2, 4, 6, 1)


def _push_start(srcs, lands, slotted, name, peers=_PEER_ORDER, after=()):
    n = len(srcs)
    e = len(after)

    def body(*refs):
        src_refs, land_refs = refs[:n], refs[n:2 * n]
        send_sems, recv_sems, token = refs[2 * n + e], refs[2 * n + e + 1], refs[-1]
        me = 4 * lax.axis_index("x") + 2 * lax.axis_index("y") + lax.axis_index("c")
        for k in peers:
            dev, peer = _peer(k)
            for a in range(n):
                pltpu.make_async_remote_copy(
                    src_ref=src_refs[a].at[peer] if slotted else src_refs[a], dst_ref=land_refs[a].at[me],
                    send_sem=send_sems.at[7 * a + k - 1], recv_sem=recv_sems.at[7 * a + k - 1],
                    device_id=dev, device_id_type=_MESH).start()
        token[...] = jnp.zeros_like(token)

    bufs = list(srcs) + list(lands)
    res = pl.pallas_call(
        body, name=name, in_specs=[_HBM] * (2 * n) + [_ANY] * e,
        out_specs=(_SEM, _SEM, *[_HBM] * (2 * n), pl.BlockSpec(memory_space=pltpu.VMEM)),
        out_shape=(pltpu.SemaphoreType.DMA((7 * n,)), pltpu.SemaphoreType.DMA((7 * n,)),
                   *[pltpu.HBM(b.shape, b.dtype) for b in bufs], _sds((8, LANES))),
        input_output_aliases={i: 2 + i for i in range(2 * n)},
        compiler_params=pltpu.CompilerParams(has_side_effects=_EFFECT),
    )(*[pltpu.with_memory_space_constraint(b, pltpu.HBM) for b in bufs], *after)
    return res[0], res[1], res[2:2 + n], res[2 + n:2 + 2 * n], res[-1]


def _push_wait(send_sems, recv_sems, srcs, lands, slotted, after, name, peers=_PEER_ORDER):
    n = len(srcs)

    def body(*refs):
        src_refs, land_refs = refs[:n], refs[n:2 * n]
        send_sems, recv_sems = refs[2 * n], refs[2 * n + 1]
        for k in peers:
            dev, peer = _peer(k)
            for a in range(n):
                cp = pltpu.make_async_remote_copy(
                    src_ref=src_refs[a].at[peer] if slotted else src_refs[a], dst_ref=land_refs[a].at[peer],
                    send_sem=send_sems.at[7 * a + k - 1], recv_sem=recv_sems.at[7 * a + k - 1],
                    device_id=dev, device_id_type=_MESH)
                cp.wait_send()
                cp.wait_recv()

    bufs = list(srcs) + list(lands)
    res = pl.pallas_call(
        body, name=name, in_specs=[_HBM] * (2 * n) + [_SEM, _SEM] + [_ANY] * len(after), out_specs=[_HBM] * (2 * n),
        out_shape=[pltpu.HBM(b.shape, b.dtype) for b in bufs],
        input_output_aliases={i: i for i in range(2 * n)},
        compiler_params=pltpu.CompilerParams(has_side_effects=_EFFECT),
    )(*bufs, send_sems, recv_sems, *after)
    return res[n:]


def _other_chips():
    x, y = lax.axis_index("x"), lax.axis_index("y")
    return ((1 - x, y), (x, 1 - y), (1 - x, 1 - y))


def _relay_start(lands, name):
    n = len(lands)

    def body(*refs):
        land_refs = refs[:n]
        send_sems, recv_sems, token = refs[n], refs[n + 1], refs[-1]
        x, y, c = lax.axis_index("x"), lax.axis_index("y"), lax.axis_index("c")
        for j, (px, py) in enumerate(_other_chips()):
            slot = 4 * px + 2 * py + c
            for a in range(n):
                pltpu.make_async_remote_copy(
                    src_ref=land_refs[a].at[slot], dst_ref=land_refs[a].at[slot],
                    send_sem=send_sems.at[3 * a + j], recv_sem=recv_sems.at[3 * a + j],
                    device_id=(x, y, 1 - c), device_id_type=_MESH).start()
        token[...] = jnp.zeros_like(token)

    res = pl.pallas_call(
        body, name=name, in_specs=[_HBM] * n,
        out_specs=(_SEM, _SEM, *[_HBM] * n, pl.BlockSpec(memory_space=pltpu.VMEM)),
        out_shape=(pltpu.SemaphoreType.DMA((3 * n,)), pltpu.SemaphoreType.DMA((3 * n,)),
                   *[pltpu.HBM(b.shape, b.dtype) for b in lands], _sds((8, LANES))),
        input_output_aliases={i: 2 + i for i in range(n)},
        compiler_params=pltpu.CompilerParams(has_side_effects=_EFFECT),
    )(*[pltpu.with_memory_space_constraint(b, pltpu.HBM) for b in lands])
    return res[0], res[1], res[2:2 + n], res[-1]


def _relay_wait(send_sems, recv_sems, lands, after, name):
    n = len(lands)

    def body(*refs):
        land_refs = refs[:n]
        send_sems, recv_sems = refs[n], refs[n + 1]
        x, y, c = lax.axis_index("x"), lax.axis_index("y"), lax.axis_index("c")
        for j, (px, py) in enumerate(_other_chips()):
            sent, received = 4 * px + 2 * py + c, 4 * px + 2 * py + (1 - c)
            for a in range(n):
                cp = pltpu.make_async_remote_copy(
                    src_ref=land_refs[a].at[sent], dst_ref=land_refs[a].at[received],
                    send_sem=send_sems.at[3 * a + j], recv_sem=recv_sems.at[3 * a + j],
                    device_id=(x, y, 1 - c), device_id_type=_MESH)
                cp.wait_send()
                cp.wait_recv()

    return pl.pallas_call(
        body, name=name, in_specs=[_HBM] * n + [_SEM, _SEM] + [_ANY] * len(after), out_specs=[_HBM] * n,
        out_shape=[pltpu.HBM(b.shape, b.dtype) for b in lands],
        input_output_aliases={i: i for i in range(n)},
        compiler_params=pltpu.CompilerParams(has_side_effects=_EFFECT),
    )(*lands, send_sems, recv_sems, *after)


def _adamw(w, g, m, v):
    m2 = ADAM_B1 * m + (1.0 - ADAM_B1) * g
    v2 = ADAM_B2 * v + (1.0 - ADAM_B2) * (g * g)
    m_hat = m2 / (1.0 - ADAM_B1 ** ADAM_STEP)
    v_hat = v2 / (1.0 - ADAM_B2 ** ADAM_STEP)
    delta = -ADAM_LR * (m_hat / (jnp.sqrt(v_hat) + ADAM_EPS) + ADAM_WD * w)
    return delta, m2, v2


def _adam_shard(parts, w, m, v, name):
    _, r, c = w.shape
    tr = max(t for t in range(16, 257, 16) if r % t == 0)

    def body(p_ref, w_ref, m_ref, v_ref, g_ref, d_ref, m2_ref, v2_ref):
        g = p_ref[0].astype(F32)
        for s in range(1, N_DEV):
            g = g + p_ref[s].astype(F32)
        g_ref[0] = g
        d_ref[0], m2_ref[0], v2_ref[0] = _adamw(w_ref[0], g, m_ref[0], v_ref[0])

    row = lambda: pl.BlockSpec((1, tr, c), lambda i: (0, i, 0))
    return pl.pallas_call(
        body, name=name, grid=(r // tr,),
        in_specs=[pl.BlockSpec((N_DEV, tr, c), lambda i: (0, i, 0)), row(), row(), row()],
        out_specs=[row(), row(), row(), row()], out_shape=[_sds((1, r, c))] * 4,
        compiler_params=_cp("parallel"),
    )(*_in_hbm([parts, w, m, v]))


def _adam_small(gs, ws, ms, vs, name):
    n = len(gs)

    def body(*refs):
        ins, outs = refs[:4 * n], refs[4 * n:]
        for i in range(n):
            g = ins[i][...]
            d, m2, v2 = _adamw(ins[n + i][...], g, ins[2 * n + i][...], ins[3 * n + i][...])
            outs[i][...] = d
            outs[n + i][...] = m2
            outs[2 * n + i][...] = v2

    res = pl.pallas_call(
        body, name=name, grid=(1,), in_specs=[_full(w.shape) for w in ws] * 4,
        out_specs=[_full(w.shape) for w in ws] * 3, out_shape=[_sds(w.shape) for w in ws] * 3,
        compiler_params=_cp("arbitrary"),
    )(*_in_hbm([*gs, *ws, *ms, *vs]))
    return res[:n], res[n:2 * n], res[2 * n:]


def _sum_slots(parts, name):
    R = parts.shape[1]

    def body(p_ref, o_ref):
        g = p_ref[0]
        for s in range(1, N_DEV):
            g = g + p_ref[s]
        o_ref[...] = g

    return pl.pallas_call(body, name=name, grid=(1,), in_specs=[_full(parts.shape)], out_specs=_full((R, LANES)),
                          out_shape=_sds((R, LANES)))(*_in_hbm([parts]))


def _pad_to(a, n, axis):
    extra = n - a.shape[axis]
    if extra == 0:
        return a
    widths = [(0, 0)] * a.ndim
    widths[axis] = (0, extra)
    return jnp.pad(a, widths)


def _ceil_to(n, k):
    return -(-n // k) * k


def _pack_rows(flats, rows_multiple):
    parts = [_pad_to(f, _ceil_to(f.shape[-1], LANES), f.ndim - 1) for f in flats]
    cat = jnp.concatenate(parts, axis=-1)
    total = _ceil_to(cat.shape[-1], LANES * rows_multiple)
    cat = _pad_to(cat, total, cat.ndim - 1)
    return cat.reshape(cat.shape[:-1] + (total // LANES, LANES))


def _unpack_rows(buf, sizes):
    flat = buf.reshape(buf.shape[:-2] + (-1,))
    out, off = [], 0
    for n in sizes:
        out.append(flat[..., off:off + n])
        off += _ceil_to(n, LANES)
    return out


_MIX_BIG = ("w_in", "w_glu", "w_proj_a", "w_proj_b", "w_out")
_BIG = _MIX_BIG + ("w_up", "w_down")
_SMALL = ("g_mix", "a_re", "a_im", "log_dt", "b_re", "b_im", "c_re", "c_im", "d_skip", "b_glu", "g_sgu", "w_s", "b_s",
          "g_ffn", "conv_b", "g_final")
_SMALL_ROWS_MULTIPLE = 8 * N_DEV
_TRANSPOSED = ("w_in", "w_up", "b_re", "b_im")


def _as_2d(a):
    return a.reshape(-1, a.shape[-1]) if a.ndim > 1 else a.reshape(1, -1)


def kernel(x, g_mix, w_in, a_re, a_im, log_dt, b_re, b_im, c_re, c_im, d_skip, w_glu, b_glu, w_proj_a, g_sgu, w_s, b_s, w_proj_b, w_out, g_ffn, w_up, conv_w, conv_b, w_down, g_final, loss_target, m_g_mix, m_w_in, m_a_re, m_a_im, m_log_dt, m_b_re, m_b_im, m_c_re, m_c_im, m_d_skip, m_w_glu, m_b_glu, m_w_proj_a, m_g_sgu, m_w_s, m_b_s, m_w_proj_b, m_w_out, m_g_ffn, m_w_up, m_conv_w, m_conv_b, m_w_down, m_g_final, v_g_mix, v_w_in, v_a_re, v_a_im, v_log_dt, v_b_re, v_b_im, v_c_re, v_c_im, v_d_skip, v_w_glu, v_b_glu, v_w_proj_a, v_g_sgu, v_w_s, v_b_s, v_w_proj_b, v_w_out, v_g_ffn, v_w_up, v_conv_w, v_conv_b, v_w_down, v_g_final):
    args = dict(locals())
    me = 4 * lax.axis_index("x") + 2 * lax.axis_index("y") + lax.axis_index("c")

    def own_slot(buf, block):
        return lax.dynamic_update_slice(buf, block[None], (me,) + (0,) * block.ndim)

    for n in _TRANSPOSED:
        for pre in ("", "m_", "v_"):
            args[pre + n] = jnp.swapaxes(args[pre + n], -1, -2)
    later = ("w_glu", "w_proj_a", "w_proj_b", "w_out", "w_up", "w_down")
    (w_in_g,), casts = _allgather([args["w_in"][0]], [MXU], "allgather_w_in", cast_only=[args[n][0] for n in later])
    sh = dict(zip(later, casts))

    def start_push(srcs, tag, peers):
        lands = [own_slot(lax.empty((N_DEV,) + s.shape, s.dtype), s) for s in srcs]
        send_sems, recv_sems, srcs, lands, token = _push_start(srcs, lands, False, "push_" + tag, peers)
        return (send_sems, recv_sems, srcs, lands), token

    mix_push, token_a = start_push([sh[n] for n in later[:4]], "mixer_weights", _SAME_CORE_AND_SIBLING)
    ffn_push, token_b = start_push([sh["w_up"], sh["w_down"], conv_w[0]], "ffn_weights", _PEER_ORDER)
    p = {n: (args[n][0] if n != "g_final" else args[n]) for n in _SMALL if n not in _TRANSPOSED}
    p.update(w_in_t=w_in_g.reshape(SSM_W + 2 * SGU_W + 2 * D_MODEL, D_MODEL),
             b_re_t=args["b_re"][0], b_im_t=args["b_im"][0])
    p["g_mix"] = p["g_mix"] + (token_a[0:1, 0:1] + token_b[0:1, 0:1])
    relay = {}

    def mixer_relay(after):
        lands = _push_wait(*mix_push, False, [after], "wait_mixer_weights", _SAME_CORE_AND_SIBLING)
        relay["send"], relay["recv"], relay["lands"], token = _relay_start(lands, "relay_mixer_weights")
        return token

    def mixer_weights(after):
        w_glu_g, w_pa_g, w_pb_g, w_out_g = _relay_wait(relay["send"], relay["recv"], relay["lands"], [after],
                                                       "wait_relay_mixer_weights")
        w_pa_full, w_pb_full = _assemble_cols([w_pa_g, w_pb_g], "assemble_cols")
        return dict(w_glu=w_glu_g.reshape(SSM_W, SSM_W), w_proj_a=w_pa_full, w_proj_b=w_pb_full,
                    w_out=w_out_g.reshape(D_MODEL, D_MODEL))

    def ffn_weights(after):
        w_up_g, w_down_g, conv_w_g = _push_wait(*ffn_push, False, [after], "wait_ffn_weights")
        return w_up_g, conv_w_g, w_down_g.reshape(D_FF, D_MODEL)

    pushes = []

    def grads_out(names, sends, after=()):
        lands = [own_slot(lax.empty(s.shape, s.dtype), lax.dynamic_index_in_dim(s, me, 0, keepdims=False))
                 for s in sends]
        send_sems, recv_sems, srcs, lands, token = _push_start(list(sends), lands, True, "push_grads_" + names[0],
                                                               after=after)
        pushes.append((names, send_sems, recv_sems, srcs, lands))
        return token

    small_names = _SMALL + ("conv_w", "loss")
    small = {}

    def small_out(grads, loss_part):
        small_g = dict(grads, loss=loss_part[0, 0:1])
        flats = [small_g[n].reshape(-1) for n in small_names]
        small["sizes"] = [f.shape[0] for f in flats]
        g_small = _pack_rows(flats, _SMALL_ROWS_MULTIPLE)
        small["rs8"] = g_small.shape[0] // N_DEV
        return grads_out(("small",), (g_small.reshape(N_DEV, small["rs8"], LANES),))

    grad_x = _local_step(x[0], loss_target[0], p, mixer_relay, mixer_weights, ffn_weights, grads_out, small_out)

    out = {}
    done = [grad_x]
    for names, send_sems, recv_sems, srcs, lands in pushes:
        parts = _push_wait(send_sems, recv_sems, srcs, lands, True, done, "wait_grads_" + names[0])
        if names == ("small",):
            small_mine = _sum_slots(parts[0], "sum_small")
            g_small_all = _allgather([small_mine], [F32], "allgather_small")[0][0].reshape(N_DEV * small["rs8"], LANES)
            pieces = dict(zip(small_names, _unpack_rows(g_small_all, small["sizes"])))
            loss = pieces["loss"][0]
            dconv_w = lax.dynamic_index_in_dim(pieces["conv_w"].reshape(N_DEV, 3, FF_SHARD), me, axis=0, keepdims=False)
            names2 = _SMALL + ("conv_w",)
            gs = [pieces[n].reshape(_as_2d(args[n]).shape) for n in _SMALL] + [dconv_w]
            ds, m2s, v2s = _adam_small(gs, [_as_2d(args[n]) for n in names2], [_as_2d(args["m_" + n]) for n in names2],
                                       [_as_2d(args["v_" + n]) for n in names2], "adam_small")
            for n, res in zip(names2, zip(gs, ds, m2s, v2s)):
                for kind, v in zip(("grad_", "delta_", "new_m_", "new_v_"), res):
                    out[kind + n] = v.reshape(args[n].shape)
            done = [ds[0]]
            continue
        for n, part in zip(names, parts):
            res = _adam_shard(part, args[n], args["m_" + n], args["v_" + n], "adam_" + n)
            for kind, v in zip(("grad_", "delta_", "new_m_", "new_v_"), res):
                out[kind + n] = v
            done = [res[0]]
    order = ("g_mix", "w_in", "a_re", "a_im", "log_dt", "b_re", "b_im", "c_re", "c_im", "d_skip", "w_glu", "b_glu",
             "w_proj_a", "g_sgu", "w_s", "b_s", "w_proj_b", "w_out", "g_ffn", "w_up", "conv_w", "conv_b", "w_down",
             "g_final")
    res = [loss, grad_x.reshape(x.shape)]
    for kind in ("grad_", "delta_", "new_m_", "new_v_"):
        res += [jnp.swapaxes(out[kind + n], -1, -2) if n in _TRANSPOSED else out[kind + n] for n in order]
    return tuple(res)
```

```python
import functools
import math

import jax
import jax.numpy as jnp
from jax import lax
from jax.experimental import pallas as pl
from jax.experimental.pallas import tpu as pltpu

F32 = jnp.float32
MXU = jnp.bfloat16
EPS = 1e-6

D_MODEL = 1024
SSM_W = 512
SSM_G, SSM_H, SSM_P = 32, 16, 64
SSM_BLK = 4
SGU_W = 512
SGU_G, SGU_D, CHUNK = 8, 64, 128
D_FF = 2816
N_DEV = 8
FF_SHARD = 2 * D_FF // N_DEV
FF_CW = 2 * FF_SHARD
FF_NCB = D_FF // FF_CW
LANES = 128

ADAM_LR, ADAM_B1, ADAM_B2, ADAM_EPS, ADAM_WD, ADAM_STEP = 0.001, 0.9, 0.999, 1e-08, 0.01, 10

VMEM_LIMIT = 48 * 1024 * 1024
WGRAD_VMEM_LIMIT = 58 * 1024 * 1024
FFN_VMEM_LIMIT = 58 * 1024 * 1024


def _cp(*sem):
    return pltpu.CompilerParams(dimension_semantics=sem, vmem_limit_bytes=VMEM_LIMIT)


def _full(shape):
    n = len(shape)
    return pl.BlockSpec(shape, lambda *_: (0,) * n)


def _sds(shape, dtype=F32):
    return jax.ShapeDtypeStruct(shape, dtype)


def _in_hbm(arrays):
    return [pltpu.with_memory_space_constraint(a, pltpu.HBM) for a in arrays]


def _dot(a, b):
    return jnp.dot(a, b, preferred_element_type=F32)


def _dot_nt(a, b):
    return lax.dot_general(a, b, (((1,), (1,)), ((), ())), preferred_element_type=F32)


def _dot_tn(a, b):
    return lax.dot_general(a, b, (((0,), (0,)), ((), ())), preferred_element_type=F32)


_GELU_C = math.sqrt(2.0 / math.pi)


def _gelu(x):
    return 0.5 * x * (1.0 + jnp.tanh(_GELU_C * (x + 0.044715 * (x * x * x))))


def _gelu_and_grad(x):
    t = jnp.tanh(_GELU_C * (x + 0.044715 * (x * x * x)))
    g = 0.5 * x * (1.0 + t)
    dg = 0.5 * (1.0 + t) + 0.5 * x * (1.0 - t * t) * (_GELU_C * (1.0 + 3.0 * 0.044715 * (x * x)))
    return g, dg


def _sigmoid(x):
    return 0.5 * jnp.tanh(0.5 * x) + 0.5


def _rms(x):
    return lax.rsqrt(jnp.mean(x * x, axis=-1, keepdims=True) + EPS)


def _rms_bwd(dxn, xn, r):
    return r * (dxn - xn * jnp.mean(dxn * xn, axis=-1, keepdims=True))


def _rowsum(x):
    return jnp.sum(x, axis=0, keepdims=True)


def _fetch_once(pairs, sems):
    copies = [pltpu.make_async_copy(src, dst, sems.at[k]) for k, (src, dst) in enumerate(pairs)]
    for cp in copies:
        cp.start()
    for cp in copies:
        cp.wait()


def _s5_disc(are, aim, ldt, br, bi):
    dt = jnp.exp(ldt)
    mag = jnp.exp(dt * are)
    abr = mag * jnp.cos(dt * aim)
    abi = mag * jnp.sin(dt * aim)
    den = are * are + aim * aim
    nr = abr - 1.0
    ni = abi
    fr = (nr * are + ni * aim) / den
    fi = (ni * are - nr * aim) / den
    return abr, abi, fr * br - fi * bi, fr * bi + fi * br


def _s5_params_fwd(are, aim, ldt, br, bi):
    def body(are_ref, aim_ref, ldt_ref, br_ref, bi_ref, o0, o1, o2, o3):
        outs = _s5_disc(are_ref[...], aim_ref[...], ldt_ref[...], br_ref[...], bi_ref[...])
        for o, v in zip((o0, o1, o2, o3), outs):
            o[...] = v
    shp = are.shape
    return pl.pallas_call(body, name="s5_params_fwd", grid=(1,), in_specs=[_full(shp)] * 5, out_specs=[_full(shp)] * 4,
                          out_shape=[_sds(shp)] * 4)(*_in_hbm([are, aim, ldt, br, bi]))


def _s5_params_bwd(are, aim, ldt, br, bi, dabr, dabi, dbr, dbi):
    def body(are_ref, aim_ref, ldt_ref, br_ref, bi_ref, c0, c1, c2, c3, o0, o1, o2, o3, o4):
        prim = (are_ref[...], aim_ref[...], ldt_ref[...], br_ref[...], bi_ref[...])
        _, vjp = jax.vjp(_s5_disc, *prim)
        outs = vjp((c0[...], c1[...], c2[...], c3[...]))
        for o, v in zip((o0, o1, o2, o3, o4), outs):
            o[...] = v
    shp = are.shape
    return pl.pallas_call(body, name="s5_params_bwd", grid=(1,), in_specs=[_full(shp)] * 9, out_specs=[_full(shp)] * 5,
                          out_shape=[_sds(shp)] * 5)(*_in_hbm([are, aim, ldt, br, bi, dabr, dabi, dbr, dbi]))


def _blockdiag(m_t):
    m = m_t.reshape(SSM_BLK, 8, SSM_H, 1, SSM_P)
    eye = jnp.eye(8, dtype=bool).reshape(1, 8, 1, 8, 1)
    return jnp.where(eye, m, jnp.zeros((), m_t.dtype)).reshape(SSM_BLK, 8 * SSM_H, 8 * SSM_P)


def _unblockdiag(pc):
    m = pc.reshape(SSM_BLK, 8, SSM_H, 8, SSM_P)
    return jnp.einsum("jghgp->jghp", m).reshape(SSM_G * SSM_H, SSM_P)


def _in_fwd(x, g_mix, w_in_t, tm):
    S = x.shape[0]

    def body(x_ref, g_ref, w_ref, h_ref, us_ref, uv_ref, gl_ref):
        xv = x_ref[...]
        h = (xv * _rms(xv) * g_ref[...]).astype(MXU)
        h_ref[...] = h
        us_ref[...] = _dot_nt(h, w_ref[0:SSM_W, :])
        uv_ref[...] = _dot_nt(h, w_ref[SSM_W:SSM_W + 2 * SGU_W, :])
        gl_ref[...] = _dot_nt(h, w_ref[SSM_W + 2 * SGU_W:, :])

    row = lambda n: pl.BlockSpec((tm, n), lambda i: (i, 0))
    return pl.pallas_call(
        body, name="in_fwd", grid=(S // tm,),
        in_specs=[row(D_MODEL), _full((1, D_MODEL)), _full(w_in_t.shape)],
        out_specs=[row(D_MODEL), row(SSM_W), row(2 * SGU_W), row(2 * D_MODEL)],
        out_shape=[_sds((S, D_MODEL), MXU), _sds((S, SSM_W)), _sds((S, 2 * SGU_W)), _sds((S, 2 * D_MODEL))],
        compiler_params=_cp("parallel"),
    )(*_in_hbm([x, g_mix, w_in_t]))


def _scan_tables(ar, ai, reverse):
    n = ar.shape[-1]
    def mul(p, q):
        return p[0] * q[0] - p[1] * q[1], p[0] * q[1] + p[1] * q[0]
    a1 = (ar, ai)
    a2 = mul(a1, a1)
    a3 = mul(a2, a1)
    a4 = mul(a2, a2)
    a5 = mul(a4, a1)
    a6 = mul(a4, a2)
    a7 = mul(a4, a3)
    a8 = mul(a4, a4)
    pw = (a1, a2, a3, a4, a5, a6, a7, a8)
    rows = lax.broadcasted_iota(jnp.int32, (8, n), 0)
    tabs = []
    for s, a in ((1, a1), (2, a2), (4, a4)):
        keep = (rows + s <= 7) if reverse else (rows >= s)
        for comp in a:
            tabs.append(jnp.where(keep, jnp.broadcast_to(comp, (8, n)), 0.0))
    for c in range(2):
        q = jnp.zeros((8, n), F32)
        for r in range(8):
            e = (8 - r) if reverse else (r + 1)
            q = jnp.where(rows == r, jnp.broadcast_to(pw[e - 1][c], (8, n)), q)
        tabs.append(q)
    return tabs


def _scan_group(xr, xi, tab_ref, cr, ci, reverse):
    for t, s in enumerate((1, 2, 4)):
        pr = tab_ref[2 * t]
        pi = tab_ref[2 * t + 1]
        sh = (8 - s) if reverse else s
        sr = pltpu.roll(xr, sh, 0)
        si = pltpu.roll(xi, sh, 0)
        xr, xi = xr + pr * sr - pi * si, xi + pr * si + pi * sr
    qr = tab_ref[6]
    qi = tab_ref[7]
    return xr + qr * cr - qi * ci, xi + qr * ci + qi * cr


def _runs_load(src_ref, dst_ref, run):
    for i in range(run):
        dst_ref[8 * i:8 * i + 8, :] = src_ref[pl.ds(i, 8, stride=run), :]


def _runs_store(val, dst_ref, run):
    for i in range(run):
        dst_ref[pl.ds(i, 8, stride=run), :] = val[8 * i:8 * i + 8, :]


def _cpow2(ar, ai, log2n):
    for _ in range(log2n):
        ar, ai = ar * ar - ai * ai, 2.0 * ar * ai
    return ar, ai


def _s5_fwd(us, abar_re, abar_im, b_re, b_im, c_re, c_im, d_skip, tm):
    S = us.shape[0]
    nt = S // tm
    w = 8 * SSM_P
    run = tm // 8
    assert run & (run - 1) == 0

    def body(us_ref, ar_ref, ai_ref, br_ref, bi_ref, cr_ref, ci_ref, d_ref, str_ref, sti_ref, ys_ref,
             tab_ref, car_ref, up_ref):
        i = pl.program_id(1)

        @pl.when(i == 0)
        def _():
            car_ref[...] = jnp.zeros_like(car_ref)
            for k, t in enumerate(_scan_tables(*_cpow2(ar_ref[...], ai_ref[...], run.bit_length() - 1), False)):
                tab_ref[k] = t

        _runs_load(us_ref, up_ref, run)
        ub = up_ref[...].astype(MXU)
        str_ref[...] = _dot(ub, br_ref[0])
        sti_ref[...] = _dot(ub, bi_ref[0])
        ar = jnp.broadcast_to(ar_ref[...], (8, w))
        ai = jnp.broadcast_to(ai_ref[...], (8, w))

        def advance(k, state):
            r0 = pl.multiple_of(k * 8, 8)
            sr, si = state
            return (ar * sr - ai * si + str_ref[pl.ds(r0, 8), :], ar * si + ai * sr + sti_ref[pl.ds(r0, 8), :])

        def emit(k, state):
            r0 = pl.multiple_of(k * 8, 8)
            sr, si = advance(k, state)
            str_ref[pl.ds(r0, 8), :] = sr
            sti_ref[pl.ds(r0, 8), :] = si
            return sr, si

        zero = jnp.zeros((8, w), F32)
        er, ei = lax.fori_loop(0, run, advance, (zero, zero))
        cr, ci = car_ref[0:1, :], car_ref[1:2, :]
        tr, ti = _scan_group(er, ei, tab_ref, cr, ci, False)
        r8 = lax.broadcasted_iota(jnp.int32, (8, w), 0)
        start = (jnp.where(r8 == 0, cr, pltpu.roll(tr, 1, 0)), jnp.where(r8 == 0, ci, pltpu.roll(ti, 1, 0)))
        car_ref[0:1, :] = tr[7:8, :]
        car_ref[1:2, :] = ti[7:8, :]
        lax.fori_loop(0, run, emit, start)
        y = _dot_nt(str_ref[...].astype(MXU), cr_ref[0]) - _dot_nt(sti_ref[...].astype(MXU), ci_ref[0])
        _runs_store(y, ys_ref, run)
        ys_ref[...] += d_ref[...] * us_ref[...]

    blk = lambda: pl.BlockSpec((1, 8 * SSM_H, w), lambda j, i: (j, 0, 0))
    return pl.pallas_call(
        body, name="s5_fwd", grid=(SSM_BLK, nt),
        in_specs=[pl.BlockSpec((tm, LANES), lambda j, i: (i, j)),
                  pl.BlockSpec((1, w), lambda j, i: (0, j)), pl.BlockSpec((1, w), lambda j, i: (0, j)),
                  blk(), blk(), blk(), blk(),
                  pl.BlockSpec((1, LANES), lambda j, i: (0, j))],
        out_specs=[pl.BlockSpec((tm, w), lambda j, i: (i, j)), pl.BlockSpec((tm, w), lambda j, i: (i, j)),
                   pl.BlockSpec((tm, LANES), lambda j, i: (i, j))],
        out_shape=[_sds((S, SSM_BLK * w)), _sds((S, SSM_BLK * w)), _sds((S, SSM_W))],
        scratch_shapes=[pltpu.VMEM((8, 8, w), F32), pltpu.VMEM((8, w), F32), pltpu.VMEM((tm, LANES), F32)],
        compiler_params=_cp("parallel", "arbitrary"),
    )(*_in_hbm([us, abar_re, abar_im, b_re, b_im, c_re, c_im, d_skip]))


def _group_halves(vp):
    first = lax.broadcasted_iota(jnp.int32, vp.shape, 1) < SGU_D
    zero = jnp.zeros((), vp.dtype)
    return jnp.where(first, vp, zero), jnp.where(first, zero, vp)


def _sgu_mix(vnb, wcat_ref):
    outs = []
    for q in range(SGU_G // 2):
        lo, hi = _group_halves(vnb[:, LANES * q:LANES * (q + 1)])
        outs.append(_dot(wcat_ref[q], jnp.concatenate([lo, hi], axis=0)))
    return jnp.concatenate(outs, axis=1)


def _mix_fwd(x, ys, uv, gl, w_glu, b_glu, w_pa, g_sgu, ws, bias_s, w_pb, w_out, g_ffn, tm):
    S = x.shape[0]

    def body(x_ref, ys_ref, uv_ref, gl_ref, wglu_ref, bglu_ref, wpa_ref, gs_ref, ws_ref, bias_ref, wpb_ref, wout_ref,
             gf_ref, yg_ref, yap_ref, sg_ref, ya_ref, yb_ref, m_ref, x1_ref, h2_ref):
        yg = _gelu(ys_ref[...])
        ygb = yg.astype(MXU)
        yg_ref[...] = ygb
        z = _dot(ygb, wglu_ref[...]) + bglu_ref[...]
        yapb = (yg * _sigmoid(z)).astype(MXU)
        yap_ref[...] = yapb
        ya = _dot(yapb, wpa_ref[...])
        ya_ref[...] = ya

        uvg = _gelu(uv_ref[...])
        u2 = uvg[:, :SGU_W]
        v2 = uvg[:, SGU_W:]
        vnb = (v2 * _rms(v2) * gs_ref[...]).astype(MXU)
        for c in range(tm // CHUNK):
            rs = slice(c * CHUNK, (c + 1) * CHUNK)
            mixed = _sgu_mix(vnb[rs], ws_ref) + bias_ref[...]
            sg_ref[rs, :] = (u2[rs] * mixed).astype(MXU)
        yb = _dot(sg_ref[...], wpb_ref[...])
        yb_ref[...] = yb

        glv = gl_ref[...]
        m = _sigmoid(glv[:, :D_MODEL]) * ya + _sigmoid(glv[:, D_MODEL:]) * yb
        mb = m.astype(MXU)
        m_ref[...] = mb
        x1 = x_ref[...] + _dot(mb, wout_ref[...])
        x1_ref[...] = x1
        h2_ref[...] = (x1 * _rms(x1) * gf_ref[...]).astype(MXU)

    row = lambda n: pl.BlockSpec((tm, n), lambda i: (i, 0))
    return pl.pallas_call(
        body, name="mix_fwd", grid=(S // tm,),
        in_specs=[row(D_MODEL), row(SSM_W), row(2 * SGU_W), row(2 * D_MODEL),
                  _full(w_glu.shape), _full(b_glu.shape), _full(w_pa.shape), _full(g_sgu.shape), _full(ws.shape),
                  _full(bias_s.shape), _full(w_pb.shape), _full(w_out.shape), _full(g_ffn.shape)],
        out_specs=[row(SSM_W), row(SSM_W), row(SGU_W), row(D_MODEL), row(D_MODEL), row(D_MODEL), row(D_MODEL),
                   row(D_MODEL)],
        out_shape=[_sds((S, SSM_W), MXU), _sds((S, SSM_W), MXU), _sds((S, SGU_W), MXU), _sds((S, D_MODEL)),
                   _sds((S, D_MODEL)), _sds((S, D_MODEL), MXU), _sds((S, D_MODEL)), _sds((S, D_MODEL), MXU)],
        compiler_params=_cp("parallel"),
    )(*_in_hbm([x, ys, uv, gl, w_glu, b_glu, w_pa, g_sgu, ws, bias_s, w_pb, w_out, g_ffn]))


def _causal_conv3(u, prev8, cw, cb):
    tm = u.shape[0]
    w0, w1, w2 = cw[0:1], cw[1:2], cw[2:3]
    body = w0 * pltpu.roll(u, 2, 0) + w1 * pltpu.roll(u, 1, 0) + w2 * u + cb
    u8 = u[0:8, :]
    r8 = lax.broadcasted_iota(jnp.int32, u8.shape, 0)
    t1 = prev8[7:8, :]
    t0 = prev8[6:7, :]
    s1 = jnp.where(r8 == 0, t1, pltpu.roll(u8, 1, 0))
    s2 = jnp.where(r8 == 0, t0, jnp.where(r8 == 1, t1, pltpu.roll(u8, 2, 0)))
    first = w0 * s2 + w1 * s1 + w2 * u8 + cb
    return jnp.concatenate([first, body[8:tm, :]], axis=0)


def _causal_conv3_adjoint(d, next8, cw):
    tm = d.shape[0]
    w0, w1, w2 = cw[0:1], cw[1:2], cw[2:3]
    n1 = pltpu.roll(d, tm - 1, 0)
    n2 = pltpu.roll(d, tm - 2, 0)
    body = w2 * d + w1 * n1 + w0 * n2
    d8 = d[tm - 8:tm, :]
    r8 = lax.broadcasted_iota(jnp.int32, d8.shape, 0)
    h0 = next8[0:1, :]
    h1 = next8[1:2, :]
    m1 = jnp.where(r8 == 7, h0, pltpu.roll(d8, 7, 0))
    m2 = jnp.where(r8 == 6, h0, jnp.where(r8 == 7, h1, pltpu.roll(d8, 6, 0)))
    last = w2 * d8 + w1 * m1 + w0 * m2
    out = jnp.concatenate([body[0:tm - 8, :], last], axis=0)
    return out, n1, n2, h0 - d[0:1, :], h1 - d[1:2, :]


def _ffn_fwd(h2, x1, tgt, w_up, conv_w, conv_b, w_down, g_final, tm):
    S = h2.shape[0]
    nt = S // tm
    ncb = FF_NCB

    def body(h2_ref, wup_hbm, cwa_ref, cwb_ref, cba_ref, cbb_ref, wd_hbm, x1_ref, gf_ref, tgt_ref,
             up_ref, ab_ref, ff_ref, dx2_ref, dx2b_ref, loss_ref, dgf_ref, acc_ref, tail_ref, wup_ref, wdn_ref, wsem):
        i = pl.program_id(0)
        cb = pl.program_id(1)

        @pl.when(i == 0)
        def _():
            tail_ref[cb] = jnp.zeros((2, 8, FF_CW), F32)

        @pl.when(jnp.logical_and(i == 0, cb == 0))
        def _():
            loss_ref[...] = jnp.zeros_like(loss_ref)
            dgf_ref[...] = jnp.zeros_like(dgf_ref)
            _fetch_once([(wup_hbm, wup_ref), (wd_hbm, wdn_ref)], wsem)

        h2v = h2_ref[...]
        ua = _dot_nt(h2v, wup_ref[cb])
        ub = _dot_nt(h2v, wup_ref[ncb + cb])
        up_ref[0, 0] = ua.astype(MXU)
        up_ref[1, 0] = ub.astype(MXU)
        a = _causal_conv3(ua, tail_ref[cb, 0], cwa_ref[0], cba_ref[0])
        b = _causal_conv3(ub, tail_ref[cb, 1], cwb_ref[0], cbb_ref[0])
        tail_ref[cb, 0] = ua[tm - 8:tm, :]
        tail_ref[cb, 1] = ub[tm - 8:tm, :]
        ab_ref[0, 0] = a
        ab_ref[1, 0] = b
        ffb = (a * _sigmoid(a) * b).astype(MXU)
        ff_ref[0] = ffb
        contrib = _dot(ffb, wdn_ref[pl.ds(pl.multiple_of(cb * FF_CW, FF_CW), FF_CW), :])

        @pl.when(cb == 0)
        def _():
            acc_ref[...] = contrib

        @pl.when(cb > 0)
        def _():
            acc_ref[...] += contrib

        @pl.when(cb == ncb - 1)
        def _():
            x2 = x1_ref[...] + acc_ref[...]
            r = _rms(x2)
            xn = x2 * r
            g = gf_ref[...]
            diff = xn * g - tgt_ref[...]
            loss_ref[...] += (0.5 / D_MODEL) * jnp.sum(diff * diff)
            dy = diff * (1.0 / D_MODEL)
            dgf_ref[...] += _rowsum(dy * xn)
            dx2 = _rms_bwd(dy * g, xn, r)
            dx2_ref[...] = dx2
            dx2b_ref[...] = dx2.astype(MXU)

    row = lambda n: pl.BlockSpec((tm, n), lambda i, c: (i, 0))
    gate = lambda r: pl.BlockSpec((1, r, FF_CW), lambda i, c: (c, 0, 0))
    lin = lambda r: pl.BlockSpec((1, r, FF_CW), lambda i, c: (ncb + c, 0, 0))
    return pl.pallas_call(
        body, name="ffn_fwd", grid=(nt, ncb),
        in_specs=[row(D_MODEL), _ANY, gate(3), lin(3), gate(1), lin(1), _ANY,
                  row(D_MODEL), _full((1, D_MODEL)), row(D_MODEL)],
        out_specs=[pl.BlockSpec((2, 1, tm, FF_CW), lambda i, c: (0, c, i, 0)),
                   pl.BlockSpec((2, 1, tm, FF_CW), lambda i, c: (0, c, i, 0)),
                   pl.BlockSpec((1, tm, FF_CW), lambda i, c: (c, i, 0)),
                   row(D_MODEL), row(D_MODEL), _full((1, LANES)), _full((1, D_MODEL))],
        out_shape=[_sds((2, ncb, S, FF_CW), MXU), _sds((2, ncb, S, FF_CW)), _sds((ncb, S, FF_CW), MXU),
                   _sds((S, D_MODEL)), _sds((S, D_MODEL), MXU), _sds((1, LANES)), _sds((1, D_MODEL))],
        scratch_shapes=[pltpu.VMEM((tm, D_MODEL), F32), pltpu.VMEM((ncb, 2, 8, FF_CW), F32),
                        pltpu.VMEM(w_up.shape, w_up.dtype), pltpu.VMEM(w_down.shape, w_down.dtype),
                        pltpu.SemaphoreType.DMA((2,))],
        compiler_params=pltpu.CompilerParams(dimension_semantics=("arbitrary", "arbitrary"),
                                             vmem_limit_bytes=FFN_VMEM_LIMIT),
    )(*_in_hbm([h2, w_up, conv_w, conv_w, conv_b, conv_b, w_down, x1, g_final, tgt]))


def _ffn_bwd(dx2, up, ab, x1, w_up, conv_w, w_down, g_ffn, tm):
    S = dx2.shape[0]
    nt = S // tm
    ncb = FF_NCB

    def body(dx2_ref, up_ref, ab_ref, cwa_ref, cwb_ref, wd_hbm, wup_hbm,
             x1_ref, g_ref, dup_ref, dx1_ref, dx1b_ref, dconv_ref, dg_ref, acc_ref, head_ref, wup_ref, wdn_ref, wsem):
        i = pl.program_id(0)
        cb = pl.program_id(1)

        @pl.when(i == 0)
        def _():
            head_ref[cb] = jnp.zeros((2, 8, FF_CW), F32)
            dconv_ref[cb] = jnp.zeros((8, FF_CW), F32)
            dconv_ref[ncb + cb] = jnp.zeros((8, FF_CW), F32)

        @pl.when(jnp.logical_and(i == 0, cb == 0))
        def _():
            dg_ref[...] = jnp.zeros_like(dg_ref)
            _fetch_once([(wup_hbm, wup_ref), (wd_hbm, wdn_ref)], wsem)

        dff = _dot_nt(dx2_ref[...].astype(MXU), wdn_ref[pl.ds(pl.multiple_of(cb * FF_CW, FF_CW), FF_CW), :])
        a = ab_ref[0, 0]
        b = ab_ref[1, 0]
        sa = _sigmoid(a)
        silu = a * sa
        da = (dff * b) * (sa + silu * (1.0 - sa))
        db = dff * silu
        dps = []
        for half, slot, d, cw_ref in ((0, cb, da, cwa_ref), (1, ncb + cb, db, cwb_ref)):
            dp, n1, n2, fix0, fix1 = _causal_conv3_adjoint(d, head_ref[cb, half], cw_ref[0])
            head_ref[cb, half] = d[0:8, :]
            dpb16 = dp.astype(MXU)
            dup_ref[half, 0] = dpb16
            dps.append(dpb16)
            u = up_ref[half, 0].astype(F32)
            u_last = u[tm - 1:tm, :]
            dconv_ref[slot, 0:1, :] += _rowsum(n2 * u) + fix0 * u[tm - 2:tm - 1, :] + fix1 * u_last
            dconv_ref[slot, 1:2, :] += _rowsum(n1 * u) + fix0 * u_last
            dconv_ref[slot, 2:3, :] += _rowsum(d * u)
            dconv_ref[slot, 3:4, :] += _rowsum(d)
        contrib = _dot(dps[0], wup_ref[cb]) + _dot(dps[1], wup_ref[ncb + cb])

        @pl.when(cb == 0)
        def _():
            acc_ref[...] = contrib

        @pl.when(cb > 0)
        def _():
            acc_ref[...] += contrib

        @pl.when(cb == ncb - 1)
        def _():
            x1v = x1_ref[...]
            r = _rms(x1v)
            xn = x1v * r
            dh2 = acc_ref[...]
            dg_ref[...] += _rowsum(dh2 * xn)
            dx1 = dx2_ref[...] + _rms_bwd(dh2 * g_ref[...], xn, r)
            dx1_ref[...] = dx1
            dx1b_ref[...] = dx1.astype(MXU)

    row = lambda n: pl.BlockSpec((tm, n), lambda i, c: (nt - 1 - i, 0))
    colb = lambda: pl.BlockSpec((2, 1, tm, FF_CW), lambda i, c: (0, c, nt - 1 - i, 0))
    gate = lambda r: pl.BlockSpec((1, r, FF_CW), lambda i, c: (c, 0, 0))
    lin = lambda r: pl.BlockSpec((1, r, FF_CW), lambda i, c: (ncb + c, 0, 0))
    return pl.pallas_call(
        body, name="ffn_bwd", grid=(nt, ncb),
        in_specs=[row(D_MODEL), colb(), colb(), gate(3), lin(3), _ANY, _ANY, row(D_MODEL), _full((1, D_MODEL))],
        out_specs=[colb(), row(D_MODEL), row(D_MODEL), _full((2 * ncb, 8, FF_CW)), _full((1, D_MODEL))],
        out_shape=[_sds((2, ncb, S, FF_CW), MXU), _sds((S, D_MODEL)), _sds((S, D_MODEL), MXU), _sds((2 * ncb, 8, FF_CW)),
                   _sds((1, D_MODEL))],
        scratch_shapes=[pltpu.VMEM((tm, D_MODEL), F32), pltpu.VMEM((ncb, 2, 8, FF_CW), F32),
                        pltpu.VMEM(w_up.shape, w_up.dtype), pltpu.VMEM(w_down.shape, w_down.dtype),
                        pltpu.SemaphoreType.DMA((2,))],
        compiler_params=pltpu.CompilerParams(dimension_semantics=("arbitrary", "arbitrary"),
                                             vmem_limit_bytes=FFN_VMEM_LIMIT),
    )(*_in_hbm([dx2, up, ab, conv_w, conv_w, w_down, w_up, x1, g_ffn]))


def _mix_bwd(dx1, gl, ya, yb, ys, uv, w_out, w_pa, w_pb, w_glu, b_glu, g_sgu, ws, ws_t, bias_s, tm):
    S = dx1.shape[0]

    def body(dx1_ref, gl_ref, ya_ref, yb_ref, ys_ref, uv_ref, wout_ref, wpa_ref, wpb_ref, wglu_ref, bglu_ref, gs_ref,
             ws_ref, wst_ref, bias_ref,
             dgl_ref, dya_ref, dyb_ref, dz_ref, dys_ref, duv_ref, dbglu_ref, dgs_ref, dws_ref, dbs_ref,
             du2_ref, dvn_ref):
        i = pl.program_id(0)

        @pl.when(i == 0)
        def _():
            dbglu_ref[...] = jnp.zeros_like(dbglu_ref)
            dgs_ref[...] = jnp.zeros_like(dgs_ref)
            dws_ref[...] = jnp.zeros_like(dws_ref)
            dbs_ref[...] = jnp.zeros_like(dbs_ref)

        dm = _dot_nt(dx1_ref[...].astype(MXU), wout_ref[...])
        glv = gl_ref[...]
        ga = _sigmoid(glv[:, :D_MODEL])
        gb = _sigmoid(glv[:, D_MODEL:])
        dgl_ref[:, :D_MODEL] = (dm * ya_ref[...] * ga * (1.0 - ga)).astype(MXU)
        dgl_ref[:, D_MODEL:] = (dm * yb_ref[...] * gb * (1.0 - gb)).astype(MXU)
        dyab = (dm * ga).astype(MXU)
        dybb = (dm * gb).astype(MXU)
        dya_ref[...] = dyab
        dyb_ref[...] = dybb

        dyap = _dot_nt(dyab, wpa_ref[...])
        yg, dgelu = _gelu_and_grad(ys_ref[...])
        sz = _sigmoid(_dot(yg.astype(MXU), wglu_ref[...]) + bglu_ref[...])
        dz = dyap * yg * sz * (1.0 - sz)
        dzb = dz.astype(MXU)
        dz_ref[...] = dzb
        dbglu_ref[...] += _rowsum(dz)
        dys_ref[...] = (dyap * sz + _dot_nt(dzb, wglu_ref[...])) * dgelu

        dsg = _dot_nt(dybb, wpb_ref[...])
        uvg, duvg = _gelu_and_grad(uv_ref[...])
        u2 = uvg[:, :SGU_W]
        v2 = uvg[:, SGU_W:]
        rv = _rms(v2)
        vhat = v2 * rv
        gs = gs_ref[...]
        vnb = (vhat * gs).astype(MXU)
        tril = (lax.broadcasted_iota(jnp.int32, (CHUNK, CHUNK), 0)
                >= lax.broadcasted_iota(jnp.int32, (CHUNK, CHUNK), 1))
        for c in range(tm // CHUNK):
            rs = slice(c * CHUNK, (c + 1) * CHUNK)
            vc = vnb[rs]
            mixed = _sgu_mix(vc, ws_ref) + bias_ref[...]
            dsg_c = dsg[rs]
            du2_ref[rs, :] = dsg_c * mixed
            dmx = dsg_c * u2[rs]
            dbs_ref[...] += dmx
            dmb = dmx.astype(MXU)
            dvn_ref[rs, :] = _sgu_mix(dmb, wst_ref)
            for q in range(SGU_G // 2):
                lanes = slice(LANES * q, LANES * (q + 1))
                for j, part in enumerate(_group_halves(dmb[:, lanes])):
                    dws_ref[2 * q + j] += jnp.where(tril, _dot_nt(part, vc[:, lanes]), 0.0)
        dvn = dvn_ref[...]
        dgs_ref[...] += _rowsum(dvn * vhat)
        dv2 = _rms_bwd(dvn * gs, vhat, rv)
        duv_ref[:, :SGU_W] = (du2_ref[...] * duvg[:, :SGU_W]).astype(MXU)
        duv_ref[:, SGU_W:] = (dv2 * duvg[:, SGU_W:]).astype(MXU)

    row = lambda n: pl.BlockSpec((tm, n), lambda i: (i, 0))
    return pl.pallas_call(
        body, name="mix_bwd", grid=(S // tm,),
        in_specs=[row(D_MODEL), row(2 * D_MODEL), row(D_MODEL), row(D_MODEL), row(SSM_W), row(2 * SGU_W),
                  _full(w_out.shape), _full(w_pa.shape), _full(w_pb.shape), _full(w_glu.shape), _full(b_glu.shape),
                  _full(g_sgu.shape), _full(ws.shape), _full(ws_t.shape), _full(bias_s.shape)],
        out_specs=[row(2 * D_MODEL), row(D_MODEL), row(D_MODEL), row(SSM_W), row(SSM_W), row(2 * SGU_W),
                   _full((1, SSM_W)), _full((1, SGU_W)), _full((SGU_G, CHUNK, CHUNK)), _full((CHUNK, SGU_W))],
        out_shape=[_sds((S, 2 * D_MODEL), MXU), _sds((S, D_MODEL), MXU), _sds((S, D_MODEL), MXU), _sds((S, SSM_W), MXU),
                   _sds((S, SSM_W)), _sds((S, 2 * SGU_W), MXU),
                   _sds((1, SSM_W)), _sds((1, SGU_W)), _sds((SGU_G, CHUNK, CHUNK)), _sds((CHUNK, SGU_W))],
        scratch_shapes=[pltpu.VMEM((tm, SGU_W), F32), pltpu.VMEM((tm, SGU_W), F32)],
        compiler_params=_cp("arbitrary"),
    )(*_in_hbm([dx1, gl, ya, yb, ys, uv, w_out, w_pa, w_pb, w_glu, b_glu, g_sgu, ws, ws_t, bias_s]))


def _s5_bwd(dys, us, st_re, st_im, abar_re, abar_im, b_re, b_im, c_re, c_im, d_skip, tm):
    S = us.shape[0]
    nt = S // tm
    w = 8 * SSM_P
    hb = tm // 8
    run = tm // 8
    assert run & (run - 1) == 0

    def body(dys_ref, us_ref, str_ref, sti_ref, hr_ref, hi_ref, ar_ref, ai_ref, br_ref, bi_ref, cr_ref, ci_ref, d_ref,
             dus_ref, dab_ref, dd_ref, dbr_ref, dbi_ref, dcr_ref, dci_ref,
             tab_ref, car_ref, gr_ref, gi_ref, dyp_ref, up_ref, dun_ref):
        i = pl.program_id(1)
        ri = nt - 1 - i

        @pl.when(i == 0)
        def _():
            car_ref[...] = jnp.zeros_like(car_ref)
            for k, t in enumerate(_scan_tables(*_cpow2(ar_ref[...], -ai_ref[...], run.bit_length() - 1), True)):
                tab_ref[k] = t
            for r in (dab_ref, dd_ref, dbr_ref, dbi_ref, dcr_ref, dci_ref):
                r[...] = jnp.zeros_like(r)

        _runs_load(dys_ref, dyp_ref, run)
        _runs_load(us_ref, up_ref, run)
        dyb = dyp_ref[...].astype(MXU)
        gr_ref[...] = _dot(dyb, cr_ref[0])
        gi_ref[...] = -_dot(dyb, ci_ref[0])
        ar = jnp.broadcast_to(ar_ref[...], (8, w))
        ai = jnp.broadcast_to(-ai_ref[...], (8, w))

        def advance(kk, state):
            r0 = pl.multiple_of((run - 1 - kk) * 8, 8)
            gr, gi = state
            return (ar * gr - ai * gi + gr_ref[pl.ds(r0, 8), :], ar * gi + ai * gr + gi_ref[pl.ds(r0, 8), :])

        def emit(kk, state):
            r0 = pl.multiple_of((run - 1 - kk) * 8, 8)
            gr, gi = advance(kk, state)
            gr_ref[pl.ds(r0, 8), :] = gr
            gi_ref[pl.ds(r0, 8), :] = gi
            return gr, gi

        zero = jnp.zeros((8, w), F32)
        er, ei = lax.fori_loop(0, run, advance, (zero, zero))
        cr, ci = car_ref[0:1, :], car_ref[1:2, :]
        tr, ti = _scan_group(er, ei, tab_ref, cr, ci, True)
        r8 = lax.broadcasted_iota(jnp.int32, (8, w), 0)
        start = (jnp.where(r8 == 7, cr, pltpu.roll(tr, 7, 0)), jnp.where(r8 == 7, ci, pltpu.roll(ti, 7, 0)))
        car_ref[0:1, :] = tr[0:1, :]
        car_ref[1:2, :] = ti[0:1, :]
        lax.fori_loop(0, run, emit, start)

        gsr = gr_ref[...]
        gsi = gi_ref[...]
        sr = str_ref[...]
        si = sti_ref[...]
        first = ri == 0

        def previous(s, halo_ref):
            head = jnp.where(r8 == 0, jnp.where(first, 0.0, halo_ref[7:8, :]), pltpu.roll(s[tm - 8:tm, :], 1, 0))
            return jnp.concatenate([head, s[0:tm - 8, :]], axis=0)

        spr = previous(sr, hr_ref)
        spi = previous(si, hi_ref)
        dab_ref[0, 0:1, :] += _rowsum(gsr * spr + gsi * spi)
        dab_ref[0, 1:2, :] += _rowsum(gsi * spr - gsr * spi)

        gbr = gsr.astype(MXU)
        gbi = gsi.astype(MXU)
        _runs_store(_dot_nt(gbr, br_ref[0]) + _dot_nt(gbi, bi_ref[0]), dun_ref, run)
        dys_v = dys_ref[...]
        dus_ref[...] = (dun_ref[...] + d_ref[...] * dys_v).astype(MXU)
        dd_ref[0, 0:1, :] += _rowsum(dys_v * us_ref[...])
        ub = up_ref[...].astype(MXU)
        dbr_ref[0] += _dot_tn(ub, gbr)
        dbi_ref[0] += _dot_tn(ub, gbi)
        dcr_ref[0] += _dot_tn(dyb, sr.astype(MXU))
        dci_ref[0] -= _dot_tn(dyb, si.astype(MXU))

    blk = lambda: pl.BlockSpec((1, 8 * SSM_H, w), lambda j, i: (j, 0, 0))
    rowl = lambda: pl.BlockSpec((tm, LANES), lambda j, i: (nt - 1 - i, j))
    roww = lambda: pl.BlockSpec((tm, w), lambda j, i: (nt - 1 - i, j))
    halo = lambda: pl.BlockSpec((8, w), lambda j, i: (jnp.maximum((nt - 1 - i) * hb - 1, 0), j))
    return pl.pallas_call(
        body, name="s5_bwd", grid=(SSM_BLK, nt),
        in_specs=[rowl(), rowl(), roww(), roww(), halo(), halo(),
                  pl.BlockSpec((1, w), lambda j, i: (0, j)), pl.BlockSpec((1, w), lambda j, i: (0, j)),
                  blk(), blk(), blk(), blk(),
                  pl.BlockSpec((1, LANES), lambda j, i: (0, j))],
        out_specs=[rowl(),
                   pl.BlockSpec((1, 8, w), lambda j, i: (j, 0, 0)), pl.BlockSpec((1, 8, LANES), lambda j, i: (j, 0, 0)),
                   blk(), blk(), blk(), blk()],
        out_shape=[_sds((S, SSM_W), MXU), _sds((SSM_BLK, 8, w)), _sds((SSM_BLK, 8, LANES)),
                   _sds((SSM_BLK, 8 * SSM_H, w)), _sds((SSM_BLK, 8 * SSM_H, w)),
                   _sds((SSM_BLK, 8 * SSM_H, w)), _sds((SSM_BLK, 8 * SSM_H, w))],
        scratch_shapes=[pltpu.VMEM((8, 8, w), F32), pltpu.VMEM((8, w), F32),
                        pltpu.VMEM((tm, w), F32), pltpu.VMEM((tm, w), F32),
                        pltpu.VMEM((tm, LANES), F32), pltpu.VMEM((tm, LANES), F32), pltpu.VMEM((tm, LANES), F32)],
        compiler_params=_cp("parallel", "arbitrary"),
    )(*_in_hbm([dys, us, st_re, st_im, st_re, st_im, abar_re, abar_im, b_re, b_im, c_re, c_im, d_skip]))


def _in_bwd(dus, duv, dgl, dx1, x, g_mix, w_in, tm):
    S = x.shape[0]

    def body(dus_ref, duv_ref, dgl_ref, dx1_ref, x_ref, g_ref, w_ref, gx_ref, dg_ref):
        @pl.when(pl.program_id(0) == 0)
        def _():
            dg_ref[...] = jnp.zeros_like(dg_ref)

        dh = (_dot(dus_ref[...], w_ref[0:SSM_W, :])
              + _dot(duv_ref[...], w_ref[SSM_W:SSM_W + 2 * SGU_W, :])
              + _dot(dgl_ref[...], w_ref[SSM_W + 2 * SGU_W:, :]))
        xv = x_ref[...]
        r = _rms(xv)
        xn = xv * r
        dg_ref[...] += _rowsum(dh * xn)
        gx_ref[...] = dx1_ref[...] + _rms_bwd(dh * g_ref[...], xn, r)

    row = lambda n: pl.BlockSpec((tm, n), lambda i: (i, 0))
    return pl.pallas_call(
        body, name="in_bwd", grid=(S // tm,),
        in_specs=[row(SSM_W), row(2 * SGU_W), row(2 * D_MODEL), row(D_MODEL), row(D_MODEL), _full((1, D_MODEL)),
                  _full(w_in.shape)],
        out_specs=[row(D_MODEL), _full((1, D_MODEL))],
        out_shape=[_sds((S, D_MODEL)), _sds((1, D_MODEL))],
        compiler_params=_cp("arbitrary"),
    )(*_in_hbm([dus, duv, dgl, dx1, x, g_mix, w_in]))


def _pick(n, cands):
    for c in cands:
        if n % c == 0:
            return c
    return n


def _wgrad_split(a, b, nsplit, tk, name):
    S, K = a.shape
    N = b.shape[1]
    c = N // nsplit

    def body(a_ref, b_ref, o_ref):
        prod = _dot_tn(a_ref[...], b_ref[...])
        for d in range(nsplit):
            o_ref[d] = prod[:, c * d:c * (d + 1)].astype(MXU)

    return pl.pallas_call(
        body, name=name, grid=(K // tk,),
        in_specs=[pl.BlockSpec((S, tk), lambda k: (0, k)), _full((S, N))],
        out_specs=pl.BlockSpec((nsplit, tk, c), lambda k: (0, k, 0)),
        out_shape=_sds((nsplit, K, c), MXU),
        compiler_params=_cp("parallel"),
    )(*_in_hbm([a, b]))


def _wgrad_in_t(dps, h1, name, after=()):
    S, K = h1.shape
    cw = 512
    counts = [b.shape[1] // cw for b in dps]
    starts = [sum(counts[:i]) for i in range(len(dps))]
    nblk = sum(counts)

    def body(*refs):
        b_refs = refs[:len(dps)]
        h_ref, o_ref = refs[len(dps)], refs[-1]
        j = pl.program_id(0)
        for b_ref, st, cnt in zip(b_refs, starts, counts):
            @pl.when(jnp.logical_and(j >= st, j < st + cnt))
            def _():
                o_ref[...] = _dot_tn(b_ref[...], h_ref[...]).astype(MXU)

    def src_spec(st, cnt):
        return pl.BlockSpec((S, cw), lambda j: (0, jnp.clip(j - st, 0, cnt - 1)))

    return pl.pallas_call(
        body, name=name, grid=(nblk,),
        in_specs=[src_spec(st, cnt) for st, cnt in zip(starts, counts)] + [_full((S, K))] + [_ANY] * len(after),
        out_specs=pl.BlockSpec((cw, K), lambda j: (j, 0)),
        out_shape=_sds((nblk * cw, K), MXU),
        compiler_params=_cp("arbitrary"),
    )(*_in_hbm([*dps, h1]), *after)


def _wgrad_blk(a3, b3, nblk, a_of, b_of, name):
    S, K = a3.shape[1:]
    N = b3.shape[2]

    def body(a_ref, b_ref, o_ref):
        o_ref[0] = _dot_tn(a_ref[0], b_ref[0]).astype(MXU)

    return pl.pallas_call(
        body, name=name, grid=(nblk,),
        in_specs=[pl.BlockSpec((1, S, K), lambda b: (a_of(b), 0, 0)),
                  pl.BlockSpec((1, S, N), lambda b: (b_of(b), 0, 0))],
        out_specs=pl.BlockSpec((1, K, N), lambda b: (b, 0, 0)),
        out_shape=_sds((nblk, K, N), MXU),
        compiler_params=pltpu.CompilerParams(dimension_semantics=("parallel",), vmem_limit_bytes=WGRAD_VMEM_LIMIT),
    )(*_in_hbm([a3, b3]))


def _assemble_cols(blocks_list, name):
    def body(*refs):
        n = len(blocks_list)
        for b_ref, o_ref in zip(refs[:n], refs[n:]):
            c = b_ref.shape[2]
            for d in range(N_DEV):
                o_ref[:, c * d:c * (d + 1)] = b_ref[d]

    outs = [_sds((b.shape[1], N_DEV * b.shape[2]), b.dtype) for b in blocks_list]
    return pl.pallas_call(
        body, name=name, grid=(1,), in_specs=[_full(b.shape) for b in blocks_list],
        out_specs=[_full(o.shape) for o in outs], out_shape=outs, compiler_params=_cp("arbitrary"),
    )(*_in_hbm(blocks_list))


def _tile(S, want):
    return want if S % want == 0 else S


def _local_step(x, tgt, p, mixer_relay, mixer_weights, ffn_weights, ffn_grads_out, grads_out, small_out):
    S = x.shape[0]
    tm = _tile(S, 256)
    tl = _tile(S, 512)

    rep = lambda a: jnp.repeat(a, SSM_H, axis=0)
    are = rep(p["a_re"])
    aim = rep(p["a_im"])
    ldt = jnp.broadcast_to(rep(p["log_dt"].reshape(SSM_G, 1)), are.shape)
    br_t = p["b_re_t"].reshape(are.shape)
    bi_t = p["b_im_t"].reshape(are.shape)
    abr, abi, bbr, bbi = _s5_params_fwd(are, aim, ldt, br_t, bi_t)
    head = lambda a: a.reshape(SSM_G, SSM_H, SSM_P)[:, 0, :].reshape(1, SSM_G * SSM_P)
    abar_re, abar_im = head(abr), head(abi)
    bd_br = _blockdiag(bbr).astype(MXU)
    bd_bi = _blockdiag(bbi).astype(MXU)
    bd_cr = _blockdiag(p["c_re"].reshape(are.shape)).astype(MXU)
    bd_ci = _blockdiag(p["c_im"].reshape(are.shape)).astype(MXU)
    d_skip = p["d_skip"].reshape(1, SSM_W)

    tril = jnp.tril(jnp.ones((CHUNK, CHUNK), dtype=bool))
    ws = jnp.where(tril[None], p["w_s"], 0.0)
    pair = lambda w: w.reshape(SGU_G // 2, 2, CHUNK, CHUNK).transpose(0, 2, 1, 3).reshape(SGU_G // 2, CHUNK, 2 * CHUNK)
    ws_b = pair(ws).astype(MXU)
    ws_t = pair(ws.transpose(0, 2, 1)).astype(MXU)
    bias_s = jnp.repeat(p["b_s"].T, SGU_D, axis=1)

    g_mix = p["g_mix"].reshape(1, D_MODEL)
    g_ffn = p["g_ffn"].reshape(1, D_MODEL)
    g_final = p["g_final"].reshape(1, D_MODEL)
    g_sgu = p["g_sgu"].reshape(1, SGU_W)
    b_glu = p["b_glu"].reshape(1, SSM_W)
    conv_b = p["conv_b"].reshape(2 * FF_NCB, 1, FF_CW)
    tf = _tile(S, 256)
    ts = _tile(S, 1024)

    h1, us, uv, gl = _in_fwd(x, g_mix, p["w_in_t"], tl)
    token = mixer_relay(us)
    st_re, st_im, ys = _s5_fwd(us, abar_re, abar_im, bd_br, bd_bi, bd_cr, bd_ci, d_skip + token[0:1, 0:1], ts)
    p = dict(p, **mixer_weights(ys))
    yg, yap, sg, ya, yb, m, x1, h2 = _mix_fwd(x, ys, uv, gl, p["w_glu"], b_glu, p["w_proj_a"], g_sgu, ws_b, bias_s,
                                              p["w_proj_b"], p["w_out"], g_ffn, tl)
    w_up, conv_w, w_down = ffn_weights(h2)
    pair_lanes = lambda a: a.reshape(N_DEV // 2, 2, a.shape[1], FF_SHARD).transpose(0, 2, 1, 3).reshape(
        N_DEV // 2, a.shape[1], FF_CW)
    w_up = w_up.reshape(2 * FF_NCB, FF_CW, D_MODEL)
    conv_w = pair_lanes(conv_w)
    up, ab, ff, dx2, dx2b, loss, dg_final = _ffn_fwd(h2, x1, tgt, w_up, conv_w, conv_b, w_down, g_final, tf)

    dup, dx1, dx1b, dconv, dg_ffn = _ffn_bwd(dx2, up, ab, x1, w_up, conv_w, w_down, g_ffn, tf)
    rows8 = lambda g: g.reshape(N_DEV, g.shape[1] // N_DEV, g.shape[2])
    g_up = _wgrad_blk(dup.reshape(2 * FF_NCB, S, FF_CW), h2[None], 2 * FF_NCB, lambda b: b, lambda b: 0,
                      "wgrad_up").reshape(N_DEV, FF_SHARD, D_MODEL)
    g_down = _wgrad_blk(ff, dx2b[None], FF_NCB, lambda b: b, lambda b: 0, "wgrad_down").reshape(
        N_DEV, D_FF // N_DEV, D_MODEL)
    token = ffn_grads_out[0](g_up, g_down)
    dgl, dya, dyb, dz, dys, duv, db_glu, dg_sgu, dws, dbs = _mix_bwd(
        dx1, gl, ya, yb, ys, uv, p["w_out"], p["w_proj_a"], p["w_proj_b"], p["w_glu"], b_glu + token[0:1, 0:1], g_sgu,
        ws_b, ws_t, bias_s, tm)
    token = ffn_grads_out[1](dys)
    token = grads_out(("w_glu", "w_proj_a", "w_proj_b", "w_out"),
                      (rows8(_wgrad_split(yg, dz, 1, SSM_W, "wgrad_glu")),
                       _wgrad_split(yap, dya, N_DEV, SSM_W, "wgrad_pa"),
                       _wgrad_split(sg, dyb, N_DEV, SGU_W, "wgrad_pb"),
                       rows8(_wgrad_split(m, dx1b, 1, 512, "wgrad_out"))), after=[token])
    dus, dab, dd, dbbr, dbbi, dcr, dci = _s5_bwd(dys, us, st_re, st_im, abar_re, abar_im, bd_br, bd_bi, bd_cr, bd_ci,
                                                 d_skip + token[0:1, 0:1], ts)
    g_in = _wgrad_in_t([dus, duv, dgl], h1, "wgrad_in")
    token = grads_out(("w_in",), (g_in.reshape(N_DEV, g_in.shape[0] // N_DEV, D_MODEL),))
    grad_x, dg_mix = _in_bwd(dus, duv, dgl, dx1, x, g_mix + token[0:1, 0:1], p["w_in_t"], tl)

    spread = lambda v: jnp.repeat(v.reshape(SSM_G, SSM_P), SSM_H, axis=0) * (1.0 / SSM_H)
    dabr = spread(dab[:, 0, :])
    dabi = spread(dab[:, 1, :])
    dare, daim, dldt, dbr_t, dbi_t = _s5_params_bwd(are, aim, ldt, br_t, bi_t, dabr, dabi,
                                                    _unblockdiag(dbbr), _unblockdiag(dbbi))
    fold = lambda a: a.reshape(SSM_G, SSM_H, SSM_P).sum(axis=1)

    grads = {
        "g_mix": dg_mix,
        "a_re": fold(dare), "a_im": fold(daim), "log_dt": fold(dldt).sum(axis=1),
        "b_re": dbr_t, "b_im": dbi_t,
        "c_re": _unblockdiag(dcr).reshape(SSM_G, SSM_H, SSM_P),
        "c_im": _unblockdiag(dci).reshape(SSM_G, SSM_H, SSM_P),
        "d_skip": dd[:, 0, :].reshape(SSM_W),
        "b_glu": db_glu,
        "g_sgu": dg_sgu,
        "w_s": dws,
        "b_s": dbs.reshape(CHUNK, SGU_G, SGU_D).sum(axis=-1).T,
        "g_ffn": dg_ffn,
        "conv_w": dconv[:, 0:3, :].reshape(N_DEV // 2, 3, 2, FF_SHARD).transpose(0, 2, 1, 3).reshape(
            N_DEV, 3, FF_SHARD),
        "conv_b": dconv[:, 3, :].reshape(2 * D_FF),
        "g_final": dg_final,
    }
    small_out(grads, loss)
    return grad_x


_ANY = pl.BlockSpec(memory_space=pl.ANY)
_MESH = pl.DeviceIdType.MESH


def _allgather(shards, dtypes, name, cast_only=()):
    n = len(shards)
    e = len(cast_only)

    def body(*refs):
        in_refs, extra_in = refs[:n], refs[n:n + e]
        out_refs, extra_out = refs[n + e:2 * n + e], refs[2 * n + e:2 * n + 2 * e]
        stage = refs[2 * n + 2 * e:3 * n + 2 * e]
        send_sems, recv_sems, local_sems = refs[3 * n + 2 * e:]
        for a in range(n):
            stage[a][...] = in_refs[a][...].astype(dtypes[a])
        for i in range(e):
            extra_out[i][...] = extra_in[i][...].astype(MXU)
        x, y, c = lax.axis_index("x"), lax.axis_index("y"), lax.axis_index("c")
        me, sibling = (x, y, c), (x, y, 1 - c)
        chips = [(1 - x, y), (x, 1 - y), (1 - x, 1 - y)]

        def slot(a, px, py, pc):
            return out_refs[a].at[4 * px + 2 * py + pc]

        def copy(a, k, block, to, src=None):
            return pltpu.make_async_remote_copy(
                src_ref=slot(a, *block) if src is None else src, dst_ref=slot(a, *block),
                send_sem=send_sems.at[a, k], recv_sem=recv_sems.at[a, k], device_id=to, device_id_type=_MESH)

        mine = [pltpu.make_async_copy(stage[a], slot(a, *me), local_sems.at[a]) for a in range(n)]
        for cp in mine:
            cp.start()
        first = []
        for j, chip in enumerate(chips):
            first += [copy(a, 1 + j, me, (*chip, c), src=stage[a]) for a in range(n)]
        first += [copy(a, 0, me, sibling, src=stage[a]) for a in range(n)]
        for cp in first:
            cp.start()
        passed = []
        for j, chip in enumerate(chips):
            for a in range(n):
                copy(a, 1 + j, (*chip, c), me).wait_recv()
                fwd = copy(a, 4 + j, (*chip, c), sibling)
                fwd.start()
                passed.append(fwd)
        for a in range(n):
            copy(a, 0, sibling, me).wait_recv()
        for j, chip in enumerate(chips):
            for a in range(n):
                copy(a, 4 + j, (*chip, 1 - c), me).wait_recv()
        for cp in first + passed:
            cp.wait_send()
        for cp in mine:
            cp.wait()

    res = pl.pallas_call(
        body, name=name, grid=(1,), in_specs=[_full(s.shape) for s in list(shards) + list(cast_only)],
        out_specs=[_ANY] * n + [_full(s.shape) for s in cast_only],
        out_shape=[_sds((N_DEV,) + s.shape, dt) for s, dt in zip(shards, dtypes)]
                  + [_sds(s.shape, MXU) for s in cast_only],
        scratch_shapes=[pltpu.VMEM(s.shape, dt) for s, dt in zip(shards, dtypes)]
                       + [pltpu.SemaphoreType.DMA((n, 7)), pltpu.SemaphoreType.DMA((n, 7)), pltpu.SemaphoreType.DMA((n,))],
        compiler_params=pltpu.CompilerParams(vmem_limit_bytes=VMEM_LIMIT),
    )(*_in_hbm([*shards, *cast_only]))
    return res[:n], res[n:]


def _all_to_all(sends, name):
    n = len(sends)

    def body(*refs):
        send_refs, recv_refs = refs[:n], refs[n:2 * n]
        send_sems, recv_sems, local_sems = refs[2 * n:]
        x, y, c = lax.axis_index("x"), lax.axis_index("y"), lax.axis_index("c")
        me = 4 * x + 2 * y + c
        mine = [pltpu.make_async_copy(send_refs[a].at[me], recv_refs[a].at[me], local_sems.at[a]) for a in range(n)]
        for cp in mine:
            cp.start()
        copies = []
        for k in (2, 4, 6, 3, 5, 7, 1):
            px = 1 - x if k & 4 else x
            py = 1 - y if k & 2 else y
            pc = 1 - c if k & 1 else c
            peer = 4 * px + 2 * py + pc
            for a in range(n):
                sems = dict(send_sem=send_sems.at[a, k - 1], recv_sem=recv_sems.at[a, k - 1],
                            device_id=(px, py, pc), device_id_type=_MESH)
                cp = pltpu.make_async_remote_copy(src_ref=send_refs[a].at[peer], dst_ref=recv_refs[a].at[me], **sems)
                cp.start()
                landing = pltpu.make_async_remote_copy(src_ref=send_refs[a].at[peer], dst_ref=recv_refs[a].at[peer],
                                                       **sems)
                copies.append((cp, landing))
        for _, landing in copies:
            landing.wait_recv()
        for cp, _ in copies:
            cp.wait_send()
        for cp in mine:
            cp.wait()

    return pl.pallas_call(
        body, name=name, in_specs=[_ANY] * n, out_specs=[_ANY] * n,
        out_shape=[_sds(s.shape, s.dtype) for s in sends],
        scratch_shapes=[pltpu.SemaphoreType.DMA((n, 7)), pltpu.SemaphoreType.DMA((n, 7)), pltpu.SemaphoreType.DMA((n,))],
    )(*sends)


_HBM = pl.BlockSpec(memory_space=pltpu.HBM)
_SEM = pl.BlockSpec(memory_space=pltpu.SEMAPHORE)
_EFFECT = pltpu.SideEffectType.DATAFLOW_SIDE_EFFECTING
_PEER_ORDER = (2, 4, 6, 3, 5, 7, 1)


def _peer(k):
    x, y, c = lax.axis_index("x"), lax.axis_index("y"), lax.axis_index("c")
    px = 1 - x if k & 4 else x
    py = 1 - y if k & 2 else y
    pc = 1 - c if k & 1 else c
    return (px, py, pc), 4 * px + 2 * py + pc


_SAME_CORE_AND_SIBLING = (2, 4, 6, 1)


_SAME_CORE = (2, 4, 6)


def _slots(slotted, peer):
    x, y, c = lax.axis_index("x"), lax.axis_index("y"), lax.axis_index("c")
    if slotted == "chip":
        return peer // 2, 2 * x + y
    return peer, 4 * x + 2 * y + c


def _push_start(srcs, lands, slotted, name, peers=_PEER_ORDER, after=()):
    n = len(srcs)
    e = len(after)

    def body(*refs):
        src_refs, land_refs = refs[:n], refs[n:2 * n]
        send_sems, recv_sems, token = refs[2 * n + e], refs[2 * n + e + 1], refs[-1]
        for k in peers:
            dev, peer = _peer(k)
            theirs, mine = _slots(slotted, peer)
            for a in range(n):
                pltpu.make_async_remote_copy(
                    src_ref=src_refs[a].at[theirs] if slotted else src_refs[a], dst_ref=land_refs[a].at[mine],
                    send_sem=send_sems.at[7 * a + k - 1], recv_sem=recv_sems.at[7 * a + k - 1],
                    device_id=dev, device_id_type=_MESH).start()
        token[...] = jnp.zeros_like(token)

    bufs = list(srcs) + list(lands)
    res = pl.pallas_call(
        body, name=name, in_specs=[_HBM] * (2 * n) + [_ANY] * e,
        out_specs=(_SEM, _SEM, *[_HBM] * (2 * n), pl.BlockSpec(memory_space=pltpu.VMEM)),
        out_shape=(pltpu.SemaphoreType.DMA((7 * n,)), pltpu.SemaphoreType.DMA((7 * n,)),
                   *[pltpu.HBM(b.shape, b.dtype) for b in bufs], _sds((8, LANES))),
        input_output_aliases={i: 2 + i for i in range(2 * n)},
        compiler_params=pltpu.CompilerParams(has_side_effects=_EFFECT),
    )(*[pltpu.with_memory_space_constraint(b, pltpu.HBM) for b in bufs], *after)
    return res[0], res[1], res[2:2 + n], res[2 + n:2 + 2 * n], res[-1]


def _push_wait(send_sems, recv_sems, srcs, lands, slotted, after, name, peers=_PEER_ORDER):
    n = len(srcs)

    def body(*refs):
        src_refs, land_refs = refs[:n], refs[n:2 * n]
        send_sems, recv_sems = refs[2 * n], refs[2 * n + 1]
        for k in peers:
            dev, peer = _peer(k)
            theirs, _ = _slots(slotted, peer)
            for a in range(n):
                cp = pltpu.make_async_remote_copy(
                    src_ref=src_refs[a].at[theirs] if slotted else src_refs[a], dst_ref=land_refs[a].at[theirs],
                    send_sem=send_sems.at[7 * a + k - 1], recv_sem=recv_sems.at[7 * a + k - 1],
                    device_id=dev, device_id_type=_MESH)
                cp.wait_send()
                cp.wait_recv()

    bufs = list(srcs) + list(lands)
    res = pl.pallas_call(
        body, name=name, in_specs=[_HBM] * (2 * n) + [_SEM, _SEM] + [_ANY] * len(after), out_specs=[_HBM] * (2 * n),
        out_shape=[pltpu.HBM(b.shape, b.dtype) for b in bufs],
        input_output_aliases={i: i for i in range(2 * n)},
        compiler_params=pltpu.CompilerParams(has_side_effects=_EFFECT),
    )(*bufs, send_sems, recv_sems, *after)
    return res[n:]


def _pair_plan(send_refs, land_refs, send_sems, recv_sems):
    x, y, c = lax.axis_index("x"), lax.axis_index("y"), lax.axis_index("c")
    return [pltpu.make_async_remote_copy(
        src_ref=send_refs[a].at[2 * j + (1 - c)], dst_ref=land_refs[a].at[j],
        send_sem=send_sems.at[4 * a + j], recv_sem=recv_sems.at[4 * a + j],
        device_id=(x, y, 1 - c), device_id_type=_MESH) for j in range(4) for a in range(len(send_refs))]


def _pair_start(sends, lands, name):
    n = len(sends)

    def body(*refs):
        for cp in _pair_plan(refs[:n], refs[n:2 * n], refs[2 * n], refs[2 * n + 1]):
            cp.start()
        refs[-1][...] = jnp.zeros_like(refs[-1])

    bufs = list(sends) + list(lands)
    res = pl.pallas_call(
        body, name=name, in_specs=[_HBM] * (2 * n),
        out_specs=(_SEM, _SEM, *[_HBM] * (2 * n), pl.BlockSpec(memory_space=pltpu.VMEM)),
        out_shape=(pltpu.SemaphoreType.DMA((4 * n,)), pltpu.SemaphoreType.DMA((4 * n,)),
                   *[pltpu.HBM(b.shape, b.dtype) for b in bufs], _sds((8, LANES))),
        input_output_aliases={i: 2 + i for i in range(2 * n)},
        compiler_params=pltpu.CompilerParams(has_side_effects=_EFFECT),
    )(*[pltpu.with_memory_space_constraint(b, pltpu.HBM) for b in bufs])
    return res[0], res[1], res[2:2 + n], res[2 + n:2 + 2 * n], res[-1]


def _pair_wait(send_sems, recv_sems, sends, lands, after, name):
    n = len(sends)

    def body(*refs):
        for cp in _pair_plan(refs[:n], refs[n:2 * n], refs[2 * n], refs[2 * n + 1]):
            cp.wait_send()
            cp.wait_recv()

    bufs = list(sends) + list(lands)
    res = pl.pallas_call(
        body, name=name, in_specs=[_HBM] * (2 * n) + [_SEM, _SEM] + [_ANY] * len(after), out_specs=[_HBM] * (2 * n),
        out_shape=[pltpu.HBM(b.shape, b.dtype) for b in bufs],
        input_output_aliases={i: i for i in range(2 * n)},
        compiler_params=pltpu.CompilerParams(has_side_effects=_EFFECT),
    )(*bufs, send_sems, recv_sems, *after)
    return res[:n], res[n:]


def _pair_sum(send, land, core, name):
    _, r, c = send.shape
    tr = max(t for t in range(16, 513, 16) if r % t == 0)

    def body(core_ref, s_ref, l_ref, o_ref):
        o_ref[0] = (s_ref[0].astype(F32) + l_ref[0].astype(F32)).astype(MXU)

    return pl.pallas_call(
        body, name=name,
        grid_spec=pltpu.PrefetchScalarGridSpec(
            num_scalar_prefetch=1, grid=(4, r // tr),
            in_specs=[pl.BlockSpec((1, tr, c), lambda j, i, core_ref: (2 * j + core_ref[0], i, 0)),
                      pl.BlockSpec((1, tr, c), lambda j, i, core_ref: (j, i, 0))],
            out_specs=pl.BlockSpec((1, tr, c), lambda j, i, core_ref: (j, i, 0))),
        out_shape=_sds((4, r, c), MXU),
        compiler_params=_cp("parallel", "parallel"),
    )(core, send, land)


def _other_chips():
    x, y = lax.axis_index("x"), lax.axis_index("y")
    return ((1 - x, y), (x, 1 - y), (1 - x, 1 - y))


def _relay_start(lands, name):
    n = len(lands)

    def body(*refs):
        land_refs = refs[:n]
        send_sems, recv_sems, token = refs[n], refs[n + 1], refs[-1]
        x, y, c = lax.axis_index("x"), lax.axis_index("y"), lax.axis_index("c")
        for j, (px, py) in enumerate(_other_chips()):
            slot = 4 * px + 2 * py + c
            for a in range(n):
                pltpu.make_async_remote_copy(
                    src_ref=land_refs[a].at[slot], dst_ref=land_refs[a].at[slot],
                    send_sem=send_sems.at[3 * a + j], recv_sem=recv_sems.at[3 * a + j],
                    device_id=(x, y, 1 - c), device_id_type=_MESH).start()
        token[...] = jnp.zeros_like(token)

    res = pl.pallas_call(
        body, name=name, in_specs=[_HBM] * n,
        out_specs=(_SEM, _SEM, *[_HBM] * n, pl.BlockSpec(memory_space=pltpu.VMEM)),
        out_shape=(pltpu.SemaphoreType.DMA((3 * n,)), pltpu.SemaphoreType.DMA((3 * n,)),
                   *[pltpu.HBM(b.shape, b.dtype) for b in lands], _sds((8, LANES))),
        input_output_aliases={i: 2 + i for i in range(n)},
        compiler_params=pltpu.CompilerParams(has_side_effects=_EFFECT),
    )(*[pltpu.with_memory_space_constraint(b, pltpu.HBM) for b in lands])
    return res[0], res[1], res[2:2 + n], res[-1]


def _relay_wait(send_sems, recv_sems, lands, after, name):
    n = len(lands)

    def body(*refs):
        land_refs = refs[:n]
        send_sems, recv_sems = refs[n], refs[n + 1]
        x, y, c = lax.axis_index("x"), lax.axis_index("y"), lax.axis_index("c")
        for j, (px, py) in enumerate(_other_chips()):
            sent, received = 4 * px + 2 * py + c, 4 * px + 2 * py + (1 - c)
            for a in range(n):
                cp = pltpu.make_async_remote_copy(
                    src_ref=land_refs[a].at[sent], dst_ref=land_refs[a].at[received],
                    send_sem=send_sems.at[3 * a + j], recv_sem=recv_sems.at[3 * a + j],
                    device_id=(x, y, 1 - c), device_id_type=_MESH)
                cp.wait_send()
                cp.wait_recv()

    return pl.pallas_call(
        body, name=name, in_specs=[_HBM] * n + [_SEM, _SEM] + [_ANY] * len(after), out_specs=[_HBM] * n,
        out_shape=[pltpu.HBM(b.shape, b.dtype) for b in lands],
        input_output_aliases={i: i for i in range(n)},
        compiler_params=pltpu.CompilerParams(has_side_effects=_EFFECT),
    )(*lands, send_sems, recv_sems, *after)


def _adamw(w, g, m, v):
    m2 = ADAM_B1 * m + (1.0 - ADAM_B1) * g
    v2 = ADAM_B2 * v + (1.0 - ADAM_B2) * (g * g)
    m_hat = m2 / (1.0 - ADAM_B1 ** ADAM_STEP)
    v_hat = v2 / (1.0 - ADAM_B2 ** ADAM_STEP)
    delta = -ADAM_LR * (m_hat / (jnp.sqrt(v_hat) + ADAM_EPS) + ADAM_WD * w)
    return delta, m2, v2


def _adam_shard(parts, w, m, v, name):
    _, r, c = w.shape
    tr = max(t for t in range(16, 257, 16) if r % t == 0)

    nparts = parts.shape[0]

    def body(p_ref, w_ref, m_ref, v_ref, g_ref, d_ref, m2_ref, v2_ref):
        g = p_ref[0].astype(F32)
        for s in range(1, nparts):
            g = g + p_ref[s].astype(F32)
        g_ref[0] = g
        d_ref[0], m2_ref[0], v2_ref[0] = _adamw(w_ref[0], g, m_ref[0], v_ref[0])

    row = lambda: pl.BlockSpec((1, tr, c), lambda i: (0, i, 0))
    return pl.pallas_call(
        body, name=name, grid=(r // tr,),
        in_specs=[pl.BlockSpec((nparts, tr, c), lambda i: (0, i, 0)), row(), row(), row()],
        out_specs=[row(), row(), row(), row()], out_shape=[_sds((1, r, c))] * 4,
        compiler_params=_cp("parallel"),
    )(*_in_hbm([parts, w, m, v]))


def _adam_small(gs, ws, ms, vs, name):
    n = len(gs)

    def body(*refs):
        ins, outs = refs[:4 * n], refs[4 * n:]
        for i in range(n):
            g = ins[i][...]
            d, m2, v2 = _adamw(ins[n + i][...], g, ins[2 * n + i][...], ins[3 * n + i][...])
            outs[i][...] = d
            outs[n + i][...] = m2
            outs[2 * n + i][...] = v2

    res = pl.pallas_call(
        body, name=name, grid=(1,), in_specs=[_full(w.shape) for w in ws] * 4,
        out_specs=[_full(w.shape) for w in ws] * 3, out_shape=[_sds(w.shape) for w in ws] * 3,
        compiler_params=_cp("arbitrary"),
    )(*_in_hbm([*gs, *ws, *ms, *vs]))
    return res[:n], res[n:2 * n], res[2 * n:]


def _sum_slots(parts, name):
    R = parts.shape[1]

    def body(p_ref, o_ref):
        g = p_ref[0]
        for s in range(1, N_DEV):
            g = g + p_ref[s]
        o_ref[...] = g

    return pl.pallas_call(body, name=name, grid=(1,), in_specs=[_full(parts.shape)], out_specs=_full((R, LANES)),
                          out_shape=_sds((R, LANES)))(*_in_hbm([parts]))


def _pad_to(a, n, axis):
    extra = n - a.shape[axis]
    if extra == 0:
        return a
    widths = [(0, 0)] * a.ndim
    widths[axis] = (0, extra)
    return jnp.pad(a, widths)


def _ceil_to(n, k):
    return -(-n // k) * k


def _pack_rows(flats, rows_multiple):
    parts = [_pad_to(f, _ceil_to(f.shape[-1], LANES), f.ndim - 1) for f in flats]
    cat = jnp.concatenate(parts, axis=-1)
    total = _ceil_to(cat.shape[-1], LANES * rows_multiple)
    cat = _pad_to(cat, total, cat.ndim - 1)
    return cat.reshape(cat.shape[:-1] + (total // LANES, LANES))


def _unpack_rows(buf, sizes):
    flat = buf.reshape(buf.shape[:-2] + (-1,))
    out, off = [], 0
    for n in sizes:
        out.append(flat[..., off:off + n])
        off += _ceil_to(n, LANES)
    return out


_MIX_BIG = ("w_in", "w_glu", "w_proj_a", "w_proj_b", "w_out")
_BIG = _MIX_BIG + ("w_up", "w_down")
_SMALL = ("g_mix", "a_re", "a_im", "log_dt", "b_re", "b_im", "c_re", "c_im", "d_skip", "b_glu", "g_sgu", "w_s", "b_s",
          "g_ffn", "conv_b", "g_final")
_SMALL_ROWS_MULTIPLE = 8 * N_DEV
_TRANSPOSED = ("w_in", "w_up", "b_re", "b_im")


def _as_2d(a):
    return a.reshape(-1, a.shape[-1]) if a.ndim > 1 else a.reshape(1, -1)


def kernel(x, g_mix, w_in, a_re, a_im, log_dt, b_re, b_im, c_re, c_im, d_skip, w_glu, b_glu, w_proj_a, g_sgu, w_s, b_s, w_proj_b, w_out, g_ffn, w_up, conv_w, conv_b, w_down, g_final, loss_target, m_g_mix, m_w_in, m_a_re, m_a_im, m_log_dt, m_b_re, m_b_im, m_c_re, m_c_im, m_d_skip, m_w_glu, m_b_glu, m_w_proj_a, m_g_sgu, m_w_s, m_b_s, m_w_proj_b, m_w_out, m_g_ffn, m_w_up, m_conv_w, m_conv_b, m_w_down, m_g_final, v_g_mix, v_w_in, v_a_re, v_a_im, v_log_dt, v_b_re, v_b_im, v_c_re, v_c_im, v_d_skip, v_w_glu, v_b_glu, v_w_proj_a, v_g_sgu, v_w_s, v_b_s, v_w_proj_b, v_w_out, v_g_ffn, v_w_up, v_conv_w, v_conv_b, v_w_down, v_g_final):
    args = dict(locals())
    me = 4 * lax.axis_index("x") + 2 * lax.axis_index("y") + lax.axis_index("c")

    def own_slot(buf, block):
        return lax.dynamic_update_slice(buf, block[None], (me,) + (0,) * block.ndim)

    for n in _TRANSPOSED:
        for pre in ("", "m_", "v_"):
            args[pre + n] = jnp.swapaxes(args[pre + n], -1, -2)
    later = ("w_glu", "w_proj_a", "w_proj_b", "w_out", "w_up", "w_down")
    (w_in_g,), casts = _allgather([args["w_in"][0]], [MXU], "allgather_w_in", cast_only=[args[n][0] for n in later])
    sh = dict(zip(later, casts))

    def start_push(srcs, tag, peers):
        lands = [own_slot(lax.empty((N_DEV,) + s.shape, s.dtype), s) for s in srcs]
        send_sems, recv_sems, srcs, lands, token = _push_start(srcs, lands, False, "push_" + tag, peers)
        return (send_sems, recv_sems, srcs, lands), token

    mix_push, token_a = start_push([sh[n] for n in later[:4]], "mixer_weights", _SAME_CORE_AND_SIBLING)
    ffn_push, token_b = start_push([sh["w_up"], sh["w_down"], conv_w[0]], "ffn_weights", _PEER_ORDER)
    p = {n: (args[n][0] if n != "g_final" else args[n]) for n in _SMALL if n not in _TRANSPOSED}
    p.update(w_in_t=w_in_g.reshape(SSM_W + 2 * SGU_W + 2 * D_MODEL, D_MODEL),
             b_re_t=args["b_re"][0], b_im_t=args["b_im"][0])
    p["g_mix"] = p["g_mix"] + (token_a[0:1, 0:1] + token_b[0:1, 0:1])
    relay = {}

    def mixer_relay(after):
        lands = _push_wait(*mix_push, False, [after], "wait_mixer_weights", _SAME_CORE_AND_SIBLING)
        relay["send"], relay["recv"], relay["lands"], token = _relay_start(lands, "relay_mixer_weights")
        return token

    def mixer_weights(after):
        w_glu_g, w_pa_g, w_pb_g, w_out_g = _relay_wait(relay["send"], relay["recv"], relay["lands"], [after],
                                                       "wait_relay_mixer_weights")
        w_pa_full, w_pb_full = _assemble_cols([w_pa_g, w_pb_g], "assemble_cols")
        return dict(w_glu=w_glu_g.reshape(SSM_W, SSM_W), w_proj_a=w_pa_full, w_proj_b=w_pb_full,
                    w_out=w_out_g.reshape(D_MODEL, D_MODEL))

    def ffn_weights(after):
        w_up_g, w_down_g, conv_w_g = _push_wait(*ffn_push, False, [after], "wait_ffn_weights")
        return w_up_g, conv_w_g, w_down_g.reshape(D_FF, D_MODEL)

    pushes = []

    def grads_out(names, sends, after=()):
        lands = [own_slot(lax.empty(s.shape, s.dtype), lax.dynamic_index_in_dim(s, me, 0, keepdims=False))
                 for s in sends]
        send_sems, recv_sems, srcs, lands, token = _push_start(list(sends), lands, True, "push_grads_" + names[0],
                                                               after=after)
        pushes.append((names, send_sems, recv_sems, srcs, lands, True, _PEER_ORDER))
        return token

    my_chip = 2 * lax.axis_index("x") + lax.axis_index("y")
    ffn_rs = {}

    def ffn_grads_start(g_up, g_down):
        sends = [g_up, g_down]
        lands = [lax.empty((N_DEV // 2,) + s.shape[1:], s.dtype) for s in sends]
        ffn_rs["pair"] = _pair_start(sends, lands, "pair_grads_ffn")
        return ffn_rs["pair"][4]

    def ffn_grads_relay(after):
        send_sems, recv_sems, sends, lands, _ = ffn_rs["pair"]
        sends, lands = _pair_wait(send_sems, recv_sems, sends, lands, [after], "wait_pair_grads_ffn")
        core = lax.axis_index("c").astype(jnp.int32).reshape(1)
        sums = [_pair_sum(s, l, core, "pair_sum_" + n) for s, l, n in zip(sends, lands, ("w_up", "w_down"))]
        lands2 = [lax.dynamic_update_slice(lax.empty(s.shape, s.dtype),
                                           lax.dynamic_index_in_dim(s, my_chip, 0, keepdims=True), (my_chip, 0, 0))
                  for s in sums]
        send_sems, recv_sems, srcs, lands2, token = _push_start(sums, lands2, "chip", "push_grads_w_up", _SAME_CORE)
        pushes.append((("w_up", "w_down"), send_sems, recv_sems, srcs, lands2, "chip", _SAME_CORE))
        return token

    small_names = _SMALL + ("conv_w", "loss")
    small = {}

    def small_out(grads, loss_part):
        small_g = dict(grads, loss=loss_part[0, 0:1])
        flats = [small_g[n].reshape(-1) for n in small_names]
        small["sizes"] = [f.shape[0] for f in flats]
        g_small = _pack_rows(flats, _SMALL_ROWS_MULTIPLE)
        small["rs8"] = g_small.shape[0] // N_DEV
        return grads_out(("small",), (g_small.reshape(N_DEV, small["rs8"], LANES),))

    grad_x = _local_step(x[0], loss_target[0], p, mixer_relay, mixer_weights, ffn_weights,
                         (ffn_grads_start, ffn_grads_relay), grads_out, small_out)

    out = {}
    done = [grad_x]
    for names, send_sems, recv_sems, srcs, lands, slotted, peers in pushes:
        parts = _push_wait(send_sems, recv_sems, srcs, lands, slotted, done, "wait_grads_" + names[0], peers)
        if names == ("small",):
            small_mine = _sum_slots(parts[0], "sum_small")
            g_small_all = _allgather([small_mine], [F32], "allgather_small")[0][0].reshape(N_DEV * small["rs8"], LANES)
            pieces = dict(zip(small_names, _unpack_rows(g_small_all, small["sizes"])))
            loss = pieces["loss"][0]
            dconv_w = lax.dynamic_index_in_dim(pieces["conv_w"].reshape(N_DEV, 3, FF_SHARD), me, axis=0, keepdims=False)
            names2 = _SMALL + ("conv_w",)
            gs = [pieces[n].reshape(_as_2d(args[n]).shape) for n in _SMALL] + [dconv_w]
            ds, m2s, v2s = _adam_small(gs, [_as_2d(args[n]) for n in names2], [_as_2d(args["m_" + n]) for n in names2],
                                       [_as_2d(args["v_" + n]) for n in names2], "adam_small")
            for n, res in zip(names2, zip(gs, ds, m2s, v2s)):
                for kind, v in zip(("grad_", "delta_", "new_m_", "new_v_"), res):
                    out[kind + n] = v.reshape(args[n].shape)
            done = [ds[0]]
            continue
        for n, part in zip(names, parts):
            res = _adam_shard(part, args[n], args["m_" + n], args["v_" + n], "adam_" + n)
            for kind, v in zip(("grad_", "delta_", "new_m_", "new_v_"), res):
                out[kind + n] = v
            done = [res[0]]
    order = ("g_mix", "w_in", "a_re", "a_im", "log_dt", "b_re", "b_im", "c_re", "c_im", "d_skip", "w_glu", "b_glu",
             "w_proj_a", "g_sgu", "w_s", "b_s", "w_proj_b", "w_out", "g_ffn", "w_up", "conv_w", "conv_b", "w_down",
             "g_final")
    res = [loss, grad_x.reshape(x.shape)]
    for kind in ("grad_", "delta_", "new_m_", "new_v_"):
        res += [jnp.swapaxes(out[kind + n], -1, -2) if n in _TRANSPOSED else out[kind + n] for n in order]
    return tuple(res)
```

```python
import functools
import math

import jax
import jax.numpy as jnp
from jax import lax
from jax.experimental import pallas as pl
from jax.experimental.pallas import tpu as pltpu

F32 = jnp.float32
MXU = jnp.bfloat16
EPS = 1e-6

D_MODEL = 1024
SSM_W = 512
SSM_G, SSM_H, SSM_P = 32, 16, 64
SSM_BLK = 4
SGU_W = 512
SGU_G, SGU_D, CHUNK = 8, 64, 128
D_FF = 2816
N_DEV = 8
FF_SHARD = 2 * D_FF // N_DEV
FF_CW = 2 * FF_SHARD
FF_NCB = D_FF // FF_CW
LANES = 128

ADAM_LR, ADAM_B1, ADAM_B2, ADAM_EPS, ADAM_WD, ADAM_STEP = 0.001, 0.9, 0.999, 1e-08, 0.01, 10

VMEM_LIMIT = 48 * 1024 * 1024
WGRAD_VMEM_LIMIT = 58 * 1024 * 1024
FFN_VMEM_LIMIT = 58 * 1024 * 1024


def _cp(*sem):
    return pltpu.CompilerParams(dimension_semantics=sem, vmem_limit_bytes=VMEM_LIMIT)


def _full(shape):
    n = len(shape)
    return pl.BlockSpec(shape, lambda *_: (0,) * n)


def _sds(shape, dtype=F32):
    return jax.ShapeDtypeStruct(shape, dtype)


def _in_hbm(arrays):
    return [pltpu.with_memory_space_constraint(a, pltpu.HBM) for a in arrays]


def _dot(a, b):
    return jnp.dot(a, b, preferred_element_type=F32)


def _dot_nt(a, b):
    return lax.dot_general(a, b, (((1,), (1,)), ((), ())), preferred_element_type=F32)


def _dot_tn(a, b):
    return lax.dot_general(a, b, (((0,), (0,)), ((), ())), preferred_element_type=F32)


_GELU_C = math.sqrt(2.0 / math.pi)


def _gelu(x):
    return 0.5 * x * (1.0 + jnp.tanh(_GELU_C * (x + 0.044715 * (x * x * x))))


def _gelu_and_grad(x):
    t = jnp.tanh(_GELU_C * (x + 0.044715 * (x * x * x)))
    g = 0.5 * x * (1.0 + t)
    dg = 0.5 * (1.0 + t) + 0.5 * x * (1.0 - t * t) * (_GELU_C * (1.0 + 3.0 * 0.044715 * (x * x)))
    return g, dg


def _sigmoid(x):
    return 0.5 * jnp.tanh(0.5 * x) + 0.5


def _rms(x):
    return lax.rsqrt(jnp.mean(x * x, axis=-1, keepdims=True) + EPS)


def _rms_bwd(dxn, xn, r):
    return r * (dxn - xn * jnp.mean(dxn * xn, axis=-1, keepdims=True))


def _rowsum(x):
    return jnp.sum(x, axis=0, keepdims=True)


def _fetch_once(pairs, sems):
    copies = [pltpu.make_async_copy(src, dst, sems.at[k]) for k, (src, dst) in enumerate(pairs)]
    for cp in copies:
        cp.start()
    for cp in copies:
        cp.wait()


def _s5_disc(are, aim, ldt, br, bi):
    dt = jnp.exp(ldt)
    mag = jnp.exp(dt * are)
    abr = mag * jnp.cos(dt * aim)
    abi = mag * jnp.sin(dt * aim)
    den = are * are + aim * aim
    nr = abr - 1.0
    ni = abi
    fr = (nr * are + ni * aim) / den
    fi = (ni * are - nr * aim) / den
    return abr, abi, fr * br - fi * bi, fr * bi + fi * br


def _s5_params_fwd(are, aim, ldt, br, bi):
    def body(are_ref, aim_ref, ldt_ref, br_ref, bi_ref, o0, o1, o2, o3):
        outs = _s5_disc(are_ref[...], aim_ref[...], ldt_ref[...], br_ref[...], bi_ref[...])
        for o, v in zip((o0, o1, o2, o3), outs):
            o[...] = v
    shp = are.shape
    return pl.pallas_call(body, name="s5_params_fwd", grid=(1,), in_specs=[_full(shp)] * 5, out_specs=[_full(shp)] * 4,
                          out_shape=[_sds(shp)] * 4)(*_in_hbm([are, aim, ldt, br, bi]))


def _s5_params_bwd(are, aim, ldt, br, bi, dabr, dabi, dbr, dbi):
    def body(are_ref, aim_ref, ldt_ref, br_ref, bi_ref, c0, c1, c2, c3, o0, o1, o2, o3, o4):
        prim = (are_ref[...], aim_ref[...], ldt_ref[...], br_ref[...], bi_ref[...])
        _, vjp = jax.vjp(_s5_disc, *prim)
        outs = vjp((c0[...], c1[...], c2[...], c3[...]))
        for o, v in zip((o0, o1, o2, o3, o4), outs):
            o[...] = v
    shp = are.shape
    return pl.pallas_call(body, name="s5_params_bwd", grid=(1,), in_specs=[_full(shp)] * 9, out_specs=[_full(shp)] * 5,
                          out_shape=[_sds(shp)] * 5)(*_in_hbm([are, aim, ldt, br, bi, dabr, dabi, dbr, dbi]))


def _blockdiag(m_t):
    m = m_t.reshape(SSM_BLK, 8, SSM_H, 1, SSM_P)
    eye = jnp.eye(8, dtype=bool).reshape(1, 8, 1, 8, 1)
    return jnp.where(eye, m, jnp.zeros((), m_t.dtype)).reshape(SSM_BLK, 8 * SSM_H, 8 * SSM_P)


def _unblockdiag(pc):
    m = pc.reshape(SSM_BLK, 8, SSM_H, 8, SSM_P)
    return jnp.einsum("jghgp->jghp", m).reshape(SSM_G * SSM_H, SSM_P)


def _in_fwd(x, g_mix, w_in_t, tm):
    S = x.shape[0]

    def body(x_ref, g_ref, w_ref, h_ref, us_ref, uv_ref, gl_ref):
        xv = x_ref[...]
        h = (xv * _rms(xv) * g_ref[...]).astype(MXU)
        h_ref[...] = h
        us_ref[...] = _dot_nt(h, w_ref[0:SSM_W, :])
        uv_ref[...] = _dot_nt(h, w_ref[SSM_W:SSM_W + 2 * SGU_W, :])
        gl_ref[...] = _dot_nt(h, w_ref[SSM_W + 2 * SGU_W:, :])

    row = lambda n: pl.BlockSpec((tm, n), lambda i: (i, 0))
    return pl.pallas_call(
        body, name="in_fwd", grid=(S // tm,),
        in_specs=[row(D_MODEL), _full((1, D_MODEL)), _full(w_in_t.shape)],
        out_specs=[row(D_MODEL), row(SSM_W), row(2 * SGU_W), row(2 * D_MODEL)],
        out_shape=[_sds((S, D_MODEL), MXU), _sds((S, SSM_W)), _sds((S, 2 * SGU_W)), _sds((S, 2 * D_MODEL))],
        compiler_params=_cp("parallel"),
    )(*_in_hbm([x, g_mix, w_in_t]))


def _scan_tables(ar, ai, reverse):
    n = ar.shape[-1]
    def mul(p, q):
        return p[0] * q[0] - p[1] * q[1], p[0] * q[1] + p[1] * q[0]
    a1 = (ar, ai)
    a2 = mul(a1, a1)
    a3 = mul(a2, a1)
    a4 = mul(a2, a2)
    a5 = mul(a4, a1)
    a6 = mul(a4, a2)
    a7 = mul(a4, a3)
    a8 = mul(a4, a4)
    pw = (a1, a2, a3, a4, a5, a6, a7, a8)
    rows = lax.broadcasted_iota(jnp.int32, (8, n), 0)
    tabs = []
    for s, a in ((1, a1), (2, a2), (4, a4)):
        keep = (rows + s <= 7) if reverse else (rows >= s)
        for comp in a:
            tabs.append(jnp.where(keep, jnp.broadcast_to(comp, (8, n)), 0.0))
    for c in range(2):
        q = jnp.zeros((8, n), F32)
        for r in range(8):
            e = (8 - r) if reverse else (r + 1)
            q = jnp.where(rows == r, jnp.broadcast_to(pw[e - 1][c], (8, n)), q)
        tabs.append(q)
    return tabs


def _scan_group(xr, xi, tab_ref, cr, ci, reverse):
    for t, s in enumerate((1, 2, 4)):
        pr = tab_ref[2 * t]
        pi = tab_ref[2 * t + 1]
        sh = (8 - s) if reverse else s
        sr = pltpu.roll(xr, sh, 0)
        si = pltpu.roll(xi, sh, 0)
        xr, xi = xr + pr * sr - pi * si, xi + pr * si + pi * sr
    qr = tab_ref[6]
    qi = tab_ref[7]
    return xr + qr * cr - qi * ci, xi + qr * ci + qi * cr


def _runs_load(src_ref, dst_ref, run):
    for i in range(run):
        dst_ref[8 * i:8 * i + 8, :] = src_ref[pl.ds(i, 8, stride=run), :]


def _runs_store(val, dst_ref, run):
    for i in range(run):
        dst_ref[pl.ds(i, 8, stride=run), :] = val[8 * i:8 * i + 8, :]


def _cpow2(ar, ai, log2n):
    for _ in range(log2n):
        ar, ai = ar * ar - ai * ai, 2.0 * ar * ai
    return ar, ai


def _s5_fwd(us, abar_re, abar_im, b_re, b_im, c_re, c_im, d_skip, tm):
    S = us.shape[0]
    nt = S // tm
    w = 8 * SSM_P
    run = tm // 8
    assert run & (run - 1) == 0

    def body(us_ref, ar_ref, ai_ref, br_ref, bi_ref, cr_ref, ci_ref, d_ref, str_ref, sti_ref, ys_ref,
             tab_ref, car_ref, up_ref):
        i = pl.program_id(1)

        @pl.when(i == 0)
        def _():
            car_ref[...] = jnp.zeros_like(car_ref)
            for k, t in enumerate(_scan_tables(*_cpow2(ar_ref[...], ai_ref[...], run.bit_length() - 1), False)):
                tab_ref[k] = t

        _runs_load(us_ref, up_ref, run)
        ub = up_ref[...].astype(MXU)
        str_ref[...] = _dot(ub, br_ref[0])
        sti_ref[...] = _dot(ub, bi_ref[0])
        ar = jnp.broadcast_to(ar_ref[...], (8, w))
        ai = jnp.broadcast_to(ai_ref[...], (8, w))

        def advance(k, state):
            r0 = pl.multiple_of(k * 8, 8)
            sr, si = state
            return (ar * sr - ai * si + str_ref[pl.ds(r0, 8), :], ar * si + ai * sr + sti_ref[pl.ds(r0, 8), :])

        def emit(k, state):
            r0 = pl.multiple_of(k * 8, 8)
            sr, si = advance(k, state)
            str_ref[pl.ds(r0, 8), :] = sr
            sti_ref[pl.ds(r0, 8), :] = si
            return sr, si

        zero = jnp.zeros((8, w), F32)
        er, ei = lax.fori_loop(0, run, advance, (zero, zero))
        cr, ci = car_ref[0:1, :], car_ref[1:2, :]
        tr, ti = _scan_group(er, ei, tab_ref, cr, ci, False)
        r8 = lax.broadcasted_iota(jnp.int32, (8, w), 0)
        start = (jnp.where(r8 == 0, cr, pltpu.roll(tr, 1, 0)), jnp.where(r8 == 0, ci, pltpu.roll(ti, 1, 0)))
        car_ref[0:1, :] = tr[7:8, :]
        car_ref[1:2, :] = ti[7:8, :]
        lax.fori_loop(0, run, emit, start)
        y = _dot_nt(str_ref[...].astype(MXU), cr_ref[0]) - _dot_nt(sti_ref[...].astype(MXU), ci_ref[0])
        _runs_store(y, ys_ref, run)
        ys_ref[...] += d_ref[...] * us_ref[...]

    blk = lambda: pl.BlockSpec((1, 8 * SSM_H, w), lambda j, i: (j, 0, 0))
    return pl.pallas_call(
        body, name="s5_fwd", grid=(SSM_BLK, nt),
        in_specs=[pl.BlockSpec((tm, LANES), lambda j, i: (i, j)),
                  pl.BlockSpec((1, w), lambda j, i: (0, j)), pl.BlockSpec((1, w), lambda j, i: (0, j)),
                  blk(), blk(), blk(), blk(),
                  pl.BlockSpec((1, LANES), lambda j, i: (0, j))],
        out_specs=[pl.BlockSpec((tm, w), lambda j, i: (i, j)), pl.BlockSpec((tm, w), lambda j, i: (i, j)),
                   pl.BlockSpec((tm, LANES), lambda j, i: (i, j))],
        out_shape=[_sds((S, SSM_BLK * w)), _sds((S, SSM_BLK * w)), _sds((S, SSM_W))],
        scratch_shapes=[pltpu.VMEM((8, 8, w), F32), pltpu.VMEM((8, w), F32), pltpu.VMEM((tm, LANES), F32)],
        compiler_params=_cp("parallel", "arbitrary"),
    )(*_in_hbm([us, abar_re, abar_im, b_re, b_im, c_re, c_im, d_skip]))


def _group_halves(vp):
    first = lax.broadcasted_iota(jnp.int32, vp.shape, 1) < SGU_D
    zero = jnp.zeros((), vp.dtype)
    return jnp.where(first, vp, zero), jnp.where(first, zero, vp)


def _sgu_mix(vnb, wcat_ref):
    outs = []
    for q in range(SGU_G // 2):
        lo, hi = _group_halves(vnb[:, LANES * q:LANES * (q + 1)])
        outs.append(_dot(wcat_ref[q], jnp.concatenate([lo, hi], axis=0)))
    return jnp.concatenate(outs, axis=1)


def _mix_fwd(x, ys, uv, gl, w_glu, b_glu, w_pa, g_sgu, ws, bias_s, w_pb, w_out, g_ffn, tm):
    S = x.shape[0]

    def body(x_ref, ys_ref, uv_ref, gl_ref, wglu_ref, bglu_ref, wpa_ref, gs_ref, ws_ref, bias_ref, wpb_ref, wout_ref,
             gf_ref, yg_ref, yap_ref, sg_ref, ya_ref, yb_ref, m_ref, x1_ref, h2_ref):
        yg = _gelu(ys_ref[...])
        ygb = yg.astype(MXU)
        yg_ref[...] = ygb
        z = _dot(ygb, wglu_ref[...]) + bglu_ref[...]
        yapb = (yg * _sigmoid(z)).astype(MXU)
        yap_ref[...] = yapb
        ya = _dot(yapb, wpa_ref[...])
        ya_ref[...] = ya

        uvg = _gelu(uv_ref[...])
        u2 = uvg[:, :SGU_W]
        v2 = uvg[:, SGU_W:]
        vnb = (v2 * _rms(v2) * gs_ref[...]).astype(MXU)
        for c in range(tm // CHUNK):
            rs = slice(c * CHUNK, (c + 1) * CHUNK)
            mixed = _sgu_mix(vnb[rs], ws_ref) + bias_ref[...]
            sg_ref[rs, :] = (u2[rs] * mixed).astype(MXU)
        yb = _dot(sg_ref[...], wpb_ref[...])
        yb_ref[...] = yb

        glv = gl_ref[...]
        m = _sigmoid(glv[:, :D_MODEL]) * ya + _sigmoid(glv[:, D_MODEL:]) * yb
        mb = m.astype(MXU)
        m_ref[...] = mb
        x1 = x_ref[...] + _dot(mb, wout_ref[...])
        x1_ref[...] = x1
        h2_ref[...] = (x1 * _rms(x1) * gf_ref[...]).astype(MXU)

    row = lambda n: pl.BlockSpec((tm, n), lambda i: (i, 0))
    return pl.pallas_call(
        body, name="mix_fwd", grid=(S // tm,),
        in_specs=[row(D_MODEL), row(SSM_W), row(2 * SGU_W), row(2 * D_MODEL),
                  _full(w_glu.shape), _full(b_glu.shape), _full(w_pa.shape), _full(g_sgu.shape), _full(ws.shape),
                  _full(bias_s.shape), _full(w_pb.shape), _full(w_out.shape), _full(g_ffn.shape)],
        out_specs=[row(SSM_W), row(SSM_W), row(SGU_W), row(D_MODEL), row(D_MODEL), row(D_MODEL), row(D_MODEL),
                   row(D_MODEL)],
        out_shape=[_sds((S, SSM_W), MXU), _sds((S, SSM_W), MXU), _sds((S, SGU_W), MXU), _sds((S, D_MODEL)),
                   _sds((S, D_MODEL)), _sds((S, D_MODEL), MXU), _sds((S, D_MODEL)), _sds((S, D_MODEL), MXU)],
        compiler_params=_cp("parallel"),
    )(*_in_hbm([x, ys, uv, gl, w_glu, b_glu, w_pa, g_sgu, ws, bias_s, w_pb, w_out, g_ffn]))


def _causal_conv3(u, prev8, cw, cb):
    tm = u.shape[0]
    w0, w1, w2 = cw[0:1], cw[1:2], cw[2:3]
    body = w0 * pltpu.roll(u, 2, 0) + w1 * pltpu.roll(u, 1, 0) + w2 * u + cb
    u8 = u[0:8, :]
    r8 = lax.broadcasted_iota(jnp.int32, u8.shape, 0)
    t1 = prev8[7:8, :]
    t0 = prev8[6:7, :]
    s1 = jnp.where(r8 == 0, t1, pltpu.roll(u8, 1, 0))
    s2 = jnp.where(r8 == 0, t0, jnp.where(r8 == 1, t1, pltpu.roll(u8, 2, 0)))
    first = w0 * s2 + w1 * s1 + w2 * u8 + cb
    return jnp.concatenate([first, body[8:tm, :]], axis=0)


def _causal_conv3_adjoint(d, next8, cw):
    tm = d.shape[0]
    w0, w1, w2 = cw[0:1], cw[1:2], cw[2:3]
    n1 = pltpu.roll(d, tm - 1, 0)
    n2 = pltpu.roll(d, tm - 2, 0)
    body = w2 * d + w1 * n1 + w0 * n2
    d8 = d[tm - 8:tm, :]
    r8 = lax.broadcasted_iota(jnp.int32, d8.shape, 0)
    h0 = next8[0:1, :]
    h1 = next8[1:2, :]
    m1 = jnp.where(r8 == 7, h0, pltpu.roll(d8, 7, 0))
    m2 = jnp.where(r8 == 6, h0, jnp.where(r8 == 7, h1, pltpu.roll(d8, 6, 0)))
    last = w2 * d8 + w1 * m1 + w0 * m2
    out = jnp.concatenate([body[0:tm - 8, :], last], axis=0)
    return out, n1, n2, h0 - d[0:1, :], h1 - d[1:2, :]


def _ffn_fwd(h2, x1, tgt, w_up, conv_w, conv_b, w_down, g_final, tm):
    S = h2.shape[0]
    nt = S // tm
    ncb = FF_NCB

    def body(h2_ref, wup_hbm, cwa_ref, cwb_ref, cba_ref, cbb_ref, wd_hbm, x1_ref, gf_ref, tgt_ref,
             up_ref, ab_ref, ff_ref, dx2_ref, dx2b_ref, loss_ref, dgf_ref, acc_ref, tail_ref, wup_ref, wdn_ref, wsem):
        i = pl.program_id(0)
        cb = pl.program_id(1)

        @pl.when(i == 0)
        def _():
            tail_ref[cb] = jnp.zeros((2, 8, FF_CW), F32)

        @pl.when(jnp.logical_and(i == 0, cb == 0))
        def _():
            loss_ref[...] = jnp.zeros_like(loss_ref)
            dgf_ref[...] = jnp.zeros_like(dgf_ref)
            _fetch_once([(wup_hbm, wup_ref), (wd_hbm, wdn_ref)], wsem)

        h2v = h2_ref[...]
        ua = _dot_nt(h2v, wup_ref[cb])
        ub = _dot_nt(h2v, wup_ref[ncb + cb])
        up_ref[0, 0] = ua.astype(MXU)
        up_ref[1, 0] = ub.astype(MXU)
        a = _causal_conv3(ua, tail_ref[cb, 0], cwa_ref[0], cba_ref[0])
        b = _causal_conv3(ub, tail_ref[cb, 1], cwb_ref[0], cbb_ref[0])
        tail_ref[cb, 0] = ua[tm - 8:tm, :]
        tail_ref[cb, 1] = ub[tm - 8:tm, :]
        ab_ref[0, 0] = a
        ab_ref[1, 0] = b
        ffb = (a * _sigmoid(a) * b).astype(MXU)
        ff_ref[0] = ffb
        contrib = _dot(ffb, wdn_ref[pl.ds(pl.multiple_of(cb * FF_CW, FF_CW), FF_CW), :])

        @pl.when(cb == 0)
        def _():
            acc_ref[...] = contrib

        @pl.when(cb > 0)
        def _():
            acc_ref[...] += contrib

        @pl.when(cb == ncb - 1)
        def _():
            x2 = x1_ref[...] + acc_ref[...]
            r = _rms(x2)
            xn = x2 * r
            g = gf_ref[...]
            diff = xn * g - tgt_ref[...]
            loss_ref[...] += (0.5 / D_MODEL) * jnp.sum(diff * diff)
            dy = diff * (1.0 / D_MODEL)
            dgf_ref[...] += _rowsum(dy * xn)
            dx2 = _rms_bwd(dy * g, xn, r)
            dx2_ref[...] = dx2
            dx2b_ref[...] = dx2.astype(MXU)

    row = lambda n: pl.BlockSpec((tm, n), lambda i, c: (i, 0))
    gate = lambda r: pl.BlockSpec((1, r, FF_CW), lambda i, c: (c, 0, 0))
    lin = lambda r: pl.BlockSpec((1, r, FF_CW), lambda i, c: (ncb + c, 0, 0))
    return pl.pallas_call(
        body, name="ffn_fwd", grid=(nt, ncb),
        in_specs=[row(D_MODEL), _ANY, gate(3), lin(3), gate(1), lin(1), _ANY,
                  row(D_MODEL), _full((1, D_MODEL)), row(D_MODEL)],
        out_specs=[pl.BlockSpec((2, 1, tm, FF_CW), lambda i, c: (0, c, i, 0)),
                   pl.BlockSpec((2, 1, tm, FF_CW), lambda i, c: (0, c, i, 0)),
                   pl.BlockSpec((1, tm, FF_CW), lambda i, c: (c, i, 0)),
                   row(D_MODEL), row(D_MODEL), _full((1, LANES)), _full((1, D_MODEL))],
        out_shape=[_sds((2, ncb, S, FF_CW), MXU), _sds((2, ncb, S, FF_CW)), _sds((ncb, S, FF_CW), MXU),
                   _sds((S, D_MODEL)), _sds((S, D_MODEL), MXU), _sds((1, LANES)), _sds((1, D_MODEL))],
        scratch_shapes=[pltpu.VMEM((tm, D_MODEL), F32), pltpu.VMEM((ncb, 2, 8, FF_CW), F32),
                        pltpu.VMEM(w_up.shape, w_up.dtype), pltpu.VMEM(w_down.shape, w_down.dtype),
                        pltpu.SemaphoreType.DMA((2,))],
        compiler_params=pltpu.CompilerParams(dimension_semantics=("arbitrary", "arbitrary"),
                                             vmem_limit_bytes=FFN_VMEM_LIMIT),
    )(*_in_hbm([h2, w_up, conv_w, conv_w, conv_b, conv_b, w_down, x1, g_final, tgt]))


def _ffn_bwd(dx2, up, ab, x1, w_up, conv_w, w_down, g_ffn, tm):
    S = dx2.shape[0]
    nt = S // tm
    ncb = FF_NCB

    def body(dx2_ref, up_ref, ab_ref, cwa_ref, cwb_ref, wd_hbm, wup_hbm,
             x1_ref, g_ref, dup_ref, dx1_ref, dx1b_ref, dconv_ref, dg_ref, acc_ref, head_ref, wup_ref, wdn_ref, wsem):
        i = pl.program_id(0)
        cb = pl.program_id(1)

        @pl.when(i == 0)
        def _():
            head_ref[cb] = jnp.zeros((2, 8, FF_CW), F32)
            dconv_ref[cb] = jnp.zeros((8, FF_CW), F32)
            dconv_ref[ncb + cb] = jnp.zeros((8, FF_CW), F32)

        @pl.when(jnp.logical_and(i == 0, cb == 0))
        def _():
            dg_ref[...] = jnp.zeros_like(dg_ref)
            _fetch_once([(wup_hbm, wup_ref), (wd_hbm, wdn_ref)], wsem)

        dff = _dot_nt(dx2_ref[...].astype(MXU), wdn_ref[pl.ds(pl.multiple_of(cb * FF_CW, FF_CW), FF_CW), :])
        a = ab_ref[0, 0]
        b = ab_ref[1, 0]
        sa = _sigmoid(a)
        silu = a * sa
        da = (dff * b) * (sa + silu * (1.0 - sa))
        db = dff * silu
        dps = []
        for half, slot, d, cw_ref in ((0, cb, da, cwa_ref), (1, ncb + cb, db, cwb_ref)):
            dp, n1, n2, fix0, fix1 = _causal_conv3_adjoint(d, head_ref[cb, half], cw_ref[0])
            head_ref[cb, half] = d[0:8, :]
            dpb16 = dp.astype(MXU)
            dup_ref[half, 0] = dpb16
            dps.append(dpb16)
            u = up_ref[half, 0].astype(F32)
            u_last = u[tm - 1:tm, :]
            dconv_ref[slot, 0:1, :] += _rowsum(n2 * u) + fix0 * u[tm - 2:tm - 1, :] + fix1 * u_last
            dconv_ref[slot, 1:2, :] += _rowsum(n1 * u) + fix0 * u_last
            dconv_ref[slot, 2:3, :] += _rowsum(d * u)
            dconv_ref[slot, 3:4, :] += _rowsum(d)
        contrib = _dot(dps[0], wup_ref[cb]) + _dot(dps[1], wup_ref[ncb + cb])

        @pl.when(cb == 0)
        def _():
            acc_ref[...] = contrib

        @pl.when(cb > 0)
        def _():
            acc_ref[...] += contrib

        @pl.when(cb == ncb - 1)
        def _():
            x1v = x1_ref[...]
            r = _rms(x1v)
            xn = x1v * r
            dh2 = acc_ref[...]
            dg_ref[...] += _rowsum(dh2 * xn)
            dx1 = dx2_ref[...] + _rms_bwd(dh2 * g_ref[...], xn, r)
            dx1_ref[...] = dx1
            dx1b_ref[...] = dx1.astype(MXU)

    row = lambda n: pl.BlockSpec((tm, n), lambda i, c: (nt - 1 - i, 0))
    colb = lambda: pl.BlockSpec((2, 1, tm, FF_CW), lambda i, c: (0, c, nt - 1 - i, 0))
    gate = lambda r: pl.BlockSpec((1, r, FF_CW), lambda i, c: (c, 0, 0))
    lin = lambda r: pl.BlockSpec((1, r, FF_CW), lambda i, c: (ncb + c, 0, 0))
    return pl.pallas_call(
        body, name="ffn_bwd", grid=(nt, ncb),
        in_specs=[row(D_MODEL), colb(), colb(), gate(3), lin(3), _ANY, _ANY, row(D_MODEL), _full((1, D_MODEL))],
        out_specs=[colb(), row(D_MODEL), row(D_MODEL), _full((2 * ncb, 8, FF_CW)), _full((1, D_MODEL))],
        out_shape=[_sds((2, ncb, S, FF_CW), MXU), _sds((S, D_MODEL)), _sds((S, D_MODEL), MXU), _sds((2 * ncb, 8, FF_CW)),
                   _sds((1, D_MODEL))],
        scratch_shapes=[pltpu.VMEM((tm, D_MODEL), F32), pltpu.VMEM((ncb, 2, 8, FF_CW), F32),
                        pltpu.VMEM(w_up.shape, w_up.dtype), pltpu.VMEM(w_down.shape, w_down.dtype),
                        pltpu.SemaphoreType.DMA((2,))],
        compiler_params=pltpu.CompilerParams(dimension_semantics=("arbitrary", "arbitrary"),
                                             vmem_limit_bytes=FFN_VMEM_LIMIT),
    )(*_in_hbm([dx2, up, ab, conv_w, conv_w, w_down, w_up, x1, g_ffn]))


def _mix_bwd(dx1, gl, ya, yb, ys, uv, w_out, w_pa, w_pb, w_glu, b_glu, g_sgu, ws, ws_t, bias_s, tm):
    S = dx1.shape[0]

    def body(dx1_ref, gl_ref, ya_ref, yb_ref, ys_ref, uv_ref, wout_ref, wpa_ref, wpb_ref, wglu_ref, bglu_ref, gs_ref,
             ws_ref, wst_ref, bias_ref,
             dgl_ref, dya_ref, dyb_ref, dz_ref, dys_ref, duv_ref, dbglu_ref, dgs_ref, dws_ref, dbs_ref,
             du2_ref, dvn_ref):
        i = pl.program_id(0)

        @pl.when(i == 0)
        def _():
            dbglu_ref[...] = jnp.zeros_like(dbglu_ref)
            dgs_ref[...] = jnp.zeros_like(dgs_ref)
            dws_ref[...] = jnp.zeros_like(dws_ref)
            dbs_ref[...] = jnp.zeros_like(dbs_ref)

        dm = _dot_nt(dx1_ref[...].astype(MXU), wout_ref[...])
        glv = gl_ref[...]
        ga = _sigmoid(glv[:, :D_MODEL])
        gb = _sigmoid(glv[:, D_MODEL:])
        dgl_ref[:, :D_MODEL] = (dm * ya_ref[...] * ga * (1.0 - ga)).astype(MXU)
        dgl_ref[:, D_MODEL:] = (dm * yb_ref[...] * gb * (1.0 - gb)).astype(MXU)
        dyab = (dm * ga).astype(MXU)
        dybb = (dm * gb).astype(MXU)
        dya_ref[...] = dyab
        dyb_ref[...] = dybb

        dyap = _dot_nt(dyab, wpa_ref[...])
        yg, dgelu = _gelu_and_grad(ys_ref[...])
        sz = _sigmoid(_dot(yg.astype(MXU), wglu_ref[...]) + bglu_ref[...])
        dz = dyap * yg * sz * (1.0 - sz)
        dzb = dz.astype(MXU)
        dz_ref[...] = dzb
        dbglu_ref[...] += _rowsum(dz)
        dys_ref[...] = (dyap * sz + _dot_nt(dzb, wglu_ref[...])) * dgelu

        dsg = _dot_nt(dybb, wpb_ref[...])
        uvg, duvg = _gelu_and_grad(uv_ref[...])
        u2 = uvg[:, :SGU_W]
        v2 = uvg[:, SGU_W:]
        rv = _rms(v2)
        vhat = v2 * rv
        gs = gs_ref[...]
        vnb = (vhat * gs).astype(MXU)
        tril = (lax.broadcasted_iota(jnp.int32, (CHUNK, CHUNK), 0)
                >= lax.broadcasted_iota(jnp.int32, (CHUNK, CHUNK), 1))
        for c in range(tm // CHUNK):
            rs = slice(c * CHUNK, (c + 1) * CHUNK)
            vc = vnb[rs]
            mixed = _sgu_mix(vc, ws_ref) + bias_ref[...]
            dsg_c = dsg[rs]
            du2_ref[rs, :] = dsg_c * mixed
            dmx = dsg_c * u2[rs]
            dbs_ref[...] += dmx
            dmb = dmx.astype(MXU)
            dvn_ref[rs, :] = _sgu_mix(dmb, wst_ref)
            for q in range(SGU_G // 2):
                lanes = slice(LANES * q, LANES * (q + 1))
                for j, part in enumerate(_group_halves(dmb[:, lanes])):
                    dws_ref[2 * q + j] += jnp.where(tril, _dot_nt(part, vc[:, lanes]), 0.0)
        dvn = dvn_ref[...]
        dgs_ref[...] += _rowsum(dvn * vhat)
        dv2 = _rms_bwd(dvn * gs, vhat, rv)
        duv_ref[:, :SGU_W] = (du2_ref[...] * duvg[:, :SGU_W]).astype(MXU)
        duv_ref[:, SGU_W:] = (dv2 * duvg[:, SGU_W:]).astype(MXU)

    row = lambda n: pl.BlockSpec((tm, n), lambda i: (i, 0))
    return pl.pallas_call(
        body, name="mix_bwd", grid=(S // tm,),
        in_specs=[row(D_MODEL), row(2 * D_MODEL), row(D_MODEL), row(D_MODEL), row(SSM_W), row(2 * SGU_W),
                  _full(w_out.shape), _full(w_pa.shape), _full(w_pb.shape), _full(w_glu.shape), _full(b_glu.shape),
                  _full(g_sgu.shape), _full(ws.shape), _full(ws_t.shape), _full(bias_s.shape)],
        out_specs=[row(2 * D_MODEL), row(D_MODEL), row(D_MODEL), row(SSM_W), row(SSM_W), row(2 * SGU_W),
                   _full((1, SSM_W)), _full((1, SGU_W)), _full((SGU_G, CHUNK, CHUNK)), _full((CHUNK, SGU_W))],
        out_shape=[_sds((S, 2 * D_MODEL), MXU), _sds((S, D_MODEL), MXU), _sds((S, D_MODEL), MXU), _sds((S, SSM_W), MXU),
                   _sds((S, SSM_W)), _sds((S, 2 * SGU_W), MXU),
                   _sds((1, SSM_W)), _sds((1, SGU_W)), _sds((SGU_G, CHUNK, CHUNK)), _sds((CHUNK, SGU_W))],
        scratch_shapes=[pltpu.VMEM((tm, SGU_W), F32), pltpu.VMEM((tm, SGU_W), F32)],
        compiler_params=_cp("arbitrary"),
    )(*_in_hbm([dx1, gl, ya, yb, ys, uv, w_out, w_pa, w_pb, w_glu, b_glu, g_sgu, ws, ws_t, bias_s]))


def _s5_bwd(dys, us, st_re, st_im, abar_re, abar_im, b_re, b_im, c_re, c_im, d_skip, tm):
    S = us.shape[0]
    nt = S // tm
    w = 8 * SSM_P
    hb = tm // 8
    run = tm // 8
    assert run & (run - 1) == 0

    def body(dys_ref, us_ref, str_ref, sti_ref, hr_ref, hi_ref, ar_ref, ai_ref, br_ref, bi_ref, cr_ref, ci_ref, d_ref,
             dus_ref, dab_ref, dd_ref, dbr_ref, dbi_ref, dcr_ref, dci_ref,
             tab_ref, car_ref, gr_ref, gi_ref, dyp_ref, up_ref, dun_ref):
        i = pl.program_id(1)
        ri = nt - 1 - i

        @pl.when(i == 0)
        def _():
            car_ref[...] = jnp.zeros_like(car_ref)
            for k, t in enumerate(_scan_tables(*_cpow2(ar_ref[...], -ai_ref[...], run.bit_length() - 1), True)):
                tab_ref[k] = t
            for r in (dab_ref, dd_ref, dbr_ref, dbi_ref, dcr_ref, dci_ref):
                r[...] = jnp.zeros_like(r)

        _runs_load(dys_ref, dyp_ref, run)
        _runs_load(us_ref, up_ref, run)
        dyb = dyp_ref[...].astype(MXU)
        gr_ref[...] = _dot(dyb, cr_ref[0])
        gi_ref[...] = -_dot(dyb, ci_ref[0])
        ar = jnp.broadcast_to(ar_ref[...], (8, w))
        ai = jnp.broadcast_to(-ai_ref[...], (8, w))

        def advance(kk, state):
            r0 = pl.multiple_of((run - 1 - kk) * 8, 8)
            gr, gi = state
            return (ar * gr - ai * gi + gr_ref[pl.ds(r0, 8), :], ar * gi + ai * gr + gi_ref[pl.ds(r0, 8), :])

        def emit(kk, state):
            r0 = pl.multiple_of((run - 1 - kk) * 8, 8)
            gr, gi = advance(kk, state)
            gr_ref[pl.ds(r0, 8), :] = gr
            gi_ref[pl.ds(r0, 8), :] = gi
            return gr, gi

        zero = jnp.zeros((8, w), F32)
        er, ei = lax.fori_loop(0, run, advance, (zero, zero))
        cr, ci = car_ref[0:1, :], car_ref[1:2, :]
        tr, ti = _scan_group(er, ei, tab_ref, cr, ci, True)
        r8 = lax.broadcasted_iota(jnp.int32, (8, w), 0)
        start = (jnp.where(r8 == 7, cr, pltpu.roll(tr, 7, 0)), jnp.where(r8 == 7, ci, pltpu.roll(ti, 7, 0)))
        car_ref[0:1, :] = tr[0:1, :]
        car_ref[1:2, :] = ti[0:1, :]
        lax.fori_loop(0, run, emit, start)

        gsr = gr_ref[...]
        gsi = gi_ref[...]
        sr = str_ref[...]
        si = sti_ref[...]
        first = ri == 0

        def previous(s, halo_ref):
            head = jnp.where(r8 == 0, jnp.where(first, 0.0, halo_ref[7:8, :]), pltpu.roll(s[tm - 8:tm, :], 1, 0))
            return jnp.concatenate([head, s[0:tm - 8, :]], axis=0)

        spr = previous(sr, hr_ref)
        spi = previous(si, hi_ref)
        dab_ref[0, 0:1, :] += _rowsum(gsr * spr + gsi * spi)
        dab_ref[0, 1:2, :] += _rowsum(gsi * spr - gsr * spi)

        gbr = gsr.astype(MXU)
        gbi = gsi.astype(MXU)
        _runs_store(_dot_nt(gbr, br_ref[0]) + _dot_nt(gbi, bi_ref[0]), dun_ref, run)
        dys_v = dys_ref[...]
        dus_ref[...] = (dun_ref[...] + d_ref[...] * dys_v).astype(MXU)
        dd_ref[0, 0:1, :] += _rowsum(dys_v * us_ref[...])
        ub = up_ref[...].astype(MXU)
        dbr_ref[0] += _dot_tn(ub, gbr)
        dbi_ref[0] += _dot_tn(ub, gbi)
        dcr_ref[0] += _dot_tn(dyb, sr.astype(MXU))
        dci_ref[0] -= _dot_tn(dyb, si.astype(MXU))

    blk = lambda: pl.BlockSpec((1, 8 * SSM_H, w), lambda j, i: (j, 0, 0))
    rowl = lambda: pl.BlockSpec((tm, LANES), lambda j, i: (nt - 1 - i, j))
    roww = lambda: pl.BlockSpec((tm, w), lambda j, i: (nt - 1 - i, j))
    halo = lambda: pl.BlockSpec((8, w), lambda j, i: (jnp.maximum((nt - 1 - i) * hb - 1, 0), j))
    return pl.pallas_call(
        body, name="s5_bwd", grid=(SSM_BLK, nt),
        in_specs=[rowl(), rowl(), roww(), roww(), halo(), halo(),
                  pl.BlockSpec((1, w), lambda j, i: (0, j)), pl.BlockSpec((1, w), lambda j, i: (0, j)),
                  blk(), blk(), blk(), blk(),
                  pl.BlockSpec((1, LANES), lambda j, i: (0, j))],
        out_specs=[rowl(),
                   pl.BlockSpec((1, 8, w), lambda j, i: (j, 0, 0)), pl.BlockSpec((1, 8, LANES), lambda j, i: (j, 0, 0)),
                   blk(), blk(), blk(), blk()],
        out_shape=[_sds((S, SSM_W), MXU), _sds((SSM_BLK, 8, w)), _sds((SSM_BLK, 8, LANES)),
                   _sds((SSM_BLK, 8 * SSM_H, w)), _sds((SSM_BLK, 8 * SSM_H, w)),
                   _sds((SSM_BLK, 8 * SSM_H, w)), _sds((SSM_BLK, 8 * SSM_H, w))],
        scratch_shapes=[pltpu.VMEM((8, 8, w), F32), pltpu.VMEM((8, w), F32),
                        pltpu.VMEM((tm, w), F32), pltpu.VMEM((tm, w), F32),
                        pltpu.VMEM((tm, LANES), F32), pltpu.VMEM((tm, LANES), F32), pltpu.VMEM((tm, LANES), F32)],
        compiler_params=_cp("parallel", "arbitrary"),
    )(*_in_hbm([dys, us, st_re, st_im, st_re, st_im, abar_re, abar_im, b_re, b_im, c_re, c_im, d_skip]))


def _in_bwd(dus, duv, dgl, dx1, x, g_mix, w_in, tm):
    S = x.shape[0]

    def body(dus_ref, duv_ref, dgl_ref, dx1_ref, x_ref, g_ref, w_ref, gx_ref, dg_ref):
        @pl.when(pl.program_id(0) == 0)
        def _():
            dg_ref[...] = jnp.zeros_like(dg_ref)

        dh = (_dot(dus_ref[...], w_ref[0:SSM_W, :])
              + _dot(duv_ref[...], w_ref[SSM_W:SSM_W + 2 * SGU_W, :])
              + _dot(dgl_ref[...], w_ref[SSM_W + 2 * SGU_W:, :]))
        xv = x_ref[...]
        r = _rms(xv)
        xn = xv * r
        dg_ref[...] += _rowsum(dh * xn)
        gx_ref[...] = dx1_ref[...] + _rms_bwd(dh * g_ref[...], xn, r)

    row = lambda n: pl.BlockSpec((tm, n), lambda i: (i, 0))
    return pl.pallas_call(
        body, name="in_bwd", grid=(S // tm,),
        in_specs=[row(SSM_W), row(2 * SGU_W), row(2 * D_MODEL), row(D_MODEL), row(D_MODEL), _full((1, D_MODEL)),
                  _full(w_in.shape)],
        out_specs=[row(D_MODEL), _full((1, D_MODEL))],
        out_shape=[_sds((S, D_MODEL)), _sds((1, D_MODEL))],
        compiler_params=_cp("arbitrary"),
    )(*_in_hbm([dus, duv, dgl, dx1, x, g_mix, w_in]))


def _pick(n, cands):
    for c in cands:
        if n % c == 0:
            return c
    return n


def _wgrad_split(a, b, nsplit, tk, name):
    S, K = a.shape
    N = b.shape[1]
    c = N // nsplit

    def body(a_ref, b_ref, o_ref):
        prod = _dot_tn(a_ref[...], b_ref[...])
        for d in range(nsplit):
            o_ref[d] = prod[:, c * d:c * (d + 1)].astype(MXU)

    return pl.pallas_call(
        body, name=name, grid=(K // tk,),
        in_specs=[pl.BlockSpec((S, tk), lambda k: (0, k)), _full((S, N))],
        out_specs=pl.BlockSpec((nsplit, tk, c), lambda k: (0, k, 0)),
        out_shape=_sds((nsplit, K, c), MXU),
        compiler_params=_cp("parallel"),
    )(*_in_hbm([a, b]))


def _wgrad_in_t(dps, h1, name, after=()):
    S, K = h1.shape
    cw = 512
    counts = [b.shape[1] // cw for b in dps]
    starts = [sum(counts[:i]) for i in range(len(dps))]
    nblk = sum(counts)

    def body(*refs):
        b_refs = refs[:len(dps)]
        h_ref, o_ref = refs[len(dps)], refs[-1]
        j = pl.program_id(0)
        for b_ref, st, cnt in zip(b_refs, starts, counts):
            @pl.when(jnp.logical_and(j >= st, j < st + cnt))
            def _():
                o_ref[...] = _dot_tn(b_ref[...], h_ref[...]).astype(MXU)

    def src_spec(st, cnt):
        return pl.BlockSpec((S, cw), lambda j: (0, jnp.clip(j - st, 0, cnt - 1)))

    return pl.pallas_call(
        body, name=name, grid=(nblk,),
        in_specs=[src_spec(st, cnt) for st, cnt in zip(starts, counts)] + [_full((S, K))] + [_ANY] * len(after),
        out_specs=pl.BlockSpec((cw, K), lambda j: (j, 0)),
        out_shape=_sds((nblk * cw, K), MXU),
        compiler_params=_cp("arbitrary"),
    )(*_in_hbm([*dps, h1]), *after)


def _wgrad_blk(a3, b3, nblk, a_of, b_of, name):
    S, K = a3.shape[1:]
    N = b3.shape[2]

    def body(a_ref, b_ref, o_ref):
        o_ref[0] = _dot_tn(a_ref[0], b_ref[0]).astype(MXU)

    return pl.pallas_call(
        body, name=name, grid=(nblk,),
        in_specs=[pl.BlockSpec((1, S, K), lambda b: (a_of(b), 0, 0)),
                  pl.BlockSpec((1, S, N), lambda b: (b_of(b), 0, 0))],
        out_specs=pl.BlockSpec((1, K, N), lambda b: (b, 0, 0)),
        out_shape=_sds((nblk, K, N), MXU),
        compiler_params=pltpu.CompilerParams(dimension_semantics=("parallel",), vmem_limit_bytes=WGRAD_VMEM_LIMIT),
    )(*_in_hbm([a3, b3]))


def _assemble_cols(blocks_list, name):
    def body(*refs):
        n = len(blocks_list)
        for b_ref, o_ref in zip(refs[:n], refs[n:]):
            c = b_ref.shape[2]
            for d in range(N_DEV):
                o_ref[:, c * d:c * (d + 1)] = b_ref[d]

    outs = [_sds((b.shape[1], N_DEV * b.shape[2]), b.dtype) for b in blocks_list]
    return pl.pallas_call(
        body, name=name, grid=(1,), in_specs=[_full(b.shape) for b in blocks_list],
        out_specs=[_full(o.shape) for o in outs], out_shape=outs, compiler_params=_cp("arbitrary"),
    )(*_in_hbm(blocks_list))


def _tile(S, want):
    return want if S % want == 0 else S


def _local_step(x, tgt, p, mixer_relay, mixer_weights, ffn_weights, grads_out, w_in_grads_out, small_out):
    S = x.shape[0]
    tm = _tile(S, 256)
    tl = _tile(S, 512)

    rep = lambda a: jnp.repeat(a, SSM_H, axis=0)
    are = rep(p["a_re"])
    aim = rep(p["a_im"])
    ldt = jnp.broadcast_to(rep(p["log_dt"].reshape(SSM_G, 1)), are.shape)
    br_t = p["b_re_t"].reshape(are.shape)
    bi_t = p["b_im_t"].reshape(are.shape)
    abr, abi, bbr, bbi = _s5_params_fwd(are, aim, ldt, br_t, bi_t)
    head = lambda a: a.reshape(SSM_G, SSM_H, SSM_P)[:, 0, :].reshape(1, SSM_G * SSM_P)
    abar_re, abar_im = head(abr), head(abi)
    bd_br = _blockdiag(bbr).astype(MXU)
    bd_bi = _blockdiag(bbi).astype(MXU)
    bd_cr = _blockdiag(p["c_re"].reshape(are.shape)).astype(MXU)
    bd_ci = _blockdiag(p["c_im"].reshape(are.shape)).astype(MXU)
    d_skip = p["d_skip"].reshape(1, SSM_W)

    tril = jnp.tril(jnp.ones((CHUNK, CHUNK), dtype=bool))
    ws = jnp.where(tril[None], p["w_s"], 0.0)
    pair = lambda w: w.reshape(SGU_G // 2, 2, CHUNK, CHUNK).transpose(0, 2, 1, 3).reshape(SGU_G // 2, CHUNK, 2 * CHUNK)
    ws_b = pair(ws).astype(MXU)
    ws_t = pair(ws.transpose(0, 2, 1)).astype(MXU)
    bias_s = jnp.repeat(p["b_s"].T, SGU_D, axis=1)

    g_mix = p["g_mix"].reshape(1, D_MODEL)
    g_ffn = p["g_ffn"].reshape(1, D_MODEL)
    g_final = p["g_final"].reshape(1, D_MODEL)
    g_sgu = p["g_sgu"].reshape(1, SGU_W)
    b_glu = p["b_glu"].reshape(1, SSM_W)
    conv_b = p["conv_b"].reshape(2 * FF_NCB, 1, FF_CW)
    tf = _tile(S, 256)
    ts = _tile(S, 1024)

    h1, us, uv, gl = _in_fwd(x, g_mix, p["w_in_t"], tl)
    token = mixer_relay(us)
    st_re, st_im, ys = _s5_fwd(us, abar_re, abar_im, bd_br, bd_bi, bd_cr, bd_ci, d_skip + token[0:1, 0:1], ts)
    p = dict(p, **mixer_weights(ys))
    yg, yap, sg, ya, yb, m, x1, h2 = _mix_fwd(x, ys, uv, gl, p["w_glu"], b_glu, p["w_proj_a"], g_sgu, ws_b, bias_s,
                                              p["w_proj_b"], p["w_out"], g_ffn, tl)
    w_up, conv_w, w_down = ffn_weights(h2)
    pair_lanes = lambda a: a.reshape(N_DEV // 2, 2, a.shape[1], FF_SHARD).transpose(0, 2, 1, 3).reshape(
        N_DEV // 2, a.shape[1], FF_CW)
    w_up = w_up.reshape(2 * FF_NCB, FF_CW, D_MODEL)
    conv_w = pair_lanes(conv_w)
    up, ab, ff, dx2, dx2b, loss, dg_final = _ffn_fwd(h2, x1, tgt, w_up, conv_w, conv_b, w_down, g_final, tf)

    dup, dx1, dx1b, dconv, dg_ffn = _ffn_bwd(dx2, up, ab, x1, w_up, conv_w, w_down, g_ffn, tf)
    rows8 = lambda g: g.reshape(N_DEV, g.shape[1] // N_DEV, g.shape[2])
    g_up = _wgrad_blk(dup.reshape(2 * FF_NCB, S, FF_CW), h2[None], 2 * FF_NCB, lambda b: b, lambda b: 0,
                      "wgrad_up").reshape(N_DEV, FF_SHARD, D_MODEL)
    g_down = _wgrad_blk(ff, dx2b[None], FF_NCB, lambda b: b, lambda b: 0, "wgrad_down").reshape(
        N_DEV, D_FF // N_DEV, D_MODEL)
    token = grads_out(("w_up", "w_down"), (g_up, g_down))
    dgl, dya, dyb, dz, dys, duv, db_glu, dg_sgu, dws, dbs = _mix_bwd(
        dx1, gl, ya, yb, ys, uv, p["w_out"], p["w_proj_a"], p["w_proj_b"], p["w_glu"], b_glu + token[0:1, 0:1], g_sgu,
        ws_b, ws_t, bias_s, tm)
    token = grads_out(("w_glu", "w_proj_a", "w_proj_b", "w_out"),
                      (rows8(_wgrad_split(yg, dz, 1, SSM_W, "wgrad_glu")),
                       _wgrad_split(yap, dya, N_DEV, SSM_W, "wgrad_pa"),
                       _wgrad_split(sg, dyb, N_DEV, SGU_W, "wgrad_pb"),
                       rows8(_wgrad_split(m, dx1b, 1, 512, "wgrad_out"))))
    dus, dab, dd, dbbr, dbbi, dcr, dci = _s5_bwd(dys, us, st_re, st_im, abar_re, abar_im, bd_br, bd_bi, bd_cr, bd_ci,
                                                 d_skip + token[0:1, 0:1], ts)
    g_in = _wgrad_in_t([dus, duv, dgl], h1, "wgrad_in")
    token = w_in_grads_out[0](g_in.reshape(N_DEV, g_in.shape[0] // N_DEV, D_MODEL))
    grad_x, dg_mix = _in_bwd(dus, duv, dgl, dx1, x, g_mix + token[0:1, 0:1], p["w_in_t"], tl)
    w_in_grads_out[1](grad_x)

    spread = lambda v: jnp.repeat(v.reshape(SSM_G, SSM_P), SSM_H, axis=0) * (1.0 / SSM_H)
    dabr = spread(dab[:, 0, :])
    dabi = spread(dab[:, 1, :])
    dare, daim, dldt, dbr_t, dbi_t = _s5_params_bwd(are, aim, ldt, br_t, bi_t, dabr, dabi,
                                                    _unblockdiag(dbbr), _unblockdiag(dbbi))
    fold = lambda a: a.reshape(SSM_G, SSM_H, SSM_P).sum(axis=1)

    grads = {
        "g_mix": dg_mix,
        "a_re": fold(dare), "a_im": fold(daim), "log_dt": fold(dldt).sum(axis=1),
        "b_re": dbr_t, "b_im": dbi_t,
        "c_re": _unblockdiag(dcr).reshape(SSM_G, SSM_H, SSM_P),
        "c_im": _unblockdiag(dci).reshape(SSM_G, SSM_H, SSM_P),
        "d_skip": dd[:, 0, :].reshape(SSM_W),
        "b_glu": db_glu,
        "g_sgu": dg_sgu,
        "w_s": dws,
        "b_s": dbs.reshape(CHUNK, SGU_G, SGU_D).sum(axis=-1).T,
        "g_ffn": dg_ffn,
        "conv_w": dconv[:, 0:3, :].reshape(N_DEV // 2, 3, 2, FF_SHARD).transpose(0, 2, 1, 3).reshape(
            N_DEV, 3, FF_SHARD),
        "conv_b": dconv[:, 3, :].reshape(2 * D_FF),
        "g_final": dg_final,
    }
    small_out(grads, loss)
    return grad_x


_ANY = pl.BlockSpec(memory_space=pl.ANY)
_MESH = pl.DeviceIdType.MESH


def _allgather(shards, dtypes, name, cast_only=()):
    n = len(shards)
    e = len(cast_only)

    def body(*refs):
        in_refs, extra_in = refs[:n], refs[n:n + e]
        out_refs, extra_out = refs[n + e:2 * n + e], refs[2 * n + e:2 * n + 2 * e]
        stage = refs[2 * n + 2 * e:3 * n + 2 * e]
        send_sems, recv_sems, local_sems = refs[3 * n + 2 * e:]
        for a in range(n):
            stage[a][...] = in_refs[a][...].astype(dtypes[a])
        for i in range(e):
            extra_out[i][...] = extra_in[i][...].astype(MXU)
        x, y, c = lax.axis_index("x"), lax.axis_index("y"), lax.axis_index("c")
        me, sibling = (x, y, c), (x, y, 1 - c)
        chips = [(1 - x, y), (x, 1 - y), (1 - x, 1 - y)]

        def slot(a, px, py, pc):
            return out_refs[a].at[4 * px + 2 * py + pc]

        def copy(a, k, block, to, src=None):
            return pltpu.make_async_remote_copy(
                src_ref=slot(a, *block) if src is None else src, dst_ref=slot(a, *block),
                send_sem=send_sems.at[a, k], recv_sem=recv_sems.at[a, k], device_id=to, device_id_type=_MESH)

        mine = [pltpu.make_async_copy(stage[a], slot(a, *me), local_sems.at[a]) for a in range(n)]
        for cp in mine:
            cp.start()
        first = []
        for j, chip in enumerate(chips):
            first += [copy(a, 1 + j, me, (*chip, c), src=stage[a]) for a in range(n)]
        first += [copy(a, 0, me, sibling, src=stage[a]) for a in range(n)]
        for cp in first:
            cp.start()
        passed = []
        for j, chip in enumerate(chips):
            for a in range(n):
                copy(a, 1 + j, (*chip, c), me).wait_recv()
                fwd = copy(a, 4 + j, (*chip, c), sibling)
                fwd.start()
                passed.append(fwd)
        for a in range(n):
            copy(a, 0, sibling, me).wait_recv()
        for j, chip in enumerate(chips):
            for a in range(n):
                copy(a, 4 + j, (*chip, 1 - c), me).wait_recv()
        for cp in first + passed:
            cp.wait_send()
        for cp in mine:
            cp.wait()

    res = pl.pallas_call(
        body, name=name, grid=(1,), in_specs=[_full(s.shape) for s in list(shards) + list(cast_only)],
        out_specs=[_ANY] * n + [_full(s.shape) for s in cast_only],
        out_shape=[_sds((N_DEV,) + s.shape, dt) for s, dt in zip(shards, dtypes)]
                  + [_sds(s.shape, MXU) for s in cast_only],
        scratch_shapes=[pltpu.VMEM(s.shape, dt) for s, dt in zip(shards, dtypes)]
                       + [pltpu.SemaphoreType.DMA((n, 7)), pltpu.SemaphoreType.DMA((n, 7)), pltpu.SemaphoreType.DMA((n,))],
        compiler_params=pltpu.CompilerParams(vmem_limit_bytes=VMEM_LIMIT),
    )(*_in_hbm([*shards, *cast_only]))
    return res[:n], res[n:]


def _all_to_all(sends, name):
    n = len(sends)

    def body(*refs):
        send_refs, recv_refs = refs[:n], refs[n:2 * n]
        send_sems, recv_sems, local_sems = refs[2 * n:]
        x, y, c = lax.axis_index("x"), lax.axis_index("y"), lax.axis_index("c")
        me = 4 * x + 2 * y + c
        mine = [pltpu.make_async_copy(send_refs[a].at[me], recv_refs[a].at[me], local_sems.at[a]) for a in range(n)]
        for cp in mine:
            cp.start()
        copies = []
        for k in (2, 4, 6, 3, 5, 7, 1):
            px = 1 - x if k & 4 else x
            py = 1 - y if k & 2 else y
            pc = 1 - c if k & 1 else c
            peer = 4 * px + 2 * py + pc
            for a in range(n):
                sems = dict(send_sem=send_sems.at[a, k - 1], recv_sem=recv_sems.at[a, k - 1],
                            device_id=(px, py, pc), device_id_type=_MESH)
                cp = pltpu.make_async_remote_copy(src_ref=send_refs[a].at[peer], dst_ref=recv_refs[a].at[me], **sems)
                cp.start()
                landing = pltpu.make_async_remote_copy(src_ref=send_refs[a].at[peer], dst_ref=recv_refs[a].at[peer],
                                                       **sems)
                copies.append((cp, landing))
        for _, landing in copies:
            landing.wait_recv()
        for cp, _ in copies:
            cp.wait_send()
        for cp in mine:
            cp.wait()

    return pl.pallas_call(
        body, name=name, in_specs=[_ANY] * n, out_specs=[_ANY] * n,
        out_shape=[_sds(s.shape, s.dtype) for s in sends],
        scratch_shapes=[pltpu.SemaphoreType.DMA((n, 7)), pltpu.SemaphoreType.DMA((n, 7)), pltpu.SemaphoreType.DMA((n,))],
    )(*sends)


_HBM = pl.BlockSpec(memory_space=pltpu.HBM)
_SEM = pl.BlockSpec(memory_space=pltpu.SEMAPHORE)
_EFFECT = pltpu.SideEffectType.DATAFLOW_SIDE_EFFECTING
_PEER_ORDER = (2, 4, 6, 3, 5, 7, 1)


def _peer(k):
    x, y, c = lax.axis_index("x"), lax.axis_index("y"), lax.axis_index("c")
    px = 1 - x if k & 4 else x
    py = 1 - y if k & 2 else y
    pc = 1 - c if k & 1 else c
    return (px, py, pc), 4 * px + 2 * py + pc


_SAME_CORE_AND_SIBLING = (2, 4, 6, 1)


_SAME_CORE = (2, 4, 6)


def _slots(slotted, peer):
    x, y, c = lax.axis_index("x"), lax.axis_index("y"), lax.axis_index("c")
    if slotted == "chip":
        return peer // 2, 2 * x + y
    return peer, 4 * x + 2 * y + c


def _push_start(srcs, lands, slotted, name, peers=_PEER_ORDER, after=()):
    n = len(srcs)
    e = len(after)

    def body(*refs):
        src_refs, land_refs = refs[:n], refs[n:2 * n]
        send_sems, recv_sems, token = refs[2 * n + e], refs[2 * n + e + 1], refs[-1]
        for k in peers:
            dev, peer = _peer(k)
            theirs, mine = _slots(slotted, peer)
            for a in range(n):
                pltpu.make_async_remote_copy(
                    src_ref=src_refs[a].at[theirs] if slotted else src_refs[a], dst_ref=land_refs[a].at[mine],
                    send_sem=send_sems.at[7 * a + k - 1], recv_sem=recv_sems.at[7 * a + k - 1],
                    device_id=dev, device_id_type=_MESH).start()
        token[...] = jnp.zeros_like(token)

    bufs = list(srcs) + list(lands)
    res = pl.pallas_call(
        body, name=name, in_specs=[_HBM] * (2 * n) + [_ANY] * e,
        out_specs=(_SEM, _SEM, *[_HBM] * (2 * n), pl.BlockSpec(memory_space=pltpu.VMEM)),
        out_shape=(pltpu.SemaphoreType.DMA((7 * n,)), pltpu.SemaphoreType.DMA((7 * n,)),
                   *[pltpu.HBM(b.shape, b.dtype) for b in bufs], _sds((8, LANES))),
        input_output_aliases={i: 2 + i for i in range(2 * n)},
        compiler_params=pltpu.CompilerParams(has_side_effects=_EFFECT),
    )(*[pltpu.with_memory_space_constraint(b, pltpu.HBM) for b in bufs], *after)
    return res[0], res[1], res[2:2 + n], res[2 + n:2 + 2 * n], res[-1]


def _push_wait(send_sems, recv_sems, srcs, lands, slotted, after, name, peers=_PEER_ORDER):
    n = len(srcs)

    def body(*refs):
        src_refs, land_refs = refs[:n], refs[n:2 * n]
        send_sems, recv_sems = refs[2 * n], refs[2 * n + 1]
        for k in peers:
            dev, peer = _peer(k)
            theirs, _ = _slots(slotted, peer)
            for a in range(n):
                cp = pltpu.make_async_remote_copy(
                    src_ref=src_refs[a].at[theirs] if slotted else src_refs[a], dst_ref=land_refs[a].at[theirs],
                    send_sem=send_sems.at[7 * a + k - 1], recv_sem=recv_sems.at[7 * a + k - 1],
                    device_id=dev, device_id_type=_MESH)
                cp.wait_send()
                cp.wait_recv()

    bufs = list(srcs) + list(lands)
    res = pl.pallas_call(
        body, name=name, in_specs=[_HBM] * (2 * n) + [_SEM, _SEM] + [_ANY] * len(after), out_specs=[_HBM] * (2 * n),
        out_shape=[pltpu.HBM(b.shape, b.dtype) for b in bufs],
        input_output_aliases={i: i for i in range(2 * n)},
        compiler_params=pltpu.CompilerParams(has_side_effects=_EFFECT),
    )(*bufs, send_sems, recv_sems, *after)
    return res[n:]


def _pair_plan(send_refs, land_refs, send_sems, recv_sems):
    x, y, c = lax.axis_index("x"), lax.axis_index("y"), lax.axis_index("c")
    return [pltpu.make_async_remote_copy(
        src_ref=send_refs[a].at[2 * j + (1 - c)], dst_ref=land_refs[a].at[j],
        send_sem=send_sems.at[4 * a + j], recv_sem=recv_sems.at[4 * a + j],
        device_id=(x, y, 1 - c), device_id_type=_MESH) for j in range(4) for a in range(len(send_refs))]


def _pair_start(sends, lands, name):
    n = len(sends)

    def body(*refs):
        for cp in _pair_plan(refs[:n], refs[n:2 * n], refs[2 * n], refs[2 * n + 1]):
            cp.start()
        refs[-1][...] = jnp.zeros_like(refs[-1])

    bufs = list(sends) + list(lands)
    res = pl.pallas_call(
        body, name=name, in_specs=[_HBM] * (2 * n),
        out_specs=(_SEM, _SEM, *[_HBM] * (2 * n), pl.BlockSpec(memory_space=pltpu.VMEM)),
        out_shape=(pltpu.SemaphoreType.DMA((4 * n,)), pltpu.SemaphoreType.DMA((4 * n,)),
                   *[pltpu.HBM(b.shape, b.dtype) for b in bufs], _sds((8, LANES))),
        input_output_aliases={i: 2 + i for i in range(2 * n)},
        compiler_params=pltpu.CompilerParams(has_side_effects=_EFFECT),
    )(*[pltpu.with_memory_space_constraint(b, pltpu.HBM) for b in bufs])
    return res[0], res[1], res[2:2 + n], res[2 + n:2 + 2 * n], res[-1]


def _pair_wait(send_sems, recv_sems, sends, lands, after, name):
    n = len(sends)

    def body(*refs):
        for cp in _pair_plan(refs[:n], refs[n:2 * n], refs[2 * n], refs[2 * n + 1]):
            cp.wait_send()
            cp.wait_recv()

    bufs = list(sends) + list(lands)
    res = pl.pallas_call(
        body, name=name, in_specs=[_HBM] * (2 * n) + [_SEM, _SEM] + [_ANY] * len(after), out_specs=[_HBM] * (2 * n),
        out_shape=[pltpu.HBM(b.shape, b.dtype) for b in bufs],
        input_output_aliases={i: i for i in range(2 * n)},
        compiler_params=pltpu.CompilerParams(has_side_effects=_EFFECT),
    )(*bufs, send_sems, recv_sems, *after)
    return res[:n], res[n:]


def _pair_sum(send, land, core, name):
    _, r, c = send.shape
    tr = max(t for t in range(16, 513, 16) if r % t == 0)

    def body(core_ref, s_ref, l_ref, o_ref):
        o_ref[0] = (s_ref[0].astype(F32) + l_ref[0].astype(F32)).astype(MXU)

    return pl.pallas_call(
        body, name=name,
        grid_spec=pltpu.PrefetchScalarGridSpec(
            num_scalar_prefetch=1, grid=(4, r // tr),
            in_specs=[pl.BlockSpec((1, tr, c), lambda j, i, core_ref: (2 * j + core_ref[0], i, 0)),
                      pl.BlockSpec((1, tr, c), lambda j, i, core_ref: (j, i, 0))],
            out_specs=pl.BlockSpec((1, tr, c), lambda j, i, core_ref: (j, i, 0))),
        out_shape=_sds((4, r, c), MXU),
        compiler_params=_cp("parallel", "parallel"),
    )(core, send, land)


def _other_chips():
    x, y = lax.axis_index("x"), lax.axis_index("y")
    return ((1 - x, y), (x, 1 - y), (1 - x, 1 - y))


def _relay_start(lands, name):
    n = len(lands)

    def body(*refs):
        land_refs = refs[:n]
        send_sems, recv_sems, token = refs[n], refs[n + 1], refs[-1]
        x, y, c = lax.axis_index("x"), lax.axis_index("y"), lax.axis_index("c")
        for j, (px, py) in enumerate(_other_chips()):
            slot = 4 * px + 2 * py + c
            for a in range(n):
                pltpu.make_async_remote_copy(
                    src_ref=land_refs[a].at[slot], dst_ref=land_refs[a].at[slot],
                    send_sem=send_sems.at[3 * a + j], recv_sem=recv_sems.at[3 * a + j],
                    device_id=(x, y, 1 - c), device_id_type=_MESH).start()
        token[...] = jnp.zeros_like(token)

    res = pl.pallas_call(
        body, name=name, in_specs=[_HBM] * n,
        out_specs=(_SEM, _SEM, *[_HBM] * n, pl.BlockSpec(memory_space=pltpu.VMEM)),
        out_shape=(pltpu.SemaphoreType.DMA((3 * n,)), pltpu.SemaphoreType.DMA((3 * n,)),
                   *[pltpu.HBM(b.shape, b.dtype) for b in lands], _sds((8, LANES))),
        input_output_aliases={i: 2 + i for i in range(n)},
        compiler_params=pltpu.CompilerParams(has_side_effects=_EFFECT),
    )(*[pltpu.with_memory_space_constraint(b, pltpu.HBM) for b in lands])
    return res[0], res[1], res[2:2 + n], res[-1]


def _relay_wait(send_sems, recv_sems, lands, after, name):
    n = len(lands)

    def body(*refs):
        land_refs = refs[:n]
        send_sems, recv_sems = refs[n], refs[n + 1]
        x, y, c = lax.axis_index("x"), lax.axis_index("y"), lax.axis_index("c")
        for j, (px, py) in enumerate(_other_chips()):
            sent, received = 4 * px + 2 * py + c, 4 * px + 2 * py + (1 - c)
            for a in range(n):
                cp = pltpu.make_async_remote_copy(
                    src_ref=land_refs[a].at[sent], dst_ref=land_refs[a].at[received],
                    send_sem=send_sems.at[3 * a + j], recv_sem=recv_sems.at[3 * a + j],
                    device_id=(x, y, 1 - c), device_id_type=_MESH)
                cp.wait_send()
                cp.wait_recv()

    return pl.pallas_call(
        body, name=name, in_specs=[_HBM] * n + [_SEM, _SEM] + [_ANY] * len(after), out_specs=[_HBM] * n,
        out_shape=[pltpu.HBM(b.shape, b.dtype) for b in lands],
        input_output_aliases={i: i for i in range(n)},
        compiler_params=pltpu.CompilerParams(has_side_effects=_EFFECT),
    )(*lands, send_sems, recv_sems, *after)


def _adamw(w, g, m, v):
    m2 = ADAM_B1 * m + (1.0 - ADAM_B1) * g
    v2 = ADAM_B2 * v + (1.0 - ADAM_B2) * (g * g)
    m_hat = m2 / (1.0 - ADAM_B1 ** ADAM_STEP)
    v_hat = v2 / (1.0 - ADAM_B2 ** ADAM_STEP)
    delta = -ADAM_LR * (m_hat / (jnp.sqrt(v_hat) + ADAM_EPS) + ADAM_WD * w)
    return delta, m2, v2


def _adam_shard(parts, w, m, v, name):
    _, r, c = w.shape
    tr = max(t for t in range(16, 257, 16) if r % t == 0)

    nparts = parts.shape[0]

    def body(p_ref, w_ref, m_ref, v_ref, g_ref, d_ref, m2_ref, v2_ref):
        g = p_ref[0].astype(F32)
        for s in range(1, nparts):
            g = g + p_ref[s].astype(F32)
        g_ref[0] = g
        d_ref[0], m2_ref[0], v2_ref[0] = _adamw(w_ref[0], g, m_ref[0], v_ref[0])

    row = lambda: pl.BlockSpec((1, tr, c), lambda i: (0, i, 0))
    return pl.pallas_call(
        body, name=name, grid=(r // tr,),
        in_specs=[pl.BlockSpec((nparts, tr, c), lambda i: (0, i, 0)), row(), row(), row()],
        out_specs=[row(), row(), row(), row()], out_shape=[_sds((1, r, c))] * 4,
        compiler_params=_cp("parallel"),
    )(*_in_hbm([parts, w, m, v]))


def _adam_small(gs, ws, ms, vs, name):
    n = len(gs)

    def body(*refs):
        ins, outs = refs[:4 * n], refs[4 * n:]
        for i in range(n):
            g = ins[i][...]
            d, m2, v2 = _adamw(ins[n + i][...], g, ins[2 * n + i][...], ins[3 * n + i][...])
            outs[i][...] = d
            outs[n + i][...] = m2
            outs[2 * n + i][...] = v2

    res = pl.pallas_call(
        body, name=name, grid=(1,), in_specs=[_full(w.shape) for w in ws] * 4,
        out_specs=[_full(w.shape) for w in ws] * 3, out_shape=[_sds(w.shape) for w in ws] * 3,
        compiler_params=_cp("arbitrary"),
    )(*_in_hbm([*gs, *ws, *ms, *vs]))
    return res[:n], res[n:2 * n], res[2 * n:]


def _sum_slots(parts, name):
    R = parts.shape[1]

    def body(p_ref, o_ref):
        g = p_ref[0]
        for s in range(1, N_DEV):
            g = g + p_ref[s]
        o_ref[...] = g

    return pl.pallas_call(body, name=name, grid=(1,), in_specs=[_full(parts.shape)], out_specs=_full((R, LANES)),
                          out_shape=_sds((R, LANES)))(*_in_hbm([parts]))


def _pad_to(a, n, axis):
    extra = n - a.shape[axis]
    if extra == 0:
        return a
    widths = [(0, 0)] * a.ndim
    widths[axis] = (0, extra)
    return jnp.pad(a, widths)


def _ceil_to(n, k):
    return -(-n // k) * k


def _pack_rows(flats, rows_multiple):
    parts = [_pad_to(f, _ceil_to(f.shape[-1], LANES), f.ndim - 1) for f in flats]
    cat = jnp.concatenate(parts, axis=-1)
    total = _ceil_to(cat.shape[-1], LANES * rows_multiple)
    cat = _pad_to(cat, total, cat.ndim - 1)
    return cat.reshape(cat.shape[:-1] + (total // LANES, LANES))


def _unpack_rows(buf, sizes):
    flat = buf.reshape(buf.shape[:-2] + (-1,))
    out, off = [], 0
    for n in sizes:
        out.append(flat[..., off:off + n])
        off += _ceil_to(n, LANES)
    return out


_MIX_BIG = ("w_in", "w_glu", "w_proj_a", "w_proj_b", "w_out")
_BIG = _MIX_BIG + ("w_up", "w_down")
_SMALL = ("g_mix", "a_re", "a_im", "log_dt", "b_re", "b_im", "c_re", "c_im", "d_skip", "b_glu", "g_sgu", "w_s", "b_s",
          "g_ffn", "conv_b", "g_final")
_SMALL_ROWS_MULTIPLE = 8 * N_DEV
_TRANSPOSED = ("w_in", "w_up", "b_re", "b_im")


def _as_2d(a):
    return a.reshape(-1, a.shape[-1]) if a.ndim > 1 else a.reshape(1, -1)


def kernel(x, g_mix, w_in, a_re, a_im, log_dt, b_re, b_im, c_re, c_im, d_skip, w_glu, b_glu, w_proj_a, g_sgu, w_s, b_s, w_proj_b, w_out, g_ffn, w_up, conv_w, conv_b, w_down, g_final, loss_target, m_g_mix, m_w_in, m_a_re, m_a_im, m_log_dt, m_b_re, m_b_im, m_c_re, m_c_im, m_d_skip, m_w_glu, m_b_glu, m_w_proj_a, m_g_sgu, m_w_s, m_b_s, m_w_proj_b, m_w_out, m_g_ffn, m_w_up, m_conv_w, m_conv_b, m_w_down, m_g_final, v_g_mix, v_w_in, v_a_re, v_a_im, v_log_dt, v_b_re, v_b_im, v_c_re, v_c_im, v_d_skip, v_w_glu, v_b_glu, v_w_proj_a, v_g_sgu, v_w_s, v_b_s, v_w_proj_b, v_w_out, v_g_ffn, v_w_up, v_conv_w, v_conv_b, v_w_down, v_g_final):
    args = dict(locals())
    me = 4 * lax.axis_index("x") + 2 * lax.axis_index("y") + lax.axis_index("c")

    def own_slot(buf, block):
        return lax.dynamic_update_slice(buf, block[None], (me,) + (0,) * block.ndim)

    for n in _TRANSPOSED:
        for pre in ("", "m_", "v_"):
            args[pre + n] = jnp.swapaxes(args[pre + n], -1, -2)
    later = ("w_glu", "w_proj_a", "w_proj_b", "w_out", "w_up", "w_down")
    (w_in_g,), casts = _allgather([args["w_in"][0]], [MXU], "allgather_w_in", cast_only=[args[n][0] for n in later])
    sh = dict(zip(later, casts))

    def start_push(srcs, tag, peers):
        lands = [own_slot(lax.empty((N_DEV,) + s.shape, s.dtype), s) for s in srcs]
        send_sems, recv_sems, srcs, lands, token = _push_start(srcs, lands, False, "push_" + tag, peers)
        return (send_sems, recv_sems, srcs, lands), token

    mix_push, token_a = start_push([sh[n] for n in later[:4]], "mixer_weights", _SAME_CORE_AND_SIBLING)
    ffn_push, token_b = start_push([sh["w_up"], sh["w_down"], conv_w[0]], "ffn_weights", _PEER_ORDER)
    p = {n: (args[n][0] if n != "g_final" else args[n]) for n in _SMALL if n not in _TRANSPOSED}
    p.update(w_in_t=w_in_g.reshape(SSM_W + 2 * SGU_W + 2 * D_MODEL, D_MODEL),
             b_re_t=args["b_re"][0], b_im_t=args["b_im"][0])
    p["g_mix"] = p["g_mix"] + (token_a[0:1, 0:1] + token_b[0:1, 0:1])
    relay = {}

    def mixer_relay(after):
        lands = _push_wait(*mix_push, False, [after], "wait_mixer_weights", _SAME_CORE_AND_SIBLING)
        relay["send"], relay["recv"], relay["lands"], token = _relay_start(lands, "relay_mixer_weights")
        return token

    def mixer_weights(after):
        w_glu_g, w_pa_g, w_pb_g, w_out_g = _relay_wait(relay["send"], relay["recv"], relay["lands"], [after],
                                                       "wait_relay_mixer_weights")
        w_pa_full, w_pb_full = _assemble_cols([w_pa_g, w_pb_g], "assemble_cols")
        return dict(w_glu=w_glu_g.reshape(SSM_W, SSM_W), w_proj_a=w_pa_full, w_proj_b=w_pb_full,
                    w_out=w_out_g.reshape(D_MODEL, D_MODEL))

    def ffn_weights(after):
        w_up_g, w_down_g, conv_w_g = _push_wait(*ffn_push, False, [after], "wait_ffn_weights")
        return w_up_g, conv_w_g, w_down_g.reshape(D_FF, D_MODEL)

    pushes = []

    def grads_out(names, sends, after=()):
        lands = [own_slot(lax.empty(s.shape, s.dtype), lax.dynamic_index_in_dim(s, me, 0, keepdims=False))
                 for s in sends]
        send_sems, recv_sems, srcs, lands, token = _push_start(list(sends), lands, True, "push_grads_" + names[0],
                                                               after=after)
        pushes.append((names, send_sems, recv_sems, srcs, lands, True, _PEER_ORDER))
        return token

    my_chip = 2 * lax.axis_index("x") + lax.axis_index("y")
    w_in_rs = {}

    def w_in_grads_start(g_in):
        lands = [lax.empty((N_DEV // 2,) + g_in.shape[1:], g_in.dtype)]
        w_in_rs["pair"] = _pair_start([g_in], lands, "pair_grads_w_in")
        return w_in_rs["pair"][4]

    def w_in_grads_relay(after):
        send_sems, recv_sems, sends, lands, _ = w_in_rs["pair"]
        sends, lands = _pair_wait(send_sems, recv_sems, sends, lands, [after], "wait_pair_grads_w_in")
        core = lax.axis_index("c").astype(jnp.int32).reshape(1)
        sums = [_pair_sum(sends[0], lands[0], core, "pair_sum_w_in")]
        lands2 = [lax.dynamic_update_slice(lax.empty(s.shape, s.dtype),
                                           lax.dynamic_index_in_dim(s, my_chip, 0, keepdims=True), (my_chip, 0, 0))
                  for s in sums]
        send_sems, recv_sems, srcs, lands2, _ = _push_start(sums, lands2, "chip", "push_grads_w_in", _SAME_CORE)
        pushes.append((("w_in",), send_sems, recv_sems, srcs, lands2, "chip", _SAME_CORE))


    small_names = _SMALL + ("conv_w", "loss")
    small = {}

    def small_out(grads, loss_part):
        small_g = dict(grads, loss=loss_part[0, 0:1])
        flats = [small_g[n].reshape(-1) for n in small_names]
        small["sizes"] = [f.shape[0] for f in flats]
        g_small = _pack_rows(flats, _SMALL_ROWS_MULTIPLE)
        small["rs8"] = g_small.shape[0] // N_DEV
        return grads_out(("small",), (g_small.reshape(N_DEV, small["rs8"], LANES),))

    grad_x = _local_step(x[0], loss_target[0], p, mixer_relay, mixer_weights, ffn_weights, grads_out,
                         (w_in_grads_start, w_in_grads_relay), small_out)

    out = {}
    done = [grad_x]
    for names, send_sems, recv_sems, srcs, lands, slotted, peers in pushes:
        parts = _push_wait(send_sems, recv_sems, srcs, lands, slotted, done, "wait_grads_" + names[0], peers)
        if names == ("small",):
            small_mine = _sum_slots(parts[0], "sum_small")
            g_small_all = _allgather([small_mine], [F32], "allgather_small")[0][0].reshape(N_DEV * small["rs8"], LANES)
            pieces = dict(zip(small_names, _unpack_rows(g_small_all, small["sizes"])))
            loss = pieces["loss"][0]
            dconv_w = lax.dynamic_index_in_dim(pieces["conv_w"].reshape(N_DEV, 3, FF_SHARD), me, axis=0, keepdims=False)
            names2 = _SMALL + ("conv_w",)
            gs = [pieces[n].reshape(_as_2d(args[n]).shape) for n in _SMALL] + [dconv_w]
            ds, m2s, v2s = _adam_small(gs, [_as_2d(args[n]) for n in names2], [_as_2d(args["m_" + n]) for n in names2],
                                       [_as_2d(args["v_" + n]) for n in names2], "adam_small")
            for n, res in zip(names2, zip(gs, ds, m2s, v2s)):
                for kind, v in zip(("grad_", "delta_", "new_m_", "new_v_"), res):
                    out[kind + n] = v.reshape(args[n].shape)
            done = [ds[0]]
            continue
        for n, part in zip(names, parts):
            res = _adam_shard(part, args[n], args["m_" + n], args["v_" + n], "adam_" + n)
            for kind, v in zip(("grad_", "delta_", "new_m_", "new_v_"), res):
                out[kind + n] = v
            done = [res[0]]
    order = ("g_mix", "w_in", "a_re", "a_im", "log_dt", "b_re", "b_im", "c_re", "c_im", "d_skip", "w_glu", "b_glu",
             "w_proj_a", "g_sgu", "w_s", "b_s", "w_proj_b", "w_out", "g_ffn", "w_up", "conv_w", "conv_b", "w_down",
             "g_final")
    res = [loss, grad_x.reshape(x.shape)]
    for kind in ("grad_", "delta_", "new_m_", "new_v_"):
        res += [jnp.swapaxes(out[kind + n], -1, -2) if n in _TRANSPOSED else out[kind + n] for n in order]
    return tuple(res)
```

```python
import math

import jax
import jax.numpy as jnp
from jax import lax
from jax.experimental import pallas as pl
from jax.experimental.pallas import tpu as pltpu

F32 = jnp.float32
MXU = jnp.bfloat16
EPS = 1e-6

D_MODEL = 1024
SSM_W = 512
SSM_G, SSM_H, SSM_P = 32, 16, 64
SSM_BLK = 4
SGU_W = 512
SGU_G, SGU_D, CHUNK = 8, 64, 128
D_FF = 2816
N_DEV = 8
FF_SHARD = 2 * D_FF // N_DEV
FF_CW = 2 * FF_SHARD
FF_NCB = D_FF // FF_CW
LANES = 128

ADAM_LR, ADAM_B1, ADAM_B2, ADAM_EPS, ADAM_WD, ADAM_STEP = 0.001, 0.9, 0.999, 1e-08, 0.01, 10

VMEM_LIMIT = 48 * 1024 * 1024
WGRAD_VMEM_LIMIT = 58 * 1024 * 1024
FFN_VMEM_LIMIT = 58 * 1024 * 1024


def _cp(*sem):
    return pltpu.CompilerParams(dimension_semantics=sem, vmem_limit_bytes=VMEM_LIMIT)


def _full(shape):
    n = len(shape)
    return pl.BlockSpec(shape, lambda *_: (0,) * n)


def _sds(shape, dtype=F32):
    return jax.ShapeDtypeStruct(shape, dtype)


def _in_hbm(arrays):
    return [pltpu.with_memory_space_constraint(a, pltpu.HBM) for a in arrays]


def _dot(a, b):
    return jnp.dot(a, b, preferred_element_type=F32)


def _dot_nt(a, b):
    return lax.dot_general(a, b, (((1,), (1,)), ((), ())), preferred_element_type=F32)


def _dot_tn(a, b):
    return lax.dot_general(a, b, (((0,), (0,)), ((), ())), preferred_element_type=F32)


_GELU_C = math.sqrt(2.0 / math.pi)


def _gelu(x):
    return 0.5 * x * (1.0 + jnp.tanh(_GELU_C * (x + 0.044715 * (x * x * x))))


def _gelu_and_grad(x):
    t = jnp.tanh(_GELU_C * (x + 0.044715 * (x * x * x)))
    g = 0.5 * x * (1.0 + t)
    dg = 0.5 * (1.0 + t) + 0.5 * x * (1.0 - t * t) * (_GELU_C * (1.0 + 3.0 * 0.044715 * (x * x)))
    return g, dg


def _sigmoid(x):
    return 0.5 * jnp.tanh(0.5 * x) + 0.5


def _rms(x):
    return lax.rsqrt(jnp.mean(x * x, axis=-1, keepdims=True) + EPS)


def _rms_bwd(dxn, xn, r):
    return r * (dxn - xn * jnp.mean(dxn * xn, axis=-1, keepdims=True))


def _rowsum(x):
    return jnp.sum(x, axis=0, keepdims=True)


def _fetch_once(pairs, sems):
    copies = [pltpu.make_async_copy(src, dst, sems.at[k]) for k, (src, dst) in enumerate(pairs)]
    for cp in copies:
        cp.start()
    for cp in copies:
        cp.wait()


def _s5_disc(are, aim, ldt, br, bi):
    dt = jnp.exp(ldt)
    mag = jnp.exp(dt * are)
    abr = mag * jnp.cos(dt * aim)
    abi = mag * jnp.sin(dt * aim)
    den = are * are + aim * aim
    nr = abr - 1.0
    ni = abi
    fr = (nr * are + ni * aim) / den
    fi = (ni * are - nr * aim) / den
    return abr, abi, fr * br - fi * bi, fr * bi + fi * br


def _s5_params_fwd(are, aim, ldt, br, bi):
    def body(are_ref, aim_ref, ldt_ref, br_ref, bi_ref, o0, o1, o2, o3):
        outs = _s5_disc(are_ref[...], aim_ref[...], ldt_ref[...], br_ref[...], bi_ref[...])
        for o, v in zip((o0, o1, o2, o3), outs):
            o[...] = v
    shp = are.shape
    return pl.pallas_call(body, name="s5_params_fwd", grid=(1,), in_specs=[_full(shp)] * 5, out_specs=[_full(shp)] * 4,
                          out_shape=[_sds(shp)] * 4)(*_in_hbm([are, aim, ldt, br, bi]))


def _s5_params_bwd(are, aim, ldt, br, bi, dabr, dabi, dbr, dbi):
    def body(are_ref, aim_ref, ldt_ref, br_ref, bi_ref, c0, c1, c2, c3, o0, o1, o2, o3, o4):
        prim = (are_ref[...], aim_ref[...], ldt_ref[...], br_ref[...], bi_ref[...])
        _, vjp = jax.vjp(_s5_disc, *prim)
        outs = vjp((c0[...], c1[...], c2[...], c3[...]))
        for o, v in zip((o0, o1, o2, o3, o4), outs):
            o[...] = v
    shp = are.shape
    return pl.pallas_call(body, name="s5_params_bwd", grid=(1,), in_specs=[_full(shp)] * 9, out_specs=[_full(shp)] * 5,
                          out_shape=[_sds(shp)] * 5)(*_in_hbm([are, aim, ldt, br, bi, dabr, dabi, dbr, dbi]))


def _blockdiag(m_t):
    m = m_t.reshape(SSM_BLK, 8, SSM_H, 1, SSM_P)
    eye = jnp.eye(8, dtype=bool).reshape(1, 8, 1, 8, 1)
    return jnp.where(eye, m, jnp.zeros((), m_t.dtype)).reshape(SSM_BLK, 8 * SSM_H, 8 * SSM_P)


def _unblockdiag(pc):
    m = pc.reshape(SSM_BLK, 8, SSM_H, 8, SSM_P)
    return jnp.einsum("jghgp->jghp", m).reshape(SSM_G * SSM_H, SSM_P)


def _in_fwd(x, g_mix, w_in_t, tm):
    S = x.shape[0]

    def body(x_ref, g_ref, w_ref, h_ref, us_ref, uv_ref, gl_ref):
        xv = x_ref[...]
        h = (xv * _rms(xv) * g_ref[...]).astype(MXU)
        h_ref[...] = h
        us_ref[...] = _dot_nt(h, w_ref[0:SSM_W, :])
        uv_ref[...] = _dot_nt(h, w_ref[SSM_W:SSM_W + 2 * SGU_W, :])
        gl_ref[...] = _dot_nt(h, w_ref[SSM_W + 2 * SGU_W:, :])

    row = lambda n: pl.BlockSpec((tm, n), lambda i: (i, 0))
    return pl.pallas_call(
        body, name="in_fwd", grid=(S // tm,),
        in_specs=[row(D_MODEL), _full((1, D_MODEL)), _full(w_in_t.shape)],
        out_specs=[row(D_MODEL), row(SSM_W), row(2 * SGU_W), row(2 * D_MODEL)],
        out_shape=[_sds((S, D_MODEL), MXU), _sds((S, SSM_W)), _sds((S, 2 * SGU_W)), _sds((S, 2 * D_MODEL))],
        compiler_params=_cp("parallel"),
    )(*_in_hbm([x, g_mix, w_in_t]))


def _scan_tables(ar, ai, reverse):
    n = ar.shape[-1]
    def mul(p, q):
        return p[0] * q[0] - p[1] * q[1], p[0] * q[1] + p[1] * q[0]
    a1 = (ar, ai)
    a2 = mul(a1, a1)
    a3 = mul(a2, a1)
    a4 = mul(a2, a2)
    a5 = mul(a4, a1)
    a6 = mul(a4, a2)
    a7 = mul(a4, a3)
    a8 = mul(a4, a4)
    pw = (a1, a2, a3, a4, a5, a6, a7, a8)
    rows = lax.broadcasted_iota(jnp.int32, (8, n), 0)
    tabs = []
    for s, a in ((1, a1), (2, a2), (4, a4)):
        keep = (rows + s <= 7) if reverse else (rows >= s)
        for comp in a:
            tabs.append(jnp.where(keep, jnp.broadcast_to(comp, (8, n)), 0.0))
    for c in range(2):
        q = jnp.zeros((8, n), F32)
        for r in range(8):
            e = (8 - r) if reverse else (r + 1)
            q = jnp.where(rows == r, jnp.broadcast_to(pw[e - 1][c], (8, n)), q)
        tabs.append(q)
    return tabs


def _scan_group(xr, xi, tab_ref, cr, ci, reverse):
    for t, s in enumerate((1, 2, 4)):
        pr = tab_ref[2 * t]
        pi = tab_ref[2 * t + 1]
        sh = (8 - s) if reverse else s
        sr = pltpu.roll(xr, sh, 0)
        si = pltpu.roll(xi, sh, 0)
        xr, xi = xr + pr * sr - pi * si, xi + pr * si + pi * sr
    qr = tab_ref[6]
    qi = tab_ref[7]
    return xr + qr * cr - qi * ci, xi + qr * ci + qi * cr


def _runs_load(src_ref, dst_ref, run):
    for i in range(run):
        dst_ref[8 * i:8 * i + 8, :] = src_ref[pl.ds(i, 8, stride=run), :]


def _runs_store(val, dst_ref, run):
    for i in range(run):
        dst_ref[pl.ds(i, 8, stride=run), :] = val[8 * i:8 * i + 8, :]


def _cpow2(ar, ai, log2n):
    for _ in range(log2n):
        ar, ai = ar * ar - ai * ai, 2.0 * ar * ai
    return ar, ai


def _s5_fwd(us, abar_re, abar_im, b_re, b_im, c_re, c_im, d_skip, tm):
    S = us.shape[0]
    nt = S // tm
    w = 8 * SSM_P
    run = tm // 8
    assert run & (run - 1) == 0

    def body(us_ref, ar_ref, ai_ref, br_ref, bi_ref, cr_ref, ci_ref, d_ref, str_ref, sti_ref, ys_ref,
             tab_ref, car_ref, up_ref):
        i = pl.program_id(1)

        @pl.when(i == 0)
        def _():
            car_ref[...] = jnp.zeros_like(car_ref)
            for k, t in enumerate(_scan_tables(*_cpow2(ar_ref[...], ai_ref[...], run.bit_length() - 1), False)):
                tab_ref[k] = t

        _runs_load(us_ref, up_ref, run)
        ub = up_ref[...].astype(MXU)
        str_ref[...] = _dot(ub, br_ref[0])
        sti_ref[...] = _dot(ub, bi_ref[0])
        ar = jnp.broadcast_to(ar_ref[...], (8, w))
        ai = jnp.broadcast_to(ai_ref[...], (8, w))

        def advance(k, state):
            r0 = pl.multiple_of(k * 8, 8)
            sr, si = state
            return (ar * sr - ai * si + str_ref[pl.ds(r0, 8), :], ar * si + ai * sr + sti_ref[pl.ds(r0, 8), :])

        def emit(k, state):
            r0 = pl.multiple_of(k * 8, 8)
            sr, si = advance(k, state)
            str_ref[pl.ds(r0, 8), :] = sr
            sti_ref[pl.ds(r0, 8), :] = si
            return sr, si

        zero = jnp.zeros((8, w), F32)
        er, ei = lax.fori_loop(0, run, advance, (zero, zero))
        cr, ci = car_ref[0:1, :], car_ref[1:2, :]
        tr, ti = _scan_group(er, ei, tab_ref, cr, ci, False)
        r8 = lax.broadcasted_iota(jnp.int32, (8, w), 0)
        start = (jnp.where(r8 == 0, cr, pltpu.roll(tr, 1, 0)), jnp.where(r8 == 0, ci, pltpu.roll(ti, 1, 0)))
        car_ref[0:1, :] = tr[7:8, :]
        car_ref[1:2, :] = ti[7:8, :]
        lax.fori_loop(0, run, emit, start)
        y = _dot_nt(str_ref[...].astype(MXU), cr_ref[0]) - _dot_nt(sti_ref[...].astype(MXU), ci_ref[0])
        _runs_store(y, ys_ref, run)
        ys_ref[...] += d_ref[...] * us_ref[...]

    blk = lambda: pl.BlockSpec((1, 8 * SSM_H, w), lambda j, i: (j, 0, 0))
    return pl.pallas_call(
        body, name="s5_fwd", grid=(SSM_BLK, nt),
        in_specs=[pl.BlockSpec((tm, LANES), lambda j, i: (i, j)),
                  pl.BlockSpec((1, w), lambda j, i: (0, j)), pl.BlockSpec((1, w), lambda j, i: (0, j)),
                  blk(), blk(), blk(), blk(),
                  pl.BlockSpec((1, LANES), lambda j, i: (0, j))],
        out_specs=[pl.BlockSpec((tm, w), lambda j, i: (i, j)), pl.BlockSpec((tm, w), lambda j, i: (i, j)),
                   pl.BlockSpec((tm, LANES), lambda j, i: (i, j))],
        out_shape=[_sds((S, SSM_BLK * w)), _sds((S, SSM_BLK * w)), _sds((S, SSM_W))],
        scratch_shapes=[pltpu.VMEM((8, 8, w), F32), pltpu.VMEM((8, w), F32), pltpu.VMEM((tm, LANES), F32)],
        compiler_params=_cp("parallel", "arbitrary"),
    )(*_in_hbm([us, abar_re, abar_im, b_re, b_im, c_re, c_im, d_skip]))


def _group_halves(vp):
    first = lax.broadcasted_iota(jnp.int32, vp.shape, 1) < SGU_D
    zero = jnp.zeros((), vp.dtype)
    return jnp.where(first, vp, zero), jnp.where(first, zero, vp)


def _sgu_mix(vnb, wcat_ref):
    outs = []
    for q in range(SGU_G // 2):
        lo, hi = _group_halves(vnb[:, LANES * q:LANES * (q + 1)])
        outs.append(_dot(wcat_ref[q], jnp.concatenate([lo, hi], axis=0)))
    return jnp.concatenate(outs, axis=1)


def _mix_fwd(x, ys, uv, gl, w_glu, b_glu, w_pa, g_sgu, ws, bias_s, w_pb, w_out, g_ffn, tm):
    S = x.shape[0]

    def body(x_ref, ys_ref, uv_ref, gl_ref, wglu_ref, bglu_ref, wpa_ref, gs_ref, ws_ref, bias_ref, wpb_ref, wout_ref,
             gf_ref, yg_ref, yap_ref, sg_ref, ya_ref, yb_ref, m_ref, x1_ref, h2_ref):
        yg = _gelu(ys_ref[...])
        ygb = yg.astype(MXU)
        yg_ref[...] = ygb
        z = _dot(ygb, wglu_ref[...]) + bglu_ref[...]
        yapb = (yg * _sigmoid(z)).astype(MXU)
        yap_ref[...] = yapb
        ya = _dot(yapb, wpa_ref[...])
        ya_ref[...] = ya

        uvg = _gelu(uv_ref[...])
        u2 = uvg[:, :SGU_W]
        v2 = uvg[:, SGU_W:]
        vnb = (v2 * _rms(v2) * gs_ref[...]).astype(MXU)
        for c in range(tm // CHUNK):
            rs = slice(c * CHUNK, (c + 1) * CHUNK)
            mixed = _sgu_mix(vnb[rs], ws_ref) + bias_ref[...]
            sg_ref[rs, :] = (u2[rs] * mixed).astype(MXU)
        yb = _dot(sg_ref[...], wpb_ref[...])
        yb_ref[...] = yb

        glv = gl_ref[...]
        m = _sigmoid(glv[:, :D_MODEL]) * ya + _sigmoid(glv[:, D_MODEL:]) * yb
        mb = m.astype(MXU)
        m_ref[...] = mb
        x1 = x_ref[...] + _dot(mb, wout_ref[...])
        x1_ref[...] = x1
        h2_ref[...] = (x1 * _rms(x1) * gf_ref[...]).astype(MXU)

    row = lambda n: pl.BlockSpec((tm, n), lambda i: (i, 0))
    return pl.pallas_call(
        body, name="mix_fwd", grid=(S // tm,),
        in_specs=[row(D_MODEL), row(SSM_W), row(2 * SGU_W), row(2 * D_MODEL),
                  _full(w_glu.shape), _full(b_glu.shape), _full(w_pa.shape), _full(g_sgu.shape), _full(ws.shape),
                  _full(bias_s.shape), _full(w_pb.shape), _full(w_out.shape), _full(g_ffn.shape)],
        out_specs=[row(SSM_W), row(SSM_W), row(SGU_W), row(D_MODEL), row(D_MODEL), row(D_MODEL), row(D_MODEL),
                   row(D_MODEL)],
        out_shape=[_sds((S, SSM_W), MXU), _sds((S, SSM_W), MXU), _sds((S, SGU_W), MXU), _sds((S, D_MODEL)),
                   _sds((S, D_MODEL)), _sds((S, D_MODEL), MXU), _sds((S, D_MODEL)), _sds((S, D_MODEL), MXU)],
        compiler_params=_cp("parallel"),
    )(*_in_hbm([x, ys, uv, gl, w_glu, b_glu, w_pa, g_sgu, ws, bias_s, w_pb, w_out, g_ffn]))


def _causal_conv3(u, prev8, cw, cb):
    tm = u.shape[0]
    w0, w1, w2 = cw[0:1], cw[1:2], cw[2:3]
    body = w0 * pltpu.roll(u, 2, 0) + w1 * pltpu.roll(u, 1, 0) + w2 * u + cb
    u8 = u[0:8, :]
    r8 = lax.broadcasted_iota(jnp.int32, u8.shape, 0)
    t1 = prev8[7:8, :]
    t0 = prev8[6:7, :]
    s1 = jnp.where(r8 == 0, t1, pltpu.roll(u8, 1, 0))
    s2 = jnp.where(r8 == 0, t0, jnp.where(r8 == 1, t1, pltpu.roll(u8, 2, 0)))
    first = w0 * s2 + w1 * s1 + w2 * u8 + cb
    return jnp.concatenate([first, body[8:tm, :]], axis=0)


def _causal_conv3_adjoint(d, next8, cw):
    tm = d.shape[0]
    w0, w1, w2 = cw[0:1], cw[1:2], cw[2:3]
    n1 = pltpu.roll(d, tm - 1, 0)
    n2 = pltpu.roll(d, tm - 2, 0)
    body = w2 * d + w1 * n1 + w0 * n2
    d8 = d[tm - 8:tm, :]
    r8 = lax.broadcasted_iota(jnp.int32, d8.shape, 0)
    h0 = next8[0:1, :]
    h1 = next8[1:2, :]
    m1 = jnp.where(r8 == 7, h0, pltpu.roll(d8, 7, 0))
    m2 = jnp.where(r8 == 6, h0, jnp.where(r8 == 7, h1, pltpu.roll(d8, 6, 0)))
    last = w2 * d8 + w1 * m1 + w0 * m2
    out = jnp.concatenate([body[0:tm - 8, :], last], axis=0)
    return out, n1, n2, h0 - d[0:1, :], h1 - d[1:2, :]


def _ffn_fwd(h2, x1, tgt, w_up, conv_w, conv_b, w_down, g_final, tm):
    S = h2.shape[0]
    nt = S // tm
    ncb = FF_NCB

    def body(h2_ref, wup_hbm, cwa_ref, cwb_ref, cba_ref, cbb_ref, wd_hbm, x1_ref, gf_ref, tgt_ref,
             up_ref, ab_ref, ff_ref, dx2_ref, dx2b_ref, loss_ref, dgf_ref, acc_ref, tail_ref, wup_ref, wdn_ref, wsem):
        i = pl.program_id(0)
        cb = pl.program_id(1)

        @pl.when(i == 0)
        def _():
            tail_ref[cb] = jnp.zeros((2, 8, FF_CW), F32)

        @pl.when(jnp.logical_and(i == 0, cb == 0))
        def _():
            loss_ref[...] = jnp.zeros_like(loss_ref)
            dgf_ref[...] = jnp.zeros_like(dgf_ref)
            _fetch_once([(wup_hbm, wup_ref), (wd_hbm, wdn_ref)], wsem)

        h2v = h2_ref[...]
        ua = _dot_nt(h2v, wup_ref[cb])
        ub = _dot_nt(h2v, wup_ref[ncb + cb])
        up_ref[0, 0] = ua.astype(MXU)
        up_ref[1, 0] = ub.astype(MXU)
        a = _causal_conv3(ua, tail_ref[cb, 0], cwa_ref[0], cba_ref[0])
        b = _causal_conv3(ub, tail_ref[cb, 1], cwb_ref[0], cbb_ref[0])
        tail_ref[cb, 0] = ua[tm - 8:tm, :]
        tail_ref[cb, 1] = ub[tm - 8:tm, :]
        ab_ref[0, 0] = a
        ab_ref[1, 0] = b
        ffb = (a * _sigmoid(a) * b).astype(MXU)
        ff_ref[0] = ffb
        contrib = _dot(ffb, wdn_ref[pl.ds(pl.multiple_of(cb * FF_CW, FF_CW), FF_CW), :])

        @pl.when(cb == 0)
        def _():
            acc_ref[...] = contrib

        @pl.when(cb > 0)
        def _():
            acc_ref[...] += contrib

        @pl.when(cb == ncb - 1)
        def _():
            x2 = x1_ref[...] + acc_ref[...]
            r = _rms(x2)
            xn = x2 * r
            g = gf_ref[...]
            diff = xn * g - tgt_ref[...]
            loss_ref[...] += (0.5 / D_MODEL) * jnp.sum(diff * diff)
            dy = diff * (1.0 / D_MODEL)
            dgf_ref[...] += _rowsum(dy * xn)
            dx2 = _rms_bwd(dy * g, xn, r)
            dx2_ref[...] = dx2
            dx2b_ref[...] = dx2.astype(MXU)

    row = lambda n: pl.BlockSpec((tm, n), lambda i, c: (i, 0))
    gate = lambda r: pl.BlockSpec((1, r, FF_CW), lambda i, c: (c, 0, 0))
    lin = lambda r: pl.BlockSpec((1, r, FF_CW), lambda i, c: (ncb + c, 0, 0))
    return pl.pallas_call(
        body, name="ffn_fwd", grid=(nt, ncb),
        in_specs=[row(D_MODEL), _ANY, gate(3), lin(3), gate(1), lin(1), _ANY,
                  row(D_MODEL), _full((1, D_MODEL)), row(D_MODEL)],
        out_specs=[pl.BlockSpec((2, 1, tm, FF_CW), lambda i, c: (0, c, i, 0)),
                   pl.BlockSpec((2, 1, tm, FF_CW), lambda i, c: (0, c, i, 0)),
                   pl.BlockSpec((1, tm, FF_CW), lambda i, c: (c, i, 0)),
                   row(D_MODEL), row(D_MODEL), _full((1, LANES)), _full((1, D_MODEL))],
        out_shape=[_sds((2, ncb, S, FF_CW), MXU), _sds((2, ncb, S, FF_CW)), _sds((ncb, S, FF_CW), MXU),
                   _sds((S, D_MODEL)), _sds((S, D_MODEL), MXU), _sds((1, LANES)), _sds((1, D_MODEL))],
        scratch_shapes=[pltpu.VMEM((tm, D_MODEL), F32), pltpu.VMEM((ncb, 2, 8, FF_CW), F32),
                        pltpu.VMEM(w_up.shape, w_up.dtype), pltpu.VMEM(w_down.shape, w_down.dtype),
                        pltpu.SemaphoreType.DMA((2,))],
        compiler_params=pltpu.CompilerParams(dimension_semantics=("arbitrary", "arbitrary"),
                                             vmem_limit_bytes=FFN_VMEM_LIMIT),
    )(*_in_hbm([h2, w_up, conv_w, conv_w, conv_b, conv_b, w_down, x1, g_final, tgt]))


def _ffn_bwd(dx2, up, ab, x1, w_up, conv_w, w_down, g_ffn, tm):
    S = dx2.shape[0]
    nt = S // tm
    ncb = FF_NCB

    def body(dx2_ref, up_ref, ab_ref, cwa_ref, cwb_ref, wd_hbm, wup_hbm,
             x1_ref, g_ref, dup_ref, dx1_ref, dx1b_ref, dconv_ref, dg_ref, acc_ref, head_ref, wup_ref, wdn_ref, wsem):
        i = pl.program_id(0)
        cb = pl.program_id(1)

        @pl.when(i == 0)
        def _():
            head_ref[cb] = jnp.zeros((2, 8, FF_CW), F32)
            dconv_ref[cb] = jnp.zeros((8, FF_CW), F32)
            dconv_ref[ncb + cb] = jnp.zeros((8, FF_CW), F32)

        @pl.when(jnp.logical_and(i == 0, cb == 0))
        def _():
            dg_ref[...] = jnp.zeros_like(dg_ref)
            _fetch_once([(wup_hbm, wup_ref), (wd_hbm, wdn_ref)], wsem)

        dff = _dot_nt(dx2_ref[...].astype(MXU), wdn_ref[pl.ds(pl.multiple_of(cb * FF_CW, FF_CW), FF_CW), :])
        a = ab_ref[0, 0]
        b = ab_ref[1, 0]
        sa = _sigmoid(a)
        silu = a * sa
        da = (dff * b) * (sa + silu * (1.0 - sa))
        db = dff * silu
        dps = []
        for half, slot, d, cw_ref in ((0, cb, da, cwa_ref), (1, ncb + cb, db, cwb_ref)):
            dp, n1, n2, fix0, fix1 = _causal_conv3_adjoint(d, head_ref[cb, half], cw_ref[0])
            head_ref[cb, half] = d[0:8, :]
            dpb16 = dp.astype(MXU)
            dup_ref[half, 0] = dpb16
            dps.append(dpb16)
            u = up_ref[half, 0].astype(F32)
            u_last = u[tm - 1:tm, :]
            dconv_ref[slot, 0:1, :] += _rowsum(n2 * u) + fix0 * u[tm - 2:tm - 1, :] + fix1 * u_last
            dconv_ref[slot, 1:2, :] += _rowsum(n1 * u) + fix0 * u_last
            dconv_ref[slot, 2:3, :] += _rowsum(d * u)
            dconv_ref[slot, 3:4, :] += _rowsum(d)
        contrib = _dot(dps[0], wup_ref[cb]) + _dot(dps[1], wup_ref[ncb + cb])

        @pl.when(cb == 0)
        def _():
            acc_ref[...] = contrib

        @pl.when(cb > 0)
        def _():
            acc_ref[...] += contrib

        @pl.when(cb == ncb - 1)
        def _():
            x1v = x1_ref[...]
            r = _rms(x1v)
            xn = x1v * r
            dh2 = acc_ref[...]
            dg_ref[...] += _rowsum(dh2 * xn)
            dx1 = dx2_ref[...] + _rms_bwd(dh2 * g_ref[...], xn, r)
            dx1_ref[...] = dx1
            dx1b_ref[...] = dx1.astype(MXU)

    row = lambda n: pl.BlockSpec((tm, n), lambda i, c: (nt - 1 - i, 0))
    colb = lambda: pl.BlockSpec((2, 1, tm, FF_CW), lambda i, c: (0, c, nt - 1 - i, 0))
    gate = lambda r: pl.BlockSpec((1, r, FF_CW), lambda i, c: (c, 0, 0))
    lin = lambda r: pl.BlockSpec((1, r, FF_CW), lambda i, c: (ncb + c, 0, 0))
    return pl.pallas_call(
        body, name="ffn_bwd", grid=(nt, ncb),
        in_specs=[row(D_MODEL), colb(), colb(), gate(3), lin(3), _ANY, _ANY, row(D_MODEL), _full((1, D_MODEL))],
        out_specs=[colb(), row(D_MODEL), row(D_MODEL), _full((2 * ncb, 8, FF_CW)), _full((1, D_MODEL))],
        out_shape=[_sds((2, ncb, S, FF_CW), MXU), _sds((S, D_MODEL)), _sds((S, D_MODEL), MXU), _sds((2 * ncb, 8, FF_CW)),
                   _sds((1, D_MODEL))],
        scratch_shapes=[pltpu.VMEM((tm, D_MODEL), F32), pltpu.VMEM((ncb, 2, 8, FF_CW), F32),
                        pltpu.VMEM(w_up.shape, w_up.dtype), pltpu.VMEM(w_down.shape, w_down.dtype),
                        pltpu.SemaphoreType.DMA((2,))],
        compiler_params=pltpu.CompilerParams(dimension_semantics=("arbitrary", "arbitrary"),
                                             vmem_limit_bytes=FFN_VMEM_LIMIT),
    )(*_in_hbm([dx2, up, ab, conv_w, conv_w, w_down, w_up, x1, g_ffn]))


def _mix_bwd(dx1, gl, ya, yb, ys, uv, w_out, w_pa, w_pb, w_glu, b_glu, g_sgu, ws, ws_t, bias_s, tm):
    S = dx1.shape[0]

    def body(dx1_ref, gl_ref, ya_ref, yb_ref, ys_ref, uv_ref, wout_ref, wpa_ref, wpb_ref, wglu_ref, bglu_ref, gs_ref,
             ws_ref, wst_ref, bias_ref,
             dgl_ref, dya_ref, dyb_ref, dz_ref, dys_ref, duv_ref, dbglu_ref, dgs_ref, dws_ref, dbs_ref,
             du2_ref, dvn_ref):
        i = pl.program_id(0)

        @pl.when(i == 0)
        def _():
            dbglu_ref[...] = jnp.zeros_like(dbglu_ref)
            dgs_ref[...] = jnp.zeros_like(dgs_ref)
            dws_ref[...] = jnp.zeros_like(dws_ref)
            dbs_ref[...] = jnp.zeros_like(dbs_ref)

        dm = _dot_nt(dx1_ref[...].astype(MXU), wout_ref[...])
        glv = gl_ref[...]
        ga = _sigmoid(glv[:, :D_MODEL])
        gb = _sigmoid(glv[:, D_MODEL:])
        dgl_ref[:, :D_MODEL] = (dm * ya_ref[...] * ga * (1.0 - ga)).astype(MXU)
        dgl_ref[:, D_MODEL:] = (dm * yb_ref[...] * gb * (1.0 - gb)).astype(MXU)
        dyab = (dm * ga).astype(MXU)
        dybb = (dm * gb).astype(MXU)
        dya_ref[...] = dyab
        dyb_ref[...] = dybb

        dyap = _dot_nt(dyab, wpa_ref[...])
        yg, dgelu = _gelu_and_grad(ys_ref[...])
        sz = _sigmoid(_dot(yg.astype(MXU), wglu_ref[...]) + bglu_ref[...])
        dz = dyap * yg * sz * (1.0 - sz)
        dzb = dz.astype(MXU)
        dz_ref[...] = dzb
        dbglu_ref[...] += _rowsum(dz)
        dys_ref[...] = (dyap * sz + _dot_nt(dzb, wglu_ref[...])) * dgelu

        dsg = _dot_nt(dybb, wpb_ref[...])
        uvg, duvg = _gelu_and_grad(uv_ref[...])
        u2 = uvg[:, :SGU_W]
        v2 = uvg[:, SGU_W:]
        rv = _rms(v2)
        vhat = v2 * rv
        gs = gs_ref[...]
        vnb = (vhat * gs).astype(MXU)
        tril = (lax.broadcasted_iota(jnp.int32, (CHUNK, CHUNK), 0)
                >= lax.broadcasted_iota(jnp.int32, (CHUNK, CHUNK), 1))
        for c in range(tm // CHUNK):
            rs = slice(c * CHUNK, (c + 1) * CHUNK)
            vc = vnb[rs]
            mixed = _sgu_mix(vc, ws_ref) + bias_ref[...]
            dsg_c = dsg[rs]
            du2_ref[rs, :] = dsg_c * mixed
            dmx = dsg_c * u2[rs]
            dbs_ref[...] += dmx
            dmb = dmx.astype(MXU)
            dvn_ref[rs, :] = _sgu_mix(dmb, wst_ref)
            for q in range(SGU_G // 2):
                lanes = slice(LANES * q, LANES * (q + 1))
                for j, part in enumerate(_group_halves(dmb[:, lanes])):
                    dws_ref[2 * q + j] += jnp.where(tril, _dot_nt(part, vc[:, lanes]), 0.0)
        dvn = dvn_ref[...]
        dgs_ref[...] += _rowsum(dvn * vhat)
        dv2 = _rms_bwd(dvn * gs, vhat, rv)
        duv_ref[:, :SGU_W] = (du2_ref[...] * duvg[:, :SGU_W]).astype(MXU)
        duv_ref[:, SGU_W:] = (dv2 * duvg[:, SGU_W:]).astype(MXU)

    row = lambda n: pl.BlockSpec((tm, n), lambda i: (i, 0))
    return pl.pallas_call(
        body, name="mix_bwd", grid=(S // tm,),
        in_specs=[row(D_MODEL), row(2 * D_MODEL), row(D_MODEL), row(D_MODEL), row(SSM_W), row(2 * SGU_W),
                  _full(w_out.shape), _full(w_pa.shape), _full(w_pb.shape), _full(w_glu.shape), _full(b_glu.shape),
                  _full(g_sgu.shape), _full(ws.shape), _full(ws_t.shape), _full(bias_s.shape)],
        out_specs=[row(2 * D_MODEL), row(D_MODEL), row(D_MODEL), row(SSM_W), row(SSM_W), row(2 * SGU_W),
                   _full((1, SSM_W)), _full((1, SGU_W)), _full((SGU_G, CHUNK, CHUNK)), _full((CHUNK, SGU_W))],
        out_shape=[_sds((S, 2 * D_MODEL), MXU), _sds((S, D_MODEL), MXU), _sds((S, D_MODEL), MXU), _sds((S, SSM_W), MXU),
                   _sds((S, SSM_W)), _sds((S, 2 * SGU_W), MXU),
                   _sds((1, SSM_W)), _sds((1, SGU_W)), _sds((SGU_G, CHUNK, CHUNK)), _sds((CHUNK, SGU_W))],
        scratch_shapes=[pltpu.VMEM((tm, SGU_W), F32), pltpu.VMEM((tm, SGU_W), F32)],
        compiler_params=_cp("arbitrary"),
    )(*_in_hbm([dx1, gl, ya, yb, ys, uv, w_out, w_pa, w_pb, w_glu, b_glu, g_sgu, ws, ws_t, bias_s]))


def _s5_bwd(dys, us, st_re, st_im, abar_re, abar_im, b_re, b_im, c_re, c_im, d_skip, tm):
    S = us.shape[0]
    nt = S // tm
    w = 8 * SSM_P
    hb = tm // 8
    run = tm // 8
    assert run & (run - 1) == 0

    def body(dys_ref, us_ref, str_ref, sti_ref, hr_ref, hi_ref, ar_ref, ai_ref, br_ref, bi_ref, cr_ref, ci_ref, d_ref,
             dus_ref, dab_ref, dd_ref, dbr_ref, dbi_ref, dcr_ref, dci_ref,
             tab_ref, car_ref, gr_ref, gi_ref, dyp_ref, up_ref, dun_ref):
        i = pl.program_id(1)
        ri = nt - 1 - i

        @pl.when(i == 0)
        def _():
            car_ref[...] = jnp.zeros_like(car_ref)
            for k, t in enumerate(_scan_tables(*_cpow2(ar_ref[...], -ai_ref[...], run.bit_length() - 1), True)):
                tab_ref[k] = t
            for r in (dab_ref, dd_ref, dbr_ref, dbi_ref, dcr_ref, dci_ref):
                r[...] = jnp.zeros_like(r)

        _runs_load(dys_ref, dyp_ref, run)
        _runs_load(us_ref, up_ref, run)
        dyb = dyp_ref[...].astype(MXU)
        gr_ref[...] = _dot(dyb, cr_ref[0])
        gi_ref[...] = -_dot(dyb, ci_ref[0])
        ar = jnp.broadcast_to(ar_ref[...], (8, w))
        ai = jnp.broadcast_to(-ai_ref[...], (8, w))

        def advance(kk, state):
            r0 = pl.multiple_of((run - 1 - kk) * 8, 8)
            gr, gi = state
            return (ar * gr - ai * gi + gr_ref[pl.ds(r0, 8), :], ar * gi + ai * gr + gi_ref[pl.ds(r0, 8), :])

        def emit(kk, state):
            r0 = pl.multiple_of((run - 1 - kk) * 8, 8)
            gr, gi = advance(kk, state)
            gr_ref[pl.ds(r0, 8), :] = gr
            gi_ref[pl.ds(r0, 8), :] = gi
            return gr, gi

        zero = jnp.zeros((8, w), F32)
        er, ei = lax.fori_loop(0, run, advance, (zero, zero))
        cr, ci = car_ref[0:1, :], car_ref[1:2, :]
        tr, ti = _scan_group(er, ei, tab_ref, cr, ci, True)
        r8 = lax.broadcasted_iota(jnp.int32, (8, w), 0)
        start = (jnp.where(r8 == 7, cr, pltpu.roll(tr, 7, 0)), jnp.where(r8 == 7, ci, pltpu.roll(ti, 7, 0)))
        car_ref[0:1, :] = tr[0:1, :]
        car_ref[1:2, :] = ti[0:1, :]
        lax.fori_loop(0, run, emit, start)

        gsr = gr_ref[...]
        gsi = gi_ref[...]
        sr = str_ref[...]
        si = sti_ref[...]
        first = ri == 0

        def previous(s, halo_ref):
            head = jnp.where(r8 == 0, jnp.where(first, 0.0, halo_ref[7:8, :]), pltpu.roll(s[tm - 8:tm, :], 1, 0))
            return jnp.concatenate([head, s[0:tm - 8, :]], axis=0)

        spr = previous(sr, hr_ref)
        spi = previous(si, hi_ref)
        dab_ref[0, 0:1, :] += _rowsum(gsr * spr + gsi * spi)
        dab_ref[0, 1:2, :] += _rowsum(gsi * spr - gsr * spi)

        gbr = gsr.astype(MXU)
        gbi = gsi.astype(MXU)
        _runs_store(_dot_nt(gbr, br_ref[0]) + _dot_nt(gbi, bi_ref[0]), dun_ref, run)
        dys_v = dys_ref[...]
        dus_ref[...] = (dun_ref[...] + d_ref[...] * dys_v).astype(MXU)
        dd_ref[0, 0:1, :] += _rowsum(dys_v * us_ref[...])
        ub = up_ref[...].astype(MXU)
        dbr_ref[0] += _dot_tn(ub, gbr)
        dbi_ref[0] += _dot_tn(ub, gbi)
        dcr_ref[0] += _dot_tn(dyb, sr.astype(MXU))
        dci_ref[0] -= _dot_tn(dyb, si.astype(MXU))

    blk = lambda: pl.BlockSpec((1, 8 * SSM_H, w), lambda j, i: (j, 0, 0))
    rowl = lambda: pl.BlockSpec((tm, LANES), lambda j, i: (nt - 1 - i, j))
    roww = lambda: pl.BlockSpec((tm, w), lambda j, i: (nt - 1 - i, j))
    halo = lambda: pl.BlockSpec((8, w), lambda j, i: (jnp.maximum((nt - 1 - i) * hb - 1, 0), j))
    return pl.pallas_call(
        body, name="s5_bwd", grid=(SSM_BLK, nt),
        in_specs=[rowl(), rowl(), roww(), roww(), halo(), halo(),
                  pl.BlockSpec((1, w), lambda j, i: (0, j)), pl.BlockSpec((1, w), lambda j, i: (0, j)),
                  blk(), blk(), blk(), blk(),
                  pl.BlockSpec((1, LANES), lambda j, i: (0, j))],
        out_specs=[rowl(),
                   pl.BlockSpec((1, 8, w), lambda j, i: (j, 0, 0)), pl.BlockSpec((1, 8, LANES), lambda j, i: (j, 0, 0)),
                   blk(), blk(), blk(), blk()],
        out_shape=[_sds((S, SSM_W), MXU), _sds((SSM_BLK, 8, w)), _sds((SSM_BLK, 8, LANES)),
                   _sds((SSM_BLK, 8 * SSM_H, w)), _sds((SSM_BLK, 8 * SSM_H, w)),
                   _sds((SSM_BLK, 8 * SSM_H, w)), _sds((SSM_BLK, 8 * SSM_H, w))],
        scratch_shapes=[pltpu.VMEM((8, 8, w), F32), pltpu.VMEM((8, w), F32),
                        pltpu.VMEM((tm, w), F32), pltpu.VMEM((tm, w), F32),
                        pltpu.VMEM((tm, LANES), F32), pltpu.VMEM((tm, LANES), F32), pltpu.VMEM((tm, LANES), F32)],
        compiler_params=_cp("parallel", "arbitrary"),
    )(*_in_hbm([dys, us, st_re, st_im, st_re, st_im, abar_re, abar_im, b_re, b_im, c_re, c_im, d_skip]))


def _in_bwd(dus, duv, dgl, dx1, x, g_mix, w_in, tm):
    S = x.shape[0]

    def body(dus_ref, duv_ref, dgl_ref, dx1_ref, x_ref, g_ref, w_ref, gx_ref, dg_ref):
        @pl.when(pl.program_id(0) == 0)
        def _():
            dg_ref[...] = jnp.zeros_like(dg_ref)

        dh = (_dot(dus_ref[...], w_ref[0:SSM_W, :])
              + _dot(duv_ref[...], w_ref[SSM_W:SSM_W + 2 * SGU_W, :])
              + _dot(dgl_ref[...], w_ref[SSM_W + 2 * SGU_W:, :]))
        xv = x_ref[...]
        r = _rms(xv)
        xn = xv * r
        dg_ref[...] += _rowsum(dh * xn)
        gx_ref[...] = dx1_ref[...] + _rms_bwd(dh * g_ref[...], xn, r)

    row = lambda n: pl.BlockSpec((tm, n), lambda i: (i, 0))
    return pl.pallas_call(
        body, name="in_bwd", grid=(S // tm,),
        in_specs=[row(SSM_W), row(2 * SGU_W), row(2 * D_MODEL), row(D_MODEL), row(D_MODEL), _full((1, D_MODEL)),
                  _full(w_in.shape)],
        out_specs=[row(D_MODEL), _full((1, D_MODEL))],
        out_shape=[_sds((S, D_MODEL)), _sds((1, D_MODEL))],
        compiler_params=_cp("arbitrary"),
    )(*_in_hbm([dus, duv, dgl, dx1, x, g_mix, w_in]))


def _wgrad_split(a, b, nsplit, tk, name):
    S, K = a.shape
    N = b.shape[1]
    c = N // nsplit

    def body(a_ref, b_ref, o_ref):
        prod = _dot_tn(a_ref[...], b_ref[...])
        for d in range(nsplit):
            o_ref[d] = prod[:, c * d:c * (d + 1)].astype(MXU)

    return pl.pallas_call(
        body, name=name, grid=(K // tk,),
        in_specs=[pl.BlockSpec((S, tk), lambda k: (0, k)), _full((S, N))],
        out_specs=pl.BlockSpec((nsplit, tk, c), lambda k: (0, k, 0)),
        out_shape=_sds((nsplit, K, c), MXU),
        compiler_params=_cp("parallel"),
    )(*_in_hbm([a, b]))


def _wgrad_in_t(dps, h1, name):
    S, K = h1.shape
    cw = 512
    counts = [b.shape[1] // cw for b in dps]
    starts = [sum(counts[:i]) for i in range(len(dps))]
    nblk = sum(counts)

    def body(*refs):
        b_refs = refs[:len(dps)]
        h_ref, o_ref = refs[len(dps)], refs[-1]
        j = pl.program_id(0)
        for b_ref, st, cnt in zip(b_refs, starts, counts):
            @pl.when(jnp.logical_and(j >= st, j < st + cnt))
            def _():
                o_ref[...] = _dot_tn(b_ref[...], h_ref[...]).astype(MXU)

    def src_spec(st, cnt):
        return pl.BlockSpec((S, cw), lambda j: (0, jnp.clip(j - st, 0, cnt - 1)))

    return pl.pallas_call(
        body, name=name, grid=(nblk,),
        in_specs=[src_spec(st, cnt) for st, cnt in zip(starts, counts)] + [_full((S, K))],
        out_specs=pl.BlockSpec((cw, K), lambda j: (j, 0)),
        out_shape=_sds((nblk * cw, K), MXU),
        compiler_params=_cp("arbitrary"),
    )(*_in_hbm([*dps, h1]))


def _wgrad_blk(a3, b3, nblk, a_of, b_of, name):
    S, K = a3.shape[1:]
    N = b3.shape[2]

    def body(a_ref, b_ref, o_ref):
        o_ref[0] = _dot_tn(a_ref[0], b_ref[0]).astype(MXU)

    return pl.pallas_call(
        body, name=name, grid=(nblk,),
        in_specs=[pl.BlockSpec((1, S, K), lambda b: (a_of(b), 0, 0)),
                  pl.BlockSpec((1, S, N), lambda b: (b_of(b), 0, 0))],
        out_specs=pl.BlockSpec((1, K, N), lambda b: (b, 0, 0)),
        out_shape=_sds((nblk, K, N), MXU),
        compiler_params=pltpu.CompilerParams(dimension_semantics=("parallel",), vmem_limit_bytes=WGRAD_VMEM_LIMIT),
    )(*_in_hbm([a3, b3]))


def _assemble_cols(blocks_list, name):
    def body(*refs):
        n = len(blocks_list)
        for b_ref, o_ref in zip(refs[:n], refs[n:]):
            c = b_ref.shape[2]
            for d in range(N_DEV):
                o_ref[:, c * d:c * (d + 1)] = b_ref[d]

    outs = [_sds((b.shape[1], N_DEV * b.shape[2]), b.dtype) for b in blocks_list]
    return pl.pallas_call(
        body, name=name, grid=(1,), in_specs=[_full(b.shape) for b in blocks_list],
        out_specs=[_full(o.shape) for o in outs], out_shape=outs, compiler_params=_cp("arbitrary"),
    )(*_in_hbm(blocks_list))


def _tile(S, want):
    return want if S % want == 0 else S


def _local_step(x, tgt, p, mixer_relay, mixer_weights, ffn_weights, grads_out, small_out):
    S = x.shape[0]
    tm = _tile(S, 256)
    tl = _tile(S, 512)

    rep = lambda a: jnp.repeat(a, SSM_H, axis=0)
    are = rep(p["a_re"])
    aim = rep(p["a_im"])
    ldt = jnp.broadcast_to(rep(p["log_dt"].reshape(SSM_G, 1)), are.shape)
    br_t = p["b_re_t"].reshape(are.shape)
    bi_t = p["b_im_t"].reshape(are.shape)
    abr, abi, bbr, bbi = _s5_params_fwd(are, aim, ldt, br_t, bi_t)
    head = lambda a: a.reshape(SSM_G, SSM_H, SSM_P)[:, 0, :].reshape(1, SSM_G * SSM_P)
    abar_re, abar_im = head(abr), head(abi)
    bd_br = _blockdiag(bbr).astype(MXU)
    bd_bi = _blockdiag(bbi).astype(MXU)
    bd_cr = _blockdiag(p["c_re"].reshape(are.shape)).astype(MXU)
    bd_ci = _blockdiag(p["c_im"].reshape(are.shape)).astype(MXU)
    d_skip = p["d_skip"].reshape(1, SSM_W)

    tril = jnp.tril(jnp.ones((CHUNK, CHUNK), dtype=bool))
    ws = jnp.where(tril[None], p["w_s"], 0.0)
    pair = lambda w: w.reshape(SGU_G // 2, 2, CHUNK, CHUNK).transpose(0, 2, 1, 3).reshape(SGU_G // 2, CHUNK, 2 * CHUNK)
    ws_b = pair(ws).astype(MXU)
    ws_t = pair(ws.transpose(0, 2, 1)).astype(MXU)
    bias_s = jnp.repeat(p["b_s"].T, SGU_D, axis=1)

    g_mix = p["g_mix"].reshape(1, D_MODEL)
    g_ffn = p["g_ffn"].reshape(1, D_MODEL)
    g_final = p["g_final"].reshape(1, D_MODEL)
    g_sgu = p["g_sgu"].reshape(1, SGU_W)
    b_glu = p["b_glu"].reshape(1, SSM_W)
    conv_b = p["conv_b"].reshape(2 * FF_NCB, 1, FF_CW)
    tf = _tile(S, 256)
    ts = _tile(S, 1024)

    h1, us, uv, gl = _in_fwd(x, g_mix, p["w_in_t"], tl)
    token = mixer_relay(us)
    st_re, st_im, ys = _s5_fwd(us, abar_re, abar_im, bd_br, bd_bi, bd_cr, bd_ci, d_skip + token[0:1, 0:1], ts)
    p = dict(p, **mixer_weights(ys))
    yg, yap, sg, ya, yb, m, x1, h2 = _mix_fwd(x, ys, uv, gl, p["w_glu"], b_glu + p["token"][0:1, 0:1], p["w_proj_a"],
                                              g_sgu, ws_b, bias_s, p["w_proj_b"], p["w_out"], g_ffn, tl)
    w_up, conv_w, w_down = ffn_weights(h2)
    pair_lanes = lambda a: a.reshape(N_DEV // 2, 2, a.shape[1], FF_SHARD).transpose(0, 2, 1, 3).reshape(
        N_DEV // 2, a.shape[1], FF_CW)
    w_up = w_up.reshape(2 * FF_NCB, FF_CW, D_MODEL)
    conv_w = pair_lanes(conv_w)
    up, ab, ff, dx2, dx2b, loss, dg_final = _ffn_fwd(h2, x1, tgt, w_up, conv_w, conv_b, w_down, g_final, tf)

    dup, dx1, dx1b, dconv, dg_ffn = _ffn_bwd(dx2, up, ab, x1, w_up, conv_w, w_down, g_ffn, tf)
    rows8 = lambda g: g.reshape(N_DEV, g.shape[1] // N_DEV, g.shape[2])
    g_up = _wgrad_blk(dup.reshape(2 * FF_NCB, S, FF_CW), h2[None], 2 * FF_NCB, lambda b: b, lambda b: 0,
                      "wgrad_up").reshape(N_DEV, FF_SHARD, D_MODEL)
    g_down = _wgrad_blk(ff, dx2b[None], FF_NCB, lambda b: b, lambda b: 0, "wgrad_down").reshape(
        N_DEV, D_FF // N_DEV, D_MODEL)
    token = grads_out(("w_up", "w_down"), (g_up, g_down))
    dgl, dya, dyb, dz, dys, duv, db_glu, dg_sgu, dws, dbs = _mix_bwd(
        dx1, gl, ya, yb, ys, uv, p["w_out"], p["w_proj_a"], p["w_proj_b"], p["w_glu"], b_glu + token[0:1, 0:1], g_sgu,
        ws_b, ws_t, bias_s, tm)
    token = grads_out(("w_glu", "w_proj_a", "w_proj_b", "w_out"),
                      (rows8(_wgrad_split(yg, dz, 1, SSM_W, "wgrad_glu")),
                       _wgrad_split(yap, dya, N_DEV, SSM_W, "wgrad_pa"),
                       _wgrad_split(sg, dyb, N_DEV, SGU_W, "wgrad_pb"),
                       rows8(_wgrad_split(m, dx1b, 1, 512, "wgrad_out"))))
    dus, dab, dd, dbbr, dbbi, dcr, dci = _s5_bwd(dys, us, st_re, st_im, abar_re, abar_im, bd_br, bd_bi, bd_cr, bd_ci,
                                                 d_skip + token[0:1, 0:1], ts)
    g_in = _wgrad_in_t([dus, duv, dgl], h1, "wgrad_in")
    token = grads_out(("w_in",), (g_in.reshape(N_DEV, g_in.shape[0] // N_DEV, D_MODEL),))
    grad_x, dg_mix = _in_bwd(dus, duv, dgl, dx1, x, g_mix + token[0:1, 0:1], p["w_in_t"], tl)

    spread = lambda v: jnp.repeat(v.reshape(SSM_G, SSM_P), SSM_H, axis=0) * (1.0 / SSM_H)
    dabr = spread(dab[:, 0, :])
    dabi = spread(dab[:, 1, :])
    dare, daim, dldt, dbr_t, dbi_t = _s5_params_bwd(are, aim, ldt, br_t, bi_t, dabr, dabi,
                                                    _unblockdiag(dbbr), _unblockdiag(dbbi))
    fold = lambda a: a.reshape(SSM_G, SSM_H, SSM_P).sum(axis=1)

    grads = {
        "g_mix": dg_mix,
        "a_re": fold(dare), "a_im": fold(daim), "log_dt": fold(dldt).sum(axis=1),
        "b_re": dbr_t, "b_im": dbi_t,
        "c_re": _unblockdiag(dcr).reshape(SSM_G, SSM_H, SSM_P),
        "c_im": _unblockdiag(dci).reshape(SSM_G, SSM_H, SSM_P),
        "d_skip": dd[:, 0, :].reshape(SSM_W),
        "b_glu": db_glu,
        "g_sgu": dg_sgu,
        "w_s": dws,
        "b_s": dbs.reshape(CHUNK, SGU_G, SGU_D).sum(axis=-1).T,
        "g_ffn": dg_ffn,
        "conv_w": dconv[:, 0:3, :].reshape(N_DEV // 2, 3, 2, FF_SHARD).transpose(0, 2, 1, 3).reshape(
            N_DEV, 3, FF_SHARD),
        "conv_b": dconv[:, 3, :].reshape(2 * D_FF),
        "g_final": dg_final,
    }
    small_out(grads, loss)
    return grad_x


_ANY = pl.BlockSpec(memory_space=pl.ANY)
_MESH = pl.DeviceIdType.MESH


def _allgather(shards, dtypes, name, cast_only=()):
    n = len(shards)
    e = len(cast_only)

    def body(*refs):
        in_refs, extra_in = refs[:n], refs[n:n + e]
        out_refs, extra_out = refs[n + e:2 * n + e], refs[2 * n + e:2 * n + 2 * e]
        stage = refs[2 * n + 2 * e:3 * n + 2 * e]
        send_sems, recv_sems, local_sems = refs[3 * n + 2 * e:]
        for a in range(n):
            stage[a][...] = in_refs[a][...].astype(dtypes[a])
        for i in range(e):
            extra_out[i][...] = extra_in[i][...].astype(MXU)
        x, y, c = lax.axis_index("x"), lax.axis_index("y"), lax.axis_index("c")
        me, sibling = (x, y, c), (x, y, 1 - c)
        chips = [(1 - x, y), (x, 1 - y), (1 - x, 1 - y)]

        def slot(a, px, py, pc):
            return out_refs[a].at[4 * px + 2 * py + pc]

        def copy(a, k, block, to, src=None):
            return pltpu.make_async_remote_copy(
                src_ref=slot(a, *block) if src is None else src, dst_ref=slot(a, *block),
                send_sem=send_sems.at[a, k], recv_sem=recv_sems.at[a, k], device_id=to, device_id_type=_MESH)

        mine = [pltpu.make_async_copy(stage[a], slot(a, *me), local_sems.at[a]) for a in range(n)]
        for cp in mine:
            cp.start()
        first = []
        for j, chip in enumerate(chips):
            first += [copy(a, 1 + j, me, (*chip, c), src=stage[a]) for a in range(n)]
        first += [copy(a, 0, me, sibling, src=stage[a]) for a in range(n)]
        for cp in first:
            cp.start()
        passed = []
        for j, chip in enumerate(chips):
            for a in range(n):
                copy(a, 1 + j, (*chip, c), me).wait_recv()
                fwd = copy(a, 4 + j, (*chip, c), sibling)
                fwd.start()
                passed.append(fwd)
        for a in range(n):
            copy(a, 0, sibling, me).wait_recv()
        for j, chip in enumerate(chips):
            for a in range(n):
                copy(a, 4 + j, (*chip, 1 - c), me).wait_recv()
        for cp in first + passed:
            cp.wait_send()
        for cp in mine:
            cp.wait()

    res = pl.pallas_call(
        body, name=name, grid=(1,), in_specs=[_full(s.shape) for s in list(shards) + list(cast_only)],
        out_specs=[_ANY] * n + [_full(s.shape) for s in cast_only],
        out_shape=[_sds((N_DEV,) + s.shape, dt) for s, dt in zip(shards, dtypes)]
                  + [_sds(s.shape, MXU) for s in cast_only],
        scratch_shapes=[pltpu.VMEM(s.shape, dt) for s, dt in zip(shards, dtypes)]
                       + [pltpu.SemaphoreType.DMA((n, 7)), pltpu.SemaphoreType.DMA((n, 7)), pltpu.SemaphoreType.DMA((n,))],
        compiler_params=pltpu.CompilerParams(vmem_limit_bytes=VMEM_LIMIT),
    )(*_in_hbm([*shards, *cast_only]))
    return res[:n], res[n:]


_HBM = pl.BlockSpec(memory_space=pltpu.HBM)
_SEM = pl.BlockSpec(memory_space=pltpu.SEMAPHORE)
_EFFECT = pltpu.SideEffectType.DATAFLOW_SIDE_EFFECTING
_PEER_ORDER = (2, 4, 6, 3, 5, 7, 1)


def _peer(k):
    x, y, c = lax.axis_index("x"), lax.axis_index("y"), lax.axis_index("c")
    px = 1 - x if k & 4 else x
    py = 1 - y if k & 2 else y
    pc = 1 - c if k & 1 else c
    return (px, py, pc), 4 * px + 2 * py + pc


_SAME_CORE_AND_SIBLING = (2, 4, 6, 1)


def _push_start(srcs, lands, slotted, name, peers=_PEER_ORDER):
    n = len(srcs)

    def body(*refs):
        src_refs, land_refs = refs[:n], refs[n:2 * n]
        send_sems, recv_sems, token = refs[2 * n], refs[2 * n + 1], refs[-1]
        mine = 4 * lax.axis_index("x") + 2 * lax.axis_index("y") + lax.axis_index("c")
        for k in peers:
            dev, theirs = _peer(k)
            for a in range(n):
                pltpu.make_async_remote_copy(
                    src_ref=src_refs[a].at[theirs] if slotted else src_refs[a], dst_ref=land_refs[a].at[mine],
                    send_sem=send_sems.at[7 * a + k - 1], recv_sem=recv_sems.at[7 * a + k - 1],
                    device_id=dev, device_id_type=_MESH).start()
        token[...] = jnp.zeros_like(token)

    bufs = list(srcs) + list(lands)
    res = pl.pallas_call(
        body, name=name, in_specs=[_HBM] * (2 * n),
        out_specs=(_SEM, _SEM, *[_HBM] * (2 * n), pl.BlockSpec(memory_space=pltpu.VMEM)),
        out_shape=(pltpu.SemaphoreType.DMA((7 * n,)), pltpu.SemaphoreType.DMA((7 * n,)),
                   *[pltpu.HBM(b.shape, b.dtype) for b in bufs], _sds((8, LANES))),
        input_output_aliases={i: 2 + i for i in range(2 * n)},
        compiler_params=pltpu.CompilerParams(has_side_effects=_EFFECT),
    )(*[pltpu.with_memory_space_constraint(b, pltpu.HBM) for b in bufs])
    return res[0], res[1], res[2:2 + n], res[2 + n:2 + 2 * n], res[-1]


def _push_wait(send_sems, recv_sems, srcs, lands, slotted, after, name, peers=_PEER_ORDER):
    n = len(srcs)

    def body(*refs):
        src_refs, land_refs = refs[:n], refs[n:2 * n]
        send_sems, recv_sems = refs[2 * n], refs[2 * n + 1]
        for k in peers:
            dev, theirs = _peer(k)
            for a in range(n):
                cp = pltpu.make_async_remote_copy(
                    src_ref=src_refs[a].at[theirs] if slotted else src_refs[a], dst_ref=land_refs[a].at[theirs],
                    send_sem=send_sems.at[7 * a + k - 1], recv_sem=recv_sems.at[7 * a + k - 1],
                    device_id=dev, device_id_type=_MESH)
                cp.wait_send()
                cp.wait_recv()

    bufs = list(srcs) + list(lands)
    res = pl.pallas_call(
        body, name=name, in_specs=[_HBM] * (2 * n) + [_SEM, _SEM] + [_ANY] * len(after), out_specs=[_HBM] * (2 * n),
        out_shape=[pltpu.HBM(b.shape, b.dtype) for b in bufs],
        input_output_aliases={i: i for i in range(2 * n)},
        compiler_params=pltpu.CompilerParams(has_side_effects=_EFFECT),
    )(*bufs, send_sems, recv_sems, *after)
    return res[n:]


def _other_chips():
    x, y = lax.axis_index("x"), lax.axis_index("y")
    return ((1 - x, y), (x, 1 - y), (1 - x, 1 - y))


def _relay_start(lands, name):
    n = len(lands)

    def body(*refs):
        land_refs = refs[:n]
        send_sems, recv_sems, token = refs[n], refs[n + 1], refs[-1]
        x, y, c = lax.axis_index("x"), lax.axis_index("y"), lax.axis_index("c")
        for j, (px, py) in enumerate(_other_chips()):
            slot = 4 * px + 2 * py + c
            for a in range(n):
                pltpu.make_async_remote_copy(
                    src_ref=land_refs[a].at[slot], dst_ref=land_refs[a].at[slot],
                    send_sem=send_sems.at[3 * a + j], recv_sem=recv_sems.at[3 * a + j],
                    device_id=(x, y, 1 - c), device_id_type=_MESH).start()
        token[...] = jnp.zeros_like(token)

    res = pl.pallas_call(
        body, name=name, in_specs=[_HBM] * n,
        out_specs=(_SEM, _SEM, *[_HBM] * n, pl.BlockSpec(memory_space=pltpu.VMEM)),
        out_shape=(pltpu.SemaphoreType.DMA((3 * n,)), pltpu.SemaphoreType.DMA((3 * n,)),
                   *[pltpu.HBM(b.shape, b.dtype) for b in lands], _sds((8, LANES))),
        input_output_aliases={i: 2 + i for i in range(n)},
        compiler_params=pltpu.CompilerParams(has_side_effects=_EFFECT),
    )(*[pltpu.with_memory_space_constraint(b, pltpu.HBM) for b in lands])
    return res[0], res[1], res[2:2 + n], res[-1]


def _relay_wait(send_sems, recv_sems, lands, after, name):
    n = len(lands)

    def body(*refs):
        land_refs = refs[:n]
        send_sems, recv_sems = refs[n], refs[n + 1]
        x, y, c = lax.axis_index("x"), lax.axis_index("y"), lax.axis_index("c")
        for j, (px, py) in enumerate(_other_chips()):
            sent, received = 4 * px + 2 * py + c, 4 * px + 2 * py + (1 - c)
            for a in range(n):
                cp = pltpu.make_async_remote_copy(
                    src_ref=land_refs[a].at[sent], dst_ref=land_refs[a].at[received],
                    send_sem=send_sems.at[3 * a + j], recv_sem=recv_sems.at[3 * a + j],
                    device_id=(x, y, 1 - c), device_id_type=_MESH)
                cp.wait_send()
                cp.wait_recv()

    return pl.pallas_call(
        body, name=name, in_specs=[_HBM] * n + [_SEM, _SEM] + [_ANY] * len(after), out_specs=[_HBM] * n,
        out_shape=[pltpu.HBM(b.shape, b.dtype) for b in lands],
        input_output_aliases={i: i for i in range(n)},
        compiler_params=pltpu.CompilerParams(has_side_effects=_EFFECT),
    )(*lands, send_sems, recv_sems, *after)


def _adamw(w, g, m, v):
    m2 = ADAM_B1 * m + (1.0 - ADAM_B1) * g
    v2 = ADAM_B2 * v + (1.0 - ADAM_B2) * (g * g)
    m_hat = m2 / (1.0 - ADAM_B1 ** ADAM_STEP)
    v_hat = v2 / (1.0 - ADAM_B2 ** ADAM_STEP)
    delta = -ADAM_LR * (m_hat / (jnp.sqrt(v_hat) + ADAM_EPS) + ADAM_WD * w)
    return delta, m2, v2


def _adam_shard(parts, w, m, v, name):
    _, r, c = w.shape
    tr = max(t for t in range(16, 257, 16) if r % t == 0)

    nparts = parts.shape[0]

    def body(p_ref, w_ref, m_ref, v_ref, g_ref, d_ref, m2_ref, v2_ref):
        g = p_ref[0].astype(F32)
        for s in range(1, nparts):
            g = g + p_ref[s].astype(F32)
        g_ref[0] = g
        d_ref[0], m2_ref[0], v2_ref[0] = _adamw(w_ref[0], g, m_ref[0], v_ref[0])

    row = lambda: pl.BlockSpec((1, tr, c), lambda i: (0, i, 0))
    return pl.pallas_call(
        body, name=name, grid=(r // tr,),
        in_specs=[pl.BlockSpec((nparts, tr, c), lambda i: (0, i, 0)), row(), row(), row()],
        out_specs=[row(), row(), row(), row()], out_shape=[_sds((1, r, c))] * 4,
        compiler_params=_cp("parallel"),
    )(*_in_hbm([parts, w, m, v]))


def _adam_small(gs, ws, ms, vs, name):
    n = len(gs)

    def body(*refs):
        ins, outs = refs[:4 * n], refs[4 * n:]
        for i in range(n):
            g = ins[i][...]
            d, m2, v2 = _adamw(ins[n + i][...], g, ins[2 * n + i][...], ins[3 * n + i][...])
            outs[i][...] = d
            outs[n + i][...] = m2
            outs[2 * n + i][...] = v2

    res = pl.pallas_call(
        body, name=name, grid=(1,), in_specs=[_full(w.shape) for w in ws] * 4,
        out_specs=[_full(w.shape) for w in ws] * 3, out_shape=[_sds(w.shape) for w in ws] * 3,
        compiler_params=_cp("arbitrary"),
    )(*_in_hbm([*gs, *ws, *ms, *vs]))
    return res[:n], res[n:2 * n], res[2 * n:]


def _sum_slots(parts, name):
    R = parts.shape[1]

    def body(p_ref, o_ref):
        g = p_ref[0]
        for s in range(1, N_DEV):
            g = g + p_ref[s]
        o_ref[...] = g

    return pl.pallas_call(body, name=name, grid=(1,), in_specs=[_full(parts.shape)], out_specs=_full((R, LANES)),
                          out_shape=_sds((R, LANES)))(*_in_hbm([parts]))


def _pad_to(a, n, axis):
    extra = n - a.shape[axis]
    if extra == 0:
        return a
    widths = [(0, 0)] * a.ndim
    widths[axis] = (0, extra)
    return jnp.pad(a, widths)


def _ceil_to(n, k):
    return -(-n // k) * k


def _pack_rows(flats, rows_multiple):
    parts = [_pad_to(f, _ceil_to(f.shape[-1], LANES), f.ndim - 1) for f in flats]
    cat = jnp.concatenate(parts, axis=-1)
    total = _ceil_to(cat.shape[-1], LANES * rows_multiple)
    cat = _pad_to(cat, total, cat.ndim - 1)
    return cat.reshape(cat.shape[:-1] + (total // LANES, LANES))


def _unpack_rows(buf, sizes):
    flat = buf.reshape(buf.shape[:-2] + (-1,))
    out, off = [], 0
    for n in sizes:
        out.append(flat[..., off:off + n])
        off += _ceil_to(n, LANES)
    return out


_MIX_BIG = ("w_in", "w_glu", "w_proj_a", "w_proj_b", "w_out")
_BIG = _MIX_BIG + ("w_up", "w_down")
_SMALL = ("g_mix", "a_re", "a_im", "log_dt", "b_re", "b_im", "c_re", "c_im", "d_skip", "b_glu", "g_sgu", "w_s", "b_s",
          "g_ffn", "conv_b", "g_final")
_SMALL_ROWS_MULTIPLE = 8 * N_DEV
_TRANSPOSED = ("w_in", "w_up", "b_re", "b_im")


def _as_2d(a):
    return a.reshape(-1, a.shape[-1]) if a.ndim > 1 else a.reshape(1, -1)


def kernel(x, g_mix, w_in, a_re, a_im, log_dt, b_re, b_im, c_re, c_im, d_skip, w_glu, b_glu, w_proj_a, g_sgu, w_s, b_s, w_proj_b, w_out, g_ffn, w_up, conv_w, conv_b, w_down, g_final, loss_target, m_g_mix, m_w_in, m_a_re, m_a_im, m_log_dt, m_b_re, m_b_im, m_c_re, m_c_im, m_d_skip, m_w_glu, m_b_glu, m_w_proj_a, m_g_sgu, m_w_s, m_b_s, m_w_proj_b, m_w_out, m_g_ffn, m_w_up, m_conv_w, m_conv_b, m_w_down, m_g_final, v_g_mix, v_w_in, v_a_re, v_a_im, v_log_dt, v_b_re, v_b_im, v_c_re, v_c_im, v_d_skip, v_w_glu, v_b_glu, v_w_proj_a, v_g_sgu, v_w_s, v_b_s, v_w_proj_b, v_w_out, v_g_ffn, v_w_up, v_conv_w, v_conv_b, v_w_down, v_g_final):
    args = dict(locals())
    me = 4 * lax.axis_index("x") + 2 * lax.axis_index("y") + lax.axis_index("c")

    def own_slot(buf, block):
        return lax.dynamic_update_slice(buf, block[None], (me,) + (0,) * block.ndim)

    for n in _TRANSPOSED:
        for pre in ("", "m_", "v_"):
            args[pre + n] = jnp.swapaxes(args[pre + n], -1, -2)
    later = ("w_glu", "w_proj_a", "w_proj_b", "w_out", "w_up", "w_down")
    (w_in_g,), casts = _allgather([args["w_in"][0]], [MXU], "allgather_w_in", cast_only=[args[n][0] for n in later])
    sh = dict(zip(later, casts))

    def start_push(srcs, tag, peers):
        lands = [own_slot(lax.empty((N_DEV,) + s.shape, s.dtype), s) for s in srcs]
        send_sems, recv_sems, srcs, lands, token = _push_start(srcs, lands, False, "push_" + tag, peers)
        return (send_sems, recv_sems, srcs, lands), token

    mix_push, token_a = start_push([sh[n] for n in later[:4]], "mixer_weights", _SAME_CORE_AND_SIBLING)
    up_push, token_b = start_push([sh["w_up"]], "w_up", _SAME_CORE_AND_SIBLING)
    down_push, token_c = start_push([sh["w_down"], conv_w[0]], "w_down", _PEER_ORDER)
    p = {n: (args[n][0] if n != "g_final" else args[n]) for n in _SMALL if n not in _TRANSPOSED}
    p.update(w_in_t=w_in_g.reshape(SSM_W + 2 * SGU_W + 2 * D_MODEL, D_MODEL),
             b_re_t=args["b_re"][0], b_im_t=args["b_im"][0])
    p["g_mix"] = p["g_mix"] + (token_a[0:1, 0:1] + token_b[0:1, 0:1] + token_c[0:1, 0:1])
    relay = {}

    def mixer_relay(after):
        lands = _push_wait(*mix_push, False, [after], "wait_mixer_weights", _SAME_CORE_AND_SIBLING)
        relay["mix"] = _relay_start(lands, "relay_mixer_weights")
        return relay["mix"][3]

    def mixer_weights(after):
        send_sems, recv_sems, lands, _ = relay["mix"]
        w_glu_g, w_pa_g, w_pb_g, w_out_g = _relay_wait(send_sems, recv_sems, lands, [after], "wait_relay_mixer_weights")
        w_pa_full, w_pb_full = _assemble_cols([w_pa_g, w_pb_g], "assemble_cols")
        lands = _push_wait(*up_push, False, [after], "wait_w_up", _SAME_CORE_AND_SIBLING)
        relay["up"] = _relay_start(lands, "relay_w_up")
        return dict(w_glu=w_glu_g.reshape(SSM_W, SSM_W), w_proj_a=w_pa_full, w_proj_b=w_pb_full,
                    w_out=w_out_g.reshape(D_MODEL, D_MODEL), token=relay["up"][3])

    def ffn_weights(after):
        send_sems, recv_sems, lands, _ = relay["up"]
        w_up_g, = _relay_wait(send_sems, recv_sems, lands, [after], "wait_relay_w_up")
        w_down_g, conv_w_g = _push_wait(*down_push, False, [after], "wait_w_down")
        return w_up_g, conv_w_g, w_down_g.reshape(D_FF, D_MODEL)

    pushes = []

    def grads_out(names, sends):
        lands = [own_slot(lax.empty(s.shape, s.dtype), lax.dynamic_index_in_dim(s, me, 0, keepdims=False))
                 for s in sends]
        send_sems, recv_sems, srcs, lands, token = _push_start(list(sends), lands, True, "push_grads_" + names[0])
        pushes.append((names, send_sems, recv_sems, srcs, lands))
        return token


    small_names = _SMALL + ("conv_w", "loss")
    small = {}

    def small_out(grads, loss_part):
        small_g = dict(grads, loss=loss_part[0, 0:1])
        flats = [small_g[n].reshape(-1) for n in small_names]
        small["sizes"] = [f.shape[0] for f in flats]
        g_small = _pack_rows(flats, _SMALL_ROWS_MULTIPLE)
        small["rs8"] = g_small.shape[0] // N_DEV
        return grads_out(("small",), (g_small.reshape(N_DEV, small["rs8"], LANES),))

    grad_x = _local_step(x[0], loss_target[0], p, mixer_relay, mixer_weights, ffn_weights, grads_out, small_out)

    out = {}
    done = [grad_x]
    for names, send_sems, recv_sems, srcs, lands in pushes:
        parts = _push_wait(send_sems, recv_sems, srcs, lands, True, done, "wait_grads_" + names[0])
        if names == ("small",):
            small_mine = _sum_slots(parts[0], "sum_small")
            g_small_all = _allgather([small_mine], [F32], "allgather_small")[0][0].reshape(N_DEV * small["rs8"], LANES)
            pieces = dict(zip(small_names, _unpack_rows(g_small_all, small["sizes"])))
            loss = pieces["loss"][0]
            dconv_w = lax.dynamic_index_in_dim(pieces["conv_w"].reshape(N_DEV, 3, FF_SHARD), me, axis=0, keepdims=False)
            names2 = _SMALL + ("conv_w",)
            gs = [pieces[n].reshape(_as_2d(args[n]).shape) for n in _SMALL] + [dconv_w]
            ds, m2s, v2s = _adam_small(gs, [_as_2d(args[n]) for n in names2], [_as_2d(args["m_" + n]) for n in names2],
                                       [_as_2d(args["v_" + n]) for n in names2], "adam_small")
            for n, res in zip(names2, zip(gs, ds, m2s, v2s)):
                for kind, v in zip(("grad_", "delta_", "new_m_", "new_v_"), res):
                    out[kind + n] = v.reshape(args[n].shape)
            done = [ds[0]]
            continue
        for n, part in zip(names, parts):
            res = _adam_shard(part, args[n], args["m_" + n], args["v_" + n], "adam_" + n)
            for kind, v in zip(("grad_", "delta_", "new_m_", "new_v_"), res):
                out[kind + n] = v
            done = [res[0]]
    order = ("g_mix", "w_in", "a_re", "a_im", "log_dt", "b_re", "b_im", "c_re", "c_im", "d_skip", "w_glu", "b_glu",
             "w_proj_a", "g_sgu", "w_s", "b_s", "w_proj_b", "w_out", "g_ffn", "w_up", "conv_w", "conv_b", "w_down",
             "g_final")
    res = [loss, grad_x.reshape(x.shape)]
    for kind in ("grad_", "delta_", "new_m_", "new_v_"):
        res += [jnp.swapaxes(out[kind + n], -1, -2) if n in _TRANSPOSED else out[kind + n] for n in order]
    return tuple(res)
```

```python
import math

import jax
import jax.numpy as jnp
from jax import lax
from jax.experimental import pallas as pl
from jax.experimental.pallas import tpu as pltpu

F32 = jnp.float32
MXU = jnp.bfloat16
EPS = 1e-6

D_MODEL = 1024
SSM_W = 512
SSM_G, SSM_H, SSM_P = 32, 16, 64
SSM_BLK = 4
SGU_W = 512
SGU_G, SGU_D, CHUNK = 8, 64, 128
D_FF = 2816
N_DEV = 8
FF_SHARD = 2 * D_FF // N_DEV
FF_CW = 2 * FF_SHARD
FF_NCB = D_FF // FF_CW
LANES = 128

ADAM_LR, ADAM_B1, ADAM_B2, ADAM_EPS, ADAM_WD, ADAM_STEP = 0.001, 0.9, 0.999, 1e-08, 0.01, 10

VMEM_LIMIT = 48 * 1024 * 1024
WGRAD_VMEM_LIMIT = 58 * 1024 * 1024
FFN_VMEM_LIMIT = 58 * 1024 * 1024


def _cp(*sem):
    return pltpu.CompilerParams(dimension_semantics=sem, vmem_limit_bytes=VMEM_LIMIT)


def _full(shape):
    n = len(shape)
    return pl.BlockSpec(shape, lambda *_: (0,) * n)


def _sds(shape, dtype=F32):
    return jax.ShapeDtypeStruct(shape, dtype)


def _in_hbm(arrays):
    return [pltpu.with_memory_space_constraint(a, pltpu.HBM) for a in arrays]


def _dot(a, b):
    return jnp.dot(a, b, preferred_element_type=F32)


def _dot_nt(a, b):
    return lax.dot_general(a, b, (((1,), (1,)), ((), ())), preferred_element_type=F32)


def _dot_tn(a, b):
    return lax.dot_general(a, b, (((0,), (0,)), ((), ())), preferred_element_type=F32)


_GELU_C = math.sqrt(2.0 / math.pi)


def _gelu(x):
    return 0.5 * x * (1.0 + jnp.tanh(_GELU_C * (x + 0.044715 * (x * x * x))))


def _gelu_and_grad(x):
    t = jnp.tanh(_GELU_C * (x + 0.044715 * (x * x * x)))
    g = 0.5 * x * (1.0 + t)
    dg = 0.5 * (1.0 + t) + 0.5 * x * (1.0 - t * t) * (_GELU_C * (1.0 + 3.0 * 0.044715 * (x * x)))
    return g, dg


def _sigmoid(x):
    return 0.5 * jnp.tanh(0.5 * x) + 0.5


def _rms(x):
    return lax.rsqrt(jnp.mean(x * x, axis=-1, keepdims=True) + EPS)


def _rms_bwd(dxn, xn, r):
    return r * (dxn - xn * jnp.mean(dxn * xn, axis=-1, keepdims=True))


def _rowsum(x):
    return jnp.sum(x, axis=0, keepdims=True)


def _fetch_once(pairs, sems):
    copies = [pltpu.make_async_copy(src, dst, sems.at[k]) for k, (src, dst) in enumerate(pairs)]
    for cp in copies:
        cp.start()
    for cp in copies:
        cp.wait()


def _s5_disc(are, aim, ldt, br, bi):
    dt = jnp.exp(ldt)
    mag = jnp.exp(dt * are)
    abr = mag * jnp.cos(dt * aim)
    abi = mag * jnp.sin(dt * aim)
    den = are * are + aim * aim
    nr = abr - 1.0
    ni = abi
    fr = (nr * are + ni * aim) / den
    fi = (ni * are - nr * aim) / den
    return abr, abi, fr * br - fi * bi, fr * bi + fi * br


def _s5_params_fwd(are, aim, ldt, br, bi):
    def body(are_ref, aim_ref, ldt_ref, br_ref, bi_ref, o0, o1, o2, o3):
        outs = _s5_disc(are_ref[...], aim_ref[...], ldt_ref[...], br_ref[...], bi_ref[...])
        for o, v in zip((o0, o1, o2, o3), outs):
            o[...] = v
    shp = are.shape
    return pl.pallas_call(body, name="s5_params_fwd", grid=(1,), in_specs=[_full(shp)] * 5, out_specs=[_full(shp)] * 4,
                          out_shape=[_sds(shp)] * 4)(*_in_hbm([are, aim, ldt, br, bi]))


def _s5_params_bwd(are, aim, ldt, br, bi, dabr, dabi, dbr, dbi):
    def body(are_ref, aim_ref, ldt_ref, br_ref, bi_ref, c0, c1, c2, c3, o0, o1, o2, o3, o4):
        prim = (are_ref[...], aim_ref[...], ldt_ref[...], br_ref[...], bi_ref[...])
        _, vjp = jax.vjp(_s5_disc, *prim)
        outs = vjp((c0[...], c1[...], c2[...], c3[...]))
        for o, v in zip((o0, o1, o2, o3, o4), outs):
            o[...] = v
    shp = are.shape
    return pl.pallas_call(body, name="s5_params_bwd", grid=(1,), in_specs=[_full(shp)] * 9, out_specs=[_full(shp)] * 5,
                          out_shape=[_sds(shp)] * 5)(*_in_hbm([are, aim, ldt, br, bi, dabr, dabi, dbr, dbi]))


def _blockdiag(m_t):
    m = m_t.reshape(SSM_BLK, 8, SSM_H, 1, SSM_P)
    eye = jnp.eye(8, dtype=bool).reshape(1, 8, 1, 8, 1)
    return jnp.where(eye, m, jnp.zeros((), m_t.dtype)).reshape(SSM_BLK, 8 * SSM_H, 8 * SSM_P)


def _unblockdiag(pc):
    m = pc.reshape(SSM_BLK, 8, SSM_H, 8, SSM_P)
    return jnp.einsum("jghgp->jghp", m).reshape(SSM_G * SSM_H, SSM_P)


def _in_fwd(x, g_mix, w_in_t, tm):
    S = x.shape[0]

    def body(x_ref, g_ref, w_ref, h_ref, us_ref, uv_ref, gl_ref):
        xv = x_ref[...]
        h = (xv * _rms(xv) * g_ref[...]).astype(MXU)
        h_ref[...] = h
        us_ref[...] = _dot_nt(h, w_ref[0:SSM_W, :])
        uv_ref[...] = _dot_nt(h, w_ref[SSM_W:SSM_W + 2 * SGU_W, :])
        gl_ref[...] = _dot_nt(h, w_ref[SSM_W + 2 * SGU_W:, :])

    row = lambda n: pl.BlockSpec((tm, n), lambda i: (i, 0))
    return pl.pallas_call(
        body, name="in_fwd", grid=(S // tm,),
        in_specs=[row(D_MODEL), _full((1, D_MODEL)), _full(w_in_t.shape)],
        out_specs=[row(D_MODEL), row(SSM_W), row(2 * SGU_W), row(2 * D_MODEL)],
        out_shape=[_sds((S, D_MODEL), MXU), _sds((S, SSM_W)), _sds((S, 2 * SGU_W)), _sds((S, 2 * D_MODEL))],
        compiler_params=_cp("parallel"),
    )(*_in_hbm([x, g_mix, w_in_t]))


def _scan_tables(ar, ai, reverse):
    n = ar.shape[-1]
    def mul(p, q):
        return p[0] * q[0] - p[1] * q[1], p[0] * q[1] + p[1] * q[0]
    a1 = (ar, ai)
    a2 = mul(a1, a1)
    a3 = mul(a2, a1)
    a4 = mul(a2, a2)
    a5 = mul(a4, a1)
    a6 = mul(a4, a2)
    a7 = mul(a4, a3)
    a8 = mul(a4, a4)
    pw = (a1, a2, a3, a4, a5, a6, a7, a8)
    rows = lax.broadcasted_iota(jnp.int32, (8, n), 0)
    tabs = []
    for s, a in ((1, a1), (2, a2), (4, a4)):
        keep = (rows + s <= 7) if reverse else (rows >= s)
        for comp in a:
            tabs.append(jnp.where(keep, jnp.broadcast_to(comp, (8, n)), 0.0))
    for c in range(2):
        q = jnp.zeros((8, n), F32)
        for r in range(8):
            e = (8 - r) if reverse else (r + 1)
            q = jnp.where(rows == r, jnp.broadcast_to(pw[e - 1][c], (8, n)), q)
        tabs.append(q)
    return tabs


def _scan_group(xr, xi, tab_ref, cr, ci, reverse):
    for t, s in enumerate((1, 2, 4)):
        pr = tab_ref[2 * t]
        pi = tab_ref[2 * t + 1]
        sh = (8 - s) if reverse else s
        sr = pltpu.roll(xr, sh, 0)
        si = pltpu.roll(xi, sh, 0)
        xr, xi = xr + pr * sr - pi * si, xi + pr * si + pi * sr
    qr = tab_ref[6]
    qi = tab_ref[7]
    return xr + qr * cr - qi * ci, xi + qr * ci + qi * cr


def _runs_load(src_ref, dst_ref, run):
    for i in range(run):
        dst_ref[8 * i:8 * i + 8, :] = src_ref[pl.ds(i, 8, stride=run), :]


def _runs_store(val, dst_ref, run):
    for i in range(run):
        dst_ref[pl.ds(i, 8, stride=run), :] = val[8 * i:8 * i + 8, :]


def _cpow2(ar, ai, log2n):
    for _ in range(log2n):
        ar, ai = ar * ar - ai * ai, 2.0 * ar * ai
    return ar, ai


def _s5_fwd(us, abar_re, abar_im, b_re, b_im, c_re, c_im, d_skip, tm):
    S = us.shape[0]
    nt = S // tm
    w = 8 * SSM_P
    run = tm // 8
    assert run & (run - 1) == 0

    def body(us_ref, ar_ref, ai_ref, br_ref, bi_ref, cr_ref, ci_ref, d_ref, str_ref, sti_ref, ys_ref,
             tab_ref, car_ref, up_ref):
        i = pl.program_id(1)

        @pl.when(i == 0)
        def _():
            car_ref[...] = jnp.zeros_like(car_ref)
            for k, t in enumerate(_scan_tables(*_cpow2(ar_ref[...], ai_ref[...], run.bit_length() - 1), False)):
                tab_ref[k] = t

        _runs_load(us_ref, up_ref, run)
        ub = up_ref[...].astype(MXU)
        str_ref[...] = _dot(ub, br_ref[0])
        sti_ref[...] = _dot(ub, bi_ref[0])
        ar = jnp.broadcast_to(ar_ref[...], (8, w))
        ai = jnp.broadcast_to(ai_ref[...], (8, w))

        def advance(k, state):
            r0 = pl.multiple_of(k * 8, 8)
            sr, si = state
            return (ar * sr - ai * si + str_ref[pl.ds(r0, 8), :], ar * si + ai * sr + sti_ref[pl.ds(r0, 8), :])

        def emit(k, state):
            r0 = pl.multiple_of(k * 8, 8)
            sr, si = advance(k, state)
            str_ref[pl.ds(r0, 8), :] = sr
            sti_ref[pl.ds(r0, 8), :] = si
            return sr, si

        zero = jnp.zeros((8, w), F32)
        er, ei = lax.fori_loop(0, run, advance, (zero, zero))
        cr, ci = car_ref[0:1, :], car_ref[1:2, :]
        tr, ti = _scan_group(er, ei, tab_ref, cr, ci, False)
        r8 = lax.broadcasted_iota(jnp.int32, (8, w), 0)
        start = (jnp.where(r8 == 0, cr, pltpu.roll(tr, 1, 0)), jnp.where(r8 == 0, ci, pltpu.roll(ti, 1, 0)))
        car_ref[0:1, :] = tr[7:8, :]
        car_ref[1:2, :] = ti[7:8, :]
        lax.fori_loop(0, run, emit, start)
        y = _dot_nt(str_ref[...].astype(MXU), cr_ref[0]) - _dot_nt(sti_ref[...].astype(MXU), ci_ref[0])
        _runs_store(y, ys_ref, run)
        ys_ref[...] += d_ref[...] * us_ref[...]

    blk = lambda: pl.BlockSpec((1, 8 * SSM_H, w), lambda j, i: (j, 0, 0))
    return pl.pallas_call(
        body, name="s5_fwd", grid=(SSM_BLK, nt),
        in_specs=[pl.BlockSpec((tm, LANES), lambda j, i: (i, j)),
                  pl.BlockSpec((1, w), lambda j, i: (0, j)), pl.BlockSpec((1, w), lambda j, i: (0, j)),
                  blk(), blk(), blk(), blk(),
                  pl.BlockSpec((1, LANES), lambda j, i: (0, j))],
        out_specs=[pl.BlockSpec((tm, w), lambda j, i: (i, j)), pl.BlockSpec((tm, w), lambda j, i: (i, j)),
                   pl.BlockSpec((tm, LANES), lambda j, i: (i, j))],
        out_shape=[_sds((S, SSM_BLK * w)), _sds((S, SSM_BLK * w)), _sds((S, SSM_W))],
        scratch_shapes=[pltpu.VMEM((8, 8, w), F32), pltpu.VMEM((8, w), F32), pltpu.VMEM((tm, LANES), F32)],
        compiler_params=_cp("parallel", "arbitrary"),
    )(*_in_hbm([us, abar_re, abar_im, b_re, b_im, c_re, c_im, d_skip]))


def _group_halves(vp):
    first = lax.broadcasted_iota(jnp.int32, vp.shape, 1) < SGU_D
    zero = jnp.zeros((), vp.dtype)
    return jnp.where(first, vp, zero), jnp.where(first, zero, vp)


def _sgu_mix(vnb, wcat_ref):
    outs = []
    for q in range(SGU_G // 2):
        lo, hi = _group_halves(vnb[:, LANES * q:LANES * (q + 1)])
        outs.append(_dot(wcat_ref[q], jnp.concatenate([lo, hi], axis=0)))
    return jnp.concatenate(outs, axis=1)


def _mix_fwd(x, ys, uv, gl, w_glu, b_glu, w_pa, g_sgu, ws, bias_s, w_pb, w_out, g_ffn, tm):
    S = x.shape[0]

    def body(x_ref, ys_ref, uv_ref, gl_ref, wglu_ref, bglu_ref, wpa_ref, gs_ref, ws_ref, bias_ref, wpb_ref, wout_ref,
             gf_ref, yg_ref, yap_ref, sg_ref, ya_ref, yb_ref, m_ref, x1_ref, h2_ref):
        yg = _gelu(ys_ref[...])
        ygb = yg.astype(MXU)
        yg_ref[...] = ygb
        z = _dot(ygb, wglu_ref[...]) + bglu_ref[...]
        yapb = (yg * _sigmoid(z)).astype(MXU)
        yap_ref[...] = yapb
        ya = _dot(yapb, wpa_ref[...])
        ya_ref[...] = ya

        uvg = _gelu(uv_ref[...])
        u2 = uvg[:, :SGU_W]
        v2 = uvg[:, SGU_W:]
        vnb = (v2 * _rms(v2) * gs_ref[...]).astype(MXU)
        for c in range(tm // CHUNK):
            rs = slice(c * CHUNK, (c + 1) * CHUNK)
            mixed = _sgu_mix(vnb[rs], ws_ref) + bias_ref[...]
            sg_ref[rs, :] = (u2[rs] * mixed).astype(MXU)
        yb = _dot(sg_ref[...], wpb_ref[...])
        yb_ref[...] = yb

        glv = gl_ref[...]
        m = _sigmoid(glv[:, :D_MODEL]) * ya + _sigmoid(glv[:, D_MODEL:]) * yb
        mb = m.astype(MXU)
        m_ref[...] = mb
        x1 = x_ref[...] + _dot(mb, wout_ref[...])
        x1_ref[...] = x1
        h2_ref[...] = (x1 * _rms(x1) * gf_ref[...]).astype(MXU)

    row = lambda n: pl.BlockSpec((tm, n), lambda i: (i, 0))
    return pl.pallas_call(
        body, name="mix_fwd", grid=(S // tm,),
        in_specs=[row(D_MODEL), row(SSM_W), row(2 * SGU_W), row(2 * D_MODEL),
                  _full(w_glu.shape), _full(b_glu.shape), _full(w_pa.shape), _full(g_sgu.shape), _full(ws.shape),
                  _full(bias_s.shape), _full(w_pb.shape), _full(w_out.shape), _full(g_ffn.shape)],
        out_specs=[row(SSM_W), row(SSM_W), row(SGU_W), row(D_MODEL), row(D_MODEL), row(D_MODEL), row(D_MODEL),
                   row(D_MODEL)],
        out_shape=[_sds((S, SSM_W), MXU), _sds((S, SSM_W), MXU), _sds((S, SGU_W), MXU), _sds((S, D_MODEL)),
                   _sds((S, D_MODEL)), _sds((S, D_MODEL), MXU), _sds((S, D_MODEL)), _sds((S, D_MODEL), MXU)],
        compiler_params=_cp("parallel"),
    )(*_in_hbm([x, ys, uv, gl, w_glu, b_glu, w_pa, g_sgu, ws, bias_s, w_pb, w_out, g_ffn]))


def _causal_conv3(u, prev8, cw, cb):
    tm = u.shape[0]
    w0, w1, w2 = cw[0:1], cw[1:2], cw[2:3]
    body = w0 * pltpu.roll(u, 2, 0) + w1 * pltpu.roll(u, 1, 0) + w2 * u + cb
    u8 = u[0:8, :]
    r8 = lax.broadcasted_iota(jnp.int32, u8.shape, 0)
    t1 = prev8[7:8, :]
    t0 = prev8[6:7, :]
    s1 = jnp.where(r8 == 0, t1, pltpu.roll(u8, 1, 0))
    s2 = jnp.where(r8 == 0, t0, jnp.where(r8 == 1, t1, pltpu.roll(u8, 2, 0)))
    first = w0 * s2 + w1 * s1 + w2 * u8 + cb
    return jnp.concatenate([first, body[8:tm, :]], axis=0)


def _causal_conv3_adjoint(d, next8, cw):
    tm = d.shape[0]
    w0, w1, w2 = cw[0:1], cw[1:2], cw[2:3]
    n1 = pltpu.roll(d, tm - 1, 0)
    n2 = pltpu.roll(d, tm - 2, 0)
    body = w2 * d + w1 * n1 + w0 * n2
    d8 = d[tm - 8:tm, :]
    r8 = lax.broadcasted_iota(jnp.int32, d8.shape, 0)
    h0 = next8[0:1, :]
    h1 = next8[1:2, :]
    m1 = jnp.where(r8 == 7, h0, pltpu.roll(d8, 7, 0))
    m2 = jnp.where(r8 == 6, h0, jnp.where(r8 == 7, h1, pltpu.roll(d8, 6, 0)))
    last = w2 * d8 + w1 * m1 + w0 * m2
    out = jnp.concatenate([body[0:tm - 8, :], last], axis=0)
    return out, n1, n2, h0 - d[0:1, :], h1 - d[1:2, :]


def _ffn_fwd(h2, x1, tgt, w_up, conv_w, conv_b, w_down, g_final, tm):
    S = h2.shape[0]
    nt = S // tm
    ncb = FF_NCB

    def body(h2_ref, wup_hbm, cwa_ref, cwb_ref, cba_ref, cbb_ref, wd_hbm, x1_ref, gf_ref, tgt_ref,
             up_ref, ab_ref, ff_ref, dx2_ref, dx2b_ref, loss_ref, dgf_ref, acc_ref, tail_ref, wup_ref, wdn_ref, wsem):
        i = pl.program_id(0)
        cb = pl.program_id(1)

        @pl.when(i == 0)
        def _():
            tail_ref[cb] = jnp.zeros((2, 8, FF_CW), F32)

        @pl.when(jnp.logical_and(i == 0, cb == 0))
        def _():
            loss_ref[...] = jnp.zeros_like(loss_ref)
            dgf_ref[...] = jnp.zeros_like(dgf_ref)
            _fetch_once([(wup_hbm, wup_ref), (wd_hbm, wdn_ref)], wsem)

        h2v = h2_ref[...]
        ua = _dot_nt(h2v, wup_ref[cb])
        ub = _dot_nt(h2v, wup_ref[ncb + cb])
        up_ref[0, 0] = ua.astype(MXU)
        up_ref[1, 0] = ub.astype(MXU)
        a = _causal_conv3(ua, tail_ref[cb, 0], cwa_ref[0], cba_ref[0])
        b = _causal_conv3(ub, tail_ref[cb, 1], cwb_ref[0], cbb_ref[0])
        tail_ref[cb, 0] = ua[tm - 8:tm, :]
        tail_ref[cb, 1] = ub[tm - 8:tm, :]
        ab_ref[0, 0] = a
        ab_ref[1, 0] = b
        ffb = (a * _sigmoid(a) * b).astype(MXU)
        ff_ref[0] = ffb
        contrib = _dot(ffb, wdn_ref[pl.ds(pl.multiple_of(cb * FF_CW, FF_CW), FF_CW), :])

        @pl.when(cb == 0)
        def _():
            acc_ref[...] = contrib

        @pl.when(cb > 0)
        def _():
            acc_ref[...] += contrib

        @pl.when(cb == ncb - 1)
        def _():
            x2 = x1_ref[...] + acc_ref[...]
            r = _rms(x2)
            xn = x2 * r
            g = gf_ref[...]
            diff = xn * g - tgt_ref[...]
            loss_ref[...] += (0.5 / D_MODEL) * jnp.sum(diff * diff)
            dy = diff * (1.0 / D_MODEL)
            dgf_ref[...] += _rowsum(dy * xn)
            dx2 = _rms_bwd(dy * g, xn, r)
            dx2_ref[...] = dx2
            dx2b_ref[...] = dx2.astype(MXU)

    row = lambda n: pl.BlockSpec((tm, n), lambda i, c: (i, 0))
    gate = lambda r: pl.BlockSpec((1, r, FF_CW), lambda i, c: (c, 0, 0))
    lin = lambda r: pl.BlockSpec((1, r, FF_CW), lambda i, c: (ncb + c, 0, 0))
    return pl.pallas_call(
        body, name="ffn_fwd", grid=(nt, ncb),
        in_specs=[row(D_MODEL), _ANY, gate(3), lin(3), gate(1), lin(1), _ANY,
                  row(D_MODEL), _full((1, D_MODEL)), row(D_MODEL)],
        out_specs=[pl.BlockSpec((2, 1, tm, FF_CW), lambda i, c: (0, c, i, 0)),
                   pl.BlockSpec((2, 1, tm, FF_CW), lambda i, c: (0, c, i, 0)),
                   pl.BlockSpec((1, tm, FF_CW), lambda i, c: (c, i, 0)),
                   row(D_MODEL), row(D_MODEL), _full((1, LANES)), _full((1, D_MODEL))],
        out_shape=[_sds((2, ncb, S, FF_CW), MXU), _sds((2, ncb, S, FF_CW)), _sds((ncb, S, FF_CW), MXU),
                   _sds((S, D_MODEL)), _sds((S, D_MODEL), MXU), _sds((1, LANES)), _sds((1, D_MODEL))],
        scratch_shapes=[pltpu.VMEM((tm, D_MODEL), F32), pltpu.VMEM((ncb, 2, 8, FF_CW), F32),
                        pltpu.VMEM(w_up.shape, w_up.dtype), pltpu.VMEM(w_down.shape, w_down.dtype),
                        pltpu.SemaphoreType.DMA((2,))],
        compiler_params=pltpu.CompilerParams(dimension_semantics=("arbitrary", "arbitrary"),
                                             vmem_limit_bytes=FFN_VMEM_LIMIT),
    )(*_in_hbm([h2, w_up, conv_w, conv_w, conv_b, conv_b, w_down, x1, g_final, tgt]))


def _ffn_bwd(dx2, up, ab, x1, w_up, conv_w, w_down, g_ffn, tm):
    S = dx2.shape[0]
    nt = S // tm
    ncb = FF_NCB

    def body(dx2_ref, up_ref, ab_ref, cwa_ref, cwb_ref, wd_hbm, wup_hbm,
             x1_ref, g_ref, dup_ref, dx1_ref, dx1b_ref, dconv_ref, dg_ref, acc_ref, head_ref, wup_ref, wdn_ref, wsem):
        i = pl.program_id(0)
        cb = pl.program_id(1)

        @pl.when(i == 0)
        def _():
            head_ref[cb] = jnp.zeros((2, 8, FF_CW), F32)
            dconv_ref[cb] = jnp.zeros((8, FF_CW), F32)
            dconv_ref[ncb + cb] = jnp.zeros((8, FF_CW), F32)

        @pl.when(jnp.logical_and(i == 0, cb == 0))
        def _():
            dg_ref[...] = jnp.zeros_like(dg_ref)
            _fetch_once([(wup_hbm, wup_ref), (wd_hbm, wdn_ref)], wsem)

        dff = _dot_nt(dx2_ref[...].astype(MXU), wdn_ref[pl.ds(pl.multiple_of(cb * FF_CW, FF_CW), FF_CW), :])
        a = ab_ref[0, 0]
        b = ab_ref[1, 0]
        sa = _sigmoid(a)
        silu = a * sa
        da = (dff * b) * (sa + silu * (1.0 - sa))
        db = dff * silu
        dps = []
        for half, slot, d, cw_ref in ((0, cb, da, cwa_ref), (1, ncb + cb, db, cwb_ref)):
            dp, n1, n2, fix0, fix1 = _causal_conv3_adjoint(d, head_ref[cb, half], cw_ref[0])
            head_ref[cb, half] = d[0:8, :]
            dpb16 = dp.astype(MXU)
            dup_ref[half, 0] = dpb16
            dps.append(dpb16)
            u = up_ref[half, 0].astype(F32)
            u_last = u[tm - 1:tm, :]
            dconv_ref[slot, 0:1, :] += _rowsum(n2 * u) + fix0 * u[tm - 2:tm - 1, :] + fix1 * u_last
            dconv_ref[slot, 1:2, :] += _rowsum(n1 * u) + fix0 * u_last
            dconv_ref[slot, 2:3, :] += _rowsum(d * u)
            dconv_ref[slot, 3:4, :] += _rowsum(d)
        contrib = _dot(dps[0], wup_ref[cb]) + _dot(dps[1], wup_ref[ncb + cb])

        @pl.when(cb == 0)
        def _():
            acc_ref[...] = contrib

        @pl.when(cb > 0)
        def _():
            acc_ref[...] += contrib

        @pl.when(cb == ncb - 1)
        def _():
            x1v = x1_ref[...]
            r = _rms(x1v)
            xn = x1v * r
            dh2 = acc_ref[...]
            dg_ref[...] += _rowsum(dh2 * xn)
            dx1 = dx2_ref[...] + _rms_bwd(dh2 * g_ref[...], xn, r)
            dx1_ref[...] = dx1
            dx1b_ref[...] = dx1.astype(MXU)

    row = lambda n: pl.BlockSpec((tm, n), lambda i, c: (nt - 1 - i, 0))
    colb = lambda: pl.BlockSpec((2, 1, tm, FF_CW), lambda i, c: (0, c, nt - 1 - i, 0))
    gate = lambda r: pl.BlockSpec((1, r, FF_CW), lambda i, c: (c, 0, 0))
    lin = lambda r: pl.BlockSpec((1, r, FF_CW), lambda i, c: (ncb + c, 0, 0))
    return pl.pallas_call(
        body, name="ffn_bwd", grid=(nt, ncb),
        in_specs=[row(D_MODEL), colb(), colb(), gate(3), lin(3), _ANY, _ANY, row(D_MODEL), _full((1, D_MODEL))],
        out_specs=[colb(), row(D_MODEL), row(D_MODEL), _full((2 * ncb, 8, FF_CW)), _full((1, D_MODEL))],
        out_shape=[_sds((2, ncb, S, FF_CW), MXU), _sds((S, D_MODEL)), _sds((S, D_MODEL), MXU), _sds((2 * ncb, 8, FF_CW)),
                   _sds((1, D_MODEL))],
        scratch_shapes=[pltpu.VMEM((tm, D_MODEL), F32), pltpu.VMEM((ncb, 2, 8, FF_CW), F32),
                        pltpu.VMEM(w_up.shape, w_up.dtype), pltpu.VMEM(w_down.shape, w_down.dtype),
                        pltpu.SemaphoreType.DMA((2,))],
        compiler_params=pltpu.CompilerParams(dimension_semantics=("arbitrary", "arbitrary"),
                                             vmem_limit_bytes=FFN_VMEM_LIMIT),
    )(*_in_hbm([dx2, up, ab, conv_w, conv_w, w_down, w_up, x1, g_ffn]))


def _mix_bwd(dx1, gl, ya, yb, ys, uv, w_out, w_pa, w_pb, w_glu, b_glu, g_sgu, ws, ws_t, bias_s, tm):
    S = dx1.shape[0]

    def body(dx1_ref, gl_ref, ya_ref, yb_ref, ys_ref, uv_ref, wout_ref, wpa_ref, wpb_ref, wglu_ref, bglu_ref, gs_ref,
             ws_ref, wst_ref, bias_ref,
             dgl_ref, dya_ref, dyb_ref, dz_ref, dys_ref, duv_ref, dbglu_ref, dgs_ref, dws_ref, dbs_ref,
             du2_ref, dvn_ref):
        i = pl.program_id(0)

        @pl.when(i == 0)
        def _():
            dbglu_ref[...] = jnp.zeros_like(dbglu_ref)
            dgs_ref[...] = jnp.zeros_like(dgs_ref)
            dws_ref[...] = jnp.zeros_like(dws_ref)
            dbs_ref[...] = jnp.zeros_like(dbs_ref)

        dm = _dot_nt(dx1_ref[...].astype(MXU), wout_ref[...])
        glv = gl_ref[...]
        ga = _sigmoid(glv[:, :D_MODEL])
        gb = _sigmoid(glv[:, D_MODEL:])
        dgl_ref[:, :D_MODEL] = (dm * ya_ref[...] * ga * (1.0 - ga)).astype(MXU)
        dgl_ref[:, D_MODEL:] = (dm * yb_ref[...] * gb * (1.0 - gb)).astype(MXU)
        dyab = (dm * ga).astype(MXU)
        dybb = (dm * gb).astype(MXU)
        dya_ref[...] = dyab
        dyb_ref[...] = dybb

        dyap = _dot_nt(dyab, wpa_ref[...])
        yg, dgelu = _gelu_and_grad(ys_ref[...])
        sz = _sigmoid(_dot(yg.astype(MXU), wglu_ref[...]) + bglu_ref[...])
        dz = dyap * yg * sz * (1.0 - sz)
        dzb = dz.astype(MXU)
        dz_ref[...] = dzb
        dbglu_ref[...] += _rowsum(dz)
        dys_ref[...] = (dyap * sz + _dot_nt(dzb, wglu_ref[...])) * dgelu

        dsg = _dot_nt(dybb, wpb_ref[...])
        uvg, duvg = _gelu_and_grad(uv_ref[...])
        u2 = uvg[:, :SGU_W]
        v2 = uvg[:, SGU_W:]
        rv = _rms(v2)
        vhat = v2 * rv
        gs = gs_ref[...]
        vnb = (vhat * gs).astype(MXU)
        tril = (lax.broadcasted_iota(jnp.int32, (CHUNK, CHUNK), 0)
                >= lax.broadcasted_iota(jnp.int32, (CHUNK, CHUNK), 1))
        for c in range(tm // CHUNK):
            rs = slice(c * CHUNK, (c + 1) * CHUNK)
            vc = vnb[rs]
            mixed = _sgu_mix(vc, ws_ref) + bias_ref[...]
            dsg_c = dsg[rs]
            du2_ref[rs, :] = dsg_c * mixed
            dmx = dsg_c * u2[rs]
            dbs_ref[...] += dmx
            dmb = dmx.astype(MXU)
            dvn_ref[rs, :] = _sgu_mix(dmb, wst_ref)
            for q in range(SGU_G // 2):
                lanes = slice(LANES * q, LANES * (q + 1))
                for j, part in enumerate(_group_halves(dmb[:, lanes])):
                    dws_ref[2 * q + j] += jnp.where(tril, _dot_nt(part, vc[:, lanes]), 0.0)
        dvn = dvn_ref[...]
        dgs_ref[...] += _rowsum(dvn * vhat)
        dv2 = _rms_bwd(dvn * gs, vhat, rv)
        duv_ref[:, :SGU_W] = (du2_ref[...] * duvg[:, :SGU_W]).astype(MXU)
        duv_ref[:, SGU_W:] = (dv2 * duvg[:, SGU_W:]).astype(MXU)

    row = lambda n: pl.BlockSpec((tm, n), lambda i: (i, 0))
    return pl.pallas_call(
        body, name="mix_bwd", grid=(S // tm,),
        in_specs=[row(D_MODEL), row(2 * D_MODEL), row(D_MODEL), row(D_MODEL), row(SSM_W), row(2 * SGU_W),
                  _full(w_out.shape), _full(w_pa.shape), _full(w_pb.shape), _full(w_glu.shape), _full(b_glu.shape),
                  _full(g_sgu.shape), _full(ws.shape), _full(ws_t.shape), _full(bias_s.shape)],
        out_specs=[row(2 * D_MODEL), row(D_MODEL), row(D_MODEL), row(SSM_W), row(SSM_W), row(2 * SGU_W),
                   _full((1, SSM_W)), _full((1, SGU_W)), _full((SGU_G, CHUNK, CHUNK)), _full((CHUNK, SGU_W))],
        out_shape=[_sds((S, 2 * D_MODEL), MXU), _sds((S, D_MODEL), MXU), _sds((S, D_MODEL), MXU), _sds((S, SSM_W), MXU),
                   _sds((S, SSM_W)), _sds((S, 2 * SGU_W), MXU),
                   _sds((1, SSM_W)), _sds((1, SGU_W)), _sds((SGU_G, CHUNK, CHUNK)), _sds((CHUNK, SGU_W))],
        scratch_shapes=[pltpu.VMEM((tm, SGU_W), F32), pltpu.VMEM((tm, SGU_W), F32)],
        compiler_params=_cp("arbitrary"),
    )(*_in_hbm([dx1, gl, ya, yb, ys, uv, w_out, w_pa, w_pb, w_glu, b_glu, g_sgu, ws, ws_t, bias_s]))


def _s5_bwd(dys, us, st_re, st_im, abar_re, abar_im, b_re, b_im, c_re, c_im, d_skip, tm):
    S = us.shape[0]
    nt = S // tm
    w = 8 * SSM_P
    hb = tm // 8
    run = tm // 8
    assert run & (run - 1) == 0

    def body(dys_ref, us_ref, str_ref, sti_ref, hr_ref, hi_ref, ar_ref, ai_ref, br_ref, bi_ref, cr_ref, ci_ref, d_ref,
             dus_ref, dab_ref, dd_ref, dbr_ref, dbi_ref, dcr_ref, dci_ref,
             tab_ref, car_ref, gr_ref, gi_ref, dyp_ref, up_ref, dun_ref):
        i = pl.program_id(1)
        ri = nt - 1 - i

        @pl.when(i == 0)
        def _():
            car_ref[...] = jnp.zeros_like(car_ref)
            for k, t in enumerate(_scan_tables(*_cpow2(ar_ref[...], -ai_ref[...], run.bit_length() - 1), True)):
                tab_ref[k] = t
            for r in (dab_ref, dd_ref, dbr_ref, dbi_ref, dcr_ref, dci_ref):
                r[...] = jnp.zeros_like(r)

        _runs_load(dys_ref, dyp_ref, run)
        _runs_load(us_ref, up_ref, run)
        dyb = dyp_ref[...].astype(MXU)
        gr_ref[...] = _dot(dyb, cr_ref[0])
        gi_ref[...] = -_dot(dyb, ci_ref[0])
        ar = jnp.broadcast_to(ar_ref[...], (8, w))
        ai = jnp.broadcast_to(-ai_ref[...], (8, w))

        def advance(kk, state):
            r0 = pl.multiple_of((run - 1 - kk) * 8, 8)
            gr, gi = state
            return (ar * gr - ai * gi + gr_ref[pl.ds(r0, 8), :], ar * gi + ai * gr + gi_ref[pl.ds(r0, 8), :])

        def emit(kk, state):
            r0 = pl.multiple_of((run - 1 - kk) * 8, 8)
            gr, gi = advance(kk, state)
            gr_ref[pl.ds(r0, 8), :] = gr
            gi_ref[pl.ds(r0, 8), :] = gi
            return gr, gi

        zero = jnp.zeros((8, w), F32)
        er, ei = lax.fori_loop(0, run, advance, (zero, zero))
        cr, ci = car_ref[0:1, :], car_ref[1:2, :]
        tr, ti = _scan_group(er, ei, tab_ref, cr, ci, True)
        r8 = lax.broadcasted_iota(jnp.int32, (8, w), 0)
        start = (jnp.where(r8 == 7, cr, pltpu.roll(tr, 7, 0)), jnp.where(r8 == 7, ci, pltpu.roll(ti, 7, 0)))
        car_ref[0:1, :] = tr[0:1, :]
        car_ref[1:2, :] = ti[0:1, :]
        lax.fori_loop(0, run, emit, start)

        gsr = gr_ref[...]
        gsi = gi_ref[...]
        sr = str_ref[...]
        si = sti_ref[...]
        first = ri == 0

        def previous(s, halo_ref):
            head = jnp.where(r8 == 0, jnp.where(first, 0.0, halo_ref[7:8, :]), pltpu.roll(s[tm - 8:tm, :], 1, 0))
            return jnp.concatenate([head, s[0:tm - 8, :]], axis=0)

        spr = previous(sr, hr_ref)
        spi = previous(si, hi_ref)
        dab_ref[0, 0:1, :] += _rowsum(gsr * spr + gsi * spi)
        dab_ref[0, 1:2, :] += _rowsum(gsi * spr - gsr * spi)

        gbr = gsr.astype(MXU)
        gbi = gsi.astype(MXU)
        _runs_store(_dot_nt(gbr, br_ref[0]) + _dot_nt(gbi, bi_ref[0]), dun_ref, run)
        dys_v = dys_ref[...]
        dus_ref[...] = (dun_ref[...] + d_ref[...] * dys_v).astype(MXU)
        dd_ref[0, 0:1, :] += _rowsum(dys_v * us_ref[...])
        ub = up_ref[...].astype(MXU)
        dbr_ref[0] += _dot_tn(ub, gbr)
        dbi_ref[0] += _dot_tn(ub, gbi)
        dcr_ref[0] += _dot_tn(dyb, sr.astype(MXU))
        dci_ref[0] -= _dot_tn(dyb, si.astype(MXU))

    blk = lambda: pl.BlockSpec((1, 8 * SSM_H, w), lambda j, i: (j, 0, 0))
    rowl = lambda: pl.BlockSpec((tm, LANES), lambda j, i: (nt - 1 - i, j))
    roww = lambda: pl.BlockSpec((tm, w), lambda j, i: (nt - 1 - i, j))
    halo = lambda: pl.BlockSpec((8, w), lambda j, i: (jnp.maximum((nt - 1 - i) * hb - 1, 0), j))
    return pl.pallas_call(
        body, name="s5_bwd", grid=(SSM_BLK, nt),
        in_specs=[rowl(), rowl(), roww(), roww(), halo(), halo(),
                  pl.BlockSpec((1, w), lambda j, i: (0, j)), pl.BlockSpec((1, w), lambda j, i: (0, j)),
                  blk(), blk(), blk(), blk(),
                  pl.BlockSpec((1, LANES), lambda j, i: (0, j))],
        out_specs=[rowl(),
                   pl.BlockSpec((1, 8, w), lambda j, i: (j, 0, 0)), pl.BlockSpec((1, 8, LANES), lambda j, i: (j, 0, 0)),
                   blk(), blk(), blk(), blk()],
        out_shape=[_sds((S, SSM_W), MXU), _sds((SSM_BLK, 8, w)), _sds((SSM_BLK, 8, LANES)),
                   _sds((SSM_BLK, 8 * SSM_H, w)), _sds((SSM_BLK, 8 * SSM_H, w)),
                   _sds((SSM_BLK, 8 * SSM_H, w)), _sds((SSM_BLK, 8 * SSM_H, w))],
        scratch_shapes=[pltpu.VMEM((8, 8, w), F32), pltpu.VMEM((8, w), F32),
                        pltpu.VMEM((tm, w), F32), pltpu.VMEM((tm, w), F32),
                        pltpu.VMEM((tm, LANES), F32), pltpu.VMEM((tm, LANES), F32), pltpu.VMEM((tm, LANES), F32)],
        compiler_params=_cp("parallel", "arbitrary"),
    )(*_in_hbm([dys, us, st_re, st_im, st_re, st_im, abar_re, abar_im, b_re, b_im, c_re, c_im, d_skip]))


def _in_bwd(dus, duv, dgl, dx1, x, g_mix, w_in, tm):
    S = x.shape[0]

    def body(dus_ref, duv_ref, dgl_ref, dx1_ref, x_ref, g_ref, w_ref, gx_ref, dg_ref):
        @pl.when(pl.program_id(0) == 0)
        def _():
            dg_ref[...] = jnp.zeros_like(dg_ref)

        dh = (_dot(dus_ref[...], w_ref[0:SSM_W, :])
              + _dot(duv_ref[...], w_ref[SSM_W:SSM_W + 2 * SGU_W, :])
              + _dot(dgl_ref[...], w_ref[SSM_W + 2 * SGU_W:, :]))
        xv = x_ref[...]
        r = _rms(xv)
        xn = xv * r
        dg_ref[...] += _rowsum(dh * xn)
        gx_ref[...] = dx1_ref[...] + _rms_bwd(dh * g_ref[...], xn, r)

    row = lambda n: pl.BlockSpec((tm, n), lambda i: (i, 0))
    return pl.pallas_call(
        body, name="in_bwd", grid=(S // tm,),
        in_specs=[row(SSM_W), row(2 * SGU_W), row(2 * D_MODEL), row(D_MODEL), row(D_MODEL), _full((1, D_MODEL)),
                  _full(w_in.shape)],
        out_specs=[row(D_MODEL), _full((1, D_MODEL))],
        out_shape=[_sds((S, D_MODEL)), _sds((1, D_MODEL))],
        compiler_params=_cp("arbitrary"),
    )(*_in_hbm([dus, duv, dgl, dx1, x, g_mix, w_in]))


def _wgrad_split(a, b, nsplit, tk, name):
    S, K = a.shape
    N = b.shape[1]
    c = N // nsplit

    def body(a_ref, b_ref, o_ref):
        prod = _dot_tn(a_ref[...], b_ref[...])
        for d in range(nsplit):
            o_ref[d] = prod[:, c * d:c * (d + 1)].astype(MXU)

    return pl.pallas_call(
        body, name=name, grid=(K // tk,),
        in_specs=[pl.BlockSpec((S, tk), lambda k: (0, k)), _full((S, N))],
        out_specs=pl.BlockSpec((nsplit, tk, c), lambda k: (0, k, 0)),
        out_shape=_sds((nsplit, K, c), MXU),
        compiler_params=_cp("parallel"),
    )(*_in_hbm([a, b]))


def _wgrad_in_t(dps, h1, name):
    S, K = h1.shape
    cw = 512
    counts = [b.shape[1] // cw for b in dps]
    starts = [sum(counts[:i]) for i in range(len(dps))]
    nblk = sum(counts)

    def body(*refs):
        b_refs = refs[:len(dps)]
        h_ref, o_ref = refs[len(dps)], refs[-1]
        j = pl.program_id(0)
        for b_ref, st, cnt in zip(b_refs, starts, counts):
            @pl.when(jnp.logical_and(j >= st, j < st + cnt))
            def _():
                o_ref[...] = _dot_tn(b_ref[...], h_ref[...]).astype(MXU)

    def src_spec(st, cnt):
        return pl.BlockSpec((S, cw), lambda j: (0, jnp.clip(j - st, 0, cnt - 1)))

    return pl.pallas_call(
        body, name=name, grid=(nblk,),
        in_specs=[src_spec(st, cnt) for st, cnt in zip(starts, counts)] + [_full((S, K))],
        out_specs=pl.BlockSpec((cw, K), lambda j: (j, 0)),
        out_shape=_sds((nblk * cw, K), MXU),
        compiler_params=_cp("arbitrary"),
    )(*_in_hbm([*dps, h1]))


def _wgrad_blk(a3, b3, nblk, a_of, b_of, name):
    S, K = a3.shape[1:]
    N = b3.shape[2]

    def body(a_ref, b_ref, o_ref):
        o_ref[0] = _dot_tn(a_ref[0], b_ref[0]).astype(MXU)

    return pl.pallas_call(
        body, name=name, grid=(nblk,),
        in_specs=[pl.BlockSpec((1, S, K), lambda b: (a_of(b), 0, 0)),
                  pl.BlockSpec((1, S, N), lambda b: (b_of(b), 0, 0))],
        out_specs=pl.BlockSpec((1, K, N), lambda b: (b, 0, 0)),
        out_shape=_sds((nblk, K, N), MXU),
        compiler_params=pltpu.CompilerParams(dimension_semantics=("parallel",), vmem_limit_bytes=WGRAD_VMEM_LIMIT),
    )(*_in_hbm([a3, b3]))


def _assemble_cols(blocks_list, name):
    def body(*refs):
        n = len(blocks_list)
        for b_ref, o_ref in zip(refs[:n], refs[n:]):
            c = b_ref.shape[2]
            for d in range(N_DEV):
                o_ref[:, c * d:c * (d + 1)] = b_ref[d]

    outs = [_sds((b.shape[1], N_DEV * b.shape[2]), b.dtype) for b in blocks_list]
    return pl.pallas_call(
        body, name=name, grid=(1,), in_specs=[_full(b.shape) for b in blocks_list],
        out_specs=[_full(o.shape) for o in outs], out_shape=outs, compiler_params=_cp("arbitrary"),
    )(*_in_hbm(blocks_list))


def _tile(S, want):
    return want if S % want == 0 else S


def _local_step(x, tgt, p, mixer_relay, mixer_weights, ffn_weights, grads_out, small_out):
    S = x.shape[0]
    tm = _tile(S, 256)
    tl = _tile(S, 512)

    rep = lambda a: jnp.repeat(a, SSM_H, axis=0)
    are = rep(p["a_re"])
    aim = rep(p["a_im"])
    ldt = jnp.broadcast_to(rep(p["log_dt"].reshape(SSM_G, 1)), are.shape)
    br_t = p["b_re_t"].reshape(are.shape)
    bi_t = p["b_im_t"].reshape(are.shape)
    abr, abi, bbr, bbi = _s5_params_fwd(are, aim, ldt, br_t, bi_t)
    head = lambda a: a.reshape(SSM_G, SSM_H, SSM_P)[:, 0, :].reshape(1, SSM_G * SSM_P)
    abar_re, abar_im = head(abr), head(abi)
    bd_br = _blockdiag(bbr).astype(MXU)
    bd_bi = _blockdiag(bbi).astype(MXU)
    bd_cr = _blockdiag(p["c_re"].reshape(are.shape)).astype(MXU)
    bd_ci = _blockdiag(p["c_im"].reshape(are.shape)).astype(MXU)
    d_skip = p["d_skip"].reshape(1, SSM_W)

    tril = jnp.tril(jnp.ones((CHUNK, CHUNK), dtype=bool))
    ws = jnp.where(tril[None], p["w_s"], 0.0)
    pair = lambda w: w.reshape(SGU_G // 2, 2, CHUNK, CHUNK).transpose(0, 2, 1, 3).reshape(SGU_G // 2, CHUNK, 2 * CHUNK)
    ws_b = pair(ws).astype(MXU)
    ws_t = pair(ws.transpose(0, 2, 1)).astype(MXU)
    bias_s = jnp.repeat(p["b_s"].T, SGU_D, axis=1)

    g_mix = p["g_mix"].reshape(1, D_MODEL)
    g_ffn = p["g_ffn"].reshape(1, D_MODEL)
    g_final = p["g_final"].reshape(1, D_MODEL)
    g_sgu = p["g_sgu"].reshape(1, SGU_W)
    b_glu = p["b_glu"].reshape(1, SSM_W)
    conv_b = p["conv_b"].reshape(2 * FF_NCB, 1, FF_CW)
    tf = _tile(S, 256)

    h1, us, uv, gl = _in_fwd(x, g_mix, p["w_in_t"], tl)
    token = mixer_relay(us)
    st_re, st_im, ys = _s5_fwd(us, abar_re, abar_im, bd_br, bd_bi, bd_cr, bd_ci, d_skip + token[0:1, 0:1], tl)
    p = dict(p, **mixer_weights(ys))
    yg, yap, sg, ya, yb, m, x1, h2 = _mix_fwd(x, ys, uv, gl, p["w_glu"], b_glu, p["w_proj_a"], g_sgu, ws_b, bias_s,
                                              p["w_proj_b"], p["w_out"], g_ffn, tm)
    w_up, conv_w, w_down = ffn_weights(h2)
    pair_lanes = lambda a: a.reshape(N_DEV // 2, 2, a.shape[1], FF_SHARD).transpose(0, 2, 1, 3).reshape(
        N_DEV // 2, a.shape[1], FF_CW)
    w_up = w_up.reshape(2 * FF_NCB, FF_CW, D_MODEL)
    conv_w = pair_lanes(conv_w)
    up, ab, ff, dx2, dx2b, loss, dg_final = _ffn_fwd(h2, x1, tgt, w_up, conv_w, conv_b, w_down, g_final, tf)

    dup, dx1, dx1b, dconv, dg_ffn = _ffn_bwd(dx2, up, ab, x1, w_up, conv_w, w_down, g_ffn, tf)
    rows8 = lambda g: g.reshape(N_DEV, g.shape[1] // N_DEV, g.shape[2])
    g_up = _wgrad_blk(dup.reshape(2 * FF_NCB, S, FF_CW), h2[None], 2 * FF_NCB, lambda b: b, lambda b: 0,
                      "wgrad_up").reshape(N_DEV, FF_SHARD, D_MODEL)
    g_down = _wgrad_blk(ff, dx2b[None], FF_NCB, lambda b: b, lambda b: 0, "wgrad_down").reshape(
        N_DEV, D_FF // N_DEV, D_MODEL)
    token = grads_out(("w_up", "w_down"), (g_up, g_down))
    dgl, dya, dyb, dz, dys, duv, db_glu, dg_sgu, dws, dbs = _mix_bwd(
        dx1, gl, ya, yb, ys, uv, p["w_out"], p["w_proj_a"], p["w_proj_b"], p["w_glu"], b_glu + token[0:1, 0:1], g_sgu,
        ws_b, ws_t, bias_s, tm)
    token = grads_out(("w_glu", "w_proj_a", "w_proj_b", "w_out"),
                      (rows8(_wgrad_split(yg, dz, 1, SSM_W, "wgrad_glu")),
                       _wgrad_split(yap, dya, N_DEV, SSM_W, "wgrad_pa"),
                       _wgrad_split(sg, dyb, N_DEV, SGU_W, "wgrad_pb"),
                       rows8(_wgrad_split(m, dx1b, 1, 512, "wgrad_out"))))
    dus, dab, dd, dbbr, dbbi, dcr, dci = _s5_bwd(dys, us, st_re, st_im, abar_re, abar_im, bd_br, bd_bi, bd_cr, bd_ci,
                                                 d_skip + token[0:1, 0:1], tl)
    g_in = _wgrad_in_t([dus, duv, dgl], h1, "wgrad_in")
    token = grads_out(("w_in",), (g_in.reshape(N_DEV, g_in.shape[0] // N_DEV, D_MODEL),))
    grad_x, dg_mix = _in_bwd(dus, duv, dgl, dx1, x, g_mix + token[0:1, 0:1], p["w_in_t"], tl)

    spread = lambda v: jnp.repeat(v.reshape(SSM_G, SSM_P), SSM_H, axis=0) * (1.0 / SSM_H)
    dabr = spread(dab[:, 0, :])
    dabi = spread(dab[:, 1, :])
    dare, daim, dldt, dbr_t, dbi_t = _s5_params_bwd(are, aim, ldt, br_t, bi_t, dabr, dabi,
                                                    _unblockdiag(dbbr), _unblockdiag(dbbi))
    fold = lambda a: a.reshape(SSM_G, SSM_H, SSM_P).sum(axis=1)

    grads = {
        "g_mix": dg_mix,
        "a_re": fold(dare), "a_im": fold(daim), "log_dt": fold(dldt).sum(axis=1),
        "b_re": dbr_t, "b_im": dbi_t,
        "c_re": _unblockdiag(dcr).reshape(SSM_G, SSM_H, SSM_P),
        "c_im": _unblockdiag(dci).reshape(SSM_G, SSM_H, SSM_P),
        "d_skip": dd[:, 0, :].reshape(SSM_W),
        "b_glu": db_glu,
        "g_sgu": dg_sgu,
        "w_s": dws,
        "b_s": dbs.reshape(CHUNK, SGU_G, SGU_D).sum(axis=-1).T,
        "g_ffn": dg_ffn,
        "conv_w": dconv[:, 0:3, :].reshape(N_DEV // 2, 3, 2, FF_SHARD).transpose(0, 2, 1, 3).reshape(
            N_DEV, 3, FF_SHARD),
        "conv_b": dconv[:, 3, :].reshape(2 * D_FF),
        "g_final": dg_final,
    }
    small_out(grads, loss)
    return grad_x


_ANY = pl.BlockSpec(memory_space=pl.ANY)
_MESH = pl.DeviceIdType.MESH


def _allgather(shards, dtypes, name, cast_only=()):
    n = len(shards)
    e = len(cast_only)

    def body(*refs):
        in_refs, extra_in = refs[:n], refs[n:n + e]
        out_refs, extra_out = refs[n + e:2 * n + e], refs[2 * n + e:2 * n + 2 * e]
        stage = refs[2 * n + 2 * e:3 * n + 2 * e]
        send_sems, recv_sems, local_sems = refs[3 * n + 2 * e:]
        for a in range(n):
            stage[a][...] = in_refs[a][...].astype(dtypes[a])
        for i in range(e):
            extra_out[i][...] = extra_in[i][...].astype(MXU)
        x, y, c = lax.axis_index("x"), lax.axis_index("y"), lax.axis_index("c")
        me, sibling = (x, y, c), (x, y, 1 - c)
        chips = [(1 - x, y), (x, 1 - y), (1 - x, 1 - y)]

        def slot(a, px, py, pc):
            return out_refs[a].at[4 * px + 2 * py + pc]

        def copy(a, k, block, to, src=None):
            return pltpu.make_async_remote_copy(
                src_ref=slot(a, *block) if src is None else src, dst_ref=slot(a, *block),
                send_sem=send_sems.at[a, k], recv_sem=recv_sems.at[a, k], device_id=to, device_id_type=_MESH)

        mine = [pltpu.make_async_copy(stage[a], slot(a, *me), local_sems.at[a]) for a in range(n)]
        for cp in mine:
            cp.start()
        first = []
        for j, chip in enumerate(chips):
            first += [copy(a, 1 + j, me, (*chip, c), src=stage[a]) for a in range(n)]
        first += [copy(a, 0, me, sibling, src=stage[a]) for a in range(n)]
        for cp in first:
            cp.start()
        passed = []
        for j, chip in enumerate(chips):
            for a in range(n):
                copy(a, 1 + j, (*chip, c), me).wait_recv()
                fwd = copy(a, 4 + j, (*chip, c), sibling)
                fwd.start()
                passed.append(fwd)
        for a in range(n):
            copy(a, 0, sibling, me).wait_recv()
        for j, chip in enumerate(chips):
            for a in range(n):
                copy(a, 4 + j, (*chip, 1 - c), me).wait_recv()
        for cp in first + passed:
            cp.wait_send()
        for cp in mine:
            cp.wait()

    res = pl.pallas_call(
        body, name=name, grid=(1,), in_specs=[_full(s.shape) for s in list(shards) + list(cast_only)],
        out_specs=[_ANY] * n + [_full(s.shape) for s in cast_only],
        out_shape=[_sds((N_DEV,) + s.shape, dt) for s, dt in zip(shards, dtypes)]
                  + [_sds(s.shape, MXU) for s in cast_only],
        scratch_shapes=[pltpu.VMEM(s.shape, dt) for s, dt in zip(shards, dtypes)]
                       + [pltpu.SemaphoreType.DMA((n, 7)), pltpu.SemaphoreType.DMA((n, 7)), pltpu.SemaphoreType.DMA((n,))],
        compiler_params=pltpu.CompilerParams(vmem_limit_bytes=VMEM_LIMIT),
    )(*_in_hbm([*shards, *cast_only]))
    return res[:n], res[n:]


_HBM = pl.BlockSpec(memory_space=pltpu.HBM)
_SEM = pl.BlockSpec(memory_space=pltpu.SEMAPHORE)
_EFFECT = pltpu.SideEffectType.DATAFLOW_SIDE_EFFECTING
_PEER_ORDER = (2, 4, 6, 3, 5, 7, 1)


def _peer(k):
    x, y, c = lax.axis_index("x"), lax.axis_index("y"), lax.axis_index("c")
    px = 1 - x if k & 4 else x
    py = 1 - y if k & 2 else y
    pc = 1 - c if k & 1 else c
    return (px, py, pc), 4 * px + 2 * py + pc


_SAME_CORE_AND_SIBLING = (2, 4, 6, 1)


def _push_start(srcs, lands, slotted, name, peers=_PEER_ORDER):
    n = len(srcs)

    def body(*refs):
        src_refs, land_refs = refs[:n], refs[n:2 * n]
        send_sems, recv_sems, token = refs[2 * n], refs[2 * n + 1], refs[-1]
        mine = 4 * lax.axis_index("x") + 2 * lax.axis_index("y") + lax.axis_index("c")
        for k in peers:
            dev, theirs = _peer(k)
            for a in range(n):
                pltpu.make_async_remote_copy(
                    src_ref=src_refs[a].at[theirs] if slotted else src_refs[a], dst_ref=land_refs[a].at[mine],
                    send_sem=send_sems.at[7 * a + k - 1], recv_sem=recv_sems.at[7 * a + k - 1],
                    device_id=dev, device_id_type=_MESH).start()
        token[...] = jnp.zeros_like(token)

    bufs = list(srcs) + list(lands)
    res = pl.pallas_call(
        body, name=name, in_specs=[_HBM] * (2 * n),
        out_specs=(_SEM, _SEM, *[_HBM] * (2 * n), pl.BlockSpec(memory_space=pltpu.VMEM)),
        out_shape=(pltpu.SemaphoreType.DMA((7 * n,)), pltpu.SemaphoreType.DMA((7 * n,)),
                   *[pltpu.HBM(b.shape, b.dtype) for b in bufs], _sds((8, LANES))),
        input_output_aliases={i: 2 + i for i in range(2 * n)},
        compiler_params=pltpu.CompilerParams(has_side_effects=_EFFECT),
    )(*[pltpu.with_memory_space_constraint(b, pltpu.HBM) for b in bufs])
    return res[0], res[1], res[2:2 + n], res[2 + n:2 + 2 * n], res[-1]


def _push_wait(send_sems, recv_sems, srcs, lands, slotted, after, name, peers=_PEER_ORDER):
    n = len(srcs)

    def body(*refs):
        src_refs, land_refs = refs[:n], refs[n:2 * n]
        send_sems, recv_sems = refs[2 * n], refs[2 * n + 1]
        for k in peers:
            dev, theirs = _peer(k)
            for a in range(n):
                cp = pltpu.make_async_remote_copy(
                    src_ref=src_refs[a].at[theirs] if slotted else src_refs[a], dst_ref=land_refs[a].at[theirs],
                    send_sem=send_sems.at[7 * a + k - 1], recv_sem=recv_sems.at[7 * a + k - 1],
                    device_id=dev, device_id_type=_MESH)
                cp.wait_send()
                cp.wait_recv()

    bufs = list(srcs) + list(lands)
    res = pl.pallas_call(
        body, name=name, in_specs=[_HBM] * (2 * n) + [_SEM, _SEM] + [_ANY] * len(after), out_specs=[_HBM] * (2 * n),
        out_shape=[pltpu.HBM(b.shape, b.dtype) for b in bufs],
        input_output_aliases={i: i for i in range(2 * n)},
        compiler_params=pltpu.CompilerParams(has_side_effects=_EFFECT),
    )(*bufs, send_sems, recv_sems, *after)
    return res[n:]


def _other_chips():
    x, y = lax.axis_index("x"), lax.axis_index("y")
    return ((1 - x, y), (x, 1 - y), (1 - x, 1 - y))


def _relay_start(lands, name):
    n = len(lands)

    def body(*refs):
        land_refs = refs[:n]
        send_sems, recv_sems, token = refs[n], refs[n + 1], refs[-1]
        x, y, c = lax.axis_index("x"), lax.axis_index("y"), lax.axis_index("c")
        for j, (px, py) in enumerate(_other_chips()):
            slot = 4 * px + 2 * py + c
            for a in range(n):
                pltpu.make_async_remote_copy(
                    src_ref=land_refs[a].at[slot], dst_ref=land_refs[a].at[slot],
                    send_sem=send_sems.at[3 * a + j], recv_sem=recv_sems.at[3 * a + j],
                    device_id=(x, y, 1 - c), device_id_type=_MESH).start()
        token[...] = jnp.zeros_like(token)

    res = pl.pallas_call(
        body, name=name, in_specs=[_HBM] * n,
        out_specs=(_SEM, _SEM, *[_HBM] * n, pl.BlockSpec(memory_space=pltpu.VMEM)),
        out_shape=(pltpu.SemaphoreType.DMA((3 * n,)), pltpu.SemaphoreType.DMA((3 * n,)),
                   *[pltpu.HBM(b.shape, b.dtype) for b in lands], _sds((8, LANES))),
        input_output_aliases={i: 2 + i for i in range(n)},
        compiler_params=pltpu.CompilerParams(has_side_effects=_EFFECT),
    )(*[pltpu.with_memory_space_constraint(b, pltpu.HBM) for b in lands])
    return res[0], res[1], res[2:2 + n], res[-1]


def _relay_wait(send_sems, recv_sems, lands, after, name):
    n = len(lands)

    def body(*refs):
        land_refs = refs[:n]
        send_sems, recv_sems = refs[n], refs[n + 1]
        x, y, c = lax.axis_index("x"), lax.axis_index("y"), lax.axis_index("c")
        for j, (px, py) in enumerate(_other_chips()):
            sent, received = 4 * px + 2 * py + c, 4 * px + 2 * py + (1 - c)
            for a in range(n):
                cp = pltpu.make_async_remote_copy(
                    src_ref=land_refs[a].at[sent], dst_ref=land_refs[a].at[received],
                    send_sem=send_sems.at[3 * a + j], recv_sem=recv_sems.at[3 * a + j],
                    device_id=(x, y, 1 - c), device_id_type=_MESH)
                cp.wait_send()
                cp.wait_recv()

    return pl.pallas_call(
        body, name=name, in_specs=[_HBM] * n + [_SEM, _SEM] + [_ANY] * len(after), out_specs=[_HBM] * n,
        out_shape=[pltpu.HBM(b.shape, b.dtype) for b in lands],
        input_output_aliases={i: i for i in range(n)},
        compiler_params=pltpu.CompilerParams(has_side_effects=_EFFECT),
    )(*lands, send_sems, recv_sems, *after)


def _adamw(w, g, m, v):
    m2 = ADAM_B1 * m + (1.0 - ADAM_B1) * g
    v2 = ADAM_B2 * v + (1.0 - ADAM_B2) * (g * g)
    m_hat = m2 / (1.0 - ADAM_B1 ** ADAM_STEP)
    v_hat = v2 / (1.0 - ADAM_B2 ** ADAM_STEP)
    delta = -ADAM_LR * (m_hat / (jnp.sqrt(v_hat) + ADAM_EPS) + ADAM_WD * w)
    return delta, m2, v2


def _adam_shard(parts, w, m, v, name):
    _, r, c = w.shape
    tr = max(t for t in range(16, 257, 16) if r % t == 0)

    nparts = parts.shape[0]

    def body(p_ref, w_ref, m_ref, v_ref, g_ref, d_ref, m2_ref, v2_ref):
        g = p_ref[0].astype(F32)
        for s in range(1, nparts):
            g = g + p_ref[s].astype(F32)
        g_ref[0] = g
        d_ref[0], m2_ref[0], v2_ref[0] = _adamw(w_ref[0], g, m_ref[0], v_ref[0])

    row = lambda: pl.BlockSpec((1, tr, c), lambda i: (0, i, 0))
    return pl.pallas_call(
        body, name=name, grid=(r // tr,),
        in_specs=[pl.BlockSpec((nparts, tr, c), lambda i: (0, i, 0)), row(), row(), row()],
        out_specs=[row(), row(), row(), row()], out_shape=[_sds((1, r, c))] * 4,
        compiler_params=_cp("parallel"),
    )(*_in_hbm([parts, w, m, v]))


def _adam_small(gs, ws, ms, vs, name):
    n = len(gs)

    def body(*refs):
        ins, outs = refs[:4 * n], refs[4 * n:]
        for i in range(n):
            g = ins[i][...]
            d, m2, v2 = _adamw(ins[n + i][...], g, ins[2 * n + i][...], ins[3 * n + i][...])
            outs[i][...] = d
            outs[n + i][...] = m2
            outs[2 * n + i][...] = v2

    res = pl.pallas_call(
        body, name=name, grid=(1,), in_specs=[_full(w.shape) for w in ws] * 4,
        out_specs=[_full(w.shape) for w in ws] * 3, out_shape=[_sds(w.shape) for w in ws] * 3,
        compiler_params=_cp("arbitrary"),
    )(*_in_hbm([*gs, *ws, *ms, *vs]))
    return res[:n], res[n:2 * n], res[2 * n:]


def _sum_slots(parts, name):
    R = parts.shape[1]

    def body(p_ref, o_ref):
        g = p_ref[0]
        for s in range(1, N_DEV):
            g = g + p_ref[s]
        o_ref[...] = g

    return pl.pallas_call(body, name=name, grid=(1,), in_specs=[_full(parts.shape)], out_specs=_full((R, LANES)),
                          out_shape=_sds((R, LANES)))(*_in_hbm([parts]))


def _pad_to(a, n, axis):
    extra = n - a.shape[axis]
    if extra == 0:
        return a
    widths = [(0, 0)] * a.ndim
    widths[axis] = (0, extra)
    return jnp.pad(a, widths)


def _ceil_to(n, k):
    return -(-n // k) * k


def _pack_rows(flats, rows_multiple):
    parts = [_pad_to(f, _ceil_to(f.shape[-1], LANES), f.ndim - 1) for f in flats]
    cat = jnp.concatenate(parts, axis=-1)
    total = _ceil_to(cat.shape[-1], LANES * rows_multiple)
    cat = _pad_to(cat, total, cat.ndim - 1)
    return cat.reshape(cat.shape[:-1] + (total // LANES, LANES))


def _unpack_rows(buf, sizes):
    flat = buf.reshape(buf.shape[:-2] + (-1,))
    out, off = [], 0
    for n in sizes:
        out.append(flat[..., off:off + n])
        off += _ceil_to(n, LANES)
    return out


_MIX_BIG = ("w_in", "w_glu", "w_proj_a", "w_proj_b", "w_out")
_BIG = _MIX_BIG + ("w_up", "w_down")
_SMALL = ("g_mix", "a_re", "a_im", "log_dt", "b_re", "b_im", "c_re", "c_im", "d_skip", "b_glu", "g_sgu", "w_s", "b_s",
          "g_ffn", "conv_b", "g_final")
_SMALL_ROWS_MULTIPLE = 8 * N_DEV
_TRANSPOSED = ("w_in", "w_up", "b_re", "b_im")


def _as_2d(a):
    return a.reshape(-1, a.shape[-1]) if a.ndim > 1 else a.reshape(1, -1)


def kernel(x, g_mix, w_in, a_re, a_im, log_dt, b_re, b_im, c_re, c_im, d_skip, w_glu, b_glu, w_proj_a, g_sgu, w_s, b_s, w_proj_b, w_out, g_ffn, w_up, conv_w, conv_b, w_down, g_final, loss_target, m_g_mix, m_w_in, m_a_re, m_a_im, m_log_dt, m_b_re, m_b_im, m_c_re, m_c_im, m_d_skip, m_w_glu, m_b_glu, m_w_proj_a, m_g_sgu, m_w_s, m_b_s, m_w_proj_b, m_w_out, m_g_ffn, m_w_up, m_conv_w, m_conv_b, m_w_down, m_g_final, v_g_mix, v_w_in, v_a_re, v_a_im, v_log_dt, v_b_re, v_b_im, v_c_re, v_c_im, v_d_skip, v_w_glu, v_b_glu, v_w_proj_a, v_g_sgu, v_w_s, v_b_s, v_w_proj_b, v_w_out, v_g_ffn, v_w_up, v_conv_w, v_conv_b, v_w_down, v_g_final):
    args = dict(locals())
    me = 4 * lax.axis_index("x") + 2 * lax.axis_index("y") + lax.axis_index("c")

    def own_slot(buf, block):
        return lax.dynamic_update_slice(buf, block[None], (me,) + (0,) * block.ndim)

    for n in _TRANSPOSED:
        for pre in ("", "m_", "v_"):
            args[pre + n] = jnp.swapaxes(args[pre + n], -1, -2)
    later = ("w_glu", "w_proj_a", "w_proj_b", "w_out", "w_up", "w_down")
    (w_in_g,), casts = _allgather([args["w_in"][0]], [MXU], "allgather_w_in", cast_only=[args[n][0] for n in later])
    sh = dict(zip(later, casts))

    def start_push(srcs, tag, peers):
        lands = [own_slot(lax.empty((N_DEV,) + s.shape, s.dtype), s) for s in srcs]
        send_sems, recv_sems, srcs, lands, token = _push_start(srcs, lands, False, "push_" + tag, peers)
        return (send_sems, recv_sems, srcs, lands), token

    mix_push, token_a = start_push([sh[n] for n in later[:4]], "mixer_weights", _SAME_CORE_AND_SIBLING)
    ffn_push, token_b = start_push([sh["w_up"], sh["w_down"], conv_w[0]], "ffn_weights", _PEER_ORDER)
    p = {n: (args[n][0] if n != "g_final" else args[n]) for n in _SMALL if n not in _TRANSPOSED}
    p.update(w_in_t=w_in_g.reshape(SSM_W + 2 * SGU_W + 2 * D_MODEL, D_MODEL),
             b_re_t=args["b_re"][0], b_im_t=args["b_im"][0])
    p["g_mix"] = p["g_mix"] + (token_a[0:1, 0:1] + token_b[0:1, 0:1])
    relay = {}

    def mixer_relay(after):
        lands = _push_wait(*mix_push, False, [after], "wait_mixer_weights", _SAME_CORE_AND_SIBLING)
        relay["send"], relay["recv"], relay["lands"], token = _relay_start(lands, "relay_mixer_weights")
        return token

    def mixer_weights(after):
        w_glu_g, w_pa_g, w_pb_g, w_out_g = _relay_wait(relay["send"], relay["recv"], relay["lands"], [after],
                                                       "wait_relay_mixer_weights")
        w_pa_full, w_pb_full = _assemble_cols([w_pa_g, w_pb_g], "assemble_cols")
        return dict(w_glu=w_glu_g.reshape(SSM_W, SSM_W), w_proj_a=w_pa_full, w_proj_b=w_pb_full,
                    w_out=w_out_g.reshape(D_MODEL, D_MODEL))

    def ffn_weights(after):
        w_up_g, w_down_g, conv_w_g = _push_wait(*ffn_push, False, [after], "wait_ffn_weights")
        return w_up_g, conv_w_g, w_down_g.reshape(D_FF, D_MODEL)

    pushes = []

    def grads_out(names, sends):
        lands = [own_slot(lax.empty(s.shape, s.dtype), lax.dynamic_index_in_dim(s, me, 0, keepdims=False))
                 for s in sends]
        send_sems, recv_sems, srcs, lands, token = _push_start(list(sends), lands, True, "push_grads_" + names[0])
        pushes.append((names, send_sems, recv_sems, srcs, lands))
        return token


    small_names = _SMALL + ("conv_w", "loss")
    small = {}

    def small_out(grads, loss_part):
        small_g = dict(grads, loss=loss_part[0, 0:1])
        flats = [small_g[n].reshape(-1) for n in small_names]
        small["sizes"] = [f.shape[0] for f in flats]
        g_small = _pack_rows(flats, _SMALL_ROWS_MULTIPLE)
        small["rs8"] = g_small.shape[0] // N_DEV
        return grads_out(("small",), (g_small.reshape(N_DEV, small["rs8"], LANES),))

    grad_x = _local_step(x[0], loss_target[0], p, mixer_relay, mixer_weights, ffn_weights, grads_out, small_out)

    out = {}
    done = [grad_x]
    for names, send_sems, recv_sems, srcs, lands in pushes:
        parts = _push_wait(send_sems, recv_sems, srcs, lands, True, done, "wait_grads_" + names[0])
        if names == ("small",):
            small_mine = _sum_slots(parts[0], "sum_small")
            g_small_all = _allgather([small_mine], [F32], "allgather_small")[0][0].reshape(N_DEV * small["rs8"], LANES)
            pieces = dict(zip(small_names, _unpack_rows(g_small_all, small["sizes"])))
            loss = pieces["loss"][0]
            dconv_w = lax.dynamic_index_in_dim(pieces["conv_w"].reshape(N_DEV, 3, FF_SHARD), me, axis=0, keepdims=False)
            names2 = _SMALL + ("conv_w",)
            gs = [pieces[n].reshape(_as_2d(args[n]).shape) for n in _SMALL] + [dconv_w]
            ds, m2s, v2s = _adam_small(gs, [_as_2d(args[n]) for n in names2], [_as_2d(args["m_" + n]) for n in names2],
                                       [_as_2d(args["v_" + n]) for n in names2], "adam_small")
            for n, res in zip(names2, zip(gs, ds, m2s, v2s)):
                for kind, v in zip(("grad_", "delta_", "new_m_", "new_v_"), res):
                    out[kind + n] = v.reshape(args[n].shape)
            done = [ds[0]]
            continue
        for n, part in zip(names, parts):
            res = _adam_shard(part, args[n], args["m_" + n], args["v_" + n], "adam_" + n)
            for kind, v in zip(("grad_", "delta_", "new_m_", "new_v_"), res):
                out[kind + n] = v
            done = [res[0]]
    order = ("g_mix", "w_in", "a_re", "a_im", "log_dt", "b_re", "b_im", "c_re", "c_im", "d_skip", "w_glu", "b_glu",
             "w_proj_a", "g_sgu", "w_s", "b_s", "w_proj_b", "w_out", "g_ffn", "w_up", "conv_w", "conv_b", "w_down",
             "g_final")
    res = [loss, grad_x.reshape(x.shape)]
    for kind in ("grad_", "delta_", "new_m_", "new_v_"):
        res += [jnp.swapaxes(out[kind + n], -1, -2) if n in _TRANSPOSED else out[kind + n] for n in order]
    return tuple(res)
```

```python
import math

import jax
import jax.numpy as jnp
from jax import lax
from jax.experimental import pallas as pl
from jax.experimental.pallas import tpu as pltpu

F32 = jnp.float32
MXU = jnp.bfloat16
EPS = 1e-6

D_MODEL = 1024
SSM_W = 512
SSM_G, SSM_H, SSM_P = 32, 16, 64
SSM_BLK = 4
SGU_W = 512
SGU_G, SGU_D, CHUNK = 8, 64, 128
D_FF = 2816
N_DEV = 8
FF_SHARD = 2 * D_FF // N_DEV
FF_CW = 2 * FF_SHARD
FF_NCB = D_FF // FF_CW
LANES = 128

ADAM_LR, ADAM_B1, ADAM_B2, ADAM_EPS, ADAM_WD, ADAM_STEP = 0.001, 0.9, 0.999, 1e-08, 0.01, 10

VMEM_LIMIT = 48 * 1024 * 1024
WGRAD_VMEM_LIMIT = 58 * 1024 * 1024
FFN_VMEM_LIMIT = 58 * 1024 * 1024


def _cp(*sem):
    return pltpu.CompilerParams(dimension_semantics=sem, vmem_limit_bytes=VMEM_LIMIT)


def _full(shape):
    n = len(shape)
    return pl.BlockSpec(shape, lambda *_: (0,) * n)


def _sds(shape, dtype=F32):
    return jax.ShapeDtypeStruct(shape, dtype)


def _in_hbm(arrays):
    return [pltpu.with_memory_space_constraint(a, pltpu.HBM) for a in arrays]


def _dot(a, b):
    return jnp.dot(a, b, preferred_element_type=F32)


def _dot_nt(a, b):
    return lax.dot_general(a, b, (((1,), (1,)), ((), ())), preferred_element_type=F32)


def _dot_tn(a, b):
    return lax.dot_general(a, b, (((0,), (0,)), ((), ())), preferred_element_type=F32)


_GELU_C = math.sqrt(2.0 / math.pi)


def _gelu(x):
    return 0.5 * x * (1.0 + jnp.tanh(_GELU_C * (x + 0.044715 * (x * x * x))))


def _gelu_and_grad(x):
    t = jnp.tanh(_GELU_C * (x + 0.044715 * (x * x * x)))
    g = 0.5 * x * (1.0 + t)
    dg = 0.5 * (1.0 + t) + 0.5 * x * (1.0 - t * t) * (_GELU_C * (1.0 + 3.0 * 0.044715 * (x * x)))
    return g, dg


def _sigmoid(x):
    return 0.5 * jnp.tanh(0.5 * x) + 0.5


def _rms(x):
    return lax.rsqrt(jnp.mean(x * x, axis=-1, keepdims=True) + EPS)


def _rms_bwd(dxn, xn, r):
    return r * (dxn - xn * jnp.mean(dxn * xn, axis=-1, keepdims=True))


def _rowsum(x):
    return jnp.sum(x, axis=0, keepdims=True)


def _fetch_once(pairs, sems):
    copies = [pltpu.make_async_copy(src, dst, sems.at[k]) for k, (src, dst) in enumerate(pairs)]
    for cp in copies:
        cp.start()
    for cp in copies:
        cp.wait()


def _s5_disc(are, aim, ldt, br, bi):
    dt = jnp.exp(ldt)
    mag = jnp.exp(dt * are)
    abr = mag * jnp.cos(dt * aim)
    abi = mag * jnp.sin(dt * aim)
    den = are * are + aim * aim
    nr = abr - 1.0
    ni = abi
    fr = (nr * are + ni * aim) / den
    fi = (ni * are - nr * aim) / den
    return abr, abi, fr * br - fi * bi, fr * bi + fi * br


def _s5_params_fwd(are, aim, ldt, br, bi):
    def body(are_ref, aim_ref, ldt_ref, br_ref, bi_ref, o0, o1, o2, o3):
        outs = _s5_disc(are_ref[...], aim_ref[...], ldt_ref[...], br_ref[...], bi_ref[...])
        for o, v in zip((o0, o1, o2, o3), outs):
            o[...] = v
    shp = are.shape
    return pl.pallas_call(body, name="s5_params_fwd", grid=(1,), in_specs=[_full(shp)] * 5, out_specs=[_full(shp)] * 4,
                          out_shape=[_sds(shp)] * 4)(*_in_hbm([are, aim, ldt, br, bi]))


def _s5_params_bwd(are, aim, ldt, br, bi, dabr, dabi, dbr, dbi):
    def body(are_ref, aim_ref, ldt_ref, br_ref, bi_ref, c0, c1, c2, c3, o0, o1, o2, o3, o4):
        prim = (are_ref[...], aim_ref[...], ldt_ref[...], br_ref[...], bi_ref[...])
        _, vjp = jax.vjp(_s5_disc, *prim)
        outs = vjp((c0[...], c1[...], c2[...], c3[...]))
        for o, v in zip((o0, o1, o2, o3, o4), outs):
            o[...] = v
    shp = are.shape
    return pl.pallas_call(body, name="s5_params_bwd", grid=(1,), in_specs=[_full(shp)] * 9, out_specs=[_full(shp)] * 5,
                          out_shape=[_sds(shp)] * 5)(*_in_hbm([are, aim, ldt, br, bi, dabr, dabi, dbr, dbi]))


def _blockdiag(m_t):
    m = m_t.reshape(SSM_BLK, 8, SSM_H, 1, SSM_P)
    eye = jnp.eye(8, dtype=bool).reshape(1, 8, 1, 8, 1)
    return jnp.where(eye, m, jnp.zeros((), m_t.dtype)).reshape(SSM_BLK, 8 * SSM_H, 8 * SSM_P)


def _unblockdiag(pc):
    m = pc.reshape(SSM_BLK, 8, SSM_H, 8, SSM_P)
    return jnp.einsum("jghgp->jghp", m).reshape(SSM_G * SSM_H, SSM_P)


def _in_fwd(x, g_mix, w_in_t, tm):
    S = x.shape[0]

    def body(x_ref, g_ref, w_ref, h_ref, us_ref, uv_ref, gl_ref):
        xv = x_ref[...]
        h = (xv * _rms(xv) * g_ref[...]).astype(MXU)
        h_ref[...] = h
        us_ref[...] = _dot_nt(h, w_ref[0:SSM_W, :])
        uv_ref[...] = _dot_nt(h, w_ref[SSM_W:SSM_W + 2 * SGU_W, :])
        gl_ref[...] = _dot_nt(h, w_ref[SSM_W + 2 * SGU_W:, :])

    row = lambda n: pl.BlockSpec((tm, n), lambda i: (i, 0))
    return pl.pallas_call(
        body, name="in_fwd", grid=(S // tm,),
        in_specs=[row(D_MODEL), _full((1, D_MODEL)), _full(w_in_t.shape)],
        out_specs=[row(D_MODEL), row(SSM_W), row(2 * SGU_W), row(2 * D_MODEL)],
        out_shape=[_sds((S, D_MODEL), MXU), _sds((S, SSM_W)), _sds((S, 2 * SGU_W)), _sds((S, 2 * D_MODEL))],
        compiler_params=_cp("parallel"),
    )(*_in_hbm([x, g_mix, w_in_t]))


def _scan_tables(ar, ai, reverse):
    n = ar.shape[-1]
    def mul(p, q):
        return p[0] * q[0] - p[1] * q[1], p[0] * q[1] + p[1] * q[0]
    a1 = (ar, ai)
    a2 = mul(a1, a1)
    a3 = mul(a2, a1)
    a4 = mul(a2, a2)
    a5 = mul(a4, a1)
    a6 = mul(a4, a2)
    a7 = mul(a4, a3)
    a8 = mul(a4, a4)
    pw = (a1, a2, a3, a4, a5, a6, a7, a8)
    rows = lax.broadcasted_iota(jnp.int32, (8, n), 0)
    tabs = []
    for s, a in ((1, a1), (2, a2), (4, a4)):
        keep = (rows + s <= 7) if reverse else (rows >= s)
        for comp in a:
            tabs.append(jnp.where(keep, jnp.broadcast_to(comp, (8, n)), 0.0))
    for c in range(2):
        q = jnp.zeros((8, n), F32)
        for r in range(8):
            e = (8 - r) if reverse else (r + 1)
            q = jnp.where(rows == r, jnp.broadcast_to(pw[e - 1][c], (8, n)), q)
        tabs.append(q)
    return tabs


def _scan_group(xr, xi, tab_ref, cr, ci, reverse):
    for t, s in enumerate((1, 2, 4)):
        pr = tab_ref[2 * t]
        pi = tab_ref[2 * t + 1]
        sh = (8 - s) if reverse else s
        sr = pltpu.roll(xr, sh, 0)
        si = pltpu.roll(xi, sh, 0)
        xr, xi = xr + pr * sr - pi * si, xi + pr * si + pi * sr
    qr = tab_ref[6]
    qi = tab_ref[7]
    return xr + qr * cr - qi * ci, xi + qr * ci + qi * cr


def _runs_load(src_ref, dst_ref, run):
    for i in range(run):
        dst_ref[8 * i:8 * i + 8, :] = src_ref[pl.ds(i, 8, stride=run), :]


def _runs_store(val, dst_ref, run):
    for i in range(run):
        dst_ref[pl.ds(i, 8, stride=run), :] = val[8 * i:8 * i + 8, :]


def _cpow2(ar, ai, log2n):
    for _ in range(log2n):
        ar, ai = ar * ar - ai * ai, 2.0 * ar * ai
    return ar, ai


def _s5_fwd(us, abar_re, abar_im, b_re, b_im, c_re, c_im, d_skip, tm):
    S = us.shape[0]
    nt = S // tm
    w = 8 * SSM_P
    run = tm // 8
    assert run & (run - 1) == 0

    def body(us_ref, ar_ref, ai_ref, br_ref, bi_ref, cr_ref, ci_ref, d_ref, str_ref, sti_ref, ys_ref,
             tab_ref, car_ref, up_ref):
        i = pl.program_id(1)

        @pl.when(i == 0)
        def _():
            car_ref[...] = jnp.zeros_like(car_ref)
            for k, t in enumerate(_scan_tables(*_cpow2(ar_ref[...], ai_ref[...], run.bit_length() - 1), False)):
                tab_ref[k] = t

        _runs_load(us_ref, up_ref, run)
        ub = up_ref[...].astype(MXU)
        str_ref[...] = _dot(ub, br_ref[0])
        sti_ref[...] = _dot(ub, bi_ref[0])
        ar = jnp.broadcast_to(ar_ref[...], (8, w))
        ai = jnp.broadcast_to(ai_ref[...], (8, w))

        def advance(k, state):
            r0 = pl.multiple_of(k * 8, 8)
            sr, si = state
            return (ar * sr - ai * si + str_ref[pl.ds(r0, 8), :], ar * si + ai * sr + sti_ref[pl.ds(r0, 8), :])

        def emit(k, state):
            r0 = pl.multiple_of(k * 8, 8)
            sr, si = advance(k, state)
            str_ref[pl.ds(r0, 8), :] = sr
            sti_ref[pl.ds(r0, 8), :] = si
            return sr, si

        zero = jnp.zeros((8, w), F32)
        er, ei = lax.fori_loop(0, run, advance, (zero, zero))
        cr, ci = car_ref[0:1, :], car_ref[1:2, :]
        tr, ti = _scan_group(er, ei, tab_ref, cr, ci, False)
        r8 = lax.broadcasted_iota(jnp.int32, (8, w), 0)
        start = (jnp.where(r8 == 0, cr, pltpu.roll(tr, 1, 0)), jnp.where(r8 == 0, ci, pltpu.roll(ti, 1, 0)))
        car_ref[0:1, :] = tr[7:8, :]
        car_ref[1:2, :] = ti[7:8, :]
        lax.fori_loop(0, run, emit, start)
        y = _dot_nt(str_ref[...].astype(MXU), cr_ref[0]) - _dot_nt(sti_ref[...].astype(MXU), ci_ref[0])
        _runs_store(y, ys_ref, run)
        ys_ref[...] += d_ref[...] * us_ref[...]

    blk = lambda: pl.BlockSpec((1, 8 * SSM_H, w), lambda j, i: (j, 0, 0))
    return pl.pallas_call(
        body, name="s5_fwd", grid=(SSM_BLK, nt),
        in_specs=[pl.BlockSpec((tm, LANES), lambda j, i: (i, j)),
                  pl.BlockSpec((1, w), lambda j, i: (0, j)), pl.BlockSpec((1, w), lambda j, i: (0, j)),
                  blk(), blk(), blk(), blk(),
                  pl.BlockSpec((1, LANES), lambda j, i: (0, j))],
        out_specs=[pl.BlockSpec((tm, w), lambda j, i: (i, j)), pl.BlockSpec((tm, w), lambda j, i: (i, j)),
                   pl.BlockSpec((tm, LANES), lambda j, i: (i, j))],
        out_shape=[_sds((S, SSM_BLK * w)), _sds((S, SSM_BLK * w)), _sds((S, SSM_W))],
        scratch_shapes=[pltpu.VMEM((8, 8, w), F32), pltpu.VMEM((8, w), F32), pltpu.VMEM((tm, LANES), F32)],
        compiler_params=_cp("parallel", "arbitrary"),
    )(*_in_hbm([us, abar_re, abar_im, b_re, b_im, c_re, c_im, d_skip]))


def _group_halves(vp):
    first = lax.broadcasted_iota(jnp.int32, vp.shape, 1) < SGU_D
    zero = jnp.zeros((), vp.dtype)
    return jnp.where(first, vp, zero), jnp.where(first, zero, vp)


def _sgu_mix(vnb, wcat_ref):
    outs = []
    for q in range(SGU_G // 2):
        lo, hi = _group_halves(vnb[:, LANES * q:LANES * (q + 1)])
        outs.append(_dot(wcat_ref[q], jnp.concatenate([lo, hi], axis=0)))
    return jnp.concatenate(outs, axis=1)


def _mix_fwd(x, ys, uv, gl, w_glu, b_glu, w_pa, g_sgu, ws, bias_s, w_pb, w_out, g_ffn, tm):
    S = x.shape[0]

    def body(x_ref, ys_ref, uv_ref, gl_ref, wglu_ref, bglu_ref, wpa_ref, gs_ref, ws_ref, bias_ref, wpb_ref, wout_ref,
             gf_ref, yg_ref, yap_ref, sg_ref, ya_ref, yb_ref, m_ref, x1_ref, h2_ref):
        yg = _gelu(ys_ref[...])
        ygb = yg.astype(MXU)
        yg_ref[...] = ygb
        z = _dot(ygb, wglu_ref[...]) + bglu_ref[...]
        yapb = (yg * _sigmoid(z)).astype(MXU)
        yap_ref[...] = yapb
        ya = _dot(yapb, wpa_ref[...])
        ya_ref[...] = ya

        uvg = _gelu(uv_ref[...])
        u2 = uvg[:, :SGU_W]
        v2 = uvg[:, SGU_W:]
        vnb = (v2 * _rms(v2) * gs_ref[...]).astype(MXU)
        for c in range(tm // CHUNK):
            rs = slice(c * CHUNK, (c + 1) * CHUNK)
            mixed = _sgu_mix(vnb[rs], ws_ref) + bias_ref[...]
            sg_ref[rs, :] = (u2[rs] * mixed).astype(MXU)
        yb = _dot(sg_ref[...], wpb_ref[...])
        yb_ref[...] = yb

        glv = gl_ref[...]
        m = _sigmoid(glv[:, :D_MODEL]) * ya + _sigmoid(glv[:, D_MODEL:]) * yb
        mb = m.astype(MXU)
        m_ref[...] = mb
        x1 = x_ref[...] + _dot(mb, wout_ref[...])
        x1_ref[...] = x1
        h2_ref[...] = (x1 * _rms(x1) * gf_ref[...]).astype(MXU)

    row = lambda n: pl.BlockSpec((tm, n), lambda i: (i, 0))
    return pl.pallas_call(
        body, name="mix_fwd", grid=(S // tm,),
        in_specs=[row(D_MODEL), row(SSM_W), row(2 * SGU_W), row(2 * D_MODEL),
                  _full(w_glu.shape), _full(b_glu.shape), _full(w_pa.shape), _full(g_sgu.shape), _full(ws.shape),
                  _full(bias_s.shape), _full(w_pb.shape), _full(w_out.shape), _full(g_ffn.shape)],
        out_specs=[row(SSM_W), row(SSM_W), row(SGU_W), row(D_MODEL), row(D_MODEL), row(D_MODEL), row(D_MODEL),
                   row(D_MODEL)],
        out_shape=[_sds((S, SSM_W), MXU), _sds((S, SSM_W), MXU), _sds((S, SGU_W), MXU), _sds((S, D_MODEL)),
                   _sds((S, D_MODEL)), _sds((S, D_MODEL), MXU), _sds((S, D_MODEL)), _sds((S, D_MODEL), MXU)],
        compiler_params=_cp("parallel"),
    )(*_in_hbm([x, ys, uv, gl, w_glu, b_glu, w_pa, g_sgu, ws, bias_s, w_pb, w_out, g_ffn]))


def _causal_conv3(u, prev8, cw, cb):
    tm = u.shape[0]
    w0, w1, w2 = cw[0:1], cw[1:2], cw[2:3]
    body = w0 * pltpu.roll(u, 2, 0) + w1 * pltpu.roll(u, 1, 0) + w2 * u + cb
    u8 = u[0:8, :]
    r8 = lax.broadcasted_iota(jnp.int32, u8.shape, 0)
    t1 = prev8[7:8, :]
    t0 = prev8[6:7, :]
    s1 = jnp.where(r8 == 0, t1, pltpu.roll(u8, 1, 0))
    s2 = jnp.where(r8 == 0, t0, jnp.where(r8 == 1, t1, pltpu.roll(u8, 2, 0)))
    first = w0 * s2 + w1 * s1 + w2 * u8 + cb
    return jnp.concatenate([first, body[8:tm, :]], axis=0)


def _causal_conv3_adjoint(d, next8, cw):
    tm = d.shape[0]
    w0, w1, w2 = cw[0:1], cw[1:2], cw[2:3]
    n1 = pltpu.roll(d, tm - 1, 0)
    n2 = pltpu.roll(d, tm - 2, 0)
    body = w2 * d + w1 * n1 + w0 * n2
    d8 = d[tm - 8:tm, :]
    r8 = lax.broadcasted_iota(jnp.int32, d8.shape, 0)
    h0 = next8[0:1, :]
    h1 = next8[1:2, :]
    m1 = jnp.where(r8 == 7, h0, pltpu.roll(d8, 7, 0))
    m2 = jnp.where(r8 == 6, h0, jnp.where(r8 == 7, h1, pltpu.roll(d8, 6, 0)))
    last = w2 * d8 + w1 * m1 + w0 * m2
    out = jnp.concatenate([body[0:tm - 8, :], last], axis=0)
    return out, n1, n2, h0 - d[0:1, :], h1 - d[1:2, :]


def _ffn_fwd(h2, x1, tgt, w_up, conv_w, conv_b, w_down, g_final, tm):
    S = h2.shape[0]
    nt = S // tm
    ncb = FF_NCB

    def body(h2_ref, wup_hbm, cwa_ref, cwb_ref, cba_ref, cbb_ref, wd_hbm, x1_ref, gf_ref, tgt_ref,
             up_ref, ab_ref, ff_ref, dx2_ref, dx2b_ref, loss_ref, dgf_ref, acc_ref, tail_ref, wup_ref, wdn_ref, wsem):
        i = pl.program_id(0)
        cb = pl.program_id(1)

        @pl.when(i == 0)
        def _():
            tail_ref[cb] = jnp.zeros((2, 8, FF_CW), F32)

        @pl.when(jnp.logical_and(i == 0, cb == 0))
        def _():
            loss_ref[...] = jnp.zeros_like(loss_ref)
            dgf_ref[...] = jnp.zeros_like(dgf_ref)
            _fetch_once([(wup_hbm, wup_ref), (wd_hbm, wdn_ref)], wsem)

        h2v = h2_ref[...]
        ua = _dot_nt(h2v, wup_ref[cb])
        ub = _dot_nt(h2v, wup_ref[ncb + cb])
        up_ref[0, 0] = ua.astype(MXU)
        up_ref[1, 0] = ub.astype(MXU)
        a = _causal_conv3(ua, tail_ref[cb, 0], cwa_ref[0], cba_ref[0])
        b = _causal_conv3(ub, tail_ref[cb, 1], cwb_ref[0], cbb_ref[0])
        tail_ref[cb, 0] = ua[tm - 8:tm, :]
        tail_ref[cb, 1] = ub[tm - 8:tm, :]
        ab_ref[0, 0] = a
        ab_ref[1, 0] = b
        ffb = (a * _sigmoid(a) * b).astype(MXU)
        ff_ref[0] = ffb
        contrib = _dot(ffb, wdn_ref[pl.ds(pl.multiple_of(cb * FF_CW, FF_CW), FF_CW), :])

        @pl.when(cb == 0)
        def _():
            acc_ref[...] = contrib

        @pl.when(cb > 0)
        def _():
            acc_ref[...] += contrib

        @pl.when(cb == ncb - 1)
        def _():
            x2 = x1_ref[...] + acc_ref[...]
            r = _rms(x2)
            xn = x2 * r
            g = gf_ref[...]
            diff = xn * g - tgt_ref[...]
            loss_ref[...] += (0.5 / D_MODEL) * jnp.sum(diff * diff)
            dy = diff * (1.0 / D_MODEL)
            dgf_ref[...] += _rowsum(dy * xn)
            dx2 = _rms_bwd(dy * g, xn, r)
            dx2_ref[...] = dx2
            dx2b_ref[...] = dx2.astype(MXU)

    row = lambda n: pl.BlockSpec((tm, n), lambda i, c: (i, 0))
    gate = lambda r: pl.BlockSpec((1, r, FF_CW), lambda i, c: (c, 0, 0))
    lin = lambda r: pl.BlockSpec((1, r, FF_CW), lambda i, c: (ncb + c, 0, 0))
    return pl.pallas_call(
        body, name="ffn_fwd", grid=(nt, ncb),
        in_specs=[row(D_MODEL), _ANY, gate(3), lin(3), gate(1), lin(1), _ANY,
                  row(D_MODEL), _full((1, D_MODEL)), row(D_MODEL)],
        out_specs=[pl.BlockSpec((2, 1, tm, FF_CW), lambda i, c: (0, c, i, 0)),
                   pl.BlockSpec((2, 1, tm, FF_CW), lambda i, c: (0, c, i, 0)),
                   pl.BlockSpec((1, tm, FF_CW), lambda i, c: (c, i, 0)),
                   row(D_MODEL), row(D_MODEL), _full((1, LANES)), _full((1, D_MODEL))],
        out_shape=[_sds((2, ncb, S, FF_CW), MXU), _sds((2, ncb, S, FF_CW)), _sds((ncb, S, FF_CW), MXU),
                   _sds((S, D_MODEL)), _sds((S, D_MODEL), MXU), _sds((1, LANES)), _sds((1, D_MODEL))],
        scratch_shapes=[pltpu.VMEM((tm, D_MODEL), F32), pltpu.VMEM((ncb, 2, 8, FF_CW), F32),
                        pltpu.VMEM(w_up.shape, w_up.dtype), pltpu.VMEM(w_down.shape, w_down.dtype),
                        pltpu.SemaphoreType.DMA((2,))],
        compiler_params=pltpu.CompilerParams(dimension_semantics=("arbitrary", "arbitrary"),
                                             vmem_limit_bytes=FFN_VMEM_LIMIT),
    )(*_in_hbm([h2, w_up, conv_w, conv_w, conv_b, conv_b, w_down, x1, g_final, tgt]))


def _ffn_bwd(dx2, up, ab, x1, w_up, conv_w, w_down, g_ffn, tm):
    S = dx2.shape[0]
    nt = S // tm
    ncb = FF_NCB

    def body(dx2_ref, up_ref, ab_ref, cwa_ref, cwb_ref, wd_hbm, wup_hbm,
             x1_ref, g_ref, dup_ref, dx1_ref, dx1b_ref, dconv_ref, dg_ref, acc_ref, head_ref, wup_ref, wdn_ref, wsem):
        i = pl.program_id(0)
        cb = pl.program_id(1)

        @pl.when(i == 0)
        def _():
            head_ref[cb] = jnp.zeros((2, 8, FF_CW), F32)
            dconv_ref[cb] = jnp.zeros((8, FF_CW), F32)
            dconv_ref[ncb + cb] = jnp.zeros((8, FF_CW), F32)

        @pl.when(jnp.logical_and(i == 0, cb == 0))
        def _():
            dg_ref[...] = jnp.zeros_like(dg_ref)
            _fetch_once([(wup_hbm, wup_ref), (wd_hbm, wdn_ref)], wsem)

        dff = _dot_nt(dx2_ref[...].astype(MXU), wdn_ref[pl.ds(pl.multiple_of(cb * FF_CW, FF_CW), FF_CW), :])
        a = ab_ref[0, 0]
        b = ab_ref[1, 0]
        sa = _sigmoid(a)
        silu = a * sa
        da = (dff * b) * (sa + silu * (1.0 - sa))
        db = dff * silu
        dps = []
        for half, slot, d, cw_ref in ((0, cb, da, cwa_ref), (1, ncb + cb, db, cwb_ref)):
            dp, n1, n2, fix0, fix1 = _causal_conv3_adjoint(d, head_ref[cb, half], cw_ref[0])
            head_ref[cb, half] = d[0:8, :]
            dpb16 = dp.astype(MXU)
            dup_ref[half, 0] = dpb16
            dps.append(dpb16)
            u = up_ref[half, 0].astype(F32)
            u_last = u[tm - 1:tm, :]
            dconv_ref[slot, 0:1, :] += _rowsum(n2 * u) + fix0 * u[tm - 2:tm - 1, :] + fix1 * u_last
            dconv_ref[slot, 1:2, :] += _rowsum(n1 * u) + fix0 * u_last
            dconv_ref[slot, 2:3, :] += _rowsum(d * u)
            dconv_ref[slot, 3:4, :] += _rowsum(d)
        contrib = _dot(dps[0], wup_ref[cb]) + _dot(dps[1], wup_ref[ncb + cb])

        @pl.when(cb == 0)
        def _():
            acc_ref[...] = contrib

        @pl.when(cb > 0)
        def _():
            acc_ref[...] += contrib

        @pl.when(cb == ncb - 1)
        def _():
            x1v = x1_ref[...]
            r = _rms(x1v)
            xn = x1v * r
            dh2 = acc_ref[...]
            dg_ref[...] += _rowsum(dh2 * xn)
            dx1 = dx2_ref[...] + _rms_bwd(dh2 * g_ref[...], xn, r)
            dx1_ref[...] = dx1
            dx1b_ref[...] = dx1.astype(MXU)

    row = lambda n: pl.BlockSpec((tm, n), lambda i, c: (nt - 1 - i, 0))
    colb = lambda: pl.BlockSpec((2, 1, tm, FF_CW), lambda i, c: (0, c, nt - 1 - i, 0))
    gate = lambda r: pl.BlockSpec((1, r, FF_CW), lambda i, c: (c, 0, 0))
    lin = lambda r: pl.BlockSpec((1, r, FF_CW), lambda i, c: (ncb + c, 0, 0))
    return pl.pallas_call(
        body, name="ffn_bwd", grid=(nt, ncb),
        in_specs=[row(D_MODEL), colb(), colb(), gate(3), lin(3), _ANY, _ANY, row(D_MODEL), _full((1, D_MODEL))],
        out_specs=[colb(), row(D_MODEL), row(D_MODEL), _full((2 * ncb, 8, FF_CW)), _full((1, D_MODEL))],
        out_shape=[_sds((2, ncb, S, FF_CW), MXU), _sds((S, D_MODEL)), _sds((S, D_MODEL), MXU), _sds((2 * ncb, 8, FF_CW)),
                   _sds((1, D_MODEL))],
        scratch_shapes=[pltpu.VMEM((tm, D_MODEL), F32), pltpu.VMEM((ncb, 2, 8, FF_CW), F32),
                        pltpu.VMEM(w_up.shape, w_up.dtype), pltpu.VMEM(w_down.shape, w_down.dtype),
                        pltpu.SemaphoreType.DMA((2,))],
        compiler_params=pltpu.CompilerParams(dimension_semantics=("arbitrary", "arbitrary"),
                                             vmem_limit_bytes=FFN_VMEM_LIMIT),
    )(*_in_hbm([dx2, up, ab, conv_w, conv_w, w_down, w_up, x1, g_ffn]))


def _mix_bwd(dx1, gl, ya, yb, ys, uv, w_out, w_pa, w_pb, w_glu, b_glu, g_sgu, ws, ws_t, bias_s, tm):
    S = dx1.shape[0]

    def body(dx1_ref, gl_ref, ya_ref, yb_ref, ys_ref, uv_ref, wout_ref, wpa_ref, wpb_ref, wglu_ref, bglu_ref, gs_ref,
             ws_ref, wst_ref, bias_ref,
             dgl_ref, dya_ref, dyb_ref, dz_ref, dys_ref, duv_ref, dbglu_ref, dgs_ref, dws_ref, dbs_ref,
             du2_ref, dvn_ref):
        i = pl.program_id(0)

        @pl.when(i == 0)
        def _():
            dbglu_ref[...] = jnp.zeros_like(dbglu_ref)
            dgs_ref[...] = jnp.zeros_like(dgs_ref)
            dws_ref[...] = jnp.zeros_like(dws_ref)
            dbs_ref[...] = jnp.zeros_like(dbs_ref)

        dm = _dot_nt(dx1_ref[...].astype(MXU), wout_ref[...])
        glv = gl_ref[...]
        ga = _sigmoid(glv[:, :D_MODEL])
        gb = _sigmoid(glv[:, D_MODEL:])
        dgl_ref[:, :D_MODEL] = (dm * ya_ref[...] * ga * (1.0 - ga)).astype(MXU)
        dgl_ref[:, D_MODEL:] = (dm * yb_ref[...] * gb * (1.0 - gb)).astype(MXU)
        dyab = (dm * ga).astype(MXU)
        dybb = (dm * gb).astype(MXU)
        dya_ref[...] = dyab
        dyb_ref[...] = dybb

        dyap = _dot_nt(dyab, wpa_ref[...])
        yg, dgelu = _gelu_and_grad(ys_ref[...])
        sz = _sigmoid(_dot(yg.astype(MXU), wglu_ref[...]) + bglu_ref[...])
        dz = dyap * yg * sz * (1.0 - sz)
        dzb = dz.astype(MXU)
        dz_ref[...] = dzb
        dbglu_ref[...] += _rowsum(dz)
        dys_ref[...] = (dyap * sz + _dot_nt(dzb, wglu_ref[...])) * dgelu

        dsg = _dot_nt(dybb, wpb_ref[...])
        uvg, duvg = _gelu_and_grad(uv_ref[...])
        u2 = uvg[:, :SGU_W]
        v2 = uvg[:, SGU_W:]
        rv = _rms(v2)
        vhat = v2 * rv
        gs = gs_ref[...]
        vnb = (vhat * gs).astype(MXU)
        tril = (lax.broadcasted_iota(jnp.int32, (CHUNK, CHUNK), 0)
                >= lax.broadcasted_iota(jnp.int32, (CHUNK, CHUNK), 1))
        for c in range(tm // CHUNK):
            rs = slice(c * CHUNK, (c + 1) * CHUNK)
            vc = vnb[rs]
            mixed = _sgu_mix(vc, ws_ref) + bias_ref[...]
            dsg_c = dsg[rs]
            du2_ref[rs, :] = dsg_c * mixed
            dmx = dsg_c * u2[rs]
            dbs_ref[...] += dmx
            dmb = dmx.astype(MXU)
            dvn_ref[rs, :] = _sgu_mix(dmb, wst_ref)
            for q in range(SGU_G // 2):
                lanes = slice(LANES * q, LANES * (q + 1))
                for j, part in enumerate(_group_halves(dmb[:, lanes])):
                    dws_ref[2 * q + j] += jnp.where(tril, _dot_nt(part, vc[:, lanes]), 0.0)
        dvn = dvn_ref[...]
        dgs_ref[...] += _rowsum(dvn * vhat)
        dv2 = _rms_bwd(dvn * gs, vhat, rv)
        duv_ref[:, :SGU_W] = (du2_ref[...] * duvg[:, :SGU_W]).astype(MXU)
        duv_ref[:, SGU_W:] = (dv2 * duvg[:, SGU_W:]).astype(MXU)

    row = lambda n: pl.BlockSpec((tm, n), lambda i: (i, 0))
    return pl.pallas_call(
        body, name="mix_bwd", grid=(S // tm,),
        in_specs=[row(D_MODEL), row(2 * D_MODEL), row(D_MODEL), row(D_MODEL), row(SSM_W), row(2 * SGU_W),
                  _full(w_out.shape), _full(w_pa.shape), _full(w_pb.shape), _full(w_glu.shape), _full(b_glu.shape),
                  _full(g_sgu.shape), _full(ws.shape), _full(ws_t.shape), _full(bias_s.shape)],
        out_specs=[row(2 * D_MODEL), row(D_MODEL), row(D_MODEL), row(SSM_W), row(SSM_W), row(2 * SGU_W),
                   _full((1, SSM_W)), _full((1, SGU_W)), _full((SGU_G, CHUNK, CHUNK)), _full((CHUNK, SGU_W))],
        out_shape=[_sds((S, 2 * D_MODEL), MXU), _sds((S, D_MODEL), MXU), _sds((S, D_MODEL), MXU), _sds((S, SSM_W), MXU),
                   _sds((S, SSM_W)), _sds((S, 2 * SGU_W), MXU),
                   _sds((1, SSM_W)), _sds((1, SGU_W)), _sds((SGU_G, CHUNK, CHUNK)), _sds((CHUNK, SGU_W))],
        scratch_shapes=[pltpu.VMEM((tm, SGU_W), F32), pltpu.VMEM((tm, SGU_W), F32)],
        compiler_params=_cp("arbitrary"),
    )(*_in_hbm([dx1, gl, ya, yb, ys, uv, w_out, w_pa, w_pb, w_glu, b_glu, g_sgu, ws, ws_t, bias_s]))


def _s5_bwd(dys, us, st_re, st_im, abar_re, abar_im, b_re, b_im, c_re, c_im, d_skip, tm):
    S = us.shape[0]
    nt = S // tm
    w = 8 * SSM_P
    hb = tm // 8
    run = tm // 8
    assert run & (run - 1) == 0

    def body(dys_ref, us_ref, str_ref, sti_ref, hr_ref, hi_ref, ar_ref, ai_ref, br_ref, bi_ref, cr_ref, ci_ref, d_ref,
             dus_ref, dab_ref, dd_ref, dbr_ref, dbi_ref, dcr_ref, dci_ref,
             tab_ref, car_ref, gr_ref, gi_ref, dyp_ref, up_ref, dun_ref):
        i = pl.program_id(1)
        ri = nt - 1 - i

        @pl.when(i == 0)
        def _():
            car_ref[...] = jnp.zeros_like(car_ref)
            for k, t in enumerate(_scan_tables(*_cpow2(ar_ref[...], -ai_ref[...], run.bit_length() - 1), True)):
                tab_ref[k] = t
            for r in (dab_ref, dd_ref, dbr_ref, dbi_ref, dcr_ref, dci_ref):
                r[...] = jnp.zeros_like(r)

        _runs_load(dys_ref, dyp_ref, run)
        _runs_load(us_ref, up_ref, run)
        dyb = dyp_ref[...].astype(MXU)
        gr_ref[...] = _dot(dyb, cr_ref[0])
        gi_ref[...] = -_dot(dyb, ci_ref[0])
        ar = jnp.broadcast_to(ar_ref[...], (8, w))
        ai = jnp.broadcast_to(-ai_ref[...], (8, w))

        def advance(kk, state):
            r0 = pl.multiple_of((run - 1 - kk) * 8, 8)
            gr, gi = state
            return (ar * gr - ai * gi + gr_ref[pl.ds(r0, 8), :], ar * gi + ai * gr + gi_ref[pl.ds(r0, 8), :])

        def emit(kk, state):
            r0 = pl.multiple_of((run - 1 - kk) * 8, 8)
            gr, gi = advance(kk, state)
            gr_ref[pl.ds(r0, 8), :] = gr
            gi_ref[pl.ds(r0, 8), :] = gi
            return gr, gi

        zero = jnp.zeros((8, w), F32)
        er, ei = lax.fori_loop(0, run, advance, (zero, zero))
        cr, ci = car_ref[0:1, :], car_ref[1:2, :]
        tr, ti = _scan_group(er, ei, tab_ref, cr, ci, True)
        r8 = lax.broadcasted_iota(jnp.int32, (8, w), 0)
        start = (jnp.where(r8 == 7, cr, pltpu.roll(tr, 7, 0)), jnp.where(r8 == 7, ci, pltpu.roll(ti, 7, 0)))
        car_ref[0:1, :] = tr[0:1, :]
        car_ref[1:2, :] = ti[0:1, :]
        lax.fori_loop(0, run, emit, start)

        gsr = gr_ref[...]
        gsi = gi_ref[...]
        sr = str_ref[...]
        si = sti_ref[...]
        first = ri == 0

        def previous(s, halo_ref):
            head = jnp.where(r8 == 0, jnp.where(first, 0.0, halo_ref[7:8, :]), pltpu.roll(s[tm - 8:tm, :], 1, 0))
            return jnp.concatenate([head, s[0:tm - 8, :]], axis=0)

        spr = previous(sr, hr_ref)
        spi = previous(si, hi_ref)
        dab_ref[0, 0:1, :] += _rowsum(gsr * spr + gsi * spi)
        dab_ref[0, 1:2, :] += _rowsum(gsi * spr - gsr * spi)

        gbr = gsr.astype(MXU)
        gbi = gsi.astype(MXU)
        _runs_store(_dot_nt(gbr, br_ref[0]) + _dot_nt(gbi, bi_ref[0]), dun_ref, run)
        dys_v = dys_ref[...]
        dus_ref[...] = (dun_ref[...] + d_ref[...] * dys_v).astype(MXU)
        dd_ref[0, 0:1, :] += _rowsum(dys_v * us_ref[...])
        ub = up_ref[...].astype(MXU)
        dbr_ref[0] += _dot_tn(ub, gbr)
        dbi_ref[0] += _dot_tn(ub, gbi)
        dcr_ref[0] += _dot_tn(dyb, sr.astype(MXU))
        dci_ref[0] -= _dot_tn(dyb, si.astype(MXU))

    blk = lambda: pl.BlockSpec((1, 8 * SSM_H, w), lambda j, i: (j, 0, 0))
    rowl = lambda: pl.BlockSpec((tm, LANES), lambda j, i: (nt - 1 - i, j))
    roww = lambda: pl.BlockSpec((tm, w), lambda j, i: (nt - 1 - i, j))
    halo = lambda: pl.BlockSpec((8, w), lambda j, i: (jnp.maximum((nt - 1 - i) * hb - 1, 0), j))
    return pl.pallas_call(
        body, name="s5_bwd", grid=(SSM_BLK, nt),
        in_specs=[rowl(), rowl(), roww(), roww(), halo(), halo(),
                  pl.BlockSpec((1, w), lambda j, i: (0, j)), pl.BlockSpec((1, w), lambda j, i: (0, j)),
                  blk(), blk(), blk(), blk(),
                  pl.BlockSpec((1, LANES), lambda j, i: (0, j))],
        out_specs=[rowl(),
                   pl.BlockSpec((1, 8, w), lambda j, i: (j, 0, 0)), pl.BlockSpec((1, 8, LANES), lambda j, i: (j, 0, 0)),
                   blk(), blk(), blk(), blk()],
        out_shape=[_sds((S, SSM_W), MXU), _sds((SSM_BLK, 8, w)), _sds((SSM_BLK, 8, LANES)),
                   _sds((SSM_BLK, 8 * SSM_H, w)), _sds((SSM_BLK, 8 * SSM_H, w)),
                   _sds((SSM_BLK, 8 * SSM_H, w)), _sds((SSM_BLK, 8 * SSM_H, w))],
        scratch_shapes=[pltpu.VMEM((8, 8, w), F32), pltpu.VMEM((8, w), F32),
                        pltpu.VMEM((tm, w), F32), pltpu.VMEM((tm, w), F32),
                        pltpu.VMEM((tm, LANES), F32), pltpu.VMEM((tm, LANES), F32), pltpu.VMEM((tm, LANES), F32)],
        compiler_params=_cp("parallel", "arbitrary"),
    )(*_in_hbm([dys, us, st_re, st_im, st_re, st_im, abar_re, abar_im, b_re, b_im, c_re, c_im, d_skip]))


def _in_bwd(dus, duv, dgl, dx1, x, g_mix, w_in, tm):
    S = x.shape[0]

    def body(dus_ref, duv_ref, dgl_ref, dx1_ref, x_ref, g_ref, w_ref, gx_ref, dg_ref):
        @pl.when(pl.program_id(0) == 0)
        def _():
            dg_ref[...] = jnp.zeros_like(dg_ref)

        dh = (_dot(dus_ref[...], w_ref[0:SSM_W, :])
              + _dot(duv_ref[...], w_ref[SSM_W:SSM_W + 2 * SGU_W, :])
              + _dot(dgl_ref[...], w_ref[SSM_W + 2 * SGU_W:, :]))
        xv = x_ref[...]
        r = _rms(xv)
        xn = xv * r
        dg_ref[...] += _rowsum(dh * xn)
        gx_ref[...] = dx1_ref[...] + _rms_bwd(dh * g_ref[...], xn, r)

    row = lambda n: pl.BlockSpec((tm, n), lambda i: (i, 0))
    return pl.pallas_call(
        body, name="in_bwd", grid=(S // tm,),
        in_specs=[row(SSM_W), row(2 * SGU_W), row(2 * D_MODEL), row(D_MODEL), row(D_MODEL), _full((1, D_MODEL)),
                  _full(w_in.shape)],
        out_specs=[row(D_MODEL), _full((1, D_MODEL))],
        out_shape=[_sds((S, D_MODEL)), _sds((1, D_MODEL))],
        compiler_params=_cp("arbitrary"),
    )(*_in_hbm([dus, duv, dgl, dx1, x, g_mix, w_in]))


def _wgrad_split(a, b, nsplit, tk, name):
    S, K = a.shape
    N = b.shape[1]
    c = N // nsplit

    def body(a_ref, b_ref, o_ref):
        prod = _dot_tn(a_ref[...], b_ref[...])
        for d in range(nsplit):
            o_ref[d] = prod[:, c * d:c * (d + 1)].astype(MXU)

    return pl.pallas_call(
        body, name=name, grid=(K // tk,),
        in_specs=[pl.BlockSpec((S, tk), lambda k: (0, k)), _full((S, N))],
        out_specs=pl.BlockSpec((nsplit, tk, c), lambda k: (0, k, 0)),
        out_shape=_sds((nsplit, K, c), MXU),
        compiler_params=_cp("parallel"),
    )(*_in_hbm([a, b]))


def _wgrad_in_t(dps, h1, name):
    S, K = h1.shape
    cw = 512
    counts = [b.shape[1] // cw for b in dps]
    starts = [sum(counts[:i]) for i in range(len(dps))]
    nblk = sum(counts)

    def body(*refs):
        b_refs = refs[:len(dps)]
        h_ref, o_ref = refs[len(dps)], refs[-1]
        j = pl.program_id(0)
        for b_ref, st, cnt in zip(b_refs, starts, counts):
            @pl.when(jnp.logical_and(j >= st, j < st + cnt))
            def _():
                o_ref[...] = _dot_tn(b_ref[...], h_ref[...]).astype(MXU)

    def src_spec(st, cnt):
        return pl.BlockSpec((S, cw), lambda j: (0, jnp.clip(j - st, 0, cnt - 1)))

    return pl.pallas_call(
        body, name=name, grid=(nblk,),
        in_specs=[src_spec(st, cnt) for st, cnt in zip(starts, counts)] + [_full((S, K))],
        out_specs=pl.BlockSpec((cw, K), lambda j: (j, 0)),
        out_shape=_sds((nblk * cw, K), MXU),
        compiler_params=_cp("arbitrary"),
    )(*_in_hbm([*dps, h1]))


def _wgrad_blk(a3, b3, nblk, a_of, b_of, name):
    S, K = a3.shape[1:]
    N = b3.shape[2]

    def body(a_ref, b_ref, o_ref):
        o_ref[0] = _dot_tn(a_ref[0], b_ref[0]).astype(MXU)

    return pl.pallas_call(
        body, name=name, grid=(nblk,),
        in_specs=[pl.BlockSpec((1, S, K), lambda b: (a_of(b), 0, 0)),
                  pl.BlockSpec((1, S, N), lambda b: (b_of(b), 0, 0))],
        out_specs=pl.BlockSpec((1, K, N), lambda b: (b, 0, 0)),
        out_shape=_sds((nblk, K, N), MXU),
        compiler_params=pltpu.CompilerParams(dimension_semantics=("parallel",), vmem_limit_bytes=WGRAD_VMEM_LIMIT),
    )(*_in_hbm([a3, b3]))


def _assemble_cols(blocks_list, name):
    def body(*refs):
        n = len(blocks_list)
        for b_ref, o_ref in zip(refs[:n], refs[n:]):
            c = b_ref.shape[2]
            for d in range(N_DEV):
                o_ref[:, c * d:c * (d + 1)] = b_ref[d]

    outs = [_sds((b.shape[1], N_DEV * b.shape[2]), b.dtype) for b in blocks_list]
    return pl.pallas_call(
        body, name=name, grid=(1,), in_specs=[_full(b.shape) for b in blocks_list],
        out_specs=[_full(o.shape) for o in outs], out_shape=outs, compiler_params=_cp("arbitrary"),
    )(*_in_hbm(blocks_list))


def _tile(S, want):
    return want if S % want == 0 else S


def _local_step(x, tgt, p, mixer_relay, mixer_weights, ffn_weights, grads_out, small_out):
    S = x.shape[0]
    tm = _tile(S, 256)
    tl = _tile(S, 512)

    rep = lambda a: jnp.repeat(a, SSM_H, axis=0)
    are = rep(p["a_re"])
    aim = rep(p["a_im"])
    ldt = jnp.broadcast_to(rep(p["log_dt"].reshape(SSM_G, 1)), are.shape)
    br_t = p["b_re_t"].reshape(are.shape)
    bi_t = p["b_im_t"].reshape(are.shape)
    abr, abi, bbr, bbi = _s5_params_fwd(are, aim, ldt, br_t, bi_t)
    head = lambda a: a.reshape(SSM_G, SSM_H, SSM_P)[:, 0, :].reshape(1, SSM_G * SSM_P)
    abar_re, abar_im = head(abr), head(abi)
    bd_br = _blockdiag(bbr).astype(MXU)
    bd_bi = _blockdiag(bbi).astype(MXU)
    bd_cr = _blockdiag(p["c_re"].reshape(are.shape)).astype(MXU)
    bd_ci = _blockdiag(p["c_im"].reshape(are.shape)).astype(MXU)
    d_skip = p["d_skip"].reshape(1, SSM_W)

    tril = jnp.tril(jnp.ones((CHUNK, CHUNK), dtype=bool))
    ws = jnp.where(tril[None], p["w_s"], 0.0)
    pair = lambda w: w.reshape(SGU_G // 2, 2, CHUNK, CHUNK).transpose(0, 2, 1, 3).reshape(SGU_G // 2, CHUNK, 2 * CHUNK)
    ws_b = pair(ws).astype(MXU)
    ws_t = pair(ws.transpose(0, 2, 1)).astype(MXU)
    bias_s = jnp.repeat(p["b_s"].T, SGU_D, axis=1)

    g_mix = p["g_mix"].reshape(1, D_MODEL)
    g_ffn = p["g_ffn"].reshape(1, D_MODEL)
    g_final = p["g_final"].reshape(1, D_MODEL)
    g_sgu = p["g_sgu"].reshape(1, SGU_W)
    b_glu = p["b_glu"].reshape(1, SSM_W)
    conv_b = p["conv_b"].reshape(2 * FF_NCB, 1, FF_CW)
    tf = _tile(S, 256)
    ts = _tile(S, 1024)

    h1, us, uv, gl = _in_fwd(x, g_mix, p["w_in_t"], tl)
    token = mixer_relay(us)
    st_re, st_im, ys = _s5_fwd(us, abar_re, abar_im, bd_br, bd_bi, bd_cr, bd_ci, d_skip + token[0:1, 0:1], ts)
    p = dict(p, **mixer_weights(ys))
    yg, yap, sg, ya, yb, m, x1, h2 = _mix_fwd(x, ys, uv, gl, p["w_glu"], b_glu, p["w_proj_a"], g_sgu, ws_b, bias_s,
                                              p["w_proj_b"], p["w_out"], g_ffn, tl)
    w_up, conv_w, w_down = ffn_weights(h2)
    pair_lanes = lambda a: a.reshape(N_DEV // 2, 2, a.shape[1], FF_SHARD).transpose(0, 2, 1, 3).reshape(
        N_DEV // 2, a.shape[1], FF_CW)
    w_up = w_up.reshape(2 * FF_NCB, FF_CW, D_MODEL)
    conv_w = pair_lanes(conv_w)
    up, ab, ff, dx2, dx2b, loss, dg_final = _ffn_fwd(h2, x1, tgt, w_up, conv_w, conv_b, w_down, g_final, tf)

    dup, dx1, dx1b, dconv, dg_ffn = _ffn_bwd(dx2, up, ab, x1, w_up, conv_w, w_down, g_ffn, tf)
    rows8 = lambda g: g.reshape(N_DEV, g.shape[1] // N_DEV, g.shape[2])
    g_up = _wgrad_blk(dup.reshape(2 * FF_NCB, S, FF_CW), h2[None], 2 * FF_NCB, lambda b: b, lambda b: 0,
                      "wgrad_up").reshape(N_DEV, FF_SHARD, D_MODEL)
    g_down = _wgrad_blk(ff, dx2b[None], FF_NCB, lambda b: b, lambda b: 0, "wgrad_down").reshape(
        N_DEV, D_FF // N_DEV, D_MODEL)
    token = grads_out(("w_up", "w_down"), (g_up, g_down))
    dgl, dya, dyb, dz, dys, duv, db_glu, dg_sgu, dws, dbs = _mix_bwd(
        dx1, gl, ya, yb, ys, uv, p["w_out"], p["w_proj_a"], p["w_proj_b"], p["w_glu"], b_glu + token[0:1, 0:1], g_sgu,
        ws_b, ws_t, bias_s, tm)
    token = grads_out(("w_glu", "w_proj_a", "w_proj_b", "w_out"),
                      (rows8(_wgrad_split(yg, dz, 1, SSM_W, "wgrad_glu")),
                       _wgrad_split(yap, dya, N_DEV, SSM_W, "wgrad_pa"),
                       _wgrad_split(sg, dyb, N_DEV, SGU_W, "wgrad_pb"),
                       rows8(_wgrad_split(m, dx1b, 1, 512, "wgrad_out"))))
    dus, dab, dd, dbbr, dbbi, dcr, dci = _s5_bwd(dys, us, st_re, st_im, abar_re, abar_im, bd_br, bd_bi, bd_cr, bd_ci,
                                                 d_skip + token[0:1, 0:1], ts)
    g_in = _wgrad_in_t([dus, duv, dgl], h1, "wgrad_in")
    token = grads_out(("w_in",), (g_in.reshape(N_DEV, g_in.shape[0] // N_DEV, D_MODEL),))
    grad_x, dg_mix = _in_bwd(dus, duv, dgl, dx1, x, g_mix + token[0:1, 0:1], p["w_in_t"], tl)

    spread = lambda v: jnp.repeat(v.reshape(SSM_G, SSM_P), SSM_H, axis=0) * (1.0 / SSM_H)
    dabr = spread(dab[:, 0, :])
    dabi = spread(dab[:, 1, :])
    dare, daim, dldt, dbr_t, dbi_t = _s5_params_bwd(are, aim, ldt, br_t, bi_t, dabr, dabi,
                                                    _unblockdiag(dbbr), _unblockdiag(dbbi))
    fold = lambda a: a.reshape(SSM_G, SSM_H, SSM_P).sum(axis=1)

    grads = {
        "g_mix": dg_mix,
        "a_re": fold(dare), "a_im": fold(daim), "log_dt": fold(dldt).sum(axis=1),
        "b_re": dbr_t, "b_im": dbi_t,
        "c_re": _unblockdiag(dcr).reshape(SSM_G, SSM_H, SSM_P),
        "c_im": _unblockdiag(dci).reshape(SSM_G, SSM_H, SSM_P),
        "d_skip": dd[:, 0, :].reshape(SSM_W),
        "b_glu": db_glu,
        "g_sgu": dg_sgu,
        "w_s": dws,
        "b_s": dbs.reshape(CHUNK, SGU_G, SGU_D).sum(axis=-1).T,
        "g_ffn": dg_ffn,
        "conv_w": dconv[:, 0:3, :].reshape(N_DEV // 2, 3, 2, FF_SHARD).transpose(0, 2, 1, 3).reshape(
            N_DEV, 3, FF_SHARD),
        "conv_b": dconv[:, 3, :].reshape(2 * D_FF),
        "g_final": dg_final,
    }
    small_out(grads, loss)
    return grad_x


_ANY = pl.BlockSpec(memory_space=pl.ANY)
_MESH = pl.DeviceIdType.MESH


def _allgather(shards, dtypes, name, cast_only=(), sum_slots=False):
    n = len(shards)
    e = len(cast_only)
    shapes = [s.shape[1:] if sum_slots else s.shape for s in shards]

    def body(*refs):
        in_refs, extra_in = refs[:n], refs[n:n + e]
        out_refs, extra_out = refs[n + e:2 * n + e], refs[2 * n + e:2 * n + 2 * e]
        stage = refs[2 * n + 2 * e:3 * n + 2 * e]
        send_sems, recv_sems, local_sems = refs[3 * n + 2 * e:]
        for a in range(n):
            if sum_slots:
                total = in_refs[a][0].astype(F32)
                for s in range(1, N_DEV):
                    total = total + in_refs[a][s].astype(F32)
                stage[a][...] = total.astype(dtypes[a])
            else:
                stage[a][...] = in_refs[a][...].astype(dtypes[a])
        for i in range(e):
            extra_out[i][...] = extra_in[i][...].astype(MXU)
        x, y, c = lax.axis_index("x"), lax.axis_index("y"), lax.axis_index("c")
        me, sibling = (x, y, c), (x, y, 1 - c)
        chips = [(1 - x, y), (x, 1 - y), (1 - x, 1 - y)]

        def slot(a, px, py, pc):
            return out_refs[a].at[4 * px + 2 * py + pc]

        def copy(a, k, block, to, src=None):
            return pltpu.make_async_remote_copy(
                src_ref=slot(a, *block) if src is None else src, dst_ref=slot(a, *block),
                send_sem=send_sems.at[a, k], recv_sem=recv_sems.at[a, k], device_id=to, device_id_type=_MESH)

        mine = [pltpu.make_async_copy(stage[a], slot(a, *me), local_sems.at[a]) for a in range(n)]
        for cp in mine:
            cp.start()
        first = []
        for j, chip in enumerate(chips):
            first += [copy(a, 1 + j, me, (*chip, c), src=stage[a]) for a in range(n)]
        first += [copy(a, 0, me, sibling, src=stage[a]) for a in range(n)]
        for cp in first:
            cp.start()
        passed = []
        for j, chip in enumerate(chips):
            for a in range(n):
                copy(a, 1 + j, (*chip, c), me).wait_recv()
                fwd = copy(a, 4 + j, (*chip, c), sibling)
                fwd.start()
                passed.append(fwd)
        for a in range(n):
            copy(a, 0, sibling, me).wait_recv()
        for j, chip in enumerate(chips):
            for a in range(n):
                copy(a, 4 + j, (*chip, 1 - c), me).wait_recv()
        for cp in first + passed:
            cp.wait_send()
        for cp in mine:
            cp.wait()

    res = pl.pallas_call(
        body, name=name, grid=(1,), in_specs=[_full(s.shape) for s in list(shards) + list(cast_only)],
        out_specs=[_ANY] * n + [_full(s.shape) for s in cast_only],
        out_shape=[_sds((N_DEV,) + shp, dt) for shp, dt in zip(shapes, dtypes)]
                  + [_sds(s.shape, MXU) for s in cast_only],
        scratch_shapes=[pltpu.VMEM(shp, dt) for shp, dt in zip(shapes, dtypes)]
                       + [pltpu.SemaphoreType.DMA((n, 7)), pltpu.SemaphoreType.DMA((n, 7)), pltpu.SemaphoreType.DMA((n,))],
        compiler_params=pltpu.CompilerParams(vmem_limit_bytes=VMEM_LIMIT),
    )(*_in_hbm([*shards, *cast_only]))
    return res[:n], res[n:]


_HBM = pl.BlockSpec(memory_space=pltpu.HBM)
_SEM = pl.BlockSpec(memory_space=pltpu.SEMAPHORE)
_EFFECT = pltpu.SideEffectType.DATAFLOW_SIDE_EFFECTING
_PEER_ORDER = (2, 4, 6, 3, 5, 7, 1)


def _peer(k):
    x, y, c = lax.axis_index("x"), lax.axis_index("y"), lax.axis_index("c")
    px = 1 - x if k & 4 else x
    py = 1 - y if k & 2 else y
    pc = 1 - c if k & 1 else c
    return (px, py, pc), 4 * px + 2 * py + pc


_SAME_CORE_AND_SIBLING = (2, 4, 6, 1)


def _push_start(srcs, lands, slotted, name, peers=_PEER_ORDER):
    n = len(srcs)

    def body(*refs):
        src_refs, land_refs = refs[:n], refs[n:2 * n]
        send_sems, recv_sems, token = refs[2 * n], refs[2 * n + 1], refs[-1]
        mine = 4 * lax.axis_index("x") + 2 * lax.axis_index("y") + lax.axis_index("c")
        for k in peers:
            dev, theirs = _peer(k)
            for a in range(n):
                pltpu.make_async_remote_copy(
                    src_ref=src_refs[a].at[theirs] if slotted else src_refs[a], dst_ref=land_refs[a].at[mine],
                    send_sem=send_sems.at[7 * a + k - 1], recv_sem=recv_sems.at[7 * a + k - 1],
                    device_id=dev, device_id_type=_MESH).start()
        token[...] = jnp.zeros_like(token)

    bufs = list(srcs) + list(lands)
    res = pl.pallas_call(
        body, name=name, in_specs=[_HBM] * (2 * n),
        out_specs=(_SEM, _SEM, *[_HBM] * (2 * n), pl.BlockSpec(memory_space=pltpu.VMEM)),
        out_shape=(pltpu.SemaphoreType.DMA((7 * n,)), pltpu.SemaphoreType.DMA((7 * n,)),
                   *[pltpu.HBM(b.shape, b.dtype) for b in bufs], _sds((8, LANES))),
        input_output_aliases={i: 2 + i for i in range(2 * n)},
        compiler_params=pltpu.CompilerParams(has_side_effects=_EFFECT),
    )(*[pltpu.with_memory_space_constraint(b, pltpu.HBM) for b in bufs])
    return res[0], res[1], res[2:2 + n], res[2 + n:2 + 2 * n], res[-1]


def _push_wait(send_sems, recv_sems, srcs, lands, slotted, after, name, peers=_PEER_ORDER):
    n = len(srcs)

    def body(*refs):
        src_refs, land_refs = refs[:n], refs[n:2 * n]
        send_sems, recv_sems = refs[2 * n], refs[2 * n + 1]
        for k in peers:
            dev, theirs = _peer(k)
            for a in range(n):
                cp = pltpu.make_async_remote_copy(
                    src_ref=src_refs[a].at[theirs] if slotted else src_refs[a], dst_ref=land_refs[a].at[theirs],
                    send_sem=send_sems.at[7 * a + k - 1], recv_sem=recv_sems.at[7 * a + k - 1],
                    device_id=dev, device_id_type=_MESH)
                cp.wait_send()
                cp.wait_recv()

    bufs = list(srcs) + list(lands)
    res = pl.pallas_call(
        body, name=name, in_specs=[_HBM] * (2 * n) + [_SEM, _SEM] + [_ANY] * len(after), out_specs=[_HBM] * (2 * n),
        out_shape=[pltpu.HBM(b.shape, b.dtype) for b in bufs],
        input_output_aliases={i: i for i in range(2 * n)},
        compiler_params=pltpu.CompilerParams(has_side_effects=_EFFECT),
    )(*bufs, send_sems, recv_sems, *after)
    return res[n:]


def _other_chips():
    x, y = lax.axis_index("x"), lax.axis_index("y")
    return ((1 - x, y), (x, 1 - y), (1 - x, 1 - y))


def _relay_start(lands, name):
    n = len(lands)

    def body(*refs):
        land_refs = refs[:n]
        send_sems, recv_sems, token = refs[n], refs[n + 1], refs[-1]
        x, y, c = lax.axis_index("x"), lax.axis_index("y"), lax.axis_index("c")
        for j, (px, py) in enumerate(_other_chips()):
            slot = 4 * px + 2 * py + c
            for a in range(n):
                pltpu.make_async_remote_copy(
                    src_ref=land_refs[a].at[slot], dst_ref=land_refs[a].at[slot],
                    send_sem=send_sems.at[3 * a + j], recv_sem=recv_sems.at[3 * a + j],
                    device_id=(x, y, 1 - c), device_id_type=_MESH).start()
        token[...] = jnp.zeros_like(token)

    res = pl.pallas_call(
        body, name=name, in_specs=[_HBM] * n,
        out_specs=(_SEM, _SEM, *[_HBM] * n, pl.BlockSpec(memory_space=pltpu.VMEM)),
        out_shape=(pltpu.SemaphoreType.DMA((3 * n,)), pltpu.SemaphoreType.DMA((3 * n,)),
                   *[pltpu.HBM(b.shape, b.dtype) for b in lands], _sds((8, LANES))),
        input_output_aliases={i: 2 + i for i in range(n)},
        compiler_params=pltpu.CompilerParams(has_side_effects=_EFFECT),
    )(*[pltpu.with_memory_space_constraint(b, pltpu.HBM) for b in lands])
    return res[0], res[1], res[2:2 + n], res[-1]


def _relay_wait(send_sems, recv_sems, lands, after, name):
    n = len(lands)

    def body(*refs):
        land_refs = refs[:n]
        send_sems, recv_sems = refs[n], refs[n + 1]
        x, y, c = lax.axis_index("x"), lax.axis_index("y"), lax.axis_index("c")
        for j, (px, py) in enumerate(_other_chips()):
            sent, received = 4 * px + 2 * py + c, 4 * px + 2 * py + (1 - c)
            for a in range(n):
                cp = pltpu.make_async_remote_copy(
                    src_ref=land_refs[a].at[sent], dst_ref=land_refs[a].at[received],
                    send_sem=send_sems.at[3 * a + j], recv_sem=recv_sems.at[3 * a + j],
                    device_id=(x, y, 1 - c), device_id_type=_MESH)
                cp.wait_send()
                cp.wait_recv()

    return pl.pallas_call(
        body, name=name, in_specs=[_HBM] * n + [_SEM, _SEM] + [_ANY] * len(after), out_specs=[_HBM] * n,
        out_shape=[pltpu.HBM(b.shape, b.dtype) for b in lands],
        input_output_aliases={i: i for i in range(n)},
        compiler_params=pltpu.CompilerParams(has_side_effects=_EFFECT),
    )(*lands, send_sems, recv_sems, *after)


def _adamw(w, g, m, v):
    m2 = ADAM_B1 * m + (1.0 - ADAM_B1) * g
    v2 = ADAM_B2 * v + (1.0 - ADAM_B2) * (g * g)
    m_hat = m2 / (1.0 - ADAM_B1 ** ADAM_STEP)
    v_hat = v2 / (1.0 - ADAM_B2 ** ADAM_STEP)
    delta = -ADAM_LR * (m_hat / (jnp.sqrt(v_hat) + ADAM_EPS) + ADAM_WD * w)
    return delta, m2, v2


def _adam_shard(parts, w, m, v, name):
    _, r, c = w.shape
    tr = max(t for t in range(16, 257, 16) if r % t == 0)

    nparts = parts.shape[0]

    def body(p_ref, w_ref, m_ref, v_ref, g_ref, d_ref, m2_ref, v2_ref):
        g = p_ref[0].astype(F32)
        for s in range(1, nparts):
            g = g + p_ref[s].astype(F32)
        g_ref[0] = g
        d_ref[0], m2_ref[0], v2_ref[0] = _adamw(w_ref[0], g, m_ref[0], v_ref[0])

    row = lambda: pl.BlockSpec((1, tr, c), lambda i: (0, i, 0))
    return pl.pallas_call(
        body, name=name, grid=(r // tr,),
        in_specs=[pl.BlockSpec((nparts, tr, c), lambda i: (0, i, 0)), row(), row(), row()],
        out_specs=[row(), row(), row(), row()], out_shape=[_sds((1, r, c))] * 4,
        compiler_params=_cp("parallel"),
    )(*_in_hbm([parts, w, m, v]))


def _adam_small(gs, ws, ms, vs, name):
    n = len(gs)

    def body(*refs):
        ins, outs = refs[:4 * n], refs[4 * n:]
        for i in range(n):
            g = ins[i][...]
            d, m2, v2 = _adamw(ins[n + i][...], g, ins[2 * n + i][...], ins[3 * n + i][...])
            outs[i][...] = d
            outs[n + i][...] = m2
            outs[2 * n + i][...] = v2

    res = pl.pallas_call(
        body, name=name, grid=(1,), in_specs=[_full(w.shape) for w in ws] * 4,
        out_specs=[_full(w.shape) for w in ws] * 3, out_shape=[_sds(w.shape) for w in ws] * 3,
        compiler_params=_cp("arbitrary"),
    )(*_in_hbm([*gs, *ws, *ms, *vs]))
    return res[:n], res[n:2 * n], res[2 * n:]


def _pad_to(a, n, axis):
    extra = n - a.shape[axis]
    if extra == 0:
        return a
    widths = [(0, 0)] * a.ndim
    widths[axis] = (0, extra)
    return jnp.pad(a, widths)


def _ceil_to(n, k):
    return -(-n // k) * k


def _pack_rows(flats, rows_multiple):
    parts = [_pad_to(f, _ceil_to(f.shape[-1], LANES), f.ndim - 1) for f in flats]
    cat = jnp.concatenate(parts, axis=-1)
    total = _ceil_to(cat.shape[-1], LANES * rows_multiple)
    cat = _pad_to(cat, total, cat.ndim - 1)
    return cat.reshape(cat.shape[:-1] + (total // LANES, LANES))


def _unpack_rows(buf, sizes):
    flat = buf.reshape(buf.shape[:-2] + (-1,))
    out, off = [], 0
    for n in sizes:
        out.append(flat[..., off:off + n])
        off += _ceil_to(n, LANES)
    return out


_MIX_BIG = ("w_in", "w_glu", "w_proj_a", "w_proj_b", "w_out")
_BIG = _MIX_BIG + ("w_up", "w_down")
_SMALL = ("g_mix", "a_re", "a_im", "log_dt", "b_re", "b_im", "c_re", "c_im", "d_skip", "b_glu", "g_sgu", "w_s", "b_s",
          "g_ffn", "conv_b", "g_final")
_SMALL_ROWS_MULTIPLE = 8 * N_DEV
_TRANSPOSED = ("w_in", "w_up", "b_re", "b_im")


def _as_2d(a):
    return a.reshape(-1, a.shape[-1]) if a.ndim > 1 else a.reshape(1, -1)


def kernel(x, g_mix, w_in, a_re, a_im, log_dt, b_re, b_im, c_re, c_im, d_skip, w_glu, b_glu, w_proj_a, g_sgu, w_s, b_s, w_proj_b, w_out, g_ffn, w_up, conv_w, conv_b, w_down, g_final, loss_target, m_g_mix, m_w_in, m_a_re, m_a_im, m_log_dt, m_b_re, m_b_im, m_c_re, m_c_im, m_d_skip, m_w_glu, m_b_glu, m_w_proj_a, m_g_sgu, m_w_s, m_b_s, m_w_proj_b, m_w_out, m_g_ffn, m_w_up, m_conv_w, m_conv_b, m_w_down, m_g_final, v_g_mix, v_w_in, v_a_re, v_a_im, v_log_dt, v_b_re, v_b_im, v_c_re, v_c_im, v_d_skip, v_w_glu, v_b_glu, v_w_proj_a, v_g_sgu, v_w_s, v_b_s, v_w_proj_b, v_w_out, v_g_ffn, v_w_up, v_conv_w, v_conv_b, v_w_down, v_g_final):
    args = dict(locals())
    me = 4 * lax.axis_index("x") + 2 * lax.axis_index("y") + lax.axis_index("c")

    def own_slot(buf, block):
        return lax.dynamic_update_slice(buf, block[None], (me,) + (0,) * block.ndim)

    for n in _TRANSPOSED:
        for pre in ("", "m_", "v_"):
            args[pre + n] = jnp.swapaxes(args[pre + n], -1, -2)
    later = ("w_glu", "w_proj_a", "w_proj_b", "w_out", "w_up", "w_down")
    (w_in_g,), casts = _allgather([args["w_in"][0]], [MXU], "allgather_w_in", cast_only=[args[n][0] for n in later])
    sh = dict(zip(later, casts))

    def start_push(srcs, tag, peers):
        lands = [own_slot(lax.empty((N_DEV,) + s.shape, s.dtype), s) for s in srcs]
        send_sems, recv_sems, srcs, lands, token = _push_start(srcs, lands, False, "push_" + tag, peers)
        return (send_sems, recv_sems, srcs, lands), token

    mix_push, token_a = start_push([sh[n] for n in later[:4]], "mixer_weights", _SAME_CORE_AND_SIBLING)
    ffn_push, token_b = start_push([sh["w_up"], sh["w_down"], conv_w[0]], "ffn_weights", _PEER_ORDER)
    p = {n: (args[n][0] if n != "g_final" else args[n]) for n in _SMALL if n not in _TRANSPOSED}
    p.update(w_in_t=w_in_g.reshape(SSM_W + 2 * SGU_W + 2 * D_MODEL, D_MODEL),
             b_re_t=args["b_re"][0], b_im_t=args["b_im"][0])
    p["g_mix"] = p["g_mix"] + (token_a[0:1, 0:1] + token_b[0:1, 0:1])
    relay = {}

    def mixer_relay(after):
        lands = _push_wait(*mix_push, False, [after], "wait_mixer_weights", _SAME_CORE_AND_SIBLING)
        relay["send"], relay["recv"], relay["lands"], token = _relay_start(lands, "relay_mixer_weights")
        return token

    def mixer_weights(after):
        w_glu_g, w_pa_g, w_pb_g, w_out_g = _relay_wait(relay["send"], relay["recv"], relay["lands"], [after],
                                                       "wait_relay_mixer_weights")
        w_pa_full, w_pb_full = _assemble_cols([w_pa_g, w_pb_g], "assemble_cols")
        return dict(w_glu=w_glu_g.reshape(SSM_W, SSM_W), w_proj_a=w_pa_full, w_proj_b=w_pb_full,
                    w_out=w_out_g.reshape(D_MODEL, D_MODEL))

    def ffn_weights(after):
        w_up_g, w_down_g, conv_w_g = _push_wait(*ffn_push, False, [after], "wait_ffn_weights")
        return w_up_g, conv_w_g, w_down_g.reshape(D_FF, D_MODEL)

    pushes = []

    def grads_out(names, sends):
        lands = [own_slot(lax.empty(s.shape, s.dtype), lax.dynamic_index_in_dim(s, me, 0, keepdims=False))
                 for s in sends]
        send_sems, recv_sems, srcs, lands, token = _push_start(list(sends), lands, True, "push_grads_" + names[0])
        pushes.append((names, send_sems, recv_sems, srcs, lands))
        return token


    small_names = _SMALL + ("conv_w", "loss")
    small = {}

    def small_out(grads, loss_part):
        small_g = dict(grads, loss=loss_part[0, 0:1])
        flats = [small_g[n].reshape(-1) for n in small_names]
        small["sizes"] = [f.shape[0] for f in flats]
        g_small = _pack_rows(flats, _SMALL_ROWS_MULTIPLE)
        small["rs8"] = g_small.shape[0] // N_DEV
        return grads_out(("small",), (g_small.reshape(N_DEV, small["rs8"], LANES),))

    grad_x = _local_step(x[0], loss_target[0], p, mixer_relay, mixer_weights, ffn_weights, grads_out, small_out)

    out = {}
    done = [grad_x]
    for names, send_sems, recv_sems, srcs, lands in pushes:
        parts = _push_wait(send_sems, recv_sems, srcs, lands, True, done, "wait_grads_" + names[0])
        if names == ("small",):
            g_small_all = _allgather([parts[0]], [F32], "allgather_small", sum_slots=True)[0][0].reshape(
                N_DEV * small["rs8"], LANES)
            pieces = dict(zip(small_names, _unpack_rows(g_small_all, small["sizes"])))
            loss = pieces["loss"][0]
            dconv_w = lax.dynamic_index_in_dim(pieces["conv_w"].reshape(N_DEV, 3, FF_SHARD), me, axis=0, keepdims=False)
            names2 = _SMALL + ("conv_w",)
            gs = [pieces[n].reshape(_as_2d(args[n]).shape) for n in _SMALL] + [dconv_w]
            ds, m2s, v2s = _adam_small(gs, [_as_2d(args[n]) for n in names2], [_as_2d(args["m_" + n]) for n in names2],
                                       [_as_2d(args["v_" + n]) for n in names2], "adam_small")
            for n, res in zip(names2, zip(gs, ds, m2s, v2s)):
                for kind, v in zip(("grad_", "delta_", "new_m_", "new_v_"), res):
                    out[kind + n] = v.reshape(args[n].shape)
            done = [ds[0]]
            continue
        for n, part in zip(names, parts):
            res = _adam_shard(part, args[n], args["m_" + n], args["v_" + n], "adam_" + n)
            for kind, v in zip(("grad_", "delta_", "new_m_", "new_v_"), res):
                out[kind + n] = v
            done = [res[0]]
    order = ("g_mix", "w_in", "a_re", "a_im", "log_dt", "b_re", "b_im", "c_re", "c_im", "d_skip", "w_glu", "b_glu",
             "w_proj_a", "g_sgu", "w_s", "b_s", "w_proj_b", "w_out", "g_ffn", "w_up", "conv_w", "conv_b", "w_down",
             "g_final")
    res = [loss, grad_x.reshape(x.shape)]
    for kind in ("grad_", "delta_", "new_m_", "new_v_"):
        res += [jnp.swapaxes(out[kind + n], -1, -2) if n in _TRANSPOSED else out[kind + n] for n in order]
    return tuple(res)
```

```python
import math

import jax
import jax.numpy as jnp
from jax import lax
from jax.experimental import pallas as pl
from jax.experimental.pallas import tpu as pltpu

F32 = jnp.float32
MXU = jnp.bfloat16
EPS = 1e-6

D_MODEL = 1024
SSM_W = 512
SSM_G, SSM_H, SSM_P = 32, 16, 64
SSM_BLK = 4
SGU_W = 512
SGU_G, SGU_D, CHUNK = 8, 64, 128
D_FF = 2816
N_DEV = 8
FF_SHARD = 2 * D_FF // N_DEV
FF_CW = 2 * FF_SHARD
FF_NCB = D_FF // FF_CW
LANES = 128

ADAM_LR, ADAM_B1, ADAM_B2, ADAM_EPS, ADAM_WD, ADAM_STEP = 0.001, 0.9, 0.999, 1e-08, 0.01, 10

VMEM_LIMIT = 48 * 1024 * 1024
WGRAD_VMEM_LIMIT = 58 * 1024 * 1024
FFN_VMEM_LIMIT = 58 * 1024 * 1024


def _cp(*sem):
    return pltpu.CompilerParams(dimension_semantics=sem, vmem_limit_bytes=VMEM_LIMIT)


def _full(shape):
    n = len(shape)
    return pl.BlockSpec(shape, lambda *_: (0,) * n)


def _sds(shape, dtype=F32):
    return jax.ShapeDtypeStruct(shape, dtype)


def _after(value, *tokens):
    return lax.optimization_barrier((value, *tokens))[0]


def _in_hbm(arrays):
    return [pltpu.with_memory_space_constraint(a, pltpu.HBM) for a in arrays]


def _dot(a, b):
    return jnp.dot(a, b, preferred_element_type=F32)


def _dot_nt(a, b):
    return lax.dot_general(a, b, (((1,), (1,)), ((), ())), preferred_element_type=F32)


def _dot_tn(a, b):
    return lax.dot_general(a, b, (((0,), (0,)), ((), ())), preferred_element_type=F32)


_GELU_C = math.sqrt(2.0 / math.pi)


def _gelu(x):
    return 0.5 * x * (1.0 + jnp.tanh(_GELU_C * (x + 0.044715 * (x * x * x))))


def _gelu_and_grad(x):
    t = jnp.tanh(_GELU_C * (x + 0.044715 * (x * x * x)))
    g = 0.5 * x * (1.0 + t)
    dg = 0.5 * (1.0 + t) + 0.5 * x * (1.0 - t * t) * (_GELU_C * (1.0 + 3.0 * 0.044715 * (x * x)))
    return g, dg


def _sigmoid(x):
    return 0.5 * jnp.tanh(0.5 * x) + 0.5


def _rms(x):
    return lax.rsqrt(jnp.mean(x * x, axis=-1, keepdims=True) + EPS)


def _rms_bwd(dxn, xn, r):
    return r * (dxn - xn * jnp.mean(dxn * xn, axis=-1, keepdims=True))


def _rowsum(x):
    return jnp.sum(x, axis=0, keepdims=True)


def _fetch_once(pairs, sems):
    copies = [pltpu.make_async_copy(src, dst, sems.at[k]) for k, (src, dst) in enumerate(pairs)]
    for cp in copies:
        cp.start()
    for cp in copies:
        cp.wait()


def _s5_disc(are, aim, ldt, br, bi):
    dt = jnp.exp(ldt)
    mag = jnp.exp(dt * are)
    abr = mag * jnp.cos(dt * aim)
    abi = mag * jnp.sin(dt * aim)
    den = are * are + aim * aim
    nr = abr - 1.0
    ni = abi
    fr = (nr * are + ni * aim) / den
    fi = (ni * are - nr * aim) / den
    return abr, abi, fr * br - fi * bi, fr * bi + fi * br


def _s5_params_fwd(are, aim, ldt, br, bi):
    def body(are_ref, aim_ref, ldt_ref, br_ref, bi_ref, o0, o1, o2, o3):
        outs = _s5_disc(are_ref[...], aim_ref[...], ldt_ref[...], br_ref[...], bi_ref[...])
        for o, v in zip((o0, o1, o2, o3), outs):
            o[...] = v
    shp = are.shape
    return pl.pallas_call(body, name="s5_params_fwd", grid=(1,), in_specs=[_full(shp)] * 5, out_specs=[_full(shp)] * 4,
                          out_shape=[_sds(shp)] * 4)(*_in_hbm([are, aim, ldt, br, bi]))


def _s5_params_bwd(are, aim, ldt, br, bi, dabr, dabi, dbr, dbi):
    def body(are_ref, aim_ref, ldt_ref, br_ref, bi_ref, c0, c1, c2, c3, o0, o1, o2, o3, o4):
        prim = (are_ref[...], aim_ref[...], ldt_ref[...], br_ref[...], bi_ref[...])
        _, vjp = jax.vjp(_s5_disc, *prim)
        outs = vjp((c0[...], c1[...], c2[...], c3[...]))
        for o, v in zip((o0, o1, o2, o3, o4), outs):
            o[...] = v
    shp = are.shape
    return pl.pallas_call(body, name="s5_params_bwd", grid=(1,), in_specs=[_full(shp)] * 9, out_specs=[_full(shp)] * 5,
                          out_shape=[_sds(shp)] * 5)(*_in_hbm([are, aim, ldt, br, bi, dabr, dabi, dbr, dbi]))


def _blockdiag(m_t):
    m = m_t.reshape(SSM_BLK, 8, SSM_H, 1, SSM_P)
    eye = jnp.eye(8, dtype=bool).reshape(1, 8, 1, 8, 1)
    return jnp.where(eye, m, jnp.zeros((), m_t.dtype)).reshape(SSM_BLK, 8 * SSM_H, 8 * SSM_P)


def _unblockdiag(pc):
    m = pc.reshape(SSM_BLK, 8, SSM_H, 8, SSM_P)
    return jnp.einsum("jghgp->jghp", m).reshape(SSM_G * SSM_H, SSM_P)


def _in_fwd(x, g_mix, w_in_t, tm):
    S = x.shape[0]

    def body(x_ref, g_ref, w_ref, h_ref, us_ref, uv_ref, gl_ref):
        xv = x_ref[...]
        h = (xv * _rms(xv) * g_ref[...]).astype(MXU)
        h_ref[...] = h
        us_ref[...] = _dot_nt(h, w_ref[0:SSM_W, :])
        uv_ref[...] = _dot_nt(h, w_ref[SSM_W:SSM_W + 2 * SGU_W, :])
        gl_ref[...] = _dot_nt(h, w_ref[SSM_W + 2 * SGU_W:, :])

    row = lambda n: pl.BlockSpec((tm, n), lambda i: (i, 0))
    return pl.pallas_call(
        body, name="in_fwd", grid=(S // tm,),
        in_specs=[row(D_MODEL), _full((1, D_MODEL)), _full(w_in_t.shape)],
        out_specs=[row(D_MODEL), row(SSM_W), row(2 * SGU_W), row(2 * D_MODEL)],
        out_shape=[_sds((S, D_MODEL), MXU), _sds((S, SSM_W)), _sds((S, 2 * SGU_W)), _sds((S, 2 * D_MODEL))],
        compiler_params=_cp("parallel"),
    )(*_in_hbm([x, g_mix, w_in_t]))


def _scan_tables(ar, ai, reverse):
    n = ar.shape[-1]
    def mul(p, q):
        return p[0] * q[0] - p[1] * q[1], p[0] * q[1] + p[1] * q[0]
    a1 = (ar, ai)
    a2 = mul(a1, a1)
    a3 = mul(a2, a1)
    a4 = mul(a2, a2)
    a5 = mul(a4, a1)
    a6 = mul(a4, a2)
    a7 = mul(a4, a3)
    a8 = mul(a4, a4)
    pw = (a1, a2, a3, a4, a5, a6, a7, a8)
    rows = lax.broadcasted_iota(jnp.int32, (8, n), 0)
    tabs = []
    for s, a in ((1, a1), (2, a2), (4, a4)):
        keep = (rows + s <= 7) if reverse else (rows >= s)
        for comp in a:
            tabs.append(jnp.where(keep, jnp.broadcast_to(comp, (8, n)), 0.0))
    for c in range(2):
        q = jnp.zeros((8, n), F32)
        for r in range(8):
            e = (8 - r) if reverse else (r + 1)
            q = jnp.where(rows == r, jnp.broadcast_to(pw[e - 1][c], (8, n)), q)
        tabs.append(q)
    return tabs


def _scan_group(xr, xi, tab_ref, cr, ci, reverse):
    for t, s in enumerate((1, 2, 4)):
        pr = tab_ref[2 * t]
        pi = tab_ref[2 * t + 1]
        sh = (8 - s) if reverse else s
        sr = pltpu.roll(xr, sh, 0)
        si = pltpu.roll(xi, sh, 0)
        xr, xi = xr + pr * sr - pi * si, xi + pr * si + pi * sr
    qr = tab_ref[6]
    qi = tab_ref[7]
    return xr + qr * cr - qi * ci, xi + qr * ci + qi * cr


def _runs_load(src_ref, dst_ref, run):
    for i in range(run):
        dst_ref[8 * i:8 * i + 8, :] = src_ref[pl.ds(i, 8, stride=run), :]


def _runs_store(val, dst_ref, run):
    for i in range(run):
        dst_ref[pl.ds(i, 8, stride=run), :] = val[8 * i:8 * i + 8, :]


def _cpow2(ar, ai, log2n):
    for _ in range(log2n):
        ar, ai = ar * ar - ai * ai, 2.0 * ar * ai
    return ar, ai


def _s5_fwd(us, abar_re, abar_im, b_re, b_im, c_re, c_im, d_skip, tm):
    S = us.shape[0]
    nt = S // tm
    w = 8 * SSM_P
    run = tm // 8
    assert run & (run - 1) == 0

    def body(us_ref, ar_ref, ai_ref, br_ref, bi_ref, cr_ref, ci_ref, d_ref, str_ref, sti_ref, ys_ref,
             tab_ref, car_ref, up_ref):
        i = pl.program_id(1)

        @pl.when(i == 0)
        def _():
            car_ref[...] = jnp.zeros_like(car_ref)
            for k, t in enumerate(_scan_tables(*_cpow2(ar_ref[...], ai_ref[...], run.bit_length() - 1), False)):
                tab_ref[k] = t

        _runs_load(us_ref, up_ref, run)
        ub = up_ref[...].astype(MXU)
        str_ref[...] = _dot(ub, br_ref[0])
        sti_ref[...] = _dot(ub, bi_ref[0])
        ar = jnp.broadcast_to(ar_ref[...], (8, w))
        ai = jnp.broadcast_to(ai_ref[...], (8, w))

        def advance(k, state):
            r0 = pl.multiple_of(k * 8, 8)
            sr, si = state
            return (ar * sr - ai * si + str_ref[pl.ds(r0, 8), :], ar * si + ai * sr + sti_ref[pl.ds(r0, 8), :])

        def emit(k, state):
            r0 = pl.multiple_of(k * 8, 8)
            sr, si = advance(k, state)
            str_ref[pl.ds(r0, 8), :] = sr
            sti_ref[pl.ds(r0, 8), :] = si
            return sr, si

        zero = jnp.zeros((8, w), F32)
        er, ei = lax.fori_loop(0, run, advance, (zero, zero))
        cr, ci = car_ref[0:1, :], car_ref[1:2, :]
        tr, ti = _scan_group(er, ei, tab_ref, cr, ci, False)
        r8 = lax.broadcasted_iota(jnp.int32, (8, w), 0)
        start = (jnp.where(r8 == 0, cr, pltpu.roll(tr, 1, 0)), jnp.where(r8 == 0, ci, pltpu.roll(ti, 1, 0)))
        car_ref[0:1, :] = tr[7:8, :]
        car_ref[1:2, :] = ti[7:8, :]
        lax.fori_loop(0, run, emit, start)
        y = _dot_nt(str_ref[...].astype(MXU), cr_ref[0]) - _dot_nt(sti_ref[...].astype(MXU), ci_ref[0])
        _runs_store(y, ys_ref, run)
        ys_ref[...] += d_ref[...] * us_ref[...]

    blk = lambda: pl.BlockSpec((1, 8 * SSM_H, w), lambda j, i: (j, 0, 0))
    return pl.pallas_call(
        body, name="s5_fwd", grid=(SSM_BLK, nt),
        in_specs=[pl.BlockSpec((tm, LANES), lambda j, i: (i, j)),
                  pl.BlockSpec((1, w), lambda j, i: (0, j)), pl.BlockSpec((1, w), lambda j, i: (0, j)),
                  blk(), blk(), blk(), blk(),
                  pl.BlockSpec((1, LANES), lambda j, i: (0, j))],
        out_specs=[pl.BlockSpec((tm, w), lambda j, i: (i, j)), pl.BlockSpec((tm, w), lambda j, i: (i, j)),
                   pl.BlockSpec((tm, LANES), lambda j, i: (i, j))],
        out_shape=[_sds((S, SSM_BLK * w)), _sds((S, SSM_BLK * w)), _sds((S, SSM_W))],
        scratch_shapes=[pltpu.VMEM((8, 8, w), F32), pltpu.VMEM((8, w), F32), pltpu.VMEM((tm, LANES), F32)],
        compiler_params=_cp("parallel", "arbitrary"),
    )(*_in_hbm([us, abar_re, abar_im, b_re, b_im, c_re, c_im, d_skip]))


def _group_halves(vp):
    first = lax.broadcasted_iota(jnp.int32, vp.shape, 1) < SGU_D
    zero = jnp.zeros((), vp.dtype)
    return jnp.where(first, vp, zero), jnp.where(first, zero, vp)


def _sgu_mix(vnb, wcat_ref):
    outs = []
    for q in range(SGU_G // 2):
        lo, hi = _group_halves(vnb[:, LANES * q:LANES * (q + 1)])
        outs.append(_dot(wcat_ref[q], jnp.concatenate([lo, hi], axis=0)))
    return jnp.concatenate(outs, axis=1)


def _mix_fwd(x, ys, uv, gl, w_glu, b_glu, w_pa, g_sgu, ws, bias_s, w_pb, w_out, g_ffn, tm):
    S = x.shape[0]

    def body(x_ref, ys_ref, uv_ref, gl_ref, wglu_ref, bglu_ref, wpa_ref, gs_ref, ws_ref, bias_ref, wpb_ref, wout_ref,
             gf_ref, yg_ref, yap_ref, sg_ref, ya_ref, yb_ref, m_ref, x1_ref, h2_ref):
        yg = _gelu(ys_ref[...])
        ygb = yg.astype(MXU)
        yg_ref[...] = ygb
        z = _dot(ygb, wglu_ref[...]) + bglu_ref[...]
        yapb = (yg * _sigmoid(z)).astype(MXU)
        yap_ref[...] = yapb
        ya = _dot(yapb, wpa_ref[...])
        ya_ref[...] = ya

        uvg = _gelu(uv_ref[...])
        u2 = uvg[:, :SGU_W]
        v2 = uvg[:, SGU_W:]
        vnb = (v2 * _rms(v2) * gs_ref[...]).astype(MXU)
        for c in range(tm // CHUNK):
            rs = slice(c * CHUNK, (c + 1) * CHUNK)
            mixed = _sgu_mix(vnb[rs], ws_ref) + bias_ref[...]
            sg_ref[rs, :] = (u2[rs] * mixed).astype(MXU)
        yb = _dot(sg_ref[...], wpb_ref[...])
        yb_ref[...] = yb

        glv = gl_ref[...]
        m = _sigmoid(glv[:, :D_MODEL]) * ya + _sigmoid(glv[:, D_MODEL:]) * yb
        mb = m.astype(MXU)
        m_ref[...] = mb
        x1 = x_ref[...] + _dot(mb, wout_ref[...])
        x1_ref[...] = x1
        h2_ref[...] = (x1 * _rms(x1) * gf_ref[...]).astype(MXU)

    row = lambda n: pl.BlockSpec((tm, n), lambda i: (i, 0))
    return pl.pallas_call(
        body, name="mix_fwd", grid=(S // tm,),
        in_specs=[row(D_MODEL), row(SSM_W), row(2 * SGU_W), row(2 * D_MODEL),
                  _full(w_glu.shape), _full(b_glu.shape), _full(w_pa.shape), _full(g_sgu.shape), _full(ws.shape),
                  _full(bias_s.shape), _full(w_pb.shape), _full(w_out.shape), _full(g_ffn.shape)],
        out_specs=[row(SSM_W), row(SSM_W), row(SGU_W), row(D_MODEL), row(D_MODEL), row(D_MODEL), row(D_MODEL),
                   row(D_MODEL)],
        out_shape=[_sds((S, SSM_W), MXU), _sds((S, SSM_W), MXU), _sds((S, SGU_W), MXU), _sds((S, D_MODEL)),
                   _sds((S, D_MODEL)), _sds((S, D_MODEL), MXU), _sds((S, D_MODEL)), _sds((S, D_MODEL), MXU)],
        compiler_params=_cp("parallel"),
    )(*_in_hbm([x, ys, uv, gl, w_glu, b_glu, w_pa, g_sgu, ws, bias_s, w_pb, w_out, g_ffn]))


def _causal_conv3(u, prev8, cw, cb):
    tm = u.shape[0]
    w0, w1, w2 = cw[0:1], cw[1:2], cw[2:3]
    body = w0 * pltpu.roll(u, 2, 0) + w1 * pltpu.roll(u, 1, 0) + w2 * u + cb
    u8 = u[0:8, :]
    r8 = lax.broadcasted_iota(jnp.int32, u8.shape, 0)
    t1 = prev8[7:8, :]
    t0 = prev8[6:7, :]
    s1 = jnp.where(r8 == 0, t1, pltpu.roll(u8, 1, 0))
    s2 = jnp.where(r8 == 0, t0, jnp.where(r8 == 1, t1, pltpu.roll(u8, 2, 0)))
    first = w0 * s2 + w1 * s1 + w2 * u8 + cb
    return jnp.concatenate([first, body[8:tm, :]], axis=0)


def _causal_conv3_adjoint(d, next8, cw):
    tm = d.shape[0]
    w0, w1, w2 = cw[0:1], cw[1:2], cw[2:3]
    n1 = pltpu.roll(d, tm - 1, 0)
    n2 = pltpu.roll(d, tm - 2, 0)
    body = w2 * d + w1 * n1 + w0 * n2
    d8 = d[tm - 8:tm, :]
    r8 = lax.broadcasted_iota(jnp.int32, d8.shape, 0)
    h0 = next8[0:1, :]
    h1 = next8[1:2, :]
    m1 = jnp.where(r8 == 7, h0, pltpu.roll(d8, 7, 0))
    m2 = jnp.where(r8 == 6, h0, jnp.where(r8 == 7, h1, pltpu.roll(d8, 6, 0)))
    last = w2 * d8 + w1 * m1 + w0 * m2
    out = jnp.concatenate([body[0:tm - 8, :], last], axis=0)
    return out, n1, n2, h0 - d[0:1, :], h1 - d[1:2, :]


def _ffn_fwd(h2, x1, tgt, w_up, conv_w, conv_b, w_down, g_final, tm):
    S = h2.shape[0]
    nt = S // tm
    ncb = FF_NCB

    def body(h2_ref, wup_hbm, cwa_ref, cwb_ref, cba_ref, cbb_ref, wd_hbm, x1_ref, gf_ref, tgt_ref,
             up_ref, ab_ref, ff_ref, dx2_ref, dx2b_ref, loss_ref, dgf_ref, acc_ref, tail_ref, wup_ref, wdn_ref, wsem):
        i = pl.program_id(0)
        cb = pl.program_id(1)

        @pl.when(i == 0)
        def _():
            tail_ref[cb] = jnp.zeros((2, 8, FF_CW), F32)

        @pl.when(jnp.logical_and(i == 0, cb == 0))
        def _():
            loss_ref[...] = jnp.zeros_like(loss_ref)
            dgf_ref[...] = jnp.zeros_like(dgf_ref)
            _fetch_once([(wup_hbm, wup_ref), (wd_hbm, wdn_ref)], wsem)

        h2v = h2_ref[...]
        ua = _dot_nt(h2v, wup_ref[cb])
        ub = _dot_nt(h2v, wup_ref[ncb + cb])
        up_ref[0, 0] = ua.astype(MXU)
        up_ref[1, 0] = ub.astype(MXU)
        a = _causal_conv3(ua, tail_ref[cb, 0], cwa_ref[0], cba_ref[0])
        b = _causal_conv3(ub, tail_ref[cb, 1], cwb_ref[0], cbb_ref[0])
        tail_ref[cb, 0] = ua[tm - 8:tm, :]
        tail_ref[cb, 1] = ub[tm - 8:tm, :]
        ab_ref[0, 0] = a
        ab_ref[1, 0] = b
        ffb = (a * _sigmoid(a) * b).astype(MXU)
        ff_ref[0] = ffb
        contrib = _dot(ffb, wdn_ref[pl.ds(pl.multiple_of(cb * FF_CW, FF_CW), FF_CW), :])

        @pl.when(cb == 0)
        def _():
            acc_ref[...] = contrib

        @pl.when(cb > 0)
        def _():
            acc_ref[...] += contrib

        @pl.when(cb == ncb - 1)
        def _():
            x2 = x1_ref[...] + acc_ref[...]
            r = _rms(x2)
            xn = x2 * r
            g = gf_ref[...]
            diff = xn * g - tgt_ref[...]
            loss_ref[...] += (0.5 / D_MODEL) * jnp.sum(diff * diff)
            dy = diff * (1.0 / D_MODEL)
            dgf_ref[...] += _rowsum(dy * xn)
            dx2 = _rms_bwd(dy * g, xn, r)
            dx2_ref[...] = dx2
            dx2b_ref[...] = dx2.astype(MXU)

    row = lambda n: pl.BlockSpec((tm, n), lambda i, c: (i, 0))
    gate = lambda r: pl.BlockSpec((1, r, FF_CW), lambda i, c: (c, 0, 0))
    lin = lambda r: pl.BlockSpec((1, r, FF_CW), lambda i, c: (ncb + c, 0, 0))
    return pl.pallas_call(
        body, name="ffn_fwd", grid=(nt, ncb),
        in_specs=[row(D_MODEL), _ANY, gate(3), lin(3), gate(1), lin(1), _ANY,
                  row(D_MODEL), _full((1, D_MODEL)), row(D_MODEL)],
        out_specs=[pl.BlockSpec((2, 1, tm, FF_CW), lambda i, c: (0, c, i, 0)),
                   pl.BlockSpec((2, 1, tm, FF_CW), lambda i, c: (0, c, i, 0)),
                   pl.BlockSpec((1, tm, FF_CW), lambda i, c: (c, i, 0)),
                   row(D_MODEL), row(D_MODEL), _full((1, LANES)), _full((1, D_MODEL))],
        out_shape=[_sds((2, ncb, S, FF_CW), MXU), _sds((2, ncb, S, FF_CW)), _sds((ncb, S, FF_CW), MXU),
                   _sds((S, D_MODEL)), _sds((S, D_MODEL), MXU), _sds((1, LANES)), _sds((1, D_MODEL))],
        scratch_shapes=[pltpu.VMEM((tm, D_MODEL), F32), pltpu.VMEM((ncb, 2, 8, FF_CW), F32),
                        pltpu.VMEM(w_up.shape, w_up.dtype), pltpu.VMEM(w_down.shape, w_down.dtype),
                        pltpu.SemaphoreType.DMA((2,))],
        compiler_params=pltpu.CompilerParams(dimension_semantics=("arbitrary", "arbitrary"),
                                             vmem_limit_bytes=FFN_VMEM_LIMIT),
    )(*_in_hbm([h2, w_up, conv_w, conv_w, conv_b, conv_b, w_down, x1, g_final, tgt]))


def _ffn_bwd(dx2, up, ab, x1, w_up, conv_w, w_down, g_ffn, tm):
    S = dx2.shape[0]
    nt = S // tm
    ncb = FF_NCB

    def body(dx2_ref, up_ref, ab_ref, cwa_ref, cwb_ref, wd_hbm, wup_hbm,
             x1_ref, g_ref, dup_ref, dx1_ref, dx1b_ref, dconv_ref, dg_ref, acc_ref, head_ref, wup_ref, wdn_ref, wsem):
        i = pl.program_id(0)
        cb = pl.program_id(1)

        @pl.when(i == 0)
        def _():
            head_ref[cb] = jnp.zeros((2, 8, FF_CW), F32)
            dconv_ref[cb] = jnp.zeros((8, FF_CW), F32)
            dconv_ref[ncb + cb] = jnp.zeros((8, FF_CW), F32)

        @pl.when(jnp.logical_and(i == 0, cb == 0))
        def _():
            dg_ref[...] = jnp.zeros_like(dg_ref)
            _fetch_once([(wup_hbm, wup_ref), (wd_hbm, wdn_ref)], wsem)

        dff = _dot_nt(dx2_ref[...].astype(MXU), wdn_ref[pl.ds(pl.multiple_of(cb * FF_CW, FF_CW), FF_CW), :])
        a = ab_ref[0, 0]
        b = ab_ref[1, 0]
        sa = _sigmoid(a)
        silu = a * sa
        da = (dff * b) * (sa + silu * (1.0 - sa))
        db = dff * silu
        dps = []
        for half, slot, d, cw_ref in ((0, cb, da, cwa_ref), (1, ncb + cb, db, cwb_ref)):
            dp, n1, n2, fix0, fix1 = _causal_conv3_adjoint(d, head_ref[cb, half], cw_ref[0])
            head_ref[cb, half] = d[0:8, :]
            dpb16 = dp.astype(MXU)
            dup_ref[half, 0] = dpb16
            dps.append(dpb16)
            u = up_ref[half, 0].astype(F32)
            u_last = u[tm - 1:tm, :]
            dconv_ref[slot, 0:1, :] += _rowsum(n2 * u) + fix0 * u[tm - 2:tm - 1, :] + fix1 * u_last
            dconv_ref[slot, 1:2, :] += _rowsum(n1 * u) + fix0 * u_last
            dconv_ref[slot, 2:3, :] += _rowsum(d * u)
            dconv_ref[slot, 3:4, :] += _rowsum(d)
        contrib = _dot(dps[0], wup_ref[cb]) + _dot(dps[1], wup_ref[ncb + cb])

        @pl.when(cb == 0)
        def _():
            acc_ref[...] = contrib

        @pl.when(cb > 0)
        def _():
            acc_ref[...] += contrib

        @pl.when(cb == ncb - 1)
        def _():
            x1v = x1_ref[...]
            r = _rms(x1v)
            xn = x1v * r
            dh2 = acc_ref[...]
            dg_ref[...] += _rowsum(dh2 * xn)
            dx1 = dx2_ref[...] + _rms_bwd(dh2 * g_ref[...], xn, r)
            dx1_ref[...] = dx1
            dx1b_ref[...] = dx1.astype(MXU)

    row = lambda n: pl.BlockSpec((tm, n), lambda i, c: (nt - 1 - i, 0))
    colb = lambda: pl.BlockSpec((2, 1, tm, FF_CW), lambda i, c: (0, c, nt - 1 - i, 0))
    gate = lambda r: pl.BlockSpec((1, r, FF_CW), lambda i, c: (c, 0, 0))
    lin = lambda r: pl.BlockSpec((1, r, FF_CW), lambda i, c: (ncb + c, 0, 0))
    return pl.pallas_call(
        body, name="ffn_bwd", grid=(nt, ncb),
        in_specs=[row(D_MODEL), colb(), colb(), gate(3), lin(3), _ANY, _ANY, row(D_MODEL), _full((1, D_MODEL))],
        out_specs=[colb(), row(D_MODEL), row(D_MODEL), _full((2 * ncb, 8, FF_CW)), _full((1, D_MODEL))],
        out_shape=[_sds((2, ncb, S, FF_CW), MXU), _sds((S, D_MODEL)), _sds((S, D_MODEL), MXU), _sds((2 * ncb, 8, FF_CW)),
                   _sds((1, D_MODEL))],
        scratch_shapes=[pltpu.VMEM((tm, D_MODEL), F32), pltpu.VMEM((ncb, 2, 8, FF_CW), F32),
                        pltpu.VMEM(w_up.shape, w_up.dtype), pltpu.VMEM(w_down.shape, w_down.dtype),
                        pltpu.SemaphoreType.DMA((2,))],
        compiler_params=pltpu.CompilerParams(dimension_semantics=("arbitrary", "arbitrary"),
                                             vmem_limit_bytes=FFN_VMEM_LIMIT),
    )(*_in_hbm([dx2, up, ab, conv_w, conv_w, w_down, w_up, x1, g_ffn]))


def _mix_bwd(dx1, gl, ya, yb, ys, uv, w_out, w_pa, w_pb, w_glu, b_glu, g_sgu, ws, ws_t, bias_s, tm):
    S = dx1.shape[0]

    def body(dx1_ref, gl_ref, ya_ref, yb_ref, ys_ref, uv_ref, wout_ref, wpa_ref, wpb_ref, wglu_ref, bglu_ref, gs_ref,
             ws_ref, wst_ref, bias_ref,
             dgl_ref, dya_ref, dyb_ref, dz_ref, dys_ref, duv_ref, dbglu_ref, dgs_ref, dws_ref, dbs_ref,
             du2_ref, dvn_ref):
        i = pl.program_id(0)

        @pl.when(i == 0)
        def _():
            dbglu_ref[...] = jnp.zeros_like(dbglu_ref)
            dgs_ref[...] = jnp.zeros_like(dgs_ref)
            dws_ref[...] = jnp.zeros_like(dws_ref)
            dbs_ref[...] = jnp.zeros_like(dbs_ref)

        dm = _dot_nt(dx1_ref[...].astype(MXU), wout_ref[...])
        glv = gl_ref[...]
        ga = _sigmoid(glv[:, :D_MODEL])
        gb = _sigmoid(glv[:, D_MODEL:])
        dgl_ref[:, :D_MODEL] = (dm * ya_ref[...] * ga * (1.0 - ga)).astype(MXU)
        dgl_ref[:, D_MODEL:] = (dm * yb_ref[...] * gb * (1.0 - gb)).astype(MXU)
        dyab = (dm * ga).astype(MXU)
        dybb = (dm * gb).astype(MXU)
        dya_ref[...] = dyab
        dyb_ref[...] = dybb

        dyap = _dot_nt(dyab, wpa_ref[...])
        yg, dgelu = _gelu_and_grad(ys_ref[...])
        sz = _sigmoid(_dot(yg.astype(MXU), wglu_ref[...]) + bglu_ref[...])
        dz = dyap * yg * sz * (1.0 - sz)
        dzb = dz.astype(MXU)
        dz_ref[...] = dzb
        dbglu_ref[...] += _rowsum(dz)
        dys_ref[...] = (dyap * sz + _dot_nt(dzb, wglu_ref[...])) * dgelu

        dsg = _dot_nt(dybb, wpb_ref[...])
        uvg, duvg = _gelu_and_grad(uv_ref[...])
        u2 = uvg[:, :SGU_W]
        v2 = uvg[:, SGU_W:]
        rv = _rms(v2)
        vhat = v2 * rv
        gs = gs_ref[...]
        vnb = (vhat * gs).astype(MXU)
        tril = (lax.broadcasted_iota(jnp.int32, (CHUNK, CHUNK), 0)
                >= lax.broadcasted_iota(jnp.int32, (CHUNK, CHUNK), 1))
        for c in range(tm // CHUNK):
            rs = slice(c * CHUNK, (c + 1) * CHUNK)
            vc = vnb[rs]
            mixed = _sgu_mix(vc, ws_ref) + bias_ref[...]
            dsg_c = dsg[rs]
            du2_ref[rs, :] = dsg_c * mixed
            dmx = dsg_c * u2[rs]
            dbs_ref[...] += dmx
            dmb = dmx.astype(MXU)
            dvn_ref[rs, :] = _sgu_mix(dmb, wst_ref)
            for q in range(SGU_G // 2):
                lanes = slice(LANES * q, LANES * (q + 1))
                for j, part in enumerate(_group_halves(dmb[:, lanes])):
                    dws_ref[2 * q + j] += jnp.where(tril, _dot_nt(part, vc[:, lanes]), 0.0)
        dvn = dvn_ref[...]
        dgs_ref[...] += _rowsum(dvn * vhat)
        dv2 = _rms_bwd(dvn * gs, vhat, rv)
        duv_ref[:, :SGU_W] = (du2_ref[...] * duvg[:, :SGU_W]).astype(MXU)
        duv_ref[:, SGU_W:] = (dv2 * duvg[:, SGU_W:]).astype(MXU)

    row = lambda n: pl.BlockSpec((tm, n), lambda i: (i, 0))
    return pl.pallas_call(
        body, name="mix_bwd", grid=(S // tm,),
        in_specs=[row(D_MODEL), row(2 * D_MODEL), row(D_MODEL), row(D_MODEL), row(SSM_W), row(2 * SGU_W),
                  _full(w_out.shape), _full(w_pa.shape), _full(w_pb.shape), _full(w_glu.shape), _full(b_glu.shape),
                  _full(g_sgu.shape), _full(ws.shape), _full(ws_t.shape), _full(bias_s.shape)],
        out_specs=[row(2 * D_MODEL), row(D_MODEL), row(D_MODEL), row(SSM_W), row(SSM_W), row(2 * SGU_W),
                   _full((1, SSM_W)), _full((1, SGU_W)), _full((SGU_G, CHUNK, CHUNK)), _full((CHUNK, SGU_W))],
        out_shape=[_sds((S, 2 * D_MODEL), MXU), _sds((S, D_MODEL), MXU), _sds((S, D_MODEL), MXU), _sds((S, SSM_W), MXU),
                   _sds((S, SSM_W)), _sds((S, 2 * SGU_W), MXU),
                   _sds((1, SSM_W)), _sds((1, SGU_W)), _sds((SGU_G, CHUNK, CHUNK)), _sds((CHUNK, SGU_W))],
        scratch_shapes=[pltpu.VMEM((tm, SGU_W), F32), pltpu.VMEM((tm, SGU_W), F32)],
        compiler_params=_cp("arbitrary"),
    )(*_in_hbm([dx1, gl, ya, yb, ys, uv, w_out, w_pa, w_pb, w_glu, b_glu, g_sgu, ws, ws_t, bias_s]))


def _s5_bwd(dys, us, st_re, st_im, abar_re, abar_im, b_re, b_im, c_re, c_im, d_skip, tm):
    S = us.shape[0]
    nt = S // tm
    w = 8 * SSM_P
    hb = tm // 8
    run = tm // 8
    assert run & (run - 1) == 0

    def body(dys_ref, us_ref, str_ref, sti_ref, hr_ref, hi_ref, ar_ref, ai_ref, br_ref, bi_ref, cr_ref, ci_ref, d_ref,
             dus_ref, dab_ref, dd_ref, dbr_ref, dbi_ref, dcr_ref, dci_ref,
             tab_ref, car_ref, gr_ref, gi_ref, dyp_ref, up_ref, dun_ref):
        i = pl.program_id(1)
        ri = nt - 1 - i

        @pl.when(i == 0)
        def _():
            car_ref[...] = jnp.zeros_like(car_ref)
            for k, t in enumerate(_scan_tables(*_cpow2(ar_ref[...], -ai_ref[...], run.bit_length() - 1), True)):
                tab_ref[k] = t
            for r in (dab_ref, dd_ref, dbr_ref, dbi_ref, dcr_ref, dci_ref):
                r[...] = jnp.zeros_like(r)

        _runs_load(dys_ref, dyp_ref, run)
        _runs_load(us_ref, up_ref, run)
        dyb = dyp_ref[...].astype(MXU)
        gr_ref[...] = _dot(dyb, cr_ref[0])
        gi_ref[...] = -_dot(dyb, ci_ref[0])
        ar = jnp.broadcast_to(ar_ref[...], (8, w))
        ai = jnp.broadcast_to(-ai_ref[...], (8, w))

        def advance(kk, state):
            r0 = pl.multiple_of((run - 1 - kk) * 8, 8)
            gr, gi = state
            return (ar * gr - ai * gi + gr_ref[pl.ds(r0, 8), :], ar * gi + ai * gr + gi_ref[pl.ds(r0, 8), :])

        def emit(kk, state):
            r0 = pl.multiple_of((run - 1 - kk) * 8, 8)
            gr, gi = advance(kk, state)
            gr_ref[pl.ds(r0, 8), :] = gr
            gi_ref[pl.ds(r0, 8), :] = gi
            return gr, gi

        zero = jnp.zeros((8, w), F32)
        er, ei = lax.fori_loop(0, run, advance, (zero, zero))
        cr, ci = car_ref[0:1, :], car_ref[1:2, :]
        tr, ti = _scan_group(er, ei, tab_ref, cr, ci, True)
        r8 = lax.broadcasted_iota(jnp.int32, (8, w), 0)
        start = (jnp.where(r8 == 7, cr, pltpu.roll(tr, 7, 0)), jnp.where(r8 == 7, ci, pltpu.roll(ti, 7, 0)))
        car_ref[0:1, :] = tr[0:1, :]
        car_ref[1:2, :] = ti[0:1, :]
        lax.fori_loop(0, run, emit, start)

        gsr = gr_ref[...]
        gsi = gi_ref[...]
        sr = str_ref[...]
        si = sti_ref[...]
        first = ri == 0

        def previous(s, halo_ref):
            head = jnp.where(r8 == 0, jnp.where(first, 0.0, halo_ref[7:8, :]), pltpu.roll(s[tm - 8:tm, :], 1, 0))
            return jnp.concatenate([head, s[0:tm - 8, :]], axis=0)

        spr = previous(sr, hr_ref)
        spi = previous(si, hi_ref)
        dab_ref[0, 0:1, :] += _rowsum(gsr * spr + gsi * spi)
        dab_ref[0, 1:2, :] += _rowsum(gsi * spr - gsr * spi)

        gbr = gsr.astype(MXU)
        gbi = gsi.astype(MXU)
        _runs_store(_dot_nt(gbr, br_ref[0]) + _dot_nt(gbi, bi_ref[0]), dun_ref, run)
        dys_v = dys_ref[...]
        dus_ref[...] = (dun_ref[...] + d_ref[...] * dys_v).astype(MXU)
        dd_ref[0, 0:1, :] += _rowsum(dys_v * us_ref[...])
        ub = up_ref[...].astype(MXU)
        dbr_ref[0] += _dot_tn(ub, gbr)
        dbi_ref[0] += _dot_tn(ub, gbi)
        dcr_ref[0] += _dot_tn(dyb, sr.astype(MXU))
        dci_ref[0] -= _dot_tn(dyb, si.astype(MXU))

    blk = lambda: pl.BlockSpec((1, 8 * SSM_H, w), lambda j, i: (j, 0, 0))
    rowl = lambda: pl.BlockSpec((tm, LANES), lambda j, i: (nt - 1 - i, j))
    roww = lambda: pl.BlockSpec((tm, w), lambda j, i: (nt - 1 - i, j))
    halo = lambda: pl.BlockSpec((8, w), lambda j, i: (jnp.maximum((nt - 1 - i) * hb - 1, 0), j))
    return pl.pallas_call(
        body, name="s5_bwd", grid=(SSM_BLK, nt),
        in_specs=[rowl(), rowl(), roww(), roww(), halo(), halo(),
                  pl.BlockSpec((1, w), lambda j, i: (0, j)), pl.BlockSpec((1, w), lambda j, i: (0, j)),
                  blk(), blk(), blk(), blk(),
                  pl.BlockSpec((1, LANES), lambda j, i: (0, j))],
        out_specs=[rowl(),
                   pl.BlockSpec((1, 8, w), lambda j, i: (j, 0, 0)), pl.BlockSpec((1, 8, LANES), lambda j, i: (j, 0, 0)),
                   blk(), blk(), blk(), blk()],
        out_shape=[_sds((S, SSM_W), MXU), _sds((SSM_BLK, 8, w)), _sds((SSM_BLK, 8, LANES)),
                   _sds((SSM_BLK, 8 * SSM_H, w)), _sds((SSM_BLK, 8 * SSM_H, w)),
                   _sds((SSM_BLK, 8 * SSM_H, w)), _sds((SSM_BLK, 8 * SSM_H, w))],
        scratch_shapes=[pltpu.VMEM((8, 8, w), F32), pltpu.VMEM((8, w), F32),
                        pltpu.VMEM((tm, w), F32), pltpu.VMEM((tm, w), F32),
                        pltpu.VMEM((tm, LANES), F32), pltpu.VMEM((tm, LANES), F32), pltpu.VMEM((tm, LANES), F32)],
        compiler_params=_cp("parallel", "arbitrary"),
    )(*_in_hbm([dys, us, st_re, st_im, st_re, st_im, abar_re, abar_im, b_re, b_im, c_re, c_im, d_skip]))


def _in_bwd(dus, duv, dgl, dx1, x, g_mix, w_in, tm):
    S = x.shape[0]

    def body(dus_ref, duv_ref, dgl_ref, dx1_ref, x_ref, g_ref, w_ref, gx_ref, dg_ref):
        @pl.when(pl.program_id(0) == 0)
        def _():
            dg_ref[...] = jnp.zeros_like(dg_ref)

        dh = (_dot(dus_ref[...], w_ref[0:SSM_W, :])
              + _dot(duv_ref[...], w_ref[SSM_W:SSM_W + 2 * SGU_W, :])
              + _dot(dgl_ref[...], w_ref[SSM_W + 2 * SGU_W:, :]))
        xv = x_ref[...]
        r = _rms(xv)
        xn = xv * r
        dg_ref[...] += _rowsum(dh * xn)
        gx_ref[...] = dx1_ref[...] + _rms_bwd(dh * g_ref[...], xn, r)

    row = lambda n: pl.BlockSpec((tm, n), lambda i: (i, 0))
    return pl.pallas_call(
        body, name="in_bwd", grid=(S // tm,),
        in_specs=[row(SSM_W), row(2 * SGU_W), row(2 * D_MODEL), row(D_MODEL), row(D_MODEL), _full((1, D_MODEL)),
                  _full(w_in.shape)],
        out_specs=[row(D_MODEL), _full((1, D_MODEL))],
        out_shape=[_sds((S, D_MODEL)), _sds((1, D_MODEL))],
        compiler_params=_cp("arbitrary"),
    )(*_in_hbm([dus, duv, dgl, dx1, x, g_mix, w_in]))


def _wgrad_split(a, b, nsplit, tk, name):
    S, K = a.shape
    N = b.shape[1]
    c = N // nsplit

    def body(a_ref, b_ref, o_ref):
        prod = _dot_tn(a_ref[...], b_ref[...])
        for d in range(nsplit):
            o_ref[d] = prod[:, c * d:c * (d + 1)].astype(MXU)

    return pl.pallas_call(
        body, name=name, grid=(K // tk,),
        in_specs=[pl.BlockSpec((S, tk), lambda k: (0, k)), _full((S, N))],
        out_specs=pl.BlockSpec((nsplit, tk, c), lambda k: (0, k, 0)),
        out_shape=_sds((nsplit, K, c), MXU),
        compiler_params=_cp("parallel"),
    )(*_in_hbm([a, b]))


def _wgrad_in_t(dps, h1, name):
    S, K = h1.shape
    cw = 512
    counts = [b.shape[1] // cw for b in dps]
    starts = [sum(counts[:i]) for i in range(len(dps))]
    nblk = sum(counts)

    def body(*refs):
        b_refs = refs[:len(dps)]
        h_ref, o_ref = refs[len(dps)], refs[-1]
        j = pl.program_id(0)
        for b_ref, st, cnt in zip(b_refs, starts, counts):
            @pl.when(jnp.logical_and(j >= st, j < st + cnt))
            def _():
                o_ref[...] = _dot_tn(b_ref[...], h_ref[...]).astype(MXU)

    def src_spec(st, cnt):
        return pl.BlockSpec((S, cw), lambda j: (0, jnp.clip(j - st, 0, cnt - 1)))

    return pl.pallas_call(
        body, name=name, grid=(nblk,),
        in_specs=[src_spec(st, cnt) for st, cnt in zip(starts, counts)] + [_full((S, K))],
        out_specs=pl.BlockSpec((cw, K), lambda j: (j, 0)),
        out_shape=_sds((nblk * cw, K), MXU),
        compiler_params=_cp("arbitrary"),
    )(*_in_hbm([*dps, h1]))


def _wgrad_blk(a3, b3, nblk, a_of, b_of, name):
    S, K = a3.shape[1:]
    N = b3.shape[2]

    def body(a_ref, b_ref, o_ref):
        o_ref[0] = _dot_tn(a_ref[0], b_ref[0]).astype(MXU)

    return pl.pallas_call(
        body, name=name, grid=(nblk,),
        in_specs=[pl.BlockSpec((1, S, K), lambda b: (a_of(b), 0, 0)),
                  pl.BlockSpec((1, S, N), lambda b: (b_of(b), 0, 0))],
        out_specs=pl.BlockSpec((1, K, N), lambda b: (b, 0, 0)),
        out_shape=_sds((nblk, K, N), MXU),
        compiler_params=pltpu.CompilerParams(dimension_semantics=("parallel",), vmem_limit_bytes=WGRAD_VMEM_LIMIT),
    )(*_in_hbm([a3, b3]))


def _assemble_cols(blocks_list, name):
    def body(*refs):
        n = len(blocks_list)
        for b_ref, o_ref in zip(refs[:n], refs[n:]):
            c = b_ref.shape[2]
            for d in range(N_DEV):
                o_ref[:, c * d:c * (d + 1)] = b_ref[d]

    outs = [_sds((b.shape[1], N_DEV * b.shape[2]), b.dtype) for b in blocks_list]
    return pl.pallas_call(
        body, name=name, grid=(1,), in_specs=[_full(b.shape) for b in blocks_list],
        out_specs=[_full(o.shape) for o in outs], out_shape=outs, compiler_params=_cp("arbitrary"),
    )(*_in_hbm(blocks_list))


def _tile(S, want):
    return want if S % want == 0 else S


def _local_step(x, tgt, p, mixer_relay, mixer_weights, ffn_weights, grads_out, small_out):
    S = x.shape[0]
    tm = _tile(S, 256)
    tl = _tile(S, 512)

    rep = lambda a: jnp.repeat(a, SSM_H, axis=0)
    are = rep(p["a_re"])
    aim = rep(p["a_im"])
    ldt = jnp.broadcast_to(rep(p["log_dt"].reshape(SSM_G, 1)), are.shape)
    br_t = p["b_re_t"].reshape(are.shape)
    bi_t = p["b_im_t"].reshape(are.shape)
    abr, abi, bbr, bbi = _s5_params_fwd(are, aim, ldt, br_t, bi_t)
    head = lambda a: a.reshape(SSM_G, SSM_H, SSM_P)[:, 0, :].reshape(1, SSM_G * SSM_P)
    abar_re, abar_im = head(abr), head(abi)
    bd_br = _blockdiag(bbr).astype(MXU)
    bd_bi = _blockdiag(bbi).astype(MXU)
    bd_cr = _blockdiag(p["c_re"].reshape(are.shape)).astype(MXU)
    bd_ci = _blockdiag(p["c_im"].reshape(are.shape)).astype(MXU)
    d_skip = p["d_skip"].reshape(1, SSM_W)

    tril = jnp.tril(jnp.ones((CHUNK, CHUNK), dtype=bool))
    ws = jnp.where(tril[None], p["w_s"], 0.0)
    pair = lambda w: w.reshape(SGU_G // 2, 2, CHUNK, CHUNK).transpose(0, 2, 1, 3).reshape(SGU_G // 2, CHUNK, 2 * CHUNK)
    ws_b = pair(ws).astype(MXU)
    ws_t = pair(ws.transpose(0, 2, 1)).astype(MXU)
    bias_s = jnp.repeat(p["b_s"].T, SGU_D, axis=1)

    g_mix = p["g_mix"].reshape(1, D_MODEL)
    g_ffn = p["g_ffn"].reshape(1, D_MODEL)
    g_final = p["g_final"].reshape(1, D_MODEL)
    g_sgu = p["g_sgu"].reshape(1, SGU_W)
    b_glu = p["b_glu"].reshape(1, SSM_W)
    conv_b = p["conv_b"].reshape(2 * FF_NCB, 1, FF_CW)
    tf = _tile(S, 256)
    ts = _tile(S, 1024)

    h1, us, uv, gl = _in_fwd(x, g_mix, p["w_in_t"], tl)
    token = mixer_relay(us)
    st_re, st_im, ys = _s5_fwd(us, abar_re, abar_im, bd_br, bd_bi, bd_cr, bd_ci, _after(d_skip, token), ts)
    p = dict(p, **mixer_weights(ys))
    yg, yap, sg, ya, yb, m, x1, h2 = _mix_fwd(x, ys, uv, gl, p["w_glu"], b_glu, p["w_proj_a"], g_sgu, ws_b, bias_s,
                                              p["w_proj_b"], p["w_out"], g_ffn, tl)
    w_up, conv_w, w_down = ffn_weights(h2)
    pair_lanes = lambda a: a.reshape(N_DEV // 2, 2, a.shape[1], FF_SHARD).transpose(0, 2, 1, 3).reshape(
        N_DEV // 2, a.shape[1], FF_CW)
    w_up = w_up.reshape(2 * FF_NCB, FF_CW, D_MODEL)
    conv_w = pair_lanes(conv_w)
    up, ab, ff, dx2, dx2b, loss, dg_final = _ffn_fwd(h2, x1, tgt, w_up, conv_w, conv_b, w_down, g_final, tf)

    dup, dx1, dx1b, dconv, dg_ffn = _ffn_bwd(dx2, up, ab, x1, w_up, conv_w, w_down, g_ffn, tf)
    rows8 = lambda g: g.reshape(N_DEV, g.shape[1] // N_DEV, g.shape[2])
    g_up = _wgrad_blk(dup.reshape(2 * FF_NCB, S, FF_CW), h2[None], 2 * FF_NCB, lambda b: b, lambda b: 0,
                      "wgrad_up").reshape(N_DEV, FF_SHARD, D_MODEL)
    g_down = _wgrad_blk(ff, dx2b[None], FF_NCB, lambda b: b, lambda b: 0, "wgrad_down").reshape(
        N_DEV, D_FF // N_DEV, D_MODEL)
    token = grads_out(("w_up", "w_down"), (g_up, g_down))
    dgl, dya, dyb, dz, dys, duv, db_glu, dg_sgu, dws, dbs = _mix_bwd(
        dx1, gl, ya, yb, ys, uv, p["w_out"], p["w_proj_a"], p["w_proj_b"], p["w_glu"], _after(b_glu, token), g_sgu,
        ws_b, ws_t, bias_s, tm)
    token = grads_out(("w_glu", "w_proj_a", "w_proj_b", "w_out"),
                      (rows8(_wgrad_split(yg, dz, 1, SSM_W, "wgrad_glu")),
                       _wgrad_split(yap, dya, N_DEV, SSM_W, "wgrad_pa"),
                       _wgrad_split(sg, dyb, N_DEV, SGU_W, "wgrad_pb"),
                       rows8(_wgrad_split(m, dx1b, 1, 512, "wgrad_out"))))
    dus, dab, dd, dbbr, dbbi, dcr, dci = _s5_bwd(dys, us, st_re, st_im, abar_re, abar_im, bd_br, bd_bi, bd_cr, bd_ci,
                                                 _after(d_skip, token), ts)
    g_in = _wgrad_in_t([dus, duv, dgl], h1, "wgrad_in")
    token = grads_out(("w_in",), (g_in.reshape(N_DEV, g_in.shape[0] // N_DEV, D_MODEL),))
    grad_x, dg_mix = _in_bwd(dus, duv, dgl, dx1, x, _after(g_mix, token), p["w_in_t"], tl)

    spread = lambda v: jnp.repeat(v.reshape(SSM_G, SSM_P), SSM_H, axis=0) * (1.0 / SSM_H)
    dabr = spread(dab[:, 0, :])
    dabi = spread(dab[:, 1, :])
    dare, daim, dldt, dbr_t, dbi_t = _s5_params_bwd(are, aim, ldt, br_t, bi_t, dabr, dabi,
                                                    _unblockdiag(dbbr), _unblockdiag(dbbi))
    fold = lambda a: a.reshape(SSM_G, SSM_H, SSM_P).sum(axis=1)

    grads = {
        "g_mix": dg_mix,
        "a_re": fold(dare), "a_im": fold(daim), "log_dt": fold(dldt).sum(axis=1),
        "b_re": dbr_t, "b_im": dbi_t,
        "c_re": _unblockdiag(dcr).reshape(SSM_G, SSM_H, SSM_P),
        "c_im": _unblockdiag(dci).reshape(SSM_G, SSM_H, SSM_P),
        "d_skip": dd[:, 0, :].reshape(SSM_W),
        "b_glu": db_glu,
        "g_sgu": dg_sgu,
        "w_s": dws,
        "b_s": dbs.reshape(CHUNK, SGU_G, SGU_D).sum(axis=-1).T,
        "g_ffn": dg_ffn,
        "conv_w": dconv[:, 0:3, :].reshape(N_DEV // 2, 3, 2, FF_SHARD).transpose(0, 2, 1, 3).reshape(
            N_DEV, 3, FF_SHARD),
        "conv_b": dconv[:, 3, :].reshape(2 * D_FF),
        "g_final": dg_final,
    }
    small_out(grads, loss)
    return grad_x


_ANY = pl.BlockSpec(memory_space=pl.ANY)
_MESH = pl.DeviceIdType.MESH


def _allgather(shards, dtypes, name, cast_only=(), sum_slots=False):
    n = len(shards)
    e = len(cast_only)
    shapes = [s.shape[1:] if sum_slots else s.shape for s in shards]

    def body(*refs):
        in_refs, extra_in = refs[:n], refs[n:n + e]
        out_refs, extra_out = refs[n + e:2 * n + e], refs[2 * n + e:2 * n + 2 * e]
        stage = refs[2 * n + 2 * e:3 * n + 2 * e]
        send_sems, recv_sems, local_sems = refs[3 * n + 2 * e:]
        for a in range(n):
            if sum_slots:
                total = in_refs[a][0].astype(F32)
                for s in range(1, N_DEV):
                    total = total + in_refs[a][s].astype(F32)
                stage[a][...] = total.astype(dtypes[a])
            else:
                stage[a][...] = in_refs[a][...].astype(dtypes[a])
        for i in range(e):
            extra_out[i][...] = extra_in[i][...].astype(MXU)
        x, y, c = lax.axis_index("x"), lax.axis_index("y"), lax.axis_index("c")
        me, sibling = (x, y, c), (x, y, 1 - c)
        chips = [(1 - x, y), (x, 1 - y), (1 - x, 1 - y)]

        def slot(a, px, py, pc):
            return out_refs[a].at[4 * px + 2 * py + pc]

        def copy(a, k, block, to, src=None):
            return pltpu.make_async_remote_copy(
                src_ref=slot(a, *block) if src is None else src, dst_ref=slot(a, *block),
                send_sem=send_sems.at[a, k], recv_sem=recv_sems.at[a, k], device_id=to, device_id_type=_MESH)

        mine = [pltpu.make_async_copy(stage[a], slot(a, *me), local_sems.at[a]) for a in range(n)]
        for cp in mine:
            cp.start()
        first = []
        for j, chip in enumerate(chips):
            first += [copy(a, 1 + j, me, (*chip, c), src=stage[a]) for a in range(n)]
        first += [copy(a, 0, me, sibling, src=stage[a]) for a in range(n)]
        for cp in first:
            cp.start()
        passed = []
        for j, chip in enumerate(chips):
            for a in range(n):
                copy(a, 1 + j, (*chip, c), me).wait_recv()
                fwd = copy(a, 4 + j, (*chip, c), sibling)
                fwd.start()
                passed.append(fwd)
        for a in range(n):
            copy(a, 0, sibling, me).wait_recv()
        for j, chip in enumerate(chips):
            for a in range(n):
                copy(a, 4 + j, (*chip, 1 - c), me).wait_recv()
        for cp in first + passed:
            cp.wait_send()
        for cp in mine:
            cp.wait()

    res = pl.pallas_call(
        body, name=name, grid=(1,), in_specs=[_full(s.shape) for s in list(shards) + list(cast_only)],
        out_specs=[_ANY] * n + [_full(s.shape) for s in cast_only],
        out_shape=[_sds((N_DEV,) + shp, dt) for shp, dt in zip(shapes, dtypes)]
                  + [_sds(s.shape, MXU) for s in cast_only],
        scratch_shapes=[pltpu.VMEM(shp, dt) for shp, dt in zip(shapes, dtypes)]
                       + [pltpu.SemaphoreType.DMA((n, 7)), pltpu.SemaphoreType.DMA((n, 7)), pltpu.SemaphoreType.DMA((n,))],
        compiler_params=pltpu.CompilerParams(vmem_limit_bytes=VMEM_LIMIT),
    )(*_in_hbm([*shards, *cast_only]))
    return res[:n], res[n:]


_HBM = pl.BlockSpec(memory_space=pltpu.HBM)
_SEM = pl.BlockSpec(memory_space=pltpu.SEMAPHORE)
_EFFECT = pltpu.SideEffectType.DATAFLOW_SIDE_EFFECTING
_PEER_ORDER = (2, 4, 6, 3, 5, 7, 1)


def _peer(k):
    x, y, c = lax.axis_index("x"), lax.axis_index("y"), lax.axis_index("c")
    px = 1 - x if k & 4 else x
    py = 1 - y if k & 2 else y
    pc = 1 - c if k & 1 else c
    return (px, py, pc), 4 * px + 2 * py + pc


_SAME_CORE_AND_SIBLING = (2, 4, 6, 1)


def _push_start(srcs, lands, slotted, name, peers=_PEER_ORDER):
    n = len(srcs)

    def body(*refs):
        src_refs, land_refs = refs[:n], refs[n:2 * n]
        send_sems, recv_sems, token = refs[2 * n], refs[2 * n + 1], refs[-1]
        mine = 4 * lax.axis_index("x") + 2 * lax.axis_index("y") + lax.axis_index("c")
        for k in peers:
            dev, theirs = _peer(k)
            for a in range(n):
                pltpu.make_async_remote_copy(
                    src_ref=src_refs[a].at[theirs] if slotted else src_refs[a], dst_ref=land_refs[a].at[mine],
                    send_sem=send_sems.at[7 * a + k - 1], recv_sem=recv_sems.at[7 * a + k - 1],
                    device_id=dev, device_id_type=_MESH).start()
        token[...] = jnp.zeros_like(token)

    bufs = list(srcs) + list(lands)
    res = pl.pallas_call(
        body, name=name, in_specs=[_HBM] * (2 * n),
        out_specs=(_SEM, _SEM, *[_HBM] * (2 * n), pl.BlockSpec(memory_space=pltpu.VMEM)),
        out_shape=(pltpu.SemaphoreType.DMA((7 * n,)), pltpu.SemaphoreType.DMA((7 * n,)),
                   *[pltpu.HBM(b.shape, b.dtype) for b in bufs], _sds((8, LANES))),
        input_output_aliases={i: 2 + i for i in range(2 * n)},
        compiler_params=pltpu.CompilerParams(has_side_effects=_EFFECT),
    )(*[pltpu.with_memory_space_constraint(b, pltpu.HBM) for b in bufs])
    return res[0], res[1], res[2:2 + n], res[2 + n:2 + 2 * n], res[-1]


def _push_wait(send_sems, recv_sems, srcs, lands, slotted, after, name, peers=_PEER_ORDER):
    n = len(srcs)

    def body(*refs):
        src_refs, land_refs = refs[:n], refs[n:2 * n]
        send_sems, recv_sems = refs[2 * n], refs[2 * n + 1]
        for k in peers:
            dev, theirs = _peer(k)
            for a in range(n):
                cp = pltpu.make_async_remote_copy(
                    src_ref=src_refs[a].at[theirs] if slotted else src_refs[a], dst_ref=land_refs[a].at[theirs],
                    send_sem=send_sems.at[7 * a + k - 1], recv_sem=recv_sems.at[7 * a + k - 1],
                    device_id=dev, device_id_type=_MESH)
                cp.wait_send()
                cp.wait_recv()

    bufs = list(srcs) + list(lands)
    res = pl.pallas_call(
        body, name=name, in_specs=[_HBM] * (2 * n) + [_SEM, _SEM] + [_ANY] * len(after), out_specs=[_HBM] * (2 * n),
        out_shape=[pltpu.HBM(b.shape, b.dtype) for b in bufs],
        input_output_aliases={i: i for i in range(2 * n)},
        compiler_params=pltpu.CompilerParams(has_side_effects=_EFFECT),
    )(*bufs, send_sems, recv_sems, *after)
    return res[n:]


def _other_chips():
    x, y = lax.axis_index("x"), lax.axis_index("y")
    return ((1 - x, y), (x, 1 - y), (1 - x, 1 - y))


def _relay_start(lands, name):
    n = len(lands)

    def body(*refs):
        land_refs = refs[:n]
        send_sems, recv_sems, token = refs[n], refs[n + 1], refs[-1]
        x, y, c = lax.axis_index("x"), lax.axis_index("y"), lax.axis_index("c")
        for j, (px, py) in enumerate(_other_chips()):
            slot = 4 * px + 2 * py + c
            for a in range(n):
                pltpu.make_async_remote_copy(
                    src_ref=land_refs[a].at[slot], dst_ref=land_refs[a].at[slot],
                    send_sem=send_sems.at[3 * a + j], recv_sem=recv_sems.at[3 * a + j],
                    device_id=(x, y, 1 - c), device_id_type=_MESH).start()
        token[...] = jnp.zeros_like(token)

    res = pl.pallas_call(
        body, name=name, in_specs=[_HBM] * n,
        out_specs=(_SEM, _SEM, *[_HBM] * n, pl.BlockSpec(memory_space=pltpu.VMEM)),
        out_shape=(pltpu.SemaphoreType.DMA((3 * n,)), pltpu.SemaphoreType.DMA((3 * n,)),
                   *[pltpu.HBM(b.shape, b.dtype) for b in lands], _sds((8, LANES))),
        input_output_aliases={i: 2 + i for i in range(n)},
        compiler_params=pltpu.CompilerParams(has_side_effects=_EFFECT),
    )(*[pltpu.with_memory_space_constraint(b, pltpu.HBM) for b in lands])
    return res[0], res[1], res[2:2 + n], res[-1]


def _relay_wait(send_sems, recv_sems, lands, after, name):
    n = len(lands)

    def body(*refs):
        land_refs = refs[:n]
        send_sems, recv_sems = refs[n], refs[n + 1]
        x, y, c = lax.axis_index("x"), lax.axis_index("y"), lax.axis_index("c")
        for j, (px, py) in enumerate(_other_chips()):
            sent, received = 4 * px + 2 * py + c, 4 * px + 2 * py + (1 - c)
            for a in range(n):
                cp = pltpu.make_async_remote_copy(
                    src_ref=land_refs[a].at[sent], dst_ref=land_refs[a].at[received],
                    send_sem=send_sems.at[3 * a + j], recv_sem=recv_sems.at[3 * a + j],
                    device_id=(x, y, 1 - c), device_id_type=_MESH)
                cp.wait_send()
                cp.wait_recv()

    return pl.pallas_call(
        body, name=name, in_specs=[_HBM] * n + [_SEM, _SEM] + [_ANY] * len(after), out_specs=[_HBM] * n,
        out_shape=[pltpu.HBM(b.shape, b.dtype) for b in lands],
        input_output_aliases={i: i for i in range(n)},
        compiler_params=pltpu.CompilerParams(has_side_effects=_EFFECT),
    )(*lands, send_sems, recv_sems, *after)


def _adamw(w, g, m, v):
    m2 = ADAM_B1 * m + (1.0 - ADAM_B1) * g
    v2 = ADAM_B2 * v + (1.0 - ADAM_B2) * (g * g)
    m_hat = m2 / (1.0 - ADAM_B1 ** ADAM_STEP)
    v_hat = v2 / (1.0 - ADAM_B2 ** ADAM_STEP)
    delta = -ADAM_LR * (m_hat / (jnp.sqrt(v_hat) + ADAM_EPS) + ADAM_WD * w)
    return delta, m2, v2


def _adam_shard(parts, w, m, v, name):
    _, r, c = w.shape
    tr = max(t for t in range(16, 257, 16) if r % t == 0)

    nparts = parts.shape[0]

    def body(p_ref, w_ref, m_ref, v_ref, g_ref, d_ref, m2_ref, v2_ref):
        g = p_ref[0].astype(F32)
        for s in range(1, nparts):
            g = g + p_ref[s].astype(F32)
        g_ref[0] = g
        d_ref[0], m2_ref[0], v2_ref[0] = _adamw(w_ref[0], g, m_ref[0], v_ref[0])

    row = lambda: pl.BlockSpec((1, tr, c), lambda i: (0, i, 0))
    return pl.pallas_call(
        body, name=name, grid=(r // tr,),
        in_specs=[pl.BlockSpec((nparts, tr, c), lambda i: (0, i, 0)), row(), row(), row()],
        out_specs=[row(), row(), row(), row()], out_shape=[_sds((1, r, c))] * 4,
        compiler_params=_cp("parallel"),
    )(*_in_hbm([parts, w, m, v]))


def _adam_small(gs, ws, ms, vs, name):
    n = len(gs)

    def body(*refs):
        ins, outs = refs[:4 * n], refs[4 * n:]
        for i in range(n):
            g = ins[i][...]
            d, m2, v2 = _adamw(ins[n + i][...], g, ins[2 * n + i][...], ins[3 * n + i][...])
            outs[i][...] = d
            outs[n + i][...] = m2
            outs[2 * n + i][...] = v2

    res = pl.pallas_call(
        body, name=name, grid=(1,), in_specs=[_full(w.shape) for w in ws] * 4,
        out_specs=[_full(w.shape) for w in ws] * 3, out_shape=[_sds(w.shape) for w in ws] * 3,
        compiler_params=_cp("arbitrary"),
    )(*_in_hbm([*gs, *ws, *ms, *vs]))
    return res[:n], res[n:2 * n], res[2 * n:]


def _pad_to(a, n, axis):
    extra = n - a.shape[axis]
    if extra == 0:
        return a
    widths = [(0, 0)] * a.ndim
    widths[axis] = (0, extra)
    return jnp.pad(a, widths)


def _ceil_to(n, k):
    return -(-n // k) * k


def _pack_rows(flats, rows_multiple):
    parts = [_pad_to(f, _ceil_to(f.shape[-1], LANES), f.ndim - 1) for f in flats]
    cat = jnp.concatenate(parts, axis=-1)
    total = _ceil_to(cat.shape[-1], LANES * rows_multiple)
    cat = _pad_to(cat, total, cat.ndim - 1)
    return cat.reshape(cat.shape[:-1] + (total // LANES, LANES))


def _unpack_rows(buf, sizes):
    flat = buf.reshape(buf.shape[:-2] + (-1,))
    out, off = [], 0
    for n in sizes:
        out.append(flat[..., off:off + n])
        off += _ceil_to(n, LANES)
    return out


_MIX_BIG = ("w_in", "w_glu", "w_proj_a", "w_proj_b", "w_out")
_BIG = _MIX_BIG + ("w_up", "w_down")
_SMALL = ("g_mix", "a_re", "a_im", "log_dt", "b_re", "b_im", "c_re", "c_im", "d_skip", "b_glu", "g_sgu", "w_s", "b_s",
          "g_ffn", "conv_b", "g_final")
_SMALL_ROWS_MULTIPLE = 8 * N_DEV
_TRANSPOSED = ("w_in", "w_up", "b_re", "b_im")


def _as_2d(a):
    return a.reshape(-1, a.shape[-1]) if a.ndim > 1 else a.reshape(1, -1)


def kernel(x, g_mix, w_in, a_re, a_im, log_dt, b_re, b_im, c_re, c_im, d_skip, w_glu, b_glu, w_proj_a, g_sgu, w_s, b_s, w_proj_b, w_out, g_ffn, w_up, conv_w, conv_b, w_down, g_final, loss_target, m_g_mix, m_w_in, m_a_re, m_a_im, m_log_dt, m_b_re, m_b_im, m_c_re, m_c_im, m_d_skip, m_w_glu, m_b_glu, m_w_proj_a, m_g_sgu, m_w_s, m_b_s, m_w_proj_b, m_w_out, m_g_ffn, m_w_up, m_conv_w, m_conv_b, m_w_down, m_g_final, v_g_mix, v_w_in, v_a_re, v_a_im, v_log_dt, v_b_re, v_b_im, v_c_re, v_c_im, v_d_skip, v_w_glu, v_b_glu, v_w_proj_a, v_g_sgu, v_w_s, v_b_s, v_w_proj_b, v_w_out, v_g_ffn, v_w_up, v_conv_w, v_conv_b, v_w_down, v_g_final):
    args = dict(locals())
    me = 4 * lax.axis_index("x") + 2 * lax.axis_index("y") + lax.axis_index("c")

    def own_slot(buf, block):
        return lax.dynamic_update_slice(buf, block[None], (me,) + (0,) * block.ndim)

    for n in _TRANSPOSED:
        for pre in ("", "m_", "v_"):
            args[pre + n] = jnp.swapaxes(args[pre + n], -1, -2)
    later = ("w_glu", "w_proj_a", "w_proj_b", "w_out", "w_up", "w_down")
    (w_in_g,), casts = _allgather([args["w_in"][0]], [MXU], "allgather_w_in", cast_only=[args[n][0] for n in later])
    sh = dict(zip(later, casts))

    def start_push(srcs, tag, peers):
        lands = [own_slot(lax.empty((N_DEV,) + s.shape, s.dtype), s) for s in srcs]
        send_sems, recv_sems, srcs, lands, token = _push_start(srcs, lands, False, "push_" + tag, peers)
        return (send_sems, recv_sems, srcs, lands), token

    mix_push, token_a = start_push([sh[n] for n in later[:4]], "mixer_weights", _SAME_CORE_AND_SIBLING)
    ffn_push, token_b = start_push([sh["w_up"], sh["w_down"], conv_w[0]], "ffn_weights", _PEER_ORDER)
    p = {n: (args[n][0] if n != "g_final" else args[n]) for n in _SMALL if n not in _TRANSPOSED}
    p.update(w_in_t=w_in_g.reshape(SSM_W + 2 * SGU_W + 2 * D_MODEL, D_MODEL),
             b_re_t=args["b_re"][0], b_im_t=args["b_im"][0])
    p["g_mix"] = _after(p["g_mix"], token_a, token_b)
    relay = {}

    def mixer_relay(after):
        lands = _push_wait(*mix_push, False, [after], "wait_mixer_weights", _SAME_CORE_AND_SIBLING)
        relay["send"], relay["recv"], relay["lands"], token = _relay_start(lands, "relay_mixer_weights")
        return token

    def mixer_weights(after):
        w_glu_g, w_pa_g, w_pb_g, w_out_g = _relay_wait(relay["send"], relay["recv"], relay["lands"], [after],
                                                       "wait_relay_mixer_weights")
        w_pa_full, w_pb_full = _assemble_cols([w_pa_g, w_pb_g], "assemble_cols")
        return dict(w_glu=w_glu_g.reshape(SSM_W, SSM_W), w_proj_a=w_pa_full, w_proj_b=w_pb_full,
                    w_out=w_out_g.reshape(D_MODEL, D_MODEL))

    def ffn_weights(after):
        w_up_g, w_down_g, conv_w_g = _push_wait(*ffn_push, False, [after], "wait_ffn_weights")
        return w_up_g, conv_w_g, w_down_g.reshape(D_FF, D_MODEL)

    pushes = []

    def grads_out(names, sends):
        lands = [own_slot(lax.empty(s.shape, s.dtype), lax.dynamic_index_in_dim(s, me, 0, keepdims=False))
                 for s in sends]
        send_sems, recv_sems, srcs, lands, token = _push_start(list(sends), lands, True, "push_grads_" + names[0])
        pushes.append((names, send_sems, recv_sems, srcs, lands))
        return token


    small_names = _SMALL + ("conv_w", "loss")
    small = {}

    def small_out(grads, loss_part):
        small_g = dict(grads, loss=loss_part[0, 0:1])
        flats = [small_g[n].reshape(-1) for n in small_names]
        small["sizes"] = [f.shape[0] for f in flats]
        g_small = _pack_rows(flats, _SMALL_ROWS_MULTIPLE)
        small["rs8"] = g_small.shape[0] // N_DEV
        return grads_out(("small",), (g_small.reshape(N_DEV, small["rs8"], LANES),))

    grad_x = _local_step(x[0], loss_target[0], p, mixer_relay, mixer_weights, ffn_weights, grads_out, small_out)

    out = {}
    done = [grad_x]
    for names, send_sems, recv_sems, srcs, lands in pushes:
        parts = _push_wait(send_sems, recv_sems, srcs, lands, True, done, "wait_grads_" + names[0])
        if names == ("small",):
            g_small_all = _allgather([parts[0]], [F32], "allgather_small", sum_slots=True)[0][0].reshape(
                N_DEV * small["rs8"], LANES)
            pieces = dict(zip(small_names, _unpack_rows(g_small_all, small["sizes"])))
            loss = pieces["loss"][0]
            dconv_w = lax.dynamic_index_in_dim(pieces["conv_w"].reshape(N_DEV, 3, FF_SHARD), me, axis=0, keepdims=False)
            names2 = _SMALL + ("conv_w",)
            gs = [pieces[n].reshape(_as_2d(args[n]).shape) for n in _SMALL] + [dconv_w]
            ds, m2s, v2s = _adam_small(gs, [_as_2d(args[n]) for n in names2], [_as_2d(args["m_" + n]) for n in names2],
                                       [_as_2d(args["v_" + n]) for n in names2], "adam_small")
            for n, res in zip(names2, zip(gs, ds, m2s, v2s)):
                for kind, v in zip(("grad_", "delta_", "new_m_", "new_v_"), res):
                    out[kind + n] = v.reshape(args[n].shape)
            done = [ds[0]]
            continue
        for n, part in zip(names, parts):
            res = _adam_shard(part, args[n], args["m_" + n], args["v_" + n], "adam_" + n)
            for kind, v in zip(("grad_", "delta_", "new_m_", "new_v_"), res):
                out[kind + n] = v
            done = [res[0]]
    order = ("g_mix", "w_in", "a_re", "a_im", "log_dt", "b_re", "b_im", "c_re", "c_im", "d_skip", "w_glu", "b_glu",
             "w_proj_a", "g_sgu", "w_s", "b_s", "w_proj_b", "w_out", "g_ffn", "w_up", "conv_w", "conv_b", "w_down",
             "g_final")
    res = [loss, grad_x.reshape(x.shape)]
    for kind in ("grad_", "delta_", "new_m_", "new_v_"):
        res += [jnp.swapaxes(out[kind + n], -1, -2) if n in _TRANSPOSED else out[kind + n] for n in order]
    return tuple(res)
```

```python
import math

import jax
import jax.numpy as jnp
from jax import lax
from jax.experimental import pallas as pl
from jax.experimental.pallas import tpu as pltpu

F32 = jnp.float32
MXU = jnp.bfloat16
EPS = 1e-6

D_MODEL = 1024
SSM_W = 512
SSM_G, SSM_H, SSM_P = 32, 16, 64
SSM_BLK = 4
SGU_W = 512
SGU_G, SGU_D, CHUNK = 8, 64, 128
D_FF = 2816
N_DEV = 8
FF_SHARD = 2 * D_FF // N_DEV
FF_CW = 2 * FF_SHARD
FF_NCB = D_FF // FF_CW
LANES = 128

ADAM_LR, ADAM_B1, ADAM_B2, ADAM_EPS, ADAM_WD, ADAM_STEP = 0.001, 0.9, 0.999, 1e-08, 0.01, 10

VMEM_LIMIT = 48 * 1024 * 1024
WGRAD_VMEM_LIMIT = 58 * 1024 * 1024
FFN_VMEM_LIMIT = 58 * 1024 * 1024


def _cp(*sem):
    return pltpu.CompilerParams(dimension_semantics=sem, vmem_limit_bytes=VMEM_LIMIT)


def _full(shape):
    n = len(shape)
    return pl.BlockSpec(shape, lambda *_: (0,) * n)


def _sds(shape, dtype=F32):
    return jax.ShapeDtypeStruct(shape, dtype)


def _in_hbm(arrays):
    return [pltpu.with_memory_space_constraint(a, pltpu.HBM) for a in arrays]


def _behind(body, n_in, after):
    def ordered(*refs):
        body(*refs[:n_in], *refs[n_in + len(after):])
    return ordered


def _dot(a, b):
    return jnp.dot(a, b, preferred_element_type=F32)


def _dot_nt(a, b):
    return lax.dot_general(a, b, (((1,), (1,)), ((), ())), preferred_element_type=F32)


def _dot_tn(a, b):
    return lax.dot_general(a, b, (((0,), (0,)), ((), ())), preferred_element_type=F32)


_GELU_C = math.sqrt(2.0 / math.pi)


def _gelu(x):
    return 0.5 * x * (1.0 + jnp.tanh(_GELU_C * (x + 0.044715 * (x * x * x))))


def _gelu_and_grad(x):
    t = jnp.tanh(_GELU_C * (x + 0.044715 * (x * x * x)))
    g = 0.5 * x * (1.0 + t)
    dg = 0.5 * (1.0 + t) + 0.5 * x * (1.0 - t * t) * (_GELU_C * (1.0 + 3.0 * 0.044715 * (x * x)))
    return g, dg


def _sigmoid(x):
    return 0.5 * jnp.tanh(0.5 * x) + 0.5


def _rms(x):
    return lax.rsqrt(jnp.mean(x * x, axis=-1, keepdims=True) + EPS)


def _rms_bwd(dxn, xn, r):
    return r * (dxn - xn * jnp.mean(dxn * xn, axis=-1, keepdims=True))


def _rowsum(x):
    return jnp.sum(x, axis=0, keepdims=True)


def _fetch_once(pairs, sems):
    copies = [pltpu.make_async_copy(src, dst, sems.at[k]) for k, (src, dst) in enumerate(pairs)]
    for cp in copies:
        cp.start()
    for cp in copies:
        cp.wait()


def _s5_disc(are, aim, ldt, br, bi):
    dt = jnp.exp(ldt)
    mag = jnp.exp(dt * are)
    abr = mag * jnp.cos(dt * aim)
    abi = mag * jnp.sin(dt * aim)
    den = are * are + aim * aim
    nr = abr - 1.0
    ni = abi
    fr = (nr * are + ni * aim) / den
    fi = (ni * are - nr * aim) / den
    return abr, abi, fr * br - fi * bi, fr * bi + fi * br


def _s5_params_fwd(are, aim, ldt, br, bi):
    def body(are_ref, aim_ref, ldt_ref, br_ref, bi_ref, o0, o1, o2, o3):
        outs = _s5_disc(are_ref[...], aim_ref[...], ldt_ref[...], br_ref[...], bi_ref[...])
        for o, v in zip((o0, o1, o2, o3), outs):
            o[...] = v
    shp = are.shape
    return pl.pallas_call(body, name="s5_params_fwd", grid=(1,), in_specs=[_full(shp)] * 5, out_specs=[_full(shp)] * 4,
                          out_shape=[_sds(shp)] * 4)(*_in_hbm([are, aim, ldt, br, bi]))


def _s5_params_bwd(are, aim, ldt, br, bi, dabr, dabi, dbr, dbi):
    def body(are_ref, aim_ref, ldt_ref, br_ref, bi_ref, c0, c1, c2, c3, o0, o1, o2, o3, o4):
        prim = (are_ref[...], aim_ref[...], ldt_ref[...], br_ref[...], bi_ref[...])
        _, vjp = jax.vjp(_s5_disc, *prim)
        outs = vjp((c0[...], c1[...], c2[...], c3[...]))
        for o, v in zip((o0, o1, o2, o3, o4), outs):
            o[...] = v
    shp = are.shape
    return pl.pallas_call(body, name="s5_params_bwd", grid=(1,), in_specs=[_full(shp)] * 9, out_specs=[_full(shp)] * 5,
                          out_shape=[_sds(shp)] * 5)(*_in_hbm([are, aim, ldt, br, bi, dabr, dabi, dbr, dbi]))


def _blockdiag(m_t):
    m = m_t.reshape(SSM_BLK, 8, SSM_H, 1, SSM_P)
    eye = jnp.eye(8, dtype=bool).reshape(1, 8, 1, 8, 1)
    return jnp.where(eye, m, jnp.zeros((), m_t.dtype)).reshape(SSM_BLK, 8 * SSM_H, 8 * SSM_P)


def _unblockdiag(pc):
    m = pc.reshape(SSM_BLK, 8, SSM_H, 8, SSM_P)
    return jnp.einsum("jghgp->jghp", m).reshape(SSM_G * SSM_H, SSM_P)


def _in_fwd(x, g_mix, w_in_t, tm, after=()):
    S = x.shape[0]

    def body(x_ref, g_ref, w_ref, h_ref, us_ref, uv_ref, gl_ref):
        xv = x_ref[...]
        h = (xv * _rms(xv) * g_ref[...]).astype(MXU)
        h_ref[...] = h
        us_ref[...] = _dot_nt(h, w_ref[0:SSM_W, :])
        uv_ref[...] = _dot_nt(h, w_ref[SSM_W:SSM_W + 2 * SGU_W, :])
        gl_ref[...] = _dot_nt(h, w_ref[SSM_W + 2 * SGU_W:, :])

    row = lambda n: pl.BlockSpec((tm, n), lambda i: (i, 0))
    return pl.pallas_call(
        _behind(body, 3, after), name="in_fwd", grid=(S // tm,),
        in_specs=[row(D_MODEL), _full((1, D_MODEL)), _full(w_in_t.shape)] + [_ANY] * len(after),
        out_specs=[row(D_MODEL), row(SSM_W), row(2 * SGU_W), row(2 * D_MODEL)],
        out_shape=[_sds((S, D_MODEL), MXU), _sds((S, SSM_W)), _sds((S, 2 * SGU_W)), _sds((S, 2 * D_MODEL))],
        compiler_params=_cp("parallel"),
    )(*_in_hbm([x, g_mix, w_in_t]), *after)


def _scan_tables(ar, ai, reverse):
    n = ar.shape[-1]
    def mul(p, q):
        return p[0] * q[0] - p[1] * q[1], p[0] * q[1] + p[1] * q[0]
    a1 = (ar, ai)
    a2 = mul(a1, a1)
    a3 = mul(a2, a1)
    a4 = mul(a2, a2)
    a5 = mul(a4, a1)
    a6 = mul(a4, a2)
    a7 = mul(a4, a3)
    a8 = mul(a4, a4)
    pw = (a1, a2, a3, a4, a5, a6, a7, a8)
    rows = lax.broadcasted_iota(jnp.int32, (8, n), 0)
    tabs = []
    for s, a in ((1, a1), (2, a2), (4, a4)):
        keep = (rows + s <= 7) if reverse else (rows >= s)
        for comp in a:
            tabs.append(jnp.where(keep, jnp.broadcast_to(comp, (8, n)), 0.0))
    for c in range(2):
        q = jnp.zeros((8, n), F32)
        for r in range(8):
            e = (8 - r) if reverse else (r + 1)
            q = jnp.where(rows == r, jnp.broadcast_to(pw[e - 1][c], (8, n)), q)
        tabs.append(q)
    return tabs


def _scan_group(xr, xi, tab_ref, cr, ci, reverse):
    for t, s in enumerate((1, 2, 4)):
        pr = tab_ref[2 * t]
        pi = tab_ref[2 * t + 1]
        sh = (8 - s) if reverse else s
        sr = pltpu.roll(xr, sh, 0)
        si = pltpu.roll(xi, sh, 0)
        xr, xi = xr + pr * sr - pi * si, xi + pr * si + pi * sr
    qr = tab_ref[6]
    qi = tab_ref[7]
    return xr + qr * cr - qi * ci, xi + qr * ci + qi * cr


def _runs_load(src_ref, dst_ref, run):
    for i in range(run):
        dst_ref[8 * i:8 * i + 8, :] = src_ref[pl.ds(i, 8, stride=run), :]


def _runs_store(val, dst_ref, run):
    for i in range(run):
        dst_ref[pl.ds(i, 8, stride=run), :] = val[8 * i:8 * i + 8, :]


def _cpow2(ar, ai, log2n):
    for _ in range(log2n):
        ar, ai = ar * ar - ai * ai, 2.0 * ar * ai
    return ar, ai


def _s5_fwd(us, abar_re, abar_im, b_re, b_im, c_re, c_im, d_skip, tm, after=()):
    S = us.shape[0]
    nt = S // tm
    w = 8 * SSM_P
    run = tm // 8
    assert run & (run - 1) == 0

    def body(us_ref, ar_ref, ai_ref, br_ref, bi_ref, cr_ref, ci_ref, d_ref, str_ref, sti_ref, ys_ref,
             tab_ref, car_ref, up_ref):
        i = pl.program_id(1)

        @pl.when(i == 0)
        def _():
            car_ref[...] = jnp.zeros_like(car_ref)
            for k, t in enumerate(_scan_tables(*_cpow2(ar_ref[...], ai_ref[...], run.bit_length() - 1), False)):
                tab_ref[k] = t

        _runs_load(us_ref, up_ref, run)
        ub = up_ref[...].astype(MXU)
        str_ref[...] = _dot(ub, br_ref[0])
        sti_ref[...] = _dot(ub, bi_ref[0])
        ar = jnp.broadcast_to(ar_ref[...], (8, w))
        ai = jnp.broadcast_to(ai_ref[...], (8, w))

        def advance(k, state):
            r0 = pl.multiple_of(k * 8, 8)
            sr, si = state
            return (ar * sr - ai * si + str_ref[pl.ds(r0, 8), :], ar * si + ai * sr + sti_ref[pl.ds(r0, 8), :])

        def emit(k, state):
            r0 = pl.multiple_of(k * 8, 8)
            sr, si = advance(k, state)
            str_ref[pl.ds(r0, 8), :] = sr
            sti_ref[pl.ds(r0, 8), :] = si
            return sr, si

        zero = jnp.zeros((8, w), F32)
        er, ei = lax.fori_loop(0, run, advance, (zero, zero))
        cr, ci = car_ref[0:1, :], car_ref[1:2, :]
        tr, ti = _scan_group(er, ei, tab_ref, cr, ci, False)
        r8 = lax.broadcasted_iota(jnp.int32, (8, w), 0)
        start = (jnp.where(r8 == 0, cr, pltpu.roll(tr, 1, 0)), jnp.where(r8 == 0, ci, pltpu.roll(ti, 1, 0)))
        car_ref[0:1, :] = tr[7:8, :]
        car_ref[1:2, :] = ti[7:8, :]
        lax.fori_loop(0, run, emit, start)
        y = _dot_nt(str_ref[...].astype(MXU), cr_ref[0]) - _dot_nt(sti_ref[...].astype(MXU), ci_ref[0])
        _runs_store(y, ys_ref, run)
        ys_ref[...] += d_ref[...] * us_ref[...]

    blk = lambda: pl.BlockSpec((1, 8 * SSM_H, w), lambda j, i: (j, 0, 0))
    return pl.pallas_call(
        _behind(body, 8, after), name="s5_fwd", grid=(SSM_BLK, nt),
        in_specs=[pl.BlockSpec((tm, LANES), lambda j, i: (i, j)),
                  pl.BlockSpec((1, w), lambda j, i: (0, j)), pl.BlockSpec((1, w), lambda j, i: (0, j)),
                  blk(), blk(), blk(), blk(),
                  pl.BlockSpec((1, LANES), lambda j, i: (0, j))] + [_ANY] * len(after),
        out_specs=[pl.BlockSpec((tm, w), lambda j, i: (i, j)), pl.BlockSpec((tm, w), lambda j, i: (i, j)),
                   pl.BlockSpec((tm, LANES), lambda j, i: (i, j))],
        out_shape=[_sds((S, SSM_BLK * w)), _sds((S, SSM_BLK * w)), _sds((S, SSM_W))],
        scratch_shapes=[pltpu.VMEM((8, 8, w), F32), pltpu.VMEM((8, w), F32), pltpu.VMEM((tm, LANES), F32)],
        compiler_params=_cp("parallel", "arbitrary"),
    )(*_in_hbm([us, abar_re, abar_im, b_re, b_im, c_re, c_im, d_skip]), *after)


def _group_halves(vp):
    first = lax.broadcasted_iota(jnp.int32, vp.shape, 1) < SGU_D
    zero = jnp.zeros((), vp.dtype)
    return jnp.where(first, vp, zero), jnp.where(first, zero, vp)


def _sgu_mix(vnb, wcat_ref):
    outs = []
    for q in range(SGU_G // 2):
        lo, hi = _group_halves(vnb[:, LANES * q:LANES * (q + 1)])
        outs.append(_dot(wcat_ref[q], jnp.concatenate([lo, hi], axis=0)))
    return jnp.concatenate(outs, axis=1)


def _mix_fwd(x, ys, uv, gl, w_glu, b_glu, w_pa, g_sgu, ws, bias_s, w_pb, w_out, g_ffn, tm):
    S = x.shape[0]

    def body(x_ref, ys_ref, uv_ref, gl_ref, wglu_ref, bglu_ref, wpa_ref, gs_ref, ws_ref, bias_ref, wpb_ref, wout_ref,
             gf_ref, yg_ref, yap_ref, sg_ref, ya_ref, yb_ref, m_ref, x1_ref, h2_ref):
        yg = _gelu(ys_ref[...])
        ygb = yg.astype(MXU)
        yg_ref[...] = ygb
        z = _dot(ygb, wglu_ref[...]) + bglu_ref[...]
        yapb = (yg * _sigmoid(z)).astype(MXU)
        yap_ref[...] = yapb
        ya = _dot(yapb, wpa_ref[...])
        ya_ref[...] = ya

        uvg = _gelu(uv_ref[...])
        u2 = uvg[:, :SGU_W]
        v2 = uvg[:, SGU_W:]
        vnb = (v2 * _rms(v2) * gs_ref[...]).astype(MXU)
        for c in range(tm // CHUNK):
            rs = slice(c * CHUNK, (c + 1) * CHUNK)
            mixed = _sgu_mix(vnb[rs], ws_ref) + bias_ref[...]
            sg_ref[rs, :] = (u2[rs] * mixed).astype(MXU)
        yb = _dot(sg_ref[...], wpb_ref[...])
        yb_ref[...] = yb

        glv = gl_ref[...]
        m = _sigmoid(glv[:, :D_MODEL]) * ya + _sigmoid(glv[:, D_MODEL:]) * yb
        mb = m.astype(MXU)
        m_ref[...] = mb
        x1 = x_ref[...] + _dot(mb, wout_ref[...])
        x1_ref[...] = x1
        h2_ref[...] = (x1 * _rms(x1) * gf_ref[...]).astype(MXU)

    row = lambda n: pl.BlockSpec((tm, n), lambda i: (i, 0))
    return pl.pallas_call(
        body, name="mix_fwd", grid=(S // tm,),
        in_specs=[row(D_MODEL), row(SSM_W), row(2 * SGU_W), row(2 * D_MODEL),
                  _full(w_glu.shape), _full(b_glu.shape), _full(w_pa.shape), _full(g_sgu.shape), _full(ws.shape),
                  _full(bias_s.shape), _full(w_pb.shape), _full(w_out.shape), _full(g_ffn.shape)],
        out_specs=[row(SSM_W), row(SSM_W), row(SGU_W), row(D_MODEL), row(D_MODEL), row(D_MODEL), row(D_MODEL),
                   row(D_MODEL)],
        out_shape=[_sds((S, SSM_W), MXU), _sds((S, SSM_W), MXU), _sds((S, SGU_W), MXU), _sds((S, D_MODEL)),
                   _sds((S, D_MODEL)), _sds((S, D_MODEL), MXU), _sds((S, D_MODEL)), _sds((S, D_MODEL), MXU)],
        compiler_params=_cp("parallel"),
    )(*_in_hbm([x, ys, uv, gl, w_glu, b_glu, w_pa, g_sgu, ws, bias_s, w_pb, w_out, g_ffn]))


def _causal_conv3(u, prev8, cw, cb):
    tm = u.shape[0]
    w0, w1, w2 = cw[0:1], cw[1:2], cw[2:3]
    body = w0 * pltpu.roll(u, 2, 0) + w1 * pltpu.roll(u, 1, 0) + w2 * u + cb
    u8 = u[0:8, :]
    r8 = lax.broadcasted_iota(jnp.int32, u8.shape, 0)
    t1 = prev8[7:8, :]
    t0 = prev8[6:7, :]
    s1 = jnp.where(r8 == 0, t1, pltpu.roll(u8, 1, 0))
    s2 = jnp.where(r8 == 0, t0, jnp.where(r8 == 1, t1, pltpu.roll(u8, 2, 0)))
    first = w0 * s2 + w1 * s1 + w2 * u8 + cb
    return jnp.concatenate([first, body[8:tm, :]], axis=0)


def _causal_conv3_adjoint(d, next8, cw):
    tm = d.shape[0]
    w0, w1, w2 = cw[0:1], cw[1:2], cw[2:3]
    n1 = pltpu.roll(d, tm - 1, 0)
    n2 = pltpu.roll(d, tm - 2, 0)
    body = w2 * d + w1 * n1 + w0 * n2
    d8 = d[tm - 8:tm, :]
    r8 = lax.broadcasted_iota(jnp.int32, d8.shape, 0)
    h0 = next8[0:1, :]
    h1 = next8[1:2, :]
    m1 = jnp.where(r8 == 7, h0, pltpu.roll(d8, 7, 0))
    m2 = jnp.where(r8 == 6, h0, jnp.where(r8 == 7, h1, pltpu.roll(d8, 6, 0)))
    last = w2 * d8 + w1 * m1 + w0 * m2
    out = jnp.concatenate([body[0:tm - 8, :], last], axis=0)
    return out, n1, n2, h0 - d[0:1, :], h1 - d[1:2, :]


def _ffn_fwd(h2, x1, tgt, w_up, conv_w, conv_b, w_down, g_final, tm):
    S = h2.shape[0]
    nt = S // tm
    ncb = FF_NCB

    def body(h2_ref, wup_hbm, cwa_ref, cwb_ref, cba_ref, cbb_ref, wd_hbm, x1_ref, gf_ref, tgt_ref,
             up_ref, ab_ref, ff_ref, dx2_ref, dx2b_ref, loss_ref, dgf_ref, acc_ref, tail_ref, wup_ref, wdn_ref, wsem):
        i = pl.program_id(0)
        cb = pl.program_id(1)

        @pl.when(i == 0)
        def _():
            tail_ref[cb] = jnp.zeros((2, 8, FF_CW), F32)

        @pl.when(jnp.logical_and(i == 0, cb == 0))
        def _():
            loss_ref[...] = jnp.zeros_like(loss_ref)
            dgf_ref[...] = jnp.zeros_like(dgf_ref)
            _fetch_once([(wup_hbm, wup_ref), (wd_hbm, wdn_ref)], wsem)

        h2v = h2_ref[...]
        ua = _dot_nt(h2v, wup_ref[cb])
        ub = _dot_nt(h2v, wup_ref[ncb + cb])
        up_ref[0, 0] = ua.astype(MXU)
        up_ref[1, 0] = ub.astype(MXU)
        a = _causal_conv3(ua, tail_ref[cb, 0], cwa_ref[0], cba_ref[0])
        b = _causal_conv3(ub, tail_ref[cb, 1], cwb_ref[0], cbb_ref[0])
        tail_ref[cb, 0] = ua[tm - 8:tm, :]
        tail_ref[cb, 1] = ub[tm - 8:tm, :]
        ab_ref[0, 0] = a
        ab_ref[1, 0] = b
        ffb = (a * _sigmoid(a) * b).astype(MXU)
        ff_ref[0] = ffb
        contrib = _dot(ffb, wdn_ref[pl.ds(pl.multiple_of(cb * FF_CW, FF_CW), FF_CW), :])

        @pl.when(cb == 0)
        def _():
            acc_ref[...] = contrib

        @pl.when(cb > 0)
        def _():
            acc_ref[...] += contrib

        @pl.when(cb == ncb - 1)
        def _():
            x2 = x1_ref[...] + acc_ref[...]
            r = _rms(x2)
            xn = x2 * r
            g = gf_ref[...]
            diff = xn * g - tgt_ref[...]
            loss_ref[...] += (0.5 / D_MODEL) * jnp.sum(diff * diff)
            dy = diff * (1.0 / D_MODEL)
            dgf_ref[...] += _rowsum(dy * xn)
            dx2 = _rms_bwd(dy * g, xn, r)
            dx2_ref[...] = dx2
            dx2b_ref[...] = dx2.astype(MXU)

    row = lambda n: pl.BlockSpec((tm, n), lambda i, c: (i, 0))
    gate = lambda r: pl.BlockSpec((1, r, FF_CW), lambda i, c: (c, 0, 0))
    lin = lambda r: pl.BlockSpec((1, r, FF_CW), lambda i, c: (ncb + c, 0, 0))
    return pl.pallas_call(
        body, name="ffn_fwd", grid=(nt, ncb),
        in_specs=[row(D_MODEL), _ANY, gate(3), lin(3), gate(1), lin(1), _ANY,
                  row(D_MODEL), _full((1, D_MODEL)), row(D_MODEL)],
        out_specs=[pl.BlockSpec((2, 1, tm, FF_CW), lambda i, c: (0, c, i, 0)),
                   pl.BlockSpec((2, 1, tm, FF_CW), lambda i, c: (0, c, i, 0)),
                   pl.BlockSpec((1, tm, FF_CW), lambda i, c: (c, i, 0)),
                   row(D_MODEL), row(D_MODEL), _full((1, LANES)), _full((1, D_MODEL))],
        out_shape=[_sds((2, ncb, S, FF_CW), MXU), _sds((2, ncb, S, FF_CW)), _sds((ncb, S, FF_CW), MXU),
                   _sds((S, D_MODEL)), _sds((S, D_MODEL), MXU), _sds((1, LANES)), _sds((1, D_MODEL))],
        scratch_shapes=[pltpu.VMEM((tm, D_MODEL), F32), pltpu.VMEM((ncb, 2, 8, FF_CW), F32),
                        pltpu.VMEM(w_up.shape, w_up.dtype), pltpu.VMEM(w_down.shape, w_down.dtype),
                        pltpu.SemaphoreType.DMA((2,))],
        compiler_params=pltpu.CompilerParams(dimension_semantics=("arbitrary", "arbitrary"),
                                             vmem_limit_bytes=FFN_VMEM_LIMIT),
    )(*_in_hbm([h2, w_up, conv_w, conv_w, conv_b, conv_b, w_down, x1, g_final, tgt]))


def _ffn_bwd(dx2, up, ab, x1, w_up, conv_w, w_down, g_ffn, tm):
    S = dx2.shape[0]
    nt = S // tm
    ncb = FF_NCB

    def body(dx2_ref, up_ref, ab_ref, cwa_ref, cwb_ref, wd_hbm, wup_hbm,
             x1_ref, g_ref, dup_ref, dx1_ref, dx1b_ref, dconv_ref, dg_ref, acc_ref, head_ref, wup_ref, wdn_ref, wsem):
        i = pl.program_id(0)
        cb = pl.program_id(1)

        @pl.when(i == 0)
        def _():
            head_ref[cb] = jnp.zeros((2, 8, FF_CW), F32)
            dconv_ref[cb] = jnp.zeros((8, FF_CW), F32)
            dconv_ref[ncb + cb] = jnp.zeros((8, FF_CW), F32)

        @pl.when(jnp.logical_and(i == 0, cb == 0))
        def _():
            dg_ref[...] = jnp.zeros_like(dg_ref)
            _fetch_once([(wup_hbm, wup_ref), (wd_hbm, wdn_ref)], wsem)

        dff = _dot_nt(dx2_ref[...].astype(MXU), wdn_ref[pl.ds(pl.multiple_of(cb * FF_CW, FF_CW), FF_CW), :])
        a = ab_ref[0, 0]
        b = ab_ref[1, 0]
        sa = _sigmoid(a)
        silu = a * sa
        da = (dff * b) * (sa + silu * (1.0 - sa))
        db = dff * silu
        dps = []
        for half, slot, d, cw_ref in ((0, cb, da, cwa_ref), (1, ncb + cb, db, cwb_ref)):
            dp, n1, n2, fix0, fix1 = _causal_conv3_adjoint(d, head_ref[cb, half], cw_ref[0])
            head_ref[cb, half] = d[0:8, :]
            dpb16 = dp.astype(MXU)
            dup_ref[half, 0] = dpb16
            dps.append(dpb16)
            u = up_ref[half, 0].astype(F32)
            u_last = u[tm - 1:tm, :]
            dconv_ref[slot, 0:1, :] += _rowsum(n2 * u) + fix0 * u[tm - 2:tm - 1, :] + fix1 * u_last
            dconv_ref[slot, 1:2, :] += _rowsum(n1 * u) + fix0 * u_last
            dconv_ref[slot, 2:3, :] += _rowsum(d * u)
            dconv_ref[slot, 3:4, :] += _rowsum(d)
        contrib = _dot(dps[0], wup_ref[cb]) + _dot(dps[1], wup_ref[ncb + cb])

        @pl.when(cb == 0)
        def _():
            acc_ref[...] = contrib

        @pl.when(cb > 0)
        def _():
            acc_ref[...] += contrib

        @pl.when(cb == ncb - 1)
        def _():
            x1v = x1_ref[...]
            r = _rms(x1v)
            xn = x1v * r
            dh2 = acc_ref[...]
            dg_ref[...] += _rowsum(dh2 * xn)
            dx1 = dx2_ref[...] + _rms_bwd(dh2 * g_ref[...], xn, r)
            dx1_ref[...] = dx1
            dx1b_ref[...] = dx1.astype(MXU)

    row = lambda n: pl.BlockSpec((tm, n), lambda i, c: (nt - 1 - i, 0))
    colb = lambda: pl.BlockSpec((2, 1, tm, FF_CW), lambda i, c: (0, c, nt - 1 - i, 0))
    gate = lambda r: pl.BlockSpec((1, r, FF_CW), lambda i, c: (c, 0, 0))
    lin = lambda r: pl.BlockSpec((1, r, FF_CW), lambda i, c: (ncb + c, 0, 0))
    return pl.pallas_call(
        body, name="ffn_bwd", grid=(nt, ncb),
        in_specs=[row(D_MODEL), colb(), colb(), gate(3), lin(3), _ANY, _ANY, row(D_MODEL), _full((1, D_MODEL))],
        out_specs=[colb(), row(D_MODEL), row(D_MODEL), _full((2 * ncb, 8, FF_CW)), _full((1, D_MODEL))],
        out_shape=[_sds((2, ncb, S, FF_CW), MXU), _sds((S, D_MODEL)), _sds((S, D_MODEL), MXU), _sds((2 * ncb, 8, FF_CW)),
                   _sds((1, D_MODEL))],
        scratch_shapes=[pltpu.VMEM((tm, D_MODEL), F32), pltpu.VMEM((ncb, 2, 8, FF_CW), F32),
                        pltpu.VMEM(w_up.shape, w_up.dtype), pltpu.VMEM(w_down.shape, w_down.dtype),
                        pltpu.SemaphoreType.DMA((2,))],
        compiler_params=pltpu.CompilerParams(dimension_semantics=("arbitrary", "arbitrary"),
                                             vmem_limit_bytes=FFN_VMEM_LIMIT),
    )(*_in_hbm([dx2, up, ab, conv_w, conv_w, w_down, w_up, x1, g_ffn]))


def _mix_bwd(dx1, gl, ya, yb, ys, uv, w_out, w_pa, w_pb, w_glu, b_glu, g_sgu, ws, ws_t, bias_s, tm, after=()):
    S = dx1.shape[0]

    def body(dx1_ref, gl_ref, ya_ref, yb_ref, ys_ref, uv_ref, wout_ref, wpa_ref, wpb_ref, wglu_ref, bglu_ref, gs_ref,
             ws_ref, wst_ref, bias_ref,
             dgl_ref, dya_ref, dyb_ref, dz_ref, dys_ref, duv_ref, dbglu_ref, dgs_ref, dws_ref, dbs_ref,
             du2_ref, dvn_ref):
        i = pl.program_id(0)

        @pl.when(i == 0)
        def _():
            dbglu_ref[...] = jnp.zeros_like(dbglu_ref)
            dgs_ref[...] = jnp.zeros_like(dgs_ref)
            dws_ref[...] = jnp.zeros_like(dws_ref)
            dbs_ref[...] = jnp.zeros_like(dbs_ref)

        dm = _dot_nt(dx1_ref[...].astype(MXU), wout_ref[...])
        glv = gl_ref[...]
        ga = _sigmoid(glv[:, :D_MODEL])
        gb = _sigmoid(glv[:, D_MODEL:])
        dgl_ref[:, :D_MODEL] = (dm * ya_ref[...] * ga * (1.0 - ga)).astype(MXU)
        dgl_ref[:, D_MODEL:] = (dm * yb_ref[...] * gb * (1.0 - gb)).astype(MXU)
        dyab = (dm * ga).astype(MXU)
        dybb = (dm * gb).astype(MXU)
        dya_ref[...] = dyab
        dyb_ref[...] = dybb

        dyap = _dot_nt(dyab, wpa_ref[...])
        yg, dgelu = _gelu_and_grad(ys_ref[...])
        sz = _sigmoid(_dot(yg.astype(MXU), wglu_ref[...]) + bglu_ref[...])
        dz = dyap * yg * sz * (1.0 - sz)
        dzb = dz.astype(MXU)
        dz_ref[...] = dzb
        dbglu_ref[...] += _rowsum(dz)
        dys_ref[...] = (dyap * sz + _dot_nt(dzb, wglu_ref[...])) * dgelu

        dsg = _dot_nt(dybb, wpb_ref[...])
        uvg, duvg = _gelu_and_grad(uv_ref[...])
        u2 = uvg[:, :SGU_W]
        v2 = uvg[:, SGU_W:]
        rv = _rms(v2)
        vhat = v2 * rv
        gs = gs_ref[...]
        vnb = (vhat * gs).astype(MXU)
        tril = (lax.broadcasted_iota(jnp.int32, (CHUNK, CHUNK), 0)
                >= lax.broadcasted_iota(jnp.int32, (CHUNK, CHUNK), 1))
        for c in range(tm // CHUNK):
            rs = slice(c * CHUNK, (c + 1) * CHUNK)
            vc = vnb[rs]
            mixed = _sgu_mix(vc, ws_ref) + bias_ref[...]
            dsg_c = dsg[rs]
            du2_ref[rs, :] = dsg_c * mixed
            dmx = dsg_c * u2[rs]
            dbs_ref[...] += dmx
            dmb = dmx.astype(MXU)
            dvn_ref[rs, :] = _sgu_mix(dmb, wst_ref)
            for q in range(SGU_G // 2):
                lanes = slice(LANES * q, LANES * (q + 1))
                for j, part in enumerate(_group_halves(dmb[:, lanes])):
                    dws_ref[2 * q + j] += jnp.where(tril, _dot_nt(part, vc[:, lanes]), 0.0)
        dvn = dvn_ref[...]
        dgs_ref[...] += _rowsum(dvn * vhat)
        dv2 = _rms_bwd(dvn * gs, vhat, rv)
        duv_ref[:, :SGU_W] = (du2_ref[...] * duvg[:, :SGU_W]).astype(MXU)
        duv_ref[:, SGU_W:] = (dv2 * duvg[:, SGU_W:]).astype(MXU)

    row = lambda n: pl.BlockSpec((tm, n), lambda i: (i, 0))
    return pl.pallas_call(
        _behind(body, 15, after), name="mix_bwd", grid=(S // tm,),
        in_specs=[row(D_MODEL), row(2 * D_MODEL), row(D_MODEL), row(D_MODEL), row(SSM_W), row(2 * SGU_W),
                  _full(w_out.shape), _full(w_pa.shape), _full(w_pb.shape), _full(w_glu.shape), _full(b_glu.shape),
                  _full(g_sgu.shape), _full(ws.shape), _full(ws_t.shape), _full(bias_s.shape)] + [_ANY] * len(after),
        out_specs=[row(2 * D_MODEL), row(D_MODEL), row(D_MODEL), row(SSM_W), row(SSM_W), row(2 * SGU_W),
                   _full((1, SSM_W)), _full((1, SGU_W)), _full((SGU_G, CHUNK, CHUNK)), _full((CHUNK, SGU_W))],
        out_shape=[_sds((S, 2 * D_MODEL), MXU), _sds((S, D_MODEL), MXU), _sds((S, D_MODEL), MXU), _sds((S, SSM_W), MXU),
                   _sds((S, SSM_W)), _sds((S, 2 * SGU_W), MXU),
                   _sds((1, SSM_W)), _sds((1, SGU_W)), _sds((SGU_G, CHUNK, CHUNK)), _sds((CHUNK, SGU_W))],
        scratch_shapes=[pltpu.VMEM((tm, SGU_W), F32), pltpu.VMEM((tm, SGU_W), F32)],
        compiler_params=_cp("arbitrary"),
    )(*_in_hbm([dx1, gl, ya, yb, ys, uv, w_out, w_pa, w_pb, w_glu, b_glu, g_sgu, ws, ws_t, bias_s]), *after)


def _s5_bwd(dys, us, st_re, st_im, abar_re, abar_im, b_re, b_im, c_re, c_im, d_skip, tm, after=()):
    S = us.shape[0]
    nt = S // tm
    w = 8 * SSM_P
    hb = tm // 8
    run = tm // 8
    assert run & (run - 1) == 0

    def body(dys_ref, us_ref, str_ref, sti_ref, hr_ref, hi_ref, ar_ref, ai_ref, br_ref, bi_ref, cr_ref, ci_ref, d_ref,
             dus_ref, dab_ref, dd_ref, dbr_ref, dbi_ref, dcr_ref, dci_ref,
             tab_ref, car_ref, gr_ref, gi_ref, dyp_ref, up_ref, dun_ref):
        i = pl.program_id(1)
        ri = nt - 1 - i

        @pl.when(i == 0)
        def _():
            car_ref[...] = jnp.zeros_like(car_ref)
            for k, t in enumerate(_scan_tables(*_cpow2(ar_ref[...], -ai_ref[...], run.bit_length() - 1), True)):
                tab_ref[k] = t
            for r in (dab_ref, dd_ref, dbr_ref, dbi_ref, dcr_ref, dci_ref):
                r[...] = jnp.zeros_like(r)

        _runs_load(dys_ref, dyp_ref, run)
        _runs_load(us_ref, up_ref, run)
        dyb = dyp_ref[...].astype(MXU)
        gr_ref[...] = _dot(dyb, cr_ref[0])
        gi_ref[...] = -_dot(dyb, ci_ref[0])
        ar = jnp.broadcast_to(ar_ref[...], (8, w))
        ai = jnp.broadcast_to(-ai_ref[...], (8, w))

        def advance(kk, state):
            r0 = pl.multiple_of((run - 1 - kk) * 8, 8)
            gr, gi = state
            return (ar * gr - ai * gi + gr_ref[pl.ds(r0, 8), :], ar * gi + ai * gr + gi_ref[pl.ds(r0, 8), :])

        def emit(kk, state):
            r0 = pl.multiple_of((run - 1 - kk) * 8, 8)
            gr, gi = advance(kk, state)
            gr_ref[pl.ds(r0, 8), :] = gr
            gi_ref[pl.ds(r0, 8), :] = gi
            return gr, gi

        zero = jnp.zeros((8, w), F32)
        er, ei = lax.fori_loop(0, run, advance, (zero, zero))
        cr, ci = car_ref[0:1, :], car_ref[1:2, :]
        tr, ti = _scan_group(er, ei, tab_ref, cr, ci, True)
        r8 = lax.broadcasted_iota(jnp.int32, (8, w), 0)
        start = (jnp.where(r8 == 7, cr, pltpu.roll(tr, 7, 0)), jnp.where(r8 == 7, ci, pltpu.roll(ti, 7, 0)))
        car_ref[0:1, :] = tr[0:1, :]
        car_ref[1:2, :] = ti[0:1, :]
        lax.fori_loop(0, run, emit, start)

        gsr = gr_ref[...]
        gsi = gi_ref[...]
        sr = str_ref[...]
        si = sti_ref[...]
        first = ri == 0

        def previous(s, halo_ref):
            head = jnp.where(r8 == 0, jnp.where(first, 0.0, halo_ref[7:8, :]), pltpu.roll(s[tm - 8:tm, :], 1, 0))
            return jnp.concatenate([head, s[0:tm - 8, :]], axis=0)

        spr = previous(sr, hr_ref)
        spi = previous(si, hi_ref)
        dab_ref[0, 0:1, :] += _rowsum(gsr * spr + gsi * spi)
        dab_ref[0, 1:2, :] += _rowsum(gsi * spr - gsr * spi)

        gbr = gsr.astype(MXU)
        gbi = gsi.astype(MXU)
        _runs_store(_dot_nt(gbr, br_ref[0]) + _dot_nt(gbi, bi_ref[0]), dun_ref, run)
        dys_v = dys_ref[...]
        dus_ref[...] = (dun_ref[...] + d_ref[...] * dys_v).astype(MXU)
        dd_ref[0, 0:1, :] += _rowsum(dys_v * us_ref[...])
        ub = up_ref[...].astype(MXU)
        dbr_ref[0] += _dot_tn(ub, gbr)
        dbi_ref[0] += _dot_tn(ub, gbi)
        dcr_ref[0] += _dot_tn(dyb, sr.astype(MXU))
        dci_ref[0] -= _dot_tn(dyb, si.astype(MXU))

    blk = lambda: pl.BlockSpec((1, 8 * SSM_H, w), lambda j, i: (j, 0, 0))
    rowl = lambda: pl.BlockSpec((tm, LANES), lambda j, i: (nt - 1 - i, j))
    roww = lambda: pl.BlockSpec((tm, w), lambda j, i: (nt - 1 - i, j))
    halo = lambda: pl.BlockSpec((8, w), lambda j, i: (jnp.maximum((nt - 1 - i) * hb - 1, 0), j))
    return pl.pallas_call(
        _behind(body, 13, after), name="s5_bwd", grid=(SSM_BLK, nt),
        in_specs=[rowl(), rowl(), roww(), roww(), halo(), halo(),
                  pl.BlockSpec((1, w), lambda j, i: (0, j)), pl.BlockSpec((1, w), lambda j, i: (0, j)),
                  blk(), blk(), blk(), blk(),
                  pl.BlockSpec((1, LANES), lambda j, i: (0, j))] + [_ANY] * len(after),
        out_specs=[rowl(),
                   pl.BlockSpec((1, 8, w), lambda j, i: (j, 0, 0)), pl.BlockSpec((1, 8, LANES), lambda j, i: (j, 0, 0)),
                   blk(), blk(), blk(), blk()],
        out_shape=[_sds((S, SSM_W), MXU), _sds((SSM_BLK, 8, w)), _sds((SSM_BLK, 8, LANES)),
                   _sds((SSM_BLK, 8 * SSM_H, w)), _sds((SSM_BLK, 8 * SSM_H, w)),
                   _sds((SSM_BLK, 8 * SSM_H, w)), _sds((SSM_BLK, 8 * SSM_H, w))],
        scratch_shapes=[pltpu.VMEM((8, 8, w), F32), pltpu.VMEM((8, w), F32),
                        pltpu.VMEM((tm, w), F32), pltpu.VMEM((tm, w), F32),
                        pltpu.VMEM((tm, LANES), F32), pltpu.VMEM((tm, LANES), F32), pltpu.VMEM((tm, LANES), F32)],
        compiler_params=_cp("parallel", "arbitrary"),
    )(*_in_hbm([dys, us, st_re, st_im, st_re, st_im, abar_re, abar_im, b_re, b_im, c_re, c_im, d_skip]), *after)


def _in_bwd(dus, duv, dgl, dx1, x, g_mix, w_in, tm, after=()):
    S = x.shape[0]

    def body(dus_ref, duv_ref, dgl_ref, dx1_ref, x_ref, g_ref, w_ref, gx_ref, dg_ref):
        @pl.when(pl.program_id(0) == 0)
        def _():
            dg_ref[...] = jnp.zeros_like(dg_ref)

        dh = (_dot(dus_ref[...], w_ref[0:SSM_W, :])
              + _dot(duv_ref[...], w_ref[SSM_W:SSM_W + 2 * SGU_W, :])
              + _dot(dgl_ref[...], w_ref[SSM_W + 2 * SGU_W:, :]))
        xv = x_ref[...]
        r = _rms(xv)
        xn = xv * r
        dg_ref[...] += _rowsum(dh * xn)
        gx_ref[...] = dx1_ref[...] + _rms_bwd(dh * g_ref[...], xn, r)

    row = lambda n: pl.BlockSpec((tm, n), lambda i: (i, 0))
    return pl.pallas_call(
        _behind(body, 7, after), name="in_bwd", grid=(S // tm,),
        in_specs=[row(SSM_W), row(2 * SGU_W), row(2 * D_MODEL), row(D_MODEL), row(D_MODEL), _full((1, D_MODEL)),
                  _full(w_in.shape)] + [_ANY] * len(after),
        out_specs=[row(D_MODEL), _full((1, D_MODEL))],
        out_shape=[_sds((S, D_MODEL)), _sds((1, D_MODEL))],
        compiler_params=_cp("arbitrary"),
    )(*_in_hbm([dus, duv, dgl, dx1, x, g_mix, w_in]), *after)


def _wgrad_split(a, b, nsplit, tk, name):
    S, K = a.shape
    N = b.shape[1]
    c = N // nsplit

    def body(a_ref, b_ref, o_ref):
        prod = _dot_tn(a_ref[...], b_ref[...])
        for d in range(nsplit):
            o_ref[d] = prod[:, c * d:c * (d + 1)].astype(MXU)

    return pl.pallas_call(
        body, name=name, grid=(K // tk,),
        in_specs=[pl.BlockSpec((S, tk), lambda k: (0, k)), _full((S, N))],
        out_specs=pl.BlockSpec((nsplit, tk, c), lambda k: (0, k, 0)),
        out_shape=_sds((nsplit, K, c), MXU),
        compiler_params=_cp("parallel"),
    )(*_in_hbm([a, b]))


def _wgrad_in_t(dps, h1, name):
    S, K = h1.shape
    cw = 512
    counts = [b.shape[1] // cw for b in dps]
    starts = [sum(counts[:i]) for i in range(len(dps))]
    nblk = sum(counts)

    def body(*refs):
        b_refs = refs[:len(dps)]
        h_ref, o_ref = refs[len(dps)], refs[-1]
        j = pl.program_id(0)
        for b_ref, st, cnt in zip(b_refs, starts, counts):
            @pl.when(jnp.logical_and(j >= st, j < st + cnt))
            def _():
                o_ref[...] = _dot_tn(b_ref[...], h_ref[...]).astype(MXU)

    def src_spec(st, cnt):
        return pl.BlockSpec((S, cw), lambda j: (0, jnp.clip(j - st, 0, cnt - 1)))

    return pl.pallas_call(
        body, name=name, grid=(nblk,),
        in_specs=[src_spec(st, cnt) for st, cnt in zip(starts, counts)] + [_full((S, K))],
        out_specs=pl.BlockSpec((cw, K), lambda j: (j, 0)),
        out_shape=_sds((nblk * cw, K), MXU),
        compiler_params=_cp("arbitrary"),
    )(*_in_hbm([*dps, h1]))


def _wgrad_blk(a3, b3, nblk, a_of, b_of, name):
    S, K = a3.shape[1:]
    N = b3.shape[2]

    def body(a_ref, b_ref, o_ref):
        o_ref[0] = _dot_tn(a_ref[0], b_ref[0]).astype(MXU)

    return pl.pallas_call(
        body, name=name, grid=(nblk,),
        in_specs=[pl.BlockSpec((1, S, K), lambda b: (a_of(b), 0, 0)),
                  pl.BlockSpec((1, S, N), lambda b: (b_of(b), 0, 0))],
        out_specs=pl.BlockSpec((1, K, N), lambda b: (b, 0, 0)),
        out_shape=_sds((nblk, K, N), MXU),
        compiler_params=pltpu.CompilerParams(dimension_semantics=("parallel",), vmem_limit_bytes=WGRAD_VMEM_LIMIT),
    )(*_in_hbm([a3, b3]))


def _assemble_cols(blocks_list, name):
    def body(*refs):
        n = len(blocks_list)
        for b_ref, o_ref in zip(refs[:n], refs[n:]):
            c = b_ref.shape[2]
            for d in range(N_DEV):
                o_ref[:, c * d:c * (d + 1)] = b_ref[d]

    outs = [_sds((b.shape[1], N_DEV * b.shape[2]), b.dtype) for b in blocks_list]
    return pl.pallas_call(
        body, name=name, grid=(1,), in_specs=[_full(b.shape) for b in blocks_list],
        out_specs=[_full(o.shape) for o in outs], out_shape=outs, compiler_params=_cp("arbitrary"),
    )(*_in_hbm(blocks_list))


def _tile(S, want):
    return want if S % want == 0 else S


def _local_step(x, tgt, p, after, mixer_relay, mixer_weights, ffn_weights, grads_out, small_out):
    S = x.shape[0]
    tm = _tile(S, 256)
    tl = _tile(S, 512)

    rep = lambda a: jnp.repeat(a, SSM_H, axis=0)
    are = rep(p["a_re"])
    aim = rep(p["a_im"])
    ldt = jnp.broadcast_to(rep(p["log_dt"].reshape(SSM_G, 1)), are.shape)
    br_t = p["b_re_t"].reshape(are.shape)
    bi_t = p["b_im_t"].reshape(are.shape)
    abr, abi, bbr, bbi = _s5_params_fwd(are, aim, ldt, br_t, bi_t)
    head = lambda a: a.reshape(SSM_G, SSM_H, SSM_P)[:, 0, :].reshape(1, SSM_G * SSM_P)
    abar_re, abar_im = head(abr), head(abi)
    bd_br = _blockdiag(bbr).astype(MXU)
    bd_bi = _blockdiag(bbi).astype(MXU)
    bd_cr = _blockdiag(p["c_re"].reshape(are.shape)).astype(MXU)
    bd_ci = _blockdiag(p["c_im"].reshape(are.shape)).astype(MXU)
    d_skip = p["d_skip"].reshape(1, SSM_W)

    tril = jnp.tril(jnp.ones((CHUNK, CHUNK), dtype=bool))
    ws = jnp.where(tril[None], p["w_s"], 0.0)
    pair = lambda w: w.reshape(SGU_G // 2, 2, CHUNK, CHUNK).transpose(0, 2, 1, 3).reshape(SGU_G // 2, CHUNK, 2 * CHUNK)
    ws_b = pair(ws).astype(MXU)
    ws_t = pair(ws.transpose(0, 2, 1)).astype(MXU)
    bias_s = jnp.repeat(p["b_s"].T, SGU_D, axis=1)

    g_mix = p["g_mix"].reshape(1, D_MODEL)
    g_ffn = p["g_ffn"].reshape(1, D_MODEL)
    g_final = p["g_final"].reshape(1, D_MODEL)
    g_sgu = p["g_sgu"].reshape(1, SGU_W)
    b_glu = p["b_glu"].reshape(1, SSM_W)
    conv_b = p["conv_b"].reshape(2 * FF_NCB, 1, FF_CW)
    tf = _tile(S, 256)
    ts = _tile(S, 1024)

    h1, us, uv, gl = _in_fwd(x, g_mix, p["w_in_t"], tl, after)
    token = mixer_relay(us)
    st_re, st_im, ys = _s5_fwd(us, abar_re, abar_im, bd_br, bd_bi, bd_cr, bd_ci, d_skip, ts, (token,))
    p = dict(p, **mixer_weights(ys))
    yg, yap, sg, ya, yb, m, x1, h2 = _mix_fwd(x, ys, uv, gl, p["w_glu"], b_glu, p["w_proj_a"], g_sgu, ws_b, bias_s,
                                              p["w_proj_b"], p["w_out"], g_ffn, tl)
    w_up, conv_w, w_down = ffn_weights(h2)
    pair_lanes = lambda a: a.reshape(N_DEV // 2, 2, a.shape[1], FF_SHARD).transpose(0, 2, 1, 3).reshape(
        N_DEV // 2, a.shape[1], FF_CW)
    w_up = w_up.reshape(2 * FF_NCB, FF_CW, D_MODEL)
    conv_w = pair_lanes(conv_w)
    up, ab, ff, dx2, dx2b, loss, dg_final = _ffn_fwd(h2, x1, tgt, w_up, conv_w, conv_b, w_down, g_final, tf)

    dup, dx1, dx1b, dconv, dg_ffn = _ffn_bwd(dx2, up, ab, x1, w_up, conv_w, w_down, g_ffn, tf)
    rows8 = lambda g: g.reshape(N_DEV, g.shape[1] // N_DEV, g.shape[2])
    g_up = _wgrad_blk(dup.reshape(2 * FF_NCB, S, FF_CW), h2[None], 2 * FF_NCB, lambda b: b, lambda b: 0,
                      "wgrad_up").reshape(N_DEV, FF_SHARD, D_MODEL)
    g_down = _wgrad_blk(ff, dx2b[None], FF_NCB, lambda b: b, lambda b: 0, "wgrad_down").reshape(
        N_DEV, D_FF // N_DEV, D_MODEL)
    token = grads_out(("w_up", "w_down"), (g_up, g_down))
    dgl, dya, dyb, dz, dys, duv, db_glu, dg_sgu, dws, dbs = _mix_bwd(
        dx1, gl, ya, yb, ys, uv, p["w_out"], p["w_proj_a"], p["w_proj_b"], p["w_glu"], b_glu, g_sgu,
        ws_b, ws_t, bias_s, tm, (token,))
    token = grads_out(("w_glu", "w_proj_a", "w_proj_b", "w_out"),
                      (rows8(_wgrad_split(yg, dz, 1, SSM_W, "wgrad_glu")),
                       _wgrad_split(yap, dya, N_DEV, SSM_W, "wgrad_pa"),
                       _wgrad_split(sg, dyb, N_DEV, SGU_W, "wgrad_pb"),
                       rows8(_wgrad_split(m, dx1b, 1, 512, "wgrad_out"))))
    dus, dab, dd, dbbr, dbbi, dcr, dci = _s5_bwd(dys, us, st_re, st_im, abar_re, abar_im, bd_br, bd_bi, bd_cr, bd_ci,
                                                 d_skip, ts, (token,))
    g_in = _wgrad_in_t([dus, duv, dgl], h1, "wgrad_in")
    token = grads_out(("w_in",), (g_in.reshape(N_DEV, g_in.shape[0] // N_DEV, D_MODEL),))
    grad_x, dg_mix = _in_bwd(dus, duv, dgl, dx1, x, g_mix, p["w_in_t"], tl, (token,))

    spread = lambda v: jnp.repeat(v.reshape(SSM_G, SSM_P), SSM_H, axis=0) * (1.0 / SSM_H)
    dabr = spread(dab[:, 0, :])
    dabi = spread(dab[:, 1, :])
    dare, daim, dldt, dbr_t, dbi_t = _s5_params_bwd(are, aim, ldt, br_t, bi_t, dabr, dabi,
                                                    _unblockdiag(dbbr), _unblockdiag(dbbi))
    fold = lambda a: a.reshape(SSM_G, SSM_H, SSM_P).sum(axis=1)

    grads = {
        "g_mix": dg_mix,
        "a_re": fold(dare), "a_im": fold(daim), "log_dt": fold(dldt).sum(axis=1),
        "b_re": dbr_t, "b_im": dbi_t,
        "c_re": _unblockdiag(dcr).reshape(SSM_G, SSM_H, SSM_P),
        "c_im": _unblockdiag(dci).reshape(SSM_G, SSM_H, SSM_P),
        "d_skip": dd[:, 0, :].reshape(SSM_W),
        "b_glu": db_glu,
        "g_sgu": dg_sgu,
        "w_s": dws,
        "b_s": dbs.reshape(CHUNK, SGU_G, SGU_D).sum(axis=-1).T,
        "g_ffn": dg_ffn,
        "conv_w": dconv[:, 0:3, :].reshape(N_DEV // 2, 3, 2, FF_SHARD).transpose(0, 2, 1, 3).reshape(
            N_DEV, 3, FF_SHARD),
        "conv_b": dconv[:, 3, :].reshape(2 * D_FF),
        "g_final": dg_final,
    }
    small_out(grads, loss)
    return grad_x


_ANY = pl.BlockSpec(memory_space=pl.ANY)
_MESH = pl.DeviceIdType.MESH


def _allgather(shards, dtypes, name, cast_only=(), sum_slots=False):
    n = len(shards)
    e = len(cast_only)
    shapes = [s.shape[1:] if sum_slots else s.shape for s in shards]

    def body(*refs):
        in_refs, extra_in = refs[:n], refs[n:n + e]
        out_refs, extra_out = refs[n + e:2 * n + e], refs[2 * n + e:2 * n + 2 * e]
        stage = refs[2 * n + 2 * e:3 * n + 2 * e]
        send_sems, recv_sems, local_sems = refs[3 * n + 2 * e:]
        for a in range(n):
            if sum_slots:
                total = in_refs[a][0].astype(F32)
                for s in range(1, N_DEV):
                    total = total + in_refs[a][s].astype(F32)
                stage[a][...] = total.astype(dtypes[a])
            else:
                stage[a][...] = in_refs[a][...].astype(dtypes[a])
        for i in range(e):
            extra_out[i][...] = extra_in[i][...].astype(MXU)
        x, y, c = lax.axis_index("x"), lax.axis_index("y"), lax.axis_index("c")
        me, sibling = (x, y, c), (x, y, 1 - c)
        chips = [(1 - x, y), (x, 1 - y), (1 - x, 1 - y)]

        def slot(a, px, py, pc):
            return out_refs[a].at[4 * px + 2 * py + pc]

        def copy(a, k, block, to, src=None):
            return pltpu.make_async_remote_copy(
                src_ref=slot(a, *block) if src is None else src, dst_ref=slot(a, *block),
                send_sem=send_sems.at[a, k], recv_sem=recv_sems.at[a, k], device_id=to, device_id_type=_MESH)

        mine = [pltpu.make_async_copy(stage[a], slot(a, *me), local_sems.at[a]) for a in range(n)]
        for cp in mine:
            cp.start()
        first = []
        for j, chip in enumerate(chips):
            first += [copy(a, 1 + j, me, (*chip, c), src=stage[a]) for a in range(n)]
        first += [copy(a, 0, me, sibling, src=stage[a]) for a in range(n)]
        for cp in first:
            cp.start()
        passed = []
        for j, chip in enumerate(chips):
            for a in range(n):
                copy(a, 1 + j, (*chip, c), me).wait_recv()
                fwd = copy(a, 4 + j, (*chip, c), sibling)
                fwd.start()
                passed.append(fwd)
        for a in range(n):
            copy(a, 0, sibling, me).wait_recv()
        for j, chip in enumerate(chips):
            for a in range(n):
                copy(a, 4 + j, (*chip, 1 - c), me).wait_recv()
        for cp in first + passed:
            cp.wait_send()
        for cp in mine:
            cp.wait()

    res = pl.pallas_call(
        body, name=name, grid=(1,), in_specs=[_full(s.shape) for s in list(shards) + list(cast_only)],
        out_specs=[_ANY] * n + [_full(s.shape) for s in cast_only],
        out_shape=[_sds((N_DEV,) + shp, dt) for shp, dt in zip(shapes, dtypes)]
                  + [_sds(s.shape, MXU) for s in cast_only],
        scratch_shapes=[pltpu.VMEM(shp, dt) for shp, dt in zip(shapes, dtypes)]
                       + [pltpu.SemaphoreType.DMA((n, 7)), pltpu.SemaphoreType.DMA((n, 7)), pltpu.SemaphoreType.DMA((n,))],
        compiler_params=pltpu.CompilerParams(vmem_limit_bytes=VMEM_LIMIT),
    )(*_in_hbm([*shards, *cast_only]))
    return res[:n], res[n:]


_HBM = pl.BlockSpec(memory_space=pltpu.HBM)
_SEM = pl.BlockSpec(memory_space=pltpu.SEMAPHORE)
_EFFECT = pltpu.SideEffectType.DATAFLOW_SIDE_EFFECTING
_PEER_ORDER = (2, 4, 6, 3, 5, 7, 1)


def _peer(k):
    x, y, c = lax.axis_index("x"), lax.axis_index("y"), lax.axis_index("c")
    px = 1 - x if k & 4 else x
    py = 1 - y if k & 2 else y
    pc = 1 - c if k & 1 else c
    return (px, py, pc), 4 * px + 2 * py + pc


_SAME_CORE_AND_SIBLING = (2, 4, 6, 1)


def _push_start(srcs, lands, slotted, name, peers=_PEER_ORDER):
    n = len(srcs)

    def body(*refs):
        src_refs, land_refs = refs[:n], refs[n:2 * n]
        send_sems, recv_sems, token = refs[2 * n], refs[2 * n + 1], refs[-1]
        mine = 4 * lax.axis_index("x") + 2 * lax.axis_index("y") + lax.axis_index("c")
        for k in peers:
            dev, theirs = _peer(k)
            for a in range(n):
                pltpu.make_async_remote_copy(
                    src_ref=src_refs[a].at[theirs] if slotted else src_refs[a], dst_ref=land_refs[a].at[mine],
                    send_sem=send_sems.at[7 * a + k - 1], recv_sem=recv_sems.at[7 * a + k - 1],
                    device_id=dev, device_id_type=_MESH).start()
        token[...] = jnp.zeros_like(token)

    bufs = list(srcs) + list(lands)
    res = pl.pallas_call(
        body, name=name, in_specs=[_HBM] * (2 * n),
        out_specs=(_SEM, _SEM, *[_HBM] * (2 * n), pl.BlockSpec(memory_space=pltpu.VMEM)),
        out_shape=(pltpu.SemaphoreType.DMA((7 * n,)), pltpu.SemaphoreType.DMA((7 * n,)),
                   *[pltpu.HBM(b.shape, b.dtype) for b in bufs], _sds((8, LANES))),
        input_output_aliases={i: 2 + i for i in range(2 * n)},
        compiler_params=pltpu.CompilerParams(has_side_effects=_EFFECT),
    )(*[pltpu.with_memory_space_constraint(b, pltpu.HBM) for b in bufs])
    return res[0], res[1], res[2:2 + n], res[2 + n:2 + 2 * n], res[-1]


def _push_wait(send_sems, recv_sems, srcs, lands, slotted, after, name, peers=_PEER_ORDER):
    n = len(srcs)

    def body(*refs):
        src_refs, land_refs = refs[:n], refs[n:2 * n]
        send_sems, recv_sems = refs[2 * n], refs[2 * n + 1]
        for k in peers:
            dev, theirs = _peer(k)
            for a in range(n):
                cp = pltpu.make_async_remote_copy(
                    src_ref=src_refs[a].at[theirs] if slotted else src_refs[a], dst_ref=land_refs[a].at[theirs],
                    send_sem=send_sems.at[7 * a + k - 1], recv_sem=recv_sems.at[7 * a + k - 1],
                    device_id=dev, device_id_type=_MESH)
                cp.wait_send()
                cp.wait_recv()

    bufs = list(srcs) + list(lands)
    res = pl.pallas_call(
        body, name=name, in_specs=[_HBM] * (2 * n) + [_SEM, _SEM] + [_ANY] * len(after), out_specs=[_HBM] * (2 * n),
        out_shape=[pltpu.HBM(b.shape, b.dtype) for b in bufs],
        input_output_aliases={i: i for i in range(2 * n)},
        compiler_params=pltpu.CompilerParams(has_side_effects=_EFFECT),
    )(*bufs, send_sems, recv_sems, *after)
    return res[n:]


def _other_chips():
    x, y = lax.axis_index("x"), lax.axis_index("y")
    return ((1 - x, y), (x, 1 - y), (1 - x, 1 - y))


def _relay_start(lands, name):
    n = len(lands)

    def body(*refs):
        land_refs = refs[:n]
        send_sems, recv_sems, token = refs[n], refs[n + 1], refs[-1]
        x, y, c = lax.axis_index("x"), lax.axis_index("y"), lax.axis_index("c")
        for j, (px, py) in enumerate(_other_chips()):
            slot = 4 * px + 2 * py + c
            for a in range(n):
                pltpu.make_async_remote_copy(
                    src_ref=land_refs[a].at[slot], dst_ref=land_refs[a].at[slot],
                    send_sem=send_sems.at[3 * a + j], recv_sem=recv_sems.at[3 * a + j],
                    device_id=(x, y, 1 - c), device_id_type=_MESH).start()
        token[...] = jnp.zeros_like(token)

    res = pl.pallas_call(
        body, name=name, in_specs=[_HBM] * n,
        out_specs=(_SEM, _SEM, *[_HBM] * n, pl.BlockSpec(memory_space=pltpu.VMEM)),
        out_shape=(pltpu.SemaphoreType.DMA((3 * n,)), pltpu.SemaphoreType.DMA((3 * n,)),
                   *[pltpu.HBM(b.shape, b.dtype) for b in lands], _sds((8, LANES))),
        input_output_aliases={i: 2 + i for i in range(n)},
        compiler_params=pltpu.CompilerParams(has_side_effects=_EFFECT),
    )(*[pltpu.with_memory_space_constraint(b, pltpu.HBM) for b in lands])
    return res[0], res[1], res[2:2 + n], res[-1]


def _relay_wait(send_sems, recv_sems, lands, after, name):
    n = len(lands)

    def body(*refs):
        land_refs = refs[:n]
        send_sems, recv_sems = refs[n], refs[n + 1]
        x, y, c = lax.axis_index("x"), lax.axis_index("y"), lax.axis_index("c")
        for j, (px, py) in enumerate(_other_chips()):
            sent, received = 4 * px + 2 * py + c, 4 * px + 2 * py + (1 - c)
            for a in range(n):
                cp = pltpu.make_async_remote_copy(
                    src_ref=land_refs[a].at[sent], dst_ref=land_refs[a].at[received],
                    send_sem=send_sems.at[3 * a + j], recv_sem=recv_sems.at[3 * a + j],
                    device_id=(x, y, 1 - c), device_id_type=_MESH)
                cp.wait_send()
                cp.wait_recv()

    return pl.pallas_call(
        body, name=name, in_specs=[_HBM] * n + [_SEM, _SEM] + [_ANY] * len(after), out_specs=[_HBM] * n,
        out_shape=[pltpu.HBM(b.shape, b.dtype) for b in lands],
        input_output_aliases={i: i for i in range(n)},
        compiler_params=pltpu.CompilerParams(has_side_effects=_EFFECT),
    )(*lands, send_sems, recv_sems, *after)


def _adamw(w, g, m, v):
    m2 = ADAM_B1 * m + (1.0 - ADAM_B1) * g
    v2 = ADAM_B2 * v + (1.0 - ADAM_B2) * (g * g)
    m_hat = m2 / (1.0 - ADAM_B1 ** ADAM_STEP)
    v_hat = v2 / (1.0 - ADAM_B2 ** ADAM_STEP)
    delta = -ADAM_LR * (m_hat / (jnp.sqrt(v_hat) + ADAM_EPS) + ADAM_WD * w)
    return delta, m2, v2


def _adam_shard(parts, w, m, v, name):
    _, r, c = w.shape
    tr = max(t for t in range(16, 257, 16) if r % t == 0)

    nparts = parts.shape[0]

    def body(p_ref, w_ref, m_ref, v_ref, g_ref, d_ref, m2_ref, v2_ref):
        g = p_ref[0].astype(F32)
        for s in range(1, nparts):
            g = g + p_ref[s].astype(F32)
        g_ref[0] = g
        d_ref[0], m2_ref[0], v2_ref[0] = _adamw(w_ref[0], g, m_ref[0], v_ref[0])

    row = lambda: pl.BlockSpec((1, tr, c), lambda i: (0, i, 0))
    return pl.pallas_call(
        body, name=name, grid=(r // tr,),
        in_specs=[pl.BlockSpec((nparts, tr, c), lambda i: (0, i, 0)), row(), row(), row()],
        out_specs=[row(), row(), row(), row()], out_shape=[_sds((1, r, c))] * 4,
        compiler_params=_cp("parallel"),
    )(*_in_hbm([parts, w, m, v]))


def _adam_small(gs, ws, ms, vs, name):
    n = len(gs)

    def body(*refs):
        ins, outs = refs[:4 * n], refs[4 * n:]
        for i in range(n):
            g = ins[i][...]
            d, m2, v2 = _adamw(ins[n + i][...], g, ins[2 * n + i][...], ins[3 * n + i][...])
            outs[i][...] = d
            outs[n + i][...] = m2
            outs[2 * n + i][...] = v2

    res = pl.pallas_call(
        body, name=name, grid=(1,), in_specs=[_full(w.shape) for w in ws] * 4,
        out_specs=[_full(w.shape) for w in ws] * 3, out_shape=[_sds(w.shape) for w in ws] * 3,
        compiler_params=_cp("arbitrary"),
    )(*_in_hbm([*gs, *ws, *ms, *vs]))
    return res[:n], res[n:2 * n], res[2 * n:]


def _pad_to(a, n, axis):
    extra = n - a.shape[axis]
    if extra == 0:
        return a
    widths = [(0, 0)] * a.ndim
    widths[axis] = (0, extra)
    return jnp.pad(a, widths)


def _ceil_to(n, k):
    return -(-n // k) * k


def _pack_rows(flats, rows_multiple):
    parts = [_pad_to(f, _ceil_to(f.shape[-1], LANES), f.ndim - 1) for f in flats]
    cat = jnp.concatenate(parts, axis=-1)
    total = _ceil_to(cat.shape[-1], LANES * rows_multiple)
    cat = _pad_to(cat, total, cat.ndim - 1)
    return cat.reshape(cat.shape[:-1] + (total // LANES, LANES))


def _unpack_rows(buf, sizes):
    flat = buf.reshape(buf.shape[:-2] + (-1,))
    out, off = [], 0
    for n in sizes:
        out.append(flat[..., off:off + n])
        off += _ceil_to(n, LANES)
    return out


_MIX_BIG = ("w_in", "w_glu", "w_proj_a", "w_proj_b", "w_out")
_BIG = _MIX_BIG + ("w_up", "w_down")
_SMALL = ("g_mix", "a_re", "a_im", "log_dt", "b_re", "b_im", "c_re", "c_im", "d_skip", "b_glu", "g_sgu", "w_s", "b_s",
          "g_ffn", "conv_b", "g_final")
_SMALL_ROWS_MULTIPLE = 8 * N_DEV
_TRANSPOSED = ("w_in", "w_up", "b_re", "b_im")


def _as_2d(a):
    return a.reshape(-1, a.shape[-1]) if a.ndim > 1 else a.reshape(1, -1)


def kernel(x, g_mix, w_in, a_re, a_im, log_dt, b_re, b_im, c_re, c_im, d_skip, w_glu, b_glu, w_proj_a, g_sgu, w_s, b_s, w_proj_b, w_out, g_ffn, w_up, conv_w, conv_b, w_down, g_final, loss_target, m_g_mix, m_w_in, m_a_re, m_a_im, m_log_dt, m_b_re, m_b_im, m_c_re, m_c_im, m_d_skip, m_w_glu, m_b_glu, m_w_proj_a, m_g_sgu, m_w_s, m_b_s, m_w_proj_b, m_w_out, m_g_ffn, m_w_up, m_conv_w, m_conv_b, m_w_down, m_g_final, v_g_mix, v_w_in, v_a_re, v_a_im, v_log_dt, v_b_re, v_b_im, v_c_re, v_c_im, v_d_skip, v_w_glu, v_b_glu, v_w_proj_a, v_g_sgu, v_w_s, v_b_s, v_w_proj_b, v_w_out, v_g_ffn, v_w_up, v_conv_w, v_conv_b, v_w_down, v_g_final):
    args = dict(locals())
    me = 4 * lax.axis_index("x") + 2 * lax.axis_index("y") + lax.axis_index("c")

    def own_slot(buf, block):
        return lax.dynamic_update_slice(buf, block[None], (me,) + (0,) * block.ndim)

    for n in _TRANSPOSED:
        for pre in ("", "m_", "v_"):
            args[pre + n] = jnp.swapaxes(args[pre + n], -1, -2)
    later = ("w_glu", "w_proj_a", "w_proj_b", "w_out", "w_up", "w_down")
    (w_in_g,), casts = _allgather([args["w_in"][0]], [MXU], "allgather_w_in", cast_only=[args[n][0] for n in later])
    sh = dict(zip(later, casts))

    def start_push(srcs, tag, peers):
        lands = [own_slot(lax.empty((N_DEV,) + s.shape, s.dtype), s) for s in srcs]
        send_sems, recv_sems, srcs, lands, token = _push_start(srcs, lands, False, "push_" + tag, peers)
        return (send_sems, recv_sems, srcs, lands), token

    mix_push, token_a = start_push([sh[n] for n in later[:4]], "mixer_weights", _SAME_CORE_AND_SIBLING)
    ffn_push, token_b = start_push([sh["w_up"], sh["w_down"], conv_w[0]], "ffn_weights", _PEER_ORDER)
    p = {n: (args[n][0] if n != "g_final" else args[n]) for n in _SMALL if n not in _TRANSPOSED}
    p.update(w_in_t=w_in_g.reshape(SSM_W + 2 * SGU_W + 2 * D_MODEL, D_MODEL),
             b_re_t=args["b_re"][0], b_im_t=args["b_im"][0])
    relay = {}

    def mixer_relay(after):
        lands = _push_wait(*mix_push, False, [after], "wait_mixer_weights", _SAME_CORE_AND_SIBLING)
        relay["send"], relay["recv"], relay["lands"], token = _relay_start(lands, "relay_mixer_weights")
        return token

    def mixer_weights(after):
        w_glu_g, w_pa_g, w_pb_g, w_out_g = _relay_wait(relay["send"], relay["recv"], relay["lands"], [after],
                                                       "wait_relay_mixer_weights")
        w_pa_full, w_pb_full = _assemble_cols([w_pa_g, w_pb_g], "assemble_cols")
        return dict(w_glu=w_glu_g.reshape(SSM_W, SSM_W), w_proj_a=w_pa_full, w_proj_b=w_pb_full,
                    w_out=w_out_g.reshape(D_MODEL, D_MODEL))

    def ffn_weights(after):
        w_up_g, w_down_g, conv_w_g = _push_wait(*ffn_push, False, [after], "wait_ffn_weights")
        return w_up_g, conv_w_g, w_down_g.reshape(D_FF, D_MODEL)

    pushes = []

    def grads_out(names, sends):
        lands = [own_slot(lax.empty(s.shape, s.dtype), lax.dynamic_index_in_dim(s, me, 0, keepdims=False))
                 for s in sends]
        send_sems, recv_sems, srcs, lands, token = _push_start(list(sends), lands, True, "push_grads_" + names[0])
        pushes.append((names, send_sems, recv_sems, srcs, lands))
        return token


    small_names = _SMALL + ("conv_w", "loss")
    small = {}

    def small_out(grads, loss_part):
        small_g = dict(grads, loss=loss_part[0, 0:1])
        flats = [small_g[n].reshape(-1) for n in small_names]
        small["sizes"] = [f.shape[0] for f in flats]
        g_small = _pack_rows(flats, _SMALL_ROWS_MULTIPLE)
        small["rs8"] = g_small.shape[0] // N_DEV
        return grads_out(("small",), (g_small.reshape(N_DEV, small["rs8"], LANES),))

    grad_x = _local_step(x[0], loss_target[0], p, (token_a, token_b), mixer_relay, mixer_weights, ffn_weights,
                         grads_out, small_out)

    out = {}
    done = [grad_x]
    for names, send_sems, recv_sems, srcs, lands in pushes:
        parts = _push_wait(send_sems, recv_sems, srcs, lands, True, done, "wait_grads_" + names[0])
        if names == ("small",):
            g_small_all = _allgather([parts[0]], [F32], "allgather_small", sum_slots=True)[0][0].reshape(
                N_DEV * small["rs8"], LANES)
            pieces = dict(zip(small_names, _unpack_rows(g_small_all, small["sizes"])))
            loss = pieces["loss"][0]
            dconv_w = lax.dynamic_index_in_dim(pieces["conv_w"].reshape(N_DEV, 3, FF_SHARD), me, axis=0, keepdims=False)
            names2 = _SMALL + ("conv_w",)
            gs = [pieces[n].reshape(_as_2d(args[n]).shape) for n in _SMALL] + [dconv_w]
            ds, m2s, v2s = _adam_small(gs, [_as_2d(args[n]) for n in names2], [_as_2d(args["m_" + n]) for n in names2],
                                       [_as_2d(args["v_" + n]) for n in names2], "adam_small")
            for n, res in zip(names2, zip(gs, ds, m2s, v2s)):
                for kind, v in zip(("grad_", "delta_", "new_m_", "new_v_"), res):
                    out[kind + n] = v.reshape(args[n].shape)
            done = [ds[0]]
            continue
        for n, part in zip(names, parts):
            res = _adam_shard(part, args[n], args["m_" + n], args["v_" + n], "adam_" + n)
            for kind, v in zip(("grad_", "delta_", "new_m_", "new_v_"), res):
                out[kind + n] = v
            done = [res[0]]
    order = ("g_mix", "w_in", "a_re", "a_im", "log_dt", "b_re", "b_im", "c_re", "c_im", "d_skip", "w_glu", "b_glu",
             "w_proj_a", "g_sgu", "w_s", "b_s", "w_proj_b", "w_out", "g_ffn", "w_up", "conv_w", "conv_b", "w_down",
             "g_final")
    res = [loss, grad_x.reshape(x.shape)]
    for kind in ("grad_", "delta_", "new_m_", "new_v_"):
        res += [jnp.swapaxes(out[kind + n], -1, -2) if n in _TRANSPOSED else out[kind + n] for n in order]
    return tuple(res)
```

```python
import math

import jax
import jax.numpy as jnp
from jax import lax
from jax.experimental import pallas as pl
from jax.experimental.pallas import tpu as pltpu

F32 = jnp.float32
MXU = jnp.bfloat16
EPS = 1e-6

D_MODEL = 1024
SSM_W = 512
SSM_G, SSM_H, SSM_P = 32, 16, 64
SSM_BLK = 4
SGU_W = 512
SGU_G, SGU_D, CHUNK = 8, 64, 128
D_FF = 2816
N_DEV = 8
FF_SHARD = 2 * D_FF // N_DEV
FF_CW = 2 * FF_SHARD
FF_NCB = D_FF // FF_CW
LANES = 128

ADAM_LR, ADAM_B1, ADAM_B2, ADAM_EPS, ADAM_WD, ADAM_STEP = 0.001, 0.9, 0.999, 1e-08, 0.01, 10

VMEM_LIMIT = 48 * 1024 * 1024
WGRAD_VMEM_LIMIT = 58 * 1024 * 1024
FFN_VMEM_LIMIT = 58 * 1024 * 1024


def _cp(*sem):
    return pltpu.CompilerParams(dimension_semantics=sem, vmem_limit_bytes=VMEM_LIMIT)


def _full(shape):
    n = len(shape)
    return pl.BlockSpec(shape, lambda *_: (0,) * n)


def _sds(shape, dtype=F32):
    return jax.ShapeDtypeStruct(shape, dtype)


def _in_hbm(arrays):
    return [pltpu.with_memory_space_constraint(a, pltpu.HBM) for a in arrays]


def _behind(body, n_in, after):
    def ordered(*refs):
        body(*refs[:n_in], *refs[n_in + len(after):])
    return ordered


def _dot(a, b):
    return jnp.dot(a, b, preferred_element_type=F32)


def _dot_nt(a, b):
    return lax.dot_general(a, b, (((1,), (1,)), ((), ())), preferred_element_type=F32)


def _dot_tn(a, b):
    return lax.dot_general(a, b, (((0,), (0,)), ((), ())), preferred_element_type=F32)


_GELU_C = math.sqrt(2.0 / math.pi)


def _gelu(x):
    return 0.5 * x * (1.0 + jnp.tanh(_GELU_C * (x + 0.044715 * (x * x * x))))


def _gelu_and_grad(x):
    t = jnp.tanh(_GELU_C * (x + 0.044715 * (x * x * x)))
    g = 0.5 * x * (1.0 + t)
    dg = 0.5 * (1.0 + t) + 0.5 * x * (1.0 - t * t) * (_GELU_C * (1.0 + 3.0 * 0.044715 * (x * x)))
    return g, dg


def _sigmoid(x):
    return 0.5 * jnp.tanh(0.5 * x) + 0.5


def _rms(x):
    return lax.rsqrt(jnp.mean(x * x, axis=-1, keepdims=True) + EPS)


def _rms_bwd(dxn, xn, r):
    return r * (dxn - xn * jnp.mean(dxn * xn, axis=-1, keepdims=True))


def _rowsum(x):
    return jnp.sum(x, axis=0, keepdims=True)


def _fetch_once(pairs, sems):
    copies = [pltpu.make_async_copy(src, dst, sems.at[k]) for k, (src, dst) in enumerate(pairs)]
    for cp in copies:
        cp.start()
    for cp in copies:
        cp.wait()


def _s5_disc(are, aim, ldt, br, bi):
    dt = jnp.exp(ldt)
    mag = jnp.exp(dt * are)
    abr = mag * jnp.cos(dt * aim)
    abi = mag * jnp.sin(dt * aim)
    den = are * are + aim * aim
    nr = abr - 1.0
    ni = abi
    fr = (nr * are + ni * aim) / den
    fi = (ni * are - nr * aim) / den
    return abr, abi, fr * br - fi * bi, fr * bi + fi * br


def _s5_params_fwd(are, aim, ldt, br, bi):
    def body(are_ref, aim_ref, ldt_ref, br_ref, bi_ref, o0, o1, o2, o3):
        outs = _s5_disc(are_ref[...], aim_ref[...], ldt_ref[...], br_ref[...], bi_ref[...])
        for o, v in zip((o0, o1, o2, o3), outs):
            o[...] = v
    shp = are.shape
    return pl.pallas_call(body, name="s5_params_fwd", grid=(1,), in_specs=[_full(shp)] * 5, out_specs=[_full(shp)] * 4,
                          out_shape=[_sds(shp)] * 4)(*_in_hbm([are, aim, ldt, br, bi]))


def _s5_params_bwd(are, aim, ldt, br, bi, dabr, dabi, dbr, dbi):
    def body(are_ref, aim_ref, ldt_ref, br_ref, bi_ref, c0, c1, c2, c3, o0, o1, o2, o3, o4):
        prim = (are_ref[...], aim_ref[...], ldt_ref[...], br_ref[...], bi_ref[...])
        _, vjp = jax.vjp(_s5_disc, *prim)
        outs = vjp((c0[...], c1[...], c2[...], c3[...]))
        for o, v in zip((o0, o1, o2, o3, o4), outs):
            o[...] = v
    shp = are.shape
    return pl.pallas_call(body, name="s5_params_bwd", grid=(1,), in_specs=[_full(shp)] * 9, out_specs=[_full(shp)] * 5,
                          out_shape=[_sds(shp)] * 5)(*_in_hbm([are, aim, ldt, br, bi, dabr, dabi, dbr, dbi]))


def _blockdiag(m_t):
    m = m_t.reshape(SSM_BLK, 8, SSM_H, 1, SSM_P)
    eye = jnp.eye(8, dtype=bool).reshape(1, 8, 1, 8, 1)
    return jnp.where(eye, m, jnp.zeros((), m_t.dtype)).reshape(SSM_BLK, 8 * SSM_H, 8 * SSM_P)


def _unblockdiag(pc):
    m = pc.reshape(SSM_BLK, 8, SSM_H, 8, SSM_P)
    return jnp.einsum("jghgp->jghp", m).reshape(SSM_G * SSM_H, SSM_P)


def _in_fwd(x, g_mix, w_in_t, tm, after=()):
    S = x.shape[0]

    def body(x_ref, g_ref, w_ref, h_ref, us_ref, uv_ref, gl_ref):
        xv = x_ref[...]
        h = (xv * _rms(xv) * g_ref[...]).astype(MXU)
        h_ref[...] = h
        us_ref[...] = _dot_nt(h, w_ref[0:SSM_W, :])
        uv_ref[...] = _dot_nt(h, w_ref[SSM_W:SSM_W + 2 * SGU_W, :])
        gl_ref[...] = _dot_nt(h, w_ref[SSM_W + 2 * SGU_W:, :])

    row = lambda n: pl.BlockSpec((tm, n), lambda i: (i, 0))
    return pl.pallas_call(
        _behind(body, 3, after), name="in_fwd", grid=(S // tm,),
        in_specs=[row(D_MODEL), _full((1, D_MODEL)), _full(w_in_t.shape)] + [_ANY] * len(after),
        out_specs=[row(D_MODEL), row(SSM_W), row(2 * SGU_W), row(2 * D_MODEL)],
        out_shape=[_sds((S, D_MODEL), MXU), _sds((S, SSM_W)), _sds((S, 2 * SGU_W)), _sds((S, 2 * D_MODEL))],
        compiler_params=_cp("parallel"),
    )(*_in_hbm([x, g_mix, w_in_t]), *after)


def _scan_tables(ar, ai, reverse):
    n = ar.shape[-1]
    def mul(p, q):
        return p[0] * q[0] - p[1] * q[1], p[0] * q[1] + p[1] * q[0]
    a1 = (ar, ai)
    a2 = mul(a1, a1)
    a3 = mul(a2, a1)
    a4 = mul(a2, a2)
    a5 = mul(a4, a1)
    a6 = mul(a4, a2)
    a7 = mul(a4, a3)
    a8 = mul(a4, a4)
    pw = (a1, a2, a3, a4, a5, a6, a7, a8)
    rows = lax.broadcasted_iota(jnp.int32, (8, n), 0)
    tabs = []
    for s, a in ((1, a1), (2, a2), (4, a4)):
        keep = (rows + s <= 7) if reverse else (rows >= s)
        for comp in a:
            tabs.append(jnp.where(keep, jnp.broadcast_to(comp, (8, n)), 0.0))
    for c in range(2):
        q = jnp.zeros((8, n), F32)
        for r in range(8):
            e = (8 - r) if reverse else (r + 1)
            q = jnp.where(rows == r, jnp.broadcast_to(pw[e - 1][c], (8, n)), q)
        tabs.append(q)
    return tabs


def _scan_group(xr, xi, tab_ref, cr, ci, reverse):
    for t, s in enumerate((1, 2, 4)):
        pr = tab_ref[2 * t]
        pi = tab_ref[2 * t + 1]
        sh = (8 - s) if reverse else s
        sr = pltpu.roll(xr, sh, 0)
        si = pltpu.roll(xi, sh, 0)
        xr, xi = xr + pr * sr - pi * si, xi + pr * si + pi * sr
    qr = tab_ref[6]
    qi = tab_ref[7]
    return xr + qr * cr - qi * ci, xi + qr * ci + qi * cr


def _runs_load(src_ref, dst_ref, run):
    for i in range(run):
        dst_ref[8 * i:8 * i + 8, :] = src_ref[pl.ds(i, 8, stride=run), :]


def _runs_store(val, dst_ref, run):
    for i in range(run):
        dst_ref[pl.ds(i, 8, stride=run), :] = val[8 * i:8 * i + 8, :]


def _cpow2(ar, ai, log2n):
    for _ in range(log2n):
        ar, ai = ar * ar - ai * ai, 2.0 * ar * ai
    return ar, ai


def _s5_fwd(us, abar_re, abar_im, b_re, b_im, c_re, c_im, d_skip, tm, after=()):
    S = us.shape[0]
    nt = S // tm
    w = 8 * SSM_P
    run = tm // 8
    assert run & (run - 1) == 0

    def body(us_ref, ar_ref, ai_ref, br_ref, bi_ref, cr_ref, ci_ref, d_ref, str_ref, sti_ref, ys_ref,
             tab_ref, car_ref, up_ref):
        i = pl.program_id(1)

        @pl.when(i == 0)
        def _():
            car_ref[...] = jnp.zeros_like(car_ref)
            for k, t in enumerate(_scan_tables(*_cpow2(ar_ref[...], ai_ref[...], run.bit_length() - 1), False)):
                tab_ref[k] = t

        _runs_load(us_ref, up_ref, run)
        ub = up_ref[...].astype(MXU)
        str_ref[...] = _dot(ub, br_ref[0])
        sti_ref[...] = _dot(ub, bi_ref[0])
        ar = jnp.broadcast_to(ar_ref[...], (8, w))
        ai = jnp.broadcast_to(ai_ref[...], (8, w))

        def advance(k, state):
            r0 = pl.multiple_of(k * 8, 8)
            sr, si = state
            return (ar * sr - ai * si + str_ref[pl.ds(r0, 8), :], ar * si + ai * sr + sti_ref[pl.ds(r0, 8), :])

        def emit(k, state):
            r0 = pl.multiple_of(k * 8, 8)
            sr, si = advance(k, state)
            str_ref[pl.ds(r0, 8), :] = sr
            sti_ref[pl.ds(r0, 8), :] = si
            return sr, si

        zero = jnp.zeros((8, w), F32)
        er, ei = lax.fori_loop(0, run, advance, (zero, zero))
        cr, ci = car_ref[0:1, :], car_ref[1:2, :]
        tr, ti = _scan_group(er, ei, tab_ref, cr, ci, False)
        r8 = lax.broadcasted_iota(jnp.int32, (8, w), 0)
        start = (jnp.where(r8 == 0, cr, pltpu.roll(tr, 1, 0)), jnp.where(r8 == 0, ci, pltpu.roll(ti, 1, 0)))
        car_ref[0:1, :] = tr[7:8, :]
        car_ref[1:2, :] = ti[7:8, :]
        lax.fori_loop(0, run, emit, start)
        y = _dot_nt(str_ref[...].astype(MXU), cr_ref[0]) - _dot_nt(sti_ref[...].astype(MXU), ci_ref[0])
        _runs_store(y, ys_ref, run)
        ys_ref[...] += d_ref[...] * us_ref[...]

    blk = lambda: pl.BlockSpec((1, 8 * SSM_H, w), lambda j, i: (j, 0, 0))
    return pl.pallas_call(
        _behind(body, 8, after), name="s5_fwd", grid=(SSM_BLK, nt),
        in_specs=[pl.BlockSpec((tm, LANES), lambda j, i: (i, j)),
                  pl.BlockSpec((1, w), lambda j, i: (0, j)), pl.BlockSpec((1, w), lambda j, i: (0, j)),
                  blk(), blk(), blk(), blk(),
                  pl.BlockSpec((1, LANES), lambda j, i: (0, j))] + [_ANY] * len(after),
        out_specs=[pl.BlockSpec((tm, w), lambda j, i: (i, j)), pl.BlockSpec((tm, w), lambda j, i: (i, j)),
                   pl.BlockSpec((tm, LANES), lambda j, i: (i, j))],
        out_shape=[_sds((S, SSM_BLK * w)), _sds((S, SSM_BLK * w)), _sds((S, SSM_W))],
        scratch_shapes=[pltpu.VMEM((8, 8, w), F32), pltpu.VMEM((8, w), F32), pltpu.VMEM((tm, LANES), F32)],
        compiler_params=_cp("parallel", "arbitrary"),
    )(*_in_hbm([us, abar_re, abar_im, b_re, b_im, c_re, c_im, d_skip]), *after)


def _group_halves(vp):
    first = lax.broadcasted_iota(jnp.int32, vp.shape, 1) < SGU_D
    zero = jnp.zeros((), vp.dtype)
    return jnp.where(first, vp, zero), jnp.where(first, zero, vp)


def _sgu_mix(vnb, wcat_ref):
    outs = []
    for q in range(SGU_G // 2):
        lo, hi = _group_halves(vnb[:, LANES * q:LANES * (q + 1)])
        outs.append(_dot(wcat_ref[q], jnp.concatenate([lo, hi], axis=0)))
    return jnp.concatenate(outs, axis=1)


def _mix_fwd(x, ys, uv, gl, w_glu, b_glu, w_pa, g_sgu, ws, bias_s, w_pb, w_out, g_ffn, tm):
    S = x.shape[0]

    def body(x_ref, ys_ref, uv_ref, gl_ref, wglu_ref, bglu_ref, wpa_ref, gs_ref, ws_ref, bias_ref, wpb_ref, wout_ref,
             gf_ref, yg_ref, yap_ref, sg_ref, ya_ref, yb_ref, m_ref, x1_ref, h2_ref):
        yg = _gelu(ys_ref[...])
        ygb = yg.astype(MXU)
        yg_ref[...] = ygb
        z = _dot(ygb, wglu_ref[...]) + bglu_ref[...]
        yapb = (yg * _sigmoid(z)).astype(MXU)
        yap_ref[...] = yapb
        ya = _dot(yapb, wpa_ref[...])
        ya_ref[...] = ya

        uvg = _gelu(uv_ref[...])
        u2 = uvg[:, :SGU_W]
        v2 = uvg[:, SGU_W:]
        vnb = (v2 * _rms(v2) * gs_ref[...]).astype(MXU)
        for c in range(tm // CHUNK):
            rs = slice(c * CHUNK, (c + 1) * CHUNK)
            mixed = _sgu_mix(vnb[rs], ws_ref) + bias_ref[...]
            sg_ref[rs, :] = (u2[rs] * mixed).astype(MXU)
        yb = _dot(sg_ref[...], wpb_ref[...])
        yb_ref[...] = yb

        glv = gl_ref[...]
        m = _sigmoid(glv[:, :D_MODEL]) * ya + _sigmoid(glv[:, D_MODEL:]) * yb
        mb = m.astype(MXU)
        m_ref[...] = mb
        x1 = x_ref[...] + _dot(mb, wout_ref[...])
        x1_ref[...] = x1
        h2_ref[...] = (x1 * _rms(x1) * gf_ref[...]).astype(MXU)

    row = lambda n: pl.BlockSpec((tm, n), lambda i: (i, 0))
    return pl.pallas_call(
        body, name="mix_fwd", grid=(S // tm,),
        in_specs=[row(D_MODEL), row(SSM_W), row(2 * SGU_W), row(2 * D_MODEL),
                  _full(w_glu.shape), _full(b_glu.shape), _full(w_pa.shape), _full(g_sgu.shape), _full(ws.shape),
                  _full(bias_s.shape), _full(w_pb.shape), _full(w_out.shape), _full(g_ffn.shape)],
        out_specs=[row(SSM_W), row(SSM_W), row(SGU_W), row(D_MODEL), row(D_MODEL), row(D_MODEL), row(D_MODEL),
                   row(D_MODEL)],
        out_shape=[_sds((S, SSM_W), MXU), _sds((S, SSM_W), MXU), _sds((S, SGU_W), MXU), _sds((S, D_MODEL)),
                   _sds((S, D_MODEL)), _sds((S, D_MODEL), MXU), _sds((S, D_MODEL)), _sds((S, D_MODEL), MXU)],
        compiler_params=_cp("parallel"),
    )(*_in_hbm([x, ys, uv, gl, w_glu, b_glu, w_pa, g_sgu, ws, bias_s, w_pb, w_out, g_ffn]))


def _causal_conv3(u, prev8, cw, cb):
    tm = u.shape[0]
    w0, w1, w2 = cw[0:1], cw[1:2], cw[2:3]
    body = w0 * pltpu.roll(u, 2, 0) + w1 * pltpu.roll(u, 1, 0) + w2 * u + cb
    u8 = u[0:8, :]
    r8 = lax.broadcasted_iota(jnp.int32, u8.shape, 0)
    t1 = prev8[7:8, :]
    t0 = prev8[6:7, :]
    s1 = jnp.where(r8 == 0, t1, pltpu.roll(u8, 1, 0))
    s2 = jnp.where(r8 == 0, t0, jnp.where(r8 == 1, t1, pltpu.roll(u8, 2, 0)))
    first = w0 * s2 + w1 * s1 + w2 * u8 + cb
    return jnp.concatenate([first, body[8:tm, :]], axis=0)


def _causal_conv3_adjoint(d, next8, cw):
    tm = d.shape[0]
    w0, w1, w2 = cw[0:1], cw[1:2], cw[2:3]
    n1 = pltpu.roll(d, tm - 1, 0)
    n2 = pltpu.roll(d, tm - 2, 0)
    body = w2 * d + w1 * n1 + w0 * n2
    d8 = d[tm - 8:tm, :]
    r8 = lax.broadcasted_iota(jnp.int32, d8.shape, 0)
    h0 = next8[0:1, :]
    h1 = next8[1:2, :]
    m1 = jnp.where(r8 == 7, h0, pltpu.roll(d8, 7, 0))
    m2 = jnp.where(r8 == 6, h0, jnp.where(r8 == 7, h1, pltpu.roll(d8, 6, 0)))
    last = w2 * d8 + w1 * m1 + w0 * m2
    out = jnp.concatenate([body[0:tm - 8, :], last], axis=0)
    return out, n1, n2, h0 - d[0:1, :], h1 - d[1:2, :]


def _ffn_fwd(h2, x1, tgt, w_up, conv_w, conv_b, w_down, g_final, tm):
    S = h2.shape[0]
    nt = S // tm
    ncb = FF_NCB

    def body(h2_ref, wup_hbm, cwa_ref, cwb_ref, cba_ref, cbb_ref, wd_hbm, x1_ref, gf_ref, tgt_ref,
             up_ref, ab_ref, ff_ref, dx2_ref, dx2b_ref, loss_ref, dgf_ref, acc_ref, tail_ref, wup_ref, wdn_ref, wsem):
        i = pl.program_id(0)
        cb = pl.program_id(1)

        @pl.when(i == 0)
        def _():
            tail_ref[cb] = jnp.zeros((2, 8, FF_CW), F32)

        @pl.when(jnp.logical_and(i == 0, cb == 0))
        def _():
            loss_ref[...] = jnp.zeros_like(loss_ref)
            dgf_ref[...] = jnp.zeros_like(dgf_ref)
            _fetch_once([(wup_hbm, wup_ref), (wd_hbm, wdn_ref)], wsem)

        h2v = h2_ref[...]
        ua = _dot_nt(h2v, wup_ref[cb])
        ub = _dot_nt(h2v, wup_ref[ncb + cb])
        up_ref[0, 0] = ua.astype(MXU)
        up_ref[1, 0] = ub.astype(MXU)
        a = _causal_conv3(ua, tail_ref[cb, 0], cwa_ref[0], cba_ref[0])
        b = _causal_conv3(ub, tail_ref[cb, 1], cwb_ref[0], cbb_ref[0])
        tail_ref[cb, 0] = ua[tm - 8:tm, :]
        tail_ref[cb, 1] = ub[tm - 8:tm, :]
        ab_ref[0, 0] = a
        ab_ref[1, 0] = b
        ffb = (a * _sigmoid(a) * b).astype(MXU)
        ff_ref[0] = ffb
        contrib = _dot(ffb, wdn_ref[pl.ds(pl.multiple_of(cb * FF_CW, FF_CW), FF_CW), :])

        @pl.when(cb == 0)
        def _():
            acc_ref[...] = contrib

        @pl.when(cb > 0)
        def _():
            acc_ref[...] += contrib

        @pl.when(cb == ncb - 1)
        def _():
            x2 = x1_ref[...] + acc_ref[...]
            r = _rms(x2)
            xn = x2 * r
            g = gf_ref[...]
            diff = xn * g - tgt_ref[...]
            loss_ref[...] += (0.5 / D_MODEL) * jnp.sum(diff * diff)
            dy = diff * (1.0 / D_MODEL)
            dgf_ref[...] += _rowsum(dy * xn)
            dx2 = _rms_bwd(dy * g, xn, r)
            dx2_ref[...] = dx2
            dx2b_ref[...] = dx2.astype(MXU)

    row = lambda n: pl.BlockSpec((tm, n), lambda i, c: (i, 0))
    gate = lambda r: pl.BlockSpec((1, r, FF_CW), lambda i, c: (c, 0, 0))
    lin = lambda r: pl.BlockSpec((1, r, FF_CW), lambda i, c: (ncb + c, 0, 0))
    return pl.pallas_call(
        body, name="ffn_fwd", grid=(nt, ncb),
        in_specs=[row(D_MODEL), _ANY, gate(3), lin(3), gate(1), lin(1), _ANY,
                  row(D_MODEL), _full((1, D_MODEL)), row(D_MODEL)],
        out_specs=[pl.BlockSpec((2, 1, tm, FF_CW), lambda i, c: (0, c, i, 0)),
                   pl.BlockSpec((2, 1, tm, FF_CW), lambda i, c: (0, c, i, 0)),
                   pl.BlockSpec((1, tm, FF_CW), lambda i, c: (c, i, 0)),
                   row(D_MODEL), row(D_MODEL), _full((1, LANES)), _full((1, D_MODEL))],
        out_shape=[_sds((2, ncb, S, FF_CW), MXU), _sds((2, ncb, S, FF_CW)), _sds((ncb, S, FF_CW), MXU),
                   _sds((S, D_MODEL)), _sds((S, D_MODEL), MXU), _sds((1, LANES)), _sds((1, D_MODEL))],
        scratch_shapes=[pltpu.VMEM((tm, D_MODEL), F32), pltpu.VMEM((ncb, 2, 8, FF_CW), F32),
                        pltpu.VMEM(w_up.shape, w_up.dtype), pltpu.VMEM(w_down.shape, w_down.dtype),
                        pltpu.SemaphoreType.DMA((2,))],
        compiler_params=pltpu.CompilerParams(dimension_semantics=("arbitrary", "arbitrary"),
                                             vmem_limit_bytes=FFN_VMEM_LIMIT),
    )(*_in_hbm([h2, w_up, conv_w, conv_w, conv_b, conv_b, w_down, x1, g_final, tgt]))


def _ffn_bwd(dx2, up, ab, x1, w_up, conv_w, w_down, g_ffn, tm):
    S = dx2.shape[0]
    nt = S // tm
    ncb = FF_NCB

    def body(dx2_ref, up_ref, ab_ref, cwa_ref, cwb_ref, wd_hbm, wup_hbm,
             x1_ref, g_ref, dup_ref, dx1_ref, dx1b_ref, dconv_ref, dg_ref, acc_ref, head_ref, wup_ref, wdn_ref, wsem):
        i = pl.program_id(0)
        cb = pl.program_id(1)

        @pl.when(i == 0)
        def _():
            head_ref[cb] = jnp.zeros((2, 8, FF_CW), F32)
            dconv_ref[cb] = jnp.zeros((8, FF_CW), F32)
            dconv_ref[ncb + cb] = jnp.zeros((8, FF_CW), F32)

        @pl.when(jnp.logical_and(i == 0, cb == 0))
        def _():
            dg_ref[...] = jnp.zeros_like(dg_ref)
            _fetch_once([(wup_hbm, wup_ref), (wd_hbm, wdn_ref)], wsem)

        dff = _dot_nt(dx2_ref[...].astype(MXU), wdn_ref[pl.ds(pl.multiple_of(cb * FF_CW, FF_CW), FF_CW), :])
        a = ab_ref[0, 0]
        b = ab_ref[1, 0]
        sa = _sigmoid(a)
        silu = a * sa
        da = (dff * b) * (sa + silu * (1.0 - sa))
        db = dff * silu
        dps = []
        for half, slot, d, cw_ref in ((0, cb, da, cwa_ref), (1, ncb + cb, db, cwb_ref)):
            dp, n1, n2, fix0, fix1 = _causal_conv3_adjoint(d, head_ref[cb, half], cw_ref[0])
            head_ref[cb, half] = d[0:8, :]
            dpb16 = dp.astype(MXU)
            dup_ref[half, 0] = dpb16
            dps.append(dpb16)
            u = up_ref[half, 0].astype(F32)
            u_last = u[tm - 1:tm, :]
            dconv_ref[slot, 0:1, :] += _rowsum(n2 * u) + fix0 * u[tm - 2:tm - 1, :] + fix1 * u_last
            dconv_ref[slot, 1:2, :] += _rowsum(n1 * u) + fix0 * u_last
            dconv_ref[slot, 2:3, :] += _rowsum(d * u)
            dconv_ref[slot, 3:4, :] += _rowsum(d)
        contrib = _dot(dps[0], wup_ref[cb]) + _dot(dps[1], wup_ref[ncb + cb])

        @pl.when(cb == 0)
        def _():
            acc_ref[...] = contrib

        @pl.when(cb > 0)
        def _():
            acc_ref[...] += contrib

        @pl.when(cb == ncb - 1)
        def _():
            x1v = x1_ref[...]
            r = _rms(x1v)
            xn = x1v * r
            dh2 = acc_ref[...]
            dg_ref[...] += _rowsum(dh2 * xn)
            dx1 = dx2_ref[...] + _rms_bwd(dh2 * g_ref[...], xn, r)
            dx1_ref[...] = dx1
            dx1b_ref[...] = dx1.astype(MXU)

    row = lambda n: pl.BlockSpec((tm, n), lambda i, c: (nt - 1 - i, 0))
    colb = lambda: pl.BlockSpec((2, 1, tm, FF_CW), lambda i, c: (0, c, nt - 1 - i, 0))
    gate = lambda r: pl.BlockSpec((1, r, FF_CW), lambda i, c: (c, 0, 0))
    lin = lambda r: pl.BlockSpec((1, r, FF_CW), lambda i, c: (ncb + c, 0, 0))
    return pl.pallas_call(
        body, name="ffn_bwd", grid=(nt, ncb),
        in_specs=[row(D_MODEL), colb(), colb(), gate(3), lin(3), _ANY, _ANY, row(D_MODEL), _full((1, D_MODEL))],
        out_specs=[colb(), row(D_MODEL), row(D_MODEL), _full((2 * ncb, 8, FF_CW)), _full((1, D_MODEL))],
        out_shape=[_sds((2, ncb, S, FF_CW), MXU), _sds((S, D_MODEL)), _sds((S, D_MODEL), MXU), _sds((2 * ncb, 8, FF_CW)),
                   _sds((1, D_MODEL))],
        scratch_shapes=[pltpu.VMEM((tm, D_MODEL), F32), pltpu.VMEM((ncb, 2, 8, FF_CW), F32),
                        pltpu.VMEM(w_up.shape, w_up.dtype), pltpu.VMEM(w_down.shape, w_down.dtype),
                        pltpu.SemaphoreType.DMA((2,))],
        compiler_params=pltpu.CompilerParams(dimension_semantics=("arbitrary", "arbitrary"),
                                             vmem_limit_bytes=FFN_VMEM_LIMIT),
    )(*_in_hbm([dx2, up, ab, conv_w, conv_w, w_down, w_up, x1, g_ffn]))


def _mix_bwd(dx1, gl, ya, yb, ys, uv, w_out, w_pa, w_pb, w_glu, b_glu, g_sgu, ws, ws_t, bias_s, tm, after=()):
    S = dx1.shape[0]

    def body(dx1_ref, gl_ref, ya_ref, yb_ref, ys_ref, uv_ref, wout_ref, wpa_ref, wpb_ref, wglu_ref, bglu_ref, gs_ref,
             ws_ref, wst_ref, bias_ref,
             dgl_ref, dya_ref, dyb_ref, dz_ref, dys_ref, duv_ref, dbglu_ref, dgs_ref, dws_ref, dbs_ref,
             du2_ref, dvn_ref):
        i = pl.program_id(0)

        @pl.when(i == 0)
        def _():
            dbglu_ref[...] = jnp.zeros_like(dbglu_ref)
            dgs_ref[...] = jnp.zeros_like(dgs_ref)
            dws_ref[...] = jnp.zeros_like(dws_ref)
            dbs_ref[...] = jnp.zeros_like(dbs_ref)

        dm = _dot_nt(dx1_ref[...].astype(MXU), wout_ref[...])
        glv = gl_ref[...]
        ga = _sigmoid(glv[:, :D_MODEL])
        gb = _sigmoid(glv[:, D_MODEL:])
        dgl_ref[:, :D_MODEL] = (dm * ya_ref[...] * ga * (1.0 - ga)).astype(MXU)
        dgl_ref[:, D_MODEL:] = (dm * yb_ref[...] * gb * (1.0 - gb)).astype(MXU)
        dyab = (dm * ga).astype(MXU)
        dybb = (dm * gb).astype(MXU)
        dya_ref[...] = dyab
        dyb_ref[...] = dybb

        dyap = _dot_nt(dyab, wpa_ref[...])
        yg, dgelu = _gelu_and_grad(ys_ref[...])
        sz = _sigmoid(_dot(yg.astype(MXU), wglu_ref[...]) + bglu_ref[...])
        dz = dyap * yg * sz * (1.0 - sz)
        dzb = dz.astype(MXU)
        dz_ref[...] = dzb
        dbglu_ref[...] += _rowsum(dz)
        dys_ref[...] = (dyap * sz + _dot_nt(dzb, wglu_ref[...])) * dgelu

        dsg = _dot_nt(dybb, wpb_ref[...])
        uvg, duvg = _gelu_and_grad(uv_ref[...])
        u2 = uvg[:, :SGU_W]
        v2 = uvg[:, SGU_W:]
        rv = _rms(v2)
        vhat = v2 * rv
        gs = gs_ref[...]
        vnb = (vhat * gs).astype(MXU)
        tril = (lax.broadcasted_iota(jnp.int32, (CHUNK, CHUNK), 0)
                >= lax.broadcasted_iota(jnp.int32, (CHUNK, CHUNK), 1))
        for c in range(tm // CHUNK):
            rs = slice(c * CHUNK, (c + 1) * CHUNK)
            vc = vnb[rs]
            mixed = _sgu_mix(vc, ws_ref) + bias_ref[...]
            dsg_c = dsg[rs]
            du2_ref[rs, :] = dsg_c * mixed
            dmx = dsg_c * u2[rs]
            dbs_ref[...] += dmx
            dmb = dmx.astype(MXU)
            dvn_ref[rs, :] = _sgu_mix(dmb, wst_ref)
            for q in range(SGU_G // 2):
                lanes = slice(LANES * q, LANES * (q + 1))
                for j, part in enumerate(_group_halves(dmb[:, lanes])):
                    dws_ref[2 * q + j] += jnp.where(tril, _dot_nt(part, vc[:, lanes]), 0.0)
        dvn = dvn_ref[...]
        dgs_ref[...] += _rowsum(dvn * vhat)
        dv2 = _rms_bwd(dvn * gs, vhat, rv)
        duv_ref[:, :SGU_W] = (du2_ref[...] * duvg[:, :SGU_W]).astype(MXU)
        duv_ref[:, SGU_W:] = (dv2 * duvg[:, SGU_W:]).astype(MXU)

    row = lambda n: pl.BlockSpec((tm, n), lambda i: (i, 0))
    return pl.pallas_call(
        _behind(body, 15, after), name="mix_bwd", grid=(S // tm,),
        in_specs=[row(D_MODEL), row(2 * D_MODEL), row(D_MODEL), row(D_MODEL), row(SSM_W), row(2 * SGU_W),
                  _full(w_out.shape), _full(w_pa.shape), _full(w_pb.shape), _full(w_glu.shape), _full(b_glu.shape),
                  _full(g_sgu.shape), _full(ws.shape), _full(ws_t.shape), _full(bias_s.shape)] + [_ANY] * len(after),
        out_specs=[row(2 * D_MODEL), row(D_MODEL), row(D_MODEL), row(SSM_W), row(SSM_W), row(2 * SGU_W),
                   _full((1, SSM_W)), _full((1, SGU_W)), _full((SGU_G, CHUNK, CHUNK)), _full((CHUNK, SGU_W))],
        out_shape=[_sds((S, 2 * D_MODEL), MXU), _sds((S, D_MODEL), MXU), _sds((S, D_MODEL), MXU), _sds((S, SSM_W), MXU),
                   _sds((S, SSM_W)), _sds((S, 2 * SGU_W), MXU),
                   _sds((1, SSM_W)), _sds((1, SGU_W)), _sds((SGU_G, CHUNK, CHUNK)), _sds((CHUNK, SGU_W))],
        scratch_shapes=[pltpu.VMEM((tm, SGU_W), F32), pltpu.VMEM((tm, SGU_W), F32)],
        compiler_params=_cp("arbitrary"),
    )(*_in_hbm([dx1, gl, ya, yb, ys, uv, w_out, w_pa, w_pb, w_glu, b_glu, g_sgu, ws, ws_t, bias_s]), *after)


def _s5_bwd(dys, us, st_re, st_im, abar_re, abar_im, b_re, b_im, c_re, c_im, d_skip, tm, after=()):
    S = us.shape[0]
    nt = S // tm
    w = 8 * SSM_P
    hb = tm // 8
    run = tm // 8
    assert run & (run - 1) == 0

    def body(dys_ref, us_ref, str_ref, sti_ref, hr_ref, hi_ref, ar_ref, ai_ref, br_ref, bi_ref, cr_ref, ci_ref, d_ref,
             dus_ref, dab_ref, dd_ref, dbr_ref, dbi_ref, dcr_ref, dci_ref,
             tab_ref, car_ref, gr_ref, gi_ref, dyp_ref, up_ref, dun_ref):
        i = pl.program_id(1)
        ri = nt - 1 - i

        @pl.when(i == 0)
        def _():
            car_ref[...] = jnp.zeros_like(car_ref)
            for k, t in enumerate(_scan_tables(*_cpow2(ar_ref[...], -ai_ref[...], run.bit_length() - 1), True)):
                tab_ref[k] = t
            for r in (dab_ref, dd_ref, dbr_ref, dbi_ref, dcr_ref, dci_ref):
                r[...] = jnp.zeros_like(r)

        _runs_load(dys_ref, dyp_ref, run)
        _runs_load(us_ref, up_ref, run)
        dyb = dyp_ref[...].astype(MXU)
        gr_ref[...] = _dot(dyb, cr_ref[0])
        gi_ref[...] = -_dot(dyb, ci_ref[0])
        ar = jnp.broadcast_to(ar_ref[...], (8, w))
        ai = jnp.broadcast_to(-ai_ref[...], (8, w))

        def advance(kk, state):
            r0 = pl.multiple_of((run - 1 - kk) * 8, 8)
            gr, gi = state
            return (ar * gr - ai * gi + gr_ref[pl.ds(r0, 8), :], ar * gi + ai * gr + gi_ref[pl.ds(r0, 8), :])

        def emit(kk, state):
            r0 = pl.multiple_of((run - 1 - kk) * 8, 8)
            gr, gi = advance(kk, state)
            gr_ref[pl.ds(r0, 8), :] = gr
            gi_ref[pl.ds(r0, 8), :] = gi
            return gr, gi

        zero = jnp.zeros((8, w), F32)
        er, ei = lax.fori_loop(0, run, advance, (zero, zero))
        cr, ci = car_ref[0:1, :], car_ref[1:2, :]
        tr, ti = _scan_group(er, ei, tab_ref, cr, ci, True)
        r8 = lax.broadcasted_iota(jnp.int32, (8, w), 0)
        start = (jnp.where(r8 == 7, cr, pltpu.roll(tr, 7, 0)), jnp.where(r8 == 7, ci, pltpu.roll(ti, 7, 0)))
        car_ref[0:1, :] = tr[0:1, :]
        car_ref[1:2, :] = ti[0:1, :]
        lax.fori_loop(0, run, emit, start)

        gsr = gr_ref[...]
        gsi = gi_ref[...]
        sr = str_ref[...]
        si = sti_ref[...]
        first = ri == 0

        def previous(s, halo_ref):
            head = jnp.where(r8 == 0, jnp.where(first, 0.0, halo_ref[7:8, :]), pltpu.roll(s[tm - 8:tm, :], 1, 0))
            return jnp.concatenate([head, s[0:tm - 8, :]], axis=0)

        spr = previous(sr, hr_ref)
        spi = previous(si, hi_ref)
        dab_ref[0, 0:1, :] += _rowsum(gsr * spr + gsi * spi)
        dab_ref[0, 1:2, :] += _rowsum(gsi * spr - gsr * spi)

        gbr = gsr.astype(MXU)
        gbi = gsi.astype(MXU)
        _runs_store(_dot_nt(gbr, br_ref[0]) + _dot_nt(gbi, bi_ref[0]), dun_ref, run)
        dys_v = dys_ref[...]
        dus_ref[...] = (dun_ref[...] + d_ref[...] * dys_v).astype(MXU)
        dd_ref[0, 0:1, :] += _rowsum(dys_v * us_ref[...])
        ub = up_ref[...].astype(MXU)
        dbr_ref[0] += _dot_tn(ub, gbr)
        dbi_ref[0] += _dot_tn(ub, gbi)
        dcr_ref[0] += _dot_tn(dyb, sr.astype(MXU))
        dci_ref[0] -= _dot_tn(dyb, si.astype(MXU))

    blk = lambda: pl.BlockSpec((1, 8 * SSM_H, w), lambda j, i: (j, 0, 0))
    rowl = lambda: pl.BlockSpec((tm, LANES), lambda j, i: (nt - 1 - i, j))
    roww = lambda: pl.BlockSpec((tm, w), lambda j, i: (nt - 1 - i, j))
    halo = lambda: pl.BlockSpec((8, w), lambda j, i: (jnp.maximum((nt - 1 - i) * hb - 1, 0), j))
    return pl.pallas_call(
        _behind(body, 13, after), name="s5_bwd", grid=(SSM_BLK, nt),
        in_specs=[rowl(), rowl(), roww(), roww(), halo(), halo(),
                  pl.BlockSpec((1, w), lambda j, i: (0, j)), pl.BlockSpec((1, w), lambda j, i: (0, j)),
                  blk(), blk(), blk(), blk(),
                  pl.BlockSpec((1, LANES), lambda j, i: (0, j))] + [_ANY] * len(after),
        out_specs=[rowl(),
                   pl.BlockSpec((1, 8, w), lambda j, i: (j, 0, 0)), pl.BlockSpec((1, 8, LANES), lambda j, i: (j, 0, 0)),
                   blk(), blk(), blk(), blk()],
        out_shape=[_sds((S, SSM_W), MXU), _sds((SSM_BLK, 8, w)), _sds((SSM_BLK, 8, LANES)),
                   _sds((SSM_BLK, 8 * SSM_H, w)), _sds((SSM_BLK, 8 * SSM_H, w)),
                   _sds((SSM_BLK, 8 * SSM_H, w)), _sds((SSM_BLK, 8 * SSM_H, w))],
        scratch_shapes=[pltpu.VMEM((8, 8, w), F32), pltpu.VMEM((8, w), F32),
                        pltpu.VMEM((tm, w), F32), pltpu.VMEM((tm, w), F32),
                        pltpu.VMEM((tm, LANES), F32), pltpu.VMEM((tm, LANES), F32), pltpu.VMEM((tm, LANES), F32)],
        compiler_params=_cp("parallel", "arbitrary"),
    )(*_in_hbm([dys, us, st_re, st_im, st_re, st_im, abar_re, abar_im, b_re, b_im, c_re, c_im, d_skip]), *after)


def _in_bwd(dus, duv, dgl, dx1, x, g_mix, w_in, tm, after=()):
    S = x.shape[0]

    def body(dus_ref, duv_ref, dgl_ref, dx1_ref, x_ref, g_ref, w_ref, gx_ref, dg_ref):
        @pl.when(pl.program_id(0) == 0)
        def _():
            dg_ref[...] = jnp.zeros_like(dg_ref)

        dh = (_dot(dus_ref[...], w_ref[0:SSM_W, :])
              + _dot(duv_ref[...], w_ref[SSM_W:SSM_W + 2 * SGU_W, :])
              + _dot(dgl_ref[...], w_ref[SSM_W + 2 * SGU_W:, :]))
        xv = x_ref[...]
        r = _rms(xv)
        xn = xv * r
        dg_ref[...] += _rowsum(dh * xn)
        gx_ref[...] = dx1_ref[...] + _rms_bwd(dh * g_ref[...], xn, r)

    row = lambda n: pl.BlockSpec((tm, n), lambda i: (i, 0))
    return pl.pallas_call(
        _behind(body, 7, after), name="in_bwd", grid=(S // tm,),
        in_specs=[row(SSM_W), row(2 * SGU_W), row(2 * D_MODEL), row(D_MODEL), row(D_MODEL), _full((1, D_MODEL)),
                  _full(w_in.shape)] + [_ANY] * len(after),
        out_specs=[row(D_MODEL), _full((1, D_MODEL))],
        out_shape=[_sds((S, D_MODEL)), _sds((1, D_MODEL))],
        compiler_params=_cp("arbitrary"),
    )(*_in_hbm([dus, duv, dgl, dx1, x, g_mix, w_in]), *after)


def _wgrad_split(a, b, nsplit, tk, name):
    S, K = a.shape
    N = b.shape[1]
    c = N // nsplit

    def body(a_ref, b_ref, o_ref):
        prod = _dot_tn(a_ref[...], b_ref[...])
        for d in range(nsplit):
            o_ref[d] = prod[:, c * d:c * (d + 1)].astype(MXU)

    return pl.pallas_call(
        body, name=name, grid=(K // tk,),
        in_specs=[pl.BlockSpec((S, tk), lambda k: (0, k)), _full((S, N))],
        out_specs=pl.BlockSpec((nsplit, tk, c), lambda k: (0, k, 0)),
        out_shape=_sds((nsplit, K, c), MXU),
        compiler_params=_cp("parallel"),
    )(*_in_hbm([a, b]))


def _wgrad_in_t(dps, h1, name):
    S, K = h1.shape
    cw = 512
    counts = [b.shape[1] // cw for b in dps]
    starts = [sum(counts[:i]) for i in range(len(dps))]
    nblk = sum(counts)

    def body(*refs):
        b_refs = refs[:len(dps)]
        h_ref, o_ref = refs[len(dps)], refs[-1]
        j = pl.program_id(0)
        for b_ref, st, cnt in zip(b_refs, starts, counts):
            @pl.when(jnp.logical_and(j >= st, j < st + cnt))
            def _():
                o_ref[...] = _dot_tn(b_ref[...], h_ref[...]).astype(MXU)

    def src_spec(st, cnt):
        return pl.BlockSpec((S, cw), lambda j: (0, jnp.clip(j - st, 0, cnt - 1)))

    return pl.pallas_call(
        body, name=name, grid=(nblk,),
        in_specs=[src_spec(st, cnt) for st, cnt in zip(starts, counts)] + [_full((S, K))],
        out_specs=pl.BlockSpec((cw, K), lambda j: (j, 0)),
        out_shape=_sds((nblk * cw, K), MXU),
        compiler_params=_cp("arbitrary"),
    )(*_in_hbm([*dps, h1]))


def _wgrad_blk(a3, b3, nblk, a_of, b_of, name):
    S, K = a3.shape[1:]
    N = b3.shape[2]

    def body(a_ref, b_ref, o_ref):
        o_ref[0] = _dot_tn(a_ref[0], b_ref[0]).astype(MXU)

    return pl.pallas_call(
        body, name=name, grid=(nblk,),
        in_specs=[pl.BlockSpec((1, S, K), lambda b: (a_of(b), 0, 0)),
                  pl.BlockSpec((1, S, N), lambda b: (b_of(b), 0, 0))],
        out_specs=pl.BlockSpec((1, K, N), lambda b: (b, 0, 0)),
        out_shape=_sds((nblk, K, N), MXU),
        compiler_params=pltpu.CompilerParams(dimension_semantics=("parallel",), vmem_limit_bytes=WGRAD_VMEM_LIMIT),
    )(*_in_hbm([a3, b3]))


def _assemble_cols(blocks_list, name):
    def body(*refs):
        n = len(blocks_list)
        for b_ref, o_ref in zip(refs[:n], refs[n:]):
            c = b_ref.shape[2]
            for d in range(N_DEV):
                o_ref[:, c * d:c * (d + 1)] = b_ref[d]

    outs = [_sds((b.shape[1], N_DEV * b.shape[2]), b.dtype) for b in blocks_list]
    return pl.pallas_call(
        body, name=name, grid=(1,), in_specs=[_full(b.shape) for b in blocks_list],
        out_specs=[_full(o.shape) for o in outs], out_shape=outs, compiler_params=_cp("arbitrary"),
    )(*_in_hbm(blocks_list))


def _tile(S, want):
    return want if S % want == 0 else S


def _local_step(x, tgt, p, after, mixer_relay, mixer_weights, ffn_weights, grads_out, small_out):
    S = x.shape[0]
    tm = _tile(S, 256)
    tl = _tile(S, 512)

    rep = lambda a: jnp.repeat(a, SSM_H, axis=0)
    are = rep(p["a_re"])
    aim = rep(p["a_im"])
    ldt = jnp.broadcast_to(rep(p["log_dt"].reshape(SSM_G, 1)), are.shape)
    br_t = p["b_re_t"].reshape(are.shape)
    bi_t = p["b_im_t"].reshape(are.shape)
    abr, abi, bbr, bbi = _s5_params_fwd(are, aim, ldt, br_t, bi_t)
    head = lambda a: a.reshape(SSM_G, SSM_H, SSM_P)[:, 0, :].reshape(1, SSM_G * SSM_P)
    abar_re, abar_im = head(abr), head(abi)
    bd_br = _blockdiag(bbr).astype(MXU)
    bd_bi = _blockdiag(bbi).astype(MXU)
    bd_cr = _blockdiag(p["c_re"].reshape(are.shape)).astype(MXU)
    bd_ci = _blockdiag(p["c_im"].reshape(are.shape)).astype(MXU)
    d_skip = p["d_skip"].reshape(1, SSM_W)

    tril = jnp.tril(jnp.ones((CHUNK, CHUNK), dtype=bool))
    ws = jnp.where(tril[None], p["w_s"], 0.0)
    pair = lambda w: w.reshape(SGU_G // 2, 2, CHUNK, CHUNK).transpose(0, 2, 1, 3).reshape(SGU_G // 2, CHUNK, 2 * CHUNK)
    ws_b = pair(ws).astype(MXU)
    ws_t = pair(ws.transpose(0, 2, 1)).astype(MXU)
    bias_s = jnp.repeat(p["b_s"].T, SGU_D, axis=1)

    g_mix = p["g_mix"].reshape(1, D_MODEL)
    g_ffn = p["g_ffn"].reshape(1, D_MODEL)
    g_final = p["g_final"].reshape(1, D_MODEL)
    g_sgu = p["g_sgu"].reshape(1, SGU_W)
    b_glu = p["b_glu"].reshape(1, SSM_W)
    conv_b = p["conv_b"].reshape(2 * FF_NCB, 1, FF_CW)
    tf = _tile(S, 256)
    ts = _tile(S, 1024)

    h1, us, uv, gl = _in_fwd(x, g_mix, p["w_in_t"], tl, after)
    token = mixer_relay(us)
    st_re, st_im, ys = _s5_fwd(us, abar_re, abar_im, bd_br, bd_bi, bd_cr, bd_ci, d_skip, ts, (token,))
    p = dict(p, **mixer_weights(ys))
    yg, yap, sg, ya, yb, m, x1, h2 = _mix_fwd(x, ys, uv, gl, p["w_glu"], b_glu, p["w_proj_a"], g_sgu, ws_b, bias_s,
                                              p["w_proj_b"], p["w_out"], g_ffn, tl)
    w_up, conv_w, w_down = ffn_weights(h2)
    pair_lanes = lambda a: a.reshape(N_DEV // 2, 2, a.shape[1], FF_SHARD).transpose(0, 2, 1, 3).reshape(
        N_DEV // 2, a.shape[1], FF_CW)
    w_up = w_up.reshape(2 * FF_NCB, FF_CW, D_MODEL)
    conv_w = pair_lanes(conv_w)
    up, ab, ff, dx2, dx2b, loss, dg_final = _ffn_fwd(h2, x1, tgt, w_up, conv_w, conv_b, w_down, g_final, tf)

    dup, dx1, dx1b, dconv, dg_ffn = _ffn_bwd(dx2, up, ab, x1, w_up, conv_w, w_down, g_ffn, tf)
    rows8 = lambda g: g.reshape(N_DEV, g.shape[1] // N_DEV, g.shape[2])
    g_up = _wgrad_blk(dup.reshape(2 * FF_NCB, S, FF_CW), h2[None], 2 * FF_NCB, lambda b: b, lambda b: 0,
                      "wgrad_up").reshape(N_DEV, FF_SHARD, D_MODEL)
    g_down = _wgrad_blk(ff, dx2b[None], FF_NCB, lambda b: b, lambda b: 0, "wgrad_down").reshape(
        N_DEV, D_FF // N_DEV, D_MODEL)
    token = grads_out(("w_up", "w_down"), (g_up, g_down))
    dgl, dya, dyb, dz, dys, duv, db_glu, dg_sgu, dws, dbs = _mix_bwd(
        dx1, gl, ya, yb, ys, uv, p["w_out"], p["w_proj_a"], p["w_proj_b"], p["w_glu"], b_glu, g_sgu,
        ws_b, ws_t, bias_s, tm, (token,))
    token = grads_out(("w_glu", "w_proj_a", "w_proj_b", "w_out"),
                      (rows8(_wgrad_split(yg, dz, 1, SSM_W, "wgrad_glu")),
                       _wgrad_split(yap, dya, N_DEV, SSM_W, "wgrad_pa"),
                       _wgrad_split(sg, dyb, N_DEV, SGU_W, "wgrad_pb"),
                       rows8(_wgrad_split(m, dx1b, 1, 512, "wgrad_out"))))
    dus, dab, dd, dbbr, dbbi, dcr, dci = _s5_bwd(dys, us, st_re, st_im, abar_re, abar_im, bd_br, bd_bi, bd_cr, bd_ci,
                                                 d_skip, ts, (token,))
    g_in = _wgrad_in_t([dus, duv, dgl], h1, "wgrad_in")
    token = grads_out(("w_in",), (g_in.reshape(N_DEV, g_in.shape[0] // N_DEV, D_MODEL),))
    grad_x, dg_mix = _in_bwd(dus, duv, dgl, dx1, x, g_mix, p["w_in_t"], tl, (token,))

    spread = lambda v: jnp.repeat(v.reshape(SSM_G, SSM_P), SSM_H, axis=0) * (1.0 / SSM_H)
    dabr = spread(dab[:, 0, :])
    dabi = spread(dab[:, 1, :])
    dare, daim, dldt, dbr_t, dbi_t = _s5_params_bwd(are, aim, ldt, br_t, bi_t, dabr, dabi,
                                                    _unblockdiag(dbbr), _unblockdiag(dbbi))
    fold = lambda a: a.reshape(SSM_G, SSM_H, SSM_P).sum(axis=1)

    grads = {
        "g_mix": dg_mix,
        "a_re": fold(dare), "a_im": fold(daim), "log_dt": fold(dldt).sum(axis=1),
        "b_re": dbr_t, "b_im": dbi_t,
        "c_re": _unblockdiag(dcr).reshape(SSM_G, SSM_H, SSM_P),
        "c_im": _unblockdiag(dci).reshape(SSM_G, SSM_H, SSM_P),
        "d_skip": dd[:, 0, :].reshape(SSM_W),
        "b_glu": db_glu,
        "g_sgu": dg_sgu,
        "w_s": dws,
        "b_s": dbs.reshape(CHUNK, SGU_G, SGU_D).sum(axis=-1).T,
        "g_ffn": dg_ffn,
        "conv_w": dconv[:, 0:3, :].reshape(N_DEV // 2, 3, 2, FF_SHARD).transpose(0, 2, 1, 3).reshape(
            N_DEV, 3, FF_SHARD),
        "conv_b": dconv[:, 3, :].reshape(2 * D_FF),
        "g_final": dg_final,
    }
    return grad_x, small_out(grads, loss)


_ANY = pl.BlockSpec(memory_space=pl.ANY)
_MESH = pl.DeviceIdType.MESH


def _allgather(shards, dtypes, name, cast_only=(), sum_slots=False):
    n = len(shards)
    e = len(cast_only)
    shapes = [s.shape[1:] if sum_slots else s.shape for s in shards]

    def body(*refs):
        in_refs, extra_in = refs[:n], refs[n:n + e]
        out_refs, extra_out = refs[n + e:2 * n + e], refs[2 * n + e:2 * n + 2 * e]
        stage = refs[2 * n + 2 * e:3 * n + 2 * e]
        send_sems, recv_sems, local_sems = refs[3 * n + 2 * e:]
        for a in range(n):
            if sum_slots:
                total = in_refs[a][0].astype(F32)
                for s in range(1, N_DEV):
                    total = total + in_refs[a][s].astype(F32)
                stage[a][...] = total.astype(dtypes[a])
            else:
                stage[a][...] = in_refs[a][...].astype(dtypes[a])
        for i in range(e):
            extra_out[i][...] = extra_in[i][...].astype(MXU)
        x, y, c = lax.axis_index("x"), lax.axis_index("y"), lax.axis_index("c")
        me, sibling = (x, y, c), (x, y, 1 - c)
        chips = [(1 - x, y), (x, 1 - y), (1 - x, 1 - y)]

        def slot(a, px, py, pc):
            return out_refs[a].at[4 * px + 2 * py + pc]

        def copy(a, k, block, to, src=None):
            return pltpu.make_async_remote_copy(
                src_ref=slot(a, *block) if src is None else src, dst_ref=slot(a, *block),
                send_sem=send_sems.at[a, k], recv_sem=recv_sems.at[a, k], device_id=to, device_id_type=_MESH)

        mine = [pltpu.make_async_copy(stage[a], slot(a, *me), local_sems.at[a]) for a in range(n)]
        for cp in mine:
            cp.start()
        first = []
        for j, chip in enumerate(chips):
            first += [copy(a, 1 + j, me, (*chip, c), src=stage[a]) for a in range(n)]
        first += [copy(a, 0, me, sibling, src=stage[a]) for a in range(n)]
        for cp in first:
            cp.start()
        passed = []
        for j, chip in enumerate(chips):
            for a in range(n):
                copy(a, 1 + j, (*chip, c), me).wait_recv()
                fwd = copy(a, 4 + j, (*chip, c), sibling)
                fwd.start()
                passed.append(fwd)
        for a in range(n):
            copy(a, 0, sibling, me).wait_recv()
        for j, chip in enumerate(chips):
            for a in range(n):
                copy(a, 4 + j, (*chip, 1 - c), me).wait_recv()
        for cp in first + passed:
            cp.wait_send()
        for cp in mine:
            cp.wait()

    res = pl.pallas_call(
        body, name=name, grid=(1,), in_specs=[_full(s.shape) for s in list(shards) + list(cast_only)],
        out_specs=[_ANY] * n + [_full(s.shape) for s in cast_only],
        out_shape=[_sds((N_DEV,) + shp, dt) for shp, dt in zip(shapes, dtypes)]
                  + [_sds(s.shape, MXU) for s in cast_only],
        scratch_shapes=[pltpu.VMEM(shp, dt) for shp, dt in zip(shapes, dtypes)]
                       + [pltpu.SemaphoreType.DMA((n, 7)), pltpu.SemaphoreType.DMA((n, 7)), pltpu.SemaphoreType.DMA((n,))],
        compiler_params=pltpu.CompilerParams(vmem_limit_bytes=VMEM_LIMIT),
    )(*_in_hbm([*shards, *cast_only]))
    return res[:n], res[n:]


_HBM = pl.BlockSpec(memory_space=pltpu.HBM)
_SEM = pl.BlockSpec(memory_space=pltpu.SEMAPHORE)
_EFFECT = pltpu.SideEffectType.DATAFLOW_SIDE_EFFECTING
_PEER_ORDER = (2, 4, 6, 3, 5, 7, 1)


def _peer(k):
    x, y, c = lax.axis_index("x"), lax.axis_index("y"), lax.axis_index("c")
    px = 1 - x if k & 4 else x
    py = 1 - y if k & 2 else y
    pc = 1 - c if k & 1 else c
    return (px, py, pc), 4 * px + 2 * py + pc


_SAME_CORE_AND_SIBLING = (2, 4, 6, 1)


def _push_start(srcs, lands, slotted, name, peers=_PEER_ORDER):
    n = len(srcs)

    def body(*refs):
        src_refs, land_refs = refs[:n], refs[n:2 * n]
        send_sems, recv_sems, token = refs[2 * n], refs[2 * n + 1], refs[-1]
        mine = 4 * lax.axis_index("x") + 2 * lax.axis_index("y") + lax.axis_index("c")
        for k in peers:
            dev, theirs = _peer(k)
            for a in range(n):
                pltpu.make_async_remote_copy(
                    src_ref=src_refs[a].at[theirs] if slotted else src_refs[a], dst_ref=land_refs[a].at[mine],
                    send_sem=send_sems.at[7 * a + k - 1], recv_sem=recv_sems.at[7 * a + k - 1],
                    device_id=dev, device_id_type=_MESH).start()
        token[...] = jnp.zeros_like(token)

    bufs = list(srcs) + list(lands)
    res = pl.pallas_call(
        body, name=name, in_specs=[_HBM] * (2 * n),
        out_specs=(_SEM, _SEM, *[_HBM] * (2 * n), pl.BlockSpec(memory_space=pltpu.VMEM)),
        out_shape=(pltpu.SemaphoreType.DMA((7 * n,)), pltpu.SemaphoreType.DMA((7 * n,)),
                   *[pltpu.HBM(b.shape, b.dtype) for b in bufs], _sds((8, LANES))),
        input_output_aliases={i: 2 + i for i in range(2 * n)},
        compiler_params=pltpu.CompilerParams(has_side_effects=_EFFECT),
    )(*[pltpu.with_memory_space_constraint(b, pltpu.HBM) for b in bufs])
    return res[0], res[1], res[2:2 + n], res[2 + n:2 + 2 * n], res[-1]


def _push_wait(send_sems, recv_sems, srcs, lands, slotted, after, name, peers=_PEER_ORDER):
    n = len(srcs)

    def body(*refs):
        src_refs, land_refs = refs[:n], refs[n:2 * n]
        send_sems, recv_sems = refs[2 * n], refs[2 * n + 1]
        for k in peers:
            dev, theirs = _peer(k)
            for a in range(n):
                cp = pltpu.make_async_remote_copy(
                    src_ref=src_refs[a].at[theirs] if slotted else src_refs[a], dst_ref=land_refs[a].at[theirs],
                    send_sem=send_sems.at[7 * a + k - 1], recv_sem=recv_sems.at[7 * a + k - 1],
                    device_id=dev, device_id_type=_MESH)
                cp.wait_send()
                cp.wait_recv()

    bufs = list(srcs) + list(lands)
    res = pl.pallas_call(
        body, name=name, in_specs=[_HBM] * (2 * n) + [_SEM, _SEM] + [_ANY] * len(after), out_specs=[_HBM] * (2 * n),
        out_shape=[pltpu.HBM(b.shape, b.dtype) for b in bufs],
        input_output_aliases={i: i for i in range(2 * n)},
        compiler_params=pltpu.CompilerParams(has_side_effects=_EFFECT),
    )(*bufs, send_sems, recv_sems, *after)
    return res[n:]


def _other_chips():
    x, y = lax.axis_index("x"), lax.axis_index("y")
    return ((1 - x, y), (x, 1 - y), (1 - x, 1 - y))


def _relay_start(lands, name):
    n = len(lands)

    def body(*refs):
        land_refs = refs[:n]
        send_sems, recv_sems, token = refs[n], refs[n + 1], refs[-1]
        x, y, c = lax.axis_index("x"), lax.axis_index("y"), lax.axis_index("c")
        for j, (px, py) in enumerate(_other_chips()):
            slot = 4 * px + 2 * py + c
            for a in range(n):
                pltpu.make_async_remote_copy(
                    src_ref=land_refs[a].at[slot], dst_ref=land_refs[a].at[slot],
                    send_sem=send_sems.at[3 * a + j], recv_sem=recv_sems.at[3 * a + j],
                    device_id=(x, y, 1 - c), device_id_type=_MESH).start()
        token[...] = jnp.zeros_like(token)

    res = pl.pallas_call(
        body, name=name, in_specs=[_HBM] * n,
        out_specs=(_SEM, _SEM, *[_HBM] * n, pl.BlockSpec(memory_space=pltpu.VMEM)),
        out_shape=(pltpu.SemaphoreType.DMA((3 * n,)), pltpu.SemaphoreType.DMA((3 * n,)),
                   *[pltpu.HBM(b.shape, b.dtype) for b in lands], _sds((8, LANES))),
        input_output_aliases={i: 2 + i for i in range(n)},
        compiler_params=pltpu.CompilerParams(has_side_effects=_EFFECT),
    )(*[pltpu.with_memory_space_constraint(b, pltpu.HBM) for b in lands])
    return res[0], res[1], res[2:2 + n], res[-1]


def _relay_wait(send_sems, recv_sems, lands, after, name):
    n = len(lands)

    def body(*refs):
        land_refs = refs[:n]
        send_sems, recv_sems = refs[n], refs[n + 1]
        x, y, c = lax.axis_index("x"), lax.axis_index("y"), lax.axis_index("c")
        for j, (px, py) in enumerate(_other_chips()):
            sent, received = 4 * px + 2 * py + c, 4 * px + 2 * py + (1 - c)
            for a in range(n):
                cp = pltpu.make_async_remote_copy(
                    src_ref=land_refs[a].at[sent], dst_ref=land_refs[a].at[received],
                    send_sem=send_sems.at[3 * a + j], recv_sem=recv_sems.at[3 * a + j],
                    device_id=(x, y, 1 - c), device_id_type=_MESH)
                cp.wait_send()
                cp.wait_recv()

    return pl.pallas_call(
        body, name=name, in_specs=[_HBM] * n + [_SEM, _SEM] + [_ANY] * len(after), out_specs=[_HBM] * n,
        out_shape=[pltpu.HBM(b.shape, b.dtype) for b in lands],
        input_output_aliases={i: i for i in range(n)},
        compiler_params=pltpu.CompilerParams(has_side_effects=_EFFECT),
    )(*lands, send_sems, recv_sems, *after)


def _adamw(w, g, m, v):
    m2 = ADAM_B1 * m + (1.0 - ADAM_B1) * g
    v2 = ADAM_B2 * v + (1.0 - ADAM_B2) * (g * g)
    m_hat = m2 / (1.0 - ADAM_B1 ** ADAM_STEP)
    v_hat = v2 / (1.0 - ADAM_B2 ** ADAM_STEP)
    delta = -ADAM_LR * (m_hat / (jnp.sqrt(v_hat) + ADAM_EPS) + ADAM_WD * w)
    return delta, m2, v2


def _adam_shard(parts, w, m, v, name):
    _, r, c = w.shape
    tr = max(t for t in range(16, 257, 16) if r % t == 0)

    nparts = parts.shape[0]

    def body(p_ref, w_ref, m_ref, v_ref, g_ref, d_ref, m2_ref, v2_ref):
        g = p_ref[0].astype(F32)
        for s in range(1, nparts):
            g = g + p_ref[s].astype(F32)
        g_ref[0] = g
        d_ref[0], m2_ref[0], v2_ref[0] = _adamw(w_ref[0], g, m_ref[0], v_ref[0])

    row = lambda: pl.BlockSpec((1, tr, c), lambda i: (0, i, 0))
    return pl.pallas_call(
        body, name=name, grid=(r // tr,),
        in_specs=[pl.BlockSpec((nparts, tr, c), lambda i: (0, i, 0)), row(), row(), row()],
        out_specs=[row(), row(), row(), row()], out_shape=[_sds((1, r, c))] * 4,
        compiler_params=_cp("parallel"),
    )(*_in_hbm([parts, w, m, v]))


def _adam_small(gs, ws, ms, vs, name):
    n = len(gs)

    def body(*refs):
        ins, outs = refs[:4 * n], refs[4 * n:]
        for i in range(n):
            g = ins[i][...]
            d, m2, v2 = _adamw(ins[n + i][...], g, ins[2 * n + i][...], ins[3 * n + i][...])
            outs[i][...] = d
            outs[n + i][...] = m2
            outs[2 * n + i][...] = v2

    res = pl.pallas_call(
        body, name=name, grid=(1,), in_specs=[_full(w.shape) for w in ws] * 4,
        out_specs=[_full(w.shape) for w in ws] * 3, out_shape=[_sds(w.shape) for w in ws] * 3,
        compiler_params=_cp("arbitrary"),
    )(*_in_hbm([*gs, *ws, *ms, *vs]))
    return res[:n], res[n:2 * n], res[2 * n:]


def _pad_to(a, n, axis):
    extra = n - a.shape[axis]
    if extra == 0:
        return a
    widths = [(0, 0)] * a.ndim
    widths[axis] = (0, extra)
    return jnp.pad(a, widths)


def _ceil_to(n, k):
    return -(-n // k) * k


def _pack_rows(flats, rows_multiple):
    parts = [_pad_to(f, _ceil_to(f.shape[-1], LANES), f.ndim - 1) for f in flats]
    cat = jnp.concatenate(parts, axis=-1)
    total = _ceil_to(cat.shape[-1], LANES * rows_multiple)
    cat = _pad_to(cat, total, cat.ndim - 1)
    return cat.reshape(cat.shape[:-1] + (total // LANES, LANES))


def _unpack_rows(buf, sizes):
    flat = buf.reshape(buf.shape[:-2] + (-1,))
    out, off = [], 0
    for n in sizes:
        out.append(flat[..., off:off + n])
        off += _ceil_to(n, LANES)
    return out


_MIX_BIG = ("w_in", "w_glu", "w_proj_a", "w_proj_b", "w_out")
_BIG = _MIX_BIG + ("w_up", "w_down")
_SMALL = ("g_mix", "a_re", "a_im", "log_dt", "b_re", "b_im", "c_re", "c_im", "d_skip", "b_glu", "g_sgu", "w_s", "b_s",
          "g_ffn", "conv_b", "g_final")
_SMALL_ROWS_MULTIPLE = 8 * N_DEV
_TRANSPOSED = ("w_in", "w_up", "b_re", "b_im")


def _as_2d(a):
    return a.reshape(-1, a.shape[-1]) if a.ndim > 1 else a.reshape(1, -1)


def kernel(x, g_mix, w_in, a_re, a_im, log_dt, b_re, b_im, c_re, c_im, d_skip, w_glu, b_glu, w_proj_a, g_sgu, w_s, b_s, w_proj_b, w_out, g_ffn, w_up, conv_w, conv_b, w_down, g_final, loss_target, m_g_mix, m_w_in, m_a_re, m_a_im, m_log_dt, m_b_re, m_b_im, m_c_re, m_c_im, m_d_skip, m_w_glu, m_b_glu, m_w_proj_a, m_g_sgu, m_w_s, m_b_s, m_w_proj_b, m_w_out, m_g_ffn, m_w_up, m_conv_w, m_conv_b, m_w_down, m_g_final, v_g_mix, v_w_in, v_a_re, v_a_im, v_log_dt, v_b_re, v_b_im, v_c_re, v_c_im, v_d_skip, v_w_glu, v_b_glu, v_w_proj_a, v_g_sgu, v_w_s, v_b_s, v_w_proj_b, v_w_out, v_g_ffn, v_w_up, v_conv_w, v_conv_b, v_w_down, v_g_final):
    args = dict(locals())
    me = 4 * lax.axis_index("x") + 2 * lax.axis_index("y") + lax.axis_index("c")

    def own_slot(buf, block):
        return lax.dynamic_update_slice(buf, block[None], (me,) + (0,) * block.ndim)

    for n in _TRANSPOSED:
        for pre in ("", "m_", "v_"):
            args[pre + n] = jnp.swapaxes(args[pre + n], -1, -2)
    later = ("w_glu", "w_proj_a", "w_proj_b", "w_out", "w_up", "w_down")
    (w_in_g,), casts = _allgather([args["w_in"][0]], [MXU], "allgather_w_in", cast_only=[args[n][0] for n in later])
    sh = dict(zip(later, casts))

    def start_push(srcs, tag, peers):
        lands = [own_slot(lax.empty((N_DEV,) + s.shape, s.dtype), s) for s in srcs]
        send_sems, recv_sems, srcs, lands, token = _push_start(srcs, lands, False, "push_" + tag, peers)
        return (send_sems, recv_sems, srcs, lands), token

    mix_push, token_a = start_push([sh[n] for n in later[:4]], "mixer_weights", _SAME_CORE_AND_SIBLING)
    ffn_push, token_b = start_push([sh["w_up"], sh["w_down"], conv_w[0]], "ffn_weights", _PEER_ORDER)
    p = {n: (args[n][0] if n != "g_final" else args[n]) for n in _SMALL if n not in _TRANSPOSED}
    p.update(w_in_t=w_in_g.reshape(SSM_W + 2 * SGU_W + 2 * D_MODEL, D_MODEL),
             b_re_t=args["b_re"][0], b_im_t=args["b_im"][0])
    relay = {}

    def mixer_relay(after):
        lands = _push_wait(*mix_push, False, [after], "wait_mixer_weights", _SAME_CORE_AND_SIBLING)
        relay["send"], relay["recv"], relay["lands"], token = _relay_start(lands, "relay_mixer_weights")
        return token

    def mixer_weights(after):
        w_glu_g, w_pa_g, w_pb_g, w_out_g = _relay_wait(relay["send"], relay["recv"], relay["lands"], [after],
                                                       "wait_relay_mixer_weights")
        w_pa_full, w_pb_full = _assemble_cols([w_pa_g, w_pb_g], "assemble_cols")
        return dict(w_glu=w_glu_g.reshape(SSM_W, SSM_W), w_proj_a=w_pa_full, w_proj_b=w_pb_full,
                    w_out=w_out_g.reshape(D_MODEL, D_MODEL))

    def ffn_weights(after):
        w_up_g, w_down_g, conv_w_g = _push_wait(*ffn_push, False, [after], "wait_ffn_weights")
        return w_up_g, conv_w_g, w_down_g.reshape(D_FF, D_MODEL)

    pushes = []

    def grads_out(names, sends):
        lands = [own_slot(lax.empty(s.shape, s.dtype), lax.dynamic_index_in_dim(s, me, 0, keepdims=False))
                 for s in sends]
        send_sems, recv_sems, srcs, lands, token = _push_start(list(sends), lands, True, "push_grads_" + names[0])
        pushes.append((names, send_sems, recv_sems, srcs, lands))
        return token


    small_names = _SMALL + ("conv_w", "loss")
    small = {}

    def small_out(grads, loss_part):
        small_g = dict(grads, loss=loss_part[0, 0:1])
        flats = [small_g[n].reshape(-1) for n in small_names]
        small["sizes"] = [f.shape[0] for f in flats]
        g_small = _pack_rows(flats, _SMALL_ROWS_MULTIPLE)
        small["rs8"] = g_small.shape[0] // N_DEV
        return grads_out(("small",), (g_small.reshape(N_DEV, small["rs8"], LANES),))

    grad_x, small_token = _local_step(x[0], loss_target[0], p, (token_a, token_b), mixer_relay, mixer_weights,
                                      ffn_weights, grads_out, small_out)

    out = {}
    done = [grad_x, small_token]
    for names, send_sems, recv_sems, srcs, lands in pushes:
        parts = _push_wait(send_sems, recv_sems, srcs, lands, True, done, "wait_grads_" + names[0])
        if names == ("small",):
            g_small_all = _allgather([parts[0]], [F32], "allgather_small", sum_slots=True)[0][0].reshape(
                N_DEV * small["rs8"], LANES)
            pieces = dict(zip(small_names, _unpack_rows(g_small_all, small["sizes"])))
            loss = pieces["loss"][0]
            dconv_w = lax.dynamic_index_in_dim(pieces["conv_w"].reshape(N_DEV, 3, FF_SHARD), me, axis=0, keepdims=False)
            names2 = _SMALL + ("conv_w",)
            gs = [pieces[n].reshape(_as_2d(args[n]).shape) for n in _SMALL] + [dconv_w]
            ds, m2s, v2s = _adam_small(gs, [_as_2d(args[n]) for n in names2], [_as_2d(args["m_" + n]) for n in names2],
                                       [_as_2d(args["v_" + n]) for n in names2], "adam_small")
            for n, res in zip(names2, zip(gs, ds, m2s, v2s)):
                for kind, v in zip(("grad_", "delta_", "new_m_", "new_v_"), res):
                    out[kind + n] = v.reshape(args[n].shape)
            done = [ds[0]]
            continue
        done = []
        for n, part in zip(names, parts):
            res = _adam_shard(part, args[n], args["m_" + n], args["v_" + n], "adam_" + n)
            for kind, v in zip(("grad_", "delta_", "new_m_", "new_v_"), res):
                out[kind + n] = v
            done.append(res[0])
    order = ("g_mix", "w_in", "a_re", "a_im", "log_dt", "b_re", "b_im", "c_re", "c_im", "d_skip", "w_glu", "b_glu",
             "w_proj_a", "g_sgu", "w_s", "b_s", "w_proj_b", "w_out", "g_ffn", "w_up", "conv_w", "conv_b", "w_down",
             "g_final")
    res = [loss, grad_x.reshape(x.shape)]
    for kind in ("grad_", "delta_", "new_m_", "new_v_"):
        res += [jnp.swapaxes(out[kind + n], -1, -2) if n in _TRANSPOSED else out[kind + n] for n in order]
    return tuple(res)
```

```python
import math

import jax
import jax.numpy as jnp
from jax import lax
from jax.experimental import pallas as pl
from jax.experimental.pallas import tpu as pltpu

F32 = jnp.float32
MXU = jnp.bfloat16
EPS = 1e-6

D_MODEL = 1024
SSM_W = 512
SSM_G, SSM_H, SSM_P = 32, 16, 64
SSM_BLK = 4
SGU_W = 512
SGU_G, SGU_D, CHUNK = 8, 64, 128
D_FF = 2816
N_DEV = 8
FF_SHARD = 2 * D_FF // N_DEV
FF_CW = 2 * FF_SHARD
FF_NCB = D_FF // FF_CW
LANES = 128

ADAM_LR, ADAM_B1, ADAM_B2, ADAM_EPS, ADAM_WD, ADAM_STEP = 0.001, 0.9, 0.999, 1e-08, 0.01, 10

VMEM_LIMIT = 48 * 1024 * 1024
WGRAD_VMEM_LIMIT = 58 * 1024 * 1024
FFN_VMEM_LIMIT = 58 * 1024 * 1024


def _cp(*sem):
    return pltpu.CompilerParams(dimension_semantics=sem, vmem_limit_bytes=VMEM_LIMIT)


def _full(shape):
    n = len(shape)
    return pl.BlockSpec(shape, lambda *_: (0,) * n)


def _sds(shape, dtype=F32):
    return jax.ShapeDtypeStruct(shape, dtype)


def _in_hbm(arrays):
    return [pltpu.with_memory_space_constraint(a, pltpu.HBM) for a in arrays]


def _behind(body, n_in, after):
    def ordered(*refs):
        body(*refs[:n_in], *refs[n_in + len(after):])
    return ordered


def _dot(a, b):
    return jnp.dot(a, b, preferred_element_type=F32)


def _dot_nt(a, b):
    return lax.dot_general(a, b, (((1,), (1,)), ((), ())), preferred_element_type=F32)


def _dot_tn(a, b):
    return lax.dot_general(a, b, (((0,), (0,)), ((), ())), preferred_element_type=F32)


_GELU_C = math.sqrt(2.0 / math.pi)


def _gelu(x):
    return 0.5 * x * (1.0 + jnp.tanh(_GELU_C * (x + 0.044715 * (x * x * x))))


def _gelu_and_grad(x):
    t = jnp.tanh(_GELU_C * (x + 0.044715 * (x * x * x)))
    g = 0.5 * x * (1.0 + t)
    dg = 0.5 * (1.0 + t) + 0.5 * x * (1.0 - t * t) * (_GELU_C * (1.0 + 3.0 * 0.044715 * (x * x)))
    return g, dg


def _sigmoid(x):
    return 0.5 * jnp.tanh(0.5 * x) + 0.5


def _rms(x):
    return lax.rsqrt(jnp.mean(x * x, axis=-1, keepdims=True) + EPS)


def _rms_bwd(dxn, xn, r):
    return r * (dxn - xn * jnp.mean(dxn * xn, axis=-1, keepdims=True))


def _rowsum(x):
    return jnp.sum(x, axis=0, keepdims=True)


def _fetch_once(pairs, sems):
    copies = [pltpu.make_async_copy(src, dst, sems.at[k]) for k, (src, dst) in enumerate(pairs)]
    for cp in copies:
        cp.start()
    for cp in copies:
        cp.wait()


def _s5_disc(are, aim, ldt, br, bi):
    dt = jnp.exp(ldt)
    mag = jnp.exp(dt * are)
    abr = mag * jnp.cos(dt * aim)
    abi = mag * jnp.sin(dt * aim)
    den = are * are + aim * aim
    nr = abr - 1.0
    ni = abi
    fr = (nr * are + ni * aim) / den
    fi = (ni * are - nr * aim) / den
    return abr, abi, fr * br - fi * bi, fr * bi + fi * br


def _s5_params_fwd(are, aim, ldt, br, bi):
    def body(are_ref, aim_ref, ldt_ref, br_ref, bi_ref, o0, o1, o2, o3):
        outs = _s5_disc(are_ref[...], aim_ref[...], ldt_ref[...], br_ref[...], bi_ref[...])
        for o, v in zip((o0, o1, o2, o3), outs):
            o[...] = v
    shp = are.shape
    return pl.pallas_call(body, name="s5_params_fwd", grid=(1,), in_specs=[_full(shp)] * 5, out_specs=[_full(shp)] * 4,
                          out_shape=[_sds(shp)] * 4)(*_in_hbm([are, aim, ldt, br, bi]))


def _s5_params_bwd(are, aim, ldt, br, bi, dabr, dabi, dbr, dbi):
    def body(are_ref, aim_ref, ldt_ref, br_ref, bi_ref, c0, c1, c2, c3, o0, o1, o2, o3, o4):
        prim = (are_ref[...], aim_ref[...], ldt_ref[...], br_ref[...], bi_ref[...])
        _, vjp = jax.vjp(_s5_disc, *prim)
        outs = vjp((c0[...], c1[...], c2[...], c3[...]))
        for o, v in zip((o0, o1, o2, o3, o4), outs):
            o[...] = v
    shp = are.shape
    return pl.pallas_call(body, name="s5_params_bwd", grid=(1,), in_specs=[_full(shp)] * 9, out_specs=[_full(shp)] * 5,
                          out_shape=[_sds(shp)] * 5)(*_in_hbm([are, aim, ldt, br, bi, dabr, dabi, dbr, dbi]))


def _blockdiag(m_t):
    m = m_t.reshape(SSM_BLK, 8, SSM_H, 1, SSM_P)
    eye = jnp.eye(8, dtype=bool).reshape(1, 8, 1, 8, 1)
    return jnp.where(eye, m, jnp.zeros((), m_t.dtype)).reshape(SSM_BLK, 8 * SSM_H, 8 * SSM_P)


def _unblockdiag(pc):
    m = pc.reshape(SSM_BLK, 8, SSM_H, 8, SSM_P)
    return jnp.einsum("jghgp->jghp", m).reshape(SSM_G * SSM_H, SSM_P)


def _in_fwd(x, g_mix, w_in_t, tm, after=()):
    S = x.shape[0]

    def body(x_ref, g_ref, w_ref, h_ref, us_ref, uv_ref, gl_ref):
        xv = x_ref[...]
        h = (xv * _rms(xv) * g_ref[...]).astype(MXU)
        h_ref[...] = h
        us_ref[...] = _dot_nt(h, w_ref[0:SSM_W, :])
        uv_ref[...] = _dot_nt(h, w_ref[SSM_W:SSM_W + 2 * SGU_W, :])
        gl_ref[...] = _dot_nt(h, w_ref[SSM_W + 2 * SGU_W:, :])

    row = lambda n: pl.BlockSpec((tm, n), lambda i: (i, 0))
    return pl.pallas_call(
        _behind(body, 3, after), name="in_fwd", grid=(S // tm,),
        in_specs=[row(D_MODEL), _full((1, D_MODEL)), _full(w_in_t.shape)] + [_ANY] * len(after),
        out_specs=[row(D_MODEL), row(SSM_W), row(2 * SGU_W), row(2 * D_MODEL)],
        out_shape=[_sds((S, D_MODEL), MXU), _sds((S, SSM_W)), _sds((S, 2 * SGU_W)), _sds((S, 2 * D_MODEL))],
        compiler_params=_cp("parallel"),
    )(*_in_hbm([x, g_mix, w_in_t]), *after)


def _scan_tables(ar, ai, reverse):
    n = ar.shape[-1]
    def mul(p, q):
        return p[0] * q[0] - p[1] * q[1], p[0] * q[1] + p[1] * q[0]
    a1 = (ar, ai)
    a2 = mul(a1, a1)
    a3 = mul(a2, a1)
    a4 = mul(a2, a2)
    a5 = mul(a4, a1)
    a6 = mul(a4, a2)
    a7 = mul(a4, a3)
    a8 = mul(a4, a4)
    pw = (a1, a2, a3, a4, a5, a6, a7, a8)
    rows = lax.broadcasted_iota(jnp.int32, (8, n), 0)
    tabs = []
    for s, a in ((1, a1), (2, a2), (4, a4)):
        keep = (rows + s <= 7) if reverse else (rows >= s)
        for comp in a:
            tabs.append(jnp.where(keep, jnp.broadcast_to(comp, (8, n)), 0.0))
    for c in range(2):
        q = jnp.zeros((8, n), F32)
        for r in range(8):
            e = (8 - r) if reverse else (r + 1)
            q = jnp.where(rows == r, jnp.broadcast_to(pw[e - 1][c], (8, n)), q)
        tabs.append(q)
    return tabs


def _scan_group(xr, xi, tab_ref, cr, ci, reverse):
    for t, s in enumerate((1, 2, 4)):
        pr = tab_ref[2 * t]
        pi = tab_ref[2 * t + 1]
        sh = (8 - s) if reverse else s
        sr = pltpu.roll(xr, sh, 0)
        si = pltpu.roll(xi, sh, 0)
        xr, xi = xr + pr * sr - pi * si, xi + pr * si + pi * sr
    qr = tab_ref[6]
    qi = tab_ref[7]
    return xr + qr * cr - qi * ci, xi + qr * ci + qi * cr


def _runs_load(src_ref, dst_ref, run):
    for i in range(run):
        dst_ref[8 * i:8 * i + 8, :] = src_ref[pl.ds(i, 8, stride=run), :]


def _runs_store(val, dst_ref, run):
    for i in range(run):
        dst_ref[pl.ds(i, 8, stride=run), :] = val[8 * i:8 * i + 8, :]


def _cpow2(ar, ai, log2n):
    for _ in range(log2n):
        ar, ai = ar * ar - ai * ai, 2.0 * ar * ai
    return ar, ai


def _s5_fwd(us, abar_re, abar_im, b_re, b_im, c_re, c_im, d_skip, tm, after=()):
    S = us.shape[0]
    nt = S // tm
    w = 8 * SSM_P
    run = tm // 8
    assert run & (run - 1) == 0

    def body(us_ref, ar_ref, ai_ref, br_ref, bi_ref, cr_ref, ci_ref, d_ref, str_ref, sti_ref, ys_ref,
             tab_ref, car_ref, up_ref):
        i = pl.program_id(1)

        @pl.when(i == 0)
        def _():
            car_ref[...] = jnp.zeros_like(car_ref)
            for k, t in enumerate(_scan_tables(*_cpow2(ar_ref[...], ai_ref[...], run.bit_length() - 1), False)):
                tab_ref[k] = t

        _runs_load(us_ref, up_ref, run)
        ub = up_ref[...].astype(MXU)
        str_ref[...] = _dot(ub, br_ref[0])
        sti_ref[...] = _dot(ub, bi_ref[0])
        ar = jnp.broadcast_to(ar_ref[...], (8, w))
        ai = jnp.broadcast_to(ai_ref[...], (8, w))

        def advance(k, state):
            r0 = pl.multiple_of(k * 8, 8)
            sr, si = state
            return (ar * sr - ai * si + str_ref[pl.ds(r0, 8), :], ar * si + ai * sr + sti_ref[pl.ds(r0, 8), :])

        def emit(k, state):
            r0 = pl.multiple_of(k * 8, 8)
            sr, si = advance(k, state)
            str_ref[pl.ds(r0, 8), :] = sr
            sti_ref[pl.ds(r0, 8), :] = si
            return sr, si

        zero = jnp.zeros((8, w), F32)
        er, ei = lax.fori_loop(0, run, advance, (zero, zero))
        cr, ci = car_ref[0:1, :], car_ref[1:2, :]
        tr, ti = _scan_group(er, ei, tab_ref, cr, ci, False)
        r8 = lax.broadcasted_iota(jnp.int32, (8, w), 0)
        start = (jnp.where(r8 == 0, cr, pltpu.roll(tr, 1, 0)), jnp.where(r8 == 0, ci, pltpu.roll(ti, 1, 0)))
        car_ref[0:1, :] = tr[7:8, :]
        car_ref[1:2, :] = ti[7:8, :]
        lax.fori_loop(0, run, emit, start)
        y = _dot_nt(str_ref[...].astype(MXU), cr_ref[0]) - _dot_nt(sti_ref[...].astype(MXU), ci_ref[0])
        _runs_store(y, ys_ref, run)
        ys_ref[...] += d_ref[...] * us_ref[...]

    blk = lambda: pl.BlockSpec((1, 8 * SSM_H, w), lambda j, i: (j, 0, 0))
    return pl.pallas_call(
        _behind(body, 8, after), name="s5_fwd", grid=(SSM_BLK, nt),
        in_specs=[pl.BlockSpec((tm, LANES), lambda j, i: (i, j)),
                  pl.BlockSpec((1, w), lambda j, i: (0, j)), pl.BlockSpec((1, w), lambda j, i: (0, j)),
                  blk(), blk(), blk(), blk(),
                  pl.BlockSpec((1, LANES), lambda j, i: (0, j))] + [_ANY] * len(after),
        out_specs=[pl.BlockSpec((tm, w), lambda j, i: (i, j)), pl.BlockSpec((tm, w), lambda j, i: (i, j)),
                   pl.BlockSpec((tm, LANES), lambda j, i: (i, j))],
        out_shape=[_sds((S, SSM_BLK * w)), _sds((S, SSM_BLK * w)), _sds((S, SSM_W))],
        scratch_shapes=[pltpu.VMEM((8, 8, w), F32), pltpu.VMEM((8, w), F32), pltpu.VMEM((tm, LANES), F32)],
        compiler_params=_cp("parallel", "arbitrary"),
    )(*_in_hbm([us, abar_re, abar_im, b_re, b_im, c_re, c_im, d_skip]), *after)


def _group_halves(vp):
    first = lax.broadcasted_iota(jnp.int32, vp.shape, 1) < SGU_D
    zero = jnp.zeros((), vp.dtype)
    return jnp.where(first, vp, zero), jnp.where(first, zero, vp)


def _sgu_mix(vnb, wcat_ref):
    outs = []
    for q in range(SGU_G // 2):
        lo, hi = _group_halves(vnb[:, LANES * q:LANES * (q + 1)])
        outs.append(_dot(wcat_ref[q], jnp.concatenate([lo, hi], axis=0)))
    return jnp.concatenate(outs, axis=1)


def _mix_fwd(x, ys, uv, gl, w_glu, b_glu, w_pa, g_sgu, ws, bias_s, w_pb, w_out, g_ffn, tm):
    S = x.shape[0]

    def body(x_ref, ys_ref, uv_ref, gl_ref, wglu_ref, bglu_ref, wpa_ref, gs_ref, ws_ref, bias_ref, wpb_ref, wout_ref,
             gf_ref, yg_ref, yap_ref, sg_ref, ya_ref, yb_ref, m_ref, x1_ref, h2_ref):
        yg = _gelu(ys_ref[...])
        ygb = yg.astype(MXU)
        yg_ref[...] = ygb
        z = _dot(ygb, wglu_ref[...]) + bglu_ref[...]
        yapb = (yg * _sigmoid(z)).astype(MXU)
        yap_ref[...] = yapb
        ya = _dot(yapb, wpa_ref[...])
        ya_ref[...] = ya

        uvg = _gelu(uv_ref[...])
        u2 = uvg[:, :SGU_W]
        v2 = uvg[:, SGU_W:]
        vnb = (v2 * _rms(v2) * gs_ref[...]).astype(MXU)
        for c in range(tm // CHUNK):
            rs = slice(c * CHUNK, (c + 1) * CHUNK)
            mixed = _sgu_mix(vnb[rs], ws_ref) + bias_ref[...]
            sg_ref[rs, :] = (u2[rs] * mixed).astype(MXU)
        yb = _dot(sg_ref[...], wpb_ref[...])
        yb_ref[...] = yb

        glv = gl_ref[...]
        m = _sigmoid(glv[:, :D_MODEL]) * ya + _sigmoid(glv[:, D_MODEL:]) * yb
        mb = m.astype(MXU)
        m_ref[...] = mb
        x1 = x_ref[...] + _dot(mb, wout_ref[...])
        x1_ref[...] = x1
        h2_ref[...] = (x1 * _rms(x1) * gf_ref[...]).astype(MXU)

    row = lambda n: pl.BlockSpec((tm, n), lambda i: (i, 0))
    return pl.pallas_call(
        body, name="mix_fwd", grid=(S // tm,),
        in_specs=[row(D_MODEL), row(SSM_W), row(2 * SGU_W), row(2 * D_MODEL),
                  _full(w_glu.shape), _full(b_glu.shape), _full(w_pa.shape), _full(g_sgu.shape), _full(ws.shape),
                  _full(bias_s.shape), _full(w_pb.shape), _full(w_out.shape), _full(g_ffn.shape)],
        out_specs=[row(SSM_W), row(SSM_W), row(SGU_W), row(D_MODEL), row(D_MODEL), row(D_MODEL), row(D_MODEL),
                   row(D_MODEL)],
        out_shape=[_sds((S, SSM_W), MXU), _sds((S, SSM_W), MXU), _sds((S, SGU_W), MXU), _sds((S, D_MODEL)),
                   _sds((S, D_MODEL)), _sds((S, D_MODEL), MXU), _sds((S, D_MODEL)), _sds((S, D_MODEL), MXU)],
        compiler_params=_cp("parallel"),
    )(*_in_hbm([x, ys, uv, gl, w_glu, b_glu, w_pa, g_sgu, ws, bias_s, w_pb, w_out, g_ffn]))


def _causal_conv3(u, prev8, cw, cb):
    tm = u.shape[0]
    w0, w1, w2 = cw[0:1], cw[1:2], cw[2:3]
    body = w0 * pltpu.roll(u, 2, 0) + w1 * pltpu.roll(u, 1, 0) + w2 * u + cb
    u8 = u[0:8, :]
    r8 = lax.broadcasted_iota(jnp.int32, u8.shape, 0)
    t1 = prev8[7:8, :]
    t0 = prev8[6:7, :]
    s1 = jnp.where(r8 == 0, t1, pltpu.roll(u8, 1, 0))
    s2 = jnp.where(r8 == 0, t0, jnp.where(r8 == 1, t1, pltpu.roll(u8, 2, 0)))
    first = w0 * s2 + w1 * s1 + w2 * u8 + cb
    return jnp.concatenate([first, body[8:tm, :]], axis=0)


def _causal_conv3_adjoint(d, next8, cw):
    tm = d.shape[0]
    w0, w1, w2 = cw[0:1], cw[1:2], cw[2:3]
    n1 = pltpu.roll(d, tm - 1, 0)
    n2 = pltpu.roll(d, tm - 2, 0)
    body = w2 * d + w1 * n1 + w0 * n2
    d8 = d[tm - 8:tm, :]
    r8 = lax.broadcasted_iota(jnp.int32, d8.shape, 0)
    h0 = next8[0:1, :]
    h1 = next8[1:2, :]
    m1 = jnp.where(r8 == 7, h0, pltpu.roll(d8, 7, 0))
    m2 = jnp.where(r8 == 6, h0, jnp.where(r8 == 7, h1, pltpu.roll(d8, 6, 0)))
    last = w2 * d8 + w1 * m1 + w0 * m2
    out = jnp.concatenate([body[0:tm - 8, :], last], axis=0)
    return out, n1, n2, h0 - d[0:1, :], h1 - d[1:2, :]


def _ffn_fwd(h2, x1, tgt, w_up, conv_w, conv_b, w_down, g_final, tm):
    S = h2.shape[0]
    nt = S // tm
    ncb = FF_NCB

    def body(h2_ref, wup_hbm, cwa_ref, cwb_ref, cba_ref, cbb_ref, wd_hbm, x1_ref, gf_ref, tgt_ref,
             up_ref, ab_ref, ff_ref, dx2_ref, dx2b_ref, loss_ref, dgf_ref, acc_ref, tail_ref, wup_ref, wdn_ref, wsem):
        i = pl.program_id(0)
        cb = pl.program_id(1)

        @pl.when(i == 0)
        def _():
            tail_ref[cb] = jnp.zeros((2, 8, FF_CW), F32)

        @pl.when(jnp.logical_and(i == 0, cb == 0))
        def _():
            loss_ref[...] = jnp.zeros_like(loss_ref)
            dgf_ref[...] = jnp.zeros_like(dgf_ref)
            _fetch_once([(wup_hbm, wup_ref), (wd_hbm, wdn_ref)], wsem)

        h2v = h2_ref[...]
        ua = _dot_nt(h2v, wup_ref[cb])
        ub = _dot_nt(h2v, wup_ref[ncb + cb])
        up_ref[0, 0] = ua.astype(MXU)
        up_ref[1, 0] = ub.astype(MXU)
        a = _causal_conv3(ua, tail_ref[cb, 0], cwa_ref[0], cba_ref[0])
        b = _causal_conv3(ub, tail_ref[cb, 1], cwb_ref[0], cbb_ref[0])
        tail_ref[cb, 0] = ua[tm - 8:tm, :]
        tail_ref[cb, 1] = ub[tm - 8:tm, :]
        ab_ref[0, 0] = a
        ab_ref[1, 0] = b
        ffb = (a * _sigmoid(a) * b).astype(MXU)
        ff_ref[0] = ffb
        contrib = _dot(ffb, wdn_ref[pl.ds(pl.multiple_of(cb * FF_CW, FF_CW), FF_CW), :])

        @pl.when(cb == 0)
        def _():
            acc_ref[...] = contrib

        @pl.when(cb > 0)
        def _():
            acc_ref[...] += contrib

        @pl.when(cb == ncb - 1)
        def _():
            x2 = x1_ref[...] + acc_ref[...]
            r = _rms(x2)
            xn = x2 * r
            g = gf_ref[...]
            diff = xn * g - tgt_ref[...]
            loss_ref[...] += (0.5 / D_MODEL) * jnp.sum(diff * diff)
            dy = diff * (1.0 / D_MODEL)
            dgf_ref[...] += _rowsum(dy * xn)
            dx2 = _rms_bwd(dy * g, xn, r)
            dx2_ref[...] = dx2
            dx2b_ref[...] = dx2.astype(MXU)

    row = lambda n: pl.BlockSpec((tm, n), lambda i, c: (i, 0))
    gate = lambda r: pl.BlockSpec((1, r, FF_CW), lambda i, c: (c, 0, 0))
    lin = lambda r: pl.BlockSpec((1, r, FF_CW), lambda i, c: (ncb + c, 0, 0))
    return pl.pallas_call(
        body, name="ffn_fwd", grid=(nt, ncb),
        in_specs=[row(D_MODEL), _ANY, gate(3), lin(3), gate(1), lin(1), _ANY,
                  row(D_MODEL), _full((1, D_MODEL)), row(D_MODEL)],
        out_specs=[pl.BlockSpec((2, 1, tm, FF_CW), lambda i, c: (0, c, i, 0)),
                   pl.BlockSpec((2, 1, tm, FF_CW), lambda i, c: (0, c, i, 0)),
                   pl.BlockSpec((1, tm, FF_CW), lambda i, c: (c, i, 0)),
                   row(D_MODEL), row(D_MODEL), _full((1, LANES)), _full((1, D_MODEL))],
        out_shape=[_sds((2, ncb, S, FF_CW), MXU), _sds((2, ncb, S, FF_CW)), _sds((ncb, S, FF_CW), MXU),
                   _sds((S, D_MODEL)), _sds((S, D_MODEL), MXU), _sds((1, LANES)), _sds((1, D_MODEL))],
        scratch_shapes=[pltpu.VMEM((tm, D_MODEL), F32), pltpu.VMEM((ncb, 2, 8, FF_CW), F32),
                        pltpu.VMEM(w_up.shape, w_up.dtype), pltpu.VMEM(w_down.shape, w_down.dtype),
                        pltpu.SemaphoreType.DMA((2,))],
        compiler_params=pltpu.CompilerParams(dimension_semantics=("arbitrary", "arbitrary"),
                                             vmem_limit_bytes=FFN_VMEM_LIMIT),
    )(*_in_hbm([h2, w_up, conv_w, conv_w, conv_b, conv_b, w_down, x1, g_final, tgt]))


def _ffn_bwd(dx2, up, ab, x1, w_up, conv_w, w_down, g_ffn, tm):
    S = dx2.shape[0]
    nt = S // tm
    ncb = FF_NCB

    def body(dx2_ref, up_ref, ab_ref, cwa_ref, cwb_ref, wd_hbm, wup_hbm,
             x1_ref, g_ref, dup_ref, dx1_ref, dx1b_ref, dconv_ref, dg_ref, acc_ref, head_ref, wup_ref, wdn_ref, wsem):
        i = pl.program_id(0)
        cb = pl.program_id(1)

        @pl.when(i == 0)
        def _():
            head_ref[cb] = jnp.zeros((2, 8, FF_CW), F32)
            dconv_ref[cb] = jnp.zeros((8, FF_CW), F32)
            dconv_ref[ncb + cb] = jnp.zeros((8, FF_CW), F32)

        @pl.when(jnp.logical_and(i == 0, cb == 0))
        def _():
            dg_ref[...] = jnp.zeros_like(dg_ref)
            _fetch_once([(wup_hbm, wup_ref), (wd_hbm, wdn_ref)], wsem)

        dff = _dot_nt(dx2_ref[...].astype(MXU), wdn_ref[pl.ds(pl.multiple_of(cb * FF_CW, FF_CW), FF_CW), :])
        a = ab_ref[0, 0]
        b = ab_ref[1, 0]
        sa = _sigmoid(a)
        silu = a * sa
        da = (dff * b) * (sa + silu * (1.0 - sa))
        db = dff * silu
        dps = []
        for half, slot, d, cw_ref in ((0, cb, da, cwa_ref), (1, ncb + cb, db, cwb_ref)):
            dp, n1, n2, fix0, fix1 = _causal_conv3_adjoint(d, head_ref[cb, half], cw_ref[0])
            head_ref[cb, half] = d[0:8, :]
            dpb16 = dp.astype(MXU)
            dup_ref[half, 0] = dpb16
            dps.append(dpb16)
            u = up_ref[half, 0].astype(F32)
            u_last = u[tm - 1:tm, :]
            dconv_ref[slot, 0:1, :] += _rowsum(n2 * u) + fix0 * u[tm - 2:tm - 1, :] + fix1 * u_last
            dconv_ref[slot, 1:2, :] += _rowsum(n1 * u) + fix0 * u_last
            dconv_ref[slot, 2:3, :] += _rowsum(d * u)
            dconv_ref[slot, 3:4, :] += _rowsum(d)
        contrib = _dot(dps[0], wup_ref[cb]) + _dot(dps[1], wup_ref[ncb + cb])

        @pl.when(cb == 0)
        def _():
            acc_ref[...] = contrib

        @pl.when(cb > 0)
        def _():
            acc_ref[...] += contrib

        @pl.when(cb == ncb - 1)
        def _():
            x1v = x1_ref[...]
            r = _rms(x1v)
            xn = x1v * r
            dh2 = acc_ref[...]
            dg_ref[...] += _rowsum(dh2 * xn)
            dx1 = dx2_ref[...] + _rms_bwd(dh2 * g_ref[...], xn, r)
            dx1_ref[...] = dx1
            dx1b_ref[...] = dx1.astype(MXU)

    row = lambda n: pl.BlockSpec((tm, n), lambda i, c: (nt - 1 - i, 0))
    colb = lambda: pl.BlockSpec((2, 1, tm, FF_CW), lambda i, c: (0, c, nt - 1 - i, 0))
    gate = lambda r: pl.BlockSpec((1, r, FF_CW), lambda i, c: (c, 0, 0))
    lin = lambda r: pl.BlockSpec((1, r, FF_CW), lambda i, c: (ncb + c, 0, 0))
    return pl.pallas_call(
        body, name="ffn_bwd", grid=(nt, ncb),
        in_specs=[row(D_MODEL), colb(), colb(), gate(3), lin(3), _ANY, _ANY, row(D_MODEL), _full((1, D_MODEL))],
        out_specs=[colb(), row(D_MODEL), row(D_MODEL), _full((2 * ncb, 8, FF_CW)), _full((1, D_MODEL))],
        out_shape=[_sds((2, ncb, S, FF_CW), MXU), _sds((S, D_MODEL)), _sds((S, D_MODEL), MXU), _sds((2 * ncb, 8, FF_CW)),
                   _sds((1, D_MODEL))],
        scratch_shapes=[pltpu.VMEM((tm, D_MODEL), F32), pltpu.VMEM((ncb, 2, 8, FF_CW), F32),
                        pltpu.VMEM(w_up.shape, w_up.dtype), pltpu.VMEM(w_down.shape, w_down.dtype),
                        pltpu.SemaphoreType.DMA((2,))],
        compiler_params=pltpu.CompilerParams(dimension_semantics=("arbitrary", "arbitrary"),
                                             vmem_limit_bytes=FFN_VMEM_LIMIT),
    )(*_in_hbm([dx2, up, ab, conv_w, conv_w, w_down, w_up, x1, g_ffn]))


def _mix_bwd(dx1, gl, ya, yb, ys, uv, w_out, w_pa, w_pb, w_glu, b_glu, g_sgu, ws, ws_t, bias_s, tm, after=()):
    S = dx1.shape[0]

    def body(dx1_ref, gl_ref, ya_ref, yb_ref, ys_ref, uv_ref, wout_ref, wpa_ref, wpb_ref, wglu_ref, bglu_ref, gs_ref,
             ws_ref, wst_ref, bias_ref,
             dgl_ref, dya_ref, dyb_ref, dz_ref, dys_ref, duv_ref, dbglu_ref, dgs_ref, dws_ref, dbs_ref,
             du2_ref, dvn_ref):
        i = pl.program_id(0)

        @pl.when(i == 0)
        def _():
            dbglu_ref[...] = jnp.zeros_like(dbglu_ref)
            dgs_ref[...] = jnp.zeros_like(dgs_ref)
            dws_ref[...] = jnp.zeros_like(dws_ref)
            dbs_ref[...] = jnp.zeros_like(dbs_ref)

        dm = _dot_nt(dx1_ref[...].astype(MXU), wout_ref[...])
        glv = gl_ref[...]
        ga = _sigmoid(glv[:, :D_MODEL])
        gb = _sigmoid(glv[:, D_MODEL:])
        dgl_ref[:, :D_MODEL] = (dm * ya_ref[...] * ga * (1.0 - ga)).astype(MXU)
        dgl_ref[:, D_MODEL:] = (dm * yb_ref[...] * gb * (1.0 - gb)).astype(MXU)
        dyab = (dm * ga).astype(MXU)
        dybb = (dm * gb).astype(MXU)
        dya_ref[...] = dyab
        dyb_ref[...] = dybb

        dyap = _dot_nt(dyab, wpa_ref[...])
        yg, dgelu = _gelu_and_grad(ys_ref[...])
        sz = _sigmoid(_dot(yg.astype(MXU), wglu_ref[...]) + bglu_ref[...])
        dz = dyap * yg * sz * (1.0 - sz)
        dzb = dz.astype(MXU)
        dz_ref[...] = dzb
        dbglu_ref[...] += _rowsum(dz)
        dys_ref[...] = (dyap * sz + _dot_nt(dzb, wglu_ref[...])) * dgelu

        dsg = _dot_nt(dybb, wpb_ref[...])
        uvg, duvg = _gelu_and_grad(uv_ref[...])
        u2 = uvg[:, :SGU_W]
        v2 = uvg[:, SGU_W:]
        rv = _rms(v2)
        vhat = v2 * rv
        gs = gs_ref[...]
        vnb = (vhat * gs).astype(MXU)
        tril = (lax.broadcasted_iota(jnp.int32, (CHUNK, CHUNK), 0)
                >= lax.broadcasted_iota(jnp.int32, (CHUNK, CHUNK), 1))
        for c in range(tm // CHUNK):
            rs = slice(c * CHUNK, (c + 1) * CHUNK)
            vc = vnb[rs]
            mixed = _sgu_mix(vc, ws_ref) + bias_ref[...]
            dsg_c = dsg[rs]
            du2_ref[rs, :] = dsg_c * mixed
            dmx = dsg_c * u2[rs]
            dbs_ref[...] += dmx
            dmb = dmx.astype(MXU)
            dvn_ref[rs, :] = _sgu_mix(dmb, wst_ref)
            for q in range(SGU_G // 2):
                lanes = slice(LANES * q, LANES * (q + 1))
                for j, part in enumerate(_group_halves(dmb[:, lanes])):
                    dws_ref[2 * q + j] += jnp.where(tril, _dot_nt(part, vc[:, lanes]), 0.0)
        dvn = dvn_ref[...]
        dgs_ref[...] += _rowsum(dvn * vhat)
        dv2 = _rms_bwd(dvn * gs, vhat, rv)
        duv_ref[:, :SGU_W] = (du2_ref[...] * duvg[:, :SGU_W]).astype(MXU)
        duv_ref[:, SGU_W:] = (dv2 * duvg[:, SGU_W:]).astype(MXU)

    row = lambda n: pl.BlockSpec((tm, n), lambda i: (i, 0))
    return pl.pallas_call(
        _behind(body, 15, after), name="mix_bwd", grid=(S // tm,),
        in_specs=[row(D_MODEL), row(2 * D_MODEL), row(D_MODEL), row(D_MODEL), row(SSM_W), row(2 * SGU_W),
                  _full(w_out.shape), _full(w_pa.shape), _full(w_pb.shape), _full(w_glu.shape), _full(b_glu.shape),
                  _full(g_sgu.shape), _full(ws.shape), _full(ws_t.shape), _full(bias_s.shape)] + [_ANY] * len(after),
        out_specs=[row(2 * D_MODEL), row(D_MODEL), row(D_MODEL), row(SSM_W), row(SSM_W), row(2 * SGU_W),
                   _full((1, SSM_W)), _full((1, SGU_W)), _full((SGU_G, CHUNK, CHUNK)), _full((CHUNK, SGU_W))],
        out_shape=[_sds((S, 2 * D_MODEL), MXU), _sds((S, D_MODEL), MXU), _sds((S, D_MODEL), MXU), _sds((S, SSM_W), MXU),
                   _sds((S, SSM_W)), _sds((S, 2 * SGU_W), MXU),
                   _sds((1, SSM_W)), _sds((1, SGU_W)), _sds((SGU_G, CHUNK, CHUNK)), _sds((CHUNK, SGU_W))],
        scratch_shapes=[pltpu.VMEM((tm, SGU_W), F32), pltpu.VMEM((tm, SGU_W), F32)],
        compiler_params=_cp("arbitrary"),
    )(*_in_hbm([dx1, gl, ya, yb, ys, uv, w_out, w_pa, w_pb, w_glu, b_glu, g_sgu, ws, ws_t, bias_s]), *after)


def _s5_bwd(dys, us, st_re, st_im, abar_re, abar_im, b_re, b_im, c_re, c_im, d_skip, tm, after=()):
    S = us.shape[0]
    nt = S // tm
    w = 8 * SSM_P
    hb = tm // 8
    run = tm // 8
    assert run & (run - 1) == 0

    def body(dys_ref, us_ref, str_ref, sti_ref, hr_ref, hi_ref, ar_ref, ai_ref, br_ref, bi_ref, cr_ref, ci_ref, d_ref,
             dus_ref, dab_ref, dd_ref, dbr_ref, dbi_ref, dcr_ref, dci_ref,
             tab_ref, car_ref, gr_ref, gi_ref, dyp_ref, up_ref, dun_ref):
        i = pl.program_id(1)
        ri = nt - 1 - i

        @pl.when(i == 0)
        def _():
            car_ref[...] = jnp.zeros_like(car_ref)
            for k, t in enumerate(_scan_tables(*_cpow2(ar_ref[...], -ai_ref[...], run.bit_length() - 1), True)):
                tab_ref[k] = t
            for r in (dab_ref, dd_ref, dbr_ref, dbi_ref, dcr_ref, dci_ref):
                r[...] = jnp.zeros_like(r)

        _runs_load(dys_ref, dyp_ref, run)
        _runs_load(us_ref, up_ref, run)
        dyb = dyp_ref[...].astype(MXU)
        gr_ref[...] = _dot(dyb, cr_ref[0])
        gi_ref[...] = -_dot(dyb, ci_ref[0])
        ar = jnp.broadcast_to(ar_ref[...], (8, w))
        ai = jnp.broadcast_to(-ai_ref[...], (8, w))

        def advance(kk, state):
            r0 = pl.multiple_of((run - 1 - kk) * 8, 8)
            gr, gi = state
            return (ar * gr - ai * gi + gr_ref[pl.ds(r0, 8), :], ar * gi + ai * gr + gi_ref[pl.ds(r0, 8), :])

        def emit(kk, state):
            r0 = pl.multiple_of((run - 1 - kk) * 8, 8)
            gr, gi = advance(kk, state)
            gr_ref[pl.ds(r0, 8), :] = gr
            gi_ref[pl.ds(r0, 8), :] = gi
            return gr, gi

        zero = jnp.zeros((8, w), F32)
        er, ei = lax.fori_loop(0, run, advance, (zero, zero))
        cr, ci = car_ref[0:1, :], car_ref[1:2, :]
        tr, ti = _scan_group(er, ei, tab_ref, cr, ci, True)
        r8 = lax.broadcasted_iota(jnp.int32, (8, w), 0)
        start = (jnp.where(r8 == 7, cr, pltpu.roll(tr, 7, 0)), jnp.where(r8 == 7, ci, pltpu.roll(ti, 7, 0)))
        car_ref[0:1, :] = tr[0:1, :]
        car_ref[1:2, :] = ti[0:1, :]
        lax.fori_loop(0, run, emit, start)

        gsr = gr_ref[...]
        gsi = gi_ref[...]
        sr = str_ref[...]
        si = sti_ref[...]
        first = ri == 0

        def previous(s, halo_ref):
            head = jnp.where(r8 == 0, jnp.where(first, 0.0, halo_ref[7:8, :]), pltpu.roll(s[tm - 8:tm, :], 1, 0))
            return jnp.concatenate([head, s[0:tm - 8, :]], axis=0)

        spr = previous(sr, hr_ref)
        spi = previous(si, hi_ref)
        dab_ref[0, 0:1, :] += _rowsum(gsr * spr + gsi * spi)
        dab_ref[0, 1:2, :] += _rowsum(gsi * spr - gsr * spi)

        gbr = gsr.astype(MXU)
        gbi = gsi.astype(MXU)
        _runs_store(_dot_nt(gbr, br_ref[0]) + _dot_nt(gbi, bi_ref[0]), dun_ref, run)
        dys_v = dys_ref[...]
        dus_ref[...] = (dun_ref[...] + d_ref[...] * dys_v).astype(MXU)
        dd_ref[0, 0:1, :] += _rowsum(dys_v * us_ref[...])
        ub = up_ref[...].astype(MXU)
        dbr_ref[0] += _dot_tn(ub, gbr)
        dbi_ref[0] += _dot_tn(ub, gbi)
        dcr_ref[0] += _dot_tn(dyb, sr.astype(MXU))
        dci_ref[0] -= _dot_tn(dyb, si.astype(MXU))

    blk = lambda: pl.BlockSpec((1, 8 * SSM_H, w), lambda j, i: (j, 0, 0))
    rowl = lambda: pl.BlockSpec((tm, LANES), lambda j, i: (nt - 1 - i, j))
    roww = lambda: pl.BlockSpec((tm, w), lambda j, i: (nt - 1 - i, j))
    halo = lambda: pl.BlockSpec((8, w), lambda j, i: (jnp.maximum((nt - 1 - i) * hb - 1, 0), j))
    return pl.pallas_call(
        _behind(body, 13, after), name="s5_bwd", grid=(SSM_BLK, nt),
        in_specs=[rowl(), rowl(), roww(), roww(), halo(), halo(),
                  pl.BlockSpec((1, w), lambda j, i: (0, j)), pl.BlockSpec((1, w), lambda j, i: (0, j)),
                  blk(), blk(), blk(), blk(),
                  pl.BlockSpec((1, LANES), lambda j, i: (0, j))] + [_ANY] * len(after),
        out_specs=[rowl(),
                   pl.BlockSpec((1, 8, w), lambda j, i: (j, 0, 0)), pl.BlockSpec((1, 8, LANES), lambda j, i: (j, 0, 0)),
                   blk(), blk(), blk(), blk()],
        out_shape=[_sds((S, SSM_W), MXU), _sds((SSM_BLK, 8, w)), _sds((SSM_BLK, 8, LANES)),
                   _sds((SSM_BLK, 8 * SSM_H, w)), _sds((SSM_BLK, 8 * SSM_H, w)),
                   _sds((SSM_BLK, 8 * SSM_H, w)), _sds((SSM_BLK, 8 * SSM_H, w))],
        scratch_shapes=[pltpu.VMEM((8, 8, w), F32), pltpu.VMEM((8, w), F32),
                        pltpu.VMEM((tm, w), F32), pltpu.VMEM((tm, w), F32),
                        pltpu.VMEM((tm, LANES), F32), pltpu.VMEM((tm, LANES), F32), pltpu.VMEM((tm, LANES), F32)],
        compiler_params=_cp("parallel", "arbitrary"),
    )(*_in_hbm([dys, us, st_re, st_im, st_re, st_im, abar_re, abar_im, b_re, b_im, c_re, c_im, d_skip]), *after)


def _in_bwd(dus, duv, dgl, dx1, x, g_mix, w_in, tm, after=()):
    S = x.shape[0]

    def body(dus_ref, duv_ref, dgl_ref, dx1_ref, x_ref, g_ref, w_ref, gx_ref, dg_ref):
        @pl.when(pl.program_id(0) == 0)
        def _():
            dg_ref[...] = jnp.zeros_like(dg_ref)

        dh = (_dot(dus_ref[...], w_ref[0:SSM_W, :])
              + _dot(duv_ref[...], w_ref[SSM_W:SSM_W + 2 * SGU_W, :])
              + _dot(dgl_ref[...], w_ref[SSM_W + 2 * SGU_W:, :]))
        xv = x_ref[...]
        r = _rms(xv)
        xn = xv * r
        dg_ref[...] += _rowsum(dh * xn)
        gx_ref[...] = dx1_ref[...] + _rms_bwd(dh * g_ref[...], xn, r)

    row = lambda n: pl.BlockSpec((tm, n), lambda i: (i, 0))
    return pl.pallas_call(
        _behind(body, 7, after), name="in_bwd", grid=(S // tm,),
        in_specs=[row(SSM_W), row(2 * SGU_W), row(2 * D_MODEL), row(D_MODEL), row(D_MODEL), _full((1, D_MODEL)),
                  _full(w_in.shape)] + [_ANY] * len(after),
        out_specs=[row(D_MODEL), _full((1, D_MODEL))],
        out_shape=[_sds((S, D_MODEL)), _sds((1, D_MODEL))],
        compiler_params=_cp("arbitrary"),
    )(*_in_hbm([dus, duv, dgl, dx1, x, g_mix, w_in]), *after)


def _wgrad_split(a, b, nsplit, tk, name):
    S, K = a.shape
    N = b.shape[1]
    c = N // nsplit

    def body(a_ref, b_ref, o_ref):
        prod = _dot_tn(a_ref[...], b_ref[...])
        for d in range(nsplit):
            o_ref[d] = prod[:, c * d:c * (d + 1)].astype(MXU)

    return pl.pallas_call(
        body, name=name, grid=(K // tk,),
        in_specs=[pl.BlockSpec((S, tk), lambda k: (0, k)), _full((S, N))],
        out_specs=pl.BlockSpec((nsplit, tk, c), lambda k: (0, k, 0)),
        out_shape=_sds((nsplit, K, c), MXU),
        compiler_params=_cp("parallel"),
    )(*_in_hbm([a, b]))


def _wgrad_in_t(dps, h1, name):
    S, K = h1.shape
    cw = 512
    counts = [b.shape[1] // cw for b in dps]
    starts = [sum(counts[:i]) for i in range(len(dps))]
    nblk = sum(counts)

    def body(*refs):
        b_refs = refs[:len(dps)]
        h_ref, o_ref = refs[len(dps)], refs[-1]
        j = pl.program_id(0)
        for b_ref, st, cnt in zip(b_refs, starts, counts):
            @pl.when(jnp.logical_and(j >= st, j < st + cnt))
            def _():
                o_ref[...] = _dot_tn(b_ref[...], h_ref[...]).astype(MXU)

    def src_spec(st, cnt):
        return pl.BlockSpec((S, cw), lambda j: (0, jnp.clip(j - st, 0, cnt - 1)))

    return pl.pallas_call(
        body, name=name, grid=(nblk,),
        in_specs=[src_spec(st, cnt) for st, cnt in zip(starts, counts)] + [_full((S, K))],
        out_specs=pl.BlockSpec((cw, K), lambda j: (j, 0)),
        out_shape=_sds((nblk * cw, K), MXU),
        compiler_params=_cp("arbitrary"),
    )(*_in_hbm([*dps, h1]))


def _wgrad_blk(a3, b3, nblk, a_of, b_of, name):
    S, K = a3.shape[1:]
    N = b3.shape[2]

    def body(a_ref, b_ref, o_ref):
        o_ref[0] = _dot_tn(a_ref[0], b_ref[0]).astype(MXU)

    return pl.pallas_call(
        body, name=name, grid=(nblk,),
        in_specs=[pl.BlockSpec((1, S, K), lambda b: (a_of(b), 0, 0)),
                  pl.BlockSpec((1, S, N), lambda b: (b_of(b), 0, 0))],
        out_specs=pl.BlockSpec((1, K, N), lambda b: (b, 0, 0)),
        out_shape=_sds((nblk, K, N), MXU),
        compiler_params=pltpu.CompilerParams(dimension_semantics=("parallel",), vmem_limit_bytes=WGRAD_VMEM_LIMIT),
    )(*_in_hbm([a3, b3]))


def _assemble_cols(blocks_list, name):
    def body(*refs):
        n = len(blocks_list)
        for b_ref, o_ref in zip(refs[:n], refs[n:]):
            c = b_ref.shape[2]
            for d in range(N_DEV):
                o_ref[:, c * d:c * (d + 1)] = b_ref[d]

    outs = [_sds((b.shape[1], N_DEV * b.shape[2]), b.dtype) for b in blocks_list]
    return pl.pallas_call(
        body, name=name, grid=(1,), in_specs=[_full(b.shape) for b in blocks_list],
        out_specs=[_full(o.shape) for o in outs], out_shape=outs, compiler_params=_cp("arbitrary"),
    )(*_in_hbm(blocks_list))


def _tile(S, want):
    return want if S % want == 0 else S


def _local_step(x, tgt, p, after, mixer_relay, mixer_weights, ffn_weights, grads_out, small_out):
    S = x.shape[0]
    tm = _tile(S, 256)
    tl = _tile(S, 512)

    rep = lambda a: jnp.repeat(a, SSM_H, axis=0)
    are = rep(p["a_re"])
    aim = rep(p["a_im"])
    ldt = jnp.broadcast_to(rep(p["log_dt"].reshape(SSM_G, 1)), are.shape)
    br_t = p["b_re_t"].reshape(are.shape)
    bi_t = p["b_im_t"].reshape(are.shape)
    abr, abi, bbr, bbi = _s5_params_fwd(are, aim, ldt, br_t, bi_t)
    head = lambda a: a.reshape(SSM_G, SSM_H, SSM_P)[:, 0, :].reshape(1, SSM_G * SSM_P)
    abar_re, abar_im = head(abr), head(abi)
    bd_br = _blockdiag(bbr).astype(MXU)
    bd_bi = _blockdiag(bbi).astype(MXU)
    bd_cr = _blockdiag(p["c_re"].reshape(are.shape)).astype(MXU)
    bd_ci = _blockdiag(p["c_im"].reshape(are.shape)).astype(MXU)
    d_skip = p["d_skip"].reshape(1, SSM_W)

    tril = jnp.tril(jnp.ones((CHUNK, CHUNK), dtype=bool))
    ws = jnp.where(tril[None], p["w_s"], 0.0)
    pair = lambda w: w.reshape(SGU_G // 2, 2, CHUNK, CHUNK).transpose(0, 2, 1, 3).reshape(SGU_G // 2, CHUNK, 2 * CHUNK)
    ws_b = pair(ws).astype(MXU)
    ws_t = pair(ws.transpose(0, 2, 1)).astype(MXU)
    bias_s = jnp.repeat(p["b_s"].T, SGU_D, axis=1)

    g_mix = p["g_mix"].reshape(1, D_MODEL)
    g_ffn = p["g_ffn"].reshape(1, D_MODEL)
    g_final = p["g_final"].reshape(1, D_MODEL)
    g_sgu = p["g_sgu"].reshape(1, SGU_W)
    b_glu = p["b_glu"].reshape(1, SSM_W)
    conv_b = p["conv_b"].reshape(2 * FF_NCB, 1, FF_CW)
    tf = _tile(S, 256)
    ts = _tile(S, 1024)

    h1, us, uv, gl = _in_fwd(x, g_mix, p["w_in_t"], tl, after)
    token = mixer_relay(us)
    st_re, st_im, ys = _s5_fwd(us, abar_re, abar_im, bd_br, bd_bi, bd_cr, bd_ci, d_skip, ts, (token,))
    p = dict(p, **mixer_weights(ys))
    yg, yap, sg, ya, yb, m, x1, h2 = _mix_fwd(x, ys, uv, gl, p["w_glu"], b_glu, p["w_proj_a"], g_sgu, ws_b, bias_s,
                                              p["w_proj_b"], p["w_out"], g_ffn, tl)
    w_up, conv_w, w_down = ffn_weights(h2)
    pair_lanes = lambda a: a.reshape(N_DEV // 2, 2, a.shape[1], FF_SHARD).transpose(0, 2, 1, 3).reshape(
        N_DEV // 2, a.shape[1], FF_CW)
    w_up = w_up.reshape(2 * FF_NCB, FF_CW, D_MODEL)
    conv_w = pair_lanes(conv_w)
    up, ab, ff, dx2, dx2b, loss, dg_final = _ffn_fwd(h2, x1, tgt, w_up, conv_w, conv_b, w_down, g_final, tf)

    dup, dx1, dx1b, dconv, dg_ffn = _ffn_bwd(dx2, up, ab, x1, w_up, conv_w, w_down, g_ffn, tf)
    rows8 = lambda g: g.reshape(N_DEV, g.shape[1] // N_DEV, g.shape[2])
    g_up = _wgrad_blk(dup.reshape(2 * FF_NCB, S, FF_CW), h2[None], 2 * FF_NCB, lambda b: b, lambda b: 0,
                      "wgrad_up").reshape(N_DEV, FF_SHARD, D_MODEL)
    g_down = _wgrad_blk(ff, dx2b[None], FF_NCB, lambda b: b, lambda b: 0, "wgrad_down").reshape(
        N_DEV, D_FF // N_DEV, D_MODEL)
    token = grads_out(("w_up", "w_down"), (g_up, g_down))
    dgl, dya, dyb, dz, dys, duv, db_glu, dg_sgu, dws, dbs = _mix_bwd(
        dx1, gl, ya, yb, ys, uv, p["w_out"], p["w_proj_a"], p["w_proj_b"], p["w_glu"], b_glu, g_sgu,
        ws_b, ws_t, bias_s, tm, (token,))
    token = grads_out(("w_glu", "w_proj_a", "w_proj_b", "w_out"),
                      (rows8(_wgrad_split(yg, dz, 1, SSM_W, "wgrad_glu")),
                       _wgrad_split(yap, dya, N_DEV, SSM_W, "wgrad_pa"),
                       _wgrad_split(sg, dyb, N_DEV, SGU_W, "wgrad_pb"),
                       rows8(_wgrad_split(m, dx1b, 1, 512, "wgrad_out"))))
    dus, dab, dd, dbbr, dbbi, dcr, dci = _s5_bwd(dys, us, st_re, st_im, abar_re, abar_im, bd_br, bd_bi, bd_cr, bd_ci,
                                                 d_skip, ts, (token,))
    g_in = _wgrad_in_t([dus, duv, dgl], h1, "wgrad_in")
    token = grads_out(("w_in",), (g_in.reshape(N_DEV, g_in.shape[0] // N_DEV, D_MODEL),))
    grad_x, dg_mix = _in_bwd(dus, duv, dgl, dx1, x, g_mix, p["w_in_t"], tl, (token,))

    spread = lambda v: jnp.repeat(v.reshape(SSM_G, SSM_P), SSM_H, axis=0) * (1.0 / SSM_H)
    dabr = spread(dab[:, 0, :])
    dabi = spread(dab[:, 1, :])
    dare, daim, dldt, dbr_t, dbi_t = _s5_params_bwd(are, aim, ldt, br_t, bi_t, dabr, dabi,
                                                    _unblockdiag(dbbr), _unblockdiag(dbbi))
    fold = lambda a: a.reshape(SSM_G, SSM_H, SSM_P).sum(axis=1)

    grads = {
        "g_mix": dg_mix,
        "a_re": fold(dare), "a_im": fold(daim), "log_dt": fold(dldt).sum(axis=1),
        "b_re": dbr_t, "b_im": dbi_t,
        "c_re": _unblockdiag(dcr).reshape(SSM_G, SSM_H, SSM_P),
        "c_im": _unblockdiag(dci).reshape(SSM_G, SSM_H, SSM_P),
        "d_skip": dd[:, 0, :].reshape(SSM_W),
        "b_glu": db_glu,
        "g_sgu": dg_sgu,
        "w_s": dws,
        "b_s": dbs.reshape(CHUNK, SGU_G, SGU_D).sum(axis=-1).T,
        "g_ffn": dg_ffn,
        "conv_w": dconv[:, 0:3, :].reshape(N_DEV // 2, 3, 2, FF_SHARD).transpose(0, 2, 1, 3).reshape(
            N_DEV, 3, FF_SHARD),
        "conv_b": dconv[:, 3, :].reshape(2 * D_FF),
        "g_final": dg_final,
    }
    return grad_x, small_out(grads, loss)


_ANY = pl.BlockSpec(memory_space=pl.ANY)
_MESH = pl.DeviceIdType.MESH


def _allgather(shards, dtypes, name, cast_only=(), sum_slots=False):
    n = len(shards)
    e = len(cast_only)
    shapes = [s.shape[1:] if sum_slots else s.shape for s in shards]

    def body(*refs):
        in_refs, extra_in = refs[:n], refs[n:n + e]
        out_refs, extra_out = refs[n + e:2 * n + e], refs[2 * n + e:2 * n + 2 * e]
        stage = refs[2 * n + 2 * e:3 * n + 2 * e]
        send_sems, recv_sems, local_sems = refs[3 * n + 2 * e:]
        for a in range(n):
            if sum_slots:
                total = in_refs[a][0].astype(F32)
                for s in range(1, N_DEV):
                    total = total + in_refs[a][s].astype(F32)
                stage[a][...] = total.astype(dtypes[a])
            else:
                stage[a][...] = in_refs[a][...].astype(dtypes[a])
        for i in range(e):
            extra_out[i][...] = extra_in[i][...].astype(MXU)
        x, y, c = lax.axis_index("x"), lax.axis_index("y"), lax.axis_index("c")
        me, sibling = (x, y, c), (x, y, 1 - c)
        chips = [(1 - x, y), (x, 1 - y), (1 - x, 1 - y)]

        def slot(a, px, py, pc):
            return out_refs[a].at[4 * px + 2 * py + pc]

        def copy(a, k, block, to, src=None):
            return pltpu.make_async_remote_copy(
                src_ref=slot(a, *block) if src is None else src, dst_ref=slot(a, *block),
                send_sem=send_sems.at[a, k], recv_sem=recv_sems.at[a, k], device_id=to, device_id_type=_MESH)

        mine = [pltpu.make_async_copy(stage[a], slot(a, *me), local_sems.at[a]) for a in range(n)]
        for cp in mine:
            cp.start()
        first = []
        for j, chip in enumerate(chips):
            first += [copy(a, 1 + j, me, (*chip, c), src=stage[a]) for a in range(n)]
        first += [copy(a, 0, me, sibling, src=stage[a]) for a in range(n)]
        for cp in first:
            cp.start()
        passed = []
        for j, chip in enumerate(chips):
            for a in range(n):
                copy(a, 1 + j, (*chip, c), me).wait_recv()
                fwd = copy(a, 4 + j, (*chip, c), sibling)
                fwd.start()
                passed.append(fwd)
        for a in range(n):
            copy(a, 0, sibling, me).wait_recv()
        for j, chip in enumerate(chips):
            for a in range(n):
                copy(a, 4 + j, (*chip, 1 - c), me).wait_recv()
        for cp in first + passed:
            cp.wait_send()
        for cp in mine:
            cp.wait()

    res = pl.pallas_call(
        body, name=name, grid=(1,), in_specs=[_full(s.shape) for s in list(shards) + list(cast_only)],
        out_specs=[_ANY] * n + [_full(s.shape) for s in cast_only],
        out_shape=[_sds((N_DEV,) + shp, dt) for shp, dt in zip(shapes, dtypes)]
                  + [_sds(s.shape, MXU) for s in cast_only],
        scratch_shapes=[pltpu.VMEM(shp, dt) for shp, dt in zip(shapes, dtypes)]
                       + [pltpu.SemaphoreType.DMA((n, 7)), pltpu.SemaphoreType.DMA((n, 7)), pltpu.SemaphoreType.DMA((n,))],
        compiler_params=pltpu.CompilerParams(vmem_limit_bytes=VMEM_LIMIT),
    )(*_in_hbm([*shards, *cast_only]))
    return res[:n], res[n:]


_HBM = pl.BlockSpec(memory_space=pltpu.HBM)
_SEM = pl.BlockSpec(memory_space=pltpu.SEMAPHORE)
_EFFECT = pltpu.SideEffectType.DATAFLOW_SIDE_EFFECTING
_PEER_ORDER = (2, 4, 6, 3, 5, 7, 1)


def _peer(k):
    x, y, c = lax.axis_index("x"), lax.axis_index("y"), lax.axis_index("c")
    px = 1 - x if k & 4 else x
    py = 1 - y if k & 2 else y
    pc = 1 - c if k & 1 else c
    return (px, py, pc), 4 * px + 2 * py + pc


_SAME_CORE_AND_SIBLING = (2, 4, 6, 1)


def _push_start(srcs, lands, slotted, name, peers=_PEER_ORDER):
    n = len(srcs)

    def body(*refs):
        src_refs, land_refs = refs[:n], refs[n:2 * n]
        send_sems, recv_sems, token, own_sems = refs[2 * n], refs[2 * n + 1], refs[-2], refs[-1]
        mine = 4 * lax.axis_index("x") + 2 * lax.axis_index("y") + lax.axis_index("c")
        for k in peers:
            dev, theirs = _peer(k)
            for a in range(n):
                pltpu.make_async_remote_copy(
                    src_ref=src_refs[a].at[theirs] if slotted else src_refs[a], dst_ref=land_refs[a].at[mine],
                    send_sem=send_sems.at[7 * a + k - 1], recv_sem=recv_sems.at[7 * a + k - 1],
                    device_id=dev, device_id_type=_MESH).start()
        own = [pltpu.make_async_copy(src_refs[a].at[mine] if slotted else src_refs[a], land_refs[a].at[mine],
                                     own_sems.at[a]) for a in range(n)]
        for cp in own:
            cp.start()
        token[...] = jnp.zeros_like(token)
        for cp in own:
            cp.wait()

    bufs = list(srcs) + list(lands)
    res = pl.pallas_call(
        body, name=name, in_specs=[_HBM] * (2 * n),
        out_specs=(_SEM, _SEM, *[_HBM] * (2 * n), pl.BlockSpec(memory_space=pltpu.VMEM)),
        out_shape=(pltpu.SemaphoreType.DMA((7 * n,)), pltpu.SemaphoreType.DMA((7 * n,)),
                   *[pltpu.HBM(b.shape, b.dtype) for b in bufs], _sds((8, LANES))),
        scratch_shapes=[pltpu.SemaphoreType.DMA((n,))],
        input_output_aliases={i: 2 + i for i in range(2 * n)},
        compiler_params=pltpu.CompilerParams(has_side_effects=_EFFECT),
    )(*[pltpu.with_memory_space_constraint(b, pltpu.HBM) for b in bufs])
    return res[0], res[1], res[2:2 + n], res[2 + n:2 + 2 * n], res[-1]


def _push_wait(send_sems, recv_sems, srcs, lands, slotted, after, name, peers=_PEER_ORDER):
    n = len(srcs)

    def body(*refs):
        src_refs, land_refs = refs[:n], refs[n:2 * n]
        send_sems, recv_sems = refs[2 * n], refs[2 * n + 1]
        for k in peers:
            dev, theirs = _peer(k)
            for a in range(n):
                cp = pltpu.make_async_remote_copy(
                    src_ref=src_refs[a].at[theirs] if slotted else src_refs[a], dst_ref=land_refs[a].at[theirs],
                    send_sem=send_sems.at[7 * a + k - 1], recv_sem=recv_sems.at[7 * a + k - 1],
                    device_id=dev, device_id_type=_MESH)
                cp.wait_send()
                cp.wait_recv()

    bufs = list(srcs) + list(lands)
    res = pl.pallas_call(
        body, name=name, in_specs=[_HBM] * (2 * n) + [_SEM, _SEM] + [_ANY] * len(after), out_specs=[_HBM] * (2 * n),
        out_shape=[pltpu.HBM(b.shape, b.dtype) for b in bufs],
        input_output_aliases={i: i for i in range(2 * n)},
        compiler_params=pltpu.CompilerParams(has_side_effects=_EFFECT),
    )(*bufs, send_sems, recv_sems, *after)
    return res[n:]


def _other_chips():
    x, y = lax.axis_index("x"), lax.axis_index("y")
    return ((1 - x, y), (x, 1 - y), (1 - x, 1 - y))


def _relay_start(lands, name):
    n = len(lands)

    def body(*refs):
        land_refs = refs[:n]
        send_sems, recv_sems, token = refs[n], refs[n + 1], refs[-1]
        x, y, c = lax.axis_index("x"), lax.axis_index("y"), lax.axis_index("c")
        for j, (px, py) in enumerate(_other_chips()):
            slot = 4 * px + 2 * py + c
            for a in range(n):
                pltpu.make_async_remote_copy(
                    src_ref=land_refs[a].at[slot], dst_ref=land_refs[a].at[slot],
                    send_sem=send_sems.at[3 * a + j], recv_sem=recv_sems.at[3 * a + j],
                    device_id=(x, y, 1 - c), device_id_type=_MESH).start()
        token[...] = jnp.zeros_like(token)

    res = pl.pallas_call(
        body, name=name, in_specs=[_HBM] * n,
        out_specs=(_SEM, _SEM, *[_HBM] * n, pl.BlockSpec(memory_space=pltpu.VMEM)),
        out_shape=(pltpu.SemaphoreType.DMA((3 * n,)), pltpu.SemaphoreType.DMA((3 * n,)),
                   *[pltpu.HBM(b.shape, b.dtype) for b in lands], _sds((8, LANES))),
        input_output_aliases={i: 2 + i for i in range(n)},
        compiler_params=pltpu.CompilerParams(has_side_effects=_EFFECT),
    )(*[pltpu.with_memory_space_constraint(b, pltpu.HBM) for b in lands])
    return res[0], res[1], res[2:2 + n], res[-1]


def _relay_wait(send_sems, recv_sems, lands, after, name):
    n = len(lands)

    def body(*refs):
        land_refs = refs[:n]
        send_sems, recv_sems = refs[n], refs[n + 1]
        x, y, c = lax.axis_index("x"), lax.axis_index("y"), lax.axis_index("c")
        for j, (px, py) in enumerate(_other_chips()):
            sent, received = 4 * px + 2 * py + c, 4 * px + 2 * py + (1 - c)
            for a in range(n):
                cp = pltpu.make_async_remote_copy(
                    src_ref=land_refs[a].at[sent], dst_ref=land_refs[a].at[received],
                    send_sem=send_sems.at[3 * a + j], recv_sem=recv_sems.at[3 * a + j],
                    device_id=(x, y, 1 - c), device_id_type=_MESH)
                cp.wait_send()
                cp.wait_recv()

    return pl.pallas_call(
        body, name=name, in_specs=[_HBM] * n + [_SEM, _SEM] + [_ANY] * len(after), out_specs=[_HBM] * n,
        out_shape=[pltpu.HBM(b.shape, b.dtype) for b in lands],
        input_output_aliases={i: i for i in range(n)},
        compiler_params=pltpu.CompilerParams(has_side_effects=_EFFECT),
    )(*lands, send_sems, recv_sems, *after)


def _adamw(w, g, m, v):
    m2 = ADAM_B1 * m + (1.0 - ADAM_B1) * g
    v2 = ADAM_B2 * v + (1.0 - ADAM_B2) * (g * g)
    m_hat = m2 / (1.0 - ADAM_B1 ** ADAM_STEP)
    v_hat = v2 / (1.0 - ADAM_B2 ** ADAM_STEP)
    delta = -ADAM_LR * (m_hat / (jnp.sqrt(v_hat) + ADAM_EPS) + ADAM_WD * w)
    return delta, m2, v2


def _adam_shard(parts, w, m, v, name):
    _, r, c = w.shape
    tr = max(t for t in range(16, 257, 16) if r % t == 0)

    nparts = parts.shape[0]

    def body(p_ref, w_ref, m_ref, v_ref, g_ref, d_ref, m2_ref, v2_ref):
        g = p_ref[0].astype(F32)
        for s in range(1, nparts):
            g = g + p_ref[s].astype(F32)
        g_ref[0] = g
        d_ref[0], m2_ref[0], v2_ref[0] = _adamw(w_ref[0], g, m_ref[0], v_ref[0])

    row = lambda: pl.BlockSpec((1, tr, c), lambda i: (0, i, 0))
    return pl.pallas_call(
        body, name=name, grid=(r // tr,),
        in_specs=[pl.BlockSpec((nparts, tr, c), lambda i: (0, i, 0)), row(), row(), row()],
        out_specs=[row(), row(), row(), row()], out_shape=[_sds((1, r, c))] * 4,
        compiler_params=_cp("parallel"),
    )(*_in_hbm([parts, w, m, v]))


def _adam_small(gs, ws, ms, vs, name):
    n = len(gs)

    def body(*refs):
        ins, outs = refs[:4 * n], refs[4 * n:]
        for i in range(n):
            g = ins[i][...]
            d, m2, v2 = _adamw(ins[n + i][...], g, ins[2 * n + i][...], ins[3 * n + i][...])
            outs[i][...] = d
            outs[n + i][...] = m2
            outs[2 * n + i][...] = v2

    res = pl.pallas_call(
        body, name=name, grid=(1,), in_specs=[_full(w.shape) for w in ws] * 4,
        out_specs=[_full(w.shape) for w in ws] * 3, out_shape=[_sds(w.shape) for w in ws] * 3,
        compiler_params=_cp("arbitrary"),
    )(*_in_hbm([*gs, *ws, *ms, *vs]))
    return res[:n], res[n:2 * n], res[2 * n:]


def _pad_to(a, n, axis):
    extra = n - a.shape[axis]
    if extra == 0:
        return a
    widths = [(0, 0)] * a.ndim
    widths[axis] = (0, extra)
    return jnp.pad(a, widths)


def _ceil_to(n, k):
    return -(-n // k) * k


def _pack_rows(flats, rows_multiple):
    parts = [_pad_to(f, _ceil_to(f.shape[-1], LANES), f.ndim - 1) for f in flats]
    cat = jnp.concatenate(parts, axis=-1)
    total = _ceil_to(cat.shape[-1], LANES * rows_multiple)
    cat = _pad_to(cat, total, cat.ndim - 1)
    return cat.reshape(cat.shape[:-1] + (total // LANES, LANES))


def _unpack_rows(buf, sizes):
    flat = buf.reshape(buf.shape[:-2] + (-1,))
    out, off = [], 0
    for n in sizes:
        out.append(flat[..., off:off + n])
        off += _ceil_to(n, LANES)
    return out


_MIX_BIG = ("w_in", "w_glu", "w_proj_a", "w_proj_b", "w_out")
_BIG = _MIX_BIG + ("w_up", "w_down")
_SMALL = ("g_mix", "a_re", "a_im", "log_dt", "b_re", "b_im", "c_re", "c_im", "d_skip", "b_glu", "g_sgu", "w_s", "b_s",
          "g_ffn", "conv_b", "g_final")
_SMALL_ROWS_MULTIPLE = 8 * N_DEV
_TRANSPOSED = ("w_in", "w_up", "b_re", "b_im")


def _as_2d(a):
    return a.reshape(-1, a.shape[-1]) if a.ndim > 1 else a.reshape(1, -1)


def kernel(x, g_mix, w_in, a_re, a_im, log_dt, b_re, b_im, c_re, c_im, d_skip, w_glu, b_glu, w_proj_a, g_sgu, w_s, b_s, w_proj_b, w_out, g_ffn, w_up, conv_w, conv_b, w_down, g_final, loss_target, m_g_mix, m_w_in, m_a_re, m_a_im, m_log_dt, m_b_re, m_b_im, m_c_re, m_c_im, m_d_skip, m_w_glu, m_b_glu, m_w_proj_a, m_g_sgu, m_w_s, m_b_s, m_w_proj_b, m_w_out, m_g_ffn, m_w_up, m_conv_w, m_conv_b, m_w_down, m_g_final, v_g_mix, v_w_in, v_a_re, v_a_im, v_log_dt, v_b_re, v_b_im, v_c_re, v_c_im, v_d_skip, v_w_glu, v_b_glu, v_w_proj_a, v_g_sgu, v_w_s, v_b_s, v_w_proj_b, v_w_out, v_g_ffn, v_w_up, v_conv_w, v_conv_b, v_w_down, v_g_final):
    args = dict(locals())
    me = 4 * lax.axis_index("x") + 2 * lax.axis_index("y") + lax.axis_index("c")

    for n in _TRANSPOSED:
        for pre in ("", "m_", "v_"):
            args[pre + n] = jnp.swapaxes(args[pre + n], -1, -2)
    later = ("w_glu", "w_proj_a", "w_proj_b", "w_out", "w_up", "w_down")
    (w_in_g,), casts = _allgather([args["w_in"][0]], [MXU], "allgather_w_in", cast_only=[args[n][0] for n in later])
    sh = dict(zip(later, casts))

    def start_push(srcs, tag, peers):
        lands = [lax.empty((N_DEV,) + s.shape, s.dtype) for s in srcs]
        send_sems, recv_sems, srcs, lands, token = _push_start(srcs, lands, False, "push_" + tag, peers)
        return (send_sems, recv_sems, srcs, lands), token

    mix_push, token_a = start_push([sh[n] for n in later[:4]], "mixer_weights", _SAME_CORE_AND_SIBLING)
    ffn_push, token_b = start_push([sh["w_up"], sh["w_down"], conv_w[0]], "ffn_weights", _PEER_ORDER)
    p = {n: (args[n][0] if n != "g_final" else args[n]) for n in _SMALL if n not in _TRANSPOSED}
    p.update(w_in_t=w_in_g.reshape(SSM_W + 2 * SGU_W + 2 * D_MODEL, D_MODEL),
             b_re_t=args["b_re"][0], b_im_t=args["b_im"][0])
    relay = {}

    def mixer_relay(after):
        lands = _push_wait(*mix_push, False, [after], "wait_mixer_weights", _SAME_CORE_AND_SIBLING)
        relay["send"], relay["recv"], relay["lands"], token = _relay_start(lands, "relay_mixer_weights")
        return token

    def mixer_weights(after):
        w_glu_g, w_pa_g, w_pb_g, w_out_g = _relay_wait(relay["send"], relay["recv"], relay["lands"], [after],
                                                       "wait_relay_mixer_weights")
        w_pa_full, w_pb_full = _assemble_cols([w_pa_g, w_pb_g], "assemble_cols")
        return dict(w_glu=w_glu_g.reshape(SSM_W, SSM_W), w_proj_a=w_pa_full, w_proj_b=w_pb_full,
                    w_out=w_out_g.reshape(D_MODEL, D_MODEL))

    def ffn_weights(after):
        w_up_g, w_down_g, conv_w_g = _push_wait(*ffn_push, False, [after], "wait_ffn_weights")
        return w_up_g, conv_w_g, w_down_g.reshape(D_FF, D_MODEL)

    pushes = []

    def grads_out(names, sends):
        lands = [lax.empty(s.shape, s.dtype) for s in sends]
        send_sems, recv_sems, srcs, lands, token = _push_start(list(sends), lands, True, "push_grads_" + names[0])
        pushes.append((names, send_sems, recv_sems, srcs, lands))
        return token


    small_names = _SMALL + ("conv_w", "loss")
    small = {}

    def small_out(grads, loss_part):
        small_g = dict(grads, loss=loss_part[0, 0:1])
        flats = [small_g[n].reshape(-1) for n in small_names]
        small["sizes"] = [f.shape[0] for f in flats]
        g_small = _pack_rows(flats, _SMALL_ROWS_MULTIPLE)
        small["rs8"] = g_small.shape[0] // N_DEV
        return grads_out(("small",), (g_small.reshape(N_DEV, small["rs8"], LANES),))

    grad_x, small_token = _local_step(x[0], loss_target[0], p, (token_a, token_b), mixer_relay, mixer_weights,
                                      ffn_weights, grads_out, small_out)

    out = {}
    done = [grad_x, small_token]
    for names, send_sems, recv_sems, srcs, lands in pushes:
        parts = _push_wait(send_sems, recv_sems, srcs, lands, True, done, "wait_grads_" + names[0])
        if names == ("small",):
            g_small_all = _allgather([parts[0]], [F32], "allgather_small", sum_slots=True)[0][0].reshape(
                N_DEV * small["rs8"], LANES)
            pieces = dict(zip(small_names, _unpack_rows(g_small_all, small["sizes"])))
            loss = pieces["loss"][0]
            dconv_w = lax.dynamic_index_in_dim(pieces["conv_w"].reshape(N_DEV, 3, FF_SHARD), me, axis=0, keepdims=False)
            names2 = _SMALL + ("conv_w",)
            gs = [pieces[n].reshape(_as_2d(args[n]).shape) for n in _SMALL] + [dconv_w]
            ds, m2s, v2s = _adam_small(gs, [_as_2d(args[n]) for n in names2], [_as_2d(args["m_" + n]) for n in names2],
                                       [_as_2d(args["v_" + n]) for n in names2], "adam_small")
            for n, res in zip(names2, zip(gs, ds, m2s, v2s)):
                for kind, v in zip(("grad_", "delta_", "new_m_", "new_v_"), res):
                    out[kind + n] = v.reshape(args[n].shape)
            done = [ds[0]]
            continue
        done = []
        for n, part in zip(names, parts):
            res = _adam_shard(part, args[n], args["m_" + n], args["v_" + n], "adam_" + n)
            for kind, v in zip(("grad_", "delta_", "new_m_", "new_v_"), res):
                out[kind + n] = v
            done.append(res[0])
    order = ("g_mix", "w_in", "a_re", "a_im", "log_dt", "b_re", "b_im", "c_re", "c_im", "d_skip", "w_glu", "b_glu",
             "w_proj_a", "g_sgu", "w_s", "b_s", "w_proj_b", "w_out", "g_ffn", "w_up", "conv_w", "conv_b", "w_down",
             "g_final")
    res = [loss, grad_x.reshape(x.shape)]
    for kind in ("grad_", "delta_", "new_m_", "new_v_"):
        res += [jnp.swapaxes(out[kind + n], -1, -2) if n in _TRANSPOSED else out[kind + n] for n in order]
    return tuple(res)
```

```python
import math

import jax
import jax.numpy as jnp
from jax import lax
from jax.experimental import pallas as pl
from jax.experimental.pallas import tpu as pltpu

F32 = jnp.float32
MXU = jnp.bfloat16
EPS = 1e-6

D_MODEL = 1024
SSM_W = 512
SSM_G, SSM_H, SSM_P = 32, 16, 64
SSM_BLK = 4
SGU_W = 512
SGU_G, SGU_D, CHUNK = 8, 64, 128
D_FF = 2816
N_DEV = 8
FF_SHARD = 2 * D_FF // N_DEV
FF_CW = 2 * FF_SHARD
FF_NCB = D_FF // FF_CW
LANES = 128

ADAM_LR, ADAM_B1, ADAM_B2, ADAM_EPS, ADAM_WD, ADAM_STEP = 0.001, 0.9, 0.999, 1e-08, 0.01, 10

VMEM_LIMIT = 48 * 1024 * 1024
WGRAD_VMEM_LIMIT = 58 * 1024 * 1024
FFN_VMEM_LIMIT = 58 * 1024 * 1024


def _cp(*sem):
    return pltpu.CompilerParams(dimension_semantics=sem, vmem_limit_bytes=VMEM_LIMIT)


def _full(shape):
    n = len(shape)
    return pl.BlockSpec(shape, lambda *_: (0,) * n)


def _sds(shape, dtype=F32):
    return jax.ShapeDtypeStruct(shape, dtype)


def _in_hbm(arrays):
    return [pltpu.with_memory_space_constraint(a, pltpu.HBM) for a in arrays]


def _behind(body, n_in, after):
    def ordered(*refs):
        body(*refs[:n_in], *refs[n_in + len(after):])
    return ordered


def _dot(a, b):
    return jnp.dot(a, b, preferred_element_type=F32)


def _dot_nt(a, b):
    return lax.dot_general(a, b, (((1,), (1,)), ((), ())), preferred_element_type=F32)


def _dot_tn(a, b):
    return lax.dot_general(a, b, (((0,), (0,)), ((), ())), preferred_element_type=F32)


_GELU_C = math.sqrt(2.0 / math.pi)


def _gelu(x):
    return 0.5 * x * (1.0 + jnp.tanh(_GELU_C * (x + 0.044715 * (x * x * x))))


def _gelu_and_grad(x):
    t = jnp.tanh(_GELU_C * (x + 0.044715 * (x * x * x)))
    g = 0.5 * x * (1.0 + t)
    dg = 0.5 * (1.0 + t) + 0.5 * x * (1.0 - t * t) * (_GELU_C * (1.0 + 3.0 * 0.044715 * (x * x)))
    return g, dg


def _sigmoid(x):
    return 0.5 * jnp.tanh(0.5 * x) + 0.5


def _rms(x):
    return lax.rsqrt(jnp.mean(x * x, axis=-1, keepdims=True) + EPS)


def _rms_bwd(dxn, xn, r):
    return r * (dxn - xn * jnp.mean(dxn * xn, axis=-1, keepdims=True))


def _rowsum(x):
    return jnp.sum(x, axis=0, keepdims=True)


def _fetch_once(pairs, sems):
    copies = [pltpu.make_async_copy(src, dst, sems.at[k]) for k, (src, dst) in enumerate(pairs)]
    for cp in copies:
        cp.start()
    for cp in copies:
        cp.wait()


def _s5_disc(are, aim, ldt, br, bi):
    dt = jnp.exp(ldt)
    mag = jnp.exp(dt * are)
    abr = mag * jnp.cos(dt * aim)
    abi = mag * jnp.sin(dt * aim)
    den = are * are + aim * aim
    nr = abr - 1.0
    ni = abi
    fr = (nr * are + ni * aim) / den
    fi = (ni * are - nr * aim) / den
    return abr, abi, fr * br - fi * bi, fr * bi + fi * br


def _s5_params_fwd(are, aim, ldt, br, bi):
    def body(are_ref, aim_ref, ldt_ref, br_ref, bi_ref, o0, o1, o2, o3):
        outs = _s5_disc(are_ref[...], aim_ref[...], ldt_ref[...], br_ref[...], bi_ref[...])
        for o, v in zip((o0, o1, o2, o3), outs):
            o[...] = v
    shp = are.shape
    return pl.pallas_call(body, name="s5_params_fwd", grid=(1,), in_specs=[_full(shp)] * 5, out_specs=[_full(shp)] * 4,
                          out_shape=[_sds(shp)] * 4)(*_in_hbm([are, aim, ldt, br, bi]))


def _s5_params_bwd(are, aim, ldt, br, bi, dabr, dabi, dbr, dbi):
    def body(are_ref, aim_ref, ldt_ref, br_ref, bi_ref, c0, c1, c2, c3, o0, o1, o2, o3, o4):
        prim = (are_ref[...], aim_ref[...], ldt_ref[...], br_ref[...], bi_ref[...])
        _, vjp = jax.vjp(_s5_disc, *prim)
        outs = vjp((c0[...], c1[...], c2[...], c3[...]))
        for o, v in zip((o0, o1, o2, o3, o4), outs):
            o[...] = v
    shp = are.shape
    return pl.pallas_call(body, name="s5_params_bwd", grid=(1,), in_specs=[_full(shp)] * 9, out_specs=[_full(shp)] * 5,
                          out_shape=[_sds(shp)] * 5)(*_in_hbm([are, aim, ldt, br, bi, dabr, dabi, dbr, dbi]))


def _blockdiag(m_t):
    m = m_t.reshape(SSM_BLK, 8, SSM_H, 1, SSM_P)
    eye = jnp.eye(8, dtype=bool).reshape(1, 8, 1, 8, 1)
    return jnp.where(eye, m, jnp.zeros((), m_t.dtype)).reshape(SSM_BLK, 8 * SSM_H, 8 * SSM_P)


def _unblockdiag(pc):
    m = pc.reshape(SSM_BLK, 8, SSM_H, 8, SSM_P)
    return jnp.einsum("jghgp->jghp", m).reshape(SSM_G * SSM_H, SSM_P)


def _in_fwd(x, g_mix, w_in_t, tm, after=()):
    S = x.shape[0]

    def body(x_ref, g_ref, w_ref, h_ref, us_ref, uv_ref, gl_ref):
        xv = x_ref[...]
        h = (xv * _rms(xv) * g_ref[...]).astype(MXU)
        h_ref[...] = h
        us_ref[...] = _dot_nt(h, w_ref[0:SSM_W, :])
        uv_ref[...] = _dot_nt(h, w_ref[SSM_W:SSM_W + 2 * SGU_W, :])
        gl_ref[...] = _dot_nt(h, w_ref[SSM_W + 2 * SGU_W:, :])

    row = lambda n: pl.BlockSpec((tm, n), lambda i: (i, 0))
    return pl.pallas_call(
        _behind(body, 3, after), name="in_fwd", grid=(S // tm,),
        in_specs=[row(D_MODEL), _full((1, D_MODEL)), _full(w_in_t.shape)] + [_ANY] * len(after),
        out_specs=[row(D_MODEL), row(SSM_W), row(2 * SGU_W), row(2 * D_MODEL)],
        out_shape=[_sds((S, D_MODEL), MXU), _sds((S, SSM_W)), _sds((S, 2 * SGU_W)), _sds((S, 2 * D_MODEL))],
        compiler_params=_cp("parallel"),
    )(*_in_hbm([x, g_mix, w_in_t]), *after)


def _scan_tables(ar, ai, reverse):
    n = ar.shape[-1]
    def mul(p, q):
        return p[0] * q[0] - p[1] * q[1], p[0] * q[1] + p[1] * q[0]
    a1 = (ar, ai)
    a2 = mul(a1, a1)
    a3 = mul(a2, a1)
    a4 = mul(a2, a2)
    a5 = mul(a4, a1)
    a6 = mul(a4, a2)
    a7 = mul(a4, a3)
    a8 = mul(a4, a4)
    pw = (a1, a2, a3, a4, a5, a6, a7, a8)
    rows = lax.broadcasted_iota(jnp.int32, (8, n), 0)
    tabs = []
    for s, a in ((1, a1), (2, a2), (4, a4)):
        keep = (rows + s <= 7) if reverse else (rows >= s)
        for comp in a:
            tabs.append(jnp.where(keep, jnp.broadcast_to(comp, (8, n)), 0.0))
    for c in range(2):
        q = jnp.zeros((8, n), F32)
        for r in range(8):
            e = (8 - r) if reverse else (r + 1)
            q = jnp.where(rows == r, jnp.broadcast_to(pw[e - 1][c], (8, n)), q)
        tabs.append(q)
    return tabs


def _scan_group(xr, xi, tab_ref, cr, ci, reverse):
    for t, s in enumerate((1, 2, 4)):
        pr = tab_ref[2 * t]
        pi = tab_ref[2 * t + 1]
        sh = (8 - s) if reverse else s
        sr = pltpu.roll(xr, sh, 0)
        si = pltpu.roll(xi, sh, 0)
        xr, xi = xr + pr * sr - pi * si, xi + pr * si + pi * sr
    qr = tab_ref[6]
    qi = tab_ref[7]
    return xr + qr * cr - qi * ci, xi + qr * ci + qi * cr


def _runs_load(src_ref, dst_ref, run):
    for i in range(run):
        dst_ref[8 * i:8 * i + 8, :] = src_ref[pl.ds(i, 8, stride=run), :]


def _runs_store(val, dst_ref, run):
    for i in range(run):
        dst_ref[pl.ds(i, 8, stride=run), :] = val[8 * i:8 * i + 8, :]


def _cpow2(ar, ai, log2n):
    for _ in range(log2n):
        ar, ai = ar * ar - ai * ai, 2.0 * ar * ai
    return ar, ai


def _s5_fwd(us, abar_re, abar_im, b_re, b_im, c_re, c_im, d_skip, tm, after=()):
    S = us.shape[0]
    nt = S // tm
    w = 8 * SSM_P
    run = tm // 8
    assert run & (run - 1) == 0

    def body(us_ref, ar_ref, ai_ref, br_ref, bi_ref, cr_ref, ci_ref, d_ref, str_ref, sti_ref, ys_ref,
             tab_ref, car_ref, up_ref):
        i = pl.program_id(1)

        @pl.when(i == 0)
        def _():
            car_ref[...] = jnp.zeros_like(car_ref)
            for k, t in enumerate(_scan_tables(*_cpow2(ar_ref[...], ai_ref[...], run.bit_length() - 1), False)):
                tab_ref[k] = t

        _runs_load(us_ref, up_ref, run)
        ub = up_ref[...].astype(MXU)
        str_ref[...] = _dot(ub, br_ref[0])
        sti_ref[...] = _dot(ub, bi_ref[0])
        ar = jnp.broadcast_to(ar_ref[...], (8, w))
        ai = jnp.broadcast_to(ai_ref[...], (8, w))

        def advance(k, state):
            r0 = pl.multiple_of(k * 8, 8)
            sr, si = state
            return (ar * sr - ai * si + str_ref[pl.ds(r0, 8), :], ar * si + ai * sr + sti_ref[pl.ds(r0, 8), :])

        def emit(k, state):
            r0 = pl.multiple_of(k * 8, 8)
            sr, si = advance(k, state)
            str_ref[pl.ds(r0, 8), :] = sr
            sti_ref[pl.ds(r0, 8), :] = si
            return sr, si

        zero = jnp.zeros((8, w), F32)
        er, ei = lax.fori_loop(0, run, advance, (zero, zero))
        cr, ci = car_ref[0:1, :], car_ref[1:2, :]
        tr, ti = _scan_group(er, ei, tab_ref, cr, ci, False)
        r8 = lax.broadcasted_iota(jnp.int32, (8, w), 0)
        start = (jnp.where(r8 == 0, cr, pltpu.roll(tr, 1, 0)), jnp.where(r8 == 0, ci, pltpu.roll(ti, 1, 0)))
        car_ref[0:1, :] = tr[7:8, :]
        car_ref[1:2, :] = ti[7:8, :]
        lax.fori_loop(0, run, emit, start)
        y = _dot_nt(str_ref[...].astype(MXU), cr_ref[0]) - _dot_nt(sti_ref[...].astype(MXU), ci_ref[0])
        _runs_store(y, ys_ref, run)
        ys_ref[...] += d_ref[...] * us_ref[...]

    blk = lambda: pl.BlockSpec((1, 8 * SSM_H, w), lambda j, i: (j, 0, 0))
    return pl.pallas_call(
        _behind(body, 8, after), name="s5_fwd", grid=(SSM_BLK, nt),
        in_specs=[pl.BlockSpec((tm, LANES), lambda j, i: (i, j)),
                  pl.BlockSpec((1, w), lambda j, i: (0, j)), pl.BlockSpec((1, w), lambda j, i: (0, j)),
                  blk(), blk(), blk(), blk(),
                  pl.BlockSpec((1, LANES), lambda j, i: (0, j))] + [_ANY] * len(after),
        out_specs=[pl.BlockSpec((tm, w), lambda j, i: (i, j)), pl.BlockSpec((tm, w), lambda j, i: (i, j)),
                   pl.BlockSpec((tm, LANES), lambda j, i: (i, j))],
        out_shape=[_sds((S, SSM_BLK * w)), _sds((S, SSM_BLK * w)), _sds((S, SSM_W))],
        scratch_shapes=[pltpu.VMEM((8, 8, w), F32), pltpu.VMEM((8, w), F32), pltpu.VMEM((tm, LANES), F32)],
        compiler_params=_cp("parallel", "arbitrary"),
    )(*_in_hbm([us, abar_re, abar_im, b_re, b_im, c_re, c_im, d_skip]), *after)


def _group_halves(vp):
    first = lax.broadcasted_iota(jnp.int32, vp.shape, 1) < SGU_D
    zero = jnp.zeros((), vp.dtype)
    return jnp.where(first, vp, zero), jnp.where(first, zero, vp)


def _sgu_mix(vnb, wcat_ref):
    outs = []
    for q in range(SGU_G // 2):
        lo, hi = _group_halves(vnb[:, LANES * q:LANES * (q + 1)])
        outs.append(_dot(wcat_ref[q], jnp.concatenate([lo, hi], axis=0)))
    return jnp.concatenate(outs, axis=1)


def _mix_fwd(x, ys, uv, gl, w_glu, b_glu, w_pa, g_sgu, ws, bias_s, w_pb, w_out, g_ffn, tm):
    S = x.shape[0]

    def body(x_ref, ys_ref, uv_ref, gl_ref, wglu_ref, bglu_ref, wpa_ref, gs_ref, ws_ref, bias_ref, wpb_ref, wout_ref,
             gf_ref, yg_ref, yap_ref, sg_ref, ya_ref, yb_ref, m_ref, x1_ref, h2_ref):
        yg = _gelu(ys_ref[...])
        ygb = yg.astype(MXU)
        yg_ref[...] = ygb
        z = _dot(ygb, wglu_ref[...]) + bglu_ref[...]
        yapb = (yg * _sigmoid(z)).astype(MXU)
        yap_ref[...] = yapb
        ya = _dot(yapb, wpa_ref[...])
        ya_ref[...] = ya

        uvg = _gelu(uv_ref[...])
        u2 = uvg[:, :SGU_W]
        v2 = uvg[:, SGU_W:]
        vnb = (v2 * _rms(v2) * gs_ref[...]).astype(MXU)
        for c in range(tm // CHUNK):
            rs = slice(c * CHUNK, (c + 1) * CHUNK)
            mixed = _sgu_mix(vnb[rs], ws_ref) + bias_ref[...]
            sg_ref[rs, :] = (u2[rs] * mixed).astype(MXU)
        yb = _dot(sg_ref[...], wpb_ref[...])
        yb_ref[...] = yb

        glv = gl_ref[...]
        m = _sigmoid(glv[:, :D_MODEL]) * ya + _sigmoid(glv[:, D_MODEL:]) * yb
        mb = m.astype(MXU)
        m_ref[...] = mb
        x1 = x_ref[...] + _dot(mb, wout_ref[...])
        x1_ref[...] = x1
        h2_ref[...] = (x1 * _rms(x1) * gf_ref[...]).astype(MXU)

    row = lambda n: pl.BlockSpec((tm, n), lambda i: (i, 0))
    return pl.pallas_call(
        body, name="mix_fwd", grid=(S // tm,),
        in_specs=[row(D_MODEL), row(SSM_W), row(2 * SGU_W), row(2 * D_MODEL),
                  _full(w_glu.shape), _full(b_glu.shape), _full(w_pa.shape), _full(g_sgu.shape), _full(ws.shape),
                  _full(bias_s.shape), _full(w_pb.shape), _full(w_out.shape), _full(g_ffn.shape)],
        out_specs=[row(SSM_W), row(SSM_W), row(SGU_W), row(D_MODEL), row(D_MODEL), row(D_MODEL), row(D_MODEL),
                   row(D_MODEL)],
        out_shape=[_sds((S, SSM_W), MXU), _sds((S, SSM_W), MXU), _sds((S, SGU_W), MXU), _sds((S, D_MODEL)),
                   _sds((S, D_MODEL)), _sds((S, D_MODEL), MXU), _sds((S, D_MODEL)), _sds((S, D_MODEL), MXU)],
        compiler_params=_cp("parallel"),
    )(*_in_hbm([x, ys, uv, gl, w_glu, b_glu, w_pa, g_sgu, ws, bias_s, w_pb, w_out, g_ffn]))


def _causal_conv3(u, prev8, cw, cb):
    tm = u.shape[0]
    w0, w1, w2 = cw[0:1], cw[1:2], cw[2:3]
    body = w0 * pltpu.roll(u, 2, 0) + w1 * pltpu.roll(u, 1, 0) + w2 * u + cb
    u8 = u[0:8, :]
    r8 = lax.broadcasted_iota(jnp.int32, u8.shape, 0)
    t1 = prev8[7:8, :]
    t0 = prev8[6:7, :]
    s1 = jnp.where(r8 == 0, t1, pltpu.roll(u8, 1, 0))
    s2 = jnp.where(r8 == 0, t0, jnp.where(r8 == 1, t1, pltpu.roll(u8, 2, 0)))
    first = w0 * s2 + w1 * s1 + w2 * u8 + cb
    return jnp.concatenate([first, body[8:tm, :]], axis=0)


def _causal_conv3_adjoint(d, next8, cw):
    tm = d.shape[0]
    w0, w1, w2 = cw[0:1], cw[1:2], cw[2:3]
    n1 = pltpu.roll(d, tm - 1, 0)
    n2 = pltpu.roll(d, tm - 2, 0)
    body = w2 * d + w1 * n1 + w0 * n2
    d8 = d[tm - 8:tm, :]
    r8 = lax.broadcasted_iota(jnp.int32, d8.shape, 0)
    h0 = next8[0:1, :]
    h1 = next8[1:2, :]
    m1 = jnp.where(r8 == 7, h0, pltpu.roll(d8, 7, 0))
    m2 = jnp.where(r8 == 6, h0, jnp.where(r8 == 7, h1, pltpu.roll(d8, 6, 0)))
    last = w2 * d8 + w1 * m1 + w0 * m2
    out = jnp.concatenate([body[0:tm - 8, :], last], axis=0)
    return out, n1, n2, h0 - d[0:1, :], h1 - d[1:2, :]


def _ffn_fwd(h2, x1, tgt, w_up, conv_w, conv_b, w_down, g_final, tm):
    S = h2.shape[0]
    nt = S // tm
    ncb = FF_NCB

    def body(h2_ref, wup_hbm, cwa_ref, cwb_ref, cba_ref, cbb_ref, wd_hbm, x1_ref, gf_ref, tgt_ref,
             up_ref, ab_ref, ff_ref, dx2_ref, dx2b_ref, loss_ref, dgf_ref, acc_ref, tail_ref, wup_ref, wdn_ref, wsem):
        i = pl.program_id(0)
        cb = pl.program_id(1)

        @pl.when(i == 0)
        def _():
            tail_ref[cb] = jnp.zeros((2, 8, FF_CW), F32)

        @pl.when(jnp.logical_and(i == 0, cb == 0))
        def _():
            loss_ref[...] = jnp.zeros_like(loss_ref)
            dgf_ref[...] = jnp.zeros_like(dgf_ref)
            _fetch_once([(wup_hbm, wup_ref), (wd_hbm, wdn_ref)], wsem)

        h2v = h2_ref[...]
        ua = _dot_nt(h2v, wup_ref[cb])
        ub = _dot_nt(h2v, wup_ref[ncb + cb])
        up_ref[0, 0] = ua.astype(MXU)
        up_ref[1, 0] = ub.astype(MXU)
        a = _causal_conv3(ua, tail_ref[cb, 0], cwa_ref[0], cba_ref[0])
        b = _causal_conv3(ub, tail_ref[cb, 1], cwb_ref[0], cbb_ref[0])
        tail_ref[cb, 0] = ua[tm - 8:tm, :]
        tail_ref[cb, 1] = ub[tm - 8:tm, :]
        ab_ref[0, 0] = a
        ab_ref[1, 0] = b
        ffb = (a * _sigmoid(a) * b).astype(MXU)
        ff_ref[0] = ffb
        contrib = _dot(ffb, wdn_ref[pl.ds(pl.multiple_of(cb * FF_CW, FF_CW), FF_CW), :])

        @pl.when(cb == 0)
        def _():
            acc_ref[...] = contrib

        @pl.when(cb > 0)
        def _():
            acc_ref[...] += contrib

        @pl.when(cb == ncb - 1)
        def _():
            x2 = x1_ref[...] + acc_ref[...]
            r = _rms(x2)
            xn = x2 * r
            g = gf_ref[...]
            diff = xn * g - tgt_ref[...]
            loss_ref[...] += (0.5 / D_MODEL) * jnp.sum(diff * diff)
            dy = diff * (1.0 / D_MODEL)
            dgf_ref[...] += _rowsum(dy * xn)
            dx2 = _rms_bwd(dy * g, xn, r)
            dx2_ref[...] = dx2
            dx2b_ref[...] = dx2.astype(MXU)

    row = lambda n: pl.BlockSpec((tm, n), lambda i, c: (i, 0))
    gate = lambda r: pl.BlockSpec((1, r, FF_CW), lambda i, c: (c, 0, 0))
    lin = lambda r: pl.BlockSpec((1, r, FF_CW), lambda i, c: (ncb + c, 0, 0))
    return pl.pallas_call(
        body, name="ffn_fwd", grid=(nt, ncb),
        in_specs=[row(D_MODEL), _ANY, gate(3), lin(3), gate(1), lin(1), _ANY,
                  row(D_MODEL), _full((1, D_MODEL)), row(D_MODEL)],
        out_specs=[pl.BlockSpec((2, 1, tm, FF_CW), lambda i, c: (0, c, i, 0)),
                   pl.BlockSpec((2, 1, tm, FF_CW), lambda i, c: (0, c, i, 0)),
                   pl.BlockSpec((1, tm, FF_CW), lambda i, c: (c, i, 0)),
                   row(D_MODEL), row(D_MODEL), _full((1, LANES)), _full((1, D_MODEL))],
        out_shape=[_sds((2, ncb, S, FF_CW), MXU), _sds((2, ncb, S, FF_CW)), _sds((ncb, S, FF_CW), MXU),
                   _sds((S, D_MODEL)), _sds((S, D_MODEL), MXU), _sds((1, LANES)), _sds((1, D_MODEL))],
        scratch_shapes=[pltpu.VMEM((tm, D_MODEL), F32), pltpu.VMEM((ncb, 2, 8, FF_CW), F32),
                        pltpu.VMEM(w_up.shape, w_up.dtype), pltpu.VMEM(w_down.shape, w_down.dtype),
                        pltpu.SemaphoreType.DMA((2,))],
        compiler_params=pltpu.CompilerParams(dimension_semantics=("arbitrary", "arbitrary"),
                                             vmem_limit_bytes=FFN_VMEM_LIMIT),
    )(*_in_hbm([h2, w_up, conv_w, conv_w, conv_b, conv_b, w_down, x1, g_final, tgt]))


def _ffn_bwd(dx2, up, ab, x1, w_up, conv_w, w_down, g_ffn, tm):
    S = dx2.shape[0]
    nt = S // tm
    ncb = FF_NCB

    def body(dx2_ref, up_ref, ab_ref, cwa_ref, cwb_ref, wd_hbm, wup_hbm,
             x1_ref, g_ref, dup_ref, dx1_ref, dx1b_ref, dconv_ref, dg_ref, acc_ref, head_ref, wup_ref, wdn_ref, wsem):
        i = pl.program_id(0)
        cb = pl.program_id(1)

        @pl.when(i == 0)
        def _():
            head_ref[cb] = jnp.zeros((2, 8, FF_CW), F32)
            dconv_ref[cb] = jnp.zeros((8, FF_CW), F32)
            dconv_ref[ncb + cb] = jnp.zeros((8, FF_CW), F32)

        @pl.when(jnp.logical_and(i == 0, cb == 0))
        def _():
            dg_ref[...] = jnp.zeros_like(dg_ref)
            _fetch_once([(wup_hbm, wup_ref), (wd_hbm, wdn_ref)], wsem)

        dff = _dot_nt(dx2_ref[...].astype(MXU), wdn_ref[pl.ds(pl.multiple_of(cb * FF_CW, FF_CW), FF_CW), :])
        a = ab_ref[0, 0]
        b = ab_ref[1, 0]
        sa = _sigmoid(a)
        silu = a * sa
        da = (dff * b) * (sa + silu * (1.0 - sa))
        db = dff * silu
        dps = []
        for half, slot, d, cw_ref in ((0, cb, da, cwa_ref), (1, ncb + cb, db, cwb_ref)):
            dp, n1, n2, fix0, fix1 = _causal_conv3_adjoint(d, head_ref[cb, half], cw_ref[0])
            head_ref[cb, half] = d[0:8, :]
            dpb16 = dp.astype(MXU)
            dup_ref[half, 0] = dpb16
            dps.append(dpb16)
            u = up_ref[half, 0].astype(F32)
            u_last = u[tm - 1:tm, :]
            dconv_ref[slot, 0:1, :] += _rowsum(n2 * u) + fix0 * u[tm - 2:tm - 1, :] + fix1 * u_last
            dconv_ref[slot, 1:2, :] += _rowsum(n1 * u) + fix0 * u_last
            dconv_ref[slot, 2:3, :] += _rowsum(d * u)
            dconv_ref[slot, 3:4, :] += _rowsum(d)
        contrib = _dot(dps[0], wup_ref[cb]) + _dot(dps[1], wup_ref[ncb + cb])

        @pl.when(cb == 0)
        def _():
            acc_ref[...] = contrib

        @pl.when(cb > 0)
        def _():
            acc_ref[...] += contrib

        @pl.when(cb == ncb - 1)
        def _():
            x1v = x1_ref[...]
            r = _rms(x1v)
            xn = x1v * r
            dh2 = acc_ref[...]
            dg_ref[...] += _rowsum(dh2 * xn)
            dx1 = dx2_ref[...] + _rms_bwd(dh2 * g_ref[...], xn, r)
            dx1_ref[...] = dx1
            dx1b_ref[...] = dx1.astype(MXU)

    row = lambda n: pl.BlockSpec((tm, n), lambda i, c: (nt - 1 - i, 0))
    colb = lambda: pl.BlockSpec((2, 1, tm, FF_CW), lambda i, c: (0, c, nt - 1 - i, 0))
    gate = lambda r: pl.BlockSpec((1, r, FF_CW), lambda i, c: (c, 0, 0))
    lin = lambda r: pl.BlockSpec((1, r, FF_CW), lambda i, c: (ncb + c, 0, 0))
    return pl.pallas_call(
        body, name="ffn_bwd", grid=(nt, ncb),
        in_specs=[row(D_MODEL), colb(), colb(), gate(3), lin(3), _ANY, _ANY, row(D_MODEL), _full((1, D_MODEL))],
        out_specs=[colb(), row(D_MODEL), row(D_MODEL), _full((2 * ncb, 8, FF_CW)), _full((1, D_MODEL))],
        out_shape=[_sds((2, ncb, S, FF_CW), MXU), _sds((S, D_MODEL)), _sds((S, D_MODEL), MXU), _sds((2 * ncb, 8, FF_CW)),
                   _sds((1, D_MODEL))],
        scratch_shapes=[pltpu.VMEM((tm, D_MODEL), F32), pltpu.VMEM((ncb, 2, 8, FF_CW), F32),
                        pltpu.VMEM(w_up.shape, w_up.dtype), pltpu.VMEM(w_down.shape, w_down.dtype),
                        pltpu.SemaphoreType.DMA((2,))],
        compiler_params=pltpu.CompilerParams(dimension_semantics=("arbitrary", "arbitrary"),
                                             vmem_limit_bytes=FFN_VMEM_LIMIT),
    )(*_in_hbm([dx2, up, ab, conv_w, conv_w, w_down, w_up, x1, g_ffn]))


def _mix_bwd(dx1, gl, ya, yb, ys, uv, w_out, w_pa, w_pb, w_glu, b_glu, g_sgu, ws, ws_t, bias_s, tm, after=()):
    S = dx1.shape[0]

    def body(dx1_ref, gl_ref, ya_ref, yb_ref, ys_ref, uv_ref, wout_ref, wpa_ref, wpb_ref, wglu_ref, bglu_ref, gs_ref,
             ws_ref, wst_ref, bias_ref,
             dgl_ref, dya_ref, dyb_ref, dz_ref, dys_ref, duv_ref, dbglu_ref, dgs_ref, dws_ref, dbs_ref,
             du2_ref, dvn_ref):
        i = pl.program_id(0)

        @pl.when(i == 0)
        def _():
            dbglu_ref[...] = jnp.zeros_like(dbglu_ref)
            dgs_ref[...] = jnp.zeros_like(dgs_ref)
            dws_ref[...] = jnp.zeros_like(dws_ref)
            dbs_ref[...] = jnp.zeros_like(dbs_ref)

        dm = _dot_nt(dx1_ref[...].astype(MXU), wout_ref[...])
        glv = gl_ref[...]
        ga = _sigmoid(glv[:, :D_MODEL])
        gb = _sigmoid(glv[:, D_MODEL:])
        dgl_ref[:, :D_MODEL] = (dm * ya_ref[...] * ga * (1.0 - ga)).astype(MXU)
        dgl_ref[:, D_MODEL:] = (dm * yb_ref[...] * gb * (1.0 - gb)).astype(MXU)
        dyab = (dm * ga).astype(MXU)
        dybb = (dm * gb).astype(MXU)
        dya_ref[...] = dyab
        dyb_ref[...] = dybb

        dyap = _dot_nt(dyab, wpa_ref[...])
        yg, dgelu = _gelu_and_grad(ys_ref[...])
        sz = _sigmoid(_dot(yg.astype(MXU), wglu_ref[...]) + bglu_ref[...])
        dz = dyap * yg * sz * (1.0 - sz)
        dzb = dz.astype(MXU)
        dz_ref[...] = dzb
        dbglu_ref[...] += _rowsum(dz)
        dys_ref[...] = (dyap * sz + _dot_nt(dzb, wglu_ref[...])) * dgelu

        dsg = _dot_nt(dybb, wpb_ref[...])
        uvg, duvg = _gelu_and_grad(uv_ref[...])
        u2 = uvg[:, :SGU_W]
        v2 = uvg[:, SGU_W:]
        rv = _rms(v2)
        vhat = v2 * rv
        gs = gs_ref[...]
        vnb = (vhat * gs).astype(MXU)
        tril = (lax.broadcasted_iota(jnp.int32, (CHUNK, CHUNK), 0)
                >= lax.broadcasted_iota(jnp.int32, (CHUNK, CHUNK), 1))
        for c in range(tm // CHUNK):
            rs = slice(c * CHUNK, (c + 1) * CHUNK)
            vc = vnb[rs]
            mixed = _sgu_mix(vc, ws_ref) + bias_ref[...]
            dsg_c = dsg[rs]
            du2_ref[rs, :] = dsg_c * mixed
            dmx = dsg_c * u2[rs]
            dbs_ref[...] += dmx
            dmb = dmx.astype(MXU)
            dvn_ref[rs, :] = _sgu_mix(dmb, wst_ref)
            for q in range(SGU_G // 2):
                lanes = slice(LANES * q, LANES * (q + 1))
                for j, part in enumerate(_group_halves(dmb[:, lanes])):
                    dws_ref[2 * q + j] += jnp.where(tril, _dot_nt(part, vc[:, lanes]), 0.0)
        dvn = dvn_ref[...]
        dgs_ref[...] += _rowsum(dvn * vhat)
        dv2 = _rms_bwd(dvn * gs, vhat, rv)
        duv_ref[:, :SGU_W] = (du2_ref[...] * duvg[:, :SGU_W]).astype(MXU)
        duv_ref[:, SGU_W:] = (dv2 * duvg[:, SGU_W:]).astype(MXU)

    row = lambda n: pl.BlockSpec((tm, n), lambda i: (i, 0))
    return pl.pallas_call(
        _behind(body, 15, after), name="mix_bwd", grid=(S // tm,),
        in_specs=[row(D_MODEL), row(2 * D_MODEL), row(D_MODEL), row(D_MODEL), row(SSM_W), row(2 * SGU_W),
                  _full(w_out.shape), _full(w_pa.shape), _full(w_pb.shape), _full(w_glu.shape), _full(b_glu.shape),
                  _full(g_sgu.shape), _full(ws.shape), _full(ws_t.shape), _full(bias_s.shape)] + [_ANY] * len(after),
        out_specs=[row(2 * D_MODEL), row(D_MODEL), row(D_MODEL), row(SSM_W), row(SSM_W), row(2 * SGU_W),
                   _full((1, SSM_W)), _full((1, SGU_W)), _full((SGU_G, CHUNK, CHUNK)), _full((CHUNK, SGU_W))],
        out_shape=[_sds((S, 2 * D_MODEL), MXU), _sds((S, D_MODEL), MXU), _sds((S, D_MODEL), MXU), _sds((S, SSM_W), MXU),
                   _sds((S, SSM_W)), _sds((S, 2 * SGU_W), MXU),
                   _sds((1, SSM_W)), _sds((1, SGU_W)), _sds((SGU_G, CHUNK, CHUNK)), _sds((CHUNK, SGU_W))],
        scratch_shapes=[pltpu.VMEM((tm, SGU_W), F32), pltpu.VMEM((tm, SGU_W), F32)],
        compiler_params=_cp("arbitrary"),
    )(*_in_hbm([dx1, gl, ya, yb, ys, uv, w_out, w_pa, w_pb, w_glu, b_glu, g_sgu, ws, ws_t, bias_s]), *after)


def _s5_bwd(dys, us, st_re, st_im, abar_re, abar_im, b_re, b_im, c_re, c_im, d_skip, tm, after=()):
    S = us.shape[0]
    nt = S // tm
    w = 8 * SSM_P
    hb = tm // 8
    run = tm // 8
    assert run & (run - 1) == 0

    def body(dys_ref, us_ref, str_ref, sti_ref, hr_ref, hi_ref, ar_ref, ai_ref, br_ref, bi_ref, cr_ref, ci_ref, d_ref,
             dus_ref, dab_ref, dd_ref, dbr_ref, dbi_ref, dcr_ref, dci_ref,
             tab_ref, car_ref, gr_ref, gi_ref, dyp_ref, up_ref, dun_ref):
        i = pl.program_id(1)
        ri = nt - 1 - i

        @pl.when(i == 0)
        def _():
            car_ref[...] = jnp.zeros_like(car_ref)
            for k, t in enumerate(_scan_tables(*_cpow2(ar_ref[...], -ai_ref[...], run.bit_length() - 1), True)):
                tab_ref[k] = t
            for r in (dab_ref, dd_ref, dbr_ref, dbi_ref, dcr_ref, dci_ref):
                r[...] = jnp.zeros_like(r)

        _runs_load(dys_ref, dyp_ref, run)
        _runs_load(us_ref, up_ref, run)
        dyb = dyp_ref[...].astype(MXU)
        gr_ref[...] = _dot(dyb, cr_ref[0])
        gi_ref[...] = -_dot(dyb, ci_ref[0])
        ar = jnp.broadcast_to(ar_ref[...], (8, w))
        ai = jnp.broadcast_to(-ai_ref[...], (8, w))

        def advance(kk, state):
            r0 = pl.multiple_of((run - 1 - kk) * 8, 8)
            gr, gi = state
            return (ar * gr - ai * gi + gr_ref[pl.ds(r0, 8), :], ar * gi + ai * gr + gi_ref[pl.ds(r0, 8), :])

        def emit(kk, state):
            r0 = pl.multiple_of((run - 1 - kk) * 8, 8)
            gr, gi = advance(kk, state)
            gr_ref[pl.ds(r0, 8), :] = gr
            gi_ref[pl.ds(r0, 8), :] = gi
            return gr, gi

        zero = jnp.zeros((8, w), F32)
        er, ei = lax.fori_loop(0, run, advance, (zero, zero))
        cr, ci = car_ref[0:1, :], car_ref[1:2, :]
        tr, ti = _scan_group(er, ei, tab_ref, cr, ci, True)
        r8 = lax.broadcasted_iota(jnp.int32, (8, w), 0)
        start = (jnp.where(r8 == 7, cr, pltpu.roll(tr, 7, 0)), jnp.where(r8 == 7, ci, pltpu.roll(ti, 7, 0)))
        car_ref[0:1, :] = tr[0:1, :]
        car_ref[1:2, :] = ti[0:1, :]
        lax.fori_loop(0, run, emit, start)

        gsr = gr_ref[...]
        gsi = gi_ref[...]
        sr = str_ref[...]
        si = sti_ref[...]
        first = ri == 0

        def previous(s, halo_ref):
            head = jnp.where(r8 == 0, jnp.where(first, 0.0, halo_ref[7:8, :]), pltpu.roll(s[tm - 8:tm, :], 1, 0))
            return jnp.concatenate([head, s[0:tm - 8, :]], axis=0)

        spr = previous(sr, hr_ref)
        spi = previous(si, hi_ref)
        dab_ref[0, 0:1, :] += _rowsum(gsr * spr + gsi * spi)
        dab_ref[0, 1:2, :] += _rowsum(gsi * spr - gsr * spi)

        gbr = gsr.astype(MXU)
        gbi = gsi.astype(MXU)
        _runs_store(_dot_nt(gbr, br_ref[0]) + _dot_nt(gbi, bi_ref[0]), dun_ref, run)
        dys_v = dys_ref[...]
        dus_ref[...] = (dun_ref[...] + d_ref[...] * dys_v).astype(MXU)
        dd_ref[0, 0:1, :] += _rowsum(dys_v * us_ref[...])
        ub = up_ref[...].astype(MXU)
        dbr_ref[0] += _dot_tn(ub, gbr)
        dbi_ref[0] += _dot_tn(ub, gbi)
        dcr_ref[0] += _dot_tn(dyb, sr.astype(MXU))
        dci_ref[0] -= _dot_tn(dyb, si.astype(MXU))

    blk = lambda: pl.BlockSpec((1, 8 * SSM_H, w), lambda j, i: (j, 0, 0))
    rowl = lambda: pl.BlockSpec((tm, LANES), lambda j, i: (nt - 1 - i, j))
    roww = lambda: pl.BlockSpec((tm, w), lambda j, i: (nt - 1 - i, j))
    halo = lambda: pl.BlockSpec((8, w), lambda j, i: (jnp.maximum((nt - 1 - i) * hb - 1, 0), j))
    return pl.pallas_call(
        _behind(body, 13, after), name="s5_bwd", grid=(SSM_BLK, nt),
        in_specs=[rowl(), rowl(), roww(), roww(), halo(), halo(),
                  pl.BlockSpec((1, w), lambda j, i: (0, j)), pl.BlockSpec((1, w), lambda j, i: (0, j)),
                  blk(), blk(), blk(), blk(),
                  pl.BlockSpec((1, LANES), lambda j, i: (0, j))] + [_ANY] * len(after),
        out_specs=[rowl(),
                   pl.BlockSpec((1, 8, w), lambda j, i: (j, 0, 0)), pl.BlockSpec((1, 8, LANES), lambda j, i: (j, 0, 0)),
                   blk(), blk(), blk(), blk()],
        out_shape=[_sds((S, SSM_W), MXU), _sds((SSM_BLK, 8, w)), _sds((SSM_BLK, 8, LANES)),
                   _sds((SSM_BLK, 8 * SSM_H, w)), _sds((SSM_BLK, 8 * SSM_H, w)),
                   _sds((SSM_BLK, 8 * SSM_H, w)), _sds((SSM_BLK, 8 * SSM_H, w))],
        scratch_shapes=[pltpu.VMEM((8, 8, w), F32), pltpu.VMEM((8, w), F32),
                        pltpu.VMEM((tm, w), F32), pltpu.VMEM((tm, w), F32),
                        pltpu.VMEM((tm, LANES), F32), pltpu.VMEM((tm, LANES), F32), pltpu.VMEM((tm, LANES), F32)],
        compiler_params=_cp("parallel", "arbitrary"),
    )(*_in_hbm([dys, us, st_re, st_im, st_re, st_im, abar_re, abar_im, b_re, b_im, c_re, c_im, d_skip]), *after)


def _in_bwd(dus, duv, dgl, dx1, x, g_mix, w_in, tm, after=()):
    S = x.shape[0]

    def body(dus_ref, duv_ref, dgl_ref, dx1_ref, x_ref, g_ref, w_ref, gx_ref, dg_ref):
        @pl.when(pl.program_id(0) == 0)
        def _():
            dg_ref[...] = jnp.zeros_like(dg_ref)

        dh = (_dot(dus_ref[...], w_ref[0:SSM_W, :])
              + _dot(duv_ref[...], w_ref[SSM_W:SSM_W + 2 * SGU_W, :])
              + _dot(dgl_ref[...], w_ref[SSM_W + 2 * SGU_W:, :]))
        xv = x_ref[...]
        r = _rms(xv)
        xn = xv * r
        dg_ref[...] += _rowsum(dh * xn)
        gx_ref[...] = dx1_ref[...] + _rms_bwd(dh * g_ref[...], xn, r)

    row = lambda n: pl.BlockSpec((tm, n), lambda i: (i, 0))
    return pl.pallas_call(
        _behind(body, 7, after), name="in_bwd", grid=(S // tm,),
        in_specs=[row(SSM_W), row(2 * SGU_W), row(2 * D_MODEL), row(D_MODEL), row(D_MODEL), _full((1, D_MODEL)),
                  _full(w_in.shape)] + [_ANY] * len(after),
        out_specs=[row(D_MODEL), _full((1, D_MODEL))],
        out_shape=[_sds((S, D_MODEL)), _sds((1, D_MODEL))],
        compiler_params=_cp("arbitrary"),
    )(*_in_hbm([dus, duv, dgl, dx1, x, g_mix, w_in]), *after)


def _wgrad_split(a, b, nsplit, tk, name):
    S, K = a.shape
    N = b.shape[1]
    c = N // nsplit

    def body(a_ref, b_ref, o_ref):
        prod = _dot_tn(a_ref[...], b_ref[...])
        for d in range(nsplit):
            o_ref[d] = prod[:, c * d:c * (d + 1)].astype(MXU)

    return pl.pallas_call(
        body, name=name, grid=(K // tk,),
        in_specs=[pl.BlockSpec((S, tk), lambda k: (0, k)), _full((S, N))],
        out_specs=pl.BlockSpec((nsplit, tk, c), lambda k: (0, k, 0)),
        out_shape=_sds((nsplit, K, c), MXU),
        compiler_params=_cp("parallel"),
    )(*_in_hbm([a, b]))


def _wgrad_in_t(dps, h1, name):
    S, K = h1.shape
    cw = 512
    counts = [b.shape[1] // cw for b in dps]
    starts = [sum(counts[:i]) for i in range(len(dps))]
    nblk = sum(counts)

    def body(*refs):
        b_refs = refs[:len(dps)]
        h_ref, o_ref = refs[len(dps)], refs[-1]
        j = pl.program_id(0)
        for b_ref, st, cnt in zip(b_refs, starts, counts):
            @pl.when(jnp.logical_and(j >= st, j < st + cnt))
            def _():
                o_ref[...] = _dot_tn(b_ref[...], h_ref[...]).astype(MXU)

    def src_spec(st, cnt):
        return pl.BlockSpec((S, cw), lambda j: (0, jnp.clip(j - st, 0, cnt - 1)))

    return pl.pallas_call(
        body, name=name, grid=(nblk,),
        in_specs=[src_spec(st, cnt) for st, cnt in zip(starts, counts)] + [_full((S, K))],
        out_specs=pl.BlockSpec((cw, K), lambda j: (j, 0)),
        out_shape=_sds((nblk * cw, K), MXU),
        compiler_params=_cp("arbitrary"),
    )(*_in_hbm([*dps, h1]))


def _wgrad_blk(a3, b3, nblk, a_of, b_of, name):
    S, K = a3.shape[1:]
    N = b3.shape[2]

    def body(a_ref, b_ref, o_ref):
        o_ref[0] = _dot_tn(a_ref[0], b_ref[0]).astype(MXU)

    return pl.pallas_call(
        body, name=name, grid=(nblk,),
        in_specs=[pl.BlockSpec((1, S, K), lambda b: (a_of(b), 0, 0)),
                  pl.BlockSpec((1, S, N), lambda b: (b_of(b), 0, 0))],
        out_specs=pl.BlockSpec((1, K, N), lambda b: (b, 0, 0)),
        out_shape=_sds((nblk, K, N), MXU),
        compiler_params=pltpu.CompilerParams(dimension_semantics=("parallel",), vmem_limit_bytes=WGRAD_VMEM_LIMIT),
    )(*_in_hbm([a3, b3]))


def _assemble_cols(blocks_list, name):
    def body(*refs):
        n = len(blocks_list)
        for b_ref, o_ref in zip(refs[:n], refs[n:]):
            c = b_ref.shape[2]
            for d in range(N_DEV):
                o_ref[:, c * d:c * (d + 1)] = b_ref[d]

    outs = [_sds((b.shape[1], N_DEV * b.shape[2]), b.dtype) for b in blocks_list]
    return pl.pallas_call(
        body, name=name, grid=(1,), in_specs=[_full(b.shape) for b in blocks_list],
        out_specs=[_full(o.shape) for o in outs], out_shape=outs, compiler_params=_cp("arbitrary"),
    )(*_in_hbm(blocks_list))


def _tile(S, want):
    return want if S % want == 0 else S


def _local_step(x, tgt, p, after, mixer_relay, mixer_weights, ffn_weights, grads_out, small_out):
    S = x.shape[0]
    tm = _tile(S, 256)
    tl = _tile(S, 512)

    rep = lambda a: jnp.repeat(a, SSM_H, axis=0)
    are = rep(p["a_re"])
    aim = rep(p["a_im"])
    ldt = jnp.broadcast_to(rep(p["log_dt"].reshape(SSM_G, 1)), are.shape)
    br_t = p["b_re_t"].reshape(are.shape)
    bi_t = p["b_im_t"].reshape(are.shape)
    abr, abi, bbr, bbi = _s5_params_fwd(are, aim, ldt, br_t, bi_t)
    head = lambda a: a.reshape(SSM_G, SSM_H, SSM_P)[:, 0, :].reshape(1, SSM_G * SSM_P)
    abar_re, abar_im = head(abr), head(abi)
    bd_br = _blockdiag(bbr).astype(MXU)
    bd_bi = _blockdiag(bbi).astype(MXU)
    bd_cr = _blockdiag(p["c_re"].reshape(are.shape)).astype(MXU)
    bd_ci = _blockdiag(p["c_im"].reshape(are.shape)).astype(MXU)
    d_skip = p["d_skip"].reshape(1, SSM_W)

    tril = jnp.tril(jnp.ones((CHUNK, CHUNK), dtype=bool))
    ws = jnp.where(tril[None], p["w_s"], 0.0)
    pair = lambda w: w.reshape(SGU_G // 2, 2, CHUNK, CHUNK).transpose(0, 2, 1, 3).reshape(SGU_G // 2, CHUNK, 2 * CHUNK)
    ws_b = pair(ws).astype(MXU)
    ws_t = pair(ws.transpose(0, 2, 1)).astype(MXU)
    bias_s = jnp.repeat(p["b_s"].T, SGU_D, axis=1)

    g_mix = p["g_mix"].reshape(1, D_MODEL)
    g_ffn = p["g_ffn"].reshape(1, D_MODEL)
    g_final = p["g_final"].reshape(1, D_MODEL)
    g_sgu = p["g_sgu"].reshape(1, SGU_W)
    b_glu = p["b_glu"].reshape(1, SSM_W)
    conv_b = p["conv_b"].reshape(2 * FF_NCB, 1, FF_CW)
    tf = _tile(S, 256)
    ts = _tile(S, 1024)

    h1, us, uv, gl = _in_fwd(x, g_mix, p["w_in_t"], tl, after)
    token = mixer_relay(us)
    st_re, st_im, ys = _s5_fwd(us, abar_re, abar_im, bd_br, bd_bi, bd_cr, bd_ci, d_skip, ts, (token,))
    p = dict(p, **mixer_weights(ys))
    yg, yap, sg, ya, yb, m, x1, h2 = _mix_fwd(x, ys, uv, gl, p["w_glu"], b_glu, p["w_proj_a"], g_sgu, ws_b, bias_s,
                                              p["w_proj_b"], p["w_out"], g_ffn, tl)
    w_up, conv_w, w_down = ffn_weights(h2)
    pair_lanes = lambda a: a.reshape(N_DEV // 2, 2, a.shape[1], FF_SHARD).transpose(0, 2, 1, 3).reshape(
        N_DEV // 2, a.shape[1], FF_CW)
    w_up = w_up.reshape(2 * FF_NCB, FF_CW, D_MODEL)
    conv_w = pair_lanes(conv_w)
    up, ab, ff, dx2, dx2b, loss, dg_final = _ffn_fwd(h2, x1, tgt, w_up, conv_w, conv_b, w_down, g_final, tf)

    dup, dx1, dx1b, dconv, dg_ffn = _ffn_bwd(dx2, up, ab, x1, w_up, conv_w, w_down, g_ffn, tf)
    rows8 = lambda g: g.reshape(N_DEV, g.shape[1] // N_DEV, g.shape[2])
    g_up = _wgrad_blk(dup.reshape(2 * FF_NCB, S, FF_CW), h2[None], 2 * FF_NCB, lambda b: b, lambda b: 0,
                      "wgrad_up").reshape(N_DEV, FF_SHARD, D_MODEL)
    g_down = _wgrad_blk(ff, dx2b[None], FF_NCB, lambda b: b, lambda b: 0, "wgrad_down").reshape(
        N_DEV, D_FF // N_DEV, D_MODEL)
    token = grads_out(("w_up", "w_down"), (g_up, g_down))
    dgl, dya, dyb, dz, dys, duv, db_glu, dg_sgu, dws, dbs = _mix_bwd(
        dx1, gl, ya, yb, ys, uv, p["w_out"], p["w_proj_a"], p["w_proj_b"], p["w_glu"], b_glu, g_sgu,
        ws_b, ws_t, bias_s, tm, (token,))
    token = grads_out(("w_glu", "w_proj_a", "w_proj_b", "w_out"),
                      (rows8(_wgrad_split(yg, dz, 1, SSM_W, "wgrad_glu")),
                       _wgrad_split(yap, dya, N_DEV, SSM_W, "wgrad_pa"),
                       _wgrad_split(sg, dyb, N_DEV, SGU_W, "wgrad_pb"),
                       rows8(_wgrad_split(m, dx1b, 1, 512, "wgrad_out"))))
    dus, dab, dd, dbbr, dbbi, dcr, dci = _s5_bwd(dys, us, st_re, st_im, abar_re, abar_im, bd_br, bd_bi, bd_cr, bd_ci,
                                                 d_skip, ts, (token,))
    g_in = _wgrad_in_t([dus, duv, dgl], h1, "wgrad_in")
    token = grads_out(("w_in",), (g_in.reshape(N_DEV, g_in.shape[0] // N_DEV, D_MODEL),))
    grad_x, dg_mix = _in_bwd(dus, duv, dgl, dx1, x, g_mix, p["w_in_t"], tl, (token,))

    spread = lambda v: jnp.repeat(v.reshape(SSM_G, SSM_P), SSM_H, axis=0) * (1.0 / SSM_H)
    dabr = spread(dab[:, 0, :])
    dabi = spread(dab[:, 1, :])
    dare, daim, dldt, dbr_t, dbi_t = _s5_params_bwd(are, aim, ldt, br_t, bi_t, dabr, dabi,
                                                    _unblockdiag(dbbr), _unblockdiag(dbbi))
    fold = lambda a: a.reshape(SSM_G, SSM_H, SSM_P).sum(axis=1)

    grads = {
        "g_mix": dg_mix,
        "a_re": fold(dare), "a_im": fold(daim), "log_dt": fold(dldt).sum(axis=1),
        "b_re": dbr_t, "b_im": dbi_t,
        "c_re": _unblockdiag(dcr).reshape(SSM_G, SSM_H, SSM_P),
        "c_im": _unblockdiag(dci).reshape(SSM_G, SSM_H, SSM_P),
        "d_skip": dd[:, 0, :].reshape(SSM_W),
        "b_glu": db_glu,
        "g_sgu": dg_sgu,
        "w_s": dws,
        "b_s": dbs.reshape(CHUNK, SGU_G, SGU_D).sum(axis=-1).T,
        "g_ffn": dg_ffn,
        "conv_w": dconv[:, 0:3, :].reshape(N_DEV // 2, 3, 2, FF_SHARD).transpose(0, 2, 1, 3).reshape(
            N_DEV, 3, FF_SHARD),
        "conv_b": dconv[:, 3, :].reshape(2 * D_FF),
        "g_final": dg_final,
    }
    return grad_x, small_out(grads, loss)


_ANY = pl.BlockSpec(memory_space=pl.ANY)
_MESH = pl.DeviceIdType.MESH


def _allgather(shards, dtypes, name, cast_only=(), sum_slots=False):
    n = len(shards)
    e = len(cast_only)
    shapes = [s.shape[1:] if sum_slots else s.shape for s in shards]

    def body(*refs):
        in_refs, extra_in = refs[:n], refs[n:n + e]
        out_refs, extra_out = refs[n + e:2 * n + e], refs[2 * n + e:2 * n + 2 * e]
        stage = refs[2 * n + 2 * e:3 * n + 2 * e]
        send_sems, recv_sems, local_sems = refs[3 * n + 2 * e:]
        for a in range(n):
            if sum_slots:
                total = in_refs[a][0].astype(F32)
                for s in range(1, N_DEV):
                    total = total + in_refs[a][s].astype(F32)
                stage[a][...] = total.astype(dtypes[a])
            else:
                stage[a][...] = in_refs[a][...].astype(dtypes[a])
        for i in range(e):
            extra_out[i][...] = extra_in[i][...].astype(MXU)
        x, y, c = lax.axis_index("x"), lax.axis_index("y"), lax.axis_index("c")
        me, sibling = (x, y, c), (x, y, 1 - c)
        chips = [(1 - x, y), (x, 1 - y), (1 - x, 1 - y)]

        def slot(a, px, py, pc):
            return out_refs[a].at[4 * px + 2 * py + pc]

        def copy(a, k, block, to, src=None):
            return pltpu.make_async_remote_copy(
                src_ref=slot(a, *block) if src is None else src, dst_ref=slot(a, *block),
                send_sem=send_sems.at[a, k], recv_sem=recv_sems.at[a, k], device_id=to, device_id_type=_MESH)

        mine = [pltpu.make_async_copy(stage[a], slot(a, *me), local_sems.at[a]) for a in range(n)]
        for cp in mine:
            cp.start()
        first = []
        for j, chip in enumerate(chips):
            first += [copy(a, 1 + j, me, (*chip, c), src=stage[a]) for a in range(n)]
        first += [copy(a, 0, me, sibling, src=stage[a]) for a in range(n)]
        for cp in first:
            cp.start()
        passed = []
        for j, chip in enumerate(chips):
            for a in range(n):
                copy(a, 1 + j, (*chip, c), me).wait_recv()
                fwd = copy(a, 4 + j, (*chip, c), sibling)
                fwd.start()
                passed.append(fwd)
        for a in range(n):
            copy(a, 0, sibling, me).wait_recv()
        for j, chip in enumerate(chips):
            for a in range(n):
                copy(a, 4 + j, (*chip, 1 - c), me).wait_recv()
        for cp in first + passed:
            cp.wait_send()
        for cp in mine:
            cp.wait()

    res = pl.pallas_call(
        body, name=name, grid=(1,), in_specs=[_full(s.shape) for s in list(shards) + list(cast_only)],
        out_specs=[_ANY] * n + [_full(s.shape) for s in cast_only],
        out_shape=[_sds((N_DEV,) + shp, dt) for shp, dt in zip(shapes, dtypes)]
                  + [_sds(s.shape, MXU) for s in cast_only],
        scratch_shapes=[pltpu.VMEM(shp, dt) for shp, dt in zip(shapes, dtypes)]
                       + [pltpu.SemaphoreType.DMA((n, 7)), pltpu.SemaphoreType.DMA((n, 7)), pltpu.SemaphoreType.DMA((n,))],
        compiler_params=pltpu.CompilerParams(vmem_limit_bytes=VMEM_LIMIT),
    )(*_in_hbm([*shards, *cast_only]))
    return res[:n], res[n:]


_HBM = pl.BlockSpec(memory_space=pltpu.HBM)
_SEM = pl.BlockSpec(memory_space=pltpu.SEMAPHORE)
_EFFECT = pltpu.SideEffectType.DATAFLOW_SIDE_EFFECTING
_PEER_ORDER = (2, 4, 6, 3, 5, 7, 1)


def _peer(k):
    x, y, c = lax.axis_index("x"), lax.axis_index("y"), lax.axis_index("c")
    px = 1 - x if k & 4 else x
    py = 1 - y if k & 2 else y
    pc = 1 - c if k & 1 else c
    return (px, py, pc), 4 * px + 2 * py + pc


_SAME_CORE_AND_SIBLING = (2, 4, 6, 1)


def _push_start(srcs, lands, slotted, name, peers=_PEER_ORDER):
    n = len(srcs)

    def body(*refs):
        src_refs, land_refs = refs[:n], refs[n:2 * n]
        send_sems, recv_sems, token, own_sems = refs[2 * n], refs[2 * n + 1], refs[-2], refs[-1]
        mine = 4 * lax.axis_index("x") + 2 * lax.axis_index("y") + lax.axis_index("c")
        own = [pltpu.make_async_copy(src_refs[a].at[mine] if slotted else src_refs[a], land_refs[a].at[mine],
                                     own_sems.at[a]) for a in range(n)]
        for cp in own:
            cp.start()
        for k in peers:
            dev, theirs = _peer(k)
            for a in range(n):
                pltpu.make_async_remote_copy(
                    src_ref=src_refs[a].at[theirs] if slotted else src_refs[a], dst_ref=land_refs[a].at[mine],
                    send_sem=send_sems.at[7 * a + k - 1], recv_sem=recv_sems.at[7 * a + k - 1],
                    device_id=dev, device_id_type=_MESH).start()
        token[...] = jnp.zeros_like(token)
        for cp in own:
            cp.wait()

    bufs = list(srcs) + list(lands)
    res = pl.pallas_call(
        body, name=name, in_specs=[_HBM] * (2 * n),
        out_specs=(_SEM, _SEM, *[_HBM] * (2 * n), pl.BlockSpec(memory_space=pltpu.VMEM)),
        out_shape=(pltpu.SemaphoreType.DMA((7 * n,)), pltpu.SemaphoreType.DMA((7 * n,)),
                   *[pltpu.HBM(b.shape, b.dtype) for b in bufs], _sds((8, LANES))),
        scratch_shapes=[pltpu.SemaphoreType.DMA((n,))],
        input_output_aliases={i: 2 + i for i in range(2 * n)},
        compiler_params=pltpu.CompilerParams(has_side_effects=_EFFECT),
    )(*[pltpu.with_memory_space_constraint(b, pltpu.HBM) for b in bufs])
    return res[0], res[1], res[2:2 + n], res[2 + n:2 + 2 * n], res[-1]


def _push_wait(send_sems, recv_sems, srcs, lands, slotted, after, name, peers=_PEER_ORDER):
    n = len(srcs)

    def body(*refs):
        src_refs, land_refs = refs[:n], refs[n:2 * n]
        send_sems, recv_sems = refs[2 * n], refs[2 * n + 1]
        for k in peers:
            dev, theirs = _peer(k)
            for a in range(n):
                cp = pltpu.make_async_remote_copy(
                    src_ref=src_refs[a].at[theirs] if slotted else src_refs[a], dst_ref=land_refs[a].at[theirs],
                    send_sem=send_sems.at[7 * a + k - 1], recv_sem=recv_sems.at[7 * a + k - 1],
                    device_id=dev, device_id_type=_MESH)
                cp.wait_send()
                cp.wait_recv()

    bufs = list(srcs) + list(lands)
    res = pl.pallas_call(
        body, name=name, in_specs=[_HBM] * (2 * n) + [_SEM, _SEM] + [_ANY] * len(after), out_specs=[_HBM] * (2 * n),
        out_shape=[pltpu.HBM(b.shape, b.dtype) for b in bufs],
        input_output_aliases={i: i for i in range(2 * n)},
        compiler_params=pltpu.CompilerParams(has_side_effects=_EFFECT),
    )(*bufs, send_sems, recv_sems, *after)
    return res[n:]


def _other_chips():
    x, y = lax.axis_index("x"), lax.axis_index("y")
    return ((1 - x, y), (x, 1 - y), (1 - x, 1 - y))


def _relay_start(lands, name):
    n = len(lands)

    def body(*refs):
        land_refs = refs[:n]
        send_sems, recv_sems, token = refs[n], refs[n + 1], refs[-1]
        x, y, c = lax.axis_index("x"), lax.axis_index("y"), lax.axis_index("c")
        for j, (px, py) in enumerate(_other_chips()):
            slot = 4 * px + 2 * py + c
            for a in range(n):
                pltpu.make_async_remote_copy(
                    src_ref=land_refs[a].at[slot], dst_ref=land_refs[a].at[slot],
                    send_sem=send_sems.at[3 * a + j], recv_sem=recv_sems.at[3 * a + j],
                    device_id=(x, y, 1 - c), device_id_type=_MESH).start()
        token[...] = jnp.zeros_like(token)

    res = pl.pallas_call(
        body, name=name, in_specs=[_HBM] * n,
        out_specs=(_SEM, _SEM, *[_HBM] * n, pl.BlockSpec(memory_space=pltpu.VMEM)),
        out_shape=(pltpu.SemaphoreType.DMA((3 * n,)), pltpu.SemaphoreType.DMA((3 * n,)),
                   *[pltpu.HBM(b.shape, b.dtype) for b in lands], _sds((8, LANES))),
        input_output_aliases={i: 2 + i for i in range(n)},
        compiler_params=pltpu.CompilerParams(has_side_effects=_EFFECT),
    )(*[pltpu.with_memory_space_constraint(b, pltpu.HBM) for b in lands])
    return res[0], res[1], res[2:2 + n], res[-1]


def _relay_wait(send_sems, recv_sems, lands, after, name):
    n = len(lands)

    def body(*refs):
        land_refs = refs[:n]
        send_sems, recv_sems = refs[n], refs[n + 1]
        x, y, c = lax.axis_index("x"), lax.axis_index("y"), lax.axis_index("c")
        for j, (px, py) in enumerate(_other_chips()):
            sent, received = 4 * px + 2 * py + c, 4 * px + 2 * py + (1 - c)
            for a in range(n):
                cp = pltpu.make_async_remote_copy(
                    src_ref=land_refs[a].at[sent], dst_ref=land_refs[a].at[received],
                    send_sem=send_sems.at[3 * a + j], recv_sem=recv_sems.at[3 * a + j],
                    device_id=(x, y, 1 - c), device_id_type=_MESH)
                cp.wait_send()
                cp.wait_recv()

    return pl.pallas_call(
        body, name=name, in_specs=[_HBM] * n + [_SEM, _SEM] + [_ANY] * len(after), out_specs=[_HBM] * n,
        out_shape=[pltpu.HBM(b.shape, b.dtype) for b in lands],
        input_output_aliases={i: i for i in range(n)},
        compiler_params=pltpu.CompilerParams(has_side_effects=_EFFECT),
    )(*lands, send_sems, recv_sems, *after)


def _adamw(w, g, m, v):
    m2 = ADAM_B1 * m + (1.0 - ADAM_B1) * g
    v2 = ADAM_B2 * v + (1.0 - ADAM_B2) * (g * g)
    m_hat = m2 / (1.0 - ADAM_B1 ** ADAM_STEP)
    v_hat = v2 / (1.0 - ADAM_B2 ** ADAM_STEP)
    delta = -ADAM_LR * (m_hat / (jnp.sqrt(v_hat) + ADAM_EPS) + ADAM_WD * w)
    return delta, m2, v2


def _adam_shard(parts, w, m, v, name):
    _, r, c = w.shape
    tr = max(t for t in range(16, 257, 16) if r % t == 0)

    nparts = parts.shape[0]

    def body(p_ref, w_ref, m_ref, v_ref, g_ref, d_ref, m2_ref, v2_ref):
        g = p_ref[0].astype(F32)
        for s in range(1, nparts):
            g = g + p_ref[s].astype(F32)
        g_ref[0] = g
        d_ref[0], m2_ref[0], v2_ref[0] = _adamw(w_ref[0], g, m_ref[0], v_ref[0])

    row = lambda: pl.BlockSpec((1, tr, c), lambda i: (0, i, 0))
    return pl.pallas_call(
        body, name=name, grid=(r // tr,),
        in_specs=[pl.BlockSpec((nparts, tr, c), lambda i: (0, i, 0)), row(), row(), row()],
        out_specs=[row(), row(), row(), row()], out_shape=[_sds((1, r, c))] * 4,
        compiler_params=_cp("parallel"),
    )(*_in_hbm([parts, w, m, v]))


def _adam_small(gs, ws, ms, vs, name):
    n = len(gs)

    def body(*refs):
        ins, outs = refs[:4 * n], refs[4 * n:]
        for i in range(n):
            g = ins[i][...]
            d, m2, v2 = _adamw(ins[n + i][...], g, ins[2 * n + i][...], ins[3 * n + i][...])
            outs[i][...] = d
            outs[n + i][...] = m2
            outs[2 * n + i][...] = v2

    res = pl.pallas_call(
        body, name=name, grid=(1,), in_specs=[_full(w.shape) for w in ws] * 4,
        out_specs=[_full(w.shape) for w in ws] * 3, out_shape=[_sds(w.shape) for w in ws] * 3,
        compiler_params=_cp("arbitrary"),
    )(*_in_hbm([*gs, *ws, *ms, *vs]))
    return res[:n], res[n:2 * n], res[2 * n:]


def _pad_to(a, n, axis):
    extra = n - a.shape[axis]
    if extra == 0:
        return a
    widths = [(0, 0)] * a.ndim
    widths[axis] = (0, extra)
    return jnp.pad(a, widths)


def _ceil_to(n, k):
    return -(-n // k) * k


def _pack_rows(flats, rows_multiple):
    parts = [_pad_to(f, _ceil_to(f.shape[-1], LANES), f.ndim - 1) for f in flats]
    cat = jnp.concatenate(parts, axis=-1)
    total = _ceil_to(cat.shape[-1], LANES * rows_multiple)
    cat = _pad_to(cat, total, cat.ndim - 1)
    return cat.reshape(cat.shape[:-1] + (total // LANES, LANES))


def _unpack_rows(buf, sizes):
    flat = buf.reshape(buf.shape[:-2] + (-1,))
    out, off = [], 0
    for n in sizes:
        out.append(flat[..., off:off + n])
        off += _ceil_to(n, LANES)
    return out


_MIX_BIG = ("w_in", "w_glu", "w_proj_a", "w_proj_b", "w_out")
_BIG = _MIX_BIG + ("w_up", "w_down")
_SMALL = ("g_mix", "a_re", "a_im", "log_dt", "b_re", "b_im", "c_re", "c_im", "d_skip", "b_glu", "g_sgu", "w_s", "b_s",
          "g_ffn", "conv_b", "g_final")
_SMALL_ROWS_MULTIPLE = 8 * N_DEV
_TRANSPOSED = ("w_in", "w_up", "b_re", "b_im")


def _as_2d(a):
    return a.reshape(-1, a.shape[-1]) if a.ndim > 1 else a.reshape(1, -1)


def kernel(x, g_mix, w_in, a_re, a_im, log_dt, b_re, b_im, c_re, c_im, d_skip, w_glu, b_glu, w_proj_a, g_sgu, w_s, b_s, w_proj_b, w_out, g_ffn, w_up, conv_w, conv_b, w_down, g_final, loss_target, m_g_mix, m_w_in, m_a_re, m_a_im, m_log_dt, m_b_re, m_b_im, m_c_re, m_c_im, m_d_skip, m_w_glu, m_b_glu, m_w_proj_a, m_g_sgu, m_w_s, m_b_s, m_w_proj_b, m_w_out, m_g_ffn, m_w_up, m_conv_w, m_conv_b, m_w_down, m_g_final, v_g_mix, v_w_in, v_a_re, v_a_im, v_log_dt, v_b_re, v_b_im, v_c_re, v_c_im, v_d_skip, v_w_glu, v_b_glu, v_w_proj_a, v_g_sgu, v_w_s, v_b_s, v_w_proj_b, v_w_out, v_g_ffn, v_w_up, v_conv_w, v_conv_b, v_w_down, v_g_final):
    args = dict(locals())
    me = 4 * lax.axis_index("x") + 2 * lax.axis_index("y") + lax.axis_index("c")

    for n in _TRANSPOSED:
        for pre in ("", "m_", "v_"):
            args[pre + n] = jnp.swapaxes(args[pre + n], -1, -2)
    later = ("w_glu", "w_proj_a", "w_proj_b", "w_out", "w_up", "w_down")
    (w_in_g,), casts = _allgather([args["w_in"][0]], [MXU], "allgather_w_in", cast_only=[args[n][0] for n in later])
    sh = dict(zip(later, casts))

    def start_push(srcs, tag, peers):
        lands = [lax.empty((N_DEV,) + s.shape, s.dtype) for s in srcs]
        send_sems, recv_sems, srcs, lands, token = _push_start(srcs, lands, False, "push_" + tag, peers)
        return (send_sems, recv_sems, srcs, lands), token

    mix_push, token_a = start_push([sh[n] for n in later[:4]], "mixer_weights", _SAME_CORE_AND_SIBLING)
    ffn_push, token_b = start_push([sh["w_up"], sh["w_down"], conv_w[0]], "ffn_weights", _PEER_ORDER)
    p = {n: (args[n][0] if n != "g_final" else args[n]) for n in _SMALL if n not in _TRANSPOSED}
    p.update(w_in_t=w_in_g.reshape(SSM_W + 2 * SGU_W + 2 * D_MODEL, D_MODEL),
             b_re_t=args["b_re"][0], b_im_t=args["b_im"][0])
    relay = {}

    def mixer_relay(after):
        lands = _push_wait(*mix_push, False, [after], "wait_mixer_weights", _SAME_CORE_AND_SIBLING)
        relay["send"], relay["recv"], relay["lands"], token = _relay_start(lands, "relay_mixer_weights")
        return token

    def mixer_weights(after):
        w_glu_g, w_pa_g, w_pb_g, w_out_g = _relay_wait(relay["send"], relay["recv"], relay["lands"], [after],
                                                       "wait_relay_mixer_weights")
        w_pa_full, w_pb_full = _assemble_cols([w_pa_g, w_pb_g], "assemble_cols")
        return dict(w_glu=w_glu_g.reshape(SSM_W, SSM_W), w_proj_a=w_pa_full, w_proj_b=w_pb_full,
                    w_out=w_out_g.reshape(D_MODEL, D_MODEL))

    def ffn_weights(after):
        w_up_g, w_down_g, conv_w_g = _push_wait(*ffn_push, False, [after], "wait_ffn_weights")
        return w_up_g, conv_w_g, w_down_g.reshape(D_FF, D_MODEL)

    pushes = []

    def grads_out(names, sends):
        lands = [lax.empty(s.shape, s.dtype) for s in sends]
        send_sems, recv_sems, srcs, lands, token = _push_start(list(sends), lands, True, "push_grads_" + names[0])
        pushes.append((names, send_sems, recv_sems, srcs, lands))
        return token


    small_names = _SMALL + ("conv_w", "loss")
    small = {}

    def small_out(grads, loss_part):
        small_g = dict(grads, loss=loss_part[0, 0:1])
        flats = [small_g[n].reshape(-1) for n in small_names]
        small["sizes"] = [f.shape[0] for f in flats]
        g_small = _pack_rows(flats, _SMALL_ROWS_MULTIPLE)
        small["rs8"] = g_small.shape[0] // N_DEV
        return grads_out(("small",), (g_small.reshape(N_DEV, small["rs8"], LANES),))

    grad_x, small_token = _local_step(x[0], loss_target[0], p, (token_a, token_b), mixer_relay, mixer_weights,
                                      ffn_weights, grads_out, small_out)

    out = {}
    done = [grad_x, small_token]
    for names, send_sems, recv_sems, srcs, lands in pushes:
        parts = _push_wait(send_sems, recv_sems, srcs, lands, True, done, "wait_grads_" + names[0])
        if names == ("small",):
            g_small_all = _allgather([parts[0]], [F32], "allgather_small", sum_slots=True)[0][0].reshape(
                N_DEV * small["rs8"], LANES)
            pieces = dict(zip(small_names, _unpack_rows(g_small_all, small["sizes"])))
            loss = pieces["loss"][0]
            dconv_w = lax.dynamic_index_in_dim(pieces["conv_w"].reshape(N_DEV, 3, FF_SHARD), me, axis=0, keepdims=False)
            names2 = _SMALL + ("conv_w",)
            gs = [pieces[n].reshape(_as_2d(args[n]).shape) for n in _SMALL] + [dconv_w]
            ds, m2s, v2s = _adam_small(gs, [_as_2d(args[n]) for n in names2], [_as_2d(args["m_" + n]) for n in names2],
                                       [_as_2d(args["v_" + n]) for n in names2], "adam_small")
            for n, res in zip(names2, zip(gs, ds, m2s, v2s)):
                for kind, v in zip(("grad_", "delta_", "new_m_", "new_v_"), res):
                    out[kind + n] = v.reshape(args[n].shape)
            done = [ds[0]]
            continue
        done = []
        for n, part in zip(names, parts):
            res = _adam_shard(part, args[n], args["m_" + n], args["v_" + n], "adam_" + n)
            for kind, v in zip(("grad_", "delta_", "new_m_", "new_v_"), res):
                out[kind + n] = v
            done.append(res[0])
    order = ("g_mix", "w_in", "a_re", "a_im", "log_dt", "b_re", "b_im", "c_re", "c_im", "d_skip", "w_glu", "b_glu",
             "w_proj_a", "g_sgu", "w_s", "b_s", "w_proj_b", "w_out", "g_ffn", "w_up", "conv_w", "conv_b", "w_down",
             "g_final")
    res = [loss, grad_x.reshape(x.shape)]
    for kind in ("grad_", "delta_", "new_m_", "new_v_"):
        res += [jnp.swapaxes(out[kind + n], -1, -2) if n in _TRANSPOSED else out[kind + n] for n in order]
    return tuple(res)
```

```python
import math

import jax
import jax.numpy as jnp
from jax import lax
from jax.experimental import pallas as pl
from jax.experimental.pallas import tpu as pltpu

F32 = jnp.float32
MXU = jnp.bfloat16
EPS = 1e-6

D_MODEL = 1024
SSM_W = 512
SSM_G, SSM_H, SSM_P = 32, 16, 64
SSM_BLK = 4
SGU_W = 512
SGU_G, SGU_D, CHUNK = 8, 64, 128
D_FF = 2816
N_DEV = 8
FF_SHARD = 2 * D_FF // N_DEV
FF_CW = 2 * FF_SHARD
FF_NCB = D_FF // FF_CW
LANES = 128

ADAM_LR, ADAM_B1, ADAM_B2, ADAM_EPS, ADAM_WD, ADAM_STEP = 0.001, 0.9, 0.999, 1e-08, 0.01, 10

VMEM_LIMIT = 48 * 1024 * 1024
WGRAD_VMEM_LIMIT = 58 * 1024 * 1024
FFN_VMEM_LIMIT = 58 * 1024 * 1024


def _cp(*sem):
    return pltpu.CompilerParams(dimension_semantics=sem, vmem_limit_bytes=VMEM_LIMIT)


def _full(shape):
    n = len(shape)
    return pl.BlockSpec(shape, lambda *_: (0,) * n)


def _once(shape):
    n = len(shape)
    return pl.BlockSpec(shape, lambda *_: (0,) * n, pipeline_mode=pl.Buffered(1))


def _sds(shape, dtype=F32):
    return jax.ShapeDtypeStruct(shape, dtype)


def _in_hbm(arrays):
    return [pltpu.with_memory_space_constraint(a, pltpu.HBM) for a in arrays]


def _behind(body, n_in, after):
    def ordered(*refs):
        body(*refs[:n_in], *refs[n_in + len(after):])
    return ordered


def _dot(a, b):
    return jnp.dot(a, b, preferred_element_type=F32)


def _dot_nt(a, b):
    return lax.dot_general(a, b, (((1,), (1,)), ((), ())), preferred_element_type=F32)


def _dot_tn(a, b):
    return lax.dot_general(a, b, (((0,), (0,)), ((), ())), preferred_element_type=F32)


_GELU_C = math.sqrt(2.0 / math.pi)


def _gelu(x):
    return 0.5 * x * (1.0 + jnp.tanh(_GELU_C * (x + 0.044715 * (x * x * x))))


def _gelu_and_grad(x):
    t = jnp.tanh(_GELU_C * (x + 0.044715 * (x * x * x)))
    g = 0.5 * x * (1.0 + t)
    dg = 0.5 * (1.0 + t) + 0.5 * x * (1.0 - t * t) * (_GELU_C * (1.0 + 3.0 * 0.044715 * (x * x)))
    return g, dg


def _sigmoid(x):
    return 0.5 * jnp.tanh(0.5 * x) + 0.5


def _rms(x):
    return lax.rsqrt(jnp.mean(x * x, axis=-1, keepdims=True) + EPS)


def _rms_bwd(dxn, xn, r):
    return r * (dxn - xn * jnp.mean(dxn * xn, axis=-1, keepdims=True))


def _rowsum(x):
    return jnp.sum(x, axis=0, keepdims=True)


def _fetch_once(pairs, sems):
    copies = [pltpu.make_async_copy(src, dst, sems.at[k]) for k, (src, dst) in enumerate(pairs)]
    for cp in copies:
        cp.start()
    for cp in copies:
        cp.wait()


def _s5_disc(are, aim, ldt, br, bi):
    dt = jnp.exp(ldt)
    mag = jnp.exp(dt * are)
    abr = mag * jnp.cos(dt * aim)
    abi = mag * jnp.sin(dt * aim)
    den = are * are + aim * aim
    nr = abr - 1.0
    ni = abi
    fr = (nr * are + ni * aim) / den
    fi = (ni * are - nr * aim) / den
    return abr, abi, fr * br - fi * bi, fr * bi + fi * br


def _s5_params_fwd(are, aim, ldt, br, bi):
    def body(are_ref, aim_ref, ldt_ref, br_ref, bi_ref, o0, o1, o2, o3):
        outs = _s5_disc(are_ref[...], aim_ref[...], ldt_ref[...], br_ref[...], bi_ref[...])
        for o, v in zip((o0, o1, o2, o3), outs):
            o[...] = v
    shp = are.shape
    return pl.pallas_call(body, name="s5_params_fwd", grid=(1,), in_specs=[_full(shp)] * 5, out_specs=[_full(shp)] * 4,
                          out_shape=[_sds(shp)] * 4)(*_in_hbm([are, aim, ldt, br, bi]))


def _s5_params_bwd(are, aim, ldt, br, bi, dabr, dabi, dbr, dbi):
    def body(are_ref, aim_ref, ldt_ref, br_ref, bi_ref, c0, c1, c2, c3, o0, o1, o2, o3, o4):
        prim = (are_ref[...], aim_ref[...], ldt_ref[...], br_ref[...], bi_ref[...])
        _, vjp = jax.vjp(_s5_disc, *prim)
        outs = vjp((c0[...], c1[...], c2[...], c3[...]))
        for o, v in zip((o0, o1, o2, o3, o4), outs):
            o[...] = v
    shp = are.shape
    return pl.pallas_call(body, name="s5_params_bwd", grid=(1,), in_specs=[_full(shp)] * 9, out_specs=[_full(shp)] * 5,
                          out_shape=[_sds(shp)] * 5)(*_in_hbm([are, aim, ldt, br, bi, dabr, dabi, dbr, dbi]))


def _blockdiag(m_t):
    m = m_t.reshape(SSM_BLK, 8, SSM_H, 1, SSM_P)
    eye = jnp.eye(8, dtype=bool).reshape(1, 8, 1, 8, 1)
    return jnp.where(eye, m, jnp.zeros((), m_t.dtype)).reshape(SSM_BLK, 8 * SSM_H, 8 * SSM_P)


def _unblockdiag(pc):
    m = pc.reshape(SSM_BLK, 8, SSM_H, 8, SSM_P)
    return jnp.einsum("jghgp->jghp", m).reshape(SSM_G * SSM_H, SSM_P)


def _in_fwd(x, g_mix, w_in_t, tm, after=()):
    S = x.shape[0]

    def body(x_ref, g_ref, w_ref, h_ref, us_ref, uv_ref, gl_ref):
        xv = x_ref[...]
        h = (xv * _rms(xv) * g_ref[...]).astype(MXU)
        h_ref[...] = h
        us_ref[...] = _dot_nt(h, w_ref[0:SSM_W, :])
        uv_ref[...] = _dot_nt(h, w_ref[SSM_W:SSM_W + 2 * SGU_W, :])
        gl_ref[...] = _dot_nt(h, w_ref[SSM_W + 2 * SGU_W:, :])

    row = lambda n: pl.BlockSpec((tm, n), lambda i: (i, 0))
    return pl.pallas_call(
        _behind(body, 3, after), name="in_fwd", grid=(S // tm,),
        in_specs=[row(D_MODEL), _full((1, D_MODEL)), _full(w_in_t.shape)] + [_ANY] * len(after),
        out_specs=[row(D_MODEL), row(SSM_W), row(2 * SGU_W), row(2 * D_MODEL)],
        out_shape=[_sds((S, D_MODEL), MXU), _sds((S, SSM_W)), _sds((S, 2 * SGU_W)), _sds((S, 2 * D_MODEL))],
        compiler_params=_cp("parallel"),
    )(*_in_hbm([x, g_mix, w_in_t]), *after)


def _scan_tables(ar, ai, reverse):
    n = ar.shape[-1]
    def mul(p, q):
        return p[0] * q[0] - p[1] * q[1], p[0] * q[1] + p[1] * q[0]
    a1 = (ar, ai)
    a2 = mul(a1, a1)
    a3 = mul(a2, a1)
    a4 = mul(a2, a2)
    a5 = mul(a4, a1)
    a6 = mul(a4, a2)
    a7 = mul(a4, a3)
    a8 = mul(a4, a4)
    pw = (a1, a2, a3, a4, a5, a6, a7, a8)
    rows = lax.broadcasted_iota(jnp.int32, (8, n), 0)
    tabs = []
    for s, a in ((1, a1), (2, a2), (4, a4)):
        keep = (rows + s <= 7) if reverse else (rows >= s)
        for comp in a:
            tabs.append(jnp.where(keep, jnp.broadcast_to(comp, (8, n)), 0.0))
    for c in range(2):
        q = jnp.zeros((8, n), F32)
        for r in range(8):
            e = (8 - r) if reverse else (r + 1)
            q = jnp.where(rows == r, jnp.broadcast_to(pw[e - 1][c], (8, n)), q)
        tabs.append(q)
    return tabs


def _scan_group(xr, xi, tab_ref, cr, ci, reverse):
    for t, s in enumerate((1, 2, 4)):
        pr = tab_ref[2 * t]
        pi = tab_ref[2 * t + 1]
        sh = (8 - s) if reverse else s
        sr = pltpu.roll(xr, sh, 0)
        si = pltpu.roll(xi, sh, 0)
        xr, xi = xr + pr * sr - pi * si, xi + pr * si + pi * sr
    qr = tab_ref[6]
    qi = tab_ref[7]
    return xr + qr * cr - qi * ci, xi + qr * ci + qi * cr


def _runs_load(src_ref, dst_ref, run):
    for i in range(run):
        dst_ref[8 * i:8 * i + 8, :] = src_ref[pl.ds(i, 8, stride=run), :]


def _runs_store(val, dst_ref, run):
    for i in range(run):
        dst_ref[pl.ds(i, 8, stride=run), :] = val[8 * i:8 * i + 8, :]


def _cpow2(ar, ai, log2n):
    for _ in range(log2n):
        ar, ai = ar * ar - ai * ai, 2.0 * ar * ai
    return ar, ai


def _s5_fwd(us, abar_re, abar_im, b_re, b_im, c_re, c_im, d_skip, tm, after=()):
    S = us.shape[0]
    nt = S // tm
    w = 8 * SSM_P
    run = tm // 8
    assert run & (run - 1) == 0

    def body(us_ref, ar_ref, ai_ref, br_ref, bi_ref, cr_ref, ci_ref, d_ref, str_ref, sti_ref, ys_ref,
             tab_ref, car_ref, up_ref):
        i = pl.program_id(1)

        @pl.when(i == 0)
        def _():
            car_ref[...] = jnp.zeros_like(car_ref)
            for k, t in enumerate(_scan_tables(*_cpow2(ar_ref[...], ai_ref[...], run.bit_length() - 1), False)):
                tab_ref[k] = t

        _runs_load(us_ref, up_ref, run)
        ub = up_ref[...].astype(MXU)
        str_ref[...] = _dot(ub, br_ref[0])
        sti_ref[...] = _dot(ub, bi_ref[0])
        ar = jnp.broadcast_to(ar_ref[...], (8, w))
        ai = jnp.broadcast_to(ai_ref[...], (8, w))

        def advance(k, state):
            r0 = pl.multiple_of(k * 8, 8)
            sr, si = state
            return (ar * sr - ai * si + str_ref[pl.ds(r0, 8), :], ar * si + ai * sr + sti_ref[pl.ds(r0, 8), :])

        def emit(k, state):
            r0 = pl.multiple_of(k * 8, 8)
            sr, si = advance(k, state)
            str_ref[pl.ds(r0, 8), :] = sr
            sti_ref[pl.ds(r0, 8), :] = si
            return sr, si

        zero = jnp.zeros((8, w), F32)
        er, ei = lax.fori_loop(0, run, advance, (zero, zero))
        cr, ci = car_ref[0:1, :], car_ref[1:2, :]
        tr, ti = _scan_group(er, ei, tab_ref, cr, ci, False)
        r8 = lax.broadcasted_iota(jnp.int32, (8, w), 0)
        start = (jnp.where(r8 == 0, cr, pltpu.roll(tr, 1, 0)), jnp.where(r8 == 0, ci, pltpu.roll(ti, 1, 0)))
        car_ref[0:1, :] = tr[7:8, :]
        car_ref[1:2, :] = ti[7:8, :]
        lax.fori_loop(0, run, emit, start)
        y = _dot_nt(str_ref[...].astype(MXU), cr_ref[0]) - _dot_nt(sti_ref[...].astype(MXU), ci_ref[0])
        _runs_store(y, ys_ref, run)
        ys_ref[...] += d_ref[...] * us_ref[...]

    blk = lambda: pl.BlockSpec((1, 8 * SSM_H, w), lambda j, i: (j, 0, 0))
    return pl.pallas_call(
        _behind(body, 8, after), name="s5_fwd", grid=(SSM_BLK, nt),
        in_specs=[pl.BlockSpec((tm, LANES), lambda j, i: (i, j)),
                  pl.BlockSpec((1, w), lambda j, i: (0, j)), pl.BlockSpec((1, w), lambda j, i: (0, j)),
                  blk(), blk(), blk(), blk(),
                  pl.BlockSpec((1, LANES), lambda j, i: (0, j))] + [_ANY] * len(after),
        out_specs=[pl.BlockSpec((tm, w), lambda j, i: (i, j)), pl.BlockSpec((tm, w), lambda j, i: (i, j)),
                   pl.BlockSpec((tm, LANES), lambda j, i: (i, j))],
        out_shape=[_sds((S, SSM_BLK * w)), _sds((S, SSM_BLK * w)), _sds((S, SSM_W))],
        scratch_shapes=[pltpu.VMEM((8, 8, w), F32), pltpu.VMEM((8, w), F32), pltpu.VMEM((tm, LANES), F32)],
        compiler_params=_cp("parallel", "arbitrary"),
    )(*_in_hbm([us, abar_re, abar_im, b_re, b_im, c_re, c_im, d_skip]), *after)


def _group_halves(vp):
    first = lax.broadcasted_iota(jnp.int32, vp.shape, 1) < SGU_D
    zero = jnp.zeros((), vp.dtype)
    return jnp.where(first, vp, zero), jnp.where(first, zero, vp)


def _sgu_mix(vnb, wcat_ref):
    outs = []
    for q in range(SGU_G // 2):
        lo, hi = _group_halves(vnb[:, LANES * q:LANES * (q + 1)])
        outs.append(_dot(wcat_ref[q], jnp.concatenate([lo, hi], axis=0)))
    return jnp.concatenate(outs, axis=1)


def _mix_fwd(x, ys, uv, gl, w_glu, b_glu, w_pa, g_sgu, ws, bias_s, w_pb, w_out, g_ffn, tm):
    S = x.shape[0]

    def body(x_ref, ys_ref, uv_ref, gl_ref, wglu_ref, bglu_ref, wpa_ref, gs_ref, ws_ref, bias_ref, wpb_ref, wout_ref,
             gf_ref, yg_ref, yap_ref, sg_ref, ya_ref, yb_ref, m_ref, x1_ref, h2_ref):
        yg = _gelu(ys_ref[...])
        ygb = yg.astype(MXU)
        yg_ref[...] = ygb
        z = _dot(ygb, wglu_ref[...]) + bglu_ref[...]
        yapb = (yg * _sigmoid(z)).astype(MXU)
        yap_ref[...] = yapb
        ya = _dot(yapb, wpa_ref[...])
        ya_ref[...] = ya

        uvg = _gelu(uv_ref[...])
        u2 = uvg[:, :SGU_W]
        v2 = uvg[:, SGU_W:]
        vnb = (v2 * _rms(v2) * gs_ref[...]).astype(MXU)
        for c in range(tm // CHUNK):
            rs = slice(c * CHUNK, (c + 1) * CHUNK)
            mixed = _sgu_mix(vnb[rs], ws_ref) + bias_ref[...]
            sg_ref[rs, :] = (u2[rs] * mixed).astype(MXU)
        yb = _dot(sg_ref[...], wpb_ref[...])
        yb_ref[...] = yb

        glv = gl_ref[...]
        m = _sigmoid(glv[:, :D_MODEL]) * ya + _sigmoid(glv[:, D_MODEL:]) * yb
        mb = m.astype(MXU)
        m_ref[...] = mb
        x1 = x_ref[...] + _dot(mb, wout_ref[...])
        x1_ref[...] = x1
        h2_ref[...] = (x1 * _rms(x1) * gf_ref[...]).astype(MXU)

    row = lambda n: pl.BlockSpec((tm, n), lambda i: (i, 0))
    return pl.pallas_call(
        body, name="mix_fwd", grid=(S // tm,),
        in_specs=[row(D_MODEL), row(SSM_W), row(2 * SGU_W), row(2 * D_MODEL),
                  _full(w_glu.shape), _full(b_glu.shape), _full(w_pa.shape), _full(g_sgu.shape), _full(ws.shape),
                  _full(bias_s.shape), _full(w_pb.shape), _full(w_out.shape), _full(g_ffn.shape)],
        out_specs=[row(SSM_W), row(SSM_W), row(SGU_W), row(D_MODEL), row(D_MODEL), row(D_MODEL), row(D_MODEL),
                   row(D_MODEL)],
        out_shape=[_sds((S, SSM_W), MXU), _sds((S, SSM_W), MXU), _sds((S, SGU_W), MXU), _sds((S, D_MODEL)),
                   _sds((S, D_MODEL)), _sds((S, D_MODEL), MXU), _sds((S, D_MODEL)), _sds((S, D_MODEL), MXU)],
        compiler_params=_cp("parallel"),
    )(*_in_hbm([x, ys, uv, gl, w_glu, b_glu, w_pa, g_sgu, ws, bias_s, w_pb, w_out, g_ffn]))


def _causal_conv3(u, prev8, cw, cb):
    tm = u.shape[0]
    w0, w1, w2 = cw[0:1], cw[1:2], cw[2:3]
    body = w0 * pltpu.roll(u, 2, 0) + w1 * pltpu.roll(u, 1, 0) + w2 * u + cb
    u8 = u[0:8, :]
    r8 = lax.broadcasted_iota(jnp.int32, u8.shape, 0)
    t1 = prev8[7:8, :]
    t0 = prev8[6:7, :]
    s1 = jnp.where(r8 == 0, t1, pltpu.roll(u8, 1, 0))
    s2 = jnp.where(r8 == 0, t0, jnp.where(r8 == 1, t1, pltpu.roll(u8, 2, 0)))
    first = w0 * s2 + w1 * s1 + w2 * u8 + cb
    return jnp.concatenate([first, body[8:tm, :]], axis=0)


def _causal_conv3_adjoint(d, next8, cw):
    tm = d.shape[0]
    w0, w1, w2 = cw[0:1], cw[1:2], cw[2:3]
    n1 = pltpu.roll(d, tm - 1, 0)
    n2 = pltpu.roll(d, tm - 2, 0)
    body = w2 * d + w1 * n1 + w0 * n2
    d8 = d[tm - 8:tm, :]
    r8 = lax.broadcasted_iota(jnp.int32, d8.shape, 0)
    h0 = next8[0:1, :]
    h1 = next8[1:2, :]
    m1 = jnp.where(r8 == 7, h0, pltpu.roll(d8, 7, 0))
    m2 = jnp.where(r8 == 6, h0, jnp.where(r8 == 7, h1, pltpu.roll(d8, 6, 0)))
    last = w2 * d8 + w1 * m1 + w0 * m2
    out = jnp.concatenate([body[0:tm - 8, :], last], axis=0)
    return out, n1, n2, h0 - d[0:1, :], h1 - d[1:2, :]


def _ffn_fwd(h2, x1, tgt, w_up, conv_w, conv_b, w_down, g_final, tm):
    S = h2.shape[0]
    nt = S // tm
    ncb = FF_NCB

    def body(h2_ref, wup_hbm, cwa_ref, cwb_ref, cba_ref, cbb_ref, wd_hbm, x1_ref, gf_ref, tgt_ref,
             up_ref, ab_ref, ff_ref, dx2_ref, dx2b_ref, loss_ref, dgf_ref, acc_ref, tail_ref, wup_ref, wdn_ref, wsem):
        i = pl.program_id(0)
        cb = pl.program_id(1)

        @pl.when(i == 0)
        def _():
            tail_ref[cb] = jnp.zeros((2, 8, FF_CW), F32)

        @pl.when(jnp.logical_and(i == 0, cb == 0))
        def _():
            loss_ref[...] = jnp.zeros_like(loss_ref)
            dgf_ref[...] = jnp.zeros_like(dgf_ref)
            _fetch_once([(wup_hbm, wup_ref), (wd_hbm, wdn_ref)], wsem)

        h2v = h2_ref[...]
        ua = _dot_nt(h2v, wup_ref[cb])
        ub = _dot_nt(h2v, wup_ref[ncb + cb])
        up_ref[0, 0] = ua.astype(MXU)
        up_ref[1, 0] = ub.astype(MXU)
        a = _causal_conv3(ua, tail_ref[cb, 0], cwa_ref[0], cba_ref[0])
        b = _causal_conv3(ub, tail_ref[cb, 1], cwb_ref[0], cbb_ref[0])
        tail_ref[cb, 0] = ua[tm - 8:tm, :]
        tail_ref[cb, 1] = ub[tm - 8:tm, :]
        ab_ref[0, 0] = a
        ab_ref[1, 0] = b
        ffb = (a * _sigmoid(a) * b).astype(MXU)
        ff_ref[0] = ffb
        contrib = _dot(ffb, wdn_ref[pl.ds(pl.multiple_of(cb * FF_CW, FF_CW), FF_CW), :])

        @pl.when(cb == 0)
        def _():
            acc_ref[...] = contrib

        @pl.when(cb > 0)
        def _():
            acc_ref[...] += contrib

        @pl.when(cb == ncb - 1)
        def _():
            x2 = x1_ref[...] + acc_ref[...]
            r = _rms(x2)
            xn = x2 * r
            g = gf_ref[...]
            diff = xn * g - tgt_ref[...]
            loss_ref[...] += (0.5 / D_MODEL) * jnp.sum(diff * diff)
            dy = diff * (1.0 / D_MODEL)
            dgf_ref[...] += _rowsum(dy * xn)
            dx2 = _rms_bwd(dy * g, xn, r)
            dx2_ref[...] = dx2
            dx2b_ref[...] = dx2.astype(MXU)

    row = lambda n: pl.BlockSpec((tm, n), lambda i, c: (i, 0))
    gate = lambda r: pl.BlockSpec((1, r, FF_CW), lambda i, c: (c, 0, 0))
    lin = lambda r: pl.BlockSpec((1, r, FF_CW), lambda i, c: (ncb + c, 0, 0))
    return pl.pallas_call(
        body, name="ffn_fwd", grid=(nt, ncb),
        in_specs=[row(D_MODEL), _ANY, gate(3), lin(3), gate(1), lin(1), _ANY,
                  row(D_MODEL), _full((1, D_MODEL)), row(D_MODEL)],
        out_specs=[pl.BlockSpec((2, 1, tm, FF_CW), lambda i, c: (0, c, i, 0)),
                   pl.BlockSpec((2, 1, tm, FF_CW), lambda i, c: (0, c, i, 0)),
                   pl.BlockSpec((1, tm, FF_CW), lambda i, c: (c, i, 0)),
                   row(D_MODEL), row(D_MODEL), _full((1, LANES)), _full((1, D_MODEL))],
        out_shape=[_sds((2, ncb, S, FF_CW), MXU), _sds((2, ncb, S, FF_CW)), _sds((ncb, S, FF_CW), MXU),
                   _sds((S, D_MODEL)), _sds((S, D_MODEL), MXU), _sds((1, LANES)), _sds((1, D_MODEL))],
        scratch_shapes=[pltpu.VMEM((tm, D_MODEL), F32), pltpu.VMEM((ncb, 2, 8, FF_CW), F32),
                        pltpu.VMEM(w_up.shape, w_up.dtype), pltpu.VMEM(w_down.shape, w_down.dtype),
                        pltpu.SemaphoreType.DMA((2,))],
        compiler_params=pltpu.CompilerParams(dimension_semantics=("arbitrary", "arbitrary"),
                                             vmem_limit_bytes=FFN_VMEM_LIMIT),
    )(*_in_hbm([h2, w_up, conv_w, conv_w, conv_b, conv_b, w_down, x1, g_final, tgt]))


def _ffn_bwd(dx2, up, ab, x1, w_up, conv_w, w_down, g_ffn, tm):
    S = dx2.shape[0]
    nt = S // tm
    ncb = FF_NCB

    def body(dx2_ref, up_ref, ab_ref, cwa_ref, cwb_ref, wd_hbm, wup_hbm,
             x1_ref, g_ref, dup_ref, dx1_ref, dx1b_ref, dconv_ref, dg_ref, acc_ref, head_ref, wup_ref, wdn_ref, wsem):
        i = pl.program_id(0)
        cb = pl.program_id(1)

        @pl.when(i == 0)
        def _():
            head_ref[cb] = jnp.zeros((2, 8, FF_CW), F32)
            dconv_ref[cb] = jnp.zeros((8, FF_CW), F32)
            dconv_ref[ncb + cb] = jnp.zeros((8, FF_CW), F32)

        @pl.when(jnp.logical_and(i == 0, cb == 0))
        def _():
            dg_ref[...] = jnp.zeros_like(dg_ref)
            _fetch_once([(wup_hbm, wup_ref), (wd_hbm, wdn_ref)], wsem)

        dff = _dot_nt(dx2_ref[...].astype(MXU), wdn_ref[pl.ds(pl.multiple_of(cb * FF_CW, FF_CW), FF_CW), :])
        a = ab_ref[0, 0]
        b = ab_ref[1, 0]
        sa = _sigmoid(a)
        silu = a * sa
        da = (dff * b) * (sa + silu * (1.0 - sa))
        db = dff * silu
        dps = []
        for half, slot, d, cw_ref in ((0, cb, da, cwa_ref), (1, ncb + cb, db, cwb_ref)):
            dp, n1, n2, fix0, fix1 = _causal_conv3_adjoint(d, head_ref[cb, half], cw_ref[0])
            head_ref[cb, half] = d[0:8, :]
            dpb16 = dp.astype(MXU)
            dup_ref[half, 0] = dpb16
            dps.append(dpb16)
            u = up_ref[half, 0].astype(F32)
            u_last = u[tm - 1:tm, :]
            dconv_ref[slot, 0:1, :] += _rowsum(n2 * u) + fix0 * u[tm - 2:tm - 1, :] + fix1 * u_last
            dconv_ref[slot, 1:2, :] += _rowsum(n1 * u) + fix0 * u_last
            dconv_ref[slot, 2:3, :] += _rowsum(d * u)
            dconv_ref[slot, 3:4, :] += _rowsum(d)
        contrib = _dot(dps[0], wup_ref[cb]) + _dot(dps[1], wup_ref[ncb + cb])

        @pl.when(cb == 0)
        def _():
            acc_ref[...] = contrib

        @pl.when(cb > 0)
        def _():
            acc_ref[...] += contrib

        @pl.when(cb == ncb - 1)
        def _():
            x1v = x1_ref[...]
            r = _rms(x1v)
            xn = x1v * r
            dh2 = acc_ref[...]
            dg_ref[...] += _rowsum(dh2 * xn)
            dx1 = dx2_ref[...] + _rms_bwd(dh2 * g_ref[...], xn, r)
            dx1_ref[...] = dx1
            dx1b_ref[...] = dx1.astype(MXU)

    row = lambda n: pl.BlockSpec((tm, n), lambda i, c: (nt - 1 - i, 0))
    colb = lambda: pl.BlockSpec((2, 1, tm, FF_CW), lambda i, c: (0, c, nt - 1 - i, 0))
    gate = lambda r: pl.BlockSpec((1, r, FF_CW), lambda i, c: (c, 0, 0))
    lin = lambda r: pl.BlockSpec((1, r, FF_CW), lambda i, c: (ncb + c, 0, 0))
    return pl.pallas_call(
        body, name="ffn_bwd", grid=(nt, ncb),
        in_specs=[row(D_MODEL), colb(), colb(), gate(3), lin(3), _ANY, _ANY, row(D_MODEL), _full((1, D_MODEL))],
        out_specs=[colb(), row(D_MODEL), row(D_MODEL), _full((2 * ncb, 8, FF_CW)), _full((1, D_MODEL))],
        out_shape=[_sds((2, ncb, S, FF_CW), MXU), _sds((S, D_MODEL)), _sds((S, D_MODEL), MXU), _sds((2 * ncb, 8, FF_CW)),
                   _sds((1, D_MODEL))],
        scratch_shapes=[pltpu.VMEM((tm, D_MODEL), F32), pltpu.VMEM((ncb, 2, 8, FF_CW), F32),
                        pltpu.VMEM(w_up.shape, w_up.dtype), pltpu.VMEM(w_down.shape, w_down.dtype),
                        pltpu.SemaphoreType.DMA((2,))],
        compiler_params=pltpu.CompilerParams(dimension_semantics=("arbitrary", "arbitrary"),
                                             vmem_limit_bytes=FFN_VMEM_LIMIT),
    )(*_in_hbm([dx2, up, ab, conv_w, conv_w, w_down, w_up, x1, g_ffn]))


def _mix_bwd(dx1, gl, ya, yb, ys, uv, w_out, w_pa, w_pb, w_glu, b_glu, g_sgu, ws, ws_t, bias_s, tm, after=()):
    S = dx1.shape[0]

    def body(dx1_ref, gl_ref, ya_ref, yb_ref, ys_ref, uv_ref, wout_ref, wpa_ref, wpb_ref, wglu_ref, bglu_ref, gs_ref,
             ws_ref, wst_ref, bias_ref,
             dgl_ref, dya_ref, dyb_ref, dz_ref, dys_ref, duv_ref, dbglu_ref, dgs_ref, dws_ref, dbs_ref,
             du2_ref, dvn_ref):
        i = pl.program_id(0)

        @pl.when(i == 0)
        def _():
            dbglu_ref[...] = jnp.zeros_like(dbglu_ref)
            dgs_ref[...] = jnp.zeros_like(dgs_ref)
            dws_ref[...] = jnp.zeros_like(dws_ref)
            dbs_ref[...] = jnp.zeros_like(dbs_ref)

        dm = _dot_nt(dx1_ref[...].astype(MXU), wout_ref[...])
        glv = gl_ref[...]
        ga = _sigmoid(glv[:, :D_MODEL])
        gb = _sigmoid(glv[:, D_MODEL:])
        dgl_ref[:, :D_MODEL] = (dm * ya_ref[...] * ga * (1.0 - ga)).astype(MXU)
        dgl_ref[:, D_MODEL:] = (dm * yb_ref[...] * gb * (1.0 - gb)).astype(MXU)
        dyab = (dm * ga).astype(MXU)
        dybb = (dm * gb).astype(MXU)
        dya_ref[...] = dyab
        dyb_ref[...] = dybb

        dyap = _dot_nt(dyab, wpa_ref[...])
        yg, dgelu = _gelu_and_grad(ys_ref[...])
        sz = _sigmoid(_dot(yg.astype(MXU), wglu_ref[...]) + bglu_ref[...])
        dz = dyap * yg * sz * (1.0 - sz)
        dzb = dz.astype(MXU)
        dz_ref[...] = dzb
        dbglu_ref[...] += _rowsum(dz)
        dys_ref[...] = (dyap * sz + _dot_nt(dzb, wglu_ref[...])) * dgelu

        dsg = _dot_nt(dybb, wpb_ref[...])
        uvg, duvg = _gelu_and_grad(uv_ref[...])
        u2 = uvg[:, :SGU_W]
        v2 = uvg[:, SGU_W:]
        rv = _rms(v2)
        vhat = v2 * rv
        gs = gs_ref[...]
        vnb = (vhat * gs).astype(MXU)
        tril = (lax.broadcasted_iota(jnp.int32, (CHUNK, CHUNK), 0)
                >= lax.broadcasted_iota(jnp.int32, (CHUNK, CHUNK), 1))
        for c in range(tm // CHUNK):
            rs = slice(c * CHUNK, (c + 1) * CHUNK)
            vc = vnb[rs]
            mixed = _sgu_mix(vc, ws_ref) + bias_ref[...]
            dsg_c = dsg[rs]
            du2_ref[rs, :] = dsg_c * mixed
            dmx = dsg_c * u2[rs]
            dbs_ref[...] += dmx
            dmb = dmx.astype(MXU)
            dvn_ref[rs, :] = _sgu_mix(dmb, wst_ref)
            for q in range(SGU_G // 2):
                lanes = slice(LANES * q, LANES * (q + 1))
                for j, part in enumerate(_group_halves(dmb[:, lanes])):
                    dws_ref[2 * q + j] += jnp.where(tril, _dot_nt(part, vc[:, lanes]), 0.0)
        dvn = dvn_ref[...]
        dgs_ref[...] += _rowsum(dvn * vhat)
        dv2 = _rms_bwd(dvn * gs, vhat, rv)
        duv_ref[:, :SGU_W] = (du2_ref[...] * duvg[:, :SGU_W]).astype(MXU)
        duv_ref[:, SGU_W:] = (dv2 * duvg[:, SGU_W:]).astype(MXU)

    row = lambda n: pl.BlockSpec((tm, n), lambda i: (i, 0))
    return pl.pallas_call(
        _behind(body, 15, after), name="mix_bwd", grid=(S // tm,),
        in_specs=[row(D_MODEL), row(2 * D_MODEL), row(D_MODEL), row(D_MODEL), row(SSM_W), row(2 * SGU_W),
                  _once(w_out.shape), _once(w_pa.shape), _once(w_pb.shape), _once(w_glu.shape), _once(b_glu.shape),
                  _once(g_sgu.shape), _once(ws.shape), _once(ws_t.shape), _once(bias_s.shape)] + [_ANY] * len(after),
        out_specs=[row(2 * D_MODEL), row(D_MODEL), row(D_MODEL), row(SSM_W), row(SSM_W), row(2 * SGU_W),
                   _full((1, SSM_W)), _full((1, SGU_W)), _full((SGU_G, CHUNK, CHUNK)), _full((CHUNK, SGU_W))],
        out_shape=[_sds((S, 2 * D_MODEL), MXU), _sds((S, D_MODEL), MXU), _sds((S, D_MODEL), MXU), _sds((S, SSM_W), MXU),
                   _sds((S, SSM_W)), _sds((S, 2 * SGU_W), MXU),
                   _sds((1, SSM_W)), _sds((1, SGU_W)), _sds((SGU_G, CHUNK, CHUNK)), _sds((CHUNK, SGU_W))],
        scratch_shapes=[pltpu.VMEM((tm, SGU_W), F32), pltpu.VMEM((tm, SGU_W), F32)],
        compiler_params=pltpu.CompilerParams(dimension_semantics=("arbitrary",), vmem_limit_bytes=WGRAD_VMEM_LIMIT),
    )(*_in_hbm([dx1, gl, ya, yb, ys, uv, w_out, w_pa, w_pb, w_glu, b_glu, g_sgu, ws, ws_t, bias_s]), *after)


def _s5_bwd(dys, us, st_re, st_im, abar_re, abar_im, b_re, b_im, c_re, c_im, d_skip, tm, after=()):
    S = us.shape[0]
    nt = S // tm
    w = 8 * SSM_P
    hb = tm // 8
    run = tm // 8
    assert run & (run - 1) == 0

    def body(dys_ref, us_ref, str_ref, sti_ref, hr_ref, hi_ref, ar_ref, ai_ref, br_ref, bi_ref, cr_ref, ci_ref, d_ref,
             dus_ref, dab_ref, dd_ref, dbr_ref, dbi_ref, dcr_ref, dci_ref,
             tab_ref, car_ref, gr_ref, gi_ref, dyp_ref, up_ref, dun_ref):
        i = pl.program_id(1)
        ri = nt - 1 - i

        @pl.when(i == 0)
        def _():
            car_ref[...] = jnp.zeros_like(car_ref)
            for k, t in enumerate(_scan_tables(*_cpow2(ar_ref[...], -ai_ref[...], run.bit_length() - 1), True)):
                tab_ref[k] = t
            for r in (dab_ref, dd_ref, dbr_ref, dbi_ref, dcr_ref, dci_ref):
                r[...] = jnp.zeros_like(r)

        _runs_load(dys_ref, dyp_ref, run)
        _runs_load(us_ref, up_ref, run)
        dyb = dyp_ref[...].astype(MXU)
        gr_ref[...] = _dot(dyb, cr_ref[0])
        gi_ref[...] = -_dot(dyb, ci_ref[0])
        ar = jnp.broadcast_to(ar_ref[...], (8, w))
        ai = jnp.broadcast_to(-ai_ref[...], (8, w))

        def advance(kk, state):
            r0 = pl.multiple_of((run - 1 - kk) * 8, 8)
            gr, gi = state
            return (ar * gr - ai * gi + gr_ref[pl.ds(r0, 8), :], ar * gi + ai * gr + gi_ref[pl.ds(r0, 8), :])

        def emit(kk, state):
            r0 = pl.multiple_of((run - 1 - kk) * 8, 8)
            gr, gi = advance(kk, state)
            gr_ref[pl.ds(r0, 8), :] = gr
            gi_ref[pl.ds(r0, 8), :] = gi
            return gr, gi

        zero = jnp.zeros((8, w), F32)
        er, ei = lax.fori_loop(0, run, advance, (zero, zero))
        cr, ci = car_ref[0:1, :], car_ref[1:2, :]
        tr, ti = _scan_group(er, ei, tab_ref, cr, ci, True)
        r8 = lax.broadcasted_iota(jnp.int32, (8, w), 0)
        start = (jnp.where(r8 == 7, cr, pltpu.roll(tr, 7, 0)), jnp.where(r8 == 7, ci, pltpu.roll(ti, 7, 0)))
        car_ref[0:1, :] = tr[0:1, :]
        car_ref[1:2, :] = ti[0:1, :]
        lax.fori_loop(0, run, emit, start)

        gsr = gr_ref[...]
        gsi = gi_ref[...]
        sr = str_ref[...]
        si = sti_ref[...]
        first = ri == 0

        def previous(s, halo_ref):
            head = jnp.where(r8 == 0, jnp.where(first, 0.0, halo_ref[7:8, :]), pltpu.roll(s[tm - 8:tm, :], 1, 0))
            return jnp.concatenate([head, s[0:tm - 8, :]], axis=0)

        spr = previous(sr, hr_ref)
        spi = previous(si, hi_ref)
        dab_ref[0, 0:1, :] += _rowsum(gsr * spr + gsi * spi)
        dab_ref[0, 1:2, :] += _rowsum(gsi * spr - gsr * spi)

        gbr = gsr.astype(MXU)
        gbi = gsi.astype(MXU)
        _runs_store(_dot_nt(gbr, br_ref[0]) + _dot_nt(gbi, bi_ref[0]), dun_ref, run)
        dys_v = dys_ref[...]
        dus_ref[...] = (dun_ref[...] + d_ref[...] * dys_v).astype(MXU)
        dd_ref[0, 0:1, :] += _rowsum(dys_v * us_ref[...])
        ub = up_ref[...].astype(MXU)
        dbr_ref[0] += _dot_tn(ub, gbr)
        dbi_ref[0] += _dot_tn(ub, gbi)
        dcr_ref[0] += _dot_tn(dyb, sr.astype(MXU))
        dci_ref[0] -= _dot_tn(dyb, si.astype(MXU))

    blk = lambda: pl.BlockSpec((1, 8 * SSM_H, w), lambda j, i: (j, 0, 0))
    rowl = lambda: pl.BlockSpec((tm, LANES), lambda j, i: (nt - 1 - i, j))
    roww = lambda: pl.BlockSpec((tm, w), lambda j, i: (nt - 1 - i, j))
    halo = lambda: pl.BlockSpec((8, w), lambda j, i: (jnp.maximum((nt - 1 - i) * hb - 1, 0), j))
    return pl.pallas_call(
        _behind(body, 13, after), name="s5_bwd", grid=(SSM_BLK, nt),
        in_specs=[rowl(), rowl(), roww(), roww(), halo(), halo(),
                  pl.BlockSpec((1, w), lambda j, i: (0, j)), pl.BlockSpec((1, w), lambda j, i: (0, j)),
                  blk(), blk(), blk(), blk(),
                  pl.BlockSpec((1, LANES), lambda j, i: (0, j))] + [_ANY] * len(after),
        out_specs=[rowl(),
                   pl.BlockSpec((1, 8, w), lambda j, i: (j, 0, 0)), pl.BlockSpec((1, 8, LANES), lambda j, i: (j, 0, 0)),
                   blk(), blk(), blk(), blk()],
        out_shape=[_sds((S, SSM_W), MXU), _sds((SSM_BLK, 8, w)), _sds((SSM_BLK, 8, LANES)),
                   _sds((SSM_BLK, 8 * SSM_H, w)), _sds((SSM_BLK, 8 * SSM_H, w)),
                   _sds((SSM_BLK, 8 * SSM_H, w)), _sds((SSM_BLK, 8 * SSM_H, w))],
        scratch_shapes=[pltpu.VMEM((8, 8, w), F32), pltpu.VMEM((8, w), F32),
                        pltpu.VMEM((tm, w), F32), pltpu.VMEM((tm, w), F32),
                        pltpu.VMEM((tm, LANES), F32), pltpu.VMEM((tm, LANES), F32), pltpu.VMEM((tm, LANES), F32)],
        compiler_params=_cp("parallel", "arbitrary"),
    )(*_in_hbm([dys, us, st_re, st_im, st_re, st_im, abar_re, abar_im, b_re, b_im, c_re, c_im, d_skip]), *after)


def _in_bwd(dus, duv, dgl, dx1, x, g_mix, w_in, tm, after=()):
    S = x.shape[0]

    def body(dus_ref, duv_ref, dgl_ref, dx1_ref, x_ref, g_ref, w_ref, gx_ref, dg_ref):
        @pl.when(pl.program_id(0) == 0)
        def _():
            dg_ref[...] = jnp.zeros_like(dg_ref)

        dh = (_dot(dus_ref[...], w_ref[0:SSM_W, :])
              + _dot(duv_ref[...], w_ref[SSM_W:SSM_W + 2 * SGU_W, :])
              + _dot(dgl_ref[...], w_ref[SSM_W + 2 * SGU_W:, :]))
        xv = x_ref[...]
        r = _rms(xv)
        xn = xv * r
        dg_ref[...] += _rowsum(dh * xn)
        gx_ref[...] = dx1_ref[...] + _rms_bwd(dh * g_ref[...], xn, r)

    row = lambda n: pl.BlockSpec((tm, n), lambda i: (i, 0))
    return pl.pallas_call(
        _behind(body, 7, after), name="in_bwd", grid=(S // tm,),
        in_specs=[row(SSM_W), row(2 * SGU_W), row(2 * D_MODEL), row(D_MODEL), row(D_MODEL), _full((1, D_MODEL)),
                  _full(w_in.shape)] + [_ANY] * len(after),
        out_specs=[row(D_MODEL), _full((1, D_MODEL))],
        out_shape=[_sds((S, D_MODEL)), _sds((1, D_MODEL))],
        compiler_params=_cp("arbitrary"),
    )(*_in_hbm([dus, duv, dgl, dx1, x, g_mix, w_in]), *after)


def _wgrad_split(a, b, nsplit, tk, name):
    S, K = a.shape
    N = b.shape[1]
    c = N // nsplit

    def body(a_ref, b_ref, o_ref):
        prod = _dot_tn(a_ref[...], b_ref[...])
        for d in range(nsplit):
            o_ref[d] = prod[:, c * d:c * (d + 1)].astype(MXU)

    return pl.pallas_call(
        body, name=name, grid=(K // tk,),
        in_specs=[pl.BlockSpec((S, tk), lambda k: (0, k)), _full((S, N))],
        out_specs=pl.BlockSpec((nsplit, tk, c), lambda k: (0, k, 0)),
        out_shape=_sds((nsplit, K, c), MXU),
        compiler_params=_cp("parallel"),
    )(*_in_hbm([a, b]))


def _wgrad_in_t(dps, h1, name):
    S, K = h1.shape
    cw = 512
    counts = [b.shape[1] // cw for b in dps]
    starts = [sum(counts[:i]) for i in range(len(dps))]
    nblk = sum(counts)

    def body(*refs):
        b_refs = refs[:len(dps)]
        h_ref, o_ref = refs[len(dps)], refs[-1]
        j = pl.program_id(0)
        for b_ref, st, cnt in zip(b_refs, starts, counts):
            @pl.when(jnp.logical_and(j >= st, j < st + cnt))
            def _():
                o_ref[...] = _dot_tn(b_ref[...], h_ref[...]).astype(MXU)

    def src_spec(st, cnt):
        return pl.BlockSpec((S, cw), lambda j: (0, jnp.clip(j - st, 0, cnt - 1)))

    return pl.pallas_call(
        body, name=name, grid=(nblk,),
        in_specs=[src_spec(st, cnt) for st, cnt in zip(starts, counts)] + [_full((S, K))],
        out_specs=pl.BlockSpec((cw, K), lambda j: (j, 0)),
        out_shape=_sds((nblk * cw, K), MXU),
        compiler_params=_cp("arbitrary"),
    )(*_in_hbm([*dps, h1]))


def _wgrad_blk(a3, b3, nblk, a_of, b_of, name):
    S, K = a3.shape[1:]
    N = b3.shape[2]

    def body(a_ref, b_ref, o_ref):
        o_ref[0] = _dot_tn(a_ref[0], b_ref[0]).astype(MXU)

    return pl.pallas_call(
        body, name=name, grid=(nblk,),
        in_specs=[pl.BlockSpec((1, S, K), lambda b: (a_of(b), 0, 0)),
                  pl.BlockSpec((1, S, N), lambda b: (b_of(b), 0, 0))],
        out_specs=pl.BlockSpec((1, K, N), lambda b: (b, 0, 0)),
        out_shape=_sds((nblk, K, N), MXU),
        compiler_params=pltpu.CompilerParams(dimension_semantics=("parallel",), vmem_limit_bytes=WGRAD_VMEM_LIMIT),
    )(*_in_hbm([a3, b3]))


def _assemble_cols(blocks_list, name):
    def body(*refs):
        n = len(blocks_list)
        for b_ref, o_ref in zip(refs[:n], refs[n:]):
            c = b_ref.shape[2]
            for d in range(N_DEV):
                o_ref[:, c * d:c * (d + 1)] = b_ref[d]

    outs = [_sds((b.shape[1], N_DEV * b.shape[2]), b.dtype) for b in blocks_list]
    return pl.pallas_call(
        body, name=name, grid=(1,), in_specs=[_full(b.shape) for b in blocks_list],
        out_specs=[_full(o.shape) for o in outs], out_shape=outs, compiler_params=_cp("arbitrary"),
    )(*_in_hbm(blocks_list))


def _tile(S, want):
    return want if S % want == 0 else S


def _local_step(x, tgt, p, after, mixer_relay, mixer_weights, ffn_weights, grads_out, small_out):
    S = x.shape[0]
    tm = _tile(S, 256)
    tl = _tile(S, 512)

    rep = lambda a: jnp.repeat(a, SSM_H, axis=0)
    are = rep(p["a_re"])
    aim = rep(p["a_im"])
    ldt = jnp.broadcast_to(rep(p["log_dt"].reshape(SSM_G, 1)), are.shape)
    br_t = p["b_re_t"].reshape(are.shape)
    bi_t = p["b_im_t"].reshape(are.shape)
    abr, abi, bbr, bbi = _s5_params_fwd(are, aim, ldt, br_t, bi_t)
    head = lambda a: a.reshape(SSM_G, SSM_H, SSM_P)[:, 0, :].reshape(1, SSM_G * SSM_P)
    abar_re, abar_im = head(abr), head(abi)
    bd_br = _blockdiag(bbr).astype(MXU)
    bd_bi = _blockdiag(bbi).astype(MXU)
    bd_cr = _blockdiag(p["c_re"].reshape(are.shape)).astype(MXU)
    bd_ci = _blockdiag(p["c_im"].reshape(are.shape)).astype(MXU)
    d_skip = p["d_skip"].reshape(1, SSM_W)

    tril = jnp.tril(jnp.ones((CHUNK, CHUNK), dtype=bool))
    ws = jnp.where(tril[None], p["w_s"], 0.0)
    pair = lambda w: w.reshape(SGU_G // 2, 2, CHUNK, CHUNK).transpose(0, 2, 1, 3).reshape(SGU_G // 2, CHUNK, 2 * CHUNK)
    ws_b = pair(ws).astype(MXU)
    ws_t = pair(ws.transpose(0, 2, 1)).astype(MXU)
    bias_s = jnp.repeat(p["b_s"].T, SGU_D, axis=1)

    g_mix = p["g_mix"].reshape(1, D_MODEL)
    g_ffn = p["g_ffn"].reshape(1, D_MODEL)
    g_final = p["g_final"].reshape(1, D_MODEL)
    g_sgu = p["g_sgu"].reshape(1, SGU_W)
    b_glu = p["b_glu"].reshape(1, SSM_W)
    conv_b = p["conv_b"].reshape(2 * FF_NCB, 1, FF_CW)
    tf = _tile(S, 256)
    ts = _tile(S, 1024)

    h1, us, uv, gl = _in_fwd(x, g_mix, p["w_in_t"], tl, after)
    token = mixer_relay(us)
    st_re, st_im, ys = _s5_fwd(us, abar_re, abar_im, bd_br, bd_bi, bd_cr, bd_ci, d_skip, ts, (token,))
    p = dict(p, **mixer_weights(ys))
    yg, yap, sg, ya, yb, m, x1, h2 = _mix_fwd(x, ys, uv, gl, p["w_glu"], b_glu, p["w_proj_a"], g_sgu, ws_b, bias_s,
                                              p["w_proj_b"], p["w_out"], g_ffn, tl)
    w_up, conv_w, w_down = ffn_weights(h2)
    pair_lanes = lambda a: a.reshape(N_DEV // 2, 2, a.shape[1], FF_SHARD).transpose(0, 2, 1, 3).reshape(
        N_DEV // 2, a.shape[1], FF_CW)
    w_up = w_up.reshape(2 * FF_NCB, FF_CW, D_MODEL)
    conv_w = pair_lanes(conv_w)
    up, ab, ff, dx2, dx2b, loss, dg_final = _ffn_fwd(h2, x1, tgt, w_up, conv_w, conv_b, w_down, g_final, tf)

    dup, dx1, dx1b, dconv, dg_ffn = _ffn_bwd(dx2, up, ab, x1, w_up, conv_w, w_down, g_ffn, tf)
    rows8 = lambda g: g.reshape(N_DEV, g.shape[1] // N_DEV, g.shape[2])
    g_up = _wgrad_blk(dup.reshape(2 * FF_NCB, S, FF_CW), h2[None], 2 * FF_NCB, lambda b: b, lambda b: 0,
                      "wgrad_up").reshape(N_DEV, FF_SHARD, D_MODEL)
    g_down = _wgrad_blk(ff, dx2b[None], FF_NCB, lambda b: b, lambda b: 0, "wgrad_down").reshape(
        N_DEV, D_FF // N_DEV, D_MODEL)
    token = grads_out(("w_up", "w_down"), (g_up, g_down))
    dgl, dya, dyb, dz, dys, duv, db_glu, dg_sgu, dws, dbs = _mix_bwd(
        dx1, gl, ya, yb, ys, uv, p["w_out"], p["w_proj_a"], p["w_proj_b"], p["w_glu"], b_glu, g_sgu,
        ws_b, ws_t, bias_s, tl, (token,))
    token = grads_out(("w_glu", "w_proj_a", "w_proj_b", "w_out"),
                      (rows8(_wgrad_split(yg, dz, 1, SSM_W, "wgrad_glu")),
                       _wgrad_split(yap, dya, N_DEV, SSM_W, "wgrad_pa"),
                       _wgrad_split(sg, dyb, N_DEV, SGU_W, "wgrad_pb"),
                       rows8(_wgrad_split(m, dx1b, 1, 512, "wgrad_out"))))
    dus, dab, dd, dbbr, dbbi, dcr, dci = _s5_bwd(dys, us, st_re, st_im, abar_re, abar_im, bd_br, bd_bi, bd_cr, bd_ci,
                                                 d_skip, ts, (token,))
    g_in = _wgrad_in_t([dus, duv, dgl], h1, "wgrad_in")
    token = grads_out(("w_in",), (g_in.reshape(N_DEV, g_in.shape[0] // N_DEV, D_MODEL),))
    grad_x, dg_mix = _in_bwd(dus, duv, dgl, dx1, x, g_mix, p["w_in_t"], tl, (token,))

    spread = lambda v: jnp.repeat(v.reshape(SSM_G, SSM_P), SSM_H, axis=0) * (1.0 / SSM_H)
    dabr = spread(dab[:, 0, :])
    dabi = spread(dab[:, 1, :])
    dare, daim, dldt, dbr_t, dbi_t = _s5_params_bwd(are, aim, ldt, br_t, bi_t, dabr, dabi,
                                                    _unblockdiag(dbbr), _unblockdiag(dbbi))
    fold = lambda a: a.reshape(SSM_G, SSM_H, SSM_P).sum(axis=1)

    grads = {
        "g_mix": dg_mix,
        "a_re": fold(dare), "a_im": fold(daim), "log_dt": fold(dldt).sum(axis=1),
        "b_re": dbr_t, "b_im": dbi_t,
        "c_re": _unblockdiag(dcr).reshape(SSM_G, SSM_H, SSM_P),
        "c_im": _unblockdiag(dci).reshape(SSM_G, SSM_H, SSM_P),
        "d_skip": dd[:, 0, :].reshape(SSM_W),
        "b_glu": db_glu,
        "g_sgu": dg_sgu,
        "w_s": dws,
        "b_s": dbs.reshape(CHUNK, SGU_G, SGU_D).sum(axis=-1).T,
        "g_ffn": dg_ffn,
        "conv_w": dconv[:, 0:3, :].reshape(N_DEV // 2, 3, 2, FF_SHARD).transpose(0, 2, 1, 3).reshape(
            N_DEV, 3, FF_SHARD),
        "conv_b": dconv[:, 3, :].reshape(2 * D_FF),
        "g_final": dg_final,
    }
    return grad_x, small_out(grads, loss)


_ANY = pl.BlockSpec(memory_space=pl.ANY)
_MESH = pl.DeviceIdType.MESH


def _allgather(shards, dtypes, name, cast_only=(), sum_slots=False):
    n = len(shards)
    e = len(cast_only)
    shapes = [s.shape[1:] if sum_slots else s.shape for s in shards]

    def body(*refs):
        in_refs, extra_in = refs[:n], refs[n:n + e]
        out_refs, extra_out = refs[n + e:2 * n + e], refs[2 * n + e:2 * n + 2 * e]
        stage = refs[2 * n + 2 * e:3 * n + 2 * e]
        send_sems, recv_sems, local_sems = refs[3 * n + 2 * e:]
        for a in range(n):
            if sum_slots:
                total = in_refs[a][0].astype(F32)
                for s in range(1, N_DEV):
                    total = total + in_refs[a][s].astype(F32)
                stage[a][...] = total.astype(dtypes[a])
            else:
                stage[a][...] = in_refs[a][...].astype(dtypes[a])
        for i in range(e):
            extra_out[i][...] = extra_in[i][...].astype(MXU)
        x, y, c = lax.axis_index("x"), lax.axis_index("y"), lax.axis_index("c")
        me, sibling = (x, y, c), (x, y, 1 - c)
        chips = [(1 - x, y), (x, 1 - y), (1 - x, 1 - y)]

        def slot(a, px, py, pc):
            return out_refs[a].at[4 * px + 2 * py + pc]

        def copy(a, k, block, to, src=None):
            return pltpu.make_async_remote_copy(
                src_ref=slot(a, *block) if src is None else src, dst_ref=slot(a, *block),
                send_sem=send_sems.at[a, k], recv_sem=recv_sems.at[a, k], device_id=to, device_id_type=_MESH)

        mine = [pltpu.make_async_copy(stage[a], slot(a, *me), local_sems.at[a]) for a in range(n)]
        for cp in mine:
            cp.start()
        first = []
        for j, chip in enumerate(chips):
            first += [copy(a, 1 + j, me, (*chip, c), src=stage[a]) for a in range(n)]
        first += [copy(a, 0, me, sibling, src=stage[a]) for a in range(n)]
        for cp in first:
            cp.start()
        passed = []
        for j, chip in enumerate(chips):
            for a in range(n):
                copy(a, 1 + j, (*chip, c), me).wait_recv()
                fwd = copy(a, 4 + j, (*chip, c), sibling)
                fwd.start()
                passed.append(fwd)
        for a in range(n):
            copy(a, 0, sibling, me).wait_recv()
        for j, chip in enumerate(chips):
            for a in range(n):
                copy(a, 4 + j, (*chip, 1 - c), me).wait_recv()
        for cp in first + passed:
            cp.wait_send()
        for cp in mine:
            cp.wait()

    res = pl.pallas_call(
        body, name=name, grid=(1,), in_specs=[_full(s.shape) for s in list(shards) + list(cast_only)],
        out_specs=[_ANY] * n + [_full(s.shape) for s in cast_only],
        out_shape=[_sds((N_DEV,) + shp, dt) for shp, dt in zip(shapes, dtypes)]
                  + [_sds(s.shape, MXU) for s in cast_only],
        scratch_shapes=[pltpu.VMEM(shp, dt) for shp, dt in zip(shapes, dtypes)]
                       + [pltpu.SemaphoreType.DMA((n, 7)), pltpu.SemaphoreType.DMA((n, 7)), pltpu.SemaphoreType.DMA((n,))],
        compiler_params=pltpu.CompilerParams(vmem_limit_bytes=VMEM_LIMIT),
    )(*_in_hbm([*shards, *cast_only]))
    return res[:n], res[n:]


_HBM = pl.BlockSpec(memory_space=pltpu.HBM)
_SEM = pl.BlockSpec(memory_space=pltpu.SEMAPHORE)
_EFFECT = pltpu.SideEffectType.DATAFLOW_SIDE_EFFECTING
_PEER_ORDER = (2, 4, 6, 3, 5, 7, 1)


def _peer(k):
    x, y, c = lax.axis_index("x"), lax.axis_index("y"), lax.axis_index("c")
    px = 1 - x if k & 4 else x
    py = 1 - y if k & 2 else y
    pc = 1 - c if k & 1 else c
    return (px, py, pc), 4 * px + 2 * py + pc


_SAME_CORE_AND_SIBLING = (2, 4, 6, 1)


def _push_start(srcs, lands, slotted, name, peers=_PEER_ORDER):
    n = len(srcs)

    def body(*refs):
        src_refs, land_refs = refs[:n], refs[n:2 * n]
        send_sems, recv_sems, token = refs[2 * n], refs[2 * n + 1], refs[-1]
        mine = 4 * lax.axis_index("x") + 2 * lax.axis_index("y") + lax.axis_index("c")
        for k in peers:
            dev, theirs = _peer(k)
            for a in range(n):
                pltpu.make_async_remote_copy(
                    src_ref=src_refs[a].at[theirs] if slotted else src_refs[a], dst_ref=land_refs[a].at[mine],
                    send_sem=send_sems.at[7 * a + k - 1], recv_sem=recv_sems.at[7 * a + k - 1],
                    device_id=dev, device_id_type=_MESH).start()
        token[...] = jnp.zeros_like(token)

    bufs = list(srcs) + list(lands)
    res = pl.pallas_call(
        body, name=name, in_specs=[_HBM] * (2 * n),
        out_specs=(_SEM, _SEM, *[_HBM] * (2 * n), pl.BlockSpec(memory_space=pltpu.VMEM)),
        out_shape=(pltpu.SemaphoreType.DMA((7 * n,)), pltpu.SemaphoreType.DMA((7 * n,)),
                   *[pltpu.HBM(b.shape, b.dtype) for b in bufs], _sds((8, LANES))),
        input_output_aliases={i: 2 + i for i in range(2 * n)},
        compiler_params=pltpu.CompilerParams(has_side_effects=_EFFECT),
    )(*[pltpu.with_memory_space_constraint(b, pltpu.HBM) for b in bufs])
    return res[0], res[1], res[2:2 + n], res[2 + n:2 + 2 * n], res[-1]


def _push_wait(send_sems, recv_sems, srcs, lands, slotted, after, name, peers=_PEER_ORDER):
    n = len(srcs)

    def body(*refs):
        src_refs, land_refs = refs[:n], refs[n:2 * n]
        send_sems, recv_sems = refs[2 * n], refs[2 * n + 1]
        for k in peers:
            dev, theirs = _peer(k)
            for a in range(n):
                cp = pltpu.make_async_remote_copy(
                    src_ref=src_refs[a].at[theirs] if slotted else src_refs[a], dst_ref=land_refs[a].at[theirs],
                    send_sem=send_sems.at[7 * a + k - 1], recv_sem=recv_sems.at[7 * a + k - 1],
                    device_id=dev, device_id_type=_MESH)
                cp.wait_send()
                cp.wait_recv()

    bufs = list(srcs) + list(lands)
    res = pl.pallas_call(
        body, name=name, in_specs=[_HBM] * (2 * n) + [_SEM, _SEM] + [_ANY] * len(after), out_specs=[_HBM] * (2 * n),
        out_shape=[pltpu.HBM(b.shape, b.dtype) for b in bufs],
        input_output_aliases={i: i for i in range(2 * n)},
        compiler_params=pltpu.CompilerParams(has_side_effects=_EFFECT),
    )(*bufs, send_sems, recv_sems, *after)
    return res[n:]


def _other_chips():
    x, y = lax.axis_index("x"), lax.axis_index("y")
    return ((1 - x, y), (x, 1 - y), (1 - x, 1 - y))


def _relay_start(lands, name):
    n = len(lands)

    def body(*refs):
        land_refs = refs[:n]
        send_sems, recv_sems, token = refs[n], refs[n + 1], refs[-1]
        x, y, c = lax.axis_index("x"), lax.axis_index("y"), lax.axis_index("c")
        for j, (px, py) in enumerate(_other_chips()):
            slot = 4 * px + 2 * py + c
            for a in range(n):
                pltpu.make_async_remote_copy(
                    src_ref=land_refs[a].at[slot], dst_ref=land_refs[a].at[slot],
                    send_sem=send_sems.at[3 * a + j], recv_sem=recv_sems.at[3 * a + j],
                    device_id=(x, y, 1 - c), device_id_type=_MESH).start()
        token[...] = jnp.zeros_like(token)

    res = pl.pallas_call(
        body, name=name, in_specs=[_HBM] * n,
        out_specs=(_SEM, _SEM, *[_HBM] * n, pl.BlockSpec(memory_space=pltpu.VMEM)),
        out_shape=(pltpu.SemaphoreType.DMA((3 * n,)), pltpu.SemaphoreType.DMA((3 * n,)),
                   *[pltpu.HBM(b.shape, b.dtype) for b in lands], _sds((8, LANES))),
        input_output_aliases={i: 2 + i for i in range(n)},
        compiler_params=pltpu.CompilerParams(has_side_effects=_EFFECT),
    )(*[pltpu.with_memory_space_constraint(b, pltpu.HBM) for b in lands])
    return res[0], res[1], res[2:2 + n], res[-1]


def _relay_wait(send_sems, recv_sems, lands, after, name):
    n = len(lands)

    def body(*refs):
        land_refs = refs[:n]
        send_sems, recv_sems = refs[n], refs[n + 1]
        x, y, c = lax.axis_index("x"), lax.axis_index("y"), lax.axis_index("c")
        for j, (px, py) in enumerate(_other_chips()):
            sent, received = 4 * px + 2 * py + c, 4 * px + 2 * py + (1 - c)
            for a in range(n):
                cp = pltpu.make_async_remote_copy(
                    src_ref=land_refs[a].at[sent], dst_ref=land_refs[a].at[received],
                    send_sem=send_sems.at[3 * a + j], recv_sem=recv_sems.at[3 * a + j],
                    device_id=(x, y, 1 - c), device_id_type=_MESH)
                cp.wait_send()
                cp.wait_recv()

    return pl.pallas_call(
        body, name=name, in_specs=[_HBM] * n + [_SEM, _SEM] + [_ANY] * len(after), out_specs=[_HBM] * n,
        out_shape=[pltpu.HBM(b.shape, b.dtype) for b in lands],
        input_output_aliases={i: i for i in range(n)},
        compiler_params=pltpu.CompilerParams(has_side_effects=_EFFECT),
    )(*lands, send_sems, recv_sems, *after)


def _adamw(w, g, m, v):
    m2 = ADAM_B1 * m + (1.0 - ADAM_B1) * g
    v2 = ADAM_B2 * v + (1.0 - ADAM_B2) * (g * g)
    m_hat = m2 / (1.0 - ADAM_B1 ** ADAM_STEP)
    v_hat = v2 / (1.0 - ADAM_B2 ** ADAM_STEP)
    delta = -ADAM_LR * (m_hat / (jnp.sqrt(v_hat) + ADAM_EPS) + ADAM_WD * w)
    return delta, m2, v2


def _adam_shard(parts, w, m, v, name):
    _, r, c = w.shape
    tr = max(t for t in range(16, 257, 16) if r % t == 0)

    nparts = parts.shape[0]

    def body(p_ref, w_ref, m_ref, v_ref, g_ref, d_ref, m2_ref, v2_ref):
        g = p_ref[0].astype(F32)
        for s in range(1, nparts):
            g = g + p_ref[s].astype(F32)
        g_ref[0] = g
        d_ref[0], m2_ref[0], v2_ref[0] = _adamw(w_ref[0], g, m_ref[0], v_ref[0])

    row = lambda: pl.BlockSpec((1, tr, c), lambda i: (0, i, 0))
    return pl.pallas_call(
        body, name=name, grid=(r // tr,),
        in_specs=[pl.BlockSpec((nparts, tr, c), lambda i: (0, i, 0)), row(), row(), row()],
        out_specs=[row(), row(), row(), row()], out_shape=[_sds((1, r, c))] * 4,
        compiler_params=_cp("parallel"),
    )(*_in_hbm([parts, w, m, v]))


def _adam_small(gs, ws, ms, vs, name):
    n = len(gs)

    def body(*refs):
        ins, outs = refs[:4 * n], refs[4 * n:]
        for i in range(n):
            g = ins[i][...]
            d, m2, v2 = _adamw(ins[n + i][...], g, ins[2 * n + i][...], ins[3 * n + i][...])
            outs[i][...] = d
            outs[n + i][...] = m2
            outs[2 * n + i][...] = v2

    res = pl.pallas_call(
        body, name=name, grid=(1,), in_specs=[_full(w.shape) for w in ws] * 4,
        out_specs=[_full(w.shape) for w in ws] * 3, out_shape=[_sds(w.shape) for w in ws] * 3,
        compiler_params=_cp("arbitrary"),
    )(*_in_hbm([*gs, *ws, *ms, *vs]))
    return res[:n], res[n:2 * n], res[2 * n:]


def _pad_to(a, n, axis):
    extra = n - a.shape[axis]
    if extra == 0:
        return a
    widths = [(0, 0)] * a.ndim
    widths[axis] = (0, extra)
    return jnp.pad(a, widths)


def _ceil_to(n, k):
    return -(-n // k) * k


def _pack_rows(flats, rows_multiple):
    parts = [_pad_to(f, _ceil_to(f.shape[-1], LANES), f.ndim - 1) for f in flats]
    cat = jnp.concatenate(parts, axis=-1)
    total = _ceil_to(cat.shape[-1], LANES * rows_multiple)
    cat = _pad_to(cat, total, cat.ndim - 1)
    return cat.reshape(cat.shape[:-1] + (total // LANES, LANES))


def _unpack_rows(buf, sizes):
    flat = buf.reshape(buf.shape[:-2] + (-1,))
    out, off = [], 0
    for n in sizes:
        out.append(flat[..., off:off + n])
        off += _ceil_to(n, LANES)
    return out


_MIX_BIG = ("w_in", "w_glu", "w_proj_a", "w_proj_b", "w_out")
_BIG = _MIX_BIG + ("w_up", "w_down")
_SMALL = ("g_mix", "a_re", "a_im", "log_dt", "b_re", "b_im", "c_re", "c_im", "d_skip", "b_glu", "g_sgu", "w_s", "b_s",
          "g_ffn", "conv_b", "g_final")
_SMALL_ROWS_MULTIPLE = 8 * N_DEV
_TRANSPOSED = ("w_in", "w_up", "b_re", "b_im")


def _as_2d(a):
    return a.reshape(-1, a.shape[-1]) if a.ndim > 1 else a.reshape(1, -1)


def kernel(x, g_mix, w_in, a_re, a_im, log_dt, b_re, b_im, c_re, c_im, d_skip, w_glu, b_glu, w_proj_a, g_sgu, w_s, b_s, w_proj_b, w_out, g_ffn, w_up, conv_w, conv_b, w_down, g_final, loss_target, m_g_mix, m_w_in, m_a_re, m_a_im, m_log_dt, m_b_re, m_b_im, m_c_re, m_c_im, m_d_skip, m_w_glu, m_b_glu, m_w_proj_a, m_g_sgu, m_w_s, m_b_s, m_w_proj_b, m_w_out, m_g_ffn, m_w_up, m_conv_w, m_conv_b, m_w_down, m_g_final, v_g_mix, v_w_in, v_a_re, v_a_im, v_log_dt, v_b_re, v_b_im, v_c_re, v_c_im, v_d_skip, v_w_glu, v_b_glu, v_w_proj_a, v_g_sgu, v_w_s, v_b_s, v_w_proj_b, v_w_out, v_g_ffn, v_w_up, v_conv_w, v_conv_b, v_w_down, v_g_final):
    args = dict(locals())
    me = 4 * lax.axis_index("x") + 2 * lax.axis_index("y") + lax.axis_index("c")

    def own_slot(buf, block):
        return lax.dynamic_update_slice(buf, block[None], (me,) + (0,) * block.ndim)

    for n in _TRANSPOSED:
        for pre in ("", "m_", "v_"):
            args[pre + n] = jnp.swapaxes(args[pre + n], -1, -2)
    later = ("w_glu", "w_proj_a", "w_proj_b", "w_out", "w_up", "w_down")
    (w_in_g,), casts = _allgather([args["w_in"][0]], [MXU], "allgather_w_in", cast_only=[args[n][0] for n in later])
    sh = dict(zip(later, casts))

    def start_push(srcs, tag, peers):
        lands = [own_slot(lax.empty((N_DEV,) + s.shape, s.dtype), s) for s in srcs]
        send_sems, recv_sems, srcs, lands, token = _push_start(srcs, lands, False, "push_" + tag, peers)
        return (send_sems, recv_sems, srcs, lands), token

    mix_push, token_a = start_push([sh[n] for n in later[:4]], "mixer_weights", _SAME_CORE_AND_SIBLING)
    ffn_push, token_b = start_push([sh["w_up"], sh["w_down"], conv_w[0]], "ffn_weights", _PEER_ORDER)
    p = {n: (args[n][0] if n != "g_final" else args[n]) for n in _SMALL if n not in _TRANSPOSED}
    p.update(w_in_t=w_in_g.reshape(SSM_W + 2 * SGU_W + 2 * D_MODEL, D_MODEL),
             b_re_t=args["b_re"][0], b_im_t=args["b_im"][0])
    relay = {}

    def mixer_relay(after):
        lands = _push_wait(*mix_push, False, [after], "wait_mixer_weights", _SAME_CORE_AND_SIBLING)
        relay["send"], relay["recv"], relay["lands"], token = _relay_start(lands, "relay_mixer_weights")
        return token

    def mixer_weights(after):
        w_glu_g, w_pa_g, w_pb_g, w_out_g = _relay_wait(relay["send"], relay["recv"], relay["lands"], [after],
                                                       "wait_relay_mixer_weights")
        w_pa_full, w_pb_full = _assemble_cols([w_pa_g, w_pb_g], "assemble_cols")
        return dict(w_glu=w_glu_g.reshape(SSM_W, SSM_W), w_proj_a=w_pa_full, w_proj_b=w_pb_full,
                    w_out=w_out_g.reshape(D_MODEL, D_MODEL))

    def ffn_weights(after):
        w_up_g, w_down_g, conv_w_g = _push_wait(*ffn_push, False, [after], "wait_ffn_weights")
        return w_up_g, conv_w_g, w_down_g.reshape(D_FF, D_MODEL)

    pushes = []

    def grads_out(names, sends):
        lands = [own_slot(lax.empty(s.shape, s.dtype), lax.dynamic_index_in_dim(s, me, 0, keepdims=False))
                 for s in sends]
        send_sems, recv_sems, srcs, lands, token = _push_start(list(sends), lands, True, "push_grads_" + names[0])
        pushes.append((names, send_sems, recv_sems, srcs, lands))
        return token


    small_names = _SMALL + ("conv_w", "loss")
    small = {}

    def small_out(grads, loss_part):
        small_g = dict(grads, loss=loss_part[0, 0:1])
        flats = [small_g[n].reshape(-1) for n in small_names]
        small["sizes"] = [f.shape[0] for f in flats]
        g_small = _pack_rows(flats, _SMALL_ROWS_MULTIPLE)
        small["rs8"] = g_small.shape[0] // N_DEV
        return grads_out(("small",), (g_small.reshape(N_DEV, small["rs8"], LANES),))

    grad_x, small_token = _local_step(x[0], loss_target[0], p, (token_a, token_b), mixer_relay, mixer_weights,
                                      ffn_weights, grads_out, small_out)

    out = {}
    done = [grad_x, small_token]
    for names, send_sems, recv_sems, srcs, lands in pushes:
        parts = _push_wait(send_sems, recv_sems, srcs, lands, True, done, "wait_grads_" + names[0])
        if names == ("small",):
            g_small_all = _allgather([parts[0]], [F32], "allgather_small", sum_slots=True)[0][0].reshape(
                N_DEV * small["rs8"], LANES)
            pieces = dict(zip(small_names, _unpack_rows(g_small_all, small["sizes"])))
            loss = pieces["loss"][0]
            dconv_w = lax.dynamic_index_in_dim(pieces["conv_w"].reshape(N_DEV, 3, FF_SHARD), me, axis=0, keepdims=False)
            names2 = _SMALL + ("conv_w",)
            gs = [pieces[n].reshape(_as_2d(args[n]).shape) for n in _SMALL] + [dconv_w]
            ds, m2s, v2s = _adam_small(gs, [_as_2d(args[n]) for n in names2], [_as_2d(args["m_" + n]) for n in names2],
                                       [_as_2d(args["v_" + n]) for n in names2], "adam_small")
            for n, res in zip(names2, zip(gs, ds, m2s, v2s)):
                for kind, v in zip(("grad_", "delta_", "new_m_", "new_v_"), res):
                    out[kind + n] = v.reshape(args[n].shape)
            done = [ds[0]]
            continue
        done = []
        for n, part in zip(names, parts):
            res = _adam_shard(part, args[n], args["m_" + n], args["v_" + n], "adam_" + n)
            for kind, v in zip(("grad_", "delta_", "new_m_", "new_v_"), res):
                out[kind + n] = v
            done.append(res[0])
    order = ("g_mix", "w_in", "a_re", "a_im", "log_dt", "b_re", "b_im", "c_re", "c_im", "d_skip", "w_glu", "b_glu",
             "w_proj_a", "g_sgu", "w_s", "b_s", "w_proj_b", "w_out", "g_ffn", "w_up", "conv_w", "conv_b", "w_down",
             "g_final")
    res = [loss, grad_x.reshape(x.shape)]
    for kind in ("grad_", "delta_", "new_m_", "new_v_"):
        res += [jnp.swapaxes(out[kind + n], -1, -2) if n in _TRANSPOSED else out[kind + n] for n in order]
    return tuple(res)
```
